```python
import math
import jax, jax.numpy as jnp
from jax import lax
import numpy as np

D_MODEL = 1024
BATCH = 8
SEQ = 4096
DEPTH = 2

CHUNK = 64
PLE_DIM = 256
D_POOL = D_MODEL // 2
D_CONV = D_MODEL - D_POOL
POOL_WINDOWS = (2, 4, 8, 16)
POOL_GROUP = D_POOL // len(POOL_WINDOWS)
CONV_KERNEL = 31
HEAD_DIM = 64
N_HEADS = D_MODEL // HEAD_DIM
LEFT_CHUNKS = 8
BAND = (LEFT_CHUNKS + 1) * CHUNK
MAX_REL_DIST = 256
D_FF = ((8 * D_MODEL // 3) + 127) // 128 * 128
FFN_CONV_KERNEL = 3
N_EVEN = (DEPTH + 1) // 2
N_ODD = DEPTH // 2
DEEPNORM_ALPHA = (2 * DEPTH) ** 0.25
DEEPNORM_BETA = (8 * DEPTH) ** -0.25
LN_EPS = 1e-5
NEG_INF = -1e30

kernel_name = "hybrid_pool_conv_chunkattn_encoder"


def layer_norm(x, g, b):
    x32 = x.astype(jnp.float32)
    mu = jnp.mean(x32, axis=-1, keepdims=True)
    var = jnp.mean(jnp.square(x32 - mu), axis=-1, keepdims=True)
    y = (x32 - mu) * lax.rsqrt(var + LN_EPS)
    return (y * g.astype(jnp.float32) + b.astype(jnp.float32)).astype(x.dtype)


def causal_dwconv(x, w, b):
    k = w.shape[0]
    c = x.shape[-1]
    y = lax.conv_general_dilated(
        x, w[:, None, :].astype(x.dtype), window_strides=(1,), padding=[(k - 1, 0)],
        dimension_numbers=('NWC', 'WIO', 'NWC'), feature_group_count=c)
    return y + b


def pool_conv_mixer(x, w_in, pool_w, pool_scale, dw_w, dw_b, cn_g, cn_b, w_out):
    bsz, s, _ = x.shape
    u = x @ w_in
    a = u[..., :D_POOL]
    b_val = u[..., D_POOL:D_POOL + D_CONV]
    b_gate = u[..., D_POOL + D_CONV:]

    a32 = a.astype(jnp.float32)
    cs = jnp.pad(jnp.cumsum(a32, axis=1), ((0, 0), (1, 0), (0, 0)))
    pos = jnp.arange(1, s + 1, dtype=jnp.float32)
    groups = []
    for g, w in enumerate(POOL_WINDOWS):
        sl = slice(g * POOL_GROUP, (g + 1) * POOL_GROUP)
        csg = cs[..., sl]
        lower = jnp.pad(csg[:, :s + 1 - w], ((0, 0), (w - 1, 0), (0, 0)))
        mean = (csg[:, 1:] - lower) / jnp.minimum(pos, float(w))[None, :, None]
        groups.append(mean - a32[..., sl])
    d = jnp.stack(groups, axis=2).astype(x.dtype)
    y_a = jnp.einsum('bsgc,gcd->bsgd', d, pool_w).reshape(bsz, s, D_POOL) * pool_scale

    glu = b_val * jax.nn.sigmoid(b_gate)
    h = causal_dwconv(glu, dw_w, dw_b)
    y_b = jax.nn.silu(layer_norm(h, cn_g, cn_b))

    return jnp.concatenate([y_a, y_b], axis=-1) @ w_out


def chunked_rel_attention(x, w_qkv, rel_bias, w_o):
    bsz, s, _ = x.shape
    nc = s // CHUNK
    pad = LEFT_CHUNKS * CHUNK
    q, k, v = jnp.split(x @ w_qkv, 3, axis=-1)
    q = q.reshape(bsz, nc, CHUNK, N_HEADS, HEAD_DIM).transpose(1, 0, 2, 3, 4)
    k = jnp.pad(k.reshape(bsz, s, N_HEADS, HEAD_DIM), ((0, 0), (pad, 0), (0, 0), (0, 0)))
    v = jnp.pad(v.reshape(bsz, s, N_HEADS, HEAD_DIM), ((0, 0), (pad, 0), (0, 0), (0, 0)))

    qi = jnp.arange(CHUNK)[:, None]
    kj = jnp.arange(BAND)[None, :]
    rel = jnp.clip(pad + qi - kj, -MAX_REL_DIST, MAX_REL_DIST) + MAX_REL_DIST
    bias = rel_bias[:, rel].astype(jnp.float32)
    scale = HEAD_DIM ** -0.5

    def one_chunk(args):
        qc, c = args
        kb = lax.dynamic_slice_in_dim(k, c * CHUNK, BAND, axis=1)
        vb = lax.dynamic_slice_in_dim(v, c * CHUNK, BAND, axis=1)
        sc = jnp.einsum('bqhd,bkhd->bhqk', qc, kb).astype(jnp.float32) * scale + bias
        key_pos = c * CHUNK - pad + jnp.arange(BAND)
        sc = jnp.where((key_pos >= 0)[None, None, None, :], sc, NEG_INF)
        pr = jax.nn.softmax(sc, axis=-1).astype(vb.dtype)
        return jnp.einsum('bhqk,bkhd->bqhd', pr, vb)

    out = lax.map(one_chunk, (q, jnp.arange(nc)))
    out = out.transpose(1, 0, 2, 3, 4).reshape(bsz, s, D_MODEL)
    return out @ w_o


def conv_ffn(x, w_up, dw_w, dw_b, w_down):
    gate, val = jnp.split(x @ w_up, 2, axis=-1)
    gate = causal_dwconv(gate, dw_w, dw_b)
    return (jax.nn.gelu(gate) * val) @ w_down


def _fwd_setup_inputs(seed: int = 0) -> dict:
    key = jax.random.key(seed)
    ks = jax.random.split(key, 32)
    f32 = jnp.float32

    def nrm(k, shape, scale):
        return jax.random.normal(k, shape, f32) * scale

    d_in_even = D_POOL + 2 * D_CONV
    return {
        "x": nrm(ks[0], (BATCH, SEQ, D_MODEL), 1.0),
        "p": nrm(ks[1], (DEPTH, BATCH, SEQ, PLE_DIM), 1.0),
        "mix_w_in": nrm(ks[2], (N_EVEN, D_MODEL, d_in_even), D_MODEL ** -0.5),
        "pool_w": nrm(ks[3], (N_EVEN, len(POOL_WINDOWS), POOL_GROUP, POOL_GROUP), POOL_GROUP ** -0.5),
        "pool_scale": 1.0 + nrm(ks[4], (N_EVEN, D_POOL), 0.1),
        "conv_dw_w": nrm(ks[5], (N_EVEN, CONV_KERNEL, D_CONV), CONV_KERNEL ** -0.5),
        "conv_dw_b": nrm(ks[6], (N_EVEN, D_CONV), 0.02),
        "conv_ln_g": 1.0 + nrm(ks[7], (N_EVEN, D_CONV), 0.02),
        "conv_ln_b": nrm(ks[8], (N_EVEN, D_CONV), 0.02),
        "mix_w_out": nrm(ks[9], (N_EVEN, D_MODEL, D_MODEL), D_MODEL ** -0.5 * DEEPNORM_BETA),
        "attn_w_qkv": nrm(ks[10], (N_ODD, D_MODEL, 3 * D_MODEL), D_MODEL ** -0.5),
        "attn_rel_bias": nrm(ks[11], (N_ODD, N_HEADS, 2 * MAX_REL_DIST + 1), 0.5),
        "attn_w_o": nrm(ks[12], (N_ODD, D_MODEL, D_MODEL), D_MODEL ** -0.5 * DEEPNORM_BETA),
        "ln_mix_g": 1.0 + nrm(ks[13], (DEPTH, D_MODEL), 0.02),
        "ln_mix_b": nrm(ks[14], (DEPTH, D_MODEL), 0.02),
        "ffn_w_up": nrm(ks[15], (DEPTH, D_MODEL, 2 * D_FF), D_MODEL ** -0.5),
        "ffn_dw_w": nrm(ks[16], (DEPTH, FFN_CONV_KERNEL, D_FF), FFN_CONV_KERNEL ** -0.5),
        "ffn_dw_b": nrm(ks[17], (DEPTH, D_FF), 0.02),
        "ffn_w_down": nrm(ks[18], (DEPTH, D_FF, D_MODEL), D_FF ** -0.5 * DEEPNORM_BETA),
        "ple_w_proj": nrm(ks[19], (DEPTH, PLE_DIM, D_MODEL), PLE_DIM ** -0.5),
        "ple_w_gate": nrm(ks[20], (DEPTH, D_MODEL, D_MODEL), D_MODEL ** -0.5),
        "ple_b_gate": nrm(ks[21], (DEPTH, D_MODEL), 0.02),
        "ln_ffn_g": 1.0 + nrm(ks[22], (DEPTH, D_MODEL), 0.02),
        "ln_ffn_b": nrm(ks[23], (DEPTH, D_MODEL), 0.02),
    }


def _fwd_reference(x, p, mix_w_in, pool_w, pool_scale, conv_dw_w, conv_dw_b, conv_ln_g,
              conv_ln_b, mix_w_out, attn_w_qkv, attn_rel_bias, attn_w_o, ln_mix_g,
              ln_mix_b, ffn_w_up, ffn_dw_w, ffn_dw_b, ffn_w_down, ple_w_proj,
              ple_w_gate, ple_b_gate, ln_ffn_g, ln_ffn_b):
    for i in range(DEPTH):
        j = i // 2
        if i % 2 == 0:
            mix = pool_conv_mixer(x, mix_w_in[j], pool_w[j], pool_scale[j], conv_dw_w[j],
                                  conv_dw_b[j], conv_ln_g[j], conv_ln_b[j], mix_w_out[j])
        else:
            mix = chunked_rel_attention(x, attn_w_qkv[j], attn_rel_bias[j], attn_w_o[j])
        x = layer_norm(DEEPNORM_ALPHA * x + mix, ln_mix_g[i], ln_mix_b[i])
        ffn = conv_ffn(x, ffn_w_up[i], ffn_dw_w[i], ffn_dw_b[i], ffn_w_down[i])
        gate = jax.nn.sigmoid(x @ ple_w_gate[i] + ple_b_gate[i])
        ple = gate * (p[i] @ ple_w_proj[i])
        x = layer_norm(DEEPNORM_ALPHA * x + ffn + ple, ln_ffn_g[i], ln_ffn_b[i])
    return x


import jax as _jax
import jax.numpy as _jnp

TWIN_FORMAT = 'train_step'
FWD_PARAMS = ['x', 'p', 'mix_w_in', 'pool_w', 'pool_scale', 'conv_dw_w', 'conv_dw_b', 'conv_ln_g', 'conv_ln_b', 'mix_w_out', 'attn_w_qkv', 'attn_rel_bias', 'attn_w_o', 'ln_mix_g', 'ln_mix_b', 'ffn_w_up', 'ffn_dw_w', 'ffn_dw_b', 'ffn_w_down', 'ple_w_proj', 'ple_w_gate', 'ple_b_gate', 'ln_ffn_g', 'ln_ffn_b']
TWIN_WEIGHTS = ['mix_w_in', 'pool_w', 'pool_scale', 'conv_dw_w', 'conv_dw_b', 'conv_ln_g', 'conv_ln_b', 'mix_w_out', 'attn_w_qkv', 'attn_rel_bias', 'attn_w_o', 'ln_mix_g', 'ln_mix_b', 'ffn_w_up', 'ffn_dw_w', 'ffn_dw_b', 'ffn_w_down', 'ple_w_proj', 'ple_w_gate', 'ple_b_gate', 'ln_ffn_g', 'ln_ffn_b']
TWIN_DIFF_INPUT = 'x'
TWIN_INPUTS = ['x', 'p', 'mix_w_in', 'pool_w', 'pool_scale', 'conv_dw_w', 'conv_dw_b', 'conv_ln_g', 'conv_ln_b', 'mix_w_out', 'attn_w_qkv', 'attn_rel_bias', 'attn_w_o', 'ln_mix_g', 'ln_mix_b', 'ffn_w_up', 'ffn_dw_w', 'ffn_dw_b', 'ffn_w_down', 'ple_w_proj', 'ple_w_gate', 'ple_b_gate', 'ln_ffn_g', 'ln_ffn_b', 'loss_target', 'm_mix_w_in', 'm_pool_w', 'm_pool_scale', 'm_conv_dw_w', 'm_conv_dw_b', 'm_conv_ln_g', 'm_conv_ln_b', 'm_mix_w_out', 'm_attn_w_qkv', 'm_attn_rel_bias', 'm_attn_w_o', 'm_ln_mix_g', 'm_ln_mix_b', 'm_ffn_w_up', 'm_ffn_dw_w', 'm_ffn_dw_b', 'm_ffn_w_down', 'm_ple_w_proj', 'm_ple_w_gate', 'm_ple_b_gate', 'm_ln_ffn_g', 'm_ln_ffn_b', 'v_mix_w_in', 'v_pool_w', 'v_pool_scale', 'v_conv_dw_w', 'v_conv_dw_b', 'v_conv_ln_g', 'v_conv_ln_b', 'v_mix_w_out', 'v_attn_w_qkv', 'v_attn_rel_bias', 'v_attn_w_o', 'v_ln_mix_g', 'v_ln_mix_b', 'v_ffn_w_up', 'v_ffn_dw_w', 'v_ffn_dw_b', 'v_ffn_w_down', 'v_ple_w_proj', 'v_ple_w_gate', 'v_ple_b_gate', 'v_ln_ffn_g', 'v_ln_ffn_b']
TWIN_OUTPUTS = ['loss', 'grad_x', 'grad_mix_w_in', 'grad_pool_w', 'grad_pool_scale', 'grad_conv_dw_w', 'grad_conv_dw_b', 'grad_conv_ln_g', 'grad_conv_ln_b', 'grad_mix_w_out', 'grad_attn_w_qkv', 'grad_attn_rel_bias', 'grad_attn_w_o', 'grad_ln_mix_g', 'grad_ln_mix_b', 'grad_ffn_w_up', 'grad_ffn_dw_w', 'grad_ffn_dw_b', 'grad_ffn_w_down', 'grad_ple_w_proj', 'grad_ple_w_gate', 'grad_ple_b_gate', 'grad_ln_ffn_g', 'grad_ln_ffn_b', 'delta_mix_w_in', 'delta_pool_w', 'delta_pool_scale', 'delta_conv_dw_w', 'delta_conv_dw_b', 'delta_conv_ln_g', 'delta_conv_ln_b', 'delta_mix_w_out', 'delta_attn_w_qkv', 'delta_attn_rel_bias', 'delta_attn_w_o', 'delta_ln_mix_g', 'delta_ln_mix_b', 'delta_ffn_w_up', 'delta_ffn_dw_w', 'delta_ffn_dw_b', 'delta_ffn_w_down', 'delta_ple_w_proj', 'delta_ple_w_gate', 'delta_ple_b_gate', 'delta_ln_ffn_g', 'delta_ln_ffn_b', 'new_m_mix_w_in', 'new_m_pool_w', 'new_m_pool_scale', 'new_m_conv_dw_w', 'new_m_conv_dw_b', 'new_m_conv_ln_g', 'new_m_conv_ln_b', 'new_m_mix_w_out', 'new_m_attn_w_qkv', 'new_m_attn_rel_bias', 'new_m_attn_w_o', 'new_m_ln_mix_g', 'new_m_ln_mix_b', 'new_m_ffn_w_up', 'new_m_ffn_dw_w', 'new_m_ffn_dw_b', 'new_m_ffn_w_down', 'new_m_ple_w_proj', 'new_m_ple_w_gate', 'new_m_ple_b_gate', 'new_m_ln_ffn_g', 'new_m_ln_ffn_b', 'new_v_mix_w_in', 'new_v_pool_w', 'new_v_pool_scale', 'new_v_conv_dw_w', 'new_v_conv_dw_b', 'new_v_conv_ln_g', 'new_v_conv_ln_b', 'new_v_mix_w_out', 'new_v_attn_w_qkv', 'new_v_attn_rel_bias', 'new_v_attn_w_o', 'new_v_ln_mix_g', 'new_v_ln_mix_b', 'new_v_ffn_w_up', 'new_v_ffn_dw_w', 'new_v_ffn_dw_b', 'new_v_ffn_w_down', 'new_v_ple_w_proj', 'new_v_ple_w_gate', 'new_v_ple_b_gate', 'new_v_ln_ffn_g', 'new_v_ln_ffn_b']
TWIN_LEAF_KINDS = {'loss': 'loss', 'grad_x': 'grad_x', 'grad_mix_w_in': 'grad_w', 'grad_pool_w': 'grad_w', 'grad_pool_scale': 'grad_w', 'grad_conv_dw_w': 'grad_w', 'grad_conv_dw_b': 'grad_w', 'grad_conv_ln_g': 'grad_w', 'grad_conv_ln_b': 'grad_w', 'grad_mix_w_out': 'grad_w', 'grad_attn_w_qkv': 'grad_w', 'grad_attn_rel_bias': 'grad_w', 'grad_attn_w_o': 'grad_w', 'grad_ln_mix_g': 'grad_w', 'grad_ln_mix_b': 'grad_w', 'grad_ffn_w_up': 'grad_w', 'grad_ffn_dw_w': 'grad_w', 'grad_ffn_dw_b': 'grad_w', 'grad_ffn_w_down': 'grad_w', 'grad_ple_w_proj': 'grad_w', 'grad_ple_w_gate': 'grad_w', 'grad_ple_b_gate': 'grad_w', 'grad_ln_ffn_g': 'grad_w', 'grad_ln_ffn_b': 'grad_w', 'delta_mix_w_in': 'delta_w', 'delta_pool_w': 'delta_w', 'delta_pool_scale': 'delta_w', 'delta_conv_dw_w': 'delta_w', 'delta_conv_dw_b': 'delta_w', 'delta_conv_ln_g': 'delta_w', 'delta_conv_ln_b': 'delta_w', 'delta_mix_w_out': 'delta_w', 'delta_attn_w_qkv': 'delta_w', 'delta_attn_rel_bias': 'delta_w', 'delta_attn_w_o': 'delta_w', 'delta_ln_mix_g': 'delta_w', 'delta_ln_mix_b': 'delta_w', 'delta_ffn_w_up': 'delta_w', 'delta_ffn_dw_w': 'delta_w', 'delta_ffn_dw_b': 'delta_w', 'delta_ffn_w_down': 'delta_w', 'delta_ple_w_proj': 'delta_w', 'delta_ple_w_gate': 'delta_w', 'delta_ple_b_gate': 'delta_w', 'delta_ln_ffn_g': 'delta_w', 'delta_ln_ffn_b': 'delta_w', 'new_m_mix_w_in': 'new_m', 'new_m_pool_w': 'new_m', 'new_m_pool_scale': 'new_m', 'new_m_conv_dw_w': 'new_m', 'new_m_conv_dw_b': 'new_m', 'new_m_conv_ln_g': 'new_m', 'new_m_conv_ln_b': 'new_m', 'new_m_mix_w_out': 'new_m', 'new_m_attn_w_qkv': 'new_m', 'new_m_attn_rel_bias': 'new_m', 'new_m_attn_w_o': 'new_m', 'new_m_ln_mix_g': 'new_m', 'new_m_ln_mix_b': 'new_m', 'new_m_ffn_w_up': 'new_m', 'new_m_ffn_dw_w': 'new_m', 'new_m_ffn_dw_b': 'new_m', 'new_m_ffn_w_down': 'new_m', 'new_m_ple_w_proj': 'new_m', 'new_m_ple_w_gate': 'new_m', 'new_m_ple_b_gate': 'new_m', 'new_m_ln_ffn_g': 'new_m', 'new_m_ln_ffn_b': 'new_m', 'new_v_mix_w_in': 'new_v', 'new_v_pool_w': 'new_v', 'new_v_pool_scale': 'new_v', 'new_v_conv_dw_w': 'new_v', 'new_v_conv_dw_b': 'new_v', 'new_v_conv_ln_g': 'new_v', 'new_v_conv_ln_b': 'new_v', 'new_v_mix_w_out': 'new_v', 'new_v_attn_w_qkv': 'new_v', 'new_v_attn_rel_bias': 'new_v', 'new_v_attn_w_o': 'new_v', 'new_v_ln_mix_g': 'new_v', 'new_v_ln_mix_b': 'new_v', 'new_v_ffn_w_up': 'new_v', 'new_v_ffn_dw_w': 'new_v', 'new_v_ffn_dw_b': 'new_v', 'new_v_ffn_w_down': 'new_v', 'new_v_ple_w_proj': 'new_v', 'new_v_ple_w_gate': 'new_v', 'new_v_ple_b_gate': 'new_v', 'new_v_ln_ffn_g': 'new_v', 'new_v_ln_ffn_b': 'new_v'}


def _forward(args):
    return _fwd_reference(*[args[k] for k in FWD_PARAMS])


def _output_shape():
    out = _jax.eval_shape(lambda: _forward(_fwd_setup_inputs(0)))
    return out.shape, out.dtype

N_MICROBATCH = 1
ADAM_LR = 0.001
ADAM_B1 = 0.9
ADAM_B2 = 0.999
ADAM_EPS = 1e-08
ADAM_WD = 0.01
ADAM_STEP = 10
PER_EXAMPLE_BATCH_AXIS = {'x': 0, 'p': 1, 'loss_target': 0}
SHARED_INPUTS = []
_WEIGHT_DTYPES = {'mix_w_in': _jnp.float32, 'pool_w': _jnp.float32, 'pool_scale': _jnp.float32, 'conv_dw_w': _jnp.float32, 'conv_dw_b': _jnp.float32, 'conv_ln_g': _jnp.float32, 'conv_ln_b': _jnp.float32, 'mix_w_out': _jnp.float32, 'attn_w_qkv': _jnp.float32, 'attn_rel_bias': _jnp.float32, 'attn_w_o': _jnp.float32, 'ln_mix_g': _jnp.float32, 'ln_mix_b': _jnp.float32, 'ffn_w_up': _jnp.float32, 'ffn_dw_w': _jnp.float32, 'ffn_dw_b': _jnp.float32, 'ffn_w_down': _jnp.float32, 'ple_w_proj': _jnp.float32, 'ple_w_gate': _jnp.float32, 'ple_b_gate': _jnp.float32, 'ln_ffn_g': _jnp.float32, 'ln_ffn_b': _jnp.float32}
MOMENT_SCALE = {'mix_w_in': 3.631429e-02, 'pool_w': 5.042978e-02, 'pool_scale': 4.915771e-02, 'conv_dw_w': 3.555532e-02, 'conv_dw_b': 9.735285e-02, 'conv_ln_g': 4.734795e-02, 'conv_ln_b': 6.105396e-02, 'mix_w_out': 9.164255e-02, 'attn_w_qkv': 9.712117e-03, 'attn_rel_bias': 3.222536e-03, 'attn_w_o': 2.146230e-02, 'ln_mix_g': 7.810784e-01, 'ln_mix_b': 4.079551e-01, 'ffn_w_up': 2.284789e-02, 'ffn_dw_w': 2.302838e-02, 'ffn_dw_b': 2.242033e-02, 'ffn_w_down': 7.462319e-02, 'ple_w_proj': 6.112926e-02, 'ple_w_gate': 2.385125e-02, 'ple_b_gate': 3.710994e-02, 'ln_ffn_g': 2.260994e+01, 'ln_ffn_b': 1.194843e+00}


def _to_microbatches(a, axis):
    t = _jnp.moveaxis(a, axis, 0)
    t = t.reshape((N_MICROBATCH, t.shape[0] // N_MICROBATCH) + t.shape[1:])
    return _jnp.moveaxis(t, 1, axis + 1)


def setup_inputs(seed: int = 0) -> dict:
    inp = _fwd_setup_inputs(seed)
    key = _jax.random.fold_in(_jax.random.key(seed), 7919)
    shape, _ = _output_shape()
    out = dict(inp)
    out["loss_target"] = _jax.random.normal(_jax.random.fold_in(key, 0), shape, _jnp.float32)
    for i, name in enumerate(TWIN_WEIGHTS):
        w = inp[name].astype(_jnp.float32)
        if MOMENT_SCALE is None:
            s = _jnp.sqrt(_jnp.mean(_jnp.square(w)) + 1e-30)
        else:
            s = MOMENT_SCALE[name]
        km, kv = _jax.random.split(_jax.random.fold_in(key, i + 1))
        out[name] = w
        out["m_" + name] = s * _jax.random.normal(km, w.shape, _jnp.float32)
        out["v_" + name] = (s * s) * _jax.random.uniform(kv, w.shape, _jnp.float32, 0.5, 1.5)
    if N_MICROBATCH > 1:
        for name, axis in PER_EXAMPLE_BATCH_AXIS.items():
            out[name] = _to_microbatches(out[name], axis)
    return {'x': out['x'], 'p': out['p'], 'mix_w_in': out['mix_w_in'], 'pool_w': out['pool_w'], 'pool_scale': out['pool_scale'], 'conv_dw_w': out['conv_dw_w'], 'conv_dw_b': out['conv_dw_b'], 'conv_ln_g': out['conv_ln_g'], 'conv_ln_b': out['conv_ln_b'], 'mix_w_out': out['mix_w_out'], 'attn_w_qkv': out['attn_w_qkv'], 'attn_rel_bias': out['attn_rel_bias'], 'attn_w_o': out['attn_w_o'], 'ln_mix_g': out['ln_mix_g'], 'ln_mix_b': out['ln_mix_b'], 'ffn_w_up': out['ffn_w_up'], 'ffn_dw_w': out['ffn_dw_w'], 'ffn_dw_b': out['ffn_dw_b'], 'ffn_w_down': out['ffn_w_down'], 'ple_w_proj': out['ple_w_proj'], 'ple_w_gate': out['ple_w_gate'], 'ple_b_gate': out['ple_b_gate'], 'ln_ffn_g': out['ln_ffn_g'], 'ln_ffn_b': out['ln_ffn_b'], 'loss_target': out['loss_target'], 'm_mix_w_in': out['m_mix_w_in'], 'm_pool_w': out['m_pool_w'], 'm_pool_scale': out['m_pool_scale'], 'm_conv_dw_w': out['m_conv_dw_w'], 'm_conv_dw_b': out['m_conv_dw_b'], 'm_conv_ln_g': out['m_conv_ln_g'], 'm_conv_ln_b': out['m_conv_ln_b'], 'm_mix_w_out': out['m_mix_w_out'], 'm_attn_w_qkv': out['m_attn_w_qkv'], 'm_attn_rel_bias': out['m_attn_rel_bias'], 'm_attn_w_o': out['m_attn_w_o'], 'm_ln_mix_g': out['m_ln_mix_g'], 'm_ln_mix_b': out['m_ln_mix_b'], 'm_ffn_w_up': out['m_ffn_w_up'], 'm_ffn_dw_w': out['m_ffn_dw_w'], 'm_ffn_dw_b': out['m_ffn_dw_b'], 'm_ffn_w_down': out['m_ffn_w_down'], 'm_ple_w_proj': out['m_ple_w_proj'], 'm_ple_w_gate': out['m_ple_w_gate'], 'm_ple_b_gate': out['m_ple_b_gate'], 'm_ln_ffn_g': out['m_ln_ffn_g'], 'm_ln_ffn_b': out['m_ln_ffn_b'], 'v_mix_w_in': out['v_mix_w_in'], 'v_pool_w': out['v_pool_w'], 'v_pool_scale': out['v_pool_scale'], 'v_conv_dw_w': out['v_conv_dw_w'], 'v_conv_dw_b': out['v_conv_dw_b'], 'v_conv_ln_g': out['v_conv_ln_g'], 'v_conv_ln_b': out['v_conv_ln_b'], 'v_mix_w_out': out['v_mix_w_out'], 'v_attn_w_qkv': out['v_attn_w_qkv'], 'v_attn_rel_bias': out['v_attn_rel_bias'], 'v_attn_w_o': out['v_attn_w_o'], 'v_ln_mix_g': out['v_ln_mix_g'], 'v_ln_mix_b': out['v_ln_mix_b'], 'v_ffn_w_up': out['v_ffn_w_up'], 'v_ffn_dw_w': out['v_ffn_dw_w'], 'v_ffn_dw_b': out['v_ffn_dw_b'], 'v_ffn_w_down': out['v_ffn_w_down'], 'v_ple_w_proj': out['v_ple_w_proj'], 'v_ple_w_gate': out['v_ple_w_gate'], 'v_ple_b_gate': out['v_ple_b_gate'], 'v_ln_ffn_g': out['v_ln_ffn_g'], 'v_ln_ffn_b': out['v_ln_ffn_b']}


def _loss(weights, diff, rest, loss_target):
    with _jax.named_scope("forward"):
        args = {**rest, TWIN_DIFF_INPUT: diff, **{k: w.astype(_WEIGHT_DTYPES[k]) for k, w in weights.items()}}
        y = _forward(args)
    with _jax.named_scope("loss_head"):
        err = _jnp.square(y.astype(_jnp.float32) - loss_target)
        return 0.5 * _jnp.sum(_jnp.mean(err, axis=-1)) if err.ndim else 0.5 * err


def _adamw(w, g, m, v):
    m = ADAM_B1 * m + (1.0 - ADAM_B1) * g
    v = ADAM_B2 * v + (1.0 - ADAM_B2) * _jnp.square(g)
    m_hat = m / (1.0 - ADAM_B1 ** ADAM_STEP)
    v_hat = v / (1.0 - ADAM_B2 ** ADAM_STEP)
    delta = -ADAM_LR * (m_hat / (_jnp.sqrt(v_hat) + ADAM_EPS) + ADAM_WD * w)
    return delta, m, v


def reference(x, p, mix_w_in, pool_w, pool_scale, conv_dw_w, conv_dw_b, conv_ln_g, conv_ln_b, mix_w_out, attn_w_qkv, attn_rel_bias, attn_w_o, ln_mix_g, ln_mix_b, ffn_w_up, ffn_dw_w, ffn_dw_b, ffn_w_down, ple_w_proj, ple_w_gate, ple_b_gate, ln_ffn_g, ln_ffn_b, loss_target, m_mix_w_in, m_pool_w, m_pool_scale, m_conv_dw_w, m_conv_dw_b, m_conv_ln_g, m_conv_ln_b, m_mix_w_out, m_attn_w_qkv, m_attn_rel_bias, m_attn_w_o, m_ln_mix_g, m_ln_mix_b, m_ffn_w_up, m_ffn_dw_w, m_ffn_dw_b, m_ffn_w_down, m_ple_w_proj, m_ple_w_gate, m_ple_b_gate, m_ln_ffn_g, m_ln_ffn_b, v_mix_w_in, v_pool_w, v_pool_scale, v_conv_dw_w, v_conv_dw_b, v_conv_ln_g, v_conv_ln_b, v_mix_w_out, v_attn_w_qkv, v_attn_rel_bias, v_attn_w_o, v_ln_mix_g, v_ln_mix_b, v_ffn_w_up, v_ffn_dw_w, v_ffn_dw_b, v_ffn_w_down, v_ple_w_proj, v_ple_w_gate, v_ple_b_gate, v_ln_ffn_g, v_ln_ffn_b):
    given = dict(x=x, p=p, mix_w_in=mix_w_in, pool_w=pool_w, pool_scale=pool_scale, conv_dw_w=conv_dw_w, conv_dw_b=conv_dw_b, conv_ln_g=conv_ln_g, conv_ln_b=conv_ln_b, mix_w_out=mix_w_out, attn_w_qkv=attn_w_qkv, attn_rel_bias=attn_rel_bias, attn_w_o=attn_w_o, ln_mix_g=ln_mix_g, ln_mix_b=ln_mix_b, ffn_w_up=ffn_w_up, ffn_dw_w=ffn_dw_w, ffn_dw_b=ffn_dw_b, ffn_w_down=ffn_w_down, ple_w_proj=ple_w_proj, ple_w_gate=ple_w_gate, ple_b_gate=ple_b_gate, ln_ffn_g=ln_ffn_g, ln_ffn_b=ln_ffn_b, loss_target=loss_target, m_mix_w_in=m_mix_w_in, m_pool_w=m_pool_w, m_pool_scale=m_pool_scale, m_conv_dw_w=m_conv_dw_w, m_conv_dw_b=m_conv_dw_b, m_conv_ln_g=m_conv_ln_g, m_conv_ln_b=m_conv_ln_b, m_mix_w_out=m_mix_w_out, m_attn_w_qkv=m_attn_w_qkv, m_attn_rel_bias=m_attn_rel_bias, m_attn_w_o=m_attn_w_o, m_ln_mix_g=m_ln_mix_g, m_ln_mix_b=m_ln_mix_b, m_ffn_w_up=m_ffn_w_up, m_ffn_dw_w=m_ffn_dw_w, m_ffn_dw_b=m_ffn_dw_b, m_ffn_w_down=m_ffn_w_down, m_ple_w_proj=m_ple_w_proj, m_ple_w_gate=m_ple_w_gate, m_ple_b_gate=m_ple_b_gate, m_ln_ffn_g=m_ln_ffn_g, m_ln_ffn_b=m_ln_ffn_b, v_mix_w_in=v_mix_w_in, v_pool_w=v_pool_w, v_pool_scale=v_pool_scale, v_conv_dw_w=v_conv_dw_w, v_conv_dw_b=v_conv_dw_b, v_conv_ln_g=v_conv_ln_g, v_conv_ln_b=v_conv_ln_b, v_mix_w_out=v_mix_w_out, v_attn_w_qkv=v_attn_w_qkv, v_attn_rel_bias=v_attn_rel_bias, v_attn_w_o=v_attn_w_o, v_ln_mix_g=v_ln_mix_g, v_ln_mix_b=v_ln_mix_b, v_ffn_w_up=v_ffn_w_up, v_ffn_dw_w=v_ffn_dw_w, v_ffn_dw_b=v_ffn_dw_b, v_ffn_w_down=v_ffn_w_down, v_ple_w_proj=v_ple_w_proj, v_ple_w_gate=v_ple_w_gate, v_ple_b_gate=v_ple_b_gate, v_ln_ffn_g=v_ln_ffn_g, v_ln_ffn_b=v_ln_ffn_b)
    weights = {n: given[n] for n in TWIN_WEIGHTS}
    shared = {n: given[n] for n in SHARED_INPUTS}
    per_example = {n: given[n] for n in ['x', 'p']}
    grad_fn = _jax.value_and_grad(_loss, argnums=(0, 1))

    def one_microbatch(ex, loss_target):
        ex = dict(ex)
        diff = ex.pop(TWIN_DIFF_INPUT)
        return grad_fn(weights, diff, {**shared, **ex}, loss_target)

    if N_MICROBATCH == 1:
        loss, (grad_w, grad_x) = one_microbatch(per_example, given["loss_target"])
    else:
        def body(carry, xs):
            loss_sum, grad_sum = carry
            l_k, (gw_k, gx_k) = one_microbatch(xs[0], xs[1])
            with _jax.named_scope("update"):
                return (loss_sum + l_k, _jax.tree.map(_jnp.add, grad_sum, gw_k)), gx_k

        init = (_jnp.zeros((), _jnp.float32), _jax.tree.map(_jnp.zeros_like, weights))
        (loss, grad_w), grad_x = _jax.lax.scan(body, init, (per_example, given["loss_target"]))
    with _jax.named_scope("update"):
        delta_w, new_m, new_v = {}, {}, {}
        for n in TWIN_WEIGHTS:
            delta_w[n], new_m[n], new_v[n] = _adamw(weights[n], grad_w[n], given["m_" + n], given["v_" + n])
    return (loss, grad_x, *[grad_w[n] for n in TWIN_WEIGHTS], *[delta_w[n] for n in TWIN_WEIGHTS],
            *[new_m[n] for n in TWIN_WEIGHTS], *[new_v[n] for n in TWIN_WEIGHTS])
```

```python
import functools
import math

import jax
import jax.numpy as jnp
from jax import lax
from jax.experimental import pallas as pl
from jax.experimental.pallas import tpu as pltpu

F32 = jnp.float32
BF16 = jnp.bfloat16

N_DEV = 8
D_MODEL = 1024
D_POOL = 512
D_CONV = 512
POOL_WINDOWS = (2, 4, 8, 16)
POOL_GROUP = 128
CONV_KERNEL = 31
CHUNK = 64
HEAD_DIM = 64
N_HEADS = 16
LEFT_CHUNKS = 8
BAND = (LEFT_CHUNKS + 1) * CHUNK
MAX_REL = 256
D_FF = 2816
PLE_DIM = 256
ALPHA = 4.0 ** 0.25
LN_EPS = 1e-5
NEG_INF = -1e30
ADAM_LR, ADAM_B1, ADAM_B2, ADAM_EPS, ADAM_WD, ADAM_STEP = 0.001, 0.9, 0.999, 1e-08, 0.01, 10

Q_BLOCK = 4 * CHUNK
KV_PAD = LEFT_CHUNKS * CHUNK
KV_SPAN = KV_PAD + Q_BLOCK
CONV_HALO = 32
FFN_HALO = 8
LANES = 1024
ADAM_ROWS = 256
VMEM_LIMIT = 56 * 1024 * 1024


def _cparams(sem=None):
    return pltpu.CompilerParams(dimension_semantics=sem, vmem_limit_bytes=VMEM_LIMIT)


def _tile(dim, pref):
    if dim <= pref:
        return dim
    t = pref - pref % 128
    while t >= 128:
        if dim % t == 0:
            return t
        t -= 128
    return dim


def _sigmoid(x):
    return 1.0 / (1.0 + jnp.exp(-x))


def _bdot(a, b, dn=(((1,), (0,)), ((), ()))):
    return lax.dot_general(a.astype(BF16), b.astype(BF16), dn, preferred_element_type=F32)


NT = (((1,), (1,)), ((), ()))
TN = (((0,), (0,)), ((), ()))


def _mm(a, b, *, ta=False, tb=False, add=None, add_scale=1.0, out_dtype=F32, tm=512, tn=512, tk=1024, name):
    if ta:
        K, M = a.shape
    else:
        M, K = a.shape
    if tb:
        N, kb = b.shape
    else:
        kb, N = b.shape
    assert K == kb, (a.shape, b.shape)
    tm, tn, tk = _tile(M, tm), _tile(N, tn), _tile(K, tk)
    nk = K // tk
    a_spec = pl.BlockSpec((tk, tm), lambda i, j, k: (k, i)) if ta else pl.BlockSpec((tm, tk), lambda i, j, k: (i, k))
    b_spec = pl.BlockSpec((tn, tk), lambda i, j, k: (j, k)) if tb else pl.BlockSpec((tk, tn), lambda i, j, k: (k, j))
    dn = (((0 if ta else 1,), (1 if tb else 0,)), ((), ()))
    has_add = add is not None

    def body(*refs):
        if has_add:
            a_ref, b_ref, add_ref, o_ref, acc = refs
        else:
            a_ref, b_ref, o_ref, acc = refs
        k = pl.program_id(2)

        @pl.when(k == 0)
        def _():
            acc[...] = jnp.zeros_like(acc)

        acc[...] += _bdot(a_ref[...], b_ref[...], dn)

        @pl.when(k == nk - 1)
        def _():
            r = acc[...]
            if has_add:
                r = r + add_scale * add_ref[...]
            o_ref[...] = r.astype(out_dtype)

    in_specs = [a_spec, b_spec]
    args = [a, b]
    if has_add:
        in_specs.append(pl.BlockSpec((tm, tn), lambda i, j, k: (i, j)))
        args.append(add)
    return pl.pallas_call(
        body,
        out_shape=jax.ShapeDtypeStruct((M, N), out_dtype),
        grid=(M // tm, N // tn, nk),
        in_specs=in_specs,
        out_specs=pl.BlockSpec((tm, tn), lambda i, j, k: (i, j)),
        scratch_shapes=[pltpu.VMEM((tm, tn), F32)],
        compiler_params=_cparams(("parallel", "parallel", "arbitrary")),
        name=name,
    )(*args)


def _layer_norm_rows(z, g, b):
    mu = jnp.mean(z, axis=-1, keepdims=True)
    zc = z - mu
    var = jnp.mean(zc * zc, axis=-1, keepdims=True)
    return zc * lax.rsqrt(var + LN_EPS) * g + b


def _proj_ln(res, a, w, ln_g, ln_b, *, ple=None, ts=256, name):
    S, D = res.shape
    ka = a.shape[1]
    has_ple = ple is not None
    row = lambda i: (i, 0)
    fix = lambda i: (0, 0)

    def body(*refs):
        if has_ple:
            res_ref, a_ref, w_ref, g_ref, b_ref, wg_ref, bg_ref, p_ref, wp_ref, z_ref, r_ref, gate_ref, proj_ref = refs
        else:
            res_ref, a_ref, w_ref, g_ref, b_ref, z_ref, r_ref = refs
        res_t = res_ref[...]
        acc = _bdot(a_ref[...], w_ref[...])
        if has_ple:
            gate = _sigmoid(_bdot(res_t, wg_ref[...]) + bg_ref[...])
            proj = _bdot(p_ref[...], wp_ref[...])
            gate_ref[...] = gate
            proj_ref[...] = proj
            acc = acc + gate * proj
        z = ALPHA * res_t + acc
        z_ref[...] = z
        r_ref[...] = _layer_norm_rows(z, g_ref[...], b_ref[...])

    in_specs = [pl.BlockSpec((ts, D), row), pl.BlockSpec((ts, ka), row), pl.BlockSpec((ka, D), fix),
                pl.BlockSpec((1, D), fix), pl.BlockSpec((1, D), fix)]
    args = [res, a, w, ln_g.reshape(1, D), ln_b.reshape(1, D)]
    n_out = 2
    if has_ple:
        wg, bg, p, wp = ple
        in_specs += [pl.BlockSpec((D, D), fix), pl.BlockSpec((1, D), fix), pl.BlockSpec((ts, PLE_DIM), row),
                     pl.BlockSpec((PLE_DIM, D), fix)]
        args += [wg, bg.reshape(1, D), p, wp]
        n_out = 4
    return pl.pallas_call(
        body,
        out_shape=[jax.ShapeDtypeStruct((S, D), F32)] * n_out,
        grid=(S // ts,),
        in_specs=in_specs,
        out_specs=[pl.BlockSpec((ts, D), row)] * n_out,
        compiler_params=_cparams(("parallel",)),
        name=name,
    )(*args)


def _mixer_fwd(u, pool_w, pool_scale, conv_w, conv_b, cln_g, cln_b, *, ts=256):
    S = u.shape[0]
    hb = CONV_HALO
    nh = ts // hb

    def body(u_ref, uh_ref, pw_ref, ps_ref, cw_ref, cb_ref, g_ref, b_ref, y_ref, d_ref, sta, stg):
        i = pl.program_id(0)
        first = i == 0
        sta[pl.ds(0, hb), :] = jnp.where(first, 0.0, uh_ref[:, 0:D_POOL])
        sta[pl.ds(hb, ts), :] = u_ref[:, 0:D_POOL]
        glu_h = uh_ref[:, D_POOL:D_POOL + D_CONV] * _sigmoid(uh_ref[:, D_POOL + D_CONV:])
        stg[pl.ds(0, hb), :] = jnp.where(first, 0.0, glu_h)
        stg[pl.ds(hb, ts), :] = u_ref[:, D_POOL:D_POOL + D_CONV] * _sigmoid(u_ref[:, D_POOL + D_CONV:])

        pos = (i * ts + lax.broadcasted_iota(jnp.int32, (ts, 1), 0) + 1).astype(F32)
        for g, w in enumerate(POOL_WINDOWS):
            lanes = pl.ds(g * POOL_GROUP, POOL_GROUP)
            a_g = sta[pl.ds(hb, ts), lanes]
            s = a_g
            for j in range(1, w):
                s = s + sta[pl.ds(hb - j, ts), lanes]
            d_g = s / jnp.minimum(pos, float(w)) - a_g
            d_ref[:, lanes] = d_g.astype(BF16)
            y_ref[:, lanes] = (_bdot(d_g, pw_ref[g]) * ps_ref[:, lanes]).astype(BF16)

        acc = jnp.zeros((ts, D_CONV), F32)
        for k in range(CONV_KERNEL):
            acc = acc + cw_ref[k:k + 1, :] * stg[pl.ds(hb - (CONV_KERNEL - 1) + k, ts), :]
        hc = acc + cb_ref[...]
        ln = _layer_norm_rows(hc, g_ref[...], b_ref[...])
        y_ref[:, D_POOL:] = (ln * _sigmoid(ln)).astype(BF16)

    fix2 = lambda i: (0, 0)
    return pl.pallas_call(
        body,
        out_shape=[jax.ShapeDtypeStruct((S, D_MODEL), BF16), jax.ShapeDtypeStruct((S, D_POOL), BF16)],
        grid=(S // ts,),
        in_specs=[pl.BlockSpec((ts, 3 * D_POOL), lambda i: (i, 0)),
                  pl.BlockSpec((hb, 3 * D_POOL), lambda i: (jnp.maximum(i * nh - 1, 0), 0)),
                  pl.BlockSpec((4, POOL_GROUP, POOL_GROUP), lambda i: (0, 0, 0)),
                  pl.BlockSpec((1, D_POOL), fix2), pl.BlockSpec((CONV_KERNEL, D_CONV), fix2),
                  pl.BlockSpec((1, D_CONV), fix2), pl.BlockSpec((1, D_CONV), fix2), pl.BlockSpec((1, D_CONV), fix2)],
        out_specs=[pl.BlockSpec((ts, D_MODEL), lambda i: (i, 0)), pl.BlockSpec((ts, D_POOL), lambda i: (i, 0))],
        scratch_shapes=[pltpu.VMEM((hb + ts, D_POOL), F32), pltpu.VMEM((hb + ts, D_CONV), F32)],
        compiler_params=_cparams(("parallel",)),
        name="mixer_fwd",
    )(u, u, pool_w, pool_scale.reshape(1, D_POOL), conv_w, conv_b.reshape(1, D_CONV), cln_g.reshape(1, D_CONV),
      cln_b.reshape(1, D_CONV))


def _mixer_bwd(u, d, dycat, pool_w, pool_scale, conv_w, conv_b, cln_g, cln_b, *, ts=256):
    S = u.shape[0]
    hb = CONV_HALO
    nh = ts // hb
    n = S // ts
    te = ts + hb
    K = CONV_KERNEL

    def body(u_ref, up_ref, un_ref, d_ref, dy_ref, dyn_ref, pw_ref, ps_ref, cw_ref, cb_ref, g_ref, b_ref,
             du_ref, dpw_ref, dps_ref, dcw_ref, dcb_ref, dg_ref, db_ref, stg, std, sth):
        i = pl.program_id(0)
        first = i == 0
        last = i == n - 1

        @pl.when(first)
        def _():
            dpw_ref[...] = jnp.zeros_like(dpw_ref)
            dps_ref[...] = jnp.zeros_like(dps_ref)
            dcw_ref[...] = jnp.zeros_like(dcw_ref)
            dcb_ref[...] = jnp.zeros_like(dcb_ref)
            dg_ref[...] = jnp.zeros_like(dg_ref)
            db_ref[...] = jnp.zeros_like(db_ref)

        pos_e = (i * ts + lax.broadcasted_iota(jnp.int32, (te, 1), 0) + 1).astype(F32)
        dya = dy_ref[:, 0:D_POOL]
        dya_n = jnp.where(last, 0.0, dyn_ref[:, 0:D_POOL])
        for g, w in enumerate(POOL_WINDOWS):
            lanes = pl.ds(g * POOL_GROUP, POOL_GROUP)
            sl = slice(g * POOL_GROUP, (g + 1) * POOL_GROUP)
            pw = pw_ref[g]
            scale = ps_ref[:, lanes]
            d_g = d_ref[:, lanes]
            pre = _bdot(d_g, pw)
            dps_ref[:, lanes] += jnp.sum(dya[:, sl] * pre, axis=0, keepdims=True)
            dys = dya[:, sl] * scale
            dpw_ref[g] += _bdot(d_g, dys, TN)
            dys_e = jnp.concatenate([dys, dya_n[:, sl] * scale], axis=0)
            dd = _bdot(dys_e, pw, NT)
            std[:, lanes] = dd / jnp.minimum(pos_e, float(w))
            da = -dd[0:ts]
            for m in range(w):
                da = da + std[pl.ds(m, ts), lanes]
            du_ref[:, lanes] = da.astype(BF16)

        glu_p = up_ref[:, D_POOL:D_POOL + D_CONV] * _sigmoid(up_ref[:, D_POOL + D_CONV:])
        stg[pl.ds(0, hb), :] = jnp.where(first, 0.0, glu_p)
        bv = u_ref[:, D_POOL:D_POOL + D_CONV]
        sg = _sigmoid(u_ref[:, D_POOL + D_CONV:])
        stg[pl.ds(hb, ts), :] = bv * sg
        glu_n = un_ref[:, D_POOL:D_POOL + D_CONV] * _sigmoid(un_ref[:, D_POOL + D_CONV:])
        stg[pl.ds(hb + ts, hb), :] = jnp.where(last, 0.0, glu_n)

        acc = jnp.zeros((te, D_CONV), F32)
        for k in range(K):
            acc = acc + cw_ref[k:k + 1, :] * stg[pl.ds(hb - (K - 1) + k, te), :]
        hc = acc + cb_ref[...]
        mu = jnp.mean(hc, axis=-1, keepdims=True)
        hcc = hc - mu
        rstd = lax.rsqrt(jnp.mean(hcc * hcc, axis=-1, keepdims=True) + LN_EPS)
        xh = hcc * rstd
        ln = xh * g_ref[...] + b_ref[...]
        sl_ = _sigmoid(ln)
        dyb = jnp.concatenate([dy_ref[:, D_POOL:], jnp.where(last, 0.0, dyn_ref[:, D_POOL:])], axis=0)
        dln = dyb * (sl_ * (1.0 + ln * (1.0 - sl_)))
        dxh = dln * g_ref[...]
        dhc = rstd * (dxh - jnp.mean(dxh, axis=-1, keepdims=True) - xh * jnp.mean(dxh * xh, axis=-1, keepdims=True))
        sth[...] = dhc
        dg_ref[...] += jnp.sum((dln * xh)[0:ts], axis=0, keepdims=True)
        db_ref[...] += jnp.sum(dln[0:ts], axis=0, keepdims=True)
        dhc_t = dhc[0:ts]
        dcb_ref[...] += jnp.sum(dhc_t, axis=0, keepdims=True)
        dglu = jnp.zeros((ts, D_CONV), F32)
        for k in range(K):
            dcw_ref[k:k + 1, :] += jnp.sum(dhc_t * stg[pl.ds(hb - (K - 1) + k, ts), :], axis=0, keepdims=True)
            dglu = dglu + cw_ref[k:k + 1, :] * sth[pl.ds(K - 1 - k, ts), :]
        du_ref[:, D_POOL:D_POOL + D_CONV] = (dglu * sg).astype(BF16)
        du_ref[:, D_POOL + D_CONV:] = (dglu * bv * sg * (1.0 - sg)).astype(BF16)

    fix2 = lambda i: (0, 0)
    prev = lambda i: (jnp.maximum(i * nh - 1, 0), 0)
    nxt = lambda i: (jnp.minimum((i + 1) * nh, S // hb - 1), 0)
    return pl.pallas_call(
        body,
        out_shape=[jax.ShapeDtypeStruct((S, 3 * D_POOL), BF16),
                   jax.ShapeDtypeStruct((4, POOL_GROUP, POOL_GROUP), F32),
                   jax.ShapeDtypeStruct((1, D_POOL), F32),
                   jax.ShapeDtypeStruct((K, D_CONV), F32),
                   jax.ShapeDtypeStruct((1, D_CONV), F32),
                   jax.ShapeDtypeStruct((1, D_CONV), F32),
                   jax.ShapeDtypeStruct((1, D_CONV), F32)],
        grid=(n,),
        in_specs=[pl.BlockSpec((ts, 3 * D_POOL), lambda i: (i, 0)),
                  pl.BlockSpec((hb, 3 * D_POOL), prev),
                  pl.BlockSpec((hb, 3 * D_POOL), nxt),
                  pl.BlockSpec((ts, D_POOL), lambda i: (i, 0)),
                  pl.BlockSpec((ts, D_MODEL), lambda i: (i, 0)),
                  pl.BlockSpec((hb, D_MODEL), nxt),
                  pl.BlockSpec((4, POOL_GROUP, POOL_GROUP), lambda i: (0, 0, 0)),
                  pl.BlockSpec((1, D_POOL), fix2), pl.BlockSpec((K, D_CONV), fix2),
                  pl.BlockSpec((1, D_CONV), fix2), pl.BlockSpec((1, D_CONV), fix2), pl.BlockSpec((1, D_CONV), fix2)],
        out_specs=[pl.BlockSpec((ts, 3 * D_POOL), lambda i: (i, 0)),
                   pl.BlockSpec((4, POOL_GROUP, POOL_GROUP), lambda i: (0, 0, 0)),
                   pl.BlockSpec((1, D_POOL), fix2), pl.BlockSpec((K, D_CONV), fix2),
                   pl.BlockSpec((1, D_CONV), fix2), pl.BlockSpec((1, D_CONV), fix2), pl.BlockSpec((1, D_CONV), fix2)],
        scratch_shapes=[pltpu.VMEM((hb + ts + hb, D_CONV), F32), pltpu.VMEM((te, D_POOL), F32),
                        pltpu.VMEM((te, D_CONV), F32)],
        compiler_params=_cparams(("arbitrary",)),
        name="mixer_bwd",
    )(u, u, u, d, dycat, dycat, pool_w, pool_scale.reshape(1, D_POOL), conv_w, conv_b.reshape(1, D_CONV),
      cln_g.reshape(1, D_CONV), cln_b.reshape(1, D_CONV))


_GELU_C = math.sqrt(2.0 / math.pi)


def _gelu_parts(x):
    inner = _GELU_C * (x + 0.044715 * x * x * x)
    th = jnp.tanh(inner)
    ge = 0.5 * x * (1.0 + th)
    dge = 0.5 * (1.0 + th) + 0.5 * x * (1.0 - th * th) * (_GELU_C * (1.0 + 3.0 * 0.044715 * x * x))
    return ge, dge


def _ffn_act_fwd(gate, val, dw_w, dw_b, *, ts=256, tc=1408, name):
    S, F = gate.shape
    hb = FFN_HALO
    nh = ts // hb
    tc = _tile(F, tc)

    def body(g_ref, gh_ref, v_ref, w_ref, b_ref, h_ref, st):
        i = pl.program_id(0)
        st[pl.ds(0, hb), :] = jnp.where(i == 0, 0.0, gh_ref[...])
        st[pl.ds(hb, ts), :] = g_ref[...]
        gc = b_ref[...] + w_ref[0:1, :] * st[pl.ds(hb - 2, ts), :] + w_ref[1:2, :] * st[pl.ds(hb - 1, ts), :] \
            + w_ref[2:3, :] * st[pl.ds(hb, ts), :]
        ge, _ = _gelu_parts(gc)
        h_ref[...] = (ge * v_ref[...]).astype(BF16)

    return pl.pallas_call(
        body,
        out_shape=jax.ShapeDtypeStruct((S, F), BF16),
        grid=(S // ts, F // tc),
        in_specs=[pl.BlockSpec((ts, tc), lambda i, j: (i, j)),
                  pl.BlockSpec((hb, tc), lambda i, j: (jnp.maximum(i * nh - 1, 0), j)),
                  pl.BlockSpec((ts, tc), lambda i, j: (i, j)),
                  pl.BlockSpec((3, tc), lambda i, j: (0, j)),
                  pl.BlockSpec((1, tc), lambda i, j: (0, j))],
        out_specs=pl.BlockSpec((ts, tc), lambda i, j: (i, j)),
        scratch_shapes=[pltpu.VMEM((hb + ts, tc), F32)],
        compiler_params=_cparams(("parallel", "parallel")),
        name=name,
    )(gate, gate, val, dw_w, dw_b.reshape(1, F))


def _ffn_act_bwd(gate, val, dh, dw_w, dw_b, *, ts=256, tc=1408, name):
    S, F = gate.shape
    hb = FFN_HALO
    nh = ts // hb
    n = S // ts
    te = ts + hb
    tc = _tile(F, tc)

    def body(g_ref, gp_ref, gn_ref, v_ref, vn_ref, dh_ref, dhn_ref, w_ref, b_ref,
             dg_ref, dv_ref, dw_ref, db_ref, st, sd):
        i = pl.program_id(1)
        first = i == 0
        last = i == n - 1

        @pl.when(first)
        def _():
            dw_ref[...] = jnp.zeros_like(dw_ref)
            db_ref[...] = jnp.zeros_like(db_ref)

        st[pl.ds(0, hb), :] = jnp.where(first, 0.0, gp_ref[...])
        st[pl.ds(hb, ts), :] = g_ref[...]
        st[pl.ds(hb + ts, hb), :] = jnp.where(last, 0.0, gn_ref[...])
        gc = b_ref[...] + w_ref[0:1, :] * st[pl.ds(hb - 2, te), :] + w_ref[1:2, :] * st[pl.ds(hb - 1, te), :] \
            + w_ref[2:3, :] * st[pl.ds(hb, te), :]
        ge, dge = _gelu_parts(gc)
        val_e = jnp.concatenate([v_ref[...], jnp.where(last, 0.0, vn_ref[...])], axis=0)
        dh_e = jnp.concatenate([dh_ref[...], jnp.where(last, 0.0, dhn_ref[...])], axis=0)
        dgc = dh_e * val_e * dge
        sd[...] = dgc
        dv_ref[...] = (dh_e[0:ts] * ge[0:ts]).astype(BF16)
        dgc_t = dgc[0:ts]
        db_ref[...] += jnp.sum(dgc_t, axis=0, keepdims=True)
        dgate = jnp.zeros((ts, tc), F32)
        for k in range(3):
            dw_ref[k:k + 1, :] += jnp.sum(dgc_t * st[pl.ds(hb - 2 + k, ts), :], axis=0, keepdims=True)
            dgate = dgate + w_ref[k:k + 1, :] * sd[pl.ds(2 - k, ts), :]
        dg_ref[...] = dgate.astype(BF16)

    cur = lambda j, i: (i, j)
    prev = lambda j, i: (jnp.maximum(i * nh - 1, 0), j)
    nxt = lambda j, i: (jnp.minimum((i + 1) * nh, S // hb - 1), j)
    return pl.pallas_call(
        body,
        out_shape=[jax.ShapeDtypeStruct((S, F), BF16), jax.ShapeDtypeStruct((S, F), BF16),
                   jax.ShapeDtypeStruct((3, F), F32), jax.ShapeDtypeStruct((1, F), F32)],
        grid=(F // tc, n),
        in_specs=[pl.BlockSpec((ts, tc), cur), pl.BlockSpec((hb, tc), prev), pl.BlockSpec((hb, tc), nxt),
                  pl.BlockSpec((ts, tc), cur), pl.BlockSpec((hb, tc), nxt),
                  pl.BlockSpec((ts, tc), cur), pl.BlockSpec((hb, tc), nxt),
                  pl.BlockSpec((3, tc), lambda j, i: (0, j)), pl.BlockSpec((1, tc), lambda j, i: (0, j))],
        out_specs=[pl.BlockSpec((ts, tc), cur), pl.BlockSpec((ts, tc), cur),
                   pl.BlockSpec((3, tc), lambda j, i: (0, j)), pl.BlockSpec((1, tc), lambda j, i: (0, j))],
        scratch_shapes=[pltpu.VMEM((hb + ts + hb, tc), F32), pltpu.VMEM((te, tc), F32)],
        compiler_params=_cparams(("parallel", "arbitrary")),
        name=name,
    )(gate, gate, gate, val, val, dh, dh, dw_w, dw_b.reshape(1, F))


def _ln_bwd(z, ln_g, ln_b, dout, *, loss_head=False, ts=256, name):
    S, D = z.shape

    def body(z_ref, g_ref, b_ref, do_ref, dz_ref, dg_ref, db_ref, loss_ref):
        i = pl.program_id(0)

        @pl.when(i == 0)
        def _():
            dg_ref[...] = jnp.zeros_like(dg_ref)
            db_ref[...] = jnp.zeros_like(db_ref)
            loss_ref[...] = jnp.zeros_like(loss_ref)

        zt = z_ref[...]
        mu = jnp.mean(zt, axis=-1, keepdims=True)
        zc = zt - mu
        rstd = lax.rsqrt(jnp.mean(zc * zc, axis=-1, keepdims=True) + LN_EPS)
        xh = zc * rstd
        if loss_head:
            err = xh * g_ref[...] + b_ref[...] - do_ref[...]
            loss_ref[...] += 0.5 * jnp.sum(jnp.mean(err * err, axis=-1, keepdims=True))
            do = err * (1.0 / D)
        else:
            do = do_ref[...]
        dg_ref[...] += jnp.sum(do * xh, axis=0, keepdims=True)
        db_ref[...] += jnp.sum(do, axis=0, keepdims=True)
        dxh = do * g_ref[...]
        dz_ref[...] = rstd * (dxh - jnp.mean(dxh, axis=-1, keepdims=True)
                              - xh * jnp.mean(dxh * xh, axis=-1, keepdims=True))

    row = lambda i: (i, 0)
    fix = lambda i: (0, 0)
    return pl.pallas_call(
        body,
        out_shape=[jax.ShapeDtypeStruct((S, D), F32), jax.ShapeDtypeStruct((1, D), F32),
                   jax.ShapeDtypeStruct((1, D), F32), jax.ShapeDtypeStruct((8, 128), F32)],
        grid=(S // ts,),
        in_specs=[pl.BlockSpec((ts, D), row), pl.BlockSpec((1, D), fix), pl.BlockSpec((1, D), fix),
                  pl.BlockSpec((ts, D), row)],
        out_specs=[pl.BlockSpec((ts, D), row), pl.BlockSpec((1, D), fix), pl.BlockSpec((1, D), fix),
                   pl.BlockSpec((8, 128), fix)],
        compiler_params=_cparams(("arbitrary",)),
        name=name,
    )(z, ln_g.reshape(1, D), ln_b.reshape(1, D), dout)


def _ple_bwd(dz, gate, proj, *, ts=256, name):
    S, D = dz.shape

    def body(dz_ref, g_ref, p_ref, ds_ref, dp_ref, db_ref):
        @pl.when(pl.program_id(0) == 0)
        def _():
            db_ref[...] = jnp.zeros_like(db_ref)

        dzt = dz_ref[...]
        g = g_ref[...]
        ds = dzt * p_ref[...] * g * (1.0 - g)
        ds_ref[...] = ds.astype(BF16)
        dp_ref[...] = (dzt * g).astype(BF16)
        db_ref[...] += jnp.sum(ds, axis=0, keepdims=True)

    row = lambda i: (i, 0)
    return pl.pallas_call(
        body,
        out_shape=[jax.ShapeDtypeStruct((S, D), BF16), jax.ShapeDtypeStruct((S, D), BF16),
                   jax.ShapeDtypeStruct((1, D), F32)],
        grid=(S // ts,),
        in_specs=[pl.BlockSpec((ts, D), row)] * 3,
        out_specs=[pl.BlockSpec((ts, D), row), pl.BlockSpec((ts, D), row), pl.BlockSpec((1, D), lambda i: (0, 0))],
        compiler_params=_cparams(("arbitrary",)),
        name=name,
    )(dz, gate, proj)


def _attn_scores(q, kc, bias, qb):
    s = _bdot(q, kc, NT) * (HEAD_DIM ** -0.5) + bias
    kpos = qb * Q_BLOCK + lax.broadcasted_iota(jnp.int32, (1, KV_SPAN), 1)
    s = jnp.where(kpos >= KV_PAD, s, NEG_INF)
    m = jnp.max(s, axis=-1, keepdims=True)
    e = jnp.exp(s - m)
    return e / jnp.sum(e, axis=-1, keepdims=True)


def _attn_fwd(q, kp, vp, bias):
    H, S, dh = q.shape
    sp = kp.shape[1]

    def body(q_ref, k_ref, v_ref, b_ref, o_ref):
        qb = pl.program_id(1)
        start = pl.multiple_of(qb * Q_BLOCK, Q_BLOCK)
        kc = k_ref[pl.ds(start, KV_SPAN), :]
        vc = v_ref[pl.ds(start, KV_SPAN), :]
        p = _attn_scores(q_ref[...], kc, b_ref[...], qb)
        o_ref[...] = _bdot(p, vc).astype(BF16)

    return pl.pallas_call(
        body,
        out_shape=jax.ShapeDtypeStruct((H, S, dh), BF16),
        grid=(H, S // Q_BLOCK),
        in_specs=[pl.BlockSpec((None, Q_BLOCK, dh), lambda h, i: (h, i, 0)),
                  pl.BlockSpec((None, sp, dh), lambda h, i: (h, 0, 0)),
                  pl.BlockSpec((None, sp, dh), lambda h, i: (h, 0, 0)),
                  pl.BlockSpec((None, Q_BLOCK, KV_SPAN), lambda h, i: (h, 0, 0))],
        out_specs=pl.BlockSpec((None, Q_BLOCK, dh), lambda h, i: (h, i, 0)),
        compiler_params=_cparams(("parallel", "arbitrary")),
        name="attn_fwd",
    )(q, kp, vp, bias)


def _attn_bwd(q, kp, vp, bias, do):
    H, S, dh = q.shape
    sp = kp.shape[1]
    scale = HEAD_DIM ** -0.5

    def body(q_ref, k_ref, v_ref, b_ref, do_ref, dq_ref, dk_ref, dv_ref, db_ref):
        qb = pl.program_id(1)

        @pl.when(qb == 0)
        def _():
            dk_ref[...] = jnp.zeros_like(dk_ref)
            dv_ref[...] = jnp.zeros_like(dv_ref)
            db_ref[...] = jnp.zeros_like(db_ref)

        start = pl.multiple_of(qb * Q_BLOCK, Q_BLOCK)
        span = pl.ds(start, KV_SPAN)
        qt = q_ref[...]
        kc = k_ref[span, :]
        vc = v_ref[span, :]
        dot = do_ref[...]
        p = _attn_scores(qt, kc, b_ref[...], qb)
        dv_ref[span, :] += _bdot(p, dot, TN)
        dp = _bdot(dot, vc, NT)
        ds = p * (dp - jnp.sum(p * dp, axis=-1, keepdims=True))
        db_ref[...] += ds
        dq_ref[...] = (scale * _bdot(ds, kc)).astype(BF16)
        dk_ref[span, :] += scale * _bdot(ds, qt, TN)

    blk = pl.BlockSpec((None, Q_BLOCK, dh), lambda h, i: (h, i, 0))
    full = pl.BlockSpec((None, sp, dh), lambda h, i: (h, 0, 0))
    bsp = pl.BlockSpec((None, Q_BLOCK, KV_SPAN), lambda h, i: (h, 0, 0))
    return pl.pallas_call(
        body,
        out_shape=[jax.ShapeDtypeStruct((H, S, dh), BF16), jax.ShapeDtypeStruct((H, sp, dh), F32),
                   jax.ShapeDtypeStruct((H, sp, dh), F32), jax.ShapeDtypeStruct((H, Q_BLOCK, KV_SPAN), F32)],
        grid=(H, S // Q_BLOCK),
        in_specs=[blk, full, full, bsp, blk],
        out_specs=[blk, full, full, bsp],
        compiler_params=_cparams(("parallel", "arbitrary")),
        name="attn_bwd",
    )(q, kp, vp, bias, do)


def _bias_blocks(rel_bias):
    H = rel_bias.shape[0]
    n_e = BAND + CHUNK - 1
    n_clip = KV_PAD + CHUNK - 1 - MAX_REL + 1
    e = jnp.concatenate([jnp.broadcast_to(rel_bias[:, 2 * MAX_REL:], (H, n_clip)),
                         jnp.flip(rel_bias[:, 2 * MAX_REL - (n_e - n_clip):2 * MAX_REL], axis=1)], axis=1)
    skew = jnp.pad(jnp.tile(e, (1, CHUNK)), ((0, 0), (0, CHUNK))).reshape(H, CHUNK, n_e + 1)
    band = jnp.flip(skew, axis=1)[:, :, :BAND]
    rows = [jnp.pad(band, ((0, 0), (0, 0), (c * CHUNK, KV_SPAN - BAND - c * CHUNK)), constant_values=NEG_INF)
            for c in range(Q_BLOCK // CHUNK)]
    return jnp.concatenate(rows, axis=1)


def _bias_blocks_grad(dblk):
    H = dblk.shape[0]
    n_e = BAND + CHUNK - 1
    n_clip = KV_PAD + CHUNK - 1 - MAX_REL + 1
    parts = jnp.stack([dblk[:, c * CHUNK:(c + 1) * CHUNK, c * CHUNK:c * CHUNK + BAND]
                       for c in range(Q_BLOCK // CHUNK)], axis=1)
    parts = jnp.flip(parts, axis=2)
    parts = jnp.pad(parts, ((0, 0), (0, 0), (0, 0), (0, n_e + 1 - BAND)))
    skew = parts.reshape(H, Q_BLOCK // CHUNK, CHUNK * (n_e + 1))[:, :, :CHUNK * n_e]
    skew = skew.reshape(H, Q_BLOCK, n_e)
    skew = jnp.pad(skew, ((0, 0), (0, 0), (0, 1)))

    def body(s_ref, o_ref):
        de = jnp.sum(s_ref[...], axis=0, keepdims=True)
        lane = lax.broadcasted_iota(jnp.int32, de.shape, 1)
        far = jnp.sum(jnp.where(lane < n_clip, de, 0.0), axis=-1, keepdims=True)
        o_ref[...] = jnp.where(lane == 0, far, jnp.where(lane < n_clip, 0.0, de))

    de = pl.pallas_call(
        body,
        out_shape=jax.ShapeDtypeStruct((H, 1, n_e + 1), F32),
        grid=(H,),
        in_specs=[pl.BlockSpec((None, Q_BLOCK, n_e + 1), lambda h: (h, 0, 0))],
        out_specs=pl.BlockSpec((None, 1, n_e + 1), lambda h: (h, 0, 0)),
        compiler_params=_cparams(("parallel",)),
        name="bias_grad_sum",
    )(skew).reshape(H, n_e + 1)
    near = jnp.flip(de[:, n_clip:n_e], axis=1)
    return jnp.concatenate([jnp.zeros((H, 2 * MAX_REL - (n_e - n_clip)), F32), near, de[:, 0:1]], axis=1)


def _ffn_forward(r1, p_l, w, l):
    up_g = _mm(r1, w["ffn_up_g"][l], tn=1408, name=f"ffn_up_g{l}")
    up_v = _mm(r1, w["ffn_up_v"][l], tn=1408, name=f"ffn_up_v{l}")
    h = _ffn_act_fwd(up_g, up_v, w["ffn_dw_w"][l], w["ffn_dw_b"][l], name=f"ffn_act{l}")
    z2, r2, gate, proj = _proj_ln(r1, h, w["ffn_w_down"][l], w["ln_ffn_g"][l], w["ln_ffn_b"][l],
                                  ple=(w["ple_w_gate"][l], w["ple_b_gate"][l], p_l, w["ple_w_proj"][l]),
                                  name=f"ffn_down_ln{l}")
    return dict(r1=r1, up_g=up_g, up_v=up_v, h=h, z2=z2, gate=gate, proj=proj), r2


def _ffn_backward(sv, dz2, p_l, w, l, grads):
    r1 = sv["r1"]
    ds, dproj, db_gate = _ple_bwd(dz2, sv["gate"], sv["proj"], name=f"ple_bwd{l}")
    dh = _mm(dz2, w["ffn_w_down"][l], tb=True, tn=1408, name=f"ffn_dh{l}")
    dgate, dval, d_dw_w, d_dw_b = _ffn_act_bwd(sv["up_g"], sv["up_v"], dh, w["ffn_dw_w"][l], w["ffn_dw_b"][l],
                                               name=f"ffn_act_bwd{l}")
    grads["ffn_w_down"][l] = _mm(sv["h"], dz2, ta=True, tm=1408, tn=1024, tk=512, name=f"d_ffn_w_down{l}")
    grads["ffn_up_g"][l] = _mm(r1, dgate, ta=True, tm=1024, tn=1408, tk=512, name=f"d_ffn_up_g{l}")
    grads["ffn_up_v"][l] = _mm(r1, dval, ta=True, tm=1024, tn=1408, tk=512, name=f"d_ffn_up_v{l}")
    grads["ple_w_gate"][l] = _mm(r1, ds, ta=True, tm=1024, tn=1024, tk=512, name=f"d_ple_w_gate{l}")
    grads["ple_w_proj"][l] = _mm(p_l, dproj, ta=True, tm=256, tn=1024, tk=512, name=f"d_ple_w_proj{l}")
    grads["ffn_dw_w"][l] = d_dw_w
    grads["ffn_dw_b"][l] = d_dw_b[0]
    grads["ple_b_gate"][l] = db_gate[0]
    t = _mm(ds, w["ple_w_gate"][l], tb=True, add=dz2, add_scale=ALPHA, tn=1024, name=f"dr1_gate{l}")
    t = _mm(dgate, w["ffn_up_g"][l], tb=True, add=t, tn=1024, tk=1408, name=f"dr1_up_g{l}")
    return _mm(dval, w["ffn_up_v"][l], tb=True, add=t, tn=1024, tk=1408, name=f"dr1_up_v{l}")


def _to_heads(a, n):
    S = a.shape[0]
    t = a.reshape(S, n, N_HEADS, HEAD_DIM).transpose(1, 2, 0, 3)
    return [t[i] for i in range(n)]


def _local_step(x, p, target, w):
    S = x.shape[0]
    grads = {k: [None, None] for k in ("ffn_w_down", "ffn_up_g", "ffn_up_v", "ple_w_gate", "ple_w_proj", "ffn_dw_w",
                                       "ffn_dw_b", "ple_b_gate", "ln_ffn_g", "ln_ffn_b", "ln_mix_g", "ln_mix_b")}

    u = _mm(x, w["mix_w_in"], name="mix_in")
    ycat, dpool = _mixer_fwd(u, w["pool_w"], w["pool_scale"], w["conv_dw_w"], w["conv_dw_b"], w["conv_ln_g"],
                             w["conv_ln_b"])
    z1, r1 = _proj_ln(x, ycat, w["mix_w_out"], w["ln_mix_g"][0], w["ln_mix_b"][0], name="mix_out_ln")
    sv0, r2 = _ffn_forward(r1, p[0], w, 0)

    qkv = _mm(r2, w["attn_w_qkv"], out_dtype=BF16, name="attn_qkv")
    q, k, v = _to_heads(qkv, 3)
    kp = jnp.pad(k, ((0, 0), (KV_PAD, 0), (0, 0)))
    vp = jnp.pad(v, ((0, 0), (KV_PAD, 0), (0, 0)))
    bias = _bias_blocks(w["attn_rel_bias"])
    attn_h = _attn_fwd(q, kp, vp, bias)
    attn = attn_h.transpose(1, 0, 2).reshape(S, D_MODEL)
    z3, r3 = _proj_ln(r2, attn, w["attn_w_o"], w["ln_mix_g"][1], w["ln_mix_b"][1], name="attn_out_ln")
    sv1, _ = _ffn_forward(r3, p[1], w, 1)

    dz4, grads["ln_ffn_g"][1], grads["ln_ffn_b"][1], loss = _ln_bwd(sv1["z2"], w["ln_ffn_g"][1], w["ln_ffn_b"][1],
                                                                    target, loss_head=True, name="loss_ln_bwd")
    dr3 = _ffn_backward(sv1, dz4, p[1], w, 1, grads)
    dz3, grads["ln_mix_g"][1], grads["ln_mix_b"][1], _ = _ln_bwd(z3, w["ln_mix_g"][1], w["ln_mix_b"][1], dr3,
                                                                name="ln_mix_bwd1")
    grads["attn_w_o"] = _mm(attn, dz3, ta=True, tm=1024, tn=1024, tk=512, name="d_attn_w_o")
    dattn = _mm(dz3, w["attn_w_o"], tb=True, out_dtype=BF16, tn=1024, name="d_attn")
    (dattn_h,) = _to_heads(dattn, 1)
    dq, dk, dv, dbias = _attn_bwd(q, kp, vp, bias, dattn_h)
    grads["attn_rel_bias"] = _bias_blocks_grad(dbias)
    dqkv = jnp.stack([dq, dk[:, KV_PAD:].astype(BF16), dv[:, KV_PAD:].astype(BF16)], axis=0)
    dqkv = dqkv.transpose(2, 0, 1, 3).reshape(S, 3 * D_MODEL)
    grads["attn_w_qkv"] = _mm(r2, dqkv, ta=True, tm=1024, tn=1024, tk=512, name="d_attn_w_qkv")
    dr2 = _mm(dqkv, w["attn_w_qkv"], tb=True, add=dz3, add_scale=ALPHA, tn=1024, name="dr2")

    dz2, grads["ln_ffn_g"][0], grads["ln_ffn_b"][0], _ = _ln_bwd(sv0["z2"], w["ln_ffn_g"][0], w["ln_ffn_b"][0], dr2,
                                                                name="ln_ffn_bwd0")
    dr1 = _ffn_backward(sv0, dz2, p[0], w, 0, grads)
    dz1, grads["ln_mix_g"][0], grads["ln_mix_b"][0], _ = _ln_bwd(z1, w["ln_mix_g"][0], w["ln_mix_b"][0], dr1,
                                                                name="ln_mix_bwd0")
    grads["mix_w_out"] = _mm(ycat, dz1, ta=True, tm=1024, tn=1024, tk=512, name="d_mix_w_out")
    dycat = _mm(dz1, w["mix_w_out"], tb=True, tn=1024, name="d_ycat")
    du, g_pw, g_ps, g_cw, g_cb, g_cg, g_cbb = _mixer_bwd(u, dpool, dycat, w["pool_w"], w["pool_scale"],
                                                         w["conv_dw_w"], w["conv_dw_b"], w["conv_ln_g"],
                                                         w["conv_ln_b"])
    grads["mix_w_in"] = _mm(x, du, ta=True, tm=1024, tn=512, tk=512, name="d_mix_w_in")
    grad_x = _mm(du, w["mix_w_in"], tb=True, add=dz1, add_scale=ALPHA, tn=1024, tk=512, name="grad_x")
    grads.update(pool_w=g_pw, pool_scale=g_ps[0], conv_dw_w=g_cw, conv_dw_b=g_cb[0], conv_ln_g=g_cg[0],
                 conv_ln_b=g_cbb[0])
    for kname in ("ln_ffn_g", "ln_ffn_b", "ln_mix_g", "ln_mix_b"):
        grads[kname] = [a[0] for a in grads[kname]]
    return loss[0, 0], grad_x, grads


def _exchange(bufs, *, gather, name):
    nb = len(bufs)

    def body(*refs):
        srcs, dsts = refs[:nb], refs[nb:2 * nb]
        send_sems, recv_sems, local_sems = refs[2 * nb:]
        x, y, c = lax.axis_index("x"), lax.axis_index("y"), lax.axis_index("c")
        me = 4 * x + 2 * y + c
        local = []
        remote = []
        for b in range(nb):
            src_me = srcs[b] if gather else srcs[b].at[me]
            cp = pltpu.make_async_copy(src_me, dsts[b].at[me], local_sems.at[b])
            cp.start()
            local.append(cp)
            for d in range(1, N_DEV):
                px, py, pc = x ^ ((d >> 2) & 1), y ^ ((d >> 1) & 1), c ^ (d & 1)
                peer = 4 * px + 2 * py + pc
                out = pltpu.make_async_remote_copy(
                    src_ref=srcs[b] if gather else srcs[b].at[peer], dst_ref=dsts[b].at[me],
                    send_sem=send_sems.at[b, d], recv_sem=recv_sems.at[b, d],
                    device_id=(px, py, pc), device_id_type=pl.DeviceIdType.MESH)
                out.start()
                inc = pltpu.make_async_remote_copy(
                    src_ref=srcs[b] if gather else srcs[b].at[peer], dst_ref=dsts[b].at[peer],
                    send_sem=send_sems.at[b, d], recv_sem=recv_sems.at[b, d],
                    device_id=(px, py, pc), device_id_type=pl.DeviceIdType.MESH)
                remote.append((out, inc))
        for cp in local:
            cp.wait()
        for out, inc in remote:
            out.wait_send()
            inc.wait_recv()

    out_shapes = []
    for b in bufs:
        r = b.shape[-2]
        out_shapes.append(jax.ShapeDtypeStruct((N_DEV, r, LANES), b.dtype))
    any_spec = pl.BlockSpec(memory_space=pl.ANY)
    return pl.pallas_call(
        body,
        out_shape=out_shapes,
        in_specs=[any_spec] * nb,
        out_specs=[any_spec] * nb,
        scratch_shapes=[pltpu.SemaphoreType.DMA((nb, N_DEV)), pltpu.SemaphoreType.DMA((nb, N_DEV)),
                        pltpu.SemaphoreType.DMA((nb,))],
        compiler_params=pltpu.CompilerParams(has_side_effects=True),
        name=name,
    )(*bufs)


def _adamw(recv, w, m, v, *, tr, name):
    R = w.shape[0]
    c1 = 1.0 - ADAM_B1 ** ADAM_STEP
    c2 = 1.0 - ADAM_B2 ** ADAM_STEP

    def body(r_ref, w_ref, m_ref, v_ref, g_ref, d_ref, mo_ref, vo_ref):
        g = r_ref[0].astype(F32)
        for i in range(1, N_DEV):
            g = g + r_ref[i].astype(F32)
        m_new = ADAM_B1 * m_ref[...] + (1.0 - ADAM_B1) * g
        v_new = ADAM_B2 * v_ref[...] + (1.0 - ADAM_B2) * (g * g)
        m_hat = m_new / c1
        v_hat = v_new / c2
        g_ref[...] = g
        d_ref[...] = -ADAM_LR * (m_hat / (jnp.sqrt(v_hat) + ADAM_EPS) + ADAM_WD * w_ref[...])
        mo_ref[...] = m_new
        vo_ref[...] = v_new

    row = pl.BlockSpec((tr, LANES), lambda i: (i, 0))
    return pl.pallas_call(
        body,
        out_shape=[jax.ShapeDtypeStruct((R, LANES), F32)] * 4,
        grid=(R // tr,),
        in_specs=[pl.BlockSpec((N_DEV, tr, LANES), lambda i: (0, i, 0)), row, row, row],
        out_specs=[row] * 4,
        compiler_params=_cparams(("parallel",)),
        name=name,
    )(recv, w, m, v)


_MM_WEIGHTS = (("mix_w_in", 2), ("mix_w_out", 1), ("attn_w_qkv", 2), ("attn_w_o", 1), ("ffn_w_up", 2),
               ("ffn_w_down", 1), ("ple_w_proj", 2), ("ple_w_gate", 1))
_EW_WEIGHTS = (("conv_dw_w", 2), ("ffn_dw_w", 2))
_REPLICATED = ("pool_w", "pool_scale", "conv_dw_b", "conv_ln_g", "conv_ln_b", "attn_rel_bias", "ln_mix_g",
               "ln_mix_b", "ffn_dw_b", "ple_b_gate", "ln_ffn_g", "ln_ffn_b")


def _pack_rows(parts, row_mult, dtype):
    lead = parts[0].shape[:-1]
    flat = jnp.concatenate([a.astype(dtype) for a in parts], axis=-1)
    n = flat.shape[-1]
    unit = row_mult * LANES
    padded = -(-n // unit) * unit
    flat = jnp.pad(flat, [(0, 0)] * len(lead) + [(0, padded - n)])
    return flat.reshape(lead + (padded // LANES, LANES))


def _unpack(flat2d, shapes):
    flat = flat2d.reshape(-1)
    out, o = [], 0
    for s in shapes:
        n = math.prod(s)
        out.append(flat[o:o + n].reshape(s))
        o += n
    return out


def _full_from_shards(g, axis):
    parts = jnp.moveaxis(g, 0, axis)
    shp = list(g.shape[1:])
    shp[axis] *= N_DEV
    return parts.reshape(shp)


def _pieces_from_full(full, axis):
    shp = list(full.shape)
    n = shp[axis] // N_DEV
    t = full.reshape(shp[:axis] + [N_DEV, n] + shp[axis + 1:])
    return jnp.moveaxis(t, axis, 0).reshape(N_DEV, -1)


def kernel(x, p, mix_w_in, pool_w, pool_scale, conv_dw_w, conv_dw_b, conv_ln_g, conv_ln_b, mix_w_out, attn_w_qkv, attn_rel_bias, attn_w_o, ln_mix_g, ln_mix_b, ffn_w_up, ffn_dw_w, ffn_dw_b, ffn_w_down, ple_w_proj, ple_w_gate, ple_b_gate, ln_ffn_g, ln_ffn_b, loss_target, m_mix_w_in, m_pool_w, m_pool_scale, m_conv_dw_w, m_conv_dw_b, m_conv_ln_g, m_conv_ln_b, m_mix_w_out, m_attn_w_qkv, m_attn_rel_bias, m_attn_w_o, m_ln_mix_g, m_ln_mix_b, m_ffn_w_up, m_ffn_dw_w, m_ffn_dw_b, m_ffn_w_down, m_ple_w_proj, m_ple_w_gate, m_ple_b_gate, m_ln_ffn_g, m_ln_ffn_b, v_mix_w_in, v_pool_w, v_pool_scale, v_conv_dw_w, v_conv_dw_b, v_conv_ln_g, v_conv_ln_b, v_mix_w_out, v_attn_w_qkv, v_attn_rel_bias, v_attn_w_o, v_ln_mix_g, v_ln_mix_b, v_ffn_w_up, v_ffn_dw_w, v_ffn_dw_b, v_ffn_w_down, v_ple_w_proj, v_ple_w_gate, v_ple_b_gate, v_ln_ffn_g, v_ln_ffn_b):
    a = dict(locals())
    names = [n for n, _ in _MM_WEIGHTS + _EW_WEIGHTS] + list(_REPLICATED)
    wts = {n: a[n] for n in names}
    mom = {n: a["m_" + n] for n in names}
    var = {n: a["v_" + n] for n in names}
    sharded = _MM_WEIGHTS + _EW_WEIGHTS

    mm_send = _pack_rows([wts[n].reshape(-1) for n, _ in _MM_WEIGHTS], 16, BF16)
    ew_send = _pack_rows([wts[n].reshape(-1) for n, _ in _EW_WEIGHTS], 8, F32)
    mm_all, ew_all = _exchange([mm_send, ew_send], gather=True, name="weight_all_gather")
    full = {}
    for (nm, ax), g in zip(_MM_WEIGHTS, zip(*[_unpack(mm_all[i], [wts[n].shape for n, _ in _MM_WEIGHTS])
                                              for i in range(N_DEV)])):
        full[nm] = _full_from_shards(jnp.stack(g), ax)
    for (nm, ax), g in zip(_EW_WEIGHTS, zip(*[_unpack(ew_all[i], [wts[n].shape for n, _ in _EW_WEIGHTS])
                                              for i in range(N_DEV)])):
        full[nm] = _full_from_shards(jnp.stack(g), ax)

    w = dict(
        mix_w_in=full["mix_w_in"][0], mix_w_out=full["mix_w_out"][0], attn_w_qkv=full["attn_w_qkv"][0],
        attn_w_o=full["attn_w_o"][0], conv_dw_w=full["conv_dw_w"][0],
        ffn_up_g=[full["ffn_w_up"][l][:, :D_FF] for l in range(2)],
        ffn_up_v=[full["ffn_w_up"][l][:, D_FF:] for l in range(2)],
        ffn_w_down=full["ffn_w_down"], ple_w_proj=full["ple_w_proj"], ple_w_gate=full["ple_w_gate"],
        ffn_dw_w=full["ffn_dw_w"],
        pool_w=pool_w[0], pool_scale=pool_scale[0], conv_dw_b=conv_dw_b[0], conv_ln_g=conv_ln_g[0],
        conv_ln_b=conv_ln_b[0], attn_rel_bias=attn_rel_bias[0], ln_mix_g=ln_mix_g, ln_mix_b=ln_mix_b,
        ffn_dw_b=ffn_dw_b, ple_b_gate=ple_b_gate, ln_ffn_g=ln_ffn_g, ln_ffn_b=ln_ffn_b)

    loss_part, grad_x, gr = _local_step(x[0], p[:, 0], loss_target[0], w)
    loss = lax.psum(loss_part, ("x", "y", "c"))

    gfull = dict(
        mix_w_in=gr["mix_w_in"][None], mix_w_out=gr["mix_w_out"][None], attn_w_qkv=gr["attn_w_qkv"][None],
        attn_w_o=gr["attn_w_o"][None], conv_dw_w=gr["conv_dw_w"][None],
        ffn_w_up=jnp.stack([jnp.concatenate([gr["ffn_up_g"][l], gr["ffn_up_v"][l]], axis=1) for l in range(2)]),
        ffn_w_down=jnp.stack(gr["ffn_w_down"]), ple_w_proj=jnp.stack(gr["ple_w_proj"]),
        ple_w_gate=jnp.stack(gr["ple_w_gate"]), ffn_dw_w=jnp.stack(gr["ffn_dw_w"]),
        pool_w=gr["pool_w"][None], pool_scale=gr["pool_scale"][None], conv_dw_b=gr["conv_dw_b"][None],
        conv_ln_g=gr["conv_ln_g"][None], conv_ln_b=gr["conv_ln_b"][None], attn_rel_bias=gr["attn_rel_bias"][None],
        ln_mix_g=jnp.stack(gr["ln_mix_g"]), ln_mix_b=jnp.stack(gr["ln_mix_b"]), ffn_dw_b=jnp.stack(gr["ffn_dw_b"]),
        ple_b_gate=jnp.stack(gr["ple_b_gate"]), ln_ffn_g=jnp.stack(gr["ln_ffn_g"]),
        ln_ffn_b=jnp.stack(gr["ln_ffn_b"]))

    g_send = _pack_rows([_pieces_from_full(gfull[n], ax) for n, ax in sharded], ADAM_ROWS, BF16)
    rep_send = _pack_rows([gfull[n].reshape(-1) for n in _REPLICATED], 8, F32)
    (g_recv,) = _exchange([g_send], gather=False, name="grad_exchange")
    (rep_recv,) = _exchange([rep_send], gather=True, name="grad_all_gather")

    def flat_state(d, group, row_mult):
        return _pack_rows([d[n].reshape(-1) for n in group], row_mult, F32)

    sh_names = [n for n, _ in sharded]
    sh_out = _adamw(g_recv, flat_state(wts, sh_names, ADAM_ROWS), flat_state(mom, sh_names, ADAM_ROWS),
                    flat_state(var, sh_names, ADAM_ROWS), tr=ADAM_ROWS, name="adamw_sharded")
    rep_out = _adamw(rep_recv, flat_state(wts, _REPLICATED, 8), flat_state(mom, _REPLICATED, 8),
                     flat_state(var, _REPLICATED, 8), tr=rep_recv.shape[1], name="adamw_replicated")

    res = [{}, {}, {}, {}]
    for k in range(4):
        for n, arr in zip(sh_names, _unpack(sh_out[k], [wts[n].shape for n in sh_names])):
            res[k][n] = arr
        for n, arr in zip(_REPLICATED, _unpack(rep_out[k], [wts[n].shape for n in _REPLICATED])):
            res[k][n] = arr
    order = ["mix_w_in", "pool_w", "pool_scale", "conv_dw_w", "conv_dw_b", "conv_ln_g", "conv_ln_b", "mix_w_out",
             "attn_w_qkv", "attn_rel_bias", "attn_w_o", "ln_mix_g", "ln_mix_b", "ffn_w_up", "ffn_dw_w", "ffn_dw_b",
             "ffn_w_down", "ple_w_proj", "ple_w_gate", "ple_b_gate", "ln_ffn_g", "ln_ffn_b"]
    outs = [loss, grad_x[None]]
    for k in range(4):
        outs += [res[k][n] for n in order]
    return tuple(outs)
```

```python
import functools
import math

import jax
import jax.numpy as jnp
from jax import lax
from jax.experimental import pallas as pl
from jax.experimental.pallas import tpu as pltpu

F32 = jnp.float32
BF16 = jnp.bfloat16

N_DEV = 8
D_MODEL = 1024
D_POOL = 512
D_CONV = 512
POOL_WINDOWS = (2, 4, 8, 16)
POOL_GROUP = 128
CONV_KERNEL = 31
CHUNK = 64
HEAD_DIM = 64
N_HEADS = 16
LEFT_CHUNKS = 8
BAND = (LEFT_CHUNKS + 1) * CHUNK
MAX_REL = 256
D_FF = 2816
PLE_DIM = 256
ALPHA = 4.0 ** 0.25
LN_EPS = 1e-5
NEG_INF = -1e30
ADAM_LR, ADAM_B1, ADAM_B2, ADAM_EPS, ADAM_WD, ADAM_STEP = 0.001, 0.9, 0.999, 1e-08, 0.01, 10

Q_BLOCK = 4 * CHUNK
KV_PAD = LEFT_CHUNKS * CHUNK
KV_SPAN = KV_PAD + Q_BLOCK
CONV_HALO = 32
FFN_HALO = 8
LANES = 1024
VMEM_LIMIT = 56 * 1024 * 1024


def _cparams(sem=None):
    return pltpu.CompilerParams(dimension_semantics=sem, vmem_limit_bytes=VMEM_LIMIT)


def _tile(dim, pref):
    if dim <= pref:
        return dim
    t = pref - pref % 128
    while t >= 128:
        if dim % t == 0:
            return t
        t -= 128
    return dim


def _sigmoid(x):
    return 1.0 / (1.0 + jnp.exp(-x))


def _bdot(a, b, dn=(((1,), (0,)), ((), ()))):
    return lax.dot_general(a.astype(BF16), b.astype(BF16), dn, preferred_element_type=F32)


NT = (((1,), (1,)), ((), ()))
TN = (((0,), (0,)), ((), ()))


def _mm(a, b, *, ta=False, tb=False, add=None, add_scale=1.0, out_dtype=F32, tm=512, tn=512, tk=1024, name):
    if ta:
        K, M = a.shape
    else:
        M, K = a.shape
    if tb:
        N, kb = b.shape
    else:
        kb, N = b.shape
    assert K == kb, (a.shape, b.shape)
    tm, tn, tk = _tile(M, tm), _tile(N, tn), _tile(K, tk)
    nk = K // tk
    a_spec = pl.BlockSpec((tk, tm), lambda i, j, k: (k, i)) if ta else pl.BlockSpec((tm, tk), lambda i, j, k: (i, k))
    b_spec = pl.BlockSpec((tn, tk), lambda i, j, k: (j, k)) if tb else pl.BlockSpec((tk, tn), lambda i, j, k: (k, j))
    dn = (((0 if ta else 1,), (1 if tb else 0,)), ((), ()))
    has_add = add is not None

    def body(*refs):
        if has_add:
            a_ref, b_ref, add_ref, o_ref, acc = refs
        else:
            a_ref, b_ref, o_ref, acc = refs
        k = pl.program_id(2)

        @pl.when(k == 0)
        def _():
            acc[...] = jnp.zeros_like(acc)

        acc[...] += _bdot(a_ref[...], b_ref[...], dn)

        @pl.when(k == nk - 1)
        def _():
            r = acc[...]
            if has_add:
                r = r + add_scale * add_ref[...]
            o_ref[...] = r.astype(out_dtype)

    in_specs = [a_spec, b_spec]
    args = [a, b]
    if has_add:
        in_specs.append(pl.BlockSpec((tm, tn), lambda i, j, k: (i, j)))
        args.append(add)
    return pl.pallas_call(
        body,
        out_shape=jax.ShapeDtypeStruct((M, N), out_dtype),
        grid=(M // tm, N // tn, nk),
        in_specs=in_specs,
        out_specs=pl.BlockSpec((tm, tn), lambda i, j, k: (i, j)),
        scratch_shapes=[pltpu.VMEM((tm, tn), F32)],
        compiler_params=_cparams(("parallel", "parallel", "arbitrary")),
        name=name,
    )(*args)


def _layer_norm_rows(z, g, b):
    mu = jnp.mean(z, axis=-1, keepdims=True)
    zc = z - mu
    var = jnp.mean(zc * zc, axis=-1, keepdims=True)
    return zc * lax.rsqrt(var + LN_EPS) * g + b


def _proj_ln(res, a, w, ln_g, ln_b, *, ple=None, ts=256, name):
    S, D = res.shape
    ka = a.shape[1]
    has_ple = ple is not None
    row = lambda i: (i, 0)
    fix = lambda i: (0, 0)

    def body(*refs):
        if has_ple:
            res_ref, a_ref, w_ref, g_ref, b_ref, wg_ref, bg_ref, p_ref, wp_ref, z_ref, r_ref, gate_ref, proj_ref = refs
        else:
            res_ref, a_ref, w_ref, g_ref, b_ref, z_ref, r_ref = refs
        res_t = res_ref[...]
        acc = _bdot(a_ref[...], w_ref[...])
        if has_ple:
            gate = _sigmoid(_bdot(res_t, wg_ref[...]) + bg_ref[...])
            proj = _bdot(p_ref[...], wp_ref[...])
            gate_ref[...] = gate
            proj_ref[...] = proj
            acc = acc + gate * proj
        z = ALPHA * res_t + acc
        z_ref[...] = z
        r_ref[...] = _layer_norm_rows(z, g_ref[...], b_ref[...])

    in_specs = [pl.BlockSpec((ts, D), row), pl.BlockSpec((ts, ka), row), pl.BlockSpec((ka, D), fix),
                pl.BlockSpec((1, D), fix), pl.BlockSpec((1, D), fix)]
    args = [res, a, w, ln_g.reshape(1, D), ln_b.reshape(1, D)]
    n_out = 2
    if has_ple:
        wg, bg, p, wp = ple
        in_specs += [pl.BlockSpec((D, D), fix), pl.BlockSpec((1, D), fix), pl.BlockSpec((ts, PLE_DIM), row),
                     pl.BlockSpec((PLE_DIM, D), fix)]
        args += [wg, bg.reshape(1, D), p, wp]
        n_out = 4
    return pl.pallas_call(
        body,
        out_shape=[jax.ShapeDtypeStruct((S, D), F32)] * n_out,
        grid=(S // ts,),
        in_specs=in_specs,
        out_specs=[pl.BlockSpec((ts, D), row)] * n_out,
        compiler_params=_cparams(("parallel",)),
        name=name,
    )(*args)


def _mixer_fwd(u, pool_w, pool_scale, conv_w, conv_b, cln_g, cln_b, *, ts=256):
    S = u.shape[0]
    hb = CONV_HALO
    nh = ts // hb

    def body(u_ref, uh_ref, pw_ref, ps_ref, cw_ref, cb_ref, g_ref, b_ref, y_ref, d_ref, sta, stg):
        i = pl.program_id(0)
        first = i == 0
        sta[pl.ds(0, hb), :] = jnp.where(first, 0.0, uh_ref[:, 0:D_POOL])
        sta[pl.ds(hb, ts), :] = u_ref[:, 0:D_POOL]
        glu_h = uh_ref[:, D_POOL:D_POOL + D_CONV] * _sigmoid(uh_ref[:, D_POOL + D_CONV:])
        stg[pl.ds(0, hb), :] = jnp.where(first, 0.0, glu_h)
        stg[pl.ds(hb, ts), :] = u_ref[:, D_POOL:D_POOL + D_CONV] * _sigmoid(u_ref[:, D_POOL + D_CONV:])

        pos = (i * ts + lax.broadcasted_iota(jnp.int32, (ts, 1), 0) + 1).astype(F32)
        for g, w in enumerate(POOL_WINDOWS):
            lanes = pl.ds(g * POOL_GROUP, POOL_GROUP)
            a_g = sta[pl.ds(hb, ts), lanes]
            s = a_g
            for j in range(1, w):
                s = s + sta[pl.ds(hb - j, ts), lanes]
            d_g = s / jnp.minimum(pos, float(w)) - a_g
            d_ref[:, lanes] = d_g.astype(BF16)
            y_ref[:, lanes] = (_bdot(d_g, pw_ref[g]) * ps_ref[:, lanes]).astype(BF16)

        acc = jnp.zeros((ts, D_CONV), F32)
        for k in range(CONV_KERNEL):
            acc = acc + cw_ref[k:k + 1, :] * stg[pl.ds(hb - (CONV_KERNEL - 1) + k, ts), :]
        hc = acc + cb_ref[...]
        ln = _layer_norm_rows(hc, g_ref[...], b_ref[...])
        y_ref[:, D_POOL:] = (ln * _sigmoid(ln)).astype(BF16)

    fix2 = lambda i: (0, 0)
    return pl.pallas_call(
        body,
        out_shape=[jax.ShapeDtypeStruct((S, D_MODEL), BF16), jax.ShapeDtypeStruct((S, D_POOL), BF16)],
        grid=(S // ts,),
        in_specs=[pl.BlockSpec((ts, 3 * D_POOL), lambda i: (i, 0)),
                  pl.BlockSpec((hb, 3 * D_POOL), lambda i: (jnp.maximum(i * nh - 1, 0), 0)),
                  pl.BlockSpec((4, POOL_GROUP, POOL_GROUP), lambda i: (0, 0, 0)),
                  pl.BlockSpec((1, D_POOL), fix2), pl.BlockSpec((CONV_KERNEL, D_CONV), fix2),
                  pl.BlockSpec((1, D_CONV), fix2), pl.BlockSpec((1, D_CONV), fix2), pl.BlockSpec((1, D_CONV), fix2)],
        out_specs=[pl.BlockSpec((ts, D_MODEL), lambda i: (i, 0)), pl.BlockSpec((ts, D_POOL), lambda i: (i, 0))],
        scratch_shapes=[pltpu.VMEM((hb + ts, D_POOL), F32), pltpu.VMEM((hb + ts, D_CONV), F32)],
        compiler_params=_cparams(("parallel",)),
        name="mixer_fwd",
    )(u, u, pool_w, pool_scale.reshape(1, D_POOL), conv_w, conv_b.reshape(1, D_CONV), cln_g.reshape(1, D_CONV),
      cln_b.reshape(1, D_CONV))


def _mixer_bwd(u, d, dycat, pool_w, pool_scale, conv_w, conv_b, cln_g, cln_b, *, ts=256):
    S = u.shape[0]
    hb = CONV_HALO
    nh = ts // hb
    n = S // ts
    te = ts + hb
    K = CONV_KERNEL

    def body(u_ref, up_ref, un_ref, d_ref, dy_ref, dyn_ref, pw_ref, ps_ref, cw_ref, cb_ref, g_ref, b_ref,
             du_ref, dpw_ref, dps_ref, dcw_ref, dcb_ref, dg_ref, db_ref, stg, std, sth):
        i = pl.program_id(0)
        first = i == 0
        last = i == n - 1

        @pl.when(first)
        def _():
            dpw_ref[...] = jnp.zeros_like(dpw_ref)
            dps_ref[...] = jnp.zeros_like(dps_ref)
            dcw_ref[...] = jnp.zeros_like(dcw_ref)
            dcb_ref[...] = jnp.zeros_like(dcb_ref)
            dg_ref[...] = jnp.zeros_like(dg_ref)
            db_ref[...] = jnp.zeros_like(db_ref)

        pos_e = (i * ts + lax.broadcasted_iota(jnp.int32, (te, 1), 0) + 1).astype(F32)
        dya = dy_ref[:, 0:D_POOL]
        dya_n = jnp.where(last, 0.0, dyn_ref[:, 0:D_POOL])
        for g, w in enumerate(POOL_WINDOWS):
            lanes = pl.ds(g * POOL_GROUP, POOL_GROUP)
            sl = slice(g * POOL_GROUP, (g + 1) * POOL_GROUP)
            pw = pw_ref[g]
            scale = ps_ref[:, lanes]
            d_g = d_ref[:, lanes]
            pre = _bdot(d_g, pw)
            dps_ref[:, lanes] += jnp.sum(dya[:, sl] * pre, axis=0, keepdims=True)
            dys = dya[:, sl] * scale
            dpw_ref[g] += _bdot(d_g, dys, TN)
            dys_e = jnp.concatenate([dys, dya_n[:, sl] * scale], axis=0)
            dd = _bdot(dys_e, pw, NT)
            std[:, lanes] = dd / jnp.minimum(pos_e, float(w))
            da = -dd[0:ts]
            for m in range(w):
                da = da + std[pl.ds(m, ts), lanes]
            du_ref[:, lanes] = da.astype(BF16)

        glu_p = up_ref[:, D_POOL:D_POOL + D_CONV] * _sigmoid(up_ref[:, D_POOL + D_CONV:])
        stg[pl.ds(0, hb), :] = jnp.where(first, 0.0, glu_p)
        bv = u_ref[:, D_POOL:D_POOL + D_CONV]
        sg = _sigmoid(u_ref[:, D_POOL + D_CONV:])
        stg[pl.ds(hb, ts), :] = bv * sg
        glu_n = un_ref[:, D_POOL:D_POOL + D_CONV] * _sigmoid(un_ref[:, D_POOL + D_CONV:])
        stg[pl.ds(hb + ts, hb), :] = jnp.where(last, 0.0, glu_n)

        acc = jnp.zeros((te, D_CONV), F32)
        for k in range(K):
            acc = acc + cw_ref[k:k + 1, :] * stg[pl.ds(hb - (K - 1) + k, te), :]
        hc = acc + cb_ref[...]
        mu = jnp.mean(hc, axis=-1, keepdims=True)
        hcc = hc - mu
        rstd = lax.rsqrt(jnp.mean(hcc * hcc, axis=-1, keepdims=True) + LN_EPS)
        xh = hcc * rstd
        ln = xh * g_ref[...] + b_ref[...]
        sl_ = _sigmoid(ln)
        dyb = jnp.concatenate([dy_ref[:, D_POOL:], jnp.where(last, 0.0, dyn_ref[:, D_POOL:])], axis=0)
        dln = dyb * (sl_ * (1.0 + ln * (1.0 - sl_)))
        dxh = dln * g_ref[...]
        dhc = rstd * (dxh - jnp.mean(dxh, axis=-1, keepdims=True) - xh * jnp.mean(dxh * xh, axis=-1, keepdims=True))
        sth[...] = dhc
        dg_ref[...] += jnp.sum((dln * xh)[0:ts], axis=0, keepdims=True)
        db_ref[...] += jnp.sum(dln[0:ts], axis=0, keepdims=True)
        dhc_t = dhc[0:ts]
        dcb_ref[...] += jnp.sum(dhc_t, axis=0, keepdims=True)
        dglu = jnp.zeros((ts, D_CONV), F32)
        for k in range(K):
            dcw_ref[k:k + 1, :] += jnp.sum(dhc_t * stg[pl.ds(hb - (K - 1) + k, ts), :], axis=0, keepdims=True)
            dglu = dglu + cw_ref[k:k + 1, :] * sth[pl.ds(K - 1 - k, ts), :]
        du_ref[:, D_POOL:D_POOL + D_CONV] = (dglu * sg).astype(BF16)
        du_ref[:, D_POOL + D_CONV:] = (dglu * bv * sg * (1.0 - sg)).astype(BF16)

    fix2 = lambda i: (0, 0)
    prev = lambda i: (jnp.maximum(i * nh - 1, 0), 0)
    nxt = lambda i: (jnp.minimum((i + 1) * nh, S // hb - 1), 0)
    return pl.pallas_call(
        body,
        out_shape=[jax.ShapeDtypeStruct((S, 3 * D_POOL), BF16),
                   jax.ShapeDtypeStruct((4, POOL_GROUP, POOL_GROUP), F32),
                   jax.ShapeDtypeStruct((1, D_POOL), F32),
                   jax.ShapeDtypeStruct((K, D_CONV), F32),
                   jax.ShapeDtypeStruct((1, D_CONV), F32),
                   jax.ShapeDtypeStruct((1, D_CONV), F32),
                   jax.ShapeDtypeStruct((1, D_CONV), F32)],
        grid=(n,),
        in_specs=[pl.BlockSpec((ts, 3 * D_POOL), lambda i: (i, 0)),
                  pl.BlockSpec((hb, 3 * D_POOL), prev),
                  pl.BlockSpec((hb, 3 * D_POOL), nxt),
                  pl.BlockSpec((ts, D_POOL), lambda i: (i, 0)),
                  pl.BlockSpec((ts, D_MODEL), lambda i: (i, 0)),
                  pl.BlockSpec((hb, D_MODEL), nxt),
                  pl.BlockSpec((4, POOL_GROUP, POOL_GROUP), lambda i: (0, 0, 0)),
                  pl.BlockSpec((1, D_POOL), fix2), pl.BlockSpec((K, D_CONV), fix2),
                  pl.BlockSpec((1, D_CONV), fix2), pl.BlockSpec((1, D_CONV), fix2), pl.BlockSpec((1, D_CONV), fix2)],
        out_specs=[pl.BlockSpec((ts, 3 * D_POOL), lambda i: (i, 0)),
                   pl.BlockSpec((4, POOL_GROUP, POOL_GROUP), lambda i: (0, 0, 0)),
                   pl.BlockSpec((1, D_POOL), fix2), pl.BlockSpec((K, D_CONV), fix2),
                   pl.BlockSpec((1, D_CONV), fix2), pl.BlockSpec((1, D_CONV), fix2), pl.BlockSpec((1, D_CONV), fix2)],
        scratch_shapes=[pltpu.VMEM((hb + ts + hb, D_CONV), F32), pltpu.VMEM((te, D_POOL), F32),
                        pltpu.VMEM((te, D_CONV), F32)],
        compiler_params=_cparams(("arbitrary",)),
        name="mixer_bwd",
    )(u, u, u, d, dycat, dycat, pool_w, pool_scale.reshape(1, D_POOL), conv_w, conv_b.reshape(1, D_CONV),
      cln_g.reshape(1, D_CONV), cln_b.reshape(1, D_CONV))


_GELU_C = math.sqrt(2.0 / math.pi)


def _gelu_parts(x):
    inner = _GELU_C * (x + 0.044715 * x * x * x)
    th = jnp.tanh(inner)
    ge = 0.5 * x * (1.0 + th)
    dge = 0.5 * (1.0 + th) + 0.5 * x * (1.0 - th * th) * (_GELU_C * (1.0 + 3.0 * 0.044715 * x * x))
    return ge, dge


def _ffn_act_fwd(gate, val, dw_w, dw_b, *, ts=256, tc=1408, name):
    S, F = gate.shape
    hb = FFN_HALO
    nh = ts // hb
    tc = _tile(F, tc)

    def body(g_ref, gh_ref, v_ref, w_ref, b_ref, h_ref, st):
        i = pl.program_id(0)
        st[pl.ds(0, hb), :] = jnp.where(i == 0, 0.0, gh_ref[...])
        st[pl.ds(hb, ts), :] = g_ref[...]
        gc = b_ref[...] + w_ref[0:1, :] * st[pl.ds(hb - 2, ts), :] + w_ref[1:2, :] * st[pl.ds(hb - 1, ts), :] \
            + w_ref[2:3, :] * st[pl.ds(hb, ts), :]
        ge, _ = _gelu_parts(gc)
        h_ref[...] = (ge * v_ref[...]).astype(BF16)

    return pl.pallas_call(
        body,
        out_shape=jax.ShapeDtypeStruct((S, F), BF16),
        grid=(S // ts, F // tc),
        in_specs=[pl.BlockSpec((ts, tc), lambda i, j: (i, j)),
                  pl.BlockSpec((hb, tc), lambda i, j: (jnp.maximum(i * nh - 1, 0), j)),
                  pl.BlockSpec((ts, tc), lambda i, j: (i, j)),
                  pl.BlockSpec((3, tc), lambda i, j: (0, j)),
                  pl.BlockSpec((1, tc), lambda i, j: (0, j))],
        out_specs=pl.BlockSpec((ts, tc), lambda i, j: (i, j)),
        scratch_shapes=[pltpu.VMEM((hb + ts, tc), F32)],
        compiler_params=_cparams(("parallel", "parallel")),
        name=name,
    )(gate, gate, val, dw_w, dw_b.reshape(1, F))


def _ffn_act_bwd(gate, val, dh, dw_w, dw_b, *, ts=256, tc=1408, name):
    S, F = gate.shape
    hb = FFN_HALO
    nh = ts // hb
    n = S // ts
    te = ts + hb
    tc = _tile(F, tc)

    def body(g_ref, gp_ref, gn_ref, v_ref, vn_ref, dh_ref, dhn_ref, w_ref, b_ref,
             dg_ref, dv_ref, dw_ref, db_ref, st, sd):
        i = pl.program_id(1)
        first = i == 0
        last = i == n - 1

        @pl.when(first)
        def _():
            dw_ref[...] = jnp.zeros_like(dw_ref)
            db_ref[...] = jnp.zeros_like(db_ref)

        st[pl.ds(0, hb), :] = jnp.where(first, 0.0, gp_ref[...])
        st[pl.ds(hb, ts), :] = g_ref[...]
        st[pl.ds(hb + ts, hb), :] = jnp.where(last, 0.0, gn_ref[...])
        gc = b_ref[...] + w_ref[0:1, :] * st[pl.ds(hb - 2, te), :] + w_ref[1:2, :] * st[pl.ds(hb - 1, te), :] \
            + w_ref[2:3, :] * st[pl.ds(hb, te), :]
        ge, dge = _gelu_parts(gc)
        val_e = jnp.concatenate([v_ref[...], jnp.where(last, 0.0, vn_ref[...])], axis=0)
        dh_e = jnp.concatenate([dh_ref[...], jnp.where(last, 0.0, dhn_ref[...])], axis=0)
        dgc = dh_e * val_e * dge
        sd[...] = dgc
        dv_ref[...] = (dh_e[0:ts] * ge[0:ts]).astype(BF16)
        dgc_t = dgc[0:ts]
        db_ref[...] += jnp.sum(dgc_t, axis=0, keepdims=True)
        dgate = jnp.zeros((ts, tc), F32)
        for k in range(3):
            dw_ref[k:k + 1, :] += jnp.sum(dgc_t * st[pl.ds(hb - 2 + k, ts), :], axis=0, keepdims=True)
            dgate = dgate + w_ref[k:k + 1, :] * sd[pl.ds(2 - k, ts), :]
        dg_ref[...] = dgate.astype(BF16)

    cur = lambda j, i: (i, j)
    prev = lambda j, i: (jnp.maximum(i * nh - 1, 0), j)
    nxt = lambda j, i: (jnp.minimum((i + 1) * nh, S // hb - 1), j)
    return pl.pallas_call(
        body,
        out_shape=[jax.ShapeDtypeStruct((S, F), BF16), jax.ShapeDtypeStruct((S, F), BF16),
                   jax.ShapeDtypeStruct((3, F), F32), jax.ShapeDtypeStruct((1, F), F32)],
        grid=(F // tc, n),
        in_specs=[pl.BlockSpec((ts, tc), cur), pl.BlockSpec((hb, tc), prev), pl.BlockSpec((hb, tc), nxt),
                  pl.BlockSpec((ts, tc), cur), pl.BlockSpec((hb, tc), nxt),
                  pl.BlockSpec((ts, tc), cur), pl.BlockSpec((hb, tc), nxt),
                  pl.BlockSpec((3, tc), lambda j, i: (0, j)), pl.BlockSpec((1, tc), lambda j, i: (0, j))],
        out_specs=[pl.BlockSpec((ts, tc), cur), pl.BlockSpec((ts, tc), cur),
                   pl.BlockSpec((3, tc), lambda j, i: (0, j)), pl.BlockSpec((1, tc), lambda j, i: (0, j))],
        scratch_shapes=[pltpu.VMEM((hb + ts + hb, tc), F32), pltpu.VMEM((te, tc), F32)],
        compiler_params=_cparams(("parallel", "arbitrary")),
        name=name,
    )(gate, gate, gate, val, val, dh, dh, dw_w, dw_b.reshape(1, F))


def _ln_bwd(z, ln_g, ln_b, dout, *, loss_head=False, ts=256, name):
    S, D = z.shape

    def body(z_ref, g_ref, b_ref, do_ref, dz_ref, dg_ref, db_ref, loss_ref):
        i = pl.program_id(0)

        @pl.when(i == 0)
        def _():
            dg_ref[...] = jnp.zeros_like(dg_ref)
            db_ref[...] = jnp.zeros_like(db_ref)
            loss_ref[...] = jnp.zeros_like(loss_ref)

        zt = z_ref[...]
        mu = jnp.mean(zt, axis=-1, keepdims=True)
        zc = zt - mu
        rstd = lax.rsqrt(jnp.mean(zc * zc, axis=-1, keepdims=True) + LN_EPS)
        xh = zc * rstd
        if loss_head:
            err = xh * g_ref[...] + b_ref[...] - do_ref[...]
            loss_ref[...] += 0.5 * jnp.sum(jnp.mean(err * err, axis=-1, keepdims=True))
            do = err * (1.0 / D)
        else:
            do = do_ref[...]
        dg_ref[...] += jnp.sum(do * xh, axis=0, keepdims=True)
        db_ref[...] += jnp.sum(do, axis=0, keepdims=True)
        dxh = do * g_ref[...]
        dz_ref[...] = rstd * (dxh - jnp.mean(dxh, axis=-1, keepdims=True)
                              - xh * jnp.mean(dxh * xh, axis=-1, keepdims=True))

    row = lambda i: (i, 0)
    fix = lambda i: (0, 0)
    return pl.pallas_call(
        body,
        out_shape=[jax.ShapeDtypeStruct((S, D), F32), jax.ShapeDtypeStruct((1, D), F32),
                   jax.ShapeDtypeStruct((1, D), F32), jax.ShapeDtypeStruct((8, 128), F32)],
        grid=(S // ts,),
        in_specs=[pl.BlockSpec((ts, D), row), pl.BlockSpec((1, D), fix), pl.BlockSpec((1, D), fix),
                  pl.BlockSpec((ts, D), row)],
        out_specs=[pl.BlockSpec((ts, D), row), pl.BlockSpec((1, D), fix), pl.BlockSpec((1, D), fix),
                   pl.BlockSpec((8, 128), fix)],
        compiler_params=_cparams(("arbitrary",)),
        name=name,
    )(z, ln_g.reshape(1, D), ln_b.reshape(1, D), dout)


def _ple_bwd(dz, gate, proj, *, ts=256, name):
    S, D = dz.shape

    def body(dz_ref, g_ref, p_ref, ds_ref, dp_ref, db_ref):
        @pl.when(pl.program_id(0) == 0)
        def _():
            db_ref[...] = jnp.zeros_like(db_ref)

        dzt = dz_ref[...]
        g = g_ref[...]
        ds = dzt * p_ref[...] * g * (1.0 - g)
        ds_ref[...] = ds.astype(BF16)
        dp_ref[...] = (dzt * g).astype(BF16)
        db_ref[...] += jnp.sum(ds, axis=0, keepdims=True)

    row = lambda i: (i, 0)
    return pl.pallas_call(
        body,
        out_shape=[jax.ShapeDtypeStruct((S, D), BF16), jax.ShapeDtypeStruct((S, D), BF16),
                   jax.ShapeDtypeStruct((1, D), F32)],
        grid=(S // ts,),
        in_specs=[pl.BlockSpec((ts, D), row)] * 3,
        out_specs=[pl.BlockSpec((ts, D), row), pl.BlockSpec((ts, D), row), pl.BlockSpec((1, D), lambda i: (0, 0))],
        compiler_params=_cparams(("arbitrary",)),
        name=name,
    )(dz, gate, proj)


def _attn_scores(q, kc, bias, qb):
    s = _bdot(q, kc, NT) * (HEAD_DIM ** -0.5) + bias
    kpos = qb * Q_BLOCK + lax.broadcasted_iota(jnp.int32, (1, KV_SPAN), 1)
    s = jnp.where(kpos >= KV_PAD, s, NEG_INF)
    m = jnp.max(s, axis=-1, keepdims=True)
    e = jnp.exp(s - m)
    return e / jnp.sum(e, axis=-1, keepdims=True)


def _attn_fwd(q, kp, vp, bias):
    H, S, dh = q.shape
    sp = kp.shape[1]

    def body(q_ref, k_ref, v_ref, b_ref, o_ref):
        qb = pl.program_id(1)
        start = pl.multiple_of(qb * Q_BLOCK, Q_BLOCK)
        kc = k_ref[pl.ds(start, KV_SPAN), :]
        vc = v_ref[pl.ds(start, KV_SPAN), :]
        p = _attn_scores(q_ref[...], kc, b_ref[...], qb)
        o_ref[...] = _bdot(p, vc).astype(BF16)

    return pl.pallas_call(
        body,
        out_shape=jax.ShapeDtypeStruct((H, S, dh), BF16),
        grid=(H, S // Q_BLOCK),
        in_specs=[pl.BlockSpec((None, Q_BLOCK, dh), lambda h, i: (h, i, 0)),
                  pl.BlockSpec((None, sp, dh), lambda h, i: (h, 0, 0)),
                  pl.BlockSpec((None, sp, dh), lambda h, i: (h, 0, 0)),
                  pl.BlockSpec((None, Q_BLOCK, KV_SPAN), lambda h, i: (h, 0, 0))],
        out_specs=pl.BlockSpec((None, Q_BLOCK, dh), lambda h, i: (h, i, 0)),
        compiler_params=_cparams(("parallel", "arbitrary")),
        name="attn_fwd",
    )(q, kp, vp, bias)


def _attn_bwd(q, kp, vp, bias, do):
    H, S, dh = q.shape
    sp = kp.shape[1]
    scale = HEAD_DIM ** -0.5

    def body(q_ref, k_ref, v_ref, b_ref, do_ref, dq_ref, dk_ref, dv_ref, db_ref):
        qb = pl.program_id(1)

        @pl.when(qb == 0)
        def _():
            dk_ref[...] = jnp.zeros_like(dk_ref)
            dv_ref[...] = jnp.zeros_like(dv_ref)
            db_ref[...] = jnp.zeros_like(db_ref)

        start = pl.multiple_of(qb * Q_BLOCK, Q_BLOCK)
        span = pl.ds(start, KV_SPAN)
        qt = q_ref[...]
        kc = k_ref[span, :]
        vc = v_ref[span, :]
        dot = do_ref[...]
        p = _attn_scores(qt, kc, b_ref[...], qb)
        dv_ref[span, :] += _bdot(p, dot, TN)
        dp = _bdot(dot, vc, NT)
        ds = p * (dp - jnp.sum(p * dp, axis=-1, keepdims=True))
        db_ref[...] += ds
        dq_ref[...] = (scale * _bdot(ds, kc)).astype(BF16)
        dk_ref[span, :] += scale * _bdot(ds, qt, TN)

    blk = pl.BlockSpec((None, Q_BLOCK, dh), lambda h, i: (h, i, 0))
    full = pl.BlockSpec((None, sp, dh), lambda h, i: (h, 0, 0))
    bsp = pl.BlockSpec((None, Q_BLOCK, KV_SPAN), lambda h, i: (h, 0, 0))
    return pl.pallas_call(
        body,
        out_shape=[jax.ShapeDtypeStruct((H, S, dh), BF16), jax.ShapeDtypeStruct((H, sp, dh), F32),
                   jax.ShapeDtypeStruct((H, sp, dh), F32), jax.ShapeDtypeStruct((H, Q_BLOCK, KV_SPAN), F32)],
        grid=(H, S // Q_BLOCK),
        in_specs=[blk, full, full, bsp, blk],
        out_specs=[blk, full, full, bsp],
        compiler_params=_cparams(("parallel", "arbitrary")),
        name="attn_bwd",
    )(q, kp, vp, bias, do)


def _bias_blocks(rel_bias):
    H = rel_bias.shape[0]
    n_e = BAND + CHUNK - 1
    n_clip = KV_PAD + CHUNK - 1 - MAX_REL + 1
    e = jnp.concatenate([jnp.broadcast_to(rel_bias[:, 2 * MAX_REL:], (H, n_clip)),
                         jnp.flip(rel_bias[:, 2 * MAX_REL - (n_e - n_clip):2 * MAX_REL], axis=1)], axis=1)
    skew = jnp.pad(jnp.tile(e, (1, CHUNK)), ((0, 0), (0, CHUNK))).reshape(H, CHUNK, n_e + 1)
    band = jnp.flip(skew, axis=1)[:, :, :BAND]
    rows = [jnp.pad(band, ((0, 0), (0, 0), (c * CHUNK, KV_SPAN - BAND - c * CHUNK)), constant_values=NEG_INF)
            for c in range(Q_BLOCK // CHUNK)]
    return jnp.concatenate(rows, axis=1)


def _bias_blocks_grad(dblk):
    H = dblk.shape[0]
    n_e = BAND + CHUNK - 1
    n_clip = KV_PAD + CHUNK - 1 - MAX_REL + 1
    parts = jnp.stack([dblk[:, c * CHUNK:(c + 1) * CHUNK, c * CHUNK:c * CHUNK + BAND]
                       for c in range(Q_BLOCK // CHUNK)], axis=1)
    parts = jnp.flip(parts, axis=2)
    parts = jnp.pad(parts, ((0, 0), (0, 0), (0, 0), (0, n_e + 1 - BAND)))
    skew = parts.reshape(H, Q_BLOCK // CHUNK, CHUNK * (n_e + 1))[:, :, :CHUNK * n_e]
    skew = skew.reshape(H, Q_BLOCK, n_e)
    skew = jnp.pad(skew, ((0, 0), (0, 0), (0, 1)))

    def body(s_ref, o_ref):
        de = jnp.sum(s_ref[...], axis=0, keepdims=True)
        lane = lax.broadcasted_iota(jnp.int32, de.shape, 1)
        far = jnp.sum(jnp.where(lane < n_clip, de, 0.0), axis=-1, keepdims=True)
        o_ref[...] = jnp.where(lane == 0, far, jnp.where(lane < n_clip, 0.0, de))

    de = pl.pallas_call(
        body,
        out_shape=jax.ShapeDtypeStruct((H, 1, n_e + 1), F32),
        grid=(H,),
        in_specs=[pl.BlockSpec((None, Q_BLOCK, n_e + 1), lambda h: (h, 0, 0))],
        out_specs=pl.BlockSpec((None, 1, n_e + 1), lambda h: (h, 0, 0)),
        compiler_params=_cparams(("parallel",)),
        name="bias_grad_sum",
    )(skew).reshape(H, n_e + 1)
    near = jnp.flip(de[:, n_clip:n_e], axis=1)
    return jnp.concatenate([jnp.zeros((H, 2 * MAX_REL - (n_e - n_clip)), F32), near, de[:, 0:1]], axis=1)


def _ffn_forward(r1, p_l, w, l):
    up_g = _mm(r1, w["ffn_up_g"][l], tn=1408, name=f"ffn_up_g{l}")
    up_v = _mm(r1, w["ffn_up_v"][l], tn=1408, name=f"ffn_up_v{l}")
    h = _ffn_act_fwd(up_g, up_v, w["ffn_dw_w"][l], w["ffn_dw_b"][l], name=f"ffn_act{l}")
    z2, r2, gate, proj = _proj_ln(r1, h, w["ffn_w_down"][l], w["ln_ffn_g"][l], w["ln_ffn_b"][l],
                                  ple=(w["ple_w_gate"][l], w["ple_b_gate"][l], p_l, w["ple_w_proj"][l]),
                                  name=f"ffn_down_ln{l}")
    return dict(r1=r1, up_g=up_g, up_v=up_v, h=h, z2=z2, gate=gate, proj=proj), r2


def _ffn_backward(sv, dz2, p_l, w, l, grads):
    r1 = sv["r1"]
    ds, dproj, db_gate = _ple_bwd(dz2, sv["gate"], sv["proj"], name=f"ple_bwd{l}")
    dh = _mm(dz2, w["ffn_w_down"][l], tb=True, tn=1408, name=f"ffn_dh{l}")
    dgate, dval, d_dw_w, d_dw_b = _ffn_act_bwd(sv["up_g"], sv["up_v"], dh, w["ffn_dw_w"][l], w["ffn_dw_b"][l],
                                               name=f"ffn_act_bwd{l}")
    grads["ffn_w_down"][l] = _mm(sv["h"], dz2, ta=True, tm=1408, tn=1024, tk=512, name=f"d_ffn_w_down{l}")
    grads["ffn_up_g"][l] = _mm(r1, dgate, ta=True, tm=1024, tn=1408, tk=512, name=f"d_ffn_up_g{l}")
    grads["ffn_up_v"][l] = _mm(r1, dval, ta=True, tm=1024, tn=1408, tk=512, name=f"d_ffn_up_v{l}")
    grads["ple_w_gate"][l] = _mm(r1, ds, ta=True, tm=1024, tn=1024, tk=512, name=f"d_ple_w_gate{l}")
    grads["ple_w_proj"][l] = _mm(p_l, dproj, ta=True, tm=256, tn=1024, tk=512, name=f"d_ple_w_proj{l}")
    grads["ffn_dw_w"][l] = d_dw_w
    grads["ffn_dw_b"][l] = d_dw_b[0]
    grads["ple_b_gate"][l] = db_gate[0]
    t = _mm(ds, w["ple_w_gate"][l], tb=True, add=dz2, add_scale=ALPHA, tn=1024, name=f"dr1_gate{l}")
    t = _mm(dgate, w["ffn_up_g"][l], tb=True, add=t, tn=1024, tk=1408, name=f"dr1_up_g{l}")
    return _mm(dval, w["ffn_up_v"][l], tb=True, add=t, tn=1024, tk=1408, name=f"dr1_up_v{l}")


def _to_heads(a, n):
    S = a.shape[0]
    t = a.reshape(S, n, N_HEADS, HEAD_DIM).transpose(1, 2, 0, 3)
    return [t[i] for i in range(n)]


def _local_step(x, p, target, w):
    S = x.shape[0]
    grads = {k: [None, None] for k in ("ffn_w_down", "ffn_up_g", "ffn_up_v", "ple_w_gate", "ple_w_proj", "ffn_dw_w",
                                       "ffn_dw_b", "ple_b_gate", "ln_ffn_g", "ln_ffn_b", "ln_mix_g", "ln_mix_b")}

    u = _mm(x, w["mix_w_in"], name="mix_in")
    ycat, dpool = _mixer_fwd(u, w["pool_w"], w["pool_scale"], w["conv_dw_w"], w["conv_dw_b"], w["conv_ln_g"],
                             w["conv_ln_b"])
    z1, r1 = _proj_ln(x, ycat, w["mix_w_out"], w["ln_mix_g"][0], w["ln_mix_b"][0], name="mix_out_ln")
    sv0, r2 = _ffn_forward(r1, p[0], w, 0)

    qkv = _mm(r2, w["attn_w_qkv"], out_dtype=BF16, name="attn_qkv")
    q, k, v = _to_heads(qkv, 3)
    kp = jnp.pad(k, ((0, 0), (KV_PAD, 0), (0, 0)))
    vp = jnp.pad(v, ((0, 0), (KV_PAD, 0), (0, 0)))
    bias = _bias_blocks(w["attn_rel_bias"])
    attn_h = _attn_fwd(q, kp, vp, bias)
    attn = attn_h.transpose(1, 0, 2).reshape(S, D_MODEL)
    z3, r3 = _proj_ln(r2, attn, w["attn_w_o"], w["ln_mix_g"][1], w["ln_mix_b"][1], name="attn_out_ln")
    sv1, _ = _ffn_forward(r3, p[1], w, 1)

    dz4, grads["ln_ffn_g"][1], grads["ln_ffn_b"][1], loss = _ln_bwd(sv1["z2"], w["ln_ffn_g"][1], w["ln_ffn_b"][1],
                                                                    target, loss_head=True, name="loss_ln_bwd")
    dr3 = _ffn_backward(sv1, dz4, p[1], w, 1, grads)
    dz3, grads["ln_mix_g"][1], grads["ln_mix_b"][1], _ = _ln_bwd(z3, w["ln_mix_g"][1], w["ln_mix_b"][1], dr3,
                                                                name="ln_mix_bwd1")
    grads["attn_w_o"] = _mm(attn, dz3, ta=True, tm=1024, tn=1024, tk=512, name="d_attn_w_o")
    dattn = _mm(dz3, w["attn_w_o"], tb=True, out_dtype=BF16, tn=1024, name="d_attn")
    (dattn_h,) = _to_heads(dattn, 1)
    dq, dk, dv, dbias = _attn_bwd(q, kp, vp, bias, dattn_h)
    grads["attn_rel_bias"] = _bias_blocks_grad(dbias)
    dqkv = jnp.stack([dq, dk[:, KV_PAD:].astype(BF16), dv[:, KV_PAD:].astype(BF16)], axis=0)
    dqkv = dqkv.transpose(2, 0, 1, 3).reshape(S, 3 * D_MODEL)
    grads["attn_w_qkv"] = _mm(r2, dqkv, ta=True, tm=1024, tn=1024, tk=512, name="d_attn_w_qkv")
    dr2 = _mm(dqkv, w["attn_w_qkv"], tb=True, add=dz3, add_scale=ALPHA, tn=1024, name="dr2")

    dz2, grads["ln_ffn_g"][0], grads["ln_ffn_b"][0], _ = _ln_bwd(sv0["z2"], w["ln_ffn_g"][0], w["ln_ffn_b"][0], dr2,
                                                                name="ln_ffn_bwd0")
    dr1 = _ffn_backward(sv0, dz2, p[0], w, 0, grads)
    dz1, grads["ln_mix_g"][0], grads["ln_mix_b"][0], _ = _ln_bwd(z1, w["ln_mix_g"][0], w["ln_mix_b"][0], dr1,
                                                                name="ln_mix_bwd0")
    grads["mix_w_out"] = _mm(ycat, dz1, ta=True, tm=1024, tn=1024, tk=512, name="d_mix_w_out")
    dycat = _mm(dz1, w["mix_w_out"], tb=True, tn=1024, name="d_ycat")
    du, g_pw, g_ps, g_cw, g_cb, g_cg, g_cbb = _mixer_bwd(u, dpool, dycat, w["pool_w"], w["pool_scale"],
                                                         w["conv_dw_w"], w["conv_dw_b"], w["conv_ln_g"],
                                                         w["conv_ln_b"])
    grads["mix_w_in"] = _mm(x, du, ta=True, tm=1024, tn=512, tk=512, name="d_mix_w_in")
    grad_x = _mm(du, w["mix_w_in"], tb=True, add=dz1, add_scale=ALPHA, tn=1024, tk=512, name="grad_x")
    grads.update(pool_w=g_pw, pool_scale=g_ps[0], conv_dw_w=g_cw, conv_dw_b=g_cb[0], conv_ln_g=g_cg[0],
                 conv_ln_b=g_cbb[0])
    for kname in ("ln_ffn_g", "ln_ffn_b", "ln_mix_g", "ln_mix_b"):
        grads[kname] = [a[0] for a in grads[kname]]
    return loss[0, 0], grad_x, grads


def _exchange(bufs, places, *, name):
    nb = len(bufs)

    def slot(ref, place, shape, k):
        if place in ("stack", "pieces"):
            return ref.at[k]
        ax = place[1]
        n = shape[ax]
        idx = (slice(None),) * ax + (pl.ds(pl.multiple_of(k * n, n), n),)
        return ref.at[idx]

    def body(*refs):
        srcs, dsts = refs[:nb], refs[nb:2 * nb]
        send_sems, recv_sems, local_sems = refs[2 * nb:]
        x, y, c = lax.axis_index("x"), lax.axis_index("y"), lax.axis_index("c")
        me = 4 * x + 2 * y + c
        local = []
        remote = []
        for b in range(nb):
            pieces = places[b] == "pieces"
            shape = bufs[b].shape
            cp = pltpu.make_async_copy(srcs[b].at[me] if pieces else srcs[b], slot(dsts[b], places[b], shape, me),
                                       local_sems.at[b])
            cp.start()
            local.append(cp)
            for d in range(1, N_DEV):
                px, py, pc = x ^ ((d >> 2) & 1), y ^ ((d >> 1) & 1), c ^ (d & 1)
                peer = 4 * px + 2 * py + pc
                src = srcs[b].at[peer] if pieces else srcs[b]
                out = pltpu.make_async_remote_copy(
                    src_ref=src, dst_ref=slot(dsts[b], places[b], shape, me),
                    send_sem=send_sems.at[b, d], recv_sem=recv_sems.at[b, d],
                    device_id=(px, py, pc), device_id_type=pl.DeviceIdType.MESH)
                out.start()
                inc = pltpu.make_async_remote_copy(
                    src_ref=src, dst_ref=slot(dsts[b], places[b], shape, peer),
                    send_sem=send_sems.at[b, d], recv_sem=recv_sems.at[b, d],
                    device_id=(px, py, pc), device_id_type=pl.DeviceIdType.MESH)
                remote.append((out, inc))
        for cp in local:
            cp.wait()
        for out, inc in remote:
            out.wait_send()
            inc.wait_recv()

    out_shapes = []
    for b, place in zip(bufs, places):
        if place == "stack":
            shp = (N_DEV,) + b.shape
        elif place == "pieces":
            shp = b.shape
        else:
            shp = tuple(s * N_DEV if i == place[1] else s for i, s in enumerate(b.shape))
        out_shapes.append(jax.ShapeDtypeStruct(shp, b.dtype))
    any_spec = pl.BlockSpec(memory_space=pl.ANY)
    return pl.pallas_call(
        body,
        out_shape=out_shapes,
        in_specs=[any_spec] * nb,
        out_specs=[any_spec] * nb,
        scratch_shapes=[pltpu.SemaphoreType.DMA((nb, N_DEV)), pltpu.SemaphoreType.DMA((nb, N_DEV)),
                        pltpu.SemaphoreType.DMA((nb,))],
        compiler_params=pltpu.CompilerParams(has_side_effects=True),
        name=name,
    )(*bufs)


def _adamw(recv, w, m, v, *, name):
    R, C = w.shape
    tr = R
    for cand in (512, 256, 128, 64, 32, 16):
        if R % cand == 0 and cand * C * 4 <= 2 * 1024 * 1024:
            tr = cand
            break
    c1 = 1.0 - ADAM_B1 ** ADAM_STEP
    c2 = 1.0 - ADAM_B2 ** ADAM_STEP

    def body(r_ref, w_ref, m_ref, v_ref, g_ref, d_ref, mo_ref, vo_ref):
        g = r_ref[0].astype(F32)
        for i in range(1, N_DEV):
            g = g + r_ref[i].astype(F32)
        m_new = ADAM_B1 * m_ref[...] + (1.0 - ADAM_B1) * g
        v_new = ADAM_B2 * v_ref[...] + (1.0 - ADAM_B2) * (g * g)
        m_hat = m_new / c1
        v_hat = v_new / c2
        g_ref[...] = g
        d_ref[...] = -ADAM_LR * (m_hat / (jnp.sqrt(v_hat) + ADAM_EPS) + ADAM_WD * w_ref[...])
        mo_ref[...] = m_new
        vo_ref[...] = v_new

    row = pl.BlockSpec((tr, C), lambda i: (i, 0))
    return pl.pallas_call(
        body,
        out_shape=[jax.ShapeDtypeStruct((R, C), F32)] * 4,
        grid=(R // tr,),
        in_specs=[pl.BlockSpec((N_DEV, tr, C), lambda i: (0, i, 0)), row, row, row],
        out_specs=[row] * 4,
        compiler_params=_cparams(("parallel",)),
        name=name,
    )(recv, w, m, v)


_SHARDED = (("mix_w_in", 2, BF16, "stack"), ("mix_w_out", 1, BF16, ("axis", 1)), ("attn_w_qkv", 2, BF16, ("axis", 2)),
            ("attn_w_o", 1, BF16, ("axis", 1)), ("ffn_w_up", 2, BF16, "stack"), ("ffn_w_down", 1, BF16, ("axis", 1)),
            ("ple_w_proj", 2, BF16, ("axis", 2)), ("ple_w_gate", 1, BF16, ("axis", 1)),
            ("conv_dw_w", 2, F32, "stack"), ("ffn_dw_w", 2, F32, "stack"))
_REPLICATED = ("pool_w", "pool_scale", "conv_dw_b", "conv_ln_g", "conv_ln_b", "attn_rel_bias", "ln_mix_g",
               "ln_mix_b", "ffn_dw_b", "ple_b_gate", "ln_ffn_g", "ln_ffn_b")


def _pack_rows(parts, row_mult, dtype):
    lead = parts[0].shape[:-1]
    flat = jnp.concatenate([a.astype(dtype) for a in parts], axis=-1)
    n = flat.shape[-1]
    unit = row_mult * LANES
    padded = -(-n // unit) * unit
    flat = jnp.pad(flat, [(0, 0)] * len(lead) + [(0, padded - n)])
    return flat.reshape(lead + (padded // LANES, LANES))


def _unpack(flat2d, shapes):
    flat = flat2d.reshape(-1)
    out, o = [], 0
    for s in shapes:
        n = math.prod(s)
        out.append(flat[o:o + n].reshape(s))
        o += n
    return out


def _full_from_shards(g, axis):
    parts = jnp.moveaxis(g, 0, axis)
    shp = list(g.shape[1:])
    shp[axis] *= N_DEV
    return parts.reshape(shp)


def _pieces_from_full(full, axis):
    shp = list(full.shape)
    n = shp[axis] // N_DEV
    t = full.reshape(shp[:axis] + [N_DEV, n] + shp[axis + 1:])
    return jnp.moveaxis(t, axis, 0)


def kernel(x, p, mix_w_in, pool_w, pool_scale, conv_dw_w, conv_dw_b, conv_ln_g, conv_ln_b, mix_w_out, attn_w_qkv, attn_rel_bias, attn_w_o, ln_mix_g, ln_mix_b, ffn_w_up, ffn_dw_w, ffn_dw_b, ffn_w_down, ple_w_proj, ple_w_gate, ple_b_gate, ln_ffn_g, ln_ffn_b, loss_target, m_mix_w_in, m_pool_w, m_pool_scale, m_conv_dw_w, m_conv_dw_b, m_conv_ln_g, m_conv_ln_b, m_mix_w_out, m_attn_w_qkv, m_attn_rel_bias, m_attn_w_o, m_ln_mix_g, m_ln_mix_b, m_ffn_w_up, m_ffn_dw_w, m_ffn_dw_b, m_ffn_w_down, m_ple_w_proj, m_ple_w_gate, m_ple_b_gate, m_ln_ffn_g, m_ln_ffn_b, v_mix_w_in, v_pool_w, v_pool_scale, v_conv_dw_w, v_conv_dw_b, v_conv_ln_g, v_conv_ln_b, v_mix_w_out, v_attn_w_qkv, v_attn_rel_bias, v_attn_w_o, v_ln_mix_g, v_ln_mix_b, v_ffn_w_up, v_ffn_dw_w, v_ffn_dw_b, v_ffn_w_down, v_ple_w_proj, v_ple_w_gate, v_ple_b_gate, v_ln_ffn_g, v_ln_ffn_b):
    a = dict(locals())
    sh_names = [n for n, _, _, _ in _SHARDED]
    names = sh_names + list(_REPLICATED)
    wts = {n: a[n] for n in names}
    mom = {n: a["m_" + n] for n in names}
    var = {n: a["v_" + n] for n in names}

    gathered = _exchange([wts[n].astype(dt) for n, _, dt, _ in _SHARDED], [pl_ for _, _, _, pl_ in _SHARDED],
                         name="weight_all_gather")
    full = {}
    for (nm, ax, _, place), g in zip(_SHARDED, gathered):
        full[nm] = _full_from_shards(g, ax) if place == "stack" else g

    w = dict(
        mix_w_in=full["mix_w_in"][0], mix_w_out=full["mix_w_out"][0], attn_w_qkv=full["attn_w_qkv"][0],
        attn_w_o=full["attn_w_o"][0], conv_dw_w=full["conv_dw_w"][0],
        ffn_up_g=[full["ffn_w_up"][l][:, :D_FF] for l in range(2)],
        ffn_up_v=[full["ffn_w_up"][l][:, D_FF:] for l in range(2)],
        ffn_w_down=full["ffn_w_down"], ple_w_proj=full["ple_w_proj"], ple_w_gate=full["ple_w_gate"],
        ffn_dw_w=full["ffn_dw_w"],
        pool_w=pool_w[0], pool_scale=pool_scale[0], conv_dw_b=conv_dw_b[0], conv_ln_g=conv_ln_g[0],
        conv_ln_b=conv_ln_b[0], attn_rel_bias=attn_rel_bias[0], ln_mix_g=ln_mix_g, ln_mix_b=ln_mix_b,
        ffn_dw_b=ffn_dw_b, ple_b_gate=ple_b_gate, ln_ffn_g=ln_ffn_g, ln_ffn_b=ln_ffn_b)

    loss_part, grad_x, gr = _local_step(x[0], p[:, 0], loss_target[0], w)
    loss = lax.psum(loss_part, ("x", "y", "c"))

    gfull = dict(
        mix_w_in=gr["mix_w_in"][None], mix_w_out=gr["mix_w_out"][None], attn_w_qkv=gr["attn_w_qkv"][None],
        attn_w_o=gr["attn_w_o"][None], conv_dw_w=gr["conv_dw_w"][None],
        ffn_w_up=jnp.stack([jnp.concatenate([gr["ffn_up_g"][l], gr["ffn_up_v"][l]], axis=1) for l in range(2)]),
        ffn_w_down=jnp.stack(gr["ffn_w_down"]), ple_w_proj=jnp.stack(gr["ple_w_proj"]),
        ple_w_gate=jnp.stack(gr["ple_w_gate"]), ffn_dw_w=jnp.stack(gr["ffn_dw_w"]),
        pool_w=gr["pool_w"][None], pool_scale=gr["pool_scale"][None], conv_dw_b=gr["conv_dw_b"][None],
        conv_ln_g=gr["conv_ln_g"][None], conv_ln_b=gr["conv_ln_b"][None], attn_rel_bias=gr["attn_rel_bias"][None],
        ln_mix_g=jnp.stack(gr["ln_mix_g"]), ln_mix_b=jnp.stack(gr["ln_mix_b"]), ffn_dw_b=jnp.stack(gr["ffn_dw_b"]),
        ple_b_gate=jnp.stack(gr["ple_b_gate"]), ln_ffn_g=jnp.stack(gr["ln_ffn_g"]),
        ln_ffn_b=jnp.stack(gr["ln_ffn_b"]))

    g_send = [_pieces_from_full(gfull[n], ax).astype(BF16) for n, ax, _, _ in _SHARDED]
    rep_send = _pack_rows([gfull[n].reshape(-1) for n in _REPLICATED], 8, F32)
    g_recv = _exchange(g_send, ["pieces"] * len(g_send), name="grad_exchange")
    (rep_recv,) = _exchange([rep_send], ["stack"], name="grad_all_gather")

    def flat_state(d):
        return _pack_rows([d[n].reshape(-1) for n in _REPLICATED], 8, F32)

    res = [{}, {}, {}, {}]
    for n, recv in zip(sh_names, g_recv):
        shp = wts[n].shape
        two_d = (math.prod(shp[:-1]), shp[-1])
        out = _adamw(recv.reshape((N_DEV,) + two_d), wts[n].reshape(two_d), mom[n].reshape(two_d),
                     var[n].reshape(two_d), name="adamw_" + n)
        for k in range(4):
            res[k][n] = out[k].reshape(shp)
    rep_out = _adamw(rep_recv, flat_state(wts), flat_state(mom), flat_state(var), name="adamw_replicated")
    for k in range(4):
        for n, arr in zip(_REPLICATED, _unpack(rep_out[k], [wts[n].shape for n in _REPLICATED])):
            res[k][n] = arr
    order = ["mix_w_in", "pool_w", "pool_scale", "conv_dw_w", "conv_dw_b", "conv_ln_g", "conv_ln_b", "mix_w_out",
             "attn_w_qkv", "attn_rel_bias", "attn_w_o", "ln_mix_g", "ln_mix_b", "ffn_w_up", "ffn_dw_w", "ffn_dw_b",
             "ffn_w_down", "ple_w_proj", "ple_w_gate", "ple_b_gate", "ln_ffn_g", "ln_ffn_b"]
    outs = [loss, grad_x[None]]
    for k in range(4):
        outs += [res[k][n] for n in order]
    return tuple(outs)
```

```python
import functools
import math

import jax
import jax.numpy as jnp
from jax import lax
from jax.experimental import pallas as pl
from jax.experimental.pallas import tpu as pltpu

F32 = jnp.float32
BF16 = jnp.bfloat16

N_DEV = 8
D_MODEL = 1024
D_POOL = 512
D_CONV = 512
POOL_WINDOWS = (2, 4, 8, 16)
POOL_GROUP = 128
CONV_KERNEL = 31
CHUNK = 64
HEAD_DIM = 64
N_HEADS = 16
LEFT_CHUNKS = 8
BAND = (LEFT_CHUNKS + 1) * CHUNK
MAX_REL = 256
D_FF = 2816
PLE_DIM = 256
ALPHA = 4.0 ** 0.25
LN_EPS = 1e-5
NEG_INF = -1e30
ADAM_LR, ADAM_B1, ADAM_B2, ADAM_EPS, ADAM_WD, ADAM_STEP = 0.001, 0.9, 0.999, 1e-08, 0.01, 10

Q_BLOCK = 4 * CHUNK
KV_PAD = LEFT_CHUNKS * CHUNK
KV_SPAN = KV_PAD + Q_BLOCK
CONV_HALO = 32
FFN_HALO = 8
LANES = 1024
VMEM_LIMIT = 56 * 1024 * 1024


def _cparams(sem=None):
    return pltpu.CompilerParams(dimension_semantics=sem, vmem_limit_bytes=VMEM_LIMIT)


def _tile(dim, pref):
    if dim <= pref:
        return dim
    t = pref - pref % 128
    while t >= 128:
        if dim % t == 0:
            return t
        t -= 128
    return dim


def _sigmoid(x):
    return 1.0 / (1.0 + jnp.exp(-x))


def _bdot(a, b, dn=(((1,), (0,)), ((), ()))):
    return lax.dot_general(a.astype(BF16), b.astype(BF16), dn, preferred_element_type=F32)


NT = (((1,), (1,)), ((), ()))
TN = (((0,), (0,)), ((), ()))


def _mm(a, b, *, ta=False, tb=False, add=None, add_scale=1.0, out_dtype=F32, tm=512, tn=512, tk=1024, name):
    if ta:
        K, M = a.shape
    else:
        M, K = a.shape
    if tb:
        N, kb = b.shape
    else:
        kb, N = b.shape
    assert K == kb, (a.shape, b.shape)
    tm, tn, tk = _tile(M, tm), _tile(N, tn), _tile(K, tk)
    nk = K // tk
    a_spec = pl.BlockSpec((tk, tm), lambda i, j, k: (k, i)) if ta else pl.BlockSpec((tm, tk), lambda i, j, k: (i, k))
    b_spec = pl.BlockSpec((tn, tk), lambda i, j, k: (j, k)) if tb else pl.BlockSpec((tk, tn), lambda i, j, k: (k, j))
    dn = (((0 if ta else 1,), (1 if tb else 0,)), ((), ()))
    has_add = add is not None

    def body(*refs):
        if has_add:
            a_ref, b_ref, add_ref, o_ref, acc = refs
        else:
            a_ref, b_ref, o_ref, acc = refs
        k = pl.program_id(2)

        @pl.when(k == 0)
        def _():
            acc[...] = jnp.zeros_like(acc)

        acc[...] += _bdot(a_ref[...], b_ref[...], dn)

        @pl.when(k == nk - 1)
        def _():
            r = acc[...]
            if has_add:
                r = r + add_scale * add_ref[...]
            o_ref[...] = r.astype(out_dtype)

    in_specs = [a_spec, b_spec]
    args = [a, b]
    if has_add:
        in_specs.append(pl.BlockSpec((tm, tn), lambda i, j, k: (i, j)))
        args.append(add)
    return pl.pallas_call(
        body,
        out_shape=jax.ShapeDtypeStruct((M, N), out_dtype),
        grid=(M // tm, N // tn, nk),
        in_specs=in_specs,
        out_specs=pl.BlockSpec((tm, tn), lambda i, j, k: (i, j)),
        scratch_shapes=[pltpu.VMEM((tm, tn), F32)],
        compiler_params=_cparams(("parallel", "parallel", "arbitrary")),
        name=name,
    )(*args)


def _layer_norm_rows(z, g, b):
    mu = jnp.mean(z, axis=-1, keepdims=True)
    zc = z - mu
    var = jnp.mean(zc * zc, axis=-1, keepdims=True)
    return zc * lax.rsqrt(var + LN_EPS) * g + b


def _proj_ln(res, a, w, ln_g, ln_b, *, ple=None, ts=256, name):
    S, D = res.shape
    ka = a.shape[1]
    has_ple = ple is not None
    row = lambda i: (i, 0)
    fix = lambda i: (0, 0)

    def body(*refs):
        if has_ple:
            res_ref, a_ref, w_ref, g_ref, b_ref, wg_ref, bg_ref, p_ref, wp_ref, z_ref, r_ref, gate_ref, proj_ref = refs
        else:
            res_ref, a_ref, w_ref, g_ref, b_ref, z_ref, r_ref = refs
        res_t = res_ref[...]
        acc = _bdot(a_ref[...], w_ref[...])
        if has_ple:
            gate = _sigmoid(_bdot(res_t, wg_ref[...]) + bg_ref[...])
            proj = _bdot(p_ref[...], wp_ref[...])
            gate_ref[...] = gate
            proj_ref[...] = proj
            acc = acc + gate * proj
        z = ALPHA * res_t + acc
        z_ref[...] = z
        r_ref[...] = _layer_norm_rows(z, g_ref[...], b_ref[...])

    in_specs = [pl.BlockSpec((ts, D), row), pl.BlockSpec((ts, ka), row), pl.BlockSpec((ka, D), fix),
                pl.BlockSpec((1, D), fix), pl.BlockSpec((1, D), fix)]
    args = [res, a, w, ln_g.reshape(1, D), ln_b.reshape(1, D)]
    n_out = 2
    if has_ple:
        wg, bg, p, wp = ple
        in_specs += [pl.BlockSpec((D, D), fix), pl.BlockSpec((1, D), fix), pl.BlockSpec((ts, PLE_DIM), row),
                     pl.BlockSpec((PLE_DIM, D), fix)]
        args += [wg, bg.reshape(1, D), p, wp]
        n_out = 4
    return pl.pallas_call(
        body,
        out_shape=[jax.ShapeDtypeStruct((S, D), F32)] * n_out,
        grid=(S // ts,),
        in_specs=in_specs,
        out_specs=[pl.BlockSpec((ts, D), row)] * n_out,
        compiler_params=_cparams(("parallel",)),
        name=name,
    )(*args)


def _mixer_fwd(u, pool_w, pool_scale, conv_w, conv_b, cln_g, cln_b, *, ts=256):
    S = u.shape[0]
    hb = CONV_HALO
    nh = ts // hb

    def body(u_ref, uh_ref, pw_ref, ps_ref, cw_ref, cb_ref, g_ref, b_ref, y_ref, d_ref, sta, stg):
        i = pl.program_id(0)
        first = i == 0
        sta[pl.ds(0, hb), :] = jnp.where(first, 0.0, uh_ref[:, 0:D_POOL])
        sta[pl.ds(hb, ts), :] = u_ref[:, 0:D_POOL]
        glu_h = uh_ref[:, D_POOL:D_POOL + D_CONV] * _sigmoid(uh_ref[:, D_POOL + D_CONV:])
        stg[pl.ds(0, hb), :] = jnp.where(first, 0.0, glu_h)
        stg[pl.ds(hb, ts), :] = u_ref[:, D_POOL:D_POOL + D_CONV] * _sigmoid(u_ref[:, D_POOL + D_CONV:])

        pos = (i * ts + lax.broadcasted_iota(jnp.int32, (ts, 1), 0) + 1).astype(F32)
        for g, w in enumerate(POOL_WINDOWS):
            lanes = pl.ds(g * POOL_GROUP, POOL_GROUP)
            a_g = sta[pl.ds(hb, ts), lanes]
            s = a_g
            for j in range(1, w):
                s = s + sta[pl.ds(hb - j, ts), lanes]
            d_g = s / jnp.minimum(pos, float(w)) - a_g
            d_ref[:, lanes] = d_g.astype(BF16)
            y_ref[:, lanes] = (_bdot(d_g, pw_ref[g]) * ps_ref[:, lanes]).astype(BF16)

        acc = jnp.zeros((ts, D_CONV), F32)
        for k in range(CONV_KERNEL):
            acc = acc + cw_ref[k:k + 1, :] * stg[pl.ds(hb - (CONV_KERNEL - 1) + k, ts), :]
        hc = acc + cb_ref[...]
        ln = _layer_norm_rows(hc, g_ref[...], b_ref[...])
        y_ref[:, D_POOL:] = (ln * _sigmoid(ln)).astype(BF16)

    fix2 = lambda i: (0, 0)
    return pl.pallas_call(
        body,
        out_shape=[jax.ShapeDtypeStruct((S, D_MODEL), BF16), jax.ShapeDtypeStruct((S, D_POOL), BF16)],
        grid=(S // ts,),
        in_specs=[pl.BlockSpec((ts, 3 * D_POOL), lambda i: (i, 0)),
                  pl.BlockSpec((hb, 3 * D_POOL), lambda i: (jnp.maximum(i * nh - 1, 0), 0)),
                  pl.BlockSpec((4, POOL_GROUP, POOL_GROUP), lambda i: (0, 0, 0)),
                  pl.BlockSpec((1, D_POOL), fix2), pl.BlockSpec((CONV_KERNEL, D_CONV), fix2),
                  pl.BlockSpec((1, D_CONV), fix2), pl.BlockSpec((1, D_CONV), fix2), pl.BlockSpec((1, D_CONV), fix2)],
        out_specs=[pl.BlockSpec((ts, D_MODEL), lambda i: (i, 0)), pl.BlockSpec((ts, D_POOL), lambda i: (i, 0))],
        scratch_shapes=[pltpu.VMEM((hb + ts, D_POOL), F32), pltpu.VMEM((hb + ts, D_CONV), F32)],
        compiler_params=_cparams(("parallel",)),
        name="mixer_fwd",
    )(u, u, pool_w, pool_scale.reshape(1, D_POOL), conv_w, conv_b.reshape(1, D_CONV), cln_g.reshape(1, D_CONV),
      cln_b.reshape(1, D_CONV))


def _mixer_bwd(u, d, dycat, pool_w, pool_scale, conv_w, conv_b, cln_g, cln_b, *, ts=256):
    S = u.shape[0]
    hb = CONV_HALO
    nh = ts // hb
    n = S // ts
    te = ts + hb
    K = CONV_KERNEL

    def body(u_ref, up_ref, un_ref, d_ref, dy_ref, dyn_ref, pw_ref, ps_ref, cw_ref, cb_ref, g_ref, b_ref,
             du_ref, dpw_ref, dps_ref, dcw_ref, dcb_ref, dg_ref, db_ref, stg, std, sth):
        i = pl.program_id(0)
        first = i == 0
        last = i == n - 1

        @pl.when(first)
        def _():
            dpw_ref[...] = jnp.zeros_like(dpw_ref)
            dps_ref[...] = jnp.zeros_like(dps_ref)
            dcw_ref[...] = jnp.zeros_like(dcw_ref)
            dcb_ref[...] = jnp.zeros_like(dcb_ref)
            dg_ref[...] = jnp.zeros_like(dg_ref)
            db_ref[...] = jnp.zeros_like(db_ref)

        pos_e = (i * ts + lax.broadcasted_iota(jnp.int32, (te, 1), 0) + 1).astype(F32)
        dya = dy_ref[:, 0:D_POOL]
        dya_n = jnp.where(last, 0.0, dyn_ref[:, 0:D_POOL])
        for g, w in enumerate(POOL_WINDOWS):
            lanes = pl.ds(g * POOL_GROUP, POOL_GROUP)
            sl = slice(g * POOL_GROUP, (g + 1) * POOL_GROUP)
            pw = pw_ref[g]
            scale = ps_ref[:, lanes]
            d_g = d_ref[:, lanes]
            pre = _bdot(d_g, pw)
            dps_ref[:, lanes] += jnp.sum(dya[:, sl] * pre, axis=0, keepdims=True)
            dys = dya[:, sl] * scale
            dpw_ref[g] += _bdot(d_g, dys, TN)
            dys_e = jnp.concatenate([dys, dya_n[:, sl] * scale], axis=0)
            dd = _bdot(dys_e, pw, NT)
            std[:, lanes] = dd / jnp.minimum(pos_e, float(w))
            da = -dd[0:ts]
            for m in range(w):
                da = da + std[pl.ds(m, ts), lanes]
            du_ref[:, lanes] = da.astype(BF16)

        glu_p = up_ref[:, D_POOL:D_POOL + D_CONV] * _sigmoid(up_ref[:, D_POOL + D_CONV:])
        stg[pl.ds(0, hb), :] = jnp.where(first, 0.0, glu_p)
        bv = u_ref[:, D_POOL:D_POOL + D_CONV]
        sg = _sigmoid(u_ref[:, D_POOL + D_CONV:])
        stg[pl.ds(hb, ts), :] = bv * sg
        glu_n = un_ref[:, D_POOL:D_POOL + D_CONV] * _sigmoid(un_ref[:, D_POOL + D_CONV:])
        stg[pl.ds(hb + ts, hb), :] = jnp.where(last, 0.0, glu_n)

        acc = jnp.zeros((te, D_CONV), F32)
        for k in range(K):
            acc = acc + cw_ref[k:k + 1, :] * stg[pl.ds(hb - (K - 1) + k, te), :]
        hc = acc + cb_ref[...]
        mu = jnp.mean(hc, axis=-1, keepdims=True)
        hcc = hc - mu
        rstd = lax.rsqrt(jnp.mean(hcc * hcc, axis=-1, keepdims=True) + LN_EPS)
        xh = hcc * rstd
        ln = xh * g_ref[...] + b_ref[...]
        sl_ = _sigmoid(ln)
        dyb = jnp.concatenate([dy_ref[:, D_POOL:], jnp.where(last, 0.0, dyn_ref[:, D_POOL:])], axis=0)
        dln = dyb * (sl_ * (1.0 + ln * (1.0 - sl_)))
        dxh = dln * g_ref[...]
        dhc = rstd * (dxh - jnp.mean(dxh, axis=-1, keepdims=True) - xh * jnp.mean(dxh * xh, axis=-1, keepdims=True))
        sth[...] = dhc
        dg_ref[...] += jnp.sum((dln * xh)[0:ts], axis=0, keepdims=True)
        db_ref[...] += jnp.sum(dln[0:ts], axis=0, keepdims=True)
        dhc_t = dhc[0:ts]
        dcb_ref[...] += jnp.sum(dhc_t, axis=0, keepdims=True)
        dglu = jnp.zeros((ts, D_CONV), F32)
        for k in range(K):
            dcw_ref[k:k + 1, :] += jnp.sum(dhc_t * stg[pl.ds(hb - (K - 1) + k, ts), :], axis=0, keepdims=True)
            dglu = dglu + cw_ref[k:k + 1, :] * sth[pl.ds(K - 1 - k, ts), :]
        du_ref[:, D_POOL:D_POOL + D_CONV] = (dglu * sg).astype(BF16)
        du_ref[:, D_POOL + D_CONV:] = (dglu * bv * sg * (1.0 - sg)).astype(BF16)

    fix2 = lambda i: (0, 0)
    prev = lambda i: (jnp.maximum(i * nh - 1, 0), 0)
    nxt = lambda i: (jnp.minimum((i + 1) * nh, S // hb - 1), 0)
    return pl.pallas_call(
        body,
        out_shape=[jax.ShapeDtypeStruct((S, 3 * D_POOL), BF16),
                   jax.ShapeDtypeStruct((4, POOL_GROUP, POOL_GROUP), F32),
                   jax.ShapeDtypeStruct((1, D_POOL), F32),
                   jax.ShapeDtypeStruct((K, D_CONV), F32),
                   jax.ShapeDtypeStruct((1, D_CONV), F32),
                   jax.ShapeDtypeStruct((1, D_CONV), F32),
                   jax.ShapeDtypeStruct((1, D_CONV), F32)],
        grid=(n,),
        in_specs=[pl.BlockSpec((ts, 3 * D_POOL), lambda i: (i, 0)),
                  pl.BlockSpec((hb, 3 * D_POOL), prev),
                  pl.BlockSpec((hb, 3 * D_POOL), nxt),
                  pl.BlockSpec((ts, D_POOL), lambda i: (i, 0)),
                  pl.BlockSpec((ts, D_MODEL), lambda i: (i, 0)),
                  pl.BlockSpec((hb, D_MODEL), nxt),
                  pl.BlockSpec((4, POOL_GROUP, POOL_GROUP), lambda i: (0, 0, 0)),
                  pl.BlockSpec((1, D_POOL), fix2), pl.BlockSpec((K, D_CONV), fix2),
                  pl.BlockSpec((1, D_CONV), fix2), pl.BlockSpec((1, D_CONV), fix2), pl.BlockSpec((1, D_CONV), fix2)],
        out_specs=[pl.BlockSpec((ts, 3 * D_POOL), lambda i: (i, 0)),
                   pl.BlockSpec((4, POOL_GROUP, POOL_GROUP), lambda i: (0, 0, 0)),
                   pl.BlockSpec((1, D_POOL), fix2), pl.BlockSpec((K, D_CONV), fix2),
                   pl.BlockSpec((1, D_CONV), fix2), pl.BlockSpec((1, D_CONV), fix2), pl.BlockSpec((1, D_CONV), fix2)],
        scratch_shapes=[pltpu.VMEM((hb + ts + hb, D_CONV), F32), pltpu.VMEM((te, D_POOL), F32),
                        pltpu.VMEM((te, D_CONV), F32)],
        compiler_params=_cparams(("arbitrary",)),
        name="mixer_bwd",
    )(u, u, u, d, dycat, dycat, pool_w, pool_scale.reshape(1, D_POOL), conv_w, conv_b.reshape(1, D_CONV),
      cln_g.reshape(1, D_CONV), cln_b.reshape(1, D_CONV))


_GELU_C = math.sqrt(2.0 / math.pi)


def _gelu_parts(x):
    inner = _GELU_C * (x + 0.044715 * x * x * x)
    th = jnp.tanh(inner)
    ge = 0.5 * x * (1.0 + th)
    dge = 0.5 * (1.0 + th) + 0.5 * x * (1.0 - th * th) * (_GELU_C * (1.0 + 3.0 * 0.044715 * x * x))
    return ge, dge


def _ffn_act_fwd(gate, val, dw_w, dw_b, *, ts=256, tc=1408, name):
    S, F = gate.shape
    hb = FFN_HALO
    nh = ts // hb
    tc = _tile(F, tc)

    def body(g_ref, gh_ref, v_ref, w_ref, b_ref, h_ref, st):
        i = pl.program_id(0)
        st[pl.ds(0, hb), :] = jnp.where(i == 0, 0.0, gh_ref[...])
        st[pl.ds(hb, ts), :] = g_ref[...]
        gc = b_ref[...] + w_ref[0:1, :] * st[pl.ds(hb - 2, ts), :] + w_ref[1:2, :] * st[pl.ds(hb - 1, ts), :] \
            + w_ref[2:3, :] * st[pl.ds(hb, ts), :]
        ge, _ = _gelu_parts(gc)
        h_ref[...] = (ge * v_ref[...]).astype(BF16)

    return pl.pallas_call(
        body,
        out_shape=jax.ShapeDtypeStruct((S, F), BF16),
        grid=(S // ts, F // tc),
        in_specs=[pl.BlockSpec((ts, tc), lambda i, j: (i, j)),
                  pl.BlockSpec((hb, tc), lambda i, j: (jnp.maximum(i * nh - 1, 0), j)),
                  pl.BlockSpec((ts, tc), lambda i, j: (i, j)),
                  pl.BlockSpec((3, tc), lambda i, j: (0, j)),
                  pl.BlockSpec((1, tc), lambda i, j: (0, j))],
        out_specs=pl.BlockSpec((ts, tc), lambda i, j: (i, j)),
        scratch_shapes=[pltpu.VMEM((hb + ts, tc), F32)],
        compiler_params=_cparams(("parallel", "parallel")),
        name=name,
    )(gate, gate, val, dw_w, dw_b.reshape(1, F))


def _ffn_act_bwd(gate, val, dh, dw_w, dw_b, *, ts=256, tc=1408, name):
    S, F = gate.shape
    hb = FFN_HALO
    nh = ts // hb
    n = S // ts
    te = ts + hb
    tc = _tile(F, tc)

    def body(g_ref, gp_ref, gn_ref, v_ref, vn_ref, dh_ref, dhn_ref, w_ref, b_ref,
             dg_ref, dv_ref, dw_ref, db_ref, st, sd):
        i = pl.program_id(1)
        first = i == 0
        last = i == n - 1

        @pl.when(first)
        def _():
            dw_ref[...] = jnp.zeros_like(dw_ref)
            db_ref[...] = jnp.zeros_like(db_ref)

        st[pl.ds(0, hb), :] = jnp.where(first, 0.0, gp_ref[...])
        st[pl.ds(hb, ts), :] = g_ref[...]
        st[pl.ds(hb + ts, hb), :] = jnp.where(last, 0.0, gn_ref[...])
        gc = b_ref[...] + w_ref[0:1, :] * st[pl.ds(hb - 2, te), :] + w_ref[1:2, :] * st[pl.ds(hb - 1, te), :] \
            + w_ref[2:3, :] * st[pl.ds(hb, te), :]
        ge, dge = _gelu_parts(gc)
        val_e = jnp.concatenate([v_ref[...], jnp.where(last, 0.0, vn_ref[...])], axis=0)
        dh_e = jnp.concatenate([dh_ref[...], jnp.where(last, 0.0, dhn_ref[...])], axis=0)
        dgc = dh_e * val_e * dge
        sd[...] = dgc
        dv_ref[...] = (dh_e[0:ts] * ge[0:ts]).astype(BF16)
        dgc_t = dgc[0:ts]
        db_ref[...] += jnp.sum(dgc_t, axis=0, keepdims=True)
        dgate = jnp.zeros((ts, tc), F32)
        for k in range(3):
            dw_ref[k:k + 1, :] += jnp.sum(dgc_t * st[pl.ds(hb - 2 + k, ts), :], axis=0, keepdims=True)
            dgate = dgate + w_ref[k:k + 1, :] * sd[pl.ds(2 - k, ts), :]
        dg_ref[...] = dgate.astype(BF16)

    cur = lambda j, i: (i, j)
    prev = lambda j, i: (jnp.maximum(i * nh - 1, 0), j)
    nxt = lambda j, i: (jnp.minimum((i + 1) * nh, S // hb - 1), j)
    return pl.pallas_call(
        body,
        out_shape=[jax.ShapeDtypeStruct((S, F), BF16), jax.ShapeDtypeStruct((S, F), BF16),
                   jax.ShapeDtypeStruct((3, F), F32), jax.ShapeDtypeStruct((1, F), F32)],
        grid=(F // tc, n),
        in_specs=[pl.BlockSpec((ts, tc), cur), pl.BlockSpec((hb, tc), prev), pl.BlockSpec((hb, tc), nxt),
                  pl.BlockSpec((ts, tc), cur), pl.BlockSpec((hb, tc), nxt),
                  pl.BlockSpec((ts, tc), cur), pl.BlockSpec((hb, tc), nxt),
                  pl.BlockSpec((3, tc), lambda j, i: (0, j)), pl.BlockSpec((1, tc), lambda j, i: (0, j))],
        out_specs=[pl.BlockSpec((ts, tc), cur), pl.BlockSpec((ts, tc), cur),
                   pl.BlockSpec((3, tc), lambda j, i: (0, j)), pl.BlockSpec((1, tc), lambda j, i: (0, j))],
        scratch_shapes=[pltpu.VMEM((hb + ts + hb, tc), F32), pltpu.VMEM((te, tc), F32)],
        compiler_params=_cparams(("parallel", "arbitrary")),
        name=name,
    )(gate, gate, gate, val, val, dh, dh, dw_w, dw_b.reshape(1, F))


def _ln_bwd(z, ln_g, ln_b, dout, *, loss_head=False, ts=256, name):
    S, D = z.shape

    def body(z_ref, g_ref, b_ref, do_ref, dz_ref, dg_ref, db_ref, loss_ref):
        i = pl.program_id(0)

        @pl.when(i == 0)
        def _():
            dg_ref[...] = jnp.zeros_like(dg_ref)
            db_ref[...] = jnp.zeros_like(db_ref)
            loss_ref[...] = jnp.zeros_like(loss_ref)

        zt = z_ref[...]
        mu = jnp.mean(zt, axis=-1, keepdims=True)
        zc = zt - mu
        rstd = lax.rsqrt(jnp.mean(zc * zc, axis=-1, keepdims=True) + LN_EPS)
        xh = zc * rstd
        if loss_head:
            err = xh * g_ref[...] + b_ref[...] - do_ref[...]
            loss_ref[...] += 0.5 * jnp.sum(jnp.mean(err * err, axis=-1, keepdims=True))
            do = err * (1.0 / D)
        else:
            do = do_ref[...]
        dg_ref[...] += jnp.sum(do * xh, axis=0, keepdims=True)
        db_ref[...] += jnp.sum(do, axis=0, keepdims=True)
        dxh = do * g_ref[...]
        dz_ref[...] = rstd * (dxh - jnp.mean(dxh, axis=-1, keepdims=True)
                              - xh * jnp.mean(dxh * xh, axis=-1, keepdims=True))

    row = lambda i: (i, 0)
    fix = lambda i: (0, 0)
    return pl.pallas_call(
        body,
        out_shape=[jax.ShapeDtypeStruct((S, D), F32), jax.ShapeDtypeStruct((1, D), F32),
                   jax.ShapeDtypeStruct((1, D), F32), jax.ShapeDtypeStruct((8, 128), F32)],
        grid=(S // ts,),
        in_specs=[pl.BlockSpec((ts, D), row), pl.BlockSpec((1, D), fix), pl.BlockSpec((1, D), fix),
                  pl.BlockSpec((ts, D), row)],
        out_specs=[pl.BlockSpec((ts, D), row), pl.BlockSpec((1, D), fix), pl.BlockSpec((1, D), fix),
                   pl.BlockSpec((8, 128), fix)],
        compiler_params=_cparams(("arbitrary",)),
        name=name,
    )(z, ln_g.reshape(1, D), ln_b.reshape(1, D), dout)


def _ple_bwd(dz, gate, proj, *, ts=256, name):
    S, D = dz.shape

    def body(dz_ref, g_ref, p_ref, ds_ref, dp_ref, db_ref):
        @pl.when(pl.program_id(0) == 0)
        def _():
            db_ref[...] = jnp.zeros_like(db_ref)

        dzt = dz_ref[...]
        g = g_ref[...]
        ds = dzt * p_ref[...] * g * (1.0 - g)
        ds_ref[...] = ds.astype(BF16)
        dp_ref[...] = (dzt * g).astype(BF16)
        db_ref[...] += jnp.sum(ds, axis=0, keepdims=True)

    row = lambda i: (i, 0)
    return pl.pallas_call(
        body,
        out_shape=[jax.ShapeDtypeStruct((S, D), BF16), jax.ShapeDtypeStruct((S, D), BF16),
                   jax.ShapeDtypeStruct((1, D), F32)],
        grid=(S // ts,),
        in_specs=[pl.BlockSpec((ts, D), row)] * 3,
        out_specs=[pl.BlockSpec((ts, D), row), pl.BlockSpec((ts, D), row), pl.BlockSpec((1, D), lambda i: (0, 0))],
        compiler_params=_cparams(("arbitrary",)),
        name=name,
    )(dz, gate, proj)


def _attn_scores(q, kc, bias, qb):
    s = _bdot(q, kc, NT) * (HEAD_DIM ** -0.5) + bias
    kpos = qb * Q_BLOCK + lax.broadcasted_iota(jnp.int32, (1, KV_SPAN), 1)
    s = jnp.where(kpos >= KV_PAD, s, NEG_INF)
    m = jnp.max(s, axis=-1, keepdims=True)
    e = jnp.exp(s - m)
    return e / jnp.sum(e, axis=-1, keepdims=True)


def _attn_fwd(q, kp, vp, bias):
    H, S, dh = q.shape
    sp = kp.shape[1]

    def body(q_ref, k_ref, v_ref, b_ref, o_ref):
        qb = pl.program_id(1)
        start = pl.multiple_of(qb * Q_BLOCK, Q_BLOCK)
        kc = k_ref[pl.ds(start, KV_SPAN), :]
        vc = v_ref[pl.ds(start, KV_SPAN), :]
        p = _attn_scores(q_ref[...], kc, b_ref[...], qb)
        o_ref[...] = _bdot(p, vc).astype(BF16)

    return pl.pallas_call(
        body,
        out_shape=jax.ShapeDtypeStruct((H, S, dh), BF16),
        grid=(H, S // Q_BLOCK),
        in_specs=[pl.BlockSpec((None, Q_BLOCK, dh), lambda h, i: (h, i, 0)),
                  pl.BlockSpec((None, sp, dh), lambda h, i: (h, 0, 0)),
                  pl.BlockSpec((None, sp, dh), lambda h, i: (h, 0, 0)),
                  pl.BlockSpec((None, Q_BLOCK, KV_SPAN), lambda h, i: (h, 0, 0))],
        out_specs=pl.BlockSpec((None, Q_BLOCK, dh), lambda h, i: (h, i, 0)),
        compiler_params=_cparams(("parallel", "arbitrary")),
        name="attn_fwd",
    )(q, kp, vp, bias)


def _attn_bwd(q, kp, vp, bias, do):
    H, S, dh = q.shape
    sp = kp.shape[1]
    scale = HEAD_DIM ** -0.5

    def body(q_ref, k_ref, v_ref, b_ref, do_ref, dq_ref, dk_ref, dv_ref, db_ref):
        qb = pl.program_id(1)

        @pl.when(qb == 0)
        def _():
            dk_ref[...] = jnp.zeros_like(dk_ref)
            dv_ref[...] = jnp.zeros_like(dv_ref)
            db_ref[...] = jnp.zeros_like(db_ref)

        start = pl.multiple_of(qb * Q_BLOCK, Q_BLOCK)
        span = pl.ds(start, KV_SPAN)
        qt = q_ref[...]
        kc = k_ref[span, :]
        vc = v_ref[span, :]
        dot = do_ref[...]
        p = _attn_scores(qt, kc, b_ref[...], qb)
        dv_ref[span, :] += _bdot(p, dot, TN)
        dp = _bdot(dot, vc, NT)
        ds = p * (dp - jnp.sum(p * dp, axis=-1, keepdims=True))
        db_ref[...] += ds
        dq_ref[...] = (scale * _bdot(ds, kc)).astype(BF16)
        dk_ref[span, :] += scale * _bdot(ds, qt, TN)

    blk = pl.BlockSpec((None, Q_BLOCK, dh), lambda h, i: (h, i, 0))
    full = pl.BlockSpec((None, sp, dh), lambda h, i: (h, 0, 0))
    bsp = pl.BlockSpec((None, Q_BLOCK, KV_SPAN), lambda h, i: (h, 0, 0))
    return pl.pallas_call(
        body,
        out_shape=[jax.ShapeDtypeStruct((H, S, dh), BF16), jax.ShapeDtypeStruct((H, sp, dh), F32),
                   jax.ShapeDtypeStruct((H, sp, dh), F32), jax.ShapeDtypeStruct((H, Q_BLOCK, KV_SPAN), F32)],
        grid=(H, S // Q_BLOCK),
        in_specs=[blk, full, full, bsp, blk],
        out_specs=[blk, full, full, bsp],
        compiler_params=_cparams(("parallel", "arbitrary")),
        name="attn_bwd",
    )(q, kp, vp, bias, do)


def _bias_blocks(rel_bias):
    H = rel_bias.shape[0]
    n_e = BAND + CHUNK - 1
    n_clip = KV_PAD + CHUNK - 1 - MAX_REL + 1
    e = jnp.concatenate([jnp.broadcast_to(rel_bias[:, 2 * MAX_REL:], (H, n_clip)),
                         jnp.flip(rel_bias[:, 2 * MAX_REL - (n_e - n_clip):2 * MAX_REL], axis=1)], axis=1)
    skew = jnp.pad(jnp.tile(e, (1, CHUNK)), ((0, 0), (0, CHUNK))).reshape(H, CHUNK, n_e + 1)
    band = jnp.flip(skew, axis=1)[:, :, :BAND]
    rows = [jnp.pad(band, ((0, 0), (0, 0), (c * CHUNK, KV_SPAN - BAND - c * CHUNK)), constant_values=NEG_INF)
            for c in range(Q_BLOCK // CHUNK)]
    return jnp.concatenate(rows, axis=1)


def _bias_blocks_grad(dblk):
    H = dblk.shape[0]
    n_e = BAND + CHUNK - 1
    n_clip = KV_PAD + CHUNK - 1 - MAX_REL + 1
    parts = jnp.stack([dblk[:, c * CHUNK:(c + 1) * CHUNK, c * CHUNK:c * CHUNK + BAND]
                       for c in range(Q_BLOCK // CHUNK)], axis=1)
    parts = jnp.flip(parts, axis=2)
    parts = jnp.pad(parts, ((0, 0), (0, 0), (0, 0), (0, n_e + 1 - BAND)))
    skew = parts.reshape(H, Q_BLOCK // CHUNK, CHUNK * (n_e + 1))[:, :, :CHUNK * n_e]
    skew = skew.reshape(H, Q_BLOCK, n_e)
    skew = jnp.pad(skew, ((0, 0), (0, 0), (0, 1)))

    def body(s_ref, o_ref):
        de = jnp.sum(s_ref[...], axis=0, keepdims=True)
        lane = lax.broadcasted_iota(jnp.int32, de.shape, 1)
        far = jnp.sum(jnp.where(lane < n_clip, de, 0.0), axis=-1, keepdims=True)
        o_ref[...] = jnp.where(lane == 0, far, jnp.where(lane < n_clip, 0.0, de))

    de = pl.pallas_call(
        body,
        out_shape=jax.ShapeDtypeStruct((H, 1, n_e + 1), F32),
        grid=(H,),
        in_specs=[pl.BlockSpec((None, Q_BLOCK, n_e + 1), lambda h: (h, 0, 0))],
        out_specs=pl.BlockSpec((None, 1, n_e + 1), lambda h: (h, 0, 0)),
        compiler_params=_cparams(("parallel",)),
        name="bias_grad_sum",
    )(skew).reshape(H, n_e + 1)
    near = jnp.flip(de[:, n_clip:n_e], axis=1)
    return jnp.concatenate([jnp.zeros((H, 2 * MAX_REL - (n_e - n_clip)), F32), near, de[:, 0:1]], axis=1)


def _ffn_forward(r1, p_l, w, l, ready):
    ready(f"up{l}", r1)
    up_g = _mm(r1, w["ffn_up_g"][l], tn=1408, name=f"ffn_up_g{l}")
    up_v = _mm(r1, w["ffn_up_v"][l], tn=1408, name=f"ffn_up_v{l}")
    h = _ffn_act_fwd(up_g, up_v, w["ffn_dw_w"][l], w["ffn_dw_b"][l], name=f"ffn_act{l}")
    ready(f"dn{l}", h)
    z2, r2, gate, proj = _proj_ln(r1, h, w["ffn_w_down"][l], w["ln_ffn_g"][l], w["ln_ffn_b"][l],
                                  ple=(w["ple_w_gate"][l], w["ple_b_gate"][l], p_l, w["ple_w_proj"][l]),
                                  name=f"ffn_down_ln{l}")
    return dict(r1=r1, up_g=up_g, up_v=up_v, h=h, z2=z2, gate=gate, proj=proj), r2


def _ffn_backward(sv, dz2, p_l, w, l, grads):
    r1 = sv["r1"]
    ds, dproj, db_gate = _ple_bwd(dz2, sv["gate"], sv["proj"], name=f"ple_bwd{l}")
    dh = _mm(dz2, w["ffn_w_down"][l], tb=True, tn=1408, name=f"ffn_dh{l}")
    dgate, dval, d_dw_w, d_dw_b = _ffn_act_bwd(sv["up_g"], sv["up_v"], dh, w["ffn_dw_w"][l], w["ffn_dw_b"][l],
                                               name=f"ffn_act_bwd{l}")
    grads["ffn_w_down"][l] = _mm(sv["h"], dz2, ta=True, tm=1408, tn=1024, tk=512, name=f"d_ffn_w_down{l}")
    grads["ffn_up_g"][l] = _mm(r1, dgate, ta=True, tm=1024, tn=1408, tk=512, name=f"d_ffn_up_g{l}")
    grads["ffn_up_v"][l] = _mm(r1, dval, ta=True, tm=1024, tn=1408, tk=512, name=f"d_ffn_up_v{l}")
    grads["ple_w_gate"][l] = _mm(r1, ds, ta=True, tm=1024, tn=1024, tk=512, name=f"d_ple_w_gate{l}")
    grads["ple_w_proj"][l] = _mm(p_l, dproj, ta=True, tm=256, tn=1024, tk=512, name=f"d_ple_w_proj{l}")
    grads["ffn_dw_w"][l] = d_dw_w
    grads["ffn_dw_b"][l] = d_dw_b[0]
    grads["ple_b_gate"][l] = db_gate[0]
    t = _mm(ds, w["ple_w_gate"][l], tb=True, add=dz2, add_scale=ALPHA, tn=1024, name=f"dr1_gate{l}")
    t = _mm(dgate, w["ffn_up_g"][l], tb=True, add=t, tn=1024, tk=1408, name=f"dr1_up_g{l}")
    return _mm(dval, w["ffn_up_v"][l], tb=True, add=t, tn=1024, tk=1408, name=f"dr1_up_v{l}")


def _to_heads(a, n):
    S = a.shape[0]
    t = a.reshape(S, n, N_HEADS, HEAD_DIM).transpose(1, 2, 0, 3)
    return [t[i] for i in range(n)]


def _local_step(x, p, target, w, ready=lambda group, after: None, emit=lambda group, grads: None):
    S = x.shape[0]
    grads = {k: [None, None] for k in ("ffn_w_down", "ffn_up_g", "ffn_up_v", "ple_w_gate", "ple_w_proj", "ffn_dw_w",
                                       "ffn_dw_b", "ple_b_gate", "ln_ffn_g", "ln_ffn_b", "ln_mix_g", "ln_mix_b")}

    ready("mix", None)
    u = _mm(x, w["mix_w_in"], name="mix_in")
    ycat, dpool = _mixer_fwd(u, w["pool_w"], w["pool_scale"], w["conv_dw_w"], w["conv_dw_b"], w["conv_ln_g"],
                             w["conv_ln_b"])
    z1, r1 = _proj_ln(x, ycat, w["mix_w_out"], w["ln_mix_g"][0], w["ln_mix_b"][0], name="mix_out_ln")
    sv0, r2 = _ffn_forward(r1, p[0], w, 0, ready)

    ready("attn", r2)
    qkv = _mm(r2, w["attn_w_qkv"], out_dtype=BF16, name="attn_qkv")
    q, k, v = _to_heads(qkv, 3)
    kp = jnp.pad(k, ((0, 0), (KV_PAD, 0), (0, 0)))
    vp = jnp.pad(v, ((0, 0), (KV_PAD, 0), (0, 0)))
    bias = _bias_blocks(w["attn_rel_bias"])
    attn_h = _attn_fwd(q, kp, vp, bias)
    attn = attn_h.transpose(1, 0, 2).reshape(S, D_MODEL)
    z3, r3 = _proj_ln(r2, attn, w["attn_w_o"], w["ln_mix_g"][1], w["ln_mix_b"][1], name="attn_out_ln")
    sv1, _ = _ffn_forward(r3, p[1], w, 1, ready)

    dz4, grads["ln_ffn_g"][1], grads["ln_ffn_b"][1], loss = _ln_bwd(sv1["z2"], w["ln_ffn_g"][1], w["ln_ffn_b"][1],
                                                                    target, loss_head=True, name="loss_ln_bwd")
    dr3 = _ffn_backward(sv1, dz4, p[1], w, 1, grads)
    emit("ffn1", grads)
    dz3, grads["ln_mix_g"][1], grads["ln_mix_b"][1], _ = _ln_bwd(z3, w["ln_mix_g"][1], w["ln_mix_b"][1], dr3,
                                                                name="ln_mix_bwd1")
    grads["attn_w_o"] = _mm(attn, dz3, ta=True, tm=1024, tn=1024, tk=512, name="d_attn_w_o")
    dattn = _mm(dz3, w["attn_w_o"], tb=True, out_dtype=BF16, tn=1024, name="d_attn")
    (dattn_h,) = _to_heads(dattn, 1)
    dq, dk, dv, dbias = _attn_bwd(q, kp, vp, bias, dattn_h)
    grads["attn_rel_bias"] = _bias_blocks_grad(dbias)
    dqkv = jnp.stack([dq, dk[:, KV_PAD:].astype(BF16), dv[:, KV_PAD:].astype(BF16)], axis=0)
    dqkv = dqkv.transpose(2, 0, 1, 3).reshape(S, 3 * D_MODEL)
    grads["attn_w_qkv"] = _mm(r2, dqkv, ta=True, tm=1024, tn=1024, tk=512, name="d_attn_w_qkv")
    emit("attn", grads)
    dr2 = _mm(dqkv, w["attn_w_qkv"], tb=True, add=dz3, add_scale=ALPHA, tn=1024, name="dr2")

    dz2, grads["ln_ffn_g"][0], grads["ln_ffn_b"][0], _ = _ln_bwd(sv0["z2"], w["ln_ffn_g"][0], w["ln_ffn_b"][0], dr2,
                                                                name="ln_ffn_bwd0")
    dr1 = _ffn_backward(sv0, dz2, p[0], w, 0, grads)
    emit("ffn0", grads)
    dz1, grads["ln_mix_g"][0], grads["ln_mix_b"][0], _ = _ln_bwd(z1, w["ln_mix_g"][0], w["ln_mix_b"][0], dr1,
                                                                name="ln_mix_bwd0")
    grads["mix_w_out"] = _mm(ycat, dz1, ta=True, tm=1024, tn=1024, tk=512, name="d_mix_w_out")
    dycat = _mm(dz1, w["mix_w_out"], tb=True, tn=1024, name="d_ycat")
    du, g_pw, g_ps, g_cw, g_cb, g_cg, g_cbb = _mixer_bwd(u, dpool, dycat, w["pool_w"], w["pool_scale"],
                                                         w["conv_dw_w"], w["conv_dw_b"], w["conv_ln_g"],
                                                         w["conv_ln_b"])
    grads["mix_w_in"] = _mm(x, du, ta=True, tm=1024, tn=512, tk=512, name="d_mix_w_in")
    grads["conv_dw_w"] = g_cw
    emit("mix", grads)
    grad_x = _mm(du, w["mix_w_in"], tb=True, add=dz1, add_scale=ALPHA, tn=1024, tk=512, name="grad_x")
    grads.update(pool_w=g_pw, pool_scale=g_ps[0], conv_dw_w=g_cw, conv_dw_b=g_cb[0], conv_ln_g=g_cg[0],
                 conv_ln_b=g_cbb[0])
    for kname in ("ln_ffn_g", "ln_ffn_b", "ln_mix_g", "ln_mix_b"):
        grads[kname] = [a[0] for a in grads[kname]]
    return loss[0, 0], grad_x, grads


def _exchange(bufs, places, *, name):
    nb = len(bufs)

    def body(*refs):
        srcs, dsts = refs[:nb], refs[nb:2 * nb]
        send_sems, recv_sems, local_sems = refs[2 * nb:]
        x, y, c = lax.axis_index("x"), lax.axis_index("y"), lax.axis_index("c")
        me = 4 * x + 2 * y + c
        local = []
        remote = []
        for b in range(nb):
            pieces = places[b] == "pieces"
            shape = bufs[b].shape
            cp = pltpu.make_async_copy(srcs[b].at[me] if pieces else srcs[b], _slot(dsts[b], places[b], shape, me),
                                       local_sems.at[b])
            cp.start()
            local.append(cp)
            for d, dev, peer in _peers(x, y, c):
                src = srcs[b].at[peer] if pieces else srcs[b]
                out = pltpu.make_async_remote_copy(
                    src_ref=src, dst_ref=_slot(dsts[b], places[b], shape, me),
                    send_sem=send_sems.at[b * N_DEV + d], recv_sem=recv_sems.at[b * N_DEV + d],
                    device_id=dev, device_id_type=pl.DeviceIdType.MESH)
                out.start()
                inc = pltpu.make_async_remote_copy(
                    src_ref=src, dst_ref=_slot(dsts[b], places[b], shape, peer),
                    send_sem=send_sems.at[b * N_DEV + d], recv_sem=recv_sems.at[b * N_DEV + d],
                    device_id=dev, device_id_type=pl.DeviceIdType.MESH)
                remote.append((out, inc))
        for cp in local:
            cp.wait()
        for out, inc in remote:
            out.wait_send()
            inc.wait_recv()

    out_shapes = [jax.ShapeDtypeStruct(_result_shape(b, place), b.dtype) for b, place in zip(bufs, places)]
    any_spec = pl.BlockSpec(memory_space=pl.ANY)
    return pl.pallas_call(
        body,
        out_shape=out_shapes,
        in_specs=[any_spec] * nb,
        out_specs=[any_spec] * nb,
        scratch_shapes=[pltpu.SemaphoreType.DMA((nb * N_DEV,)), pltpu.SemaphoreType.DMA((nb * N_DEV,)),
                        pltpu.SemaphoreType.DMA((nb,))],
        compiler_params=pltpu.CompilerParams(has_side_effects=True),
        name=name,
    )(*bufs)


_HBM = pl.BlockSpec(memory_space=pltpu.HBM)
_SEM = pl.BlockSpec(memory_space=pltpu.SEMAPHORE)
_EFFECT = pltpu.SideEffectType.DATAFLOW_SIDE_EFFECTING


def _slot(ref, place, shape, k):
    if place in ("stack", "pieces"):
        return ref.at[k]
    ax = place[1]
    n = shape[ax]
    return ref.at[(slice(None),) * ax + (pl.ds(pl.multiple_of(k * n, n), n),)]


def _result_shape(buf, place):
    if place == "stack":
        return (N_DEV,) + buf.shape
    if place == "pieces":
        return buf.shape
    return tuple(s * N_DEV if i == place[1] else s for i, s in enumerate(buf.shape))


def _peers(x, y, c):
    for d in range(1, N_DEV):
        px, py, pc = x ^ ((d >> 2) & 1), y ^ ((d >> 1) & 1), c ^ (d & 1)
        yield d, (px, py, pc), 4 * px + 2 * py + pc


def _exchange_start(bufs, places, after, *, name):
    nb = len(bufs)
    lands = [lax.empty(_result_shape(b, p_), b.dtype) for b, p_ in zip(bufs, places)]
    has_after = after is not None

    def body(*refs):
        srcs, dsts = refs[:nb], refs[nb:2 * nb]
        outs = refs[2 * nb + has_after:]
        send_sems, recv_sems, token = outs[0], outs[1], outs[2 + 2 * nb]
        x, y, c = lax.axis_index("x"), lax.axis_index("y"), lax.axis_index("c")
        me = 4 * x + 2 * y + c
        for b in range(nb):
            for d, dev, peer in _peers(x, y, c):
                pltpu.make_async_remote_copy(
                    src_ref=srcs[b].at[peer] if places[b] == "pieces" else srcs[b],
                    dst_ref=_slot(dsts[b], places[b], bufs[b].shape, me),
                    send_sem=send_sems.at[b * N_DEV + d], recv_sem=recv_sems.at[b * N_DEV + d],
                    device_id=dev, device_id_type=pl.DeviceIdType.MESH).start()
        token[...] = jnp.zeros_like(token)

    sems = pltpu.SemaphoreType.DMA((nb * N_DEV,))
    ins = [pltpu.with_memory_space_constraint(a, pltpu.HBM) for a in list(bufs) + lands]
    out = pl.pallas_call(
        body,
        out_shape=(sems, sems, *[pltpu.HBM(a.shape, a.dtype) for a in ins], jax.ShapeDtypeStruct((8, 128), F32)),
        in_specs=[_HBM] * (2 * nb) + ([pl.BlockSpec(memory_space=pl.ANY)] if has_after else []),
        out_specs=(_SEM, _SEM, *[_HBM] * (2 * nb), pl.BlockSpec(memory_space=pltpu.VMEM)),
        input_output_aliases={i: 2 + i for i in range(2 * nb)},
        compiler_params=pltpu.CompilerParams(has_side_effects=_EFFECT),
        name=name,
    )(*ins, *([after] if has_after else []))
    return dict(send=out[0], recv=out[1], srcs=out[2:2 + nb], lands=out[2 + nb:2 + 2 * nb], token=out[-1],
                places=places)


def _exchange_wait(h, after, *, name):
    nb = len(h["srcs"])
    places = h["places"]
    shapes = [a.shape for a in h["srcs"]]

    def body(*refs):
        srcs, dsts, send_sems, recv_sems = refs[:nb], refs[nb:2 * nb], refs[2 * nb], refs[2 * nb + 1]
        local_sems = refs[-1]
        x, y, c = lax.axis_index("x"), lax.axis_index("y"), lax.axis_index("c")
        me = 4 * x + 2 * y + c
        for b in range(nb):
            pieces = places[b] == "pieces"
            for d, dev, peer in _peers(x, y, c):
                cp = pltpu.make_async_remote_copy(
                    src_ref=srcs[b].at[peer] if pieces else srcs[b],
                    dst_ref=_slot(dsts[b], places[b], shapes[b], peer),
                    send_sem=send_sems.at[b * N_DEV + d], recv_sem=recv_sems.at[b * N_DEV + d],
                    device_id=dev, device_id_type=pl.DeviceIdType.MESH)
                cp.wait_send()
                cp.wait_recv()
        own = [pltpu.make_async_copy(srcs[b].at[me] if places[b] == "pieces" else srcs[b],
                                     _slot(dsts[b], places[b], shapes[b], me), local_sems.at[b]) for b in range(nb)]
        for cp in own:
            cp.start()
        for cp in own:
            cp.wait()

    ins = list(h["srcs"]) + list(h["lands"])
    out = pl.pallas_call(
        body,
        out_shape=tuple(pltpu.HBM(a.shape, a.dtype) for a in ins),
        in_specs=[_HBM] * (2 * nb) + [_SEM, _SEM, pl.BlockSpec(memory_space=pl.ANY)],
        out_specs=tuple([_HBM] * (2 * nb)),
        input_output_aliases={i: i for i in range(2 * nb)},
        scratch_shapes=[pltpu.SemaphoreType.DMA((nb,))],
        compiler_params=pltpu.CompilerParams(has_side_effects=_EFFECT),
        name=name,
    )(*ins, h["send"], h["recv"], after)
    return out[nb:]


def _adamw(recv, w, m, v, *, name):
    R, C = w.shape
    tr = R
    for cand in (512, 256, 128, 64, 32, 16):
        if R % cand == 0 and cand * C * 4 <= 2 * 1024 * 1024:
            tr = cand
            break
    c1 = 1.0 - ADAM_B1 ** ADAM_STEP
    c2 = 1.0 - ADAM_B2 ** ADAM_STEP

    def body(r_ref, w_ref, m_ref, v_ref, g_ref, d_ref, mo_ref, vo_ref):
        g = r_ref[0].astype(F32)
        for i in range(1, N_DEV):
            g = g + r_ref[i].astype(F32)
        m_new = ADAM_B1 * m_ref[...] + (1.0 - ADAM_B1) * g
        v_new = ADAM_B2 * v_ref[...] + (1.0 - ADAM_B2) * (g * g)
        m_hat = m_new / c1
        v_hat = v_new / c2
        g_ref[...] = g
        d_ref[...] = -ADAM_LR * (m_hat / (jnp.sqrt(v_hat) + ADAM_EPS) + ADAM_WD * w_ref[...])
        mo_ref[...] = m_new
        vo_ref[...] = v_new

    row = pl.BlockSpec((tr, C), lambda i: (i, 0))
    return pl.pallas_call(
        body,
        out_shape=[jax.ShapeDtypeStruct((R, C), F32)] * 4,
        grid=(R // tr,),
        in_specs=[pl.BlockSpec((N_DEV, tr, C), lambda i: (0, i, 0)), row, row, row],
        out_specs=[row] * 4,
        compiler_params=_cparams(("parallel",)),
        name=name,
    )(recv, w, m, v)


def _ffn_groups(l):
    return ((f"up{l}", (("ffn_w_up", l, BF16, "stack"), ("ffn_dw_w", l, F32, "stack"))),
            (f"dn{l}", (("ffn_w_down", l, BF16, ("axis", 0)), ("ple_w_gate", l, BF16, ("axis", 0)),
                        ("ple_w_proj", l, BF16, ("axis", 1)))))


_GATHER_GROUPS = (
    ("mix", (("mix_w_in", 0, BF16, "stack"), ("conv_dw_w", 0, F32, "stack"), ("mix_w_out", 0, BF16, ("axis", 0)))),
    *_ffn_groups(0),
    ("attn", (("attn_w_qkv", 0, BF16, ("axis", 1)), ("attn_w_o", 0, BF16, ("axis", 0)))),
    *_ffn_groups(1))
_SHARDED = ("mix_w_in", "conv_dw_w", "mix_w_out", "attn_w_qkv", "attn_w_o", "ffn_w_up", "ffn_dw_w", "ffn_w_down",
            "ple_w_gate", "ple_w_proj")
_REPLICATED = ("pool_w", "pool_scale", "conv_dw_b", "conv_ln_g", "conv_ln_b", "attn_rel_bias", "ln_mix_g",
               "ln_mix_b", "ffn_dw_b", "ple_b_gate", "ln_ffn_g", "ln_ffn_b")


def _pack_rows(parts, row_mult, dtype):
    lead = parts[0].shape[:-1]
    flat = jnp.concatenate([a.astype(dtype) for a in parts], axis=-1)
    n = flat.shape[-1]
    unit = row_mult * LANES
    padded = -(-n // unit) * unit
    flat = jnp.pad(flat, [(0, 0)] * len(lead) + [(0, padded - n)])
    return flat.reshape(lead + (padded // LANES, LANES))


def _unpack(flat2d, shapes):
    flat = flat2d.reshape(-1)
    out, o = [], 0
    for s in shapes:
        n = math.prod(s)
        out.append(flat[o:o + n].reshape(s))
        o += n
    return out


def _full_from_shards(g, axis):
    parts = jnp.moveaxis(g, 0, axis)
    shp = list(g.shape[1:])
    shp[axis] *= g.shape[0]
    return parts.reshape(shp)


def _pieces_from_full(full, axis, k=N_DEV):
    shp = list(full.shape)
    n = shp[axis] // k
    t = full.reshape(shp[:axis] + [k, n] + shp[axis + 1:])
    return jnp.moveaxis(t, axis, 0)


def kernel(x, p, mix_w_in, pool_w, pool_scale, conv_dw_w, conv_dw_b, conv_ln_g, conv_ln_b, mix_w_out, attn_w_qkv, attn_rel_bias, attn_w_o, ln_mix_g, ln_mix_b, ffn_w_up, ffn_dw_w, ffn_dw_b, ffn_w_down, ple_w_proj, ple_w_gate, ple_b_gate, ln_ffn_g, ln_ffn_b, loss_target, m_mix_w_in, m_pool_w, m_pool_scale, m_conv_dw_w, m_conv_dw_b, m_conv_ln_g, m_conv_ln_b, m_mix_w_out, m_attn_w_qkv, m_attn_rel_bias, m_attn_w_o, m_ln_mix_g, m_ln_mix_b, m_ffn_w_up, m_ffn_dw_w, m_ffn_dw_b, m_ffn_w_down, m_ple_w_proj, m_ple_w_gate, m_ple_b_gate, m_ln_ffn_g, m_ln_ffn_b, v_mix_w_in, v_pool_w, v_pool_scale, v_conv_dw_w, v_conv_dw_b, v_conv_ln_g, v_conv_ln_b, v_mix_w_out, v_attn_w_qkv, v_attn_rel_bias, v_attn_w_o, v_ln_mix_g, v_ln_mix_b, v_ffn_w_up, v_ffn_dw_w, v_ffn_dw_b, v_ffn_w_down, v_ple_w_proj, v_ple_w_gate, v_ple_b_gate, v_ln_ffn_g, v_ln_ffn_b):
    a = dict(locals())
    sh_names = list(_SHARDED)
    names = sh_names + list(_REPLICATED)
    wts = {n: a[n] for n in names}
    mom = {n: a["m_" + n] for n in names}
    var = {n: a["v_" + n] for n in names}

    gather = {}
    token = None
    for group, items in _GATHER_GROUPS:
        gather[group] = _exchange_start([wts[n][l].astype(dt) for n, l, dt, _ in items], [pl_ for *_, pl_ in items],
                                        token, name="gather_start_" + group)
        token = gather[group]["token"]

    w = dict(pool_w=pool_w[0], pool_scale=pool_scale[0], conv_dw_b=conv_dw_b[0], conv_ln_g=conv_ln_g[0],
             conv_ln_b=conv_ln_b[0], attn_rel_bias=attn_rel_bias[0], ln_mix_g=ln_mix_g, ln_mix_b=ln_mix_b,
             ffn_dw_b=ffn_dw_b, ple_b_gate=ple_b_gate, ln_ffn_g=ln_ffn_g, ln_ffn_b=ln_ffn_b)
    for n in ("ffn_up_g", "ffn_up_v", "ffn_dw_w", "ffn_w_down", "ple_w_gate", "ple_w_proj"):
        w[n] = [None, None]

    def ready(group, after):
        got = _exchange_wait(gather[group], token if after is None else after, name="gather_wait_" + group)
        if group == "mix":
            w["mix_w_in"], w["conv_dw_w"] = _full_from_shards(got[0], 1), _full_from_shards(got[1], 1)
            w["mix_w_out"] = got[2]
        elif group == "attn":
            w["attn_w_qkv"], w["attn_w_o"] = got
        elif group[:2] == "up":
            l = int(group[2])
            w["ffn_up_g"][l] = _full_from_shards(got[0][:N_DEV // 2], 1)
            w["ffn_up_v"][l] = _full_from_shards(got[0][N_DEV // 2:], 1)
            w["ffn_dw_w"][l] = _full_from_shards(got[1], 1)
        else:
            l = int(group[2])
            w["ffn_w_down"][l], w["ple_w_gate"][l], w["ple_w_proj"][l] = got

    scatter = {}

    def emit(group, gr):
        if group[:3] == "ffn":
            l = int(group[3])
            pieces = [jnp.concatenate([_pieces_from_full(gr["ffn_up_g"][l], 1, N_DEV // 2),
                                       _pieces_from_full(gr["ffn_up_v"][l], 1, N_DEV // 2)]),
                      _pieces_from_full(gr["ffn_dw_w"][l], 1), _pieces_from_full(gr["ffn_w_down"][l], 0),
                      _pieces_from_full(gr["ple_w_gate"][l], 0), _pieces_from_full(gr["ple_w_proj"][l], 1)]
        elif group == "attn":
            pieces = [_pieces_from_full(gr["attn_w_qkv"], 1), _pieces_from_full(gr["attn_w_o"], 0)]
        else:
            pieces = [_pieces_from_full(gr["mix_w_in"], 1), _pieces_from_full(gr["conv_dw_w"], 1),
                      _pieces_from_full(gr["mix_w_out"], 0)]
        scatter[group] = _exchange_start([a.astype(BF16) for a in pieces], ["pieces"] * len(pieces), None,
                                         name="grad_start_" + group)

    loss_part, grad_x, gr = _local_step(x[0], p[:, 0], loss_target[0], w, ready, emit)
    loss = lax.psum(loss_part, ("x", "y", "c"))

    recv = {}
    after = grad_x
    for group in ("ffn1", "attn", "ffn0", "mix"):
        recv[group] = _exchange_wait(scatter[group], after, name="grad_wait_" + group)
        after = recv[group][0]
    got = {"mix_w_in": [recv["mix"][0]], "conv_dw_w": [recv["mix"][1]], "mix_w_out": [recv["mix"][2]],
           "attn_w_qkv": [recv["attn"][0]], "attn_w_o": [recv["attn"][1]]}
    for i, n in enumerate(("ffn_w_up", "ffn_dw_w", "ffn_w_down", "ple_w_gate", "ple_w_proj")):
        got[n] = [recv["ffn0"][i], recv["ffn1"][i]]

    res = [{}, {}, {}, {}]
    for n in sh_names:
        outs_l = [_adamw(r, wts[n][l], mom[n][l], var[n][l], name=f"adamw_{n}{l}") for l, r in enumerate(got[n])]
        for k in range(4):
            res[k][n] = jnp.stack([o[k] for o in outs_l])

    gfull = dict(
        pool_w=gr["pool_w"][None], pool_scale=gr["pool_scale"][None], conv_dw_b=gr["conv_dw_b"][None],
        conv_ln_g=gr["conv_ln_g"][None], conv_ln_b=gr["conv_ln_b"][None], attn_rel_bias=gr["attn_rel_bias"][None],
        ln_mix_g=jnp.stack(gr["ln_mix_g"]), ln_mix_b=jnp.stack(gr["ln_mix_b"]), ffn_dw_b=jnp.stack(gr["ffn_dw_b"]),
        ple_b_gate=jnp.stack(gr["ple_b_gate"]), ln_ffn_g=jnp.stack(gr["ln_ffn_g"]),
        ln_ffn_b=jnp.stack(gr["ln_ffn_b"]))
    rep_send = _pack_rows([gfull[n].reshape(-1) for n in _REPLICATED], 8, F32)
    (rep_recv,) = _exchange([rep_send], ["stack"], name="grad_all_gather")

    def flat_state(d):
        return _pack_rows([d[n].reshape(-1) for n in _REPLICATED], 8, F32)

    rep_out = _adamw(rep_recv, flat_state(wts), flat_state(mom), flat_state(var), name="adamw_replicated")
    for k in range(4):
        for n, arr in zip(_REPLICATED, _unpack(rep_out[k], [wts[n].shape for n in _REPLICATED])):
            res[k][n] = arr
    order = ["mix_w_in", "pool_w", "pool_scale", "conv_dw_w", "conv_dw_b", "conv_ln_g", "conv_ln_b", "mix_w_out",
             "attn_w_qkv", "attn_rel_bias", "attn_w_o", "ln_mix_g", "ln_mix_b", "ffn_w_up", "ffn_dw_w", "ffn_dw_b",
             "ffn_w_down", "ple_w_proj", "ple_w_gate", "ple_b_gate", "ln_ffn_g", "ln_ffn_b"]
    outs = [loss, grad_x[None]]
    for k in range(4):
        outs += [res[k][n] for n in order]
    return tuple(outs)
```

```python
import functools
import math

import jax
import jax.numpy as jnp
from jax import lax
from jax.experimental import pallas as pl
from jax.experimental.pallas import tpu as pltpu

F32 = jnp.float32
BF16 = jnp.bfloat16

N_DEV = 8
D_MODEL = 1024
D_POOL = 512
D_CONV = 512
POOL_WINDOWS = (2, 4, 8, 16)
POOL_GROUP = 128
CONV_KERNEL = 31
CHUNK = 64
HEAD_DIM = 64
N_HEADS = 16
LEFT_CHUNKS = 8
BAND = (LEFT_CHUNKS + 1) * CHUNK
MAX_REL = 256
D_FF = 2816
PLE_DIM = 256
ALPHA = 4.0 ** 0.25
LN_EPS = 1e-5
NEG_INF = -1e30
ADAM_LR, ADAM_B1, ADAM_B2, ADAM_EPS, ADAM_WD, ADAM_STEP = 0.001, 0.9, 0.999, 1e-08, 0.01, 10

Q_BLOCK = 4 * CHUNK
KV_PAD = LEFT_CHUNKS * CHUNK
KV_SPAN = KV_PAD + Q_BLOCK
CONV_HALO = 32
FFN_HALO = 8
LANES = 1024
VMEM_LIMIT = 56 * 1024 * 1024


def _cparams(sem=None):
    return pltpu.CompilerParams(dimension_semantics=sem, vmem_limit_bytes=VMEM_LIMIT)


def _tile(dim, pref):
    if dim <= pref:
        return dim
    t = pref - pref % 128
    while t >= 128:
        if dim % t == 0:
            return t
        t -= 128
    return dim


def _sigmoid(x):
    return 1.0 / (1.0 + jnp.exp(-x))


def _bdot(a, b, dn=(((1,), (0,)), ((), ()))):
    return lax.dot_general(a.astype(BF16), b.astype(BF16), dn, preferred_element_type=F32)


NT = (((1,), (1,)), ((), ()))
TN = (((0,), (0,)), ((), ()))


def _mm(a, b, *, ta=False, tb=False, add=None, add_scale=1.0, out_dtype=F32, tm=512, tn=512, tk=1024, name):
    if ta:
        K, M = a.shape
    else:
        M, K = a.shape
    if tb:
        N, kb = b.shape
    else:
        kb, N = b.shape
    assert K == kb, (a.shape, b.shape)
    tm, tn, tk = _tile(M, tm), _tile(N, tn), _tile(K, tk)
    nk = K // tk
    a_spec = pl.BlockSpec((tk, tm), lambda i, j, k: (k, i)) if ta else pl.BlockSpec((tm, tk), lambda i, j, k: (i, k))
    b_spec = pl.BlockSpec((tn, tk), lambda i, j, k: (j, k)) if tb else pl.BlockSpec((tk, tn), lambda i, j, k: (k, j))
    dn = (((0 if ta else 1,), (1 if tb else 0,)), ((), ()))
    has_add = add is not None

    def body(*refs):
        if has_add:
            a_ref, b_ref, add_ref, o_ref, acc = refs
        else:
            a_ref, b_ref, o_ref, acc = refs
        k = pl.program_id(2)

        @pl.when(k == 0)
        def _():
            acc[...] = jnp.zeros_like(acc)

        acc[...] += _bdot(a_ref[...], b_ref[...], dn)

        @pl.when(k == nk - 1)
        def _():
            r = acc[...]
            if has_add:
                r = r + add_scale * add_ref[...]
            o_ref[...] = r.astype(out_dtype)

    in_specs = [a_spec, b_spec]
    args = [a, b]
    if has_add:
        in_specs.append(pl.BlockSpec((tm, tn), lambda i, j, k: (i, j)))
        args.append(add)
    return pl.pallas_call(
        body,
        out_shape=jax.ShapeDtypeStruct((M, N), out_dtype),
        grid=(M // tm, N // tn, nk),
        in_specs=in_specs,
        out_specs=pl.BlockSpec((tm, tn), lambda i, j, k: (i, j)),
        scratch_shapes=[pltpu.VMEM((tm, tn), F32)],
        compiler_params=_cparams(("parallel", "parallel", "arbitrary")),
        name=name,
    )(*args)


def _layer_norm_rows(z, g, b):
    mu = jnp.mean(z, axis=-1, keepdims=True)
    zc = z - mu
    var = jnp.mean(zc * zc, axis=-1, keepdims=True)
    return zc * lax.rsqrt(var + LN_EPS) * g + b


def _proj_ln(res, a, w, ln_g, ln_b, *, ple=None, ts=256, name):
    S, D = res.shape
    ka = a.shape[1]
    has_ple = ple is not None
    row = lambda i: (i, 0)
    fix = lambda i: (0, 0)

    def body(*refs):
        if has_ple:
            res_ref, a_ref, w_ref, g_ref, b_ref, wg_ref, bg_ref, p_ref, wp_ref, z_ref, r_ref, gate_ref, proj_ref = refs
        else:
            res_ref, a_ref, w_ref, g_ref, b_ref, z_ref, r_ref = refs
        res_t = res_ref[...]
        acc = _bdot(a_ref[...], w_ref[...])
        if has_ple:
            gate = _sigmoid(_bdot(res_t, wg_ref[...]) + bg_ref[...])
            proj = _bdot(p_ref[...], wp_ref[...])
            gate_ref[...] = gate
            proj_ref[...] = proj
            acc = acc + gate * proj
        z = ALPHA * res_t + acc
        z_ref[...] = z
        r_ref[...] = _layer_norm_rows(z, g_ref[...], b_ref[...])

    in_specs = [pl.BlockSpec((ts, D), row), pl.BlockSpec((ts, ka), row), pl.BlockSpec((ka, D), fix),
                pl.BlockSpec((1, D), fix), pl.BlockSpec((1, D), fix)]
    args = [res, a, w, ln_g.reshape(1, D), ln_b.reshape(1, D)]
    n_out = 2
    if has_ple:
        wg, bg, p, wp = ple
        in_specs += [pl.BlockSpec((D, D), fix), pl.BlockSpec((1, D), fix), pl.BlockSpec((ts, PLE_DIM), row),
                     pl.BlockSpec((PLE_DIM, D), fix)]
        args += [wg, bg.reshape(1, D), p, wp]
        n_out = 4
    return pl.pallas_call(
        body,
        out_shape=[jax.ShapeDtypeStruct((S, D), F32)] * n_out,
        grid=(S // ts,),
        in_specs=in_specs,
        out_specs=[pl.BlockSpec((ts, D), row)] * n_out,
        compiler_params=_cparams(("parallel",)),
        name=name,
    )(*args)


def _mixer_fwd(u, pool_w, pool_scale, conv_w, conv_b, cln_g, cln_b, *, ts=256):
    S = u.shape[0]
    hb = CONV_HALO
    nh = ts // hb

    def body(u_ref, uh_ref, pw_ref, ps_ref, cw_ref, cb_ref, g_ref, b_ref, y_ref, d_ref, sta, stg):
        i = pl.program_id(0)
        first = i == 0
        sta[pl.ds(0, hb), :] = jnp.where(first, 0.0, uh_ref[:, 0:D_POOL])
        sta[pl.ds(hb, ts), :] = u_ref[:, 0:D_POOL]
        glu_h = uh_ref[:, D_POOL:D_POOL + D_CONV] * _sigmoid(uh_ref[:, D_POOL + D_CONV:])
        stg[pl.ds(0, hb), :] = jnp.where(first, 0.0, glu_h)
        stg[pl.ds(hb, ts), :] = u_ref[:, D_POOL:D_POOL + D_CONV] * _sigmoid(u_ref[:, D_POOL + D_CONV:])

        pos = (i * ts + lax.broadcasted_iota(jnp.int32, (ts, 1), 0) + 1).astype(F32)
        for g, w in enumerate(POOL_WINDOWS):
            lanes = pl.ds(g * POOL_GROUP, POOL_GROUP)
            a_g = sta[pl.ds(hb, ts), lanes]
            s = a_g
            for j in range(1, w):
                s = s + sta[pl.ds(hb - j, ts), lanes]
            d_g = s / jnp.minimum(pos, float(w)) - a_g
            d_ref[:, lanes] = d_g.astype(BF16)
            y_ref[:, lanes] = (_bdot(d_g, pw_ref[g]) * ps_ref[:, lanes]).astype(BF16)

        acc = jnp.zeros((ts, D_CONV), F32)
        for k in range(CONV_KERNEL):
            acc = acc + cw_ref[k:k + 1, :] * stg[pl.ds(hb - (CONV_KERNEL - 1) + k, ts), :]
        hc = acc + cb_ref[...]
        ln = _layer_norm_rows(hc, g_ref[...], b_ref[...])
        y_ref[:, D_POOL:] = (ln * _sigmoid(ln)).astype(BF16)

    fix2 = lambda i: (0, 0)
    return pl.pallas_call(
        body,
        out_shape=[jax.ShapeDtypeStruct((S, D_MODEL), BF16), jax.ShapeDtypeStruct((S, D_POOL), BF16)],
        grid=(S // ts,),
        in_specs=[pl.BlockSpec((ts, 3 * D_POOL), lambda i: (i, 0)),
                  pl.BlockSpec((hb, 3 * D_POOL), lambda i: (jnp.maximum(i * nh - 1, 0), 0)),
                  pl.BlockSpec((4, POOL_GROUP, POOL_GROUP), lambda i: (0, 0, 0)),
                  pl.BlockSpec((1, D_POOL), fix2), pl.BlockSpec((CONV_KERNEL, D_CONV), fix2),
                  pl.BlockSpec((1, D_CONV), fix2), pl.BlockSpec((1, D_CONV), fix2), pl.BlockSpec((1, D_CONV), fix2)],
        out_specs=[pl.BlockSpec((ts, D_MODEL), lambda i: (i, 0)), pl.BlockSpec((ts, D_POOL), lambda i: (i, 0))],
        scratch_shapes=[pltpu.VMEM((hb + ts, D_POOL), F32), pltpu.VMEM((hb + ts, D_CONV), F32)],
        compiler_params=_cparams(("parallel",)),
        name="mixer_fwd",
    )(u, u, pool_w, pool_scale.reshape(1, D_POOL), conv_w, conv_b.reshape(1, D_CONV), cln_g.reshape(1, D_CONV),
      cln_b.reshape(1, D_CONV))


def _mixer_bwd(u, d, dycat, pool_w, pool_scale, conv_w, conv_b, cln_g, cln_b, *, ts=256):
    S = u.shape[0]
    hb = CONV_HALO
    nh = ts // hb
    n = S // ts
    te = ts + hb
    K = CONV_KERNEL

    def body(u_ref, up_ref, un_ref, d_ref, dy_ref, dyn_ref, pw_ref, ps_ref, cw_ref, cb_ref, g_ref, b_ref,
             du_ref, dpw_ref, dps_ref, dcw_ref, dcb_ref, dg_ref, db_ref, stg, std, sth):
        i = pl.program_id(0)
        first = i == 0
        last = i == n - 1

        @pl.when(first)
        def _():
            dpw_ref[...] = jnp.zeros_like(dpw_ref)
            dps_ref[...] = jnp.zeros_like(dps_ref)
            dcw_ref[...] = jnp.zeros_like(dcw_ref)
            dcb_ref[...] = jnp.zeros_like(dcb_ref)
            dg_ref[...] = jnp.zeros_like(dg_ref)
            db_ref[...] = jnp.zeros_like(db_ref)

        pos_e = (i * ts + lax.broadcasted_iota(jnp.int32, (te, 1), 0) + 1).astype(F32)
        dya = dy_ref[:, 0:D_POOL]
        dya_n = jnp.where(last, 0.0, dyn_ref[:, 0:D_POOL])
        for g, w in enumerate(POOL_WINDOWS):
            lanes = pl.ds(g * POOL_GROUP, POOL_GROUP)
            sl = slice(g * POOL_GROUP, (g + 1) * POOL_GROUP)
            pw = pw_ref[g]
            scale = ps_ref[:, lanes]
            d_g = d_ref[:, lanes]
            pre = _bdot(d_g, pw)
            dps_ref[:, lanes] += jnp.sum(dya[:, sl] * pre, axis=0, keepdims=True)
            dys = dya[:, sl] * scale
            dpw_ref[g] += _bdot(d_g, dys, TN)
            dys_e = jnp.concatenate([dys, dya_n[:, sl] * scale], axis=0)
            dd = _bdot(dys_e, pw, NT)
            std[:, lanes] = dd / jnp.minimum(pos_e, float(w))
            da = -dd[0:ts]
            for m in range(w):
                da = da + std[pl.ds(m, ts), lanes]
            du_ref[:, lanes] = da.astype(BF16)

        glu_p = up_ref[:, D_POOL:D_POOL + D_CONV] * _sigmoid(up_ref[:, D_POOL + D_CONV:])
        stg[pl.ds(0, hb), :] = jnp.where(first, 0.0, glu_p)
        bv = u_ref[:, D_POOL:D_POOL + D_CONV]
        sg = _sigmoid(u_ref[:, D_POOL + D_CONV:])
        stg[pl.ds(hb, ts), :] = bv * sg
        glu_n = un_ref[:, D_POOL:D_POOL + D_CONV] * _sigmoid(un_ref[:, D_POOL + D_CONV:])
        stg[pl.ds(hb + ts, hb), :] = jnp.where(last, 0.0, glu_n)

        acc = jnp.zeros((te, D_CONV), F32)
        for k in range(K):
            acc = acc + cw_ref[k:k + 1, :] * stg[pl.ds(hb - (K - 1) + k, te), :]
        hc = acc + cb_ref[...]
        mu = jnp.mean(hc, axis=-1, keepdims=True)
        hcc = hc - mu
        rstd = lax.rsqrt(jnp.mean(hcc * hcc, axis=-1, keepdims=True) + LN_EPS)
        xh = hcc * rstd
        ln = xh * g_ref[...] + b_ref[...]
        sl_ = _sigmoid(ln)
        dyb = jnp.concatenate([dy_ref[:, D_POOL:], jnp.where(last, 0.0, dyn_ref[:, D_POOL:])], axis=0)
        dln = dyb * (sl_ * (1.0 + ln * (1.0 - sl_)))
        dxh = dln * g_ref[...]
        dhc = rstd * (dxh - jnp.mean(dxh, axis=-1, keepdims=True) - xh * jnp.mean(dxh * xh, axis=-1, keepdims=True))
        sth[...] = dhc
        dg_ref[...] += jnp.sum((dln * xh)[0:ts], axis=0, keepdims=True)
        db_ref[...] += jnp.sum(dln[0:ts], axis=0, keepdims=True)
        dhc_t = dhc[0:ts]
        dcb_ref[...] += jnp.sum(dhc_t, axis=0, keepdims=True)
        dglu = jnp.zeros((ts, D_CONV), F32)
        for k in range(K):
            dcw_ref[k:k + 1, :] += jnp.sum(dhc_t * stg[pl.ds(hb - (K - 1) + k, ts), :], axis=0, keepdims=True)
            dglu = dglu + cw_ref[k:k + 1, :] * sth[pl.ds(K - 1 - k, ts), :]
        du_ref[:, D_POOL:D_POOL + D_CONV] = (dglu * sg).astype(BF16)
        du_ref[:, D_POOL + D_CONV:] = (dglu * bv * sg * (1.0 - sg)).astype(BF16)

    fix2 = lambda i: (0, 0)
    prev = lambda i: (jnp.maximum(i * nh - 1, 0), 0)
    nxt = lambda i: (jnp.minimum((i + 1) * nh, S // hb - 1), 0)
    return pl.pallas_call(
        body,
        out_shape=[jax.ShapeDtypeStruct((S, 3 * D_POOL), BF16),
                   jax.ShapeDtypeStruct((4, POOL_GROUP, POOL_GROUP), F32),
                   jax.ShapeDtypeStruct((1, D_POOL), F32),
                   jax.ShapeDtypeStruct((K, D_CONV), F32),
                   jax.ShapeDtypeStruct((1, D_CONV), F32),
                   jax.ShapeDtypeStruct((1, D_CONV), F32),
                   jax.ShapeDtypeStruct((1, D_CONV), F32)],
        grid=(n,),
        in_specs=[pl.BlockSpec((ts, 3 * D_POOL), lambda i: (i, 0)),
                  pl.BlockSpec((hb, 3 * D_POOL), prev),
                  pl.BlockSpec((hb, 3 * D_POOL), nxt),
                  pl.BlockSpec((ts, D_POOL), lambda i: (i, 0)),
                  pl.BlockSpec((ts, D_MODEL), lambda i: (i, 0)),
                  pl.BlockSpec((hb, D_MODEL), nxt),
                  pl.BlockSpec((4, POOL_GROUP, POOL_GROUP), lambda i: (0, 0, 0)),
                  pl.BlockSpec((1, D_POOL), fix2), pl.BlockSpec((K, D_CONV), fix2),
                  pl.BlockSpec((1, D_CONV), fix2), pl.BlockSpec((1, D_CONV), fix2), pl.BlockSpec((1, D_CONV), fix2)],
        out_specs=[pl.BlockSpec((ts, 3 * D_POOL), lambda i: (i, 0)),
                   pl.BlockSpec((4, POOL_GROUP, POOL_GROUP), lambda i: (0, 0, 0)),
                   pl.BlockSpec((1, D_POOL), fix2), pl.BlockSpec((K, D_CONV), fix2),
                   pl.BlockSpec((1, D_CONV), fix2), pl.BlockSpec((1, D_CONV), fix2), pl.BlockSpec((1, D_CONV), fix2)],
        scratch_shapes=[pltpu.VMEM((hb + ts + hb, D_CONV), F32), pltpu.VMEM((te, D_POOL), F32),
                        pltpu.VMEM((te, D_CONV), F32)],
        compiler_params=_cparams(("arbitrary",)),
        name="mixer_bwd",
    )(u, u, u, d, dycat, dycat, pool_w, pool_scale.reshape(1, D_POOL), conv_w, conv_b.reshape(1, D_CONV),
      cln_g.reshape(1, D_CONV), cln_b.reshape(1, D_CONV))


_GELU_C = math.sqrt(2.0 / math.pi)


def _gelu_parts(x):
    inner = _GELU_C * (x + 0.044715 * x * x * x)
    th = jnp.tanh(inner)
    ge = 0.5 * x * (1.0 + th)
    dge = 0.5 * (1.0 + th) + 0.5 * x * (1.0 - th * th) * (_GELU_C * (1.0 + 3.0 * 0.044715 * x * x))
    return ge, dge


def _ffn_act_fwd(gate, val, dw_w, dw_b, *, ts=256, tc=1408, name):
    S, F = gate.shape
    hb = FFN_HALO
    nh = ts // hb
    tc = _tile(F, tc)

    def body(g_ref, gh_ref, v_ref, w_ref, b_ref, h_ref, st):
        i = pl.program_id(0)
        st[pl.ds(0, hb), :] = jnp.where(i == 0, 0.0, gh_ref[...])
        st[pl.ds(hb, ts), :] = g_ref[...]
        gc = b_ref[...] + w_ref[0:1, :] * st[pl.ds(hb - 2, ts), :] + w_ref[1:2, :] * st[pl.ds(hb - 1, ts), :] \
            + w_ref[2:3, :] * st[pl.ds(hb, ts), :]
        ge, _ = _gelu_parts(gc)
        h_ref[...] = (ge * v_ref[...]).astype(BF16)

    return pl.pallas_call(
        body,
        out_shape=jax.ShapeDtypeStruct((S, F), BF16),
        grid=(S // ts, F // tc),
        in_specs=[pl.BlockSpec((ts, tc), lambda i, j: (i, j)),
                  pl.BlockSpec((hb, tc), lambda i, j: (jnp.maximum(i * nh - 1, 0), j)),
                  pl.BlockSpec((ts, tc), lambda i, j: (i, j)),
                  pl.BlockSpec((3, tc), lambda i, j: (0, j)),
                  pl.BlockSpec((1, tc), lambda i, j: (0, j))],
        out_specs=pl.BlockSpec((ts, tc), lambda i, j: (i, j)),
        scratch_shapes=[pltpu.VMEM((hb + ts, tc), F32)],
        compiler_params=_cparams(("parallel", "parallel")),
        name=name,
    )(gate, gate, val, dw_w, dw_b.reshape(1, F))


def _ffn_act_bwd(gate, val, dh, dw_w, dw_b, *, ts=256, tc=1408, name):
    S, F = gate.shape
    hb = FFN_HALO
    nh = ts // hb
    n = S // ts
    te = ts + hb
    tc = _tile(F, tc)

    def body(g_ref, gp_ref, gn_ref, v_ref, vn_ref, dh_ref, dhn_ref, w_ref, b_ref,
             dg_ref, dv_ref, dw_ref, db_ref, st, sd):
        i = pl.program_id(1)
        first = i == 0
        last = i == n - 1

        @pl.when(first)
        def _():
            dw_ref[...] = jnp.zeros_like(dw_ref)
            db_ref[...] = jnp.zeros_like(db_ref)

        st[pl.ds(0, hb), :] = jnp.where(first, 0.0, gp_ref[...])
        st[pl.ds(hb, ts), :] = g_ref[...]
        st[pl.ds(hb + ts, hb), :] = jnp.where(last, 0.0, gn_ref[...])
        gc = b_ref[...] + w_ref[0:1, :] * st[pl.ds(hb - 2, te), :] + w_ref[1:2, :] * st[pl.ds(hb - 1, te), :] \
            + w_ref[2:3, :] * st[pl.ds(hb, te), :]
        ge, dge = _gelu_parts(gc)
        val_e = jnp.concatenate([v_ref[...], jnp.where(last, 0.0, vn_ref[...])], axis=0)
        dh_e = jnp.concatenate([dh_ref[...], jnp.where(last, 0.0, dhn_ref[...])], axis=0)
        dgc = dh_e * val_e * dge
        sd[...] = dgc
        dv_ref[...] = (dh_e[0:ts] * ge[0:ts]).astype(BF16)
        dgc_t = dgc[0:ts]
        db_ref[...] += jnp.sum(dgc_t, axis=0, keepdims=True)
        dgate = jnp.zeros((ts, tc), F32)
        for k in range(3):
            dw_ref[k:k + 1, :] += jnp.sum(dgc_t * st[pl.ds(hb - 2 + k, ts), :], axis=0, keepdims=True)
            dgate = dgate + w_ref[k:k + 1, :] * sd[pl.ds(2 - k, ts), :]
        dg_ref[...] = dgate.astype(BF16)

    cur = lambda j, i: (i, j)
    prev = lambda j, i: (jnp.maximum(i * nh - 1, 0), j)
    nxt = lambda j, i: (jnp.minimum((i + 1) * nh, S // hb - 1), j)
    return pl.pallas_call(
        body,
        out_shape=[jax.ShapeDtypeStruct((S, F), BF16), jax.ShapeDtypeStruct((S, F), BF16),
                   jax.ShapeDtypeStruct((3, F), F32), jax.ShapeDtypeStruct((1, F), F32)],
        grid=(F // tc, n),
        in_specs=[pl.BlockSpec((ts, tc), cur), pl.BlockSpec((hb, tc), prev), pl.BlockSpec((hb, tc), nxt),
                  pl.BlockSpec((ts, tc), cur), pl.BlockSpec((hb, tc), nxt),
                  pl.BlockSpec((ts, tc), cur), pl.BlockSpec((hb, tc), nxt),
                  pl.BlockSpec((3, tc), lambda j, i: (0, j)), pl.BlockSpec((1, tc), lambda j, i: (0, j))],
        out_specs=[pl.BlockSpec((ts, tc), cur), pl.BlockSpec((ts, tc), cur),
                   pl.BlockSpec((3, tc), lambda j, i: (0, j)), pl.BlockSpec((1, tc), lambda j, i: (0, j))],
        scratch_shapes=[pltpu.VMEM((hb + ts + hb, tc), F32), pltpu.VMEM((te, tc), F32)],
        compiler_params=_cparams(("parallel", "arbitrary")),
        name=name,
    )(gate, gate, gate, val, val, dh, dh, dw_w, dw_b.reshape(1, F))


def _ln_bwd(z, ln_g, ln_b, dout, *, loss_head=False, ts=256, name):
    S, D = z.shape

    def body(z_ref, g_ref, b_ref, do_ref, dz_ref, dg_ref, db_ref, loss_ref):
        i = pl.program_id(0)

        @pl.when(i == 0)
        def _():
            dg_ref[...] = jnp.zeros_like(dg_ref)
            db_ref[...] = jnp.zeros_like(db_ref)
            loss_ref[...] = jnp.zeros_like(loss_ref)

        zt = z_ref[...]
        mu = jnp.mean(zt, axis=-1, keepdims=True)
        zc = zt - mu
        rstd = lax.rsqrt(jnp.mean(zc * zc, axis=-1, keepdims=True) + LN_EPS)
        xh = zc * rstd
        if loss_head:
            err = xh * g_ref[...] + b_ref[...] - do_ref[...]
            loss_ref[...] += 0.5 * jnp.sum(jnp.mean(err * err, axis=-1, keepdims=True))
            do = err * (1.0 / D)
        else:
            do = do_ref[...]
        dg_ref[...] += jnp.sum(do * xh, axis=0, keepdims=True)
        db_ref[...] += jnp.sum(do, axis=0, keepdims=True)
        dxh = do * g_ref[...]
        dz_ref[...] = rstd * (dxh - jnp.mean(dxh, axis=-1, keepdims=True)
                              - xh * jnp.mean(dxh * xh, axis=-1, keepdims=True))

    row = lambda i: (i, 0)
    fix = lambda i: (0, 0)
    return pl.pallas_call(
        body,
        out_shape=[jax.ShapeDtypeStruct((S, D), F32), jax.ShapeDtypeStruct((1, D), F32),
                   jax.ShapeDtypeStruct((1, D), F32), jax.ShapeDtypeStruct((8, 128), F32)],
        grid=(S // ts,),
        in_specs=[pl.BlockSpec((ts, D), row), pl.BlockSpec((1, D), fix), pl.BlockSpec((1, D), fix),
                  pl.BlockSpec((ts, D), row)],
        out_specs=[pl.BlockSpec((ts, D), row), pl.BlockSpec((1, D), fix), pl.BlockSpec((1, D), fix),
                   pl.BlockSpec((8, 128), fix)],
        compiler_params=_cparams(("arbitrary",)),
        name=name,
    )(z, ln_g.reshape(1, D), ln_b.reshape(1, D), dout)


def _ple_bwd(dz, gate, proj, *, ts=256, name):
    S, D = dz.shape

    def body(dz_ref, g_ref, p_ref, ds_ref, dp_ref, db_ref):
        @pl.when(pl.program_id(0) == 0)
        def _():
            db_ref[...] = jnp.zeros_like(db_ref)

        dzt = dz_ref[...]
        g = g_ref[...]
        ds = dzt * p_ref[...] * g * (1.0 - g)
        ds_ref[...] = ds.astype(BF16)
        dp_ref[...] = (dzt * g).astype(BF16)
        db_ref[...] += jnp.sum(ds, axis=0, keepdims=True)

    row = lambda i: (i, 0)
    return pl.pallas_call(
        body,
        out_shape=[jax.ShapeDtypeStruct((S, D), BF16), jax.ShapeDtypeStruct((S, D), BF16),
                   jax.ShapeDtypeStruct((1, D), F32)],
        grid=(S // ts,),
        in_specs=[pl.BlockSpec((ts, D), row)] * 3,
        out_specs=[pl.BlockSpec((ts, D), row), pl.BlockSpec((ts, D), row), pl.BlockSpec((1, D), lambda i: (0, 0))],
        compiler_params=_cparams(("arbitrary",)),
        name=name,
    )(dz, gate, proj)


def _attn_scores(q, kc, bias, qb):
    s = _bdot(q, kc, NT) * (HEAD_DIM ** -0.5) + bias
    kpos = qb * Q_BLOCK + lax.broadcasted_iota(jnp.int32, (1, KV_SPAN), 1)
    s = jnp.where(kpos >= KV_PAD, s, NEG_INF)
    m = jnp.max(s, axis=-1, keepdims=True)
    e = jnp.exp(s - m)
    return e / jnp.sum(e, axis=-1, keepdims=True)


def _attn_fwd(q, kp, vp, bias):
    H, S, dh = q.shape
    sp = kp.shape[1]

    def body(q_ref, k_ref, v_ref, b_ref, o_ref):
        qb = pl.program_id(1)
        start = pl.multiple_of(qb * Q_BLOCK, Q_BLOCK)
        kc = k_ref[pl.ds(start, KV_SPAN), :]
        vc = v_ref[pl.ds(start, KV_SPAN), :]
        p = _attn_scores(q_ref[...], kc, b_ref[...], qb)
        o_ref[...] = _bdot(p, vc).astype(BF16)

    return pl.pallas_call(
        body,
        out_shape=jax.ShapeDtypeStruct((H, S, dh), BF16),
        grid=(H, S // Q_BLOCK),
        in_specs=[pl.BlockSpec((None, Q_BLOCK, dh), lambda h, i: (h, i, 0)),
                  pl.BlockSpec((None, sp, dh), lambda h, i: (h, 0, 0)),
                  pl.BlockSpec((None, sp, dh), lambda h, i: (h, 0, 0)),
                  pl.BlockSpec((None, Q_BLOCK, KV_SPAN), lambda h, i: (h, 0, 0))],
        out_specs=pl.BlockSpec((None, Q_BLOCK, dh), lambda h, i: (h, i, 0)),
        compiler_params=_cparams(("parallel", "arbitrary")),
        name="attn_fwd",
    )(q, kp, vp, bias)


def _attn_bwd(q, kp, vp, bias, do):
    H, S, dh = q.shape
    sp = kp.shape[1]
    scale = HEAD_DIM ** -0.5

    def body(q_ref, k_ref, v_ref, b_ref, do_ref, dq_ref, dk_ref, dv_ref, db_ref):
        qb = pl.program_id(1)

        @pl.when(qb == 0)
        def _():
            dk_ref[...] = jnp.zeros_like(dk_ref)
            dv_ref[...] = jnp.zeros_like(dv_ref)
            db_ref[...] = jnp.zeros_like(db_ref)

        start = pl.multiple_of(qb * Q_BLOCK, Q_BLOCK)
        span = pl.ds(start, KV_SPAN)
        qt = q_ref[...]
        kc = k_ref[span, :]
        vc = v_ref[span, :]
        dot = do_ref[...]
        p = _attn_scores(qt, kc, b_ref[...], qb)
        dv_ref[span, :] += _bdot(p, dot, TN)
        dp = _bdot(dot, vc, NT)
        ds = p * (dp - jnp.sum(p * dp, axis=-1, keepdims=True))
        db_ref[...] += ds
        dq_ref[...] = (scale * _bdot(ds, kc)).astype(BF16)
        dk_ref[span, :] += scale * _bdot(ds, qt, TN)

    blk = pl.BlockSpec((None, Q_BLOCK, dh), lambda h, i: (h, i, 0))
    full = pl.BlockSpec((None, sp, dh), lambda h, i: (h, 0, 0))
    bsp = pl.BlockSpec((None, Q_BLOCK, KV_SPAN), lambda h, i: (h, 0, 0))
    return pl.pallas_call(
        body,
        out_shape=[jax.ShapeDtypeStruct((H, S, dh), BF16), jax.ShapeDtypeStruct((H, sp, dh), F32),
                   jax.ShapeDtypeStruct((H, sp, dh), F32), jax.ShapeDtypeStruct((H, Q_BLOCK, KV_SPAN), F32)],
        grid=(H, S // Q_BLOCK),
        in_specs=[blk, full, full, bsp, blk],
        out_specs=[blk, full, full, bsp],
        compiler_params=_cparams(("parallel", "arbitrary")),
        name="attn_bwd",
    )(q, kp, vp, bias, do)


def _bias_blocks(rel_bias):
    H = rel_bias.shape[0]
    n_e = BAND + CHUNK - 1
    n_clip = KV_PAD + CHUNK - 1 - MAX_REL + 1
    e = jnp.concatenate([jnp.broadcast_to(rel_bias[:, 2 * MAX_REL:], (H, n_clip)),
                         jnp.flip(rel_bias[:, 2 * MAX_REL - (n_e - n_clip):2 * MAX_REL], axis=1)], axis=1)
    skew = jnp.pad(jnp.tile(e, (1, CHUNK)), ((0, 0), (0, CHUNK))).reshape(H, CHUNK, n_e + 1)
    band = jnp.flip(skew, axis=1)[:, :, :BAND]
    rows = [jnp.pad(band, ((0, 0), (0, 0), (c * CHUNK, KV_SPAN - BAND - c * CHUNK)), constant_values=NEG_INF)
            for c in range(Q_BLOCK // CHUNK)]
    return jnp.concatenate(rows, axis=1)


def _bias_blocks_grad(dblk):
    H = dblk.shape[0]
    n_e = BAND + CHUNK - 1
    n_clip = KV_PAD + CHUNK - 1 - MAX_REL + 1
    parts = jnp.stack([dblk[:, c * CHUNK:(c + 1) * CHUNK, c * CHUNK:c * CHUNK + BAND]
                       for c in range(Q_BLOCK // CHUNK)], axis=1)
    parts = jnp.flip(parts, axis=2)
    parts = jnp.pad(parts, ((0, 0), (0, 0), (0, 0), (0, n_e + 1 - BAND)))
    skew = parts.reshape(H, Q_BLOCK // CHUNK, CHUNK * (n_e + 1))[:, :, :CHUNK * n_e]
    skew = skew.reshape(H, Q_BLOCK, n_e)
    skew = jnp.pad(skew, ((0, 0), (0, 0), (0, 1)))

    def body(s_ref, o_ref):
        de = jnp.sum(s_ref[...], axis=0, keepdims=True)
        lane = lax.broadcasted_iota(jnp.int32, de.shape, 1)
        far = jnp.sum(jnp.where(lane < n_clip, de, 0.0), axis=-1, keepdims=True)
        o_ref[...] = jnp.where(lane == 0, far, jnp.where(lane < n_clip, 0.0, de))

    de = pl.pallas_call(
        body,
        out_shape=jax.ShapeDtypeStruct((H, 1, n_e + 1), F32),
        grid=(H,),
        in_specs=[pl.BlockSpec((None, Q_BLOCK, n_e + 1), lambda h: (h, 0, 0))],
        out_specs=pl.BlockSpec((None, 1, n_e + 1), lambda h: (h, 0, 0)),
        compiler_params=_cparams(("parallel",)),
        name="bias_grad_sum",
    )(skew).reshape(H, n_e + 1)
    near = jnp.flip(de[:, n_clip:n_e], axis=1)
    return jnp.concatenate([jnp.zeros((H, 2 * MAX_REL - (n_e - n_clip)), F32), near, de[:, 0:1]], axis=1)


def _ffn_forward(r1, p_l, w, l, ready):
    ready(f"up{l}", r1)
    up_g = _mm(r1, w["ffn_up_g"][l], tn=1408, name=f"ffn_up_g{l}")
    up_v = _mm(r1, w["ffn_up_v"][l], tn=1408, name=f"ffn_up_v{l}")
    h = _ffn_act_fwd(up_g, up_v, w["ffn_dw_w"][l], w["ffn_dw_b"][l], name=f"ffn_act{l}")
    ready(f"dn{l}", h)
    z2, r2, gate, proj = _proj_ln(r1, h, w["ffn_w_down"][l], w["ln_ffn_g"][l], w["ln_ffn_b"][l],
                                  ple=(w["ple_w_gate"][l], w["ple_b_gate"][l], p_l, w["ple_w_proj"][l]),
                                  name=f"ffn_down_ln{l}")
    return dict(r1=r1, up_g=up_g, up_v=up_v, h=h, z2=z2, gate=gate, proj=proj), r2


def _ffn_backward(sv, dz2, p_l, w, l, grads):
    r1 = sv["r1"]
    ds, dproj, db_gate = _ple_bwd(dz2, sv["gate"], sv["proj"], name=f"ple_bwd{l}")
    dh = _mm(dz2, w["ffn_w_down"][l], tb=True, tn=1408, name=f"ffn_dh{l}")
    dgate, dval, d_dw_w, d_dw_b = _ffn_act_bwd(sv["up_g"], sv["up_v"], dh, w["ffn_dw_w"][l], w["ffn_dw_b"][l],
                                               name=f"ffn_act_bwd{l}")
    grads["ffn_w_down"][l] = _mm(sv["h"], dz2, ta=True, tm=1408, tn=1024, tk=512, name=f"d_ffn_w_down{l}")
    grads["ffn_up_g"][l] = _mm(r1, dgate, ta=True, tm=1024, tn=1408, tk=512, name=f"d_ffn_up_g{l}")
    grads["ffn_up_v"][l] = _mm(r1, dval, ta=True, tm=1024, tn=1408, tk=512, name=f"d_ffn_up_v{l}")
    grads["ple_w_gate"][l] = _mm(r1, ds, ta=True, tm=1024, tn=1024, tk=512, name=f"d_ple_w_gate{l}")
    grads["ple_w_proj"][l] = _mm(p_l, dproj, ta=True, tm=256, tn=1024, tk=512, name=f"d_ple_w_proj{l}")
    grads["ffn_dw_w"][l] = d_dw_w
    grads["ffn_dw_b"][l] = d_dw_b[0]
    grads["ple_b_gate"][l] = db_gate[0]
    t = _mm(ds, w["ple_w_gate"][l], tb=True, add=dz2, add_scale=ALPHA, tn=1024, name=f"dr1_gate{l}")
    t = _mm(dgate, w["ffn_up_g"][l], tb=True, add=t, tn=1024, tk=1408, name=f"dr1_up_g{l}")
    return _mm(dval, w["ffn_up_v"][l], tb=True, add=t, tn=1024, tk=1408, name=f"dr1_up_v{l}")


def _tie(x, token):
    if token is None:
        return x
    return lax.optimization_barrier((x, token))[0]


def _to_heads(a, n):
    S = a.shape[0]
    t = a.reshape(S, n, N_HEADS, HEAD_DIM).transpose(1, 2, 0, 3)
    return [t[i] for i in range(n)]


def _local_step(x, p, target, w, ready=lambda group, after: None, emit=lambda group, grads: None):
    S = x.shape[0]
    grads = {k: [None, None] for k in ("ffn_w_down", "ffn_up_g", "ffn_up_v", "ple_w_gate", "ple_w_proj", "ffn_dw_w",
                                       "ffn_dw_b", "ple_b_gate", "ln_ffn_g", "ln_ffn_b", "ln_mix_g", "ln_mix_b")}

    ready("mix", None)
    u = _mm(x, w["mix_w_in"], name="mix_in")
    ycat, dpool = _mixer_fwd(u, w["pool_w"], w["pool_scale"], w["conv_dw_w"], w["conv_dw_b"], w["conv_ln_g"],
                             w["conv_ln_b"])
    z1, r1 = _proj_ln(x, ycat, w["mix_w_out"], w["ln_mix_g"][0], w["ln_mix_b"][0], name="mix_out_ln")
    sv0, r2 = _ffn_forward(r1, p[0], w, 0, ready)

    ready("attn", r2)
    qkv = _mm(r2, w["attn_w_qkv"], out_dtype=BF16, name="attn_qkv")
    q, k, v = _to_heads(qkv, 3)
    kp = jnp.pad(k, ((0, 0), (KV_PAD, 0), (0, 0)))
    vp = jnp.pad(v, ((0, 0), (KV_PAD, 0), (0, 0)))
    bias = _bias_blocks(w["attn_rel_bias"])
    attn_h = _attn_fwd(q, kp, vp, bias)
    attn = attn_h.transpose(1, 0, 2).reshape(S, D_MODEL)
    z3, r3 = _proj_ln(r2, attn, w["attn_w_o"], w["ln_mix_g"][1], w["ln_mix_b"][1], name="attn_out_ln")
    sv1, _ = _ffn_forward(r3, p[1], w, 1, ready)

    dz4, grads["ln_ffn_g"][1], grads["ln_ffn_b"][1], loss = _ln_bwd(sv1["z2"], w["ln_ffn_g"][1], w["ln_ffn_b"][1],
                                                                    target, loss_head=True, name="loss_ln_bwd")
    dr3 = _ffn_backward(sv1, dz4, p[1], w, 1, grads)
    dr3 = _tie(dr3, emit("ffn1", grads))
    dz3, grads["ln_mix_g"][1], grads["ln_mix_b"][1], _ = _ln_bwd(z3, w["ln_mix_g"][1], w["ln_mix_b"][1], dr3,
                                                                name="ln_mix_bwd1")
    grads["attn_w_o"] = _mm(attn, dz3, ta=True, tm=1024, tn=1024, tk=512, name="d_attn_w_o")
    dattn = _mm(dz3, w["attn_w_o"], tb=True, out_dtype=BF16, tn=1024, name="d_attn")
    (dattn_h,) = _to_heads(dattn, 1)
    dq, dk, dv, dbias = _attn_bwd(q, kp, vp, bias, dattn_h)
    grads["attn_rel_bias"] = _bias_blocks_grad(dbias)
    dqkv = jnp.stack([dq, dk[:, KV_PAD:].astype(BF16), dv[:, KV_PAD:].astype(BF16)], axis=0)
    dqkv = dqkv.transpose(2, 0, 1, 3).reshape(S, 3 * D_MODEL)
    grads["attn_w_qkv"] = _mm(r2, dqkv, ta=True, tm=1024, tn=1024, tk=512, name="d_attn_w_qkv")
    dqkv = _tie(dqkv, emit("attn", grads))
    dr2 = _mm(dqkv, w["attn_w_qkv"], tb=True, add=dz3, add_scale=ALPHA, tn=1024, name="dr2")

    dz2, grads["ln_ffn_g"][0], grads["ln_ffn_b"][0], _ = _ln_bwd(sv0["z2"], w["ln_ffn_g"][0], w["ln_ffn_b"][0], dr2,
                                                                name="ln_ffn_bwd0")
    dr1 = _ffn_backward(sv0, dz2, p[0], w, 0, grads)
    dr1 = _tie(dr1, emit("ffn0", grads))
    dz1, grads["ln_mix_g"][0], grads["ln_mix_b"][0], _ = _ln_bwd(z1, w["ln_mix_g"][0], w["ln_mix_b"][0], dr1,
                                                                name="ln_mix_bwd0")
    grads["mix_w_out"] = _mm(ycat, dz1, ta=True, tm=1024, tn=1024, tk=512, name="d_mix_w_out")
    dycat = _mm(dz1, w["mix_w_out"], tb=True, tn=1024, name="d_ycat")
    du, g_pw, g_ps, g_cw, g_cb, g_cg, g_cbb = _mixer_bwd(u, dpool, dycat, w["pool_w"], w["pool_scale"],
                                                         w["conv_dw_w"], w["conv_dw_b"], w["conv_ln_g"],
                                                         w["conv_ln_b"])
    grads["mix_w_in"] = _mm(x, du, ta=True, tm=1024, tn=512, tk=512, name="d_mix_w_in")
    grads["conv_dw_w"] = g_cw
    du = _tie(du, emit("mix", grads))
    grad_x = _mm(du, w["mix_w_in"], tb=True, add=dz1, add_scale=ALPHA, tn=1024, tk=512, name="grad_x")
    grads.update(pool_w=g_pw, pool_scale=g_ps[0], conv_dw_w=g_cw, conv_dw_b=g_cb[0], conv_ln_g=g_cg[0],
                 conv_ln_b=g_cbb[0])
    for kname in ("ln_ffn_g", "ln_ffn_b", "ln_mix_g", "ln_mix_b"):
        grads[kname] = [a[0] for a in grads[kname]]
    return loss[0, 0], grad_x, grads


def _exchange(bufs, places, *, name):
    nb = len(bufs)

    def body(*refs):
        srcs, dsts = refs[:nb], refs[nb:2 * nb]
        send_sems, recv_sems, local_sems = refs[2 * nb:]
        x, y, c = lax.axis_index("x"), lax.axis_index("y"), lax.axis_index("c")
        me = 4 * x + 2 * y + c
        local = []
        remote = []
        for b in range(nb):
            pieces = places[b] == "pieces"
            shape = bufs[b].shape
            cp = pltpu.make_async_copy(srcs[b].at[me] if pieces else srcs[b], _slot(dsts[b], places[b], shape, me),
                                       local_sems.at[b])
            cp.start()
            local.append(cp)
            for d, dev, peer in _peers(x, y, c):
                src = srcs[b].at[peer] if pieces else srcs[b]
                out = pltpu.make_async_remote_copy(
                    src_ref=src, dst_ref=_slot(dsts[b], places[b], shape, me),
                    send_sem=send_sems.at[b * N_DEV + d], recv_sem=recv_sems.at[b * N_DEV + d],
                    device_id=dev, device_id_type=pl.DeviceIdType.MESH)
                out.start()
                inc = pltpu.make_async_remote_copy(
                    src_ref=src, dst_ref=_slot(dsts[b], places[b], shape, peer),
                    send_sem=send_sems.at[b * N_DEV + d], recv_sem=recv_sems.at[b * N_DEV + d],
                    device_id=dev, device_id_type=pl.DeviceIdType.MESH)
                remote.append((out, inc))
        for cp in local:
            cp.wait()
        for out, inc in remote:
            out.wait_send()
            inc.wait_recv()

    out_shapes = [jax.ShapeDtypeStruct(_result_shape(b, place), b.dtype) for b, place in zip(bufs, places)]
    any_spec = pl.BlockSpec(memory_space=pl.ANY)
    return pl.pallas_call(
        body,
        out_shape=out_shapes,
        in_specs=[any_spec] * nb,
        out_specs=[any_spec] * nb,
        scratch_shapes=[pltpu.SemaphoreType.DMA((nb * N_DEV,)), pltpu.SemaphoreType.DMA((nb * N_DEV,)),
                        pltpu.SemaphoreType.DMA((nb,))],
        compiler_params=pltpu.CompilerParams(has_side_effects=True),
        name=name,
    )(*bufs)


_HBM = pl.BlockSpec(memory_space=pltpu.HBM)
_SEM = pl.BlockSpec(memory_space=pltpu.SEMAPHORE)
_EFFECT = pltpu.SideEffectType.DATAFLOW_SIDE_EFFECTING


def _slot(ref, place, shape, k):
    if place in ("stack", "pieces"):
        return ref.at[k]
    ax = place[1]
    n = shape[ax]
    return ref.at[(slice(None),) * ax + (pl.ds(pl.multiple_of(k * n, n), n),)]


def _result_shape(buf, place):
    if place == "stack":
        return (N_DEV,) + buf.shape
    if place == "pieces":
        return buf.shape
    return tuple(s * N_DEV if i == place[1] else s for i, s in enumerate(buf.shape))


def _peers(x, y, c):
    for d in range(1, N_DEV):
        px, py, pc = x ^ ((d >> 2) & 1), y ^ ((d >> 1) & 1), c ^ (d & 1)
        yield d, (px, py, pc), 4 * px + 2 * py + pc


def _exchange_start(bufs, places, after, *, name):
    nb = len(bufs)
    lands = [lax.empty(_result_shape(b, p_), b.dtype) for b, p_ in zip(bufs, places)]
    has_after = after is not None

    def body(*refs):
        srcs, dsts = refs[:nb], refs[nb:2 * nb]
        outs = refs[2 * nb + has_after:]
        send_sems, recv_sems, token = outs[0], outs[1], outs[2 + 2 * nb]
        x, y, c = lax.axis_index("x"), lax.axis_index("y"), lax.axis_index("c")
        me = 4 * x + 2 * y + c
        for b in range(nb):
            for d, dev, peer in _peers(x, y, c):
                pltpu.make_async_remote_copy(
                    src_ref=srcs[b].at[peer] if places[b] == "pieces" else srcs[b],
                    dst_ref=_slot(dsts[b], places[b], bufs[b].shape, me),
                    send_sem=send_sems.at[b * N_DEV + d], recv_sem=recv_sems.at[b * N_DEV + d],
                    device_id=dev, device_id_type=pl.DeviceIdType.MESH).start()
            pltpu.make_async_copy(srcs[b].at[me] if places[b] == "pieces" else srcs[b],
                                  _slot(dsts[b], places[b], bufs[b].shape, me), recv_sems.at[b * N_DEV]).start()
        token[...] = jnp.zeros_like(token)

    sems = pltpu.SemaphoreType.DMA((nb * N_DEV,))
    ins = [pltpu.with_memory_space_constraint(a, pltpu.HBM) for a in list(bufs) + lands]
    out = pl.pallas_call(
        body,
        out_shape=(sems, sems, *[pltpu.HBM(a.shape, a.dtype) for a in ins], jax.ShapeDtypeStruct((8, 128), F32)),
        in_specs=[_HBM] * (2 * nb) + ([pl.BlockSpec(memory_space=pl.ANY)] if has_after else []),
        out_specs=(_SEM, _SEM, *[_HBM] * (2 * nb), pl.BlockSpec(memory_space=pltpu.VMEM)),
        input_output_aliases={i: 2 + i for i in range(2 * nb)},
        compiler_params=pltpu.CompilerParams(has_side_effects=_EFFECT),
        name=name,
    )(*ins, *([after] if has_after else []))
    return dict(send=out[0], recv=out[1], srcs=out[2:2 + nb], lands=out[2 + nb:2 + 2 * nb], token=out[-1],
                places=places)


def _exchange_wait(h, after, *, name):
    nb = len(h["srcs"])
    places = h["places"]
    shapes = [a.shape for a in h["srcs"]]

    def body(*refs):
        srcs, dsts, send_sems, recv_sems = refs[:nb], refs[nb:2 * nb], refs[2 * nb], refs[2 * nb + 1]
        x, y, c = lax.axis_index("x"), lax.axis_index("y"), lax.axis_index("c")
        me = 4 * x + 2 * y + c
        for b in range(nb):
            pieces = places[b] == "pieces"
            for d, dev, peer in _peers(x, y, c):
                cp = pltpu.make_async_remote_copy(
                    src_ref=srcs[b].at[peer] if pieces else srcs[b],
                    dst_ref=_slot(dsts[b], places[b], shapes[b], peer),
                    send_sem=send_sems.at[b * N_DEV + d], recv_sem=recv_sems.at[b * N_DEV + d],
                    device_id=dev, device_id_type=pl.DeviceIdType.MESH)
                cp.wait_send()
                cp.wait_recv()
            pltpu.make_async_copy(srcs[b].at[me] if pieces else srcs[b], _slot(dsts[b], places[b], shapes[b], me),
                                  recv_sems.at[b * N_DEV]).wait()

    ins = list(h["srcs"]) + list(h["lands"])
    out = pl.pallas_call(
        body,
        out_shape=tuple(pltpu.HBM(a.shape, a.dtype) for a in ins),
        in_specs=[_HBM] * (2 * nb) + [_SEM, _SEM, pl.BlockSpec(memory_space=pl.ANY)],
        out_specs=tuple([_HBM] * (2 * nb)),
        input_output_aliases={i: i for i in range(2 * nb)},
        compiler_params=pltpu.CompilerParams(has_side_effects=_EFFECT),
        name=name,
    )(*ins, h["send"], h["recv"], after)
    return out[nb:]


def _adamw(recv, w, m, v, *, name):
    R, C = w.shape
    tr = R
    for cand in (512, 256, 128, 64, 32, 16):
        if R % cand == 0 and cand * C * 4 <= 2 * 1024 * 1024:
            tr = cand
            break
    c1 = 1.0 - ADAM_B1 ** ADAM_STEP
    c2 = 1.0 - ADAM_B2 ** ADAM_STEP

    def body(r_ref, w_ref, m_ref, v_ref, g_ref, d_ref, mo_ref, vo_ref):
        g = r_ref[0].astype(F32)
        for i in range(1, N_DEV):
            g = g + r_ref[i].astype(F32)
        m_new = ADAM_B1 * m_ref[...] + (1.0 - ADAM_B1) * g
        v_new = ADAM_B2 * v_ref[...] + (1.0 - ADAM_B2) * (g * g)
        m_hat = m_new / c1
        v_hat = v_new / c2
        g_ref[...] = g
        d_ref[...] = -ADAM_LR * (m_hat / (jnp.sqrt(v_hat) + ADAM_EPS) + ADAM_WD * w_ref[...])
        mo_ref[...] = m_new
        vo_ref[...] = v_new

    row = pl.BlockSpec((tr, C), lambda i: (i, 0))
    return pl.pallas_call(
        body,
        out_shape=[jax.ShapeDtypeStruct((R, C), F32)] * 4,
        grid=(R // tr,),
        in_specs=[pl.BlockSpec((N_DEV, tr, C), lambda i: (0, i, 0)), row, row, row],
        out_specs=[row] * 4,
        compiler_params=_cparams(("parallel",)),
        name=name,
    )(recv, w, m, v)


def _ffn_groups(l):
    return ((f"up{l}", (("ffn_w_up", l, BF16, "stack"), ("ffn_dw_w", l, F32, "stack"))),
            (f"dn{l}", (("ffn_w_down", l, BF16, ("axis", 0)), ("ple_w_gate", l, BF16, ("axis", 0)),
                        ("ple_w_proj", l, BF16, ("axis", 1)))))


_GATHER_GROUPS = (
    ("mix", (("mix_w_in", 0, BF16, "stack"), ("conv_dw_w", 0, F32, "stack"), ("mix_w_out", 0, BF16, ("axis", 0)))),
    *_ffn_groups(0),
    ("attn", (("attn_w_qkv", 0, BF16, ("axis", 1)), ("attn_w_o", 0, BF16, ("axis", 0)))),
    *_ffn_groups(1))
_SHARDED = ("mix_w_in", "conv_dw_w", "mix_w_out", "attn_w_qkv", "attn_w_o", "ffn_w_up", "ffn_dw_w", "ffn_w_down",
            "ple_w_gate", "ple_w_proj")
_REPLICATED = ("pool_w", "pool_scale", "conv_dw_b", "conv_ln_g", "conv_ln_b", "attn_rel_bias", "ln_mix_g",
               "ln_mix_b", "ffn_dw_b", "ple_b_gate", "ln_ffn_g", "ln_ffn_b")


def _pack_rows(parts, row_mult, dtype):
    lead = parts[0].shape[:-1]
    flat = jnp.concatenate([a.astype(dtype) for a in parts], axis=-1)
    n = flat.shape[-1]
    unit = row_mult * LANES
    padded = -(-n // unit) * unit
    flat = jnp.pad(flat, [(0, 0)] * len(lead) + [(0, padded - n)])
    return flat.reshape(lead + (padded // LANES, LANES))


def _unpack(flat2d, shapes):
    flat = flat2d.reshape(-1)
    out, o = [], 0
    for s in shapes:
        n = math.prod(s)
        out.append(flat[o:o + n].reshape(s))
        o += n
    return out


def _full_from_shards(g, axis):
    parts = jnp.moveaxis(g, 0, axis)
    shp = list(g.shape[1:])
    shp[axis] *= g.shape[0]
    return parts.reshape(shp)


def _pieces_from_full(full, axis, k=N_DEV):
    shp = list(full.shape)
    n = shp[axis] // k
    t = full.reshape(shp[:axis] + [k, n] + shp[axis + 1:])
    return jnp.moveaxis(t, axis, 0)


def kernel(x, p, mix_w_in, pool_w, pool_scale, conv_dw_w, conv_dw_b, conv_ln_g, conv_ln_b, mix_w_out, attn_w_qkv, attn_rel_bias, attn_w_o, ln_mix_g, ln_mix_b, ffn_w_up, ffn_dw_w, ffn_dw_b, ffn_w_down, ple_w_proj, ple_w_gate, ple_b_gate, ln_ffn_g, ln_ffn_b, loss_target, m_mix_w_in, m_pool_w, m_pool_scale, m_conv_dw_w, m_conv_dw_b, m_conv_ln_g, m_conv_ln_b, m_mix_w_out, m_attn_w_qkv, m_attn_rel_bias, m_attn_w_o, m_ln_mix_g, m_ln_mix_b, m_ffn_w_up, m_ffn_dw_w, m_ffn_dw_b, m_ffn_w_down, m_ple_w_proj, m_ple_w_gate, m_ple_b_gate, m_ln_ffn_g, m_ln_ffn_b, v_mix_w_in, v_pool_w, v_pool_scale, v_conv_dw_w, v_conv_dw_b, v_conv_ln_g, v_conv_ln_b, v_mix_w_out, v_attn_w_qkv, v_attn_rel_bias, v_attn_w_o, v_ln_mix_g, v_ln_mix_b, v_ffn_w_up, v_ffn_dw_w, v_ffn_dw_b, v_ffn_w_down, v_ple_w_proj, v_ple_w_gate, v_ple_b_gate, v_ln_ffn_g, v_ln_ffn_b):
    a = dict(locals())
    sh_names = list(_SHARDED)
    names = sh_names + list(_REPLICATED)
    wts = {n: a[n] for n in names}
    mom = {n: a["m_" + n] for n in names}
    var = {n: a["v_" + n] for n in names}

    gather = {}
    token = None
    for group, items in _GATHER_GROUPS:
        gather[group] = _exchange_start([wts[n][l].astype(dt) for n, l, dt, _ in items], [pl_ for *_, pl_ in items],
                                        token, name="gather_start_" + group)
        token = gather[group]["token"]

    w = dict(pool_w=pool_w[0], pool_scale=pool_scale[0], conv_dw_b=conv_dw_b[0], conv_ln_g=conv_ln_g[0],
             conv_ln_b=conv_ln_b[0], attn_rel_bias=attn_rel_bias[0], ln_mix_g=ln_mix_g, ln_mix_b=ln_mix_b,
             ffn_dw_b=ffn_dw_b, ple_b_gate=ple_b_gate, ln_ffn_g=ln_ffn_g, ln_ffn_b=ln_ffn_b)
    for n in ("ffn_up_g", "ffn_up_v", "ffn_dw_w", "ffn_w_down", "ple_w_gate", "ple_w_proj"):
        w[n] = [None, None]

    def ready(group, after):
        got = _exchange_wait(gather[group], token if after is None else after, name="gather_wait_" + group)
        if group == "mix":
            w["mix_w_in"], w["conv_dw_w"] = _full_from_shards(got[0], 1), _full_from_shards(got[1], 1)
            w["mix_w_out"] = got[2]
        elif group == "attn":
            w["attn_w_qkv"], w["attn_w_o"] = got
        elif group[:2] == "up":
            l = int(group[2])
            w["ffn_up_g"][l] = _full_from_shards(got[0][:N_DEV // 2], 1)
            w["ffn_up_v"][l] = _full_from_shards(got[0][N_DEV // 2:], 1)
            w["ffn_dw_w"][l] = _full_from_shards(got[1], 1)
        else:
            l = int(group[2])
            w["ffn_w_down"][l], w["ple_w_gate"][l], w["ple_w_proj"][l] = got

    scatter = {}

    def emit(group, gr):
        if group[:3] == "ffn":
            l = int(group[3])
            pieces = [jnp.concatenate([_pieces_from_full(gr["ffn_up_g"][l], 1, N_DEV // 2),
                                       _pieces_from_full(gr["ffn_up_v"][l], 1, N_DEV // 2)]),
                      _pieces_from_full(gr["ffn_dw_w"][l], 1), _pieces_from_full(gr["ffn_w_down"][l], 0),
                      _pieces_from_full(gr["ple_w_gate"][l], 0), _pieces_from_full(gr["ple_w_proj"][l], 1)]
        elif group == "attn":
            pieces = [_pieces_from_full(gr["attn_w_qkv"], 1), _pieces_from_full(gr["attn_w_o"], 0)]
        else:
            pieces = [_pieces_from_full(gr["mix_w_in"], 1), _pieces_from_full(gr["conv_dw_w"], 1),
                      _pieces_from_full(gr["mix_w_out"], 0)]
        scatter[group] = _exchange_start([a.astype(BF16) for a in pieces], ["pieces"] * len(pieces), None,
                                         name="grad_start_" + group)
        return scatter[group]["token"]

    loss_part, grad_x, gr = _local_step(x[0], p[:, 0], loss_target[0], w, ready, emit)
    loss = lax.psum(loss_part, ("x", "y", "c"))

    recv = {}
    after = grad_x
    for group in ("ffn1", "attn", "ffn0", "mix"):
        recv[group] = _exchange_wait(scatter[group], after, name="grad_wait_" + group)
        after = recv[group][0]
    got = {"mix_w_in": [recv["mix"][0]], "conv_dw_w": [recv["mix"][1]], "mix_w_out": [recv["mix"][2]],
           "attn_w_qkv": [recv["attn"][0]], "attn_w_o": [recv["attn"][1]]}
    for i, n in enumerate(("ffn_w_up", "ffn_dw_w", "ffn_w_down", "ple_w_gate", "ple_w_proj")):
        got[n] = [recv["ffn0"][i], recv["ffn1"][i]]

    res = [{}, {}, {}, {}]
    for n in sh_names:
        outs_l = [_adamw(r, wts[n][l], mom[n][l], var[n][l], name=f"adamw_{n}{l}") for l, r in enumerate(got[n])]
        for k in range(4):
            res[k][n] = jnp.stack([o[k] for o in outs_l])

    gfull = dict(
        pool_w=gr["pool_w"][None], pool_scale=gr["pool_scale"][None], conv_dw_b=gr["conv_dw_b"][None],
        conv_ln_g=gr["conv_ln_g"][None], conv_ln_b=gr["conv_ln_b"][None], attn_rel_bias=gr["attn_rel_bias"][None],
        ln_mix_g=jnp.stack(gr["ln_mix_g"]), ln_mix_b=jnp.stack(gr["ln_mix_b"]), ffn_dw_b=jnp.stack(gr["ffn_dw_b"]),
        ple_b_gate=jnp.stack(gr["ple_b_gate"]), ln_ffn_g=jnp.stack(gr["ln_ffn_g"]),
        ln_ffn_b=jnp.stack(gr["ln_ffn_b"]))
    rep_send = _pack_rows([gfull[n].reshape(-1) for n in _REPLICATED], 8, F32)
    (rep_recv,) = _exchange([rep_send], ["stack"], name="grad_all_gather")

    def flat_state(d):
        return _pack_rows([d[n].reshape(-1) for n in _REPLICATED], 8, F32)

    rep_out = _adamw(rep_recv, flat_state(wts), flat_state(mom), flat_state(var), name="adamw_replicated")
    for k in range(4):
        for n, arr in zip(_REPLICATED, _unpack(rep_out[k], [wts[n].shape for n in _REPLICATED])):
            res[k][n] = arr
    order = ["mix_w_in", "pool_w", "pool_scale", "conv_dw_w", "conv_dw_b", "conv_ln_g", "conv_ln_b", "mix_w_out",
             "attn_w_qkv", "attn_rel_bias", "attn_w_o", "ln_mix_g", "ln_mix_b", "ffn_w_up", "ffn_dw_w", "ffn_dw_b",
             "ffn_w_down", "ple_w_proj", "ple_w_gate", "ple_b_gate", "ln_ffn_g", "ln_ffn_b"]
    outs = [loss, grad_x[None]]
    for k in range(4):
        outs += [res[k][n] for n in order]
    return tuple(outs)
```

```python
import functools
import math

import jax
import jax.numpy as jnp
from jax import lax
from jax.experimental import pallas as pl
from jax.experimental.pallas import tpu as pltpu

F32 = jnp.float32
BF16 = jnp.bfloat16

N_DEV = 8
D_MODEL = 1024
D_POOL = 512
D_CONV = 512
POOL_WINDOWS = (2, 4, 8, 16)
POOL_GROUP = 128
CONV_KERNEL = 31
CHUNK = 64
HEAD_DIM = 64
N_HEADS = 16
LEFT_CHUNKS = 8
BAND = (LEFT_CHUNKS + 1) * CHUNK
MAX_REL = 256
D_FF = 2816
PLE_DIM = 256
ALPHA = 4.0 ** 0.25
LN_EPS = 1e-5
NEG_INF = -1e30
ADAM_LR, ADAM_B1, ADAM_B2, ADAM_EPS, ADAM_WD, ADAM_STEP = 0.001, 0.9, 0.999, 1e-08, 0.01, 10

Q_BLOCK = 4 * CHUNK
KV_PAD = LEFT_CHUNKS * CHUNK
KV_SPAN = KV_PAD + Q_BLOCK
CONV_HALO = 32
FFN_HALO = 8
LANES = 1024
VMEM_LIMIT = 56 * 1024 * 1024


def _cparams(sem=None):
    return pltpu.CompilerParams(dimension_semantics=sem, vmem_limit_bytes=VMEM_LIMIT)


def _tile(dim, pref):
    if dim <= pref:
        return dim
    t = pref - pref % 128
    while t >= 128:
        if dim % t == 0:
            return t
        t -= 128
    return dim


def _sigmoid(x):
    return 1.0 / (1.0 + jnp.exp(-x))


def _bdot(a, b, dn=(((1,), (0,)), ((), ()))):
    return lax.dot_general(a.astype(BF16), b.astype(BF16), dn, preferred_element_type=F32)


NT = (((1,), (1,)), ((), ()))
TN = (((0,), (0,)), ((), ()))


def _mm(a, b, *, ta=False, tb=False, add=None, add_scale=1.0, out_dtype=F32, tm=512, tn=512, tk=1024, dep=None, name):
    if ta:
        K, M = a.shape
    else:
        M, K = a.shape
    if tb:
        N, kb = b.shape
    else:
        kb, N = b.shape
    assert K == kb, (a.shape, b.shape)
    tm, tn, tk = _tile(M, tm), _tile(N, tn), _tile(K, tk)
    nk = K // tk
    a_spec = pl.BlockSpec((tk, tm), lambda i, j, k: (k, i)) if ta else pl.BlockSpec((tm, tk), lambda i, j, k: (i, k))
    b_spec = pl.BlockSpec((tn, tk), lambda i, j, k: (j, k)) if tb else pl.BlockSpec((tk, tn), lambda i, j, k: (k, j))
    dn = (((0 if ta else 1,), (1 if tb else 0,)), ((), ()))
    has_add = add is not None

    def body(*refs):
        if dep is not None:
            refs = refs[:-3] + refs[-2:]
        if has_add:
            a_ref, b_ref, add_ref, o_ref, acc = refs
        else:
            a_ref, b_ref, o_ref, acc = refs
        k = pl.program_id(2)

        @pl.when(k == 0)
        def _():
            acc[...] = jnp.zeros_like(acc)

        acc[...] += _bdot(a_ref[...], b_ref[...], dn)

        @pl.when(k == nk - 1)
        def _():
            r = acc[...]
            if has_add:
                r = r + add_scale * add_ref[...]
            o_ref[...] = r.astype(out_dtype)

    in_specs = [a_spec, b_spec]
    args = [a, b]
    if has_add:
        in_specs.append(pl.BlockSpec((tm, tn), lambda i, j, k: (i, j)))
        args.append(add)
    if dep is not None:
        in_specs.append(pl.BlockSpec(memory_space=pl.ANY))
        args.append(dep)
    return pl.pallas_call(
        body,
        out_shape=jax.ShapeDtypeStruct((M, N), out_dtype),
        grid=(M // tm, N // tn, nk),
        in_specs=in_specs,
        out_specs=pl.BlockSpec((tm, tn), lambda i, j, k: (i, j)),
        scratch_shapes=[pltpu.VMEM((tm, tn), F32)],
        compiler_params=_cparams(("parallel", "parallel", "arbitrary")),
        name=name,
    )(*args)


def _layer_norm_rows(z, g, b):
    mu = jnp.mean(z, axis=-1, keepdims=True)
    zc = z - mu
    var = jnp.mean(zc * zc, axis=-1, keepdims=True)
    return zc * lax.rsqrt(var + LN_EPS) * g + b


def _proj_ln(res, a, w, ln_g, ln_b, *, ple=None, ts=256, name):
    S, D = res.shape
    ka = a.shape[1]
    has_ple = ple is not None
    row = lambda i: (i, 0)
    fix = lambda i: (0, 0)

    def body(*refs):
        if has_ple:
            res_ref, a_ref, w_ref, g_ref, b_ref, wg_ref, bg_ref, p_ref, wp_ref, z_ref, r_ref, gate_ref, proj_ref = refs
        else:
            res_ref, a_ref, w_ref, g_ref, b_ref, z_ref, r_ref = refs
        res_t = res_ref[...]
        acc = _bdot(a_ref[...], w_ref[...])
        if has_ple:
            gate = _sigmoid(_bdot(res_t, wg_ref[...]) + bg_ref[...])
            proj = _bdot(p_ref[...], wp_ref[...])
            gate_ref[...] = gate
            proj_ref[...] = proj
            acc = acc + gate * proj
        z = ALPHA * res_t + acc
        z_ref[...] = z
        r_ref[...] = _layer_norm_rows(z, g_ref[...], b_ref[...])

    in_specs = [pl.BlockSpec((ts, D), row), pl.BlockSpec((ts, ka), row), pl.BlockSpec((ka, D), fix),
                pl.BlockSpec((1, D), fix), pl.BlockSpec((1, D), fix)]
    args = [res, a, w, ln_g.reshape(1, D), ln_b.reshape(1, D)]
    n_out = 2
    if has_ple:
        wg, bg, p, wp = ple
        in_specs += [pl.BlockSpec((D, D), fix), pl.BlockSpec((1, D), fix), pl.BlockSpec((ts, PLE_DIM), row),
                     pl.BlockSpec((PLE_DIM, D), fix)]
        args += [wg, bg.reshape(1, D), p, wp]
        n_out = 4
    return pl.pallas_call(
        body,
        out_shape=[jax.ShapeDtypeStruct((S, D), F32)] * n_out,
        grid=(S // ts,),
        in_specs=in_specs,
        out_specs=[pl.BlockSpec((ts, D), row)] * n_out,
        compiler_params=_cparams(("parallel",)),
        name=name,
    )(*args)


def _mixer_fwd(u, pool_w, pool_scale, conv_w, conv_b, cln_g, cln_b, *, ts=256):
    S = u.shape[0]
    hb = CONV_HALO
    nh = ts // hb

    def body(u_ref, uh_ref, pw_ref, ps_ref, cw_ref, cb_ref, g_ref, b_ref, y_ref, d_ref, sta, stg):
        i = pl.program_id(0)
        first = i == 0
        sta[pl.ds(0, hb), :] = jnp.where(first, 0.0, uh_ref[:, 0:D_POOL])
        sta[pl.ds(hb, ts), :] = u_ref[:, 0:D_POOL]
        glu_h = uh_ref[:, D_POOL:D_POOL + D_CONV] * _sigmoid(uh_ref[:, D_POOL + D_CONV:])
        stg[pl.ds(0, hb), :] = jnp.where(first, 0.0, glu_h)
        stg[pl.ds(hb, ts), :] = u_ref[:, D_POOL:D_POOL + D_CONV] * _sigmoid(u_ref[:, D_POOL + D_CONV:])

        pos = (i * ts + lax.broadcasted_iota(jnp.int32, (ts, 1), 0) + 1).astype(F32)
        for g, w in enumerate(POOL_WINDOWS):
            lanes = pl.ds(g * POOL_GROUP, POOL_GROUP)
            a_g = sta[pl.ds(hb, ts), lanes]
            s = a_g
            for j in range(1, w):
                s = s + sta[pl.ds(hb - j, ts), lanes]
            d_g = s / jnp.minimum(pos, float(w)) - a_g
            d_ref[:, lanes] = d_g.astype(BF16)
            y_ref[:, lanes] = (_bdot(d_g, pw_ref[g]) * ps_ref[:, lanes]).astype(BF16)

        acc = jnp.zeros((ts, D_CONV), F32)
        for k in range(CONV_KERNEL):
            acc = acc + cw_ref[k:k + 1, :] * stg[pl.ds(hb - (CONV_KERNEL - 1) + k, ts), :]
        hc = acc + cb_ref[...]
        ln = _layer_norm_rows(hc, g_ref[...], b_ref[...])
        y_ref[:, D_POOL:] = (ln * _sigmoid(ln)).astype(BF16)

    fix2 = lambda i: (0, 0)
    return pl.pallas_call(
        body,
        out_shape=[jax.ShapeDtypeStruct((S, D_MODEL), BF16), jax.ShapeDtypeStruct((S, D_POOL), BF16)],
        grid=(S // ts,),
        in_specs=[pl.BlockSpec((ts, 3 * D_POOL), lambda i: (i, 0)),
                  pl.BlockSpec((hb, 3 * D_POOL), lambda i: (jnp.maximum(i * nh - 1, 0), 0)),
                  pl.BlockSpec((4, POOL_GROUP, POOL_GROUP), lambda i: (0, 0, 0)),
                  pl.BlockSpec((1, D_POOL), fix2), pl.BlockSpec((CONV_KERNEL, D_CONV), fix2),
                  pl.BlockSpec((1, D_CONV), fix2), pl.BlockSpec((1, D_CONV), fix2), pl.BlockSpec((1, D_CONV), fix2)],
        out_specs=[pl.BlockSpec((ts, D_MODEL), lambda i: (i, 0)), pl.BlockSpec((ts, D_POOL), lambda i: (i, 0))],
        scratch_shapes=[pltpu.VMEM((hb + ts, D_POOL), F32), pltpu.VMEM((hb + ts, D_CONV), F32)],
        compiler_params=_cparams(("parallel",)),
        name="mixer_fwd",
    )(u, u, pool_w, pool_scale.reshape(1, D_POOL), conv_w, conv_b.reshape(1, D_CONV), cln_g.reshape(1, D_CONV),
      cln_b.reshape(1, D_CONV))


def _mixer_bwd(u, d, dycat, pool_w, pool_scale, conv_w, conv_b, cln_g, cln_b, *, ts=256):
    S = u.shape[0]
    hb = CONV_HALO
    nh = ts // hb
    n = S // ts
    te = ts + hb
    K = CONV_KERNEL

    def body(u_ref, up_ref, un_ref, d_ref, dy_ref, dyn_ref, pw_ref, ps_ref, cw_ref, cb_ref, g_ref, b_ref,
             du_ref, dpw_ref, dps_ref, dcw_ref, dcb_ref, dg_ref, db_ref, stg, std, sth):
        i = pl.program_id(0)
        first = i == 0
        last = i == n - 1

        @pl.when(first)
        def _():
            dpw_ref[...] = jnp.zeros_like(dpw_ref)
            dps_ref[...] = jnp.zeros_like(dps_ref)
            dcw_ref[...] = jnp.zeros_like(dcw_ref)
            dcb_ref[...] = jnp.zeros_like(dcb_ref)
            dg_ref[...] = jnp.zeros_like(dg_ref)
            db_ref[...] = jnp.zeros_like(db_ref)

        pos_e = (i * ts + lax.broadcasted_iota(jnp.int32, (te, 1), 0) + 1).astype(F32)
        dya = dy_ref[:, 0:D_POOL]
        dya_n = jnp.where(last, 0.0, dyn_ref[:, 0:D_POOL])
        for g, w in enumerate(POOL_WINDOWS):
            lanes = pl.ds(g * POOL_GROUP, POOL_GROUP)
            sl = slice(g * POOL_GROUP, (g + 1) * POOL_GROUP)
            pw = pw_ref[g]
            scale = ps_ref[:, lanes]
            d_g = d_ref[:, lanes]
            pre = _bdot(d_g, pw)
            dps_ref[:, lanes] += jnp.sum(dya[:, sl] * pre, axis=0, keepdims=True)
            dys = dya[:, sl] * scale
            dpw_ref[g] += _bdot(d_g, dys, TN)
            dys_e = jnp.concatenate([dys, dya_n[:, sl] * scale], axis=0)
            dd = _bdot(dys_e, pw, NT)
            std[:, lanes] = dd / jnp.minimum(pos_e, float(w))
            da = -dd[0:ts]
            for m in range(w):
                da = da + std[pl.ds(m, ts), lanes]
            du_ref[:, lanes] = da.astype(BF16)

        glu_p = up_ref[:, D_POOL:D_POOL + D_CONV] * _sigmoid(up_ref[:, D_POOL + D_CONV:])
        stg[pl.ds(0, hb), :] = jnp.where(first, 0.0, glu_p)
        bv = u_ref[:, D_POOL:D_POOL + D_CONV]
        sg = _sigmoid(u_ref[:, D_POOL + D_CONV:])
        stg[pl.ds(hb, ts), :] = bv * sg
        glu_n = un_ref[:, D_POOL:D_POOL + D_CONV] * _sigmoid(un_ref[:, D_POOL + D_CONV:])
        stg[pl.ds(hb + ts, hb), :] = jnp.where(last, 0.0, glu_n)

        acc = jnp.zeros((te, D_CONV), F32)
        for k in range(K):
            acc = acc + cw_ref[k:k + 1, :] * stg[pl.ds(hb - (K - 1) + k, te), :]
        hc = acc + cb_ref[...]
        mu = jnp.mean(hc, axis=-1, keepdims=True)
        hcc = hc - mu
        rstd = lax.rsqrt(jnp.mean(hcc * hcc, axis=-1, keepdims=True) + LN_EPS)
        xh = hcc * rstd
        ln = xh * g_ref[...] + b_ref[...]
        sl_ = _sigmoid(ln)
        dyb = jnp.concatenate([dy_ref[:, D_POOL:], jnp.where(last, 0.0, dyn_ref[:, D_POOL:])], axis=0)
        dln = dyb * (sl_ * (1.0 + ln * (1.0 - sl_)))
        dxh = dln * g_ref[...]
        dhc = rstd * (dxh - jnp.mean(dxh, axis=-1, keepdims=True) - xh * jnp.mean(dxh * xh, axis=-1, keepdims=True))
        sth[...] = dhc
        dg_ref[...] += jnp.sum((dln * xh)[0:ts], axis=0, keepdims=True)
        db_ref[...] += jnp.sum(dln[0:ts], axis=0, keepdims=True)
        dhc_t = dhc[0:ts]
        dcb_ref[...] += jnp.sum(dhc_t, axis=0, keepdims=True)
        dglu = jnp.zeros((ts, D_CONV), F32)
        for k in range(K):
            dcw_ref[k:k + 1, :] += jnp.sum(dhc_t * stg[pl.ds(hb - (K - 1) + k, ts), :], axis=0, keepdims=True)
            dglu = dglu + cw_ref[k:k + 1, :] * sth[pl.ds(K - 1 - k, ts), :]
        du_ref[:, D_POOL:D_POOL + D_CONV] = (dglu * sg).astype(BF16)
        du_ref[:, D_POOL + D_CONV:] = (dglu * bv * sg * (1.0 - sg)).astype(BF16)

    fix2 = lambda i: (0, 0)
    prev = lambda i: (jnp.maximum(i * nh - 1, 0), 0)
    nxt = lambda i: (jnp.minimum((i + 1) * nh, S // hb - 1), 0)
    return pl.pallas_call(
        body,
        out_shape=[jax.ShapeDtypeStruct((S, 3 * D_POOL), BF16),
                   jax.ShapeDtypeStruct((4, POOL_GROUP, POOL_GROUP), F32),
                   jax.ShapeDtypeStruct((1, D_POOL), F32),
                   jax.ShapeDtypeStruct((K, D_CONV), F32),
                   jax.ShapeDtypeStruct((1, D_CONV), F32),
                   jax.ShapeDtypeStruct((1, D_CONV), F32),
                   jax.ShapeDtypeStruct((1, D_CONV), F32)],
        grid=(n,),
        in_specs=[pl.BlockSpec((ts, 3 * D_POOL), lambda i: (i, 0)),
                  pl.BlockSpec((hb, 3 * D_POOL), prev),
                  pl.BlockSpec((hb, 3 * D_POOL), nxt),
                  pl.BlockSpec((ts, D_POOL), lambda i: (i, 0)),
                  pl.BlockSpec((ts, D_MODEL), lambda i: (i, 0)),
                  pl.BlockSpec((hb, D_MODEL), nxt),
                  pl.BlockSpec((4, POOL_GROUP, POOL_GROUP), lambda i: (0, 0, 0)),
                  pl.BlockSpec((1, D_POOL), fix2), pl.BlockSpec((K, D_CONV), fix2),
                  pl.BlockSpec((1, D_CONV), fix2), pl.BlockSpec((1, D_CONV), fix2), pl.BlockSpec((1, D_CONV), fix2)],
        out_specs=[pl.BlockSpec((ts, 3 * D_POOL), lambda i: (i, 0)),
                   pl.BlockSpec((4, POOL_GROUP, POOL_GROUP), lambda i: (0, 0, 0)),
                   pl.BlockSpec((1, D_POOL), fix2), pl.BlockSpec((K, D_CONV), fix2),
                   pl.BlockSpec((1, D_CONV), fix2), pl.BlockSpec((1, D_CONV), fix2), pl.BlockSpec((1, D_CONV), fix2)],
        scratch_shapes=[pltpu.VMEM((hb + ts + hb, D_CONV), F32), pltpu.VMEM((te, D_POOL), F32),
                        pltpu.VMEM((te, D_CONV), F32)],
        compiler_params=_cparams(("arbitrary",)),
        name="mixer_bwd",
    )(u, u, u, d, dycat, dycat, pool_w, pool_scale.reshape(1, D_POOL), conv_w, conv_b.reshape(1, D_CONV),
      cln_g.reshape(1, D_CONV), cln_b.reshape(1, D_CONV))


_GELU_C = math.sqrt(2.0 / math.pi)


def _gelu_parts(x):
    inner = _GELU_C * (x + 0.044715 * x * x * x)
    th = jnp.tanh(inner)
    ge = 0.5 * x * (1.0 + th)
    dge = 0.5 * (1.0 + th) + 0.5 * x * (1.0 - th * th) * (_GELU_C * (1.0 + 3.0 * 0.044715 * x * x))
    return ge, dge


def _ffn_act_fwd(gate, val, dw_w, dw_b, *, ts=256, tc=1408, name):
    S, F = gate.shape
    hb = FFN_HALO
    nh = ts // hb
    tc = _tile(F, tc)

    def body(g_ref, gh_ref, v_ref, w_ref, b_ref, h_ref, st):
        i = pl.program_id(0)
        st[pl.ds(0, hb), :] = jnp.where(i == 0, 0.0, gh_ref[...])
        st[pl.ds(hb, ts), :] = g_ref[...]
        gc = b_ref[...] + w_ref[0:1, :] * st[pl.ds(hb - 2, ts), :] + w_ref[1:2, :] * st[pl.ds(hb - 1, ts), :] \
            + w_ref[2:3, :] * st[pl.ds(hb, ts), :]
        ge, _ = _gelu_parts(gc)
        h_ref[...] = (ge * v_ref[...]).astype(BF16)

    return pl.pallas_call(
        body,
        out_shape=jax.ShapeDtypeStruct((S, F), BF16),
        grid=(S // ts, F // tc),
        in_specs=[pl.BlockSpec((ts, tc), lambda i, j: (i, j)),
                  pl.BlockSpec((hb, tc), lambda i, j: (jnp.maximum(i * nh - 1, 0), j)),
                  pl.BlockSpec((ts, tc), lambda i, j: (i, j)),
                  pl.BlockSpec((3, tc), lambda i, j: (0, j)),
                  pl.BlockSpec((1, tc), lambda i, j: (0, j))],
        out_specs=pl.BlockSpec((ts, tc), lambda i, j: (i, j)),
        scratch_shapes=[pltpu.VMEM((hb + ts, tc), F32)],
        compiler_params=_cparams(("parallel", "parallel")),
        name=name,
    )(gate, gate, val, dw_w, dw_b.reshape(1, F))


def _ffn_act_bwd(gate, val, dh, dw_w, dw_b, *, ts=256, tc=1408, name):
    S, F = gate.shape
    hb = FFN_HALO
    nh = ts // hb
    n = S // ts
    te = ts + hb
    tc = _tile(F, tc)

    def body(g_ref, gp_ref, gn_ref, v_ref, vn_ref, dh_ref, dhn_ref, w_ref, b_ref,
             dg_ref, dv_ref, dw_ref, db_ref, st, sd):
        i = pl.program_id(1)
        first = i == 0
        last = i == n - 1

        @pl.when(first)
        def _():
            dw_ref[...] = jnp.zeros_like(dw_ref)
            db_ref[...] = jnp.zeros_like(db_ref)

        st[pl.ds(0, hb), :] = jnp.where(first, 0.0, gp_ref[...])
        st[pl.ds(hb, ts), :] = g_ref[...]
        st[pl.ds(hb + ts, hb), :] = jnp.where(last, 0.0, gn_ref[...])
        gc = b_ref[...] + w_ref[0:1, :] * st[pl.ds(hb - 2, te), :] + w_ref[1:2, :] * st[pl.ds(hb - 1, te), :] \
            + w_ref[2:3, :] * st[pl.ds(hb, te), :]
        ge, dge = _gelu_parts(gc)
        val_e = jnp.concatenate([v_ref[...], jnp.where(last, 0.0, vn_ref[...])], axis=0)
        dh_e = jnp.concatenate([dh_ref[...], jnp.where(last, 0.0, dhn_ref[...])], axis=0)
        dgc = dh_e * val_e * dge
        sd[...] = dgc
        dv_ref[...] = (dh_e[0:ts] * ge[0:ts]).astype(BF16)
        dgc_t = dgc[0:ts]
        db_ref[...] += jnp.sum(dgc_t, axis=0, keepdims=True)
        dgate = jnp.zeros((ts, tc), F32)
        for k in range(3):
            dw_ref[k:k + 1, :] += jnp.sum(dgc_t * st[pl.ds(hb - 2 + k, ts), :], axis=0, keepdims=True)
            dgate = dgate + w_ref[k:k + 1, :] * sd[pl.ds(2 - k, ts), :]
        dg_ref[...] = dgate.astype(BF16)

    cur = lambda j, i: (i, j)
    prev = lambda j, i: (jnp.maximum(i * nh - 1, 0), j)
    nxt = lambda j, i: (jnp.minimum((i + 1) * nh, S // hb - 1), j)
    return pl.pallas_call(
        body,
        out_shape=[jax.ShapeDtypeStruct((S, F), BF16), jax.ShapeDtypeStruct((S, F), BF16),
                   jax.ShapeDtypeStruct((3, F), F32), jax.ShapeDtypeStruct((1, F), F32)],
        grid=(F // tc, n),
        in_specs=[pl.BlockSpec((ts, tc), cur), pl.BlockSpec((hb, tc), prev), pl.BlockSpec((hb, tc), nxt),
                  pl.BlockSpec((ts, tc), cur), pl.BlockSpec((hb, tc), nxt),
                  pl.BlockSpec((ts, tc), cur), pl.BlockSpec((hb, tc), nxt),
                  pl.BlockSpec((3, tc), lambda j, i: (0, j)), pl.BlockSpec((1, tc), lambda j, i: (0, j))],
        out_specs=[pl.BlockSpec((ts, tc), cur), pl.BlockSpec((ts, tc), cur),
                   pl.BlockSpec((3, tc), lambda j, i: (0, j)), pl.BlockSpec((1, tc), lambda j, i: (0, j))],
        scratch_shapes=[pltpu.VMEM((hb + ts + hb, tc), F32), pltpu.VMEM((te, tc), F32)],
        compiler_params=_cparams(("parallel", "arbitrary")),
        name=name,
    )(gate, gate, gate, val, val, dh, dh, dw_w, dw_b.reshape(1, F))


def _ln_bwd(z, ln_g, ln_b, dout, *, loss_head=False, ts=256, dep=None, name):
    S, D = z.shape

    def body(z_ref, g_ref, b_ref, do_ref, *rest):
        dz_ref, dg_ref, db_ref, loss_ref = rest[-4:]
        i = pl.program_id(0)

        @pl.when(i == 0)
        def _():
            dg_ref[...] = jnp.zeros_like(dg_ref)
            db_ref[...] = jnp.zeros_like(db_ref)
            loss_ref[...] = jnp.zeros_like(loss_ref)

        zt = z_ref[...]
        mu = jnp.mean(zt, axis=-1, keepdims=True)
        zc = zt - mu
        rstd = lax.rsqrt(jnp.mean(zc * zc, axis=-1, keepdims=True) + LN_EPS)
        xh = zc * rstd
        if loss_head:
            err = xh * g_ref[...] + b_ref[...] - do_ref[...]
            loss_ref[...] += 0.5 * jnp.sum(jnp.mean(err * err, axis=-1, keepdims=True))
            do = err * (1.0 / D)
        else:
            do = do_ref[...]
        dg_ref[...] += jnp.sum(do * xh, axis=0, keepdims=True)
        db_ref[...] += jnp.sum(do, axis=0, keepdims=True)
        dxh = do * g_ref[...]
        dz_ref[...] = rstd * (dxh - jnp.mean(dxh, axis=-1, keepdims=True)
                              - xh * jnp.mean(dxh * xh, axis=-1, keepdims=True))

    row = lambda i: (i, 0)
    fix = lambda i: (0, 0)
    return pl.pallas_call(
        body,
        out_shape=[jax.ShapeDtypeStruct((S, D), F32), jax.ShapeDtypeStruct((1, D), F32),
                   jax.ShapeDtypeStruct((1, D), F32), jax.ShapeDtypeStruct((8, 128), F32)],
        grid=(S // ts,),
        in_specs=[pl.BlockSpec((ts, D), row), pl.BlockSpec((1, D), fix), pl.BlockSpec((1, D), fix),
                  pl.BlockSpec((ts, D), row)] + ([pl.BlockSpec(memory_space=pl.ANY)] if dep is not None else []),
        out_specs=[pl.BlockSpec((ts, D), row), pl.BlockSpec((1, D), fix), pl.BlockSpec((1, D), fix),
                   pl.BlockSpec((8, 128), fix)],
        compiler_params=_cparams(("arbitrary",)),
        name=name,
    )(z, ln_g.reshape(1, D), ln_b.reshape(1, D), dout, *([dep] if dep is not None else []))


def _ple_bwd(dz, gate, proj, *, ts=256, name):
    S, D = dz.shape

    def body(dz_ref, g_ref, p_ref, ds_ref, dp_ref, db_ref):
        @pl.when(pl.program_id(0) == 0)
        def _():
            db_ref[...] = jnp.zeros_like(db_ref)

        dzt = dz_ref[...]
        g = g_ref[...]
        ds = dzt * p_ref[...] * g * (1.0 - g)
        ds_ref[...] = ds.astype(BF16)
        dp_ref[...] = (dzt * g).astype(BF16)
        db_ref[...] += jnp.sum(ds, axis=0, keepdims=True)

    row = lambda i: (i, 0)
    return pl.pallas_call(
        body,
        out_shape=[jax.ShapeDtypeStruct((S, D), BF16), jax.ShapeDtypeStruct((S, D), BF16),
                   jax.ShapeDtypeStruct((1, D), F32)],
        grid=(S // ts,),
        in_specs=[pl.BlockSpec((ts, D), row)] * 3,
        out_specs=[pl.BlockSpec((ts, D), row), pl.BlockSpec((ts, D), row), pl.BlockSpec((1, D), lambda i: (0, 0))],
        compiler_params=_cparams(("arbitrary",)),
        name=name,
    )(dz, gate, proj)


def _attn_scores(q, kc, bias, qb):
    s = _bdot(q, kc, NT) * (HEAD_DIM ** -0.5) + bias
    kpos = qb * Q_BLOCK + lax.broadcasted_iota(jnp.int32, (1, KV_SPAN), 1)
    s = jnp.where(kpos >= KV_PAD, s, NEG_INF)
    m = jnp.max(s, axis=-1, keepdims=True)
    e = jnp.exp(s - m)
    return e / jnp.sum(e, axis=-1, keepdims=True)


def _attn_fwd(q, kp, vp, bias):
    H, S, dh = q.shape
    sp = kp.shape[1]

    def body(q_ref, k_ref, v_ref, b_ref, o_ref):
        qb = pl.program_id(1)
        start = pl.multiple_of(qb * Q_BLOCK, Q_BLOCK)
        kc = k_ref[pl.ds(start, KV_SPAN), :]
        vc = v_ref[pl.ds(start, KV_SPAN), :]
        p = _attn_scores(q_ref[...], kc, b_ref[...], qb)
        o_ref[...] = _bdot(p, vc).astype(BF16)

    return pl.pallas_call(
        body,
        out_shape=jax.ShapeDtypeStruct((H, S, dh), BF16),
        grid=(H, S // Q_BLOCK),
        in_specs=[pl.BlockSpec((None, Q_BLOCK, dh), lambda h, i: (h, i, 0)),
                  pl.BlockSpec((None, sp, dh), lambda h, i: (h, 0, 0)),
                  pl.BlockSpec((None, sp, dh), lambda h, i: (h, 0, 0)),
                  pl.BlockSpec((None, Q_BLOCK, KV_SPAN), lambda h, i: (h, 0, 0))],
        out_specs=pl.BlockSpec((None, Q_BLOCK, dh), lambda h, i: (h, i, 0)),
        compiler_params=_cparams(("parallel", "arbitrary")),
        name="attn_fwd",
    )(q, kp, vp, bias)


def _attn_bwd(q, kp, vp, bias, do):
    H, S, dh = q.shape
    sp = kp.shape[1]
    scale = HEAD_DIM ** -0.5

    def body(q_ref, k_ref, v_ref, b_ref, do_ref, dq_ref, dk_ref, dv_ref, db_ref):
        qb = pl.program_id(1)

        @pl.when(qb == 0)
        def _():
            dk_ref[...] = jnp.zeros_like(dk_ref)
            dv_ref[...] = jnp.zeros_like(dv_ref)
            db_ref[...] = jnp.zeros_like(db_ref)

        start = pl.multiple_of(qb * Q_BLOCK, Q_BLOCK)
        span = pl.ds(start, KV_SPAN)
        qt = q_ref[...]
        kc = k_ref[span, :]
        vc = v_ref[span, :]
        dot = do_ref[...]
        p = _attn_scores(qt, kc, b_ref[...], qb)
        dv_ref[span, :] += _bdot(p, dot, TN)
        dp = _bdot(dot, vc, NT)
        ds = p * (dp - jnp.sum(p * dp, axis=-1, keepdims=True))
        db_ref[...] += ds
        dq_ref[...] = (scale * _bdot(ds, kc)).astype(BF16)
        dk_ref[span, :] += scale * _bdot(ds, qt, TN)

    blk = pl.BlockSpec((None, Q_BLOCK, dh), lambda h, i: (h, i, 0))
    full = pl.BlockSpec((None, sp, dh), lambda h, i: (h, 0, 0))
    bsp = pl.BlockSpec((None, Q_BLOCK, KV_SPAN), lambda h, i: (h, 0, 0))
    return pl.pallas_call(
        body,
        out_shape=[jax.ShapeDtypeStruct((H, S, dh), BF16), jax.ShapeDtypeStruct((H, sp, dh), F32),
                   jax.ShapeDtypeStruct((H, sp, dh), F32), jax.ShapeDtypeStruct((H, Q_BLOCK, KV_SPAN), F32)],
        grid=(H, S // Q_BLOCK),
        in_specs=[blk, full, full, bsp, blk],
        out_specs=[blk, full, full, bsp],
        compiler_params=_cparams(("parallel", "arbitrary")),
        name="attn_bwd",
    )(q, kp, vp, bias, do)


def _bias_blocks(rel_bias):
    H = rel_bias.shape[0]
    n_e = BAND + CHUNK - 1
    n_clip = KV_PAD + CHUNK - 1 - MAX_REL + 1
    e = jnp.concatenate([jnp.broadcast_to(rel_bias[:, 2 * MAX_REL:], (H, n_clip)),
                         jnp.flip(rel_bias[:, 2 * MAX_REL - (n_e - n_clip):2 * MAX_REL], axis=1)], axis=1)
    skew = jnp.pad(jnp.tile(e, (1, CHUNK)), ((0, 0), (0, CHUNK))).reshape(H, CHUNK, n_e + 1)
    band = jnp.flip(skew, axis=1)[:, :, :BAND]
    rows = [jnp.pad(band, ((0, 0), (0, 0), (c * CHUNK, KV_SPAN - BAND - c * CHUNK)), constant_values=NEG_INF)
            for c in range(Q_BLOCK // CHUNK)]
    return jnp.concatenate(rows, axis=1)


def _bias_blocks_grad(dblk):
    H = dblk.shape[0]
    n_e = BAND + CHUNK - 1
    n_clip = KV_PAD + CHUNK - 1 - MAX_REL + 1
    parts = jnp.stack([dblk[:, c * CHUNK:(c + 1) * CHUNK, c * CHUNK:c * CHUNK + BAND]
                       for c in range(Q_BLOCK // CHUNK)], axis=1)
    parts = jnp.flip(parts, axis=2)
    parts = jnp.pad(parts, ((0, 0), (0, 0), (0, 0), (0, n_e + 1 - BAND)))
    skew = parts.reshape(H, Q_BLOCK // CHUNK, CHUNK * (n_e + 1))[:, :, :CHUNK * n_e]
    skew = skew.reshape(H, Q_BLOCK, n_e)
    skew = jnp.pad(skew, ((0, 0), (0, 0), (0, 1)))

    def body(s_ref, o_ref):
        de = jnp.sum(s_ref[...], axis=0, keepdims=True)
        lane = lax.broadcasted_iota(jnp.int32, de.shape, 1)
        far = jnp.sum(jnp.where(lane < n_clip, de, 0.0), axis=-1, keepdims=True)
        o_ref[...] = jnp.where(lane == 0, far, jnp.where(lane < n_clip, 0.0, de))

    de = pl.pallas_call(
        body,
        out_shape=jax.ShapeDtypeStruct((H, 1, n_e + 1), F32),
        grid=(H,),
        in_specs=[pl.BlockSpec((None, Q_BLOCK, n_e + 1), lambda h: (h, 0, 0))],
        out_specs=pl.BlockSpec((None, 1, n_e + 1), lambda h: (h, 0, 0)),
        compiler_params=_cparams(("parallel",)),
        name="bias_grad_sum",
    )(skew).reshape(H, n_e + 1)
    near = jnp.flip(de[:, n_clip:n_e], axis=1)
    return jnp.concatenate([jnp.zeros((H, 2 * MAX_REL - (n_e - n_clip)), F32), near, de[:, 0:1]], axis=1)


def _ffn_forward(r1, p_l, w, l, ready):
    ready(f"up{l}", r1)
    up_g = _mm(r1, w["ffn_up_g"][l], tn=1408, name=f"ffn_up_g{l}")
    up_v = _mm(r1, w["ffn_up_v"][l], tn=1408, name=f"ffn_up_v{l}")
    h = _ffn_act_fwd(up_g, up_v, w["ffn_dw_w"][l], w["ffn_dw_b"][l], name=f"ffn_act{l}")
    ready(f"dn{l}", h)
    z2, r2, gate, proj = _proj_ln(r1, h, w["ffn_w_down"][l], w["ln_ffn_g"][l], w["ln_ffn_b"][l],
                                  ple=(w["ple_w_gate"][l], w["ple_b_gate"][l], p_l, w["ple_w_proj"][l]),
                                  name=f"ffn_down_ln{l}")
    return dict(r1=r1, up_g=up_g, up_v=up_v, h=h, z2=z2, gate=gate, proj=proj), r2


def _ffn_backward(sv, dz2, p_l, w, l, grads):
    r1 = sv["r1"]
    ds, dproj, db_gate = _ple_bwd(dz2, sv["gate"], sv["proj"], name=f"ple_bwd{l}")
    dh = _mm(dz2, w["ffn_w_down"][l], tb=True, tn=1408, name=f"ffn_dh{l}")
    dgate, dval, d_dw_w, d_dw_b = _ffn_act_bwd(sv["up_g"], sv["up_v"], dh, w["ffn_dw_w"][l], w["ffn_dw_b"][l],
                                               name=f"ffn_act_bwd{l}")
    grads["ffn_w_down"][l] = _mm(sv["h"], dz2, ta=True, tm=1408, tn=1024, tk=512, name=f"d_ffn_w_down{l}")
    grads["ffn_up_g"][l] = _mm(r1, dgate, ta=True, tm=1024, tn=1408, tk=512, name=f"d_ffn_up_g{l}")
    grads["ffn_up_v"][l] = _mm(r1, dval, ta=True, tm=1024, tn=1408, tk=512, name=f"d_ffn_up_v{l}")
    grads["ple_w_gate"][l] = _mm(r1, ds, ta=True, tm=1024, tn=1024, tk=512, name=f"d_ple_w_gate{l}")
    grads["ple_w_proj"][l] = _mm(p_l, dproj, ta=True, tm=256, tn=1024, tk=512, name=f"d_ple_w_proj{l}")
    grads["ffn_dw_w"][l] = d_dw_w
    grads["ffn_dw_b"][l] = d_dw_b[0]
    grads["ple_b_gate"][l] = db_gate[0]
    t = _mm(ds, w["ple_w_gate"][l], tb=True, add=dz2, add_scale=ALPHA, tn=1024, name=f"dr1_gate{l}")
    t = _mm(dgate, w["ffn_up_g"][l], tb=True, add=t, tn=1024, tk=1408, name=f"dr1_up_g{l}")
    return _mm(dval, w["ffn_up_v"][l], tb=True, add=t, tn=1024, tk=1408, name=f"dr1_up_v{l}")


def _to_heads(a, n):
    S = a.shape[0]
    t = a.reshape(S, n, N_HEADS, HEAD_DIM).transpose(1, 2, 0, 3)
    return [t[i] for i in range(n)]


def _local_step(x, p, target, w, ready=lambda group, after: None, emit=lambda group, grads: None):
    S = x.shape[0]
    grads = {k: [None, None] for k in ("ffn_w_down", "ffn_up_g", "ffn_up_v", "ple_w_gate", "ple_w_proj", "ffn_dw_w",
                                       "ffn_dw_b", "ple_b_gate", "ln_ffn_g", "ln_ffn_b", "ln_mix_g", "ln_mix_b")}

    ready("mix", None)
    u = _mm(x, w["mix_w_in"], name="mix_in")
    ycat, dpool = _mixer_fwd(u, w["pool_w"], w["pool_scale"], w["conv_dw_w"], w["conv_dw_b"], w["conv_ln_g"],
                             w["conv_ln_b"])
    z1, r1 = _proj_ln(x, ycat, w["mix_w_out"], w["ln_mix_g"][0], w["ln_mix_b"][0], name="mix_out_ln")
    sv0, r2 = _ffn_forward(r1, p[0], w, 0, ready)

    ready("attn", r2)
    qkv = _mm(r2, w["attn_w_qkv"], out_dtype=BF16, name="attn_qkv")
    q, k, v = _to_heads(qkv, 3)
    kp = jnp.pad(k, ((0, 0), (KV_PAD, 0), (0, 0)))
    vp = jnp.pad(v, ((0, 0), (KV_PAD, 0), (0, 0)))
    bias = _bias_blocks(w["attn_rel_bias"])
    attn_h = _attn_fwd(q, kp, vp, bias)
    attn = attn_h.transpose(1, 0, 2).reshape(S, D_MODEL)
    z3, r3 = _proj_ln(r2, attn, w["attn_w_o"], w["ln_mix_g"][1], w["ln_mix_b"][1], name="attn_out_ln")
    sv1, _ = _ffn_forward(r3, p[1], w, 1, ready)

    dz4, grads["ln_ffn_g"][1], grads["ln_ffn_b"][1], loss = _ln_bwd(sv1["z2"], w["ln_ffn_g"][1], w["ln_ffn_b"][1],
                                                                    target, loss_head=True, name="loss_ln_bwd")
    dr3 = _ffn_backward(sv1, dz4, p[1], w, 1, grads)
    dz3, grads["ln_mix_g"][1], grads["ln_mix_b"][1], _ = _ln_bwd(z3, w["ln_mix_g"][1], w["ln_mix_b"][1], dr3,
                                                                dep=emit("ffn1", grads), name="ln_mix_bwd1")
    grads["attn_w_o"] = _mm(attn, dz3, ta=True, tm=1024, tn=1024, tk=512, name="d_attn_w_o")
    dattn = _mm(dz3, w["attn_w_o"], tb=True, out_dtype=BF16, tn=1024, name="d_attn")
    (dattn_h,) = _to_heads(dattn, 1)
    dq, dk, dv, dbias = _attn_bwd(q, kp, vp, bias, dattn_h)
    grads["attn_rel_bias"] = _bias_blocks_grad(dbias)
    dqkv = jnp.stack([dq, dk[:, KV_PAD:].astype(BF16), dv[:, KV_PAD:].astype(BF16)], axis=0)
    dqkv = dqkv.transpose(2, 0, 1, 3).reshape(S, 3 * D_MODEL)
    grads["attn_w_qkv"] = _mm(r2, dqkv, ta=True, tm=1024, tn=1024, tk=512, name="d_attn_w_qkv")
    dr2 = _mm(dqkv, w["attn_w_qkv"], tb=True, add=dz3, add_scale=ALPHA, tn=1024, dep=emit("attn", grads), name="dr2")

    dz2, grads["ln_ffn_g"][0], grads["ln_ffn_b"][0], _ = _ln_bwd(sv0["z2"], w["ln_ffn_g"][0], w["ln_ffn_b"][0], dr2,
                                                                name="ln_ffn_bwd0")
    dr1 = _ffn_backward(sv0, dz2, p[0], w, 0, grads)
    dz1, grads["ln_mix_g"][0], grads["ln_mix_b"][0], _ = _ln_bwd(z1, w["ln_mix_g"][0], w["ln_mix_b"][0], dr1,
                                                                dep=emit("ffn0", grads), name="ln_mix_bwd0")
    grads["mix_w_out"] = _mm(ycat, dz1, ta=True, tm=1024, tn=1024, tk=512, name="d_mix_w_out")
    dycat = _mm(dz1, w["mix_w_out"], tb=True, tn=1024, name="d_ycat")
    du, g_pw, g_ps, g_cw, g_cb, g_cg, g_cbb = _mixer_bwd(u, dpool, dycat, w["pool_w"], w["pool_scale"],
                                                         w["conv_dw_w"], w["conv_dw_b"], w["conv_ln_g"],
                                                         w["conv_ln_b"])
    grads["mix_w_in"] = _mm(x, du, ta=True, tm=1024, tn=512, tk=512, name="d_mix_w_in")
    grads["conv_dw_w"] = g_cw
    grad_x = _mm(du, w["mix_w_in"], tb=True, add=dz1, add_scale=ALPHA, tn=1024, tk=512, dep=emit("mix", grads),
                 name="grad_x")
    grads.update(pool_w=g_pw, pool_scale=g_ps[0], conv_dw_w=g_cw, conv_dw_b=g_cb[0], conv_ln_g=g_cg[0],
                 conv_ln_b=g_cbb[0])
    for kname in ("ln_ffn_g", "ln_ffn_b", "ln_mix_g", "ln_mix_b"):
        grads[kname] = [a[0] for a in grads[kname]]
    return loss[0, 0], grad_x, grads


def _exchange(bufs, places, *, name):
    nb = len(bufs)

    def body(*refs):
        srcs, dsts = refs[:nb], refs[nb:2 * nb]
        send_sems, recv_sems, local_sems = refs[2 * nb:]
        x, y, c = lax.axis_index("x"), lax.axis_index("y"), lax.axis_index("c")
        me = 4 * x + 2 * y + c
        local = []
        remote = []
        for b in range(nb):
            pieces = places[b] == "pieces"
            shape = bufs[b].shape
            cp = pltpu.make_async_copy(srcs[b].at[me] if pieces else srcs[b], _slot(dsts[b], places[b], shape, me),
                                       local_sems.at[b])
            cp.start()
            local.append(cp)
            for d, dev, peer in _peers(x, y, c):
                src = srcs[b].at[peer] if pieces else srcs[b]
                out = pltpu.make_async_remote_copy(
                    src_ref=src, dst_ref=_slot(dsts[b], places[b], shape, me),
                    send_sem=send_sems.at[b * N_DEV + d], recv_sem=recv_sems.at[b * N_DEV + d],
                    device_id=dev, device_id_type=pl.DeviceIdType.MESH)
                out.start()
                inc = pltpu.make_async_remote_copy(
                    src_ref=src, dst_ref=_slot(dsts[b], places[b], shape, peer),
                    send_sem=send_sems.at[b * N_DEV + d], recv_sem=recv_sems.at[b * N_DEV + d],
                    device_id=dev, device_id_type=pl.DeviceIdType.MESH)
                remote.append((out, inc))
        for cp in local:
            cp.wait()
        for out, inc in remote:
            out.wait_send()
            inc.wait_recv()

    out_shapes = [jax.ShapeDtypeStruct(_result_shape(b, place), b.dtype) for b, place in zip(bufs, places)]
    any_spec = pl.BlockSpec(memory_space=pl.ANY)
    return pl.pallas_call(
        body,
        out_shape=out_shapes,
        in_specs=[any_spec] * nb,
        out_specs=[any_spec] * nb,
        scratch_shapes=[pltpu.SemaphoreType.DMA((nb * N_DEV,)), pltpu.SemaphoreType.DMA((nb * N_DEV,)),
                        pltpu.SemaphoreType.DMA((nb,))],
        compiler_params=pltpu.CompilerParams(has_side_effects=True),
        name=name,
    )(*bufs)


_HBM = pl.BlockSpec(memory_space=pltpu.HBM)
_SEM = pl.BlockSpec(memory_space=pltpu.SEMAPHORE)
_EFFECT = pltpu.SideEffectType.DATAFLOW_SIDE_EFFECTING


def _slot(ref, place, shape, k):
    if place in ("stack", "pieces"):
        return ref.at[k]
    ax = place[1]
    n = shape[ax]
    return ref.at[(slice(None),) * ax + (pl.ds(pl.multiple_of(k * n, n), n),)]


def _result_shape(buf, place):
    if place == "stack":
        return (N_DEV,) + buf.shape
    if place == "pieces":
        return buf.shape
    return tuple(s * N_DEV if i == place[1] else s for i, s in enumerate(buf.shape))


def _peers(x, y, c):
    for d in range(1, N_DEV):
        px, py, pc = x ^ ((d >> 2) & 1), y ^ ((d >> 1) & 1), c ^ (d & 1)
        yield d, (px, py, pc), 4 * px + 2 * py + pc


def _exchange_start(bufs, places, after, *, name):
    nb = len(bufs)
    lands = [lax.empty(_result_shape(b, p_), b.dtype) for b, p_ in zip(bufs, places)]
    has_after = after is not None

    def body(*refs):
        srcs, dsts = refs[:nb], refs[nb:2 * nb]
        outs = refs[2 * nb + has_after:]
        send_sems, recv_sems, token = outs[0], outs[1], outs[2 + 2 * nb]
        x, y, c = lax.axis_index("x"), lax.axis_index("y"), lax.axis_index("c")
        me = 4 * x + 2 * y + c
        for b in range(nb):
            for d, dev, peer in _peers(x, y, c):
                pltpu.make_async_remote_copy(
                    src_ref=srcs[b].at[peer] if places[b] == "pieces" else srcs[b],
                    dst_ref=_slot(dsts[b], places[b], bufs[b].shape, me),
                    send_sem=send_sems.at[b * N_DEV + d], recv_sem=recv_sems.at[b * N_DEV + d],
                    device_id=dev, device_id_type=pl.DeviceIdType.MESH).start()
            pltpu.make_async_copy(srcs[b].at[me] if places[b] == "pieces" else srcs[b],
                                  _slot(dsts[b], places[b], bufs[b].shape, me), recv_sems.at[b * N_DEV]).start()
        token[...] = jnp.zeros_like(token)

    sems = pltpu.SemaphoreType.DMA((nb * N_DEV,))
    ins = [pltpu.with_memory_space_constraint(a, pltpu.HBM) for a in list(bufs) + lands]
    out = pl.pallas_call(
        body,
        out_shape=(sems, sems, *[pltpu.HBM(a.shape, a.dtype) for a in ins], jax.ShapeDtypeStruct((8, 128), F32)),
        in_specs=[_HBM] * (2 * nb) + ([pl.BlockSpec(memory_space=pl.ANY)] if has_after else []),
        out_specs=(_SEM, _SEM, *[_HBM] * (2 * nb), pl.BlockSpec(memory_space=pltpu.VMEM)),
        input_output_aliases={i: 2 + i for i in range(2 * nb)},
        compiler_params=pltpu.CompilerParams(has_side_effects=_EFFECT),
        name=name,
    )(*ins, *([after] if has_after else []))
    return dict(send=out[0], recv=out[1], srcs=out[2:2 + nb], lands=out[2 + nb:2 + 2 * nb], token=out[-1],
                places=places)


def _exchange_wait(h, after, *, name):
    nb = len(h["srcs"])
    places = h["places"]
    shapes = [a.shape for a in h["srcs"]]

    def body(*refs):
        srcs, dsts, send_sems, recv_sems = refs[:nb], refs[nb:2 * nb], refs[2 * nb], refs[2 * nb + 1]
        x, y, c = lax.axis_index("x"), lax.axis_index("y"), lax.axis_index("c")
        me = 4 * x + 2 * y + c
        for b in range(nb):
            pieces = places[b] == "pieces"
            for d, dev, peer in _peers(x, y, c):
                cp = pltpu.make_async_remote_copy(
                    src_ref=srcs[b].at[peer] if pieces else srcs[b],
                    dst_ref=_slot(dsts[b], places[b], shapes[b], peer),
                    send_sem=send_sems.at[b * N_DEV + d], recv_sem=recv_sems.at[b * N_DEV + d],
                    device_id=dev, device_id_type=pl.DeviceIdType.MESH)
                cp.wait_send()
                cp.wait_recv()
            pltpu.make_async_copy(srcs[b].at[me] if pieces else srcs[b], _slot(dsts[b], places[b], shapes[b], me),
                                  recv_sems.at[b * N_DEV]).wait()

    ins = list(h["srcs"]) + list(h["lands"])
    out = pl.pallas_call(
        body,
        out_shape=tuple(pltpu.HBM(a.shape, a.dtype) for a in ins),
        in_specs=[_HBM] * (2 * nb) + [_SEM, _SEM, pl.BlockSpec(memory_space=pl.ANY)],
        out_specs=tuple([_HBM] * (2 * nb)),
        input_output_aliases={i: i for i in range(2 * nb)},
        compiler_params=pltpu.CompilerParams(has_side_effects=_EFFECT),
        name=name,
    )(*ins, h["send"], h["recv"], after)
    return out[nb:]


def _adamw(recv, w, m, v, *, name):
    R, C = w.shape
    tr = R
    for cand in (512, 256, 128, 64, 32, 16):
        if R % cand == 0 and cand * C * 4 <= 2 * 1024 * 1024:
            tr = cand
            break
    c1 = 1.0 - ADAM_B1 ** ADAM_STEP
    c2 = 1.0 - ADAM_B2 ** ADAM_STEP

    def body(r_ref, w_ref, m_ref, v_ref, g_ref, d_ref, mo_ref, vo_ref):
        g = r_ref[0].astype(F32)
        for i in range(1, N_DEV):
            g = g + r_ref[i].astype(F32)
        m_new = ADAM_B1 * m_ref[...] + (1.0 - ADAM_B1) * g
        v_new = ADAM_B2 * v_ref[...] + (1.0 - ADAM_B2) * (g * g)
        m_hat = m_new / c1
        v_hat = v_new / c2
        g_ref[...] = g
        d_ref[...] = -ADAM_LR * (m_hat / (jnp.sqrt(v_hat) + ADAM_EPS) + ADAM_WD * w_ref[...])
        mo_ref[...] = m_new
        vo_ref[...] = v_new

    row = pl.BlockSpec((tr, C), lambda i: (i, 0))
    return pl.pallas_call(
        body,
        out_shape=[jax.ShapeDtypeStruct((R, C), F32)] * 4,
        grid=(R // tr,),
        in_specs=[pl.BlockSpec((N_DEV, tr, C), lambda i: (0, i, 0)), row, row, row],
        out_specs=[row] * 4,
        compiler_params=_cparams(("parallel",)),
        name=name,
    )(recv, w, m, v)


def _ffn_groups(l):
    return ((f"up{l}", (("ffn_w_up", l, BF16, "stack"), ("ffn_dw_w", l, F32, "stack"))),
            (f"dn{l}", (("ffn_w_down", l, BF16, ("axis", 0)), ("ple_w_gate", l, BF16, ("axis", 0)),
                        ("ple_w_proj", l, BF16, ("axis", 1)))))


_GATHER_GROUPS = (
    ("mix", (("mix_w_in", 0, BF16, "stack"), ("conv_dw_w", 0, F32, "stack"), ("mix_w_out", 0, BF16, ("axis", 0)))),
    *_ffn_groups(0),
    ("attn", (("attn_w_qkv", 0, BF16, ("axis", 1)), ("attn_w_o", 0, BF16, ("axis", 0)))),
    *_ffn_groups(1))
_SHARDED = ("mix_w_in", "conv_dw_w", "mix_w_out", "attn_w_qkv", "attn_w_o", "ffn_w_up", "ffn_dw_w", "ffn_w_down",
            "ple_w_gate", "ple_w_proj")
_REPLICATED = ("pool_w", "pool_scale", "conv_dw_b", "conv_ln_g", "conv_ln_b", "attn_rel_bias", "ln_mix_g",
               "ln_mix_b", "ffn_dw_b", "ple_b_gate", "ln_ffn_g", "ln_ffn_b")


def _pack_rows(parts, row_mult, dtype):
    lead = parts[0].shape[:-1]
    flat = jnp.concatenate([a.astype(dtype) for a in parts], axis=-1)
    n = flat.shape[-1]
    unit = row_mult * LANES
    padded = -(-n // unit) * unit
    flat = jnp.pad(flat, [(0, 0)] * len(lead) + [(0, padded - n)])
    return flat.reshape(lead + (padded // LANES, LANES))


def _unpack(flat2d, shapes):
    flat = flat2d.reshape(-1)
    out, o = [], 0
    for s in shapes:
        n = math.prod(s)
        out.append(flat[o:o + n].reshape(s))
        o += n
    return out


def _full_from_shards(g, axis):
    parts = jnp.moveaxis(g, 0, axis)
    shp = list(g.shape[1:])
    shp[axis] *= g.shape[0]
    return parts.reshape(shp)


def _pieces_from_full(full, axis, k=N_DEV):
    shp = list(full.shape)
    n = shp[axis] // k
    t = full.reshape(shp[:axis] + [k, n] + shp[axis + 1:])
    return jnp.moveaxis(t, axis, 0)


def kernel(x, p, mix_w_in, pool_w, pool_scale, conv_dw_w, conv_dw_b, conv_ln_g, conv_ln_b, mix_w_out, attn_w_qkv, attn_rel_bias, attn_w_o, ln_mix_g, ln_mix_b, ffn_w_up, ffn_dw_w, ffn_dw_b, ffn_w_down, ple_w_proj, ple_w_gate, ple_b_gate, ln_ffn_g, ln_ffn_b, loss_target, m_mix_w_in, m_pool_w, m_pool_scale, m_conv_dw_w, m_conv_dw_b, m_conv_ln_g, m_conv_ln_b, m_mix_w_out, m_attn_w_qkv, m_attn_rel_bias, m_attn_w_o, m_ln_mix_g, m_ln_mix_b, m_ffn_w_up, m_ffn_dw_w, m_ffn_dw_b, m_ffn_w_down, m_ple_w_proj, m_ple_w_gate, m_ple_b_gate, m_ln_ffn_g, m_ln_ffn_b, v_mix_w_in, v_pool_w, v_pool_scale, v_conv_dw_w, v_conv_dw_b, v_conv_ln_g, v_conv_ln_b, v_mix_w_out, v_attn_w_qkv, v_attn_rel_bias, v_attn_w_o, v_ln_mix_g, v_ln_mix_b, v_ffn_w_up, v_ffn_dw_w, v_ffn_dw_b, v_ffn_w_down, v_ple_w_proj, v_ple_w_gate, v_ple_b_gate, v_ln_ffn_g, v_ln_ffn_b):
    a = dict(locals())
    sh_names = list(_SHARDED)
    names = sh_names + list(_REPLICATED)
    wts = {n: a[n] for n in names}
    mom = {n: a["m_" + n] for n in names}
    var = {n: a["v_" + n] for n in names}

    gather = {}
    token = None
    for group, items in _GATHER_GROUPS:
        gather[group] = _exchange_start([wts[n][l].astype(dt) for n, l, dt, _ in items], [pl_ for *_, pl_ in items],
                                        token, name="gather_start_" + group)
        token = gather[group]["token"]

    w = dict(pool_w=pool_w[0], pool_scale=pool_scale[0], conv_dw_b=conv_dw_b[0], conv_ln_g=conv_ln_g[0],
             conv_ln_b=conv_ln_b[0], attn_rel_bias=attn_rel_bias[0], ln_mix_g=ln_mix_g, ln_mix_b=ln_mix_b,
             ffn_dw_b=ffn_dw_b, ple_b_gate=ple_b_gate, ln_ffn_g=ln_ffn_g, ln_ffn_b=ln_ffn_b)
    for n in ("ffn_up_g", "ffn_up_v", "ffn_dw_w", "ffn_w_down", "ple_w_gate", "ple_w_proj"):
        w[n] = [None, None]

    def ready(group, after):
        got = _exchange_wait(gather[group], token if after is None else after, name="gather_wait_" + group)
        if group == "mix":
            w["mix_w_in"], w["conv_dw_w"] = _full_from_shards(got[0], 1), _full_from_shards(got[1], 1)
            w["mix_w_out"] = got[2]
        elif group == "attn":
            w["attn_w_qkv"], w["attn_w_o"] = got
        elif group[:2] == "up":
            l = int(group[2])
            w["ffn_up_g"][l] = _full_from_shards(got[0][:N_DEV // 2], 1)
            w["ffn_up_v"][l] = _full_from_shards(got[0][N_DEV // 2:], 1)
            w["ffn_dw_w"][l] = _full_from_shards(got[1], 1)
        else:
            l = int(group[2])
            w["ffn_w_down"][l], w["ple_w_gate"][l], w["ple_w_proj"][l] = got

    scatter = {}

    def emit(group, gr):
        if group[:3] == "ffn":
            l = int(group[3])
            pieces = [jnp.concatenate([_pieces_from_full(gr["ffn_up_g"][l], 1, N_DEV // 2),
                                       _pieces_from_full(gr["ffn_up_v"][l], 1, N_DEV // 2)]),
                      _pieces_from_full(gr["ffn_dw_w"][l], 1), _pieces_from_full(gr["ffn_w_down"][l], 0),
                      _pieces_from_full(gr["ple_w_gate"][l], 0), _pieces_from_full(gr["ple_w_proj"][l], 1)]
        elif group == "attn":
            pieces = [_pieces_from_full(gr["attn_w_qkv"], 1), _pieces_from_full(gr["attn_w_o"], 0)]
        else:
            pieces = [_pieces_from_full(gr["mix_w_in"], 1), _pieces_from_full(gr["conv_dw_w"], 1),
                      _pieces_from_full(gr["mix_w_out"], 0)]
        scatter[group] = _exchange_start([a.astype(BF16) for a in pieces], ["pieces"] * len(pieces), None,
                                         name="grad_start_" + group)
        return scatter[group]["token"]

    loss_part, grad_x, gr = _local_step(x[0], p[:, 0], loss_target[0], w, ready, emit)
    loss = lax.psum(loss_part, ("x", "y", "c"))

    recv = {}
    after = grad_x
    for group in ("ffn1", "attn", "ffn0", "mix"):
        recv[group] = _exchange_wait(scatter[group], after, name="grad_wait_" + group)
        after = recv[group][0]
    got = {"mix_w_in": [recv["mix"][0]], "conv_dw_w": [recv["mix"][1]], "mix_w_out": [recv["mix"][2]],
           "attn_w_qkv": [recv["attn"][0]], "attn_w_o": [recv["attn"][1]]}
    for i, n in enumerate(("ffn_w_up", "ffn_dw_w", "ffn_w_down", "ple_w_gate", "ple_w_proj")):
        got[n] = [recv["ffn0"][i], recv["ffn1"][i]]

    res = [{}, {}, {}, {}]
    for n in sh_names:
        outs_l = [_adamw(r, wts[n][l], mom[n][l], var[n][l], name=f"adamw_{n}{l}") for l, r in enumerate(got[n])]
        for k in range(4):
            res[k][n] = jnp.stack([o[k] for o in outs_l])

    gfull = dict(
        pool_w=gr["pool_w"][None], pool_scale=gr["pool_scale"][None], conv_dw_b=gr["conv_dw_b"][None],
        conv_ln_g=gr["conv_ln_g"][None], conv_ln_b=gr["conv_ln_b"][None], attn_rel_bias=gr["attn_rel_bias"][None],
        ln_mix_g=jnp.stack(gr["ln_mix_g"]), ln_mix_b=jnp.stack(gr["ln_mix_b"]), ffn_dw_b=jnp.stack(gr["ffn_dw_b"]),
        ple_b_gate=jnp.stack(gr["ple_b_gate"]), ln_ffn_g=jnp.stack(gr["ln_ffn_g"]),
        ln_ffn_b=jnp.stack(gr["ln_ffn_b"]))
    rep_send = _pack_rows([gfull[n].reshape(-1) for n in _REPLICATED], 8, F32)
    (rep_recv,) = _exchange([rep_send], ["stack"], name="grad_all_gather")

    def flat_state(d):
        return _pack_rows([d[n].reshape(-1) for n in _REPLICATED], 8, F32)

    rep_out = _adamw(rep_recv, flat_state(wts), flat_state(mom), flat_state(var), name="adamw_replicated")
    for k in range(4):
        for n, arr in zip(_REPLICATED, _unpack(rep_out[k], [wts[n].shape for n in _REPLICATED])):
            res[k][n] = arr
    order = ["mix_w_in", "pool_w", "pool_scale", "conv_dw_w", "conv_dw_b", "conv_ln_g", "conv_ln_b", "mix_w_out",
             "attn_w_qkv", "attn_rel_bias", "attn_w_o", "ln_mix_g", "ln_mix_b", "ffn_w_up", "ffn_dw_w", "ffn_dw_b",
             "ffn_w_down", "ple_w_proj", "ple_w_gate", "ple_b_gate", "ln_ffn_g", "ln_ffn_b"]
    outs = [loss, grad_x[None]]
    for k in range(4):
        outs += [res[k][n] for n in order]
    return tuple(outs)
```

```python
import functools
import math

import jax
import jax.numpy as jnp
from jax import lax
from jax.experimental import pallas as pl
from jax.experimental.pallas import tpu as pltpu

F32 = jnp.float32
BF16 = jnp.bfloat16

N_DEV = 8
D_MODEL = 1024
D_POOL = 512
D_CONV = 512
POOL_WINDOWS = (2, 4, 8, 16)
POOL_GROUP = 128
CONV_KERNEL = 31
CHUNK = 64
HEAD_DIM = 64
N_HEADS = 16
LEFT_CHUNKS = 8
BAND = (LEFT_CHUNKS + 1) * CHUNK
MAX_REL = 256
D_FF = 2816
PLE_DIM = 256
ALPHA = 4.0 ** 0.25
LN_EPS = 1e-5
NEG_INF = -1e30
ADAM_LR, ADAM_B1, ADAM_B2, ADAM_EPS, ADAM_WD, ADAM_STEP = 0.001, 0.9, 0.999, 1e-08, 0.01, 10

Q_BLOCK = 4 * CHUNK
KV_PAD = LEFT_CHUNKS * CHUNK
KV_SPAN = KV_PAD + Q_BLOCK
CONV_HALO = 32
FFN_HALO = 16
SUB_ROWS, SUB_LANES = 64, 128
LANES = 1024
VMEM_LIMIT = 56 * 1024 * 1024


def _cparams(sem=None):
    return pltpu.CompilerParams(dimension_semantics=sem, vmem_limit_bytes=VMEM_LIMIT)


def _tile(dim, pref):
    if dim <= pref:
        return dim
    t = pref - pref % 128
    while t >= 128:
        if dim % t == 0:
            return t
        t -= 128
    return dim


def _sigmoid(x):
    return 1.0 / (1.0 + jnp.exp(-x))


def _bdot(a, b, dn=(((1,), (0,)), ((), ()))):
    return lax.dot_general(a.astype(BF16), b.astype(BF16), dn, preferred_element_type=F32)


NT = (((1,), (1,)), ((), ()))
TN = (((0,), (0,)), ((), ()))


def _mm(a, b, *, ta=False, tb=False, add=None, add_scale=1.0, out_dtype=F32, tm=512, tn=512, tk=1024, dep=None, name):
    if ta:
        K, M = a.shape
    else:
        M, K = a.shape
    if tb:
        N, kb = b.shape
    else:
        kb, N = b.shape
    assert K == kb, (a.shape, b.shape)
    tm, tn, tk = _tile(M, tm), _tile(N, tn), _tile(K, tk)
    nk = K // tk
    a_spec = pl.BlockSpec((tk, tm), lambda i, j, k: (k, i)) if ta else pl.BlockSpec((tm, tk), lambda i, j, k: (i, k))
    b_spec = pl.BlockSpec((tn, tk), lambda i, j, k: (j, k)) if tb else pl.BlockSpec((tk, tn), lambda i, j, k: (k, j))
    dn = (((0 if ta else 1,), (1 if tb else 0,)), ((), ()))
    has_add = add is not None

    def body(*refs):
        if dep is not None:
            refs = refs[:-3] + refs[-2:]
        if has_add:
            a_ref, b_ref, add_ref, o_ref, acc = refs
        else:
            a_ref, b_ref, o_ref, acc = refs
        k = pl.program_id(2)

        @pl.when(k == 0)
        def _():
            acc[...] = jnp.zeros_like(acc)

        acc[...] += _bdot(a_ref[...], b_ref[...], dn)

        @pl.when(k == nk - 1)
        def _():
            r = acc[...]
            if has_add:
                r = r + add_scale * add_ref[...]
            o_ref[...] = r.astype(out_dtype)

    in_specs = [a_spec, b_spec]
    args = [a, b]
    if has_add:
        in_specs.append(pl.BlockSpec((tm, tn), lambda i, j, k: (i, j)))
        args.append(add)
    if dep is not None:
        in_specs.append(pl.BlockSpec(memory_space=pl.ANY))
        args.append(dep)
    return pl.pallas_call(
        body,
        out_shape=jax.ShapeDtypeStruct((M, N), out_dtype),
        grid=(M // tm, N // tn, nk),
        in_specs=in_specs,
        out_specs=pl.BlockSpec((tm, tn), lambda i, j, k: (i, j)),
        scratch_shapes=[pltpu.VMEM((tm, tn), F32)],
        compiler_params=_cparams(("parallel", "parallel", "arbitrary")),
        name=name,
    )(*args)


def _layer_norm_rows(z, g, b):
    mu = jnp.mean(z, axis=-1, keepdims=True)
    zc = z - mu
    var = jnp.mean(zc * zc, axis=-1, keepdims=True)
    return zc * lax.rsqrt(var + LN_EPS) * g + b


def _proj_ln(res, a, w, ln_g, ln_b, *, ple=None, ts=256, name):
    S, D = res.shape
    ka = a.shape[1]
    has_ple = ple is not None
    row = lambda i: (i, 0)
    fix = lambda i: (0, 0)

    def body(*refs):
        if has_ple:
            res_ref, a_ref, w_ref, g_ref, b_ref, wg_ref, bg_ref, p_ref, wp_ref, z_ref, r_ref, gate_ref, proj_ref = refs
        else:
            res_ref, a_ref, w_ref, g_ref, b_ref, z_ref, r_ref = refs
        res_t = res_ref[...]
        acc = _bdot(a_ref[...], w_ref[...])
        if has_ple:
            gate = _sigmoid(_bdot(res_t, wg_ref[...]) + bg_ref[...])
            proj = _bdot(p_ref[...], wp_ref[...])
            gate_ref[...] = gate
            proj_ref[...] = proj
            acc = acc + gate * proj
        z = ALPHA * res_t + acc
        z_ref[...] = z
        r_ref[...] = _layer_norm_rows(z, g_ref[...], b_ref[...])

    in_specs = [pl.BlockSpec((ts, D), row), pl.BlockSpec((ts, ka), row), pl.BlockSpec((ka, D), fix),
                pl.BlockSpec((1, D), fix), pl.BlockSpec((1, D), fix)]
    args = [res, a, w, ln_g.reshape(1, D), ln_b.reshape(1, D)]
    n_out = 2
    if has_ple:
        wg, bg, p, wp = ple
        in_specs += [pl.BlockSpec((D, D), fix), pl.BlockSpec((1, D), fix), pl.BlockSpec((ts, PLE_DIM), row),
                     pl.BlockSpec((PLE_DIM, D), fix)]
        args += [wg, bg.reshape(1, D), p, wp]
        n_out = 4
    return pl.pallas_call(
        body,
        out_shape=[jax.ShapeDtypeStruct((S, D), F32)] * n_out,
        grid=(S // ts,),
        in_specs=in_specs,
        out_specs=[pl.BlockSpec((ts, D), row)] * n_out,
        compiler_params=_cparams(("parallel",)),
        name=name,
    )(*args)


def _mixer_fwd(u, pool_w, pool_scale, conv_w, conv_b, cln_g, cln_b, *, ts=256):
    S = u.shape[0]
    hb = CONV_HALO
    nh = ts // hb

    def body(u_ref, uh_ref, pw_ref, ps_ref, cw_ref, cb_ref, g_ref, b_ref, y_ref, d_ref, sta, stg):
        i = pl.program_id(0)
        first = i == 0
        sta[pl.ds(0, hb), :] = jnp.where(first, 0.0, uh_ref[:, 0:D_POOL])
        sta[pl.ds(hb, ts), :] = u_ref[:, 0:D_POOL]
        glu_h = uh_ref[:, D_POOL:D_POOL + D_CONV] * _sigmoid(uh_ref[:, D_POOL + D_CONV:])
        stg[pl.ds(0, hb), :] = jnp.where(first, 0.0, glu_h)
        stg[pl.ds(hb, ts), :] = u_ref[:, D_POOL:D_POOL + D_CONV] * _sigmoid(u_ref[:, D_POOL + D_CONV:])

        pos = (i * ts + lax.broadcasted_iota(jnp.int32, (ts, 1), 0) + 1).astype(F32)
        for g, w in enumerate(POOL_WINDOWS):
            lanes = pl.ds(g * POOL_GROUP, POOL_GROUP)
            a_g = sta[pl.ds(hb, ts), lanes]
            s = a_g
            for j in range(1, w):
                s = s + sta[pl.ds(hb - j, ts), lanes]
            d_g = s / jnp.minimum(pos, float(w)) - a_g
            d_ref[:, lanes] = d_g.astype(BF16)
            y_ref[:, lanes] = (_bdot(d_g, pw_ref[g]) * ps_ref[:, lanes]).astype(BF16)

        acc = jnp.zeros((ts, D_CONV), F32)
        for k in range(CONV_KERNEL):
            acc = acc + cw_ref[k:k + 1, :] * stg[pl.ds(hb - (CONV_KERNEL - 1) + k, ts), :]
        hc = acc + cb_ref[...]
        ln = _layer_norm_rows(hc, g_ref[...], b_ref[...])
        y_ref[:, D_POOL:] = (ln * _sigmoid(ln)).astype(BF16)

    fix2 = lambda i: (0, 0)
    return pl.pallas_call(
        body,
        out_shape=[jax.ShapeDtypeStruct((S, D_MODEL), BF16), jax.ShapeDtypeStruct((S, D_POOL), BF16)],
        grid=(S // ts,),
        in_specs=[pl.BlockSpec((ts, 3 * D_POOL), lambda i: (i, 0)),
                  pl.BlockSpec((hb, 3 * D_POOL), lambda i: (jnp.maximum(i * nh - 1, 0), 0)),
                  pl.BlockSpec((4, POOL_GROUP, POOL_GROUP), lambda i: (0, 0, 0)),
                  pl.BlockSpec((1, D_POOL), fix2), pl.BlockSpec((CONV_KERNEL, D_CONV), fix2),
                  pl.BlockSpec((1, D_CONV), fix2), pl.BlockSpec((1, D_CONV), fix2), pl.BlockSpec((1, D_CONV), fix2)],
        out_specs=[pl.BlockSpec((ts, D_MODEL), lambda i: (i, 0)), pl.BlockSpec((ts, D_POOL), lambda i: (i, 0))],
        scratch_shapes=[pltpu.VMEM((hb + ts, D_POOL), F32), pltpu.VMEM((hb + ts, D_CONV), F32)],
        compiler_params=_cparams(("parallel",)),
        name="mixer_fwd",
    )(u, u, pool_w, pool_scale.reshape(1, D_POOL), conv_w, conv_b.reshape(1, D_CONV), cln_g.reshape(1, D_CONV),
      cln_b.reshape(1, D_CONV))


def _mixer_bwd(u, d, dycat, pool_w, pool_scale, conv_w, conv_b, cln_g, cln_b, *, ts=256):
    S = u.shape[0]
    hb = CONV_HALO
    nh = ts // hb
    n = S // ts
    te = ts + hb
    K = CONV_KERNEL

    def body(u_ref, up_ref, un_ref, d_ref, dy_ref, dyn_ref, pw_ref, ps_ref, cw_ref, cb_ref, g_ref, b_ref,
             du_ref, dpw_ref, dps_ref, dcw_ref, dcb_ref, dg_ref, db_ref, stg, std, sth):
        i = pl.program_id(0)
        first = i == 0
        last = i == n - 1

        @pl.when(first)
        def _():
            dpw_ref[...] = jnp.zeros_like(dpw_ref)
            dps_ref[...] = jnp.zeros_like(dps_ref)
            dcw_ref[...] = jnp.zeros_like(dcw_ref)
            dcb_ref[...] = jnp.zeros_like(dcb_ref)
            dg_ref[...] = jnp.zeros_like(dg_ref)
            db_ref[...] = jnp.zeros_like(db_ref)

        pos_e = (i * ts + lax.broadcasted_iota(jnp.int32, (te, 1), 0) + 1).astype(F32)
        dya = dy_ref[:, 0:D_POOL]
        dya_n = jnp.where(last, 0.0, dyn_ref[:, 0:D_POOL])
        for g, w in enumerate(POOL_WINDOWS):
            lanes = pl.ds(g * POOL_GROUP, POOL_GROUP)
            sl = slice(g * POOL_GROUP, (g + 1) * POOL_GROUP)
            pw = pw_ref[g]
            scale = ps_ref[:, lanes]
            d_g = d_ref[:, lanes]
            pre = _bdot(d_g, pw)
            dps_ref[:, lanes] += jnp.sum(dya[:, sl] * pre, axis=0, keepdims=True)
            dys = dya[:, sl] * scale
            dpw_ref[g] += _bdot(d_g, dys, TN)
            dys_e = jnp.concatenate([dys, dya_n[:, sl] * scale], axis=0)
            dd = _bdot(dys_e, pw, NT)
            std[:, lanes] = dd / jnp.minimum(pos_e, float(w))
            da = -dd[0:ts]
            for m in range(w):
                da = da + std[pl.ds(m, ts), lanes]
            du_ref[:, lanes] = da.astype(BF16)

        glu_p = up_ref[:, D_POOL:D_POOL + D_CONV] * _sigmoid(up_ref[:, D_POOL + D_CONV:])
        stg[pl.ds(0, hb), :] = jnp.where(first, 0.0, glu_p)
        bv = u_ref[:, D_POOL:D_POOL + D_CONV]
        sg = _sigmoid(u_ref[:, D_POOL + D_CONV:])
        stg[pl.ds(hb, ts), :] = bv * sg
        glu_n = un_ref[:, D_POOL:D_POOL + D_CONV] * _sigmoid(un_ref[:, D_POOL + D_CONV:])
        stg[pl.ds(hb + ts, hb), :] = jnp.where(last, 0.0, glu_n)

        acc = jnp.zeros((te, D_CONV), F32)
        for k in range(K):
            acc = acc + cw_ref[k:k + 1, :] * stg[pl.ds(hb - (K - 1) + k, te), :]
        hc = acc + cb_ref[...]
        mu = jnp.mean(hc, axis=-1, keepdims=True)
        hcc = hc - mu
        rstd = lax.rsqrt(jnp.mean(hcc * hcc, axis=-1, keepdims=True) + LN_EPS)
        xh = hcc * rstd
        ln = xh * g_ref[...] + b_ref[...]
        sl_ = _sigmoid(ln)
        dyb = jnp.concatenate([dy_ref[:, D_POOL:], jnp.where(last, 0.0, dyn_ref[:, D_POOL:])], axis=0)
        dln = dyb * (sl_ * (1.0 + ln * (1.0 - sl_)))
        dxh = dln * g_ref[...]
        dhc = rstd * (dxh - jnp.mean(dxh, axis=-1, keepdims=True) - xh * jnp.mean(dxh * xh, axis=-1, keepdims=True))
        sth[...] = dhc
        dg_ref[...] += jnp.sum((dln * xh)[0:ts], axis=0, keepdims=True)
        db_ref[...] += jnp.sum(dln[0:ts], axis=0, keepdims=True)
        dhc_t = dhc[0:ts]
        dcb_ref[...] += jnp.sum(dhc_t, axis=0, keepdims=True)
        dglu = jnp.zeros((ts, D_CONV), F32)
        for k in range(K):
            dcw_ref[k:k + 1, :] += jnp.sum(dhc_t * stg[pl.ds(hb - (K - 1) + k, ts), :], axis=0, keepdims=True)
            dglu = dglu + cw_ref[k:k + 1, :] * sth[pl.ds(K - 1 - k, ts), :]
        du_ref[:, D_POOL:D_POOL + D_CONV] = (dglu * sg).astype(BF16)
        du_ref[:, D_POOL + D_CONV:] = (dglu * bv * sg * (1.0 - sg)).astype(BF16)

    fix2 = lambda i: (0, 0)
    prev = lambda i: (jnp.maximum(i * nh - 1, 0), 0)
    nxt = lambda i: (jnp.minimum((i + 1) * nh, S // hb - 1), 0)
    return pl.pallas_call(
        body,
        out_shape=[jax.ShapeDtypeStruct((S, 3 * D_POOL), BF16),
                   jax.ShapeDtypeStruct((4, POOL_GROUP, POOL_GROUP), F32),
                   jax.ShapeDtypeStruct((1, D_POOL), F32),
                   jax.ShapeDtypeStruct((K, D_CONV), F32),
                   jax.ShapeDtypeStruct((1, D_CONV), F32),
                   jax.ShapeDtypeStruct((1, D_CONV), F32),
                   jax.ShapeDtypeStruct((1, D_CONV), F32)],
        grid=(n,),
        in_specs=[pl.BlockSpec((ts, 3 * D_POOL), lambda i: (i, 0)),
                  pl.BlockSpec((hb, 3 * D_POOL), prev),
                  pl.BlockSpec((hb, 3 * D_POOL), nxt),
                  pl.BlockSpec((ts, D_POOL), lambda i: (i, 0)),
                  pl.BlockSpec((ts, D_MODEL), lambda i: (i, 0)),
                  pl.BlockSpec((hb, D_MODEL), nxt),
                  pl.BlockSpec((4, POOL_GROUP, POOL_GROUP), lambda i: (0, 0, 0)),
                  pl.BlockSpec((1, D_POOL), fix2), pl.BlockSpec((K, D_CONV), fix2),
                  pl.BlockSpec((1, D_CONV), fix2), pl.BlockSpec((1, D_CONV), fix2), pl.BlockSpec((1, D_CONV), fix2)],
        out_specs=[pl.BlockSpec((ts, 3 * D_POOL), lambda i: (i, 0)),
                   pl.BlockSpec((4, POOL_GROUP, POOL_GROUP), lambda i: (0, 0, 0)),
                   pl.BlockSpec((1, D_POOL), fix2), pl.BlockSpec((K, D_CONV), fix2),
                   pl.BlockSpec((1, D_CONV), fix2), pl.BlockSpec((1, D_CONV), fix2), pl.BlockSpec((1, D_CONV), fix2)],
        scratch_shapes=[pltpu.VMEM((hb + ts + hb, D_CONV), F32), pltpu.VMEM((te, D_POOL), F32),
                        pltpu.VMEM((te, D_CONV), F32)],
        compiler_params=_cparams(("arbitrary",)),
        name="mixer_bwd",
    )(u, u, u, d, dycat, dycat, pool_w, pool_scale.reshape(1, D_POOL), conv_w, conv_b.reshape(1, D_CONV),
      cln_g.reshape(1, D_CONV), cln_b.reshape(1, D_CONV))


_GELU_C = math.sqrt(2.0 / math.pi)


def _gelu_parts(x):
    inner = _GELU_C * (x + 0.044715 * x * x * x)
    th = jnp.tanh(inner)
    ge = 0.5 * x * (1.0 + th)
    dge = 0.5 * (1.0 + th) + 0.5 * x * (1.0 - th * th) * (_GELU_C * (1.0 + 3.0 * 0.044715 * x * x))
    return ge, dge


def _ffn_act_fwd(gate, val, dw_w, dw_b, *, ts=256, tc=1408, name):
    S, F = gate.shape
    hb = FFN_HALO
    nh = ts // hb
    tc = _tile(F, tc)

    def body(g_ref, gh_ref, v_ref, w_ref, b_ref, h_ref, st):
        i = pl.program_id(0)
        st[pl.ds(0, hb), :] = jnp.where(i == 0, 0.0, gh_ref[...].astype(F32))
        st[pl.ds(hb, ts), :] = g_ref[...].astype(F32)
        for c0 in range(0, tc, SUB_LANES):
            ln = pl.ds(c0, SUB_LANES)
            w0, w1, w2, b = w_ref[0:1, ln], w_ref[1:2, ln], w_ref[2:3, ln], b_ref[:, ln]
            for r0 in range(0, ts, SUB_ROWS):
                gc = b + w0 * st[pl.ds(hb - 2 + r0, SUB_ROWS), ln] + w1 * st[pl.ds(hb - 1 + r0, SUB_ROWS), ln] \
                    + w2 * st[pl.ds(hb + r0, SUB_ROWS), ln]
                ge, _ = _gelu_parts(gc)
                rows = pl.ds(r0, SUB_ROWS)
                h_ref[rows, ln] = (ge * v_ref[rows, ln].astype(F32)).astype(BF16)

    return pl.pallas_call(
        body,
        out_shape=jax.ShapeDtypeStruct((S, F), BF16),
        grid=(S // ts, F // tc),
        in_specs=[pl.BlockSpec((ts, tc), lambda i, j: (i, j)),
                  pl.BlockSpec((hb, tc), lambda i, j: (jnp.maximum(i * nh - 1, 0), j)),
                  pl.BlockSpec((ts, tc), lambda i, j: (i, j)),
                  pl.BlockSpec((3, tc), lambda i, j: (0, j)),
                  pl.BlockSpec((1, tc), lambda i, j: (0, j))],
        out_specs=pl.BlockSpec((ts, tc), lambda i, j: (i, j)),
        scratch_shapes=[pltpu.VMEM((hb + ts, tc), F32)],
        compiler_params=_cparams(("parallel", "parallel")),
        name=name,
    )(gate, gate, val, dw_w, dw_b.reshape(1, F))


def _ffn_act_bwd(gate, val, dh, dw_w, dw_b, *, ts=256, tc=1408, name):
    S, F = gate.shape
    hb = FFN_HALO
    nh = ts // hb
    n = S // ts
    te = ts + hb
    tc = _tile(F, tc)

    def body(g_ref, gp_ref, gn_ref, v_ref, vn_ref, dh_ref, dhn_ref, w_ref, b_ref,
             dg_ref, dv_ref, dw_ref, db_ref, st, sd):
        i = pl.program_id(1)
        first = i == 0
        last = i == n - 1

        @pl.when(first)
        def _():
            dw_ref[...] = jnp.zeros_like(dw_ref)
            db_ref[...] = jnp.zeros_like(db_ref)

        st[pl.ds(0, hb), :] = jnp.where(first, 0.0, gp_ref[...].astype(F32))
        st[pl.ds(hb, ts), :] = g_ref[...].astype(F32)
        st[pl.ds(hb + ts, hb), :] = jnp.where(last, 0.0, gn_ref[...].astype(F32))
        for c0 in range(0, tc, SUB_LANES):
            ln = pl.ds(c0, SUB_LANES)
            w0, w1, w2, b = w_ref[0:1, ln], w_ref[1:2, ln], w_ref[2:3, ln], b_ref[:, ln]
            db_acc = jnp.zeros((8, SUB_LANES), F32)
            dw_acc = [jnp.zeros((8, SUB_LANES), F32) for _ in range(3)]
            for r0 in range(0, te, SUB_ROWS):
                rc = min(SUB_ROWS, te - r0)
                gc = b + w0 * st[pl.ds(hb - 2 + r0, rc), ln] + w1 * st[pl.ds(hb - 1 + r0, rc), ln] \
                    + w2 * st[pl.ds(hb + r0, rc), ln]
                ge, dge = _gelu_parts(gc)
                if r0 < ts:
                    rows = pl.ds(r0, rc)
                    val, dh = v_ref[rows, ln].astype(F32), dh_ref[rows, ln].astype(F32)
                else:
                    val = jnp.where(last, 0.0, vn_ref[:, ln].astype(F32)[0:rc])
                    dh = jnp.where(last, 0.0, dhn_ref[:, ln].astype(F32)[0:rc])
                dgc = dh * val * dge
                sd[pl.ds(r0, rc), ln] = dgc
                if r0 < ts:
                    dv_ref[rows, ln] = (dh * ge).astype(BF16)
                    db_acc = db_acc + jnp.sum(dgc.reshape(rc // 8, 8, SUB_LANES), axis=0)
                    for k in range(3):
                        tap = dgc * st[pl.ds(hb - 2 + k + r0, rc), ln]
                        dw_acc[k] = dw_acc[k] + jnp.sum(tap.reshape(rc // 8, 8, SUB_LANES), axis=0)
            db_ref[:, ln] += jnp.sum(db_acc, axis=0, keepdims=True)
            for k in range(3):
                dw_ref[k:k + 1, ln] += jnp.sum(dw_acc[k], axis=0, keepdims=True)
            for r0 in range(0, ts, SUB_ROWS):
                dgate = w0 * sd[pl.ds(2 + r0, SUB_ROWS), ln] + w1 * sd[pl.ds(1 + r0, SUB_ROWS), ln] \
                    + w2 * sd[pl.ds(r0, SUB_ROWS), ln]
                dg_ref[pl.ds(r0, SUB_ROWS), ln] = dgate.astype(BF16)

    cur = lambda j, i: (i, j)
    prev = lambda j, i: (jnp.maximum(i * nh - 1, 0), j)
    nxt = lambda j, i: (jnp.minimum((i + 1) * nh, S // hb - 1), j)
    return pl.pallas_call(
        body,
        out_shape=[jax.ShapeDtypeStruct((S, F), BF16), jax.ShapeDtypeStruct((S, F), BF16),
                   jax.ShapeDtypeStruct((3, F), F32), jax.ShapeDtypeStruct((1, F), F32)],
        grid=(F // tc, n),
        in_specs=[pl.BlockSpec((ts, tc), cur), pl.BlockSpec((hb, tc), prev), pl.BlockSpec((hb, tc), nxt),
                  pl.BlockSpec((ts, tc), cur), pl.BlockSpec((hb, tc), nxt),
                  pl.BlockSpec((ts, tc), cur), pl.BlockSpec((hb, tc), nxt),
                  pl.BlockSpec((3, tc), lambda j, i: (0, j)), pl.BlockSpec((1, tc), lambda j, i: (0, j))],
        out_specs=[pl.BlockSpec((ts, tc), cur), pl.BlockSpec((ts, tc), cur),
                   pl.BlockSpec((3, tc), lambda j, i: (0, j)), pl.BlockSpec((1, tc), lambda j, i: (0, j))],
        scratch_shapes=[pltpu.VMEM((hb + ts + hb, tc), F32), pltpu.VMEM((te, tc), F32)],
        compiler_params=_cparams(("parallel", "arbitrary")),
        name=name,
    )(gate, gate, gate, val, val, dh, dh, dw_w, dw_b.reshape(1, F))


def _ln_bwd(z, ln_g, ln_b, dout, *, loss_head=False, ts=256, dep=None, name):
    S, D = z.shape

    def body(z_ref, g_ref, b_ref, do_ref, *rest):
        dz_ref, dg_ref, db_ref, loss_ref = rest[-4:]
        i = pl.program_id(0)

        @pl.when(i == 0)
        def _():
            dg_ref[...] = jnp.zeros_like(dg_ref)
            db_ref[...] = jnp.zeros_like(db_ref)
            loss_ref[...] = jnp.zeros_like(loss_ref)

        zt = z_ref[...]
        mu = jnp.mean(zt, axis=-1, keepdims=True)
        zc = zt - mu
        rstd = lax.rsqrt(jnp.mean(zc * zc, axis=-1, keepdims=True) + LN_EPS)
        xh = zc * rstd
        if loss_head:
            err = xh * g_ref[...] + b_ref[...] - do_ref[...]
            loss_ref[...] += 0.5 * jnp.sum(jnp.mean(err * err, axis=-1, keepdims=True))
            do = err * (1.0 / D)
        else:
            do = do_ref[...]
        dg_ref[...] += jnp.sum(do * xh, axis=0, keepdims=True)
        db_ref[...] += jnp.sum(do, axis=0, keepdims=True)
        dxh = do * g_ref[...]
        dz_ref[...] = rstd * (dxh - jnp.mean(dxh, axis=-1, keepdims=True)
                              - xh * jnp.mean(dxh * xh, axis=-1, keepdims=True))

    row = lambda i: (i, 0)
    fix = lambda i: (0, 0)
    return pl.pallas_call(
        body,
        out_shape=[jax.ShapeDtypeStruct((S, D), F32), jax.ShapeDtypeStruct((1, D), F32),
                   jax.ShapeDtypeStruct((1, D), F32), jax.ShapeDtypeStruct((8, 128), F32)],
        grid=(S // ts,),
        in_specs=[pl.BlockSpec((ts, D), row), pl.BlockSpec((1, D), fix), pl.BlockSpec((1, D), fix),
                  pl.BlockSpec((ts, D), row)] + ([pl.BlockSpec(memory_space=pl.ANY)] if dep is not None else []),
        out_specs=[pl.BlockSpec((ts, D), row), pl.BlockSpec((1, D), fix), pl.BlockSpec((1, D), fix),
                   pl.BlockSpec((8, 128), fix)],
        compiler_params=_cparams(("arbitrary",)),
        name=name,
    )(z, ln_g.reshape(1, D), ln_b.reshape(1, D), dout, *([dep] if dep is not None else []))


def _ple_bwd(dz, gate, proj, *, ts=256, name):
    S, D = dz.shape

    def body(dz_ref, g_ref, p_ref, ds_ref, dp_ref, db_ref):
        @pl.when(pl.program_id(0) == 0)
        def _():
            db_ref[...] = jnp.zeros_like(db_ref)

        dzt = dz_ref[...]
        g = g_ref[...]
        ds = dzt * p_ref[...] * g * (1.0 - g)
        ds_ref[...] = ds.astype(BF16)
        dp_ref[...] = (dzt * g).astype(BF16)
        db_ref[...] += jnp.sum(ds, axis=0, keepdims=True)

    row = lambda i: (i, 0)
    return pl.pallas_call(
        body,
        out_shape=[jax.ShapeDtypeStruct((S, D), BF16), jax.ShapeDtypeStruct((S, D), BF16),
                   jax.ShapeDtypeStruct((1, D), F32)],
        grid=(S // ts,),
        in_specs=[pl.BlockSpec((ts, D), row)] * 3,
        out_specs=[pl.BlockSpec((ts, D), row), pl.BlockSpec((ts, D), row), pl.BlockSpec((1, D), lambda i: (0, 0))],
        compiler_params=_cparams(("arbitrary",)),
        name=name,
    )(dz, gate, proj)


HEAD_PAIR = 2 * HEAD_DIM


def _attn_probs(qj, kc, bias, qb):
    s = _bdot(qj, kc, NT) * (HEAD_DIM ** -0.5) + bias
    kpos = qb * Q_BLOCK + lax.broadcasted_iota(jnp.int32, (1, KV_SPAN), 1)
    s = jnp.where(kpos >= KV_PAD, s, NEG_INF)
    m = jnp.max(s, axis=-1, keepdims=True)
    e = jnp.exp(s - m)
    return e * (1.0 / jnp.sum(e, axis=-1, keepdims=True))


def _pad_keys(qb, k_ref, v_ref, kp, vp):
    @pl.when(qb == 0)
    def _():
        kp[pl.ds(0, KV_PAD), :] = jnp.zeros((KV_PAD, HEAD_PAIR), BF16)
        vp[pl.ds(0, KV_PAD), :] = jnp.zeros((KV_PAD, HEAD_PAIR), BF16)
        kp[pl.ds(KV_PAD, k_ref.shape[0]), :] = k_ref[...]
        vp[pl.ds(KV_PAD, v_ref.shape[0]), :] = v_ref[...]


def _attn_fwd(qkv, bias):
    S = qkv.shape[0]
    nhp = N_HEADS // 2

    def body(q_ref, k_ref, v_ref, b_ref, o_ref, kp, vp):
        qb = pl.program_id(1)
        _pad_keys(qb, k_ref, v_ref, kp, vp)
        span = pl.ds(pl.multiple_of(qb * Q_BLOCK, Q_BLOCK), KV_SPAN)
        kc, vc = kp[span, :], vp[span, :]
        qt = q_ref[...]
        first = lax.broadcasted_iota(jnp.int32, (1, HEAD_PAIR), 1) < HEAD_DIM
        outs = []
        for j in range(2):
            qj = jnp.where(first if j == 0 else ~first, qt, jnp.zeros_like(qt))
            outs.append(_bdot(_attn_probs(qj, kc, b_ref[j], qb), vc))
        o_ref[...] = jnp.where(first, outs[0], outs[1]).astype(BF16)

    return pl.pallas_call(
        body,
        out_shape=jax.ShapeDtypeStruct((S, D_MODEL), BF16),
        grid=(nhp, S // Q_BLOCK),
        in_specs=[pl.BlockSpec((Q_BLOCK, HEAD_PAIR), lambda h, i: (i, h)),
                  pl.BlockSpec((S, HEAD_PAIR), lambda h, i: (0, nhp + h)),
                  pl.BlockSpec((S, HEAD_PAIR), lambda h, i: (0, 2 * nhp + h)),
                  pl.BlockSpec((2, Q_BLOCK, KV_SPAN), lambda h, i: (h, 0, 0))],
        out_specs=pl.BlockSpec((Q_BLOCK, HEAD_PAIR), lambda h, i: (i, h)),
        scratch_shapes=[pltpu.VMEM((KV_PAD + S, HEAD_PAIR), BF16), pltpu.VMEM((KV_PAD + S, HEAD_PAIR), BF16)],
        compiler_params=_cparams(("parallel", "arbitrary")),
        name="attn_fwd",
    )(qkv, qkv, qkv, bias)


def _attn_bwd(qkv, bias, do):
    S = qkv.shape[0]
    nhp = N_HEADS // 2
    nq = S // Q_BLOCK
    scale = HEAD_DIM ** -0.5

    def body(q_ref, k_ref, v_ref, b_ref, do_ref, dq_ref, dk_ref, dv_ref, db_ref, kp, vp, dka, dva):
        qb = pl.program_id(1)
        _pad_keys(qb, k_ref, v_ref, kp, vp)

        @pl.when(qb == 0)
        def _():
            dka[...] = jnp.zeros_like(dka)
            dva[...] = jnp.zeros_like(dva)
            db_ref[...] = jnp.zeros_like(db_ref)

        span = pl.ds(pl.multiple_of(qb * Q_BLOCK, Q_BLOCK), KV_SPAN)
        kc, vc = kp[span, :], vp[span, :]
        qt, dot = q_ref[...], do_ref[...]
        first = lax.broadcasted_iota(jnp.int32, (1, HEAD_PAIR), 1) < HEAD_DIM
        dqs = []
        for j in range(2):
            mine = first if j == 0 else ~first
            qj = jnp.where(mine, qt, jnp.zeros_like(qt))
            doj = jnp.where(mine, dot, jnp.zeros_like(dot))
            p = _attn_probs(qj, kc, b_ref[j], qb)
            dva[span, :] += _bdot(p, doj, TN)
            dp = _bdot(doj, vc, NT)
            ds = p * (dp - jnp.sum(p * dp, axis=-1, keepdims=True))
            db_ref[j] += ds
            dqs.append(_bdot(ds, kc))
            dka[span, :] += scale * _bdot(ds, qj, TN)
        dq_ref[...] = (scale * jnp.where(first, dqs[0], dqs[1])).astype(BF16)

        @pl.when(qb == nq - 1)
        def _():
            dk_ref[...] = dka[pl.ds(KV_PAD, S), :].astype(BF16)
            dv_ref[...] = dva[pl.ds(KV_PAD, S), :].astype(BF16)

    blk = pl.BlockSpec((Q_BLOCK, HEAD_PAIR), lambda h, i: (i, h))
    col = pl.BlockSpec((S, HEAD_PAIR), lambda h, i: (0, h))
    bsp = pl.BlockSpec((2, Q_BLOCK, KV_SPAN), lambda h, i: (h, 0, 0))
    return pl.pallas_call(
        body,
        out_shape=[jax.ShapeDtypeStruct((S, D_MODEL), BF16)] * 3
        + [jax.ShapeDtypeStruct((N_HEADS, Q_BLOCK, KV_SPAN), F32)],
        grid=(nhp, nq),
        in_specs=[blk, pl.BlockSpec((S, HEAD_PAIR), lambda h, i: (0, nhp + h)),
                  pl.BlockSpec((S, HEAD_PAIR), lambda h, i: (0, 2 * nhp + h)), bsp, blk],
        out_specs=[blk, col, col, bsp],
        scratch_shapes=[pltpu.VMEM((KV_PAD + S, HEAD_PAIR), BF16), pltpu.VMEM((KV_PAD + S, HEAD_PAIR), BF16),
                        pltpu.VMEM((KV_PAD + S, HEAD_PAIR), F32), pltpu.VMEM((KV_PAD + S, HEAD_PAIR), F32)],
        compiler_params=_cparams(("parallel", "arbitrary")),
        name="attn_bwd",
    )(qkv, qkv, qkv, bias, do)


def _bias_blocks(rel_bias):
    H = rel_bias.shape[0]
    n_e = BAND + CHUNK - 1
    n_clip = KV_PAD + CHUNK - 1 - MAX_REL + 1
    e = jnp.concatenate([jnp.broadcast_to(rel_bias[:, 2 * MAX_REL:], (H, n_clip)),
                         jnp.flip(rel_bias[:, 2 * MAX_REL - (n_e - n_clip):2 * MAX_REL], axis=1)], axis=1)
    skew = jnp.pad(jnp.tile(e, (1, CHUNK)), ((0, 0), (0, CHUNK))).reshape(H, CHUNK, n_e + 1)
    band = jnp.flip(skew, axis=1)[:, :, :BAND]
    rows = [jnp.pad(band, ((0, 0), (0, 0), (c * CHUNK, KV_SPAN - BAND - c * CHUNK)), constant_values=NEG_INF)
            for c in range(Q_BLOCK // CHUNK)]
    return jnp.concatenate(rows, axis=1)


def _bias_blocks_grad(dblk):
    H = dblk.shape[0]
    n_e = BAND + CHUNK - 1
    n_clip = KV_PAD + CHUNK - 1 - MAX_REL + 1
    parts = jnp.stack([dblk[:, c * CHUNK:(c + 1) * CHUNK, c * CHUNK:c * CHUNK + BAND]
                       for c in range(Q_BLOCK // CHUNK)], axis=1)
    parts = jnp.flip(parts, axis=2)
    parts = jnp.pad(parts, ((0, 0), (0, 0), (0, 0), (0, n_e + 1 - BAND)))
    skew = parts.reshape(H, Q_BLOCK // CHUNK, CHUNK * (n_e + 1))[:, :, :CHUNK * n_e]
    skew = skew.reshape(H, Q_BLOCK, n_e)
    skew = jnp.pad(skew, ((0, 0), (0, 0), (0, 1)))

    def body(s_ref, o_ref):
        de = jnp.sum(s_ref[...], axis=0, keepdims=True)
        lane = lax.broadcasted_iota(jnp.int32, de.shape, 1)
        far = jnp.sum(jnp.where(lane < n_clip, de, 0.0), axis=-1, keepdims=True)
        o_ref[...] = jnp.where(lane == 0, far, jnp.where(lane < n_clip, 0.0, de))

    de = pl.pallas_call(
        body,
        out_shape=jax.ShapeDtypeStruct((H, 1, n_e + 1), F32),
        grid=(H,),
        in_specs=[pl.BlockSpec((None, Q_BLOCK, n_e + 1), lambda h: (h, 0, 0))],
        out_specs=pl.BlockSpec((None, 1, n_e + 1), lambda h: (h, 0, 0)),
        compiler_params=_cparams(("parallel",)),
        name="bias_grad_sum",
    )(skew).reshape(H, n_e + 1)
    near = jnp.flip(de[:, n_clip:n_e], axis=1)
    return jnp.concatenate([jnp.zeros((H, 2 * MAX_REL - (n_e - n_clip)), F32), near, de[:, 0:1]], axis=1)


def _ffn_forward(r1, p_l, w, l, ready):
    ready(f"up{l}", r1)
    up_g = _mm(r1, w["ffn_up_g"][l], tn=1408, out_dtype=BF16, name=f"ffn_up_g{l}")
    up_v = _mm(r1, w["ffn_up_v"][l], tn=1408, out_dtype=BF16, name=f"ffn_up_v{l}")
    h = _ffn_act_fwd(up_g, up_v, w["ffn_dw_w"][l], w["ffn_dw_b"][l], name=f"ffn_act{l}")
    ready(f"dn{l}", h)
    z2, r2, gate, proj = _proj_ln(r1, h, w["ffn_w_down"][l], w["ln_ffn_g"][l], w["ln_ffn_b"][l],
                                  ple=(w["ple_w_gate"][l], w["ple_b_gate"][l], p_l, w["ple_w_proj"][l]),
                                  name=f"ffn_down_ln{l}")
    return dict(r1=r1, up_g=up_g, up_v=up_v, h=h, z2=z2, gate=gate, proj=proj), r2


def _ffn_backward(sv, dz2, p_l, w, l, grads):
    r1 = sv["r1"]
    ds, dproj, db_gate = _ple_bwd(dz2, sv["gate"], sv["proj"], name=f"ple_bwd{l}")
    dh = _mm(dz2, w["ffn_w_down"][l], tb=True, tn=1408, out_dtype=BF16, name=f"ffn_dh{l}")
    dgate, dval, d_dw_w, d_dw_b = _ffn_act_bwd(sv["up_g"], sv["up_v"], dh, w["ffn_dw_w"][l], w["ffn_dw_b"][l],
                                               name=f"ffn_act_bwd{l}")
    grads["ffn_w_down"][l] = _mm(sv["h"], dz2, ta=True, tm=1408, tn=1024, tk=512, name=f"d_ffn_w_down{l}")
    grads["ffn_up_g"][l] = _mm(r1, dgate, ta=True, tm=1024, tn=1408, tk=512, name=f"d_ffn_up_g{l}")
    grads["ffn_up_v"][l] = _mm(r1, dval, ta=True, tm=1024, tn=1408, tk=512, name=f"d_ffn_up_v{l}")
    grads["ple_w_gate"][l] = _mm(r1, ds, ta=True, tm=1024, tn=1024, tk=512, name=f"d_ple_w_gate{l}")
    grads["ple_w_proj"][l] = _mm(p_l, dproj, ta=True, tm=256, tn=1024, tk=512, name=f"d_ple_w_proj{l}")
    grads["ffn_dw_w"][l] = d_dw_w
    grads["ffn_dw_b"][l] = d_dw_b[0]
    grads["ple_b_gate"][l] = db_gate[0]
    t = _mm(ds, w["ple_w_gate"][l], tb=True, add=dz2, add_scale=ALPHA, tn=1024, name=f"dr1_gate{l}")
    t = _mm(dgate, w["ffn_up_g"][l], tb=True, add=t, tn=1024, tk=1408, name=f"dr1_up_g{l}")
    return _mm(dval, w["ffn_up_v"][l], tb=True, add=t, tn=1024, tk=1408, name=f"dr1_up_v{l}")


def _local_step(x, p, target, w, ready=lambda group, after: None, emit=lambda group, grads: None):
    grads = {k: [None, None] for k in ("ffn_w_down", "ffn_up_g", "ffn_up_v", "ple_w_gate", "ple_w_proj", "ffn_dw_w",
                                       "ffn_dw_b", "ple_b_gate", "ln_ffn_g", "ln_ffn_b", "ln_mix_g", "ln_mix_b")}

    ready("mix", None)
    u = _mm(x, w["mix_w_in"], name="mix_in")
    ycat, dpool = _mixer_fwd(u, w["pool_w"], w["pool_scale"], w["conv_dw_w"], w["conv_dw_b"], w["conv_ln_g"],
                             w["conv_ln_b"])
    z1, r1 = _proj_ln(x, ycat, w["mix_w_out"], w["ln_mix_g"][0], w["ln_mix_b"][0], name="mix_out_ln")
    sv0, r2 = _ffn_forward(r1, p[0], w, 0, ready)

    ready("attn", r2)
    qkv = _mm(r2, w["attn_w_qkv"], out_dtype=BF16, name="attn_qkv")
    bias = _bias_blocks(w["attn_rel_bias"])
    attn = _attn_fwd(qkv, bias)
    z3, r3 = _proj_ln(r2, attn, w["attn_w_o"], w["ln_mix_g"][1], w["ln_mix_b"][1], name="attn_out_ln")
    sv1, _ = _ffn_forward(r3, p[1], w, 1, ready)

    dz4, grads["ln_ffn_g"][1], grads["ln_ffn_b"][1], loss = _ln_bwd(sv1["z2"], w["ln_ffn_g"][1], w["ln_ffn_b"][1],
                                                                    target, loss_head=True, name="loss_ln_bwd")
    dr3 = _ffn_backward(sv1, dz4, p[1], w, 1, grads)
    dz3, grads["ln_mix_g"][1], grads["ln_mix_b"][1], _ = _ln_bwd(z3, w["ln_mix_g"][1], w["ln_mix_b"][1], dr3,
                                                                dep=emit("ffn1", grads), name="ln_mix_bwd1")
    grads["attn_w_o"] = _mm(attn, dz3, ta=True, tm=1024, tn=1024, tk=512, name="d_attn_w_o")
    dattn = _mm(dz3, w["attn_w_o"], tb=True, out_dtype=BF16, tn=1024, name="d_attn")
    dq, dk, dv, dbias = _attn_bwd(qkv, bias, dattn)
    grads["attn_rel_bias"] = _bias_blocks_grad(dbias)
    dqkv = jnp.concatenate([dq, dk, dv], axis=1)
    grads["attn_w_qkv"] = _mm(r2, dqkv, ta=True, tm=1024, tn=1024, tk=512, name="d_attn_w_qkv")
    dr2 = _mm(dqkv, w["attn_w_qkv"], tb=True, add=dz3, add_scale=ALPHA, tn=1024, dep=emit("attn", grads), name="dr2")

    dz2, grads["ln_ffn_g"][0], grads["ln_ffn_b"][0], _ = _ln_bwd(sv0["z2"], w["ln_ffn_g"][0], w["ln_ffn_b"][0], dr2,
                                                                name="ln_ffn_bwd0")
    dr1 = _ffn_backward(sv0, dz2, p[0], w, 0, grads)
    dz1, grads["ln_mix_g"][0], grads["ln_mix_b"][0], _ = _ln_bwd(z1, w["ln_mix_g"][0], w["ln_mix_b"][0], dr1,
                                                                dep=emit("ffn0", grads), name="ln_mix_bwd0")
    grads["mix_w_out"] = _mm(ycat, dz1, ta=True, tm=1024, tn=1024, tk=512, name="d_mix_w_out")
    dycat = _mm(dz1, w["mix_w_out"], tb=True, tn=1024, name="d_ycat")
    du, g_pw, g_ps, g_cw, g_cb, g_cg, g_cbb = _mixer_bwd(u, dpool, dycat, w["pool_w"], w["pool_scale"],
                                                         w["conv_dw_w"], w["conv_dw_b"], w["conv_ln_g"],
                                                         w["conv_ln_b"])
    grads["mix_w_in"] = _mm(x, du, ta=True, tm=1024, tn=512, tk=512, name="d_mix_w_in")
    grads["conv_dw_w"] = g_cw
    grad_x = _mm(du, w["mix_w_in"], tb=True, add=dz1, add_scale=ALPHA, tn=1024, tk=512, dep=emit("mix", grads),
                 name="grad_x")
    grads.update(pool_w=g_pw, pool_scale=g_ps[0], conv_dw_w=g_cw, conv_dw_b=g_cb[0], conv_ln_g=g_cg[0],
                 conv_ln_b=g_cbb[0])
    for kname in ("ln_ffn_g", "ln_ffn_b", "ln_mix_g", "ln_mix_b"):
        grads[kname] = [a[0] for a in grads[kname]]
    return loss[0, 0], grad_x, grads


def _exchange(bufs, places, *, name):
    nb = len(bufs)

    def body(*refs):
        srcs, dsts = refs[:nb], refs[nb:2 * nb]
        send_sems, recv_sems, local_sems = refs[2 * nb:]
        x, y, c = lax.axis_index("x"), lax.axis_index("y"), lax.axis_index("c")
        me = 4 * x + 2 * y + c
        local = []
        remote = []
        for b in range(nb):
            pieces = places[b] == "pieces"
            shape = bufs[b].shape
            cp = pltpu.make_async_copy(srcs[b].at[me] if pieces else srcs[b], _slot(dsts[b], places[b], shape, me),
                                       local_sems.at[b])
            cp.start()
            local.append(cp)
            for d, dev, peer in _peers(x, y, c):
                src = srcs[b].at[peer] if pieces else srcs[b]
                out = pltpu.make_async_remote_copy(
                    src_ref=src, dst_ref=_slot(dsts[b], places[b], shape, me),
                    send_sem=send_sems.at[b * N_DEV + d], recv_sem=recv_sems.at[b * N_DEV + d],
                    device_id=dev, device_id_type=pl.DeviceIdType.MESH)
                out.start()
                inc = pltpu.make_async_remote_copy(
                    src_ref=src, dst_ref=_slot(dsts[b], places[b], shape, peer),
                    send_sem=send_sems.at[b * N_DEV + d], recv_sem=recv_sems.at[b * N_DEV + d],
                    device_id=dev, device_id_type=pl.DeviceIdType.MESH)
                remote.append((out, inc))
        for cp in local:
            cp.wait()
        for out, inc in remote:
            out.wait_send()
            inc.wait_recv()

    out_shapes = [jax.ShapeDtypeStruct(_result_shape(b, place), b.dtype) for b, place in zip(bufs, places)]
    any_spec = pl.BlockSpec(memory_space=pl.ANY)
    return pl.pallas_call(
        body,
        out_shape=out_shapes,
        in_specs=[any_spec] * nb,
        out_specs=[any_spec] * nb,
        scratch_shapes=[pltpu.SemaphoreType.DMA((nb * N_DEV,)), pltpu.SemaphoreType.DMA((nb * N_DEV,)),
                        pltpu.SemaphoreType.DMA((nb,))],
        compiler_params=pltpu.CompilerParams(has_side_effects=True),
        name=name,
    )(*bufs)


_HBM = pl.BlockSpec(memory_space=pltpu.HBM)
_SEM = pl.BlockSpec(memory_space=pltpu.SEMAPHORE)
_EFFECT = pltpu.SideEffectType.DATAFLOW_SIDE_EFFECTING


def _slot(ref, place, shape, k):
    if place in ("stack", "pieces"):
        return ref.at[k]
    ax = place[1]
    n = shape[ax]
    return ref.at[(slice(None),) * ax + (pl.ds(pl.multiple_of(k * n, n), n),)]


def _result_shape(buf, place):
    if place == "stack":
        return (N_DEV,) + buf.shape
    if place == "pieces":
        return buf.shape
    return tuple(s * N_DEV if i == place[1] else s for i, s in enumerate(buf.shape))


def _peers(x, y, c):
    for d in range(1, N_DEV):
        px, py, pc = x ^ ((d >> 2) & 1), y ^ ((d >> 1) & 1), c ^ (d & 1)
        yield d, (px, py, pc), 4 * px + 2 * py + pc


def _exchange_start(bufs, places, after, *, name):
    nb = len(bufs)
    lands = [lax.empty(_result_shape(b, p_), b.dtype) for b, p_ in zip(bufs, places)]
    has_after = after is not None

    def body(*refs):
        srcs, dsts = refs[:nb], refs[nb:2 * nb]
        outs = refs[2 * nb + has_after:]
        send_sems, recv_sems, token = outs[0], outs[1], outs[2 + 2 * nb]
        x, y, c = lax.axis_index("x"), lax.axis_index("y"), lax.axis_index("c")
        me = 4 * x + 2 * y + c
        for b in range(nb):
            for d, dev, peer in _peers(x, y, c):
                pltpu.make_async_remote_copy(
                    src_ref=srcs[b].at[peer] if places[b] == "pieces" else srcs[b],
                    dst_ref=_slot(dsts[b], places[b], bufs[b].shape, me),
                    send_sem=send_sems.at[b * N_DEV + d], recv_sem=recv_sems.at[b * N_DEV + d],
                    device_id=dev, device_id_type=pl.DeviceIdType.MESH).start()
            pltpu.make_async_copy(srcs[b].at[me] if places[b] == "pieces" else srcs[b],
                                  _slot(dsts[b], places[b], bufs[b].shape, me), recv_sems.at[b * N_DEV]).start()
        token[...] = jnp.zeros_like(token)

    sems = pltpu.SemaphoreType.DMA((nb * N_DEV,))
    ins = [pltpu.with_memory_space_constraint(a, pltpu.HBM) for a in list(bufs) + lands]
    out = pl.pallas_call(
        body,
        out_shape=(sems, sems, *[pltpu.HBM(a.shape, a.dtype) for a in ins], jax.ShapeDtypeStruct((8, 128), F32)),
        in_specs=[_HBM] * (2 * nb) + ([pl.BlockSpec(memory_space=pl.ANY)] if has_after else []),
        out_specs=(_SEM, _SEM, *[_HBM] * (2 * nb), pl.BlockSpec(memory_space=pltpu.VMEM)),
        input_output_aliases={i: 2 + i for i in range(2 * nb)},
        compiler_params=pltpu.CompilerParams(has_side_effects=_EFFECT),
        name=name,
    )(*ins, *([after] if has_after else []))
    return dict(send=out[0], recv=out[1], srcs=out[2:2 + nb], lands=out[2 + nb:2 + 2 * nb], token=out[-1],
                places=places)


def _exchange_wait(h, after, *, name):
    nb = len(h["srcs"])
    places = h["places"]
    shapes = [a.shape for a in h["srcs"]]

    def body(*refs):
        srcs, dsts, send_sems, recv_sems = refs[:nb], refs[nb:2 * nb], refs[2 * nb], refs[2 * nb + 1]
        x, y, c = lax.axis_index("x"), lax.axis_index("y"), lax.axis_index("c")
        me = 4 * x + 2 * y + c
        for b in range(nb):
            pieces = places[b] == "pieces"
            for d, dev, peer in _peers(x, y, c):
                cp = pltpu.make_async_remote_copy(
                    src_ref=srcs[b].at[peer] if pieces else srcs[b],
                    dst_ref=_slot(dsts[b], places[b], shapes[b], peer),
                    send_sem=send_sems.at[b * N_DEV + d], recv_sem=recv_sems.at[b * N_DEV + d],
                    device_id=dev, device_id_type=pl.DeviceIdType.MESH)
                cp.wait_send()
                cp.wait_recv()
            pltpu.make_async_copy(srcs[b].at[me] if pieces else srcs[b], _slot(dsts[b], places[b], shapes[b], me),
                                  recv_sems.at[b * N_DEV]).wait()

    ins = list(h["srcs"]) + list(h["lands"])
    out = pl.pallas_call(
        body,
        out_shape=tuple(pltpu.HBM(a.shape, a.dtype) for a in ins),
        in_specs=[_HBM] * (2 * nb) + [_SEM, _SEM, pl.BlockSpec(memory_space=pl.ANY)],
        out_specs=tuple([_HBM] * (2 * nb)),
        input_output_aliases={i: i for i in range(2 * nb)},
        compiler_params=pltpu.CompilerParams(has_side_effects=_EFFECT),
        name=name,
    )(*ins, h["send"], h["recv"], after)
    return out[nb:]


def _adamw(recv, w, m, v, *, name):
    R, C = w.shape
    tr = R
    for cand in (512, 256, 128, 64, 32, 16):
        if R % cand == 0 and cand * C * 4 <= 2 * 1024 * 1024:
            tr = cand
            break
    c1 = 1.0 - ADAM_B1 ** ADAM_STEP
    c2 = 1.0 - ADAM_B2 ** ADAM_STEP

    def body(r_ref, w_ref, m_ref, v_ref, g_ref, d_ref, mo_ref, vo_ref):
        g = r_ref[0].astype(F32)
        for i in range(1, N_DEV):
            g = g + r_ref[i].astype(F32)
        m_new = ADAM_B1 * m_ref[...] + (1.0 - ADAM_B1) * g
        v_new = ADAM_B2 * v_ref[...] + (1.0 - ADAM_B2) * (g * g)
        m_hat = m_new / c1
        v_hat = v_new / c2
        g_ref[...] = g
        d_ref[...] = -ADAM_LR * (m_hat / (jnp.sqrt(v_hat) + ADAM_EPS) + ADAM_WD * w_ref[...])
        mo_ref[...] = m_new
        vo_ref[...] = v_new

    row = pl.BlockSpec((tr, C), lambda i: (i, 0))
    return pl.pallas_call(
        body,
        out_shape=[jax.ShapeDtypeStruct((R, C), F32)] * 4,
        grid=(R // tr,),
        in_specs=[pl.BlockSpec((N_DEV, tr, C), lambda i: (0, i, 0)), row, row, row],
        out_specs=[row] * 4,
        compiler_params=_cparams(("parallel",)),
        name=name,
    )(recv, w, m, v)


def _ffn_groups(l):
    return ((f"up{l}", (("ffn_w_up", l, BF16, "stack"), ("ffn_dw_w", l, F32, "stack"))),
            (f"dn{l}", (("ffn_w_down", l, BF16, ("axis", 0)), ("ple_w_gate", l, BF16, ("axis", 0)),
                        ("ple_w_proj", l, BF16, ("axis", 1)))))


_GATHER_GROUPS = (
    ("mix", (("mix_w_in", 0, BF16, "stack"), ("conv_dw_w", 0, F32, "stack"), ("mix_w_out", 0, BF16, ("axis", 0)))),
    *_ffn_groups(0),
    ("attn", (("attn_w_qkv", 0, BF16, ("axis", 1)), ("attn_w_o", 0, BF16, ("axis", 0)))),
    *_ffn_groups(1))
_SHARDED = ("mix_w_in", "conv_dw_w", "mix_w_out", "attn_w_qkv", "attn_w_o", "ffn_w_up", "ffn_dw_w", "ffn_w_down",
            "ple_w_gate", "ple_w_proj")
_REPLICATED = ("pool_w", "pool_scale", "conv_dw_b", "conv_ln_g", "conv_ln_b", "attn_rel_bias", "ln_mix_g",
               "ln_mix_b", "ffn_dw_b", "ple_b_gate", "ln_ffn_g", "ln_ffn_b")


def _pack_rows(parts, row_mult, dtype):
    lead = parts[0].shape[:-1]
    flat = jnp.concatenate([a.astype(dtype) for a in parts], axis=-1)
    n = flat.shape[-1]
    unit = row_mult * LANES
    padded = -(-n // unit) * unit
    flat = jnp.pad(flat, [(0, 0)] * len(lead) + [(0, padded - n)])
    return flat.reshape(lead + (padded // LANES, LANES))


def _unpack(flat2d, shapes):
    flat = flat2d.reshape(-1)
    out, o = [], 0
    for s in shapes:
        n = math.prod(s)
        out.append(flat[o:o + n].reshape(s))
        o += n
    return out


def _full_from_shards(g, axis):
    parts = jnp.moveaxis(g, 0, axis)
    shp = list(g.shape[1:])
    shp[axis] *= g.shape[0]
    return parts.reshape(shp)


def _pieces_from_full(full, axis, k=N_DEV):
    shp = list(full.shape)
    n = shp[axis] // k
    t = full.reshape(shp[:axis] + [k, n] + shp[axis + 1:])
    return jnp.moveaxis(t, axis, 0)


def kernel(x, p, mix_w_in, pool_w, pool_scale, conv_dw_w, conv_dw_b, conv_ln_g, conv_ln_b, mix_w_out, attn_w_qkv, attn_rel_bias, attn_w_o, ln_mix_g, ln_mix_b, ffn_w_up, ffn_dw_w, ffn_dw_b, ffn_w_down, ple_w_proj, ple_w_gate, ple_b_gate, ln_ffn_g, ln_ffn_b, loss_target, m_mix_w_in, m_pool_w, m_pool_scale, m_conv_dw_w, m_conv_dw_b, m_conv_ln_g, m_conv_ln_b, m_mix_w_out, m_attn_w_qkv, m_attn_rel_bias, m_attn_w_o, m_ln_mix_g, m_ln_mix_b, m_ffn_w_up, m_ffn_dw_w, m_ffn_dw_b, m_ffn_w_down, m_ple_w_proj, m_ple_w_gate, m_ple_b_gate, m_ln_ffn_g, m_ln_ffn_b, v_mix_w_in, v_pool_w, v_pool_scale, v_conv_dw_w, v_conv_dw_b, v_conv_ln_g, v_conv_ln_b, v_mix_w_out, v_attn_w_qkv, v_attn_rel_bias, v_attn_w_o, v_ln_mix_g, v_ln_mix_b, v_ffn_w_up, v_ffn_dw_w, v_ffn_dw_b, v_ffn_w_down, v_ple_w_proj, v_ple_w_gate, v_ple_b_gate, v_ln_ffn_g, v_ln_ffn_b):
    a = dict(locals())
    sh_names = list(_SHARDED)
    names = sh_names + list(_REPLICATED)
    wts = {n: a[n] for n in names}
    mom = {n: a["m_" + n] for n in names}
    var = {n: a["v_" + n] for n in names}

    gather = {}
    token = None
    for group, items in _GATHER_GROUPS:
        gather[group] = _exchange_start([wts[n][l].astype(dt) for n, l, dt, _ in items], [pl_ for *_, pl_ in items],
                                        token, name="gather_start_" + group)
        token = gather[group]["token"]

    w = dict(pool_w=pool_w[0], pool_scale=pool_scale[0], conv_dw_b=conv_dw_b[0], conv_ln_g=conv_ln_g[0],
             conv_ln_b=conv_ln_b[0], attn_rel_bias=attn_rel_bias[0], ln_mix_g=ln_mix_g, ln_mix_b=ln_mix_b,
             ffn_dw_b=ffn_dw_b, ple_b_gate=ple_b_gate, ln_ffn_g=ln_ffn_g, ln_ffn_b=ln_ffn_b)
    for n in ("ffn_up_g", "ffn_up_v", "ffn_dw_w", "ffn_w_down", "ple_w_gate", "ple_w_proj"):
        w[n] = [None, None]

    def ready(group, after):
        got = _exchange_wait(gather[group], token if after is None else after, name="gather_wait_" + group)
        if group == "mix":
            w["mix_w_in"], w["conv_dw_w"] = _full_from_shards(got[0], 1), _full_from_shards(got[1], 1)
            w["mix_w_out"] = got[2]
        elif group == "attn":
            w["attn_w_qkv"], w["attn_w_o"] = got
        elif group[:2] == "up":
            l = int(group[2])
            w["ffn_up_g"][l] = _full_from_shards(got[0][:N_DEV // 2], 1)
            w["ffn_up_v"][l] = _full_from_shards(got[0][N_DEV // 2:], 1)
            w["ffn_dw_w"][l] = _full_from_shards(got[1], 1)
        else:
            l = int(group[2])
            w["ffn_w_down"][l], w["ple_w_gate"][l], w["ple_w_proj"][l] = got

    scatter = {}

    def emit(group, gr):
        if group[:3] == "ffn":
            l = int(group[3])
            pieces = [jnp.concatenate([_pieces_from_full(gr["ffn_up_g"][l], 1, N_DEV // 2),
                                       _pieces_from_full(gr["ffn_up_v"][l], 1, N_DEV // 2)]),
                      _pieces_from_full(gr["ffn_dw_w"][l], 1), _pieces_from_full(gr["ffn_w_down"][l], 0),
                      _pieces_from_full(gr["ple_w_gate"][l], 0), _pieces_from_full(gr["ple_w_proj"][l], 1)]
        elif group == "attn":
            pieces = [_pieces_from_full(gr["attn_w_qkv"], 1), _pieces_from_full(gr["attn_w_o"], 0)]
        else:
            pieces = [_pieces_from_full(gr["mix_w_in"], 1), _pieces_from_full(gr["conv_dw_w"], 1),
                      _pieces_from_full(gr["mix_w_out"], 0)]
        scatter[group] = _exchange_start([a.astype(BF16) for a in pieces], ["pieces"] * len(pieces), None,
                                         name="grad_start_" + group)
        return scatter[group]["token"]

    loss_part, grad_x, gr = _local_step(x[0], p[:, 0], loss_target[0], w, ready, emit)
    loss = lax.psum(loss_part, ("x", "y", "c"))

    recv = {}
    after = grad_x
    for group in ("ffn1", "attn", "ffn0", "mix"):
        recv[group] = _exchange_wait(scatter[group], after, name="grad_wait_" + group)
        after = recv[group][0]
    got = {"mix_w_in": [recv["mix"][0]], "conv_dw_w": [recv["mix"][1]], "mix_w_out": [recv["mix"][2]],
           "attn_w_qkv": [recv["attn"][0]], "attn_w_o": [recv["attn"][1]]}
    for i, n in enumerate(("ffn_w_up", "ffn_dw_w", "ffn_w_down", "ple_w_gate", "ple_w_proj")):
        got[n] = [recv["ffn0"][i], recv["ffn1"][i]]

    res = [{}, {}, {}, {}]
    for n in sh_names:
        outs_l = [_adamw(r, wts[n][l], mom[n][l], var[n][l], name=f"adamw_{n}{l}") for l, r in enumerate(got[n])]
        for k in range(4):
            res[k][n] = jnp.stack([o[k] for o in outs_l])

    gfull = dict(
        pool_w=gr["pool_w"][None], pool_scale=gr["pool_scale"][None], conv_dw_b=gr["conv_dw_b"][None],
        conv_ln_g=gr["conv_ln_g"][None], conv_ln_b=gr["conv_ln_b"][None], attn_rel_bias=gr["attn_rel_bias"][None],
        ln_mix_g=jnp.stack(gr["ln_mix_g"]), ln_mix_b=jnp.stack(gr["ln_mix_b"]), ffn_dw_b=jnp.stack(gr["ffn_dw_b"]),
        ple_b_gate=jnp.stack(gr["ple_b_gate"]), ln_ffn_g=jnp.stack(gr["ln_ffn_g"]),
        ln_ffn_b=jnp.stack(gr["ln_ffn_b"]))
    rep_send = _pack_rows([gfull[n].reshape(-1) for n in _REPLICATED], 8, F32)
    (rep_recv,) = _exchange([rep_send], ["stack"], name="grad_all_gather")

    def flat_state(d):
        return _pack_rows([d[n].reshape(-1) for n in _REPLICATED], 8, F32)

    rep_out = _adamw(rep_recv, flat_state(wts), flat_state(mom), flat_state(var), name="adamw_replicated")
    for k in range(4):
        for n, arr in zip(_REPLICATED, _unpack(rep_out[k], [wts[n].shape for n in _REPLICATED])):
            res[k][n] = arr
    order = ["mix_w_in", "pool_w", "pool_scale", "conv_dw_w", "conv_dw_b", "conv_ln_g", "conv_ln_b", "mix_w_out",
             "attn_w_qkv", "attn_rel_bias", "attn_w_o", "ln_mix_g", "ln_mix_b", "ffn_w_up", "ffn_dw_w", "ffn_dw_b",
             "ffn_w_down", "ple_w_proj", "ple_w_gate", "ple_b_gate", "ln_ffn_g", "ln_ffn_b"]
    outs = [loss, grad_x[None]]
    for k in range(4):
        outs += [res[k][n] for n in order]
    return tuple(outs)
```

```python
import functools
import math

import jax
import jax.numpy as jnp
from jax import lax
from jax.experimental import pallas as pl
from jax.experimental.pallas import tpu as pltpu

F32 = jnp.float32
BF16 = jnp.bfloat16

N_DEV = 8
D_MODEL = 1024
D_POOL = 512
D_CONV = 512
POOL_WINDOWS = (2, 4, 8, 16)
POOL_GROUP = 128
CONV_KERNEL = 31
CHUNK = 64
HEAD_DIM = 64
N_HEADS = 16
LEFT_CHUNKS = 8
BAND = (LEFT_CHUNKS + 1) * CHUNK
MAX_REL = 256
D_FF = 2816
PLE_DIM = 256
ALPHA = 4.0 ** 0.25
LN_EPS = 1e-5
NEG_INF = -1e30
ADAM_LR, ADAM_B1, ADAM_B2, ADAM_EPS, ADAM_WD, ADAM_STEP = 0.001, 0.9, 0.999, 1e-08, 0.01, 10

Q_BLOCK = 4 * CHUNK
KV_PAD = LEFT_CHUNKS * CHUNK
KV_SPAN = KV_PAD + Q_BLOCK
CONV_HALO = 32
FFN_HALO = 16
SUB_ROWS, SUB_LANES = 64, 128
LANES = 1024
VMEM_LIMIT = 56 * 1024 * 1024


def _cparams(sem=None):
    return pltpu.CompilerParams(dimension_semantics=sem, vmem_limit_bytes=VMEM_LIMIT)


def _tile(dim, pref):
    if dim <= pref:
        return dim
    t = pref - pref % 128
    while t >= 128:
        if dim % t == 0:
            return t
        t -= 128
    return dim


def _sigmoid(x):
    return 1.0 / (1.0 + jnp.exp(-x))


def _bdot(a, b, dn=(((1,), (0,)), ((), ()))):
    return lax.dot_general(a.astype(BF16), b.astype(BF16), dn, preferred_element_type=F32)


NT = (((1,), (1,)), ((), ()))
TN = (((0,), (0,)), ((), ()))


def _mm(a, b, *, ta=False, tb=False, add=None, add_scale=1.0, out_dtype=F32, tm=512, tn=512, tk=1024, dep=None, name):
    if ta:
        K, M = a.shape
    else:
        M, K = a.shape
    if tb:
        N, kb = b.shape
    else:
        kb, N = b.shape
    assert K == kb, (a.shape, b.shape)
    tm, tn, tk = _tile(M, tm), _tile(N, tn), _tile(K, tk)
    nk = K // tk
    a_spec = pl.BlockSpec((tk, tm), lambda i, j, k: (k, i)) if ta else pl.BlockSpec((tm, tk), lambda i, j, k: (i, k))
    b_spec = pl.BlockSpec((tn, tk), lambda i, j, k: (j, k)) if tb else pl.BlockSpec((tk, tn), lambda i, j, k: (k, j))
    dn = (((0 if ta else 1,), (1 if tb else 0,)), ((), ()))
    has_add = add is not None

    def body(*refs):
        if dep is not None:
            refs = refs[:-3] + refs[-2:]
        if has_add:
            a_ref, b_ref, add_ref, o_ref, acc = refs
        else:
            a_ref, b_ref, o_ref, acc = refs
        k = pl.program_id(2)

        @pl.when(k == 0)
        def _():
            acc[...] = jnp.zeros_like(acc)

        acc[...] += _bdot(a_ref[...], b_ref[...], dn)

        @pl.when(k == nk - 1)
        def _():
            r = acc[...]
            if has_add:
                r = r + add_scale * add_ref[...]
            o_ref[...] = r.astype(out_dtype)

    in_specs = [a_spec, b_spec]
    args = [a, b]
    if has_add:
        in_specs.append(pl.BlockSpec((tm, tn), lambda i, j, k: (i, j)))
        args.append(add)
    if dep is not None:
        in_specs.append(pl.BlockSpec(memory_space=pl.ANY))
        args.append(dep)
    return pl.pallas_call(
        body,
        out_shape=jax.ShapeDtypeStruct((M, N), out_dtype),
        grid=(M // tm, N // tn, nk),
        in_specs=in_specs,
        out_specs=pl.BlockSpec((tm, tn), lambda i, j, k: (i, j)),
        scratch_shapes=[pltpu.VMEM((tm, tn), F32)],
        compiler_params=_cparams(("parallel", "parallel", "arbitrary")),
        name=name,
    )(*args)


def _mm_rows(pairs, *, add=None, add_scale=1.0, out_dtype=F32, tm=256, dep=None, name):
    M = pairs[0][0].shape[0]
    N = pairs[0][1].shape[0] if pairs[0][2] else pairs[0][1].shape[1]
    n = len(pairs)
    has_add = add is not None

    def body(*refs):
        o_ref = refs[-1]
        acc = None
        for i, (_, _, tr) in enumerate(pairs):
            part = _bdot(refs[2 * i][...], refs[2 * i + 1][...], NT if tr else (((1,), (0,)), ((), ())))
            acc = part if acc is None else acc + part
        if has_add:
            acc = acc + add_scale * refs[2 * n][...]
        o_ref[...] = acc.astype(out_dtype)

    in_specs, args = [], []
    for a, w_, _ in pairs:
        in_specs += [pl.BlockSpec((tm, a.shape[1]), lambda i: (i, 0)), pl.BlockSpec(w_.shape, lambda i: (0, 0))]
        args += [a, w_]
    if has_add:
        in_specs.append(pl.BlockSpec((tm, N), lambda i: (i, 0)))
        args.append(add)
    if dep is not None:
        in_specs.append(pl.BlockSpec(memory_space=pl.ANY))
        args.append(dep)
    return pl.pallas_call(
        body,
        out_shape=jax.ShapeDtypeStruct((M, N), out_dtype),
        grid=(M // tm,),
        in_specs=in_specs,
        out_specs=pl.BlockSpec((tm, N), lambda i: (i, 0)),
        compiler_params=_cparams(("parallel",)),
        name=name,
    )(*args)


def _layer_norm_rows(z, g, b):
    mu = jnp.mean(z, axis=-1, keepdims=True)
    zc = z - mu
    var = jnp.mean(zc * zc, axis=-1, keepdims=True)
    return zc * lax.rsqrt(var + LN_EPS) * g + b


def _proj_ln(res, a, w, ln_g, ln_b, *, ple=None, ts=256, name):
    S, D = res.shape
    ka = a.shape[1]
    has_ple = ple is not None
    row = lambda i: (i, 0)
    fix = lambda i: (0, 0)

    def body(*refs):
        if has_ple:
            res_ref, a_ref, w_ref, g_ref, b_ref, wg_ref, bg_ref, p_ref, wp_ref, z_ref, r_ref, gate_ref, proj_ref = refs
        else:
            res_ref, a_ref, w_ref, g_ref, b_ref, z_ref, r_ref = refs
        res_t = res_ref[...]
        acc = _bdot(a_ref[...], w_ref[...])
        if has_ple:
            gate = _sigmoid(_bdot(res_t, wg_ref[...]) + bg_ref[...])
            proj = _bdot(p_ref[...], wp_ref[...])
            gate_ref[...] = gate
            proj_ref[...] = proj
            acc = acc + gate * proj
        z = ALPHA * res_t + acc
        z_ref[...] = z
        r_ref[...] = _layer_norm_rows(z, g_ref[...], b_ref[...])

    in_specs = [pl.BlockSpec((ts, D), row), pl.BlockSpec((ts, ka), row), pl.BlockSpec((ka, D), fix),
                pl.BlockSpec((1, D), fix), pl.BlockSpec((1, D), fix)]
    args = [res, a, w, ln_g.reshape(1, D), ln_b.reshape(1, D)]
    n_out = 2
    if has_ple:
        wg, bg, p, wp = ple
        in_specs += [pl.BlockSpec((D, D), fix), pl.BlockSpec((1, D), fix), pl.BlockSpec((ts, PLE_DIM), row),
                     pl.BlockSpec((PLE_DIM, D), fix)]
        args += [wg, bg.reshape(1, D), p, wp]
        n_out = 4
    return pl.pallas_call(
        body,
        out_shape=[jax.ShapeDtypeStruct((S, D), F32)] * n_out,
        grid=(S // ts,),
        in_specs=in_specs,
        out_specs=[pl.BlockSpec((ts, D), row)] * n_out,
        compiler_params=_cparams(("parallel",)),
        name=name,
    )(*args)


CONV_ROWS = 32
LN_ROWS = 16


def _shifted_copies(src, dst, rows):
    for b in range(1, 8):
        for c0 in range(0, src.shape[1], SUB_LANES):
            ln = pl.ds(c0, SUB_LANES)
            for r0 in range(0, rows, SUB_ROWS):
                rc = min(SUB_ROWS, rows - r0)
                dst[b - 1, pl.ds(r0, rc), ln] = src[pl.ds(r0 + b, rc), ln]


def _rows_at(src, copies, off, n, ln):
    b = off % 8
    return src[pl.ds(off, n), ln] if b == 0 else copies[b - 1, pl.ds(off - b, n), ln]


def _conv31(stg, gsh, cw_ref, cb_ref, out, rows, first_off):
    for c0 in range(0, D_CONV, SUB_LANES):
        ln = pl.ds(c0, SUB_LANES)
        for r0 in range(0, rows, CONV_ROWS):
            acc = jnp.zeros((CONV_ROWS, SUB_LANES), F32) + cb_ref[:, ln]
            for k in range(CONV_KERNEL):
                acc = acc + cw_ref[k:k + 1, ln] * _rows_at(stg, gsh, first_off + k + r0, CONV_ROWS, ln)
            out[pl.ds(r0, CONV_ROWS), ln] = acc


def _mixer_fwd(u, pool_w, pool_scale, conv_w, conv_b, cln_g, cln_b, *, ts=256):
    S = u.shape[0]
    hb = CONV_HALO
    nh = ts // hb

    def body(u_ref, uh_ref, pw_ref, ps_ref, cw_ref, cb_ref, g_ref, b_ref, y_ref, d_ref, sta, stg, gsh, hcs):
        i = pl.program_id(0)
        first = i == 0
        sta[pl.ds(0, hb), :] = jnp.where(first, 0.0, uh_ref[:, 0:D_POOL])
        sta[pl.ds(hb, ts), :] = u_ref[:, 0:D_POOL]
        glu_h = uh_ref[:, D_POOL:D_POOL + D_CONV] * _sigmoid(uh_ref[:, D_POOL + D_CONV:])
        stg[pl.ds(0, hb), :] = jnp.where(first, 0.0, glu_h)
        stg[pl.ds(hb, ts), :] = u_ref[:, D_POOL:D_POOL + D_CONV] * _sigmoid(u_ref[:, D_POOL + D_CONV:])

        pos = (i * ts + lax.broadcasted_iota(jnp.int32, (ts, 1), 0) + 1).astype(F32)
        for g, w in enumerate(POOL_WINDOWS):
            lanes = pl.ds(g * POOL_GROUP, POOL_GROUP)
            a_g = sta[pl.ds(hb, ts), lanes]
            s = a_g
            for j in range(1, w):
                s = s + sta[pl.ds(hb - j, ts), lanes]
            d_g = s / jnp.minimum(pos, float(w)) - a_g
            d_ref[:, lanes] = d_g.astype(BF16)
            y_ref[:, lanes] = (_bdot(d_g, pw_ref[g]) * ps_ref[:, lanes]).astype(BF16)

        _shifted_copies(stg, gsh, hb + ts - 8)
        _conv31(stg, gsh, cw_ref, cb_ref, hcs, ts, hb - (CONV_KERNEL - 1))
        for r0 in range(0, ts, LN_ROWS):
            rows = pl.ds(r0, LN_ROWS)
            ln = _layer_norm_rows(hcs[rows, :], g_ref[...], b_ref[...])
            y_ref[rows, D_POOL:] = (ln * _sigmoid(ln)).astype(BF16)

    fix2 = lambda i: (0, 0)
    return pl.pallas_call(
        body,
        out_shape=[jax.ShapeDtypeStruct((S, D_MODEL), BF16), jax.ShapeDtypeStruct((S, D_POOL), BF16)],
        grid=(S // ts,),
        in_specs=[pl.BlockSpec((ts, 3 * D_POOL), lambda i: (i, 0)),
                  pl.BlockSpec((hb, 3 * D_POOL), lambda i: (jnp.maximum(i * nh - 1, 0), 0)),
                  pl.BlockSpec((4, POOL_GROUP, POOL_GROUP), lambda i: (0, 0, 0)),
                  pl.BlockSpec((1, D_POOL), fix2), pl.BlockSpec((CONV_KERNEL, D_CONV), fix2),
                  pl.BlockSpec((1, D_CONV), fix2), pl.BlockSpec((1, D_CONV), fix2), pl.BlockSpec((1, D_CONV), fix2)],
        out_specs=[pl.BlockSpec((ts, D_MODEL), lambda i: (i, 0)), pl.BlockSpec((ts, D_POOL), lambda i: (i, 0))],
        scratch_shapes=[pltpu.VMEM((hb + ts, D_POOL), F32), pltpu.VMEM((hb + ts, D_CONV), F32),
                        pltpu.VMEM((7, hb + ts - 8, D_CONV), F32), pltpu.VMEM((ts, D_CONV), F32)],
        compiler_params=_cparams(("parallel",)),
        name="mixer_fwd",
    )(u, u, pool_w, pool_scale.reshape(1, D_POOL), conv_w, conv_b.reshape(1, D_CONV), cln_g.reshape(1, D_CONV),
      cln_b.reshape(1, D_CONV))


def _mixer_bwd(u, d, dycat, pool_w, pool_scale, conv_w, conv_b, cln_g, cln_b, *, ts=256):
    S = u.shape[0]
    hb = CONV_HALO
    nh = ts // hb
    n = S // ts
    te = ts + hb
    K = CONV_KERNEL

    def body(u_ref, up_ref, un_ref, d_ref, dy_ref, dyn_ref, pw_ref, ps_ref, cw_ref, cb_ref, g_ref, b_ref,
             du_ref, dpw_ref, dps_ref, dcw_ref, dcb_ref, dg_ref, db_ref, stg, std, sth, gsh, hcs, hsh):
        i = pl.program_id(0)
        first = i == 0
        last = i == n - 1

        @pl.when(first)
        def _():
            dpw_ref[...] = jnp.zeros_like(dpw_ref)
            dps_ref[...] = jnp.zeros_like(dps_ref)
            dcw_ref[...] = jnp.zeros_like(dcw_ref)
            dcb_ref[...] = jnp.zeros_like(dcb_ref)
            dg_ref[...] = jnp.zeros_like(dg_ref)
            db_ref[...] = jnp.zeros_like(db_ref)

        pos_e = (i * ts + lax.broadcasted_iota(jnp.int32, (te, 1), 0) + 1).astype(F32)
        dya = dy_ref[:, 0:D_POOL]
        dya_n = jnp.where(last, 0.0, dyn_ref[:, 0:D_POOL])
        for g, w in enumerate(POOL_WINDOWS):
            lanes = pl.ds(g * POOL_GROUP, POOL_GROUP)
            sl = slice(g * POOL_GROUP, (g + 1) * POOL_GROUP)
            pw = pw_ref[g]
            scale = ps_ref[:, lanes]
            d_g = d_ref[:, lanes]
            pre = _bdot(d_g, pw)
            dps_ref[:, lanes] += jnp.sum(dya[:, sl] * pre, axis=0, keepdims=True)
            dys = dya[:, sl] * scale
            dpw_ref[g] += _bdot(d_g, dys, TN)
            dys_e = jnp.concatenate([dys, dya_n[:, sl] * scale], axis=0)
            dd = _bdot(dys_e, pw, NT)
            std[:, lanes] = dd / jnp.minimum(pos_e, float(w))
            da = -dd[0:ts]
            for m in range(w):
                da = da + std[pl.ds(m, ts), lanes]
            du_ref[:, lanes] = da.astype(BF16)

        glu_p = up_ref[:, D_POOL:D_POOL + D_CONV] * _sigmoid(up_ref[:, D_POOL + D_CONV:])
        stg[pl.ds(0, hb), :] = jnp.where(first, 0.0, glu_p)
        bv = u_ref[:, D_POOL:D_POOL + D_CONV]
        sg = _sigmoid(u_ref[:, D_POOL + D_CONV:])
        stg[pl.ds(hb, ts), :] = bv * sg
        glu_n = un_ref[:, D_POOL:D_POOL + D_CONV] * _sigmoid(un_ref[:, D_POOL + D_CONV:])
        stg[pl.ds(hb + ts, hb), :] = jnp.where(last, 0.0, glu_n)
        _shifted_copies(stg, gsh, hb + te - 8)
        _conv31(stg, gsh, cw_ref, cb_ref, hcs, te, hb - (K - 1))

        sums = [jnp.zeros((8, D_CONV), F32) for _ in range(3)]
        for r0 in range(0, te, LN_ROWS):
            rows = pl.ds(r0, LN_ROWS)
            hc = hcs[rows, :]
            hcc = hc - jnp.mean(hc, axis=-1, keepdims=True)
            rstd = lax.rsqrt(jnp.mean(hcc * hcc, axis=-1, keepdims=True) + LN_EPS)
            xh = hcc * rstd
            ln = xh * g_ref[...] + b_ref[...]
            sl_ = _sigmoid(ln)
            if r0 < ts:
                dyb = dy_ref[rows, D_POOL:]
            else:
                dyb = jnp.where(last, 0.0, dyn_ref[pl.ds(r0 - ts, LN_ROWS), D_POOL:])
            dln = dyb * (sl_ * (1.0 + ln * (1.0 - sl_)))
            dxh = dln * g_ref[...]
            dhc = rstd * (dxh - jnp.mean(dxh, axis=-1, keepdims=True)
                          - xh * jnp.mean(dxh * xh, axis=-1, keepdims=True))
            sth[rows, :] = dhc
            if r0 < ts:
                for n_, term in enumerate((dln * xh, dln, dhc)):
                    sums[n_] = sums[n_] + jnp.sum(term.reshape(LN_ROWS // 8, 8, D_CONV), axis=0)
        dg_ref[...] += jnp.sum(sums[0], axis=0, keepdims=True)
        db_ref[...] += jnp.sum(sums[1], axis=0, keepdims=True)
        dcb_ref[...] += jnp.sum(sums[2], axis=0, keepdims=True)

        _shifted_copies(sth, hsh, te - 8)
        for c0 in range(0, D_CONV, SUB_LANES):
            ln_ = pl.ds(c0, SUB_LANES)
            for r0 in range(0, ts, CONV_ROWS):
                rows = pl.ds(r0, CONV_ROWS)
                dglu = jnp.zeros((CONV_ROWS, SUB_LANES), F32)
                for k in range(K):
                    dglu = dglu + cw_ref[k:k + 1, ln_] * _rows_at(sth, hsh, K - 1 - k + r0, CONV_ROWS, ln_)
                bv = u_ref[rows, pl.ds(D_POOL + c0, SUB_LANES)]
                sg = _sigmoid(u_ref[rows, pl.ds(D_POOL + D_CONV + c0, SUB_LANES)])
                du_ref[rows, pl.ds(D_POOL + c0, SUB_LANES)] = (dglu * sg).astype(BF16)
                du_ref[rows, pl.ds(D_POOL + D_CONV + c0, SUB_LANES)] = (dglu * bv * sg * (1.0 - sg)).astype(BF16)
            for k in range(K):
                tap = jnp.zeros((8, SUB_LANES), F32)
                for r0 in range(0, ts, CONV_ROWS):
                    prod = sth[pl.ds(r0, CONV_ROWS), ln_] * _rows_at(stg, gsh, hb - (K - 1) + k + r0, CONV_ROWS, ln_)
                    tap = tap + jnp.sum(prod.reshape(CONV_ROWS // 8, 8, SUB_LANES), axis=0)
                dcw_ref[k:k + 1, ln_] += jnp.sum(tap, axis=0, keepdims=True)

    fix2 = lambda i: (0, 0)
    prev = lambda i: (jnp.maximum(i * nh - 1, 0), 0)
    nxt = lambda i: (jnp.minimum((i + 1) * nh, S // hb - 1), 0)
    return pl.pallas_call(
        body,
        out_shape=[jax.ShapeDtypeStruct((S, 3 * D_POOL), BF16),
                   jax.ShapeDtypeStruct((4, POOL_GROUP, POOL_GROUP), F32),
                   jax.ShapeDtypeStruct((1, D_POOL), F32),
                   jax.ShapeDtypeStruct((K, D_CONV), F32),
                   jax.ShapeDtypeStruct((1, D_CONV), F32),
                   jax.ShapeDtypeStruct((1, D_CONV), F32),
                   jax.ShapeDtypeStruct((1, D_CONV), F32)],
        grid=(n,),
        in_specs=[pl.BlockSpec((ts, 3 * D_POOL), lambda i: (i, 0)),
                  pl.BlockSpec((hb, 3 * D_POOL), prev),
                  pl.BlockSpec((hb, 3 * D_POOL), nxt),
                  pl.BlockSpec((ts, D_POOL), lambda i: (i, 0)),
                  pl.BlockSpec((ts, D_MODEL), lambda i: (i, 0)),
                  pl.BlockSpec((hb, D_MODEL), nxt),
                  pl.BlockSpec((4, POOL_GROUP, POOL_GROUP), lambda i: (0, 0, 0)),
                  pl.BlockSpec((1, D_POOL), fix2), pl.BlockSpec((K, D_CONV), fix2),
                  pl.BlockSpec((1, D_CONV), fix2), pl.BlockSpec((1, D_CONV), fix2), pl.BlockSpec((1, D_CONV), fix2)],
        out_specs=[pl.BlockSpec((ts, 3 * D_POOL), lambda i: (i, 0)),
                   pl.BlockSpec((4, POOL_GROUP, POOL_GROUP), lambda i: (0, 0, 0)),
                   pl.BlockSpec((1, D_POOL), fix2), pl.BlockSpec((K, D_CONV), fix2),
                   pl.BlockSpec((1, D_CONV), fix2), pl.BlockSpec((1, D_CONV), fix2), pl.BlockSpec((1, D_CONV), fix2)],
        scratch_shapes=[pltpu.VMEM((hb + ts + hb, D_CONV), F32), pltpu.VMEM((te, D_POOL), F32),
                        pltpu.VMEM((te, D_CONV), F32), pltpu.VMEM((7, hb + te - 8, D_CONV), F32),
                        pltpu.VMEM((te, D_CONV), F32), pltpu.VMEM((7, te - 8, D_CONV), F32)],
        compiler_params=_cparams(("arbitrary",)),
        name="mixer_bwd",
    )(u, u, u, d, dycat, dycat, pool_w, pool_scale.reshape(1, D_POOL), conv_w, conv_b.reshape(1, D_CONV),
      cln_g.reshape(1, D_CONV), cln_b.reshape(1, D_CONV))


_GELU_C = math.sqrt(2.0 / math.pi)


def _gelu_parts(x):
    inner = _GELU_C * (x + 0.044715 * x * x * x)
    th = jnp.tanh(inner)
    ge = 0.5 * x * (1.0 + th)
    dge = 0.5 * (1.0 + th) + 0.5 * x * (1.0 - th * th) * (_GELU_C * (1.0 + 3.0 * 0.044715 * x * x))
    return ge, dge


def _ffn_act_fwd(gate, val, dw_w, dw_b, *, ts=256, tc=1408, name):
    S, F = gate.shape
    hb = FFN_HALO
    nh = ts // hb
    tc = _tile(F, tc)

    def body(g_ref, gh_ref, v_ref, w_ref, b_ref, h_ref, st):
        i = pl.program_id(0)
        st[pl.ds(0, hb), :] = jnp.where(i == 0, 0.0, gh_ref[...].astype(F32))
        st[pl.ds(hb, ts), :] = g_ref[...].astype(F32)
        for c0 in range(0, tc, SUB_LANES):
            ln = pl.ds(c0, SUB_LANES)
            w0, w1, w2, b = w_ref[0:1, ln], w_ref[1:2, ln], w_ref[2:3, ln], b_ref[:, ln]
            for r0 in range(0, ts, SUB_ROWS):
                gc = b + w0 * st[pl.ds(hb - 2 + r0, SUB_ROWS), ln] + w1 * st[pl.ds(hb - 1 + r0, SUB_ROWS), ln] \
                    + w2 * st[pl.ds(hb + r0, SUB_ROWS), ln]
                ge, _ = _gelu_parts(gc)
                rows = pl.ds(r0, SUB_ROWS)
                h_ref[rows, ln] = (ge * v_ref[rows, ln].astype(F32)).astype(BF16)

    return pl.pallas_call(
        body,
        out_shape=jax.ShapeDtypeStruct((S, F), BF16),
        grid=(S // ts, F // tc),
        in_specs=[pl.BlockSpec((ts, tc), lambda i, j: (i, j)),
                  pl.BlockSpec((hb, tc), lambda i, j: (jnp.maximum(i * nh - 1, 0), j)),
                  pl.BlockSpec((ts, tc), lambda i, j: (i, j)),
                  pl.BlockSpec((3, tc), lambda i, j: (0, j)),
                  pl.BlockSpec((1, tc), lambda i, j: (0, j))],
        out_specs=pl.BlockSpec((ts, tc), lambda i, j: (i, j)),
        scratch_shapes=[pltpu.VMEM((hb + ts, tc), F32)],
        compiler_params=_cparams(("parallel", "parallel")),
        name=name,
    )(gate, gate, val, dw_w, dw_b.reshape(1, F))


def _ffn_act_bwd(gate, val, dh, dw_w, dw_b, *, ts=256, tc=1408, name):
    S, F = gate.shape
    hb = FFN_HALO
    nh = ts // hb
    n = S // ts
    te = ts + hb
    tc = _tile(F, tc)

    def body(g_ref, gp_ref, gn_ref, v_ref, vn_ref, dh_ref, dhn_ref, w_ref, b_ref,
             dg_ref, dv_ref, dw_ref, db_ref, st, sd):
        i = pl.program_id(1)
        first = i == 0
        last = i == n - 1

        @pl.when(first)
        def _():
            dw_ref[...] = jnp.zeros_like(dw_ref)
            db_ref[...] = jnp.zeros_like(db_ref)

        st[pl.ds(0, hb), :] = jnp.where(first, 0.0, gp_ref[...].astype(F32))
        st[pl.ds(hb, ts), :] = g_ref[...].astype(F32)
        st[pl.ds(hb + ts, hb), :] = jnp.where(last, 0.0, gn_ref[...].astype(F32))
        for c0 in range(0, tc, SUB_LANES):
            ln = pl.ds(c0, SUB_LANES)
            w0, w1, w2, b = w_ref[0:1, ln], w_ref[1:2, ln], w_ref[2:3, ln], b_ref[:, ln]
            db_acc = jnp.zeros((8, SUB_LANES), F32)
            dw_acc = [jnp.zeros((8, SUB_LANES), F32) for _ in range(3)]
            for r0 in range(0, te, SUB_ROWS):
                rc = min(SUB_ROWS, te - r0)
                gc = b + w0 * st[pl.ds(hb - 2 + r0, rc), ln] + w1 * st[pl.ds(hb - 1 + r0, rc), ln] \
                    + w2 * st[pl.ds(hb + r0, rc), ln]
                ge, dge = _gelu_parts(gc)
                if r0 < ts:
                    rows = pl.ds(r0, rc)
                    val, dh = v_ref[rows, ln].astype(F32), dh_ref[rows, ln].astype(F32)
                else:
                    val = jnp.where(last, 0.0, vn_ref[:, ln].astype(F32)[0:rc])
                    dh = jnp.where(last, 0.0, dhn_ref[:, ln].astype(F32)[0:rc])
                dgc = dh * val * dge
                sd[pl.ds(r0, rc), ln] = dgc
                if r0 < ts:
                    dv_ref[rows, ln] = (dh * ge).astype(BF16)
                    db_acc = db_acc + jnp.sum(dgc.reshape(rc // 8, 8, SUB_LANES), axis=0)
                    for k in range(3):
                        tap = dgc * st[pl.ds(hb - 2 + k + r0, rc), ln]
                        dw_acc[k] = dw_acc[k] + jnp.sum(tap.reshape(rc // 8, 8, SUB_LANES), axis=0)
            db_ref[:, ln] += jnp.sum(db_acc, axis=0, keepdims=True)
            for k in range(3):
                dw_ref[k:k + 1, ln] += jnp.sum(dw_acc[k], axis=0, keepdims=True)
            for r0 in range(0, ts, SUB_ROWS):
                dgate = w0 * sd[pl.ds(2 + r0, SUB_ROWS), ln] + w1 * sd[pl.ds(1 + r0, SUB_ROWS), ln] \
                    + w2 * sd[pl.ds(r0, SUB_ROWS), ln]
                dg_ref[pl.ds(r0, SUB_ROWS), ln] = dgate.astype(BF16)

    cur = lambda j, i: (i, j)
    prev = lambda j, i: (jnp.maximum(i * nh - 1, 0), j)
    nxt = lambda j, i: (jnp.minimum((i + 1) * nh, S // hb - 1), j)
    return pl.pallas_call(
        body,
        out_shape=[jax.ShapeDtypeStruct((S, F), BF16), jax.ShapeDtypeStruct((S, F), BF16),
                   jax.ShapeDtypeStruct((3, F), F32), jax.ShapeDtypeStruct((1, F), F32)],
        grid=(F // tc, n),
        in_specs=[pl.BlockSpec((ts, tc), cur), pl.BlockSpec((hb, tc), prev), pl.BlockSpec((hb, tc), nxt),
                  pl.BlockSpec((ts, tc), cur), pl.BlockSpec((hb, tc), nxt),
                  pl.BlockSpec((ts, tc), cur), pl.BlockSpec((hb, tc), nxt),
                  pl.BlockSpec((3, tc), lambda j, i: (0, j)), pl.BlockSpec((1, tc), lambda j, i: (0, j))],
        out_specs=[pl.BlockSpec((ts, tc), cur), pl.BlockSpec((ts, tc), cur),
                   pl.BlockSpec((3, tc), lambda j, i: (0, j)), pl.BlockSpec((1, tc), lambda j, i: (0, j))],
        scratch_shapes=[pltpu.VMEM((hb + ts + hb, tc), F32), pltpu.VMEM((te, tc), F32)],
        compiler_params=_cparams(("parallel", "arbitrary")),
        name=name,
    )(gate, gate, gate, val, val, dh, dh, dw_w, dw_b.reshape(1, F))


def _ln_bwd(z, ln_g, ln_b, dout, *, loss_head=False, ts=256, dep=None, name):
    S, D = z.shape

    def body(z_ref, g_ref, b_ref, do_ref, *rest):
        dz_ref, dg_ref, db_ref, loss_ref = rest[-4:]
        i = pl.program_id(0)

        @pl.when(i == 0)
        def _():
            dg_ref[...] = jnp.zeros_like(dg_ref)
            db_ref[...] = jnp.zeros_like(db_ref)
            loss_ref[...] = jnp.zeros_like(loss_ref)

        zt = z_ref[...]
        mu = jnp.mean(zt, axis=-1, keepdims=True)
        zc = zt - mu
        rstd = lax.rsqrt(jnp.mean(zc * zc, axis=-1, keepdims=True) + LN_EPS)
        xh = zc * rstd
        if loss_head:
            err = xh * g_ref[...] + b_ref[...] - do_ref[...]
            loss_ref[...] += 0.5 * jnp.sum(jnp.mean(err * err, axis=-1, keepdims=True))
            do = err * (1.0 / D)
        else:
            do = do_ref[...]
        dg_ref[...] += jnp.sum(do * xh, axis=0, keepdims=True)
        db_ref[...] += jnp.sum(do, axis=0, keepdims=True)
        dxh = do * g_ref[...]
        dz_ref[...] = rstd * (dxh - jnp.mean(dxh, axis=-1, keepdims=True)
                              - xh * jnp.mean(dxh * xh, axis=-1, keepdims=True))

    row = lambda i: (i, 0)
    fix = lambda i: (0, 0)
    return pl.pallas_call(
        body,
        out_shape=[jax.ShapeDtypeStruct((S, D), F32), jax.ShapeDtypeStruct((1, D), F32),
                   jax.ShapeDtypeStruct((1, D), F32), jax.ShapeDtypeStruct((8, 128), F32)],
        grid=(S // ts,),
        in_specs=[pl.BlockSpec((ts, D), row), pl.BlockSpec((1, D), fix), pl.BlockSpec((1, D), fix),
                  pl.BlockSpec((ts, D), row)] + ([pl.BlockSpec(memory_space=pl.ANY)] if dep is not None else []),
        out_specs=[pl.BlockSpec((ts, D), row), pl.BlockSpec((1, D), fix), pl.BlockSpec((1, D), fix),
                   pl.BlockSpec((8, 128), fix)],
        compiler_params=_cparams(("arbitrary",)),
        name=name,
    )(z, ln_g.reshape(1, D), ln_b.reshape(1, D), dout, *([dep] if dep is not None else []))


def _ple_bwd(dz, gate, proj, *, ts=256, name):
    S, D = dz.shape

    def body(dz_ref, g_ref, p_ref, ds_ref, dp_ref, db_ref):
        @pl.when(pl.program_id(0) == 0)
        def _():
            db_ref[...] = jnp.zeros_like(db_ref)

        dzt = dz_ref[...]
        g = g_ref[...]
        ds = dzt * p_ref[...] * g * (1.0 - g)
        ds_ref[...] = ds.astype(BF16)
        dp_ref[...] = (dzt * g).astype(BF16)
        db_ref[...] += jnp.sum(ds, axis=0, keepdims=True)

    row = lambda i: (i, 0)
    return pl.pallas_call(
        body,
        out_shape=[jax.ShapeDtypeStruct((S, D), BF16), jax.ShapeDtypeStruct((S, D), BF16),
                   jax.ShapeDtypeStruct((1, D), F32)],
        grid=(S // ts,),
        in_specs=[pl.BlockSpec((ts, D), row)] * 3,
        out_specs=[pl.BlockSpec((ts, D), row), pl.BlockSpec((ts, D), row), pl.BlockSpec((1, D), lambda i: (0, 0))],
        compiler_params=_cparams(("arbitrary",)),
        name=name,
    )(dz, gate, proj)


HEAD_PAIR = 2 * HEAD_DIM


def _attn_probs(qj, kc, bias, qb):
    s = _bdot(qj, kc, NT) * (HEAD_DIM ** -0.5) + bias
    kpos = qb * Q_BLOCK + lax.broadcasted_iota(jnp.int32, (1, KV_SPAN), 1)
    s = jnp.where(kpos >= KV_PAD, s, NEG_INF)
    m = jnp.max(s, axis=-1, keepdims=True)
    e = jnp.exp(s - m)
    return e * (1.0 / jnp.sum(e, axis=-1, keepdims=True))


def _pad_keys(qb, k_ref, v_ref, kp, vp):
    @pl.when(qb == 0)
    def _():
        kp[pl.ds(0, KV_PAD), :] = jnp.zeros((KV_PAD, HEAD_PAIR), BF16)
        vp[pl.ds(0, KV_PAD), :] = jnp.zeros((KV_PAD, HEAD_PAIR), BF16)
        kp[pl.ds(KV_PAD, k_ref.shape[0]), :] = k_ref[...]
        vp[pl.ds(KV_PAD, v_ref.shape[0]), :] = v_ref[...]


def _attn_fwd(qkv, bias):
    S = qkv.shape[0]
    nhp = N_HEADS // 2

    def body(q_ref, k_ref, v_ref, b_ref, o_ref, kp, vp):
        qb = pl.program_id(1)
        _pad_keys(qb, k_ref, v_ref, kp, vp)
        span = pl.ds(pl.multiple_of(qb * Q_BLOCK, Q_BLOCK), KV_SPAN)
        kc, vc = kp[span, :], vp[span, :]
        qt = q_ref[...]
        first = lax.broadcasted_iota(jnp.int32, (1, HEAD_PAIR), 1) < HEAD_DIM
        outs = []
        for j in range(2):
            qj = jnp.where(first if j == 0 else ~first, qt, jnp.zeros_like(qt))
            outs.append(_bdot(_attn_probs(qj, kc, b_ref[j], qb), vc))
        o_ref[...] = jnp.where(first, outs[0], outs[1]).astype(BF16)

    return pl.pallas_call(
        body,
        out_shape=jax.ShapeDtypeStruct((S, D_MODEL), BF16),
        grid=(nhp, S // Q_BLOCK),
        in_specs=[pl.BlockSpec((Q_BLOCK, HEAD_PAIR), lambda h, i: (i, h)),
                  pl.BlockSpec((S, HEAD_PAIR), lambda h, i: (0, nhp + h)),
                  pl.BlockSpec((S, HEAD_PAIR), lambda h, i: (0, 2 * nhp + h)),
                  pl.BlockSpec((2, Q_BLOCK, KV_SPAN), lambda h, i: (h, 0, 0))],
        out_specs=pl.BlockSpec((Q_BLOCK, HEAD_PAIR), lambda h, i: (i, h)),
        scratch_shapes=[pltpu.VMEM((KV_PAD + S, HEAD_PAIR), BF16), pltpu.VMEM((KV_PAD + S, HEAD_PAIR), BF16)],
        compiler_params=_cparams(("parallel", "arbitrary")),
        name="attn_fwd",
    )(qkv, qkv, qkv, bias)


def _attn_bwd(qkv, bias, do):
    S = qkv.shape[0]
    nhp = N_HEADS // 2
    nq = S // Q_BLOCK
    scale = HEAD_DIM ** -0.5

    def body(q_ref, k_ref, v_ref, b_ref, do_ref, dq_ref, dk_ref, dv_ref, db_ref, kp, vp, dka, dva):
        qb = pl.program_id(1)
        _pad_keys(qb, k_ref, v_ref, kp, vp)

        @pl.when(qb == 0)
        def _():
            dka[...] = jnp.zeros_like(dka)
            dva[...] = jnp.zeros_like(dva)
            db_ref[...] = jnp.zeros_like(db_ref)

        span = pl.ds(pl.multiple_of(qb * Q_BLOCK, Q_BLOCK), KV_SPAN)
        kc, vc = kp[span, :], vp[span, :]
        qt, dot = q_ref[...], do_ref[...]
        first = lax.broadcasted_iota(jnp.int32, (1, HEAD_PAIR), 1) < HEAD_DIM
        dqs = []
        for j in range(2):
            mine = first if j == 0 else ~first
            qj = jnp.where(mine, qt, jnp.zeros_like(qt))
            doj = jnp.where(mine, dot, jnp.zeros_like(dot))
            p = _attn_probs(qj, kc, b_ref[j], qb)
            dva[span, :] += _bdot(p, doj, TN)
            dp = _bdot(doj, vc, NT)
            ds = p * (dp - jnp.sum(p * dp, axis=-1, keepdims=True))
            db_ref[j] += ds
            dqs.append(_bdot(ds, kc))
            dka[span, :] += scale * _bdot(ds, qj, TN)
        dq_ref[...] = (scale * jnp.where(first, dqs[0], dqs[1])).astype(BF16)

        @pl.when(qb == nq - 1)
        def _():
            dk_ref[...] = dka[pl.ds(KV_PAD, S), :].astype(BF16)
            dv_ref[...] = dva[pl.ds(KV_PAD, S), :].astype(BF16)

    blk = pl.BlockSpec((Q_BLOCK, HEAD_PAIR), lambda h, i: (i, h))
    col = pl.BlockSpec((S, HEAD_PAIR), lambda h, i: (0, h))
    bsp = pl.BlockSpec((2, Q_BLOCK, KV_SPAN), lambda h, i: (h, 0, 0))
    return pl.pallas_call(
        body,
        out_shape=[jax.ShapeDtypeStruct((S, D_MODEL), BF16)] * 3
        + [jax.ShapeDtypeStruct((N_HEADS, Q_BLOCK, KV_SPAN), F32)],
        grid=(nhp, nq),
        in_specs=[blk, pl.BlockSpec((S, HEAD_PAIR), lambda h, i: (0, nhp + h)),
                  pl.BlockSpec((S, HEAD_PAIR), lambda h, i: (0, 2 * nhp + h)), bsp, blk],
        out_specs=[blk, col, col, bsp],
        scratch_shapes=[pltpu.VMEM((KV_PAD + S, HEAD_PAIR), BF16), pltpu.VMEM((KV_PAD + S, HEAD_PAIR), BF16),
                        pltpu.VMEM((KV_PAD + S, HEAD_PAIR), F32), pltpu.VMEM((KV_PAD + S, HEAD_PAIR), F32)],
        compiler_params=_cparams(("parallel", "arbitrary")),
        name="attn_bwd",
    )(qkv, qkv, qkv, bias, do)


def _bias_blocks(rel_bias):
    H = rel_bias.shape[0]
    n_e = BAND + CHUNK - 1
    n_clip = KV_PAD + CHUNK - 1 - MAX_REL + 1
    e = jnp.concatenate([jnp.broadcast_to(rel_bias[:, 2 * MAX_REL:], (H, n_clip)),
                         jnp.flip(rel_bias[:, 2 * MAX_REL - (n_e - n_clip):2 * MAX_REL], axis=1)], axis=1)
    skew = jnp.pad(jnp.tile(e, (1, CHUNK)), ((0, 0), (0, CHUNK))).reshape(H, CHUNK, n_e + 1)
    band = jnp.flip(skew, axis=1)[:, :, :BAND]
    rows = [jnp.pad(band, ((0, 0), (0, 0), (c * CHUNK, KV_SPAN - BAND - c * CHUNK)), constant_values=NEG_INF)
            for c in range(Q_BLOCK // CHUNK)]
    return jnp.concatenate(rows, axis=1)


def _bias_blocks_grad(dblk):
    H = dblk.shape[0]
    n_e = BAND + CHUNK - 1
    n_clip = KV_PAD + CHUNK - 1 - MAX_REL + 1
    parts = jnp.stack([dblk[:, c * CHUNK:(c + 1) * CHUNK, c * CHUNK:c * CHUNK + BAND]
                       for c in range(Q_BLOCK // CHUNK)], axis=1)
    parts = jnp.flip(parts, axis=2)
    parts = jnp.pad(parts, ((0, 0), (0, 0), (0, 0), (0, n_e + 1 - BAND)))
    skew = parts.reshape(H, Q_BLOCK // CHUNK, CHUNK * (n_e + 1))[:, :, :CHUNK * n_e]
    skew = skew.reshape(H, Q_BLOCK, n_e)
    skew = jnp.pad(skew, ((0, 0), (0, 0), (0, 1)))

    def body(s_ref, o_ref):
        de = jnp.sum(s_ref[...], axis=0, keepdims=True)
        lane = lax.broadcasted_iota(jnp.int32, de.shape, 1)
        far = jnp.sum(jnp.where(lane < n_clip, de, 0.0), axis=-1, keepdims=True)
        o_ref[...] = jnp.where(lane == 0, far, jnp.where(lane < n_clip, 0.0, de))

    de = pl.pallas_call(
        body,
        out_shape=jax.ShapeDtypeStruct((H, 1, n_e + 1), F32),
        grid=(H,),
        in_specs=[pl.BlockSpec((None, Q_BLOCK, n_e + 1), lambda h: (h, 0, 0))],
        out_specs=pl.BlockSpec((None, 1, n_e + 1), lambda h: (h, 0, 0)),
        compiler_params=_cparams(("parallel",)),
        name="bias_grad_sum",
    )(skew).reshape(H, n_e + 1)
    near = jnp.flip(de[:, n_clip:n_e], axis=1)
    return jnp.concatenate([jnp.zeros((H, 2 * MAX_REL - (n_e - n_clip)), F32), near, de[:, 0:1]], axis=1)


def _ffn_forward(r1, p_l, w, l, ready):
    ready(f"up{l}", r1)
    up_g = _mm_rows([(r1, w["ffn_up_g"][l], False)], out_dtype=BF16, name=f"ffn_up_g{l}")
    up_v = _mm_rows([(r1, w["ffn_up_v"][l], False)], out_dtype=BF16, name=f"ffn_up_v{l}")
    h = _ffn_act_fwd(up_g, up_v, w["ffn_dw_w"][l], w["ffn_dw_b"][l], name=f"ffn_act{l}")
    ready(f"dn{l}", h)
    z2, r2, gate, proj = _proj_ln(r1, h, w["ffn_w_down"][l], w["ln_ffn_g"][l], w["ln_ffn_b"][l],
                                  ple=(w["ple_w_gate"][l], w["ple_b_gate"][l], p_l, w["ple_w_proj"][l]),
                                  name=f"ffn_down_ln{l}")
    return dict(r1=r1, up_g=up_g, up_v=up_v, h=h, z2=z2, gate=gate, proj=proj), r2


def _ffn_backward(sv, dz2, p_l, w, l, grads):
    r1 = sv["r1"]
    ds, dproj, db_gate = _ple_bwd(dz2, sv["gate"], sv["proj"], name=f"ple_bwd{l}")
    dh = _mm_rows([(dz2, w["ffn_w_down"][l], True)], out_dtype=BF16, name=f"ffn_dh{l}")
    dgate, dval, d_dw_w, d_dw_b = _ffn_act_bwd(sv["up_g"], sv["up_v"], dh, w["ffn_dw_w"][l], w["ffn_dw_b"][l],
                                               name=f"ffn_act_bwd{l}")
    grads["ffn_w_down"][l] = _mm(sv["h"], dz2, ta=True, tm=1408, tn=1024, tk=512, name=f"d_ffn_w_down{l}")
    grads["ffn_up_g"][l] = _mm(r1, dgate, ta=True, tm=1024, tn=1408, tk=512, name=f"d_ffn_up_g{l}")
    grads["ffn_up_v"][l] = _mm(r1, dval, ta=True, tm=1024, tn=1408, tk=512, name=f"d_ffn_up_v{l}")
    grads["ple_w_gate"][l] = _mm(r1, ds, ta=True, tm=1024, tn=1024, tk=512, name=f"d_ple_w_gate{l}")
    grads["ple_w_proj"][l] = _mm(p_l, dproj, ta=True, tm=256, tn=1024, tk=512, name=f"d_ple_w_proj{l}")
    grads["ffn_dw_w"][l] = d_dw_w
    grads["ffn_dw_b"][l] = d_dw_b[0]
    grads["ple_b_gate"][l] = db_gate[0]
    return _mm_rows([(ds, w["ple_w_gate"][l], True), (dgate, w["ffn_up_g"][l], True), (dval, w["ffn_up_v"][l], True)],
                    add=dz2, add_scale=ALPHA, name=f"dr1_{l}")


def _local_step(x, p, target, w, ready=lambda group, after: None, emit=lambda group, grads: None):
    grads = {k: [None, None] for k in ("ffn_w_down", "ffn_up_g", "ffn_up_v", "ple_w_gate", "ple_w_proj", "ffn_dw_w",
                                       "ffn_dw_b", "ple_b_gate", "ln_ffn_g", "ln_ffn_b", "ln_mix_g", "ln_mix_b")}

    ready("mix", None)
    u = _mm_rows([(x, w["mix_w_in"], False)], name="mix_in")
    ycat, dpool = _mixer_fwd(u, w["pool_w"], w["pool_scale"], w["conv_dw_w"], w["conv_dw_b"], w["conv_ln_g"],
                             w["conv_ln_b"])
    z1, r1 = _proj_ln(x, ycat, w["mix_w_out"], w["ln_mix_g"][0], w["ln_mix_b"][0], name="mix_out_ln")
    sv0, r2 = _ffn_forward(r1, p[0], w, 0, ready)

    ready("attn", r2)
    qkv = _mm_rows([(r2, w["attn_w_qkv"], False)], out_dtype=BF16, name="attn_qkv")
    bias = _bias_blocks(w["attn_rel_bias"])
    attn = _attn_fwd(qkv, bias)
    z3, r3 = _proj_ln(r2, attn, w["attn_w_o"], w["ln_mix_g"][1], w["ln_mix_b"][1], name="attn_out_ln")
    sv1, _ = _ffn_forward(r3, p[1], w, 1, ready)

    dz4, grads["ln_ffn_g"][1], grads["ln_ffn_b"][1], loss = _ln_bwd(sv1["z2"], w["ln_ffn_g"][1], w["ln_ffn_b"][1],
                                                                    target, loss_head=True, name="loss_ln_bwd")
    dr3 = _ffn_backward(sv1, dz4, p[1], w, 1, grads)
    dz3, grads["ln_mix_g"][1], grads["ln_mix_b"][1], _ = _ln_bwd(z3, w["ln_mix_g"][1], w["ln_mix_b"][1], dr3,
                                                                dep=emit("ffn1", grads), name="ln_mix_bwd1")
    grads["attn_w_o"] = _mm(attn, dz3, ta=True, tm=1024, tn=1024, tk=512, name="d_attn_w_o")
    dattn = _mm_rows([(dz3, w["attn_w_o"], True)], out_dtype=BF16, name="d_attn")
    dq, dk, dv, dbias = _attn_bwd(qkv, bias, dattn)
    grads["attn_rel_bias"] = _bias_blocks_grad(dbias)
    dqkv = jnp.concatenate([dq, dk, dv], axis=1)
    grads["attn_w_qkv"] = _mm(r2, dqkv, ta=True, tm=1024, tn=1024, tk=512, name="d_attn_w_qkv")
    dr2 = _mm_rows([(dqkv, w["attn_w_qkv"], True)], add=dz3, add_scale=ALPHA, dep=emit("attn", grads), name="dr2")

    dz2, grads["ln_ffn_g"][0], grads["ln_ffn_b"][0], _ = _ln_bwd(sv0["z2"], w["ln_ffn_g"][0], w["ln_ffn_b"][0], dr2,
                                                                name="ln_ffn_bwd0")
    dr1 = _ffn_backward(sv0, dz2, p[0], w, 0, grads)
    dz1, grads["ln_mix_g"][0], grads["ln_mix_b"][0], _ = _ln_bwd(z1, w["ln_mix_g"][0], w["ln_mix_b"][0], dr1,
                                                                dep=emit("ffn0", grads), name="ln_mix_bwd0")
    grads["mix_w_out"] = _mm(ycat, dz1, ta=True, tm=1024, tn=1024, tk=512, name="d_mix_w_out")
    dycat = _mm_rows([(dz1, w["mix_w_out"], True)], name="d_ycat")
    du, g_pw, g_ps, g_cw, g_cb, g_cg, g_cbb = _mixer_bwd(u, dpool, dycat, w["pool_w"], w["pool_scale"],
                                                         w["conv_dw_w"], w["conv_dw_b"], w["conv_ln_g"],
                                                         w["conv_ln_b"])
    grads["mix_w_in"] = _mm(x, du, ta=True, tm=1024, tn=512, tk=512, name="d_mix_w_in")
    grads["conv_dw_w"] = g_cw
    grad_x = _mm_rows([(du, w["mix_w_in"], True)], add=dz1, add_scale=ALPHA, dep=emit("mix", grads), name="grad_x")
    grads.update(pool_w=g_pw, pool_scale=g_ps[0], conv_dw_w=g_cw, conv_dw_b=g_cb[0], conv_ln_g=g_cg[0],
                 conv_ln_b=g_cbb[0])
    for kname in ("ln_ffn_g", "ln_ffn_b", "ln_mix_g", "ln_mix_b"):
        grads[kname] = [a[0] for a in grads[kname]]
    return loss[0, 0], grad_x, grads


def _exchange(bufs, places, *, name):
    nb = len(bufs)

    def body(*refs):
        srcs, dsts = refs[:nb], refs[nb:2 * nb]
        send_sems, recv_sems, local_sems = refs[2 * nb:]
        x, y, c = lax.axis_index("x"), lax.axis_index("y"), lax.axis_index("c")
        me = 4 * x + 2 * y + c
        local = []
        remote = []
        for b in range(nb):
            pieces = places[b] == "pieces"
            shape = bufs[b].shape
            cp = pltpu.make_async_copy(srcs[b].at[me] if pieces else srcs[b], _slot(dsts[b], places[b], shape, me),
                                       local_sems.at[b])
            cp.start()
            local.append(cp)
            for d, dev, peer in _peers(x, y, c):
                src = srcs[b].at[peer] if pieces else srcs[b]
                out = pltpu.make_async_remote_copy(
                    src_ref=src, dst_ref=_slot(dsts[b], places[b], shape, me),
                    send_sem=send_sems.at[b * N_DEV + d], recv_sem=recv_sems.at[b * N_DEV + d],
                    device_id=dev, device_id_type=pl.DeviceIdType.MESH)
                out.start()
                inc = pltpu.make_async_remote_copy(
                    src_ref=src, dst_ref=_slot(dsts[b], places[b], shape, peer),
                    send_sem=send_sems.at[b * N_DEV + d], recv_sem=recv_sems.at[b * N_DEV + d],
                    device_id=dev, device_id_type=pl.DeviceIdType.MESH)
                remote.append((out, inc))
        for cp in local:
            cp.wait()
        for out, inc in remote:
            out.wait_send()
            inc.wait_recv()

    out_shapes = [jax.ShapeDtypeStruct(_result_shape(b, place), b.dtype) for b, place in zip(bufs, places)]
    any_spec = pl.BlockSpec(memory_space=pl.ANY)
    return pl.pallas_call(
        body,
        out_shape=out_shapes,
        in_specs=[any_spec] * nb,
        out_specs=[any_spec] * nb,
        scratch_shapes=[pltpu.SemaphoreType.DMA((nb * N_DEV,)), pltpu.SemaphoreType.DMA((nb * N_DEV,)),
                        pltpu.SemaphoreType.DMA((nb,))],
        compiler_params=pltpu.CompilerParams(has_side_effects=True),
        name=name,
    )(*bufs)


_HBM = pl.BlockSpec(memory_space=pltpu.HBM)
_SEM = pl.BlockSpec(memory_space=pltpu.SEMAPHORE)
_EFFECT = pltpu.SideEffectType.DATAFLOW_SIDE_EFFECTING


def _slot(ref, place, shape, k):
    if place in ("stack", "pieces"):
        return ref.at[k]
    ax = place[1]
    n = shape[ax]
    return ref.at[(slice(None),) * ax + (pl.ds(pl.multiple_of(k * n, n), n),)]


def _result_shape(buf, place):
    if place == "stack":
        return (N_DEV,) + buf.shape
    if place == "pieces":
        return buf.shape
    return tuple(s * N_DEV if i == place[1] else s for i, s in enumerate(buf.shape))


def _peers(x, y, c):
    for d in range(1, N_DEV):
        px, py, pc = x ^ ((d >> 2) & 1), y ^ ((d >> 1) & 1), c ^ (d & 1)
        yield d, (px, py, pc), 4 * px + 2 * py + pc


def _exchange_start(bufs, places, after, *, name):
    nb = len(bufs)
    lands = [lax.empty(_result_shape(b, p_), b.dtype) for b, p_ in zip(bufs, places)]
    has_after = after is not None

    def body(*refs):
        srcs, dsts = refs[:nb], refs[nb:2 * nb]
        outs = refs[2 * nb + has_after:]
        send_sems, recv_sems, token = outs[0], outs[1], outs[2 + 2 * nb]
        x, y, c = lax.axis_index("x"), lax.axis_index("y"), lax.axis_index("c")
        me = 4 * x + 2 * y + c
        for b in range(nb):
            for d, dev, peer in _peers(x, y, c):
                pltpu.make_async_remote_copy(
                    src_ref=srcs[b].at[peer] if places[b] == "pieces" else srcs[b],
                    dst_ref=_slot(dsts[b], places[b], bufs[b].shape, me),
                    send_sem=send_sems.at[b * N_DEV + d], recv_sem=recv_sems.at[b * N_DEV + d],
                    device_id=dev, device_id_type=pl.DeviceIdType.MESH).start()
            pltpu.make_async_copy(srcs[b].at[me] if places[b] == "pieces" else srcs[b],
                                  _slot(dsts[b], places[b], bufs[b].shape, me), recv_sems.at[b * N_DEV]).start()
        token[...] = jnp.zeros_like(token)

    sems = pltpu.SemaphoreType.DMA((nb * N_DEV,))
    ins = [pltpu.with_memory_space_constraint(a, pltpu.HBM) for a in list(bufs) + lands]
    out = pl.pallas_call(
        body,
        out_shape=(sems, sems, *[pltpu.HBM(a.shape, a.dtype) for a in ins], jax.ShapeDtypeStruct((8, 128), F32)),
        in_specs=[_HBM] * (2 * nb) + ([pl.BlockSpec(memory_space=pl.ANY)] if has_after else []),
        out_specs=(_SEM, _SEM, *[_HBM] * (2 * nb), pl.BlockSpec(memory_space=pltpu.VMEM)),
        input_output_aliases={i: 2 + i for i in range(2 * nb)},
        compiler_params=pltpu.CompilerParams(has_side_effects=_EFFECT),
        name=name,
    )(*ins, *([after] if has_after else []))
    return dict(send=out[0], recv=out[1], srcs=out[2:2 + nb], lands=out[2 + nb:2 + 2 * nb], token=out[-1],
                places=places)


def _exchange_wait(h, after, *, name):
    nb = len(h["srcs"])
    places = h["places"]
    shapes = [a.shape for a in h["srcs"]]

    def body(*refs):
        srcs, dsts, send_sems, recv_sems = refs[:nb], refs[nb:2 * nb], refs[2 * nb], refs[2 * nb + 1]
        x, y, c = lax.axis_index("x"), lax.axis_index("y"), lax.axis_index("c")
        me = 4 * x + 2 * y + c
        for b in range(nb):
            pieces = places[b] == "pieces"
            for d, dev, peer in _peers(x, y, c):
                cp = pltpu.make_async_remote_copy(
                    src_ref=srcs[b].at[peer] if pieces else srcs[b],
                    dst_ref=_slot(dsts[b], places[b], shapes[b], peer),
                    send_sem=send_sems.at[b * N_DEV + d], recv_sem=recv_sems.at[b * N_DEV + d],
                    device_id=dev, device_id_type=pl.DeviceIdType.MESH)
                cp.wait_send()
                cp.wait_recv()
            pltpu.make_async_copy(srcs[b].at[me] if pieces else srcs[b], _slot(dsts[b], places[b], shapes[b], me),
                                  recv_sems.at[b * N_DEV]).wait()

    ins = list(h["srcs"]) + list(h["lands"])
    out = pl.pallas_call(
        body,
        out_shape=tuple(pltpu.HBM(a.shape, a.dtype) for a in ins),
        in_specs=[_HBM] * (2 * nb) + [_SEM, _SEM, pl.BlockSpec(memory_space=pl.ANY)],
        out_specs=tuple([_HBM] * (2 * nb)),
        input_output_aliases={i: i for i in range(2 * nb)},
        compiler_params=pltpu.CompilerParams(has_side_effects=_EFFECT),
        name=name,
    )(*ins, h["send"], h["recv"], after)
    return out[nb:]


def _adamw(recv, w, m, v, *, name):
    R, C = w.shape
    tr = R
    for cand in (512, 256, 128, 64, 32, 16):
        if R % cand == 0 and cand * C * 4 <= 2 * 1024 * 1024:
            tr = cand
            break
    c1 = 1.0 - ADAM_B1 ** ADAM_STEP
    c2 = 1.0 - ADAM_B2 ** ADAM_STEP

    def body(r_ref, w_ref, m_ref, v_ref, g_ref, d_ref, mo_ref, vo_ref):
        g = r_ref[0].astype(F32)
        for i in range(1, N_DEV):
            g = g + r_ref[i].astype(F32)
        m_new = ADAM_B1 * m_ref[...] + (1.0 - ADAM_B1) * g
        v_new = ADAM_B2 * v_ref[...] + (1.0 - ADAM_B2) * (g * g)
        m_hat = m_new / c1
        v_hat = v_new / c2
        g_ref[...] = g
        d_ref[...] = -ADAM_LR * (m_hat / (jnp.sqrt(v_hat) + ADAM_EPS) + ADAM_WD * w_ref[...])
        mo_ref[...] = m_new
        vo_ref[...] = v_new

    row = pl.BlockSpec((tr, C), lambda i: (i, 0))
    return pl.pallas_call(
        body,
        out_shape=[jax.ShapeDtypeStruct((R, C), F32)] * 4,
        grid=(R // tr,),
        in_specs=[pl.BlockSpec((N_DEV, tr, C), lambda i: (0, i, 0)), row, row, row],
        out_specs=[row] * 4,
        compiler_params=_cparams(("parallel",)),
        name=name,
    )(recv, w, m, v)


def _ffn_groups(l):
    return ((f"up{l}", (("ffn_w_up", l, BF16, "stack"), ("ffn_dw_w", l, F32, "stack"))),
            (f"dn{l}", (("ffn_w_down", l, BF16, ("axis", 0)), ("ple_w_gate", l, BF16, ("axis", 0)),
                        ("ple_w_proj", l, BF16, ("axis", 1)))))


_GATHER_GROUPS = (
    ("mix", (("mix_w_in", 0, BF16, "stack"), ("conv_dw_w", 0, F32, "stack"), ("mix_w_out", 0, BF16, ("axis", 0)))),
    *_ffn_groups(0),
    ("attn", (("attn_w_qkv", 0, BF16, ("axis", 1)), ("attn_w_o", 0, BF16, ("axis", 0)))),
    *_ffn_groups(1))
_SHARDED = ("mix_w_in", "conv_dw_w", "mix_w_out", "attn_w_qkv", "attn_w_o", "ffn_w_up", "ffn_dw_w", "ffn_w_down",
            "ple_w_gate", "ple_w_proj")
_REPLICATED = ("pool_w", "pool_scale", "conv_dw_b", "conv_ln_g", "conv_ln_b", "attn_rel_bias", "ln_mix_g",
               "ln_mix_b", "ffn_dw_b", "ple_b_gate", "ln_ffn_g", "ln_ffn_b")


def _pack_rows(parts, row_mult, dtype):
    lead = parts[0].shape[:-1]
    flat = jnp.concatenate([a.astype(dtype) for a in parts], axis=-1)
    n = flat.shape[-1]
    unit = row_mult * LANES
    padded = -(-n // unit) * unit
    flat = jnp.pad(flat, [(0, 0)] * len(lead) + [(0, padded - n)])
    return flat.reshape(lead + (padded // LANES, LANES))


def _unpack(flat2d, shapes):
    flat = flat2d.reshape(-1)
    out, o = [], 0
    for s in shapes:
        n = math.prod(s)
        out.append(flat[o:o + n].reshape(s))
        o += n
    return out


def _full_from_shards(g, axis):
    parts = jnp.moveaxis(g, 0, axis)
    shp = list(g.shape[1:])
    shp[axis] *= g.shape[0]
    return parts.reshape(shp)


def _pieces_from_full(full, axis, k=N_DEV):
    shp = list(full.shape)
    n = shp[axis] // k
    t = full.reshape(shp[:axis] + [k, n] + shp[axis + 1:])
    return jnp.moveaxis(t, axis, 0)


def kernel(x, p, mix_w_in, pool_w, pool_scale, conv_dw_w, conv_dw_b, conv_ln_g, conv_ln_b, mix_w_out, attn_w_qkv, attn_rel_bias, attn_w_o, ln_mix_g, ln_mix_b, ffn_w_up, ffn_dw_w, ffn_dw_b, ffn_w_down, ple_w_proj, ple_w_gate, ple_b_gate, ln_ffn_g, ln_ffn_b, loss_target, m_mix_w_in, m_pool_w, m_pool_scale, m_conv_dw_w, m_conv_dw_b, m_conv_ln_g, m_conv_ln_b, m_mix_w_out, m_attn_w_qkv, m_attn_rel_bias, m_attn_w_o, m_ln_mix_g, m_ln_mix_b, m_ffn_w_up, m_ffn_dw_w, m_ffn_dw_b, m_ffn_w_down, m_ple_w_proj, m_ple_w_gate, m_ple_b_gate, m_ln_ffn_g, m_ln_ffn_b, v_mix_w_in, v_pool_w, v_pool_scale, v_conv_dw_w, v_conv_dw_b, v_conv_ln_g, v_conv_ln_b, v_mix_w_out, v_attn_w_qkv, v_attn_rel_bias, v_attn_w_o, v_ln_mix_g, v_ln_mix_b, v_ffn_w_up, v_ffn_dw_w, v_ffn_dw_b, v_ffn_w_down, v_ple_w_proj, v_ple_w_gate, v_ple_b_gate, v_ln_ffn_g, v_ln_ffn_b):
    a = dict(locals())
    sh_names = list(_SHARDED)
    names = sh_names + list(_REPLICATED)
    wts = {n: a[n] for n in names}
    mom = {n: a["m_" + n] for n in names}
    var = {n: a["v_" + n] for n in names}

    gather = {}
    token = None
    for group, items in _GATHER_GROUPS:
        gather[group] = _exchange_start([wts[n][l].astype(dt) for n, l, dt, _ in items], [pl_ for *_, pl_ in items],
                                        token, name="gather_start_" + group)
        token = gather[group]["token"]

    w = dict(pool_w=pool_w[0], pool_scale=pool_scale[0], conv_dw_b=conv_dw_b[0], conv_ln_g=conv_ln_g[0],
             conv_ln_b=conv_ln_b[0], attn_rel_bias=attn_rel_bias[0], ln_mix_g=ln_mix_g, ln_mix_b=ln_mix_b,
             ffn_dw_b=ffn_dw_b, ple_b_gate=ple_b_gate, ln_ffn_g=ln_ffn_g, ln_ffn_b=ln_ffn_b)
    for n in ("ffn_up_g", "ffn_up_v", "ffn_dw_w", "ffn_w_down", "ple_w_gate", "ple_w_proj"):
        w[n] = [None, None]

    def ready(group, after):
        got = _exchange_wait(gather[group], token if after is None else after, name="gather_wait_" + group)
        if group == "mix":
            w["mix_w_in"], w["conv_dw_w"] = _full_from_shards(got[0], 1), _full_from_shards(got[1], 1)
            w["mix_w_out"] = got[2]
        elif group == "attn":
            w["attn_w_qkv"], w["attn_w_o"] = got
        elif group[:2] == "up":
            l = int(group[2])
            w["ffn_up_g"][l] = _full_from_shards(got[0][:N_DEV // 2], 1)
            w["ffn_up_v"][l] = _full_from_shards(got[0][N_DEV // 2:], 1)
            w["ffn_dw_w"][l] = _full_from_shards(got[1], 1)
        else:
            l = int(group[2])
            w["ffn_w_down"][l], w["ple_w_gate"][l], w["ple_w_proj"][l] = got

    scatter = {}

    def emit(group, gr):
        if group[:3] == "ffn":
            l = int(group[3])
            pieces = [jnp.concatenate([_pieces_from_full(gr["ffn_up_g"][l], 1, N_DEV // 2),
                                       _pieces_from_full(gr["ffn_up_v"][l], 1, N_DEV // 2)]),
                      _pieces_from_full(gr["ffn_dw_w"][l], 1), _pieces_from_full(gr["ffn_w_down"][l], 0),
                      _pieces_from_full(gr["ple_w_gate"][l], 0), _pieces_from_full(gr["ple_w_proj"][l], 1)]
        elif group == "attn":
            pieces = [_pieces_from_full(gr["attn_w_qkv"], 1), _pieces_from_full(gr["attn_w_o"], 0)]
        else:
            pieces = [_pieces_from_full(gr["mix_w_in"], 1), _pieces_from_full(gr["conv_dw_w"], 1),
                      _pieces_from_full(gr["mix_w_out"], 0)]
        scatter[group] = _exchange_start([a.astype(BF16) for a in pieces], ["pieces"] * len(pieces), None,
                                         name="grad_start_" + group)
        return scatter[group]["token"]

    loss_part, grad_x, gr = _local_step(x[0], p[:, 0], loss_target[0], w, ready, emit)
    loss = lax.psum(loss_part, ("x", "y", "c"))

    recv = {}
    after = grad_x
    for group in ("ffn1", "attn", "ffn0", "mix"):
        recv[group] = _exchange_wait(scatter[group], after, name="grad_wait_" + group)
        after = recv[group][0]
    got = {"mix_w_in": [recv["mix"][0]], "conv_dw_w": [recv["mix"][1]], "mix_w_out": [recv["mix"][2]],
           "attn_w_qkv": [recv["attn"][0]], "attn_w_o": [recv["attn"][1]]}
    for i, n in enumerate(("ffn_w_up", "ffn_dw_w", "ffn_w_down", "ple_w_gate", "ple_w_proj")):
        got[n] = [recv["ffn0"][i], recv["ffn1"][i]]

    res = [{}, {}, {}, {}]
    for n in sh_names:
        outs_l = [_adamw(r, wts[n][l], mom[n][l], var[n][l], name=f"adamw_{n}{l}") for l, r in enumerate(got[n])]
        for k in range(4):
            res[k][n] = jnp.stack([o[k] for o in outs_l])

    gfull = dict(
        pool_w=gr["pool_w"][None], pool_scale=gr["pool_scale"][None], conv_dw_b=gr["conv_dw_b"][None],
        conv_ln_g=gr["conv_ln_g"][None], conv_ln_b=gr["conv_ln_b"][None], attn_rel_bias=gr["attn_rel_bias"][None],
        ln_mix_g=jnp.stack(gr["ln_mix_g"]), ln_mix_b=jnp.stack(gr["ln_mix_b"]), ffn_dw_b=jnp.stack(gr["ffn_dw_b"]),
        ple_b_gate=jnp.stack(gr["ple_b_gate"]), ln_ffn_g=jnp.stack(gr["ln_ffn_g"]),
        ln_ffn_b=jnp.stack(gr["ln_ffn_b"]))
    rep_send = _pack_rows([gfull[n].reshape(-1) for n in _REPLICATED], 8, F32)
    (rep_recv,) = _exchange([rep_send], ["stack"], name="grad_all_gather")

    def flat_state(d):
        return _pack_rows([d[n].reshape(-1) for n in _REPLICATED], 8, F32)

    rep_out = _adamw(rep_recv, flat_state(wts), flat_state(mom), flat_state(var), name="adamw_replicated")
    for k in range(4):
        for n, arr in zip(_REPLICATED, _unpack(rep_out[k], [wts[n].shape for n in _REPLICATED])):
            res[k][n] = arr
    order = ["mix_w_in", "pool_w", "pool_scale", "conv_dw_w", "conv_dw_b", "conv_ln_g", "conv_ln_b", "mix_w_out",
             "attn_w_qkv", "attn_rel_bias", "attn_w_o", "ln_mix_g", "ln_mix_b", "ffn_w_up", "ffn_dw_w", "ffn_dw_b",
             "ffn_w_down", "ple_w_proj", "ple_w_gate", "ple_b_gate", "ln_ffn_g", "ln_ffn_b"]
    outs = [loss, grad_x[None]]
    for k in range(4):
        outs += [res[k][n] for n in order]
    return tuple(outs)
```

```python
import functools
import math

import jax
import jax.numpy as jnp
from jax import lax
from jax.experimental import pallas as pl
from jax.experimental.pallas import tpu as pltpu

F32 = jnp.float32
BF16 = jnp.bfloat16

N_DEV = 8
D_MODEL = 1024
D_POOL = 512
D_CONV = 512
POOL_WINDOWS = (2, 4, 8, 16)
POOL_GROUP = 128
CONV_KERNEL = 31
CHUNK = 64
HEAD_DIM = 64
N_HEADS = 16
LEFT_CHUNKS = 8
BAND = (LEFT_CHUNKS + 1) * CHUNK
MAX_REL = 256
D_FF = 2816
PLE_DIM = 256
ALPHA = 4.0 ** 0.25
LN_EPS = 1e-5
NEG_INF = -1e30
ADAM_LR, ADAM_B1, ADAM_B2, ADAM_EPS, ADAM_WD, ADAM_STEP = 0.001, 0.9, 0.999, 1e-08, 0.01, 10

Q_BLOCK = 4 * CHUNK
KV_PAD = LEFT_CHUNKS * CHUNK
KV_SPAN = KV_PAD + Q_BLOCK
CONV_HALO = 32
FFN_HALO = 16
SUB_ROWS, SUB_LANES = 64, 128
LANES = 1024
VMEM_LIMIT = 56 * 1024 * 1024


def _cparams(sem=None):
    return pltpu.CompilerParams(dimension_semantics=sem, vmem_limit_bytes=VMEM_LIMIT)


def _tile(dim, pref):
    if dim <= pref:
        return dim
    t = pref - pref % 128
    while t >= 128:
        if dim % t == 0:
            return t
        t -= 128
    return dim


def _sigmoid(x):
    return 1.0 / (1.0 + jnp.exp(-x))


def _bdot(a, b, dn=(((1,), (0,)), ((), ()))):
    return lax.dot_general(a.astype(BF16), b.astype(BF16), dn, preferred_element_type=F32)


NT = (((1,), (1,)), ((), ()))
TN = (((0,), (0,)), ((), ()))


def _wgrad(a, b, *, tm=1024, tn=1024, tk=512, name):
    K, M = a.shape
    kb, N = b.shape
    assert K == kb, (a.shape, b.shape)
    tm, tn, tk = _tile(M, tm), _tile(N, tn), _tile(K, tk)
    nk = K // tk

    def body(a_ref, b_ref, o_ref, acc):
        k = pl.program_id(2)

        @pl.when(k == 0)
        def _():
            acc[...] = jnp.zeros_like(acc)

        acc[...] += _bdot(a_ref[...], b_ref[...], TN)

        @pl.when(k == nk - 1)
        def _():
            o_ref[...] = acc[...].astype(BF16)

    return pl.pallas_call(
        body,
        out_shape=jax.ShapeDtypeStruct((M, N), BF16),
        grid=(M // tm, N // tn, nk),
        in_specs=[pl.BlockSpec((tk, tm), lambda i, j, k: (k, i)), pl.BlockSpec((tk, tn), lambda i, j, k: (k, j))],
        out_specs=pl.BlockSpec((tm, tn), lambda i, j, k: (i, j)),
        scratch_shapes=[pltpu.VMEM((tm, tn), F32)],
        compiler_params=_cparams(("parallel", "parallel", "arbitrary")),
        name=name,
    )(a, b)


def _mm_rows(pairs, *, add=None, add_scale=1.0, out_dtype=F32, tm=256, dep=None, name):
    M = pairs[0][0].shape[0]
    N = pairs[0][1].shape[0] if pairs[0][2] else pairs[0][1].shape[1]
    n = len(pairs)
    has_add = add is not None

    def body(*refs):
        o_ref = refs[-1]
        acc = None
        for i, (_, _, tr) in enumerate(pairs):
            part = _bdot(refs[2 * i][...], refs[2 * i + 1][...], NT if tr else (((1,), (0,)), ((), ())))
            acc = part if acc is None else acc + part
        if has_add:
            acc = acc + add_scale * refs[2 * n][...]
        o_ref[...] = acc.astype(out_dtype)

    in_specs, args = [], []
    for a, w_, _ in pairs:
        in_specs += [pl.BlockSpec((tm, a.shape[1]), lambda i: (i, 0)), pl.BlockSpec(w_.shape, lambda i: (0, 0))]
        args += [a, w_]
    if has_add:
        in_specs.append(pl.BlockSpec((tm, N), lambda i: (i, 0)))
        args.append(add)
    if dep is not None:
        in_specs.append(pl.BlockSpec(memory_space=pl.ANY))
        args.append(dep)
    return pl.pallas_call(
        body,
        out_shape=jax.ShapeDtypeStruct((M, N), out_dtype),
        grid=(M // tm,),
        in_specs=in_specs,
        out_specs=pl.BlockSpec((tm, N), lambda i: (i, 0)),
        compiler_params=_cparams(("parallel",)),
        name=name,
    )(*args)


def _layer_norm_rows(z, g, b):
    mu = jnp.mean(z, axis=-1, keepdims=True)
    zc = z - mu
    var = jnp.mean(zc * zc, axis=-1, keepdims=True)
    return zc * lax.rsqrt(var + LN_EPS) * g + b


def _proj_ln(res, a, w, ln_g, ln_b, *, ple=None, ts=256, name):
    S, D = res.shape
    ka = a.shape[1]
    has_ple = ple is not None
    row = lambda i: (i, 0)
    fix = lambda i: (0, 0)

    def body(*refs):
        if has_ple:
            res_ref, a_ref, w_ref, g_ref, b_ref, wg_ref, bg_ref, p_ref, wp_ref, z_ref, r_ref, gate_ref, proj_ref = refs
        else:
            res_ref, a_ref, w_ref, g_ref, b_ref, z_ref, r_ref = refs
        res_t = res_ref[...]
        acc = _bdot(a_ref[...], w_ref[...])
        if has_ple:
            gate = _sigmoid(_bdot(res_t, wg_ref[...]) + bg_ref[...])
            proj = _bdot(p_ref[...], wp_ref[...])
            gate_ref[...] = gate
            proj_ref[...] = proj
            acc = acc + gate * proj
        z = ALPHA * res_t + acc
        z_ref[...] = z
        r_ref[...] = _layer_norm_rows(z, g_ref[...], b_ref[...])

    in_specs = [pl.BlockSpec((ts, D), row), pl.BlockSpec((ts, ka), row), pl.BlockSpec((ka, D), fix),
                pl.BlockSpec((1, D), fix), pl.BlockSpec((1, D), fix)]
    args = [res, a, w, ln_g.reshape(1, D), ln_b.reshape(1, D)]
    n_out = 2
    if has_ple:
        wg, bg, p, wp = ple
        in_specs += [pl.BlockSpec((D, D), fix), pl.BlockSpec((1, D), fix), pl.BlockSpec((ts, PLE_DIM), row),
                     pl.BlockSpec((PLE_DIM, D), fix)]
        args += [wg, bg.reshape(1, D), p, wp]
        n_out = 4
    return pl.pallas_call(
        body,
        out_shape=[jax.ShapeDtypeStruct((S, D), F32)] * n_out,
        grid=(S // ts,),
        in_specs=in_specs,
        out_specs=[pl.BlockSpec((ts, D), row)] * n_out,
        compiler_params=_cparams(("parallel",)),
        name=name,
    )(*args)


CONV_ROWS = 32
LN_ROWS = 16


def _shifted_copies(src, dst, rows):
    for b in range(1, 8):
        for c0 in range(0, src.shape[1], SUB_LANES):
            ln = pl.ds(c0, SUB_LANES)
            for r0 in range(0, rows, SUB_ROWS):
                rc = min(SUB_ROWS, rows - r0)
                dst[b - 1, pl.ds(r0, rc), ln] = src[pl.ds(r0 + b, rc), ln]


def _rows_at(src, copies, off, n, ln):
    b = off % 8
    return src[pl.ds(off, n), ln] if b == 0 else copies[b - 1, pl.ds(off - b, n), ln]


def _conv31(stg, gsh, cw_ref, cb_ref, out, rows, first_off):
    for c0 in range(0, D_CONV, SUB_LANES):
        ln = pl.ds(c0, SUB_LANES)
        for r0 in range(0, rows, CONV_ROWS):
            acc = jnp.zeros((CONV_ROWS, SUB_LANES), F32) + cb_ref[:, ln]
            for k in range(CONV_KERNEL):
                acc = acc + cw_ref[k:k + 1, ln] * _rows_at(stg, gsh, first_off + k + r0, CONV_ROWS, ln)
            out[pl.ds(r0, CONV_ROWS), ln] = acc


def _mixer_fwd(u, pool_w, pool_scale, conv_w, conv_b, cln_g, cln_b, *, ts=256):
    S = u.shape[0]
    hb = CONV_HALO
    nh = ts // hb

    def body(u_ref, uh_ref, pw_ref, ps_ref, cw_ref, cb_ref, g_ref, b_ref, y_ref, d_ref, sta, stg, gsh, hcs):
        i = pl.program_id(0)
        first = i == 0
        sta[pl.ds(0, hb), :] = jnp.where(first, 0.0, uh_ref[:, 0:D_POOL])
        sta[pl.ds(hb, ts), :] = u_ref[:, 0:D_POOL]
        glu_h = uh_ref[:, D_POOL:D_POOL + D_CONV] * _sigmoid(uh_ref[:, D_POOL + D_CONV:])
        stg[pl.ds(0, hb), :] = jnp.where(first, 0.0, glu_h)
        stg[pl.ds(hb, ts), :] = u_ref[:, D_POOL:D_POOL + D_CONV] * _sigmoid(u_ref[:, D_POOL + D_CONV:])

        pos = (i * ts + lax.broadcasted_iota(jnp.int32, (ts, 1), 0) + 1).astype(F32)
        for g, w in enumerate(POOL_WINDOWS):
            lanes = pl.ds(g * POOL_GROUP, POOL_GROUP)
            a_g = sta[pl.ds(hb, ts), lanes]
            s = a_g
            for j in range(1, w):
                s = s + sta[pl.ds(hb - j, ts), lanes]
            d_g = s / jnp.minimum(pos, float(w)) - a_g
            d_ref[:, lanes] = d_g.astype(BF16)
            y_ref[:, lanes] = (_bdot(d_g, pw_ref[g]) * ps_ref[:, lanes]).astype(BF16)

        _shifted_copies(stg, gsh, hb + ts - 8)
        _conv31(stg, gsh, cw_ref, cb_ref, hcs, ts, hb - (CONV_KERNEL - 1))
        for r0 in range(0, ts, LN_ROWS):
            rows = pl.ds(r0, LN_ROWS)
            ln = _layer_norm_rows(hcs[rows, :], g_ref[...], b_ref[...])
            y_ref[rows, D_POOL:] = (ln * _sigmoid(ln)).astype(BF16)

    fix2 = lambda i: (0, 0)
    return pl.pallas_call(
        body,
        out_shape=[jax.ShapeDtypeStruct((S, D_MODEL), BF16), jax.ShapeDtypeStruct((S, D_POOL), BF16)],
        grid=(S // ts,),
        in_specs=[pl.BlockSpec((ts, 3 * D_POOL), lambda i: (i, 0)),
                  pl.BlockSpec((hb, 3 * D_POOL), lambda i: (jnp.maximum(i * nh - 1, 0), 0)),
                  pl.BlockSpec((4, POOL_GROUP, POOL_GROUP), lambda i: (0, 0, 0)),
                  pl.BlockSpec((1, D_POOL), fix2), pl.BlockSpec((CONV_KERNEL, D_CONV), fix2),
                  pl.BlockSpec((1, D_CONV), fix2), pl.BlockSpec((1, D_CONV), fix2), pl.BlockSpec((1, D_CONV), fix2)],
        out_specs=[pl.BlockSpec((ts, D_MODEL), lambda i: (i, 0)), pl.BlockSpec((ts, D_POOL), lambda i: (i, 0))],
        scratch_shapes=[pltpu.VMEM((hb + ts, D_POOL), F32), pltpu.VMEM((hb + ts, D_CONV), F32),
                        pltpu.VMEM((7, hb + ts - 8, D_CONV), F32), pltpu.VMEM((ts, D_CONV), F32)],
        compiler_params=_cparams(("parallel",)),
        name="mixer_fwd",
    )(u, u, pool_w, pool_scale.reshape(1, D_POOL), conv_w, conv_b.reshape(1, D_CONV), cln_g.reshape(1, D_CONV),
      cln_b.reshape(1, D_CONV))


def _mixer_bwd(u, d, dycat, pool_w, pool_scale, conv_w, conv_b, cln_g, cln_b, *, ts=256):
    S = u.shape[0]
    hb = CONV_HALO
    nh = ts // hb
    n = S // ts
    te = ts + hb
    K = CONV_KERNEL

    def body(u_ref, up_ref, un_ref, d_ref, dy_ref, dyn_ref, pw_ref, ps_ref, cw_ref, cb_ref, g_ref, b_ref,
             du_ref, dpw_ref, dps_ref, dcw_ref, dcb_ref, dg_ref, db_ref, stg, std, sth, gsh, hcs, hsh):
        i = pl.program_id(0)
        first = i == 0
        last = i == n - 1

        @pl.when(first)
        def _():
            dpw_ref[...] = jnp.zeros_like(dpw_ref)
            dps_ref[...] = jnp.zeros_like(dps_ref)
            dcw_ref[...] = jnp.zeros_like(dcw_ref)
            dcb_ref[...] = jnp.zeros_like(dcb_ref)
            dg_ref[...] = jnp.zeros_like(dg_ref)
            db_ref[...] = jnp.zeros_like(db_ref)

        pos_e = (i * ts + lax.broadcasted_iota(jnp.int32, (te, 1), 0) + 1).astype(F32)
        dya = dy_ref[:, 0:D_POOL]
        dya_n = jnp.where(last, 0.0, dyn_ref[:, 0:D_POOL])
        for g, w in enumerate(POOL_WINDOWS):
            lanes = pl.ds(g * POOL_GROUP, POOL_GROUP)
            sl = slice(g * POOL_GROUP, (g + 1) * POOL_GROUP)
            pw = pw_ref[g]
            scale = ps_ref[:, lanes]
            d_g = d_ref[:, lanes]
            pre = _bdot(d_g, pw)
            dps_ref[:, lanes] += jnp.sum(dya[:, sl] * pre, axis=0, keepdims=True)
            dys = dya[:, sl] * scale
            dpw_ref[g] += _bdot(d_g, dys, TN)
            dys_e = jnp.concatenate([dys, dya_n[:, sl] * scale], axis=0)
            dd = _bdot(dys_e, pw, NT)
            std[:, lanes] = dd / jnp.minimum(pos_e, float(w))
            da = -dd[0:ts]
            for m in range(w):
                da = da + std[pl.ds(m, ts), lanes]
            du_ref[:, lanes] = da.astype(BF16)

        glu_p = up_ref[:, D_POOL:D_POOL + D_CONV] * _sigmoid(up_ref[:, D_POOL + D_CONV:])
        stg[pl.ds(0, hb), :] = jnp.where(first, 0.0, glu_p)
        bv = u_ref[:, D_POOL:D_POOL + D_CONV]
        sg = _sigmoid(u_ref[:, D_POOL + D_CONV:])
        stg[pl.ds(hb, ts), :] = bv * sg
        glu_n = un_ref[:, D_POOL:D_POOL + D_CONV] * _sigmoid(un_ref[:, D_POOL + D_CONV:])
        stg[pl.ds(hb + ts, hb), :] = jnp.where(last, 0.0, glu_n)
        _shifted_copies(stg, gsh, hb + te - 8)
        _conv31(stg, gsh, cw_ref, cb_ref, hcs, te, hb - (K - 1))

        sums = [jnp.zeros((8, D_CONV), F32) for _ in range(3)]
        for r0 in range(0, te, LN_ROWS):
            rows = pl.ds(r0, LN_ROWS)
            hc = hcs[rows, :]
            hcc = hc - jnp.mean(hc, axis=-1, keepdims=True)
            rstd = lax.rsqrt(jnp.mean(hcc * hcc, axis=-1, keepdims=True) + LN_EPS)
            xh = hcc * rstd
            ln = xh * g_ref[...] + b_ref[...]
            sl_ = _sigmoid(ln)
            if r0 < ts:
                dyb = dy_ref[rows, D_POOL:]
            else:
                dyb = jnp.where(last, 0.0, dyn_ref[pl.ds(r0 - ts, LN_ROWS), D_POOL:])
            dln = dyb * (sl_ * (1.0 + ln * (1.0 - sl_)))
            dxh = dln * g_ref[...]
            dhc = rstd * (dxh - jnp.mean(dxh, axis=-1, keepdims=True)
                          - xh * jnp.mean(dxh * xh, axis=-1, keepdims=True))
            sth[rows, :] = dhc
            if r0 < ts:
                for n_, term in enumerate((dln * xh, dln, dhc)):
                    sums[n_] = sums[n_] + jnp.sum(term.reshape(LN_ROWS // 8, 8, D_CONV), axis=0)
        dg_ref[...] += jnp.sum(sums[0], axis=0, keepdims=True)
        db_ref[...] += jnp.sum(sums[1], axis=0, keepdims=True)
        dcb_ref[...] += jnp.sum(sums[2], axis=0, keepdims=True)

        _shifted_copies(sth, hsh, te - 8)
        for c0 in range(0, D_CONV, SUB_LANES):
            ln_ = pl.ds(c0, SUB_LANES)
            for r0 in range(0, ts, CONV_ROWS):
                rows = pl.ds(r0, CONV_ROWS)
                dglu = jnp.zeros((CONV_ROWS, SUB_LANES), F32)
                for k in range(K):
                    dglu = dglu + cw_ref[k:k + 1, ln_] * _rows_at(sth, hsh, K - 1 - k + r0, CONV_ROWS, ln_)
                bv = u_ref[rows, pl.ds(D_POOL + c0, SUB_LANES)]
                sg = _sigmoid(u_ref[rows, pl.ds(D_POOL + D_CONV + c0, SUB_LANES)])
                du_ref[rows, pl.ds(D_POOL + c0, SUB_LANES)] = (dglu * sg).astype(BF16)
                du_ref[rows, pl.ds(D_POOL + D_CONV + c0, SUB_LANES)] = (dglu * bv * sg * (1.0 - sg)).astype(BF16)
            for k in range(K):
                tap = jnp.zeros((8, SUB_LANES), F32)
                for r0 in range(0, ts, CONV_ROWS):
                    prod = sth[pl.ds(r0, CONV_ROWS), ln_] * _rows_at(stg, gsh, hb - (K - 1) + k + r0, CONV_ROWS, ln_)
                    tap = tap + jnp.sum(prod.reshape(CONV_ROWS // 8, 8, SUB_LANES), axis=0)
                dcw_ref[k:k + 1, ln_] += jnp.sum(tap, axis=0, keepdims=True)

    fix2 = lambda i: (0, 0)
    prev = lambda i: (jnp.maximum(i * nh - 1, 0), 0)
    nxt = lambda i: (jnp.minimum((i + 1) * nh, S // hb - 1), 0)
    return pl.pallas_call(
        body,
        out_shape=[jax.ShapeDtypeStruct((S, 3 * D_POOL), BF16),
                   jax.ShapeDtypeStruct((4, POOL_GROUP, POOL_GROUP), F32),
                   jax.ShapeDtypeStruct((1, D_POOL), F32),
                   jax.ShapeDtypeStruct((K, D_CONV), F32),
                   jax.ShapeDtypeStruct((1, D_CONV), F32),
                   jax.ShapeDtypeStruct((1, D_CONV), F32),
                   jax.ShapeDtypeStruct((1, D_CONV), F32)],
        grid=(n,),
        in_specs=[pl.BlockSpec((ts, 3 * D_POOL), lambda i: (i, 0)),
                  pl.BlockSpec((hb, 3 * D_POOL), prev),
                  pl.BlockSpec((hb, 3 * D_POOL), nxt),
                  pl.BlockSpec((ts, D_POOL), lambda i: (i, 0)),
                  pl.BlockSpec((ts, D_MODEL), lambda i: (i, 0)),
                  pl.BlockSpec((hb, D_MODEL), nxt),
                  pl.BlockSpec((4, POOL_GROUP, POOL_GROUP), lambda i: (0, 0, 0)),
                  pl.BlockSpec((1, D_POOL), fix2), pl.BlockSpec((K, D_CONV), fix2),
                  pl.BlockSpec((1, D_CONV), fix2), pl.BlockSpec((1, D_CONV), fix2), pl.BlockSpec((1, D_CONV), fix2)],
        out_specs=[pl.BlockSpec((ts, 3 * D_POOL), lambda i: (i, 0)),
                   pl.BlockSpec((4, POOL_GROUP, POOL_GROUP), lambda i: (0, 0, 0)),
                   pl.BlockSpec((1, D_POOL), fix2), pl.BlockSpec((K, D_CONV), fix2),
                   pl.BlockSpec((1, D_CONV), fix2), pl.BlockSpec((1, D_CONV), fix2), pl.BlockSpec((1, D_CONV), fix2)],
        scratch_shapes=[pltpu.VMEM((hb + ts + hb, D_CONV), F32), pltpu.VMEM((te, D_POOL), F32),
                        pltpu.VMEM((te, D_CONV), F32), pltpu.VMEM((7, hb + te - 8, D_CONV), F32),
                        pltpu.VMEM((te, D_CONV), F32), pltpu.VMEM((7, te - 8, D_CONV), F32)],
        compiler_params=_cparams(("arbitrary",)),
        name="mixer_bwd",
    )(u, u, u, d, dycat, dycat, pool_w, pool_scale.reshape(1, D_POOL), conv_w, conv_b.reshape(1, D_CONV),
      cln_g.reshape(1, D_CONV), cln_b.reshape(1, D_CONV))


_GELU_C = math.sqrt(2.0 / math.pi)


def _gelu_parts(x):
    inner = _GELU_C * (x + 0.044715 * x * x * x)
    th = jnp.tanh(inner)
    ge = 0.5 * x * (1.0 + th)
    dge = 0.5 * (1.0 + th) + 0.5 * x * (1.0 - th * th) * (_GELU_C * (1.0 + 3.0 * 0.044715 * x * x))
    return ge, dge


def _ffn_act_fwd(gate, val, dw_w, dw_b, *, ts=256, tc=1408, name):
    S, F = gate.shape
    hb = FFN_HALO
    nh = ts // hb
    tc = _tile(F, tc)

    def body(g_ref, gh_ref, v_ref, w_ref, b_ref, h_ref, st):
        i = pl.program_id(0)
        st[pl.ds(0, hb), :] = jnp.where(i == 0, 0.0, gh_ref[...].astype(F32))
        st[pl.ds(hb, ts), :] = g_ref[...].astype(F32)
        for c0 in range(0, tc, SUB_LANES):
            ln = pl.ds(c0, SUB_LANES)
            w0, w1, w2, b = w_ref[0:1, ln], w_ref[1:2, ln], w_ref[2:3, ln], b_ref[:, ln]
            for r0 in range(0, ts, SUB_ROWS):
                gc = b + w0 * st[pl.ds(hb - 2 + r0, SUB_ROWS), ln] + w1 * st[pl.ds(hb - 1 + r0, SUB_ROWS), ln] \
                    + w2 * st[pl.ds(hb + r0, SUB_ROWS), ln]
                ge, _ = _gelu_parts(gc)
                rows = pl.ds(r0, SUB_ROWS)
                h_ref[rows, ln] = (ge * v_ref[rows, ln].astype(F32)).astype(BF16)

    return pl.pallas_call(
        body,
        out_shape=jax.ShapeDtypeStruct((S, F), BF16),
        grid=(S // ts, F // tc),
        in_specs=[pl.BlockSpec((ts, tc), lambda i, j: (i, j)),
                  pl.BlockSpec((hb, tc), lambda i, j: (jnp.maximum(i * nh - 1, 0), j)),
                  pl.BlockSpec((ts, tc), lambda i, j: (i, j)),
                  pl.BlockSpec((3, tc), lambda i, j: (0, j)),
                  pl.BlockSpec((1, tc), lambda i, j: (0, j))],
        out_specs=pl.BlockSpec((ts, tc), lambda i, j: (i, j)),
        scratch_shapes=[pltpu.VMEM((hb + ts, tc), F32)],
        compiler_params=_cparams(("parallel", "parallel")),
        name=name,
    )(gate, gate, val, dw_w, dw_b.reshape(1, F))


def _ffn_act_bwd(gate, val, dh, dw_w, dw_b, *, ts=256, tc=1408, name):
    S, F = gate.shape
    hb = FFN_HALO
    nh = ts // hb
    n = S // ts
    te = ts + hb
    tc = _tile(F, tc)

    def body(g_ref, gp_ref, gn_ref, v_ref, vn_ref, dh_ref, dhn_ref, w_ref, b_ref,
             dg_ref, dv_ref, dw_ref, db_ref, st, sd):
        i = pl.program_id(1)
        first = i == 0
        last = i == n - 1

        @pl.when(first)
        def _():
            dw_ref[...] = jnp.zeros_like(dw_ref)
            db_ref[...] = jnp.zeros_like(db_ref)

        st[pl.ds(0, hb), :] = jnp.where(first, 0.0, gp_ref[...].astype(F32))
        st[pl.ds(hb, ts), :] = g_ref[...].astype(F32)
        st[pl.ds(hb + ts, hb), :] = jnp.where(last, 0.0, gn_ref[...].astype(F32))
        for c0 in range(0, tc, SUB_LANES):
            ln = pl.ds(c0, SUB_LANES)
            w0, w1, w2, b = w_ref[0:1, ln], w_ref[1:2, ln], w_ref[2:3, ln], b_ref[:, ln]
            db_acc = jnp.zeros((8, SUB_LANES), F32)
            dw_acc = [jnp.zeros((8, SUB_LANES), F32) for _ in range(3)]
            for r0 in range(0, te, SUB_ROWS):
                rc = min(SUB_ROWS, te - r0)
                taps = [st[pl.ds(hb - 2 + k + r0, rc), ln] for k in range(3)]
                gc = b + w0 * taps[0] + w1 * taps[1] + w2 * taps[2]
                ge, dge = _gelu_parts(gc)
                if r0 < ts:
                    rows = pl.ds(r0, rc)
                    val, dh = v_ref[rows, ln].astype(F32), dh_ref[rows, ln].astype(F32)
                else:
                    val = jnp.where(last, 0.0, vn_ref[:, ln].astype(F32)[0:rc])
                    dh = jnp.where(last, 0.0, dhn_ref[:, ln].astype(F32)[0:rc])
                dgc = dh * val * dge
                sd[pl.ds(r0, rc), ln] = dgc
                if r0 < ts:
                    dv_ref[rows, ln] = (dh * ge).astype(BF16)
                    db_acc = db_acc + jnp.sum(dgc.reshape(rc // 8, 8, SUB_LANES), axis=0)
                    for k in range(3):
                        dw_acc[k] = dw_acc[k] + jnp.sum((dgc * taps[k]).reshape(rc // 8, 8, SUB_LANES), axis=0)
            db_ref[:, ln] += jnp.sum(db_acc, axis=0, keepdims=True)
            for k in range(3):
                dw_ref[k:k + 1, ln] += jnp.sum(dw_acc[k], axis=0, keepdims=True)
            for r0 in range(0, ts, SUB_ROWS):
                dgate = w0 * sd[pl.ds(2 + r0, SUB_ROWS), ln] + w1 * sd[pl.ds(1 + r0, SUB_ROWS), ln] \
                    + w2 * sd[pl.ds(r0, SUB_ROWS), ln]
                dg_ref[pl.ds(r0, SUB_ROWS), ln] = dgate.astype(BF16)

    cur = lambda j, i: (i, j)
    prev = lambda j, i: (jnp.maximum(i * nh - 1, 0), j)
    nxt = lambda j, i: (jnp.minimum((i + 1) * nh, S // hb - 1), j)
    return pl.pallas_call(
        body,
        out_shape=[jax.ShapeDtypeStruct((S, F), BF16), jax.ShapeDtypeStruct((S, F), BF16),
                   jax.ShapeDtypeStruct((3, F), F32), jax.ShapeDtypeStruct((1, F), F32)],
        grid=(F // tc, n),
        in_specs=[pl.BlockSpec((ts, tc), cur), pl.BlockSpec((hb, tc), prev), pl.BlockSpec((hb, tc), nxt),
                  pl.BlockSpec((ts, tc), cur), pl.BlockSpec((hb, tc), nxt),
                  pl.BlockSpec((ts, tc), cur), pl.BlockSpec((hb, tc), nxt),
                  pl.BlockSpec((3, tc), lambda j, i: (0, j)), pl.BlockSpec((1, tc), lambda j, i: (0, j))],
        out_specs=[pl.BlockSpec((ts, tc), cur), pl.BlockSpec((ts, tc), cur),
                   pl.BlockSpec((3, tc), lambda j, i: (0, j)), pl.BlockSpec((1, tc), lambda j, i: (0, j))],
        scratch_shapes=[pltpu.VMEM((hb + ts + hb, tc), F32), pltpu.VMEM((te, tc), F32)],
        compiler_params=_cparams(("parallel", "arbitrary")),
        name=name,
    )(gate, gate, gate, val, val, dh, dh, dw_w, dw_b.reshape(1, F))


def _ln_bwd(z, ln_g, ln_b, dout, *, loss_head=False, ts=256, dep=None, name):
    S, D = z.shape

    def body(z_ref, g_ref, b_ref, do_ref, *rest):
        dz_ref, dg_ref, db_ref, loss_ref = rest[-4:]
        i = pl.program_id(0)

        @pl.when(i == 0)
        def _():
            dg_ref[...] = jnp.zeros_like(dg_ref)
            db_ref[...] = jnp.zeros_like(db_ref)
            loss_ref[...] = jnp.zeros_like(loss_ref)

        zt = z_ref[...]
        mu = jnp.mean(zt, axis=-1, keepdims=True)
        zc = zt - mu
        rstd = lax.rsqrt(jnp.mean(zc * zc, axis=-1, keepdims=True) + LN_EPS)
        xh = zc * rstd
        if loss_head:
            err = xh * g_ref[...] + b_ref[...] - do_ref[...]
            loss_ref[...] += 0.5 * jnp.sum(jnp.mean(err * err, axis=-1, keepdims=True))
            do = err * (1.0 / D)
        else:
            do = do_ref[...]
        dg_ref[...] += jnp.sum(do * xh, axis=0, keepdims=True)
        db_ref[...] += jnp.sum(do, axis=0, keepdims=True)
        dxh = do * g_ref[...]
        dz_ref[...] = rstd * (dxh - jnp.mean(dxh, axis=-1, keepdims=True)
                              - xh * jnp.mean(dxh * xh, axis=-1, keepdims=True))

    row = lambda i: (i, 0)
    fix = lambda i: (0, 0)
    return pl.pallas_call(
        body,
        out_shape=[jax.ShapeDtypeStruct((S, D), F32), jax.ShapeDtypeStruct((1, D), F32),
                   jax.ShapeDtypeStruct((1, D), F32), jax.ShapeDtypeStruct((8, 128), F32)],
        grid=(S // ts,),
        in_specs=[pl.BlockSpec((ts, D), row), pl.BlockSpec((1, D), fix), pl.BlockSpec((1, D), fix),
                  pl.BlockSpec((ts, D), row)] + ([pl.BlockSpec(memory_space=pl.ANY)] if dep is not None else []),
        out_specs=[pl.BlockSpec((ts, D), row), pl.BlockSpec((1, D), fix), pl.BlockSpec((1, D), fix),
                   pl.BlockSpec((8, 128), fix)],
        compiler_params=_cparams(("arbitrary",)),
        name=name,
    )(z, ln_g.reshape(1, D), ln_b.reshape(1, D), dout, *([dep] if dep is not None else []))


def _ple_bwd(dz, gate, proj, *, ts=256, name):
    S, D = dz.shape

    def body(dz_ref, g_ref, p_ref, ds_ref, dp_ref, db_ref):
        @pl.when(pl.program_id(0) == 0)
        def _():
            db_ref[...] = jnp.zeros_like(db_ref)

        dzt = dz_ref[...]
        g = g_ref[...]
        ds = dzt * p_ref[...] * g * (1.0 - g)
        ds_ref[...] = ds.astype(BF16)
        dp_ref[...] = (dzt * g).astype(BF16)
        db_ref[...] += jnp.sum(ds, axis=0, keepdims=True)

    row = lambda i: (i, 0)
    return pl.pallas_call(
        body,
        out_shape=[jax.ShapeDtypeStruct((S, D), BF16), jax.ShapeDtypeStruct((S, D), BF16),
                   jax.ShapeDtypeStruct((1, D), F32)],
        grid=(S // ts,),
        in_specs=[pl.BlockSpec((ts, D), row)] * 3,
        out_specs=[pl.BlockSpec((ts, D), row), pl.BlockSpec((ts, D), row), pl.BlockSpec((1, D), lambda i: (0, 0))],
        compiler_params=_cparams(("arbitrary",)),
        name=name,
    )(dz, gate, proj)


HEAD_PAIR = 2 * HEAD_DIM


ATT_ROWS = 32
ATT_SCALE = HEAD_DIM ** -0.5


def _softmax_piece(s_ref, b_ref, j, rows, qb):
    s = s_ref[j, rows, :] + b_ref[j, rows, :]
    kpos = qb * Q_BLOCK + lax.broadcasted_iota(jnp.int32, (1, KV_SPAN), 1)
    s = jnp.where(kpos >= KV_PAD, s, NEG_INF)
    e = jnp.exp(s - jnp.max(s, axis=-1, keepdims=True))
    return e * (1.0 / jnp.sum(e, axis=-1, keepdims=True))


def _pad_keys(qb, k_ref, v_ref, kp, vp):
    @pl.when(qb == 0)
    def _():
        kp[pl.ds(0, KV_PAD), :] = jnp.zeros((KV_PAD, HEAD_PAIR), BF16)
        vp[pl.ds(0, KV_PAD), :] = jnp.zeros((KV_PAD, HEAD_PAIR), BF16)
        kp[pl.ds(KV_PAD, k_ref.shape[0]), :] = k_ref[...]
        vp[pl.ds(KV_PAD, v_ref.shape[0]), :] = v_ref[...]


def _attn_fwd(qkv, bias):
    S = qkv.shape[0]
    nhp = N_HEADS // 2

    def body(q_ref, k_ref, v_ref, b_ref, o_ref, kp, vp, s_scr, p_scr):
        qb = pl.program_id(1)
        _pad_keys(qb, k_ref, v_ref, kp, vp)
        span = pl.ds(pl.multiple_of(qb * Q_BLOCK, Q_BLOCK), KV_SPAN)
        kc, vc = kp[span, :], vp[span, :]
        qt = q_ref[...] * ATT_SCALE
        first = lax.broadcasted_iota(jnp.int32, (1, HEAD_PAIR), 1) < HEAD_DIM
        outs = []
        for j in range(2):
            qj = jnp.where(first if j == 0 else ~first, qt, jnp.zeros_like(qt))
            s_scr[j] = _bdot(qj, kc, NT)
            for r0 in range(0, Q_BLOCK, ATT_ROWS):
                rows = pl.ds(r0, ATT_ROWS)
                p_scr[j, rows, :] = _softmax_piece(s_scr, b_ref, j, rows, qb).astype(BF16)
            outs.append(_bdot(p_scr[j], vc))
        o_ref[...] = jnp.where(first, outs[0], outs[1]).astype(BF16)

    return pl.pallas_call(
        body,
        out_shape=jax.ShapeDtypeStruct((S, D_MODEL), BF16),
        grid=(nhp, S // Q_BLOCK),
        in_specs=[pl.BlockSpec((Q_BLOCK, HEAD_PAIR), lambda h, i: (i, h)),
                  pl.BlockSpec((S, HEAD_PAIR), lambda h, i: (0, nhp + h)),
                  pl.BlockSpec((S, HEAD_PAIR), lambda h, i: (0, 2 * nhp + h)),
                  pl.BlockSpec((2, Q_BLOCK, KV_SPAN), lambda h, i: (h, 0, 0))],
        out_specs=pl.BlockSpec((Q_BLOCK, HEAD_PAIR), lambda h, i: (i, h)),
        scratch_shapes=[pltpu.VMEM((KV_PAD + S, HEAD_PAIR), BF16), pltpu.VMEM((KV_PAD + S, HEAD_PAIR), BF16),
                        pltpu.VMEM((2, Q_BLOCK, KV_SPAN), F32), pltpu.VMEM((2, Q_BLOCK, KV_SPAN), BF16)],
        compiler_params=_cparams(("parallel", "arbitrary")),
        name="attn_fwd",
    )(qkv, qkv, qkv, bias)


def _attn_bwd(qkv, bias, do):
    S = qkv.shape[0]
    nhp = N_HEADS // 2
    nq = S // Q_BLOCK
    scale = HEAD_DIM ** -0.5

    def body(q_ref, k_ref, v_ref, b_ref, do_ref, dq_ref, dk_ref, dv_ref, db_ref, kp, vp, dka, dva,
             s_scr, dp_scr, p_scr, ds_scr):
        qb = pl.program_id(1)
        _pad_keys(qb, k_ref, v_ref, kp, vp)

        @pl.when(qb == 0)
        def _():
            dka[...] = jnp.zeros_like(dka)
            dva[...] = jnp.zeros_like(dva)
            db_ref[...] = jnp.zeros_like(db_ref)

        span = pl.ds(pl.multiple_of(qb * Q_BLOCK, Q_BLOCK), KV_SPAN)
        kc, vc = kp[span, :], vp[span, :]
        qt, dot = q_ref[...] * ATT_SCALE, do_ref[...]
        first = lax.broadcasted_iota(jnp.int32, (1, HEAD_PAIR), 1) < HEAD_DIM
        dqs = []
        for j in range(2):
            mine = first if j == 0 else ~first
            qj = jnp.where(mine, qt, jnp.zeros_like(qt))
            doj = jnp.where(mine, dot, jnp.zeros_like(dot))
            s_scr[j] = _bdot(qj, kc, NT)
            dp_scr[j] = _bdot(doj, vc, NT)
            for r0 in range(0, Q_BLOCK, ATT_ROWS):
                rows = pl.ds(r0, ATT_ROWS)
                p = _softmax_piece(s_scr, b_ref, j, rows, qb)
                dp = dp_scr[j, rows, :]
                ds = p * (dp - jnp.sum(p * dp, axis=-1, keepdims=True))
                db_ref[j, rows, :] += ds
                p_scr[j, rows, :] = p.astype(BF16)
                ds_scr[j, rows, :] = ds.astype(BF16)
            dva[span, :] += _bdot(p_scr[j], doj, TN)
            dqs.append(_bdot(ds_scr[j], kc))
            dka[span, :] += _bdot(ds_scr[j], qj, TN)
        dq_ref[...] = (scale * jnp.where(first, dqs[0], dqs[1])).astype(BF16)

        @pl.when(qb == nq - 1)
        def _():
            dk_ref[...] = dka[pl.ds(KV_PAD, S), :].astype(BF16)
            dv_ref[...] = dva[pl.ds(KV_PAD, S), :].astype(BF16)

    blk = pl.BlockSpec((Q_BLOCK, HEAD_PAIR), lambda h, i: (i, h))
    col = pl.BlockSpec((S, HEAD_PAIR), lambda h, i: (0, h))
    bsp = pl.BlockSpec((2, Q_BLOCK, KV_SPAN), lambda h, i: (h, 0, 0))
    return pl.pallas_call(
        body,
        out_shape=[jax.ShapeDtypeStruct((S, D_MODEL), BF16)] * 3
        + [jax.ShapeDtypeStruct((N_HEADS, Q_BLOCK, KV_SPAN), F32)],
        grid=(nhp, nq),
        in_specs=[blk, pl.BlockSpec((S, HEAD_PAIR), lambda h, i: (0, nhp + h)),
                  pl.BlockSpec((S, HEAD_PAIR), lambda h, i: (0, 2 * nhp + h)), bsp, blk],
        out_specs=[blk, col, col, bsp],
        scratch_shapes=[pltpu.VMEM((KV_PAD + S, HEAD_PAIR), BF16), pltpu.VMEM((KV_PAD + S, HEAD_PAIR), BF16),
                        pltpu.VMEM((KV_PAD + S, HEAD_PAIR), F32), pltpu.VMEM((KV_PAD + S, HEAD_PAIR), F32),
                        pltpu.VMEM((2, Q_BLOCK, KV_SPAN), F32), pltpu.VMEM((2, Q_BLOCK, KV_SPAN), F32),
                        pltpu.VMEM((2, Q_BLOCK, KV_SPAN), BF16), pltpu.VMEM((2, Q_BLOCK, KV_SPAN), BF16)],
        compiler_params=_cparams(("parallel", "arbitrary")),
        name="attn_bwd",
    )(qkv, qkv, qkv, bias, do)


def _bias_blocks(rel_bias):
    H = rel_bias.shape[0]
    n_e = BAND + CHUNK - 1
    n_clip = KV_PAD + CHUNK - 1 - MAX_REL + 1
    e = jnp.concatenate([jnp.broadcast_to(rel_bias[:, 2 * MAX_REL:], (H, n_clip)),
                         jnp.flip(rel_bias[:, 2 * MAX_REL - (n_e - n_clip):2 * MAX_REL], axis=1)], axis=1)
    skew = jnp.pad(jnp.tile(e, (1, CHUNK)), ((0, 0), (0, CHUNK))).reshape(H, CHUNK, n_e + 1)
    band = jnp.flip(skew, axis=1)[:, :, :BAND]
    rows = [jnp.pad(band, ((0, 0), (0, 0), (c * CHUNK, KV_SPAN - BAND - c * CHUNK)), constant_values=NEG_INF)
            for c in range(Q_BLOCK // CHUNK)]
    return jnp.concatenate(rows, axis=1)


def _bias_blocks_grad(dblk):
    H = dblk.shape[0]
    n_e = BAND + CHUNK - 1
    n_clip = KV_PAD + CHUNK - 1 - MAX_REL + 1
    parts = jnp.stack([dblk[:, c * CHUNK:(c + 1) * CHUNK, c * CHUNK:c * CHUNK + BAND]
                       for c in range(Q_BLOCK // CHUNK)], axis=1)
    parts = jnp.flip(parts, axis=2)
    parts = jnp.pad(parts, ((0, 0), (0, 0), (0, 0), (0, n_e + 1 - BAND)))
    skew = parts.reshape(H, Q_BLOCK // CHUNK, CHUNK * (n_e + 1))[:, :, :CHUNK * n_e]
    skew = skew.reshape(H, Q_BLOCK, n_e)
    skew = jnp.pad(skew, ((0, 0), (0, 0), (0, 1)))

    def body(s_ref, o_ref):
        de = jnp.sum(s_ref[...], axis=0, keepdims=True)
        lane = lax.broadcasted_iota(jnp.int32, de.shape, 1)
        far = jnp.sum(jnp.where(lane < n_clip, de, 0.0), axis=-1, keepdims=True)
        o_ref[...] = jnp.where(lane == 0, far, jnp.where(lane < n_clip, 0.0, de))

    de = pl.pallas_call(
        body,
        out_shape=jax.ShapeDtypeStruct((H, 1, n_e + 1), F32),
        grid=(H,),
        in_specs=[pl.BlockSpec((None, Q_BLOCK, n_e + 1), lambda h: (h, 0, 0))],
        out_specs=pl.BlockSpec((None, 1, n_e + 1), lambda h: (h, 0, 0)),
        compiler_params=_cparams(("parallel",)),
        name="bias_grad_sum",
    )(skew).reshape(H, n_e + 1)
    near = jnp.flip(de[:, n_clip:n_e], axis=1)
    return jnp.concatenate([jnp.zeros((H, 2 * MAX_REL - (n_e - n_clip)), F32), near, de[:, 0:1]], axis=1)


def _ffn_forward(r1, p_l, w, l, ready):
    ready(f"up{l}", r1)
    up_g = _mm_rows([(r1, w["ffn_up_g"][l], False)], out_dtype=BF16, name=f"ffn_up_g{l}")
    up_v = _mm_rows([(r1, w["ffn_up_v"][l], False)], out_dtype=BF16, name=f"ffn_up_v{l}")
    h = _ffn_act_fwd(up_g, up_v, w["ffn_dw_w"][l], w["ffn_dw_b"][l], name=f"ffn_act{l}")
    ready(f"dn{l}", h)
    z2, r2, gate, proj = _proj_ln(r1, h, w["ffn_w_down"][l], w["ln_ffn_g"][l], w["ln_ffn_b"][l],
                                  ple=(w["ple_w_gate"][l], w["ple_b_gate"][l], p_l, w["ple_w_proj"][l]),
                                  name=f"ffn_down_ln{l}")
    return dict(r1=r1, up_g=up_g, up_v=up_v, h=h, z2=z2, gate=gate, proj=proj), r2


def _ffn_backward(sv, dz2, p_l, w, l, grads):
    r1 = sv["r1"]
    ds, dproj, db_gate = _ple_bwd(dz2, sv["gate"], sv["proj"], name=f"ple_bwd{l}")
    dh = _mm_rows([(dz2, w["ffn_w_down"][l], True)], out_dtype=BF16, name=f"ffn_dh{l}")
    dgate, dval, d_dw_w, d_dw_b = _ffn_act_bwd(sv["up_g"], sv["up_v"], dh, w["ffn_dw_w"][l], w["ffn_dw_b"][l],
                                               name=f"ffn_act_bwd{l}")
    grads["ffn_w_down"][l] = _wgrad(sv["h"], dz2, tm=1408, name=f"d_ffn_w_down{l}")
    grads["ffn_up_g"][l] = _wgrad(r1, dgate, tn=1408, name=f"d_ffn_up_g{l}")
    grads["ffn_up_v"][l] = _wgrad(r1, dval, tn=1408, name=f"d_ffn_up_v{l}")
    grads["ple_w_gate"][l] = _wgrad(r1, ds, name=f"d_ple_w_gate{l}")
    grads["ple_w_proj"][l] = _wgrad(p_l, dproj, name=f"d_ple_w_proj{l}")
    grads["ffn_dw_w"][l] = d_dw_w
    grads["ffn_dw_b"][l] = d_dw_b[0]
    grads["ple_b_gate"][l] = db_gate[0]
    return _mm_rows([(ds, w["ple_w_gate"][l], True), (dgate, w["ffn_up_g"][l], True), (dval, w["ffn_up_v"][l], True)],
                    add=dz2, add_scale=ALPHA, name=f"dr1_{l}")


def _local_step(x, p, target, w, ready=lambda group, after: None, emit=lambda group, grads: None):
    grads = {k: [None, None] for k in ("ffn_w_down", "ffn_up_g", "ffn_up_v", "ple_w_gate", "ple_w_proj", "ffn_dw_w",
                                       "ffn_dw_b", "ple_b_gate", "ln_ffn_g", "ln_ffn_b", "ln_mix_g", "ln_mix_b")}

    ready("mix", None)
    u = _mm_rows([(x, w["mix_w_in"], False)], name="mix_in")
    ycat, dpool = _mixer_fwd(u, w["pool_w"], w["pool_scale"], w["conv_dw_w"], w["conv_dw_b"], w["conv_ln_g"],
                             w["conv_ln_b"])
    ready("mixo", ycat)
    z1, r1 = _proj_ln(x, ycat, w["mix_w_out"], w["ln_mix_g"][0], w["ln_mix_b"][0], name="mix_out_ln")
    sv0, r2 = _ffn_forward(r1, p[0], w, 0, ready)

    ready("attn", r2)
    qkv = _mm_rows([(r2, w["attn_w_qkv"], False)], out_dtype=BF16, name="attn_qkv")
    bias = _bias_blocks(w["attn_rel_bias"])
    attn = _attn_fwd(qkv, bias)
    z3, r3 = _proj_ln(r2, attn, w["attn_w_o"], w["ln_mix_g"][1], w["ln_mix_b"][1], name="attn_out_ln")
    sv1, _ = _ffn_forward(r3, p[1], w, 1, ready)

    dz4, grads["ln_ffn_g"][1], grads["ln_ffn_b"][1], loss = _ln_bwd(sv1["z2"], w["ln_ffn_g"][1], w["ln_ffn_b"][1],
                                                                    target, loss_head=True, name="loss_ln_bwd")
    dr3 = _ffn_backward(sv1, dz4, p[1], w, 1, grads)
    dz3, grads["ln_mix_g"][1], grads["ln_mix_b"][1], _ = _ln_bwd(z3, w["ln_mix_g"][1], w["ln_mix_b"][1], dr3,
                                                                dep=emit("ffn1", grads), name="ln_mix_bwd1")
    grads["attn_w_o"] = _wgrad(attn, dz3, name="d_attn_w_o")
    dattn = _mm_rows([(dz3, w["attn_w_o"], True)], out_dtype=BF16, name="d_attn")
    dq, dk, dv, dbias = _attn_bwd(qkv, bias, dattn)
    grads["attn_rel_bias"] = _bias_blocks_grad(dbias)
    dqkv = jnp.concatenate([dq, dk, dv], axis=1)
    grads["attn_w_qkv"] = _wgrad(r2, dqkv, name="d_attn_w_qkv")
    dr2 = _mm_rows([(dqkv, w["attn_w_qkv"], True)], add=dz3, add_scale=ALPHA, dep=emit("attn", grads), name="dr2")

    dz2, grads["ln_ffn_g"][0], grads["ln_ffn_b"][0], _ = _ln_bwd(sv0["z2"], w["ln_ffn_g"][0], w["ln_ffn_b"][0], dr2,
                                                                name="ln_ffn_bwd0")
    dr1 = _ffn_backward(sv0, dz2, p[0], w, 0, grads)
    dz1, grads["ln_mix_g"][0], grads["ln_mix_b"][0], _ = _ln_bwd(z1, w["ln_mix_g"][0], w["ln_mix_b"][0], dr1,
                                                                dep=emit("ffn0", grads), name="ln_mix_bwd0")
    grads["mix_w_out"] = _wgrad(ycat, dz1, name="d_mix_w_out")
    dycat = _mm_rows([(dz1, w["mix_w_out"], True)], name="d_ycat")
    du, g_pw, g_ps, g_cw, g_cb, g_cg, g_cbb = _mixer_bwd(u, dpool, dycat, w["pool_w"], w["pool_scale"],
                                                         w["conv_dw_w"], w["conv_dw_b"], w["conv_ln_g"],
                                                         w["conv_ln_b"])
    grads["mix_w_in"] = _wgrad(x, du, tn=512, name="d_mix_w_in")
    grads["conv_dw_w"] = g_cw
    grad_x = _mm_rows([(du, w["mix_w_in"], True)], add=dz1, add_scale=ALPHA, dep=emit("mix", grads), name="grad_x")
    grads.update(pool_w=g_pw, pool_scale=g_ps[0], conv_dw_w=g_cw, conv_dw_b=g_cb[0], conv_ln_g=g_cg[0],
                 conv_ln_b=g_cbb[0])
    for kname in ("ln_ffn_g", "ln_ffn_b", "ln_mix_g", "ln_mix_b"):
        grads[kname] = [a[0] for a in grads[kname]]
    return loss[0, 0], grad_x, grads


def _exchange(bufs, places, *, name):
    nb = len(bufs)

    def body(*refs):
        srcs, dsts = refs[:nb], refs[nb:2 * nb]
        send_sems, recv_sems, local_sems = refs[2 * nb:]
        x, y, c = lax.axis_index("x"), lax.axis_index("y"), lax.axis_index("c")
        me = 4 * x + 2 * y + c
        local = []
        remote = []
        for b in range(nb):
            pieces = places[b] == "pieces"
            shape = bufs[b].shape
            cp = pltpu.make_async_copy(srcs[b].at[me] if pieces else srcs[b], _slot(dsts[b], places[b], shape, me),
                                       local_sems.at[b])
            cp.start()
            local.append(cp)
            for d, dev, peer in _peers(x, y, c):
                src = srcs[b].at[peer] if pieces else srcs[b]
                out = pltpu.make_async_remote_copy(
                    src_ref=src, dst_ref=_slot(dsts[b], places[b], shape, me),
                    send_sem=send_sems.at[b * N_DEV + d], recv_sem=recv_sems.at[b * N_DEV + d],
                    device_id=dev, device_id_type=pl.DeviceIdType.MESH)
                out.start()
                inc = pltpu.make_async_remote_copy(
                    src_ref=src, dst_ref=_slot(dsts[b], places[b], shape, peer),
                    send_sem=send_sems.at[b * N_DEV + d], recv_sem=recv_sems.at[b * N_DEV + d],
                    device_id=dev, device_id_type=pl.DeviceIdType.MESH)
                remote.append((out, inc))
        for cp in local:
            cp.wait()
        for out, inc in remote:
            out.wait_send()
            inc.wait_recv()

    out_shapes = [jax.ShapeDtypeStruct(_result_shape(b, place), b.dtype) for b, place in zip(bufs, places)]
    any_spec = pl.BlockSpec(memory_space=pl.ANY)
    return pl.pallas_call(
        body,
        out_shape=out_shapes,
        in_specs=[any_spec] * nb,
        out_specs=[any_spec] * nb,
        scratch_shapes=[pltpu.SemaphoreType.DMA((nb * N_DEV,)), pltpu.SemaphoreType.DMA((nb * N_DEV,)),
                        pltpu.SemaphoreType.DMA((nb,))],
        compiler_params=pltpu.CompilerParams(has_side_effects=True),
        name=name,
    )(*bufs)


_HBM = pl.BlockSpec(memory_space=pltpu.HBM)
_SEM = pl.BlockSpec(memory_space=pltpu.SEMAPHORE)
_EFFECT = pltpu.SideEffectType.DATAFLOW_SIDE_EFFECTING


def _slot(ref, place, shape, k):
    if place in ("stack", "pieces"):
        return ref.at[k]
    ax = place[1]
    n = shape[ax]
    return ref.at[(slice(None),) * ax + (pl.ds(pl.multiple_of(k * n, n), n),)]


def _result_shape(buf, place):
    if place == "stack":
        return (N_DEV,) + buf.shape
    if place == "pieces":
        return buf.shape
    return tuple(s * N_DEV if i == place[1] else s for i, s in enumerate(buf.shape))


def _peers(x, y, c):
    for d in range(1, N_DEV):
        px, py, pc = x ^ ((d >> 2) & 1), y ^ ((d >> 1) & 1), c ^ (d & 1)
        yield d, (px, py, pc), 4 * px + 2 * py + pc


def _exchange_start(bufs, places, after, *, name):
    nb = len(bufs)
    lands = [lax.empty(_result_shape(b, p_), b.dtype) for b, p_ in zip(bufs, places)]
    has_after = after is not None

    def body(*refs):
        srcs, dsts = refs[:nb], refs[nb:2 * nb]
        outs = refs[2 * nb + has_after:]
        send_sems, recv_sems, token = outs[0], outs[1], outs[2 + 2 * nb]
        x, y, c = lax.axis_index("x"), lax.axis_index("y"), lax.axis_index("c")
        me = 4 * x + 2 * y + c
        for b in range(nb):
            for d, dev, peer in _peers(x, y, c):
                pltpu.make_async_remote_copy(
                    src_ref=srcs[b].at[peer] if places[b] == "pieces" else srcs[b],
                    dst_ref=_slot(dsts[b], places[b], bufs[b].shape, me),
                    send_sem=send_sems.at[b * N_DEV + d], recv_sem=recv_sems.at[b * N_DEV + d],
                    device_id=dev, device_id_type=pl.DeviceIdType.MESH).start()
            pltpu.make_async_copy(srcs[b].at[me] if places[b] == "pieces" else srcs[b],
                                  _slot(dsts[b], places[b], bufs[b].shape, me), recv_sems.at[b * N_DEV]).start()
        token[...] = jnp.zeros_like(token)

    sems = pltpu.SemaphoreType.DMA((nb * N_DEV,))
    ins = [pltpu.with_memory_space_constraint(a, pltpu.HBM) for a in list(bufs) + lands]
    out = pl.pallas_call(
        body,
        out_shape=(sems, sems, *[pltpu.HBM(a.shape, a.dtype) for a in ins], jax.ShapeDtypeStruct((8, 128), F32)),
        in_specs=[_HBM] * (2 * nb) + ([pl.BlockSpec(memory_space=pl.ANY)] if has_after else []),
        out_specs=(_SEM, _SEM, *[_HBM] * (2 * nb), pl.BlockSpec(memory_space=pltpu.VMEM)),
        input_output_aliases={i: 2 + i for i in range(2 * nb)},
        compiler_params=pltpu.CompilerParams(has_side_effects=_EFFECT),
        name=name,
    )(*ins, *([after] if has_after else []))
    return dict(send=out[0], recv=out[1], srcs=out[2:2 + nb], lands=out[2 + nb:2 + 2 * nb], token=out[-1],
                places=places)


def _exchange_wait(h, after, *, name):
    nb = len(h["srcs"])
    places = h["places"]
    shapes = [a.shape for a in h["srcs"]]

    def body(*refs):
        srcs, dsts, send_sems, recv_sems = refs[:nb], refs[nb:2 * nb], refs[2 * nb], refs[2 * nb + 1]
        x, y, c = lax.axis_index("x"), lax.axis_index("y"), lax.axis_index("c")
        me = 4 * x + 2 * y + c
        for b in range(nb):
            pieces = places[b] == "pieces"
            for d, dev, peer in _peers(x, y, c):
                cp = pltpu.make_async_remote_copy(
                    src_ref=srcs[b].at[peer] if pieces else srcs[b],
                    dst_ref=_slot(dsts[b], places[b], shapes[b], peer),
                    send_sem=send_sems.at[b * N_DEV + d], recv_sem=recv_sems.at[b * N_DEV + d],
                    device_id=dev, device_id_type=pl.DeviceIdType.MESH)
                cp.wait_send()
                cp.wait_recv()
            pltpu.make_async_copy(srcs[b].at[me] if pieces else srcs[b], _slot(dsts[b], places[b], shapes[b], me),
                                  recv_sems.at[b * N_DEV]).wait()

    ins = list(h["srcs"]) + list(h["lands"])
    out = pl.pallas_call(
        body,
        out_shape=tuple(pltpu.HBM(a.shape, a.dtype) for a in ins),
        in_specs=[_HBM] * (2 * nb) + [_SEM, _SEM, pl.BlockSpec(memory_space=pl.ANY)],
        out_specs=tuple([_HBM] * (2 * nb)),
        input_output_aliases={i: i for i in range(2 * nb)},
        compiler_params=pltpu.CompilerParams(has_side_effects=_EFFECT),
        name=name,
    )(*ins, h["send"], h["recv"], after)
    return out[nb:]


def _adamw(recv, w, m, v, *, name):
    R, C = w.shape
    tr = R
    for cand in (512, 256, 128, 64, 32, 16):
        if R % cand == 0 and cand * C * 4 <= 2 * 1024 * 1024:
            tr = cand
            break
    c1 = 1.0 - ADAM_B1 ** ADAM_STEP
    c2 = 1.0 - ADAM_B2 ** ADAM_STEP

    def body(r_ref, w_ref, m_ref, v_ref, g_ref, d_ref, mo_ref, vo_ref):
        g = r_ref[0].astype(F32)
        for i in range(1, N_DEV):
            g = g + r_ref[i].astype(F32)
        m_new = ADAM_B1 * m_ref[...] + (1.0 - ADAM_B1) * g
        v_new = ADAM_B2 * v_ref[...] + (1.0 - ADAM_B2) * (g * g)
        m_hat = m_new / c1
        v_hat = v_new / c2
        g_ref[...] = g
        d_ref[...] = -ADAM_LR * (m_hat / (jnp.sqrt(v_hat) + ADAM_EPS) + ADAM_WD * w_ref[...])
        mo_ref[...] = m_new
        vo_ref[...] = v_new

    row = pl.BlockSpec((tr, C), lambda i: (i, 0))
    return pl.pallas_call(
        body,
        out_shape=[jax.ShapeDtypeStruct((R, C), F32)] * 4,
        grid=(R // tr,),
        in_specs=[pl.BlockSpec((N_DEV, tr, C), lambda i: (0, i, 0)), row, row, row],
        out_specs=[row] * 4,
        compiler_params=_cparams(("parallel",)),
        name=name,
    )(recv, w, m, v)


def _ffn_groups(l):
    return ((f"up{l}", (("ffn_w_up", l, BF16, "stack"), ("ffn_dw_w", l, F32, "stack"))),
            (f"dn{l}", (("ffn_w_down", l, BF16, ("axis", 0)), ("ple_w_gate", l, BF16, ("axis", 0)),
                        ("ple_w_proj", l, BF16, ("axis", 1)))))


_GATHER_GROUPS = (
    ("mix", (("mix_w_in", 0, BF16, "stack"), ("conv_dw_w", 0, F32, "stack"))),
    ("mixo", (("mix_w_out", 0, BF16, ("axis", 0)),)),
    *_ffn_groups(0),
    ("attn", (("attn_w_qkv", 0, BF16, ("axis", 1)), ("attn_w_o", 0, BF16, ("axis", 0)))),
    *_ffn_groups(1))
_SHARDED = ("mix_w_in", "conv_dw_w", "mix_w_out", "attn_w_qkv", "attn_w_o", "ffn_w_up", "ffn_dw_w", "ffn_w_down",
            "ple_w_gate", "ple_w_proj")
_REPLICATED = ("pool_w", "pool_scale", "conv_dw_b", "conv_ln_g", "conv_ln_b", "attn_rel_bias", "ln_mix_g",
               "ln_mix_b", "ffn_dw_b", "ple_b_gate", "ln_ffn_g", "ln_ffn_b")


def _pack_rows(parts, row_mult, dtype):
    lead = parts[0].shape[:-1]
    flat = jnp.concatenate([a.astype(dtype) for a in parts], axis=-1)
    n = flat.shape[-1]
    unit = row_mult * LANES
    padded = -(-n // unit) * unit
    flat = jnp.pad(flat, [(0, 0)] * len(lead) + [(0, padded - n)])
    return flat.reshape(lead + (padded // LANES, LANES))


def _unpack(flat2d, shapes):
    flat = flat2d.reshape(-1)
    out, o = [], 0
    for s in shapes:
        n = math.prod(s)
        out.append(flat[o:o + n].reshape(s))
        o += n
    return out


def _full_from_shards(g, axis):
    parts = jnp.moveaxis(g, 0, axis)
    shp = list(g.shape[1:])
    shp[axis] *= g.shape[0]
    return parts.reshape(shp)


def _pieces_from_full(full, axis, k=N_DEV):
    shp = list(full.shape)
    n = shp[axis] // k
    t = full.reshape(shp[:axis] + [k, n] + shp[axis + 1:])
    return jnp.moveaxis(t, axis, 0)


def kernel(x, p, mix_w_in, pool_w, pool_scale, conv_dw_w, conv_dw_b, conv_ln_g, conv_ln_b, mix_w_out, attn_w_qkv, attn_rel_bias, attn_w_o, ln_mix_g, ln_mix_b, ffn_w_up, ffn_dw_w, ffn_dw_b, ffn_w_down, ple_w_proj, ple_w_gate, ple_b_gate, ln_ffn_g, ln_ffn_b, loss_target, m_mix_w_in, m_pool_w, m_pool_scale, m_conv_dw_w, m_conv_dw_b, m_conv_ln_g, m_conv_ln_b, m_mix_w_out, m_attn_w_qkv, m_attn_rel_bias, m_attn_w_o, m_ln_mix_g, m_ln_mix_b, m_ffn_w_up, m_ffn_dw_w, m_ffn_dw_b, m_ffn_w_down, m_ple_w_proj, m_ple_w_gate, m_ple_b_gate, m_ln_ffn_g, m_ln_ffn_b, v_mix_w_in, v_pool_w, v_pool_scale, v_conv_dw_w, v_conv_dw_b, v_conv_ln_g, v_conv_ln_b, v_mix_w_out, v_attn_w_qkv, v_attn_rel_bias, v_attn_w_o, v_ln_mix_g, v_ln_mix_b, v_ffn_w_up, v_ffn_dw_w, v_ffn_dw_b, v_ffn_w_down, v_ple_w_proj, v_ple_w_gate, v_ple_b_gate, v_ln_ffn_g, v_ln_ffn_b):
    a = dict(locals())
    sh_names = list(_SHARDED)
    names = sh_names + list(_REPLICATED)
    wts = {n: a[n] for n in names}
    mom = {n: a["m_" + n] for n in names}
    var = {n: a["v_" + n] for n in names}

    gather = {}
    token = None
    for group, items in _GATHER_GROUPS:
        gather[group] = _exchange_start([wts[n][l].astype(dt) for n, l, dt, _ in items], [pl_ for *_, pl_ in items],
                                        token, name="gather_start_" + group)
        token = gather[group]["token"]

    w = dict(pool_w=pool_w[0], pool_scale=pool_scale[0], conv_dw_b=conv_dw_b[0], conv_ln_g=conv_ln_g[0],
             conv_ln_b=conv_ln_b[0], attn_rel_bias=attn_rel_bias[0], ln_mix_g=ln_mix_g, ln_mix_b=ln_mix_b,
             ffn_dw_b=ffn_dw_b, ple_b_gate=ple_b_gate, ln_ffn_g=ln_ffn_g, ln_ffn_b=ln_ffn_b)
    for n in ("ffn_up_g", "ffn_up_v", "ffn_dw_w", "ffn_w_down", "ple_w_gate", "ple_w_proj"):
        w[n] = [None, None]

    def ready(group, after):
        got = _exchange_wait(gather[group], token if after is None else after, name="gather_wait_" + group)
        if group == "mix":
            w["mix_w_in"], w["conv_dw_w"] = _full_from_shards(got[0], 1), _full_from_shards(got[1], 1)
        elif group == "mixo":
            (w["mix_w_out"],) = got
        elif group == "attn":
            w["attn_w_qkv"], w["attn_w_o"] = got
        elif group[:2] == "up":
            l = int(group[2])
            w["ffn_up_g"][l] = _full_from_shards(got[0][:N_DEV // 2], 1)
            w["ffn_up_v"][l] = _full_from_shards(got[0][N_DEV // 2:], 1)
            w["ffn_dw_w"][l] = _full_from_shards(got[1], 1)
        else:
            l = int(group[2])
            w["ffn_w_down"][l], w["ple_w_gate"][l], w["ple_w_proj"][l] = got

    scatter = {}

    def emit(group, gr):
        if group[:3] == "ffn":
            l = int(group[3])
            pieces = [jnp.concatenate([_pieces_from_full(gr["ffn_up_g"][l], 1, N_DEV // 2),
                                       _pieces_from_full(gr["ffn_up_v"][l], 1, N_DEV // 2)]),
                      _pieces_from_full(gr["ffn_dw_w"][l], 1), _pieces_from_full(gr["ffn_w_down"][l], 0),
                      _pieces_from_full(gr["ple_w_gate"][l], 0), _pieces_from_full(gr["ple_w_proj"][l], 1)]
        elif group == "attn":
            pieces = [_pieces_from_full(gr["attn_w_qkv"], 1), _pieces_from_full(gr["attn_w_o"], 0)]
        else:
            pieces = [_pieces_from_full(gr["mix_w_in"], 1), _pieces_from_full(gr["conv_dw_w"], 1),
                      _pieces_from_full(gr["mix_w_out"], 0)]
        scatter[group] = _exchange_start([a.astype(BF16) for a in pieces], ["pieces"] * len(pieces), None,
                                         name="grad_start_" + group)
        return scatter[group]["token"]

    loss_part, grad_x, gr = _local_step(x[0], p[:, 0], loss_target[0], w, ready, emit)
    loss = lax.psum(loss_part, ("x", "y", "c"))

    recv = {}
    after = grad_x
    for group in ("ffn1", "attn", "ffn0", "mix"):
        recv[group] = _exchange_wait(scatter[group], after, name="grad_wait_" + group)
        after = recv[group][0]
    got = {"mix_w_in": [recv["mix"][0]], "conv_dw_w": [recv["mix"][1]], "mix_w_out": [recv["mix"][2]],
           "attn_w_qkv": [recv["attn"][0]], "attn_w_o": [recv["attn"][1]]}
    for i, n in enumerate(("ffn_w_up", "ffn_dw_w", "ffn_w_down", "ple_w_gate", "ple_w_proj")):
        got[n] = [recv["ffn0"][i], recv["ffn1"][i]]

    res = [{}, {}, {}, {}]
    for n in sh_names:
        outs_l = [_adamw(r, wts[n][l], mom[n][l], var[n][l], name=f"adamw_{n}{l}") for l, r in enumerate(got[n])]
        for k in range(4):
            res[k][n] = jnp.stack([o[k] for o in outs_l])

    gfull = dict(
        pool_w=gr["pool_w"][None], pool_scale=gr["pool_scale"][None], conv_dw_b=gr["conv_dw_b"][None],
        conv_ln_g=gr["conv_ln_g"][None], conv_ln_b=gr["conv_ln_b"][None], attn_rel_bias=gr["attn_rel_bias"][None],
        ln_mix_g=jnp.stack(gr["ln_mix_g"]), ln_mix_b=jnp.stack(gr["ln_mix_b"]), ffn_dw_b=jnp.stack(gr["ffn_dw_b"]),
        ple_b_gate=jnp.stack(gr["ple_b_gate"]), ln_ffn_g=jnp.stack(gr["ln_ffn_g"]),
        ln_ffn_b=jnp.stack(gr["ln_ffn_b"]))
    rep_send = _pack_rows([gfull[n].reshape(-1) for n in _REPLICATED], 8, F32)
    (rep_recv,) = _exchange([rep_send], ["stack"], name="grad_all_gather")

    def flat_state(d):
        return _pack_rows([d[n].reshape(-1) for n in _REPLICATED], 8, F32)

    rep_out = _adamw(rep_recv, flat_state(wts), flat_state(mom), flat_state(var), name="adamw_replicated")
    for k in range(4):
        for n, arr in zip(_REPLICATED, _unpack(rep_out[k], [wts[n].shape for n in _REPLICATED])):
            res[k][n] = arr
    order = ["mix_w_in", "pool_w", "pool_scale", "conv_dw_w", "conv_dw_b", "conv_ln_g", "conv_ln_b", "mix_w_out",
             "attn_w_qkv", "attn_rel_bias", "attn_w_o", "ln_mix_g", "ln_mix_b", "ffn_w_up", "ffn_dw_w", "ffn_dw_b",
             "ffn_w_down", "ple_w_proj", "ple_w_gate", "ple_b_gate", "ln_ffn_g", "ln_ffn_b"]
    outs = [loss, grad_x[None]]
    for k in range(4):
        outs += [res[k][n] for n in order]
    return tuple(outs)
```

```python
import functools
import math

import jax
import jax.numpy as jnp
from jax import lax
from jax.experimental import pallas as pl
from jax.experimental.pallas import tpu as pltpu

F32 = jnp.float32
BF16 = jnp.bfloat16

N_DEV = 8
D_MODEL = 1024
D_POOL = 512
D_CONV = 512
POOL_WINDOWS = (2, 4, 8, 16)
POOL_GROUP = 128
CONV_KERNEL = 31
CHUNK = 64
HEAD_DIM = 64
N_HEADS = 16
LEFT_CHUNKS = 8
BAND = (LEFT_CHUNKS + 1) * CHUNK
MAX_REL = 256
D_FF = 2816
PLE_DIM = 256
ALPHA = 4.0 ** 0.25
LN_EPS = 1e-5
NEG_INF = -1e30
ADAM_LR, ADAM_B1, ADAM_B2, ADAM_EPS, ADAM_WD, ADAM_STEP = 0.001, 0.9, 0.999, 1e-08, 0.01, 10

Q_BLOCK = 4 * CHUNK
KV_PAD = LEFT_CHUNKS * CHUNK
KV_SPAN = KV_PAD + Q_BLOCK
CONV_HALO = 32
FFN_HALO = 16
SUB_ROWS, SUB_LANES = 64, 128
LANES = 1024
VMEM_LIMIT = 56 * 1024 * 1024


def _cparams(sem=None):
    return pltpu.CompilerParams(dimension_semantics=sem, vmem_limit_bytes=VMEM_LIMIT)


def _tile(dim, pref):
    if dim <= pref:
        return dim
    t = pref - pref % 128
    while t >= 128:
        if dim % t == 0:
            return t
        t -= 128
    return dim


def _sigmoid(x):
    return 1.0 / (1.0 + jnp.exp(-x))


def _bdot(a, b, dn=(((1,), (0,)), ((), ()))):
    return lax.dot_general(a.astype(BF16), b.astype(BF16), dn, preferred_element_type=F32)


NT = (((1,), (1,)), ((), ()))
TN = (((0,), (0,)), ((), ()))


def _wgrad(a, b, *, tm=1024, tn=1024, tk=512, piece=None, name):
    K, M = a.shape
    kb, N = b.shape
    assert K == kb, (a.shape, b.shape)
    tm, tn, tk = _tile(M, tm), _tile(N, tn), _tile(K, tk)
    nk = K // tk
    per = 1 if piece is None else tn // piece
    assert piece is None or tn == per * piece

    def body(a_ref, b_ref, o_ref, acc):
        k = pl.program_id(2)

        @pl.when(k == 0)
        def _():
            acc[...] = jnp.zeros_like(acc)

        acc[...] += _bdot(a_ref[...], b_ref[...], TN)

        @pl.when(k == nk - 1)
        def _():
            if piece is None:
                o_ref[...] = acc[...].astype(BF16)
            else:
                for s in range(per):
                    o_ref[s] = acc[:, s * piece:(s + 1) * piece].astype(BF16)

    if piece is None:
        out_shape, out_spec = (M, N), pl.BlockSpec((tm, tn), lambda i, j, k: (i, j))
    else:
        out_shape, out_spec = (N // piece, M, piece), pl.BlockSpec((per, tm, piece), lambda i, j, k: (j, i, 0))
    return pl.pallas_call(
        body,
        out_shape=jax.ShapeDtypeStruct(out_shape, BF16),
        grid=(M // tm, N // tn, nk),
        in_specs=[pl.BlockSpec((tk, tm), lambda i, j, k: (k, i)), pl.BlockSpec((tk, tn), lambda i, j, k: (k, j))],
        out_specs=out_spec,
        scratch_shapes=[pltpu.VMEM((tm, tn), F32)],
        compiler_params=_cparams(("parallel", "parallel", "arbitrary")),
        name=name,
    )(a, b)


def _join_shards(g, *, k=None, part=0, tm=256, name):
    n, M, wd = g.shape
    k = n if k is None else k

    def body(g_ref, o_ref):
        for j in range(k):
            o_ref[:, j * wd:(j + 1) * wd] = g_ref[j]

    return pl.pallas_call(
        body,
        out_shape=jax.ShapeDtypeStruct((M, k * wd), g.dtype),
        grid=(M // tm,),
        in_specs=[pl.BlockSpec((k, tm, wd), lambda i: (part, i, 0))],
        out_specs=pl.BlockSpec((tm, k * wd), lambda i: (i, 0)),
        compiler_params=_cparams(("parallel",)),
        name=name,
    )(g)


def _mm_rows(pairs, *, add=None, add_scale=1.0, out_dtype=F32, tm=256, dep=None, name):
    M = pairs[0][0].shape[0]
    N = pairs[0][1].shape[0] if pairs[0][2] else pairs[0][1].shape[1]
    n = len(pairs)
    has_add = add is not None

    def body(*refs):
        o_ref = refs[-1]
        acc = None
        for i, (_, _, tr) in enumerate(pairs):
            part = _bdot(refs[2 * i][...], refs[2 * i + 1][...], NT if tr else (((1,), (0,)), ((), ())))
            acc = part if acc is None else acc + part
        if has_add:
            acc = acc + add_scale * refs[2 * n][...]
        o_ref[...] = acc.astype(out_dtype)

    in_specs, args = [], []
    for a, w_, _ in pairs:
        in_specs += [pl.BlockSpec((tm, a.shape[1]), lambda i: (i, 0)), pl.BlockSpec(w_.shape, lambda i: (0, 0))]
        args += [a, w_]
    if has_add:
        in_specs.append(pl.BlockSpec((tm, N), lambda i: (i, 0)))
        args.append(add)
    if dep is not None:
        in_specs.append(pl.BlockSpec(memory_space=pl.ANY))
        args.append(dep)
    return pl.pallas_call(
        body,
        out_shape=jax.ShapeDtypeStruct((M, N), out_dtype),
        grid=(M // tm,),
        in_specs=in_specs,
        out_specs=pl.BlockSpec((tm, N), lambda i: (i, 0)),
        compiler_params=_cparams(("parallel",)),
        name=name,
    )(*args)


def _layer_norm_rows(z, g, b):
    mu = jnp.mean(z, axis=-1, keepdims=True)
    zc = z - mu
    var = jnp.mean(zc * zc, axis=-1, keepdims=True)
    return zc * lax.rsqrt(var + LN_EPS) * g + b


def _proj_ln(res, a, w, ln_g, ln_b, *, ple=None, ts=256, name):
    S, D = res.shape
    ka = a.shape[1]
    has_ple = ple is not None
    row = lambda i: (i, 0)
    fix = lambda i: (0, 0)

    def body(*refs):
        if has_ple:
            (res_ref, a_ref, w_ref, g_ref, b_ref, wg_ref, bg_ref, p_ref, wp_ref, z_ref, r_ref, gate_ref, proj_ref,
             acc) = refs
        else:
            res_ref, a_ref, w_ref, g_ref, b_ref, z_ref, r_ref, acc = refs
        acc[...] = _bdot(a_ref[...], w_ref[...])
        if has_ple:
            gate_ref[...] = _bdot(res_ref[...], wg_ref[...])
            proj_ref[...] = _bdot(p_ref[...], wp_ref[...])
        for r0 in range(0, ts, LN_ROWS):
            rows = pl.ds(r0, LN_ROWS)
            z = ALPHA * res_ref[rows, :] + acc[rows, :]
            if has_ple:
                gate = _sigmoid(gate_ref[rows, :] + bg_ref[...])
                gate_ref[rows, :] = gate
                z = z + gate * proj_ref[rows, :]
            z_ref[rows, :] = z
            r_ref[rows, :] = _layer_norm_rows(z, g_ref[...], b_ref[...])

    in_specs = [pl.BlockSpec((ts, D), row), pl.BlockSpec((ts, ka), row), pl.BlockSpec((ka, D), fix),
                pl.BlockSpec((1, D), fix), pl.BlockSpec((1, D), fix)]
    args = [res, a, w, ln_g.reshape(1, D), ln_b.reshape(1, D)]
    n_out = 2
    if has_ple:
        wg, bg, p, wp = ple
        in_specs += [pl.BlockSpec((D, D), fix), pl.BlockSpec((1, D), fix), pl.BlockSpec((ts, PLE_DIM), row),
                     pl.BlockSpec((PLE_DIM, D), fix)]
        args += [wg, bg.reshape(1, D), p, wp]
        n_out = 4
    return pl.pallas_call(
        body,
        out_shape=[jax.ShapeDtypeStruct((S, D), F32)] * n_out,
        grid=(S // ts,),
        in_specs=in_specs,
        out_specs=[pl.BlockSpec((ts, D), row)] * n_out,
        scratch_shapes=[pltpu.VMEM((ts, D), F32)],
        compiler_params=_cparams(("parallel",)),
        name=name,
    )(*args)


CONV_ROWS = 32
LN_ROWS = 16


def _shifted_copies(src, dst, rows):
    for b in range(1, 8):
        for c0 in range(0, src.shape[1], SUB_LANES):
            ln = pl.ds(c0, SUB_LANES)
            for r0 in range(0, rows, SUB_ROWS):
                rc = min(SUB_ROWS, rows - r0)
                dst[b - 1, pl.ds(r0, rc), ln] = src[pl.ds(r0 + b, rc), ln]


def _rows_at(src, copies, off, n, ln):
    b = off % 8
    return src[pl.ds(off, n), ln] if b == 0 else copies[b - 1, pl.ds(off - b, n), ln]


def _conv31(stg, gsh, cw_ref, cb_ref, out, rows, first_off):
    for c0 in range(0, D_CONV, SUB_LANES):
        ln = pl.ds(c0, SUB_LANES)
        for r0 in range(0, rows, CONV_ROWS):
            acc = jnp.zeros((CONV_ROWS, SUB_LANES), F32) + cb_ref[:, ln]
            for k in range(CONV_KERNEL):
                acc = acc + cw_ref[k:k + 1, ln] * _rows_at(stg, gsh, first_off + k + r0, CONV_ROWS, ln)
            out[pl.ds(r0, CONV_ROWS), ln] = acc


def _mixer_fwd(u, pool_w, pool_scale, conv_w, conv_b, cln_g, cln_b, *, ts=256):
    S = u.shape[0]
    hb = CONV_HALO
    nh = ts // hb

    def body(u_ref, uh_ref, pw_ref, ps_ref, cw_ref, cb_ref, g_ref, b_ref, y_ref, d_ref, sta, stg, gsh, hcs):
        i = pl.program_id(0)
        first = i == 0
        sta[pl.ds(0, hb), :] = jnp.where(first, 0.0, uh_ref[:, 0:D_POOL])
        sta[pl.ds(hb, ts), :] = u_ref[:, 0:D_POOL]
        glu_h = uh_ref[:, D_POOL:D_POOL + D_CONV] * _sigmoid(uh_ref[:, D_POOL + D_CONV:])
        stg[pl.ds(0, hb), :] = jnp.where(first, 0.0, glu_h)
        stg[pl.ds(hb, ts), :] = u_ref[:, D_POOL:D_POOL + D_CONV] * _sigmoid(u_ref[:, D_POOL + D_CONV:])

        pos = (i * ts + lax.broadcasted_iota(jnp.int32, (ts, 1), 0) + 1).astype(F32)
        for g, w in enumerate(POOL_WINDOWS):
            lanes = pl.ds(g * POOL_GROUP, POOL_GROUP)
            a_g = sta[pl.ds(hb, ts), lanes]
            s = a_g
            for j in range(1, w):
                s = s + sta[pl.ds(hb - j, ts), lanes]
            d_g = s / jnp.minimum(pos, float(w)) - a_g
            d_ref[:, lanes] = d_g.astype(BF16)
            y_ref[:, lanes] = (_bdot(d_g, pw_ref[g]) * ps_ref[:, lanes]).astype(BF16)

        _shifted_copies(stg, gsh, hb + ts - 8)
        _conv31(stg, gsh, cw_ref, cb_ref, hcs, ts, hb - (CONV_KERNEL - 1))
        for r0 in range(0, ts, LN_ROWS):
            rows = pl.ds(r0, LN_ROWS)
            ln = _layer_norm_rows(hcs[rows, :], g_ref[...], b_ref[...])
            y_ref[rows, D_POOL:] = (ln * _sigmoid(ln)).astype(BF16)

    fix2 = lambda i: (0, 0)
    return pl.pallas_call(
        body,
        out_shape=[jax.ShapeDtypeStruct((S, D_MODEL), BF16), jax.ShapeDtypeStruct((S, D_POOL), BF16)],
        grid=(S // ts,),
        in_specs=[pl.BlockSpec((ts, 3 * D_POOL), lambda i: (i, 0)),
                  pl.BlockSpec((hb, 3 * D_POOL), lambda i: (jnp.maximum(i * nh - 1, 0), 0)),
                  pl.BlockSpec((4, POOL_GROUP, POOL_GROUP), lambda i: (0, 0, 0)),
                  pl.BlockSpec((1, D_POOL), fix2), pl.BlockSpec((CONV_KERNEL, D_CONV), fix2),
                  pl.BlockSpec((1, D_CONV), fix2), pl.BlockSpec((1, D_CONV), fix2), pl.BlockSpec((1, D_CONV), fix2)],
        out_specs=[pl.BlockSpec((ts, D_MODEL), lambda i: (i, 0)), pl.BlockSpec((ts, D_POOL), lambda i: (i, 0))],
        scratch_shapes=[pltpu.VMEM((hb + ts, D_POOL), F32), pltpu.VMEM((hb + ts, D_CONV), F32),
                        pltpu.VMEM((7, hb + ts - 8, D_CONV), F32), pltpu.VMEM((ts, D_CONV), F32)],
        compiler_params=_cparams(("parallel",)),
        name="mixer_fwd",
    )(u, u, pool_w, pool_scale.reshape(1, D_POOL), conv_w, conv_b.reshape(1, D_CONV), cln_g.reshape(1, D_CONV),
      cln_b.reshape(1, D_CONV))


def _mixer_bwd(u, d, dycat, pool_w, pool_scale, conv_w, conv_b, cln_g, cln_b, *, ts=256):
    S = u.shape[0]
    hb = CONV_HALO
    nh = ts // hb
    n = S // ts
    te = ts + hb
    K = CONV_KERNEL

    def body(u_ref, up_ref, un_ref, d_ref, dy_ref, dyn_ref, pw_ref, ps_ref, cw_ref, cb_ref, g_ref, b_ref,
             du_ref, dpw_ref, dps_ref, dcw_ref, dcb_ref, dg_ref, db_ref, stg, std, sth, gsh, hcs, hsh):
        i = pl.program_id(0)
        first = i == 0
        last = i == n - 1

        @pl.when(first)
        def _():
            dpw_ref[...] = jnp.zeros_like(dpw_ref)
            dps_ref[...] = jnp.zeros_like(dps_ref)
            dcw_ref[...] = jnp.zeros_like(dcw_ref)
            dcb_ref[...] = jnp.zeros_like(dcb_ref)
            dg_ref[...] = jnp.zeros_like(dg_ref)
            db_ref[...] = jnp.zeros_like(db_ref)

        pos_e = (i * ts + lax.broadcasted_iota(jnp.int32, (te, 1), 0) + 1).astype(F32)
        dya = dy_ref[:, 0:D_POOL]
        dya_n = jnp.where(last, 0.0, dyn_ref[:, 0:D_POOL])
        for g, w in enumerate(POOL_WINDOWS):
            lanes = pl.ds(g * POOL_GROUP, POOL_GROUP)
            sl = slice(g * POOL_GROUP, (g + 1) * POOL_GROUP)
            pw = pw_ref[g]
            scale = ps_ref[:, lanes]
            d_g = d_ref[:, lanes]
            pre = _bdot(d_g, pw)
            dps_ref[:, lanes] += jnp.sum(dya[:, sl] * pre, axis=0, keepdims=True)
            dys = dya[:, sl] * scale
            dpw_ref[g] += _bdot(d_g, dys, TN)
            dys_e = jnp.concatenate([dys, dya_n[:, sl] * scale], axis=0)
            dd = _bdot(dys_e, pw, NT)
            std[:, lanes] = dd / jnp.minimum(pos_e, float(w))
            da = -dd[0:ts]
            for m in range(w):
                da = da + std[pl.ds(m, ts), lanes]
            du_ref[:, lanes] = da.astype(BF16)

        glu_p = up_ref[:, D_POOL:D_POOL + D_CONV] * _sigmoid(up_ref[:, D_POOL + D_CONV:])
        stg[pl.ds(0, hb), :] = jnp.where(first, 0.0, glu_p)
        bv = u_ref[:, D_POOL:D_POOL + D_CONV]
        sg = _sigmoid(u_ref[:, D_POOL + D_CONV:])
        stg[pl.ds(hb, ts), :] = bv * sg
        glu_n = un_ref[:, D_POOL:D_POOL + D_CONV] * _sigmoid(un_ref[:, D_POOL + D_CONV:])
        stg[pl.ds(hb + ts, hb), :] = jnp.where(last, 0.0, glu_n)
        _shifted_copies(stg, gsh, hb + te - 8)
        _conv31(stg, gsh, cw_ref, cb_ref, hcs, te, hb - (K - 1))

        sums = [jnp.zeros((8, D_CONV), F32) for _ in range(3)]
        for r0 in range(0, te, LN_ROWS):
            rows = pl.ds(r0, LN_ROWS)
            hc = hcs[rows, :]
            hcc = hc - jnp.mean(hc, axis=-1, keepdims=True)
            rstd = lax.rsqrt(jnp.mean(hcc * hcc, axis=-1, keepdims=True) + LN_EPS)
            xh = hcc * rstd
            ln = xh * g_ref[...] + b_ref[...]
            sl_ = _sigmoid(ln)
            if r0 < ts:
                dyb = dy_ref[rows, D_POOL:]
            else:
                dyb = jnp.where(last, 0.0, dyn_ref[pl.ds(r0 - ts, LN_ROWS), D_POOL:])
            dln = dyb * (sl_ * (1.0 + ln * (1.0 - sl_)))
            dxh = dln * g_ref[...]
            dhc = rstd * (dxh - jnp.mean(dxh, axis=-1, keepdims=True)
                          - xh * jnp.mean(dxh * xh, axis=-1, keepdims=True))
            sth[rows, :] = dhc
            if r0 < ts:
                for n_, term in enumerate((dln * xh, dln, dhc)):
                    sums[n_] = sums[n_] + jnp.sum(term.reshape(LN_ROWS // 8, 8, D_CONV), axis=0)
        dg_ref[...] += jnp.sum(sums[0], axis=0, keepdims=True)
        db_ref[...] += jnp.sum(sums[1], axis=0, keepdims=True)
        dcb_ref[...] += jnp.sum(sums[2], axis=0, keepdims=True)

        _shifted_copies(sth, hsh, te - 8)
        for c0 in range(0, D_CONV, SUB_LANES):
            ln_ = pl.ds(c0, SUB_LANES)
            for r0 in range(0, ts, CONV_ROWS):
                rows = pl.ds(r0, CONV_ROWS)
                dglu = jnp.zeros((CONV_ROWS, SUB_LANES), F32)
                for k in range(K):
                    dglu = dglu + cw_ref[k:k + 1, ln_] * _rows_at(sth, hsh, K - 1 - k + r0, CONV_ROWS, ln_)
                bv = u_ref[rows, pl.ds(D_POOL + c0, SUB_LANES)]
                sg = _sigmoid(u_ref[rows, pl.ds(D_POOL + D_CONV + c0, SUB_LANES)])
                du_ref[rows, pl.ds(D_POOL + c0, SUB_LANES)] = (dglu * sg).astype(BF16)
                du_ref[rows, pl.ds(D_POOL + D_CONV + c0, SUB_LANES)] = (dglu * bv * sg * (1.0 - sg)).astype(BF16)
            for k in range(K):
                tap = jnp.zeros((8, SUB_LANES), F32)
                for r0 in range(0, ts, CONV_ROWS):
                    prod = sth[pl.ds(r0, CONV_ROWS), ln_] * _rows_at(stg, gsh, hb - (K - 1) + k + r0, CONV_ROWS, ln_)
                    tap = tap + jnp.sum(prod.reshape(CONV_ROWS // 8, 8, SUB_LANES), axis=0)
                dcw_ref[k:k + 1, ln_] += jnp.sum(tap, axis=0, keepdims=True)

    fix2 = lambda i: (0, 0)
    prev = lambda i: (jnp.maximum(i * nh - 1, 0), 0)
    nxt = lambda i: (jnp.minimum((i + 1) * nh, S // hb - 1), 0)
    return pl.pallas_call(
        body,
        out_shape=[jax.ShapeDtypeStruct((S, 3 * D_POOL), BF16),
                   jax.ShapeDtypeStruct((4, POOL_GROUP, POOL_GROUP), F32),
                   jax.ShapeDtypeStruct((1, D_POOL), F32),
                   jax.ShapeDtypeStruct((K, D_CONV), F32),
                   jax.ShapeDtypeStruct((1, D_CONV), F32),
                   jax.ShapeDtypeStruct((1, D_CONV), F32),
                   jax.ShapeDtypeStruct((1, D_CONV), F32)],
        grid=(n,),
        in_specs=[pl.BlockSpec((ts, 3 * D_POOL), lambda i: (i, 0)),
                  pl.BlockSpec((hb, 3 * D_POOL), prev),
                  pl.BlockSpec((hb, 3 * D_POOL), nxt),
                  pl.BlockSpec((ts, D_POOL), lambda i: (i, 0)),
                  pl.BlockSpec((ts, D_MODEL), lambda i: (i, 0)),
                  pl.BlockSpec((hb, D_MODEL), nxt),
                  pl.BlockSpec((4, POOL_GROUP, POOL_GROUP), lambda i: (0, 0, 0)),
                  pl.BlockSpec((1, D_POOL), fix2), pl.BlockSpec((K, D_CONV), fix2),
                  pl.BlockSpec((1, D_CONV), fix2), pl.BlockSpec((1, D_CONV), fix2), pl.BlockSpec((1, D_CONV), fix2)],
        out_specs=[pl.BlockSpec((ts, 3 * D_POOL), lambda i: (i, 0)),
                   pl.BlockSpec((4, POOL_GROUP, POOL_GROUP), lambda i: (0, 0, 0)),
                   pl.BlockSpec((1, D_POOL), fix2), pl.BlockSpec((K, D_CONV), fix2),
                   pl.BlockSpec((1, D_CONV), fix2), pl.BlockSpec((1, D_CONV), fix2), pl.BlockSpec((1, D_CONV), fix2)],
        scratch_shapes=[pltpu.VMEM((hb + ts + hb, D_CONV), F32), pltpu.VMEM((te, D_POOL), F32),
                        pltpu.VMEM((te, D_CONV), F32), pltpu.VMEM((7, hb + te - 8, D_CONV), F32),
                        pltpu.VMEM((te, D_CONV), F32), pltpu.VMEM((7, te - 8, D_CONV), F32)],
        compiler_params=_cparams(("arbitrary",)),
        name="mixer_bwd",
    )(u, u, u, d, dycat, dycat, pool_w, pool_scale.reshape(1, D_POOL), conv_w, conv_b.reshape(1, D_CONV),
      cln_g.reshape(1, D_CONV), cln_b.reshape(1, D_CONV))


_GELU_C = math.sqrt(2.0 / math.pi)


def _gelu_parts(x):
    inner = _GELU_C * (x + 0.044715 * x * x * x)
    th = jnp.tanh(inner)
    ge = 0.5 * x * (1.0 + th)
    dge = 0.5 * (1.0 + th) + 0.5 * x * (1.0 - th * th) * (_GELU_C * (1.0 + 3.0 * 0.044715 * x * x))
    return ge, dge


def _ffn_act_fwd(gate, val, dw_w, dw_b, *, ts=256, tc=1408, name):
    S, F = gate.shape
    hb = FFN_HALO
    nh = ts // hb
    tc = _tile(F, tc)

    def body(g_ref, gh_ref, v_ref, w_ref, b_ref, h_ref, st):
        i = pl.program_id(0)
        st[pl.ds(0, hb), :] = jnp.where(i == 0, 0.0, gh_ref[...].astype(F32))
        st[pl.ds(hb, ts), :] = g_ref[...].astype(F32)
        for c0 in range(0, tc, SUB_LANES):
            ln = pl.ds(c0, SUB_LANES)
            w0, w1, w2, b = w_ref[0:1, ln], w_ref[1:2, ln], w_ref[2:3, ln], b_ref[:, ln]
            for r0 in range(0, ts, SUB_ROWS):
                gc = b + w0 * st[pl.ds(hb - 2 + r0, SUB_ROWS), ln] + w1 * st[pl.ds(hb - 1 + r0, SUB_ROWS), ln] \
                    + w2 * st[pl.ds(hb + r0, SUB_ROWS), ln]
                ge, _ = _gelu_parts(gc)
                rows = pl.ds(r0, SUB_ROWS)
                h_ref[rows, ln] = (ge * v_ref[rows, ln].astype(F32)).astype(BF16)

    return pl.pallas_call(
        body,
        out_shape=jax.ShapeDtypeStruct((S, F), BF16),
        grid=(S // ts, F // tc),
        in_specs=[pl.BlockSpec((ts, tc), lambda i, j: (i, j)),
                  pl.BlockSpec((hb, tc), lambda i, j: (jnp.maximum(i * nh - 1, 0), j)),
                  pl.BlockSpec((ts, tc), lambda i, j: (i, j)),
                  pl.BlockSpec((3, tc), lambda i, j: (0, j)),
                  pl.BlockSpec((1, tc), lambda i, j: (0, j))],
        out_specs=pl.BlockSpec((ts, tc), lambda i, j: (i, j)),
        scratch_shapes=[pltpu.VMEM((hb + ts, tc), F32)],
        compiler_params=_cparams(("parallel", "parallel")),
        name=name,
    )(gate, gate, val, dw_w, dw_b.reshape(1, F))


def _ffn_act_bwd(gate, val, dh, dw_w, dw_b, *, ts=256, tc=1408, name):
    S, F = gate.shape
    hb = FFN_HALO
    nh = ts // hb
    n = S // ts
    te = ts + hb
    tc = _tile(F, tc)

    def body(g_ref, gp_ref, gn_ref, v_ref, vn_ref, dh_ref, dhn_ref, w_ref, b_ref,
             dg_ref, dv_ref, dw_ref, db_ref, st, sd):
        i = pl.program_id(1)
        first = i == 0
        last = i == n - 1

        @pl.when(first)
        def _():
            dw_ref[...] = jnp.zeros_like(dw_ref)
            db_ref[...] = jnp.zeros_like(db_ref)

        st[pl.ds(0, hb), :] = jnp.where(first, 0.0, gp_ref[...].astype(F32))
        st[pl.ds(hb, ts), :] = g_ref[...].astype(F32)
        st[pl.ds(hb + ts, hb), :] = jnp.where(last, 0.0, gn_ref[...].astype(F32))
        for c0 in range(0, tc, SUB_LANES):
            ln = pl.ds(c0, SUB_LANES)
            w0, w1, w2, b = w_ref[0:1, ln], w_ref[1:2, ln], w_ref[2:3, ln], b_ref[:, ln]
            db_acc = jnp.zeros((8, SUB_LANES), F32)
            dw_acc = [jnp.zeros((8, SUB_LANES), F32) for _ in range(3)]
            for r0 in range(0, te, SUB_ROWS):
                rc = min(SUB_ROWS, te - r0)
                taps = [st[pl.ds(hb - 2 + k + r0, rc), ln] for k in range(3)]
                gc = b + w0 * taps[0] + w1 * taps[1] + w2 * taps[2]
                ge, dge = _gelu_parts(gc)
                if r0 < ts:
                    rows = pl.ds(r0, rc)
                    val, dh = v_ref[rows, ln].astype(F32), dh_ref[rows, ln].astype(F32)
                else:
                    val = jnp.where(last, 0.0, vn_ref[:, ln].astype(F32)[0:rc])
                    dh = jnp.where(last, 0.0, dhn_ref[:, ln].astype(F32)[0:rc])
                dgc = dh * val * dge
                sd[pl.ds(r0, rc), ln] = dgc
                if r0 < ts:
                    dv_ref[rows, ln] = (dh * ge).astype(BF16)
                    db_acc = db_acc + jnp.sum(dgc.reshape(rc // 8, 8, SUB_LANES), axis=0)
                    for k in range(3):
                        dw_acc[k] = dw_acc[k] + jnp.sum((dgc * taps[k]).reshape(rc // 8, 8, SUB_LANES), axis=0)
            db_ref[:, ln] += jnp.sum(db_acc, axis=0, keepdims=True)
            for k in range(3):
                dw_ref[k:k + 1, ln] += jnp.sum(dw_acc[k], axis=0, keepdims=True)
            for r0 in range(0, ts, SUB_ROWS):
                dgate = w0 * sd[pl.ds(2 + r0, SUB_ROWS), ln] + w1 * sd[pl.ds(1 + r0, SUB_ROWS), ln] \
                    + w2 * sd[pl.ds(r0, SUB_ROWS), ln]
                dg_ref[pl.ds(r0, SUB_ROWS), ln] = dgate.astype(BF16)

    cur = lambda j, i: (i, j)
    prev = lambda j, i: (jnp.maximum(i * nh - 1, 0), j)
    nxt = lambda j, i: (jnp.minimum((i + 1) * nh, S // hb - 1), j)
    return pl.pallas_call(
        body,
        out_shape=[jax.ShapeDtypeStruct((S, F), BF16), jax.ShapeDtypeStruct((S, F), BF16),
                   jax.ShapeDtypeStruct((3, F), F32), jax.ShapeDtypeStruct((1, F), F32)],
        grid=(F // tc, n),
        in_specs=[pl.BlockSpec((ts, tc), cur), pl.BlockSpec((hb, tc), prev), pl.BlockSpec((hb, tc), nxt),
                  pl.BlockSpec((ts, tc), cur), pl.BlockSpec((hb, tc), nxt),
                  pl.BlockSpec((ts, tc), cur), pl.BlockSpec((hb, tc), nxt),
                  pl.BlockSpec((3, tc), lambda j, i: (0, j)), pl.BlockSpec((1, tc), lambda j, i: (0, j))],
        out_specs=[pl.BlockSpec((ts, tc), cur), pl.BlockSpec((ts, tc), cur),
                   pl.BlockSpec((3, tc), lambda j, i: (0, j)), pl.BlockSpec((1, tc), lambda j, i: (0, j))],
        scratch_shapes=[pltpu.VMEM((hb + ts + hb, tc), F32), pltpu.VMEM((te, tc), F32)],
        compiler_params=_cparams(("parallel", "arbitrary")),
        name=name,
    )(gate, gate, gate, val, val, dh, dh, dw_w, dw_b.reshape(1, F))


def _ln_bwd(z, ln_g, ln_b, dout, *, loss_head=False, ts=256, dep=None, name):
    S, D = z.shape

    def body(z_ref, g_ref, b_ref, do_ref, *rest):
        dz_ref, dg_ref, db_ref, loss_ref = rest[-4:]
        i = pl.program_id(0)

        @pl.when(i == 0)
        def _():
            dg_ref[...] = jnp.zeros_like(dg_ref)
            db_ref[...] = jnp.zeros_like(db_ref)
            loss_ref[...] = jnp.zeros_like(loss_ref)

        dg_acc = jnp.zeros((8, D), F32)
        db_acc = jnp.zeros((8, D), F32)
        loss_acc = jnp.zeros((1, 1), F32)
        for r0 in range(0, ts, LN_ROWS):
            rows = pl.ds(r0, LN_ROWS)
            zt = z_ref[rows, :]
            zc = zt - jnp.mean(zt, axis=-1, keepdims=True)
            rstd = lax.rsqrt(jnp.mean(zc * zc, axis=-1, keepdims=True) + LN_EPS)
            xh = zc * rstd
            if loss_head:
                err = xh * g_ref[...] + b_ref[...] - do_ref[rows, :]
                loss_acc = loss_acc + 0.5 * jnp.sum(jnp.mean(err * err, axis=-1, keepdims=True), keepdims=True)
                do = err * (1.0 / D)
            else:
                do = do_ref[rows, :]
            dg_acc = dg_acc + jnp.sum((do * xh).reshape(LN_ROWS // 8, 8, D), axis=0)
            db_acc = db_acc + jnp.sum(do.reshape(LN_ROWS // 8, 8, D), axis=0)
            dxh = do * g_ref[...]
            dz_ref[rows, :] = rstd * (dxh - jnp.mean(dxh, axis=-1, keepdims=True)
                                      - xh * jnp.mean(dxh * xh, axis=-1, keepdims=True))
        dg_ref[...] += jnp.sum(dg_acc, axis=0, keepdims=True)
        db_ref[...] += jnp.sum(db_acc, axis=0, keepdims=True)
        if loss_head:
            loss_ref[...] += loss_acc

    row = lambda i: (i, 0)
    fix = lambda i: (0, 0)
    return pl.pallas_call(
        body,
        out_shape=[jax.ShapeDtypeStruct((S, D), F32), jax.ShapeDtypeStruct((1, D), F32),
                   jax.ShapeDtypeStruct((1, D), F32), jax.ShapeDtypeStruct((8, 128), F32)],
        grid=(S // ts,),
        in_specs=[pl.BlockSpec((ts, D), row), pl.BlockSpec((1, D), fix), pl.BlockSpec((1, D), fix),
                  pl.BlockSpec((ts, D), row)] + ([pl.BlockSpec(memory_space=pl.ANY)] if dep is not None else []),
        out_specs=[pl.BlockSpec((ts, D), row), pl.BlockSpec((1, D), fix), pl.BlockSpec((1, D), fix),
                   pl.BlockSpec((8, 128), fix)],
        compiler_params=_cparams(("arbitrary",)),
        name=name,
    )(z, ln_g.reshape(1, D), ln_b.reshape(1, D), dout, *([dep] if dep is not None else []))


def _ple_bwd(dz, gate, proj, *, ts=256, name):
    S, D = dz.shape

    def body(dz_ref, g_ref, p_ref, ds_ref, dp_ref, db_ref):
        @pl.when(pl.program_id(0) == 0)
        def _():
            db_ref[...] = jnp.zeros_like(db_ref)

        db_acc = jnp.zeros((8, D), F32)
        for r0 in range(0, ts, LN_ROWS):
            rows = pl.ds(r0, LN_ROWS)
            dzt = dz_ref[rows, :]
            g = g_ref[rows, :]
            ds = dzt * p_ref[rows, :] * g * (1.0 - g)
            ds_ref[rows, :] = ds.astype(BF16)
            dp_ref[rows, :] = (dzt * g).astype(BF16)
            db_acc = db_acc + jnp.sum(ds.reshape(LN_ROWS // 8, 8, D), axis=0)
        db_ref[...] += jnp.sum(db_acc, axis=0, keepdims=True)

    row = lambda i: (i, 0)
    return pl.pallas_call(
        body,
        out_shape=[jax.ShapeDtypeStruct((S, D), BF16), jax.ShapeDtypeStruct((S, D), BF16),
                   jax.ShapeDtypeStruct((1, D), F32)],
        grid=(S // ts,),
        in_specs=[pl.BlockSpec((ts, D), row)] * 3,
        out_specs=[pl.BlockSpec((ts, D), row), pl.BlockSpec((ts, D), row), pl.BlockSpec((1, D), lambda i: (0, 0))],
        compiler_params=_cparams(("arbitrary",)),
        name=name,
    )(dz, gate, proj)


HEAD_PAIR = 2 * HEAD_DIM


ATT_ROWS = 32
ATT_SCALE = HEAD_DIM ** -0.5


def _softmax_piece(s_ref, b_ref, j, rows, qb):
    s = s_ref[j, rows, :] + b_ref[j, rows, :]
    kpos = qb * Q_BLOCK + lax.broadcasted_iota(jnp.int32, (1, KV_SPAN), 1)
    s = jnp.where(kpos >= KV_PAD, s, NEG_INF)
    e = jnp.exp(s - jnp.max(s, axis=-1, keepdims=True))
    return e * (1.0 / jnp.sum(e, axis=-1, keepdims=True))


def _pad_keys(qb, k_ref, v_ref, kp, vp):
    @pl.when(qb == 0)
    def _():
        kp[pl.ds(0, KV_PAD), :] = jnp.zeros((KV_PAD, HEAD_PAIR), BF16)
        vp[pl.ds(0, KV_PAD), :] = jnp.zeros((KV_PAD, HEAD_PAIR), BF16)
        kp[pl.ds(KV_PAD, k_ref.shape[0]), :] = k_ref[...]
        vp[pl.ds(KV_PAD, v_ref.shape[0]), :] = v_ref[...]


def _attn_fwd(qkv, bias):
    S = qkv.shape[0]
    nhp = N_HEADS // 2

    def body(q_ref, k_ref, v_ref, b_ref, o_ref, kp, vp, s_scr, p_scr):
        qb = pl.program_id(1)
        _pad_keys(qb, k_ref, v_ref, kp, vp)
        span = pl.ds(pl.multiple_of(qb * Q_BLOCK, Q_BLOCK), KV_SPAN)
        kc, vc = kp[span, :], vp[span, :]
        qt = q_ref[...] * ATT_SCALE
        first = lax.broadcasted_iota(jnp.int32, (1, HEAD_PAIR), 1) < HEAD_DIM
        outs = []
        for j in range(2):
            qj = jnp.where(first if j == 0 else ~first, qt, jnp.zeros_like(qt))
            s_scr[j] = _bdot(qj, kc, NT)
            for r0 in range(0, Q_BLOCK, ATT_ROWS):
                rows = pl.ds(r0, ATT_ROWS)
                p_scr[j, rows, :] = _softmax_piece(s_scr, b_ref, j, rows, qb).astype(BF16)
            outs.append(_bdot(p_scr[j], vc))
        o_ref[...] = jnp.where(first, outs[0], outs[1]).astype(BF16)

    return pl.pallas_call(
        body,
        out_shape=jax.ShapeDtypeStruct((S, D_MODEL), BF16),
        grid=(nhp, S // Q_BLOCK),
        in_specs=[pl.BlockSpec((Q_BLOCK, HEAD_PAIR), lambda h, i: (i, h)),
                  pl.BlockSpec((S, HEAD_PAIR), lambda h, i: (0, nhp + h)),
                  pl.BlockSpec((S, HEAD_PAIR), lambda h, i: (0, 2 * nhp + h)),
                  pl.BlockSpec((2, Q_BLOCK, KV_SPAN), lambda h, i: (h, 0, 0))],
        out_specs=pl.BlockSpec((Q_BLOCK, HEAD_PAIR), lambda h, i: (i, h)),
        scratch_shapes=[pltpu.VMEM((KV_PAD + S, HEAD_PAIR), BF16), pltpu.VMEM((KV_PAD + S, HEAD_PAIR), BF16),
                        pltpu.VMEM((2, Q_BLOCK, KV_SPAN), F32), pltpu.VMEM((2, Q_BLOCK, KV_SPAN), BF16)],
        compiler_params=_cparams(("parallel", "arbitrary")),
        name="attn_fwd",
    )(qkv, qkv, qkv, bias)


def _attn_bwd(qkv, bias, do):
    S = qkv.shape[0]
    nhp = N_HEADS // 2
    nq = S // Q_BLOCK
    scale = HEAD_DIM ** -0.5

    def body(q_ref, k_ref, v_ref, b_ref, do_ref, dq_ref, dk_ref, dv_ref, db_ref, kp, vp, dka, dva,
             s_scr, dp_scr, p_scr, ds_scr):
        qb = pl.program_id(1)
        _pad_keys(qb, k_ref, v_ref, kp, vp)

        @pl.when(qb == 0)
        def _():
            dka[...] = jnp.zeros_like(dka)
            dva[...] = jnp.zeros_like(dva)
            db_ref[...] = jnp.zeros_like(db_ref)

        span = pl.ds(pl.multiple_of(qb * Q_BLOCK, Q_BLOCK), KV_SPAN)
        kc, vc = kp[span, :], vp[span, :]
        qt, dot = q_ref[...] * ATT_SCALE, do_ref[...]
        first = lax.broadcasted_iota(jnp.int32, (1, HEAD_PAIR), 1) < HEAD_DIM
        dqs = []
        for j in range(2):
            mine = first if j == 0 else ~first
            qj = jnp.where(mine, qt, jnp.zeros_like(qt))
            doj = jnp.where(mine, dot, jnp.zeros_like(dot))
            s_scr[j] = _bdot(qj, kc, NT)
            dp_scr[j] = _bdot(doj, vc, NT)
            for r0 in range(0, Q_BLOCK, ATT_ROWS):
                rows = pl.ds(r0, ATT_ROWS)
                p = _softmax_piece(s_scr, b_ref, j, rows, qb)
                dp = dp_scr[j, rows, :]
                ds = p * (dp - jnp.sum(p * dp, axis=-1, keepdims=True))
                db_ref[j, rows, :] += ds
                p_scr[j, rows, :] = p.astype(BF16)
                ds_scr[j, rows, :] = ds.astype(BF16)
            dva[span, :] += _bdot(p_scr[j], doj, TN)
            dqs.append(_bdot(ds_scr[j], kc))
            dka[span, :] += _bdot(ds_scr[j], qj, TN)
        dq_ref[...] = (scale * jnp.where(first, dqs[0], dqs[1])).astype(BF16)

        @pl.when(qb == nq - 1)
        def _():
            dk_ref[...] = dka[pl.ds(KV_PAD, S), :].astype(BF16)
            dv_ref[...] = dva[pl.ds(KV_PAD, S), :].astype(BF16)

    blk = pl.BlockSpec((Q_BLOCK, HEAD_PAIR), lambda h, i: (i, h))
    col = pl.BlockSpec((S, HEAD_PAIR), lambda h, i: (0, h))
    bsp = pl.BlockSpec((2, Q_BLOCK, KV_SPAN), lambda h, i: (h, 0, 0))
    return pl.pallas_call(
        body,
        out_shape=[jax.ShapeDtypeStruct((S, D_MODEL), BF16)] * 3
        + [jax.ShapeDtypeStruct((N_HEADS, Q_BLOCK, KV_SPAN), F32)],
        grid=(nhp, nq),
        in_specs=[blk, pl.BlockSpec((S, HEAD_PAIR), lambda h, i: (0, nhp + h)),
                  pl.BlockSpec((S, HEAD_PAIR), lambda h, i: (0, 2 * nhp + h)), bsp, blk],
        out_specs=[blk, col, col, bsp],
        scratch_shapes=[pltpu.VMEM((KV_PAD + S, HEAD_PAIR), BF16), pltpu.VMEM((KV_PAD + S, HEAD_PAIR), BF16),
                        pltpu.VMEM((KV_PAD + S, HEAD_PAIR), F32), pltpu.VMEM((KV_PAD + S, HEAD_PAIR), F32),
                        pltpu.VMEM((2, Q_BLOCK, KV_SPAN), F32), pltpu.VMEM((2, Q_BLOCK, KV_SPAN), F32),
                        pltpu.VMEM((2, Q_BLOCK, KV_SPAN), BF16), pltpu.VMEM((2, Q_BLOCK, KV_SPAN), BF16)],
        compiler_params=_cparams(("parallel", "arbitrary")),
        name="attn_bwd",
    )(qkv, qkv, qkv, bias, do)


def _bias_blocks(rel_bias):
    H = rel_bias.shape[0]
    n_e = BAND + CHUNK - 1
    n_clip = KV_PAD + CHUNK - 1 - MAX_REL + 1
    e = jnp.concatenate([jnp.broadcast_to(rel_bias[:, 2 * MAX_REL:], (H, n_clip)),
                         jnp.flip(rel_bias[:, 2 * MAX_REL - (n_e - n_clip):2 * MAX_REL], axis=1)], axis=1)
    skew = jnp.pad(jnp.tile(e, (1, CHUNK)), ((0, 0), (0, CHUNK))).reshape(H, CHUNK, n_e + 1)
    band = jnp.flip(skew, axis=1)[:, :, :BAND]
    rows = [jnp.pad(band, ((0, 0), (0, 0), (c * CHUNK, KV_SPAN - BAND - c * CHUNK)), constant_values=NEG_INF)
            for c in range(Q_BLOCK // CHUNK)]
    return jnp.concatenate(rows, axis=1)


def _bias_blocks_grad(dblk):
    H = dblk.shape[0]
    n_e = BAND + CHUNK - 1
    n_clip = KV_PAD + CHUNK - 1 - MAX_REL + 1
    parts = jnp.stack([dblk[:, c * CHUNK:(c + 1) * CHUNK, c * CHUNK:c * CHUNK + BAND]
                       for c in range(Q_BLOCK // CHUNK)], axis=1)
    parts = jnp.flip(parts, axis=2)
    parts = jnp.pad(parts, ((0, 0), (0, 0), (0, 0), (0, n_e + 1 - BAND)))
    skew = parts.reshape(H, Q_BLOCK // CHUNK, CHUNK * (n_e + 1))[:, :, :CHUNK * n_e]
    skew = skew.reshape(H, Q_BLOCK, n_e)
    skew = jnp.pad(skew, ((0, 0), (0, 0), (0, 1)))

    def body(s_ref, o_ref):
        de = jnp.sum(s_ref[...], axis=0, keepdims=True)
        lane = lax.broadcasted_iota(jnp.int32, de.shape, 1)
        far = jnp.sum(jnp.where(lane < n_clip, de, 0.0), axis=-1, keepdims=True)
        o_ref[...] = jnp.where(lane == 0, far, jnp.where(lane < n_clip, 0.0, de))

    de = pl.pallas_call(
        body,
        out_shape=jax.ShapeDtypeStruct((H, 1, n_e + 1), F32),
        grid=(H,),
        in_specs=[pl.BlockSpec((None, Q_BLOCK, n_e + 1), lambda h: (h, 0, 0))],
        out_specs=pl.BlockSpec((None, 1, n_e + 1), lambda h: (h, 0, 0)),
        compiler_params=_cparams(("parallel",)),
        name="bias_grad_sum",
    )(skew).reshape(H, n_e + 1)
    near = jnp.flip(de[:, n_clip:n_e], axis=1)
    return jnp.concatenate([jnp.zeros((H, 2 * MAX_REL - (n_e - n_clip)), F32), near, de[:, 0:1]], axis=1)


def _ffn_forward(r1, p_l, w, l, ready):
    ready(f"up{l}", r1)
    up_g = _mm_rows([(r1, w["ffn_up_g"][l], False)], out_dtype=BF16, name=f"ffn_up_g{l}")
    up_v = _mm_rows([(r1, w["ffn_up_v"][l], False)], out_dtype=BF16, name=f"ffn_up_v{l}")
    h = _ffn_act_fwd(up_g, up_v, w["ffn_dw_w"][l], w["ffn_dw_b"][l], name=f"ffn_act{l}")
    ready(f"dn{l}", h)
    z2, r2, gate, proj = _proj_ln(r1, h, w["ffn_w_down"][l], w["ln_ffn_g"][l], w["ln_ffn_b"][l],
                                  ple=(w["ple_w_gate"][l], w["ple_b_gate"][l], p_l, w["ple_w_proj"][l]),
                                  name=f"ffn_down_ln{l}")
    return dict(r1=r1, up_g=up_g, up_v=up_v, h=h, z2=z2, gate=gate, proj=proj), r2


def _ffn_backward(sv, dz2, p_l, w, l, grads):
    r1 = sv["r1"]
    ds, dproj, db_gate = _ple_bwd(dz2, sv["gate"], sv["proj"], name=f"ple_bwd{l}")
    dh = _mm_rows([(dz2, w["ffn_w_down"][l], True)], out_dtype=BF16, name=f"ffn_dh{l}")
    dgate, dval, d_dw_w, d_dw_b = _ffn_act_bwd(sv["up_g"], sv["up_v"], dh, w["ffn_dw_w"][l], w["ffn_dw_b"][l],
                                               name=f"ffn_act_bwd{l}")
    grads["ffn_w_down"][l] = _wgrad(sv["h"], dz2, tm=1408, name=f"d_ffn_w_down{l}")
    grads["ffn_up_g"][l] = _wgrad(r1, dgate, tn=1408, piece=D_FF // 4, name=f"d_ffn_up_g{l}")
    grads["ffn_up_v"][l] = _wgrad(r1, dval, tn=1408, piece=D_FF // 4, name=f"d_ffn_up_v{l}")
    grads["ple_w_gate"][l] = _wgrad(r1, ds, name=f"d_ple_w_gate{l}")
    grads["ple_w_proj"][l] = _wgrad(p_l, dproj, piece=D_MODEL // N_DEV, name=f"d_ple_w_proj{l}")
    grads["ffn_dw_w"][l] = d_dw_w
    grads["ffn_dw_b"][l] = d_dw_b[0]
    grads["ple_b_gate"][l] = db_gate[0]
    return _mm_rows([(ds, w["ple_w_gate"][l], True), (dgate, w["ffn_up_g"][l], True), (dval, w["ffn_up_v"][l], True)],
                    add=dz2, add_scale=ALPHA, name=f"dr1_{l}")


def _local_step(x, p, target, w, ready=lambda group, after: None, emit=lambda group, grads: None):
    grads = {k: [None, None] for k in ("ffn_w_down", "ffn_up_g", "ffn_up_v", "ple_w_gate", "ple_w_proj", "ffn_dw_w",
                                       "ffn_dw_b", "ple_b_gate", "ln_ffn_g", "ln_ffn_b", "ln_mix_g", "ln_mix_b")}

    ready("mix", None)
    u = _mm_rows([(x, w["mix_w_in"], False)], name="mix_in")
    ycat, dpool = _mixer_fwd(u, w["pool_w"], w["pool_scale"], w["conv_dw_w"], w["conv_dw_b"], w["conv_ln_g"],
                             w["conv_ln_b"])
    ready("mixo", ycat)
    z1, r1 = _proj_ln(x, ycat, w["mix_w_out"], w["ln_mix_g"][0], w["ln_mix_b"][0], name="mix_out_ln")
    sv0, r2 = _ffn_forward(r1, p[0], w, 0, ready)

    ready("attn", r2)
    qkv = _mm_rows([(r2, w["attn_w_qkv"], False)], out_dtype=BF16, name="attn_qkv")
    bias = _bias_blocks(w["attn_rel_bias"])
    attn = _attn_fwd(qkv, bias)
    z3, r3 = _proj_ln(r2, attn, w["attn_w_o"], w["ln_mix_g"][1], w["ln_mix_b"][1], name="attn_out_ln")
    sv1, _ = _ffn_forward(r3, p[1], w, 1, ready)

    dz4, grads["ln_ffn_g"][1], grads["ln_ffn_b"][1], loss = _ln_bwd(sv1["z2"], w["ln_ffn_g"][1], w["ln_ffn_b"][1],
                                                                    target, loss_head=True, name="loss_ln_bwd")
    dr3 = _ffn_backward(sv1, dz4, p[1], w, 1, grads)
    dz3, grads["ln_mix_g"][1], grads["ln_mix_b"][1], _ = _ln_bwd(z3, w["ln_mix_g"][1], w["ln_mix_b"][1], dr3,
                                                                dep=emit("ffn1", grads), name="ln_mix_bwd1")
    grads["attn_w_o"] = _wgrad(attn, dz3, name="d_attn_w_o")
    dattn = _mm_rows([(dz3, w["attn_w_o"], True)], out_dtype=BF16, name="d_attn")
    dq, dk, dv, dbias = _attn_bwd(qkv, bias, dattn)
    grads["attn_rel_bias"] = _bias_blocks_grad(dbias)
    dqkv = jnp.concatenate([dq, dk, dv], axis=1)
    grads["attn_w_qkv"] = _wgrad(r2, dqkv, tn=768, piece=3 * D_MODEL // N_DEV, name="d_attn_w_qkv")
    dr2 = _mm_rows([(dqkv, w["attn_w_qkv"], True)], add=dz3, add_scale=ALPHA, dep=emit("attn", grads), name="dr2")

    dz2, grads["ln_ffn_g"][0], grads["ln_ffn_b"][0], _ = _ln_bwd(sv0["z2"], w["ln_ffn_g"][0], w["ln_ffn_b"][0], dr2,
                                                                name="ln_ffn_bwd0")
    dr1 = _ffn_backward(sv0, dz2, p[0], w, 0, grads)
    dz1, grads["ln_mix_g"][0], grads["ln_mix_b"][0], _ = _ln_bwd(z1, w["ln_mix_g"][0], w["ln_mix_b"][0], dr1,
                                                                dep=emit("ffn0", grads), name="ln_mix_bwd0")
    grads["mix_w_out"] = _wgrad(ycat, dz1, name="d_mix_w_out")
    dycat = _mm_rows([(dz1, w["mix_w_out"], True)], name="d_ycat")
    du, g_pw, g_ps, g_cw, g_cb, g_cg, g_cbb = _mixer_bwd(u, dpool, dycat, w["pool_w"], w["pool_scale"],
                                                         w["conv_dw_w"], w["conv_dw_b"], w["conv_ln_g"],
                                                         w["conv_ln_b"])
    grads["mix_w_in"] = _wgrad(x, du, tn=768, piece=3 * D_POOL // N_DEV, name="d_mix_w_in")
    grads["conv_dw_w"] = g_cw
    grad_x = _mm_rows([(du, w["mix_w_in"], True)], add=dz1, add_scale=ALPHA, dep=emit("mix", grads), name="grad_x")
    grads.update(pool_w=g_pw, pool_scale=g_ps[0], conv_dw_w=g_cw, conv_dw_b=g_cb[0], conv_ln_g=g_cg[0],
                 conv_ln_b=g_cbb[0])
    for kname in ("ln_ffn_g", "ln_ffn_b", "ln_mix_g", "ln_mix_b"):
        grads[kname] = [a[0] for a in grads[kname]]
    return loss[0, 0], grad_x, grads


_HBM = pl.BlockSpec(memory_space=pltpu.HBM)
_SEM = pl.BlockSpec(memory_space=pltpu.SEMAPHORE)
_EFFECT = pltpu.SideEffectType.DATAFLOW_SIDE_EFFECTING


def _slot(ref, place, shape, k):
    if place in ("stack", "pieces"):
        return ref.at[k]
    ax = place[1]
    n = shape[ax]
    return ref.at[(slice(None),) * ax + (pl.ds(pl.multiple_of(k * n, n), n),)]


def _result_shape(buf, place):
    if place == "stack":
        return (N_DEV,) + buf.shape
    if place == "pieces":
        return buf.shape
    return tuple(s * N_DEV if i == place[1] else s for i, s in enumerate(buf.shape))


def _peers(x, y, c):
    for d in range(1, N_DEV):
        px, py, pc = x ^ ((d >> 2) & 1), y ^ ((d >> 1) & 1), c ^ (d & 1)
        yield d, (px, py, pc), 4 * px + 2 * py + pc


def _exchange_start(bufs, places, after, *, name):
    nb = len(bufs)
    lands = [lax.empty(_result_shape(b, p_), b.dtype) for b, p_ in zip(bufs, places)]
    has_after = after is not None

    def body(*refs):
        srcs, dsts = refs[:nb], refs[nb:2 * nb]
        outs = refs[2 * nb + has_after:]
        send_sems, recv_sems, token = outs[0], outs[1], outs[2 + 2 * nb]
        x, y, c = lax.axis_index("x"), lax.axis_index("y"), lax.axis_index("c")
        me = 4 * x + 2 * y + c
        for b in range(nb):
            for d, dev, peer in _peers(x, y, c):
                pltpu.make_async_remote_copy(
                    src_ref=srcs[b].at[peer] if places[b] == "pieces" else srcs[b],
                    dst_ref=_slot(dsts[b], places[b], bufs[b].shape, me),
                    send_sem=send_sems.at[b * N_DEV + d], recv_sem=recv_sems.at[b * N_DEV + d],
                    device_id=dev, device_id_type=pl.DeviceIdType.MESH).start()
            pltpu.make_async_copy(srcs[b].at[me] if places[b] == "pieces" else srcs[b],
                                  _slot(dsts[b], places[b], bufs[b].shape, me), recv_sems.at[b * N_DEV]).start()
        token[...] = jnp.zeros_like(token)

    sems = pltpu.SemaphoreType.DMA((nb * N_DEV,))
    ins = [pltpu.with_memory_space_constraint(a, pltpu.HBM) for a in list(bufs) + lands]
    out = pl.pallas_call(
        body,
        out_shape=(sems, sems, *[pltpu.HBM(a.shape, a.dtype) for a in ins], jax.ShapeDtypeStruct((8, 128), F32)),
        in_specs=[_HBM] * (2 * nb) + ([pl.BlockSpec(memory_space=pl.ANY)] if has_after else []),
        out_specs=(_SEM, _SEM, *[_HBM] * (2 * nb), pl.BlockSpec(memory_space=pltpu.VMEM)),
        input_output_aliases={i: 2 + i for i in range(2 * nb)},
        compiler_params=pltpu.CompilerParams(has_side_effects=_EFFECT),
        name=name,
    )(*ins, *([after] if has_after else []))
    return dict(send=out[0], recv=out[1], srcs=out[2:2 + nb], lands=out[2 + nb:2 + 2 * nb], token=out[-1],
                places=places)


def _exchange_wait(h, after, *, name):
    nb = len(h["srcs"])
    places = h["places"]
    shapes = [a.shape for a in h["srcs"]]

    def body(*refs):
        srcs, dsts, send_sems, recv_sems = refs[:nb], refs[nb:2 * nb], refs[2 * nb], refs[2 * nb + 1]
        x, y, c = lax.axis_index("x"), lax.axis_index("y"), lax.axis_index("c")
        me = 4 * x + 2 * y + c
        for b in range(nb):
            pieces = places[b] == "pieces"
            for d, dev, peer in _peers(x, y, c):
                cp = pltpu.make_async_remote_copy(
                    src_ref=srcs[b].at[peer] if pieces else srcs[b],
                    dst_ref=_slot(dsts[b], places[b], shapes[b], peer),
                    send_sem=send_sems.at[b * N_DEV + d], recv_sem=recv_sems.at[b * N_DEV + d],
                    device_id=dev, device_id_type=pl.DeviceIdType.MESH)
                cp.wait_send()
                cp.wait_recv()
            pltpu.make_async_copy(srcs[b].at[me] if pieces else srcs[b], _slot(dsts[b], places[b], shapes[b], me),
                                  recv_sems.at[b * N_DEV]).wait()

    ins = list(h["srcs"]) + list(h["lands"])
    out = pl.pallas_call(
        body,
        out_shape=tuple(pltpu.HBM(a.shape, a.dtype) for a in ins),
        in_specs=[_HBM] * (2 * nb) + [_SEM, _SEM, pl.BlockSpec(memory_space=pl.ANY)],
        out_specs=tuple([_HBM] * (2 * nb)),
        input_output_aliases={i: i for i in range(2 * nb)},
        compiler_params=pltpu.CompilerParams(has_side_effects=_EFFECT),
        name=name,
    )(*ins, h["send"], h["recv"], after)
    return out[nb:]


def _adamw(recv, w, m, v, *, name):
    R, C = w.shape
    tr = R
    for cand in (512, 256, 128, 64, 32, 16):
        if R % cand == 0 and cand * C * 4 <= 2 * 1024 * 1024:
            tr = cand
            break
    c1 = 1.0 - ADAM_B1 ** ADAM_STEP
    c2 = 1.0 - ADAM_B2 ** ADAM_STEP

    def body(r_ref, w_ref, m_ref, v_ref, g_ref, d_ref, mo_ref, vo_ref):
        g = r_ref[0].astype(F32)
        for i in range(1, N_DEV):
            g = g + r_ref[i].astype(F32)
        m_new = ADAM_B1 * m_ref[...] + (1.0 - ADAM_B1) * g
        v_new = ADAM_B2 * v_ref[...] + (1.0 - ADAM_B2) * (g * g)
        m_hat = m_new / c1
        v_hat = v_new / c2
        g_ref[...] = g
        d_ref[...] = -ADAM_LR * (m_hat / (jnp.sqrt(v_hat) + ADAM_EPS) + ADAM_WD * w_ref[...])
        mo_ref[...] = m_new
        vo_ref[...] = v_new

    row = pl.BlockSpec((tr, C), lambda i: (i, 0))
    return pl.pallas_call(
        body,
        out_shape=[jax.ShapeDtypeStruct((R, C), F32)] * 4,
        grid=(R // tr,),
        in_specs=[pl.BlockSpec((N_DEV, tr, C), lambda i: (0, i, 0)), row, row, row],
        out_specs=[row] * 4,
        compiler_params=_cparams(("parallel",)),
        name=name,
    )(recv, w, m, v)


def _ffn_groups(l):
    return ((f"up{l}", (("ffn_w_up", l, BF16, "stack"), ("ffn_dw_w", l, F32, "stack"))),
            (f"dn{l}", (("ffn_w_down", l, BF16, ("axis", 0)), ("ple_w_gate", l, BF16, ("axis", 0)),
                        ("ple_w_proj", l, BF16, ("axis", 1)))))


_GATHER_GROUPS = (
    ("mix", (("mix_w_in", 0, BF16, "stack"), ("conv_dw_w", 0, F32, "stack"))),
    ("mixo", (("mix_w_out", 0, BF16, ("axis", 0)),)),
    *_ffn_groups(0),
    ("attn", (("attn_w_qkv", 0, BF16, ("axis", 1)), ("attn_w_o", 0, BF16, ("axis", 0)))),
    *_ffn_groups(1))
_SHARDED = ("mix_w_in", "conv_dw_w", "mix_w_out", "attn_w_qkv", "attn_w_o", "ffn_w_up", "ffn_dw_w", "ffn_w_down",
            "ple_w_gate", "ple_w_proj")
_REPLICATED = ("pool_w", "pool_scale", "conv_dw_b", "conv_ln_g", "conv_ln_b", "attn_rel_bias", "ln_mix_g",
               "ln_mix_b", "ffn_dw_b", "ple_b_gate", "ln_ffn_g", "ln_ffn_b")


def _pack_rows(parts, row_mult, dtype):
    lead = parts[0].shape[:-1]
    flat = jnp.concatenate([a.astype(dtype) for a in parts], axis=-1)
    n = flat.shape[-1]
    unit = row_mult * LANES
    padded = -(-n // unit) * unit
    flat = jnp.pad(flat, [(0, 0)] * len(lead) + [(0, padded - n)])
    return flat.reshape(lead + (padded // LANES, LANES))


def _unpack(flat2d, shapes):
    flat = flat2d.reshape(-1)
    out, o = [], 0
    for s in shapes:
        n = math.prod(s)
        out.append(flat[o:o + n].reshape(s))
        o += n
    return out


def _full_from_shards(g, axis):
    parts = jnp.moveaxis(g, 0, axis)
    shp = list(g.shape[1:])
    shp[axis] *= g.shape[0]
    return parts.reshape(shp)


def _pieces_from_full(full, axis, k=N_DEV):
    shp = list(full.shape)
    n = shp[axis] // k
    t = full.reshape(shp[:axis] + [k, n] + shp[axis + 1:])
    return jnp.moveaxis(t, axis, 0)


def kernel(x, p, mix_w_in, pool_w, pool_scale, conv_dw_w, conv_dw_b, conv_ln_g, conv_ln_b, mix_w_out, attn_w_qkv, attn_rel_bias, attn_w_o, ln_mix_g, ln_mix_b, ffn_w_up, ffn_dw_w, ffn_dw_b, ffn_w_down, ple_w_proj, ple_w_gate, ple_b_gate, ln_ffn_g, ln_ffn_b, loss_target, m_mix_w_in, m_pool_w, m_pool_scale, m_conv_dw_w, m_conv_dw_b, m_conv_ln_g, m_conv_ln_b, m_mix_w_out, m_attn_w_qkv, m_attn_rel_bias, m_attn_w_o, m_ln_mix_g, m_ln_mix_b, m_ffn_w_up, m_ffn_dw_w, m_ffn_dw_b, m_ffn_w_down, m_ple_w_proj, m_ple_w_gate, m_ple_b_gate, m_ln_ffn_g, m_ln_ffn_b, v_mix_w_in, v_pool_w, v_pool_scale, v_conv_dw_w, v_conv_dw_b, v_conv_ln_g, v_conv_ln_b, v_mix_w_out, v_attn_w_qkv, v_attn_rel_bias, v_attn_w_o, v_ln_mix_g, v_ln_mix_b, v_ffn_w_up, v_ffn_dw_w, v_ffn_dw_b, v_ffn_w_down, v_ple_w_proj, v_ple_w_gate, v_ple_b_gate, v_ln_ffn_g, v_ln_ffn_b):
    a = dict(locals())
    sh_names = list(_SHARDED)
    names = sh_names + list(_REPLICATED)
    wts = {n: a[n] for n in names}
    mom = {n: a["m_" + n] for n in names}
    var = {n: a["v_" + n] for n in names}

    gather = {}
    token = None
    for group, items in _GATHER_GROUPS:
        gather[group] = _exchange_start([wts[n][l].astype(dt) for n, l, dt, _ in items], [pl_ for *_, pl_ in items],
                                        token, name="gather_start_" + group)
        token = gather[group]["token"]

    w = dict(pool_w=pool_w[0], pool_scale=pool_scale[0], conv_dw_b=conv_dw_b[0], conv_ln_g=conv_ln_g[0],
             conv_ln_b=conv_ln_b[0], attn_rel_bias=attn_rel_bias[0], ln_mix_g=ln_mix_g, ln_mix_b=ln_mix_b,
             ffn_dw_b=ffn_dw_b, ple_b_gate=ple_b_gate, ln_ffn_g=ln_ffn_g, ln_ffn_b=ln_ffn_b)
    for n in ("ffn_up_g", "ffn_up_v", "ffn_dw_w", "ffn_w_down", "ple_w_gate", "ple_w_proj"):
        w[n] = [None, None]

    def ready(group, after):
        got = _exchange_wait(gather[group], token if after is None else after, name="gather_wait_" + group)
        if group == "mix":
            w["mix_w_in"] = _join_shards(got[0], name="join_mix_w_in")
            w["conv_dw_w"] = _full_from_shards(got[1], 1)
        elif group == "mixo":
            (w["mix_w_out"],) = got
        elif group == "attn":
            w["attn_w_qkv"], w["attn_w_o"] = got
        elif group[:2] == "up":
            l = int(group[2])
            w["ffn_up_g"][l] = _join_shards(got[0], k=N_DEV // 2, part=0, name=f"join_ffn_up_g{l}")
            w["ffn_up_v"][l] = _join_shards(got[0], k=N_DEV // 2, part=1, name=f"join_ffn_up_v{l}")
            w["ffn_dw_w"][l] = _full_from_shards(got[1], 1)
        else:
            l = int(group[2])
            w["ffn_w_down"][l], w["ple_w_gate"][l], w["ple_w_proj"][l] = got

    scatter = {}

    def emit(group, gr):
        if group[:3] == "ffn":
            l = int(group[3])
            pieces = [jnp.concatenate([gr["ffn_up_g"][l], gr["ffn_up_v"][l]]),
                      _pieces_from_full(gr["ffn_dw_w"][l], 1), _pieces_from_full(gr["ffn_w_down"][l], 0),
                      _pieces_from_full(gr["ple_w_gate"][l], 0), gr["ple_w_proj"][l]]
        elif group == "attn":
            pieces = [gr["attn_w_qkv"], _pieces_from_full(gr["attn_w_o"], 0)]
        else:
            pieces = [gr["mix_w_in"], _pieces_from_full(gr["conv_dw_w"], 1), _pieces_from_full(gr["mix_w_out"], 0)]
        scatter[group] = _exchange_start([a.astype(BF16) for a in pieces], ["pieces"] * len(pieces), None,
                                         name="grad_start_" + group)
        return scatter[group]["token"]

    loss_part, grad_x, gr = _local_step(x[0], p[:, 0], loss_target[0], w, ready, emit)
    loss = lax.psum(loss_part, ("x", "y", "c"))

    gfull = dict(
        pool_w=gr["pool_w"][None], pool_scale=gr["pool_scale"][None], conv_dw_b=gr["conv_dw_b"][None],
        conv_ln_g=gr["conv_ln_g"][None], conv_ln_b=gr["conv_ln_b"][None], attn_rel_bias=gr["attn_rel_bias"][None],
        ln_mix_g=jnp.stack(gr["ln_mix_g"]), ln_mix_b=jnp.stack(gr["ln_mix_b"]), ffn_dw_b=jnp.stack(gr["ffn_dw_b"]),
        ple_b_gate=jnp.stack(gr["ple_b_gate"]), ln_ffn_g=jnp.stack(gr["ln_ffn_g"]),
        ln_ffn_b=jnp.stack(gr["ln_ffn_b"]))
    rep_send = _pack_rows([gfull[n].reshape(-1) for n in _REPLICATED], 8, F32)
    rep_handle = _exchange_start([rep_send], ["stack"], None, name="grad_start_replicated")

    group_weights = {"ffn1": (("ffn_w_up", 1), ("ffn_dw_w", 1), ("ffn_w_down", 1), ("ple_w_gate", 1), ("ple_w_proj", 1)),
                     "attn": (("attn_w_qkv", 0), ("attn_w_o", 0)),
                     "ffn0": (("ffn_w_up", 0), ("ffn_dw_w", 0), ("ffn_w_down", 0), ("ple_w_gate", 0), ("ple_w_proj", 0)),
                     "mix": (("mix_w_in", 0), ("conv_dw_w", 0), ("mix_w_out", 0))}
    per_layer = {n: {} for n in sh_names}
    after = grad_x
    for group in ("ffn1", "attn", "ffn0", "mix"):
        recv = _exchange_wait(scatter[group], after, name="grad_wait_" + group)
        for (n, l), r in zip(group_weights[group], recv):
            per_layer[n][l] = _adamw(r, wts[n][l], mom[n][l], var[n][l], name=f"adamw_{n}{l}")
            after = per_layer[n][l][0]
    res = [{}, {}, {}, {}]
    for n in sh_names:
        for k in range(4):
            res[k][n] = jnp.stack([per_layer[n][l][k] for l in sorted(per_layer[n])])
    (rep_recv,) = _exchange_wait(rep_handle, after, name="grad_wait_replicated")

    def flat_state(d):
        return _pack_rows([d[n].reshape(-1) for n in _REPLICATED], 8, F32)

    rep_out = _adamw(rep_recv, flat_state(wts), flat_state(mom), flat_state(var), name="adamw_replicated")
    for k in range(4):
        for n, arr in zip(_REPLICATED, _unpack(rep_out[k], [wts[n].shape for n in _REPLICATED])):
            res[k][n] = arr
    order = ["mix_w_in", "pool_w", "pool_scale", "conv_dw_w", "conv_dw_b", "conv_ln_g", "conv_ln_b", "mix_w_out",
             "attn_w_qkv", "attn_rel_bias", "attn_w_o", "ln_mix_g", "ln_mix_b", "ffn_w_up", "ffn_dw_w", "ffn_dw_b",
             "ffn_w_down", "ple_w_proj", "ple_w_gate", "ple_b_gate", "ln_ffn_g", "ln_ffn_b"]
    outs = [loss, grad_x[None]]
    for k in range(4):
        outs += [res[k][n] for n in order]
    return tuple(outs)
```

```python
import functools
import math

import jax
import jax.numpy as jnp
from jax import lax
from jax.experimental import pallas as pl
from jax.experimental.pallas import tpu as pltpu

F32 = jnp.float32
BF16 = jnp.bfloat16

N_DEV = 8
D_MODEL = 1024
D_POOL = 512
D_CONV = 512
POOL_WINDOWS = (2, 4, 8, 16)
POOL_GROUP = 128
CONV_KERNEL = 31
CHUNK = 64
HEAD_DIM = 64
N_HEADS = 16
LEFT_CHUNKS = 8
BAND = (LEFT_CHUNKS + 1) * CHUNK
MAX_REL = 256
D_FF = 2816
PLE_DIM = 256
ALPHA = 4.0 ** 0.25
LN_EPS = 1e-5
NEG_INF = -1e30
ADAM_LR, ADAM_B1, ADAM_B2, ADAM_EPS, ADAM_WD, ADAM_STEP = 0.001, 0.9, 0.999, 1e-08, 0.01, 10

Q_BLOCK = 4 * CHUNK
KV_PAD = LEFT_CHUNKS * CHUNK
KV_SPAN = KV_PAD + Q_BLOCK
CONV_HALO = 32
FFN_HALO = 16
SUB_ROWS, SUB_LANES = 64, 128
LANES = 1024
VMEM_LIMIT = 56 * 1024 * 1024


def _cparams(sem=None):
    return pltpu.CompilerParams(dimension_semantics=sem, vmem_limit_bytes=VMEM_LIMIT)


def _tile(dim, pref):
    if dim <= pref:
        return dim
    t = pref - pref % 128
    while t >= 128:
        if dim % t == 0:
            return t
        t -= 128
    return dim


def _sigmoid(x):
    return 1.0 / (1.0 + jnp.exp(-x))


def _bdot(a, b, dn=(((1,), (0,)), ((), ()))):
    return lax.dot_general(a.astype(BF16), b.astype(BF16), dn, preferred_element_type=F32)


NT = (((1,), (1,)), ((), ()))
TN = (((0,), (0,)), ((), ()))


def _wgrad(a, b, *, tm=1024, tn=1024, tk=1024, piece=None, name):
    K, M = a.shape
    kb, N = b.shape
    assert K == kb, (a.shape, b.shape)
    tm, tn, tk = _tile(M, tm), _tile(N, tn), _tile(K, tk)
    nk = K // tk
    per = 1 if piece is None else tn // piece
    assert piece is None or tn == per * piece

    def body(a_ref, b_ref, o_ref, acc):
        k = pl.program_id(2)

        @pl.when(k == 0)
        def _():
            acc[...] = jnp.zeros_like(acc)

        acc[...] += _bdot(a_ref[...], b_ref[...], TN)

        @pl.when(k == nk - 1)
        def _():
            if piece is None:
                o_ref[...] = acc[...].astype(BF16)
            else:
                for s in range(per):
                    o_ref[s] = acc[:, s * piece:(s + 1) * piece].astype(BF16)

    if piece is None:
        out_shape, out_spec = (M, N), pl.BlockSpec((tm, tn), lambda i, j, k: (i, j))
    else:
        out_shape, out_spec = (N // piece, M, piece), pl.BlockSpec((per, tm, piece), lambda i, j, k: (j, i, 0))
    return pl.pallas_call(
        body,
        out_shape=jax.ShapeDtypeStruct(out_shape, BF16),
        grid=(M // tm, N // tn, nk),
        in_specs=[pl.BlockSpec((tk, tm), lambda i, j, k: (k, i)), pl.BlockSpec((tk, tn), lambda i, j, k: (k, j))],
        out_specs=out_spec,
        scratch_shapes=[pltpu.VMEM((tm, tn), F32)],
        compiler_params=_cparams(("parallel", "parallel", "arbitrary")),
        name=name,
    )(a, b)


def _join_shards(g, *, k=None, part=0, tm=256, name):
    n, M, wd = g.shape
    k = n if k is None else k

    def body(g_ref, o_ref):
        for j in range(k):
            o_ref[:, j * wd:(j + 1) * wd] = g_ref[j]

    return pl.pallas_call(
        body,
        out_shape=jax.ShapeDtypeStruct((M, k * wd), g.dtype),
        grid=(M // tm,),
        in_specs=[pl.BlockSpec((k, tm, wd), lambda i: (part, i, 0))],
        out_specs=pl.BlockSpec((tm, k * wd), lambda i: (i, 0)),
        compiler_params=_cparams(("parallel",)),
        name=name,
    )(g)


def _mm_rows(pairs, *, add=None, add_scale=1.0, out_dtype=F32, tm=256, dep=None, name):
    M = pairs[0][0].shape[0]
    N = pairs[0][1].shape[0] if pairs[0][2] else pairs[0][1].shape[1]
    n = len(pairs)
    has_add = add is not None

    def body(*refs):
        o_ref = refs[-1]
        acc = None
        for i, (_, _, tr) in enumerate(pairs):
            part = _bdot(refs[2 * i][...], refs[2 * i + 1][...], NT if tr else (((1,), (0,)), ((), ())))
            acc = part if acc is None else acc + part
        if has_add:
            acc = acc + add_scale * refs[2 * n][...]
        o_ref[...] = acc.astype(out_dtype)

    in_specs, args = [], []
    for a, w_, _ in pairs:
        in_specs += [pl.BlockSpec((tm, a.shape[1]), lambda i: (i, 0)), pl.BlockSpec(w_.shape, lambda i: (0, 0))]
        args += [a, w_]
    if has_add:
        in_specs.append(pl.BlockSpec((tm, N), lambda i: (i, 0)))
        args.append(add)
    if dep is not None:
        in_specs.append(pl.BlockSpec(memory_space=pl.ANY))
        args.append(dep)
    return pl.pallas_call(
        body,
        out_shape=jax.ShapeDtypeStruct((M, N), out_dtype),
        grid=(M // tm,),
        in_specs=in_specs,
        out_specs=pl.BlockSpec((tm, N), lambda i: (i, 0)),
        compiler_params=_cparams(("parallel",)),
        name=name,
    )(*args)


def _layer_norm_rows(z, g, b):
    mu = jnp.mean(z, axis=-1, keepdims=True)
    zc = z - mu
    var = jnp.mean(zc * zc, axis=-1, keepdims=True)
    return zc * lax.rsqrt(var + LN_EPS) * g + b


def _proj_ln(res, a, w, ln_g, ln_b, *, ple=None, ts=256, name):
    S, D = res.shape
    ka = a.shape[1]
    has_ple = ple is not None
    row = lambda i: (i, 0)
    fix = lambda i: (0, 0)

    def body(*refs):
        if has_ple:
            (res_ref, a_ref, w_ref, g_ref, b_ref, wg_ref, bg_ref, p_ref, wp_ref, z_ref, r_ref, rb_ref, gate_ref,
             proj_ref, acc) = refs
        else:
            res_ref, a_ref, w_ref, g_ref, b_ref, z_ref, r_ref, rb_ref, acc = refs
        acc[...] = _bdot(a_ref[...], w_ref[...])
        if has_ple:
            gate_ref[...] = _bdot(res_ref[...], wg_ref[...])
            proj_ref[...] = _bdot(p_ref[...], wp_ref[...])
        for r0 in range(0, ts, LN_ROWS):
            rows = pl.ds(r0, LN_ROWS)
            z = ALPHA * res_ref[rows, :] + acc[rows, :]
            if has_ple:
                gate = _sigmoid(gate_ref[rows, :] + bg_ref[...])
                gate_ref[rows, :] = gate
                z = z + gate * proj_ref[rows, :]
            z_ref[rows, :] = z
            r = _layer_norm_rows(z, g_ref[...], b_ref[...])
            r_ref[rows, :] = r
            rb_ref[rows, :] = r.astype(BF16)

    in_specs = [pl.BlockSpec((ts, D), row), pl.BlockSpec((ts, ka), row), pl.BlockSpec((ka, D), fix),
                pl.BlockSpec((1, D), fix), pl.BlockSpec((1, D), fix)]
    args = [res, a, w, ln_g.reshape(1, D), ln_b.reshape(1, D)]
    out_dtypes = [F32, F32, BF16]
    if has_ple:
        wg, bg, p, wp = ple
        in_specs += [pl.BlockSpec((D, D), fix), pl.BlockSpec((1, D), fix), pl.BlockSpec((ts, PLE_DIM), row),
                     pl.BlockSpec((PLE_DIM, D), fix)]
        args += [wg, bg.reshape(1, D), p, wp]
        out_dtypes += [F32, F32]
    return pl.pallas_call(
        body,
        out_shape=[jax.ShapeDtypeStruct((S, D), dt) for dt in out_dtypes],
        grid=(S // ts,),
        in_specs=in_specs,
        out_specs=[pl.BlockSpec((ts, D), row)] * len(out_dtypes),
        scratch_shapes=[pltpu.VMEM((ts, D), F32)],
        compiler_params=_cparams(("parallel",)),
        name=name,
    )(*args)


CONV_ROWS = 32
LN_ROWS = 16


def _shifted_copies(src, dst, rows):
    for b in range(1, 8):
        for c0 in range(0, src.shape[1], SUB_LANES):
            ln = pl.ds(c0, SUB_LANES)
            for r0 in range(0, rows, SUB_ROWS):
                rc = min(SUB_ROWS, rows - r0)
                dst[b - 1, pl.ds(r0, rc), ln] = src[pl.ds(r0 + b, rc), ln]


def _rows_at(src, copies, off, n, ln):
    b = off % 8
    return src[pl.ds(off, n), ln] if b == 0 else copies[b - 1, pl.ds(off - b, n), ln]


def _conv31(stg, gsh, cw_ref, cb_ref, out, rows, first_off):
    for c0 in range(0, D_CONV, SUB_LANES):
        ln = pl.ds(c0, SUB_LANES)
        for r0 in range(0, rows, CONV_ROWS):
            acc = jnp.zeros((CONV_ROWS, SUB_LANES), F32) + cb_ref[:, ln]
            for k in range(CONV_KERNEL):
                acc = acc + cw_ref[k:k + 1, ln] * _rows_at(stg, gsh, first_off + k + r0, CONV_ROWS, ln)
            out[pl.ds(r0, CONV_ROWS), ln] = acc


def _mixer_fwd(u, pool_w, pool_scale, conv_w, conv_b, cln_g, cln_b, *, ts=256):
    S = u.shape[0]
    hb = CONV_HALO
    nh = ts // hb

    def body(u_ref, uh_ref, pw_ref, ps_ref, cw_ref, cb_ref, g_ref, b_ref, y_ref, d_ref, sta, stg, gsh, hcs):
        i = pl.program_id(0)
        first = i == 0
        sta[pl.ds(0, hb), :] = jnp.where(first, 0.0, uh_ref[:, 0:D_POOL])
        sta[pl.ds(hb, ts), :] = u_ref[:, 0:D_POOL]
        glu_h = uh_ref[:, D_POOL:D_POOL + D_CONV] * _sigmoid(uh_ref[:, D_POOL + D_CONV:])
        stg[pl.ds(0, hb), :] = jnp.where(first, 0.0, glu_h)
        stg[pl.ds(hb, ts), :] = u_ref[:, D_POOL:D_POOL + D_CONV] * _sigmoid(u_ref[:, D_POOL + D_CONV:])

        pos = (i * ts + lax.broadcasted_iota(jnp.int32, (ts, 1), 0) + 1).astype(F32)
        for g, w in enumerate(POOL_WINDOWS):
            lanes = pl.ds(g * POOL_GROUP, POOL_GROUP)
            a_g = sta[pl.ds(hb, ts), lanes]
            s = a_g
            for j in range(1, w):
                s = s + sta[pl.ds(hb - j, ts), lanes]
            d_g = s / jnp.minimum(pos, float(w)) - a_g
            d_ref[:, lanes] = d_g.astype(BF16)
            y_ref[:, lanes] = (_bdot(d_g, pw_ref[g]) * ps_ref[:, lanes]).astype(BF16)

        _shifted_copies(stg, gsh, hb + ts - 8)
        _conv31(stg, gsh, cw_ref, cb_ref, hcs, ts, hb - (CONV_KERNEL - 1))
        for r0 in range(0, ts, LN_ROWS):
            rows = pl.ds(r0, LN_ROWS)
            ln = _layer_norm_rows(hcs[rows, :], g_ref[...], b_ref[...])
            y_ref[rows, D_POOL:] = (ln * _sigmoid(ln)).astype(BF16)

    fix2 = lambda i: (0, 0)
    return pl.pallas_call(
        body,
        out_shape=[jax.ShapeDtypeStruct((S, D_MODEL), BF16), jax.ShapeDtypeStruct((S, D_POOL), BF16)],
        grid=(S // ts,),
        in_specs=[pl.BlockSpec((ts, 3 * D_POOL), lambda i: (i, 0)),
                  pl.BlockSpec((hb, 3 * D_POOL), lambda i: (jnp.maximum(i * nh - 1, 0), 0)),
                  pl.BlockSpec((4, POOL_GROUP, POOL_GROUP), lambda i: (0, 0, 0)),
                  pl.BlockSpec((1, D_POOL), fix2), pl.BlockSpec((CONV_KERNEL, D_CONV), fix2),
                  pl.BlockSpec((1, D_CONV), fix2), pl.BlockSpec((1, D_CONV), fix2), pl.BlockSpec((1, D_CONV), fix2)],
        out_specs=[pl.BlockSpec((ts, D_MODEL), lambda i: (i, 0)), pl.BlockSpec((ts, D_POOL), lambda i: (i, 0))],
        scratch_shapes=[pltpu.VMEM((hb + ts, D_POOL), F32), pltpu.VMEM((hb + ts, D_CONV), F32),
                        pltpu.VMEM((7, hb + ts - 8, D_CONV), F32), pltpu.VMEM((ts, D_CONV), F32)],
        compiler_params=_cparams(("parallel",)),
        name="mixer_fwd",
    )(u, u, pool_w, pool_scale.reshape(1, D_POOL), conv_w, conv_b.reshape(1, D_CONV), cln_g.reshape(1, D_CONV),
      cln_b.reshape(1, D_CONV))


def _mixer_bwd(u, d, dycat, pool_w, pool_scale, conv_w, conv_b, cln_g, cln_b, *, ts=256):
    S = u.shape[0]
    hb = CONV_HALO
    nh = ts // hb
    n = S // ts
    te = ts + hb
    K = CONV_KERNEL

    def body(u_ref, up_ref, un_ref, d_ref, dy_ref, dyn_ref, pw_ref, ps_ref, cw_ref, cb_ref, g_ref, b_ref,
             du_ref, dpw_ref, dps_ref, dcw_ref, dcb_ref, dg_ref, db_ref, stg, std, sth, gsh, hcs, hsh):
        i = pl.program_id(0)
        first = i == 0
        last = i == n - 1

        @pl.when(first)
        def _():
            dpw_ref[...] = jnp.zeros_like(dpw_ref)
            dps_ref[...] = jnp.zeros_like(dps_ref)
            dcw_ref[...] = jnp.zeros_like(dcw_ref)
            dcb_ref[...] = jnp.zeros_like(dcb_ref)
            dg_ref[...] = jnp.zeros_like(dg_ref)
            db_ref[...] = jnp.zeros_like(db_ref)

        pos_e = (i * ts + lax.broadcasted_iota(jnp.int32, (te, 1), 0) + 1).astype(F32)
        dya = dy_ref[:, 0:D_POOL]
        dya_n = jnp.where(last, 0.0, dyn_ref[:, 0:D_POOL])
        for g, w in enumerate(POOL_WINDOWS):
            lanes = pl.ds(g * POOL_GROUP, POOL_GROUP)
            sl = slice(g * POOL_GROUP, (g + 1) * POOL_GROUP)
            pw = pw_ref[g]
            scale = ps_ref[:, lanes]
            d_g = d_ref[:, lanes]
            pre = _bdot(d_g, pw)
            dps_ref[:, lanes] += jnp.sum(dya[:, sl] * pre, axis=0, keepdims=True)
            dys = dya[:, sl] * scale
            dpw_ref[g] += _bdot(d_g, dys, TN)
            dys_e = jnp.concatenate([dys, dya_n[:, sl] * scale], axis=0)
            dd = _bdot(dys_e, pw, NT)
            std[:, lanes] = dd / jnp.minimum(pos_e, float(w))
            da = -dd[0:ts]
            for m in range(w):
                da = da + std[pl.ds(m, ts), lanes]
            du_ref[:, lanes] = da.astype(BF16)

        glu_p = up_ref[:, D_POOL:D_POOL + D_CONV] * _sigmoid(up_ref[:, D_POOL + D_CONV:])
        stg[pl.ds(0, hb), :] = jnp.where(first, 0.0, glu_p)
        bv = u_ref[:, D_POOL:D_POOL + D_CONV]
        sg = _sigmoid(u_ref[:, D_POOL + D_CONV:])
        stg[pl.ds(hb, ts), :] = bv * sg
        glu_n = un_ref[:, D_POOL:D_POOL + D_CONV] * _sigmoid(un_ref[:, D_POOL + D_CONV:])
        stg[pl.ds(hb + ts, hb), :] = jnp.where(last, 0.0, glu_n)
        _shifted_copies(stg, gsh, hb + te - 8)
        _conv31(stg, gsh, cw_ref, cb_ref, hcs, te, hb - (K - 1))

        sums = [jnp.zeros((8, D_CONV), F32) for _ in range(3)]
        for r0 in range(0, te, LN_ROWS):
            rows = pl.ds(r0, LN_ROWS)
            hc = hcs[rows, :]
            hcc = hc - jnp.mean(hc, axis=-1, keepdims=True)
            rstd = lax.rsqrt(jnp.mean(hcc * hcc, axis=-1, keepdims=True) + LN_EPS)
            xh = hcc * rstd
            ln = xh * g_ref[...] + b_ref[...]
            sl_ = _sigmoid(ln)
            if r0 < ts:
                dyb = dy_ref[rows, D_POOL:]
            else:
                dyb = jnp.where(last, 0.0, dyn_ref[pl.ds(r0 - ts, LN_ROWS), D_POOL:])
            dln = dyb * (sl_ * (1.0 + ln * (1.0 - sl_)))
            dxh = dln * g_ref[...]
            dhc = rstd * (dxh - jnp.mean(dxh, axis=-1, keepdims=True)
                          - xh * jnp.mean(dxh * xh, axis=-1, keepdims=True))
            sth[rows, :] = dhc
            if r0 < ts:
                for n_, term in enumerate((dln * xh, dln, dhc)):
                    sums[n_] = sums[n_] + jnp.sum(term.reshape(LN_ROWS // 8, 8, D_CONV), axis=0)
        dg_ref[...] += jnp.sum(sums[0], axis=0, keepdims=True)
        db_ref[...] += jnp.sum(sums[1], axis=0, keepdims=True)
        dcb_ref[...] += jnp.sum(sums[2], axis=0, keepdims=True)

        _shifted_copies(sth, hsh, te - 8)
        for c0 in range(0, D_CONV, SUB_LANES):
            ln_ = pl.ds(c0, SUB_LANES)
            for r0 in range(0, ts, CONV_ROWS):
                rows = pl.ds(r0, CONV_ROWS)
                dglu = jnp.zeros((CONV_ROWS, SUB_LANES), F32)
                for k in range(K):
                    dglu = dglu + cw_ref[k:k + 1, ln_] * _rows_at(sth, hsh, K - 1 - k + r0, CONV_ROWS, ln_)
                bv = u_ref[rows, pl.ds(D_POOL + c0, SUB_LANES)]
                sg = _sigmoid(u_ref[rows, pl.ds(D_POOL + D_CONV + c0, SUB_LANES)])
                du_ref[rows, pl.ds(D_POOL + c0, SUB_LANES)] = (dglu * sg).astype(BF16)
                du_ref[rows, pl.ds(D_POOL + D_CONV + c0, SUB_LANES)] = (dglu * bv * sg * (1.0 - sg)).astype(BF16)
            for k in range(K):
                tap = jnp.zeros((8, SUB_LANES), F32)
                for r0 in range(0, ts, CONV_ROWS):
                    prod = sth[pl.ds(r0, CONV_ROWS), ln_] * _rows_at(stg, gsh, hb - (K - 1) + k + r0, CONV_ROWS, ln_)
                    tap = tap + jnp.sum(prod.reshape(CONV_ROWS // 8, 8, SUB_LANES), axis=0)
                dcw_ref[k:k + 1, ln_] += jnp.sum(tap, axis=0, keepdims=True)

    fix2 = lambda i: (0, 0)
    prev = lambda i: (jnp.maximum(i * nh - 1, 0), 0)
    nxt = lambda i: (jnp.minimum((i + 1) * nh, S // hb - 1), 0)
    return pl.pallas_call(
        body,
        out_shape=[jax.ShapeDtypeStruct((S, 3 * D_POOL), BF16),
                   jax.ShapeDtypeStruct((4, POOL_GROUP, POOL_GROUP), F32),
                   jax.ShapeDtypeStruct((1, D_POOL), F32),
                   jax.ShapeDtypeStruct((K, D_CONV), F32),
                   jax.ShapeDtypeStruct((1, D_CONV), F32),
                   jax.ShapeDtypeStruct((1, D_CONV), F32),
                   jax.ShapeDtypeStruct((1, D_CONV), F32)],
        grid=(n,),
        in_specs=[pl.BlockSpec((ts, 3 * D_POOL), lambda i: (i, 0)),
                  pl.BlockSpec((hb, 3 * D_POOL), prev),
                  pl.BlockSpec((hb, 3 * D_POOL), nxt),
                  pl.BlockSpec((ts, D_POOL), lambda i: (i, 0)),
                  pl.BlockSpec((ts, D_MODEL), lambda i: (i, 0)),
                  pl.BlockSpec((hb, D_MODEL), nxt),
                  pl.BlockSpec((4, POOL_GROUP, POOL_GROUP), lambda i: (0, 0, 0)),
                  pl.BlockSpec((1, D_POOL), fix2), pl.BlockSpec((K, D_CONV), fix2),
                  pl.BlockSpec((1, D_CONV), fix2), pl.BlockSpec((1, D_CONV), fix2), pl.BlockSpec((1, D_CONV), fix2)],
        out_specs=[pl.BlockSpec((ts, 3 * D_POOL), lambda i: (i, 0)),
                   pl.BlockSpec((4, POOL_GROUP, POOL_GROUP), lambda i: (0, 0, 0)),
                   pl.BlockSpec((1, D_POOL), fix2), pl.BlockSpec((K, D_CONV), fix2),
                   pl.BlockSpec((1, D_CONV), fix2), pl.BlockSpec((1, D_CONV), fix2), pl.BlockSpec((1, D_CONV), fix2)],
        scratch_shapes=[pltpu.VMEM((hb + ts + hb, D_CONV), F32), pltpu.VMEM((te, D_POOL), F32),
                        pltpu.VMEM((te, D_CONV), F32), pltpu.VMEM((7, hb + te - 8, D_CONV), F32),
                        pltpu.VMEM((te, D_CONV), F32), pltpu.VMEM((7, te - 8, D_CONV), F32)],
        compiler_params=_cparams(("arbitrary",)),
        name="mixer_bwd",
    )(u, u, u, d, dycat, dycat, pool_w, pool_scale.reshape(1, D_POOL), conv_w, conv_b.reshape(1, D_CONV),
      cln_g.reshape(1, D_CONV), cln_b.reshape(1, D_CONV))


_GELU_C = math.sqrt(2.0 / math.pi)


def _gelu_parts(x):
    inner = _GELU_C * (x + 0.044715 * x * x * x)
    th = jnp.tanh(inner)
    ge = 0.5 * x * (1.0 + th)
    dge = 0.5 * (1.0 + th) + 0.5 * x * (1.0 - th * th) * (_GELU_C * (1.0 + 3.0 * 0.044715 * x * x))
    return ge, dge


def _ffn_act_fwd(gate, val, dw_w, dw_b, *, ts=256, tc=1408, name):
    S, F = gate.shape
    hb = FFN_HALO
    nh = ts // hb
    tc = _tile(F, tc)

    def body(g_ref, gh_ref, v_ref, w_ref, b_ref, h_ref, st):
        i = pl.program_id(0)
        st[pl.ds(0, hb), :] = jnp.where(i == 0, 0.0, gh_ref[...].astype(F32))
        st[pl.ds(hb, ts), :] = g_ref[...].astype(F32)
        for c0 in range(0, tc, SUB_LANES):
            ln = pl.ds(c0, SUB_LANES)
            w0, w1, w2, b = w_ref[0:1, ln], w_ref[1:2, ln], w_ref[2:3, ln], b_ref[:, ln]
            for r0 in range(0, ts, SUB_ROWS):
                gc = b + w0 * st[pl.ds(hb - 2 + r0, SUB_ROWS), ln] + w1 * st[pl.ds(hb - 1 + r0, SUB_ROWS), ln] \
                    + w2 * st[pl.ds(hb + r0, SUB_ROWS), ln]
                ge, _ = _gelu_parts(gc)
                rows = pl.ds(r0, SUB_ROWS)
                h_ref[rows, ln] = (ge * v_ref[rows, ln].astype(F32)).astype(BF16)

    return pl.pallas_call(
        body,
        out_shape=jax.ShapeDtypeStruct((S, F), BF16),
        grid=(S // ts, F // tc),
        in_specs=[pl.BlockSpec((ts, tc), lambda i, j: (i, j)),
                  pl.BlockSpec((hb, tc), lambda i, j: (jnp.maximum(i * nh - 1, 0), j)),
                  pl.BlockSpec((ts, tc), lambda i, j: (i, j)),
                  pl.BlockSpec((3, tc), lambda i, j: (0, j)),
                  pl.BlockSpec((1, tc), lambda i, j: (0, j))],
        out_specs=pl.BlockSpec((ts, tc), lambda i, j: (i, j)),
        scratch_shapes=[pltpu.VMEM((hb + ts, tc), F32)],
        compiler_params=_cparams(("parallel", "parallel")),
        name=name,
    )(gate, gate, val, dw_w, dw_b.reshape(1, F))


def _ffn_act_bwd(gate, val, dh, dw_w, dw_b, *, ts=256, tc=1408, name):
    S, F = gate.shape
    hb = FFN_HALO
    nh = ts // hb
    n = S // ts
    te = ts + hb
    tc = _tile(F, tc)

    def body(g_ref, gp_ref, gn_ref, v_ref, vn_ref, dh_ref, dhn_ref, w_ref, b_ref,
             dg_ref, dv_ref, dw_ref, db_ref, st, sd):
        i = pl.program_id(1)
        first = i == 0
        last = i == n - 1

        @pl.when(first)
        def _():
            dw_ref[...] = jnp.zeros_like(dw_ref)
            db_ref[...] = jnp.zeros_like(db_ref)

        st[pl.ds(0, hb), :] = jnp.where(first, 0.0, gp_ref[...].astype(F32))
        st[pl.ds(hb, ts), :] = g_ref[...].astype(F32)
        st[pl.ds(hb + ts, hb), :] = jnp.where(last, 0.0, gn_ref[...].astype(F32))
        for c0 in range(0, tc, SUB_LANES):
            ln = pl.ds(c0, SUB_LANES)
            w0, w1, w2, b = w_ref[0:1, ln], w_ref[1:2, ln], w_ref[2:3, ln], b_ref[:, ln]
            db_acc = jnp.zeros((8, SUB_LANES), F32)
            dw_acc = [jnp.zeros((8, SUB_LANES), F32) for _ in range(3)]
            for r0 in range(0, te, SUB_ROWS):
                rc = min(SUB_ROWS, te - r0)
                taps = [st[pl.ds(hb - 2 + k + r0, rc), ln] for k in range(3)]
                gc = b + w0 * taps[0] + w1 * taps[1] + w2 * taps[2]
                ge, dge = _gelu_parts(gc)
                if r0 < ts:
                    rows = pl.ds(r0, rc)
                    val, dh = v_ref[rows, ln].astype(F32), dh_ref[rows, ln].astype(F32)
                else:
                    val = jnp.where(last, 0.0, vn_ref[:, ln].astype(F32)[0:rc])
                    dh = jnp.where(last, 0.0, dhn_ref[:, ln].astype(F32)[0:rc])
                dgc = dh * val * dge
                sd[pl.ds(r0, rc), ln] = dgc
                if r0 < ts:
                    dv_ref[rows, ln] = (dh * ge).astype(BF16)
                    db_acc = db_acc + jnp.sum(dgc.reshape(rc // 8, 8, SUB_LANES), axis=0)
                    for k in range(3):
                        dw_acc[k] = dw_acc[k] + jnp.sum((dgc * taps[k]).reshape(rc // 8, 8, SUB_LANES), axis=0)
            db_ref[:, ln] += jnp.sum(db_acc, axis=0, keepdims=True)
            for k in range(3):
                dw_ref[k:k + 1, ln] += jnp.sum(dw_acc[k], axis=0, keepdims=True)
            for r0 in range(0, ts, SUB_ROWS):
                dgate = w0 * sd[pl.ds(2 + r0, SUB_ROWS), ln] + w1 * sd[pl.ds(1 + r0, SUB_ROWS), ln] \
                    + w2 * sd[pl.ds(r0, SUB_ROWS), ln]
                dg_ref[pl.ds(r0, SUB_ROWS), ln] = dgate.astype(BF16)

    cur = lambda j, i: (i, j)
    prev = lambda j, i: (jnp.maximum(i * nh - 1, 0), j)
    nxt = lambda j, i: (jnp.minimum((i + 1) * nh, S // hb - 1), j)
    return pl.pallas_call(
        body,
        out_shape=[jax.ShapeDtypeStruct((S, F), BF16), jax.ShapeDtypeStruct((S, F), BF16),
                   jax.ShapeDtypeStruct((3, F), F32), jax.ShapeDtypeStruct((1, F), F32)],
        grid=(F // tc, n),
        in_specs=[pl.BlockSpec((ts, tc), cur), pl.BlockSpec((hb, tc), prev), pl.BlockSpec((hb, tc), nxt),
                  pl.BlockSpec((ts, tc), cur), pl.BlockSpec((hb, tc), nxt),
                  pl.BlockSpec((ts, tc), cur), pl.BlockSpec((hb, tc), nxt),
                  pl.BlockSpec((3, tc), lambda j, i: (0, j)), pl.BlockSpec((1, tc), lambda j, i: (0, j))],
        out_specs=[pl.BlockSpec((ts, tc), cur), pl.BlockSpec((ts, tc), cur),
                   pl.BlockSpec((3, tc), lambda j, i: (0, j)), pl.BlockSpec((1, tc), lambda j, i: (0, j))],
        scratch_shapes=[pltpu.VMEM((hb + ts + hb, tc), F32), pltpu.VMEM((te, tc), F32)],
        compiler_params=_cparams(("parallel", "arbitrary")),
        name=name,
    )(gate, gate, gate, val, val, dh, dh, dw_w, dw_b.reshape(1, F))


def _ln_bwd(z, ln_g, ln_b, dout, *, loss_head=False, ts=256, dep=None, name):
    S, D = z.shape

    def body(z_ref, g_ref, b_ref, do_ref, *rest):
        dz_ref, dzb_ref, dg_ref, db_ref, loss_ref = rest[-5:]
        i = pl.program_id(0)

        @pl.when(i == 0)
        def _():
            dg_ref[...] = jnp.zeros_like(dg_ref)
            db_ref[...] = jnp.zeros_like(db_ref)
            loss_ref[...] = jnp.zeros_like(loss_ref)

        dg_acc = jnp.zeros((8, D), F32)
        db_acc = jnp.zeros((8, D), F32)
        loss_acc = jnp.zeros((1, 1), F32)
        for r0 in range(0, ts, LN_ROWS):
            rows = pl.ds(r0, LN_ROWS)
            zt = z_ref[rows, :]
            zc = zt - jnp.mean(zt, axis=-1, keepdims=True)
            rstd = lax.rsqrt(jnp.mean(zc * zc, axis=-1, keepdims=True) + LN_EPS)
            xh = zc * rstd
            if loss_head:
                err = xh * g_ref[...] + b_ref[...] - do_ref[rows, :]
                loss_acc = loss_acc + 0.5 * jnp.sum(jnp.mean(err * err, axis=-1, keepdims=True), keepdims=True)
                do = err * (1.0 / D)
            else:
                do = do_ref[rows, :]
            dg_acc = dg_acc + jnp.sum((do * xh).reshape(LN_ROWS // 8, 8, D), axis=0)
            db_acc = db_acc + jnp.sum(do.reshape(LN_ROWS // 8, 8, D), axis=0)
            dxh = do * g_ref[...]
            dz = rstd * (dxh - jnp.mean(dxh, axis=-1, keepdims=True) - xh * jnp.mean(dxh * xh, axis=-1, keepdims=True))
            dz_ref[rows, :] = dz
            dzb_ref[rows, :] = dz.astype(BF16)
        dg_ref[...] += jnp.sum(dg_acc, axis=0, keepdims=True)
        db_ref[...] += jnp.sum(db_acc, axis=0, keepdims=True)
        if loss_head:
            loss_ref[...] += loss_acc

    row = lambda i: (i, 0)
    fix = lambda i: (0, 0)
    return pl.pallas_call(
        body,
        out_shape=[jax.ShapeDtypeStruct((S, D), F32), jax.ShapeDtypeStruct((S, D), BF16),
                   jax.ShapeDtypeStruct((1, D), F32), jax.ShapeDtypeStruct((1, D), F32),
                   jax.ShapeDtypeStruct((8, 128), F32)],
        grid=(S // ts,),
        in_specs=[pl.BlockSpec((ts, D), row), pl.BlockSpec((1, D), fix), pl.BlockSpec((1, D), fix),
                  pl.BlockSpec((ts, D), row)] + ([pl.BlockSpec(memory_space=pl.ANY)] if dep is not None else []),
        out_specs=[pl.BlockSpec((ts, D), row), pl.BlockSpec((ts, D), row), pl.BlockSpec((1, D), fix),
                   pl.BlockSpec((1, D), fix), pl.BlockSpec((8, 128), fix)],
        compiler_params=_cparams(("arbitrary",)),
        name=name,
    )(z, ln_g.reshape(1, D), ln_b.reshape(1, D), dout, *([dep] if dep is not None else []))


def _ple_bwd(dz, gate, proj, *, ts=256, name):
    S, D = dz.shape

    def body(dz_ref, g_ref, p_ref, ds_ref, dp_ref, db_ref):
        @pl.when(pl.program_id(0) == 0)
        def _():
            db_ref[...] = jnp.zeros_like(db_ref)

        db_acc = jnp.zeros((8, D), F32)
        for r0 in range(0, ts, LN_ROWS):
            rows = pl.ds(r0, LN_ROWS)
            dzt = dz_ref[rows, :]
            g = g_ref[rows, :]
            ds = dzt * p_ref[rows, :] * g * (1.0 - g)
            ds_ref[rows, :] = ds.astype(BF16)
            dp_ref[rows, :] = (dzt * g).astype(BF16)
            db_acc = db_acc + jnp.sum(ds.reshape(LN_ROWS // 8, 8, D), axis=0)
        db_ref[...] += jnp.sum(db_acc, axis=0, keepdims=True)

    row = lambda i: (i, 0)
    return pl.pallas_call(
        body,
        out_shape=[jax.ShapeDtypeStruct((S, D), BF16), jax.ShapeDtypeStruct((S, D), BF16),
                   jax.ShapeDtypeStruct((1, D), F32)],
        grid=(S // ts,),
        in_specs=[pl.BlockSpec((ts, D), row)] * 3,
        out_specs=[pl.BlockSpec((ts, D), row), pl.BlockSpec((ts, D), row), pl.BlockSpec((1, D), lambda i: (0, 0))],
        compiler_params=_cparams(("arbitrary",)),
        name=name,
    )(dz, gate, proj)


HEAD_PAIR = 2 * HEAD_DIM


ATT_ROWS = 32
ATT_SCALE = HEAD_DIM ** -0.5


def _softmax_piece(s_ref, b_ref, j, rows, qb):
    s = s_ref[j, rows, :] + b_ref[j, rows, :]
    kpos = qb * Q_BLOCK + lax.broadcasted_iota(jnp.int32, (1, KV_SPAN), 1)
    s = jnp.where(kpos >= KV_PAD, s, NEG_INF)
    e = jnp.exp(s - jnp.max(s, axis=-1, keepdims=True))
    return e * (1.0 / jnp.sum(e, axis=-1, keepdims=True))


def _pad_keys(qb, k_ref, v_ref, kp, vp):
    @pl.when(qb == 0)
    def _():
        kp[pl.ds(0, KV_PAD), :] = jnp.zeros((KV_PAD, HEAD_PAIR), BF16)
        vp[pl.ds(0, KV_PAD), :] = jnp.zeros((KV_PAD, HEAD_PAIR), BF16)
        kp[pl.ds(KV_PAD, k_ref.shape[0]), :] = k_ref[...]
        vp[pl.ds(KV_PAD, v_ref.shape[0]), :] = v_ref[...]


def _attn_fwd(qkv, bias):
    S = qkv.shape[0]
    nhp = N_HEADS // 2

    def body(q_ref, k_ref, v_ref, b_ref, o_ref, kp, vp, s_scr, p_scr):
        qb = pl.program_id(1)
        _pad_keys(qb, k_ref, v_ref, kp, vp)
        span = pl.ds(pl.multiple_of(qb * Q_BLOCK, Q_BLOCK), KV_SPAN)
        kc, vc = kp[span, :], vp[span, :]
        qt = q_ref[...] * ATT_SCALE
        first = lax.broadcasted_iota(jnp.int32, (1, HEAD_PAIR), 1) < HEAD_DIM
        outs = []
        for j in range(2):
            qj = jnp.where(first if j == 0 else ~first, qt, jnp.zeros_like(qt))
            s_scr[j] = _bdot(qj, kc, NT)
            for r0 in range(0, Q_BLOCK, ATT_ROWS):
                rows = pl.ds(r0, ATT_ROWS)
                p_scr[j, rows, :] = _softmax_piece(s_scr, b_ref, j, rows, qb).astype(BF16)
            outs.append(_bdot(p_scr[j], vc))
        o_ref[...] = jnp.where(first, outs[0], outs[1]).astype(BF16)

    return pl.pallas_call(
        body,
        out_shape=jax.ShapeDtypeStruct((S, D_MODEL), BF16),
        grid=(nhp, S // Q_BLOCK),
        in_specs=[pl.BlockSpec((Q_BLOCK, HEAD_PAIR), lambda h, i: (i, h)),
                  pl.BlockSpec((S, HEAD_PAIR), lambda h, i: (0, nhp + h)),
                  pl.BlockSpec((S, HEAD_PAIR), lambda h, i: (0, 2 * nhp + h)),
                  pl.BlockSpec((2, Q_BLOCK, KV_SPAN), lambda h, i: (h, 0, 0))],
        out_specs=pl.BlockSpec((Q_BLOCK, HEAD_PAIR), lambda h, i: (i, h)),
        scratch_shapes=[pltpu.VMEM((KV_PAD + S, HEAD_PAIR), BF16), pltpu.VMEM((KV_PAD + S, HEAD_PAIR), BF16),
                        pltpu.VMEM((2, Q_BLOCK, KV_SPAN), F32), pltpu.VMEM((2, Q_BLOCK, KV_SPAN), BF16)],
        compiler_params=_cparams(("parallel", "arbitrary")),
        name="attn_fwd",
    )(qkv, qkv, qkv, bias)


def _attn_bwd(qkv, bias, do):
    S = qkv.shape[0]
    nhp = N_HEADS // 2
    nq = S // Q_BLOCK
    scale = HEAD_DIM ** -0.5

    def body(q_ref, k_ref, v_ref, b_ref, do_ref, dq_ref, dk_ref, dv_ref, db_ref, kp, vp, dka, dva,
             s_scr, dp_scr, p_scr, ds_scr):
        qb = pl.program_id(1)
        _pad_keys(qb, k_ref, v_ref, kp, vp)

        @pl.when(qb == 0)
        def _():
            dka[...] = jnp.zeros_like(dka)
            dva[...] = jnp.zeros_like(dva)
            db_ref[...] = jnp.zeros_like(db_ref)

        span = pl.ds(pl.multiple_of(qb * Q_BLOCK, Q_BLOCK), KV_SPAN)
        kc, vc = kp[span, :], vp[span, :]
        qt, dot = q_ref[...] * ATT_SCALE, do_ref[...]
        first = lax.broadcasted_iota(jnp.int32, (1, HEAD_PAIR), 1) < HEAD_DIM
        dqs = []
        for j in range(2):
            mine = first if j == 0 else ~first
            qj = jnp.where(mine, qt, jnp.zeros_like(qt))
            doj = jnp.where(mine, dot, jnp.zeros_like(dot))
            s_scr[j] = _bdot(qj, kc, NT)
            dp_scr[j] = _bdot(doj, vc, NT)
            for r0 in range(0, Q_BLOCK, ATT_ROWS):
                rows = pl.ds(r0, ATT_ROWS)
                p = _softmax_piece(s_scr, b_ref, j, rows, qb)
                dp = dp_scr[j, rows, :]
                ds = p * (dp - jnp.sum(p * dp, axis=-1, keepdims=True))
                db_ref[j, rows, :] += ds
                p_scr[j, rows, :] = p.astype(BF16)
                ds_scr[j, rows, :] = ds.astype(BF16)
            dva[span, :] += _bdot(p_scr[j], doj, TN)
            dqs.append(_bdot(ds_scr[j], kc))
            dka[span, :] += _bdot(ds_scr[j], qj, TN)
        dq_ref[...] = (scale * jnp.where(first, dqs[0], dqs[1])).astype(BF16)

        @pl.when(qb == nq - 1)
        def _():
            dk_ref[...] = dka[pl.ds(KV_PAD, S), :].astype(BF16)
            dv_ref[...] = dva[pl.ds(KV_PAD, S), :].astype(BF16)

    blk = pl.BlockSpec((Q_BLOCK, HEAD_PAIR), lambda h, i: (i, h))
    col = pl.BlockSpec((S, HEAD_PAIR), lambda h, i: (0, h))
    bsp = pl.BlockSpec((2, Q_BLOCK, KV_SPAN), lambda h, i: (h, 0, 0))
    return pl.pallas_call(
        body,
        out_shape=[jax.ShapeDtypeStruct((S, D_MODEL), BF16)] * 3
        + [jax.ShapeDtypeStruct((N_HEADS, Q_BLOCK, KV_SPAN), F32)],
        grid=(nhp, nq),
        in_specs=[blk, pl.BlockSpec((S, HEAD_PAIR), lambda h, i: (0, nhp + h)),
                  pl.BlockSpec((S, HEAD_PAIR), lambda h, i: (0, 2 * nhp + h)), bsp, blk],
        out_specs=[blk, col, col, bsp],
        scratch_shapes=[pltpu.VMEM((KV_PAD + S, HEAD_PAIR), BF16), pltpu.VMEM((KV_PAD + S, HEAD_PAIR), BF16),
                        pltpu.VMEM((KV_PAD + S, HEAD_PAIR), F32), pltpu.VMEM((KV_PAD + S, HEAD_PAIR), F32),
                        pltpu.VMEM((2, Q_BLOCK, KV_SPAN), F32), pltpu.VMEM((2, Q_BLOCK, KV_SPAN), F32),
                        pltpu.VMEM((2, Q_BLOCK, KV_SPAN), BF16), pltpu.VMEM((2, Q_BLOCK, KV_SPAN), BF16)],
        compiler_params=_cparams(("parallel", "arbitrary")),
        name="attn_bwd",
    )(qkv, qkv, qkv, bias, do)


def _bias_blocks(rel_bias):
    H = rel_bias.shape[0]
    n_e = BAND + CHUNK - 1
    n_clip = KV_PAD + CHUNK - 1 - MAX_REL + 1
    e = jnp.concatenate([jnp.broadcast_to(rel_bias[:, 2 * MAX_REL:], (H, n_clip)),
                         jnp.flip(rel_bias[:, 2 * MAX_REL - (n_e - n_clip):2 * MAX_REL], axis=1)], axis=1)
    skew = jnp.pad(jnp.tile(e, (1, CHUNK)), ((0, 0), (0, CHUNK))).reshape(H, CHUNK, n_e + 1)
    band = jnp.flip(skew, axis=1)[:, :, :BAND]
    rows = [jnp.pad(band, ((0, 0), (0, 0), (c * CHUNK, KV_SPAN - BAND - c * CHUNK)), constant_values=NEG_INF)
            for c in range(Q_BLOCK // CHUNK)]
    return jnp.concatenate(rows, axis=1)


def _bias_blocks_grad(dblk):
    H = dblk.shape[0]
    n_e = BAND + CHUNK - 1
    n_clip = KV_PAD + CHUNK - 1 - MAX_REL + 1
    parts = jnp.stack([dblk[:, c * CHUNK:(c + 1) * CHUNK, c * CHUNK:c * CHUNK + BAND]
                       for c in range(Q_BLOCK // CHUNK)], axis=1)
    parts = jnp.flip(parts, axis=2)
    parts = jnp.pad(parts, ((0, 0), (0, 0), (0, 0), (0, n_e + 1 - BAND)))
    skew = parts.reshape(H, Q_BLOCK // CHUNK, CHUNK * (n_e + 1))[:, :, :CHUNK * n_e]
    skew = skew.reshape(H, Q_BLOCK, n_e)
    skew = jnp.pad(skew, ((0, 0), (0, 0), (0, 1)))

    def body(s_ref, o_ref):
        de = jnp.sum(s_ref[...], axis=0, keepdims=True)
        lane = lax.broadcasted_iota(jnp.int32, de.shape, 1)
        far = jnp.sum(jnp.where(lane < n_clip, de, 0.0), axis=-1, keepdims=True)
        o_ref[...] = jnp.where(lane == 0, far, jnp.where(lane < n_clip, 0.0, de))

    de = pl.pallas_call(
        body,
        out_shape=jax.ShapeDtypeStruct((H, 1, n_e + 1), F32),
        grid=(H,),
        in_specs=[pl.BlockSpec((None, Q_BLOCK, n_e + 1), lambda h: (h, 0, 0))],
        out_specs=pl.BlockSpec((None, 1, n_e + 1), lambda h: (h, 0, 0)),
        compiler_params=_cparams(("parallel",)),
        name="bias_grad_sum",
    )(skew).reshape(H, n_e + 1)
    near = jnp.flip(de[:, n_clip:n_e], axis=1)
    return jnp.concatenate([jnp.zeros((H, 2 * MAX_REL - (n_e - n_clip)), F32), near, de[:, 0:1]], axis=1)


def _ffn_forward(r1, r1b, p_l, w, l, ready):
    ready(f"up{l}", r1b)
    up_g = _mm_rows([(r1b, w["ffn_up_g"][l], False)], out_dtype=BF16, name=f"ffn_up_g{l}")
    up_v = _mm_rows([(r1b, w["ffn_up_v"][l], False)], out_dtype=BF16, name=f"ffn_up_v{l}")
    h = _ffn_act_fwd(up_g, up_v, w["ffn_dw_w"][l], w["ffn_dw_b"][l], name=f"ffn_act{l}")
    ready(f"dn{l}", h)
    z2, r2, r2b, gate, proj = _proj_ln(r1, h, w["ffn_w_down"][l], w["ln_ffn_g"][l], w["ln_ffn_b"][l],
                                       ple=(w["ple_w_gate"][l], w["ple_b_gate"][l], p_l, w["ple_w_proj"][l]),
                                       name=f"ffn_down_ln{l}")
    return dict(r1b=r1b, up_g=up_g, up_v=up_v, h=h, z2=z2, gate=gate, proj=proj), r2, r2b


def _ffn_backward(sv, dz2, dz2b, p_l, w, l, grads):
    r1b = sv["r1b"]
    ds, dproj, db_gate = _ple_bwd(dz2, sv["gate"], sv["proj"], name=f"ple_bwd{l}")
    dh = _mm_rows([(dz2b, w["ffn_w_down"][l], True)], out_dtype=BF16, name=f"ffn_dh{l}")
    dgate, dval, d_dw_w, d_dw_b = _ffn_act_bwd(sv["up_g"], sv["up_v"], dh, w["ffn_dw_w"][l], w["ffn_dw_b"][l],
                                               name=f"ffn_act_bwd{l}")
    grads["ffn_w_down"][l] = _wgrad(sv["h"], dz2b, tm=1408, name=f"d_ffn_w_down{l}")
    grads["ffn_up_g"][l] = _wgrad(r1b, dgate, tn=1408, piece=D_FF // 4, name=f"d_ffn_up_g{l}")
    grads["ffn_up_v"][l] = _wgrad(r1b, dval, tn=1408, piece=D_FF // 4, name=f"d_ffn_up_v{l}")
    grads["ple_w_gate"][l] = _wgrad(r1b, ds, name=f"d_ple_w_gate{l}")
    grads["ple_w_proj"][l] = _wgrad(p_l, dproj, piece=D_MODEL // N_DEV, name=f"d_ple_w_proj{l}")
    grads["ffn_dw_w"][l] = d_dw_w
    grads["ffn_dw_b"][l] = d_dw_b[0]
    grads["ple_b_gate"][l] = db_gate[0]
    return _mm_rows([(ds, w["ple_w_gate"][l], True), (dgate, w["ffn_up_g"][l], True), (dval, w["ffn_up_v"][l], True)],
                    add=dz2, add_scale=ALPHA, name=f"dr1_{l}")


def _local_step(x, p, target, w, ready=lambda group, after: None, emit=lambda group, grads: None):
    grads = {k: [None, None] for k in ("ffn_w_down", "ffn_up_g", "ffn_up_v", "ple_w_gate", "ple_w_proj", "ffn_dw_w",
                                       "ffn_dw_b", "ple_b_gate", "ln_ffn_g", "ln_ffn_b", "ln_mix_g", "ln_mix_b")}

    xb, pb = x.astype(BF16), p.astype(BF16)
    ready("mix", None)
    u = _mm_rows([(xb, w["mix_w_in"], False)], name="mix_in")
    ycat, dpool = _mixer_fwd(u, w["pool_w"], w["pool_scale"], w["conv_dw_w"], w["conv_dw_b"], w["conv_ln_g"],
                             w["conv_ln_b"])
    ready("mixo", ycat)
    z1, r1, r1b = _proj_ln(x, ycat, w["mix_w_out"], w["ln_mix_g"][0], w["ln_mix_b"][0], name="mix_out_ln")
    sv0, r2, r2b = _ffn_forward(r1, r1b, pb[0], w, 0, ready)

    ready("attn", r2b)
    qkv = _mm_rows([(r2b, w["attn_w_qkv"], False)], out_dtype=BF16, name="attn_qkv")
    bias = _bias_blocks(w["attn_rel_bias"])
    attn = _attn_fwd(qkv, bias)
    z3, r3, r3b = _proj_ln(r2, attn, w["attn_w_o"], w["ln_mix_g"][1], w["ln_mix_b"][1], name="attn_out_ln")
    sv1, _, _ = _ffn_forward(r3, r3b, pb[1], w, 1, ready)

    dz4, dz4b, grads["ln_ffn_g"][1], grads["ln_ffn_b"][1], loss = _ln_bwd(
        sv1["z2"], w["ln_ffn_g"][1], w["ln_ffn_b"][1], target, loss_head=True, name="loss_ln_bwd")
    dr3 = _ffn_backward(sv1, dz4, dz4b, pb[1], w, 1, grads)
    dz3, dz3b, grads["ln_mix_g"][1], grads["ln_mix_b"][1], _ = _ln_bwd(
        z3, w["ln_mix_g"][1], w["ln_mix_b"][1], dr3, dep=emit("ffn1", grads), name="ln_mix_bwd1")
    grads["attn_w_o"] = _wgrad(attn, dz3b, name="d_attn_w_o")
    dattn = _mm_rows([(dz3b, w["attn_w_o"], True)], out_dtype=BF16, name="d_attn")
    dq, dk, dv, dbias = _attn_bwd(qkv, bias, dattn)
    grads["attn_rel_bias"] = _bias_blocks_grad(dbias)
    dqkv = jnp.concatenate([dq, dk, dv], axis=1)
    grads["attn_w_qkv"] = _wgrad(r2b, dqkv, tn=768, piece=3 * D_MODEL // N_DEV, name="d_attn_w_qkv")
    dr2 = _mm_rows([(dqkv, w["attn_w_qkv"], True)], add=dz3, add_scale=ALPHA, dep=emit("attn", grads), name="dr2")

    dz2, dz2b, grads["ln_ffn_g"][0], grads["ln_ffn_b"][0], _ = _ln_bwd(
        sv0["z2"], w["ln_ffn_g"][0], w["ln_ffn_b"][0], dr2, name="ln_ffn_bwd0")
    dr1 = _ffn_backward(sv0, dz2, dz2b, pb[0], w, 0, grads)
    dz1, dz1b, grads["ln_mix_g"][0], grads["ln_mix_b"][0], _ = _ln_bwd(
        z1, w["ln_mix_g"][0], w["ln_mix_b"][0], dr1, dep=emit("ffn0", grads), name="ln_mix_bwd0")
    grads["mix_w_out"] = _wgrad(ycat, dz1b, name="d_mix_w_out")
    dycat = _mm_rows([(dz1b, w["mix_w_out"], True)], name="d_ycat")
    du, g_pw, g_ps, g_cw, g_cb, g_cg, g_cbb = _mixer_bwd(u, dpool, dycat, w["pool_w"], w["pool_scale"],
                                                         w["conv_dw_w"], w["conv_dw_b"], w["conv_ln_g"],
                                                         w["conv_ln_b"])
    grads["mix_w_in"] = _wgrad(xb, du, tn=768, piece=3 * D_POOL // N_DEV, name="d_mix_w_in")
    grads["conv_dw_w"] = g_cw
    grad_x = _mm_rows([(du, w["mix_w_in"], True)], add=dz1, add_scale=ALPHA, dep=emit("mix", grads), name="grad_x")
    grads.update(pool_w=g_pw, pool_scale=g_ps[0], conv_dw_w=g_cw, conv_dw_b=g_cb[0], conv_ln_g=g_cg[0],
                 conv_ln_b=g_cbb[0])
    for kname in ("ln_ffn_g", "ln_ffn_b", "ln_mix_g", "ln_mix_b"):
        grads[kname] = [a[0] for a in grads[kname]]
    return loss[0, 0], grad_x, grads


_HBM = pl.BlockSpec(memory_space=pltpu.HBM)
_SEM = pl.BlockSpec(memory_space=pltpu.SEMAPHORE)
_EFFECT = pltpu.SideEffectType.DATAFLOW_SIDE_EFFECTING


def _slot(ref, place, shape, k):
    if place in ("stack", "pieces"):
        return ref.at[k]
    ax = place[1]
    n = shape[ax]
    return ref.at[(slice(None),) * ax + (pl.ds(pl.multiple_of(k * n, n), n),)]


def _result_shape(buf, place):
    if place == "stack":
        return (N_DEV,) + buf.shape
    if place == "pieces":
        return buf.shape
    return tuple(s * N_DEV if i == place[1] else s for i, s in enumerate(buf.shape))


def _peers(x, y, c):
    for d in range(1, N_DEV):
        px, py, pc = x ^ ((d >> 2) & 1), y ^ ((d >> 1) & 1), c ^ (d & 1)
        yield d, (px, py, pc), 4 * px + 2 * py + pc


def _exchange_start(bufs, places, after, *, name):
    nb = len(bufs)
    lands = [lax.empty(_result_shape(b, p_), b.dtype) for b, p_ in zip(bufs, places)]
    has_after = after is not None

    def body(*refs):
        srcs, dsts = refs[:nb], refs[nb:2 * nb]
        outs = refs[2 * nb + has_after:]
        send_sems, recv_sems, token = outs[0], outs[1], outs[2 + 2 * nb]
        x, y, c = lax.axis_index("x"), lax.axis_index("y"), lax.axis_index("c")
        me = 4 * x + 2 * y + c
        for b in range(nb):
            for d, dev, peer in _peers(x, y, c):
                pltpu.make_async_remote_copy(
                    src_ref=srcs[b].at[peer] if places[b] == "pieces" else srcs[b],
                    dst_ref=_slot(dsts[b], places[b], bufs[b].shape, me),
                    send_sem=send_sems.at[b * N_DEV + d], recv_sem=recv_sems.at[b * N_DEV + d],
                    device_id=dev, device_id_type=pl.DeviceIdType.MESH).start()
            pltpu.make_async_copy(srcs[b].at[me] if places[b] == "pieces" else srcs[b],
                                  _slot(dsts[b], places[b], bufs[b].shape, me), recv_sems.at[b * N_DEV]).start()
        token[...] = jnp.zeros_like(token)

    sems = pltpu.SemaphoreType.DMA((nb * N_DEV,))
    ins = [pltpu.with_memory_space_constraint(a, pltpu.HBM) for a in list(bufs) + lands]
    out = pl.pallas_call(
        body,
        out_shape=(sems, sems, *[pltpu.HBM(a.shape, a.dtype) for a in ins], jax.ShapeDtypeStruct((8, 128), F32)),
        in_specs=[_HBM] * (2 * nb) + ([pl.BlockSpec(memory_space=pl.ANY)] if has_after else []),
        out_specs=(_SEM, _SEM, *[_HBM] * (2 * nb), pl.BlockSpec(memory_space=pltpu.VMEM)),
        input_output_aliases={i: 2 + i for i in range(2 * nb)},
        compiler_params=pltpu.CompilerParams(has_side_effects=_EFFECT),
        name=name,
    )(*ins, *([after] if has_after else []))
    return dict(send=out[0], recv=out[1], srcs=out[2:2 + nb], lands=out[2 + nb:2 + 2 * nb], token=out[-1],
                places=places)


def _exchange_wait(h, after, *, name):
    nb = len(h["srcs"])
    places = h["places"]
    shapes = [a.shape for a in h["srcs"]]

    def body(*refs):
        srcs, dsts, send_sems, recv_sems = refs[:nb], refs[nb:2 * nb], refs[2 * nb], refs[2 * nb + 1]
        x, y, c = lax.axis_index("x"), lax.axis_index("y"), lax.axis_index("c")
        me = 4 * x + 2 * y + c
        for b in range(nb):
            pieces = places[b] == "pieces"
            for d, dev, peer in _peers(x, y, c):
                cp = pltpu.make_async_remote_copy(
                    src_ref=srcs[b].at[peer] if pieces else srcs[b],
                    dst_ref=_slot(dsts[b], places[b], shapes[b], peer),
                    send_sem=send_sems.at[b * N_DEV + d], recv_sem=recv_sems.at[b * N_DEV + d],
                    device_id=dev, device_id_type=pl.DeviceIdType.MESH)
                cp.wait_send()
                cp.wait_recv()
            pltpu.make_async_copy(srcs[b].at[me] if pieces else srcs[b], _slot(dsts[b], places[b], shapes[b], me),
                                  recv_sems.at[b * N_DEV]).wait()

    ins = list(h["srcs"]) + list(h["lands"])
    out = pl.pallas_call(
        body,
        out_shape=tuple(pltpu.HBM(a.shape, a.dtype) for a in ins),
        in_specs=[_HBM] * (2 * nb) + [_SEM, _SEM, pl.BlockSpec(memory_space=pl.ANY)],
        out_specs=tuple([_HBM] * (2 * nb)),
        input_output_aliases={i: i for i in range(2 * nb)},
        compiler_params=pltpu.CompilerParams(has_side_effects=_EFFECT),
        name=name,
    )(*ins, h["send"], h["recv"], after)
    return out[nb:]


def _adamw(recv, w, m, v, *, name):
    R, C = w.shape
    tr = R
    for cand in (512, 256, 128, 64, 32, 16):
        if R % cand == 0 and cand * C * 4 <= 2 * 1024 * 1024:
            tr = cand
            break
    c1 = 1.0 - ADAM_B1 ** ADAM_STEP
    c2 = 1.0 - ADAM_B2 ** ADAM_STEP

    def body(r_ref, w_ref, m_ref, v_ref, g_ref, d_ref, mo_ref, vo_ref):
        g = r_ref[0].astype(F32)
        for i in range(1, N_DEV):
            g = g + r_ref[i].astype(F32)
        m_new = ADAM_B1 * m_ref[...] + (1.0 - ADAM_B1) * g
        v_new = ADAM_B2 * v_ref[...] + (1.0 - ADAM_B2) * (g * g)
        m_hat = m_new / c1
        v_hat = v_new / c2
        g_ref[...] = g
        d_ref[...] = -ADAM_LR * (m_hat / (jnp.sqrt(v_hat) + ADAM_EPS) + ADAM_WD * w_ref[...])
        mo_ref[...] = m_new
        vo_ref[...] = v_new

    row = pl.BlockSpec((tr, C), lambda i: (i, 0))
    return pl.pallas_call(
        body,
        out_shape=[jax.ShapeDtypeStruct((R, C), F32)] * 4,
        grid=(R // tr,),
        in_specs=[pl.BlockSpec((N_DEV, tr, C), lambda i: (0, i, 0)), row, row, row],
        out_specs=[row] * 4,
        compiler_params=_cparams(("parallel",)),
        name=name,
    )(recv, w, m, v)


def _ffn_groups(l):
    return ((f"up{l}", (("ffn_w_up", l, BF16, "stack"), ("ffn_dw_w", l, F32, "stack"))),
            (f"dn{l}", (("ffn_w_down", l, BF16, ("axis", 0)), ("ple_w_gate", l, BF16, ("axis", 0)),
                        ("ple_w_proj", l, BF16, ("axis", 1)))))


_GATHER_GROUPS = (
    ("mix", (("mix_w_in", 0, BF16, "stack"), ("conv_dw_w", 0, F32, "stack"))),
    ("mixo", (("mix_w_out", 0, BF16, ("axis", 0)),)),
    *_ffn_groups(0),
    ("attn", (("attn_w_qkv", 0, BF16, ("axis", 1)), ("attn_w_o", 0, BF16, ("axis", 0)))),
    *_ffn_groups(1))
_SHARDED = ("mix_w_in", "conv_dw_w", "mix_w_out", "attn_w_qkv", "attn_w_o", "ffn_w_up", "ffn_dw_w", "ffn_w_down",
            "ple_w_gate", "ple_w_proj")
_REPLICATED = ("pool_w", "pool_scale", "conv_dw_b", "conv_ln_g", "conv_ln_b", "attn_rel_bias", "ln_mix_g",
               "ln_mix_b", "ffn_dw_b", "ple_b_gate", "ln_ffn_g", "ln_ffn_b")


def _pack_rows(parts, row_mult, dtype):
    lead = parts[0].shape[:-1]
    flat = jnp.concatenate([a.astype(dtype) for a in parts], axis=-1)
    n = flat.shape[-1]
    unit = row_mult * LANES
    padded = -(-n // unit) * unit
    flat = jnp.pad(flat, [(0, 0)] * len(lead) + [(0, padded - n)])
    return flat.reshape(lead + (padded // LANES, LANES))


def _unpack(flat2d, shapes):
    flat = flat2d.reshape(-1)
    out, o = [], 0
    for s in shapes:
        n = math.prod(s)
        out.append(flat[o:o + n].reshape(s))
        o += n
    return out


def _full_from_shards(g, axis):
    parts = jnp.moveaxis(g, 0, axis)
    shp = list(g.shape[1:])
    shp[axis] *= g.shape[0]
    return parts.reshape(shp)


def _pieces_from_full(full, axis, k=N_DEV):
    shp = list(full.shape)
    n = shp[axis] // k
    t = full.reshape(shp[:axis] + [k, n] + shp[axis + 1:])
    return jnp.moveaxis(t, axis, 0)


def kernel(x, p, mix_w_in, pool_w, pool_scale, conv_dw_w, conv_dw_b, conv_ln_g, conv_ln_b, mix_w_out, attn_w_qkv, attn_rel_bias, attn_w_o, ln_mix_g, ln_mix_b, ffn_w_up, ffn_dw_w, ffn_dw_b, ffn_w_down, ple_w_proj, ple_w_gate, ple_b_gate, ln_ffn_g, ln_ffn_b, loss_target, m_mix_w_in, m_pool_w, m_pool_scale, m_conv_dw_w, m_conv_dw_b, m_conv_ln_g, m_conv_ln_b, m_mix_w_out, m_attn_w_qkv, m_attn_rel_bias, m_attn_w_o, m_ln_mix_g, m_ln_mix_b, m_ffn_w_up, m_ffn_dw_w, m_ffn_dw_b, m_ffn_w_down, m_ple_w_proj, m_ple_w_gate, m_ple_b_gate, m_ln_ffn_g, m_ln_ffn_b, v_mix_w_in, v_pool_w, v_pool_scale, v_conv_dw_w, v_conv_dw_b, v_conv_ln_g, v_conv_ln_b, v_mix_w_out, v_attn_w_qkv, v_attn_rel_bias, v_attn_w_o, v_ln_mix_g, v_ln_mix_b, v_ffn_w_up, v_ffn_dw_w, v_ffn_dw_b, v_ffn_w_down, v_ple_w_proj, v_ple_w_gate, v_ple_b_gate, v_ln_ffn_g, v_ln_ffn_b):
    a = dict(locals())
    sh_names = list(_SHARDED)
    names = sh_names + list(_REPLICATED)
    wts = {n: a[n] for n in names}
    mom = {n: a["m_" + n] for n in names}
    var = {n: a["v_" + n] for n in names}

    gather = {}
    token = None
    for group, items in _GATHER_GROUPS:
        gather[group] = _exchange_start([wts[n][l].astype(dt) for n, l, dt, _ in items], [pl_ for *_, pl_ in items],
                                        token, name="gather_start_" + group)
        token = gather[group]["token"]

    w = dict(pool_w=pool_w[0], pool_scale=pool_scale[0], conv_dw_b=conv_dw_b[0], conv_ln_g=conv_ln_g[0],
             conv_ln_b=conv_ln_b[0], attn_rel_bias=attn_rel_bias[0], ln_mix_g=ln_mix_g, ln_mix_b=ln_mix_b,
             ffn_dw_b=ffn_dw_b, ple_b_gate=ple_b_gate, ln_ffn_g=ln_ffn_g, ln_ffn_b=ln_ffn_b)
    for n in ("ffn_up_g", "ffn_up_v", "ffn_dw_w", "ffn_w_down", "ple_w_gate", "ple_w_proj"):
        w[n] = [None, None]

    def ready(group, after):
        got = _exchange_wait(gather[group], token if after is None else after, name="gather_wait_" + group)
        if group == "mix":
            w["mix_w_in"] = _join_shards(got[0], name="join_mix_w_in")
            w["conv_dw_w"] = _full_from_shards(got[1], 1)
        elif group == "mixo":
            (w["mix_w_out"],) = got
        elif group == "attn":
            w["attn_w_qkv"], w["attn_w_o"] = got
        elif group[:2] == "up":
            l = int(group[2])
            w["ffn_up_g"][l] = _join_shards(got[0], k=N_DEV // 2, part=0, name=f"join_ffn_up_g{l}")
            w["ffn_up_v"][l] = _join_shards(got[0], k=N_DEV // 2, part=1, name=f"join_ffn_up_v{l}")
            w["ffn_dw_w"][l] = _full_from_shards(got[1], 1)
        else:
            l = int(group[2])
            w["ffn_w_down"][l], w["ple_w_gate"][l], w["ple_w_proj"][l] = got

    scatter = {}

    def emit(group, gr):
        if group[:3] == "ffn":
            l = int(group[3])
            pieces = [jnp.concatenate([gr["ffn_up_g"][l], gr["ffn_up_v"][l]]),
                      _pieces_from_full(gr["ffn_dw_w"][l], 1), _pieces_from_full(gr["ffn_w_down"][l], 0),
                      _pieces_from_full(gr["ple_w_gate"][l], 0), gr["ple_w_proj"][l]]
        elif group == "attn":
            pieces = [gr["attn_w_qkv"], _pieces_from_full(gr["attn_w_o"], 0)]
        else:
            pieces = [gr["mix_w_in"], _pieces_from_full(gr["conv_dw_w"], 1), _pieces_from_full(gr["mix_w_out"], 0)]
        scatter[group] = _exchange_start([a.astype(BF16) for a in pieces], ["pieces"] * len(pieces), None,
                                         name="grad_start_" + group)
        return scatter[group]["token"]

    loss_part, grad_x, gr = _local_step(x[0], p[:, 0], loss_target[0], w, ready, emit)
    loss = lax.psum(loss_part, ("x", "y", "c"))

    gfull = dict(
        pool_w=gr["pool_w"][None], pool_scale=gr["pool_scale"][None], conv_dw_b=gr["conv_dw_b"][None],
        conv_ln_g=gr["conv_ln_g"][None], conv_ln_b=gr["conv_ln_b"][None], attn_rel_bias=gr["attn_rel_bias"][None],
        ln_mix_g=jnp.stack(gr["ln_mix_g"]), ln_mix_b=jnp.stack(gr["ln_mix_b"]), ffn_dw_b=jnp.stack(gr["ffn_dw_b"]),
        ple_b_gate=jnp.stack(gr["ple_b_gate"]), ln_ffn_g=jnp.stack(gr["ln_ffn_g"]),
        ln_ffn_b=jnp.stack(gr["ln_ffn_b"]))
    rep_send = _pack_rows([gfull[n].reshape(-1) for n in _REPLICATED], 8, F32)
    rep_handle = _exchange_start([rep_send], ["stack"], None, name="grad_start_replicated")

    group_weights = {"ffn1": (("ffn_w_up", 1), ("ffn_dw_w", 1), ("ffn_w_down", 1), ("ple_w_gate", 1), ("ple_w_proj", 1)),
                     "attn": (("attn_w_qkv", 0), ("attn_w_o", 0)),
                     "ffn0": (("ffn_w_up", 0), ("ffn_dw_w", 0), ("ffn_w_down", 0), ("ple_w_gate", 0), ("ple_w_proj", 0)),
                     "mix": (("mix_w_in", 0), ("conv_dw_w", 0), ("mix_w_out", 0))}
    per_layer = {n: {} for n in sh_names}
    after = grad_x
    for group in ("ffn1", "attn", "ffn0", "mix"):
        recv = _exchange_wait(scatter[group], after, name="grad_wait_" + group)
        for (n, l), r in zip(group_weights[group], recv):
            per_layer[n][l] = _adamw(r, wts[n][l], mom[n][l], var[n][l], name=f"adamw_{n}{l}")
            after = per_layer[n][l][0]
    res = [{}, {}, {}, {}]
    for n in sh_names:
        for k in range(4):
            res[k][n] = jnp.stack([per_layer[n][l][k] for l in sorted(per_layer[n])])
    (rep_recv,) = _exchange_wait(rep_handle, after, name="grad_wait_replicated")

    def flat_state(d):
        return _pack_rows([d[n].reshape(-1) for n in _REPLICATED], 8, F32)

    rep_out = _adamw(rep_recv, flat_state(wts), flat_state(mom), flat_state(var), name="adamw_replicated")
    for k in range(4):
        for n, arr in zip(_REPLICATED, _unpack(rep_out[k], [wts[n].shape for n in _REPLICATED])):
            res[k][n] = arr
    order = ["mix_w_in", "pool_w", "pool_scale", "conv_dw_w", "conv_dw_b", "conv_ln_g", "conv_ln_b", "mix_w_out",
             "attn_w_qkv", "attn_rel_bias", "attn_w_o", "ln_mix_g", "ln_mix_b", "ffn_w_up", "ffn_dw_w", "ffn_dw_b",
             "ffn_w_down", "ple_w_proj", "ple_w_gate", "ple_b_gate", "ln_ffn_g", "ln_ffn_b"]
    outs = [loss, grad_x[None]]
    for k in range(4):
        outs += [res[k][n] for n in order]
    return tuple(outs)
```

```python
import functools
import math

import jax
import jax.numpy as jnp
from jax import lax
from jax.experimental import pallas as pl
from jax.experimental.pallas import tpu as pltpu

F32 = jnp.float32
BF16 = jnp.bfloat16

N_DEV = 8
D_MODEL = 1024
D_POOL = 512
D_CONV = 512
POOL_WINDOWS = (2, 4, 8, 16)
POOL_GROUP = 128
CONV_KERNEL = 31
CHUNK = 64
HEAD_DIM = 64
N_HEADS = 16
LEFT_CHUNKS = 8
BAND = (LEFT_CHUNKS + 1) * CHUNK
MAX_REL = 256
D_FF = 2816
PLE_DIM = 256
ALPHA = 4.0 ** 0.25
LN_EPS = 1e-5
NEG_INF = -1e30
ADAM_LR, ADAM_B1, ADAM_B2, ADAM_EPS, ADAM_WD, ADAM_STEP = 0.001, 0.9, 0.999, 1e-08, 0.01, 10

Q_BLOCK = 4 * CHUNK
KV_PAD = LEFT_CHUNKS * CHUNK
KV_SPAN = KV_PAD + Q_BLOCK
CONV_HALO = 32
FFN_HALO = 16
SUB_ROWS, SUB_LANES = 64, 128
LANES = 1024
VMEM_LIMIT = 56 * 1024 * 1024


def _cparams(sem=None):
    return pltpu.CompilerParams(dimension_semantics=sem, vmem_limit_bytes=VMEM_LIMIT)


def _tile(dim, pref):
    if dim <= pref:
        return dim
    t = pref - pref % 128
    while t >= 128:
        if dim % t == 0:
            return t
        t -= 128
    return dim


def _sigmoid(x):
    return 1.0 / (1.0 + jnp.exp(-x))


def _bdot(a, b, dn=(((1,), (0,)), ((), ()))):
    return lax.dot_general(a.astype(BF16), b.astype(BF16), dn, preferred_element_type=F32)


NT = (((1,), (1,)), ((), ()))
TN = (((0,), (0,)), ((), ()))


def _wgrad(a, b, *, tm=1024, tn=1024, tk=1024, piece=None, name):
    K, M = a.shape
    kb, N = b.shape
    assert K == kb, (a.shape, b.shape)
    tm, tn, tk = _tile(M, tm), _tile(N, tn), _tile(K, tk)
    nk = K // tk
    per = 1 if piece is None else tn // piece
    assert piece is None or tn == per * piece

    def body(a_ref, b_ref, o_ref, acc):
        k = pl.program_id(2)

        @pl.when(k == 0)
        def _():
            acc[...] = jnp.zeros_like(acc)

        acc[...] += _bdot(a_ref[...], b_ref[...], TN)

        @pl.when(k == nk - 1)
        def _():
            if piece is None:
                o_ref[...] = acc[...].astype(BF16)
            else:
                for s in range(per):
                    o_ref[s] = acc[:, s * piece:(s + 1) * piece].astype(BF16)

    if piece is None:
        out_shape, out_spec = (M, N), pl.BlockSpec((tm, tn), lambda i, j, k: (i, j))
    else:
        out_shape, out_spec = (N // piece, M, piece), pl.BlockSpec((per, tm, piece), lambda i, j, k: (j, i, 0))
    return pl.pallas_call(
        body,
        out_shape=jax.ShapeDtypeStruct(out_shape, BF16),
        grid=(M // tm, N // tn, nk),
        in_specs=[pl.BlockSpec((tk, tm), lambda i, j, k: (k, i)), pl.BlockSpec((tk, tn), lambda i, j, k: (k, j))],
        out_specs=out_spec,
        scratch_shapes=[pltpu.VMEM((tm, tn), F32)],
        compiler_params=_cparams(("parallel", "parallel", "arbitrary")),
        name=name,
    )(a, b)


def _join_shards(g, *, k=None, part=0, tm=256, name):
    n, M, wd = g.shape
    k = n if k is None else k

    def body(g_ref, o_ref):
        for j in range(k):
            o_ref[:, j * wd:(j + 1) * wd] = g_ref[j]

    return pl.pallas_call(
        body,
        out_shape=jax.ShapeDtypeStruct((M, k * wd), g.dtype),
        grid=(M // tm,),
        in_specs=[pl.BlockSpec((k, tm, wd), lambda i: (part, i, 0))],
        out_specs=pl.BlockSpec((tm, k * wd), lambda i: (i, 0)),
        compiler_params=_cparams(("parallel",)),
        name=name,
    )(g)


def _mm_rows(pairs, *, add=None, add_scale=1.0, out_dtype=F32, tm=256, dep=None, name):
    M = pairs[0][0].shape[0]
    N = pairs[0][1].shape[0] if pairs[0][2] else pairs[0][1].shape[1]
    n = len(pairs)
    has_add = add is not None

    def body(*refs):
        o_ref = refs[-1]
        acc = None
        for i, (_, _, tr) in enumerate(pairs):
            part = _bdot(refs[2 * i][...], refs[2 * i + 1][...], NT if tr else (((1,), (0,)), ((), ())))
            acc = part if acc is None else acc + part
        if has_add:
            acc = acc + add_scale * refs[2 * n][...]
        o_ref[...] = acc.astype(out_dtype)

    in_specs, args = [], []
    for a, w_, _ in pairs:
        in_specs += [pl.BlockSpec((tm, a.shape[1]), lambda i: (i, 0)), pl.BlockSpec(w_.shape, lambda i: (0, 0))]
        args += [a, w_]
    if has_add:
        in_specs.append(pl.BlockSpec((tm, N), lambda i: (i, 0)))
        args.append(add)
    if dep is not None:
        in_specs.append(pl.BlockSpec(memory_space=pl.ANY))
        args.append(dep)
    return pl.pallas_call(
        body,
        out_shape=jax.ShapeDtypeStruct((M, N), out_dtype),
        grid=(M // tm,),
        in_specs=in_specs,
        out_specs=pl.BlockSpec((tm, N), lambda i: (i, 0)),
        compiler_params=_cparams(("parallel",)),
        name=name,
    )(*args)


def _layer_norm_rows(z, g, b):
    mu = jnp.mean(z, axis=-1, keepdims=True)
    zc = z - mu
    var = jnp.mean(zc * zc, axis=-1, keepdims=True)
    return zc * lax.rsqrt(var + LN_EPS) * g + b


def _proj_ln(res, a, w, ln_g, ln_b, *, ple=None, ts=256, name):
    S, D = res.shape
    ka = a.shape[1]
    has_ple = ple is not None
    row = lambda i: (i, 0)
    fix = lambda i: (0, 0)

    def body(*refs):
        if has_ple:
            (res_ref, a_ref, w_ref, g_ref, b_ref, wg_ref, bg_ref, p_ref, wp_ref, z_ref, r_ref, rb_ref, gate_ref,
             proj_ref, acc) = refs
        else:
            res_ref, a_ref, w_ref, g_ref, b_ref, z_ref, r_ref, rb_ref, acc = refs
        acc[...] = _bdot(a_ref[...], w_ref[...])
        if has_ple:
            gate_ref[...] = _bdot(res_ref[...], wg_ref[...])
            proj_ref[...] = _bdot(p_ref[...], wp_ref[...])
        for r0 in range(0, ts, LN_ROWS):
            rows = pl.ds(r0, LN_ROWS)
            z = ALPHA * res_ref[rows, :] + acc[rows, :]
            if has_ple:
                gate = _sigmoid(gate_ref[rows, :] + bg_ref[...])
                gate_ref[rows, :] = gate
                z = z + gate * proj_ref[rows, :]
            z_ref[rows, :] = z
            r = _layer_norm_rows(z, g_ref[...], b_ref[...])
            r_ref[rows, :] = r
            rb_ref[rows, :] = r.astype(BF16)

    in_specs = [pl.BlockSpec((ts, D), row), pl.BlockSpec((ts, ka), row), pl.BlockSpec((ka, D), fix),
                pl.BlockSpec((1, D), fix), pl.BlockSpec((1, D), fix)]
    args = [res, a, w, ln_g.reshape(1, D), ln_b.reshape(1, D)]
    out_dtypes = [F32, F32, BF16]
    if has_ple:
        wg, bg, p, wp = ple
        in_specs += [pl.BlockSpec((D, D), fix), pl.BlockSpec((1, D), fix), pl.BlockSpec((ts, PLE_DIM), row),
                     pl.BlockSpec((PLE_DIM, D), fix)]
        args += [wg, bg.reshape(1, D), p, wp]
        out_dtypes += [F32, F32]
    return pl.pallas_call(
        body,
        out_shape=[jax.ShapeDtypeStruct((S, D), dt) for dt in out_dtypes],
        grid=(S // ts,),
        in_specs=in_specs,
        out_specs=[pl.BlockSpec((ts, D), row)] * len(out_dtypes),
        scratch_shapes=[pltpu.VMEM((ts, D), F32)],
        compiler_params=_cparams(("parallel",)),
        name=name,
    )(*args)


CONV_ROWS = 32
LN_ROWS = 16


def _shifted_copies(src, dst, rows):
    for b in range(1, 8):
        for c0 in range(0, src.shape[1], SUB_LANES):
            ln = pl.ds(c0, SUB_LANES)
            for r0 in range(0, rows, SUB_ROWS):
                rc = min(SUB_ROWS, rows - r0)
                dst[b - 1, pl.ds(r0, rc), ln] = src[pl.ds(r0 + b, rc), ln]


def _rows_at(src, copies, off, n, ln):
    b = off % 8
    return src[pl.ds(off, n), ln] if b == 0 else copies[b - 1, pl.ds(off - b, n), ln]


def _conv31(stg, gsh, cw_ref, cb_ref, out, rows, first_off):
    for c0 in range(0, D_CONV, SUB_LANES):
        ln = pl.ds(c0, SUB_LANES)
        for r0 in range(0, rows, CONV_ROWS):
            acc = jnp.zeros((CONV_ROWS, SUB_LANES), F32) + cb_ref[:, ln]
            for k in range(CONV_KERNEL):
                acc = acc + cw_ref[k:k + 1, ln] * _rows_at(stg, gsh, first_off + k + r0, CONV_ROWS, ln)
            out[pl.ds(r0, CONV_ROWS), ln] = acc


def _mixer_fwd(u, pool_w, pool_scale, conv_w, conv_b, cln_g, cln_b, *, ts=256):
    S = u.shape[0]
    hb = CONV_HALO
    nh = ts // hb

    def body(u_ref, uh_ref, pw_ref, ps_ref, cw_ref, cb_ref, g_ref, b_ref, y_ref, d_ref, sta, stg, gsh, hcs):
        i = pl.program_id(0)
        first = i == 0
        sta[pl.ds(0, hb), :] = jnp.where(first, 0.0, uh_ref[:, 0:D_POOL])
        sta[pl.ds(hb, ts), :] = u_ref[:, 0:D_POOL]
        glu_h = uh_ref[:, D_POOL:D_POOL + D_CONV] * _sigmoid(uh_ref[:, D_POOL + D_CONV:])
        stg[pl.ds(0, hb), :] = jnp.where(first, 0.0, glu_h)
        stg[pl.ds(hb, ts), :] = u_ref[:, D_POOL:D_POOL + D_CONV] * _sigmoid(u_ref[:, D_POOL + D_CONV:])

        pos = (i * ts + lax.broadcasted_iota(jnp.int32, (ts, 1), 0) + 1).astype(F32)
        for g, w in enumerate(POOL_WINDOWS):
            lanes = pl.ds(g * POOL_GROUP, POOL_GROUP)
            a_g = sta[pl.ds(hb, ts), lanes]
            s = a_g
            for j in range(1, w):
                s = s + sta[pl.ds(hb - j, ts), lanes]
            d_g = s / jnp.minimum(pos, float(w)) - a_g
            d_ref[:, lanes] = d_g.astype(BF16)
            y_ref[:, lanes] = (_bdot(d_g, pw_ref[g]) * ps_ref[:, lanes]).astype(BF16)

        _shifted_copies(stg, gsh, hb + ts - 8)
        _conv31(stg, gsh, cw_ref, cb_ref, hcs, ts, hb - (CONV_KERNEL - 1))
        for r0 in range(0, ts, LN_ROWS):
            rows = pl.ds(r0, LN_ROWS)
            ln = _layer_norm_rows(hcs[rows, :], g_ref[...], b_ref[...])
            y_ref[rows, D_POOL:] = (ln * _sigmoid(ln)).astype(BF16)

    fix2 = lambda i: (0, 0)
    return pl.pallas_call(
        body,
        out_shape=[jax.ShapeDtypeStruct((S, D_MODEL), BF16), jax.ShapeDtypeStruct((S, D_POOL), BF16)],
        grid=(S // ts,),
        in_specs=[pl.BlockSpec((ts, 3 * D_POOL), lambda i: (i, 0)),
                  pl.BlockSpec((hb, 3 * D_POOL), lambda i: (jnp.maximum(i * nh - 1, 0), 0)),
                  pl.BlockSpec((4, POOL_GROUP, POOL_GROUP), lambda i: (0, 0, 0)),
                  pl.BlockSpec((1, D_POOL), fix2), pl.BlockSpec((CONV_KERNEL, D_CONV), fix2),
                  pl.BlockSpec((1, D_CONV), fix2), pl.BlockSpec((1, D_CONV), fix2), pl.BlockSpec((1, D_CONV), fix2)],
        out_specs=[pl.BlockSpec((ts, D_MODEL), lambda i: (i, 0)), pl.BlockSpec((ts, D_POOL), lambda i: (i, 0))],
        scratch_shapes=[pltpu.VMEM((hb + ts, D_POOL), F32), pltpu.VMEM((hb + ts, D_CONV), F32),
                        pltpu.VMEM((7, hb + ts - 8, D_CONV), F32), pltpu.VMEM((ts, D_CONV), F32)],
        compiler_params=_cparams(("parallel",)),
        name="mixer_fwd",
    )(u, u, pool_w, pool_scale.reshape(1, D_POOL), conv_w, conv_b.reshape(1, D_CONV), cln_g.reshape(1, D_CONV),
      cln_b.reshape(1, D_CONV))


def _mixer_bwd(u, d, dycat, pool_w, pool_scale, conv_w, conv_b, cln_g, cln_b, *, ts=256):
    S = u.shape[0]
    hb = CONV_HALO
    nh = ts // hb
    n = S // ts
    te = ts + hb
    K = CONV_KERNEL

    def body(u_ref, up_ref, un_ref, d_ref, dy_ref, dyn_ref, pw_ref, ps_ref, cw_ref, cb_ref, g_ref, b_ref,
             du_ref, dpw_ref, dps_ref, dcw_ref, dcb_ref, dg_ref, db_ref, stg, std, sth, gsh, hcs, hsh):
        i = pl.program_id(0)
        first = i == 0
        last = i == n - 1

        @pl.when(first)
        def _():
            dpw_ref[...] = jnp.zeros_like(dpw_ref)
            dps_ref[...] = jnp.zeros_like(dps_ref)
            dcw_ref[...] = jnp.zeros_like(dcw_ref)
            dcb_ref[...] = jnp.zeros_like(dcb_ref)
            dg_ref[...] = jnp.zeros_like(dg_ref)
            db_ref[...] = jnp.zeros_like(db_ref)

        pos_e = (i * ts + lax.broadcasted_iota(jnp.int32, (te, 1), 0) + 1).astype(F32)
        dya = dy_ref[:, 0:D_POOL]
        dya_n = jnp.where(last, 0.0, dyn_ref[:, 0:D_POOL])
        for g, w in enumerate(POOL_WINDOWS):
            lanes = pl.ds(g * POOL_GROUP, POOL_GROUP)
            sl = slice(g * POOL_GROUP, (g + 1) * POOL_GROUP)
            pw = pw_ref[g]
            scale = ps_ref[:, lanes]
            d_g = d_ref[:, lanes]
            pre = _bdot(d_g, pw)
            dps_ref[:, lanes] += jnp.sum(dya[:, sl] * pre, axis=0, keepdims=True)
            dys = dya[:, sl] * scale
            dpw_ref[g] += _bdot(d_g, dys, TN)
            dys_e = jnp.concatenate([dys, dya_n[:, sl] * scale], axis=0)
            dd = _bdot(dys_e, pw, NT)
            std[:, lanes] = dd / jnp.minimum(pos_e, float(w))
            da = -dd[0:ts]
            for m in range(w):
                da = da + std[pl.ds(m, ts), lanes]
            du_ref[:, lanes] = da.astype(BF16)

        glu_p = up_ref[:, D_POOL:D_POOL + D_CONV] * _sigmoid(up_ref[:, D_POOL + D_CONV:])
        stg[pl.ds(0, hb), :] = jnp.where(first, 0.0, glu_p)
        bv = u_ref[:, D_POOL:D_POOL + D_CONV]
        sg = _sigmoid(u_ref[:, D_POOL + D_CONV:])
        stg[pl.ds(hb, ts), :] = bv * sg
        glu_n = un_ref[:, D_POOL:D_POOL + D_CONV] * _sigmoid(un_ref[:, D_POOL + D_CONV:])
        stg[pl.ds(hb + ts, hb), :] = jnp.where(last, 0.0, glu_n)
        _shifted_copies(stg, gsh, hb + te - 8)
        _conv31(stg, gsh, cw_ref, cb_ref, hcs, te, hb - (K - 1))

        sums = [jnp.zeros((8, D_CONV), F32) for _ in range(3)]
        for r0 in range(0, te, LN_ROWS):
            rows = pl.ds(r0, LN_ROWS)
            hc = hcs[rows, :]
            hcc = hc - jnp.mean(hc, axis=-1, keepdims=True)
            rstd = lax.rsqrt(jnp.mean(hcc * hcc, axis=-1, keepdims=True) + LN_EPS)
            xh = hcc * rstd
            ln = xh * g_ref[...] + b_ref[...]
            sl_ = _sigmoid(ln)
            if r0 < ts:
                dyb = dy_ref[rows, D_POOL:]
            else:
                dyb = jnp.where(last, 0.0, dyn_ref[pl.ds(r0 - ts, LN_ROWS), D_POOL:])
            dln = dyb * (sl_ * (1.0 + ln * (1.0 - sl_)))
            dxh = dln * g_ref[...]
            dhc = rstd * (dxh - jnp.mean(dxh, axis=-1, keepdims=True)
                          - xh * jnp.mean(dxh * xh, axis=-1, keepdims=True))
            sth[rows, :] = dhc
            if r0 < ts:
                for n_, term in enumerate((dln * xh, dln, dhc)):
                    sums[n_] = sums[n_] + jnp.sum(term.reshape(LN_ROWS // 8, 8, D_CONV), axis=0)
        dg_ref[...] += jnp.sum(sums[0], axis=0, keepdims=True)
        db_ref[...] += jnp.sum(sums[1], axis=0, keepdims=True)
        dcb_ref[...] += jnp.sum(sums[2], axis=0, keepdims=True)

        _shifted_copies(sth, hsh, te - 8)
        for c0 in range(0, D_CONV, SUB_LANES):
            ln_ = pl.ds(c0, SUB_LANES)
            for r0 in range(0, ts, CONV_ROWS):
                rows = pl.ds(r0, CONV_ROWS)
                dglu = jnp.zeros((CONV_ROWS, SUB_LANES), F32)
                for k in range(K):
                    dglu = dglu + cw_ref[k:k + 1, ln_] * _rows_at(sth, hsh, K - 1 - k + r0, CONV_ROWS, ln_)
                bv = u_ref[rows, pl.ds(D_POOL + c0, SUB_LANES)]
                sg = _sigmoid(u_ref[rows, pl.ds(D_POOL + D_CONV + c0, SUB_LANES)])
                du_ref[rows, pl.ds(D_POOL + c0, SUB_LANES)] = (dglu * sg).astype(BF16)
                du_ref[rows, pl.ds(D_POOL + D_CONV + c0, SUB_LANES)] = (dglu * bv * sg * (1.0 - sg)).astype(BF16)
            for k in range(K):
                tap = jnp.zeros((8, SUB_LANES), F32)
                for r0 in range(0, ts, CONV_ROWS):
                    prod = sth[pl.ds(r0, CONV_ROWS), ln_] * _rows_at(stg, gsh, hb - (K - 1) + k + r0, CONV_ROWS, ln_)
                    tap = tap + jnp.sum(prod.reshape(CONV_ROWS // 8, 8, SUB_LANES), axis=0)
                dcw_ref[k:k + 1, ln_] += jnp.sum(tap, axis=0, keepdims=True)

    fix2 = lambda i: (0, 0)
    prev = lambda i: (jnp.maximum(i * nh - 1, 0), 0)
    nxt = lambda i: (jnp.minimum((i + 1) * nh, S // hb - 1), 0)
    return pl.pallas_call(
        body,
        out_shape=[jax.ShapeDtypeStruct((S, 3 * D_POOL), BF16),
                   jax.ShapeDtypeStruct((4, POOL_GROUP, POOL_GROUP), F32),
                   jax.ShapeDtypeStruct((1, D_POOL), F32),
                   jax.ShapeDtypeStruct((K, D_CONV), F32),
                   jax.ShapeDtypeStruct((1, D_CONV), F32),
                   jax.ShapeDtypeStruct((1, D_CONV), F32),
                   jax.ShapeDtypeStruct((1, D_CONV), F32)],
        grid=(n,),
        in_specs=[pl.BlockSpec((ts, 3 * D_POOL), lambda i: (i, 0)),
                  pl.BlockSpec((hb, 3 * D_POOL), prev),
                  pl.BlockSpec((hb, 3 * D_POOL), nxt),
                  pl.BlockSpec((ts, D_POOL), lambda i: (i, 0)),
                  pl.BlockSpec((ts, D_MODEL), lambda i: (i, 0)),
                  pl.BlockSpec((hb, D_MODEL), nxt),
                  pl.BlockSpec((4, POOL_GROUP, POOL_GROUP), lambda i: (0, 0, 0)),
                  pl.BlockSpec((1, D_POOL), fix2), pl.BlockSpec((K, D_CONV), fix2),
                  pl.BlockSpec((1, D_CONV), fix2), pl.BlockSpec((1, D_CONV), fix2), pl.BlockSpec((1, D_CONV), fix2)],
        out_specs=[pl.BlockSpec((ts, 3 * D_POOL), lambda i: (i, 0)),
                   pl.BlockSpec((4, POOL_GROUP, POOL_GROUP), lambda i: (0, 0, 0)),
                   pl.BlockSpec((1, D_POOL), fix2), pl.BlockSpec((K, D_CONV), fix2),
                   pl.BlockSpec((1, D_CONV), fix2), pl.BlockSpec((1, D_CONV), fix2), pl.BlockSpec((1, D_CONV), fix2)],
        scratch_shapes=[pltpu.VMEM((hb + ts + hb, D_CONV), F32), pltpu.VMEM((te, D_POOL), F32),
                        pltpu.VMEM((te, D_CONV), F32), pltpu.VMEM((7, hb + te - 8, D_CONV), F32),
                        pltpu.VMEM((te, D_CONV), F32), pltpu.VMEM((7, te - 8, D_CONV), F32)],
        compiler_params=_cparams(("arbitrary",)),
        name="mixer_bwd",
    )(u, u, u, d, dycat, dycat, pool_w, pool_scale.reshape(1, D_POOL), conv_w, conv_b.reshape(1, D_CONV),
      cln_g.reshape(1, D_CONV), cln_b.reshape(1, D_CONV))


_GELU_C = math.sqrt(2.0 / math.pi)


def _gelu_parts(x):
    inner = _GELU_C * (x + 0.044715 * x * x * x)
    th = jnp.tanh(inner)
    ge = 0.5 * x * (1.0 + th)
    dge = 0.5 * (1.0 + th) + 0.5 * x * (1.0 - th * th) * (_GELU_C * (1.0 + 3.0 * 0.044715 * x * x))
    return ge, dge


def _ffn_act_fwd(gate, val, dw_w, dw_b, *, ts=256, tc=1408, name):
    S, F = gate.shape
    hb = FFN_HALO
    nh = ts // hb
    tc = _tile(F, tc)

    def body(g_ref, gh_ref, v_ref, w_ref, b_ref, h_ref, st):
        i = pl.program_id(0)
        st[pl.ds(0, hb), :] = jnp.where(i == 0, 0.0, gh_ref[...].astype(F32))
        st[pl.ds(hb, ts), :] = g_ref[...].astype(F32)
        for c0 in range(0, tc, SUB_LANES):
            ln = pl.ds(c0, SUB_LANES)
            w0, w1, w2, b = w_ref[0:1, ln], w_ref[1:2, ln], w_ref[2:3, ln], b_ref[:, ln]
            for r0 in range(0, ts, SUB_ROWS):
                gc = b + w0 * st[pl.ds(hb - 2 + r0, SUB_ROWS), ln] + w1 * st[pl.ds(hb - 1 + r0, SUB_ROWS), ln] \
                    + w2 * st[pl.ds(hb + r0, SUB_ROWS), ln]
                ge, _ = _gelu_parts(gc)
                rows = pl.ds(r0, SUB_ROWS)
                h_ref[rows, ln] = (ge * v_ref[rows, ln].astype(F32)).astype(BF16)

    return pl.pallas_call(
        body,
        out_shape=jax.ShapeDtypeStruct((S, F), BF16),
        grid=(S // ts, F // tc),
        in_specs=[pl.BlockSpec((ts, tc), lambda i, j: (i, j)),
                  pl.BlockSpec((hb, tc), lambda i, j: (jnp.maximum(i * nh - 1, 0), j)),
                  pl.BlockSpec((ts, tc), lambda i, j: (i, j)),
                  pl.BlockSpec((3, tc), lambda i, j: (0, j)),
                  pl.BlockSpec((1, tc), lambda i, j: (0, j))],
        out_specs=pl.BlockSpec((ts, tc), lambda i, j: (i, j)),
        scratch_shapes=[pltpu.VMEM((hb + ts, tc), F32)],
        compiler_params=_cparams(("parallel", "parallel")),
        name=name,
    )(gate, gate, val, dw_w, dw_b.reshape(1, F))


def _ffn_act_bwd(gate, val, dh, dw_w, dw_b, *, ts=256, tc=1408, name):
    S, F = gate.shape
    hb = FFN_HALO
    nh = ts // hb
    n = S // ts
    te = ts + hb
    tc = _tile(F, tc)

    def body(g_ref, gp_ref, gn_ref, v_ref, vn_ref, dh_ref, dhn_ref, w_ref, b_ref,
             dg_ref, dv_ref, dw_ref, db_ref, st, sd):
        i = pl.program_id(1)
        first = i == 0
        last = i == n - 1

        @pl.when(first)
        def _():
            dw_ref[...] = jnp.zeros_like(dw_ref)
            db_ref[...] = jnp.zeros_like(db_ref)

        st[pl.ds(0, hb), :] = jnp.where(first, 0.0, gp_ref[...].astype(F32))
        st[pl.ds(hb, ts), :] = g_ref[...].astype(F32)
        st[pl.ds(hb + ts, hb), :] = jnp.where(last, 0.0, gn_ref[...].astype(F32))
        for c0 in range(0, tc, SUB_LANES):
            ln = pl.ds(c0, SUB_LANES)
            w0, w1, w2, b = w_ref[0:1, ln], w_ref[1:2, ln], w_ref[2:3, ln], b_ref[:, ln]
            db_acc = jnp.zeros((8, SUB_LANES), F32)
            dw_acc = [jnp.zeros((8, SUB_LANES), F32) for _ in range(3)]
            for r0 in range(0, te, SUB_ROWS):
                rc = min(SUB_ROWS, te - r0)
                taps = [st[pl.ds(hb - 2 + k + r0, rc), ln] for k in range(3)]
                gc = b + w0 * taps[0] + w1 * taps[1] + w2 * taps[2]
                ge, dge = _gelu_parts(gc)
                if r0 < ts:
                    rows = pl.ds(r0, rc)
                    val, dh = v_ref[rows, ln].astype(F32), dh_ref[rows, ln].astype(F32)
                else:
                    val = jnp.where(last, 0.0, vn_ref[:, ln].astype(F32)[0:rc])
                    dh = jnp.where(last, 0.0, dhn_ref[:, ln].astype(F32)[0:rc])
                dgc = dh * val * dge
                sd[pl.ds(r0, rc), ln] = dgc
                if r0 < ts:
                    dv_ref[rows, ln] = (dh * ge).astype(BF16)
                    db_acc = db_acc + jnp.sum(dgc.reshape(rc // 8, 8, SUB_LANES), axis=0)
                    for k in range(3):
                        dw_acc[k] = dw_acc[k] + jnp.sum((dgc * taps[k]).reshape(rc // 8, 8, SUB_LANES), axis=0)
            db_ref[:, ln] += jnp.sum(db_acc, axis=0, keepdims=True)
            for k in range(3):
                dw_ref[k:k + 1, ln] += jnp.sum(dw_acc[k], axis=0, keepdims=True)
            for r0 in range(0, ts, SUB_ROWS):
                dgate = w0 * sd[pl.ds(2 + r0, SUB_ROWS), ln] + w1 * sd[pl.ds(1 + r0, SUB_ROWS), ln] \
                    + w2 * sd[pl.ds(r0, SUB_ROWS), ln]
                dg_ref[pl.ds(r0, SUB_ROWS), ln] = dgate.astype(BF16)

    cur = lambda j, i: (i, j)
    prev = lambda j, i: (jnp.maximum(i * nh - 1, 0), j)
    nxt = lambda j, i: (jnp.minimum((i + 1) * nh, S // hb - 1), j)
    return pl.pallas_call(
        body,
        out_shape=[jax.ShapeDtypeStruct((S, F), BF16), jax.ShapeDtypeStruct((S, F), BF16),
                   jax.ShapeDtypeStruct((3, F), F32), jax.ShapeDtypeStruct((1, F), F32)],
        grid=(F // tc, n),
        in_specs=[pl.BlockSpec((ts, tc), cur), pl.BlockSpec((hb, tc), prev), pl.BlockSpec((hb, tc), nxt),
                  pl.BlockSpec((ts, tc), cur), pl.BlockSpec((hb, tc), nxt),
                  pl.BlockSpec((ts, tc), cur), pl.BlockSpec((hb, tc), nxt),
                  pl.BlockSpec((3, tc), lambda j, i: (0, j)), pl.BlockSpec((1, tc), lambda j, i: (0, j))],
        out_specs=[pl.BlockSpec((ts, tc), cur), pl.BlockSpec((ts, tc), cur),
                   pl.BlockSpec((3, tc), lambda j, i: (0, j)), pl.BlockSpec((1, tc), lambda j, i: (0, j))],
        scratch_shapes=[pltpu.VMEM((hb + ts + hb, tc), F32), pltpu.VMEM((te, tc), F32)],
        compiler_params=_cparams(("parallel", "arbitrary")),
        name=name,
    )(gate, gate, gate, val, val, dh, dh, dw_w, dw_b.reshape(1, F))


def _ln_bwd(z, ln_g, ln_b, dout, *, loss_head=False, ts=256, dep=None, name):
    S, D = z.shape

    def body(z_ref, g_ref, b_ref, do_ref, *rest):
        dz_ref, dzb_ref, dg_ref, db_ref, loss_ref = rest[-5:]
        i = pl.program_id(0)

        @pl.when(i == 0)
        def _():
            dg_ref[...] = jnp.zeros_like(dg_ref)
            db_ref[...] = jnp.zeros_like(db_ref)
            loss_ref[...] = jnp.zeros_like(loss_ref)

        dg_acc = jnp.zeros((8, D), F32)
        db_acc = jnp.zeros((8, D), F32)
        loss_acc = jnp.zeros((1, 1), F32)
        for r0 in range(0, ts, LN_ROWS):
            rows = pl.ds(r0, LN_ROWS)
            zt = z_ref[rows, :]
            zc = zt - jnp.mean(zt, axis=-1, keepdims=True)
            rstd = lax.rsqrt(jnp.mean(zc * zc, axis=-1, keepdims=True) + LN_EPS)
            xh = zc * rstd
            if loss_head:
                err = xh * g_ref[...] + b_ref[...] - do_ref[rows, :]
                loss_acc = loss_acc + 0.5 * jnp.sum(jnp.mean(err * err, axis=-1, keepdims=True), keepdims=True)
                do = err * (1.0 / D)
            else:
                do = do_ref[rows, :]
            dg_acc = dg_acc + jnp.sum((do * xh).reshape(LN_ROWS // 8, 8, D), axis=0)
            db_acc = db_acc + jnp.sum(do.reshape(LN_ROWS // 8, 8, D), axis=0)
            dxh = do * g_ref[...]
            dz = rstd * (dxh - jnp.mean(dxh, axis=-1, keepdims=True) - xh * jnp.mean(dxh * xh, axis=-1, keepdims=True))
            dz_ref[rows, :] = dz
            dzb_ref[rows, :] = dz.astype(BF16)
        dg_ref[...] += jnp.sum(dg_acc, axis=0, keepdims=True)
        db_ref[...] += jnp.sum(db_acc, axis=0, keepdims=True)
        if loss_head:
            loss_ref[...] += loss_acc

    row = lambda i: (i, 0)
    fix = lambda i: (0, 0)
    return pl.pallas_call(
        body,
        out_shape=[jax.ShapeDtypeStruct((S, D), F32), jax.ShapeDtypeStruct((S, D), BF16),
                   jax.ShapeDtypeStruct((1, D), F32), jax.ShapeDtypeStruct((1, D), F32),
                   jax.ShapeDtypeStruct((8, 128), F32)],
        grid=(S // ts,),
        in_specs=[pl.BlockSpec((ts, D), row), pl.BlockSpec((1, D), fix), pl.BlockSpec((1, D), fix),
                  pl.BlockSpec((ts, D), row)] + ([pl.BlockSpec(memory_space=pl.ANY)] if dep is not None else []),
        out_specs=[pl.BlockSpec((ts, D), row), pl.BlockSpec((ts, D), row), pl.BlockSpec((1, D), fix),
                   pl.BlockSpec((1, D), fix), pl.BlockSpec((8, 128), fix)],
        compiler_params=_cparams(("arbitrary",)),
        name=name,
    )(z, ln_g.reshape(1, D), ln_b.reshape(1, D), dout, *([dep] if dep is not None else []))


def _ple_bwd(dz, gate, proj, *, ts=256, name):
    S, D = dz.shape

    def body(dz_ref, g_ref, p_ref, ds_ref, dp_ref, db_ref):
        @pl.when(pl.program_id(0) == 0)
        def _():
            db_ref[...] = jnp.zeros_like(db_ref)

        db_acc = jnp.zeros((8, D), F32)
        for r0 in range(0, ts, LN_ROWS):
            rows = pl.ds(r0, LN_ROWS)
            dzt = dz_ref[rows, :]
            g = g_ref[rows, :]
            ds = dzt * p_ref[rows, :] * g * (1.0 - g)
            ds_ref[rows, :] = ds.astype(BF16)
            dp_ref[rows, :] = (dzt * g).astype(BF16)
            db_acc = db_acc + jnp.sum(ds.reshape(LN_ROWS // 8, 8, D), axis=0)
        db_ref[...] += jnp.sum(db_acc, axis=0, keepdims=True)

    row = lambda i: (i, 0)
    return pl.pallas_call(
        body,
        out_shape=[jax.ShapeDtypeStruct((S, D), BF16), jax.ShapeDtypeStruct((S, D), BF16),
                   jax.ShapeDtypeStruct((1, D), F32)],
        grid=(S // ts,),
        in_specs=[pl.BlockSpec((ts, D), row)] * 3,
        out_specs=[pl.BlockSpec((ts, D), row), pl.BlockSpec((ts, D), row), pl.BlockSpec((1, D), lambda i: (0, 0))],
        compiler_params=_cparams(("arbitrary",)),
        name=name,
    )(dz, gate, proj)


HEAD_PAIR = 2 * HEAD_DIM


ATT_ROWS = 32
ATT_SCALE = HEAD_DIM ** -0.5


def _softmax_piece(s_ref, b_ref, j, rows, qb):
    s = s_ref[j, rows, :] + b_ref[j, rows, :]
    kpos = qb * Q_BLOCK + lax.broadcasted_iota(jnp.int32, (1, KV_SPAN), 1)
    s = jnp.where(kpos >= KV_PAD, s, NEG_INF)
    e = jnp.exp(s - jnp.max(s, axis=-1, keepdims=True))
    return e * (1.0 / jnp.sum(e, axis=-1, keepdims=True))


def _pad_keys(qb, k_ref, v_ref, kp, vp):
    @pl.when(qb == 0)
    def _():
        kp[pl.ds(0, KV_PAD), :] = jnp.zeros((KV_PAD, HEAD_PAIR), BF16)
        vp[pl.ds(0, KV_PAD), :] = jnp.zeros((KV_PAD, HEAD_PAIR), BF16)
        kp[pl.ds(KV_PAD, k_ref.shape[0]), :] = k_ref[...]
        vp[pl.ds(KV_PAD, v_ref.shape[0]), :] = v_ref[...]


def _attn_fwd(qkv, bias):
    S = qkv.shape[0]
    nhp = N_HEADS // 2

    def body(q_ref, k_ref, v_ref, b_ref, o_ref, kp, vp, s_scr, p_scr):
        qb = pl.program_id(1)
        _pad_keys(qb, k_ref, v_ref, kp, vp)
        span = pl.ds(pl.multiple_of(qb * Q_BLOCK, Q_BLOCK), KV_SPAN)
        kc, vc = kp[span, :], vp[span, :]
        qt = q_ref[...] * ATT_SCALE
        first = lax.broadcasted_iota(jnp.int32, (1, HEAD_PAIR), 1) < HEAD_DIM
        for j in range(2):
            s_scr[j] = _bdot(jnp.where(first if j == 0 else ~first, qt, jnp.zeros_like(qt)), kc, NT)
        outs = []
        for j in range(2):
            for r0 in range(0, Q_BLOCK, ATT_ROWS):
                rows = pl.ds(r0, ATT_ROWS)
                p_scr[j, rows, :] = _softmax_piece(s_scr, b_ref, j, rows, qb).astype(BF16)
            outs.append(_bdot(p_scr[j], vc))
        o_ref[...] = jnp.where(first, outs[0], outs[1]).astype(BF16)

    return pl.pallas_call(
        body,
        out_shape=jax.ShapeDtypeStruct((S, D_MODEL), BF16),
        grid=(nhp, S // Q_BLOCK),
        in_specs=[pl.BlockSpec((Q_BLOCK, HEAD_PAIR), lambda h, i: (i, h)),
                  pl.BlockSpec((S, HEAD_PAIR), lambda h, i: (0, nhp + h)),
                  pl.BlockSpec((S, HEAD_PAIR), lambda h, i: (0, 2 * nhp + h)),
                  pl.BlockSpec((2, Q_BLOCK, KV_SPAN), lambda h, i: (h, 0, 0))],
        out_specs=pl.BlockSpec((Q_BLOCK, HEAD_PAIR), lambda h, i: (i, h)),
        scratch_shapes=[pltpu.VMEM((KV_PAD + S, HEAD_PAIR), BF16), pltpu.VMEM((KV_PAD + S, HEAD_PAIR), BF16),
                        pltpu.VMEM((2, Q_BLOCK, KV_SPAN), F32), pltpu.VMEM((2, Q_BLOCK, KV_SPAN), BF16)],
        compiler_params=_cparams(("parallel", "arbitrary")),
        name="attn_fwd",
    )(qkv, qkv, qkv, bias)


def _attn_bwd(qkv, bias, do):
    S = qkv.shape[0]
    nhp = N_HEADS // 2
    nq = S // Q_BLOCK
    scale = HEAD_DIM ** -0.5

    def body(q_ref, k_ref, v_ref, b_ref, do_ref, dq_ref, dk_ref, dv_ref, db_ref, kp, vp, dka, dva,
             s_scr, dp_scr, p_scr, ds_scr):
        qb = pl.program_id(1)
        _pad_keys(qb, k_ref, v_ref, kp, vp)

        @pl.when(qb == 0)
        def _():
            dka[...] = jnp.zeros_like(dka)
            dva[...] = jnp.zeros_like(dva)
            db_ref[...] = jnp.zeros_like(db_ref)

        span = pl.ds(pl.multiple_of(qb * Q_BLOCK, Q_BLOCK), KV_SPAN)
        kc, vc = kp[span, :], vp[span, :]
        qt, dot = q_ref[...] * ATT_SCALE, do_ref[...]
        first = lax.broadcasted_iota(jnp.int32, (1, HEAD_PAIR), 1) < HEAD_DIM
        dqs = []
        qs = [jnp.where(first if j == 0 else ~first, qt, jnp.zeros_like(qt)) for j in range(2)]
        dos = [jnp.where(first if j == 0 else ~first, dot, jnp.zeros_like(dot)) for j in range(2)]
        for j in range(2):
            s_scr[j] = _bdot(qs[j], kc, NT)
            dp_scr[j] = _bdot(dos[j], vc, NT)
        for j in range(2):
            qj, doj = qs[j], dos[j]
            for r0 in range(0, Q_BLOCK, ATT_ROWS):
                rows = pl.ds(r0, ATT_ROWS)
                p = _softmax_piece(s_scr, b_ref, j, rows, qb)
                dp = dp_scr[j, rows, :]
                ds = p * (dp - jnp.sum(p * dp, axis=-1, keepdims=True))
                db_ref[j, rows, :] += ds
                p_scr[j, rows, :] = p.astype(BF16)
                ds_scr[j, rows, :] = ds.astype(BF16)
            dva[span, :] += _bdot(p_scr[j], doj, TN)
            dqs.append(_bdot(ds_scr[j], kc))
            dka[span, :] += _bdot(ds_scr[j], qj, TN)
        dq_ref[...] = (scale * jnp.where(first, dqs[0], dqs[1])).astype(BF16)

        @pl.when(qb == nq - 1)
        def _():
            dk_ref[...] = dka[pl.ds(KV_PAD, S), :].astype(BF16)
            dv_ref[...] = dva[pl.ds(KV_PAD, S), :].astype(BF16)

    blk = pl.BlockSpec((Q_BLOCK, HEAD_PAIR), lambda h, i: (i, h))
    col = pl.BlockSpec((S, HEAD_PAIR), lambda h, i: (0, h))
    bsp = pl.BlockSpec((2, Q_BLOCK, KV_SPAN), lambda h, i: (h, 0, 0))
    return pl.pallas_call(
        body,
        out_shape=[jax.ShapeDtypeStruct((S, D_MODEL), BF16)] * 3
        + [jax.ShapeDtypeStruct((N_HEADS, Q_BLOCK, KV_SPAN), F32)],
        grid=(nhp, nq),
        in_specs=[blk, pl.BlockSpec((S, HEAD_PAIR), lambda h, i: (0, nhp + h)),
                  pl.BlockSpec((S, HEAD_PAIR), lambda h, i: (0, 2 * nhp + h)), bsp, blk],
        out_specs=[blk, col, col, bsp],
        scratch_shapes=[pltpu.VMEM((KV_PAD + S, HEAD_PAIR), BF16), pltpu.VMEM((KV_PAD + S, HEAD_PAIR), BF16),
                        pltpu.VMEM((KV_PAD + S, HEAD_PAIR), F32), pltpu.VMEM((KV_PAD + S, HEAD_PAIR), F32),
                        pltpu.VMEM((2, Q_BLOCK, KV_SPAN), F32), pltpu.VMEM((2, Q_BLOCK, KV_SPAN), F32),
                        pltpu.VMEM((2, Q_BLOCK, KV_SPAN), BF16), pltpu.VMEM((2, Q_BLOCK, KV_SPAN), BF16)],
        compiler_params=_cparams(("parallel", "arbitrary")),
        name="attn_bwd",
    )(qkv, qkv, qkv, bias, do)


def _bias_blocks(rel_bias):
    H = rel_bias.shape[0]
    n_e = BAND + CHUNK - 1
    n_clip = KV_PAD + CHUNK - 1 - MAX_REL + 1
    e = jnp.concatenate([jnp.broadcast_to(rel_bias[:, 2 * MAX_REL:], (H, n_clip)),
                         jnp.flip(rel_bias[:, 2 * MAX_REL - (n_e - n_clip):2 * MAX_REL], axis=1)], axis=1)
    skew = jnp.pad(jnp.tile(e, (1, CHUNK)), ((0, 0), (0, CHUNK))).reshape(H, CHUNK, n_e + 1)
    band = jnp.flip(skew, axis=1)[:, :, :BAND]
    rows = [jnp.pad(band, ((0, 0), (0, 0), (c * CHUNK, KV_SPAN - BAND - c * CHUNK)), constant_values=NEG_INF)
            for c in range(Q_BLOCK // CHUNK)]
    return jnp.concatenate(rows, axis=1)


def _bias_blocks_grad(dblk):
    H = dblk.shape[0]
    n_e = BAND + CHUNK - 1
    n_clip = KV_PAD + CHUNK - 1 - MAX_REL + 1
    parts = jnp.stack([dblk[:, c * CHUNK:(c + 1) * CHUNK, c * CHUNK:c * CHUNK + BAND]
                       for c in range(Q_BLOCK // CHUNK)], axis=1)
    parts = jnp.flip(parts, axis=2)
    parts = jnp.pad(parts, ((0, 0), (0, 0), (0, 0), (0, n_e + 1 - BAND)))
    skew = parts.reshape(H, Q_BLOCK // CHUNK, CHUNK * (n_e + 1))[:, :, :CHUNK * n_e]
    skew = skew.reshape(H, Q_BLOCK, n_e)
    skew = jnp.pad(skew, ((0, 0), (0, 0), (0, 1)))

    def body(s_ref, o_ref):
        de = jnp.sum(s_ref[...], axis=0, keepdims=True)
        lane = lax.broadcasted_iota(jnp.int32, de.shape, 1)
        far = jnp.sum(jnp.where(lane < n_clip, de, 0.0), axis=-1, keepdims=True)
        o_ref[...] = jnp.where(lane == 0, far, jnp.where(lane < n_clip, 0.0, de))

    de = pl.pallas_call(
        body,
        out_shape=jax.ShapeDtypeStruct((H, 1, n_e + 1), F32),
        grid=(H,),
        in_specs=[pl.BlockSpec((None, Q_BLOCK, n_e + 1), lambda h: (h, 0, 0))],
        out_specs=pl.BlockSpec((None, 1, n_e + 1), lambda h: (h, 0, 0)),
        compiler_params=_cparams(("parallel",)),
        name="bias_grad_sum",
    )(skew).reshape(H, n_e + 1)
    near = jnp.flip(de[:, n_clip:n_e], axis=1)
    return jnp.concatenate([jnp.zeros((H, 2 * MAX_REL - (n_e - n_clip)), F32), near, de[:, 0:1]], axis=1)


def _ffn_forward(r1, r1b, p_l, w, l, ready):
    ready(f"up{l}", r1b)
    up_g = _mm_rows([(r1b, w["ffn_up_g"][l], False)], out_dtype=BF16, name=f"ffn_up_g{l}")
    up_v = _mm_rows([(r1b, w["ffn_up_v"][l], False)], out_dtype=BF16, name=f"ffn_up_v{l}")
    h = _ffn_act_fwd(up_g, up_v, w["ffn_dw_w"][l], w["ffn_dw_b"][l], name=f"ffn_act{l}")
    ready(f"dn{l}", h)
    z2, r2, r2b, gate, proj = _proj_ln(r1, h, w["ffn_w_down"][l], w["ln_ffn_g"][l], w["ln_ffn_b"][l],
                                       ple=(w["ple_w_gate"][l], w["ple_b_gate"][l], p_l, w["ple_w_proj"][l]),
                                       name=f"ffn_down_ln{l}")
    return dict(r1b=r1b, up_g=up_g, up_v=up_v, h=h, z2=z2, gate=gate, proj=proj), r2, r2b


def _ffn_backward(sv, dz2, dz2b, p_l, w, l, grads):
    r1b = sv["r1b"]
    ds, dproj, db_gate = _ple_bwd(dz2, sv["gate"], sv["proj"], name=f"ple_bwd{l}")
    dh = _mm_rows([(dz2b, w["ffn_w_down"][l], True)], out_dtype=BF16, name=f"ffn_dh{l}")
    dgate, dval, d_dw_w, d_dw_b = _ffn_act_bwd(sv["up_g"], sv["up_v"], dh, w["ffn_dw_w"][l], w["ffn_dw_b"][l],
                                               name=f"ffn_act_bwd{l}")
    grads["ffn_w_down"][l] = _wgrad(sv["h"], dz2b, tm=1408, name=f"d_ffn_w_down{l}")
    grads["ffn_up_g"][l] = _wgrad(r1b, dgate, tn=1408, piece=D_FF // 4, name=f"d_ffn_up_g{l}")
    grads["ffn_up_v"][l] = _wgrad(r1b, dval, tn=1408, piece=D_FF // 4, name=f"d_ffn_up_v{l}")
    grads["ple_w_gate"][l] = _wgrad(r1b, ds, name=f"d_ple_w_gate{l}")
    grads["ple_w_proj"][l] = _wgrad(p_l, dproj, piece=D_MODEL // N_DEV, name=f"d_ple_w_proj{l}")
    grads["ffn_dw_w"][l] = d_dw_w
    grads["ffn_dw_b"][l] = d_dw_b[0]
    grads["ple_b_gate"][l] = db_gate[0]
    return _mm_rows([(ds, w["ple_w_gate"][l], True), (dgate, w["ffn_up_g"][l], True), (dval, w["ffn_up_v"][l], True)],
                    add=dz2, add_scale=ALPHA, name=f"dr1_{l}")


def _local_step(x, p, target, w, ready=lambda group, after: None, emit=lambda group, grads: None):
    grads = {k: [None, None] for k in ("ffn_w_down", "ffn_up_g", "ffn_up_v", "ple_w_gate", "ple_w_proj", "ffn_dw_w",
                                       "ffn_dw_b", "ple_b_gate", "ln_ffn_g", "ln_ffn_b", "ln_mix_g", "ln_mix_b")}

    xb, pb = x.astype(BF16), p.astype(BF16)
    ready("mix", None)
    u = _mm_rows([(xb, w["mix_w_in"], False)], name="mix_in")
    ycat, dpool = _mixer_fwd(u, w["pool_w"], w["pool_scale"], w["conv_dw_w"], w["conv_dw_b"], w["conv_ln_g"],
                             w["conv_ln_b"])
    ready("mixo", ycat)
    z1, r1, r1b = _proj_ln(x, ycat, w["mix_w_out"], w["ln_mix_g"][0], w["ln_mix_b"][0], name="mix_out_ln")
    sv0, r2, r2b = _ffn_forward(r1, r1b, pb[0], w, 0, ready)

    ready("attn", r2b)
    qkv = _mm_rows([(r2b, w["attn_w_qkv"], False)], out_dtype=BF16, name="attn_qkv")
    bias = _bias_blocks(w["attn_rel_bias"])
    attn = _attn_fwd(qkv, bias)
    z3, r3, r3b = _proj_ln(r2, attn, w["attn_w_o"], w["ln_mix_g"][1], w["ln_mix_b"][1], name="attn_out_ln")
    sv1, _, _ = _ffn_forward(r3, r3b, pb[1], w, 1, ready)

    dz4, dz4b, grads["ln_ffn_g"][1], grads["ln_ffn_b"][1], loss = _ln_bwd(
        sv1["z2"], w["ln_ffn_g"][1], w["ln_ffn_b"][1], target, loss_head=True, name="loss_ln_bwd")
    dr3 = _ffn_backward(sv1, dz4, dz4b, pb[1], w, 1, grads)
    dz3, dz3b, grads["ln_mix_g"][1], grads["ln_mix_b"][1], _ = _ln_bwd(
        z3, w["ln_mix_g"][1], w["ln_mix_b"][1], dr3, dep=emit("ffn1", grads), name="ln_mix_bwd1")
    grads["attn_w_o"] = _wgrad(attn, dz3b, name="d_attn_w_o")
    dattn = _mm_rows([(dz3b, w["attn_w_o"], True)], out_dtype=BF16, name="d_attn")
    dq, dk, dv, dbias = _attn_bwd(qkv, bias, dattn)
    grads["attn_rel_bias"] = _bias_blocks_grad(dbias)
    dqkv = jnp.concatenate([dq, dk, dv], axis=1)
    grads["attn_w_qkv"] = _wgrad(r2b, dqkv, tn=768, piece=3 * D_MODEL // N_DEV, name="d_attn_w_qkv")
    dr2 = _mm_rows([(dqkv, w["attn_w_qkv"], True)], add=dz3, add_scale=ALPHA, dep=emit("attn", grads), name="dr2")

    dz2, dz2b, grads["ln_ffn_g"][0], grads["ln_ffn_b"][0], _ = _ln_bwd(
        sv0["z2"], w["ln_ffn_g"][0], w["ln_ffn_b"][0], dr2, name="ln_ffn_bwd0")
    dr1 = _ffn_backward(sv0, dz2, dz2b, pb[0], w, 0, grads)
    dz1, dz1b, grads["ln_mix_g"][0], grads["ln_mix_b"][0], _ = _ln_bwd(
        z1, w["ln_mix_g"][0], w["ln_mix_b"][0], dr1, dep=emit("ffn0", grads), name="ln_mix_bwd0")
    grads["mix_w_out"] = _wgrad(ycat, dz1b, name="d_mix_w_out")
    dycat = _mm_rows([(dz1b, w["mix_w_out"], True)], name="d_ycat")
    du, g_pw, g_ps, g_cw, g_cb, g_cg, g_cbb = _mixer_bwd(u, dpool, dycat, w["pool_w"], w["pool_scale"],
                                                         w["conv_dw_w"], w["conv_dw_b"], w["conv_ln_g"],
                                                         w["conv_ln_b"])
    grads["mix_w_in"] = _wgrad(xb, du, tn=768, piece=3 * D_POOL // N_DEV, name="d_mix_w_in")
    grads["conv_dw_w"] = g_cw
    grad_x = _mm_rows([(du, w["mix_w_in"], True)], add=dz1, add_scale=ALPHA, dep=emit("mix", grads), name="grad_x")
    grads.update(pool_w=g_pw, pool_scale=g_ps[0], conv_dw_w=g_cw, conv_dw_b=g_cb[0], conv_ln_g=g_cg[0],
                 conv_ln_b=g_cbb[0])
    for kname in ("ln_ffn_g", "ln_ffn_b", "ln_mix_g", "ln_mix_b"):
        grads[kname] = [a[0] for a in grads[kname]]
    return loss[0, 0], grad_x, grads


_HBM = pl.BlockSpec(memory_space=pltpu.HBM)
_SEM = pl.BlockSpec(memory_space=pltpu.SEMAPHORE)
_EFFECT = pltpu.SideEffectType.DATAFLOW_SIDE_EFFECTING


def _slot(ref, place, shape, k):
    if place in ("stack", "pieces"):
        return ref.at[k]
    ax = place[1]
    n = shape[ax]
    return ref.at[(slice(None),) * ax + (pl.ds(pl.multiple_of(k * n, n), n),)]


def _result_shape(buf, place):
    if place == "stack":
        return (N_DEV,) + buf.shape
    if place == "pieces":
        return buf.shape
    return tuple(s * N_DEV if i == place[1] else s for i, s in enumerate(buf.shape))


def _peers(x, y, c):
    for d in range(1, N_DEV):
        px, py, pc = x ^ ((d >> 2) & 1), y ^ ((d >> 1) & 1), c ^ (d & 1)
        yield d, (px, py, pc), 4 * px + 2 * py + pc


def _exchange_start(bufs, places, after, *, name):
    nb = len(bufs)
    lands = [lax.empty(_result_shape(b, p_), b.dtype) for b, p_ in zip(bufs, places)]
    has_after = after is not None

    def body(*refs):
        srcs, dsts = refs[:nb], refs[nb:2 * nb]
        outs = refs[2 * nb + has_after:]
        send_sems, recv_sems, token = outs[0], outs[1], outs[2 + 2 * nb]
        x, y, c = lax.axis_index("x"), lax.axis_index("y"), lax.axis_index("c")
        me = 4 * x + 2 * y + c
        for b in range(nb):
            for d, dev, peer in _peers(x, y, c):
                pltpu.make_async_remote_copy(
                    src_ref=srcs[b].at[peer] if places[b] == "pieces" else srcs[b],
                    dst_ref=_slot(dsts[b], places[b], bufs[b].shape, me),
                    send_sem=send_sems.at[b * N_DEV + d], recv_sem=recv_sems.at[b * N_DEV + d],
                    device_id=dev, device_id_type=pl.DeviceIdType.MESH).start()
            pltpu.make_async_copy(srcs[b].at[me] if places[b] == "pieces" else srcs[b],
                                  _slot(dsts[b], places[b], bufs[b].shape, me), recv_sems.at[b * N_DEV]).start()
        token[...] = jnp.zeros_like(token)

    sems = pltpu.SemaphoreType.DMA((nb * N_DEV,))
    ins = [pltpu.with_memory_space_constraint(a, pltpu.HBM) for a in list(bufs) + lands]
    out = pl.pallas_call(
        body,
        out_shape=(sems, sems, *[pltpu.HBM(a.shape, a.dtype) for a in ins], jax.ShapeDtypeStruct((8, 128), F32)),
        in_specs=[_HBM] * (2 * nb) + ([pl.BlockSpec(memory_space=pl.ANY)] if has_after else []),
        out_specs=(_SEM, _SEM, *[_HBM] * (2 * nb), pl.BlockSpec(memory_space=pltpu.VMEM)),
        input_output_aliases={i: 2 + i for i in range(2 * nb)},
        compiler_params=pltpu.CompilerParams(has_side_effects=_EFFECT),
        name=name,
    )(*ins, *([after] if has_after else []))
    return dict(send=out[0], recv=out[1], srcs=out[2:2 + nb], lands=out[2 + nb:2 + 2 * nb], token=out[-1],
                places=places)


def _exchange_wait(h, after, *, name):
    nb = len(h["srcs"])
    places = h["places"]
    shapes = [a.shape for a in h["srcs"]]

    def body(*refs):
        srcs, dsts, send_sems, recv_sems = refs[:nb], refs[nb:2 * nb], refs[2 * nb], refs[2 * nb + 1]
        x, y, c = lax.axis_index("x"), lax.axis_index("y"), lax.axis_index("c")
        me = 4 * x + 2 * y + c
        for b in range(nb):
            pieces = places[b] == "pieces"
            for d, dev, peer in _peers(x, y, c):
                cp = pltpu.make_async_remote_copy(
                    src_ref=srcs[b].at[peer] if pieces else srcs[b],
                    dst_ref=_slot(dsts[b], places[b], shapes[b], peer),
                    send_sem=send_sems.at[b * N_DEV + d], recv_sem=recv_sems.at[b * N_DEV + d],
                    device_id=dev, device_id_type=pl.DeviceIdType.MESH)
                cp.wait_send()
                cp.wait_recv()
            pltpu.make_async_copy(srcs[b].at[me] if pieces else srcs[b], _slot(dsts[b], places[b], shapes[b], me),
                                  recv_sems.at[b * N_DEV]).wait()

    ins = list(h["srcs"]) + list(h["lands"])
    out = pl.pallas_call(
        body,
        out_shape=tuple(pltpu.HBM(a.shape, a.dtype) for a in ins),
        in_specs=[_HBM] * (2 * nb) + [_SEM, _SEM, pl.BlockSpec(memory_space=pl.ANY)],
        out_specs=tuple([_HBM] * (2 * nb)),
        input_output_aliases={i: i for i in range(2 * nb)},
        compiler_params=pltpu.CompilerParams(has_side_effects=_EFFECT),
        name=name,
    )(*ins, h["send"], h["recv"], after)
    return out[nb:]


def _adamw(recv, w, m, v, *, name):
    R, C = w.shape
    tr = R
    for cand in (512, 256, 128, 64, 32, 16):
        if R % cand == 0 and cand * C * 4 <= 2 * 1024 * 1024:
            tr = cand
            break
    c1 = 1.0 - ADAM_B1 ** ADAM_STEP
    c2 = 1.0 - ADAM_B2 ** ADAM_STEP

    def body(r_ref, w_ref, m_ref, v_ref, g_ref, d_ref, mo_ref, vo_ref):
        g = r_ref[0].astype(F32)
        for i in range(1, N_DEV):
            g = g + r_ref[i].astype(F32)
        m_new = ADAM_B1 * m_ref[...] + (1.0 - ADAM_B1) * g
        v_new = ADAM_B2 * v_ref[...] + (1.0 - ADAM_B2) * (g * g)
        m_hat = m_new / c1
        v_hat = v_new / c2
        g_ref[...] = g
        d_ref[...] = -ADAM_LR * (m_hat / (jnp.sqrt(v_hat) + ADAM_EPS) + ADAM_WD * w_ref[...])
        mo_ref[...] = m_new
        vo_ref[...] = v_new

    row = pl.BlockSpec((tr, C), lambda i: (i, 0))
    return pl.pallas_call(
        body,
        out_shape=[jax.ShapeDtypeStruct((R, C), F32)] * 4,
        grid=(R // tr,),
        in_specs=[pl.BlockSpec((N_DEV, tr, C), lambda i: (0, i, 0)), row, row, row],
        out_specs=[row] * 4,
        compiler_params=_cparams(("parallel",)),
        name=name,
    )(recv, w, m, v)


def _ffn_groups(l):
    return ((f"up{l}", (("ffn_w_up", l, BF16, "stack"), ("ffn_dw_w", l, F32, "stack"))),
            (f"dn{l}", (("ffn_w_down", l, BF16, ("axis", 0)), ("ple_w_gate", l, BF16, ("axis", 0)),
                        ("ple_w_proj", l, BF16, ("axis", 1)))))


_GATHER_GROUPS = (
    ("mix", (("mix_w_in", 0, BF16, "stack"), ("conv_dw_w", 0, F32, "stack"))),
    ("mixo", (("mix_w_out", 0, BF16, ("axis", 0)),)),
    *_ffn_groups(0),
    ("attn", (("attn_w_qkv", 0, BF16, ("axis", 1)), ("attn_w_o", 0, BF16, ("axis", 0)))),
    *_ffn_groups(1))
_SHARDED = ("mix_w_in", "conv_dw_w", "mix_w_out", "attn_w_qkv", "attn_w_o", "ffn_w_up", "ffn_dw_w", "ffn_w_down",
            "ple_w_gate", "ple_w_proj")
_REPLICATED = ("pool_w", "pool_scale", "conv_dw_b", "conv_ln_g", "conv_ln_b", "attn_rel_bias", "ln_mix_g",
               "ln_mix_b", "ffn_dw_b", "ple_b_gate", "ln_ffn_g", "ln_ffn_b")


def _pack_rows(parts, row_mult, dtype):
    lead = parts[0].shape[:-1]
    flat = jnp.concatenate([a.astype(dtype) for a in parts], axis=-1)
    n = flat.shape[-1]
    unit = row_mult * LANES
    padded = -(-n // unit) * unit
    flat = jnp.pad(flat, [(0, 0)] * len(lead) + [(0, padded - n)])
    return flat.reshape(lead + (padded // LANES, LANES))


def _unpack(flat2d, shapes):
    flat = flat2d.reshape(-1)
    out, o = [], 0
    for s in shapes:
        n = math.prod(s)
        out.append(flat[o:o + n].reshape(s))
        o += n
    return out


def _full_from_shards(g, axis):
    parts = jnp.moveaxis(g, 0, axis)
    shp = list(g.shape[1:])
    shp[axis] *= g.shape[0]
    return parts.reshape(shp)


def _pieces_from_full(full, axis, k=N_DEV):
    shp = list(full.shape)
    n = shp[axis] // k
    t = full.reshape(shp[:axis] + [k, n] + shp[axis + 1:])
    return jnp.moveaxis(t, axis, 0)


def kernel(x, p, mix_w_in, pool_w, pool_scale, conv_dw_w, conv_dw_b, conv_ln_g, conv_ln_b, mix_w_out, attn_w_qkv, attn_rel_bias, attn_w_o, ln_mix_g, ln_mix_b, ffn_w_up, ffn_dw_w, ffn_dw_b, ffn_w_down, ple_w_proj, ple_w_gate, ple_b_gate, ln_ffn_g, ln_ffn_b, loss_target, m_mix_w_in, m_pool_w, m_pool_scale, m_conv_dw_w, m_conv_dw_b, m_conv_ln_g, m_conv_ln_b, m_mix_w_out, m_attn_w_qkv, m_attn_rel_bias, m_attn_w_o, m_ln_mix_g, m_ln_mix_b, m_ffn_w_up, m_ffn_dw_w, m_ffn_dw_b, m_ffn_w_down, m_ple_w_proj, m_ple_w_gate, m_ple_b_gate, m_ln_ffn_g, m_ln_ffn_b, v_mix_w_in, v_pool_w, v_pool_scale, v_conv_dw_w, v_conv_dw_b, v_conv_ln_g, v_conv_ln_b, v_mix_w_out, v_attn_w_qkv, v_attn_rel_bias, v_attn_w_o, v_ln_mix_g, v_ln_mix_b, v_ffn_w_up, v_ffn_dw_w, v_ffn_dw_b, v_ffn_w_down, v_ple_w_proj, v_ple_w_gate, v_ple_b_gate, v_ln_ffn_g, v_ln_ffn_b):
    a = dict(locals())
    sh_names = list(_SHARDED)
    names = sh_names + list(_REPLICATED)
    wts = {n: a[n] for n in names}
    mom = {n: a["m_" + n] for n in names}
    var = {n: a["v_" + n] for n in names}

    gather = {}
    token = None
    for group, items in _GATHER_GROUPS:
        gather[group] = _exchange_start([wts[n][l].astype(dt) for n, l, dt, _ in items], [pl_ for *_, pl_ in items],
                                        token, name="gather_start_" + group)
        token = gather[group]["token"]

    w = dict(pool_w=pool_w[0], pool_scale=pool_scale[0], conv_dw_b=conv_dw_b[0], conv_ln_g=conv_ln_g[0],
             conv_ln_b=conv_ln_b[0], attn_rel_bias=attn_rel_bias[0], ln_mix_g=ln_mix_g, ln_mix_b=ln_mix_b,
             ffn_dw_b=ffn_dw_b, ple_b_gate=ple_b_gate, ln_ffn_g=ln_ffn_g, ln_ffn_b=ln_ffn_b)
    for n in ("ffn_up_g", "ffn_up_v", "ffn_dw_w", "ffn_w_down", "ple_w_gate", "ple_w_proj"):
        w[n] = [None, None]

    def ready(group, after):
        got = _exchange_wait(gather[group], token if after is None else after, name="gather_wait_" + group)
        if group == "mix":
            w["mix_w_in"] = _join_shards(got[0], name="join_mix_w_in")
            w["conv_dw_w"] = _full_from_shards(got[1], 1)
        elif group == "mixo":
            (w["mix_w_out"],) = got
        elif group == "attn":
            w["attn_w_qkv"], w["attn_w_o"] = got
        elif group[:2] == "up":
            l = int(group[2])
            w["ffn_up_g"][l] = _join_shards(got[0], k=N_DEV // 2, part=0, name=f"join_ffn_up_g{l}")
            w["ffn_up_v"][l] = _join_shards(got[0], k=N_DEV // 2, part=1, name=f"join_ffn_up_v{l}")
            w["ffn_dw_w"][l] = _full_from_shards(got[1], 1)
        else:
            l = int(group[2])
            w["ffn_w_down"][l], w["ple_w_gate"][l], w["ple_w_proj"][l] = got

    scatter = {}

    def emit(group, gr):
        if group[:3] == "ffn":
            l = int(group[3])
            pieces = [jnp.concatenate([gr["ffn_up_g"][l], gr["ffn_up_v"][l]]),
                      _pieces_from_full(gr["ffn_dw_w"][l], 1), _pieces_from_full(gr["ffn_w_down"][l], 0),
                      _pieces_from_full(gr["ple_w_gate"][l], 0), gr["ple_w_proj"][l]]
        elif group == "attn":
            pieces = [gr["attn_w_qkv"], _pieces_from_full(gr["attn_w_o"], 0)]
        else:
            pieces = [gr["mix_w_in"], _pieces_from_full(gr["conv_dw_w"], 1), _pieces_from_full(gr["mix_w_out"], 0)]
        scatter[group] = _exchange_start([a.astype(BF16) for a in pieces], ["pieces"] * len(pieces), None,
                                         name="grad_start_" + group)
        return scatter[group]["token"]

    loss_part, grad_x, gr = _local_step(x[0], p[:, 0], loss_target[0], w, ready, emit)
    loss = lax.psum(loss_part, ("x", "y", "c"))

    gfull = dict(
        pool_w=gr["pool_w"][None], pool_scale=gr["pool_scale"][None], conv_dw_b=gr["conv_dw_b"][None],
        conv_ln_g=gr["conv_ln_g"][None], conv_ln_b=gr["conv_ln_b"][None], attn_rel_bias=gr["attn_rel_bias"][None],
        ln_mix_g=jnp.stack(gr["ln_mix_g"]), ln_mix_b=jnp.stack(gr["ln_mix_b"]), ffn_dw_b=jnp.stack(gr["ffn_dw_b"]),
        ple_b_gate=jnp.stack(gr["ple_b_gate"]), ln_ffn_g=jnp.stack(gr["ln_ffn_g"]),
        ln_ffn_b=jnp.stack(gr["ln_ffn_b"]))
    rep_send = _pack_rows([gfull[n].reshape(-1) for n in _REPLICATED], 8, F32)
    rep_handle = _exchange_start([rep_send], ["stack"], None, name="grad_start_replicated")

    group_weights = {"ffn1": (("ffn_w_up", 1), ("ffn_dw_w", 1), ("ffn_w_down", 1), ("ple_w_gate", 1), ("ple_w_proj", 1)),
                     "attn": (("attn_w_qkv", 0), ("attn_w_o", 0)),
                     "ffn0": (("ffn_w_up", 0), ("ffn_dw_w", 0), ("ffn_w_down", 0), ("ple_w_gate", 0), ("ple_w_proj", 0)),
                     "mix": (("mix_w_in", 0), ("conv_dw_w", 0), ("mix_w_out", 0))}
    per_layer = {n: {} for n in sh_names}
    after = grad_x
    for group in ("ffn1", "attn", "ffn0", "mix"):
        recv = _exchange_wait(scatter[group], after, name="grad_wait_" + group)
        for (n, l), r in zip(group_weights[group], recv):
            per_layer[n][l] = _adamw(r, wts[n][l], mom[n][l], var[n][l], name=f"adamw_{n}{l}")
            after = per_layer[n][l][0]
    res = [{}, {}, {}, {}]
    for n in sh_names:
        for k in range(4):
            res[k][n] = jnp.stack([per_layer[n][l][k] for l in sorted(per_layer[n])])
    (rep_recv,) = _exchange_wait(rep_handle, after, name="grad_wait_replicated")

    def flat_state(d):
        return _pack_rows([d[n].reshape(-1) for n in _REPLICATED], 8, F32)

    rep_out = _adamw(rep_recv, flat_state(wts), flat_state(mom), flat_state(var), name="adamw_replicated")
    for k in range(4):
        for n, arr in zip(_REPLICATED, _unpack(rep_out[k], [wts[n].shape for n in _REPLICATED])):
            res[k][n] = arr
    order = ["mix_w_in", "pool_w", "pool_scale", "conv_dw_w", "conv_dw_b", "conv_ln_g", "conv_ln_b", "mix_w_out",
             "attn_w_qkv", "attn_rel_bias", "attn_w_o", "ln_mix_g", "ln_mix_b", "ffn_w_up", "ffn_dw_w", "ffn_dw_b",
             "ffn_w_down", "ple_w_proj", "ple_w_gate", "ple_b_gate", "ln_ffn_g", "ln_ffn_b"]
    outs = [loss, grad_x[None]]
    for k in range(4):
        outs += [res[k][n] for n in order]
    return tuple(outs)
```

```python
import functools
import math

import jax
import jax.numpy as jnp
from jax import lax
from jax.experimental import pallas as pl
from jax.experimental.pallas import tpu as pltpu

F32 = jnp.float32
BF16 = jnp.bfloat16

N_DEV = 8
D_MODEL = 1024
D_POOL = 512
D_CONV = 512
POOL_WINDOWS = (2, 4, 8, 16)
POOL_GROUP = 128
CONV_KERNEL = 31
CHUNK = 64
HEAD_DIM = 64
N_HEADS = 16
LEFT_CHUNKS = 8
BAND = (LEFT_CHUNKS + 1) * CHUNK
MAX_REL = 256
D_FF = 2816
PLE_DIM = 256
ALPHA = 4.0 ** 0.25
LN_EPS = 1e-5
NEG_INF = -1e30
ADAM_LR, ADAM_B1, ADAM_B2, ADAM_EPS, ADAM_WD, ADAM_STEP = 0.001, 0.9, 0.999, 1e-08, 0.01, 10

Q_BLOCK = 4 * CHUNK
KV_PAD = LEFT_CHUNKS * CHUNK
KV_SPAN = KV_PAD + Q_BLOCK
CONV_HALO = 32
FFN_HALO = 16
SUB_ROWS, SUB_LANES = 64, 128
LANES = 1024
VMEM_LIMIT = 56 * 1024 * 1024


def _cparams(sem=None):
    return pltpu.CompilerParams(dimension_semantics=sem, vmem_limit_bytes=VMEM_LIMIT)


def _tile(dim, pref):
    if dim <= pref:
        return dim
    t = pref - pref % 128
    while t >= 128:
        if dim % t == 0:
            return t
        t -= 128
    return dim


def _sigmoid(x):
    return 1.0 / (1.0 + jnp.exp(-x))


def _bdot(a, b, dn=(((1,), (0,)), ((), ()))):
    return lax.dot_general(a.astype(BF16), b.astype(BF16), dn, preferred_element_type=F32)


NT = (((1,), (1,)), ((), ()))
TN = (((0,), (0,)), ((), ()))


def _wgrad(a, b, *, tm=1024, tn=1024, tk=1024, piece=None, name):
    K, M = a.shape
    kb, N = b.shape
    assert K == kb, (a.shape, b.shape)
    tm, tn, tk = _tile(M, tm), _tile(N, tn), _tile(K, tk)
    nk = K // tk
    per = 1 if piece is None else tn // piece
    assert piece is None or tn == per * piece

    def body(a_ref, b_ref, o_ref, acc):
        k = pl.program_id(2)

        @pl.when(k == 0)
        def _():
            acc[...] = jnp.zeros_like(acc)

        acc[...] += _bdot(a_ref[...], b_ref[...], TN)

        @pl.when(k == nk - 1)
        def _():
            if piece is None:
                o_ref[...] = acc[...].astype(BF16)
            else:
                for s in range(per):
                    o_ref[s] = acc[:, s * piece:(s + 1) * piece].astype(BF16)

    if piece is None:
        out_shape, out_spec = (M, N), pl.BlockSpec((tm, tn), lambda i, j, k: (i, j))
    else:
        out_shape, out_spec = (N // piece, M, piece), pl.BlockSpec((per, tm, piece), lambda i, j, k: (j, i, 0))
    return pl.pallas_call(
        body,
        out_shape=jax.ShapeDtypeStruct(out_shape, BF16),
        grid=(M // tm, N // tn, nk),
        in_specs=[pl.BlockSpec((tk, tm), lambda i, j, k: (k, i)), pl.BlockSpec((tk, tn), lambda i, j, k: (k, j))],
        out_specs=out_spec,
        scratch_shapes=[pltpu.VMEM((tm, tn), F32)],
        compiler_params=_cparams(("parallel", "parallel", "arbitrary")),
        name=name,
    )(a, b)


def _join_shards(g, *, k=None, part=0, tm=256, name):
    n, M, wd = g.shape
    k = n if k is None else k

    def body(g_ref, o_ref):
        for j in range(k):
            o_ref[:, j * wd:(j + 1) * wd] = g_ref[j]

    return pl.pallas_call(
        body,
        out_shape=jax.ShapeDtypeStruct((M, k * wd), g.dtype),
        grid=(M // tm,),
        in_specs=[pl.BlockSpec((k, tm, wd), lambda i: (part, i, 0))],
        out_specs=pl.BlockSpec((tm, k * wd), lambda i: (i, 0)),
        compiler_params=_cparams(("parallel",)),
        name=name,
    )(g)


def _mm_rows(pairs, *, add=None, add_scale=1.0, out_dtype=F32, tm=256, dep=None, name):
    M = pairs[0][0].shape[0]
    N = pairs[0][1].shape[0] if pairs[0][2] else pairs[0][1].shape[1]
    n = len(pairs)
    has_add = add is not None

    def body(*refs):
        o_ref = refs[-1]
        acc = None
        for i, (_, _, tr) in enumerate(pairs):
            part = _bdot(refs[2 * i][...], refs[2 * i + 1][...], NT if tr else (((1,), (0,)), ((), ())))
            acc = part if acc is None else acc + part
        if has_add:
            acc = acc + add_scale * refs[2 * n][...]
        o_ref[...] = acc.astype(out_dtype)

    in_specs, args = [], []
    for a, w_, _ in pairs:
        in_specs += [pl.BlockSpec((tm, a.shape[1]), lambda i: (i, 0)), pl.BlockSpec(w_.shape, lambda i: (0, 0))]
        args += [a, w_]
    if has_add:
        in_specs.append(pl.BlockSpec((tm, N), lambda i: (i, 0)))
        args.append(add)
    if dep is not None:
        in_specs.append(pl.BlockSpec(memory_space=pl.ANY))
        args.append(dep)
    return pl.pallas_call(
        body,
        out_shape=jax.ShapeDtypeStruct((M, N), out_dtype),
        grid=(M // tm,),
        in_specs=in_specs,
        out_specs=pl.BlockSpec((tm, N), lambda i: (i, 0)),
        compiler_params=_cparams(("parallel",)),
        name=name,
    )(*args)


def _layer_norm_rows(z, g, b):
    mu = jnp.mean(z, axis=-1, keepdims=True)
    zc = z - mu
    var = jnp.mean(zc * zc, axis=-1, keepdims=True)
    return zc * lax.rsqrt(var + LN_EPS) * g + b


def _proj_ln(res, a, w, ln_g, ln_b, *, ple=None, ts=256, name):
    S, D = res.shape
    ka = a.shape[1]
    has_ple = ple is not None
    row = lambda i: (i, 0)
    fix = lambda i: (0, 0)

    def body(*refs):
        if has_ple:
            (res_ref, a_ref, w_ref, g_ref, b_ref, wg_ref, bg_ref, p_ref, wp_ref, z_ref, r_ref, rb_ref, gate_ref,
             proj_ref, acc) = refs
        else:
            res_ref, a_ref, w_ref, g_ref, b_ref, z_ref, r_ref, rb_ref, acc = refs
        acc[...] = _bdot(a_ref[...], w_ref[...])
        if has_ple:
            gate_ref[...] = _bdot(res_ref[...], wg_ref[...])
            proj_ref[...] = _bdot(p_ref[...], wp_ref[...])
        for r0 in range(0, ts, LN_ROWS):
            rows = pl.ds(r0, LN_ROWS)
            z = ALPHA * res_ref[rows, :] + acc[rows, :]
            if has_ple:
                gate = _sigmoid(gate_ref[rows, :] + bg_ref[...])
                gate_ref[rows, :] = gate
                z = z + gate * proj_ref[rows, :]
            z_ref[rows, :] = z
            r = _layer_norm_rows(z, g_ref[...], b_ref[...])
            r_ref[rows, :] = r
            rb_ref[rows, :] = r.astype(BF16)

    in_specs = [pl.BlockSpec((ts, D), row), pl.BlockSpec((ts, ka), row), pl.BlockSpec((ka, D), fix),
                pl.BlockSpec((1, D), fix), pl.BlockSpec((1, D), fix)]
    args = [res, a, w, ln_g.reshape(1, D), ln_b.reshape(1, D)]
    out_dtypes = [F32, F32, BF16]
    if has_ple:
        wg, bg, p, wp = ple
        in_specs += [pl.BlockSpec((D, D), fix), pl.BlockSpec((1, D), fix), pl.BlockSpec((ts, PLE_DIM), row),
                     pl.BlockSpec((PLE_DIM, D), fix)]
        args += [wg, bg.reshape(1, D), p, wp]
        out_dtypes += [F32, F32]
    return pl.pallas_call(
        body,
        out_shape=[jax.ShapeDtypeStruct((S, D), dt) for dt in out_dtypes],
        grid=(S // ts,),
        in_specs=in_specs,
        out_specs=[pl.BlockSpec((ts, D), row)] * len(out_dtypes),
        scratch_shapes=[pltpu.VMEM((ts, D), F32)],
        compiler_params=_cparams(("parallel",)),
        name=name,
    )(*args)


CONV_ROWS = 32
LN_ROWS = 16


def _shifted_copies(src, dst, rows):
    for b in range(1, 8):
        for c0 in range(0, src.shape[1], SUB_LANES):
            ln = pl.ds(c0, SUB_LANES)
            for r0 in range(0, rows, SUB_ROWS):
                rc = min(SUB_ROWS, rows - r0)
                dst[b - 1, pl.ds(r0, rc), ln] = src[pl.ds(r0 + b, rc), ln]


def _rows_at(src, copies, off, n, ln):
    b = off % 8
    return src[pl.ds(off, n), ln] if b == 0 else copies[b - 1, pl.ds(off - b, n), ln]


def _conv31(stg, gsh, cw_ref, cb_ref, out, rows, first_off):
    for c0 in range(0, D_CONV, SUB_LANES):
        ln = pl.ds(c0, SUB_LANES)
        for r0 in range(0, rows, CONV_ROWS):
            acc = jnp.zeros((CONV_ROWS, SUB_LANES), F32) + cb_ref[:, ln]
            for k in range(CONV_KERNEL):
                acc = acc + cw_ref[k:k + 1, ln] * _rows_at(stg, gsh, first_off + k + r0, CONV_ROWS, ln)
            out[pl.ds(r0, CONV_ROWS), ln] = acc


def _mixer_fwd(u, pool_w, pool_scale, conv_w, conv_b, cln_g, cln_b, *, ts=256):
    S = u.shape[0]
    hb = CONV_HALO
    nh = ts // hb

    def body(u_ref, uh_ref, pw_ref, ps_ref, cw_ref, cb_ref, g_ref, b_ref, y_ref, d_ref, sta, stg, gsh, hcs):
        i = pl.program_id(0)
        first = i == 0
        sta[pl.ds(0, hb), :] = jnp.where(first, 0.0, uh_ref[:, 0:D_POOL])
        sta[pl.ds(hb, ts), :] = u_ref[:, 0:D_POOL]
        glu_h = uh_ref[:, D_POOL:D_POOL + D_CONV] * _sigmoid(uh_ref[:, D_POOL + D_CONV:])
        stg[pl.ds(0, hb), :] = jnp.where(first, 0.0, glu_h)
        stg[pl.ds(hb, ts), :] = u_ref[:, D_POOL:D_POOL + D_CONV] * _sigmoid(u_ref[:, D_POOL + D_CONV:])

        pos = (i * ts + lax.broadcasted_iota(jnp.int32, (ts, 1), 0) + 1).astype(F32)
        for g, w in enumerate(POOL_WINDOWS):
            lanes = pl.ds(g * POOL_GROUP, POOL_GROUP)
            a_g = sta[pl.ds(hb, ts), lanes]
            s = a_g
            for j in range(1, w):
                s = s + sta[pl.ds(hb - j, ts), lanes]
            d_g = s / jnp.minimum(pos, float(w)) - a_g
            d_ref[:, lanes] = d_g.astype(BF16)
            y_ref[:, lanes] = (_bdot(d_g, pw_ref[g]) * ps_ref[:, lanes]).astype(BF16)

        _shifted_copies(stg, gsh, hb + ts - 8)
        _conv31(stg, gsh, cw_ref, cb_ref, hcs, ts, hb - (CONV_KERNEL - 1))
        for r0 in range(0, ts, LN_ROWS):
            rows = pl.ds(r0, LN_ROWS)
            ln = _layer_norm_rows(hcs[rows, :], g_ref[...], b_ref[...])
            y_ref[rows, D_POOL:] = (ln * _sigmoid(ln)).astype(BF16)

    fix2 = lambda i: (0, 0)
    return pl.pallas_call(
        body,
        out_shape=[jax.ShapeDtypeStruct((S, D_MODEL), BF16), jax.ShapeDtypeStruct((S, D_POOL), BF16)],
        grid=(S // ts,),
        in_specs=[pl.BlockSpec((ts, 3 * D_POOL), lambda i: (i, 0)),
                  pl.BlockSpec((hb, 3 * D_POOL), lambda i: (jnp.maximum(i * nh - 1, 0), 0)),
                  pl.BlockSpec((4, POOL_GROUP, POOL_GROUP), lambda i: (0, 0, 0)),
                  pl.BlockSpec((1, D_POOL), fix2), pl.BlockSpec((CONV_KERNEL, D_CONV), fix2),
                  pl.BlockSpec((1, D_CONV), fix2), pl.BlockSpec((1, D_CONV), fix2), pl.BlockSpec((1, D_CONV), fix2)],
        out_specs=[pl.BlockSpec((ts, D_MODEL), lambda i: (i, 0)), pl.BlockSpec((ts, D_POOL), lambda i: (i, 0))],
        scratch_shapes=[pltpu.VMEM((hb + ts, D_POOL), F32), pltpu.VMEM((hb + ts, D_CONV), F32),
                        pltpu.VMEM((7, hb + ts - 8, D_CONV), F32), pltpu.VMEM((ts, D_CONV), F32)],
        compiler_params=_cparams(("parallel",)),
        name="mixer_fwd",
    )(u, u, pool_w, pool_scale.reshape(1, D_POOL), conv_w, conv_b.reshape(1, D_CONV), cln_g.reshape(1, D_CONV),
      cln_b.reshape(1, D_CONV))


def _mixer_bwd(u, d, dycat, pool_w, pool_scale, conv_w, conv_b, cln_g, cln_b, *, ts=256):
    S = u.shape[0]
    hb = CONV_HALO
    nh = ts // hb
    n = S // ts
    te = ts + hb
    K = CONV_KERNEL

    def body(u_ref, up_ref, un_ref, d_ref, dy_ref, dyn_ref, pw_ref, ps_ref, cw_ref, cb_ref, g_ref, b_ref,
             du_ref, dpw_ref, dps_ref, dcw_ref, dcb_ref, dg_ref, db_ref, stg, std, sth, gsh, hcs, hsh):
        i = pl.program_id(0)
        first = i == 0
        last = i == n - 1

        @pl.when(first)
        def _():
            dpw_ref[...] = jnp.zeros_like(dpw_ref)
            dps_ref[...] = jnp.zeros_like(dps_ref)
            dcw_ref[...] = jnp.zeros_like(dcw_ref)
            dcb_ref[...] = jnp.zeros_like(dcb_ref)
            dg_ref[...] = jnp.zeros_like(dg_ref)
            db_ref[...] = jnp.zeros_like(db_ref)

        pos_e = (i * ts + lax.broadcasted_iota(jnp.int32, (te, 1), 0) + 1).astype(F32)
        dya = dy_ref[:, 0:D_POOL]
        dya_n = jnp.where(last, 0.0, dyn_ref[:, 0:D_POOL])
        for g, w in enumerate(POOL_WINDOWS):
            lanes = pl.ds(g * POOL_GROUP, POOL_GROUP)
            sl = slice(g * POOL_GROUP, (g + 1) * POOL_GROUP)
            pw = pw_ref[g]
            scale = ps_ref[:, lanes]
            d_g = d_ref[:, lanes]
            pre = _bdot(d_g, pw)
            dps_ref[:, lanes] += jnp.sum(dya[:, sl] * pre, axis=0, keepdims=True)
            dys = dya[:, sl] * scale
            dpw_ref[g] += _bdot(d_g, dys, TN)
            dys_e = jnp.concatenate([dys, dya_n[:, sl] * scale], axis=0)
            dd = _bdot(dys_e, pw, NT)
            std[:, lanes] = dd / jnp.minimum(pos_e, float(w))
            da = -dd[0:ts]
            for m in range(w):
                da = da + std[pl.ds(m, ts), lanes]
            du_ref[:, lanes] = da.astype(BF16)

        glu_p = up_ref[:, D_POOL:D_POOL + D_CONV] * _sigmoid(up_ref[:, D_POOL + D_CONV:])
        stg[pl.ds(0, hb), :] = jnp.where(first, 0.0, glu_p)
        bv = u_ref[:, D_POOL:D_POOL + D_CONV]
        sg = _sigmoid(u_ref[:, D_POOL + D_CONV:])
        stg[pl.ds(hb, ts), :] = bv * sg
        glu_n = un_ref[:, D_POOL:D_POOL + D_CONV] * _sigmoid(un_ref[:, D_POOL + D_CONV:])
        stg[pl.ds(hb + ts, hb), :] = jnp.where(last, 0.0, glu_n)
        _shifted_copies(stg, gsh, hb + te - 8)
        _conv31(stg, gsh, cw_ref, cb_ref, hcs, te, hb - (K - 1))

        sums = [jnp.zeros((8, D_CONV), F32) for _ in range(3)]
        for r0 in range(0, te, LN_ROWS):
            rows = pl.ds(r0, LN_ROWS)
            hc = hcs[rows, :]
            hcc = hc - jnp.mean(hc, axis=-1, keepdims=True)
            rstd = lax.rsqrt(jnp.mean(hcc * hcc, axis=-1, keepdims=True) + LN_EPS)
            xh = hcc * rstd
            ln = xh * g_ref[...] + b_ref[...]
            sl_ = _sigmoid(ln)
            if r0 < ts:
                dyb = dy_ref[rows, D_POOL:]
            else:
                dyb = jnp.where(last, 0.0, dyn_ref[pl.ds(r0 - ts, LN_ROWS), D_POOL:])
            dln = dyb * (sl_ * (1.0 + ln * (1.0 - sl_)))
            dxh = dln * g_ref[...]
            dhc = rstd * (dxh - jnp.mean(dxh, axis=-1, keepdims=True)
                          - xh * jnp.mean(dxh * xh, axis=-1, keepdims=True))
            sth[rows, :] = dhc
            if r0 < ts:
                for n_, term in enumerate((dln * xh, dln, dhc)):
                    sums[n_] = sums[n_] + jnp.sum(term.reshape(LN_ROWS // 8, 8, D_CONV), axis=0)
        dg_ref[...] += jnp.sum(sums[0], axis=0, keepdims=True)
        db_ref[...] += jnp.sum(sums[1], axis=0, keepdims=True)
        dcb_ref[...] += jnp.sum(sums[2], axis=0, keepdims=True)

        _shifted_copies(sth, hsh, te - 8)
        for c0 in range(0, D_CONV, SUB_LANES):
            ln_ = pl.ds(c0, SUB_LANES)
            for r0 in range(0, ts, CONV_ROWS):
                rows = pl.ds(r0, CONV_ROWS)
                dglu = jnp.zeros((CONV_ROWS, SUB_LANES), F32)
                for k in range(K):
                    dglu = dglu + cw_ref[k:k + 1, ln_] * _rows_at(sth, hsh, K - 1 - k + r0, CONV_ROWS, ln_)
                bv = u_ref[rows, pl.ds(D_POOL + c0, SUB_LANES)]
                sg = _sigmoid(u_ref[rows, pl.ds(D_POOL + D_CONV + c0, SUB_LANES)])
                du_ref[rows, pl.ds(D_POOL + c0, SUB_LANES)] = (dglu * sg).astype(BF16)
                du_ref[rows, pl.ds(D_POOL + D_CONV + c0, SUB_LANES)] = (dglu * bv * sg * (1.0 - sg)).astype(BF16)
            for k in range(K):
                tap = jnp.zeros((8, SUB_LANES), F32)
                for r0 in range(0, ts, CONV_ROWS):
                    prod = sth[pl.ds(r0, CONV_ROWS), ln_] * _rows_at(stg, gsh, hb - (K - 1) + k + r0, CONV_ROWS, ln_)
                    tap = tap + jnp.sum(prod.reshape(CONV_ROWS // 8, 8, SUB_LANES), axis=0)
                dcw_ref[k:k + 1, ln_] += jnp.sum(tap, axis=0, keepdims=True)

    fix2 = lambda i: (0, 0)
    prev = lambda i: (jnp.maximum(i * nh - 1, 0), 0)
    nxt = lambda i: (jnp.minimum((i + 1) * nh, S // hb - 1), 0)
    return pl.pallas_call(
        body,
        out_shape=[jax.ShapeDtypeStruct((S, 3 * D_POOL), BF16),
                   jax.ShapeDtypeStruct((4, POOL_GROUP, POOL_GROUP), F32),
                   jax.ShapeDtypeStruct((1, D_POOL), F32),
                   jax.ShapeDtypeStruct((K, D_CONV), F32),
                   jax.ShapeDtypeStruct((1, D_CONV), F32),
                   jax.ShapeDtypeStruct((1, D_CONV), F32),
                   jax.ShapeDtypeStruct((1, D_CONV), F32)],
        grid=(n,),
        in_specs=[pl.BlockSpec((ts, 3 * D_POOL), lambda i: (i, 0)),
                  pl.BlockSpec((hb, 3 * D_POOL), prev),
                  pl.BlockSpec((hb, 3 * D_POOL), nxt),
                  pl.BlockSpec((ts, D_POOL), lambda i: (i, 0)),
                  pl.BlockSpec((ts, D_MODEL), lambda i: (i, 0)),
                  pl.BlockSpec((hb, D_MODEL), nxt),
                  pl.BlockSpec((4, POOL_GROUP, POOL_GROUP), lambda i: (0, 0, 0)),
                  pl.BlockSpec((1, D_POOL), fix2), pl.BlockSpec((K, D_CONV), fix2),
                  pl.BlockSpec((1, D_CONV), fix2), pl.BlockSpec((1, D_CONV), fix2), pl.BlockSpec((1, D_CONV), fix2)],
        out_specs=[pl.BlockSpec((ts, 3 * D_POOL), lambda i: (i, 0)),
                   pl.BlockSpec((4, POOL_GROUP, POOL_GROUP), lambda i: (0, 0, 0)),
                   pl.BlockSpec((1, D_POOL), fix2), pl.BlockSpec((K, D_CONV), fix2),
                   pl.BlockSpec((1, D_CONV), fix2), pl.BlockSpec((1, D_CONV), fix2), pl.BlockSpec((1, D_CONV), fix2)],
        scratch_shapes=[pltpu.VMEM((hb + ts + hb, D_CONV), F32), pltpu.VMEM((te, D_POOL), F32),
                        pltpu.VMEM((te, D_CONV), F32), pltpu.VMEM((7, hb + te - 8, D_CONV), F32),
                        pltpu.VMEM((te, D_CONV), F32), pltpu.VMEM((7, te - 8, D_CONV), F32)],
        compiler_params=_cparams(("arbitrary",)),
        name="mixer_bwd",
    )(u, u, u, d, dycat, dycat, pool_w, pool_scale.reshape(1, D_POOL), conv_w, conv_b.reshape(1, D_CONV),
      cln_g.reshape(1, D_CONV), cln_b.reshape(1, D_CONV))


_GELU_C = math.sqrt(2.0 / math.pi)


def _gelu_parts(x):
    inner = _GELU_C * (x + 0.044715 * x * x * x)
    th = jnp.tanh(inner)
    ge = 0.5 * x * (1.0 + th)
    dge = 0.5 * (1.0 + th) + 0.5 * x * (1.0 - th * th) * (_GELU_C * (1.0 + 3.0 * 0.044715 * x * x))
    return ge, dge


def _ffn_act_fwd(gate, val, dw_w, dw_b, *, ts=256, tc=1408, name):
    S, F = gate.shape
    hb = FFN_HALO
    nh = ts // hb
    tc = _tile(F, tc)

    def body(g_ref, gh_ref, v_ref, w_ref, b_ref, h_ref, st):
        i = pl.program_id(0)
        st[pl.ds(0, hb), :] = jnp.where(i == 0, 0.0, gh_ref[...].astype(F32))
        st[pl.ds(hb, ts), :] = g_ref[...].astype(F32)
        for c0 in range(0, tc, SUB_LANES):
            ln = pl.ds(c0, SUB_LANES)
            w0, w1, w2, b = w_ref[0:1, ln], w_ref[1:2, ln], w_ref[2:3, ln], b_ref[:, ln]
            for r0 in range(0, ts, SUB_ROWS):
                gc = b + w0 * st[pl.ds(hb - 2 + r0, SUB_ROWS), ln] + w1 * st[pl.ds(hb - 1 + r0, SUB_ROWS), ln] \
                    + w2 * st[pl.ds(hb + r0, SUB_ROWS), ln]
                ge, _ = _gelu_parts(gc)
                rows = pl.ds(r0, SUB_ROWS)
                h_ref[rows, ln] = (ge * v_ref[rows, ln].astype(F32)).astype(BF16)

    return pl.pallas_call(
        body,
        out_shape=jax.ShapeDtypeStruct((S, F), BF16),
        grid=(S // ts, F // tc),
        in_specs=[pl.BlockSpec((ts, tc), lambda i, j: (i, j)),
                  pl.BlockSpec((hb, tc), lambda i, j: (jnp.maximum(i * nh - 1, 0), j)),
                  pl.BlockSpec((ts, tc), lambda i, j: (i, j)),
                  pl.BlockSpec((3, tc), lambda i, j: (0, j)),
                  pl.BlockSpec((1, tc), lambda i, j: (0, j))],
        out_specs=pl.BlockSpec((ts, tc), lambda i, j: (i, j)),
        scratch_shapes=[pltpu.VMEM((hb + ts, tc), F32)],
        compiler_params=_cparams(("parallel", "parallel")),
        name=name,
    )(gate, gate, val, dw_w, dw_b.reshape(1, F))


def _ffn_act_bwd(gate, val, dh, dw_w, dw_b, *, ts=256, tc=1408, name):
    S, F = gate.shape
    hb = FFN_HALO
    nh = ts // hb
    n = S // ts
    te = ts + hb
    tc = _tile(F, tc)

    def body(g_ref, gp_ref, gn_ref, v_ref, vn_ref, dh_ref, dhn_ref, w_ref, b_ref,
             dg_ref, dv_ref, dw_ref, db_ref, st, sd):
        i = pl.program_id(1)
        first = i == 0
        last = i == n - 1

        @pl.when(first)
        def _():
            dw_ref[...] = jnp.zeros_like(dw_ref)
            db_ref[...] = jnp.zeros_like(db_ref)

        st[pl.ds(0, hb), :] = jnp.where(first, 0.0, gp_ref[...].astype(F32))
        st[pl.ds(hb, ts), :] = g_ref[...].astype(F32)
        st[pl.ds(hb + ts, hb), :] = jnp.where(last, 0.0, gn_ref[...].astype(F32))
        for c0 in range(0, tc, SUB_LANES):
            ln = pl.ds(c0, SUB_LANES)
            w0, w1, w2, b = w_ref[0:1, ln], w_ref[1:2, ln], w_ref[2:3, ln], b_ref[:, ln]
            db_acc = jnp.zeros((8, SUB_LANES), F32)
            dw_acc = [jnp.zeros((8, SUB_LANES), F32) for _ in range(3)]
            for r0 in range(0, te, SUB_ROWS):
                rc = min(SUB_ROWS, te - r0)
                taps = [st[pl.ds(hb - 2 + k + r0, rc), ln] for k in range(3)]
                gc = b + w0 * taps[0] + w1 * taps[1] + w2 * taps[2]
                ge, dge = _gelu_parts(gc)
                if r0 < ts:
                    rows = pl.ds(r0, rc)
                    val, dh = v_ref[rows, ln].astype(F32), dh_ref[rows, ln].astype(F32)
                else:
                    val = jnp.where(last, 0.0, vn_ref[:, ln].astype(F32)[0:rc])
                    dh = jnp.where(last, 0.0, dhn_ref[:, ln].astype(F32)[0:rc])
                dgc = dh * val * dge
                sd[pl.ds(r0, rc), ln] = dgc
                if r0 < ts:
                    dv_ref[rows, ln] = (dh * ge).astype(BF16)
                    db_acc = db_acc + jnp.sum(dgc.reshape(rc // 8, 8, SUB_LANES), axis=0)
                    for k in range(3):
                        dw_acc[k] = dw_acc[k] + jnp.sum((dgc * taps[k]).reshape(rc // 8, 8, SUB_LANES), axis=0)
            db_ref[:, ln] += jnp.sum(db_acc, axis=0, keepdims=True)
            for k in range(3):
                dw_ref[k:k + 1, ln] += jnp.sum(dw_acc[k], axis=0, keepdims=True)
            for r0 in range(0, ts, SUB_ROWS):
                dgate = w0 * sd[pl.ds(2 + r0, SUB_ROWS), ln] + w1 * sd[pl.ds(1 + r0, SUB_ROWS), ln] \
                    + w2 * sd[pl.ds(r0, SUB_ROWS), ln]
                dg_ref[pl.ds(r0, SUB_ROWS), ln] = dgate.astype(BF16)

    cur = lambda j, i: (i, j)
    prev = lambda j, i: (jnp.maximum(i * nh - 1, 0), j)
    nxt = lambda j, i: (jnp.minimum((i + 1) * nh, S // hb - 1), j)
    return pl.pallas_call(
        body,
        out_shape=[jax.ShapeDtypeStruct((S, F), BF16), jax.ShapeDtypeStruct((S, F), BF16),
                   jax.ShapeDtypeStruct((3, F), F32), jax.ShapeDtypeStruct((1, F), F32)],
        grid=(F // tc, n),
        in_specs=[pl.BlockSpec((ts, tc), cur), pl.BlockSpec((hb, tc), prev), pl.BlockSpec((hb, tc), nxt),
                  pl.BlockSpec((ts, tc), cur), pl.BlockSpec((hb, tc), nxt),
                  pl.BlockSpec((ts, tc), cur), pl.BlockSpec((hb, tc), nxt),
                  pl.BlockSpec((3, tc), lambda j, i: (0, j)), pl.BlockSpec((1, tc), lambda j, i: (0, j))],
        out_specs=[pl.BlockSpec((ts, tc), cur), pl.BlockSpec((ts, tc), cur),
                   pl.BlockSpec((3, tc), lambda j, i: (0, j)), pl.BlockSpec((1, tc), lambda j, i: (0, j))],
        scratch_shapes=[pltpu.VMEM((hb + ts + hb, tc), F32), pltpu.VMEM((te, tc), F32)],
        compiler_params=_cparams(("parallel", "arbitrary")),
        name=name,
    )(gate, gate, gate, val, val, dh, dh, dw_w, dw_b.reshape(1, F))


def _ln_bwd(z, ln_g, ln_b, dout, *, loss_head=False, ts=256, dep=None, name):
    S, D = z.shape

    def body(z_ref, g_ref, b_ref, do_ref, *rest):
        dz_ref, dzb_ref, dg_ref, db_ref, loss_ref = rest[-5:]
        i = pl.program_id(0)

        @pl.when(i == 0)
        def _():
            dg_ref[...] = jnp.zeros_like(dg_ref)
            db_ref[...] = jnp.zeros_like(db_ref)
            loss_ref[...] = jnp.zeros_like(loss_ref)

        dg_acc = jnp.zeros((8, D), F32)
        db_acc = jnp.zeros((8, D), F32)
        loss_acc = jnp.zeros((1, 1), F32)
        for r0 in range(0, ts, LN_ROWS):
            rows = pl.ds(r0, LN_ROWS)
            zt = z_ref[rows, :]
            zc = zt - jnp.mean(zt, axis=-1, keepdims=True)
            rstd = lax.rsqrt(jnp.mean(zc * zc, axis=-1, keepdims=True) + LN_EPS)
            xh = zc * rstd
            if loss_head:
                err = xh * g_ref[...] + b_ref[...] - do_ref[rows, :]
                loss_acc = loss_acc + 0.5 * jnp.sum(jnp.mean(err * err, axis=-1, keepdims=True), keepdims=True)
                do = err * (1.0 / D)
            else:
                do = do_ref[rows, :]
            dg_acc = dg_acc + jnp.sum((do * xh).reshape(LN_ROWS // 8, 8, D), axis=0)
            db_acc = db_acc + jnp.sum(do.reshape(LN_ROWS // 8, 8, D), axis=0)
            dxh = do * g_ref[...]
            dz = rstd * (dxh - jnp.mean(dxh, axis=-1, keepdims=True) - xh * jnp.mean(dxh * xh, axis=-1, keepdims=True))
            dz_ref[rows, :] = dz
            dzb_ref[rows, :] = dz.astype(BF16)
        dg_ref[...] += jnp.sum(dg_acc, axis=0, keepdims=True)
        db_ref[...] += jnp.sum(db_acc, axis=0, keepdims=True)
        if loss_head:
            loss_ref[...] += loss_acc

    row = lambda i: (i, 0)
    fix = lambda i: (0, 0)
    return pl.pallas_call(
        body,
        out_shape=[jax.ShapeDtypeStruct((S, D), F32), jax.ShapeDtypeStruct((S, D), BF16),
                   jax.ShapeDtypeStruct((1, D), F32), jax.ShapeDtypeStruct((1, D), F32),
                   jax.ShapeDtypeStruct((8, 128), F32)],
        grid=(S // ts,),
        in_specs=[pl.BlockSpec((ts, D), row), pl.BlockSpec((1, D), fix), pl.BlockSpec((1, D), fix),
                  pl.BlockSpec((ts, D), row)] + ([pl.BlockSpec(memory_space=pl.ANY)] if dep is not None else []),
        out_specs=[pl.BlockSpec((ts, D), row), pl.BlockSpec((ts, D), row), pl.BlockSpec((1, D), fix),
                   pl.BlockSpec((1, D), fix), pl.BlockSpec((8, 128), fix)],
        compiler_params=_cparams(("arbitrary",)),
        name=name,
    )(z, ln_g.reshape(1, D), ln_b.reshape(1, D), dout, *([dep] if dep is not None else []))


def _ple_bwd(dz, gate, proj, *, ts=256, name):
    S, D = dz.shape

    def body(dz_ref, g_ref, p_ref, ds_ref, dp_ref, db_ref):
        @pl.when(pl.program_id(0) == 0)
        def _():
            db_ref[...] = jnp.zeros_like(db_ref)

        db_acc = jnp.zeros((8, D), F32)
        for r0 in range(0, ts, LN_ROWS):
            rows = pl.ds(r0, LN_ROWS)
            dzt = dz_ref[rows, :]
            g = g_ref[rows, :]
            ds = dzt * p_ref[rows, :] * g * (1.0 - g)
            ds_ref[rows, :] = ds.astype(BF16)
            dp_ref[rows, :] = (dzt * g).astype(BF16)
            db_acc = db_acc + jnp.sum(ds.reshape(LN_ROWS // 8, 8, D), axis=0)
        db_ref[...] += jnp.sum(db_acc, axis=0, keepdims=True)

    row = lambda i: (i, 0)
    return pl.pallas_call(
        body,
        out_shape=[jax.ShapeDtypeStruct((S, D), BF16), jax.ShapeDtypeStruct((S, D), BF16),
                   jax.ShapeDtypeStruct((1, D), F32)],
        grid=(S // ts,),
        in_specs=[pl.BlockSpec((ts, D), row)] * 3,
        out_specs=[pl.BlockSpec((ts, D), row), pl.BlockSpec((ts, D), row), pl.BlockSpec((1, D), lambda i: (0, 0))],
        compiler_params=_cparams(("arbitrary",)),
        name=name,
    )(dz, gate, proj)


HEAD_PAIR = 2 * HEAD_DIM


ATT_ROWS = 32
ATT_SCALE = HEAD_DIM ** -0.5


def _softmax_piece(s_ref, b_ref, j, rows, qb):
    s = s_ref[j, rows, :] + b_ref[j, rows, :]
    kpos = qb * Q_BLOCK + lax.broadcasted_iota(jnp.int32, (1, KV_SPAN), 1)
    s = jnp.where(kpos >= KV_PAD, s, NEG_INF)
    e = jnp.exp(s - jnp.max(s, axis=-1, keepdims=True))
    return e * (1.0 / jnp.sum(e, axis=-1, keepdims=True))


def _pad_keys(qb, k_ref, v_ref, kp, vp):
    @pl.when(qb == 0)
    def _():
        kp[pl.ds(0, KV_PAD), :] = jnp.zeros((KV_PAD, HEAD_PAIR), BF16)
        vp[pl.ds(0, KV_PAD), :] = jnp.zeros((KV_PAD, HEAD_PAIR), BF16)
        kp[pl.ds(KV_PAD, k_ref.shape[0]), :] = k_ref[...]
        vp[pl.ds(KV_PAD, v_ref.shape[0]), :] = v_ref[...]


def _attn_fwd(qkv, bias):
    S = qkv.shape[0]
    nhp = N_HEADS // 2

    def body(q_ref, k_ref, v_ref, b_ref, o_ref, kp, vp, s_scr, p_scr):
        qb = pl.program_id(1)
        _pad_keys(qb, k_ref, v_ref, kp, vp)
        span = pl.ds(pl.multiple_of(qb * Q_BLOCK, Q_BLOCK), KV_SPAN)
        kc, vc = kp[span, :], vp[span, :]
        qt = q_ref[...] * ATT_SCALE
        first = lax.broadcasted_iota(jnp.int32, (1, HEAD_PAIR), 1) < HEAD_DIM
        for j in range(2):
            s_scr[j] = _bdot(jnp.where(first if j == 0 else ~first, qt, jnp.zeros_like(qt)), kc, NT)
        outs = []
        for j in range(2):
            for r0 in range(0, Q_BLOCK, ATT_ROWS):
                rows = pl.ds(r0, ATT_ROWS)
                p_scr[j, rows, :] = _softmax_piece(s_scr, b_ref, j, rows, qb).astype(BF16)
            outs.append(_bdot(p_scr[j], vc))
        o_ref[...] = jnp.where(first, outs[0], outs[1]).astype(BF16)

    return pl.pallas_call(
        body,
        out_shape=jax.ShapeDtypeStruct((S, D_MODEL), BF16),
        grid=(nhp, S // Q_BLOCK),
        in_specs=[pl.BlockSpec((Q_BLOCK, HEAD_PAIR), lambda h, i: (i, h)),
                  pl.BlockSpec((S, HEAD_PAIR), lambda h, i: (0, nhp + h)),
                  pl.BlockSpec((S, HEAD_PAIR), lambda h, i: (0, 2 * nhp + h)),
                  pl.BlockSpec((2, Q_BLOCK, KV_SPAN), lambda h, i: (h, 0, 0))],
        out_specs=pl.BlockSpec((Q_BLOCK, HEAD_PAIR), lambda h, i: (i, h)),
        scratch_shapes=[pltpu.VMEM((KV_PAD + S, HEAD_PAIR), BF16), pltpu.VMEM((KV_PAD + S, HEAD_PAIR), BF16),
                        pltpu.VMEM((2, Q_BLOCK, KV_SPAN), F32), pltpu.VMEM((2, Q_BLOCK, KV_SPAN), BF16)],
        compiler_params=_cparams(("parallel", "arbitrary")),
        name="attn_fwd",
    )(qkv, qkv, qkv, bias)


def _attn_bwd(qkv, bias, do):
    S = qkv.shape[0]
    nhp = N_HEADS // 2
    nq = S // Q_BLOCK
    scale = HEAD_DIM ** -0.5

    def body(q_ref, k_ref, v_ref, b_ref, do_ref, dq_ref, dk_ref, dv_ref, db_ref, kp, vp, dka, dva,
             s_scr, dp_scr, p_scr, ds_scr):
        qb = pl.program_id(1)
        _pad_keys(qb, k_ref, v_ref, kp, vp)

        @pl.when(qb == 0)
        def _():
            dka[...] = jnp.zeros_like(dka)
            dva[...] = jnp.zeros_like(dva)
            db_ref[...] = jnp.zeros_like(db_ref)

        span = pl.ds(pl.multiple_of(qb * Q_BLOCK, Q_BLOCK), KV_SPAN)
        kc, vc = kp[span, :], vp[span, :]
        qt, dot = q_ref[...] * ATT_SCALE, do_ref[...]
        first = lax.broadcasted_iota(jnp.int32, (1, HEAD_PAIR), 1) < HEAD_DIM
        dqs = []
        qs = [jnp.where(first if j == 0 else ~first, qt, jnp.zeros_like(qt)) for j in range(2)]
        dos = [jnp.where(first if j == 0 else ~first, dot, jnp.zeros_like(dot)) for j in range(2)]
        for j in range(2):
            s_scr[j] = _bdot(qs[j], kc, NT)
            dp_scr[j] = _bdot(dos[j], vc, NT)
        for j in range(2):
            qj, doj = qs[j], dos[j]
            for r0 in range(0, Q_BLOCK, ATT_ROWS):
                rows = pl.ds(r0, ATT_ROWS)
                p = _softmax_piece(s_scr, b_ref, j, rows, qb)
                dp = dp_scr[j, rows, :]
                ds = p * (dp - jnp.sum(p * dp, axis=-1, keepdims=True))
                db_ref[j, rows, :] += ds
                p_scr[j, rows, :] = p.astype(BF16)
                ds_scr[j, rows, :] = ds.astype(BF16)
            dva[span, :] += _bdot(p_scr[j], doj, TN)
            dqs.append(_bdot(ds_scr[j], kc))
            dka[span, :] += _bdot(ds_scr[j], qj, TN)
        dq_ref[...] = (scale * jnp.where(first, dqs[0], dqs[1])).astype(BF16)

        @pl.when(qb == nq - 1)
        def _():
            dk_ref[...] = dka[pl.ds(KV_PAD, S), :].astype(BF16)
            dv_ref[...] = dva[pl.ds(KV_PAD, S), :].astype(BF16)

    blk = pl.BlockSpec((Q_BLOCK, HEAD_PAIR), lambda h, i: (i, h))
    col = pl.BlockSpec((S, HEAD_PAIR), lambda h, i: (0, h))
    bsp = pl.BlockSpec((2, Q_BLOCK, KV_SPAN), lambda h, i: (h, 0, 0))
    return pl.pallas_call(
        body,
        out_shape=[jax.ShapeDtypeStruct((S, D_MODEL), BF16)] * 3
        + [jax.ShapeDtypeStruct((N_HEADS, Q_BLOCK, KV_SPAN), F32)],
        grid=(nhp, nq),
        in_specs=[blk, pl.BlockSpec((S, HEAD_PAIR), lambda h, i: (0, nhp + h)),
                  pl.BlockSpec((S, HEAD_PAIR), lambda h, i: (0, 2 * nhp + h)), bsp, blk],
        out_specs=[blk, col, col, bsp],
        scratch_shapes=[pltpu.VMEM((KV_PAD + S, HEAD_PAIR), BF16), pltpu.VMEM((KV_PAD + S, HEAD_PAIR), BF16),
                        pltpu.VMEM((KV_PAD + S, HEAD_PAIR), F32), pltpu.VMEM((KV_PAD + S, HEAD_PAIR), F32),
                        pltpu.VMEM((2, Q_BLOCK, KV_SPAN), F32), pltpu.VMEM((2, Q_BLOCK, KV_SPAN), F32),
                        pltpu.VMEM((2, Q_BLOCK, KV_SPAN), BF16), pltpu.VMEM((2, Q_BLOCK, KV_SPAN), BF16)],
        compiler_params=_cparams(("parallel", "arbitrary")),
        name="attn_bwd",
    )(qkv, qkv, qkv, bias, do)


N_DIST = BAND + CHUNK - 1
N_FAR = KV_PAD + CHUNK - MAX_REL


def _shear_rows(x, towards_right):
    row = lax.broadcasted_iota(jnp.int32, (Q_BLOCK, 1), 0)
    for bit in range(Q_BLOCK.bit_length() - 1):
        step = 1 << bit
        x = jnp.where((row & step) != 0, pltpu.roll(x, step if towards_right else KV_SPAN - step, 1), x)
    return x


def _bias_blocks(rel_bias):
    H = rel_bias.shape[0]
    e = jnp.concatenate([jnp.broadcast_to(rel_bias[:, 2 * MAX_REL:], (H, N_FAR)),
                         jnp.flip(rel_bias[:, 2 * MAX_REL - (N_DIST - N_FAR):2 * MAX_REL], axis=1),
                         jnp.zeros((H, KV_SPAN - N_DIST), F32)], axis=1).reshape(H, 1, KV_SPAN)

    def body(e_ref, o_ref):
        first = pltpu.roll(jnp.broadcast_to(e_ref[...], (Q_BLOCK, KV_SPAN)), KV_SPAN - (CHUNK - 1), 1)
        x = _shear_rows(first, True)
        row = lax.broadcasted_iota(jnp.int32, (Q_BLOCK, 1), 0)
        chunk0 = row - (row & (CHUNK - 1))
        k = lax.broadcasted_iota(jnp.int32, (1, KV_SPAN), 1)
        o_ref[...] = jnp.where((k >= chunk0) & (k < chunk0 + BAND), x, NEG_INF)

    return pl.pallas_call(
        body,
        out_shape=jax.ShapeDtypeStruct((H, Q_BLOCK, KV_SPAN), F32),
        grid=(H,),
        in_specs=[pl.BlockSpec((None, 1, KV_SPAN), lambda h: (h, 0, 0))],
        out_specs=pl.BlockSpec((None, Q_BLOCK, KV_SPAN), lambda h: (h, 0, 0)),
        compiler_params=_cparams(("parallel",)),
        name="bias_blocks",
    )(e)


def _bias_blocks_grad(dblk):
    H = dblk.shape[0]

    def body(d_ref, o_ref):
        x = pltpu.roll(_shear_rows(d_ref[...], False), CHUNK - 1, 1)
        de = jnp.sum(x, axis=0, keepdims=True)
        lane = lax.broadcasted_iota(jnp.int32, de.shape, 1)
        far = jnp.sum(jnp.where(lane < N_FAR, de, 0.0), axis=-1, keepdims=True)
        o_ref[...] = jnp.where(lane == 0, far, jnp.where(lane < N_FAR, 0.0, de))

    de = pl.pallas_call(
        body,
        out_shape=jax.ShapeDtypeStruct((H, 1, KV_SPAN), F32),
        grid=(H,),
        in_specs=[pl.BlockSpec((None, Q_BLOCK, KV_SPAN), lambda h: (h, 0, 0))],
        out_specs=pl.BlockSpec((None, 1, KV_SPAN), lambda h: (h, 0, 0)),
        compiler_params=_cparams(("parallel",)),
        name="bias_grad_sum",
    )(dblk).reshape(H, KV_SPAN)
    near = jnp.flip(de[:, N_FAR:N_DIST], axis=1)
    return jnp.concatenate([jnp.zeros((H, 2 * MAX_REL - (N_DIST - N_FAR)), F32), near, de[:, 0:1]], axis=1)


def _ffn_forward(r1, r1b, p_l, w, l, ready):
    ready(f"up{l}", r1b)
    up_g = _mm_rows([(r1b, w["ffn_up_g"][l], False)], out_dtype=BF16, name=f"ffn_up_g{l}")
    up_v = _mm_rows([(r1b, w["ffn_up_v"][l], False)], out_dtype=BF16, name=f"ffn_up_v{l}")
    h = _ffn_act_fwd(up_g, up_v, w["ffn_dw_w"][l], w["ffn_dw_b"][l], name=f"ffn_act{l}")
    ready(f"dn{l}", h)
    z2, r2, r2b, gate, proj = _proj_ln(r1, h, w["ffn_w_down"][l], w["ln_ffn_g"][l], w["ln_ffn_b"][l],
                                       ple=(w["ple_w_gate"][l], w["ple_b_gate"][l], p_l, w["ple_w_proj"][l]),
                                       name=f"ffn_down_ln{l}")
    return dict(r1b=r1b, up_g=up_g, up_v=up_v, h=h, z2=z2, gate=gate, proj=proj), r2, r2b


def _ffn_backward(sv, dz2, dz2b, p_l, w, l, grads):
    r1b = sv["r1b"]
    ds, dproj, db_gate = _ple_bwd(dz2, sv["gate"], sv["proj"], name=f"ple_bwd{l}")
    dh = _mm_rows([(dz2b, w["ffn_w_down"][l], True)], out_dtype=BF16, name=f"ffn_dh{l}")
    dgate, dval, d_dw_w, d_dw_b = _ffn_act_bwd(sv["up_g"], sv["up_v"], dh, w["ffn_dw_w"][l], w["ffn_dw_b"][l],
                                               name=f"ffn_act_bwd{l}")
    grads["ffn_w_down"][l] = _wgrad(sv["h"], dz2b, tm=1408, name=f"d_ffn_w_down{l}")
    grads["ffn_up_g"][l] = _wgrad(r1b, dgate, tn=1408, piece=D_FF // 4, name=f"d_ffn_up_g{l}")
    grads["ffn_up_v"][l] = _wgrad(r1b, dval, tn=1408, piece=D_FF // 4, name=f"d_ffn_up_v{l}")
    grads["ple_w_gate"][l] = _wgrad(r1b, ds, name=f"d_ple_w_gate{l}")
    grads["ple_w_proj"][l] = _wgrad(p_l, dproj, piece=D_MODEL // N_DEV, name=f"d_ple_w_proj{l}")
    grads["ffn_dw_w"][l] = d_dw_w
    grads["ffn_dw_b"][l] = d_dw_b[0]
    grads["ple_b_gate"][l] = db_gate[0]
    return _mm_rows([(ds, w["ple_w_gate"][l], True), (dgate, w["ffn_up_g"][l], True), (dval, w["ffn_up_v"][l], True)],
                    add=dz2, add_scale=ALPHA, name=f"dr1_{l}")


def _local_step(x, p, target, w, ready=lambda group, after: None, emit=lambda group, grads: None):
    grads = {k: [None, None] for k in ("ffn_w_down", "ffn_up_g", "ffn_up_v", "ple_w_gate", "ple_w_proj", "ffn_dw_w",
                                       "ffn_dw_b", "ple_b_gate", "ln_ffn_g", "ln_ffn_b", "ln_mix_g", "ln_mix_b")}

    xb, pb = x.astype(BF16), p.astype(BF16)
    ready("mix", None)
    u = _mm_rows([(xb, w["mix_w_in"], False)], name="mix_in")
    ycat, dpool = _mixer_fwd(u, w["pool_w"], w["pool_scale"], w["conv_dw_w"], w["conv_dw_b"], w["conv_ln_g"],
                             w["conv_ln_b"])
    ready("mixo", ycat)
    z1, r1, r1b = _proj_ln(x, ycat, w["mix_w_out"], w["ln_mix_g"][0], w["ln_mix_b"][0], name="mix_out_ln")
    sv0, r2, r2b = _ffn_forward(r1, r1b, pb[0], w, 0, ready)

    ready("attn", r2b)
    qkv = _mm_rows([(r2b, w["attn_w_qkv"], False)], out_dtype=BF16, name="attn_qkv")
    bias = _bias_blocks(w["attn_rel_bias"])
    attn = _attn_fwd(qkv, bias)
    z3, r3, r3b = _proj_ln(r2, attn, w["attn_w_o"], w["ln_mix_g"][1], w["ln_mix_b"][1], name="attn_out_ln")
    sv1, _, _ = _ffn_forward(r3, r3b, pb[1], w, 1, ready)

    dz4, dz4b, grads["ln_ffn_g"][1], grads["ln_ffn_b"][1], loss = _ln_bwd(
        sv1["z2"], w["ln_ffn_g"][1], w["ln_ffn_b"][1], target, loss_head=True, name="loss_ln_bwd")
    dr3 = _ffn_backward(sv1, dz4, dz4b, pb[1], w, 1, grads)
    dz3, dz3b, grads["ln_mix_g"][1], grads["ln_mix_b"][1], _ = _ln_bwd(
        z3, w["ln_mix_g"][1], w["ln_mix_b"][1], dr3, dep=emit("ffn1", grads), name="ln_mix_bwd1")
    grads["attn_w_o"] = _wgrad(attn, dz3b, name="d_attn_w_o")
    dattn = _mm_rows([(dz3b, w["attn_w_o"], True)], out_dtype=BF16, name="d_attn")
    dq, dk, dv, dbias = _attn_bwd(qkv, bias, dattn)
    grads["attn_rel_bias"] = _bias_blocks_grad(dbias)
    dqkv = jnp.concatenate([dq, dk, dv], axis=1)
    grads["attn_w_qkv"] = _wgrad(r2b, dqkv, tn=768, piece=3 * D_MODEL // N_DEV, name="d_attn_w_qkv")
    dr2 = _mm_rows([(dqkv, w["attn_w_qkv"], True)], add=dz3, add_scale=ALPHA, dep=emit("attn", grads), name="dr2")

    dz2, dz2b, grads["ln_ffn_g"][0], grads["ln_ffn_b"][0], _ = _ln_bwd(
        sv0["z2"], w["ln_ffn_g"][0], w["ln_ffn_b"][0], dr2, name="ln_ffn_bwd0")
    dr1 = _ffn_backward(sv0, dz2, dz2b, pb[0], w, 0, grads)
    dz1, dz1b, grads["ln_mix_g"][0], grads["ln_mix_b"][0], _ = _ln_bwd(
        z1, w["ln_mix_g"][0], w["ln_mix_b"][0], dr1, dep=emit("ffn0", grads), name="ln_mix_bwd0")
    grads["mix_w_out"] = _wgrad(ycat, dz1b, name="d_mix_w_out")
    dycat = _mm_rows([(dz1b, w["mix_w_out"], True)], name="d_ycat")
    du, g_pw, g_ps, g_cw, g_cb, g_cg, g_cbb = _mixer_bwd(u, dpool, dycat, w["pool_w"], w["pool_scale"],
                                                         w["conv_dw_w"], w["conv_dw_b"], w["conv_ln_g"],
                                                         w["conv_ln_b"])
    grads["mix_w_in"] = _wgrad(xb, du, tn=768, piece=3 * D_POOL // N_DEV, name="d_mix_w_in")
    grads["conv_dw_w"] = g_cw
    grad_x = _mm_rows([(du, w["mix_w_in"], True)], add=dz1, add_scale=ALPHA, dep=emit("mix", grads), name="grad_x")
    grads.update(pool_w=g_pw, pool_scale=g_ps[0], conv_dw_w=g_cw, conv_dw_b=g_cb[0], conv_ln_g=g_cg[0],
                 conv_ln_b=g_cbb[0])
    for kname in ("ln_ffn_g", "ln_ffn_b", "ln_mix_g", "ln_mix_b"):
        grads[kname] = [a[0] for a in grads[kname]]
    return loss[0, 0], grad_x, grads


_HBM = pl.BlockSpec(memory_space=pltpu.HBM)
_SEM = pl.BlockSpec(memory_space=pltpu.SEMAPHORE)
_EFFECT = pltpu.SideEffectType.DATAFLOW_SIDE_EFFECTING


def _slot(ref, place, shape, k):
    if place in ("stack", "pieces"):
        return ref.at[k]
    ax = place[1]
    n = shape[ax]
    return ref.at[(slice(None),) * ax + (pl.ds(pl.multiple_of(k * n, n), n),)]


def _result_shape(buf, place):
    if place == "stack":
        return (N_DEV,) + buf.shape
    if place == "pieces":
        return buf.shape
    return tuple(s * N_DEV if i == place[1] else s for i, s in enumerate(buf.shape))


def _peers(x, y, c):
    for d in range(1, N_DEV):
        px, py, pc = x ^ ((d >> 2) & 1), y ^ ((d >> 1) & 1), c ^ (d & 1)
        yield d, (px, py, pc), 4 * px + 2 * py + pc


def _exchange_start(bufs, places, after, *, name):
    nb = len(bufs)
    lands = [lax.empty(_result_shape(b, p_), b.dtype) for b, p_ in zip(bufs, places)]
    has_after = after is not None

    def body(*refs):
        srcs, dsts = refs[:nb], refs[nb:2 * nb]
        outs = refs[2 * nb + has_after:]
        send_sems, recv_sems, token = outs[0], outs[1], outs[2 + 2 * nb]
        x, y, c = lax.axis_index("x"), lax.axis_index("y"), lax.axis_index("c")
        me = 4 * x + 2 * y + c
        for b in range(nb):
            for d, dev, peer in _peers(x, y, c):
                pltpu.make_async_remote_copy(
                    src_ref=srcs[b].at[peer] if places[b] == "pieces" else srcs[b],
                    dst_ref=_slot(dsts[b], places[b], bufs[b].shape, me),
                    send_sem=send_sems.at[b * N_DEV + d], recv_sem=recv_sems.at[b * N_DEV + d],
                    device_id=dev, device_id_type=pl.DeviceIdType.MESH).start()
            pltpu.make_async_copy(srcs[b].at[me] if places[b] == "pieces" else srcs[b],
                                  _slot(dsts[b], places[b], bufs[b].shape, me), recv_sems.at[b * N_DEV]).start()
        token[...] = jnp.zeros_like(token)

    sems = pltpu.SemaphoreType.DMA((nb * N_DEV,))
    ins = [pltpu.with_memory_space_constraint(a, pltpu.HBM) for a in list(bufs) + lands]
    out = pl.pallas_call(
        body,
        out_shape=(sems, sems, *[pltpu.HBM(a.shape, a.dtype) for a in ins], jax.ShapeDtypeStruct((8, 128), F32)),
        in_specs=[_HBM] * (2 * nb) + ([pl.BlockSpec(memory_space=pl.ANY)] if has_after else []),
        out_specs=(_SEM, _SEM, *[_HBM] * (2 * nb), pl.BlockSpec(memory_space=pltpu.VMEM)),
        input_output_aliases={i: 2 + i for i in range(2 * nb)},
        compiler_params=pltpu.CompilerParams(has_side_effects=_EFFECT),
        name=name,
    )(*ins, *([after] if has_after else []))
    return dict(send=out[0], recv=out[1], srcs=out[2:2 + nb], lands=out[2 + nb:2 + 2 * nb], token=out[-1],
                places=places)


def _exchange_wait(h, after, *, name):
    nb = len(h["srcs"])
    places = h["places"]
    shapes = [a.shape for a in h["srcs"]]

    def body(*refs):
        srcs, dsts, send_sems, recv_sems = refs[:nb], refs[nb:2 * nb], refs[2 * nb], refs[2 * nb + 1]
        x, y, c = lax.axis_index("x"), lax.axis_index("y"), lax.axis_index("c")
        me = 4 * x + 2 * y + c
        for b in range(nb):
            pieces = places[b] == "pieces"
            for d, dev, peer in _peers(x, y, c):
                cp = pltpu.make_async_remote_copy(
                    src_ref=srcs[b].at[peer] if pieces else srcs[b],
                    dst_ref=_slot(dsts[b], places[b], shapes[b], peer),
                    send_sem=send_sems.at[b * N_DEV + d], recv_sem=recv_sems.at[b * N_DEV + d],
                    device_id=dev, device_id_type=pl.DeviceIdType.MESH)
                cp.wait_send()
                cp.wait_recv()
            pltpu.make_async_copy(srcs[b].at[me] if pieces else srcs[b], _slot(dsts[b], places[b], shapes[b], me),
                                  recv_sems.at[b * N_DEV]).wait()

    ins = list(h["srcs"]) + list(h["lands"])
    out = pl.pallas_call(
        body,
        out_shape=tuple(pltpu.HBM(a.shape, a.dtype) for a in ins),
        in_specs=[_HBM] * (2 * nb) + [_SEM, _SEM, pl.BlockSpec(memory_space=pl.ANY)],
        out_specs=tuple([_HBM] * (2 * nb)),
        input_output_aliases={i: i for i in range(2 * nb)},
        compiler_params=pltpu.CompilerParams(has_side_effects=_EFFECT),
        name=name,
    )(*ins, h["send"], h["recv"], after)
    return out[nb:]


def _adamw(recv, w, m, v, *, layer=0, into=None, name):
    L, R, C = w.shape
    tr = R
    for cand in (512, 256, 128, 64, 32, 16):
        if R % cand == 0 and cand * C * 4 <= 2 * 1024 * 1024:
            tr = cand
            break
    c1 = 1.0 - ADAM_B1 ** ADAM_STEP
    c2 = 1.0 - ADAM_B2 ** ADAM_STEP

    def body(r_ref, w_ref, m_ref, v_ref, *rest):
        g_ref, d_ref, mo_ref, vo_ref = rest[-4:]
        g = r_ref[0].astype(F32)
        for i in range(1, N_DEV):
            g = g + r_ref[i].astype(F32)
        m_new = ADAM_B1 * m_ref[...] + (1.0 - ADAM_B1) * g
        v_new = ADAM_B2 * v_ref[...] + (1.0 - ADAM_B2) * (g * g)
        m_hat = m_new / c1
        v_hat = v_new / c2
        g_ref[...] = g
        d_ref[...] = -ADAM_LR * (m_hat / (jnp.sqrt(v_hat) + ADAM_EPS) + ADAM_WD * w_ref[...])
        mo_ref[...] = m_new
        vo_ref[...] = v_new

    row = pl.BlockSpec((None, tr, C), lambda i: (layer, i, 0))
    others = [] if into is None else list(into)
    return pl.pallas_call(
        body,
        out_shape=[jax.ShapeDtypeStruct((L, R, C), F32)] * 4,
        grid=(R // tr,),
        in_specs=[pl.BlockSpec((N_DEV, tr, C), lambda i: (0, i, 0)), row, row, row]
        + [pl.BlockSpec(memory_space=pl.ANY)] * len(others),
        out_specs=[row] * 4,
        input_output_aliases={4 + k: k for k in range(len(others))},
        compiler_params=_cparams(("parallel",)),
        name=name,
    )(recv, w, m, v, *others)


def _ffn_groups(l):
    return ((f"up{l}", (("ffn_w_up", l, BF16, "stack"), ("ffn_dw_w", l, F32, "stack"))),
            (f"dn{l}", (("ffn_w_down", l, BF16, ("axis", 0)), ("ple_w_gate", l, BF16, ("axis", 0)),
                        ("ple_w_proj", l, BF16, ("axis", 1)))))


_GATHER_GROUPS = (
    ("mix", (("mix_w_in", 0, BF16, "stack"), ("conv_dw_w", 0, F32, "stack"))),
    ("mixo", (("mix_w_out", 0, BF16, ("axis", 0)),)),
    *_ffn_groups(0),
    ("attn", (("attn_w_qkv", 0, BF16, ("axis", 1)), ("attn_w_o", 0, BF16, ("axis", 0)))),
    *_ffn_groups(1))
_SHARDED = ("mix_w_in", "conv_dw_w", "mix_w_out", "attn_w_qkv", "attn_w_o", "ffn_w_up", "ffn_dw_w", "ffn_w_down",
            "ple_w_gate", "ple_w_proj")
_REPLICATED = ("pool_w", "pool_scale", "conv_dw_b", "conv_ln_g", "conv_ln_b", "attn_rel_bias", "ln_mix_g",
               "ln_mix_b", "ffn_dw_b", "ple_b_gate", "ln_ffn_g", "ln_ffn_b")


def _pack_rows(parts, row_mult, dtype):
    lead = parts[0].shape[:-1]
    flat = jnp.concatenate([a.astype(dtype) for a in parts], axis=-1)
    n = flat.shape[-1]
    unit = row_mult * LANES
    padded = -(-n // unit) * unit
    flat = jnp.pad(flat, [(0, 0)] * len(lead) + [(0, padded - n)])
    return flat.reshape(lead + (padded // LANES, LANES))


def _unpack(flat2d, shapes):
    flat = flat2d.reshape(-1)
    out, o = [], 0
    for s in shapes:
        n = math.prod(s)
        out.append(flat[o:o + n].reshape(s))
        o += n
    return out


def _full_from_shards(g, axis):
    parts = jnp.moveaxis(g, 0, axis)
    shp = list(g.shape[1:])
    shp[axis] *= g.shape[0]
    return parts.reshape(shp)


def _pieces_from_full(full, axis, k=N_DEV):
    shp = list(full.shape)
    n = shp[axis] // k
    t = full.reshape(shp[:axis] + [k, n] + shp[axis + 1:])
    return jnp.moveaxis(t, axis, 0)


def kernel(x, p, mix_w_in, pool_w, pool_scale, conv_dw_w, conv_dw_b, conv_ln_g, conv_ln_b, mix_w_out, attn_w_qkv, attn_rel_bias, attn_w_o, ln_mix_g, ln_mix_b, ffn_w_up, ffn_dw_w, ffn_dw_b, ffn_w_down, ple_w_proj, ple_w_gate, ple_b_gate, ln_ffn_g, ln_ffn_b, loss_target, m_mix_w_in, m_pool_w, m_pool_scale, m_conv_dw_w, m_conv_dw_b, m_conv_ln_g, m_conv_ln_b, m_mix_w_out, m_attn_w_qkv, m_attn_rel_bias, m_attn_w_o, m_ln_mix_g, m_ln_mix_b, m_ffn_w_up, m_ffn_dw_w, m_ffn_dw_b, m_ffn_w_down, m_ple_w_proj, m_ple_w_gate, m_ple_b_gate, m_ln_ffn_g, m_ln_ffn_b, v_mix_w_in, v_pool_w, v_pool_scale, v_conv_dw_w, v_conv_dw_b, v_conv_ln_g, v_conv_ln_b, v_mix_w_out, v_attn_w_qkv, v_attn_rel_bias, v_attn_w_o, v_ln_mix_g, v_ln_mix_b, v_ffn_w_up, v_ffn_dw_w, v_ffn_dw_b, v_ffn_w_down, v_ple_w_proj, v_ple_w_gate, v_ple_b_gate, v_ln_ffn_g, v_ln_ffn_b):
    a = dict(locals())
    sh_names = list(_SHARDED)
    names = sh_names + list(_REPLICATED)
    wts = {n: a[n] for n in names}
    mom = {n: a["m_" + n] for n in names}
    var = {n: a["v_" + n] for n in names}

    gather = {}
    token = None
    for group, items in _GATHER_GROUPS:
        gather[group] = _exchange_start([wts[n][l].astype(dt) for n, l, dt, _ in items], [pl_ for *_, pl_ in items],
                                        token, name="gather_start_" + group)
        token = gather[group]["token"]

    w = dict(pool_w=pool_w[0], pool_scale=pool_scale[0], conv_dw_b=conv_dw_b[0], conv_ln_g=conv_ln_g[0],
             conv_ln_b=conv_ln_b[0], attn_rel_bias=attn_rel_bias[0], ln_mix_g=ln_mix_g, ln_mix_b=ln_mix_b,
             ffn_dw_b=ffn_dw_b, ple_b_gate=ple_b_gate, ln_ffn_g=ln_ffn_g, ln_ffn_b=ln_ffn_b)
    for n in ("ffn_up_g", "ffn_up_v", "ffn_dw_w", "ffn_w_down", "ple_w_gate", "ple_w_proj"):
        w[n] = [None, None]

    def ready(group, after):
        got = _exchange_wait(gather[group], token if after is None else after, name="gather_wait_" + group)
        if group == "mix":
            w["mix_w_in"] = _join_shards(got[0], name="join_mix_w_in")
            w["conv_dw_w"] = _full_from_shards(got[1], 1)
        elif group == "mixo":
            (w["mix_w_out"],) = got
        elif group == "attn":
            w["attn_w_qkv"], w["attn_w_o"] = got
        elif group[:2] == "up":
            l = int(group[2])
            w["ffn_up_g"][l] = _join_shards(got[0], k=N_DEV // 2, part=0, name=f"join_ffn_up_g{l}")
            w["ffn_up_v"][l] = _join_shards(got[0], k=N_DEV // 2, part=1, name=f"join_ffn_up_v{l}")
            w["ffn_dw_w"][l] = _full_from_shards(got[1], 1)
        else:
            l = int(group[2])
            w["ffn_w_down"][l], w["ple_w_gate"][l], w["ple_w_proj"][l] = got

    scatter = {}

    def emit(group, gr):
        if group[:3] == "ffn":
            l = int(group[3])
            pieces = [jnp.concatenate([gr["ffn_up_g"][l], gr["ffn_up_v"][l]]),
                      _pieces_from_full(gr["ffn_dw_w"][l], 1), _pieces_from_full(gr["ffn_w_down"][l], 0),
                      _pieces_from_full(gr["ple_w_gate"][l], 0), gr["ple_w_proj"][l]]
        elif group == "attn":
            pieces = [gr["attn_w_qkv"], _pieces_from_full(gr["attn_w_o"], 0)]
        else:
            pieces = [gr["mix_w_in"], _pieces_from_full(gr["conv_dw_w"], 1), _pieces_from_full(gr["mix_w_out"], 0)]
        scatter[group] = _exchange_start([a.astype(BF16) for a in pieces], ["pieces"] * len(pieces), None,
                                         name="grad_start_" + group)
        return scatter[group]["token"]

    loss_part, grad_x, gr = _local_step(x[0], p[:, 0], loss_target[0], w, ready, emit)
    loss = lax.psum(loss_part, ("x", "y", "c"))

    gfull = dict(
        pool_w=gr["pool_w"][None], pool_scale=gr["pool_scale"][None], conv_dw_b=gr["conv_dw_b"][None],
        conv_ln_g=gr["conv_ln_g"][None], conv_ln_b=gr["conv_ln_b"][None], attn_rel_bias=gr["attn_rel_bias"][None],
        ln_mix_g=jnp.stack(gr["ln_mix_g"]), ln_mix_b=jnp.stack(gr["ln_mix_b"]), ffn_dw_b=jnp.stack(gr["ffn_dw_b"]),
        ple_b_gate=jnp.stack(gr["ple_b_gate"]), ln_ffn_g=jnp.stack(gr["ln_ffn_g"]),
        ln_ffn_b=jnp.stack(gr["ln_ffn_b"]))
    rep_send = _pack_rows([gfull[n].reshape(-1) for n in _REPLICATED], 8, F32)
    rep_handle = _exchange_start([rep_send], ["stack"], None, name="grad_start_replicated")

    group_weights = {"ffn1": (("ffn_w_up", 1), ("ffn_dw_w", 1), ("ffn_w_down", 1), ("ple_w_gate", 1), ("ple_w_proj", 1)),
                     "attn": (("attn_w_qkv", 0), ("attn_w_o", 0)),
                     "ffn0": (("ffn_w_up", 0), ("ffn_dw_w", 0), ("ffn_w_down", 0), ("ple_w_gate", 0), ("ple_w_proj", 0)),
                     "mix": (("mix_w_in", 0), ("conv_dw_w", 0), ("mix_w_out", 0))}
    updated = {}
    after = grad_x
    for group in ("ffn1", "attn", "ffn0", "mix"):
        recv = _exchange_wait(scatter[group], after, name="grad_wait_" + group)
        for (n, l), r in zip(group_weights[group], recv):
            updated[n] = _adamw(r, wts[n], mom[n], var[n], layer=l, into=updated.get(n), name=f"adamw_{n}{l}")
            after = updated[n][0]
    res = [{n: updated[n][k] for n in sh_names} for k in range(4)]
    (rep_recv,) = _exchange_wait(rep_handle, after, name="grad_wait_replicated")

    def flat_state(d):
        return _pack_rows([d[n].reshape(-1) for n in _REPLICATED], 8, F32)[None]

    rep_out = _adamw(rep_recv, flat_state(wts), flat_state(mom), flat_state(var), name="adamw_replicated")
    for k in range(4):
        for n, arr in zip(_REPLICATED, _unpack(rep_out[k][0], [wts[n].shape for n in _REPLICATED])):
            res[k][n] = arr
    order = ["mix_w_in", "pool_w", "pool_scale", "conv_dw_w", "conv_dw_b", "conv_ln_g", "conv_ln_b", "mix_w_out",
             "attn_w_qkv", "attn_rel_bias", "attn_w_o", "ln_mix_g", "ln_mix_b", "ffn_w_up", "ffn_dw_w", "ffn_dw_b",
             "ffn_w_down", "ple_w_proj", "ple_w_gate", "ple_b_gate", "ln_ffn_g", "ln_ffn_b"]
    outs = [loss, grad_x[None]]
    for k in range(4):
        outs += [res[k][n] for n in order]
    return tuple(outs)
```

```python
import functools
import math

import jax
import jax.numpy as jnp
from jax import lax
from jax.experimental import pallas as pl
from jax.experimental.pallas import tpu as pltpu

F32 = jnp.float32
BF16 = jnp.bfloat16

N_DEV = 8
D_MODEL = 1024
D_POOL = 512
D_CONV = 512
POOL_WINDOWS = (2, 4, 8, 16)
POOL_GROUP = 128
CONV_KERNEL = 31
CHUNK = 64
HEAD_DIM = 64
N_HEADS = 16
LEFT_CHUNKS = 8
BAND = (LEFT_CHUNKS + 1) * CHUNK
MAX_REL = 256
D_FF = 2816
PLE_DIM = 256
ALPHA = 4.0 ** 0.25
LN_EPS = 1e-5
NEG_INF = -1e30
ADAM_LR, ADAM_B1, ADAM_B2, ADAM_EPS, ADAM_WD, ADAM_STEP = 0.001, 0.9, 0.999, 1e-08, 0.01, 10

Q_BLOCK = 4 * CHUNK
KV_PAD = LEFT_CHUNKS * CHUNK
KV_SPAN = KV_PAD + Q_BLOCK
CONV_HALO = 32
FFN_HALO = 16
SUB_ROWS, SUB_LANES = 64, 128
LANES = 1024
VMEM_LIMIT = 56 * 1024 * 1024


def _cparams(sem=None):
    return pltpu.CompilerParams(dimension_semantics=sem, vmem_limit_bytes=VMEM_LIMIT)


def _tile(dim, pref):
    if dim <= pref:
        return dim
    t = pref - pref % 128
    while t >= 128:
        if dim % t == 0:
            return t
        t -= 128
    return dim


def _sigmoid(x):
    return 1.0 / (1.0 + jnp.exp(-x))


def _bdot(a, b, dn=(((1,), (0,)), ((), ()))):
    return lax.dot_general(a.astype(BF16), b.astype(BF16), dn, preferred_element_type=F32)


WHOLE = (0, 1)
NT = (((1,), (1,)), ((), ()))
TN = (((0,), (0,)), ((), ()))


def _wgrad(a, b, *, tm=1024, tn=1024, tk=1024, piece=None, part=(0, 1), into=None, name):
    K, M = a.shape
    kb, N = b.shape
    assert K == kb, (a.shape, b.shape)
    tm, tn, tk = _tile(M, tm), _tile(N, tn), _tile(K, tk)
    nk = K // tk
    per = 1 if piece is None else tn // piece
    assert piece is None or tn == per * piece

    def body(a_ref, b_ref, *rest):
        o_ref, acc = rest[-2:]
        k = pl.program_id(2)

        @pl.when(k == 0)
        def _():
            acc[...] = jnp.zeros_like(acc)

        acc[...] += _bdot(a_ref[...], b_ref[...], TN)

        @pl.when(k == nk - 1)
        def _():
            if piece is None:
                o_ref[...] = acc[...].astype(BF16)
            else:
                for s in range(per):
                    o_ref[s] = acc[:, s * piece:(s + 1) * piece].astype(BF16)

    if piece is None:
        first = part[0] * (M // tm)
        out_shape = (part[1] * M, N)
        out_spec = pl.BlockSpec((tm, tn), lambda i, j, k: (first + i, j))
    else:
        out_shape, out_spec = (N // piece, M, piece), pl.BlockSpec((per, tm, piece), lambda i, j, k: (j, i, 0))
    others = [] if into is None else [into]
    return pl.pallas_call(
        body,
        out_shape=jax.ShapeDtypeStruct(out_shape, BF16),
        grid=(M // tm, N // tn, nk),
        in_specs=[pl.BlockSpec((tk, tm), lambda i, j, k: (k, i)), pl.BlockSpec((tk, tn), lambda i, j, k: (k, j))]
        + [pl.BlockSpec(memory_space=pl.ANY)] * len(others),
        out_specs=out_spec,
        input_output_aliases={2: 0} if others else {},
        scratch_shapes=[pltpu.VMEM((tm, tn), F32)],
        compiler_params=_cparams(("parallel", "parallel", "arbitrary")),
        name=name,
    )(a, b, *others)


def _mm_rows(pairs, *, add=None, add_scale=1.0, out_dtype=F32, tm=256, dep=None, name):
    M = pairs[0][0].shape[0]
    n = len(pairs)
    has_add = add is not None
    w_rows = [w_.shape[0] // part[1] for _, w_, _, part in pairs]
    N = w_rows[0] if pairs[0][2] else pairs[0][1].shape[1]

    def body(*refs):
        o_ref = refs[-1]
        acc = None
        for i, (_, _, tr, _) in enumerate(pairs):
            part = _bdot(refs[2 * i][...], refs[2 * i + 1][...], NT if tr else (((1,), (0,)), ((), ())))
            acc = part if acc is None else acc + part
        if has_add:
            acc = acc + add_scale * refs[2 * n][...]
        o_ref[...] = acc.astype(out_dtype)

    in_specs, args = [], []
    for (a, w_, _, part), rows in zip(pairs, w_rows):
        in_specs += [pl.BlockSpec((tm, a.shape[1]), lambda i: (i, 0)),
                     pl.BlockSpec((rows, w_.shape[1]), functools.partial(lambda i, j: (j, 0), j=part[0]))]
        args += [a, w_]
    if has_add:
        in_specs.append(pl.BlockSpec((tm, N), lambda i: (i, 0)))
        args.append(add)
    if dep is not None:
        in_specs.append(pl.BlockSpec(memory_space=pl.ANY))
        args.append(dep)
    return pl.pallas_call(
        body,
        out_shape=jax.ShapeDtypeStruct((M, N), out_dtype),
        grid=(M // tm,),
        in_specs=in_specs,
        out_specs=pl.BlockSpec((tm, N), lambda i: (i, 0)),
        compiler_params=_cparams(("parallel",)),
        name=name,
    )(*args)


def _layer_norm_rows(z, g, b):
    mu = jnp.mean(z, axis=-1, keepdims=True)
    zc = z - mu
    var = jnp.mean(zc * zc, axis=-1, keepdims=True)
    return zc * lax.rsqrt(var + LN_EPS) * g + b


def _proj_ln(res, a, w, ln_g, ln_b, *, ple=None, ts=256, name):
    S, D = res.shape
    ka = a.shape[1]
    has_ple = ple is not None
    row = lambda i: (i, 0)
    fix = lambda i: (0, 0)

    def body(*refs):
        if has_ple:
            (res_ref, a_ref, w_ref, g_ref, b_ref, wg_ref, bg_ref, p_ref, wp_ref, z_ref, r_ref, rb_ref, gate_ref,
             proj_ref, acc) = refs
        else:
            res_ref, a_ref, w_ref, g_ref, b_ref, z_ref, r_ref, rb_ref, acc = refs
        acc[...] = _bdot(a_ref[...], w_ref[...])
        if has_ple:
            gate_ref[...] = _bdot(res_ref[...], wg_ref[...])
            proj_ref[...] = _bdot(p_ref[...], wp_ref[...])
        for r0 in range(0, ts, LN_ROWS):
            rows = pl.ds(r0, LN_ROWS)
            z = ALPHA * res_ref[rows, :] + acc[rows, :]
            if has_ple:
                gate = _sigmoid(gate_ref[rows, :] + bg_ref[...])
                gate_ref[rows, :] = gate
                z = z + gate * proj_ref[rows, :]
            z_ref[rows, :] = z
            r = _layer_norm_rows(z, g_ref[...], b_ref[...])
            r_ref[rows, :] = r
            rb_ref[rows, :] = r.astype(BF16)

    in_specs = [pl.BlockSpec((ts, D), row), pl.BlockSpec((ts, ka), row), pl.BlockSpec((ka, D), fix),
                pl.BlockSpec((1, D), fix), pl.BlockSpec((1, D), fix)]
    args = [res, a, w, ln_g.reshape(1, D), ln_b.reshape(1, D)]
    out_dtypes = [F32, F32, BF16]
    if has_ple:
        wg, bg, p, wp = ple
        in_specs += [pl.BlockSpec((D, D), fix), pl.BlockSpec((1, D), fix), pl.BlockSpec((ts, PLE_DIM), row),
                     pl.BlockSpec((PLE_DIM, D), fix)]
        args += [wg, bg.reshape(1, D), p, wp]
        out_dtypes += [F32, F32]
    return pl.pallas_call(
        body,
        out_shape=[jax.ShapeDtypeStruct((S, D), dt) for dt in out_dtypes],
        grid=(S // ts,),
        in_specs=in_specs,
        out_specs=[pl.BlockSpec((ts, D), row)] * len(out_dtypes),
        scratch_shapes=[pltpu.VMEM((ts, D), F32)],
        compiler_params=_cparams(("parallel",)),
        name=name,
    )(*args)


CONV_ROWS = 32
LN_ROWS = 16


def _shifted_copies(src, dst, rows):
    for b in range(1, 8):
        for c0 in range(0, src.shape[1], SUB_LANES):
            ln = pl.ds(c0, SUB_LANES)
            for r0 in range(0, rows, SUB_ROWS):
                rc = min(SUB_ROWS, rows - r0)
                dst[b - 1, pl.ds(r0, rc), ln] = src[pl.ds(r0 + b, rc), ln]


def _rows_at(src, copies, off, n, ln):
    b = off % 8
    return src[pl.ds(off, n), ln] if b == 0 else copies[b - 1, pl.ds(off - b, n), ln]


def _conv31(stg, gsh, cw_ref, cb_ref, out, rows, first_off):
    for c0 in range(0, D_CONV, SUB_LANES):
        ln = pl.ds(c0, SUB_LANES)
        for r0 in range(0, rows, CONV_ROWS):
            acc = jnp.zeros((CONV_ROWS, SUB_LANES), F32) + cb_ref[:, ln]
            for k in range(CONV_KERNEL):
                acc = acc + cw_ref[k:k + 1, ln] * _rows_at(stg, gsh, first_off + k + r0, CONV_ROWS, ln)
            out[pl.ds(r0, CONV_ROWS), ln] = acc


def _mixer_fwd(u, pool_w, pool_scale, conv_w, conv_b, cln_g, cln_b, *, ts=256):
    S = u.shape[0]
    hb = CONV_HALO
    nh = ts // hb

    def body(u_ref, uh_ref, pw_ref, ps_ref, cw_ref, cb_ref, g_ref, b_ref, y_ref, d_ref, sta, stg, gsh, hcs):
        i = pl.program_id(0)
        first = i == 0
        sta[pl.ds(0, hb), :] = jnp.where(first, 0.0, uh_ref[:, 0:D_POOL])
        sta[pl.ds(hb, ts), :] = u_ref[:, 0:D_POOL]
        glu_h = uh_ref[:, D_POOL:D_POOL + D_CONV] * _sigmoid(uh_ref[:, D_POOL + D_CONV:])
        stg[pl.ds(0, hb), :] = jnp.where(first, 0.0, glu_h)
        stg[pl.ds(hb, ts), :] = u_ref[:, D_POOL:D_POOL + D_CONV] * _sigmoid(u_ref[:, D_POOL + D_CONV:])

        pos = (i * ts + lax.broadcasted_iota(jnp.int32, (ts, 1), 0) + 1).astype(F32)
        for g, w in enumerate(POOL_WINDOWS):
            lanes = pl.ds(g * POOL_GROUP, POOL_GROUP)
            a_g = sta[pl.ds(hb, ts), lanes]
            s = a_g
            for j in range(1, w):
                s = s + sta[pl.ds(hb - j, ts), lanes]
            d_g = s / jnp.minimum(pos, float(w)) - a_g
            d_ref[:, lanes] = d_g.astype(BF16)
            y_ref[:, lanes] = (_bdot(d_g, pw_ref[g]) * ps_ref[:, lanes]).astype(BF16)

        _shifted_copies(stg, gsh, hb + ts - 8)
        _conv31(stg, gsh, cw_ref, cb_ref, hcs, ts, hb - (CONV_KERNEL - 1))
        for r0 in range(0, ts, LN_ROWS):
            rows = pl.ds(r0, LN_ROWS)
            ln = _layer_norm_rows(hcs[rows, :], g_ref[...], b_ref[...])
            y_ref[rows, D_POOL:] = (ln * _sigmoid(ln)).astype(BF16)

    fix2 = lambda i: (0, 0)
    return pl.pallas_call(
        body,
        out_shape=[jax.ShapeDtypeStruct((S, D_MODEL), BF16), jax.ShapeDtypeStruct((S, D_POOL), BF16)],
        grid=(S // ts,),
        in_specs=[pl.BlockSpec((ts, 3 * D_POOL), lambda i: (i, 0)),
                  pl.BlockSpec((hb, 3 * D_POOL), lambda i: (jnp.maximum(i * nh - 1, 0), 0)),
                  pl.BlockSpec((4, POOL_GROUP, POOL_GROUP), lambda i: (0, 0, 0)),
                  pl.BlockSpec((1, D_POOL), fix2), pl.BlockSpec((CONV_KERNEL, D_CONV), fix2),
                  pl.BlockSpec((1, D_CONV), fix2), pl.BlockSpec((1, D_CONV), fix2), pl.BlockSpec((1, D_CONV), fix2)],
        out_specs=[pl.BlockSpec((ts, D_MODEL), lambda i: (i, 0)), pl.BlockSpec((ts, D_POOL), lambda i: (i, 0))],
        scratch_shapes=[pltpu.VMEM((hb + ts, D_POOL), F32), pltpu.VMEM((hb + ts, D_CONV), F32),
                        pltpu.VMEM((7, hb + ts - 8, D_CONV), F32), pltpu.VMEM((ts, D_CONV), F32)],
        compiler_params=_cparams(("parallel",)),
        name="mixer_fwd",
    )(u, u, pool_w, pool_scale.reshape(1, D_POOL), conv_w, conv_b.reshape(1, D_CONV), cln_g.reshape(1, D_CONV),
      cln_b.reshape(1, D_CONV))


def _mixer_bwd(u, d, dycat, pool_w, pool_scale, conv_w, conv_b, cln_g, cln_b, *, ts=256):
    S = u.shape[0]
    hb = CONV_HALO
    nh = ts // hb
    n = S // ts
    te = ts + hb
    K = CONV_KERNEL

    def body(u_ref, up_ref, un_ref, d_ref, dy_ref, dyn_ref, pw_ref, ps_ref, cw_ref, cb_ref, g_ref, b_ref,
             du_ref, dpw_ref, dps_ref, dcw_ref, dcb_ref, dg_ref, db_ref, stg, std, sth, gsh, hcs, hsh):
        i = pl.program_id(0)
        first = i == 0
        last = i == n - 1

        @pl.when(first)
        def _():
            dpw_ref[...] = jnp.zeros_like(dpw_ref)
            dps_ref[...] = jnp.zeros_like(dps_ref)
            dcw_ref[...] = jnp.zeros_like(dcw_ref)
            dcb_ref[...] = jnp.zeros_like(dcb_ref)
            dg_ref[...] = jnp.zeros_like(dg_ref)
            db_ref[...] = jnp.zeros_like(db_ref)

        pos_e = (i * ts + lax.broadcasted_iota(jnp.int32, (te, 1), 0) + 1).astype(F32)
        dya = dy_ref[:, 0:D_POOL]
        dya_n = jnp.where(last, 0.0, dyn_ref[:, 0:D_POOL])
        for g, w in enumerate(POOL_WINDOWS):
            lanes = pl.ds(g * POOL_GROUP, POOL_GROUP)
            sl = slice(g * POOL_GROUP, (g + 1) * POOL_GROUP)
            pw = pw_ref[g]
            scale = ps_ref[:, lanes]
            d_g = d_ref[:, lanes]
            pre = _bdot(d_g, pw)
            dps_ref[:, lanes] += jnp.sum(dya[:, sl] * pre, axis=0, keepdims=True)
            dys = dya[:, sl] * scale
            dpw_ref[g] += _bdot(d_g, dys, TN)
            dys_e = jnp.concatenate([dys, dya_n[:, sl] * scale], axis=0)
            dd = _bdot(dys_e, pw, NT)
            std[:, lanes] = dd / jnp.minimum(pos_e, float(w))
            da = -dd[0:ts]
            for m in range(w):
                da = da + std[pl.ds(m, ts), lanes]
            du_ref[:, lanes] = da.astype(BF16)

        glu_p = up_ref[:, D_POOL:D_POOL + D_CONV] * _sigmoid(up_ref[:, D_POOL + D_CONV:])
        stg[pl.ds(0, hb), :] = jnp.where(first, 0.0, glu_p)
        bv = u_ref[:, D_POOL:D_POOL + D_CONV]
        sg = _sigmoid(u_ref[:, D_POOL + D_CONV:])
        stg[pl.ds(hb, ts), :] = bv * sg
        glu_n = un_ref[:, D_POOL:D_POOL + D_CONV] * _sigmoid(un_ref[:, D_POOL + D_CONV:])
        stg[pl.ds(hb + ts, hb), :] = jnp.where(last, 0.0, glu_n)
        _shifted_copies(stg, gsh, hb + te - 8)
        _conv31(stg, gsh, cw_ref, cb_ref, hcs, te, hb - (K - 1))

        sums = [jnp.zeros((8, D_CONV), F32) for _ in range(3)]
        for r0 in range(0, te, LN_ROWS):
            rows = pl.ds(r0, LN_ROWS)
            hc = hcs[rows, :]
            hcc = hc - jnp.mean(hc, axis=-1, keepdims=True)
            rstd = lax.rsqrt(jnp.mean(hcc * hcc, axis=-1, keepdims=True) + LN_EPS)
            xh = hcc * rstd
            ln = xh * g_ref[...] + b_ref[...]
            sl_ = _sigmoid(ln)
            if r0 < ts:
                dyb = dy_ref[rows, D_POOL:]
            else:
                dyb = jnp.where(last, 0.0, dyn_ref[pl.ds(r0 - ts, LN_ROWS), D_POOL:])
            dln = dyb * (sl_ * (1.0 + ln * (1.0 - sl_)))
            dxh = dln * g_ref[...]
            dhc = rstd * (dxh - jnp.mean(dxh, axis=-1, keepdims=True)
                          - xh * jnp.mean(dxh * xh, axis=-1, keepdims=True))
            sth[rows, :] = dhc
            if r0 < ts:
                for n_, term in enumerate((dln * xh, dln, dhc)):
                    sums[n_] = sums[n_] + jnp.sum(term.reshape(LN_ROWS // 8, 8, D_CONV), axis=0)
        dg_ref[...] += jnp.sum(sums[0], axis=0, keepdims=True)
        db_ref[...] += jnp.sum(sums[1], axis=0, keepdims=True)
        dcb_ref[...] += jnp.sum(sums[2], axis=0, keepdims=True)

        _shifted_copies(sth, hsh, te - 8)
        for c0 in range(0, D_CONV, SUB_LANES):
            ln_ = pl.ds(c0, SUB_LANES)
            for r0 in range(0, ts, CONV_ROWS):
                rows = pl.ds(r0, CONV_ROWS)
                dglu = jnp.zeros((CONV_ROWS, SUB_LANES), F32)
                for k in range(K):
                    dglu = dglu + cw_ref[k:k + 1, ln_] * _rows_at(sth, hsh, K - 1 - k + r0, CONV_ROWS, ln_)
                bv = u_ref[rows, pl.ds(D_POOL + c0, SUB_LANES)]
                sg = _sigmoid(u_ref[rows, pl.ds(D_POOL + D_CONV + c0, SUB_LANES)])
                du_ref[rows, pl.ds(D_POOL + c0, SUB_LANES)] = (dglu * sg).astype(BF16)
                du_ref[rows, pl.ds(D_POOL + D_CONV + c0, SUB_LANES)] = (dglu * bv * sg * (1.0 - sg)).astype(BF16)
            for k in range(K):
                tap = jnp.zeros((8, SUB_LANES), F32)
                for r0 in range(0, ts, CONV_ROWS):
                    prod = sth[pl.ds(r0, CONV_ROWS), ln_] * _rows_at(stg, gsh, hb - (K - 1) + k + r0, CONV_ROWS, ln_)
                    tap = tap + jnp.sum(prod.reshape(CONV_ROWS // 8, 8, SUB_LANES), axis=0)
                dcw_ref[k:k + 1, ln_] += jnp.sum(tap, axis=0, keepdims=True)

    fix2 = lambda i: (0, 0)
    prev = lambda i: (jnp.maximum(i * nh - 1, 0), 0)
    nxt = lambda i: (jnp.minimum((i + 1) * nh, S // hb - 1), 0)
    return pl.pallas_call(
        body,
        out_shape=[jax.ShapeDtypeStruct((S, 3 * D_POOL), BF16),
                   jax.ShapeDtypeStruct((4, POOL_GROUP, POOL_GROUP), F32),
                   jax.ShapeDtypeStruct((1, D_POOL), F32),
                   jax.ShapeDtypeStruct((K, D_CONV), F32),
                   jax.ShapeDtypeStruct((1, D_CONV), F32),
                   jax.ShapeDtypeStruct((1, D_CONV), F32),
                   jax.ShapeDtypeStruct((1, D_CONV), F32)],
        grid=(n,),
        in_specs=[pl.BlockSpec((ts, 3 * D_POOL), lambda i: (i, 0)),
                  pl.BlockSpec((hb, 3 * D_POOL), prev),
                  pl.BlockSpec((hb, 3 * D_POOL), nxt),
                  pl.BlockSpec((ts, D_POOL), lambda i: (i, 0)),
                  pl.BlockSpec((ts, D_MODEL), lambda i: (i, 0)),
                  pl.BlockSpec((hb, D_MODEL), nxt),
                  pl.BlockSpec((4, POOL_GROUP, POOL_GROUP), lambda i: (0, 0, 0)),
                  pl.BlockSpec((1, D_POOL), fix2), pl.BlockSpec((K, D_CONV), fix2),
                  pl.BlockSpec((1, D_CONV), fix2), pl.BlockSpec((1, D_CONV), fix2), pl.BlockSpec((1, D_CONV), fix2)],
        out_specs=[pl.BlockSpec((ts, 3 * D_POOL), lambda i: (i, 0)),
                   pl.BlockSpec((4, POOL_GROUP, POOL_GROUP), lambda i: (0, 0, 0)),
                   pl.BlockSpec((1, D_POOL), fix2), pl.BlockSpec((K, D_CONV), fix2),
                   pl.BlockSpec((1, D_CONV), fix2), pl.BlockSpec((1, D_CONV), fix2), pl.BlockSpec((1, D_CONV), fix2)],
        scratch_shapes=[pltpu.VMEM((hb + ts + hb, D_CONV), F32), pltpu.VMEM((te, D_POOL), F32),
                        pltpu.VMEM((te, D_CONV), F32), pltpu.VMEM((7, hb + te - 8, D_CONV), F32),
                        pltpu.VMEM((te, D_CONV), F32), pltpu.VMEM((7, te - 8, D_CONV), F32)],
        compiler_params=_cparams(("arbitrary",)),
        name="mixer_bwd",
    )(u, u, u, d, dycat, dycat, pool_w, pool_scale.reshape(1, D_POOL), conv_w, conv_b.reshape(1, D_CONV),
      cln_g.reshape(1, D_CONV), cln_b.reshape(1, D_CONV))


_GELU_C = math.sqrt(2.0 / math.pi)


def _gelu_parts(x):
    inner = _GELU_C * (x + 0.044715 * x * x * x)
    th = jnp.tanh(inner)
    ge = 0.5 * x * (1.0 + th)
    dge = 0.5 * (1.0 + th) + 0.5 * x * (1.0 - th * th) * (_GELU_C * (1.0 + 3.0 * 0.044715 * x * x))
    return ge, dge


def _ffn_act_fwd(gate, val, dw_w, dw_b, *, ts=256, tc=1408, name):
    S, F = gate.shape
    hb = FFN_HALO
    nh = ts // hb
    tc = _tile(F, tc)

    def body(g_ref, gh_ref, v_ref, w_ref, b_ref, h_ref, st):
        i = pl.program_id(0)
        st[pl.ds(0, hb), :] = jnp.where(i == 0, 0.0, gh_ref[...].astype(F32))
        st[pl.ds(hb, ts), :] = g_ref[...].astype(F32)
        for c0 in range(0, tc, SUB_LANES):
            ln = pl.ds(c0, SUB_LANES)
            w0, w1, w2, b = w_ref[0:1, ln], w_ref[1:2, ln], w_ref[2:3, ln], b_ref[:, ln]
            for r0 in range(0, ts, SUB_ROWS):
                gc = b + w0 * st[pl.ds(hb - 2 + r0, SUB_ROWS), ln] + w1 * st[pl.ds(hb - 1 + r0, SUB_ROWS), ln] \
                    + w2 * st[pl.ds(hb + r0, SUB_ROWS), ln]
                ge, _ = _gelu_parts(gc)
                rows = pl.ds(r0, SUB_ROWS)
                h_ref[rows, ln] = (ge * v_ref[rows, ln].astype(F32)).astype(BF16)

    return pl.pallas_call(
        body,
        out_shape=jax.ShapeDtypeStruct((S, F), BF16),
        grid=(S // ts, F // tc),
        in_specs=[pl.BlockSpec((ts, tc), lambda i, j: (i, j)),
                  pl.BlockSpec((hb, tc), lambda i, j: (jnp.maximum(i * nh - 1, 0), j)),
                  pl.BlockSpec((ts, tc), lambda i, j: (i, j)),
                  pl.BlockSpec((3, tc), lambda i, j: (0, j)),
                  pl.BlockSpec((1, tc), lambda i, j: (0, j))],
        out_specs=pl.BlockSpec((ts, tc), lambda i, j: (i, j)),
        scratch_shapes=[pltpu.VMEM((hb + ts, tc), F32)],
        compiler_params=_cparams(("parallel", "parallel")),
        name=name,
    )(gate, gate, val, dw_w, dw_b.reshape(1, F))


def _ffn_act_bwd(gate, val, dh, dw_w, dw_b, *, ts=256, tc=1408, name):
    S, F = gate.shape
    hb = FFN_HALO
    nh = ts // hb
    n = S // ts
    te = ts + hb
    tc = _tile(F, tc)

    def body(g_ref, gp_ref, gn_ref, v_ref, vn_ref, dh_ref, dhn_ref, w_ref, b_ref,
             dg_ref, dv_ref, dw_ref, db_ref, st, sd):
        i = pl.program_id(1)
        first = i == 0
        last = i == n - 1

        @pl.when(first)
        def _():
            dw_ref[...] = jnp.zeros_like(dw_ref)
            db_ref[...] = jnp.zeros_like(db_ref)

        st[pl.ds(0, hb), :] = jnp.where(first, 0.0, gp_ref[...].astype(F32))
        st[pl.ds(hb, ts), :] = g_ref[...].astype(F32)
        st[pl.ds(hb + ts, hb), :] = jnp.where(last, 0.0, gn_ref[...].astype(F32))
        for c0 in range(0, tc, SUB_LANES):
            ln = pl.ds(c0, SUB_LANES)
            w0, w1, w2, b = w_ref[0:1, ln], w_ref[1:2, ln], w_ref[2:3, ln], b_ref[:, ln]
            db_acc = jnp.zeros((8, SUB_LANES), F32)
            dw_acc = [jnp.zeros((8, SUB_LANES), F32) for _ in range(3)]
            for r0 in range(0, te, SUB_ROWS):
                rc = min(SUB_ROWS, te - r0)
                taps = [st[pl.ds(hb - 2 + k + r0, rc), ln] for k in range(3)]
                gc = b + w0 * taps[0] + w1 * taps[1] + w2 * taps[2]
                ge, dge = _gelu_parts(gc)
                if r0 < ts:
                    rows = pl.ds(r0, rc)
                    val, dh = v_ref[rows, ln].astype(F32), dh_ref[rows, ln].astype(F32)
                else:
                    val = jnp.where(last, 0.0, vn_ref[:, ln].astype(F32)[0:rc])
                    dh = jnp.where(last, 0.0, dhn_ref[:, ln].astype(F32)[0:rc])
                dgc = dh * val * dge
                sd[pl.ds(r0, rc), ln] = dgc
                if r0 < ts:
                    dv_ref[rows, ln] = (dh * ge).astype(BF16)
                    db_acc = db_acc + jnp.sum(dgc.reshape(rc // 8, 8, SUB_LANES), axis=0)
                    for k in range(3):
                        dw_acc[k] = dw_acc[k] + jnp.sum((dgc * taps[k]).reshape(rc // 8, 8, SUB_LANES), axis=0)
            db_ref[:, ln] += jnp.sum(db_acc, axis=0, keepdims=True)
            for k in range(3):
                dw_ref[k:k + 1, ln] += jnp.sum(dw_acc[k], axis=0, keepdims=True)
            for r0 in range(0, ts, SUB_ROWS):
                dgate = w0 * sd[pl.ds(2 + r0, SUB_ROWS), ln] + w1 * sd[pl.ds(1 + r0, SUB_ROWS), ln] \
                    + w2 * sd[pl.ds(r0, SUB_ROWS), ln]
                dg_ref[pl.ds(r0, SUB_ROWS), ln] = dgate.astype(BF16)

    cur = lambda j, i: (i, j)
    prev = lambda j, i: (jnp.maximum(i * nh - 1, 0), j)
    nxt = lambda j, i: (jnp.minimum((i + 1) * nh, S // hb - 1), j)
    return pl.pallas_call(
        body,
        out_shape=[jax.ShapeDtypeStruct((S, F), BF16), jax.ShapeDtypeStruct((S, F), BF16),
                   jax.ShapeDtypeStruct((3, F), F32), jax.ShapeDtypeStruct((1, F), F32)],
        grid=(F // tc, n),
        in_specs=[pl.BlockSpec((ts, tc), cur), pl.BlockSpec((hb, tc), prev), pl.BlockSpec((hb, tc), nxt),
                  pl.BlockSpec((ts, tc), cur), pl.BlockSpec((hb, tc), nxt),
                  pl.BlockSpec((ts, tc), cur), pl.BlockSpec((hb, tc), nxt),
                  pl.BlockSpec((3, tc), lambda j, i: (0, j)), pl.BlockSpec((1, tc), lambda j, i: (0, j))],
        out_specs=[pl.BlockSpec((ts, tc), cur), pl.BlockSpec((ts, tc), cur),
                   pl.BlockSpec((3, tc), lambda j, i: (0, j)), pl.BlockSpec((1, tc), lambda j, i: (0, j))],
        scratch_shapes=[pltpu.VMEM((hb + ts + hb, tc), F32), pltpu.VMEM((te, tc), F32)],
        compiler_params=_cparams(("parallel", "arbitrary")),
        name=name,
    )(gate, gate, gate, val, val, dh, dh, dw_w, dw_b.reshape(1, F))


def _ln_bwd(z, ln_g, ln_b, dout, *, loss_head=False, ts=256, dep=None, name):
    S, D = z.shape

    def body(z_ref, g_ref, b_ref, do_ref, *rest):
        dz_ref, dzb_ref, dg_ref, db_ref, loss_ref = rest[-5:]
        i = pl.program_id(0)

        @pl.when(i == 0)
        def _():
            dg_ref[...] = jnp.zeros_like(dg_ref)
            db_ref[...] = jnp.zeros_like(db_ref)
            loss_ref[...] = jnp.zeros_like(loss_ref)

        dg_acc = jnp.zeros((8, D), F32)
        db_acc = jnp.zeros((8, D), F32)
        loss_acc = jnp.zeros((1, 1), F32)
        for r0 in range(0, ts, LN_ROWS):
            rows = pl.ds(r0, LN_ROWS)
            zt = z_ref[rows, :]
            zc = zt - jnp.mean(zt, axis=-1, keepdims=True)
            rstd = lax.rsqrt(jnp.mean(zc * zc, axis=-1, keepdims=True) + LN_EPS)
            xh = zc * rstd
            if loss_head:
                err = xh * g_ref[...] + b_ref[...] - do_ref[rows, :]
                loss_acc = loss_acc + 0.5 * jnp.sum(jnp.mean(err * err, axis=-1, keepdims=True), keepdims=True)
                do = err * (1.0 / D)
            else:
                do = do_ref[rows, :]
            dg_acc = dg_acc + jnp.sum((do * xh).reshape(LN_ROWS // 8, 8, D), axis=0)
            db_acc = db_acc + jnp.sum(do.reshape(LN_ROWS // 8, 8, D), axis=0)
            dxh = do * g_ref[...]
            dz = rstd * (dxh - jnp.mean(dxh, axis=-1, keepdims=True) - xh * jnp.mean(dxh * xh, axis=-1, keepdims=True))
            dz_ref[rows, :] = dz
            dzb_ref[rows, :] = dz.astype(BF16)
        dg_ref[...] += jnp.sum(dg_acc, axis=0, keepdims=True)
        db_ref[...] += jnp.sum(db_acc, axis=0, keepdims=True)
        if loss_head:
            loss_ref[...] += loss_acc

    row = lambda i: (i, 0)
    fix = lambda i: (0, 0)
    return pl.pallas_call(
        body,
        out_shape=[jax.ShapeDtypeStruct((S, D), F32), jax.ShapeDtypeStruct((S, D), BF16),
                   jax.ShapeDtypeStruct((1, D), F32), jax.ShapeDtypeStruct((1, D), F32),
                   jax.ShapeDtypeStruct((8, 128), F32)],
        grid=(S // ts,),
        in_specs=[pl.BlockSpec((ts, D), row), pl.BlockSpec((1, D), fix), pl.BlockSpec((1, D), fix),
                  pl.BlockSpec((ts, D), row)] + ([pl.BlockSpec(memory_space=pl.ANY)] if dep is not None else []),
        out_specs=[pl.BlockSpec((ts, D), row), pl.BlockSpec((ts, D), row), pl.BlockSpec((1, D), fix),
                   pl.BlockSpec((1, D), fix), pl.BlockSpec((8, 128), fix)],
        compiler_params=_cparams(("arbitrary",)),
        name=name,
    )(z, ln_g.reshape(1, D), ln_b.reshape(1, D), dout, *([dep] if dep is not None else []))


def _ple_bwd(dz, gate, proj, *, ts=256, name):
    S, D = dz.shape

    def body(dz_ref, g_ref, p_ref, ds_ref, dp_ref, db_ref):
        @pl.when(pl.program_id(0) == 0)
        def _():
            db_ref[...] = jnp.zeros_like(db_ref)

        db_acc = jnp.zeros((8, D), F32)
        for r0 in range(0, ts, LN_ROWS):
            rows = pl.ds(r0, LN_ROWS)
            dzt = dz_ref[rows, :]
            g = g_ref[rows, :]
            ds = dzt * p_ref[rows, :] * g * (1.0 - g)
            ds_ref[rows, :] = ds.astype(BF16)
            dp_ref[rows, :] = (dzt * g).astype(BF16)
            db_acc = db_acc + jnp.sum(ds.reshape(LN_ROWS // 8, 8, D), axis=0)
        db_ref[...] += jnp.sum(db_acc, axis=0, keepdims=True)

    row = lambda i: (i, 0)
    return pl.pallas_call(
        body,
        out_shape=[jax.ShapeDtypeStruct((S, D), BF16), jax.ShapeDtypeStruct((S, D), BF16),
                   jax.ShapeDtypeStruct((1, D), F32)],
        grid=(S // ts,),
        in_specs=[pl.BlockSpec((ts, D), row)] * 3,
        out_specs=[pl.BlockSpec((ts, D), row), pl.BlockSpec((ts, D), row), pl.BlockSpec((1, D), lambda i: (0, 0))],
        compiler_params=_cparams(("arbitrary",)),
        name=name,
    )(dz, gate, proj)


HEAD_PAIR = 2 * HEAD_DIM


ATT_ROWS = 32
ATT_SCALE = HEAD_DIM ** -0.5


def _softmax_piece(s_ref, b_ref, j, rows, qb):
    s = s_ref[j, rows, :] + b_ref[j, rows, :]
    kpos = qb * Q_BLOCK + lax.broadcasted_iota(jnp.int32, (1, KV_SPAN), 1)
    s = jnp.where(kpos >= KV_PAD, s, NEG_INF)
    e = jnp.exp(s - jnp.max(s, axis=-1, keepdims=True))
    return e * (1.0 / jnp.sum(e, axis=-1, keepdims=True))


def _pad_keys(qb, k_ref, v_ref, kp, vp):
    @pl.when(qb == 0)
    def _():
        kp[pl.ds(0, KV_PAD), :] = jnp.zeros((KV_PAD, HEAD_PAIR), BF16)
        vp[pl.ds(0, KV_PAD), :] = jnp.zeros((KV_PAD, HEAD_PAIR), BF16)
        kp[pl.ds(KV_PAD, k_ref.shape[0]), :] = k_ref[...]
        vp[pl.ds(KV_PAD, v_ref.shape[0]), :] = v_ref[...]


def _attn_fwd(qkv, bias):
    S = qkv.shape[0]
    nhp = N_HEADS // 2

    def body(q_ref, k_ref, v_ref, b_ref, o_ref, kp, vp, s_scr, p_scr):
        qb = pl.program_id(1)
        _pad_keys(qb, k_ref, v_ref, kp, vp)
        span = pl.ds(pl.multiple_of(qb * Q_BLOCK, Q_BLOCK), KV_SPAN)
        kc, vc = kp[span, :], vp[span, :]
        qt = q_ref[...] * ATT_SCALE
        first = lax.broadcasted_iota(jnp.int32, (1, HEAD_PAIR), 1) < HEAD_DIM
        for j in range(2):
            s_scr[j] = _bdot(jnp.where(first if j == 0 else ~first, qt, jnp.zeros_like(qt)), kc, NT)
        outs = []
        for j in range(2):
            for r0 in range(0, Q_BLOCK, ATT_ROWS):
                rows = pl.ds(r0, ATT_ROWS)
                p_scr[j, rows, :] = _softmax_piece(s_scr, b_ref, j, rows, qb).astype(BF16)
            outs.append(_bdot(p_scr[j], vc))
        o_ref[...] = jnp.where(first, outs[0], outs[1]).astype(BF16)

    return pl.pallas_call(
        body,
        out_shape=jax.ShapeDtypeStruct((S, D_MODEL), BF16),
        grid=(nhp, S // Q_BLOCK),
        in_specs=[pl.BlockSpec((Q_BLOCK, HEAD_PAIR), lambda h, i: (i, h)),
                  pl.BlockSpec((S, HEAD_PAIR), lambda h, i: (0, nhp + h)),
                  pl.BlockSpec((S, HEAD_PAIR), lambda h, i: (0, 2 * nhp + h)),
                  pl.BlockSpec((2, Q_BLOCK, KV_SPAN), lambda h, i: (h, 0, 0))],
        out_specs=pl.BlockSpec((Q_BLOCK, HEAD_PAIR), lambda h, i: (i, h)),
        scratch_shapes=[pltpu.VMEM((KV_PAD + S, HEAD_PAIR), BF16), pltpu.VMEM((KV_PAD + S, HEAD_PAIR), BF16),
                        pltpu.VMEM((2, Q_BLOCK, KV_SPAN), F32), pltpu.VMEM((2, Q_BLOCK, KV_SPAN), BF16)],
        compiler_params=_cparams(("parallel", "arbitrary")),
        name="attn_fwd",
    )(qkv, qkv, qkv, bias)


def _attn_bwd(qkv, bias, do):
    S = qkv.shape[0]
    nhp = N_HEADS // 2
    nq = S // Q_BLOCK
    scale = HEAD_DIM ** -0.5

    def body(q_ref, k_ref, v_ref, b_ref, do_ref, dq_ref, dk_ref, dv_ref, db_ref, kp, vp, dka, dva,
             s_scr, dp_scr, p_scr, ds_scr):
        qb = pl.program_id(1)
        _pad_keys(qb, k_ref, v_ref, kp, vp)

        @pl.when(qb == 0)
        def _():
            dka[...] = jnp.zeros_like(dka)
            dva[...] = jnp.zeros_like(dva)
            db_ref[...] = jnp.zeros_like(db_ref)

        span = pl.ds(pl.multiple_of(qb * Q_BLOCK, Q_BLOCK), KV_SPAN)
        kc, vc = kp[span, :], vp[span, :]
        qt, dot = q_ref[...] * ATT_SCALE, do_ref[...]
        first = lax.broadcasted_iota(jnp.int32, (1, HEAD_PAIR), 1) < HEAD_DIM
        dqs = []
        qs = [jnp.where(first if j == 0 else ~first, qt, jnp.zeros_like(qt)) for j in range(2)]
        dos = [jnp.where(first if j == 0 else ~first, dot, jnp.zeros_like(dot)) for j in range(2)]
        for j in range(2):
            s_scr[j] = _bdot(qs[j], kc, NT)
            dp_scr[j] = _bdot(dos[j], vc, NT)
        for j in range(2):
            qj, doj = qs[j], dos[j]
            for r0 in range(0, Q_BLOCK, ATT_ROWS):
                rows = pl.ds(r0, ATT_ROWS)
                p = _softmax_piece(s_scr, b_ref, j, rows, qb)
                dp = dp_scr[j, rows, :]
                ds = p * (dp - jnp.sum(p * dp, axis=-1, keepdims=True))
                db_ref[j, rows, :] += ds
                p_scr[j, rows, :] = p.astype(BF16)
                ds_scr[j, rows, :] = ds.astype(BF16)
            dva[span, :] += _bdot(p_scr[j], doj, TN)
            dqs.append(_bdot(ds_scr[j], kc))
            dka[span, :] += _bdot(ds_scr[j], qj, TN)
        dq_ref[...] = (scale * jnp.where(first, dqs[0], dqs[1])).astype(BF16)

        @pl.when(qb == nq - 1)
        def _():
            dk_ref[...] = dka[pl.ds(KV_PAD, S), :].astype(BF16)
            dv_ref[...] = dva[pl.ds(KV_PAD, S), :].astype(BF16)

    blk = pl.BlockSpec((Q_BLOCK, HEAD_PAIR), lambda h, i: (i, h))
    col = pl.BlockSpec((S, HEAD_PAIR), lambda h, i: (0, h))
    bsp = pl.BlockSpec((2, Q_BLOCK, KV_SPAN), lambda h, i: (h, 0, 0))
    return pl.pallas_call(
        body,
        out_shape=[jax.ShapeDtypeStruct((S, D_MODEL), BF16)] * 3
        + [jax.ShapeDtypeStruct((N_HEADS, Q_BLOCK, KV_SPAN), F32)],
        grid=(nhp, nq),
        in_specs=[blk, pl.BlockSpec((S, HEAD_PAIR), lambda h, i: (0, nhp + h)),
                  pl.BlockSpec((S, HEAD_PAIR), lambda h, i: (0, 2 * nhp + h)), bsp, blk],
        out_specs=[blk, col, col, bsp],
        scratch_shapes=[pltpu.VMEM((KV_PAD + S, HEAD_PAIR), BF16), pltpu.VMEM((KV_PAD + S, HEAD_PAIR), BF16),
                        pltpu.VMEM((KV_PAD + S, HEAD_PAIR), F32), pltpu.VMEM((KV_PAD + S, HEAD_PAIR), F32),
                        pltpu.VMEM((2, Q_BLOCK, KV_SPAN), F32), pltpu.VMEM((2, Q_BLOCK, KV_SPAN), F32),
                        pltpu.VMEM((2, Q_BLOCK, KV_SPAN), BF16), pltpu.VMEM((2, Q_BLOCK, KV_SPAN), BF16)],
        compiler_params=_cparams(("parallel", "arbitrary")),
        name="attn_bwd",
    )(qkv, qkv, qkv, bias, do)


N_DIST = BAND + CHUNK - 1
N_FAR = KV_PAD + CHUNK - MAX_REL


def _shear_rows(x, towards_right):
    row = lax.broadcasted_iota(jnp.int32, (Q_BLOCK, 1), 0)
    for bit in range(Q_BLOCK.bit_length() - 1):
        step = 1 << bit
        x = jnp.where((row & step) != 0, pltpu.roll(x, step if towards_right else KV_SPAN - step, 1), x)
    return x


def _bias_blocks(rel_bias):
    H = rel_bias.shape[0]
    e = jnp.concatenate([jnp.broadcast_to(rel_bias[:, 2 * MAX_REL:], (H, N_FAR)),
                         jnp.flip(rel_bias[:, 2 * MAX_REL - (N_DIST - N_FAR):2 * MAX_REL], axis=1),
                         jnp.zeros((H, KV_SPAN - N_DIST), F32)], axis=1).reshape(H, 1, KV_SPAN)

    def body(e_ref, o_ref):
        first = pltpu.roll(jnp.broadcast_to(e_ref[...], (Q_BLOCK, KV_SPAN)), KV_SPAN - (CHUNK - 1), 1)
        x = _shear_rows(first, True)
        row = lax.broadcasted_iota(jnp.int32, (Q_BLOCK, 1), 0)
        chunk0 = row - (row & (CHUNK - 1))
        k = lax.broadcasted_iota(jnp.int32, (1, KV_SPAN), 1)
        o_ref[...] = jnp.where((k >= chunk0) & (k < chunk0 + BAND), x, NEG_INF)

    return pl.pallas_call(
        body,
        out_shape=jax.ShapeDtypeStruct((H, Q_BLOCK, KV_SPAN), F32),
        grid=(H,),
        in_specs=[pl.BlockSpec((None, 1, KV_SPAN), lambda h: (h, 0, 0))],
        out_specs=pl.BlockSpec((None, Q_BLOCK, KV_SPAN), lambda h: (h, 0, 0)),
        compiler_params=_cparams(("parallel",)),
        name="bias_blocks",
    )(e)


def _bias_blocks_grad(dblk):
    H = dblk.shape[0]

    def body(d_ref, o_ref):
        x = pltpu.roll(_shear_rows(d_ref[...], False), CHUNK - 1, 1)
        de = jnp.sum(x, axis=0, keepdims=True)
        lane = lax.broadcasted_iota(jnp.int32, de.shape, 1)
        far = jnp.sum(jnp.where(lane < N_FAR, de, 0.0), axis=-1, keepdims=True)
        o_ref[...] = jnp.where(lane == 0, far, jnp.where(lane < N_FAR, 0.0, de))

    de = pl.pallas_call(
        body,
        out_shape=jax.ShapeDtypeStruct((H, 1, KV_SPAN), F32),
        grid=(H,),
        in_specs=[pl.BlockSpec((None, Q_BLOCK, KV_SPAN), lambda h: (h, 0, 0))],
        out_specs=pl.BlockSpec((None, 1, KV_SPAN), lambda h: (h, 0, 0)),
        compiler_params=_cparams(("parallel",)),
        name="bias_grad_sum",
    )(dblk).reshape(H, KV_SPAN)
    near = jnp.flip(de[:, N_FAR:N_DIST], axis=1)
    return jnp.concatenate([jnp.zeros((H, 2 * MAX_REL - (N_DIST - N_FAR)), F32), near, de[:, 0:1]], axis=1)


def _ffn_forward(r1, r1b, p_l, w, l, ready):
    ready(f"up{l}", r1b)
    up_g = _mm_rows([(r1b, w["ffn_up_t"][l], True, (0, 2))], out_dtype=BF16, name=f"ffn_up_g{l}")
    up_v = _mm_rows([(r1b, w["ffn_up_t"][l], True, (1, 2))], out_dtype=BF16, name=f"ffn_up_v{l}")
    h = _ffn_act_fwd(up_g, up_v, w["ffn_dw_w"][l], w["ffn_dw_b"][l], name=f"ffn_act{l}")
    ready(f"dn{l}", h)
    z2, r2, r2b, gate, proj = _proj_ln(r1, h, w["ffn_w_down"][l], w["ln_ffn_g"][l], w["ln_ffn_b"][l],
                                       ple=(w["ple_w_gate"][l], w["ple_b_gate"][l], p_l, w["ple_w_proj"][l]),
                                       name=f"ffn_down_ln{l}")
    return dict(r1b=r1b, up_g=up_g, up_v=up_v, h=h, z2=z2, gate=gate, proj=proj), r2, r2b


def _ffn_backward(sv, dz2, dz2b, p_l, w, l, grads):
    r1b = sv["r1b"]
    ds, dproj, db_gate = _ple_bwd(dz2, sv["gate"], sv["proj"], name=f"ple_bwd{l}")
    dh = _mm_rows([(dz2b, w["ffn_w_down"][l], True, WHOLE)], out_dtype=BF16, name=f"ffn_dh{l}")
    dgate, dval, d_dw_w, d_dw_b = _ffn_act_bwd(sv["up_g"], sv["up_v"], dh, w["ffn_dw_w"][l], w["ffn_dw_b"][l],
                                               name=f"ffn_act_bwd{l}")
    grads["ffn_w_down"][l] = _wgrad(sv["h"], dz2b, tm=1408, name=f"d_ffn_w_down{l}")
    d_up_g = _wgrad(dgate, r1b, tm=1408, part=(0, 2), name=f"d_ffn_up_g{l}")
    grads["ffn_up_t"][l] = _wgrad(dval, r1b, tm=1408, part=(1, 2), into=d_up_g, name=f"d_ffn_up_v{l}")
    grads["ple_w_gate"][l] = _wgrad(r1b, ds, name=f"d_ple_w_gate{l}")
    grads["ple_w_proj"][l] = _wgrad(p_l, dproj, piece=D_MODEL // N_DEV, name=f"d_ple_w_proj{l}")
    grads["ffn_dw_w"][l] = d_dw_w
    grads["ffn_dw_b"][l] = d_dw_b[0]
    grads["ple_b_gate"][l] = db_gate[0]
    return _mm_rows([(ds, w["ple_w_gate"][l], True, WHOLE), (dgate, w["ffn_up_t"][l], False, (0, 2)),
                     (dval, w["ffn_up_t"][l], False, (1, 2))], add=dz2, add_scale=ALPHA, name=f"dr1_{l}")


def _local_step(x, p, target, w, ready=lambda group, after: None, emit=lambda group, grads: None):
    grads = {k: [None, None] for k in ("ffn_w_down", "ffn_up_t", "ple_w_gate", "ple_w_proj", "ffn_dw_w",
                                       "ffn_dw_b", "ple_b_gate", "ln_ffn_g", "ln_ffn_b", "ln_mix_g", "ln_mix_b")}

    xb, pb = x.astype(BF16), p.astype(BF16)
    ready("mix", None)
    u = _mm_rows([(xb, w["mix_w_in_t"], True, WHOLE)], name="mix_in")
    ycat, dpool = _mixer_fwd(u, w["pool_w"], w["pool_scale"], w["conv_dw_w"], w["conv_dw_b"], w["conv_ln_g"],
                             w["conv_ln_b"])
    ready("mixo", ycat)
    z1, r1, r1b = _proj_ln(x, ycat, w["mix_w_out"], w["ln_mix_g"][0], w["ln_mix_b"][0], name="mix_out_ln")
    sv0, r2, r2b = _ffn_forward(r1, r1b, pb[0], w, 0, ready)

    ready("attn", r2b)
    qkv = _mm_rows([(r2b, w["attn_w_qkv"], False, WHOLE)], out_dtype=BF16, name="attn_qkv")
    bias = _bias_blocks(w["attn_rel_bias"])
    attn = _attn_fwd(qkv, bias)
    z3, r3, r3b = _proj_ln(r2, attn, w["attn_w_o"], w["ln_mix_g"][1], w["ln_mix_b"][1], name="attn_out_ln")
    sv1, _, _ = _ffn_forward(r3, r3b, pb[1], w, 1, ready)

    dz4, dz4b, grads["ln_ffn_g"][1], grads["ln_ffn_b"][1], loss = _ln_bwd(
        sv1["z2"], w["ln_ffn_g"][1], w["ln_ffn_b"][1], target, loss_head=True, name="loss_ln_bwd")
    dr3 = _ffn_backward(sv1, dz4, dz4b, pb[1], w, 1, grads)
    dz3, dz3b, grads["ln_mix_g"][1], grads["ln_mix_b"][1], _ = _ln_bwd(
        z3, w["ln_mix_g"][1], w["ln_mix_b"][1], dr3, dep=emit("ffn1", grads), name="ln_mix_bwd1")
    grads["attn_w_o"] = _wgrad(attn, dz3b, name="d_attn_w_o")
    dattn = _mm_rows([(dz3b, w["attn_w_o"], True, WHOLE)], out_dtype=BF16, name="d_attn")
    dq, dk, dv, dbias = _attn_bwd(qkv, bias, dattn)
    grads["attn_rel_bias"] = _bias_blocks_grad(dbias)
    dqkv = jnp.concatenate([dq, dk, dv], axis=1)
    grads["attn_w_qkv"] = _wgrad(r2b, dqkv, tn=768, piece=3 * D_MODEL // N_DEV, name="d_attn_w_qkv")
    dr2 = _mm_rows([(dqkv, w["attn_w_qkv"], True, WHOLE)], add=dz3, add_scale=ALPHA, dep=emit("attn", grads),
                   name="dr2")

    dz2, dz2b, grads["ln_ffn_g"][0], grads["ln_ffn_b"][0], _ = _ln_bwd(
        sv0["z2"], w["ln_ffn_g"][0], w["ln_ffn_b"][0], dr2, name="ln_ffn_bwd0")
    dr1 = _ffn_backward(sv0, dz2, dz2b, pb[0], w, 0, grads)
    dz1, dz1b, grads["ln_mix_g"][0], grads["ln_mix_b"][0], _ = _ln_bwd(
        z1, w["ln_mix_g"][0], w["ln_mix_b"][0], dr1, dep=emit("ffn0", grads), name="ln_mix_bwd0")
    grads["mix_w_out"] = _wgrad(ycat, dz1b, name="d_mix_w_out")
    dycat = _mm_rows([(dz1b, w["mix_w_out"], True, WHOLE)], name="d_ycat")
    du, g_pw, g_ps, g_cw, g_cb, g_cg, g_cbb = _mixer_bwd(u, dpool, dycat, w["pool_w"], w["pool_scale"],
                                                         w["conv_dw_w"], w["conv_dw_b"], w["conv_ln_g"],
                                                         w["conv_ln_b"])
    grads["mix_w_in_t"] = _wgrad(du, xb, name="d_mix_w_in")
    grads.update(pool_w=g_pw, pool_scale=g_ps[0], conv_dw_w=g_cw, conv_dw_b=g_cb[0], conv_ln_g=g_cg[0],
                 conv_ln_b=g_cbb[0])
    for kname in ("ln_ffn_g", "ln_ffn_b", "ln_mix_g", "ln_mix_b"):
        grads[kname] = [a[0] for a in grads[kname]]
    grad_x = _mm_rows([(du, w["mix_w_in_t"], False, WHOLE)], add=dz1, add_scale=ALPHA, dep=emit("mix", grads),
                      name="grad_x")
    return loss[0, 0], grad_x, grads


_HBM = pl.BlockSpec(memory_space=pltpu.HBM)
_SEM = pl.BlockSpec(memory_space=pltpu.SEMAPHORE)
_EFFECT = pltpu.SideEffectType.DATAFLOW_SIDE_EFFECTING


def _slot(ref, place, shape, k):
    if place in ("stack", "pieces"):
        return ref.at[k]
    ax = place[1]
    n = shape[ax]
    return ref.at[(slice(None),) * ax + (pl.ds(pl.multiple_of(k * n, n), n),)]


def _result_shape(buf, place):
    if place == "stack":
        return (N_DEV,) + buf.shape
    if place == "pieces":
        return buf.shape
    return tuple(s * N_DEV if i == place[1] else s for i, s in enumerate(buf.shape))


def _peers(x, y, c):
    for d in range(1, N_DEV):
        px, py, pc = x ^ ((d >> 2) & 1), y ^ ((d >> 1) & 1), c ^ (d & 1)
        yield d, (px, py, pc), 4 * px + 2 * py + pc


def _exchange_start(bufs, places, after, *, name):
    nb = len(bufs)
    lands = [lax.empty(_result_shape(b, p_), b.dtype) for b, p_ in zip(bufs, places)]
    has_after = after is not None

    def body(*refs):
        srcs, dsts = refs[:nb], refs[nb:2 * nb]
        outs = refs[2 * nb + has_after:]
        send_sems, recv_sems, token = outs[0], outs[1], outs[2 + 2 * nb]
        x, y, c = lax.axis_index("x"), lax.axis_index("y"), lax.axis_index("c")
        me = 4 * x + 2 * y + c
        for b in range(nb):
            for d, dev, peer in _peers(x, y, c):
                pltpu.make_async_remote_copy(
                    src_ref=srcs[b].at[peer] if places[b] == "pieces" else srcs[b],
                    dst_ref=_slot(dsts[b], places[b], bufs[b].shape, me),
                    send_sem=send_sems.at[b * N_DEV + d], recv_sem=recv_sems.at[b * N_DEV + d],
                    device_id=dev, device_id_type=pl.DeviceIdType.MESH).start()
            pltpu.make_async_copy(srcs[b].at[me] if places[b] == "pieces" else srcs[b],
                                  _slot(dsts[b], places[b], bufs[b].shape, me), recv_sems.at[b * N_DEV]).start()
        token[...] = jnp.zeros_like(token)

    sems = pltpu.SemaphoreType.DMA((nb * N_DEV,))
    ins = [pltpu.with_memory_space_constraint(a, pltpu.HBM) for a in list(bufs) + lands]
    out = pl.pallas_call(
        body,
        out_shape=(sems, sems, *[pltpu.HBM(a.shape, a.dtype) for a in ins], jax.ShapeDtypeStruct((8, 128), F32)),
        in_specs=[_HBM] * (2 * nb) + ([pl.BlockSpec(memory_space=pl.ANY)] if has_after else []),
        out_specs=(_SEM, _SEM, *[_HBM] * (2 * nb), pl.BlockSpec(memory_space=pltpu.VMEM)),
        input_output_aliases={i: 2 + i for i in range(2 * nb)},
        compiler_params=pltpu.CompilerParams(has_side_effects=_EFFECT),
        name=name,
    )(*ins, *([after] if has_after else []))
    return dict(send=out[0], recv=out[1], srcs=out[2:2 + nb], lands=out[2 + nb:2 + 2 * nb], token=out[-1],
                places=places)


def _exchange_wait(h, after, *, name):
    nb = len(h["srcs"])
    places = h["places"]
    shapes = [a.shape for a in h["srcs"]]

    def body(*refs):
        srcs, dsts, send_sems, recv_sems = refs[:nb], refs[nb:2 * nb], refs[2 * nb], refs[2 * nb + 1]
        x, y, c = lax.axis_index("x"), lax.axis_index("y"), lax.axis_index("c")
        me = 4 * x + 2 * y + c
        for b in range(nb):
            pieces = places[b] == "pieces"
            for d, dev, peer in _peers(x, y, c):
                cp = pltpu.make_async_remote_copy(
                    src_ref=srcs[b].at[peer] if pieces else srcs[b],
                    dst_ref=_slot(dsts[b], places[b], shapes[b], peer),
                    send_sem=send_sems.at[b * N_DEV + d], recv_sem=recv_sems.at[b * N_DEV + d],
                    device_id=dev, device_id_type=pl.DeviceIdType.MESH)
                cp.wait_send()
                cp.wait_recv()
            pltpu.make_async_copy(srcs[b].at[me] if pieces else srcs[b], _slot(dsts[b], places[b], shapes[b], me),
                                  recv_sems.at[b * N_DEV]).wait()

    ins = list(h["srcs"]) + list(h["lands"])
    out = pl.pallas_call(
        body,
        out_shape=tuple(pltpu.HBM(a.shape, a.dtype) for a in ins),
        in_specs=[_HBM] * (2 * nb) + [_SEM, _SEM, pl.BlockSpec(memory_space=pl.ANY)],
        out_specs=tuple([_HBM] * (2 * nb)),
        input_output_aliases={i: i for i in range(2 * nb)},
        compiler_params=pltpu.CompilerParams(has_side_effects=_EFFECT),
        name=name,
    )(*ins, h["send"], h["recv"], after)
    return out[nb:]


def _adamw(recv, w, m, v, *, layer=0, into=None, name):
    L, R, C = w.shape
    tr = R
    for cand in (512, 256, 128, 64, 32, 16):
        if R % cand == 0 and cand * C * 4 <= 2 * 1024 * 1024:
            tr = cand
            break
    c1 = 1.0 - ADAM_B1 ** ADAM_STEP
    c2 = 1.0 - ADAM_B2 ** ADAM_STEP

    def body(r_ref, w_ref, m_ref, v_ref, *rest):
        g_ref, d_ref, mo_ref, vo_ref = rest[-4:]
        g = r_ref[0].astype(F32)
        for i in range(1, N_DEV):
            g = g + r_ref[i].astype(F32)
        m_new = ADAM_B1 * m_ref[...] + (1.0 - ADAM_B1) * g
        v_new = ADAM_B2 * v_ref[...] + (1.0 - ADAM_B2) * (g * g)
        m_hat = m_new / c1
        v_hat = v_new / c2
        g_ref[...] = g
        d_ref[...] = -ADAM_LR * (m_hat / (jnp.sqrt(v_hat) + ADAM_EPS) + ADAM_WD * w_ref[...])
        mo_ref[...] = m_new
        vo_ref[...] = v_new

    row = pl.BlockSpec((None, tr, C), lambda i: (layer, i, 0))
    others = [] if into is None else list(into)
    return pl.pallas_call(
        body,
        out_shape=[jax.ShapeDtypeStruct((L, R, C), F32)] * 4,
        grid=(R // tr,),
        in_specs=[pl.BlockSpec((N_DEV, tr, C), lambda i: (0, i, 0)), row, row, row]
        + [pl.BlockSpec(memory_space=pl.ANY)] * len(others),
        out_specs=[row] * 4,
        input_output_aliases={4 + k: k for k in range(len(others))},
        compiler_params=_cparams(("parallel",)),
        name=name,
    )(recv, w, m, v, *others)


_TRANSPOSED = ("mix_w_in", "ffn_w_up")


def _ffn_groups(l):
    return ((f"up{l}", (("ffn_w_up", l, BF16, ("axis", 0)), ("ffn_dw_w", l, F32, "stack"))),
            (f"dn{l}", (("ffn_w_down", l, BF16, ("axis", 0)), ("ple_w_gate", l, BF16, ("axis", 0)),
                        ("ple_w_proj", l, BF16, ("axis", 1)))))


_GATHER_GROUPS = (
    ("mix", (("mix_w_in", 0, BF16, ("axis", 0)), ("conv_dw_w", 0, F32, "stack"))),
    ("mixo", (("mix_w_out", 0, BF16, ("axis", 0)),)),
    *_ffn_groups(0),
    ("attn", (("attn_w_qkv", 0, BF16, ("axis", 1)), ("attn_w_o", 0, BF16, ("axis", 0)))),
    *_ffn_groups(1))
_SHARDED = ("mix_w_in", "conv_dw_w", "mix_w_out", "attn_w_qkv", "attn_w_o", "ffn_w_up", "ffn_dw_w", "ffn_w_down",
            "ple_w_gate", "ple_w_proj")
_REPLICATED = ("pool_w", "pool_scale", "conv_dw_b", "conv_ln_g", "conv_ln_b", "attn_rel_bias", "ln_mix_g",
               "ln_mix_b", "ffn_dw_b", "ple_b_gate", "ln_ffn_g", "ln_ffn_b")


def _pack_rows(parts, row_mult, dtype):
    lead = parts[0].shape[:-1]
    flat = jnp.concatenate([a.astype(dtype) for a in parts], axis=-1)
    n = flat.shape[-1]
    unit = row_mult * LANES
    padded = -(-n // unit) * unit
    flat = jnp.pad(flat, [(0, 0)] * len(lead) + [(0, padded - n)])
    return flat.reshape(lead + (padded // LANES, LANES))


def _unpack(flat2d, shapes):
    flat = flat2d.reshape(-1)
    out, o = [], 0
    for s in shapes:
        n = math.prod(s)
        out.append(flat[o:o + n].reshape(s))
        o += n
    return out


def _full_from_shards(g, axis):
    parts = jnp.moveaxis(g, 0, axis)
    shp = list(g.shape[1:])
    shp[axis] *= g.shape[0]
    return parts.reshape(shp)


def _pieces_from_full(full, axis, k=N_DEV):
    shp = list(full.shape)
    n = shp[axis] // k
    t = full.reshape(shp[:axis] + [k, n] + shp[axis + 1:])
    return jnp.moveaxis(t, axis, 0)


def kernel(x, p, mix_w_in, pool_w, pool_scale, conv_dw_w, conv_dw_b, conv_ln_g, conv_ln_b, mix_w_out, attn_w_qkv, attn_rel_bias, attn_w_o, ln_mix_g, ln_mix_b, ffn_w_up, ffn_dw_w, ffn_dw_b, ffn_w_down, ple_w_proj, ple_w_gate, ple_b_gate, ln_ffn_g, ln_ffn_b, loss_target, m_mix_w_in, m_pool_w, m_pool_scale, m_conv_dw_w, m_conv_dw_b, m_conv_ln_g, m_conv_ln_b, m_mix_w_out, m_attn_w_qkv, m_attn_rel_bias, m_attn_w_o, m_ln_mix_g, m_ln_mix_b, m_ffn_w_up, m_ffn_dw_w, m_ffn_dw_b, m_ffn_w_down, m_ple_w_proj, m_ple_w_gate, m_ple_b_gate, m_ln_ffn_g, m_ln_ffn_b, v_mix_w_in, v_pool_w, v_pool_scale, v_conv_dw_w, v_conv_dw_b, v_conv_ln_g, v_conv_ln_b, v_mix_w_out, v_attn_w_qkv, v_attn_rel_bias, v_attn_w_o, v_ln_mix_g, v_ln_mix_b, v_ffn_w_up, v_ffn_dw_w, v_ffn_dw_b, v_ffn_w_down, v_ple_w_proj, v_ple_w_gate, v_ple_b_gate, v_ln_ffn_g, v_ln_ffn_b):
    a = dict(locals())
    sh_names = list(_SHARDED)
    names = sh_names + list(_REPLICATED)
    wts = {n: a[n] for n in names}
    mom = {n: a["m_" + n] for n in names}
    var = {n: a["v_" + n] for n in names}

    for n in _TRANSPOSED:
        wts[n], mom[n], var[n] = (jnp.swapaxes(d[n], 1, 2) for d in (wts, mom, var))
    gather = {}
    token = None
    for group, items in _GATHER_GROUPS:
        gather[group] = _exchange_start([wts[n][l].astype(dt) for n, l, dt, _ in items], [pl_ for *_, pl_ in items],
                                        token, name="gather_start_" + group)
        token = gather[group]["token"]

    w = dict(pool_w=pool_w[0], pool_scale=pool_scale[0], conv_dw_b=conv_dw_b[0], conv_ln_g=conv_ln_g[0],
             conv_ln_b=conv_ln_b[0], attn_rel_bias=attn_rel_bias[0], ln_mix_g=ln_mix_g, ln_mix_b=ln_mix_b,
             ffn_dw_b=ffn_dw_b, ple_b_gate=ple_b_gate, ln_ffn_g=ln_ffn_g, ln_ffn_b=ln_ffn_b)
    for n in ("ffn_up_t", "ffn_dw_w", "ffn_w_down", "ple_w_gate", "ple_w_proj"):
        w[n] = [None, None]

    def ready(group, after):
        got = _exchange_wait(gather[group], token if after is None else after, name="gather_wait_" + group)
        if group == "mix":
            w["mix_w_in_t"], w["conv_dw_w"] = got[0], _full_from_shards(got[1], 1)
        elif group == "mixo":
            (w["mix_w_out"],) = got
        elif group == "attn":
            w["attn_w_qkv"], w["attn_w_o"] = got
        elif group[:2] == "up":
            l = int(group[2])
            w["ffn_up_t"][l], w["ffn_dw_w"][l] = got[0], _full_from_shards(got[1], 1)
        else:
            l = int(group[2])
            w["ffn_w_down"][l], w["ple_w_gate"][l], w["ple_w_proj"][l] = got

    scatter = {}

    def emit(group, gr):
        if group[:3] == "ffn":
            l = int(group[3])
            pieces = [_pieces_from_full(gr["ffn_up_t"][l], 0),
                      _pieces_from_full(gr["ffn_dw_w"][l], 1), _pieces_from_full(gr["ffn_w_down"][l], 0),
                      _pieces_from_full(gr["ple_w_gate"][l], 0), gr["ple_w_proj"][l]]
        elif group == "attn":
            pieces = [gr["attn_w_qkv"], _pieces_from_full(gr["attn_w_o"], 0)]
        else:
            pieces = [_pieces_from_full(gr["mix_w_in_t"], 0), _pieces_from_full(gr["conv_dw_w"], 1),
                      _pieces_from_full(gr["mix_w_out"], 0)]
        scatter[group] = _exchange_start([a.astype(BF16) for a in pieces], ["pieces"] * len(pieces), None,
                                         name="grad_start_" + group)
        if group != "mix":
            return scatter[group]["token"]
        gfull = dict(
            pool_w=gr["pool_w"][None], pool_scale=gr["pool_scale"][None], conv_dw_b=gr["conv_dw_b"][None],
            conv_ln_g=gr["conv_ln_g"][None], conv_ln_b=gr["conv_ln_b"][None],
            attn_rel_bias=gr["attn_rel_bias"][None], ln_mix_g=jnp.stack(gr["ln_mix_g"]),
            ln_mix_b=jnp.stack(gr["ln_mix_b"]), ffn_dw_b=jnp.stack(gr["ffn_dw_b"]),
            ple_b_gate=jnp.stack(gr["ple_b_gate"]), ln_ffn_g=jnp.stack(gr["ln_ffn_g"]),
            ln_ffn_b=jnp.stack(gr["ln_ffn_b"]))
        rep_send = _pack_rows([gfull[n].reshape(-1) for n in _REPLICATED], 8, F32)
        scatter["replicated"] = _exchange_start([rep_send], ["stack"], scatter[group]["token"],
                                                name="grad_start_replicated")
        return scatter["replicated"]["token"]

    loss_part, grad_x, gr = _local_step(x[0], p[:, 0], loss_target[0], w, ready, emit)
    loss = lax.psum(loss_part, ("x", "y", "c"))

    group_weights = {"ffn1": (("ffn_w_up", 1), ("ffn_dw_w", 1), ("ffn_w_down", 1), ("ple_w_gate", 1), ("ple_w_proj", 1)),
                     "attn": (("attn_w_qkv", 0), ("attn_w_o", 0)),
                     "ffn0": (("ffn_w_up", 0), ("ffn_dw_w", 0), ("ffn_w_down", 0), ("ple_w_gate", 0), ("ple_w_proj", 0)),
                     "mix": (("mix_w_in", 0), ("conv_dw_w", 0), ("mix_w_out", 0))}
    updated = {}
    after = grad_x
    for group in ("ffn1", "attn", "ffn0", "mix"):
        recv = _exchange_wait(scatter[group], after, name="grad_wait_" + group)
        for (n, l), r in zip(group_weights[group], recv):
            updated[n] = _adamw(r, wts[n], mom[n], var[n], layer=l, into=updated.get(n), name=f"adamw_{n}{l}")
            after = updated[n][0]
    res = [{n: jnp.swapaxes(updated[n][k], 1, 2) if n in _TRANSPOSED else updated[n][k] for n in sh_names}
           for k in range(4)]
    (rep_recv,) = _exchange_wait(scatter["replicated"], after, name="grad_wait_replicated")

    def flat_state(d):
        return _pack_rows([d[n].reshape(-1) for n in _REPLICATED], 8, F32)[None]

    rep_out = _adamw(rep_recv, flat_state(wts), flat_state(mom), flat_state(var), name="adamw_replicated")
    for k in range(4):
        for n, arr in zip(_REPLICATED, _unpack(rep_out[k][0], [wts[n].shape for n in _REPLICATED])):
            res[k][n] = arr
    order = ["mix_w_in", "pool_w", "pool_scale", "conv_dw_w", "conv_dw_b", "conv_ln_g", "conv_ln_b", "mix_w_out",
             "attn_w_qkv", "attn_rel_bias", "attn_w_o", "ln_mix_g", "ln_mix_b", "ffn_w_up", "ffn_dw_w", "ffn_dw_b",
             "ffn_w_down", "ple_w_proj", "ple_w_gate", "ple_b_gate", "ln_ffn_g", "ln_ffn_b"]
    outs = [loss, grad_x[None]]
    for k in range(4):
        outs += [res[k][n] for n in order]
    return tuple(outs)
```

```python
import functools
import math

import jax
import jax.numpy as jnp
from jax import lax
from jax.experimental import pallas as pl
from jax.experimental.pallas import tpu as pltpu

F32 = jnp.float32
BF16 = jnp.bfloat16

N_DEV = 8
D_MODEL = 1024
D_POOL = 512
D_CONV = 512
POOL_WINDOWS = (2, 4, 8, 16)
POOL_GROUP = 128
CONV_KERNEL = 31
CHUNK = 64
HEAD_DIM = 64
N_HEADS = 16
LEFT_CHUNKS = 8
BAND = (LEFT_CHUNKS + 1) * CHUNK
MAX_REL = 256
D_FF = 2816
PLE_DIM = 256
ALPHA = 4.0 ** 0.25
LN_EPS = 1e-5
NEG_INF = -1e30
ADAM_LR, ADAM_B1, ADAM_B2, ADAM_EPS, ADAM_WD, ADAM_STEP = 0.001, 0.9, 0.999, 1e-08, 0.01, 10

Q_BLOCK = 4 * CHUNK
KV_PAD = LEFT_CHUNKS * CHUNK
KV_SPAN = KV_PAD + Q_BLOCK
CONV_HALO = 32
FFN_HALO = 16
SUB_ROWS, SUB_LANES = 64, 128
LANES = 1024
VMEM_LIMIT = 56 * 1024 * 1024


def _cparams(sem=None):
    return pltpu.CompilerParams(dimension_semantics=sem, vmem_limit_bytes=VMEM_LIMIT)


def _tile(dim, pref):
    if dim <= pref:
        return dim
    t = pref - pref % 128
    while t >= 128:
        if dim % t == 0:
            return t
        t -= 128
    return dim


def _sigmoid(x):
    return 1.0 / (1.0 + jnp.exp(-x))


def _bdot(a, b, dn=(((1,), (0,)), ((), ()))):
    return lax.dot_general(a.astype(BF16), b.astype(BF16), dn, preferred_element_type=F32)


WHOLE = (0, 1)
NT = (((1,), (1,)), ((), ()))
TN = (((0,), (0,)), ((), ()))


def _wgrad(a, b, *, tm=1024, tn=1024, tk=1024, piece=None, part=(0, 1), into=None, name):
    K, M = a.shape
    kb, N = b.shape
    assert K == kb, (a.shape, b.shape)
    tm, tn, tk = _tile(M, tm), _tile(N, tn), _tile(K, tk)
    nk = K // tk
    per = 1 if piece is None else tn // piece
    assert piece is None or tn == per * piece

    def body(a_ref, b_ref, *rest):
        o_ref, acc = rest[-2:]
        k = pl.program_id(2)

        @pl.when(k == 0)
        def _():
            acc[...] = jnp.zeros_like(acc)

        acc[...] += _bdot(a_ref[...], b_ref[...], TN)

        @pl.when(k == nk - 1)
        def _():
            if piece is None:
                o_ref[...] = acc[...].astype(BF16)
            else:
                for s in range(per):
                    o_ref[s] = acc[:, s * piece:(s + 1) * piece].astype(BF16)

    if piece is None:
        first = part[0] * (M // tm)
        out_shape = (part[1] * M, N)
        out_spec = pl.BlockSpec((tm, tn), lambda i, j, k: (first + i, j))
    else:
        out_shape, out_spec = (N // piece, M, piece), pl.BlockSpec((per, tm, piece), lambda i, j, k: (j, i, 0))
    others = [] if into is None else [into]
    return pl.pallas_call(
        body,
        out_shape=jax.ShapeDtypeStruct(out_shape, BF16),
        grid=(M // tm, N // tn, nk),
        in_specs=[pl.BlockSpec((tk, tm), lambda i, j, k: (k, i)), pl.BlockSpec((tk, tn), lambda i, j, k: (k, j))]
        + [pl.BlockSpec(memory_space=pl.ANY)] * len(others),
        out_specs=out_spec,
        input_output_aliases={2: 0} if others else {},
        scratch_shapes=[pltpu.VMEM((tm, tn), F32)],
        compiler_params=_cparams(("parallel", "parallel", "arbitrary")),
        name=name,
    )(a, b, *others)


def _mm_rows(pairs, *, add=None, add_scale=1.0, out_dtype=F32, tm=256, dep=None, ln_bwd=None, name):
    M = pairs[0][0].shape[0]
    n = len(pairs)
    has_add = add is not None
    w_rows = [w_.shape[0] // part[1] for _, w_, _, part in pairs]
    N = w_rows[0] if pairs[0][2] else pairs[0][1].shape[1]

    def body(*refs):
        acc = None
        for i, (_, _, tr, _) in enumerate(pairs):
            part = _bdot(refs[2 * i][...], refs[2 * i + 1][...], NT if tr else (((1,), (0,)), ((), ())))
            acc = part if acc is None else acc + part
        if has_add:
            acc = acc + add_scale * refs[2 * n][...]
        if ln_bwd is None:
            refs[-1][...] = acc.astype(out_dtype)
            return
        z_ref, g_ref = refs[2 * n + has_add], refs[2 * n + has_add + 1]
        dz_ref, dzb_ref, dg_ref, db_ref = refs[-4:]

        @pl.when(pl.program_id(0) == 0)
        def _():
            dg_ref[...] = jnp.zeros_like(dg_ref)
            db_ref[...] = jnp.zeros_like(db_ref)

        dg_acc = jnp.zeros((8, N), F32)
        db_acc = jnp.zeros((8, N), F32)
        for r0 in range(0, tm, LN_ROWS):
            rows = pl.ds(r0, LN_ROWS)
            do = acc[r0:r0 + LN_ROWS]
            dz, xh = _ln_bwd_rows(z_ref[rows, :], g_ref[...], do)
            dz_ref[rows, :] = dz
            dzb_ref[rows, :] = dz.astype(BF16)
            dg_acc = dg_acc + jnp.sum((do * xh).reshape(LN_ROWS // 8, 8, N), axis=0)
            db_acc = db_acc + jnp.sum(do.reshape(LN_ROWS // 8, 8, N), axis=0)
        dg_ref[...] += jnp.sum(dg_acc, axis=0, keepdims=True)
        db_ref[...] += jnp.sum(db_acc, axis=0, keepdims=True)

    in_specs, args = [], []
    for (a, w_, _, part), rows in zip(pairs, w_rows):
        in_specs += [pl.BlockSpec((tm, a.shape[1]), lambda i: (i, 0)),
                     pl.BlockSpec((rows, w_.shape[1]), functools.partial(lambda i, j: (j, 0), j=part[0]))]
        args += [a, w_]
    row = pl.BlockSpec((tm, N), lambda i: (i, 0))
    fix = pl.BlockSpec((1, N), lambda i: (0, 0))
    if has_add:
        in_specs.append(row)
        args.append(add)
    if ln_bwd is not None:
        in_specs += [row, fix]
        args += [ln_bwd[0], ln_bwd[1].reshape(1, N)]
    if dep is not None:
        in_specs.append(pl.BlockSpec(memory_space=pl.ANY))
        args.append(dep)
    if ln_bwd is None:
        out_shape, out_specs = jax.ShapeDtypeStruct((M, N), out_dtype), row
    else:
        out_shape = [jax.ShapeDtypeStruct((M, N), F32), jax.ShapeDtypeStruct((M, N), BF16),
                     jax.ShapeDtypeStruct((1, N), F32), jax.ShapeDtypeStruct((1, N), F32)]
        out_specs = [row, row, fix, fix]
    return pl.pallas_call(
        body,
        out_shape=out_shape,
        grid=(M // tm,),
        in_specs=in_specs,
        out_specs=out_specs,
        compiler_params=_cparams(("parallel",) if ln_bwd is None else ("arbitrary",)),
        name=name,
    )(*args)


def _ln_bwd_rows(zt, g, do):
    zc = zt - jnp.mean(zt, axis=-1, keepdims=True)
    rstd = lax.rsqrt(jnp.mean(zc * zc, axis=-1, keepdims=True) + LN_EPS)
    xh = zc * rstd
    dxh = do * g
    return rstd * (dxh - jnp.mean(dxh, axis=-1, keepdims=True) - xh * jnp.mean(dxh * xh, axis=-1, keepdims=True)), xh


def _layer_norm_rows(z, g, b):
    mu = jnp.mean(z, axis=-1, keepdims=True)
    zc = z - mu
    var = jnp.mean(zc * zc, axis=-1, keepdims=True)
    return zc * lax.rsqrt(var + LN_EPS) * g + b


def _proj_ln(res, a, w, ln_g, ln_b, *, ple=None, ts=256, name):
    S, D = res.shape
    ka = a.shape[1]
    has_ple = ple is not None
    row = lambda i: (i, 0)
    fix = lambda i: (0, 0)

    def body(*refs):
        if has_ple:
            (res_ref, a_ref, w_ref, g_ref, b_ref, wg_ref, bg_ref, p_ref, wp_ref, z_ref, r_ref, rb_ref, gate_ref,
             proj_ref, acc) = refs
        else:
            res_ref, a_ref, w_ref, g_ref, b_ref, z_ref, r_ref, rb_ref, acc = refs
        acc[...] = _bdot(a_ref[...], w_ref[...])
        if has_ple:
            gate_ref[...] = _bdot(res_ref[...], wg_ref[...])
            proj_ref[...] = _bdot(p_ref[...], wp_ref[...])
        for r0 in range(0, ts, LN_ROWS):
            rows = pl.ds(r0, LN_ROWS)
            z = ALPHA * res_ref[rows, :] + acc[rows, :]
            if has_ple:
                gate = _sigmoid(gate_ref[rows, :] + bg_ref[...])
                gate_ref[rows, :] = gate
                z = z + gate * proj_ref[rows, :]
            z_ref[rows, :] = z
            r = _layer_norm_rows(z, g_ref[...], b_ref[...])
            r_ref[rows, :] = r
            rb_ref[rows, :] = r.astype(BF16)

    in_specs = [pl.BlockSpec((ts, D), row), pl.BlockSpec((ts, ka), row), pl.BlockSpec((ka, D), fix),
                pl.BlockSpec((1, D), fix), pl.BlockSpec((1, D), fix)]
    args = [res, a, w, ln_g.reshape(1, D), ln_b.reshape(1, D)]
    out_dtypes = [F32, F32, BF16]
    if has_ple:
        wg, bg, p, wp = ple
        in_specs += [pl.BlockSpec((D, D), fix), pl.BlockSpec((1, D), fix), pl.BlockSpec((ts, PLE_DIM), row),
                     pl.BlockSpec((PLE_DIM, D), fix)]
        args += [wg, bg.reshape(1, D), p, wp]
        out_dtypes += [F32, F32]
    return pl.pallas_call(
        body,
        out_shape=[jax.ShapeDtypeStruct((S, D), dt) for dt in out_dtypes],
        grid=(S // ts,),
        in_specs=in_specs,
        out_specs=[pl.BlockSpec((ts, D), row)] * len(out_dtypes),
        scratch_shapes=[pltpu.VMEM((ts, D), F32)],
        compiler_params=_cparams(("parallel",)),
        name=name,
    )(*args)


CONV_ROWS = 32
LN_ROWS = 16


def _shifted_copies(src, dst, rows):
    for b in range(1, 8):
        for c0 in range(0, src.shape[1], SUB_LANES):
            ln = pl.ds(c0, SUB_LANES)
            for r0 in range(0, rows, SUB_ROWS):
                rc = min(SUB_ROWS, rows - r0)
                dst[b - 1, pl.ds(r0, rc), ln] = src[pl.ds(r0 + b, rc), ln]


def _rows_at(src, copies, off, n, ln):
    b = off % 8
    return src[pl.ds(off, n), ln] if b == 0 else copies[b - 1, pl.ds(off - b, n), ln]


def _conv31(stg, gsh, cw_ref, cb_ref, out, rows, first_off):
    for c0 in range(0, D_CONV, SUB_LANES):
        ln = pl.ds(c0, SUB_LANES)
        for r0 in range(0, rows, CONV_ROWS):
            acc = jnp.zeros((CONV_ROWS, SUB_LANES), F32) + cb_ref[:, ln]
            for k in range(CONV_KERNEL):
                acc = acc + cw_ref[k:k + 1, ln] * _rows_at(stg, gsh, first_off + k + r0, CONV_ROWS, ln)
            out[pl.ds(r0, CONV_ROWS), ln] = acc


def _mixer_fwd(u, pool_w, pool_scale, conv_w, conv_b, cln_g, cln_b, *, ts=256):
    S = u.shape[0]
    hb = CONV_HALO
    nh = ts // hb

    def body(u_ref, uh_ref, pw_ref, ps_ref, cw_ref, cb_ref, g_ref, b_ref, y_ref, d_ref, sta, stg, gsh, hcs):
        i = pl.program_id(0)
        first = i == 0
        sta[pl.ds(0, hb), :] = jnp.where(first, 0.0, uh_ref[:, 0:D_POOL])
        sta[pl.ds(hb, ts), :] = u_ref[:, 0:D_POOL]
        glu_h = uh_ref[:, D_POOL:D_POOL + D_CONV] * _sigmoid(uh_ref[:, D_POOL + D_CONV:])
        stg[pl.ds(0, hb), :] = jnp.where(first, 0.0, glu_h)
        stg[pl.ds(hb, ts), :] = u_ref[:, D_POOL:D_POOL + D_CONV] * _sigmoid(u_ref[:, D_POOL + D_CONV:])

        pos = (i * ts + lax.broadcasted_iota(jnp.int32, (ts, 1), 0) + 1).astype(F32)
        for g, w in enumerate(POOL_WINDOWS):
            lanes = pl.ds(g * POOL_GROUP, POOL_GROUP)
            a_g = sta[pl.ds(hb, ts), lanes]
            s = a_g
            for j in range(1, w):
                s = s + sta[pl.ds(hb - j, ts), lanes]
            d_g = s / jnp.minimum(pos, float(w)) - a_g
            d_ref[:, lanes] = d_g.astype(BF16)
            y_ref[:, lanes] = (_bdot(d_g, pw_ref[g]) * ps_ref[:, lanes]).astype(BF16)

        _shifted_copies(stg, gsh, hb + ts - 8)
        _conv31(stg, gsh, cw_ref, cb_ref, hcs, ts, hb - (CONV_KERNEL - 1))
        for r0 in range(0, ts, LN_ROWS):
            rows = pl.ds(r0, LN_ROWS)
            ln = _layer_norm_rows(hcs[rows, :], g_ref[...], b_ref[...])
            y_ref[rows, D_POOL:] = (ln * _sigmoid(ln)).astype(BF16)

    fix2 = lambda i: (0, 0)
    return pl.pallas_call(
        body,
        out_shape=[jax.ShapeDtypeStruct((S, D_MODEL), BF16), jax.ShapeDtypeStruct((S, D_POOL), BF16)],
        grid=(S // ts,),
        in_specs=[pl.BlockSpec((ts, 3 * D_POOL), lambda i: (i, 0)),
                  pl.BlockSpec((hb, 3 * D_POOL), lambda i: (jnp.maximum(i * nh - 1, 0), 0)),
                  pl.BlockSpec((4, POOL_GROUP, POOL_GROUP), lambda i: (0, 0, 0)),
                  pl.BlockSpec((1, D_POOL), fix2), pl.BlockSpec((CONV_KERNEL, D_CONV), fix2),
                  pl.BlockSpec((1, D_CONV), fix2), pl.BlockSpec((1, D_CONV), fix2), pl.BlockSpec((1, D_CONV), fix2)],
        out_specs=[pl.BlockSpec((ts, D_MODEL), lambda i: (i, 0)), pl.BlockSpec((ts, D_POOL), lambda i: (i, 0))],
        scratch_shapes=[pltpu.VMEM((hb + ts, D_POOL), F32), pltpu.VMEM((hb + ts, D_CONV), F32),
                        pltpu.VMEM((7, hb + ts - 8, D_CONV), F32), pltpu.VMEM((ts, D_CONV), F32)],
        compiler_params=_cparams(("parallel",)),
        name="mixer_fwd",
    )(u, u, pool_w, pool_scale.reshape(1, D_POOL), conv_w, conv_b.reshape(1, D_CONV), cln_g.reshape(1, D_CONV),
      cln_b.reshape(1, D_CONV))


def _mixer_bwd(u, d, dycat, pool_w, pool_scale, conv_w, conv_b, cln_g, cln_b, *, ts=256):
    S = u.shape[0]
    hb = CONV_HALO
    nh = ts // hb
    n = S // ts
    te = ts + hb
    K = CONV_KERNEL

    def body(u_ref, up_ref, un_ref, d_ref, dy_ref, dyn_ref, pw_ref, ps_ref, cw_ref, cb_ref, g_ref, b_ref,
             du_ref, dpw_ref, dps_ref, dcw_ref, dcb_ref, dg_ref, db_ref, stg, std, sth, gsh, hcs, hsh):
        i = pl.program_id(0)
        first = i == 0
        last = i == n - 1

        @pl.when(first)
        def _():
            dpw_ref[...] = jnp.zeros_like(dpw_ref)
            dps_ref[...] = jnp.zeros_like(dps_ref)
            dcw_ref[...] = jnp.zeros_like(dcw_ref)
            dcb_ref[...] = jnp.zeros_like(dcb_ref)
            dg_ref[...] = jnp.zeros_like(dg_ref)
            db_ref[...] = jnp.zeros_like(db_ref)

        pos_e = (i * ts + lax.broadcasted_iota(jnp.int32, (te, 1), 0) + 1).astype(F32)
        dya = dy_ref[:, 0:D_POOL]
        dya_n = jnp.where(last, 0.0, dyn_ref[:, 0:D_POOL])
        for g, w in enumerate(POOL_WINDOWS):
            lanes = pl.ds(g * POOL_GROUP, POOL_GROUP)
            sl = slice(g * POOL_GROUP, (g + 1) * POOL_GROUP)
            pw = pw_ref[g]
            scale = ps_ref[:, lanes]
            d_g = d_ref[:, lanes]
            pre = _bdot(d_g, pw)
            dps_ref[:, lanes] += jnp.sum(dya[:, sl] * pre, axis=0, keepdims=True)
            dys = dya[:, sl] * scale
            dpw_ref[g] += _bdot(d_g, dys, TN)
            dys_e = jnp.concatenate([dys, dya_n[:, sl] * scale], axis=0)
            dd = _bdot(dys_e, pw, NT)
            std[:, lanes] = dd / jnp.minimum(pos_e, float(w))
            da = -dd[0:ts]
            for m in range(w):
                da = da + std[pl.ds(m, ts), lanes]
            du_ref[:, lanes] = da.astype(BF16)

        glu_p = up_ref[:, D_POOL:D_POOL + D_CONV] * _sigmoid(up_ref[:, D_POOL + D_CONV:])
        stg[pl.ds(0, hb), :] = jnp.where(first, 0.0, glu_p)
        bv = u_ref[:, D_POOL:D_POOL + D_CONV]
        sg = _sigmoid(u_ref[:, D_POOL + D_CONV:])
        stg[pl.ds(hb, ts), :] = bv * sg
        glu_n = un_ref[:, D_POOL:D_POOL + D_CONV] * _sigmoid(un_ref[:, D_POOL + D_CONV:])
        stg[pl.ds(hb + ts, hb), :] = jnp.where(last, 0.0, glu_n)
        _shifted_copies(stg, gsh, hb + te - 8)
        _conv31(stg, gsh, cw_ref, cb_ref, hcs, te, hb - (K - 1))

        sums = [jnp.zeros((8, D_CONV), F32) for _ in range(3)]
        for r0 in range(0, te, LN_ROWS):
            rows = pl.ds(r0, LN_ROWS)
            hc = hcs[rows, :]
            hcc = hc - jnp.mean(hc, axis=-1, keepdims=True)
            rstd = lax.rsqrt(jnp.mean(hcc * hcc, axis=-1, keepdims=True) + LN_EPS)
            xh = hcc * rstd
            ln = xh * g_ref[...] + b_ref[...]
            sl_ = _sigmoid(ln)
            if r0 < ts:
                dyb = dy_ref[rows, D_POOL:]
            else:
                dyb = jnp.where(last, 0.0, dyn_ref[pl.ds(r0 - ts, LN_ROWS), D_POOL:])
            dln = dyb * (sl_ * (1.0 + ln * (1.0 - sl_)))
            dxh = dln * g_ref[...]
            dhc = rstd * (dxh - jnp.mean(dxh, axis=-1, keepdims=True)
                          - xh * jnp.mean(dxh * xh, axis=-1, keepdims=True))
            sth[rows, :] = dhc
            if r0 < ts:
                for n_, term in enumerate((dln * xh, dln, dhc)):
                    sums[n_] = sums[n_] + jnp.sum(term.reshape(LN_ROWS // 8, 8, D_CONV), axis=0)
        dg_ref[...] += jnp.sum(sums[0], axis=0, keepdims=True)
        db_ref[...] += jnp.sum(sums[1], axis=0, keepdims=True)
        dcb_ref[...] += jnp.sum(sums[2], axis=0, keepdims=True)

        _shifted_copies(sth, hsh, te - 8)
        for c0 in range(0, D_CONV, SUB_LANES):
            ln_ = pl.ds(c0, SUB_LANES)
            for r0 in range(0, ts, CONV_ROWS):
                rows = pl.ds(r0, CONV_ROWS)
                dglu = jnp.zeros((CONV_ROWS, SUB_LANES), F32)
                for k in range(K):
                    dglu = dglu + cw_ref[k:k + 1, ln_] * _rows_at(sth, hsh, K - 1 - k + r0, CONV_ROWS, ln_)
                bv = u_ref[rows, pl.ds(D_POOL + c0, SUB_LANES)]
                sg = _sigmoid(u_ref[rows, pl.ds(D_POOL + D_CONV + c0, SUB_LANES)])
                du_ref[rows, pl.ds(D_POOL + c0, SUB_LANES)] = (dglu * sg).astype(BF16)
                du_ref[rows, pl.ds(D_POOL + D_CONV + c0, SUB_LANES)] = (dglu * bv * sg * (1.0 - sg)).astype(BF16)
            for k in range(K):
                tap = jnp.zeros((8, SUB_LANES), F32)
                for r0 in range(0, ts, CONV_ROWS):
                    prod = sth[pl.ds(r0, CONV_ROWS), ln_] * _rows_at(stg, gsh, hb - (K - 1) + k + r0, CONV_ROWS, ln_)
                    tap = tap + jnp.sum(prod.reshape(CONV_ROWS // 8, 8, SUB_LANES), axis=0)
                dcw_ref[k:k + 1, ln_] += jnp.sum(tap, axis=0, keepdims=True)

    fix2 = lambda i: (0, 0)
    prev = lambda i: (jnp.maximum(i * nh - 1, 0), 0)
    nxt = lambda i: (jnp.minimum((i + 1) * nh, S // hb - 1), 0)
    return pl.pallas_call(
        body,
        out_shape=[jax.ShapeDtypeStruct((S, 3 * D_POOL), BF16),
                   jax.ShapeDtypeStruct((4, POOL_GROUP, POOL_GROUP), F32),
                   jax.ShapeDtypeStruct((1, D_POOL), F32),
                   jax.ShapeDtypeStruct((K, D_CONV), F32),
                   jax.ShapeDtypeStruct((1, D_CONV), F32),
                   jax.ShapeDtypeStruct((1, D_CONV), F32),
                   jax.ShapeDtypeStruct((1, D_CONV), F32)],
        grid=(n,),
        in_specs=[pl.BlockSpec((ts, 3 * D_POOL), lambda i: (i, 0)),
                  pl.BlockSpec((hb, 3 * D_POOL), prev),
                  pl.BlockSpec((hb, 3 * D_POOL), nxt),
                  pl.BlockSpec((ts, D_POOL), lambda i: (i, 0)),
                  pl.BlockSpec((ts, D_MODEL), lambda i: (i, 0)),
                  pl.BlockSpec((hb, D_MODEL), nxt),
                  pl.BlockSpec((4, POOL_GROUP, POOL_GROUP), lambda i: (0, 0, 0)),
                  pl.BlockSpec((1, D_POOL), fix2), pl.BlockSpec((K, D_CONV), fix2),
                  pl.BlockSpec((1, D_CONV), fix2), pl.BlockSpec((1, D_CONV), fix2), pl.BlockSpec((1, D_CONV), fix2)],
        out_specs=[pl.BlockSpec((ts, 3 * D_POOL), lambda i: (i, 0)),
                   pl.BlockSpec((4, POOL_GROUP, POOL_GROUP), lambda i: (0, 0, 0)),
                   pl.BlockSpec((1, D_POOL), fix2), pl.BlockSpec((K, D_CONV), fix2),
                   pl.BlockSpec((1, D_CONV), fix2), pl.BlockSpec((1, D_CONV), fix2), pl.BlockSpec((1, D_CONV), fix2)],
        scratch_shapes=[pltpu.VMEM((hb + ts + hb, D_CONV), F32), pltpu.VMEM((te, D_POOL), F32),
                        pltpu.VMEM((te, D_CONV), F32), pltpu.VMEM((7, hb + te - 8, D_CONV), F32),
                        pltpu.VMEM((te, D_CONV), F32), pltpu.VMEM((7, te - 8, D_CONV), F32)],
        compiler_params=_cparams(("arbitrary",)),
        name="mixer_bwd",
    )(u, u, u, d, dycat, dycat, pool_w, pool_scale.reshape(1, D_POOL), conv_w, conv_b.reshape(1, D_CONV),
      cln_g.reshape(1, D_CONV), cln_b.reshape(1, D_CONV))


_GELU_C = math.sqrt(2.0 / math.pi)


def _gelu_parts(x):
    inner = _GELU_C * (x + 0.044715 * x * x * x)
    th = jnp.tanh(inner)
    ge = 0.5 * x * (1.0 + th)
    dge = 0.5 * (1.0 + th) + 0.5 * x * (1.0 - th * th) * (_GELU_C * (1.0 + 3.0 * 0.044715 * x * x))
    return ge, dge


def _ffn_act_fwd(gate, val, dw_w, dw_b, *, ts=256, tc=1408, name):
    S, F = gate.shape
    hb = FFN_HALO
    nh = ts // hb
    tc = _tile(F, tc)

    def body(g_ref, gh_ref, v_ref, w_ref, b_ref, h_ref, st):
        i = pl.program_id(0)
        st[pl.ds(0, hb), :] = jnp.where(i == 0, 0.0, gh_ref[...].astype(F32))
        st[pl.ds(hb, ts), :] = g_ref[...].astype(F32)
        for c0 in range(0, tc, SUB_LANES):
            ln = pl.ds(c0, SUB_LANES)
            w0, w1, w2, b = w_ref[0:1, ln], w_ref[1:2, ln], w_ref[2:3, ln], b_ref[:, ln]
            for r0 in range(0, ts, SUB_ROWS):
                gc = b + w0 * st[pl.ds(hb - 2 + r0, SUB_ROWS), ln] + w1 * st[pl.ds(hb - 1 + r0, SUB_ROWS), ln] \
                    + w2 * st[pl.ds(hb + r0, SUB_ROWS), ln]
                ge, _ = _gelu_parts(gc)
                rows = pl.ds(r0, SUB_ROWS)
                h_ref[rows, ln] = (ge * v_ref[rows, ln].astype(F32)).astype(BF16)

    return pl.pallas_call(
        body,
        out_shape=jax.ShapeDtypeStruct((S, F), BF16),
        grid=(S // ts, F // tc),
        in_specs=[pl.BlockSpec((ts, tc), lambda i, j: (i, j)),
                  pl.BlockSpec((hb, tc), lambda i, j: (jnp.maximum(i * nh - 1, 0), j)),
                  pl.BlockSpec((ts, tc), lambda i, j: (i, j)),
                  pl.BlockSpec((3, tc), lambda i, j: (0, j)),
                  pl.BlockSpec((1, tc), lambda i, j: (0, j))],
        out_specs=pl.BlockSpec((ts, tc), lambda i, j: (i, j)),
        scratch_shapes=[pltpu.VMEM((hb + ts, tc), F32)],
        compiler_params=_cparams(("parallel", "parallel")),
        name=name,
    )(gate, gate, val, dw_w, dw_b.reshape(1, F))


def _ffn_act_bwd(gate, val, dh, dw_w, dw_b, *, ts=256, tc=1408, name):
    S, F = gate.shape
    hb = FFN_HALO
    nh = ts // hb
    n = S // ts
    te = ts + hb
    tc = _tile(F, tc)

    def body(g_ref, gp_ref, gn_ref, v_ref, vn_ref, dh_ref, dhn_ref, w_ref, b_ref,
             dg_ref, dv_ref, dw_ref, db_ref, st, sd):
        i = pl.program_id(1)
        first = i == 0
        last = i == n - 1

        @pl.when(first)
        def _():
            dw_ref[...] = jnp.zeros_like(dw_ref)
            db_ref[...] = jnp.zeros_like(db_ref)

        st[pl.ds(0, hb), :] = jnp.where(first, 0.0, gp_ref[...].astype(F32))
        st[pl.ds(hb, ts), :] = g_ref[...].astype(F32)
        st[pl.ds(hb + ts, hb), :] = jnp.where(last, 0.0, gn_ref[...].astype(F32))
        for c0 in range(0, tc, SUB_LANES):
            ln = pl.ds(c0, SUB_LANES)
            w0, w1, w2, b = w_ref[0:1, ln], w_ref[1:2, ln], w_ref[2:3, ln], b_ref[:, ln]
            db_acc = jnp.zeros((8, SUB_LANES), F32)
            dw_acc = [jnp.zeros((8, SUB_LANES), F32) for _ in range(3)]
            for r0 in range(0, te, SUB_ROWS):
                rc = min(SUB_ROWS, te - r0)
                taps = [st[pl.ds(hb - 2 + k + r0, rc), ln] for k in range(3)]
                gc = b + w0 * taps[0] + w1 * taps[1] + w2 * taps[2]
                ge, dge = _gelu_parts(gc)
                if r0 < ts:
                    rows = pl.ds(r0, rc)
                    val, dh = v_ref[rows, ln].astype(F32), dh_ref[rows, ln].astype(F32)
                else:
                    val = jnp.where(last, 0.0, vn_ref[:, ln].astype(F32)[0:rc])
                    dh = jnp.where(last, 0.0, dhn_ref[:, ln].astype(F32)[0:rc])
                dgc = dh * val * dge
                sd[pl.ds(r0, rc), ln] = dgc
                if r0 < ts:
                    dv_ref[rows, ln] = (dh * ge).astype(BF16)
                    db_acc = db_acc + jnp.sum(dgc.reshape(rc // 8, 8, SUB_LANES), axis=0)
                    for k in range(3):
                        dw_acc[k] = dw_acc[k] + jnp.sum((dgc * taps[k]).reshape(rc // 8, 8, SUB_LANES), axis=0)
            db_ref[:, ln] += jnp.sum(db_acc, axis=0, keepdims=True)
            for k in range(3):
                dw_ref[k:k + 1, ln] += jnp.sum(dw_acc[k], axis=0, keepdims=True)
            for r0 in range(0, ts, SUB_ROWS):
                dgate = w0 * sd[pl.ds(2 + r0, SUB_ROWS), ln] + w1 * sd[pl.ds(1 + r0, SUB_ROWS), ln] \
                    + w2 * sd[pl.ds(r0, SUB_ROWS), ln]
                dg_ref[pl.ds(r0, SUB_ROWS), ln] = dgate.astype(BF16)

    cur = lambda j, i: (i, j)
    prev = lambda j, i: (jnp.maximum(i * nh - 1, 0), j)
    nxt = lambda j, i: (jnp.minimum((i + 1) * nh, S // hb - 1), j)
    return pl.pallas_call(
        body,
        out_shape=[jax.ShapeDtypeStruct((S, F), BF16), jax.ShapeDtypeStruct((S, F), BF16),
                   jax.ShapeDtypeStruct((3, F), F32), jax.ShapeDtypeStruct((1, F), F32)],
        grid=(F // tc, n),
        in_specs=[pl.BlockSpec((ts, tc), cur), pl.BlockSpec((hb, tc), prev), pl.BlockSpec((hb, tc), nxt),
                  pl.BlockSpec((ts, tc), cur), pl.BlockSpec((hb, tc), nxt),
                  pl.BlockSpec((ts, tc), cur), pl.BlockSpec((hb, tc), nxt),
                  pl.BlockSpec((3, tc), lambda j, i: (0, j)), pl.BlockSpec((1, tc), lambda j, i: (0, j))],
        out_specs=[pl.BlockSpec((ts, tc), cur), pl.BlockSpec((ts, tc), cur),
                   pl.BlockSpec((3, tc), lambda j, i: (0, j)), pl.BlockSpec((1, tc), lambda j, i: (0, j))],
        scratch_shapes=[pltpu.VMEM((hb + ts + hb, tc), F32), pltpu.VMEM((te, tc), F32)],
        compiler_params=_cparams(("parallel", "arbitrary")),
        name=name,
    )(gate, gate, gate, val, val, dh, dh, dw_w, dw_b.reshape(1, F))


def _loss_ln_bwd(z, ln_g, ln_b, target, *, ts=256, name):
    S, D = z.shape

    def body(z_ref, g_ref, b_ref, t_ref, dz_ref, dzb_ref, dg_ref, db_ref, loss_ref):
        i = pl.program_id(0)

        @pl.when(i == 0)
        def _():
            dg_ref[...] = jnp.zeros_like(dg_ref)
            db_ref[...] = jnp.zeros_like(db_ref)
            loss_ref[...] = jnp.zeros_like(loss_ref)

        dg_acc = jnp.zeros((8, D), F32)
        db_acc = jnp.zeros((8, D), F32)
        loss_acc = jnp.zeros((1, 1), F32)
        for r0 in range(0, ts, LN_ROWS):
            rows = pl.ds(r0, LN_ROWS)
            zt = z_ref[rows, :]
            err = _layer_norm_rows(zt, g_ref[...], b_ref[...]) - t_ref[rows, :]
            loss_acc = loss_acc + 0.5 * jnp.sum(jnp.mean(err * err, axis=-1, keepdims=True), keepdims=True)
            do = err * (1.0 / D)
            dz, xh = _ln_bwd_rows(zt, g_ref[...], do)
            dg_acc = dg_acc + jnp.sum((do * xh).reshape(LN_ROWS // 8, 8, D), axis=0)
            db_acc = db_acc + jnp.sum(do.reshape(LN_ROWS // 8, 8, D), axis=0)
            dz_ref[rows, :] = dz
            dzb_ref[rows, :] = dz.astype(BF16)
        dg_ref[...] += jnp.sum(dg_acc, axis=0, keepdims=True)
        db_ref[...] += jnp.sum(db_acc, axis=0, keepdims=True)
        loss_ref[...] += loss_acc

    row = lambda i: (i, 0)
    fix = lambda i: (0, 0)
    return pl.pallas_call(
        body,
        out_shape=[jax.ShapeDtypeStruct((S, D), F32), jax.ShapeDtypeStruct((S, D), BF16),
                   jax.ShapeDtypeStruct((1, D), F32), jax.ShapeDtypeStruct((1, D), F32),
                   jax.ShapeDtypeStruct((8, 128), F32)],
        grid=(S // ts,),
        in_specs=[pl.BlockSpec((ts, D), row), pl.BlockSpec((1, D), fix), pl.BlockSpec((1, D), fix),
                  pl.BlockSpec((ts, D), row)],
        out_specs=[pl.BlockSpec((ts, D), row), pl.BlockSpec((ts, D), row), pl.BlockSpec((1, D), fix),
                   pl.BlockSpec((1, D), fix), pl.BlockSpec((8, 128), fix)],
        compiler_params=_cparams(("arbitrary",)),
        name=name,
    )(z, ln_g.reshape(1, D), ln_b.reshape(1, D), target)


def _ple_bwd(dz, gate, proj, *, ts=256, name):
    S, D = dz.shape

    def body(dz_ref, g_ref, p_ref, ds_ref, dp_ref, db_ref):
        @pl.when(pl.program_id(0) == 0)
        def _():
            db_ref[...] = jnp.zeros_like(db_ref)

        db_acc = jnp.zeros((8, D), F32)
        for r0 in range(0, ts, LN_ROWS):
            rows = pl.ds(r0, LN_ROWS)
            dzt = dz_ref[rows, :]
            g = g_ref[rows, :]
            ds = dzt * p_ref[rows, :] * g * (1.0 - g)
            ds_ref[rows, :] = ds.astype(BF16)
            dp_ref[rows, :] = (dzt * g).astype(BF16)
            db_acc = db_acc + jnp.sum(ds.reshape(LN_ROWS // 8, 8, D), axis=0)
        db_ref[...] += jnp.sum(db_acc, axis=0, keepdims=True)

    row = lambda i: (i, 0)
    return pl.pallas_call(
        body,
        out_shape=[jax.ShapeDtypeStruct((S, D), BF16), jax.ShapeDtypeStruct((S, D), BF16),
                   jax.ShapeDtypeStruct((1, D), F32)],
        grid=(S // ts,),
        in_specs=[pl.BlockSpec((ts, D), row)] * 3,
        out_specs=[pl.BlockSpec((ts, D), row), pl.BlockSpec((ts, D), row), pl.BlockSpec((1, D), lambda i: (0, 0))],
        compiler_params=_cparams(("arbitrary",)),
        name=name,
    )(dz, gate, proj)


HEAD_PAIR = 2 * HEAD_DIM


ATT_ROWS = 32
ATT_SCALE = HEAD_DIM ** -0.5


def _softmax_piece(scores, bias, qb):
    s = scores + bias
    kpos = qb * Q_BLOCK + lax.broadcasted_iota(jnp.int32, (1, KV_SPAN), 1)
    s = jnp.where(kpos >= KV_PAD, s, NEG_INF)
    e = jnp.exp(s - jnp.max(s, axis=-1, keepdims=True))
    return e * (1.0 / jnp.sum(e, axis=-1, keepdims=True))


def _pad_keys(qb, k_ref, v_ref, kp, vp):
    @pl.when(qb == 0)
    def _():
        kp[pl.ds(0, KV_PAD), :] = jnp.zeros((KV_PAD, HEAD_PAIR), BF16)
        vp[pl.ds(0, KV_PAD), :] = jnp.zeros((KV_PAD, HEAD_PAIR), BF16)
        kp[pl.ds(KV_PAD, k_ref.shape[0]), :] = k_ref[...]
        vp[pl.ds(KV_PAD, v_ref.shape[0]), :] = v_ref[...]


def _attn_fwd(qkv, bias):
    S = qkv.shape[0]
    nhp = N_HEADS // 2

    def body(q_ref, k_ref, v_ref, b_ref, o_ref, kp, vp, p_scr):
        qb = pl.program_id(1)
        _pad_keys(qb, k_ref, v_ref, kp, vp)
        span = pl.ds(pl.multiple_of(qb * Q_BLOCK, Q_BLOCK), KV_SPAN)
        kc, vc = kp[span, :], vp[span, :]
        qt = q_ref[...] * ATT_SCALE
        first = lax.broadcasted_iota(jnp.int32, (1, HEAD_PAIR), 1) < HEAD_DIM
        scores = [_bdot(jnp.where(first if j == 0 else ~first, qt, jnp.zeros_like(qt)), kc, NT) for j in range(2)]
        outs = []
        for j in range(2):
            for r0 in range(0, Q_BLOCK, ATT_ROWS):
                rows = pl.ds(r0, ATT_ROWS)
                p_scr[j, rows, :] = _softmax_piece(scores[j][r0:r0 + ATT_ROWS], b_ref[j, rows, :], qb).astype(BF16)
            outs.append(_bdot(p_scr[j], vc))
        o_ref[...] = jnp.where(first, outs[0], outs[1]).astype(BF16)

    return pl.pallas_call(
        body,
        out_shape=jax.ShapeDtypeStruct((S, D_MODEL), BF16),
        grid=(nhp, S // Q_BLOCK),
        in_specs=[pl.BlockSpec((Q_BLOCK, HEAD_PAIR), lambda h, i: (i, h)),
                  pl.BlockSpec((S, HEAD_PAIR), lambda h, i: (0, nhp + h)),
                  pl.BlockSpec((S, HEAD_PAIR), lambda h, i: (0, 2 * nhp + h)),
                  pl.BlockSpec((2, Q_BLOCK, KV_SPAN), lambda h, i: (h, 0, 0))],
        out_specs=pl.BlockSpec((Q_BLOCK, HEAD_PAIR), lambda h, i: (i, h)),
        scratch_shapes=[pltpu.VMEM((KV_PAD + S, HEAD_PAIR), BF16), pltpu.VMEM((KV_PAD + S, HEAD_PAIR), BF16),
                        pltpu.VMEM((2, Q_BLOCK, KV_SPAN), BF16)],
        compiler_params=_cparams(("parallel", "arbitrary")),
        name="attn_fwd",
    )(qkv, qkv, qkv, bias)


def _attn_bwd(qkv, bias, do):
    S = qkv.shape[0]
    nhp = N_HEADS // 2
    nq = S // Q_BLOCK
    scale = HEAD_DIM ** -0.5

    def body(q_ref, k_ref, v_ref, b_ref, do_ref, dq_ref, dk_ref, dv_ref, db_ref, kp, vp, dka, dva,
             p_scr, ds_scr):
        qb = pl.program_id(1)
        _pad_keys(qb, k_ref, v_ref, kp, vp)

        @pl.when(qb == 0)
        def _():
            dka[...] = jnp.zeros_like(dka)
            dva[...] = jnp.zeros_like(dva)
            db_ref[...] = jnp.zeros_like(db_ref)

        span = pl.ds(pl.multiple_of(qb * Q_BLOCK, Q_BLOCK), KV_SPAN)
        kc, vc = kp[span, :], vp[span, :]
        qt, dot = q_ref[...] * ATT_SCALE, do_ref[...]
        first = lax.broadcasted_iota(jnp.int32, (1, HEAD_PAIR), 1) < HEAD_DIM
        dqs = []
        qs = [jnp.where(first if j == 0 else ~first, qt, jnp.zeros_like(qt)) for j in range(2)]
        dos = [jnp.where(first if j == 0 else ~first, dot, jnp.zeros_like(dot)) for j in range(2)]
        scores = [_bdot(qs[j], kc, NT) for j in range(2)]
        dps = [_bdot(dos[j], vc, NT) for j in range(2)]
        for j in range(2):
            qj, doj = qs[j], dos[j]
            for r0 in range(0, Q_BLOCK, ATT_ROWS):
                rows = pl.ds(r0, ATT_ROWS)
                p = _softmax_piece(scores[j][r0:r0 + ATT_ROWS], b_ref[j, rows, :], qb)
                dp = dps[j][r0:r0 + ATT_ROWS]
                ds = p * (dp - jnp.sum(p * dp, axis=-1, keepdims=True))
                db_ref[j, rows, :] += ds
                p_scr[j, rows, :] = p.astype(BF16)
                ds_scr[j, rows, :] = ds.astype(BF16)
            dva[span, :] += _bdot(p_scr[j], doj, TN)
            dqs.append(_bdot(ds_scr[j], kc))
            dka[span, :] += _bdot(ds_scr[j], qj, TN)
        dq_ref[...] = (scale * jnp.where(first, dqs[0], dqs[1])).astype(BF16)

        @pl.when(qb == nq - 1)
        def _():
            dk_ref[...] = dka[pl.ds(KV_PAD, S), :].astype(BF16)
            dv_ref[...] = dva[pl.ds(KV_PAD, S), :].astype(BF16)

    blk = pl.BlockSpec((Q_BLOCK, HEAD_PAIR), lambda h, i: (i, h))
    col = pl.BlockSpec((S, HEAD_PAIR), lambda h, i: (0, h))
    bsp = pl.BlockSpec((2, Q_BLOCK, KV_SPAN), lambda h, i: (h, 0, 0))
    return pl.pallas_call(
        body,
        out_shape=[jax.ShapeDtypeStruct((S, D_MODEL), BF16)] * 3
        + [jax.ShapeDtypeStruct((N_HEADS, Q_BLOCK, KV_SPAN), F32)],
        grid=(nhp, nq),
        in_specs=[blk, pl.BlockSpec((S, HEAD_PAIR), lambda h, i: (0, nhp + h)),
                  pl.BlockSpec((S, HEAD_PAIR), lambda h, i: (0, 2 * nhp + h)), bsp, blk],
        out_specs=[blk, col, col, bsp],
        scratch_shapes=[pltpu.VMEM((KV_PAD + S, HEAD_PAIR), BF16), pltpu.VMEM((KV_PAD + S, HEAD_PAIR), BF16),
                        pltpu.VMEM((KV_PAD + S, HEAD_PAIR), F32), pltpu.VMEM((KV_PAD + S, HEAD_PAIR), F32),
                        pltpu.VMEM((2, Q_BLOCK, KV_SPAN), BF16), pltpu.VMEM((2, Q_BLOCK, KV_SPAN), BF16)],
        compiler_params=_cparams(("parallel", "arbitrary")),
        name="attn_bwd",
    )(qkv, qkv, qkv, bias, do)


N_DIST = BAND + CHUNK - 1
N_FAR = KV_PAD + CHUNK - MAX_REL


def _shear_rows(x, towards_right):
    row = lax.broadcasted_iota(jnp.int32, (Q_BLOCK, 1), 0)
    for bit in range(Q_BLOCK.bit_length() - 1):
        step = 1 << bit
        x = jnp.where((row & step) != 0, pltpu.roll(x, step if towards_right else KV_SPAN - step, 1), x)
    return x


def _bias_blocks(rel_bias):
    H = rel_bias.shape[0]
    e = jnp.concatenate([jnp.broadcast_to(rel_bias[:, 2 * MAX_REL:], (H, N_FAR)),
                         jnp.flip(rel_bias[:, 2 * MAX_REL - (N_DIST - N_FAR):2 * MAX_REL], axis=1),
                         jnp.zeros((H, KV_SPAN - N_DIST), F32)], axis=1).reshape(H, 1, KV_SPAN)

    def body(e_ref, o_ref):
        first = pltpu.roll(jnp.broadcast_to(e_ref[...], (Q_BLOCK, KV_SPAN)), KV_SPAN - (CHUNK - 1), 1)
        x = _shear_rows(first, True)
        row = lax.broadcasted_iota(jnp.int32, (Q_BLOCK, 1), 0)
        chunk0 = row - (row & (CHUNK - 1))
        k = lax.broadcasted_iota(jnp.int32, (1, KV_SPAN), 1)
        o_ref[...] = jnp.where((k >= chunk0) & (k < chunk0 + BAND), x, NEG_INF)

    return pl.pallas_call(
        body,
        out_shape=jax.ShapeDtypeStruct((H, Q_BLOCK, KV_SPAN), F32),
        grid=(H,),
        in_specs=[pl.BlockSpec((None, 1, KV_SPAN), lambda h: (h, 0, 0))],
        out_specs=pl.BlockSpec((None, Q_BLOCK, KV_SPAN), lambda h: (h, 0, 0)),
        compiler_params=_cparams(("parallel",)),
        name="bias_blocks",
    )(e)


def _bias_blocks_grad(dblk):
    H = dblk.shape[0]

    def body(d_ref, o_ref):
        x = pltpu.roll(_shear_rows(d_ref[...], False), CHUNK - 1, 1)
        de = jnp.sum(x, axis=0, keepdims=True)
        lane = lax.broadcasted_iota(jnp.int32, de.shape, 1)
        far = jnp.sum(jnp.where(lane < N_FAR, de, 0.0), axis=-1, keepdims=True)
        o_ref[...] = jnp.where(lane == 0, far, jnp.where(lane < N_FAR, 0.0, de))

    de = pl.pallas_call(
        body,
        out_shape=jax.ShapeDtypeStruct((H, 1, KV_SPAN), F32),
        grid=(H,),
        in_specs=[pl.BlockSpec((None, Q_BLOCK, KV_SPAN), lambda h: (h, 0, 0))],
        out_specs=pl.BlockSpec((None, 1, KV_SPAN), lambda h: (h, 0, 0)),
        compiler_params=_cparams(("parallel",)),
        name="bias_grad_sum",
    )(dblk).reshape(H, KV_SPAN)
    near = jnp.flip(de[:, N_FAR:N_DIST], axis=1)
    return jnp.concatenate([jnp.zeros((H, 2 * MAX_REL - (N_DIST - N_FAR)), F32), near, de[:, 0:1]], axis=1)


def _ffn_forward(r1, r1b, p_l, w, l, ready):
    ready(f"up{l}", r1b)
    up_g = _mm_rows([(r1b, w["ffn_up_t"][l], True, (0, 2))], out_dtype=BF16, name=f"ffn_up_g{l}")
    up_v = _mm_rows([(r1b, w["ffn_up_t"][l], True, (1, 2))], out_dtype=BF16, name=f"ffn_up_v{l}")
    h = _ffn_act_fwd(up_g, up_v, w["ffn_dw_w"][l], w["ffn_dw_b"][l], name=f"ffn_act{l}")
    ready(f"dn{l}", h)
    z2, r2, r2b, gate, proj = _proj_ln(r1, h, w["ffn_w_down"][l], w["ln_ffn_g"][l], w["ln_ffn_b"][l],
                                       ple=(w["ple_w_gate"][l], w["ple_b_gate"][l], p_l, w["ple_w_proj"][l]),
                                       name=f"ffn_down_ln{l}")
    return dict(r1b=r1b, up_g=up_g, up_v=up_v, h=h, z2=z2, gate=gate, proj=proj), r2, r2b


def _ffn_backward(sv, dz2, dz2b, p_l, w, l, grads, ln_bwd, emit):
    r1b = sv["r1b"]
    ds, dproj, db_gate = _ple_bwd(dz2, sv["gate"], sv["proj"], name=f"ple_bwd{l}")
    dh = _mm_rows([(dz2b, w["ffn_w_down"][l], True, WHOLE)], out_dtype=BF16, name=f"ffn_dh{l}")
    dgate, dval, d_dw_w, d_dw_b = _ffn_act_bwd(sv["up_g"], sv["up_v"], dh, w["ffn_dw_w"][l], w["ffn_dw_b"][l],
                                               name=f"ffn_act_bwd{l}")
    grads["ffn_w_down"][l] = _wgrad(sv["h"], dz2b, tm=1408, name=f"d_ffn_w_down{l}")
    d_up_g = _wgrad(dgate, r1b, tm=1408, part=(0, 2), name=f"d_ffn_up_g{l}")
    grads["ffn_up_t"][l] = _wgrad(dval, r1b, tm=1408, part=(1, 2), into=d_up_g, name=f"d_ffn_up_v{l}")
    grads["ple_w_gate"][l] = _wgrad(r1b, ds, name=f"d_ple_w_gate{l}")
    grads["ple_w_proj"][l] = _wgrad(p_l, dproj, piece=D_MODEL // N_DEV, name=f"d_ple_w_proj{l}")
    grads["ffn_dw_w"][l] = d_dw_w
    grads["ffn_dw_b"][l] = d_dw_b[0]
    grads["ple_b_gate"][l] = db_gate[0]
    return _mm_rows([(ds, w["ple_w_gate"][l], True, WHOLE), (dgate, w["ffn_up_t"][l], False, (0, 2)),
                     (dval, w["ffn_up_t"][l], False, (1, 2))], add=dz2, add_scale=ALPHA, ln_bwd=ln_bwd, dep=emit(),
                    name=f"dr1_{l}")


def _local_step(x, p, target, w, ready=lambda group, after: None, emit=lambda group, grads: None):
    grads = {k: [None, None] for k in ("ffn_w_down", "ffn_up_t", "ple_w_gate", "ple_w_proj", "ffn_dw_w",
                                       "ffn_dw_b", "ple_b_gate", "ln_ffn_g", "ln_ffn_b", "ln_mix_g", "ln_mix_b")}

    xb, pb = x.astype(BF16), p.astype(BF16)
    ready("mix", None)
    u = _mm_rows([(xb, w["mix_w_in_t"], True, WHOLE)], name="mix_in")
    ycat, dpool = _mixer_fwd(u, w["pool_w"], w["pool_scale"], w["conv_dw_w"], w["conv_dw_b"], w["conv_ln_g"],
                             w["conv_ln_b"])
    ready("mixo", ycat)
    z1, r1, r1b = _proj_ln(x, ycat, w["mix_w_out"], w["ln_mix_g"][0], w["ln_mix_b"][0], name="mix_out_ln")
    sv0, r2, r2b = _ffn_forward(r1, r1b, pb[0], w, 0, ready)

    ready("attn", r2b)
    qkv = _mm_rows([(r2b, w["attn_w_qkv"], False, WHOLE)], out_dtype=BF16, name="attn_qkv")
    bias = _bias_blocks(w["attn_rel_bias"])
    attn = _attn_fwd(qkv, bias)
    z3, r3, r3b = _proj_ln(r2, attn, w["attn_w_o"], w["ln_mix_g"][1], w["ln_mix_b"][1], name="attn_out_ln")
    sv1, _, _ = _ffn_forward(r3, r3b, pb[1], w, 1, ready)

    dz4, dz4b, grads["ln_ffn_g"][1], grads["ln_ffn_b"][1], loss = _loss_ln_bwd(
        sv1["z2"], w["ln_ffn_g"][1], w["ln_ffn_b"][1], target, name="loss_ln_bwd")
    dz3, dz3b, grads["ln_mix_g"][1], grads["ln_mix_b"][1] = _ffn_backward(
        sv1, dz4, dz4b, pb[1], w, 1, grads, (z3, w["ln_mix_g"][1]), lambda: emit("ffn1", grads))
    grads["attn_w_o"] = _wgrad(attn, dz3b, name="d_attn_w_o")
    dattn = _mm_rows([(dz3b, w["attn_w_o"], True, WHOLE)], out_dtype=BF16, name="d_attn")
    dq, dk, dv, dbias = _attn_bwd(qkv, bias, dattn)
    grads["attn_rel_bias"] = _bias_blocks_grad(dbias)
    dqkv = jnp.concatenate([dq, dk, dv], axis=1)
    grads["attn_w_qkv"] = _wgrad(r2b, dqkv, tn=768, piece=3 * D_MODEL // N_DEV, name="d_attn_w_qkv")
    dz2, dz2b, grads["ln_ffn_g"][0], grads["ln_ffn_b"][0] = _mm_rows(
        [(dqkv, w["attn_w_qkv"], True, WHOLE)], add=dz3, add_scale=ALPHA, ln_bwd=(sv0["z2"], w["ln_ffn_g"][0]),
        dep=emit("attn", grads), name="dr2")
    dz1, dz1b, grads["ln_mix_g"][0], grads["ln_mix_b"][0] = _ffn_backward(
        sv0, dz2, dz2b, pb[0], w, 0, grads, (z1, w["ln_mix_g"][0]), lambda: emit("ffn0", grads))
    grads["mix_w_out"] = _wgrad(ycat, dz1b, name="d_mix_w_out")
    dycat = _mm_rows([(dz1b, w["mix_w_out"], True, WHOLE)], name="d_ycat")
    du, g_pw, g_ps, g_cw, g_cb, g_cg, g_cbb = _mixer_bwd(u, dpool, dycat, w["pool_w"], w["pool_scale"],
                                                         w["conv_dw_w"], w["conv_dw_b"], w["conv_ln_g"],
                                                         w["conv_ln_b"])
    grads["mix_w_in_t"] = _wgrad(du, xb, name="d_mix_w_in")
    grads.update(pool_w=g_pw, pool_scale=g_ps[0], conv_dw_w=g_cw, conv_dw_b=g_cb[0], conv_ln_g=g_cg[0],
                 conv_ln_b=g_cbb[0])
    for kname in ("ln_ffn_g", "ln_ffn_b", "ln_mix_g", "ln_mix_b"):
        grads[kname] = [a[0] for a in grads[kname]]
    grad_x = _mm_rows([(du, w["mix_w_in_t"], False, WHOLE)], add=dz1, add_scale=ALPHA, dep=emit("mix", grads),
                      name="grad_x")
    return loss[0, 0], grad_x, grads


_HBM = pl.BlockSpec(memory_space=pltpu.HBM)
_SEM = pl.BlockSpec(memory_space=pltpu.SEMAPHORE)
_EFFECT = pltpu.SideEffectType.DATAFLOW_SIDE_EFFECTING


def _slot(ref, place, shape, k):
    if place in ("stack", "pieces"):
        return ref.at[k]
    ax = place[1]
    n = shape[ax]
    return ref.at[(slice(None),) * ax + (pl.ds(pl.multiple_of(k * n, n), n),)]


def _result_shape(buf, place):
    if place == "stack":
        return (N_DEV,) + buf.shape
    if place == "pieces":
        return buf.shape
    return tuple(s * N_DEV if i == place[1] else s for i, s in enumerate(buf.shape))


def _peers(x, y, c):
    for d in range(1, N_DEV):
        px, py, pc = x ^ ((d >> 2) & 1), y ^ ((d >> 1) & 1), c ^ (d & 1)
        yield d, (px, py, pc), 4 * px + 2 * py + pc


def _exchange_start(bufs, places, after, *, name):
    nb = len(bufs)
    lands = [lax.empty(_result_shape(b, p_), b.dtype) for b, p_ in zip(bufs, places)]
    has_after = after is not None

    def body(*refs):
        srcs, dsts = refs[:nb], refs[nb:2 * nb]
        outs = refs[2 * nb + has_after:]
        send_sems, recv_sems, token = outs[0], outs[1], outs[2 + 2 * nb]
        x, y, c = lax.axis_index("x"), lax.axis_index("y"), lax.axis_index("c")
        me = 4 * x + 2 * y + c
        for b in range(nb):
            for d, dev, peer in _peers(x, y, c):
                pltpu.make_async_remote_copy(
                    src_ref=srcs[b].at[peer] if places[b] == "pieces" else srcs[b],
                    dst_ref=_slot(dsts[b], places[b], bufs[b].shape, me),
                    send_sem=send_sems.at[b * N_DEV + d], recv_sem=recv_sems.at[b * N_DEV + d],
                    device_id=dev, device_id_type=pl.DeviceIdType.MESH).start()
            pltpu.make_async_copy(srcs[b].at[me] if places[b] == "pieces" else srcs[b],
                                  _slot(dsts[b], places[b], bufs[b].shape, me), recv_sems.at[b * N_DEV]).start()
        token[...] = jnp.zeros_like(token)

    sems = pltpu.SemaphoreType.DMA((nb * N_DEV,))
    ins = [pltpu.with_memory_space_constraint(a, pltpu.HBM) for a in list(bufs) + lands]
    out = pl.pallas_call(
        body,
        out_shape=(sems, sems, *[pltpu.HBM(a.shape, a.dtype) for a in ins], jax.ShapeDtypeStruct((8, 128), F32)),
        in_specs=[_HBM] * (2 * nb) + ([pl.BlockSpec(memory_space=pl.ANY)] if has_after else []),
        out_specs=(_SEM, _SEM, *[_HBM] * (2 * nb), pl.BlockSpec(memory_space=pltpu.VMEM)),
        input_output_aliases={i: 2 + i for i in range(2 * nb)},
        compiler_params=pltpu.CompilerParams(has_side_effects=_EFFECT),
        name=name,
    )(*ins, *([after] if has_after else []))
    return dict(send=out[0], recv=out[1], srcs=out[2:2 + nb], lands=out[2 + nb:2 + 2 * nb], token=out[-1],
                places=places)


def _exchange_wait(h, after, *, name):
    nb = len(h["srcs"])
    places = h["places"]
    shapes = [a.shape for a in h["srcs"]]

    def body(*refs):
        srcs, dsts, send_sems, recv_sems = refs[:nb], refs[nb:2 * nb], refs[2 * nb], refs[2 * nb + 1]
        x, y, c = lax.axis_index("x"), lax.axis_index("y"), lax.axis_index("c")
        me = 4 * x + 2 * y + c
        for b in range(nb):
            pieces = places[b] == "pieces"
            for d, dev, peer in _peers(x, y, c):
                cp = pltpu.make_async_remote_copy(
                    src_ref=srcs[b].at[peer] if pieces else srcs[b],
                    dst_ref=_slot(dsts[b], places[b], shapes[b], peer),
                    send_sem=send_sems.at[b * N_DEV + d], recv_sem=recv_sems.at[b * N_DEV + d],
                    device_id=dev, device_id_type=pl.DeviceIdType.MESH)
                cp.wait_send()
                cp.wait_recv()
            pltpu.make_async_copy(srcs[b].at[me] if pieces else srcs[b], _slot(dsts[b], places[b], shapes[b], me),
                                  recv_sems.at[b * N_DEV]).wait()

    ins = list(h["srcs"]) + list(h["lands"])
    out = pl.pallas_call(
        body,
        out_shape=tuple(pltpu.HBM(a.shape, a.dtype) for a in ins),
        in_specs=[_HBM] * (2 * nb) + [_SEM, _SEM, pl.BlockSpec(memory_space=pl.ANY)],
        out_specs=tuple([_HBM] * (2 * nb)),
        input_output_aliases={i: i for i in range(2 * nb)},
        compiler_params=pltpu.CompilerParams(has_side_effects=_EFFECT),
        name=name,
    )(*ins, h["send"], h["recv"], after)
    return out[nb:]


def _adamw(recv, w, m, v, *, layer=0, into=None, name):
    L, R, C = w.shape
    tr = R
    for cand in (512, 256, 128, 64, 32, 16):
        if R % cand == 0 and cand * C * 4 <= 2 * 1024 * 1024:
            tr = cand
            break
    c1 = 1.0 - ADAM_B1 ** ADAM_STEP
    c2 = 1.0 - ADAM_B2 ** ADAM_STEP

    def body(r_ref, w_ref, m_ref, v_ref, *rest):
        g_ref, d_ref, mo_ref, vo_ref = rest[-4:]
        g = r_ref[0].astype(F32)
        for i in range(1, N_DEV):
            g = g + r_ref[i].astype(F32)
        m_new = ADAM_B1 * m_ref[...] + (1.0 - ADAM_B1) * g
        v_new = ADAM_B2 * v_ref[...] + (1.0 - ADAM_B2) * (g * g)
        m_hat = m_new / c1
        v_hat = v_new / c2
        g_ref[...] = g
        d_ref[...] = -ADAM_LR * (m_hat / (jnp.sqrt(v_hat) + ADAM_EPS) + ADAM_WD * w_ref[...])
        mo_ref[...] = m_new
        vo_ref[...] = v_new

    row = pl.BlockSpec((None, tr, C), lambda i: (layer, i, 0))
    others = [] if into is None else list(into)
    return pl.pallas_call(
        body,
        out_shape=[jax.ShapeDtypeStruct((L, R, C), F32)] * 4,
        grid=(R // tr,),
        in_specs=[pl.BlockSpec((N_DEV, tr, C), lambda i: (0, i, 0)), row, row, row]
        + [pl.BlockSpec(memory_space=pl.ANY)] * len(others),
        out_specs=[row] * 4,
        input_output_aliases={4 + k: k for k in range(len(others))},
        compiler_params=_cparams(("parallel",)),
        name=name,
    )(recv, w, m, v, *others)


_TRANSPOSED = ("mix_w_in", "ffn_w_up")


def _ffn_groups(l):
    return ((f"up{l}", (("ffn_w_up", l, BF16, ("axis", 0)), ("ffn_dw_w", l, F32, "stack"))),
            (f"dn{l}", (("ffn_w_down", l, BF16, ("axis", 0)), ("ple_w_gate", l, BF16, ("axis", 0)),
                        ("ple_w_proj", l, BF16, ("axis", 1)))))


_GATHER_GROUPS = (
    ("mix", (("mix_w_in", 0, BF16, ("axis", 0)), ("conv_dw_w", 0, F32, "stack"))),
    ("mixo", (("mix_w_out", 0, BF16, ("axis", 0)),)),
    *_ffn_groups(0),
    ("attn", (("attn_w_qkv", 0, BF16, ("axis", 1)), ("attn_w_o", 0, BF16, ("axis", 0)))),
    *_ffn_groups(1))
_SHARDED = ("mix_w_in", "conv_dw_w", "mix_w_out", "attn_w_qkv", "attn_w_o", "ffn_w_up", "ffn_dw_w", "ffn_w_down",
            "ple_w_gate", "ple_w_proj")
_REPLICATED = ("pool_w", "pool_scale", "conv_dw_b", "conv_ln_g", "conv_ln_b", "attn_rel_bias", "ln_mix_g",
               "ln_mix_b", "ffn_dw_b", "ple_b_gate", "ln_ffn_g", "ln_ffn_b")


def _pack_rows(parts, row_mult, dtype):
    lead = parts[0].shape[:-1]
    flat = jnp.concatenate([a.astype(dtype) for a in parts], axis=-1)
    n = flat.shape[-1]
    unit = row_mult * LANES
    padded = -(-n // unit) * unit
    flat = jnp.pad(flat, [(0, 0)] * len(lead) + [(0, padded - n)])
    return flat.reshape(lead + (padded // LANES, LANES))


def _unpack(flat2d, shapes):
    flat = flat2d.reshape(-1)
    out, o = [], 0
    for s in shapes:
        n = math.prod(s)
        out.append(flat[o:o + n].reshape(s))
        o += n
    return out


def _full_from_shards(g, axis):
    parts = jnp.moveaxis(g, 0, axis)
    shp = list(g.shape[1:])
    shp[axis] *= g.shape[0]
    return parts.reshape(shp)


def _pieces_from_full(full, axis, k=N_DEV):
    shp = list(full.shape)
    n = shp[axis] // k
    t = full.reshape(shp[:axis] + [k, n] + shp[axis + 1:])
    return jnp.moveaxis(t, axis, 0)


def kernel(x, p, mix_w_in, pool_w, pool_scale, conv_dw_w, conv_dw_b, conv_ln_g, conv_ln_b, mix_w_out, attn_w_qkv, attn_rel_bias, attn_w_o, ln_mix_g, ln_mix_b, ffn_w_up, ffn_dw_w, ffn_dw_b, ffn_w_down, ple_w_proj, ple_w_gate, ple_b_gate, ln_ffn_g, ln_ffn_b, loss_target, m_mix_w_in, m_pool_w, m_pool_scale, m_conv_dw_w, m_conv_dw_b, m_conv_ln_g, m_conv_ln_b, m_mix_w_out, m_attn_w_qkv, m_attn_rel_bias, m_attn_w_o, m_ln_mix_g, m_ln_mix_b, m_ffn_w_up, m_ffn_dw_w, m_ffn_dw_b, m_ffn_w_down, m_ple_w_proj, m_ple_w_gate, m_ple_b_gate, m_ln_ffn_g, m_ln_ffn_b, v_mix_w_in, v_pool_w, v_pool_scale, v_conv_dw_w, v_conv_dw_b, v_conv_ln_g, v_conv_ln_b, v_mix_w_out, v_attn_w_qkv, v_attn_rel_bias, v_attn_w_o, v_ln_mix_g, v_ln_mix_b, v_ffn_w_up, v_ffn_dw_w, v_ffn_dw_b, v_ffn_w_down, v_ple_w_proj, v_ple_w_gate, v_ple_b_gate, v_ln_ffn_g, v_ln_ffn_b):
    a = dict(locals())
    sh_names = list(_SHARDED)
    names = sh_names + list(_REPLICATED)
    wts = {n: a[n] for n in names}
    mom = {n: a["m_" + n] for n in names}
    var = {n: a["v_" + n] for n in names}

    for n in _TRANSPOSED:
        wts[n], mom[n], var[n] = (jnp.swapaxes(d[n], 1, 2) for d in (wts, mom, var))
    gather = {}
    token = None
    for group, items in _GATHER_GROUPS:
        gather[group] = _exchange_start([wts[n][l].astype(dt) for n, l, dt, _ in items], [pl_ for *_, pl_ in items],
                                        token, name="gather_start_" + group)
        token = gather[group]["token"]

    w = dict(pool_w=pool_w[0], pool_scale=pool_scale[0], conv_dw_b=conv_dw_b[0], conv_ln_g=conv_ln_g[0],
             conv_ln_b=conv_ln_b[0], attn_rel_bias=attn_rel_bias[0], ln_mix_g=ln_mix_g, ln_mix_b=ln_mix_b,
             ffn_dw_b=ffn_dw_b, ple_b_gate=ple_b_gate, ln_ffn_g=ln_ffn_g, ln_ffn_b=ln_ffn_b)
    for n in ("ffn_up_t", "ffn_dw_w", "ffn_w_down", "ple_w_gate", "ple_w_proj"):
        w[n] = [None, None]

    def ready(group, after):
        got = _exchange_wait(gather[group], token if after is None else after, name="gather_wait_" + group)
        if group == "mix":
            w["mix_w_in_t"], w["conv_dw_w"] = got[0], _full_from_shards(got[1], 1)
        elif group == "mixo":
            (w["mix_w_out"],) = got
        elif group == "attn":
            w["attn_w_qkv"], w["attn_w_o"] = got
        elif group[:2] == "up":
            l = int(group[2])
            w["ffn_up_t"][l], w["ffn_dw_w"][l] = got[0], _full_from_shards(got[1], 1)
        else:
            l = int(group[2])
            w["ffn_w_down"][l], w["ple_w_gate"][l], w["ple_w_proj"][l] = got

    scatter = {}

    def emit(group, gr):
        if group[:3] == "ffn":
            l = int(group[3])
            pieces = [_pieces_from_full(gr["ffn_up_t"][l], 0),
                      _pieces_from_full(gr["ffn_dw_w"][l], 1), _pieces_from_full(gr["ffn_w_down"][l], 0),
                      _pieces_from_full(gr["ple_w_gate"][l], 0), gr["ple_w_proj"][l]]
        elif group == "attn":
            pieces = [gr["attn_w_qkv"], _pieces_from_full(gr["attn_w_o"], 0)]
        else:
            pieces = [_pieces_from_full(gr["mix_w_in_t"], 0), _pieces_from_full(gr["conv_dw_w"], 1),
                      _pieces_from_full(gr["mix_w_out"], 0)]
        scatter[group] = _exchange_start([a.astype(BF16) for a in pieces], ["pieces"] * len(pieces), None,
                                         name="grad_start_" + group)
        if group != "mix":
            return scatter[group]["token"]
        gfull = dict(
            pool_w=gr["pool_w"][None], pool_scale=gr["pool_scale"][None], conv_dw_b=gr["conv_dw_b"][None],
            conv_ln_g=gr["conv_ln_g"][None], conv_ln_b=gr["conv_ln_b"][None],
            attn_rel_bias=gr["attn_rel_bias"][None], ln_mix_g=jnp.stack(gr["ln_mix_g"]),
            ln_mix_b=jnp.stack(gr["ln_mix_b"]), ffn_dw_b=jnp.stack(gr["ffn_dw_b"]),
            ple_b_gate=jnp.stack(gr["ple_b_gate"]), ln_ffn_g=jnp.stack(gr["ln_ffn_g"]),
            ln_ffn_b=jnp.stack(gr["ln_ffn_b"]))
        rep_send = _pack_rows([gfull[n].reshape(-1) for n in _REPLICATED], 8, F32)
        scatter["replicated"] = _exchange_start([rep_send], ["stack"], scatter[group]["token"],
                                                name="grad_start_replicated")
        return scatter["replicated"]["token"]

    loss_part, grad_x, gr = _local_step(x[0], p[:, 0], loss_target[0], w, ready, emit)
    loss = lax.psum(loss_part, ("x", "y", "c"))

    group_weights = {"ffn1": (("ffn_w_up", 1), ("ffn_dw_w", 1), ("ffn_w_down", 1), ("ple_w_gate", 1), ("ple_w_proj", 1)),
                     "attn": (("attn_w_qkv", 0), ("attn_w_o", 0)),
                     "ffn0": (("ffn_w_up", 0), ("ffn_dw_w", 0), ("ffn_w_down", 0), ("ple_w_gate", 0), ("ple_w_proj", 0)),
                     "mix": (("mix_w_in", 0), ("conv_dw_w", 0), ("mix_w_out", 0))}
    updated = {}
    after = grad_x
    for group in ("ffn1", "attn", "ffn0", "mix"):
        recv = _exchange_wait(scatter[group], after, name="grad_wait_" + group)
        for (n, l), r in zip(group_weights[group], recv):
            updated[n] = _adamw(r, wts[n], mom[n], var[n], layer=l, into=updated.get(n), name=f"adamw_{n}{l}")
            after = updated[n][0]
    res = [{n: jnp.swapaxes(updated[n][k], 1, 2) if n in _TRANSPOSED else updated[n][k] for n in sh_names}
           for k in range(4)]
    (rep_recv,) = _exchange_wait(scatter["replicated"], after, name="grad_wait_replicated")

    def flat_state(d):
        return _pack_rows([d[n].reshape(-1) for n in _REPLICATED], 8, F32)[None]

    rep_out = _adamw(rep_recv, flat_state(wts), flat_state(mom), flat_state(var), name="adamw_replicated")
    for k in range(4):
        for n, arr in zip(_REPLICATED, _unpack(rep_out[k][0], [wts[n].shape for n in _REPLICATED])):
            res[k][n] = arr
    order = ["mix_w_in", "pool_w", "pool_scale", "conv_dw_w", "conv_dw_b", "conv_ln_g", "conv_ln_b", "mix_w_out",
             "attn_w_qkv", "attn_rel_bias", "attn_w_o", "ln_mix_g", "ln_mix_b", "ffn_w_up", "ffn_dw_w", "ffn_dw_b",
             "ffn_w_down", "ple_w_proj", "ple_w_gate", "ple_b_gate", "ln_ffn_g", "ln_ffn_b"]
    outs = [loss, grad_x[None]]
    for k in range(4):
        outs += [res[k][n] for n in order]
    return tuple(outs)
```

```python
import functools
import math

import jax
import jax.numpy as jnp
from jax import lax
from jax.experimental import pallas as pl
from jax.experimental.pallas import tpu as pltpu

F32 = jnp.float32
BF16 = jnp.bfloat16

N_DEV = 8
D_MODEL = 1024
D_POOL = 512
D_CONV = 512
POOL_WINDOWS = (2, 4, 8, 16)
POOL_GROUP = 128
CONV_KERNEL = 31
CHUNK = 64
HEAD_DIM = 64
N_HEADS = 16
LEFT_CHUNKS = 8
BAND = (LEFT_CHUNKS + 1) * CHUNK
MAX_REL = 256
D_FF = 2816
PLE_DIM = 256
ALPHA = 4.0 ** 0.25
LN_EPS = 1e-5
NEG_INF = -1e30
ADAM_LR, ADAM_B1, ADAM_B2, ADAM_EPS, ADAM_WD, ADAM_STEP = 0.001, 0.9, 0.999, 1e-08, 0.01, 10

Q_BLOCK = 4 * CHUNK
KV_PAD = LEFT_CHUNKS * CHUNK
KV_SPAN = KV_PAD + Q_BLOCK
CONV_HALO = 32
FFN_HALO = 16
SUB_ROWS, SUB_LANES = 64, 128
LANES = 1024
VMEM_LIMIT = 56 * 1024 * 1024


def _cparams(sem=None):
    return pltpu.CompilerParams(dimension_semantics=sem, vmem_limit_bytes=VMEM_LIMIT)


def _tile(dim, pref):
    if dim <= pref:
        return dim
    t = pref - pref % 128
    while t >= 128:
        if dim % t == 0:
            return t
        t -= 128
    return dim


def _sigmoid(x):
    return 1.0 / (1.0 + jnp.exp(-x))


def _bdot(a, b, dn=(((1,), (0,)), ((), ()))):
    return lax.dot_general(a.astype(BF16), b.astype(BF16), dn, preferred_element_type=F32)


WHOLE = (0, 1)
NT = (((1,), (1,)), ((), ()))
TN = (((0,), (0,)), ((), ()))


def _wgrad(a, b, *, tm=1024, tn=1024, tk=1024, piece=None, part=(0, 1), into=None, name):
    K, M = a.shape
    kb, N = b.shape
    assert K == kb, (a.shape, b.shape)
    tm, tn, tk = _tile(M, tm), _tile(N, tn), _tile(K, tk)
    nk = K // tk
    per = 1 if piece is None else tn // piece
    assert piece is None or tn == per * piece

    def body(a_ref, b_ref, *rest):
        o_ref, acc = rest[-2:]
        k = pl.program_id(2)

        @pl.when(k == 0)
        def _():
            acc[...] = jnp.zeros_like(acc)

        acc[...] += _bdot(a_ref[...], b_ref[...], TN)

        @pl.when(k == nk - 1)
        def _():
            if piece is None:
                o_ref[...] = acc[...].astype(BF16)
            else:
                for s in range(per):
                    o_ref[s] = acc[:, s * piece:(s + 1) * piece].astype(BF16)

    if piece is None:
        first = part[0] * (M // tm)
        out_shape = (part[1] * M, N)
        out_spec = pl.BlockSpec((tm, tn), lambda i, j, k: (first + i, j))
    else:
        out_shape, out_spec = (N // piece, M, piece), pl.BlockSpec((per, tm, piece), lambda i, j, k: (j, i, 0))
    others = [] if into is None else [into]
    return pl.pallas_call(
        body,
        out_shape=jax.ShapeDtypeStruct(out_shape, BF16),
        grid=(M // tm, N // tn, nk),
        in_specs=[pl.BlockSpec((tk, tm), lambda i, j, k: (k, i)), pl.BlockSpec((tk, tn), lambda i, j, k: (k, j))]
        + [pl.BlockSpec(memory_space=pl.ANY)] * len(others),
        out_specs=out_spec,
        input_output_aliases={2: 0} if others else {},
        scratch_shapes=[pltpu.VMEM((tm, tn), F32)],
        compiler_params=_cparams(("parallel", "parallel", "arbitrary")),
        name=name,
    )(a, b, *others)


def _mm_rows(pairs, *, add=None, add_scale=1.0, out_dtype=F32, tm=256, dep=None, ln_bwd=None, name):
    M = pairs[0][0].shape[0]
    n = len(pairs)
    has_add = add is not None
    w_rows = [w_.shape[0] // part[1] for _, w_, _, part in pairs]
    N = w_rows[0] if pairs[0][2] else pairs[0][1].shape[1]

    def body(*refs):
        acc = None
        for i, (_, _, tr, _) in enumerate(pairs):
            part = _bdot(refs[2 * i][...], refs[2 * i + 1][...], NT if tr else (((1,), (0,)), ((), ())))
            acc = part if acc is None else acc + part
        if has_add:
            acc = acc + add_scale * refs[2 * n][...]
        if ln_bwd is None:
            refs[-1][...] = acc.astype(out_dtype)
            return
        z_ref, g_ref = refs[2 * n + has_add], refs[2 * n + has_add + 1]
        dz_ref, dzb_ref, dg_ref, db_ref = refs[-4:]

        @pl.when(pl.program_id(0) == 0)
        def _():
            dg_ref[...] = jnp.zeros_like(dg_ref)
            db_ref[...] = jnp.zeros_like(db_ref)

        dg_acc = jnp.zeros((8, N), F32)
        db_acc = jnp.zeros((8, N), F32)
        for r0 in range(0, tm, LN_ROWS):
            rows = pl.ds(r0, LN_ROWS)
            do = acc[r0:r0 + LN_ROWS]
            dz, xh = _ln_bwd_rows(z_ref[rows, :], g_ref[...], do)
            dz_ref[rows, :] = dz
            dzb_ref[rows, :] = dz.astype(BF16)
            dg_acc = dg_acc + jnp.sum((do * xh).reshape(LN_ROWS // 8, 8, N), axis=0)
            db_acc = db_acc + jnp.sum(do.reshape(LN_ROWS // 8, 8, N), axis=0)
        dg_ref[...] += jnp.sum(dg_acc, axis=0, keepdims=True)
        db_ref[...] += jnp.sum(db_acc, axis=0, keepdims=True)

    in_specs, args = [], []
    for (a, w_, _, part), rows in zip(pairs, w_rows):
        in_specs += [pl.BlockSpec((tm, a.shape[1]), lambda i: (i, 0)),
                     pl.BlockSpec((rows, w_.shape[1]), functools.partial(lambda i, j: (j, 0), j=part[0]))]
        args += [a, w_]
    row = pl.BlockSpec((tm, N), lambda i: (i, 0))
    fix = pl.BlockSpec((1, N), lambda i: (0, 0))
    if has_add:
        in_specs.append(row)
        args.append(add)
    if ln_bwd is not None:
        in_specs += [row, fix]
        args += [ln_bwd[0], ln_bwd[1].reshape(1, N)]
    if dep is not None:
        in_specs.append(pl.BlockSpec(memory_space=pl.ANY))
        args.append(dep)
    if ln_bwd is None:
        out_shape, out_specs = jax.ShapeDtypeStruct((M, N), out_dtype), row
    else:
        out_shape = [jax.ShapeDtypeStruct((M, N), F32), jax.ShapeDtypeStruct((M, N), BF16),
                     jax.ShapeDtypeStruct((1, N), F32), jax.ShapeDtypeStruct((1, N), F32)]
        out_specs = [row, row, fix, fix]
    return pl.pallas_call(
        body,
        out_shape=out_shape,
        grid=(M // tm,),
        in_specs=in_specs,
        out_specs=out_specs,
        compiler_params=_cparams(("parallel",) if ln_bwd is None else ("arbitrary",)),
        name=name,
    )(*args)


def _ln_bwd_rows(zt, g, do):
    zc = zt - jnp.mean(zt, axis=-1, keepdims=True)
    rstd = lax.rsqrt(jnp.mean(zc * zc, axis=-1, keepdims=True) + LN_EPS)
    xh = zc * rstd
    dxh = do * g
    return rstd * (dxh - jnp.mean(dxh, axis=-1, keepdims=True) - xh * jnp.mean(dxh * xh, axis=-1, keepdims=True)), xh


def _layer_norm_rows(z, g, b):
    mu = jnp.mean(z, axis=-1, keepdims=True)
    zc = z - mu
    var = jnp.mean(zc * zc, axis=-1, keepdims=True)
    return zc * lax.rsqrt(var + LN_EPS) * g + b


def _proj_ln(res, a, w, ln_g, ln_b, *, ple=None, ts=256, name):
    S, D = res.shape
    ka = a.shape[1]
    has_ple = ple is not None
    row = lambda i: (i, 0)
    fix = lambda i: (0, 0)

    def body(*refs):
        if has_ple:
            (res_ref, a_ref, w_ref, g_ref, b_ref, wg_ref, bg_ref, p_ref, wp_ref, z_ref, r_ref, rb_ref, gate_ref,
             proj_ref, acc) = refs
        else:
            res_ref, a_ref, w_ref, g_ref, b_ref, z_ref, r_ref, rb_ref, acc = refs
        acc[...] = _bdot(a_ref[...], w_ref[...])
        if has_ple:
            gate_ref[...] = _bdot(res_ref[...], wg_ref[...])
            proj_ref[...] = _bdot(p_ref[...], wp_ref[...])
        for r0 in range(0, ts, LN_ROWS):
            rows = pl.ds(r0, LN_ROWS)
            z = ALPHA * res_ref[rows, :] + acc[rows, :]
            if has_ple:
                gate = _sigmoid(gate_ref[rows, :] + bg_ref[...])
                gate_ref[rows, :] = gate
                z = z + gate * proj_ref[rows, :]
            z_ref[rows, :] = z
            r = _layer_norm_rows(z, g_ref[...], b_ref[...])
            r_ref[rows, :] = r
            rb_ref[rows, :] = r.astype(BF16)

    in_specs = [pl.BlockSpec((ts, D), row), pl.BlockSpec((ts, ka), row), pl.BlockSpec((ka, D), fix),
                pl.BlockSpec((1, D), fix), pl.BlockSpec((1, D), fix)]
    args = [res, a, w, ln_g.reshape(1, D), ln_b.reshape(1, D)]
    out_dtypes = [F32, F32, BF16]
    if has_ple:
        wg, bg, p, wp = ple
        in_specs += [pl.BlockSpec((D, D), fix), pl.BlockSpec((1, D), fix), pl.BlockSpec((ts, PLE_DIM), row),
                     pl.BlockSpec((PLE_DIM, D), fix)]
        args += [wg, bg.reshape(1, D), p, wp]
        out_dtypes += [F32, F32]
    return pl.pallas_call(
        body,
        out_shape=[jax.ShapeDtypeStruct((S, D), dt) for dt in out_dtypes],
        grid=(S // ts,),
        in_specs=in_specs,
        out_specs=[pl.BlockSpec((ts, D), row)] * len(out_dtypes),
        scratch_shapes=[pltpu.VMEM((ts, D), F32)],
        compiler_params=_cparams(("parallel",)),
        name=name,
    )(*args)


CONV_ROWS = 32
LN_ROWS = 16


def _shifted_copies(src, dst, rows):
    for c0 in range(0, src.shape[1], SUB_LANES):
        ln = pl.ds(c0, SUB_LANES)
        for r0 in range(0, rows, SUB_ROWS):
            rc = min(SUB_ROWS, rows - r0)
            for b, shifted in enumerate(_rows_ahead(src, r0, rc, ln, range(1, 8))):
                dst[b, pl.ds(r0, rc), ln] = shifted


def _rows_at(src, copies, off, n, ln):
    b = off % 8
    return src[pl.ds(off, n), ln] if b == 0 else copies[b - 1, pl.ds(off - b, n), ln]


def _conv31(stg, gsh, cw_ref, cb_ref, out, rows, first_off):
    for c0 in range(0, D_CONV, SUB_LANES):
        ln = pl.ds(c0, SUB_LANES)
        for r0 in range(0, rows, CONV_ROWS):
            acc = jnp.zeros((CONV_ROWS, SUB_LANES), F32) + cb_ref[:, ln]
            for k in range(CONV_KERNEL):
                acc = acc + cw_ref[k:k + 1, ln] * _rows_at(stg, gsh, first_off + k + r0, CONV_ROWS, ln)
            out[pl.ds(r0, CONV_ROWS), ln] = acc


def _mixer_fwd(u, pool_w, pool_scale, conv_w, conv_b, cln_g, cln_b, *, ts=256):
    S = u.shape[0]
    hb = CONV_HALO
    nh = ts // hb

    def body(u_ref, uh_ref, pw_ref, ps_ref, cw_ref, cb_ref, g_ref, b_ref, y_ref, d_ref, sta, stg, gsh, hcs):
        i = pl.program_id(0)
        first = i == 0
        sta[pl.ds(0, hb), :] = jnp.where(first, 0.0, uh_ref[:, 0:D_POOL])
        sta[pl.ds(hb, ts), :] = u_ref[:, 0:D_POOL]
        glu_h = uh_ref[:, D_POOL:D_POOL + D_CONV] * _sigmoid(uh_ref[:, D_POOL + D_CONV:])
        stg[pl.ds(0, hb), :] = jnp.where(first, 0.0, glu_h)
        stg[pl.ds(hb, ts), :] = u_ref[:, D_POOL:D_POOL + D_CONV] * _sigmoid(u_ref[:, D_POOL + D_CONV:])

        for g, w in enumerate(POOL_WINDOWS):
            lanes = pl.ds(g * POOL_GROUP, POOL_GROUP)
            for r0 in range(0, ts, SUB_ROWS):
                s = None
                for q in range(0, w, 8):
                    for tap in _rows_back(sta, hb + r0 - q, SUB_ROWS, lanes, range(min(8, w - q))):
                        s = tap if s is None else s + tap
                pos = (i * ts + r0 + lax.broadcasted_iota(jnp.int32, (SUB_ROWS, 1), 0) + 1).astype(F32)
                d_g = s / jnp.minimum(pos, float(w)) - sta[pl.ds(hb + r0, SUB_ROWS), lanes]
                d_ref[pl.ds(r0, SUB_ROWS), lanes] = d_g.astype(BF16)
            y_ref[:, lanes] = (_bdot(d_ref[:, lanes], pw_ref[g]) * ps_ref[:, lanes]).astype(BF16)

        _shifted_copies(stg, gsh, hb + ts - 8)
        _conv31(stg, gsh, cw_ref, cb_ref, hcs, ts, hb - (CONV_KERNEL - 1))
        for r0 in range(0, ts, LN_ROWS):
            rows = pl.ds(r0, LN_ROWS)
            ln = _layer_norm_rows(hcs[rows, :], g_ref[...], b_ref[...])
            y_ref[rows, D_POOL:] = (ln * _sigmoid(ln)).astype(BF16)

    fix2 = lambda i: (0, 0)
    return pl.pallas_call(
        body,
        out_shape=[jax.ShapeDtypeStruct((S, D_MODEL), BF16), jax.ShapeDtypeStruct((S, D_POOL), BF16)],
        grid=(S // ts,),
        in_specs=[pl.BlockSpec((ts, 3 * D_POOL), lambda i: (i, 0)),
                  pl.BlockSpec((hb, 3 * D_POOL), lambda i: (jnp.maximum(i * nh - 1, 0), 0)),
                  pl.BlockSpec((4, POOL_GROUP, POOL_GROUP), lambda i: (0, 0, 0)),
                  pl.BlockSpec((1, D_POOL), fix2), pl.BlockSpec((CONV_KERNEL, D_CONV), fix2),
                  pl.BlockSpec((1, D_CONV), fix2), pl.BlockSpec((1, D_CONV), fix2), pl.BlockSpec((1, D_CONV), fix2)],
        out_specs=[pl.BlockSpec((ts, D_MODEL), lambda i: (i, 0)), pl.BlockSpec((ts, D_POOL), lambda i: (i, 0))],
        scratch_shapes=[pltpu.VMEM((hb + ts, D_POOL), F32), pltpu.VMEM((hb + ts, D_CONV), F32),
                        pltpu.VMEM((7, hb + ts - 8, D_CONV), F32), pltpu.VMEM((ts, D_CONV), F32)],
        compiler_params=_cparams(("parallel",)),
        name="mixer_fwd",
    )(u, u, pool_w, pool_scale.reshape(1, D_POOL), conv_w, conv_b.reshape(1, D_CONV), cln_g.reshape(1, D_CONV),
      cln_b.reshape(1, D_CONV))


def _mixer_bwd(u, d, dycat, pool_w, pool_scale, conv_w, conv_b, cln_g, cln_b, *, ts=256):
    S = u.shape[0]
    hb = CONV_HALO
    nh = ts // hb
    n = S // ts
    te = ts + hb
    K = CONV_KERNEL

    def body(u_ref, up_ref, un_ref, d_ref, dy_ref, dyn_ref, pw_ref, ps_ref, cw_ref, cb_ref, g_ref, b_ref,
             du_ref, dpw_ref, dps_ref, dcw_ref, dcb_ref, dg_ref, db_ref, stg, std, sth, gsh, hcs, hsh):
        i = pl.program_id(0)
        first = i == 0
        last = i == n - 1

        @pl.when(first)
        def _():
            dpw_ref[...] = jnp.zeros_like(dpw_ref)
            dps_ref[...] = jnp.zeros_like(dps_ref)
            dcw_ref[...] = jnp.zeros_like(dcw_ref)
            dcb_ref[...] = jnp.zeros_like(dcb_ref)
            dg_ref[...] = jnp.zeros_like(dg_ref)
            db_ref[...] = jnp.zeros_like(db_ref)

        pos_e = (i * ts + lax.broadcasted_iota(jnp.int32, (te, 1), 0) + 1).astype(F32)
        dya = dy_ref[:, 0:D_POOL]
        dya_n = jnp.where(last, 0.0, dyn_ref[:, 0:D_POOL])
        for g, w in enumerate(POOL_WINDOWS):
            lanes = pl.ds(g * POOL_GROUP, POOL_GROUP)
            sl = slice(g * POOL_GROUP, (g + 1) * POOL_GROUP)
            pw = pw_ref[g]
            scale = ps_ref[:, lanes]
            d_g = d_ref[:, lanes]
            pre = _bdot(d_g, pw)
            dps_ref[:, lanes] += jnp.sum(dya[:, sl] * pre, axis=0, keepdims=True)
            dys = dya[:, sl] * scale
            dpw_ref[g] += _bdot(d_g, dys, TN)
            dys_e = jnp.concatenate([dys, dya_n[:, sl] * scale], axis=0)
            dd = _bdot(dys_e, pw, NT)
            std[:, lanes] = dd / jnp.minimum(pos_e, float(w))
            for r0 in range(0, ts, SUB_ROWS):
                da = -dd[r0:r0 + SUB_ROWS]
                for q in range(0, w, 8):
                    for tap in _rows_ahead(std, r0 + q, SUB_ROWS, lanes, range(min(8, w - q))):
                        da = da + tap
                du_ref[pl.ds(r0, SUB_ROWS), lanes] = da.astype(BF16)

        glu_p = up_ref[:, D_POOL:D_POOL + D_CONV] * _sigmoid(up_ref[:, D_POOL + D_CONV:])
        stg[pl.ds(0, hb), :] = jnp.where(first, 0.0, glu_p)
        bv = u_ref[:, D_POOL:D_POOL + D_CONV]
        sg = _sigmoid(u_ref[:, D_POOL + D_CONV:])
        stg[pl.ds(hb, ts), :] = bv * sg
        glu_n = un_ref[:, D_POOL:D_POOL + D_CONV] * _sigmoid(un_ref[:, D_POOL + D_CONV:])
        stg[pl.ds(hb + ts, hb), :] = jnp.where(last, 0.0, glu_n)
        _shifted_copies(stg, gsh, hb + te - 8)
        _conv31(stg, gsh, cw_ref, cb_ref, hcs, te, hb - (K - 1))

        sums = [jnp.zeros((8, D_CONV), F32) for _ in range(3)]
        for r0 in range(0, te, LN_ROWS):
            rows = pl.ds(r0, LN_ROWS)
            hc = hcs[rows, :]
            hcc = hc - jnp.mean(hc, axis=-1, keepdims=True)
            rstd = lax.rsqrt(jnp.mean(hcc * hcc, axis=-1, keepdims=True) + LN_EPS)
            xh = hcc * rstd
            ln = xh * g_ref[...] + b_ref[...]
            sl_ = _sigmoid(ln)
            if r0 < ts:
                dyb = dy_ref[rows, D_POOL:]
            else:
                dyb = jnp.where(last, 0.0, dyn_ref[pl.ds(r0 - ts, LN_ROWS), D_POOL:])
            dln = dyb * (sl_ * (1.0 + ln * (1.0 - sl_)))
            dxh = dln * g_ref[...]
            dhc = rstd * (dxh - jnp.mean(dxh, axis=-1, keepdims=True)
                          - xh * jnp.mean(dxh * xh, axis=-1, keepdims=True))
            sth[rows, :] = dhc
            if r0 < ts:
                for n_, term in enumerate((dln * xh, dln, dhc)):
                    sums[n_] = sums[n_] + jnp.sum(term.reshape(LN_ROWS // 8, 8, D_CONV), axis=0)
        dg_ref[...] += jnp.sum(sums[0], axis=0, keepdims=True)
        db_ref[...] += jnp.sum(sums[1], axis=0, keepdims=True)
        dcb_ref[...] += jnp.sum(sums[2], axis=0, keepdims=True)

        _shifted_copies(sth, hsh, te - 8)
        for c0 in range(0, D_CONV, SUB_LANES):
            ln_ = pl.ds(c0, SUB_LANES)
            for r0 in range(0, ts, CONV_ROWS):
                rows = pl.ds(r0, CONV_ROWS)
                dglu = jnp.zeros((CONV_ROWS, SUB_LANES), F32)
                for k in range(K):
                    dglu = dglu + cw_ref[k:k + 1, ln_] * _rows_at(sth, hsh, K - 1 - k + r0, CONV_ROWS, ln_)
                bv = u_ref[rows, pl.ds(D_POOL + c0, SUB_LANES)]
                sg = _sigmoid(u_ref[rows, pl.ds(D_POOL + D_CONV + c0, SUB_LANES)])
                du_ref[rows, pl.ds(D_POOL + c0, SUB_LANES)] = (dglu * sg).astype(BF16)
                du_ref[rows, pl.ds(D_POOL + D_CONV + c0, SUB_LANES)] = (dglu * bv * sg * (1.0 - sg)).astype(BF16)
            for k in range(K):
                tap = jnp.zeros((8, SUB_LANES), F32)
                for r0 in range(0, ts, CONV_ROWS):
                    prod = sth[pl.ds(r0, CONV_ROWS), ln_] * _rows_at(stg, gsh, hb - (K - 1) + k + r0, CONV_ROWS, ln_)
                    tap = tap + jnp.sum(prod.reshape(CONV_ROWS // 8, 8, SUB_LANES), axis=0)
                dcw_ref[k:k + 1, ln_] += jnp.sum(tap, axis=0, keepdims=True)

    fix2 = lambda i: (0, 0)
    prev = lambda i: (jnp.maximum(i * nh - 1, 0), 0)
    nxt = lambda i: (jnp.minimum((i + 1) * nh, S // hb - 1), 0)
    return pl.pallas_call(
        body,
        out_shape=[jax.ShapeDtypeStruct((S, 3 * D_POOL), BF16),
                   jax.ShapeDtypeStruct((4, POOL_GROUP, POOL_GROUP), F32),
                   jax.ShapeDtypeStruct((1, D_POOL), F32),
                   jax.ShapeDtypeStruct((K, D_CONV), F32),
                   jax.ShapeDtypeStruct((1, D_CONV), F32),
                   jax.ShapeDtypeStruct((1, D_CONV), F32),
                   jax.ShapeDtypeStruct((1, D_CONV), F32)],
        grid=(n,),
        in_specs=[pl.BlockSpec((ts, 3 * D_POOL), lambda i: (i, 0)),
                  pl.BlockSpec((hb, 3 * D_POOL), prev),
                  pl.BlockSpec((hb, 3 * D_POOL), nxt),
                  pl.BlockSpec((ts, D_POOL), lambda i: (i, 0)),
                  pl.BlockSpec((ts, D_MODEL), lambda i: (i, 0)),
                  pl.BlockSpec((hb, D_MODEL), nxt),
                  pl.BlockSpec((4, POOL_GROUP, POOL_GROUP), lambda i: (0, 0, 0)),
                  pl.BlockSpec((1, D_POOL), fix2), pl.BlockSpec((K, D_CONV), fix2),
                  pl.BlockSpec((1, D_CONV), fix2), pl.BlockSpec((1, D_CONV), fix2), pl.BlockSpec((1, D_CONV), fix2)],
        out_specs=[pl.BlockSpec((ts, 3 * D_POOL), lambda i: (i, 0)),
                   pl.BlockSpec((4, POOL_GROUP, POOL_GROUP), lambda i: (0, 0, 0)),
                   pl.BlockSpec((1, D_POOL), fix2), pl.BlockSpec((K, D_CONV), fix2),
                   pl.BlockSpec((1, D_CONV), fix2), pl.BlockSpec((1, D_CONV), fix2), pl.BlockSpec((1, D_CONV), fix2)],
        scratch_shapes=[pltpu.VMEM((hb + ts + hb, D_CONV), F32), pltpu.VMEM((te, D_POOL), F32),
                        pltpu.VMEM((te, D_CONV), F32), pltpu.VMEM((7, hb + te - 8, D_CONV), F32),
                        pltpu.VMEM((te, D_CONV), F32), pltpu.VMEM((7, te - 8, D_CONV), F32)],
        compiler_params=_cparams(("arbitrary",)),
        name="mixer_bwd",
    )(u, u, u, d, dycat, dycat, pool_w, pool_scale.reshape(1, D_POOL), conv_w, conv_b.reshape(1, D_CONV),
      cln_g.reshape(1, D_CONV), cln_b.reshape(1, D_CONV))


_GELU_C = math.sqrt(2.0 / math.pi)


def _gelu_parts(x):
    inner = _GELU_C * (x + 0.044715 * x * x * x)
    th = jnp.tanh(inner)
    ge = 0.5 * x * (1.0 + th)
    dge = 0.5 * (1.0 + th) + 0.5 * x * (1.0 - th * th) * (_GELU_C * (1.0 + 3.0 * 0.044715 * x * x))
    return ge, dge


def _rows_back(ref, r, n, ln, shifts):
    ext = ref[pl.ds(r - 8, n + 8), ln]
    return [(pltpu.roll(ext, s, 0) if s else ext)[8:] for s in shifts]


def _rows_ahead(ref, r, n, ln, shifts):
    ext = ref[pl.ds(r, n + 8), ln]
    return [(pltpu.roll(ext, n + 8 - s, 0) if s else ext)[:n] for s in shifts]


def _ffn_act_fwd(gate, val, dw_w, dw_b, *, ts=256, tc=1408, name):
    S, F = gate.shape
    hb = FFN_HALO
    nh = ts // hb
    tc = _tile(F, tc)

    def body(g_ref, gh_ref, v_ref, w_ref, b_ref, h_ref, st):
        i = pl.program_id(0)
        st[pl.ds(0, hb), :] = jnp.where(i == 0, 0.0, gh_ref[...].astype(F32))
        st[pl.ds(hb, ts), :] = g_ref[...].astype(F32)
        for c0 in range(0, tc, SUB_LANES):
            ln = pl.ds(c0, SUB_LANES)
            w0, w1, w2, b = w_ref[0:1, ln], w_ref[1:2, ln], w_ref[2:3, ln], b_ref[:, ln]
            for r0 in range(0, ts, SUB_ROWS):
                taps = _rows_back(st, hb + r0, SUB_ROWS, ln, (2, 1, 0))
                gc = b + w0 * taps[0] + w1 * taps[1] + w2 * taps[2]
                ge, _ = _gelu_parts(gc)
                rows = pl.ds(r0, SUB_ROWS)
                h_ref[rows, ln] = (ge * v_ref[rows, ln].astype(F32)).astype(BF16)

    return pl.pallas_call(
        body,
        out_shape=jax.ShapeDtypeStruct((S, F), BF16),
        grid=(S // ts, F // tc),
        in_specs=[pl.BlockSpec((ts, tc), lambda i, j: (i, j)),
                  pl.BlockSpec((hb, tc), lambda i, j: (jnp.maximum(i * nh - 1, 0), j)),
                  pl.BlockSpec((ts, tc), lambda i, j: (i, j)),
                  pl.BlockSpec((3, tc), lambda i, j: (0, j)),
                  pl.BlockSpec((1, tc), lambda i, j: (0, j))],
        out_specs=pl.BlockSpec((ts, tc), lambda i, j: (i, j)),
        scratch_shapes=[pltpu.VMEM((hb + ts, tc), F32)],
        compiler_params=_cparams(("parallel", "parallel")),
        name=name,
    )(gate, gate, val, dw_w, dw_b.reshape(1, F))


def _ffn_act_bwd(gate, val, dh, dw_w, dw_b, *, ts=256, tc=1408, name):
    S, F = gate.shape
    hb = FFN_HALO
    nh = ts // hb
    n = S // ts
    te = ts + hb
    tc = _tile(F, tc)

    def body(g_ref, gp_ref, gn_ref, v_ref, vn_ref, dh_ref, dhn_ref, w_ref, b_ref,
             dg_ref, dv_ref, dw_ref, db_ref, st, sd):
        i = pl.program_id(1)
        first = i == 0
        last = i == n - 1

        @pl.when(first)
        def _():
            dw_ref[...] = jnp.zeros_like(dw_ref)
            db_ref[...] = jnp.zeros_like(db_ref)

        st[pl.ds(0, hb), :] = jnp.where(first, 0.0, gp_ref[...].astype(F32))
        st[pl.ds(hb, ts), :] = g_ref[...].astype(F32)
        st[pl.ds(hb + ts, hb), :] = jnp.where(last, 0.0, gn_ref[...].astype(F32))
        for c0 in range(0, tc, SUB_LANES):
            ln = pl.ds(c0, SUB_LANES)
            w0, w1, w2, b = w_ref[0:1, ln], w_ref[1:2, ln], w_ref[2:3, ln], b_ref[:, ln]
            db_acc = jnp.zeros((8, SUB_LANES), F32)
            dw_acc = [jnp.zeros((8, SUB_LANES), F32) for _ in range(3)]
            for r0 in range(0, te, SUB_ROWS):
                rc = min(SUB_ROWS, te - r0)
                taps = _rows_back(st, hb + r0, rc, ln, (2, 1, 0))
                gc = b + w0 * taps[0] + w1 * taps[1] + w2 * taps[2]
                ge, dge = _gelu_parts(gc)
                if r0 < ts:
                    rows = pl.ds(r0, rc)
                    val, dh = v_ref[rows, ln].astype(F32), dh_ref[rows, ln].astype(F32)
                else:
                    val = jnp.where(last, 0.0, vn_ref[:, ln].astype(F32)[0:rc])
                    dh = jnp.where(last, 0.0, dhn_ref[:, ln].astype(F32)[0:rc])
                dgc = dh * val * dge
                sd[pl.ds(r0, rc), ln] = dgc
                if r0 < ts:
                    dv_ref[rows, ln] = (dh * ge).astype(BF16)
                    db_acc = db_acc + jnp.sum(dgc.reshape(rc // 8, 8, SUB_LANES), axis=0)
                    for k in range(3):
                        dw_acc[k] = dw_acc[k] + jnp.sum((dgc * taps[k]).reshape(rc // 8, 8, SUB_LANES), axis=0)
            db_ref[:, ln] += jnp.sum(db_acc, axis=0, keepdims=True)
            for k in range(3):
                dw_ref[k:k + 1, ln] += jnp.sum(dw_acc[k], axis=0, keepdims=True)
            for r0 in range(0, ts, SUB_ROWS):
                ahead = _rows_ahead(sd, r0, SUB_ROWS, ln, (2, 1, 0))
                dg_ref[pl.ds(r0, SUB_ROWS), ln] = (w0 * ahead[0] + w1 * ahead[1] + w2 * ahead[2]).astype(BF16)

    cur = lambda j, i: (i, j)
    prev = lambda j, i: (jnp.maximum(i * nh - 1, 0), j)
    nxt = lambda j, i: (jnp.minimum((i + 1) * nh, S // hb - 1), j)
    return pl.pallas_call(
        body,
        out_shape=[jax.ShapeDtypeStruct((S, F), BF16), jax.ShapeDtypeStruct((S, F), BF16),
                   jax.ShapeDtypeStruct((3, F), F32), jax.ShapeDtypeStruct((1, F), F32)],
        grid=(F // tc, n),
        in_specs=[pl.BlockSpec((ts, tc), cur), pl.BlockSpec((hb, tc), prev), pl.BlockSpec((hb, tc), nxt),
                  pl.BlockSpec((ts, tc), cur), pl.BlockSpec((hb, tc), nxt),
                  pl.BlockSpec((ts, tc), cur), pl.BlockSpec((hb, tc), nxt),
                  pl.BlockSpec((3, tc), lambda j, i: (0, j)), pl.BlockSpec((1, tc), lambda j, i: (0, j))],
        out_specs=[pl.BlockSpec((ts, tc), cur), pl.BlockSpec((ts, tc), cur),
                   pl.BlockSpec((3, tc), lambda j, i: (0, j)), pl.BlockSpec((1, tc), lambda j, i: (0, j))],
        scratch_shapes=[pltpu.VMEM((hb + ts + hb, tc), F32), pltpu.VMEM((te, tc), F32)],
        compiler_params=_cparams(("parallel", "arbitrary")),
        name=name,
    )(gate, gate, gate, val, val, dh, dh, dw_w, dw_b.reshape(1, F))


def _loss_ln_bwd(z, ln_g, ln_b, target, *, ts=256, name):
    S, D = z.shape

    def body(z_ref, g_ref, b_ref, t_ref, dz_ref, dzb_ref, dg_ref, db_ref, loss_ref):
        i = pl.program_id(0)

        @pl.when(i == 0)
        def _():
            dg_ref[...] = jnp.zeros_like(dg_ref)
            db_ref[...] = jnp.zeros_like(db_ref)
            loss_ref[...] = jnp.zeros_like(loss_ref)

        dg_acc = jnp.zeros((8, D), F32)
        db_acc = jnp.zeros((8, D), F32)
        loss_acc = jnp.zeros((1, 1), F32)
        for r0 in range(0, ts, LN_ROWS):
            rows = pl.ds(r0, LN_ROWS)
            zt = z_ref[rows, :]
            err = _layer_norm_rows(zt, g_ref[...], b_ref[...]) - t_ref[rows, :]
            loss_acc = loss_acc + 0.5 * jnp.sum(jnp.mean(err * err, axis=-1, keepdims=True), keepdims=True)
            do = err * (1.0 / D)
            dz, xh = _ln_bwd_rows(zt, g_ref[...], do)
            dg_acc = dg_acc + jnp.sum((do * xh).reshape(LN_ROWS // 8, 8, D), axis=0)
            db_acc = db_acc + jnp.sum(do.reshape(LN_ROWS // 8, 8, D), axis=0)
            dz_ref[rows, :] = dz
            dzb_ref[rows, :] = dz.astype(BF16)
        dg_ref[...] += jnp.sum(dg_acc, axis=0, keepdims=True)
        db_ref[...] += jnp.sum(db_acc, axis=0, keepdims=True)
        loss_ref[...] += loss_acc

    row = lambda i: (i, 0)
    fix = lambda i: (0, 0)
    return pl.pallas_call(
        body,
        out_shape=[jax.ShapeDtypeStruct((S, D), F32), jax.ShapeDtypeStruct((S, D), BF16),
                   jax.ShapeDtypeStruct((1, D), F32), jax.ShapeDtypeStruct((1, D), F32),
                   jax.ShapeDtypeStruct((8, 128), F32)],
        grid=(S // ts,),
        in_specs=[pl.BlockSpec((ts, D), row), pl.BlockSpec((1, D), fix), pl.BlockSpec((1, D), fix),
                  pl.BlockSpec((ts, D), row)],
        out_specs=[pl.BlockSpec((ts, D), row), pl.BlockSpec((ts, D), row), pl.BlockSpec((1, D), fix),
                   pl.BlockSpec((1, D), fix), pl.BlockSpec((8, 128), fix)],
        compiler_params=_cparams(("arbitrary",)),
        name=name,
    )(z, ln_g.reshape(1, D), ln_b.reshape(1, D), target)


def _ple_bwd(dz, gate, proj, *, ts=256, name):
    S, D = dz.shape

    def body(dz_ref, g_ref, p_ref, ds_ref, dp_ref, db_ref):
        @pl.when(pl.program_id(0) == 0)
        def _():
            db_ref[...] = jnp.zeros_like(db_ref)

        db_acc = jnp.zeros((8, D), F32)
        for r0 in range(0, ts, LN_ROWS):
            rows = pl.ds(r0, LN_ROWS)
            dzt = dz_ref[rows, :]
            g = g_ref[rows, :]
            ds = dzt * p_ref[rows, :] * g * (1.0 - g)
            ds_ref[rows, :] = ds.astype(BF16)
            dp_ref[rows, :] = (dzt * g).astype(BF16)
            db_acc = db_acc + jnp.sum(ds.reshape(LN_ROWS // 8, 8, D), axis=0)
        db_ref[...] += jnp.sum(db_acc, axis=0, keepdims=True)

    row = lambda i: (i, 0)
    return pl.pallas_call(
        body,
        out_shape=[jax.ShapeDtypeStruct((S, D), BF16), jax.ShapeDtypeStruct((S, D), BF16),
                   jax.ShapeDtypeStruct((1, D), F32)],
        grid=(S // ts,),
        in_specs=[pl.BlockSpec((ts, D), row)] * 3,
        out_specs=[pl.BlockSpec((ts, D), row), pl.BlockSpec((ts, D), row), pl.BlockSpec((1, D), lambda i: (0, 0))],
        compiler_params=_cparams(("arbitrary",)),
        name=name,
    )(dz, gate, proj)


HEAD_PAIR = 2 * HEAD_DIM


ATT_ROWS = 32
ATT_SCALE = HEAD_DIM ** -0.5


def _softmax_piece(scores, bias, qb):
    s = scores + bias
    kpos = qb * Q_BLOCK + lax.broadcasted_iota(jnp.int32, (1, KV_SPAN), 1)
    s = jnp.where(kpos >= KV_PAD, s, NEG_INF)
    e = jnp.exp(s - jnp.max(s, axis=-1, keepdims=True))
    return e * (1.0 / jnp.sum(e, axis=-1, keepdims=True))


def _pad_keys(qb, k_ref, v_ref, kp, vp):
    @pl.when(qb == 0)
    def _():
        kp[pl.ds(0, KV_PAD), :] = jnp.zeros((KV_PAD, HEAD_PAIR), BF16)
        vp[pl.ds(0, KV_PAD), :] = jnp.zeros((KV_PAD, HEAD_PAIR), BF16)
        kp[pl.ds(KV_PAD, k_ref.shape[0]), :] = k_ref[...]
        vp[pl.ds(KV_PAD, v_ref.shape[0]), :] = v_ref[...]


def _attn_fwd(qkv, bias):
    S = qkv.shape[0]
    nhp = N_HEADS // 2

    def body(q_ref, k_ref, v_ref, b_ref, o_ref, kp, vp, p_scr):
        qb = pl.program_id(1)
        _pad_keys(qb, k_ref, v_ref, kp, vp)
        span = pl.ds(pl.multiple_of(qb * Q_BLOCK, Q_BLOCK), KV_SPAN)
        kc, vc = kp[span, :], vp[span, :]
        qt = q_ref[...] * ATT_SCALE
        first = lax.broadcasted_iota(jnp.int32, (1, HEAD_PAIR), 1) < HEAD_DIM
        scores = [_bdot(jnp.where(first if j == 0 else ~first, qt, jnp.zeros_like(qt)), kc, NT) for j in range(2)]
        outs = []
        for j in range(2):
            for r0 in range(0, Q_BLOCK, ATT_ROWS):
                rows = pl.ds(r0, ATT_ROWS)
                p_scr[j, rows, :] = _softmax_piece(scores[j][r0:r0 + ATT_ROWS], b_ref[j, rows, :], qb).astype(BF16)
            outs.append(_bdot(p_scr[j], vc))
        o_ref[...] = jnp.where(first, outs[0], outs[1]).astype(BF16)

    return pl.pallas_call(
        body,
        out_shape=jax.ShapeDtypeStruct((S, D_MODEL), BF16),
        grid=(nhp, S // Q_BLOCK),
        in_specs=[pl.BlockSpec((Q_BLOCK, HEAD_PAIR), lambda h, i: (i, h)),
                  pl.BlockSpec((S, HEAD_PAIR), lambda h, i: (0, nhp + h)),
                  pl.BlockSpec((S, HEAD_PAIR), lambda h, i: (0, 2 * nhp + h)),
                  pl.BlockSpec((2, Q_BLOCK, KV_SPAN), lambda h, i: (h, 0, 0))],
        out_specs=pl.BlockSpec((Q_BLOCK, HEAD_PAIR), lambda h, i: (i, h)),
        scratch_shapes=[pltpu.VMEM((KV_PAD + S, HEAD_PAIR), BF16), pltpu.VMEM((KV_PAD + S, HEAD_PAIR), BF16),
                        pltpu.VMEM((2, Q_BLOCK, KV_SPAN), BF16)],
        compiler_params=_cparams(("parallel", "arbitrary")),
        name="attn_fwd",
    )(qkv, qkv, qkv, bias)


def _attn_bwd(qkv, bias, do):
    S = qkv.shape[0]
    nhp = N_HEADS // 2
    nq = S // Q_BLOCK
    scale = HEAD_DIM ** -0.5

    def body(q_ref, k_ref, v_ref, b_ref, do_ref, dq_ref, dk_ref, dv_ref, db_ref, kp, vp, dka, dva,
             p_scr, ds_scr):
        qb = pl.program_id(1)
        _pad_keys(qb, k_ref, v_ref, kp, vp)

        @pl.when(qb == 0)
        def _():
            dka[...] = jnp.zeros_like(dka)
            dva[...] = jnp.zeros_like(dva)
            db_ref[...] = jnp.zeros_like(db_ref)

        span = pl.ds(pl.multiple_of(qb * Q_BLOCK, Q_BLOCK), KV_SPAN)
        kc, vc = kp[span, :], vp[span, :]
        qt, dot = q_ref[...] * ATT_SCALE, do_ref[...]
        first = lax.broadcasted_iota(jnp.int32, (1, HEAD_PAIR), 1) < HEAD_DIM
        dqs = []
        qs = [jnp.where(first if j == 0 else ~first, qt, jnp.zeros_like(qt)) for j in range(2)]
        dos = [jnp.where(first if j == 0 else ~first, dot, jnp.zeros_like(dot)) for j in range(2)]
        scores = [_bdot(qs[j], kc, NT) for j in range(2)]
        dps = [_bdot(dos[j], vc, NT) for j in range(2)]
        for j in range(2):
            qj, doj = qs[j], dos[j]
            for r0 in range(0, Q_BLOCK, ATT_ROWS):
                rows = pl.ds(r0, ATT_ROWS)
                p = _softmax_piece(scores[j][r0:r0 + ATT_ROWS], b_ref[j, rows, :], qb)
                dp = dps[j][r0:r0 + ATT_ROWS]
                ds = p * (dp - jnp.sum(p * dp, axis=-1, keepdims=True))
                db_ref[j, rows, :] += ds
                p_scr[j, rows, :] = p.astype(BF16)
                ds_scr[j, rows, :] = ds.astype(BF16)
            dva[span, :] += _bdot(p_scr[j], doj, TN)
            dqs.append(_bdot(ds_scr[j], kc))
            dka[span, :] += _bdot(ds_scr[j], qj, TN)
        dq_ref[...] = (scale * jnp.where(first, dqs[0], dqs[1])).astype(BF16)

        @pl.when(qb == nq - 1)
        def _():
            dk_ref[...] = dka[pl.ds(KV_PAD, S), :].astype(BF16)
            dv_ref[...] = dva[pl.ds(KV_PAD, S), :].astype(BF16)

    blk = pl.BlockSpec((Q_BLOCK, HEAD_PAIR), lambda h, i: (i, h))
    col = pl.BlockSpec((S, HEAD_PAIR), lambda h, i: (0, h))
    bsp = pl.BlockSpec((2, Q_BLOCK, KV_SPAN), lambda h, i: (h, 0, 0))
    return pl.pallas_call(
        body,
        out_shape=[jax.ShapeDtypeStruct((S, D_MODEL), BF16)] * 3
        + [jax.ShapeDtypeStruct((N_HEADS, Q_BLOCK, KV_SPAN), F32)],
        grid=(nhp, nq),
        in_specs=[blk, pl.BlockSpec((S, HEAD_PAIR), lambda h, i: (0, nhp + h)),
                  pl.BlockSpec((S, HEAD_PAIR), lambda h, i: (0, 2 * nhp + h)), bsp, blk],
        out_specs=[blk, col, col, bsp],
        scratch_shapes=[pltpu.VMEM((KV_PAD + S, HEAD_PAIR), BF16), pltpu.VMEM((KV_PAD + S, HEAD_PAIR), BF16),
                        pltpu.VMEM((KV_PAD + S, HEAD_PAIR), F32), pltpu.VMEM((KV_PAD + S, HEAD_PAIR), F32),
                        pltpu.VMEM((2, Q_BLOCK, KV_SPAN), BF16), pltpu.VMEM((2, Q_BLOCK, KV_SPAN), BF16)],
        compiler_params=_cparams(("parallel", "arbitrary")),
        name="attn_bwd",
    )(qkv, qkv, qkv, bias, do)


N_DIST = BAND + CHUNK - 1
N_FAR = KV_PAD + CHUNK - MAX_REL


def _shear_rows(x, towards_right):
    row = lax.broadcasted_iota(jnp.int32, (Q_BLOCK, 1), 0)
    for bit in range(Q_BLOCK.bit_length() - 1):
        step = 1 << bit
        x = jnp.where((row & step) != 0, pltpu.roll(x, step if towards_right else KV_SPAN - step, 1), x)
    return x


def _bias_blocks(rel_bias):
    H = rel_bias.shape[0]
    e = jnp.concatenate([jnp.broadcast_to(rel_bias[:, 2 * MAX_REL:], (H, N_FAR)),
                         jnp.flip(rel_bias[:, 2 * MAX_REL - (N_DIST - N_FAR):2 * MAX_REL], axis=1),
                         jnp.zeros((H, KV_SPAN - N_DIST), F32)], axis=1).reshape(H, 1, KV_SPAN)

    def body(e_ref, o_ref):
        first = pltpu.roll(jnp.broadcast_to(e_ref[...], (Q_BLOCK, KV_SPAN)), KV_SPAN - (CHUNK - 1), 1)
        x = _shear_rows(first, True)
        row = lax.broadcasted_iota(jnp.int32, (Q_BLOCK, 1), 0)
        chunk0 = row - (row & (CHUNK - 1))
        k = lax.broadcasted_iota(jnp.int32, (1, KV_SPAN), 1)
        o_ref[...] = jnp.where((k >= chunk0) & (k < chunk0 + BAND), x, NEG_INF)

    return pl.pallas_call(
        body,
        out_shape=jax.ShapeDtypeStruct((H, Q_BLOCK, KV_SPAN), F32),
        grid=(H,),
        in_specs=[pl.BlockSpec((None, 1, KV_SPAN), lambda h: (h, 0, 0))],
        out_specs=pl.BlockSpec((None, Q_BLOCK, KV_SPAN), lambda h: (h, 0, 0)),
        compiler_params=_cparams(("parallel",)),
        name="bias_blocks",
    )(e)


def _bias_blocks_grad(dblk):
    H = dblk.shape[0]

    def body(d_ref, o_ref):
        x = pltpu.roll(_shear_rows(d_ref[...], False), CHUNK - 1, 1)
        de = jnp.sum(x, axis=0, keepdims=True)
        lane = lax.broadcasted_iota(jnp.int32, de.shape, 1)
        far = jnp.sum(jnp.where(lane < N_FAR, de, 0.0), axis=-1, keepdims=True)
        o_ref[...] = jnp.where(lane == 0, far, jnp.where(lane < N_FAR, 0.0, de))

    de = pl.pallas_call(
        body,
        out_shape=jax.ShapeDtypeStruct((H, 1, KV_SPAN), F32),
        grid=(H,),
        in_specs=[pl.BlockSpec((None, Q_BLOCK, KV_SPAN), lambda h: (h, 0, 0))],
        out_specs=pl.BlockSpec((None, 1, KV_SPAN), lambda h: (h, 0, 0)),
        compiler_params=_cparams(("parallel",)),
        name="bias_grad_sum",
    )(dblk).reshape(H, KV_SPAN)
    near = jnp.flip(de[:, N_FAR:N_DIST], axis=1)
    return jnp.concatenate([jnp.zeros((H, 2 * MAX_REL - (N_DIST - N_FAR)), F32), near, de[:, 0:1]], axis=1)


def _ffn_forward(r1, r1b, p_l, w, l, ready):
    ready(f"up{l}", r1b)
    up_g = _mm_rows([(r1b, w["ffn_up_t"][l], True, (0, 2))], out_dtype=BF16, name=f"ffn_up_g{l}")
    up_v = _mm_rows([(r1b, w["ffn_up_t"][l], True, (1, 2))], out_dtype=BF16, name=f"ffn_up_v{l}")
    h = _ffn_act_fwd(up_g, up_v, w["ffn_dw_w"][l], w["ffn_dw_b"][l], name=f"ffn_act{l}")
    ready(f"dn{l}", h)
    z2, r2, r2b, gate, proj = _proj_ln(r1, h, w["ffn_w_down"][l], w["ln_ffn_g"][l], w["ln_ffn_b"][l],
                                       ple=(w["ple_w_gate"][l], w["ple_b_gate"][l], p_l, w["ple_w_proj"][l]),
                                       name=f"ffn_down_ln{l}")
    return dict(r1b=r1b, up_g=up_g, up_v=up_v, h=h, z2=z2, gate=gate, proj=proj), r2, r2b


def _ffn_backward(sv, dz2, dz2b, p_l, w, l, grads, ln_bwd, emit):
    r1b = sv["r1b"]
    ds, dproj, db_gate = _ple_bwd(dz2, sv["gate"], sv["proj"], name=f"ple_bwd{l}")
    dh = _mm_rows([(dz2b, w["ffn_w_down"][l], True, WHOLE)], out_dtype=BF16, name=f"ffn_dh{l}")
    dgate, dval, d_dw_w, d_dw_b = _ffn_act_bwd(sv["up_g"], sv["up_v"], dh, w["ffn_dw_w"][l], w["ffn_dw_b"][l],
                                               name=f"ffn_act_bwd{l}")
    grads["ffn_w_down"][l] = _wgrad(sv["h"], dz2b, tm=1408, name=f"d_ffn_w_down{l}")
    d_up_g = _wgrad(dgate, r1b, tm=1408, part=(0, 2), name=f"d_ffn_up_g{l}")
    grads["ffn_up_t"][l] = _wgrad(dval, r1b, tm=1408, part=(1, 2), into=d_up_g, name=f"d_ffn_up_v{l}")
    grads["ple_w_gate"][l] = _wgrad(r1b, ds, name=f"d_ple_w_gate{l}")
    grads["ple_w_proj"][l] = _wgrad(p_l, dproj, piece=D_MODEL // N_DEV, name=f"d_ple_w_proj{l}")
    grads["ffn_dw_w"][l] = d_dw_w
    grads["ffn_dw_b"][l] = d_dw_b[0]
    grads["ple_b_gate"][l] = db_gate[0]
    return _mm_rows([(ds, w["ple_w_gate"][l], True, WHOLE), (dgate, w["ffn_up_t"][l], False, (0, 2)),
                     (dval, w["ffn_up_t"][l], False, (1, 2))], add=dz2, add_scale=ALPHA, ln_bwd=ln_bwd, dep=emit(),
                    name=f"dr1_{l}")


def _local_step(x, p, target, w, ready=lambda group, after: None, emit=lambda group, grads: None):
    grads = {k: [None, None] for k in ("ffn_w_down", "ffn_up_t", "ple_w_gate", "ple_w_proj", "ffn_dw_w",
                                       "ffn_dw_b", "ple_b_gate", "ln_ffn_g", "ln_ffn_b", "ln_mix_g", "ln_mix_b")}

    xb, pb = x.astype(BF16), p.astype(BF16)
    ready("mix", None)
    u = _mm_rows([(xb, w["mix_w_in_t"], True, WHOLE)], name="mix_in")
    ycat, dpool = _mixer_fwd(u, w["pool_w"], w["pool_scale"], w["conv_dw_w"], w["conv_dw_b"], w["conv_ln_g"],
                             w["conv_ln_b"])
    ready("mixo", ycat)
    z1, r1, r1b = _proj_ln(x, ycat, w["mix_w_out"], w["ln_mix_g"][0], w["ln_mix_b"][0], name="mix_out_ln")
    sv0, r2, r2b = _ffn_forward(r1, r1b, pb[0], w, 0, ready)

    ready("attn", r2b)
    qkv = _mm_rows([(r2b, w["attn_w_qkv"], False, WHOLE)], out_dtype=BF16, name="attn_qkv")
    bias = _bias_blocks(w["attn_rel_bias"])
    attn = _attn_fwd(qkv, bias)
    z3, r3, r3b = _proj_ln(r2, attn, w["attn_w_o"], w["ln_mix_g"][1], w["ln_mix_b"][1], name="attn_out_ln")
    sv1, _, _ = _ffn_forward(r3, r3b, pb[1], w, 1, ready)

    dz4, dz4b, grads["ln_ffn_g"][1], grads["ln_ffn_b"][1], loss = _loss_ln_bwd(
        sv1["z2"], w["ln_ffn_g"][1], w["ln_ffn_b"][1], target, name="loss_ln_bwd")
    dz3, dz3b, grads["ln_mix_g"][1], grads["ln_mix_b"][1] = _ffn_backward(
        sv1, dz4, dz4b, pb[1], w, 1, grads, (z3, w["ln_mix_g"][1]), lambda: emit("ffn1", grads))
    grads["attn_w_o"] = _wgrad(attn, dz3b, name="d_attn_w_o")
    dattn = _mm_rows([(dz3b, w["attn_w_o"], True, WHOLE)], out_dtype=BF16, name="d_attn")
    dq, dk, dv, dbias = _attn_bwd(qkv, bias, dattn)
    grads["attn_rel_bias"] = _bias_blocks_grad(dbias)
    dqkv = jnp.concatenate([dq, dk, dv], axis=1)
    grads["attn_w_qkv"] = _wgrad(r2b, dqkv, tn=768, piece=3 * D_MODEL // N_DEV, name="d_attn_w_qkv")
    dz2, dz2b, grads["ln_ffn_g"][0], grads["ln_ffn_b"][0] = _mm_rows(
        [(dqkv, w["attn_w_qkv"], True, WHOLE)], add=dz3, add_scale=ALPHA, ln_bwd=(sv0["z2"], w["ln_ffn_g"][0]),
        dep=emit("attn", grads), name="dr2")
    dz1, dz1b, grads["ln_mix_g"][0], grads["ln_mix_b"][0] = _ffn_backward(
        sv0, dz2, dz2b, pb[0], w, 0, grads, (z1, w["ln_mix_g"][0]), lambda: emit("ffn0", grads))
    grads["mix_w_out"] = _wgrad(ycat, dz1b, name="d_mix_w_out")
    dycat = _mm_rows([(dz1b, w["mix_w_out"], True, WHOLE)], name="d_ycat")
    du, g_pw, g_ps, g_cw, g_cb, g_cg, g_cbb = _mixer_bwd(u, dpool, dycat, w["pool_w"], w["pool_scale"],
                                                         w["conv_dw_w"], w["conv_dw_b"], w["conv_ln_g"],
                                                         w["conv_ln_b"])
    grads["mix_w_in_t"] = _wgrad(du, xb, name="d_mix_w_in")
    grads.update(pool_w=g_pw, pool_scale=g_ps[0], conv_dw_w=g_cw, conv_dw_b=g_cb[0], conv_ln_g=g_cg[0],
                 conv_ln_b=g_cbb[0])
    for kname in ("ln_ffn_g", "ln_ffn_b", "ln_mix_g", "ln_mix_b"):
        grads[kname] = [a[0] for a in grads[kname]]
    grad_x = _mm_rows([(du, w["mix_w_in_t"], False, WHOLE)], add=dz1, add_scale=ALPHA, dep=emit("mix", grads),
                      name="grad_x")
    return loss[0, 0], grad_x, grads


_HBM = pl.BlockSpec(memory_space=pltpu.HBM)
_SEM = pl.BlockSpec(memory_space=pltpu.SEMAPHORE)
_EFFECT = pltpu.SideEffectType.DATAFLOW_SIDE_EFFECTING


def _slot(ref, place, shape, k):
    if place in ("stack", "pieces"):
        return ref.at[k]
    ax = place[1]
    n = shape[ax]
    return ref.at[(slice(None),) * ax + (pl.ds(pl.multiple_of(k * n, n), n),)]


def _result_shape(buf, place):
    if place == "stack":
        return (N_DEV,) + buf.shape
    if place == "pieces":
        return buf.shape
    return tuple(s * N_DEV if i == place[1] else s for i, s in enumerate(buf.shape))


def _peers(x, y, c):
    for d in range(1, N_DEV):
        px, py, pc = x ^ ((d >> 2) & 1), y ^ ((d >> 1) & 1), c ^ (d & 1)
        yield d, (px, py, pc), 4 * px + 2 * py + pc


def _exchange_start(bufs, places, after, *, name):
    nb = len(bufs)
    lands = [lax.empty(_result_shape(b, p_), b.dtype) for b, p_ in zip(bufs, places)]
    has_after = after is not None

    def body(*refs):
        srcs, dsts = refs[:nb], refs[nb:2 * nb]
        outs = refs[2 * nb + has_after:]
        send_sems, recv_sems, token = outs[0], outs[1], outs[2 + 2 * nb]
        x, y, c = lax.axis_index("x"), lax.axis_index("y"), lax.axis_index("c")
        me = 4 * x + 2 * y + c
        for b in range(nb):
            for d, dev, peer in _peers(x, y, c):
                pltpu.make_async_remote_copy(
                    src_ref=srcs[b].at[peer] if places[b] == "pieces" else srcs[b],
                    dst_ref=_slot(dsts[b], places[b], bufs[b].shape, me),
                    send_sem=send_sems.at[b * N_DEV + d], recv_sem=recv_sems.at[b * N_DEV + d],
                    device_id=dev, device_id_type=pl.DeviceIdType.MESH).start()
            pltpu.make_async_copy(srcs[b].at[me] if places[b] == "pieces" else srcs[b],
                                  _slot(dsts[b], places[b], bufs[b].shape, me), recv_sems.at[b * N_DEV]).start()
        token[...] = jnp.zeros_like(token)

    sems = pltpu.SemaphoreType.DMA((nb * N_DEV,))
    ins = [pltpu.with_memory_space_constraint(a, pltpu.HBM) for a in list(bufs) + lands]
    out = pl.pallas_call(
        body,
        out_shape=(sems, sems, *[pltpu.HBM(a.shape, a.dtype) for a in ins], jax.ShapeDtypeStruct((8, 128), F32)),
        in_specs=[_HBM] * (2 * nb) + ([pl.BlockSpec(memory_space=pl.ANY)] if has_after else []),
        out_specs=(_SEM, _SEM, *[_HBM] * (2 * nb), pl.BlockSpec(memory_space=pltpu.VMEM)),
        input_output_aliases={i: 2 + i for i in range(2 * nb)},
        compiler_params=pltpu.CompilerParams(has_side_effects=_EFFECT),
        name=name,
    )(*ins, *([after] if has_after else []))
    return dict(send=out[0], recv=out[1], srcs=out[2:2 + nb], lands=out[2 + nb:2 + 2 * nb], token=out[-1],
                places=places)


def _exchange_wait(h, after, *, name):
    nb = len(h["srcs"])
    places = h["places"]
    shapes = [a.shape for a in h["srcs"]]

    def body(*refs):
        srcs, dsts, send_sems, recv_sems = refs[:nb], refs[nb:2 * nb], refs[2 * nb], refs[2 * nb + 1]
        x, y, c = lax.axis_index("x"), lax.axis_index("y"), lax.axis_index("c")
        me = 4 * x + 2 * y + c
        for b in range(nb):
            pieces = places[b] == "pieces"
            for d, dev, peer in _peers(x, y, c):
                cp = pltpu.make_async_remote_copy(
                    src_ref=srcs[b].at[peer] if pieces else srcs[b],
                    dst_ref=_slot(dsts[b], places[b], shapes[b], peer),
                    send_sem=send_sems.at[b * N_DEV + d], recv_sem=recv_sems.at[b * N_DEV + d],
                    device_id=dev, device_id_type=pl.DeviceIdType.MESH)
                cp.wait_send()
                cp.wait_recv()
            pltpu.make_async_copy(srcs[b].at[me] if pieces else srcs[b], _slot(dsts[b], places[b], shapes[b], me),
                                  recv_sems.at[b * N_DEV]).wait()

    ins = list(h["srcs"]) + list(h["lands"])
    out = pl.pallas_call(
        body,
        out_shape=tuple(pltpu.HBM(a.shape, a.dtype) for a in ins),
        in_specs=[_HBM] * (2 * nb) + [_SEM, _SEM, pl.BlockSpec(memory_space=pl.ANY)],
        out_specs=tuple([_HBM] * (2 * nb)),
        input_output_aliases={i: i for i in range(2 * nb)},
        compiler_params=pltpu.CompilerParams(has_side_effects=_EFFECT),
        name=name,
    )(*ins, h["send"], h["recv"], after)
    return out[nb:]


def _adamw(recv, w, m, v, *, layer=0, into=None, name):
    L, R, C = w.shape
    tr = R
    for cand in (512, 256, 128, 64, 32, 16):
        if R % cand == 0 and cand * C * 4 <= 2 * 1024 * 1024:
            tr = cand
            break
    c1 = 1.0 - ADAM_B1 ** ADAM_STEP
    c2 = 1.0 - ADAM_B2 ** ADAM_STEP

    def body(r_ref, w_ref, m_ref, v_ref, *rest):
        g_ref, d_ref, mo_ref, vo_ref = rest[-4:]
        g = r_ref[0].astype(F32)
        for i in range(1, N_DEV):
            g = g + r_ref[i].astype(F32)
        m_new = ADAM_B1 * m_ref[...] + (1.0 - ADAM_B1) * g
        v_new = ADAM_B2 * v_ref[...] + (1.0 - ADAM_B2) * (g * g)
        m_hat = m_new / c1
        v_hat = v_new / c2
        g_ref[...] = g
        d_ref[...] = -ADAM_LR * (m_hat / (jnp.sqrt(v_hat) + ADAM_EPS) + ADAM_WD * w_ref[...])
        mo_ref[...] = m_new
        vo_ref[...] = v_new

    row = pl.BlockSpec((None, tr, C), lambda i: (layer, i, 0))
    others = [] if into is None else list(into)
    return pl.pallas_call(
        body,
        out_shape=[jax.ShapeDtypeStruct((L, R, C), F32)] * 4,
        grid=(R // tr,),
        in_specs=[pl.BlockSpec((N_DEV, tr, C), lambda i: (0, i, 0)), row, row, row]
        + [pl.BlockSpec(memory_space=pl.ANY)] * len(others),
        out_specs=[row] * 4,
        input_output_aliases={4 + k: k for k in range(len(others))},
        compiler_params=_cparams(("parallel",)),
        name=name,
    )(recv, w, m, v, *others)


_TRANSPOSED = ("mix_w_in", "ffn_w_up")


def _ffn_groups(l):
    return ((f"up{l}", (("ffn_w_up", l, BF16, ("axis", 0)), ("ffn_dw_w", l, F32, "stack"))),
            (f"dn{l}", (("ffn_w_down", l, BF16, ("axis", 0)), ("ple_w_gate", l, BF16, ("axis", 0)),
                        ("ple_w_proj", l, BF16, ("axis", 1)))))


_GATHER_GROUPS = (
    ("mix", (("mix_w_in", 0, BF16, ("axis", 0)), ("conv_dw_w", 0, F32, "stack"))),
    ("mixo", (("mix_w_out", 0, BF16, ("axis", 0)),)),
    *_ffn_groups(0),
    ("attn", (("attn_w_qkv", 0, BF16, ("axis", 1)), ("attn_w_o", 0, BF16, ("axis", 0)))),
    *_ffn_groups(1))
_SHARDED = ("mix_w_in", "conv_dw_w", "mix_w_out", "attn_w_qkv", "attn_w_o", "ffn_w_up", "ffn_dw_w", "ffn_w_down",
            "ple_w_gate", "ple_w_proj")
_REPLICATED = ("pool_w", "pool_scale", "conv_dw_b", "conv_ln_g", "conv_ln_b", "attn_rel_bias", "ln_mix_g",
               "ln_mix_b", "ffn_dw_b", "ple_b_gate", "ln_ffn_g", "ln_ffn_b")


def _pack_rows(parts, row_mult, dtype):
    lead = parts[0].shape[:-1]
    flat = jnp.concatenate([a.astype(dtype) for a in parts], axis=-1)
    n = flat.shape[-1]
    unit = row_mult * LANES
    padded = -(-n // unit) * unit
    flat = jnp.pad(flat, [(0, 0)] * len(lead) + [(0, padded - n)])
    return flat.reshape(lead + (padded // LANES, LANES))


def _unpack(flat2d, shapes):
    flat = flat2d.reshape(-1)
    out, o = [], 0
    for s in shapes:
        n = math.prod(s)
        out.append(flat[o:o + n].reshape(s))
        o += n
    return out


def _full_from_shards(g, axis):
    parts = jnp.moveaxis(g, 0, axis)
    shp = list(g.shape[1:])
    shp[axis] *= g.shape[0]
    return parts.reshape(shp)


def _pieces_from_full(full, axis, k=N_DEV):
    shp = list(full.shape)
    n = shp[axis] // k
    t = full.reshape(shp[:axis] + [k, n] + shp[axis + 1:])
    return jnp.moveaxis(t, axis, 0)


def kernel(x, p, mix_w_in, pool_w, pool_scale, conv_dw_w, conv_dw_b, conv_ln_g, conv_ln_b, mix_w_out, attn_w_qkv, attn_rel_bias, attn_w_o, ln_mix_g, ln_mix_b, ffn_w_up, ffn_dw_w, ffn_dw_b, ffn_w_down, ple_w_proj, ple_w_gate, ple_b_gate, ln_ffn_g, ln_ffn_b, loss_target, m_mix_w_in, m_pool_w, m_pool_scale, m_conv_dw_w, m_conv_dw_b, m_conv_ln_g, m_conv_ln_b, m_mix_w_out, m_attn_w_qkv, m_attn_rel_bias, m_attn_w_o, m_ln_mix_g, m_ln_mix_b, m_ffn_w_up, m_ffn_dw_w, m_ffn_dw_b, m_ffn_w_down, m_ple_w_proj, m_ple_w_gate, m_ple_b_gate, m_ln_ffn_g, m_ln_ffn_b, v_mix_w_in, v_pool_w, v_pool_scale, v_conv_dw_w, v_conv_dw_b, v_conv_ln_g, v_conv_ln_b, v_mix_w_out, v_attn_w_qkv, v_attn_rel_bias, v_attn_w_o, v_ln_mix_g, v_ln_mix_b, v_ffn_w_up, v_ffn_dw_w, v_ffn_dw_b, v_ffn_w_down, v_ple_w_proj, v_ple_w_gate, v_ple_b_gate, v_ln_ffn_g, v_ln_ffn_b):
    a = dict(locals())
    sh_names = list(_SHARDED)
    names = sh_names + list(_REPLICATED)
    wts = {n: a[n] for n in names}
    mom = {n: a["m_" + n] for n in names}
    var = {n: a["v_" + n] for n in names}

    for n in _TRANSPOSED:
        wts[n], mom[n], var[n] = (jnp.swapaxes(d[n], 1, 2) for d in (wts, mom, var))
    gather = {}
    token = None
    for group, items in _GATHER_GROUPS:
        gather[group] = _exchange_start([wts[n][l].astype(dt) for n, l, dt, _ in items], [pl_ for *_, pl_ in items],
                                        token, name="gather_start_" + group)
        token = gather[group]["token"]

    w = dict(pool_w=pool_w[0], pool_scale=pool_scale[0], conv_dw_b=conv_dw_b[0], conv_ln_g=conv_ln_g[0],
             conv_ln_b=conv_ln_b[0], attn_rel_bias=attn_rel_bias[0], ln_mix_g=ln_mix_g, ln_mix_b=ln_mix_b,
             ffn_dw_b=ffn_dw_b, ple_b_gate=ple_b_gate, ln_ffn_g=ln_ffn_g, ln_ffn_b=ln_ffn_b)
    for n in ("ffn_up_t", "ffn_dw_w", "ffn_w_down", "ple_w_gate", "ple_w_proj"):
        w[n] = [None, None]

    def ready(group, after):
        got = _exchange_wait(gather[group], token if after is None else after, name="gather_wait_" + group)
        if group == "mix":
            w["mix_w_in_t"], w["conv_dw_w"] = got[0], _full_from_shards(got[1], 1)
        elif group == "mixo":
            (w["mix_w_out"],) = got
        elif group == "attn":
            w["attn_w_qkv"], w["attn_w_o"] = got
        elif group[:2] == "up":
            l = int(group[2])
            w["ffn_up_t"][l], w["ffn_dw_w"][l] = got[0], _full_from_shards(got[1], 1)
        else:
            l = int(group[2])
            w["ffn_w_down"][l], w["ple_w_gate"][l], w["ple_w_proj"][l] = got

    scatter = {}

    def emit(group, gr):
        if group[:3] == "ffn":
            l = int(group[3])
            pieces = [_pieces_from_full(gr["ffn_up_t"][l], 0),
                      _pieces_from_full(gr["ffn_dw_w"][l], 1), _pieces_from_full(gr["ffn_w_down"][l], 0),
                      _pieces_from_full(gr["ple_w_gate"][l], 0), gr["ple_w_proj"][l]]
        elif group == "attn":
            pieces = [gr["attn_w_qkv"], _pieces_from_full(gr["attn_w_o"], 0)]
        else:
            pieces = [_pieces_from_full(gr["mix_w_in_t"], 0), _pieces_from_full(gr["conv_dw_w"], 1),
                      _pieces_from_full(gr["mix_w_out"], 0)]
        scatter[group] = _exchange_start([a.astype(BF16) for a in pieces], ["pieces"] * len(pieces), None,
                                         name="grad_start_" + group)
        if group != "mix":
            return scatter[group]["token"]
        gfull = dict(
            pool_w=gr["pool_w"][None], pool_scale=gr["pool_scale"][None], conv_dw_b=gr["conv_dw_b"][None],
            conv_ln_g=gr["conv_ln_g"][None], conv_ln_b=gr["conv_ln_b"][None],
            attn_rel_bias=gr["attn_rel_bias"][None], ln_mix_g=jnp.stack(gr["ln_mix_g"]),
            ln_mix_b=jnp.stack(gr["ln_mix_b"]), ffn_dw_b=jnp.stack(gr["ffn_dw_b"]),
            ple_b_gate=jnp.stack(gr["ple_b_gate"]), ln_ffn_g=jnp.stack(gr["ln_ffn_g"]),
            ln_ffn_b=jnp.stack(gr["ln_ffn_b"]))
        rep_send = _pack_rows([gfull[n].reshape(-1) for n in _REPLICATED], 8, F32)
        scatter["replicated"] = _exchange_start([rep_send], ["stack"], scatter[group]["token"],
                                                name="grad_start_replicated")
        return scatter["replicated"]["token"]

    loss_part, grad_x, gr = _local_step(x[0], p[:, 0], loss_target[0], w, ready, emit)
    loss = lax.psum(loss_part, ("x", "y", "c"))

    group_weights = {"ffn1": (("ffn_w_up", 1), ("ffn_dw_w", 1), ("ffn_w_down", 1), ("ple_w_gate", 1), ("ple_w_proj", 1)),
                     "attn": (("attn_w_qkv", 0), ("attn_w_o", 0)),
                     "ffn0": (("ffn_w_up", 0), ("ffn_dw_w", 0), ("ffn_w_down", 0), ("ple_w_gate", 0), ("ple_w_proj", 0)),
                     "mix": (("mix_w_in", 0), ("conv_dw_w", 0), ("mix_w_out", 0))}
    updated = {}
    after = grad_x
    for group in ("ffn1", "attn", "ffn0", "mix"):
        recv = _exchange_wait(scatter[group], after, name="grad_wait_" + group)
        for (n, l), r in zip(group_weights[group], recv):
            updated[n] = _adamw(r, wts[n], mom[n], var[n], layer=l, into=updated.get(n), name=f"adamw_{n}{l}")
            after = updated[n][0]
    res = [{n: jnp.swapaxes(updated[n][k], 1, 2) if n in _TRANSPOSED else updated[n][k] for n in sh_names}
           for k in range(4)]
    (rep_recv,) = _exchange_wait(scatter["replicated"], after, name="grad_wait_replicated")

    def flat_state(d):
        return _pack_rows([d[n].reshape(-1) for n in _REPLICATED], 8, F32)[None]

    rep_out = _adamw(rep_recv, flat_state(wts), flat_state(mom), flat_state(var), name="adamw_replicated")
    for k in range(4):
        for n, arr in zip(_REPLICATED, _unpack(rep_out[k][0], [wts[n].shape for n in _REPLICATED])):
            res[k][n] = arr
    order = ["mix_w_in", "pool_w", "pool_scale", "conv_dw_w", "conv_dw_b", "conv_ln_g", "conv_ln_b", "mix_w_out",
             "attn_w_qkv", "attn_rel_bias", "attn_w_o", "ln_mix_g", "ln_mix_b", "ffn_w_up", "ffn_dw_w", "ffn_dw_b",
             "ffn_w_down", "ple_w_proj", "ple_w_gate", "ple_b_gate", "ln_ffn_g", "ln_ffn_b"]
    outs = [loss, grad_x[None]]
    for k in range(4):
        outs += [res[k][n] for n in order]
    return tuple(outs)
```

```python
import functools
import math

import jax
import jax.numpy as jnp
from jax import lax
from jax.experimental import pallas as pl
from jax.experimental.pallas import tpu as pltpu

F32 = jnp.float32
BF16 = jnp.bfloat16

N_DEV = 8
D_MODEL = 1024
D_POOL = 512
D_CONV = 512
POOL_WINDOWS = (2, 4, 8, 16)
POOL_GROUP = 128
CONV_KERNEL = 31
CHUNK = 64
HEAD_DIM = 64
N_HEADS = 16
LEFT_CHUNKS = 8
BAND = (LEFT_CHUNKS + 1) * CHUNK
MAX_REL = 256
D_FF = 2816
PLE_DIM = 256
ALPHA = 4.0 ** 0.25
LN_EPS = 1e-5
NEG_INF = -1e30
ADAM_LR, ADAM_B1, ADAM_B2, ADAM_EPS, ADAM_WD, ADAM_STEP = 0.001, 0.9, 0.999, 1e-08, 0.01, 10

Q_BLOCK = 4 * CHUNK
KV_PAD = LEFT_CHUNKS * CHUNK
KV_SPAN = KV_PAD + Q_BLOCK
CONV_HALO = 32
FFN_HALO = 16
SUB_ROWS, SUB_LANES = 64, 128
LANES = 1024
VMEM_LIMIT = 56 * 1024 * 1024


def _cparams(sem=None):
    return pltpu.CompilerParams(dimension_semantics=sem, vmem_limit_bytes=VMEM_LIMIT)


def _tile(dim, pref):
    if dim <= pref:
        return dim
    t = pref - pref % 128
    while t >= 128:
        if dim % t == 0:
            return t
        t -= 128
    return dim


def _sigmoid(x):
    return 1.0 / (1.0 + jnp.exp(-x))


def _bdot(a, b, dn=(((1,), (0,)), ((), ()))):
    return lax.dot_general(a.astype(BF16), b.astype(BF16), dn, preferred_element_type=F32)


WHOLE = (0, 1)
NT = (((1,), (1,)), ((), ()))
TN = (((0,), (0,)), ((), ()))


def _wgrad(a, b, *, tm=1024, tn=1024, tk=1024, piece=None, part=(0, 1), into=None, name):
    K, M = a.shape
    kb, N = b.shape
    assert K == kb, (a.shape, b.shape)
    tm, tn, tk = _tile(M, tm), _tile(N, tn), _tile(K, tk)
    nk = K // tk
    per = 1 if piece is None else tn // piece
    assert piece is None or tn == per * piece

    def body(a_ref, b_ref, *rest):
        o_ref, acc = rest[-2:]
        k = pl.program_id(2)

        @pl.when(k == 0)
        def _():
            acc[...] = jnp.zeros_like(acc)

        acc[...] += _bdot(a_ref[...], b_ref[...], TN)

        @pl.when(k == nk - 1)
        def _():
            if piece is None:
                o_ref[...] = acc[...].astype(BF16)
            else:
                for s in range(per):
                    o_ref[s] = acc[:, s * piece:(s + 1) * piece].astype(BF16)

    if piece is None:
        first = part[0] * (M // tm)
        out_shape = (part[1] * M, N)
        out_spec = pl.BlockSpec((tm, tn), lambda i, j, k: (first + i, j))
    else:
        out_shape, out_spec = (N // piece, M, piece), pl.BlockSpec((per, tm, piece), lambda i, j, k: (j, i, 0))
    others = [] if into is None else [into]
    return pl.pallas_call(
        body,
        out_shape=jax.ShapeDtypeStruct(out_shape, BF16),
        grid=(M // tm, N // tn, nk),
        in_specs=[pl.BlockSpec((tk, tm), lambda i, j, k: (k, i)), pl.BlockSpec((tk, tn), lambda i, j, k: (k, j))]
        + [pl.BlockSpec(memory_space=pl.ANY)] * len(others),
        out_specs=out_spec,
        input_output_aliases={2: 0} if others else {},
        scratch_shapes=[pltpu.VMEM((tm, tn), F32)],
        compiler_params=_cparams(("parallel", "parallel", "arbitrary")),
        name=name,
    )(a, b, *others)


def _mm_rows(pairs, *, add=None, add_scale=1.0, out_dtype=F32, tm=256, dep=None, ln_bwd=None, name):
    M = pairs[0][0].shape[0]
    n = len(pairs)
    has_add = add is not None
    w_rows = [w_.shape[0] // part[1] for _, w_, _, part in pairs]
    N = w_rows[0] if pairs[0][2] else pairs[0][1].shape[1]

    def body(*refs):
        acc = None
        for i, (_, _, tr, _) in enumerate(pairs):
            part = _bdot(refs[2 * i][...], refs[2 * i + 1][...], NT if tr else (((1,), (0,)), ((), ())))
            acc = part if acc is None else acc + part
        if has_add:
            acc = acc + add_scale * refs[2 * n][...]
        if ln_bwd is None:
            refs[-1][...] = acc.astype(out_dtype)
            return
        z_ref, g_ref = refs[2 * n + has_add], refs[2 * n + has_add + 1]
        dz_ref, dzb_ref, dg_ref, db_ref = refs[-4:]

        @pl.when(pl.program_id(0) == 0)
        def _():
            dg_ref[...] = jnp.zeros_like(dg_ref)
            db_ref[...] = jnp.zeros_like(db_ref)

        dg_acc = jnp.zeros((8, N), F32)
        db_acc = jnp.zeros((8, N), F32)
        for r0 in range(0, tm, LN_ROWS):
            rows = pl.ds(r0, LN_ROWS)
            do = acc[r0:r0 + LN_ROWS]
            dz, xh = _ln_bwd_rows(z_ref[rows, :], g_ref[...], do)
            dz_ref[rows, :] = dz
            dzb_ref[rows, :] = dz.astype(BF16)
            dg_acc = dg_acc + jnp.sum((do * xh).reshape(LN_ROWS // 8, 8, N), axis=0)
            db_acc = db_acc + jnp.sum(do.reshape(LN_ROWS // 8, 8, N), axis=0)
        dg_ref[...] += jnp.sum(dg_acc, axis=0, keepdims=True)
        db_ref[...] += jnp.sum(db_acc, axis=0, keepdims=True)

    in_specs, args = [], []
    for (a, w_, _, part), rows in zip(pairs, w_rows):
        in_specs += [pl.BlockSpec((tm, a.shape[1]), lambda i: (i, 0)),
                     pl.BlockSpec((rows, w_.shape[1]), functools.partial(lambda i, j: (j, 0), j=part[0]))]
        args += [a, w_]
    row = pl.BlockSpec((tm, N), lambda i: (i, 0))
    fix = pl.BlockSpec((1, N), lambda i: (0, 0))
    if has_add:
        in_specs.append(row)
        args.append(add)
    if ln_bwd is not None:
        in_specs += [row, fix]
        args += [ln_bwd[0], ln_bwd[1].reshape(1, N)]
    if dep is not None:
        in_specs.append(pl.BlockSpec(memory_space=pl.ANY))
        args.append(dep)
    if ln_bwd is None:
        out_shape, out_specs = jax.ShapeDtypeStruct((M, N), out_dtype), row
    else:
        out_shape = [jax.ShapeDtypeStruct((M, N), F32), jax.ShapeDtypeStruct((M, N), BF16),
                     jax.ShapeDtypeStruct((1, N), F32), jax.ShapeDtypeStruct((1, N), F32)]
        out_specs = [row, row, fix, fix]
    return pl.pallas_call(
        body,
        out_shape=out_shape,
        grid=(M // tm,),
        in_specs=in_specs,
        out_specs=out_specs,
        compiler_params=_cparams(("parallel",) if ln_bwd is None else ("arbitrary",)),
        name=name,
    )(*args)


def _ln_bwd_rows(zt, g, do):
    zc = zt - jnp.mean(zt, axis=-1, keepdims=True)
    rstd = lax.rsqrt(jnp.mean(zc * zc, axis=-1, keepdims=True) + LN_EPS)
    xh = zc * rstd
    dxh = do * g
    return rstd * (dxh - jnp.mean(dxh, axis=-1, keepdims=True) - xh * jnp.mean(dxh * xh, axis=-1, keepdims=True)), xh


def _layer_norm_rows(z, g, b):
    mu = jnp.mean(z, axis=-1, keepdims=True)
    zc = z - mu
    var = jnp.mean(zc * zc, axis=-1, keepdims=True)
    return zc * lax.rsqrt(var + LN_EPS) * g + b


def _proj_ln(res, a, w, ln_g, ln_b, *, ple=None, ts=256, name):
    S, D = res.shape
    ka = a.shape[1]
    has_ple = ple is not None
    row = lambda i: (i, 0)
    fix = lambda i: (0, 0)

    def body(*refs):
        if has_ple:
            (res_ref, a_ref, w_ref, g_ref, b_ref, wg_ref, bg_ref, p_ref, wp_ref, z_ref, r_ref, rb_ref, gate_ref,
             proj_ref, acc) = refs
        else:
            res_ref, a_ref, w_ref, g_ref, b_ref, z_ref, r_ref, rb_ref, acc = refs
        acc[...] = _bdot(a_ref[...], w_ref[...])
        if has_ple:
            gate_ref[...] = _bdot(res_ref[...], wg_ref[...])
            proj_ref[...] = _bdot(p_ref[...], wp_ref[...])
        for r0 in range(0, ts, LN_ROWS):
            rows = pl.ds(r0, LN_ROWS)
            z = ALPHA * res_ref[rows, :] + acc[rows, :]
            if has_ple:
                gate = _sigmoid(gate_ref[rows, :] + bg_ref[...])
                gate_ref[rows, :] = gate
                z = z + gate * proj_ref[rows, :]
            z_ref[rows, :] = z
            r = _layer_norm_rows(z, g_ref[...], b_ref[...])
            r_ref[rows, :] = r
            rb_ref[rows, :] = r.astype(BF16)

    in_specs = [pl.BlockSpec((ts, D), row), pl.BlockSpec((ts, ka), row), pl.BlockSpec((ka, D), fix),
                pl.BlockSpec((1, D), fix), pl.BlockSpec((1, D), fix)]
    args = [res, a, w, ln_g.reshape(1, D), ln_b.reshape(1, D)]
    out_dtypes = [F32, F32, BF16]
    if has_ple:
        wg, bg, p, wp = ple
        in_specs += [pl.BlockSpec((D, D), fix), pl.BlockSpec((1, D), fix), pl.BlockSpec((ts, PLE_DIM), row),
                     pl.BlockSpec((PLE_DIM, D), fix)]
        args += [wg, bg.reshape(1, D), p, wp]
        out_dtypes += [F32, F32]
    return pl.pallas_call(
        body,
        out_shape=[jax.ShapeDtypeStruct((S, D), dt) for dt in out_dtypes],
        grid=(S // ts,),
        in_specs=in_specs,
        out_specs=[pl.BlockSpec((ts, D), row)] * len(out_dtypes),
        scratch_shapes=[pltpu.VMEM((ts, D), F32)],
        compiler_params=_cparams(("parallel",)),
        name=name,
    )(*args)


CONV_ROWS = 32
LN_ROWS = 16


def _shifted_copies(src, dst, rows):
    for c0 in range(0, src.shape[1], SUB_LANES):
        ln = pl.ds(c0, SUB_LANES)
        for r0 in range(0, rows, SUB_ROWS):
            rc = min(SUB_ROWS, rows - r0)
            for b, shifted in enumerate(_rows_ahead(src, r0, rc, ln, range(1, 8))):
                dst[b, pl.ds(r0, rc), ln] = shifted


def _rows_at(src, copies, off, n, ln):
    b = off % 8
    return src[pl.ds(off, n), ln] if b == 0 else copies[b - 1, pl.ds(off - b, n), ln]


def _conv31(stg, gsh, cw_ref, cb_ref, out, rows, first_off):
    for c0 in range(0, D_CONV, SUB_LANES):
        ln = pl.ds(c0, SUB_LANES)
        for r0 in range(0, rows, CONV_ROWS):
            acc = jnp.zeros((CONV_ROWS, SUB_LANES), F32) + cb_ref[:, ln]
            for k in range(CONV_KERNEL):
                acc = acc + cw_ref[k:k + 1, ln] * _rows_at(stg, gsh, first_off + k + r0, CONV_ROWS, ln)
            out[pl.ds(r0, CONV_ROWS), ln] = acc


def _mixer_fwd(u, pool_w, pool_scale, conv_w, conv_b, cln_g, cln_b, *, ts=256):
    S = u.shape[0]
    hb = CONV_HALO
    nh = ts // hb

    def body(u_ref, uh_ref, pw_ref, ps_ref, cw_ref, cb_ref, g_ref, b_ref, y_ref, d_ref, hcs, sta, stg, gsh):
        i = pl.program_id(0)
        first = i == 0
        sta[pl.ds(0, hb), :] = jnp.where(first, 0.0, uh_ref[:, 0:D_POOL])
        sta[pl.ds(hb, ts), :] = u_ref[:, 0:D_POOL]
        glu_h = uh_ref[:, D_POOL:D_POOL + D_CONV] * _sigmoid(uh_ref[:, D_POOL + D_CONV:])
        stg[pl.ds(0, hb), :] = jnp.where(first, 0.0, glu_h)
        stg[pl.ds(hb, ts), :] = u_ref[:, D_POOL:D_POOL + D_CONV] * _sigmoid(u_ref[:, D_POOL + D_CONV:])

        for g, w in enumerate(POOL_WINDOWS):
            lanes = pl.ds(g * POOL_GROUP, POOL_GROUP)
            for r0 in range(0, ts, SUB_ROWS):
                s = None
                for q in range(0, w, 8):
                    for tap in _rows_back(sta, hb + r0 - q, SUB_ROWS, lanes, range(min(8, w - q))):
                        s = tap if s is None else s + tap
                pos = (i * ts + r0 + lax.broadcasted_iota(jnp.int32, (SUB_ROWS, 1), 0) + 1).astype(F32)
                d_g = s / jnp.minimum(pos, float(w)) - sta[pl.ds(hb + r0, SUB_ROWS), lanes]
                d_ref[pl.ds(r0, SUB_ROWS), lanes] = d_g.astype(BF16)
            y_ref[:, lanes] = (_bdot(d_ref[:, lanes], pw_ref[g]) * ps_ref[:, lanes]).astype(BF16)

        _shifted_copies(stg, gsh, hb + ts - 8)
        _conv31(stg, gsh, cw_ref, cb_ref, hcs, ts, hb - (CONV_KERNEL - 1))
        for r0 in range(0, ts, LN_ROWS):
            rows = pl.ds(r0, LN_ROWS)
            ln = _layer_norm_rows(hcs[rows, :], g_ref[...], b_ref[...])
            y_ref[rows, D_POOL:] = (ln * _sigmoid(ln)).astype(BF16)

    fix2 = lambda i: (0, 0)
    return pl.pallas_call(
        body,
        out_shape=[jax.ShapeDtypeStruct((S, D_MODEL), BF16), jax.ShapeDtypeStruct((S, D_POOL), BF16),
                   jax.ShapeDtypeStruct((S, D_CONV), F32)],
        grid=(S // ts,),
        in_specs=[pl.BlockSpec((ts, 3 * D_POOL), lambda i: (i, 0)),
                  pl.BlockSpec((hb, 3 * D_POOL), lambda i: (jnp.maximum(i * nh - 1, 0), 0)),
                  pl.BlockSpec((4, POOL_GROUP, POOL_GROUP), lambda i: (0, 0, 0)),
                  pl.BlockSpec((1, D_POOL), fix2), pl.BlockSpec((CONV_KERNEL, D_CONV), fix2),
                  pl.BlockSpec((1, D_CONV), fix2), pl.BlockSpec((1, D_CONV), fix2), pl.BlockSpec((1, D_CONV), fix2)],
        out_specs=[pl.BlockSpec((ts, D_MODEL), lambda i: (i, 0)), pl.BlockSpec((ts, D_POOL), lambda i: (i, 0)),
                   pl.BlockSpec((ts, D_CONV), lambda i: (i, 0))],
        scratch_shapes=[pltpu.VMEM((hb + ts, D_POOL), F32), pltpu.VMEM((hb + ts, D_CONV), F32),
                        pltpu.VMEM((7, hb + ts - 8, D_CONV), F32)],
        compiler_params=_cparams(("parallel",)),
        name="mixer_fwd",
    )(u, u, pool_w, pool_scale.reshape(1, D_POOL), conv_w, conv_b.reshape(1, D_CONV), cln_g.reshape(1, D_CONV),
      cln_b.reshape(1, D_CONV))


def _mixer_bwd(u, d, hc, dycat, pool_w, pool_scale, conv_w, cln_g, cln_b, *, ts=256):
    S = u.shape[0]
    hb = CONV_HALO
    nh = ts // hb
    n = S // ts
    te = ts + hb
    K = CONV_KERNEL

    def body(u_ref, up_ref, un_ref, d_ref, hc_ref, hcn_ref, dy_ref, dyn_ref, pw_ref, ps_ref, cw_ref, g_ref, b_ref,
             du_ref, dpw_ref, dps_ref, dcw_ref, dcb_ref, dg_ref, db_ref, stg, std, sth, gsh, hsh):
        i = pl.program_id(0)
        first = i == 0
        last = i == n - 1

        @pl.when(first)
        def _():
            dpw_ref[...] = jnp.zeros_like(dpw_ref)
            dps_ref[...] = jnp.zeros_like(dps_ref)
            dcw_ref[...] = jnp.zeros_like(dcw_ref)
            dcb_ref[...] = jnp.zeros_like(dcb_ref)
            dg_ref[...] = jnp.zeros_like(dg_ref)
            db_ref[...] = jnp.zeros_like(db_ref)

        pos_e = (i * ts + lax.broadcasted_iota(jnp.int32, (te, 1), 0) + 1).astype(F32)
        dya = dy_ref[:, 0:D_POOL]
        dya_n = jnp.where(last, 0.0, dyn_ref[:, 0:D_POOL])
        for g, w in enumerate(POOL_WINDOWS):
            lanes = pl.ds(g * POOL_GROUP, POOL_GROUP)
            sl = slice(g * POOL_GROUP, (g + 1) * POOL_GROUP)
            pw = pw_ref[g]
            scale = ps_ref[:, lanes]
            d_g = d_ref[:, lanes]
            pre = _bdot(d_g, pw)
            dps_ref[:, lanes] += jnp.sum(dya[:, sl] * pre, axis=0, keepdims=True)
            dys = dya[:, sl] * scale
            dpw_ref[g] += _bdot(d_g, dys, TN)
            dys_e = jnp.concatenate([dys, dya_n[:, sl] * scale], axis=0)
            dd = _bdot(dys_e, pw, NT)
            std[:, lanes] = dd / jnp.minimum(pos_e, float(w))
            for r0 in range(0, ts, SUB_ROWS):
                da = -dd[r0:r0 + SUB_ROWS]
                for q in range(0, w, 8):
                    for tap in _rows_ahead(std, r0 + q, SUB_ROWS, lanes, range(min(8, w - q))):
                        da = da + tap
                du_ref[pl.ds(r0, SUB_ROWS), lanes] = da.astype(BF16)

        glu_p = up_ref[:, D_POOL:D_POOL + D_CONV] * _sigmoid(up_ref[:, D_POOL + D_CONV:])
        stg[pl.ds(0, hb), :] = jnp.where(first, 0.0, glu_p)
        bv = u_ref[:, D_POOL:D_POOL + D_CONV]
        sg = _sigmoid(u_ref[:, D_POOL + D_CONV:])
        stg[pl.ds(hb, ts), :] = bv * sg
        glu_n = un_ref[:, D_POOL:D_POOL + D_CONV] * _sigmoid(un_ref[:, D_POOL + D_CONV:])
        stg[pl.ds(hb + ts, hb), :] = jnp.where(last, 0.0, glu_n)
        _shifted_copies(stg, gsh, hb + te - 8)

        sums = [jnp.zeros((8, D_CONV), F32) for _ in range(3)]
        for r0 in range(0, te, LN_ROWS):
            rows = pl.ds(r0, LN_ROWS)
            hc = hc_ref[rows, :] if r0 < ts else hcn_ref[pl.ds(r0 - ts, LN_ROWS), :]
            hcc = hc - jnp.mean(hc, axis=-1, keepdims=True)
            rstd = lax.rsqrt(jnp.mean(hcc * hcc, axis=-1, keepdims=True) + LN_EPS)
            xh = hcc * rstd
            ln = xh * g_ref[...] + b_ref[...]
            sl_ = _sigmoid(ln)
            if r0 < ts:
                dyb = dy_ref[rows, D_POOL:]
            else:
                dyb = jnp.where(last, 0.0, dyn_ref[pl.ds(r0 - ts, LN_ROWS), D_POOL:])
            dln = dyb * (sl_ * (1.0 + ln * (1.0 - sl_)))
            dxh = dln * g_ref[...]
            dhc = rstd * (dxh - jnp.mean(dxh, axis=-1, keepdims=True)
                          - xh * jnp.mean(dxh * xh, axis=-1, keepdims=True))
            sth[rows, :] = dhc
            if r0 < ts:
                for n_, term in enumerate((dln * xh, dln, dhc)):
                    sums[n_] = sums[n_] + jnp.sum(term.reshape(LN_ROWS // 8, 8, D_CONV), axis=0)
        dg_ref[...] += jnp.sum(sums[0], axis=0, keepdims=True)
        db_ref[...] += jnp.sum(sums[1], axis=0, keepdims=True)
        dcb_ref[...] += jnp.sum(sums[2], axis=0, keepdims=True)

        _shifted_copies(sth, hsh, te - 8)
        for c0 in range(0, D_CONV, SUB_LANES):
            ln_ = pl.ds(c0, SUB_LANES)
            for r0 in range(0, ts, CONV_ROWS):
                rows = pl.ds(r0, CONV_ROWS)
                dglu = jnp.zeros((CONV_ROWS, SUB_LANES), F32)
                for k in range(K):
                    dglu = dglu + cw_ref[k:k + 1, ln_] * _rows_at(sth, hsh, K - 1 - k + r0, CONV_ROWS, ln_)
                bv = u_ref[rows, pl.ds(D_POOL + c0, SUB_LANES)]
                sg = _sigmoid(u_ref[rows, pl.ds(D_POOL + D_CONV + c0, SUB_LANES)])
                du_ref[rows, pl.ds(D_POOL + c0, SUB_LANES)] = (dglu * sg).astype(BF16)
                du_ref[rows, pl.ds(D_POOL + D_CONV + c0, SUB_LANES)] = (dglu * bv * sg * (1.0 - sg)).astype(BF16)
            for k in range(K):
                tap = jnp.zeros((8, SUB_LANES), F32)
                for r0 in range(0, ts, CONV_ROWS):
                    prod = sth[pl.ds(r0, CONV_ROWS), ln_] * _rows_at(stg, gsh, hb - (K - 1) + k + r0, CONV_ROWS, ln_)
                    tap = tap + jnp.sum(prod.reshape(CONV_ROWS // 8, 8, SUB_LANES), axis=0)
                dcw_ref[k:k + 1, ln_] += jnp.sum(tap, axis=0, keepdims=True)

    fix2 = lambda i: (0, 0)
    prev = lambda i: (jnp.maximum(i * nh - 1, 0), 0)
    nxt = lambda i: (jnp.minimum((i + 1) * nh, S // hb - 1), 0)
    return pl.pallas_call(
        body,
        out_shape=[jax.ShapeDtypeStruct((S, 3 * D_POOL), BF16),
                   jax.ShapeDtypeStruct((4, POOL_GROUP, POOL_GROUP), F32),
                   jax.ShapeDtypeStruct((1, D_POOL), F32),
                   jax.ShapeDtypeStruct((K, D_CONV), F32),
                   jax.ShapeDtypeStruct((1, D_CONV), F32),
                   jax.ShapeDtypeStruct((1, D_CONV), F32),
                   jax.ShapeDtypeStruct((1, D_CONV), F32)],
        grid=(n,),
        in_specs=[pl.BlockSpec((ts, 3 * D_POOL), lambda i: (i, 0)),
                  pl.BlockSpec((hb, 3 * D_POOL), prev),
                  pl.BlockSpec((hb, 3 * D_POOL), nxt),
                  pl.BlockSpec((ts, D_POOL), lambda i: (i, 0)),
                  pl.BlockSpec((ts, D_CONV), lambda i: (i, 0)),
                  pl.BlockSpec((hb, D_CONV), nxt),
                  pl.BlockSpec((ts, D_MODEL), lambda i: (i, 0)),
                  pl.BlockSpec((hb, D_MODEL), nxt),
                  pl.BlockSpec((4, POOL_GROUP, POOL_GROUP), lambda i: (0, 0, 0)),
                  pl.BlockSpec((1, D_POOL), fix2), pl.BlockSpec((K, D_CONV), fix2),
                  pl.BlockSpec((1, D_CONV), fix2), pl.BlockSpec((1, D_CONV), fix2)],
        out_specs=[pl.BlockSpec((ts, 3 * D_POOL), lambda i: (i, 0)),
                   pl.BlockSpec((4, POOL_GROUP, POOL_GROUP), lambda i: (0, 0, 0)),
                   pl.BlockSpec((1, D_POOL), fix2), pl.BlockSpec((K, D_CONV), fix2),
                   pl.BlockSpec((1, D_CONV), fix2), pl.BlockSpec((1, D_CONV), fix2), pl.BlockSpec((1, D_CONV), fix2)],
        scratch_shapes=[pltpu.VMEM((hb + ts + hb, D_CONV), F32), pltpu.VMEM((te, D_POOL), F32),
                        pltpu.VMEM((te, D_CONV), F32), pltpu.VMEM((7, hb + te - 8, D_CONV), F32),
                        pltpu.VMEM((7, te - 8, D_CONV), F32)],
        compiler_params=_cparams(("arbitrary",)),
        name="mixer_bwd",
    )(u, u, u, d, hc, hc, dycat, dycat, pool_w, pool_scale.reshape(1, D_POOL), conv_w, cln_g.reshape(1, D_CONV),
      cln_b.reshape(1, D_CONV))


_GELU_C = math.sqrt(2.0 / math.pi)


def _gelu_parts(x):
    inner = _GELU_C * (x + 0.044715 * x * x * x)
    th = jnp.tanh(inner)
    ge = 0.5 * x * (1.0 + th)
    dge = 0.5 * (1.0 + th) + 0.5 * x * (1.0 - th * th) * (_GELU_C * (1.0 + 3.0 * 0.044715 * x * x))
    return ge, dge


def _rows_back(ref, r, n, ln, shifts):
    ext = ref[pl.ds(r - 8, n + 8), ln]
    return [(pltpu.roll(ext, s, 0) if s else ext)[8:] for s in shifts]


def _rows_ahead(ref, r, n, ln, shifts):
    ext = ref[pl.ds(r, n + 8), ln]
    return [(pltpu.roll(ext, n + 8 - s, 0) if s else ext)[:n] for s in shifts]


def _ffn_act_fwd(gate, val, dw_w, dw_b, *, ts=256, tc=1408, name):
    S, F = gate.shape
    hb = FFN_HALO
    nh = ts // hb
    tc = _tile(F, tc)

    def body(g_ref, gh_ref, v_ref, w_ref, b_ref, h_ref, st):
        i = pl.program_id(0)
        st[pl.ds(0, hb), :] = jnp.where(i == 0, 0.0, gh_ref[...].astype(F32))
        st[pl.ds(hb, ts), :] = g_ref[...].astype(F32)
        for c0 in range(0, tc, SUB_LANES):
            ln = pl.ds(c0, SUB_LANES)
            w0, w1, w2, b = w_ref[0:1, ln], w_ref[1:2, ln], w_ref[2:3, ln], b_ref[:, ln]
            for r0 in range(0, ts, SUB_ROWS):
                taps = _rows_back(st, hb + r0, SUB_ROWS, ln, (2, 1, 0))
                gc = b + w0 * taps[0] + w1 * taps[1] + w2 * taps[2]
                ge, _ = _gelu_parts(gc)
                rows = pl.ds(r0, SUB_ROWS)
                h_ref[rows, ln] = (ge * v_ref[rows, ln].astype(F32)).astype(BF16)

    return pl.pallas_call(
        body,
        out_shape=jax.ShapeDtypeStruct((S, F), BF16),
        grid=(S // ts, F // tc),
        in_specs=[pl.BlockSpec((ts, tc), lambda i, j: (i, j)),
                  pl.BlockSpec((hb, tc), lambda i, j: (jnp.maximum(i * nh - 1, 0), j)),
                  pl.BlockSpec((ts, tc), lambda i, j: (i, j)),
                  pl.BlockSpec((3, tc), lambda i, j: (0, j)),
                  pl.BlockSpec((1, tc), lambda i, j: (0, j))],
        out_specs=pl.BlockSpec((ts, tc), lambda i, j: (i, j)),
        scratch_shapes=[pltpu.VMEM((hb + ts, tc), F32)],
        compiler_params=_cparams(("parallel", "parallel")),
        name=name,
    )(gate, gate, val, dw_w, dw_b.reshape(1, F))


def _ffn_act_bwd(gate, val, dh, dw_w, dw_b, *, ts=256, tc=1408, name):
    S, F = gate.shape
    hb = FFN_HALO
    nh = ts // hb
    n = S // ts
    te = ts + hb
    tc = _tile(F, tc)

    def body(g_ref, gp_ref, gn_ref, v_ref, vn_ref, dh_ref, dhn_ref, w_ref, b_ref,
             dg_ref, dv_ref, dw_ref, db_ref, st, sd):
        i = pl.program_id(1)
        first = i == 0
        last = i == n - 1

        @pl.when(first)
        def _():
            dw_ref[...] = jnp.zeros_like(dw_ref)
            db_ref[...] = jnp.zeros_like(db_ref)

        st[pl.ds(0, hb), :] = jnp.where(first, 0.0, gp_ref[...].astype(F32))
        st[pl.ds(hb, ts), :] = g_ref[...].astype(F32)
        st[pl.ds(hb + ts, hb), :] = jnp.where(last, 0.0, gn_ref[...].astype(F32))
        for c0 in range(0, tc, SUB_LANES):
            ln = pl.ds(c0, SUB_LANES)
            w0, w1, w2, b = w_ref[0:1, ln], w_ref[1:2, ln], w_ref[2:3, ln], b_ref[:, ln]
            db_acc = jnp.zeros((8, SUB_LANES), F32)
            dw_acc = [jnp.zeros((8, SUB_LANES), F32) for _ in range(3)]
            for r0 in range(0, te, SUB_ROWS):
                rc = min(SUB_ROWS, te - r0)
                taps = _rows_back(st, hb + r0, rc, ln, (2, 1, 0))
                gc = b + w0 * taps[0] + w1 * taps[1] + w2 * taps[2]
                ge, dge = _gelu_parts(gc)
                if r0 < ts:
                    rows = pl.ds(r0, rc)
                    val, dh = v_ref[rows, ln].astype(F32), dh_ref[rows, ln].astype(F32)
                else:
                    val = jnp.where(last, 0.0, vn_ref[:, ln].astype(F32)[0:rc])
                    dh = jnp.where(last, 0.0, dhn_ref[:, ln].astype(F32)[0:rc])
                dgc = dh * val * dge
                sd[pl.ds(r0, rc), ln] = dgc
                if r0 < ts:
                    dv_ref[rows, ln] = (dh * ge).astype(BF16)
                    db_acc = db_acc + jnp.sum(dgc.reshape(rc // 8, 8, SUB_LANES), axis=0)
                    for k in range(3):
                        dw_acc[k] = dw_acc[k] + jnp.sum((dgc * taps[k]).reshape(rc // 8, 8, SUB_LANES), axis=0)
            db_ref[:, ln] += jnp.sum(db_acc, axis=0, keepdims=True)
            for k in range(3):
                dw_ref[k:k + 1, ln] += jnp.sum(dw_acc[k], axis=0, keepdims=True)
            for r0 in range(0, ts, SUB_ROWS):
                ahead = _rows_ahead(sd, r0, SUB_ROWS, ln, (2, 1, 0))
                dg_ref[pl.ds(r0, SUB_ROWS), ln] = (w0 * ahead[0] + w1 * ahead[1] + w2 * ahead[2]).astype(BF16)

    cur = lambda j, i: (i, j)
    prev = lambda j, i: (jnp.maximum(i * nh - 1, 0), j)
    nxt = lambda j, i: (jnp.minimum((i + 1) * nh, S // hb - 1), j)
    return pl.pallas_call(
        body,
        out_shape=[jax.ShapeDtypeStruct((S, F), BF16), jax.ShapeDtypeStruct((S, F), BF16),
                   jax.ShapeDtypeStruct((3, F), F32), jax.ShapeDtypeStruct((1, F), F32)],
        grid=(F // tc, n),
        in_specs=[pl.BlockSpec((ts, tc), cur), pl.BlockSpec((hb, tc), prev), pl.BlockSpec((hb, tc), nxt),
                  pl.BlockSpec((ts, tc), cur), pl.BlockSpec((hb, tc), nxt),
                  pl.BlockSpec((ts, tc), cur), pl.BlockSpec((hb, tc), nxt),
                  pl.BlockSpec((3, tc), lambda j, i: (0, j)), pl.BlockSpec((1, tc), lambda j, i: (0, j))],
        out_specs=[pl.BlockSpec((ts, tc), cur), pl.BlockSpec((ts, tc), cur),
                   pl.BlockSpec((3, tc), lambda j, i: (0, j)), pl.BlockSpec((1, tc), lambda j, i: (0, j))],
        scratch_shapes=[pltpu.VMEM((hb + ts + hb, tc), F32), pltpu.VMEM((te, tc), F32)],
        compiler_params=_cparams(("parallel", "arbitrary")),
        name=name,
    )(gate, gate, gate, val, val, dh, dh, dw_w, dw_b.reshape(1, F))


def _loss_ln_bwd(z, ln_g, ln_b, target, *, ts=256, name):
    S, D = z.shape

    def body(z_ref, g_ref, b_ref, t_ref, dz_ref, dzb_ref, dg_ref, db_ref, loss_ref):
        i = pl.program_id(0)

        @pl.when(i == 0)
        def _():
            dg_ref[...] = jnp.zeros_like(dg_ref)
            db_ref[...] = jnp.zeros_like(db_ref)
            loss_ref[...] = jnp.zeros_like(loss_ref)

        dg_acc = jnp.zeros((8, D), F32)
        db_acc = jnp.zeros((8, D), F32)
        loss_acc = jnp.zeros((1, 1), F32)
        for r0 in range(0, ts, LN_ROWS):
            rows = pl.ds(r0, LN_ROWS)
            zt = z_ref[rows, :]
            err = _layer_norm_rows(zt, g_ref[...], b_ref[...]) - t_ref[rows, :]
            loss_acc = loss_acc + 0.5 * jnp.sum(jnp.mean(err * err, axis=-1, keepdims=True), keepdims=True)
            do = err * (1.0 / D)
            dz, xh = _ln_bwd_rows(zt, g_ref[...], do)
            dg_acc = dg_acc + jnp.sum((do * xh).reshape(LN_ROWS // 8, 8, D), axis=0)
            db_acc = db_acc + jnp.sum(do.reshape(LN_ROWS // 8, 8, D), axis=0)
            dz_ref[rows, :] = dz
            dzb_ref[rows, :] = dz.astype(BF16)
        dg_ref[...] += jnp.sum(dg_acc, axis=0, keepdims=True)
        db_ref[...] += jnp.sum(db_acc, axis=0, keepdims=True)
        loss_ref[...] += loss_acc

    row = lambda i: (i, 0)
    fix = lambda i: (0, 0)
    return pl.pallas_call(
        body,
        out_shape=[jax.ShapeDtypeStruct((S, D), F32), jax.ShapeDtypeStruct((S, D), BF16),
                   jax.ShapeDtypeStruct((1, D), F32), jax.ShapeDtypeStruct((1, D), F32),
                   jax.ShapeDtypeStruct((8, 128), F32)],
        grid=(S // ts,),
        in_specs=[pl.BlockSpec((ts, D), row), pl.BlockSpec((1, D), fix), pl.BlockSpec((1, D), fix),
                  pl.BlockSpec((ts, D), row)],
        out_specs=[pl.BlockSpec((ts, D), row), pl.BlockSpec((ts, D), row), pl.BlockSpec((1, D), fix),
                   pl.BlockSpec((1, D), fix), pl.BlockSpec((8, 128), fix)],
        compiler_params=_cparams(("arbitrary",)),
        name=name,
    )(z, ln_g.reshape(1, D), ln_b.reshape(1, D), target)


def _ple_bwd(dz, gate, proj, *, ts=256, name):
    S, D = dz.shape

    def body(dz_ref, g_ref, p_ref, ds_ref, dp_ref, db_ref):
        @pl.when(pl.program_id(0) == 0)
        def _():
            db_ref[...] = jnp.zeros_like(db_ref)

        db_acc = jnp.zeros((8, D), F32)
        for r0 in range(0, ts, LN_ROWS):
            rows = pl.ds(r0, LN_ROWS)
            dzt = dz_ref[rows, :]
            g = g_ref[rows, :]
            ds = dzt * p_ref[rows, :] * g * (1.0 - g)
            ds_ref[rows, :] = ds.astype(BF16)
            dp_ref[rows, :] = (dzt * g).astype(BF16)
            db_acc = db_acc + jnp.sum(ds.reshape(LN_ROWS // 8, 8, D), axis=0)
        db_ref[...] += jnp.sum(db_acc, axis=0, keepdims=True)

    row = lambda i: (i, 0)
    return pl.pallas_call(
        body,
        out_shape=[jax.ShapeDtypeStruct((S, D), BF16), jax.ShapeDtypeStruct((S, D), BF16),
                   jax.ShapeDtypeStruct((1, D), F32)],
        grid=(S // ts,),
        in_specs=[pl.BlockSpec((ts, D), row)] * 3,
        out_specs=[pl.BlockSpec((ts, D), row), pl.BlockSpec((ts, D), row), pl.BlockSpec((1, D), lambda i: (0, 0))],
        compiler_params=_cparams(("arbitrary",)),
        name=name,
    )(dz, gate, proj)


HEAD_PAIR = 2 * HEAD_DIM


ATT_ROWS = 32
ATT_SCALE = HEAD_DIM ** -0.5


def _softmax_piece(scores, bias, qb):
    s = scores + bias
    kpos = qb * Q_BLOCK + lax.broadcasted_iota(jnp.int32, (1, KV_SPAN), 1)
    s = jnp.where(kpos >= KV_PAD, s, NEG_INF)
    e = jnp.exp(s - jnp.max(s, axis=-1, keepdims=True))
    return e * (1.0 / jnp.sum(e, axis=-1, keepdims=True))


def _pad_keys(qb, k_ref, v_ref, kp, vp):
    @pl.when(qb == 0)
    def _():
        kp[pl.ds(0, KV_PAD), :] = jnp.zeros((KV_PAD, HEAD_PAIR), BF16)
        vp[pl.ds(0, KV_PAD), :] = jnp.zeros((KV_PAD, HEAD_PAIR), BF16)
        kp[pl.ds(KV_PAD, k_ref.shape[0]), :] = k_ref[...]
        vp[pl.ds(KV_PAD, v_ref.shape[0]), :] = v_ref[...]


def _attn_fwd(qkv, bias):
    S = qkv.shape[0]
    nhp = N_HEADS // 2

    def body(q_ref, k_ref, v_ref, b_ref, o_ref, kp, vp, p_scr):
        qb = pl.program_id(1)
        _pad_keys(qb, k_ref, v_ref, kp, vp)
        span = pl.ds(pl.multiple_of(qb * Q_BLOCK, Q_BLOCK), KV_SPAN)
        kc, vc = kp[span, :], vp[span, :]
        qt = q_ref[...] * ATT_SCALE
        first = lax.broadcasted_iota(jnp.int32, (1, HEAD_PAIR), 1) < HEAD_DIM
        scores = [_bdot(jnp.where(first if j == 0 else ~first, qt, jnp.zeros_like(qt)), kc, NT) for j in range(2)]
        outs = []
        for j in range(2):
            for r0 in range(0, Q_BLOCK, ATT_ROWS):
                rows = pl.ds(r0, ATT_ROWS)
                p_scr[j, rows, :] = _softmax_piece(scores[j][r0:r0 + ATT_ROWS], b_ref[j, rows, :], qb).astype(BF16)
            outs.append(_bdot(p_scr[j], vc))
        o_ref[...] = jnp.where(first, outs[0], outs[1]).astype(BF16)

    return pl.pallas_call(
        body,
        out_shape=jax.ShapeDtypeStruct((S, D_MODEL), BF16),
        grid=(nhp, S // Q_BLOCK),
        in_specs=[pl.BlockSpec((Q_BLOCK, HEAD_PAIR), lambda h, i: (i, h)),
                  pl.BlockSpec((S, HEAD_PAIR), lambda h, i: (0, nhp + h)),
                  pl.BlockSpec((S, HEAD_PAIR), lambda h, i: (0, 2 * nhp + h)),
                  pl.BlockSpec((2, Q_BLOCK, KV_SPAN), lambda h, i: (h, 0, 0))],
        out_specs=pl.BlockSpec((Q_BLOCK, HEAD_PAIR), lambda h, i: (i, h)),
        scratch_shapes=[pltpu.VMEM((KV_PAD + S, HEAD_PAIR), BF16), pltpu.VMEM((KV_PAD + S, HEAD_PAIR), BF16),
                        pltpu.VMEM((2, Q_BLOCK, KV_SPAN), BF16)],
        compiler_params=_cparams(("parallel", "arbitrary")),
        name="attn_fwd",
    )(qkv, qkv, qkv, bias)


def _attn_bwd(qkv, bias, do):
    S = qkv.shape[0]
    nhp = N_HEADS // 2
    nq = S // Q_BLOCK
    scale = HEAD_DIM ** -0.5

    def body(q_ref, k_ref, v_ref, b_ref, do_ref, dq_ref, dk_ref, dv_ref, db_ref, kp, vp, dka, dva,
             p_scr, ds_scr):
        qb = pl.program_id(1)
        _pad_keys(qb, k_ref, v_ref, kp, vp)

        @pl.when(qb == 0)
        def _():
            dka[...] = jnp.zeros_like(dka)
            dva[...] = jnp.zeros_like(dva)
            db_ref[...] = jnp.zeros_like(db_ref)

        span = pl.ds(pl.multiple_of(qb * Q_BLOCK, Q_BLOCK), KV_SPAN)
        kc, vc = kp[span, :], vp[span, :]
        qt, dot = q_ref[...] * ATT_SCALE, do_ref[...]
        first = lax.broadcasted_iota(jnp.int32, (1, HEAD_PAIR), 1) < HEAD_DIM
        dqs = []
        qs = [jnp.where(first if j == 0 else ~first, qt, jnp.zeros_like(qt)) for j in range(2)]
        dos = [jnp.where(first if j == 0 else ~first, dot, jnp.zeros_like(dot)) for j in range(2)]
        scores = [_bdot(qs[j], kc, NT) for j in range(2)]
        dps = [_bdot(dos[j], vc, NT) for j in range(2)]
        for j in range(2):
            qj, doj = qs[j], dos[j]
            for r0 in range(0, Q_BLOCK, ATT_ROWS):
                rows = pl.ds(r0, ATT_ROWS)
                p = _softmax_piece(scores[j][r0:r0 + ATT_ROWS], b_ref[j, rows, :], qb)
                dp = dps[j][r0:r0 + ATT_ROWS]
                ds = p * (dp - jnp.sum(p * dp, axis=-1, keepdims=True))
                db_ref[j, rows, :] += ds
                p_scr[j, rows, :] = p.astype(BF16)
                ds_scr[j, rows, :] = ds.astype(BF16)
            dva[span, :] += _bdot(p_scr[j], doj, TN)
            dqs.append(_bdot(ds_scr[j], kc))
            dka[span, :] += _bdot(ds_scr[j], qj, TN)
        dq_ref[...] = (scale * jnp.where(first, dqs[0], dqs[1])).astype(BF16)

        @pl.when(qb == nq - 1)
        def _():
            dk_ref[...] = dka[pl.ds(KV_PAD, S), :].astype(BF16)
            dv_ref[...] = dva[pl.ds(KV_PAD, S), :].astype(BF16)

    blk = pl.BlockSpec((Q_BLOCK, HEAD_PAIR), lambda h, i: (i, h))
    col = pl.BlockSpec((S, HEAD_PAIR), lambda h, i: (0, h))
    bsp = pl.BlockSpec((2, Q_BLOCK, KV_SPAN), lambda h, i: (h, 0, 0))
    return pl.pallas_call(
        body,
        out_shape=[jax.ShapeDtypeStruct((S, D_MODEL), BF16)] * 3
        + [jax.ShapeDtypeStruct((N_HEADS, Q_BLOCK, KV_SPAN), F32)],
        grid=(nhp, nq),
        in_specs=[blk, pl.BlockSpec((S, HEAD_PAIR), lambda h, i: (0, nhp + h)),
                  pl.BlockSpec((S, HEAD_PAIR), lambda h, i: (0, 2 * nhp + h)), bsp, blk],
        out_specs=[blk, col, col, bsp],
        scratch_shapes=[pltpu.VMEM((KV_PAD + S, HEAD_PAIR), BF16), pltpu.VMEM((KV_PAD + S, HEAD_PAIR), BF16),
                        pltpu.VMEM((KV_PAD + S, HEAD_PAIR), F32), pltpu.VMEM((KV_PAD + S, HEAD_PAIR), F32),
                        pltpu.VMEM((2, Q_BLOCK, KV_SPAN), BF16), pltpu.VMEM((2, Q_BLOCK, KV_SPAN), BF16)],
        compiler_params=_cparams(("parallel", "arbitrary")),
        name="attn_bwd",
    )(qkv, qkv, qkv, bias, do)


N_DIST = BAND + CHUNK - 1
N_FAR = KV_PAD + CHUNK - MAX_REL


def _shear_rows(x, towards_right):
    row = lax.broadcasted_iota(jnp.int32, (Q_BLOCK, 1), 0)
    for bit in range(Q_BLOCK.bit_length() - 1):
        step = 1 << bit
        x = jnp.where((row & step) != 0, pltpu.roll(x, step if towards_right else KV_SPAN - step, 1), x)
    return x


def _bias_blocks(rel_bias):
    H = rel_bias.shape[0]
    e = jnp.concatenate([jnp.broadcast_to(rel_bias[:, 2 * MAX_REL:], (H, N_FAR)),
                         jnp.flip(rel_bias[:, 2 * MAX_REL - (N_DIST - N_FAR):2 * MAX_REL], axis=1),
                         jnp.zeros((H, KV_SPAN - N_DIST), F32)], axis=1).reshape(H, 1, KV_SPAN)

    def body(e_ref, o_ref):
        first = pltpu.roll(jnp.broadcast_to(e_ref[...], (Q_BLOCK, KV_SPAN)), KV_SPAN - (CHUNK - 1), 1)
        x = _shear_rows(first, True)
        row = lax.broadcasted_iota(jnp.int32, (Q_BLOCK, 1), 0)
        chunk0 = row - (row & (CHUNK - 1))
        k = lax.broadcasted_iota(jnp.int32, (1, KV_SPAN), 1)
        o_ref[...] = jnp.where((k >= chunk0) & (k < chunk0 + BAND), x, NEG_INF)

    return pl.pallas_call(
        body,
        out_shape=jax.ShapeDtypeStruct((H, Q_BLOCK, KV_SPAN), F32),
        grid=(H,),
        in_specs=[pl.BlockSpec((None, 1, KV_SPAN), lambda h: (h, 0, 0))],
        out_specs=pl.BlockSpec((None, Q_BLOCK, KV_SPAN), lambda h: (h, 0, 0)),
        compiler_params=_cparams(("parallel",)),
        name="bias_blocks",
    )(e)


def _bias_blocks_grad(dblk):
    H = dblk.shape[0]

    def body(d_ref, o_ref):
        x = pltpu.roll(_shear_rows(d_ref[...], False), CHUNK - 1, 1)
        de = jnp.sum(x, axis=0, keepdims=True)
        lane = lax.broadcasted_iota(jnp.int32, de.shape, 1)
        far = jnp.sum(jnp.where(lane < N_FAR, de, 0.0), axis=-1, keepdims=True)
        o_ref[...] = jnp.where(lane == 0, far, jnp.where(lane < N_FAR, 0.0, de))

    de = pl.pallas_call(
        body,
        out_shape=jax.ShapeDtypeStruct((H, 1, KV_SPAN), F32),
        grid=(H,),
        in_specs=[pl.BlockSpec((None, Q_BLOCK, KV_SPAN), lambda h: (h, 0, 0))],
        out_specs=pl.BlockSpec((None, 1, KV_SPAN), lambda h: (h, 0, 0)),
        compiler_params=_cparams(("parallel",)),
        name="bias_grad_sum",
    )(dblk).reshape(H, KV_SPAN)
    near = jnp.flip(de[:, N_FAR:N_DIST], axis=1)
    return jnp.concatenate([jnp.zeros((H, 2 * MAX_REL - (N_DIST - N_FAR)), F32), near, de[:, 0:1]], axis=1)


def _ffn_forward(r1, r1b, p_l, w, l, ready):
    ready(f"up{l}", r1b)
    up_g = _mm_rows([(r1b, w["ffn_up_t"][l], True, (0, 2))], out_dtype=BF16, name=f"ffn_up_g{l}")
    up_v = _mm_rows([(r1b, w["ffn_up_t"][l], True, (1, 2))], out_dtype=BF16, name=f"ffn_up_v{l}")
    h = _ffn_act_fwd(up_g, up_v, w["ffn_dw_w"][l], w["ffn_dw_b"][l], name=f"ffn_act{l}")
    ready(f"dn{l}", h)
    z2, r2, r2b, gate, proj = _proj_ln(r1, h, w["ffn_w_down"][l], w["ln_ffn_g"][l], w["ln_ffn_b"][l],
                                       ple=(w["ple_w_gate"][l], w["ple_b_gate"][l], p_l, w["ple_w_proj"][l]),
                                       name=f"ffn_down_ln{l}")
    return dict(r1b=r1b, up_g=up_g, up_v=up_v, h=h, z2=z2, gate=gate, proj=proj), r2, r2b


def _ffn_backward(sv, dz2, dz2b, p_l, w, l, grads, ln_bwd, emit):
    r1b = sv["r1b"]
    ds, dproj, db_gate = _ple_bwd(dz2, sv["gate"], sv["proj"], name=f"ple_bwd{l}")
    dh = _mm_rows([(dz2b, w["ffn_w_down"][l], True, WHOLE)], out_dtype=BF16, name=f"ffn_dh{l}")
    dgate, dval, d_dw_w, d_dw_b = _ffn_act_bwd(sv["up_g"], sv["up_v"], dh, w["ffn_dw_w"][l], w["ffn_dw_b"][l],
                                               name=f"ffn_act_bwd{l}")
    grads["ffn_w_down"][l] = _wgrad(sv["h"], dz2b, tm=1408, name=f"d_ffn_w_down{l}")
    d_up_g = _wgrad(dgate, r1b, tm=1408, part=(0, 2), name=f"d_ffn_up_g{l}")
    grads["ffn_up_t"][l] = _wgrad(dval, r1b, tm=1408, part=(1, 2), into=d_up_g, name=f"d_ffn_up_v{l}")
    grads["ple_w_gate"][l] = _wgrad(r1b, ds, name=f"d_ple_w_gate{l}")
    grads["ple_w_proj"][l] = _wgrad(p_l, dproj, piece=D_MODEL // N_DEV, name=f"d_ple_w_proj{l}")
    grads["ffn_dw_w"][l] = d_dw_w
    grads["ffn_dw_b"][l] = d_dw_b[0]
    grads["ple_b_gate"][l] = db_gate[0]
    return _mm_rows([(ds, w["ple_w_gate"][l], True, WHOLE), (dgate, w["ffn_up_t"][l], False, (0, 2)),
                     (dval, w["ffn_up_t"][l], False, (1, 2))], add=dz2, add_scale=ALPHA, ln_bwd=ln_bwd, dep=emit(),
                    name=f"dr1_{l}")


def _local_step(x, p, target, w, ready=lambda group, after: None, emit=lambda group, grads: None):
    grads = {k: [None, None] for k in ("ffn_w_down", "ffn_up_t", "ple_w_gate", "ple_w_proj", "ffn_dw_w",
                                       "ffn_dw_b", "ple_b_gate", "ln_ffn_g", "ln_ffn_b", "ln_mix_g", "ln_mix_b")}

    xb, pb = x.astype(BF16), p.astype(BF16)
    ready("mix", None)
    u = _mm_rows([(xb, w["mix_w_in_t"], True, WHOLE)], name="mix_in")
    ycat, dpool, hconv = _mixer_fwd(u, w["pool_w"], w["pool_scale"], w["conv_dw_w"], w["conv_dw_b"], w["conv_ln_g"],
                                    w["conv_ln_b"])
    ready("mixo", ycat)
    z1, r1, r1b = _proj_ln(x, ycat, w["mix_w_out"], w["ln_mix_g"][0], w["ln_mix_b"][0], name="mix_out_ln")
    sv0, r2, r2b = _ffn_forward(r1, r1b, pb[0], w, 0, ready)

    ready("attn", r2b)
    qkv = _mm_rows([(r2b, w["attn_w_qkv"], False, WHOLE)], out_dtype=BF16, name="attn_qkv")
    bias = _bias_blocks(w["attn_rel_bias"])
    attn = _attn_fwd(qkv, bias)
    z3, r3, r3b = _proj_ln(r2, attn, w["attn_w_o"], w["ln_mix_g"][1], w["ln_mix_b"][1], name="attn_out_ln")
    sv1, _, _ = _ffn_forward(r3, r3b, pb[1], w, 1, ready)

    dz4, dz4b, grads["ln_ffn_g"][1], grads["ln_ffn_b"][1], loss = _loss_ln_bwd(
        sv1["z2"], w["ln_ffn_g"][1], w["ln_ffn_b"][1], target, name="loss_ln_bwd")
    dz3, dz3b, grads["ln_mix_g"][1], grads["ln_mix_b"][1] = _ffn_backward(
        sv1, dz4, dz4b, pb[1], w, 1, grads, (z3, w["ln_mix_g"][1]), lambda: emit("ffn1", grads))
    grads["attn_w_o"] = _wgrad(attn, dz3b, name="d_attn_w_o")
    dattn = _mm_rows([(dz3b, w["attn_w_o"], True, WHOLE)], out_dtype=BF16, name="d_attn")
    dq, dk, dv, dbias = _attn_bwd(qkv, bias, dattn)
    grads["attn_rel_bias"] = _bias_blocks_grad(dbias)
    dqkv = jnp.concatenate([dq, dk, dv], axis=1)
    grads["attn_w_qkv"] = _wgrad(r2b, dqkv, tn=768, piece=3 * D_MODEL // N_DEV, name="d_attn_w_qkv")
    dz2, dz2b, grads["ln_ffn_g"][0], grads["ln_ffn_b"][0] = _mm_rows(
        [(dqkv, w["attn_w_qkv"], True, WHOLE)], add=dz3, add_scale=ALPHA, ln_bwd=(sv0["z2"], w["ln_ffn_g"][0]),
        dep=emit("attn", grads), name="dr2")
    dz1, dz1b, grads["ln_mix_g"][0], grads["ln_mix_b"][0] = _ffn_backward(
        sv0, dz2, dz2b, pb[0], w, 0, grads, (z1, w["ln_mix_g"][0]), lambda: emit("ffn0", grads))
    grads["mix_w_out"] = _wgrad(ycat, dz1b, name="d_mix_w_out")
    dycat = _mm_rows([(dz1b, w["mix_w_out"], True, WHOLE)], name="d_ycat")
    du, g_pw, g_ps, g_cw, g_cb, g_cg, g_cbb = _mixer_bwd(u, dpool, hconv, dycat, w["pool_w"], w["pool_scale"],
                                                         w["conv_dw_w"], w["conv_ln_g"], w["conv_ln_b"])
    grads["mix_w_in_t"] = _wgrad(du, xb, name="d_mix_w_in")
    grads.update(pool_w=g_pw, pool_scale=g_ps[0], conv_dw_w=g_cw, conv_dw_b=g_cb[0], conv_ln_g=g_cg[0],
                 conv_ln_b=g_cbb[0])
    for kname in ("ln_ffn_g", "ln_ffn_b", "ln_mix_g", "ln_mix_b"):
        grads[kname] = [a[0] for a in grads[kname]]
    grad_x = _mm_rows([(du, w["mix_w_in_t"], False, WHOLE)], add=dz1, add_scale=ALPHA, dep=emit("mix", grads),
                      name="grad_x")
    return loss[0, 0], grad_x, grads


_HBM = pl.BlockSpec(memory_space=pltpu.HBM)
_SEM = pl.BlockSpec(memory_space=pltpu.SEMAPHORE)
_EFFECT = pltpu.SideEffectType.DATAFLOW_SIDE_EFFECTING


def _slot(ref, place, shape, k):
    if place in ("stack", "pieces"):
        return ref.at[k]
    ax = place[1]
    n = shape[ax]
    return ref.at[(slice(None),) * ax + (pl.ds(pl.multiple_of(k * n, n), n),)]


def _result_shape(buf, place):
    if place == "stack":
        return (N_DEV,) + buf.shape
    if place == "pieces":
        return buf.shape
    return tuple(s * N_DEV if i == place[1] else s for i, s in enumerate(buf.shape))


def _peers(x, y, c):
    for d in range(1, N_DEV):
        px, py, pc = x ^ ((d >> 2) & 1), y ^ ((d >> 1) & 1), c ^ (d & 1)
        yield d, (px, py, pc), 4 * px + 2 * py + pc


def _exchange_start(bufs, places, after, *, name):
    nb = len(bufs)
    lands = [lax.empty(_result_shape(b, p_), b.dtype) for b, p_ in zip(bufs, places)]
    has_after = after is not None

    def body(*refs):
        srcs, dsts = refs[:nb], refs[nb:2 * nb]
        outs = refs[2 * nb + has_after:]
        send_sems, recv_sems, token = outs[0], outs[1], outs[2 + 2 * nb]
        x, y, c = lax.axis_index("x"), lax.axis_index("y"), lax.axis_index("c")
        me = 4 * x + 2 * y + c
        for b in range(nb):
            for d, dev, peer in _peers(x, y, c):
                pltpu.make_async_remote_copy(
                    src_ref=srcs[b].at[peer] if places[b] == "pieces" else srcs[b],
                    dst_ref=_slot(dsts[b], places[b], bufs[b].shape, me),
                    send_sem=send_sems.at[b * N_DEV + d], recv_sem=recv_sems.at[b * N_DEV + d],
                    device_id=dev, device_id_type=pl.DeviceIdType.MESH).start()
            pltpu.make_async_copy(srcs[b].at[me] if places[b] == "pieces" else srcs[b],
                                  _slot(dsts[b], places[b], bufs[b].shape, me), recv_sems.at[b * N_DEV]).start()
        token[...] = jnp.zeros_like(token)

    sems = pltpu.SemaphoreType.DMA((nb * N_DEV,))
    ins = [pltpu.with_memory_space_constraint(a, pltpu.HBM) for a in list(bufs) + lands]
    out = pl.pallas_call(
        body,
        out_shape=(sems, sems, *[pltpu.HBM(a.shape, a.dtype) for a in ins], jax.ShapeDtypeStruct((8, 128), F32)),
        in_specs=[_HBM] * (2 * nb) + ([pl.BlockSpec(memory_space=pl.ANY)] if has_after else []),
        out_specs=(_SEM, _SEM, *[_HBM] * (2 * nb), pl.BlockSpec(memory_space=pltpu.VMEM)),
        input_output_aliases={i: 2 + i for i in range(2 * nb)},
        compiler_params=pltpu.CompilerParams(has_side_effects=_EFFECT),
        name=name,
    )(*ins, *([after] if has_after else []))
    return dict(send=out[0], recv=out[1], srcs=out[2:2 + nb], lands=out[2 + nb:2 + 2 * nb], token=out[-1],
                places=places)


def _exchange_wait(h, after, *, name):
    nb = len(h["srcs"])
    places = h["places"]
    shapes = [a.shape for a in h["srcs"]]

    def body(*refs):
        srcs, dsts, send_sems, recv_sems = refs[:nb], refs[nb:2 * nb], refs[2 * nb], refs[2 * nb + 1]
        x, y, c = lax.axis_index("x"), lax.axis_index("y"), lax.axis_index("c")
        me = 4 * x + 2 * y + c
        for b in range(nb):
            pieces = places[b] == "pieces"
            for d, dev, peer in _peers(x, y, c):
                cp = pltpu.make_async_remote_copy(
                    src_ref=srcs[b].at[peer] if pieces else srcs[b],
                    dst_ref=_slot(dsts[b], places[b], shapes[b], peer),
                    send_sem=send_sems.at[b * N_DEV + d], recv_sem=recv_sems.at[b * N_DEV + d],
                    device_id=dev, device_id_type=pl.DeviceIdType.MESH)
                cp.wait_send()
                cp.wait_recv()
            pltpu.make_async_copy(srcs[b].at[me] if pieces else srcs[b], _slot(dsts[b], places[b], shapes[b], me),
                                  recv_sems.at[b * N_DEV]).wait()

    ins = list(h["srcs"]) + list(h["lands"])
    out = pl.pallas_call(
        body,
        out_shape=tuple(pltpu.HBM(a.shape, a.dtype) for a in ins),
        in_specs=[_HBM] * (2 * nb) + [_SEM, _SEM, pl.BlockSpec(memory_space=pl.ANY)],
        out_specs=tuple([_HBM] * (2 * nb)),
        input_output_aliases={i: i for i in range(2 * nb)},
        compiler_params=pltpu.CompilerParams(has_side_effects=_EFFECT),
        name=name,
    )(*ins, h["send"], h["recv"], after)
    return out[nb:]


def _adamw(recv, w, m, v, *, layer=0, into=None, name):
    L, R, C = w.shape
    tr = R
    for cand in (512, 256, 128, 64, 32, 16):
        if R % cand == 0 and cand * C * 4 <= 2 * 1024 * 1024:
            tr = cand
            break
    c1 = 1.0 - ADAM_B1 ** ADAM_STEP
    c2 = 1.0 - ADAM_B2 ** ADAM_STEP

    def body(r_ref, w_ref, m_ref, v_ref, *rest):
        g_ref, d_ref, mo_ref, vo_ref = rest[-4:]
        g = r_ref[0].astype(F32)
        for i in range(1, N_DEV):
            g = g + r_ref[i].astype(F32)
        m_new = ADAM_B1 * m_ref[...] + (1.0 - ADAM_B1) * g
        v_new = ADAM_B2 * v_ref[...] + (1.0 - ADAM_B2) * (g * g)
        m_hat = m_new / c1
        v_hat = v_new / c2
        g_ref[...] = g
        d_ref[...] = -ADAM_LR * (m_hat / (jnp.sqrt(v_hat) + ADAM_EPS) + ADAM_WD * w_ref[...])
        mo_ref[...] = m_new
        vo_ref[...] = v_new

    row = pl.BlockSpec((None, tr, C), lambda i: (layer, i, 0))
    others = [] if into is None else list(into)
    return pl.pallas_call(
        body,
        out_shape=[jax.ShapeDtypeStruct((L, R, C), F32)] * 4,
        grid=(R // tr,),
        in_specs=[pl.BlockSpec((N_DEV, tr, C), lambda i: (0, i, 0)), row, row, row]
        + [pl.BlockSpec(memory_space=pl.ANY)] * len(others),
        out_specs=[row] * 4,
        input_output_aliases={4 + k: k for k in range(len(others))},
        compiler_params=_cparams(("parallel",)),
        name=name,
    )(recv, w, m, v, *others)


_TRANSPOSED = ("mix_w_in", "ffn_w_up")


def _ffn_groups(l):
    return ((f"up{l}", (("ffn_w_up", l, BF16, ("axis", 0)), ("ffn_dw_w", l, F32, "stack"))),
            (f"dn{l}", (("ffn_w_down", l, BF16, ("axis", 0)), ("ple_w_gate", l, BF16, ("axis", 0)),
                        ("ple_w_proj", l, BF16, ("axis", 1)))))


_GATHER_GROUPS = (
    ("mix", (("mix_w_in", 0, BF16, ("axis", 0)), ("conv_dw_w", 0, F32, "stack"))),
    ("mixo", (("mix_w_out", 0, BF16, ("axis", 0)),)),
    *_ffn_groups(0),
    ("attn", (("attn_w_qkv", 0, BF16, ("axis", 1)), ("attn_w_o", 0, BF16, ("axis", 0)))),
    *_ffn_groups(1))
_SHARDED = ("mix_w_in", "conv_dw_w", "mix_w_out", "attn_w_qkv", "attn_w_o", "ffn_w_up", "ffn_dw_w", "ffn_w_down",
            "ple_w_gate", "ple_w_proj")
_REPLICATED = ("pool_w", "pool_scale", "conv_dw_b", "conv_ln_g", "conv_ln_b", "attn_rel_bias", "ln_mix_g",
               "ln_mix_b", "ffn_dw_b", "ple_b_gate", "ln_ffn_g", "ln_ffn_b")


def _pack_rows(parts, row_mult, dtype):
    lead = parts[0].shape[:-1]
    flat = jnp.concatenate([a.astype(dtype) for a in parts], axis=-1)
    n = flat.shape[-1]
    unit = row_mult * LANES
    padded = -(-n // unit) * unit
    flat = jnp.pad(flat, [(0, 0)] * len(lead) + [(0, padded - n)])
    return flat.reshape(lead + (padded // LANES, LANES))


def _unpack(flat2d, shapes):
    flat = flat2d.reshape(-1)
    out, o = [], 0
    for s in shapes:
        n = math.prod(s)
        out.append(flat[o:o + n].reshape(s))
        o += n
    return out


def _full_from_shards(g, axis):
    parts = jnp.moveaxis(g, 0, axis)
    shp = list(g.shape[1:])
    shp[axis] *= g.shape[0]
    return parts.reshape(shp)


def _pieces_from_full(full, axis, k=N_DEV):
    shp = list(full.shape)
    n = shp[axis] // k
    t = full.reshape(shp[:axis] + [k, n] + shp[axis + 1:])
    return jnp.moveaxis(t, axis, 0)


def kernel(x, p, mix_w_in, pool_w, pool_scale, conv_dw_w, conv_dw_b, conv_ln_g, conv_ln_b, mix_w_out, attn_w_qkv, attn_rel_bias, attn_w_o, ln_mix_g, ln_mix_b, ffn_w_up, ffn_dw_w, ffn_dw_b, ffn_w_down, ple_w_proj, ple_w_gate, ple_b_gate, ln_ffn_g, ln_ffn_b, loss_target, m_mix_w_in, m_pool_w, m_pool_scale, m_conv_dw_w, m_conv_dw_b, m_conv_ln_g, m_conv_ln_b, m_mix_w_out, m_attn_w_qkv, m_attn_rel_bias, m_attn_w_o, m_ln_mix_g, m_ln_mix_b, m_ffn_w_up, m_ffn_dw_w, m_ffn_dw_b, m_ffn_w_down, m_ple_w_proj, m_ple_w_gate, m_ple_b_gate, m_ln_ffn_g, m_ln_ffn_b, v_mix_w_in, v_pool_w, v_pool_scale, v_conv_dw_w, v_conv_dw_b, v_conv_ln_g, v_conv_ln_b, v_mix_w_out, v_attn_w_qkv, v_attn_rel_bias, v_attn_w_o, v_ln_mix_g, v_ln_mix_b, v_ffn_w_up, v_ffn_dw_w, v_ffn_dw_b, v_ffn_w_down, v_ple_w_proj, v_ple_w_gate, v_ple_b_gate, v_ln_ffn_g, v_ln_ffn_b):
    a = dict(locals())
    sh_names = list(_SHARDED)
    names = sh_names + list(_REPLICATED)
    wts = {n: a[n] for n in names}
    mom = {n: a["m_" + n] for n in names}
    var = {n: a["v_" + n] for n in names}

    for n in _TRANSPOSED:
        wts[n], mom[n], var[n] = (jnp.swapaxes(d[n], 1, 2) for d in (wts, mom, var))
    gather = {}
    token = None
    for group, items in _GATHER_GROUPS:
        gather[group] = _exchange_start([wts[n][l].astype(dt) for n, l, dt, _ in items], [pl_ for *_, pl_ in items],
                                        token, name="gather_start_" + group)
        token = gather[group]["token"]

    w = dict(pool_w=pool_w[0], pool_scale=pool_scale[0], conv_dw_b=conv_dw_b[0], conv_ln_g=conv_ln_g[0],
             conv_ln_b=conv_ln_b[0], attn_rel_bias=attn_rel_bias[0], ln_mix_g=ln_mix_g, ln_mix_b=ln_mix_b,
             ffn_dw_b=ffn_dw_b, ple_b_gate=ple_b_gate, ln_ffn_g=ln_ffn_g, ln_ffn_b=ln_ffn_b)
    for n in ("ffn_up_t", "ffn_dw_w", "ffn_w_down", "ple_w_gate", "ple_w_proj"):
        w[n] = [None, None]

    def ready(group, after):
        got = _exchange_wait(gather[group], token if after is None else after, name="gather_wait_" + group)
        if group == "mix":
            w["mix_w_in_t"], w["conv_dw_w"] = got[0], _full_from_shards(got[1], 1)
        elif group == "mixo":
            (w["mix_w_out"],) = got
        elif group == "attn":
            w["attn_w_qkv"], w["attn_w_o"] = got
        elif group[:2] == "up":
            l = int(group[2])
            w["ffn_up_t"][l], w["ffn_dw_w"][l] = got[0], _full_from_shards(got[1], 1)
        else:
            l = int(group[2])
            w["ffn_w_down"][l], w["ple_w_gate"][l], w["ple_w_proj"][l] = got

    scatter = {}

    def emit(group, gr):
        if group[:3] == "ffn":
            l = int(group[3])
            pieces = [_pieces_from_full(gr["ffn_up_t"][l], 0),
                      _pieces_from_full(gr["ffn_dw_w"][l], 1), _pieces_from_full(gr["ffn_w_down"][l], 0),
                      _pieces_from_full(gr["ple_w_gate"][l], 0), gr["ple_w_proj"][l]]
        elif group == "attn":
            pieces = [gr["attn_w_qkv"], _pieces_from_full(gr["attn_w_o"], 0)]
        else:
            pieces = [_pieces_from_full(gr["mix_w_in_t"], 0), _pieces_from_full(gr["conv_dw_w"], 1),
                      _pieces_from_full(gr["mix_w_out"], 0)]
        scatter[group] = _exchange_start([a.astype(BF16) for a in pieces], ["pieces"] * len(pieces), None,
                                         name="grad_start_" + group)
        if group != "mix":
            return scatter[group]["token"]
        gfull = dict(
            pool_w=gr["pool_w"][None], pool_scale=gr["pool_scale"][None], conv_dw_b=gr["conv_dw_b"][None],
            conv_ln_g=gr["conv_ln_g"][None], conv_ln_b=gr["conv_ln_b"][None],
            attn_rel_bias=gr["attn_rel_bias"][None], ln_mix_g=jnp.stack(gr["ln_mix_g"]),
            ln_mix_b=jnp.stack(gr["ln_mix_b"]), ffn_dw_b=jnp.stack(gr["ffn_dw_b"]),
            ple_b_gate=jnp.stack(gr["ple_b_gate"]), ln_ffn_g=jnp.stack(gr["ln_ffn_g"]),
            ln_ffn_b=jnp.stack(gr["ln_ffn_b"]))
        rep_send = _pack_rows([gfull[n].reshape(-1) for n in _REPLICATED], 8, F32)
        scatter["replicated"] = _exchange_start([rep_send], ["stack"], scatter[group]["token"],
                                                name="grad_start_replicated")
        return scatter["replicated"]["token"]

    loss_part, grad_x, gr = _local_step(x[0], p[:, 0], loss_target[0], w, ready, emit)
    loss = lax.psum(loss_part, ("x", "y", "c"))

    group_weights = {"ffn1": (("ffn_w_up", 1), ("ffn_dw_w", 1), ("ffn_w_down", 1), ("ple_w_gate", 1), ("ple_w_proj", 1)),
                     "attn": (("attn_w_qkv", 0), ("attn_w_o", 0)),
                     "ffn0": (("ffn_w_up", 0), ("ffn_dw_w", 0), ("ffn_w_down", 0), ("ple_w_gate", 0), ("ple_w_proj", 0)),
                     "mix": (("mix_w_in", 0), ("conv_dw_w", 0), ("mix_w_out", 0))}
    updated = {}
    after = grad_x
    for group in ("ffn1", "attn", "ffn0", "mix"):
        recv = _exchange_wait(scatter[group], after, name="grad_wait_" + group)
        for (n, l), r in zip(group_weights[group], recv):
            updated[n] = _adamw(r, wts[n], mom[n], var[n], layer=l, into=updated.get(n), name=f"adamw_{n}{l}")
            after = updated[n][0]
    res = [{n: jnp.swapaxes(updated[n][k], 1, 2) if n in _TRANSPOSED else updated[n][k] for n in sh_names}
           for k in range(4)]
    (rep_recv,) = _exchange_wait(scatter["replicated"], after, name="grad_wait_replicated")

    def flat_state(d):
        return _pack_rows([d[n].reshape(-1) for n in _REPLICATED], 8, F32)[None]

    rep_out = _adamw(rep_recv, flat_state(wts), flat_state(mom), flat_state(var), name="adamw_replicated")
    for k in range(4):
        for n, arr in zip(_REPLICATED, _unpack(rep_out[k][0], [wts[n].shape for n in _REPLICATED])):
            res[k][n] = arr
    order = ["mix_w_in", "pool_w", "pool_scale", "conv_dw_w", "conv_dw_b", "conv_ln_g", "conv_ln_b", "mix_w_out",
             "attn_w_qkv", "attn_rel_bias", "attn_w_o", "ln_mix_g", "ln_mix_b", "ffn_w_up", "ffn_dw_w", "ffn_dw_b",
             "ffn_w_down", "ple_w_proj", "ple_w_gate", "ple_b_gate", "ln_ffn_g", "ln_ffn_b"]
    outs = [loss, grad_x[None]]
    for k in range(4):
        outs += [res[k][n] for n in order]
    return tuple(outs)
```

```python
import functools
import math

import jax
import jax.numpy as jnp
from jax import lax
from jax.experimental import pallas as pl
from jax.experimental.pallas import tpu as pltpu

F32 = jnp.float32
BF16 = jnp.bfloat16

N_DEV = 8
D_MODEL = 1024
D_POOL = 512
D_CONV = 512
POOL_WINDOWS = (2, 4, 8, 16)
POOL_GROUP = 128
CONV_KERNEL = 31
CHUNK = 64
HEAD_DIM = 64
N_HEADS = 16
LEFT_CHUNKS = 8
BAND = (LEFT_CHUNKS + 1) * CHUNK
MAX_REL = 256
D_FF = 2816
PLE_DIM = 256
ALPHA = 4.0 ** 0.25
LN_EPS = 1e-5
NEG_INF = -1e30
ADAM_LR, ADAM_B1, ADAM_B2, ADAM_EPS, ADAM_WD, ADAM_STEP = 0.001, 0.9, 0.999, 1e-08, 0.01, 10

Q_BLOCK = 4 * CHUNK
KV_PAD = LEFT_CHUNKS * CHUNK
KV_SPAN = KV_PAD + Q_BLOCK
CONV_HALO = 32
FFN_HALO = 16
SUB_ROWS, SUB_LANES = 64, 128
LANES = 1024
VMEM_LIMIT = 56 * 1024 * 1024


def _cparams(sem=None):
    return pltpu.CompilerParams(dimension_semantics=sem, vmem_limit_bytes=VMEM_LIMIT)


def _tile(dim, pref):
    if dim <= pref:
        return dim
    t = pref - pref % 128
    while t >= 128:
        if dim % t == 0:
            return t
        t -= 128
    return dim


def _sigmoid(x):
    return 1.0 / (1.0 + jnp.exp(-x))


def _bdot(a, b, dn=(((1,), (0,)), ((), ()))):
    return lax.dot_general(a.astype(BF16), b.astype(BF16), dn, preferred_element_type=F32)


WHOLE = (0, 1)
NT = (((1,), (1,)), ((), ()))
TN = (((0,), (0,)), ((), ()))


def _wgrad(a, b, *, tm=1024, tn=1024, tk=1024, piece=None, part=(0, 1), into=None, name):
    K, M = a.shape
    kb, N = b.shape
    assert K == kb, (a.shape, b.shape)
    tm, tn, tk = _tile(M, tm), _tile(N, tn), _tile(K, tk)
    nk = K // tk
    per = 1 if piece is None else tn // piece
    assert piece is None or tn == per * piece

    def body(a_ref, b_ref, *rest):
        o_ref, acc = rest[-2:]
        k = pl.program_id(2)

        @pl.when(k == 0)
        def _():
            acc[...] = jnp.zeros_like(acc)

        acc[...] += _bdot(a_ref[...], b_ref[...], TN)

        @pl.when(k == nk - 1)
        def _():
            if piece is None:
                o_ref[...] = acc[...].astype(BF16)
            else:
                for s in range(per):
                    o_ref[s] = acc[:, s * piece:(s + 1) * piece].astype(BF16)

    if piece is None:
        first = part[0] * (M // tm)
        out_shape = (part[1] * M, N)
        out_spec = pl.BlockSpec((tm, tn), lambda i, j, k: (first + i, j))
    else:
        out_shape, out_spec = (N // piece, M, piece), pl.BlockSpec((per, tm, piece), lambda i, j, k: (j, i, 0))
    others = [] if into is None else [into]
    return pl.pallas_call(
        body,
        out_shape=jax.ShapeDtypeStruct(out_shape, BF16),
        grid=(M // tm, N // tn, nk),
        in_specs=[pl.BlockSpec((tk, tm), lambda i, j, k: (k, i)), pl.BlockSpec((tk, tn), lambda i, j, k: (k, j))]
        + [pl.BlockSpec(memory_space=pl.ANY)] * len(others),
        out_specs=out_spec,
        input_output_aliases={2: 0} if others else {},
        scratch_shapes=[pltpu.VMEM((tm, tn), F32)],
        compiler_params=_cparams(("parallel", "parallel", "arbitrary")),
        name=name,
    )(a, b, *others)


def _mm_rows(pairs, *, add=None, add_scale=1.0, out_dtype=F32, tm=256, dep=None, ln_bwd=None, name):
    M = pairs[0][0].shape[0]
    n = len(pairs)
    has_add = add is not None
    w_rows = [w_.shape[0] // part[1] for _, w_, _, part in pairs]
    N = w_rows[0] if pairs[0][2] else pairs[0][1].shape[1]

    def body(*refs):
        acc = None
        for i, (_, _, tr, _) in enumerate(pairs):
            part = _bdot(refs[2 * i][...], refs[2 * i + 1][...], NT if tr else (((1,), (0,)), ((), ())))
            acc = part if acc is None else acc + part
        if has_add:
            acc = acc + add_scale * refs[2 * n][...]
        if ln_bwd is None:
            refs[-1][...] = acc.astype(out_dtype)
            return
        z_ref, g_ref = refs[2 * n + has_add], refs[2 * n + has_add + 1]
        dz_ref, dzb_ref, dg_ref, db_ref = refs[-4:]

        @pl.when(pl.program_id(0) == 0)
        def _():
            dg_ref[...] = jnp.zeros_like(dg_ref)
            db_ref[...] = jnp.zeros_like(db_ref)

        dg_acc = jnp.zeros((8, N), F32)
        db_acc = jnp.zeros((8, N), F32)
        for r0 in range(0, tm, LN_ROWS):
            rows = pl.ds(r0, LN_ROWS)
            do = acc[r0:r0 + LN_ROWS]
            dz, xh = _ln_bwd_rows(z_ref[rows, :], g_ref[...], do)
            dz_ref[rows, :] = dz
            dzb_ref[rows, :] = dz.astype(BF16)
            dg_acc = dg_acc + jnp.sum((do * xh).reshape(LN_ROWS // 8, 8, N), axis=0)
            db_acc = db_acc + jnp.sum(do.reshape(LN_ROWS // 8, 8, N), axis=0)
        dg_ref[...] += jnp.sum(dg_acc, axis=0, keepdims=True)
        db_ref[...] += jnp.sum(db_acc, axis=0, keepdims=True)

    in_specs, args = [], []
    for (a, w_, _, part), rows in zip(pairs, w_rows):
        in_specs += [pl.BlockSpec((tm, a.shape[1]), lambda i: (i, 0)),
                     pl.BlockSpec((rows, w_.shape[1]), functools.partial(lambda i, j: (j, 0), j=part[0]))]
        args += [a, w_]
    row = pl.BlockSpec((tm, N), lambda i: (i, 0))
    fix = pl.BlockSpec((1, N), lambda i: (0, 0))
    if has_add:
        in_specs.append(row)
        args.append(add)
    if ln_bwd is not None:
        in_specs += [row, fix]
        args += [ln_bwd[0], ln_bwd[1].reshape(1, N)]
    if dep is not None:
        in_specs.append(pl.BlockSpec(memory_space=pl.ANY))
        args.append(dep)
    if ln_bwd is None:
        out_shape, out_specs = jax.ShapeDtypeStruct((M, N), out_dtype), row
    else:
        out_shape = [jax.ShapeDtypeStruct((M, N), F32), jax.ShapeDtypeStruct((M, N), BF16),
                     jax.ShapeDtypeStruct((1, N), F32), jax.ShapeDtypeStruct((1, N), F32)]
        out_specs = [row, row, fix, fix]
    return pl.pallas_call(
        body,
        out_shape=out_shape,
        grid=(M // tm,),
        in_specs=in_specs,
        out_specs=out_specs,
        compiler_params=_cparams(("parallel",) if ln_bwd is None else ("arbitrary",)),
        name=name,
    )(*args)


def _ln_bwd_rows(zt, g, do):
    zc = zt - jnp.mean(zt, axis=-1, keepdims=True)
    rstd = lax.rsqrt(jnp.mean(zc * zc, axis=-1, keepdims=True) + LN_EPS)
    xh = zc * rstd
    dxh = do * g
    return rstd * (dxh - jnp.mean(dxh, axis=-1, keepdims=True) - xh * jnp.mean(dxh * xh, axis=-1, keepdims=True)), xh


def _layer_norm_rows(z, g, b):
    mu = jnp.mean(z, axis=-1, keepdims=True)
    zc = z - mu
    var = jnp.mean(zc * zc, axis=-1, keepdims=True)
    return zc * lax.rsqrt(var + LN_EPS) * g + b


def _proj_ln(res, a, w, ln_g, ln_b, *, ple=None, ts=256, name):
    S, D = res.shape
    ka = a.shape[1]
    has_ple = ple is not None
    row = lambda i: (i, 0)
    fix = lambda i: (0, 0)

    def body(*refs):
        if has_ple:
            (res_ref, a_ref, w_ref, g_ref, b_ref, wg_ref, bg_ref, p_ref, wp_ref, z_ref, r_ref, rb_ref, gate_ref,
             proj_ref, acc) = refs
        else:
            res_ref, a_ref, w_ref, g_ref, b_ref, z_ref, r_ref, rb_ref, acc = refs
        acc[...] = _bdot(a_ref[...], w_ref[...])
        if has_ple:
            gate_ref[...] = _bdot(res_ref[...], wg_ref[...])
            proj_ref[...] = _bdot(p_ref[...], wp_ref[...])
        for r0 in range(0, ts, LN_ROWS):
            rows = pl.ds(r0, LN_ROWS)
            z = ALPHA * res_ref[rows, :] + acc[rows, :]
            if has_ple:
                gate = _sigmoid(gate_ref[rows, :] + bg_ref[...])
                gate_ref[rows, :] = gate
                z = z + gate * proj_ref[rows, :]
            z_ref[rows, :] = z
            r = _layer_norm_rows(z, g_ref[...], b_ref[...])
            r_ref[rows, :] = r
            rb_ref[rows, :] = r.astype(BF16)

    in_specs = [pl.BlockSpec((ts, D), row), pl.BlockSpec((ts, ka), row), pl.BlockSpec((ka, D), fix),
                pl.BlockSpec((1, D), fix), pl.BlockSpec((1, D), fix)]
    args = [res, a, w, ln_g.reshape(1, D), ln_b.reshape(1, D)]
    out_dtypes = [F32, F32, BF16]
    if has_ple:
        wg, bg, p, wp = ple
        in_specs += [pl.BlockSpec((D, D), fix), pl.BlockSpec((1, D), fix), pl.BlockSpec((ts, PLE_DIM), row),
                     pl.BlockSpec((PLE_DIM, D), fix)]
        args += [wg, bg.reshape(1, D), p, wp]
        out_dtypes += [F32, F32]
    return pl.pallas_call(
        body,
        out_shape=[jax.ShapeDtypeStruct((S, D), dt) for dt in out_dtypes],
        grid=(S // ts,),
        in_specs=in_specs,
        out_specs=[pl.BlockSpec((ts, D), row)] * len(out_dtypes),
        scratch_shapes=[pltpu.VMEM((ts, D), F32)],
        compiler_params=_cparams(("parallel",)),
        name=name,
    )(*args)


CONV_ROWS = 32
LN_ROWS = 16


def _shifted_copies(src, dst, rows):
    for c0 in range(0, src.shape[1], SUB_LANES):
        ln = pl.ds(c0, SUB_LANES)
        for r0 in range(0, rows, SUB_ROWS):
            rc = min(SUB_ROWS, rows - r0)
            for b, shifted in enumerate(_rows_ahead(src, r0, rc, ln, range(1, 8))):
                dst[b, pl.ds(r0, rc), ln] = shifted


def _rows_at(src, copies, off, n, ln):
    b = off % 8
    return src[pl.ds(off, n), ln] if b == 0 else copies[b - 1, pl.ds(off - b, n), ln]


def _conv31(stg, gsh, cw_ref, cb_ref, out, rows, first_off):
    for c0 in range(0, D_CONV, SUB_LANES):
        ln = pl.ds(c0, SUB_LANES)
        for r0 in range(0, rows, CONV_ROWS):
            acc = jnp.zeros((CONV_ROWS, SUB_LANES), F32) + cb_ref[:, ln]
            for k in range(CONV_KERNEL):
                acc = acc + cw_ref[k:k + 1, ln] * _rows_at(stg, gsh, first_off + k + r0, CONV_ROWS, ln)
            out[pl.ds(r0, CONV_ROWS), ln] = acc


def _mixer_fwd(u, pool_w, pool_scale, conv_w, conv_b, cln_g, cln_b, *, ts=256):
    S = u.shape[0]
    hb = CONV_HALO
    nh = ts // hb

    def body(u_ref, uh_ref, pw_ref, ps_ref, cw_ref, cb_ref, g_ref, b_ref, y_ref, d_ref, hcs, sta, stg, gsh):
        i = pl.program_id(0)
        first = i == 0
        sta[pl.ds(0, hb), :] = jnp.where(first, 0.0, uh_ref[:, 0:D_POOL])
        sta[pl.ds(hb, ts), :] = u_ref[:, 0:D_POOL]
        glu_h = uh_ref[:, D_POOL:D_POOL + D_CONV] * _sigmoid(uh_ref[:, D_POOL + D_CONV:])
        stg[pl.ds(0, hb), :] = jnp.where(first, 0.0, glu_h)
        stg[pl.ds(hb, ts), :] = u_ref[:, D_POOL:D_POOL + D_CONV] * _sigmoid(u_ref[:, D_POOL + D_CONV:])

        for g, w in enumerate(POOL_WINDOWS):
            lanes = pl.ds(g * POOL_GROUP, POOL_GROUP)
            for r0 in range(0, ts, SUB_ROWS):
                s = None
                for q in range(0, w, 8):
                    for tap in _rows_back(sta, hb + r0 - q, SUB_ROWS, lanes, range(min(8, w - q))):
                        s = tap if s is None else s + tap
                pos = (i * ts + r0 + lax.broadcasted_iota(jnp.int32, (SUB_ROWS, 1), 0) + 1).astype(F32)
                d_g = s / jnp.minimum(pos, float(w)) - sta[pl.ds(hb + r0, SUB_ROWS), lanes]
                d_ref[pl.ds(r0, SUB_ROWS), lanes] = d_g.astype(BF16)
            y_ref[:, lanes] = (_bdot(d_ref[:, lanes], pw_ref[g]) * ps_ref[:, lanes]).astype(BF16)

        _shifted_copies(stg, gsh, hb + ts - 8)
        _conv31(stg, gsh, cw_ref, cb_ref, hcs, ts, hb - (CONV_KERNEL - 1))
        for r0 in range(0, ts, LN_ROWS):
            rows = pl.ds(r0, LN_ROWS)
            ln = _layer_norm_rows(hcs[rows, :], g_ref[...], b_ref[...])
            y_ref[rows, D_POOL:] = (ln * _sigmoid(ln)).astype(BF16)

    fix2 = lambda i: (0, 0)
    return pl.pallas_call(
        body,
        out_shape=[jax.ShapeDtypeStruct((S, D_MODEL), BF16), jax.ShapeDtypeStruct((S, D_POOL), BF16),
                   jax.ShapeDtypeStruct((S, D_CONV), F32)],
        grid=(S // ts,),
        in_specs=[pl.BlockSpec((ts, 3 * D_POOL), lambda i: (i, 0)),
                  pl.BlockSpec((hb, 3 * D_POOL), lambda i: (jnp.maximum(i * nh - 1, 0), 0)),
                  pl.BlockSpec((4, POOL_GROUP, POOL_GROUP), lambda i: (0, 0, 0)),
                  pl.BlockSpec((1, D_POOL), fix2), pl.BlockSpec((CONV_KERNEL, D_CONV), fix2),
                  pl.BlockSpec((1, D_CONV), fix2), pl.BlockSpec((1, D_CONV), fix2), pl.BlockSpec((1, D_CONV), fix2)],
        out_specs=[pl.BlockSpec((ts, D_MODEL), lambda i: (i, 0)), pl.BlockSpec((ts, D_POOL), lambda i: (i, 0)),
                   pl.BlockSpec((ts, D_CONV), lambda i: (i, 0))],
        scratch_shapes=[pltpu.VMEM((hb + ts, D_POOL), F32), pltpu.VMEM((hb + ts, D_CONV), F32),
                        pltpu.VMEM((7, hb + ts - 8, D_CONV), F32)],
        compiler_params=_cparams(("parallel",)),
        name="mixer_fwd",
    )(u, u, pool_w, pool_scale.reshape(1, D_POOL), conv_w, conv_b.reshape(1, D_CONV), cln_g.reshape(1, D_CONV),
      cln_b.reshape(1, D_CONV))


def _mixer_bwd(u, d, hc, dycat, pool_w, pool_scale, conv_w, cln_g, cln_b, *, ts=256):
    S = u.shape[0]
    hb = CONV_HALO
    nh = ts // hb
    n = S // ts
    te = ts + hb
    K = CONV_KERNEL

    def body(u_ref, up_ref, un_ref, d_ref, hc_ref, hcn_ref, dy_ref, dyn_ref, pw_ref, ps_ref, cw_ref, g_ref, b_ref,
             du_ref, dpw_ref, dps_ref, dcw_ref, dcb_ref, dg_ref, db_ref, stg, std, sth, gsh, hsh):
        i = pl.program_id(0)
        first = i == 0
        last = i == n - 1

        @pl.when(first)
        def _():
            dpw_ref[...] = jnp.zeros_like(dpw_ref)
            dps_ref[...] = jnp.zeros_like(dps_ref)
            dcw_ref[...] = jnp.zeros_like(dcw_ref)
            dcb_ref[...] = jnp.zeros_like(dcb_ref)
            dg_ref[...] = jnp.zeros_like(dg_ref)
            db_ref[...] = jnp.zeros_like(db_ref)

        pos_e = (i * ts + lax.broadcasted_iota(jnp.int32, (te, 1), 0) + 1).astype(F32)
        dya = dy_ref[:, 0:D_POOL]
        dya_n = jnp.where(last, 0.0, dyn_ref[:, 0:D_POOL])
        for g, w in enumerate(POOL_WINDOWS):
            lanes = pl.ds(g * POOL_GROUP, POOL_GROUP)
            sl = slice(g * POOL_GROUP, (g + 1) * POOL_GROUP)
            pw = pw_ref[g]
            scale = ps_ref[:, lanes]
            d_g = d_ref[:, lanes]
            pre = _bdot(d_g, pw)
            dps_ref[:, lanes] += jnp.sum(dya[:, sl] * pre, axis=0, keepdims=True)
            dys = dya[:, sl] * scale
            dpw_ref[g] += _bdot(d_g, dys, TN)
            dys_e = jnp.concatenate([dys, dya_n[:, sl] * scale], axis=0)
            dd = _bdot(dys_e, pw, NT)
            std[:, lanes] = dd / jnp.minimum(pos_e, float(w))
            for r0 in range(0, ts, SUB_ROWS):
                da = -dd[r0:r0 + SUB_ROWS]
                for q in range(0, w, 8):
                    for tap in _rows_ahead(std, r0 + q, SUB_ROWS, lanes, range(min(8, w - q))):
                        da = da + tap
                du_ref[pl.ds(r0, SUB_ROWS), lanes] = da.astype(BF16)

        glu_p = up_ref[:, D_POOL:D_POOL + D_CONV] * _sigmoid(up_ref[:, D_POOL + D_CONV:])
        stg[pl.ds(0, hb), :] = jnp.where(first, 0.0, glu_p)
        bv = u_ref[:, D_POOL:D_POOL + D_CONV]
        sg = _sigmoid(u_ref[:, D_POOL + D_CONV:])
        stg[pl.ds(hb, ts), :] = bv * sg
        glu_n = un_ref[:, D_POOL:D_POOL + D_CONV] * _sigmoid(un_ref[:, D_POOL + D_CONV:])
        stg[pl.ds(hb + ts, hb), :] = jnp.where(last, 0.0, glu_n)
        _shifted_copies(stg, gsh, hb + te - 8)

        sums = [jnp.zeros((8, D_CONV), F32) for _ in range(3)]
        for r0 in range(0, te, LN_ROWS):
            rows = pl.ds(r0, LN_ROWS)
            hc = hc_ref[rows, :] if r0 < ts else hcn_ref[pl.ds(r0 - ts, LN_ROWS), :]
            hcc = hc - jnp.mean(hc, axis=-1, keepdims=True)
            rstd = lax.rsqrt(jnp.mean(hcc * hcc, axis=-1, keepdims=True) + LN_EPS)
            xh = hcc * rstd
            ln = xh * g_ref[...] + b_ref[...]
            sl_ = _sigmoid(ln)
            if r0 < ts:
                dyb = dy_ref[rows, D_POOL:]
            else:
                dyb = jnp.where(last, 0.0, dyn_ref[pl.ds(r0 - ts, LN_ROWS), D_POOL:])
            dln = dyb * (sl_ * (1.0 + ln * (1.0 - sl_)))
            dxh = dln * g_ref[...]
            dhc = rstd * (dxh - jnp.mean(dxh, axis=-1, keepdims=True)
                          - xh * jnp.mean(dxh * xh, axis=-1, keepdims=True))
            sth[rows, :] = dhc
            if r0 < ts:
                for n_, term in enumerate((dln * xh, dln, dhc)):
                    sums[n_] = sums[n_] + jnp.sum(term.reshape(LN_ROWS // 8, 8, D_CONV), axis=0)
        dg_ref[...] += jnp.sum(sums[0], axis=0, keepdims=True)
        db_ref[...] += jnp.sum(sums[1], axis=0, keepdims=True)
        dcb_ref[...] += jnp.sum(sums[2], axis=0, keepdims=True)

        _shifted_copies(sth, hsh, te - 8)
        for c0 in range(0, D_CONV, SUB_LANES):
            ln_ = pl.ds(c0, SUB_LANES)
            for r0 in range(0, ts, CONV_ROWS):
                rows = pl.ds(r0, CONV_ROWS)
                dglu = jnp.zeros((CONV_ROWS, SUB_LANES), F32)
                for k in range(K):
                    dglu = dglu + cw_ref[k:k + 1, ln_] * _rows_at(sth, hsh, K - 1 - k + r0, CONV_ROWS, ln_)
                bv = u_ref[rows, pl.ds(D_POOL + c0, SUB_LANES)]
                sg = _sigmoid(u_ref[rows, pl.ds(D_POOL + D_CONV + c0, SUB_LANES)])
                du_ref[rows, pl.ds(D_POOL + c0, SUB_LANES)] = (dglu * sg).astype(BF16)
                du_ref[rows, pl.ds(D_POOL + D_CONV + c0, SUB_LANES)] = (dglu * bv * sg * (1.0 - sg)).astype(BF16)
            for k in range(K):
                tap = jnp.zeros((8, SUB_LANES), F32)
                for r0 in range(0, ts, CONV_ROWS):
                    prod = sth[pl.ds(r0, CONV_ROWS), ln_] * _rows_at(stg, gsh, hb - (K - 1) + k + r0, CONV_ROWS, ln_)
                    tap = tap + jnp.sum(prod.reshape(CONV_ROWS // 8, 8, SUB_LANES), axis=0)
                dcw_ref[k:k + 1, ln_] += jnp.sum(tap, axis=0, keepdims=True)

    fix2 = lambda i: (0, 0)
    prev = lambda i: (jnp.maximum(i * nh - 1, 0), 0)
    nxt = lambda i: (jnp.minimum((i + 1) * nh, S // hb - 1), 0)
    return pl.pallas_call(
        body,
        out_shape=[jax.ShapeDtypeStruct((S, 3 * D_POOL), BF16),
                   jax.ShapeDtypeStruct((4, POOL_GROUP, POOL_GROUP), F32),
                   jax.ShapeDtypeStruct((1, D_POOL), F32),
                   jax.ShapeDtypeStruct((K, D_CONV), F32),
                   jax.ShapeDtypeStruct((1, D_CONV), F32),
                   jax.ShapeDtypeStruct((1, D_CONV), F32),
                   jax.ShapeDtypeStruct((1, D_CONV), F32)],
        grid=(n,),
        in_specs=[pl.BlockSpec((ts, 3 * D_POOL), lambda i: (i, 0)),
                  pl.BlockSpec((hb, 3 * D_POOL), prev),
                  pl.BlockSpec((hb, 3 * D_POOL), nxt),
                  pl.BlockSpec((ts, D_POOL), lambda i: (i, 0)),
                  pl.BlockSpec((ts, D_CONV), lambda i: (i, 0)),
                  pl.BlockSpec((hb, D_CONV), nxt),
                  pl.BlockSpec((ts, D_MODEL), lambda i: (i, 0)),
                  pl.BlockSpec((hb, D_MODEL), nxt),
                  pl.BlockSpec((4, POOL_GROUP, POOL_GROUP), lambda i: (0, 0, 0)),
                  pl.BlockSpec((1, D_POOL), fix2), pl.BlockSpec((K, D_CONV), fix2),
                  pl.BlockSpec((1, D_CONV), fix2), pl.BlockSpec((1, D_CONV), fix2)],
        out_specs=[pl.BlockSpec((ts, 3 * D_POOL), lambda i: (i, 0)),
                   pl.BlockSpec((4, POOL_GROUP, POOL_GROUP), lambda i: (0, 0, 0)),
                   pl.BlockSpec((1, D_POOL), fix2), pl.BlockSpec((K, D_CONV), fix2),
                   pl.BlockSpec((1, D_CONV), fix2), pl.BlockSpec((1, D_CONV), fix2), pl.BlockSpec((1, D_CONV), fix2)],
        scratch_shapes=[pltpu.VMEM((hb + ts + hb, D_CONV), F32), pltpu.VMEM((te, D_POOL), F32),
                        pltpu.VMEM((te, D_CONV), F32), pltpu.VMEM((7, hb + te - 8, D_CONV), F32),
                        pltpu.VMEM((7, te - 8, D_CONV), F32)],
        compiler_params=_cparams(("arbitrary",)),
        name="mixer_bwd",
    )(u, u, u, d, hc, hc, dycat, dycat, pool_w, pool_scale.reshape(1, D_POOL), conv_w, cln_g.reshape(1, D_CONV),
      cln_b.reshape(1, D_CONV))


_GELU_C = math.sqrt(2.0 / math.pi)


def _gelu_parts(x):
    inner = _GELU_C * (x + 0.044715 * x * x * x)
    th = jnp.tanh(inner)
    ge = 0.5 * x * (1.0 + th)
    dge = 0.5 * (1.0 + th) + 0.5 * x * (1.0 - th * th) * (_GELU_C * (1.0 + 3.0 * 0.044715 * x * x))
    return ge, dge


def _rows_back(ref, r, n, ln, shifts):
    ext = ref[pl.ds(r - 8, n + 8), ln]
    return [(pltpu.roll(ext, s, 0) if s else ext)[8:] for s in shifts]


def _rows_ahead(ref, r, n, ln, shifts):
    ext = ref[pl.ds(r, n + 8), ln]
    return [(pltpu.roll(ext, n + 8 - s, 0) if s else ext)[:n] for s in shifts]


def _ffn_act_fwd(gate, val, dw_w, dw_b, *, ts=256, tc=1408, name):
    S, F = gate.shape
    hb = FFN_HALO
    nh = ts // hb
    tc = _tile(F, tc)

    def body(g_ref, gh_ref, v_ref, w_ref, b_ref, h_ref, st):
        i = pl.program_id(0)
        st[pl.ds(0, hb), :] = jnp.where(i == 0, 0.0, gh_ref[...].astype(F32))
        st[pl.ds(hb, ts), :] = g_ref[...].astype(F32)
        for c0 in range(0, tc, SUB_LANES):
            ln = pl.ds(c0, SUB_LANES)
            w0, w1, w2, b = w_ref[0:1, ln], w_ref[1:2, ln], w_ref[2:3, ln], b_ref[:, ln]
            for r0 in range(0, ts, SUB_ROWS):
                taps = _rows_back(st, hb + r0, SUB_ROWS, ln, (2, 1, 0))
                gc = b + w0 * taps[0] + w1 * taps[1] + w2 * taps[2]
                ge, _ = _gelu_parts(gc)
                rows = pl.ds(r0, SUB_ROWS)
                h_ref[rows, ln] = (ge * v_ref[rows, ln].astype(F32)).astype(BF16)

    return pl.pallas_call(
        body,
        out_shape=jax.ShapeDtypeStruct((S, F), BF16),
        grid=(S // ts, F // tc),
        in_specs=[pl.BlockSpec((ts, tc), lambda i, j: (i, j)),
                  pl.BlockSpec((hb, tc), lambda i, j: (jnp.maximum(i * nh - 1, 0), j)),
                  pl.BlockSpec((ts, tc), lambda i, j: (i, j)),
                  pl.BlockSpec((3, tc), lambda i, j: (0, j)),
                  pl.BlockSpec((1, tc), lambda i, j: (0, j))],
        out_specs=pl.BlockSpec((ts, tc), lambda i, j: (i, j)),
        scratch_shapes=[pltpu.VMEM((hb + ts, tc), F32)],
        compiler_params=_cparams(("parallel", "parallel")),
        name=name,
    )(gate, gate, val, dw_w, dw_b.reshape(1, F))


def _ffn_act_bwd(gate, val, dh, dw_w, dw_b, *, ts=256, tc=1408, name):
    S, F = gate.shape
    hb = FFN_HALO
    nh = ts // hb
    n = S // ts
    te = ts + hb
    tc = _tile(F, tc)

    def body(g_ref, gp_ref, gn_ref, v_ref, vn_ref, dh_ref, dhn_ref, w_ref, b_ref,
             dg_ref, dv_ref, dw_ref, db_ref, st, sd):
        i = pl.program_id(1)
        first = i == 0
        last = i == n - 1

        @pl.when(first)
        def _():
            dw_ref[...] = jnp.zeros_like(dw_ref)
            db_ref[...] = jnp.zeros_like(db_ref)

        st[pl.ds(0, hb), :] = jnp.where(first, 0.0, gp_ref[...].astype(F32))
        st[pl.ds(hb, ts), :] = g_ref[...].astype(F32)
        st[pl.ds(hb + ts, hb), :] = jnp.where(last, 0.0, gn_ref[...].astype(F32))
        for c0 in range(0, tc, SUB_LANES):
            ln = pl.ds(c0, SUB_LANES)
            w0, w1, w2, b = w_ref[0:1, ln], w_ref[1:2, ln], w_ref[2:3, ln], b_ref[:, ln]
            db_acc = jnp.zeros((8, SUB_LANES), F32)
            dw_acc = [jnp.zeros((8, SUB_LANES), F32) for _ in range(3)]
            for r0 in range(0, te, SUB_ROWS):
                rc = min(SUB_ROWS, te - r0)
                taps = _rows_back(st, hb + r0, rc, ln, (2, 1, 0))
                gc = b + w0 * taps[0] + w1 * taps[1] + w2 * taps[2]
                ge, dge = _gelu_parts(gc)
                if r0 < ts:
                    rows = pl.ds(r0, rc)
                    val, dh = v_ref[rows, ln].astype(F32), dh_ref[rows, ln].astype(F32)
                else:
                    val = jnp.where(last, 0.0, vn_ref[:, ln].astype(F32)[0:rc])
                    dh = jnp.where(last, 0.0, dhn_ref[:, ln].astype(F32)[0:rc])
                dgc = dh * val * dge
                sd[pl.ds(r0, rc), ln] = dgc
                if r0 < ts:
                    dv_ref[rows, ln] = (dh * ge).astype(BF16)
                    db_acc = db_acc + jnp.sum(dgc.reshape(rc // 8, 8, SUB_LANES), axis=0)
                    for k in range(3):
                        dw_acc[k] = dw_acc[k] + jnp.sum((dgc * taps[k]).reshape(rc // 8, 8, SUB_LANES), axis=0)
            db_ref[:, ln] += jnp.sum(db_acc, axis=0, keepdims=True)
            for k in range(3):
                dw_ref[k:k + 1, ln] += jnp.sum(dw_acc[k], axis=0, keepdims=True)
            for r0 in range(0, ts, SUB_ROWS):
                ahead = _rows_ahead(sd, r0, SUB_ROWS, ln, (2, 1, 0))
                dg_ref[pl.ds(r0, SUB_ROWS), ln] = (w0 * ahead[0] + w1 * ahead[1] + w2 * ahead[2]).astype(BF16)

    cur = lambda j, i: (i, j)
    prev = lambda j, i: (jnp.maximum(i * nh - 1, 0), j)
    nxt = lambda j, i: (jnp.minimum((i + 1) * nh, S // hb - 1), j)
    return pl.pallas_call(
        body,
        out_shape=[jax.ShapeDtypeStruct((S, F), BF16), jax.ShapeDtypeStruct((S, F), BF16),
                   jax.ShapeDtypeStruct((3, F), F32), jax.ShapeDtypeStruct((1, F), F32)],
        grid=(F // tc, n),
        in_specs=[pl.BlockSpec((ts, tc), cur), pl.BlockSpec((hb, tc), prev), pl.BlockSpec((hb, tc), nxt),
                  pl.BlockSpec((ts, tc), cur), pl.BlockSpec((hb, tc), nxt),
                  pl.BlockSpec((ts, tc), cur), pl.BlockSpec((hb, tc), nxt),
                  pl.BlockSpec((3, tc), lambda j, i: (0, j)), pl.BlockSpec((1, tc), lambda j, i: (0, j))],
        out_specs=[pl.BlockSpec((ts, tc), cur), pl.BlockSpec((ts, tc), cur),
                   pl.BlockSpec((3, tc), lambda j, i: (0, j)), pl.BlockSpec((1, tc), lambda j, i: (0, j))],
        scratch_shapes=[pltpu.VMEM((hb + ts + hb, tc), F32), pltpu.VMEM((te, tc), F32)],
        compiler_params=_cparams(("parallel", "arbitrary")),
        name=name,
    )(gate, gate, gate, val, val, dh, dh, dw_w, dw_b.reshape(1, F))


def _loss_ln_bwd(z, ln_g, ln_b, target, *, ts=256, name):
    S, D = z.shape

    def body(z_ref, g_ref, b_ref, t_ref, dz_ref, dzb_ref, dg_ref, db_ref, loss_ref):
        i = pl.program_id(0)

        @pl.when(i == 0)
        def _():
            dg_ref[...] = jnp.zeros_like(dg_ref)
            db_ref[...] = jnp.zeros_like(db_ref)
            loss_ref[...] = jnp.zeros_like(loss_ref)

        dg_acc = jnp.zeros((8, D), F32)
        db_acc = jnp.zeros((8, D), F32)
        loss_acc = jnp.zeros((1, 1), F32)
        for r0 in range(0, ts, LN_ROWS):
            rows = pl.ds(r0, LN_ROWS)
            zt = z_ref[rows, :]
            err = _layer_norm_rows(zt, g_ref[...], b_ref[...]) - t_ref[rows, :]
            loss_acc = loss_acc + 0.5 * jnp.sum(jnp.mean(err * err, axis=-1, keepdims=True), keepdims=True)
            do = err * (1.0 / D)
            dz, xh = _ln_bwd_rows(zt, g_ref[...], do)
            dg_acc = dg_acc + jnp.sum((do * xh).reshape(LN_ROWS // 8, 8, D), axis=0)
            db_acc = db_acc + jnp.sum(do.reshape(LN_ROWS // 8, 8, D), axis=0)
            dz_ref[rows, :] = dz
            dzb_ref[rows, :] = dz.astype(BF16)
        dg_ref[...] += jnp.sum(dg_acc, axis=0, keepdims=True)
        db_ref[...] += jnp.sum(db_acc, axis=0, keepdims=True)
        loss_ref[...] += loss_acc

    row = lambda i: (i, 0)
    fix = lambda i: (0, 0)
    return pl.pallas_call(
        body,
        out_shape=[jax.ShapeDtypeStruct((S, D), F32), jax.ShapeDtypeStruct((S, D), BF16),
                   jax.ShapeDtypeStruct((1, D), F32), jax.ShapeDtypeStruct((1, D), F32),
                   jax.ShapeDtypeStruct((8, 128), F32)],
        grid=(S // ts,),
        in_specs=[pl.BlockSpec((ts, D), row), pl.BlockSpec((1, D), fix), pl.BlockSpec((1, D), fix),
                  pl.BlockSpec((ts, D), row)],
        out_specs=[pl.BlockSpec((ts, D), row), pl.BlockSpec((ts, D), row), pl.BlockSpec((1, D), fix),
                   pl.BlockSpec((1, D), fix), pl.BlockSpec((8, 128), fix)],
        compiler_params=_cparams(("arbitrary",)),
        name=name,
    )(z, ln_g.reshape(1, D), ln_b.reshape(1, D), target)


def _ple_bwd(dz, gate, proj, *, ts=256, name):
    S, D = dz.shape

    def body(dz_ref, g_ref, p_ref, ds_ref, dp_ref, db_ref):
        @pl.when(pl.program_id(0) == 0)
        def _():
            db_ref[...] = jnp.zeros_like(db_ref)

        db_acc = jnp.zeros((8, D), F32)
        for r0 in range(0, ts, LN_ROWS):
            rows = pl.ds(r0, LN_ROWS)
            dzt = dz_ref[rows, :]
            g = g_ref[rows, :]
            ds = dzt * p_ref[rows, :] * g * (1.0 - g)
            ds_ref[rows, :] = ds.astype(BF16)
            dp_ref[rows, :] = (dzt * g).astype(BF16)
            db_acc = db_acc + jnp.sum(ds.reshape(LN_ROWS // 8, 8, D), axis=0)
        db_ref[...] += jnp.sum(db_acc, axis=0, keepdims=True)

    row = lambda i: (i, 0)
    return pl.pallas_call(
        body,
        out_shape=[jax.ShapeDtypeStruct((S, D), BF16), jax.ShapeDtypeStruct((S, D), BF16),
                   jax.ShapeDtypeStruct((1, D), F32)],
        grid=(S // ts,),
        in_specs=[pl.BlockSpec((ts, D), row)] * 3,
        out_specs=[pl.BlockSpec((ts, D), row), pl.BlockSpec((ts, D), row), pl.BlockSpec((1, D), lambda i: (0, 0))],
        compiler_params=_cparams(("arbitrary",)),
        name=name,
    )(dz, gate, proj)


HEAD_PAIR = 2 * HEAD_DIM


ATT_ROWS = 32
ATT_SCALE = HEAD_DIM ** -0.5


def _softmax_piece(scores, bias, qb, front):
    s = scores + bias
    if front:
        kpos = qb * Q_BLOCK + lax.broadcasted_iota(jnp.int32, (1, KV_SPAN), 1)
        s = jnp.where(kpos >= KV_PAD, s, NEG_INF)
    e = jnp.exp(s - jnp.max(s, axis=-1, keepdims=True))
    return e, 1.0 / jnp.sum(e, axis=-1, keepdims=True)


def _front_or_not(qb, step):
    @pl.when(qb < KV_PAD // Q_BLOCK)
    def _():
        step(True)

    @pl.when(qb >= KV_PAD // Q_BLOCK)
    def _():
        step(False)


def _pad_keys(qb, k_ref, v_ref, kp, vp):
    @pl.when(qb == 0)
    def _():
        kp[pl.ds(0, KV_PAD), :] = jnp.zeros((KV_PAD, HEAD_PAIR), BF16)
        vp[pl.ds(0, KV_PAD), :] = jnp.zeros((KV_PAD, HEAD_PAIR), BF16)
        kp[pl.ds(KV_PAD, k_ref.shape[0]), :] = k_ref[...]
        vp[pl.ds(KV_PAD, v_ref.shape[0]), :] = v_ref[...]


def _attn_fwd(qkv, bias):
    S = qkv.shape[0]
    nhp = N_HEADS // 2

    def body(q_ref, k_ref, v_ref, b_ref, o_ref, kp, vp, p_scr):
        qb = pl.program_id(1)
        _pad_keys(qb, k_ref, v_ref, kp, vp)
        span = pl.ds(pl.multiple_of(qb * Q_BLOCK, Q_BLOCK), KV_SPAN)
        kc, vc = kp[span, :], vp[span, :]
        qt = q_ref[...] * ATT_SCALE
        first = lax.broadcasted_iota(jnp.int32, (1, HEAD_PAIR), 1) < HEAD_DIM
        def step(front):
            scores = [_bdot(jnp.where(first if j == 0 else ~first, qt, jnp.zeros_like(qt)), kc, NT)
                      for j in range(2)]
            outs = []
            for j in range(2):
                inv = []
                for r0 in range(0, Q_BLOCK, ATT_ROWS):
                    rows = pl.ds(r0, ATT_ROWS)
                    e, inv_piece = _softmax_piece(scores[j][r0:r0 + ATT_ROWS], b_ref[j, rows, :], qb, front)
                    p_scr[j, rows, :] = e.astype(BF16)
                    inv.append(inv_piece)
                outs.append(_bdot(p_scr[j], vc) * jnp.concatenate(inv, axis=0))
            o_ref[...] = jnp.where(first, outs[0], outs[1]).astype(BF16)

        _front_or_not(qb, step)

    return pl.pallas_call(
        body,
        out_shape=jax.ShapeDtypeStruct((S, D_MODEL), BF16),
        grid=(nhp, S // Q_BLOCK),
        in_specs=[pl.BlockSpec((Q_BLOCK, HEAD_PAIR), lambda h, i: (i, h)),
                  pl.BlockSpec((S, HEAD_PAIR), lambda h, i: (0, nhp + h)),
                  pl.BlockSpec((S, HEAD_PAIR), lambda h, i: (0, 2 * nhp + h)),
                  pl.BlockSpec((2, Q_BLOCK, KV_SPAN), lambda h, i: (h, 0, 0))],
        out_specs=pl.BlockSpec((Q_BLOCK, HEAD_PAIR), lambda h, i: (i, h)),
        scratch_shapes=[pltpu.VMEM((KV_PAD + S, HEAD_PAIR), BF16), pltpu.VMEM((KV_PAD + S, HEAD_PAIR), BF16),
                        pltpu.VMEM((2, Q_BLOCK, KV_SPAN), BF16)],
        compiler_params=_cparams(("parallel", "arbitrary")),
        name="attn_fwd",
    )(qkv, qkv, qkv, bias)


def _attn_bwd(qkv, bias, do):
    S = qkv.shape[0]
    nhp = N_HEADS // 2
    nq = S // Q_BLOCK
    scale = HEAD_DIM ** -0.5

    def body(q_ref, k_ref, v_ref, b_ref, do_ref, dq_ref, dk_ref, dv_ref, db_ref, kp, vp, dka, dva,
             p_scr, ds_scr):
        qb = pl.program_id(1)
        _pad_keys(qb, k_ref, v_ref, kp, vp)

        @pl.when(qb == 0)
        def _():
            dka[...] = jnp.zeros_like(dka)
            dva[...] = jnp.zeros_like(dva)
            db_ref[...] = jnp.zeros_like(db_ref)

        span = pl.ds(pl.multiple_of(qb * Q_BLOCK, Q_BLOCK), KV_SPAN)
        kc, vc = kp[span, :], vp[span, :]
        qt, dot = q_ref[...] * ATT_SCALE, do_ref[...]
        first = lax.broadcasted_iota(jnp.int32, (1, HEAD_PAIR), 1) < HEAD_DIM
        qs = [jnp.where(first if j == 0 else ~first, qt, jnp.zeros_like(qt)) for j in range(2)]
        dos = [jnp.where(first if j == 0 else ~first, dot, jnp.zeros_like(dot)) for j in range(2)]

        def step(front):
            scores = [_bdot(qs[j], kc, NT) for j in range(2)]
            dps = [_bdot(dos[j], vc, NT) for j in range(2)]
            dqs = []
            for j in range(2):
                for r0 in range(0, Q_BLOCK, ATT_ROWS):
                    rows = pl.ds(r0, ATT_ROWS)
                    e, inv = _softmax_piece(scores[j][r0:r0 + ATT_ROWS], b_ref[j, rows, :], qb, front)
                    p = e * inv
                    dp = dps[j][r0:r0 + ATT_ROWS]
                    ds = p * (dp - jnp.sum(p * dp, axis=-1, keepdims=True))
                    db_ref[j, rows, :] += ds
                    p_scr[j, rows, :] = p.astype(BF16)
                    ds_scr[j, rows, :] = ds.astype(BF16)
                dva[span, :] += _bdot(p_scr[j], dos[j], TN)
                dqs.append(_bdot(ds_scr[j], kc))
                dka[span, :] += _bdot(ds_scr[j], qs[j], TN)
            dq_ref[...] = (scale * jnp.where(first, dqs[0], dqs[1])).astype(BF16)

        _front_or_not(qb, step)

        @pl.when(qb == nq - 1)
        def _():
            dk_ref[...] = dka[pl.ds(KV_PAD, S), :].astype(BF16)
            dv_ref[...] = dva[pl.ds(KV_PAD, S), :].astype(BF16)

    blk = pl.BlockSpec((Q_BLOCK, HEAD_PAIR), lambda h, i: (i, h))
    col = pl.BlockSpec((S, HEAD_PAIR), lambda h, i: (0, h))
    bsp = pl.BlockSpec((2, Q_BLOCK, KV_SPAN), lambda h, i: (h, 0, 0))
    return pl.pallas_call(
        body,
        out_shape=[jax.ShapeDtypeStruct((S, D_MODEL), BF16)] * 3
        + [jax.ShapeDtypeStruct((N_HEADS, Q_BLOCK, KV_SPAN), F32)],
        grid=(nhp, nq),
        in_specs=[blk, pl.BlockSpec((S, HEAD_PAIR), lambda h, i: (0, nhp + h)),
                  pl.BlockSpec((S, HEAD_PAIR), lambda h, i: (0, 2 * nhp + h)), bsp, blk],
        out_specs=[blk, col, col, bsp],
        scratch_shapes=[pltpu.VMEM((KV_PAD + S, HEAD_PAIR), BF16), pltpu.VMEM((KV_PAD + S, HEAD_PAIR), BF16),
                        pltpu.VMEM((KV_PAD + S, HEAD_PAIR), F32), pltpu.VMEM((KV_PAD + S, HEAD_PAIR), F32),
                        pltpu.VMEM((2, Q_BLOCK, KV_SPAN), BF16), pltpu.VMEM((2, Q_BLOCK, KV_SPAN), BF16)],
        compiler_params=_cparams(("parallel", "arbitrary")),
        name="attn_bwd",
    )(qkv, qkv, qkv, bias, do)


N_DIST = BAND + CHUNK - 1
N_FAR = KV_PAD + CHUNK - MAX_REL


def _shear_rows(x, towards_right):
    row = lax.broadcasted_iota(jnp.int32, (Q_BLOCK, 1), 0)
    for bit in range(Q_BLOCK.bit_length() - 1):
        step = 1 << bit
        x = jnp.where((row & step) != 0, pltpu.roll(x, step if towards_right else KV_SPAN - step, 1), x)
    return x


def _bias_blocks(rel_bias):
    H = rel_bias.shape[0]
    e = jnp.concatenate([jnp.broadcast_to(rel_bias[:, 2 * MAX_REL:], (H, N_FAR)),
                         jnp.flip(rel_bias[:, 2 * MAX_REL - (N_DIST - N_FAR):2 * MAX_REL], axis=1),
                         jnp.zeros((H, KV_SPAN - N_DIST), F32)], axis=1).reshape(H, 1, KV_SPAN)

    def body(e_ref, o_ref):
        first = pltpu.roll(jnp.broadcast_to(e_ref[...], (Q_BLOCK, KV_SPAN)), KV_SPAN - (CHUNK - 1), 1)
        x = _shear_rows(first, True)
        row = lax.broadcasted_iota(jnp.int32, (Q_BLOCK, 1), 0)
        chunk0 = row - (row & (CHUNK - 1))
        k = lax.broadcasted_iota(jnp.int32, (1, KV_SPAN), 1)
        o_ref[...] = jnp.where((k >= chunk0) & (k < chunk0 + BAND), x, NEG_INF)

    return pl.pallas_call(
        body,
        out_shape=jax.ShapeDtypeStruct((H, Q_BLOCK, KV_SPAN), F32),
        grid=(H,),
        in_specs=[pl.BlockSpec((None, 1, KV_SPAN), lambda h: (h, 0, 0))],
        out_specs=pl.BlockSpec((None, Q_BLOCK, KV_SPAN), lambda h: (h, 0, 0)),
        compiler_params=_cparams(("parallel",)),
        name="bias_blocks",
    )(e)


def _bias_blocks_grad(dblk):
    H = dblk.shape[0]

    def body(d_ref, o_ref):
        x = pltpu.roll(_shear_rows(d_ref[...], False), CHUNK - 1, 1)
        de = jnp.sum(x, axis=0, keepdims=True)
        lane = lax.broadcasted_iota(jnp.int32, de.shape, 1)
        far = jnp.sum(jnp.where(lane < N_FAR, de, 0.0), axis=-1, keepdims=True)
        o_ref[...] = jnp.where(lane == 0, far, jnp.where(lane < N_FAR, 0.0, de))

    de = pl.pallas_call(
        body,
        out_shape=jax.ShapeDtypeStruct((H, 1, KV_SPAN), F32),
        grid=(H,),
        in_specs=[pl.BlockSpec((None, Q_BLOCK, KV_SPAN), lambda h: (h, 0, 0))],
        out_specs=pl.BlockSpec((None, 1, KV_SPAN), lambda h: (h, 0, 0)),
        compiler_params=_cparams(("parallel",)),
        name="bias_grad_sum",
    )(dblk).reshape(H, KV_SPAN)
    near = jnp.flip(de[:, N_FAR:N_DIST], axis=1)
    return jnp.concatenate([jnp.zeros((H, 2 * MAX_REL - (N_DIST - N_FAR)), F32), near, de[:, 0:1]], axis=1)


def _ffn_forward(r1, r1b, p_l, w, l, ready):
    ready(f"up{l}", r1b)
    up_g = _mm_rows([(r1b, w["ffn_up_t"][l], True, (0, 2))], out_dtype=BF16, name=f"ffn_up_g{l}")
    up_v = _mm_rows([(r1b, w["ffn_up_t"][l], True, (1, 2))], out_dtype=BF16, name=f"ffn_up_v{l}")
    h = _ffn_act_fwd(up_g, up_v, w["ffn_dw_w"][l], w["ffn_dw_b"][l], name=f"ffn_act{l}")
    ready(f"dn{l}", h)
    z2, r2, r2b, gate, proj = _proj_ln(r1, h, w["ffn_w_down"][l], w["ln_ffn_g"][l], w["ln_ffn_b"][l],
                                       ple=(w["ple_w_gate"][l], w["ple_b_gate"][l], p_l, w["ple_w_proj"][l]),
                                       name=f"ffn_down_ln{l}")
    return dict(r1b=r1b, up_g=up_g, up_v=up_v, h=h, z2=z2, gate=gate, proj=proj), r2, r2b


def _ffn_backward(sv, dz2, dz2b, p_l, w, l, grads, ln_bwd, emit):
    r1b = sv["r1b"]
    ds, dproj, db_gate = _ple_bwd(dz2, sv["gate"], sv["proj"], name=f"ple_bwd{l}")
    dh = _mm_rows([(dz2b, w["ffn_w_down"][l], True, WHOLE)], out_dtype=BF16, name=f"ffn_dh{l}")
    dgate, dval, d_dw_w, d_dw_b = _ffn_act_bwd(sv["up_g"], sv["up_v"], dh, w["ffn_dw_w"][l], w["ffn_dw_b"][l],
                                               name=f"ffn_act_bwd{l}")
    grads["ffn_w_down"][l] = _wgrad(sv["h"], dz2b, tm=1408, name=f"d_ffn_w_down{l}")
    d_up_g = _wgrad(dgate, r1b, tm=1408, part=(0, 2), name=f"d_ffn_up_g{l}")
    grads["ffn_up_t"][l] = _wgrad(dval, r1b, tm=1408, part=(1, 2), into=d_up_g, name=f"d_ffn_up_v{l}")
    grads["ple_w_gate"][l] = _wgrad(r1b, ds, name=f"d_ple_w_gate{l}")
    grads["ple_w_proj"][l] = _wgrad(p_l, dproj, piece=D_MODEL // N_DEV, name=f"d_ple_w_proj{l}")
    grads["ffn_dw_w"][l] = d_dw_w
    grads["ffn_dw_b"][l] = d_dw_b[0]
    grads["ple_b_gate"][l] = db_gate[0]
    return _mm_rows([(ds, w["ple_w_gate"][l], True, WHOLE), (dgate, w["ffn_up_t"][l], False, (0, 2)),
                     (dval, w["ffn_up_t"][l], False, (1, 2))], add=dz2, add_scale=ALPHA, ln_bwd=ln_bwd, dep=emit(),
                    name=f"dr1_{l}")


def _local_step(x, p, target, w, ready=lambda group, after: None, emit=lambda group, grads: None):
    grads = {k: [None, None] for k in ("ffn_w_down", "ffn_up_t", "ple_w_gate", "ple_w_proj", "ffn_dw_w",
                                       "ffn_dw_b", "ple_b_gate", "ln_ffn_g", "ln_ffn_b", "ln_mix_g", "ln_mix_b")}

    xb, pb = x.astype(BF16), p.astype(BF16)
    ready("mix", None)
    u = _mm_rows([(xb, w["mix_w_in_t"], True, WHOLE)], name="mix_in")
    ycat, dpool, hconv = _mixer_fwd(u, w["pool_w"], w["pool_scale"], w["conv_dw_w"], w["conv_dw_b"], w["conv_ln_g"],
                                    w["conv_ln_b"])
    ready("mixo", ycat)
    z1, r1, r1b = _proj_ln(x, ycat, w["mix_w_out"], w["ln_mix_g"][0], w["ln_mix_b"][0], name="mix_out_ln")
    sv0, r2, r2b = _ffn_forward(r1, r1b, pb[0], w, 0, ready)

    ready("attn", r2b)
    qkv = _mm_rows([(r2b, w["attn_w_qkv"], False, WHOLE)], out_dtype=BF16, name="attn_qkv")
    bias = _bias_blocks(w["attn_rel_bias"])
    attn = _attn_fwd(qkv, bias)
    z3, r3, r3b = _proj_ln(r2, attn, w["attn_w_o"], w["ln_mix_g"][1], w["ln_mix_b"][1], name="attn_out_ln")
    sv1, _, _ = _ffn_forward(r3, r3b, pb[1], w, 1, ready)

    dz4, dz4b, grads["ln_ffn_g"][1], grads["ln_ffn_b"][1], loss = _loss_ln_bwd(
        sv1["z2"], w["ln_ffn_g"][1], w["ln_ffn_b"][1], target, name="loss_ln_bwd")
    dz3, dz3b, grads["ln_mix_g"][1], grads["ln_mix_b"][1] = _ffn_backward(
        sv1, dz4, dz4b, pb[1], w, 1, grads, (z3, w["ln_mix_g"][1]), lambda: emit("ffn1", grads))
    grads["attn_w_o"] = _wgrad(attn, dz3b, name="d_attn_w_o")
    dattn = _mm_rows([(dz3b, w["attn_w_o"], True, WHOLE)], out_dtype=BF16, name="d_attn")
    dq, dk, dv, dbias = _attn_bwd(qkv, bias, dattn)
    grads["attn_rel_bias"] = _bias_blocks_grad(dbias)
    dqkv = jnp.concatenate([dq, dk, dv], axis=1)
    grads["attn_w_qkv"] = _wgrad(r2b, dqkv, tn=768, piece=3 * D_MODEL // N_DEV, name="d_attn_w_qkv")
    dz2, dz2b, grads["ln_ffn_g"][0], grads["ln_ffn_b"][0] = _mm_rows(
        [(dqkv, w["attn_w_qkv"], True, WHOLE)], add=dz3, add_scale=ALPHA, ln_bwd=(sv0["z2"], w["ln_ffn_g"][0]),
        dep=emit("attn", grads), name="dr2")
    dz1, dz1b, grads["ln_mix_g"][0], grads["ln_mix_b"][0] = _ffn_backward(
        sv0, dz2, dz2b, pb[0], w, 0, grads, (z1, w["ln_mix_g"][0]), lambda: emit("ffn0", grads))
    grads["mix_w_out"] = _wgrad(ycat, dz1b, name="d_mix_w_out")
    dycat = _mm_rows([(dz1b, w["mix_w_out"], True, WHOLE)], name="d_ycat")
    du, g_pw, g_ps, g_cw, g_cb, g_cg, g_cbb = _mixer_bwd(u, dpool, hconv, dycat, w["pool_w"], w["pool_scale"],
                                                         w["conv_dw_w"], w["conv_ln_g"], w["conv_ln_b"])
    grads["mix_w_in_t"] = _wgrad(du, xb, name="d_mix_w_in")
    grads.update(pool_w=g_pw, pool_scale=g_ps[0], conv_dw_w=g_cw, conv_dw_b=g_cb[0], conv_ln_g=g_cg[0],
                 conv_ln_b=g_cbb[0])
    for kname in ("ln_ffn_g", "ln_ffn_b", "ln_mix_g", "ln_mix_b"):
        grads[kname] = [a[0] for a in grads[kname]]
    grad_x = _mm_rows([(du, w["mix_w_in_t"], False, WHOLE)], add=dz1, add_scale=ALPHA, dep=emit("mix", grads),
                      name="grad_x")
    return loss[0, 0], grad_x, grads


_HBM = pl.BlockSpec(memory_space=pltpu.HBM)
_SEM = pl.BlockSpec(memory_space=pltpu.SEMAPHORE)
_EFFECT = pltpu.SideEffectType.DATAFLOW_SIDE_EFFECTING


def _slot(ref, place, shape, k):
    if place in ("stack", "pieces"):
        return ref.at[k]
    ax = place[1]
    n = shape[ax]
    return ref.at[(slice(None),) * ax + (pl.ds(pl.multiple_of(k * n, n), n),)]


def _result_shape(buf, place):
    if place == "stack":
        return (N_DEV,) + buf.shape
    if place == "pieces":
        return buf.shape
    return tuple(s * N_DEV if i == place[1] else s for i, s in enumerate(buf.shape))


def _peers(x, y, c):
    for d in range(1, N_DEV):
        px, py, pc = x ^ ((d >> 2) & 1), y ^ ((d >> 1) & 1), c ^ (d & 1)
        yield d, (px, py, pc), 4 * px + 2 * py + pc


def _exchange_start_groups(groups, after, *, name):
    bufs = [b for g_bufs, _ in groups for b in g_bufs]
    places = [p_ for _, g_places in groups for p_ in g_places]
    first = [sum(len(g[0]) for g in groups[:k]) for k in range(len(groups) + 1)]
    nb, ng = len(bufs), len(groups)
    lands = [lax.empty(_result_shape(b, p_), b.dtype) for b, p_ in zip(bufs, places)]
    has_after = after is not None

    def body(*refs):
        srcs, dsts = refs[:nb], refs[nb:2 * nb]
        outs = refs[2 * nb + has_after:]
        token = outs[2 * ng + 2 * nb]
        x, y, c = lax.axis_index("x"), lax.axis_index("y"), lax.axis_index("c")
        me = 4 * x + 2 * y + c
        for k in range(ng):
            send_sems, recv_sems = outs[2 * k], outs[2 * k + 1]
            for b in range(first[k], first[k + 1]):
                pair = (b - first[k]) * N_DEV
                for d, dev, peer in _peers(x, y, c):
                    pltpu.make_async_remote_copy(
                        src_ref=srcs[b].at[peer] if places[b] == "pieces" else srcs[b],
                        dst_ref=_slot(dsts[b], places[b], bufs[b].shape, me),
                        send_sem=send_sems.at[pair + d], recv_sem=recv_sems.at[pair + d],
                        device_id=dev, device_id_type=pl.DeviceIdType.MESH).start()
                pltpu.make_async_copy(srcs[b].at[me] if places[b] == "pieces" else srcs[b],
                                      _slot(dsts[b], places[b], bufs[b].shape, me), recv_sems.at[pair]).start()
        token[...] = jnp.zeros_like(token)

    sems = [pltpu.SemaphoreType.DMA(((first[k + 1] - first[k]) * N_DEV,)) for k in range(ng) for _ in range(2)]
    ins = [pltpu.with_memory_space_constraint(a, pltpu.HBM) for a in bufs + lands]
    out = pl.pallas_call(
        body,
        out_shape=(*sems, *[pltpu.HBM(a.shape, a.dtype) for a in ins], jax.ShapeDtypeStruct((8, 128), F32)),
        in_specs=[_HBM] * (2 * nb) + ([pl.BlockSpec(memory_space=pl.ANY)] if has_after else []),
        out_specs=(*[_SEM] * (2 * ng), *[_HBM] * (2 * nb), pl.BlockSpec(memory_space=pltpu.VMEM)),
        input_output_aliases={i: 2 * ng + i for i in range(2 * nb)},
        compiler_params=pltpu.CompilerParams(has_side_effects=_EFFECT),
        name=name,
    )(*ins, *([after] if has_after else []))
    srcs_thru, lands_thru = out[2 * ng:2 * ng + nb], out[2 * ng + nb:2 * ng + 2 * nb]
    handles = [dict(send=out[2 * k], recv=out[2 * k + 1], srcs=srcs_thru[first[k]:first[k + 1]],
                    lands=lands_thru[first[k]:first[k + 1]], places=groups[k][1]) for k in range(ng)]
    return handles, out[-1]


def _exchange_start(bufs, places, after, *, name):
    (handle,), token = _exchange_start_groups([(bufs, places)], after, name=name)
    return dict(handle, token=token)


def _exchange_wait(h, after, *, name):
    nb = len(h["srcs"])
    places = h["places"]
    shapes = [a.shape for a in h["srcs"]]

    def body(*refs):
        srcs, dsts, send_sems, recv_sems = refs[:nb], refs[nb:2 * nb], refs[2 * nb], refs[2 * nb + 1]
        x, y, c = lax.axis_index("x"), lax.axis_index("y"), lax.axis_index("c")
        me = 4 * x + 2 * y + c
        for b in range(nb):
            pieces = places[b] == "pieces"
            for d, dev, peer in _peers(x, y, c):
                cp = pltpu.make_async_remote_copy(
                    src_ref=srcs[b].at[peer] if pieces else srcs[b],
                    dst_ref=_slot(dsts[b], places[b], shapes[b], peer),
                    send_sem=send_sems.at[b * N_DEV + d], recv_sem=recv_sems.at[b * N_DEV + d],
                    device_id=dev, device_id_type=pl.DeviceIdType.MESH)
                cp.wait_send()
                cp.wait_recv()
            pltpu.make_async_copy(srcs[b].at[me] if pieces else srcs[b], _slot(dsts[b], places[b], shapes[b], me),
                                  recv_sems.at[b * N_DEV]).wait()

    ins = list(h["srcs"]) + list(h["lands"])
    out = pl.pallas_call(
        body,
        out_shape=tuple(pltpu.HBM(a.shape, a.dtype) for a in ins),
        in_specs=[_HBM] * (2 * nb) + [_SEM, _SEM, pl.BlockSpec(memory_space=pl.ANY)],
        out_specs=tuple([_HBM] * (2 * nb)),
        input_output_aliases={i: i for i in range(2 * nb)},
        compiler_params=pltpu.CompilerParams(has_side_effects=_EFFECT),
        name=name,
    )(*ins, h["send"], h["recv"], after)
    return out[nb:]


def _adamw(recv, w, m, v, *, layer=0, into=None, name):
    L, R, C = w.shape
    tr = R
    for cand in (512, 256, 128, 64, 32, 16):
        if R % cand == 0 and cand * C * 4 <= 2 * 1024 * 1024:
            tr = cand
            break
    c1 = 1.0 - ADAM_B1 ** ADAM_STEP
    c2 = 1.0 - ADAM_B2 ** ADAM_STEP

    def body(r_ref, w_ref, m_ref, v_ref, *rest):
        g_ref, d_ref, mo_ref, vo_ref = rest[-4:]
        g = r_ref[0].astype(F32)
        for i in range(1, N_DEV):
            g = g + r_ref[i].astype(F32)
        m_new = ADAM_B1 * m_ref[...] + (1.0 - ADAM_B1) * g
        v_new = ADAM_B2 * v_ref[...] + (1.0 - ADAM_B2) * (g * g)
        m_hat = m_new / c1
        v_hat = v_new / c2
        g_ref[...] = g
        d_ref[...] = -ADAM_LR * (m_hat / (jnp.sqrt(v_hat) + ADAM_EPS) + ADAM_WD * w_ref[...])
        mo_ref[...] = m_new
        vo_ref[...] = v_new

    row = pl.BlockSpec((None, tr, C), lambda i: (layer, i, 0))
    others = [] if into is None else list(into)
    return pl.pallas_call(
        body,
        out_shape=[jax.ShapeDtypeStruct((L, R, C), F32)] * 4,
        grid=(R // tr,),
        in_specs=[pl.BlockSpec((N_DEV, tr, C), lambda i: (0, i, 0)), row, row, row]
        + [pl.BlockSpec(memory_space=pl.ANY)] * len(others),
        out_specs=[row] * 4,
        input_output_aliases={4 + k: k for k in range(len(others))},
        compiler_params=_cparams(("parallel",)),
        name=name,
    )(recv, w, m, v, *others)


_TRANSPOSED = ("mix_w_in", "ffn_w_up")


def _ffn_groups(l):
    return ((f"up{l}", (("ffn_w_up", l, BF16, ("axis", 0)), ("ffn_dw_w", l, F32, "stack"))),
            (f"dn{l}", (("ffn_w_down", l, BF16, ("axis", 0)), ("ple_w_gate", l, BF16, ("axis", 0)),
                        ("ple_w_proj", l, BF16, ("axis", 1)))))


_GATHER_GROUPS = (
    ("mix", (("mix_w_in", 0, BF16, ("axis", 0)), ("conv_dw_w", 0, F32, "stack"))),
    ("mixo", (("mix_w_out", 0, BF16, ("axis", 0)),)),
    *_ffn_groups(0),
    ("attn", (("attn_w_qkv", 0, BF16, ("axis", 1)), ("attn_w_o", 0, BF16, ("axis", 0)))),
    *_ffn_groups(1))
_SHARDED = ("mix_w_in", "conv_dw_w", "mix_w_out", "attn_w_qkv", "attn_w_o", "ffn_w_up", "ffn_dw_w", "ffn_w_down",
            "ple_w_gate", "ple_w_proj")
_REPLICATED = ("pool_w", "pool_scale", "conv_dw_b", "conv_ln_g", "conv_ln_b", "attn_rel_bias", "ln_mix_g",
               "ln_mix_b", "ffn_dw_b", "ple_b_gate", "ln_ffn_g", "ln_ffn_b")


def _pack_rows(parts, row_mult, dtype):
    lead = parts[0].shape[:-1]
    flat = jnp.concatenate([a.astype(dtype) for a in parts], axis=-1)
    n = flat.shape[-1]
    unit = row_mult * LANES
    padded = -(-n // unit) * unit
    flat = jnp.pad(flat, [(0, 0)] * len(lead) + [(0, padded - n)])
    return flat.reshape(lead + (padded // LANES, LANES))


def _unpack(flat2d, shapes):
    flat = flat2d.reshape(-1)
    out, o = [], 0
    for s in shapes:
        n = math.prod(s)
        out.append(flat[o:o + n].reshape(s))
        o += n
    return out


def _full_from_shards(g, axis):
    parts = jnp.moveaxis(g, 0, axis)
    shp = list(g.shape[1:])
    shp[axis] *= g.shape[0]
    return parts.reshape(shp)


def _pieces_from_full(full, axis, k=N_DEV):
    shp = list(full.shape)
    n = shp[axis] // k
    t = full.reshape(shp[:axis] + [k, n] + shp[axis + 1:])
    return jnp.moveaxis(t, axis, 0)


def kernel(x, p, mix_w_in, pool_w, pool_scale, conv_dw_w, conv_dw_b, conv_ln_g, conv_ln_b, mix_w_out, attn_w_qkv, attn_rel_bias, attn_w_o, ln_mix_g, ln_mix_b, ffn_w_up, ffn_dw_w, ffn_dw_b, ffn_w_down, ple_w_proj, ple_w_gate, ple_b_gate, ln_ffn_g, ln_ffn_b, loss_target, m_mix_w_in, m_pool_w, m_pool_scale, m_conv_dw_w, m_conv_dw_b, m_conv_ln_g, m_conv_ln_b, m_mix_w_out, m_attn_w_qkv, m_attn_rel_bias, m_attn_w_o, m_ln_mix_g, m_ln_mix_b, m_ffn_w_up, m_ffn_dw_w, m_ffn_dw_b, m_ffn_w_down, m_ple_w_proj, m_ple_w_gate, m_ple_b_gate, m_ln_ffn_g, m_ln_ffn_b, v_mix_w_in, v_pool_w, v_pool_scale, v_conv_dw_w, v_conv_dw_b, v_conv_ln_g, v_conv_ln_b, v_mix_w_out, v_attn_w_qkv, v_attn_rel_bias, v_attn_w_o, v_ln_mix_g, v_ln_mix_b, v_ffn_w_up, v_ffn_dw_w, v_ffn_dw_b, v_ffn_w_down, v_ple_w_proj, v_ple_w_gate, v_ple_b_gate, v_ln_ffn_g, v_ln_ffn_b):
    a = dict(locals())
    sh_names = list(_SHARDED)
    names = sh_names + list(_REPLICATED)
    wts = {n: a[n] for n in names}
    mom = {n: a["m_" + n] for n in names}
    var = {n: a["v_" + n] for n in names}

    for n in _TRANSPOSED:
        wts[n], mom[n], var[n] = (jnp.swapaxes(d[n], 1, 2) for d in (wts, mom, var))
    handles, token = _exchange_start_groups(
        [([wts[n][l].astype(dt) for n, l, dt, _ in items], [pl_ for *_, pl_ in items]) for _, items in _GATHER_GROUPS],
        None, name="gather_start")
    gather = {group: h for (group, _), h in zip(_GATHER_GROUPS, handles)}

    w = dict(pool_w=pool_w[0], pool_scale=pool_scale[0], conv_dw_b=conv_dw_b[0], conv_ln_g=conv_ln_g[0],
             conv_ln_b=conv_ln_b[0], attn_rel_bias=attn_rel_bias[0], ln_mix_g=ln_mix_g, ln_mix_b=ln_mix_b,
             ffn_dw_b=ffn_dw_b, ple_b_gate=ple_b_gate, ln_ffn_g=ln_ffn_g, ln_ffn_b=ln_ffn_b)
    for n in ("ffn_up_t", "ffn_dw_w", "ffn_w_down", "ple_w_gate", "ple_w_proj"):
        w[n] = [None, None]

    def ready(group, after):
        got = _exchange_wait(gather[group], token if after is None else after, name="gather_wait_" + group)
        if group == "mix":
            w["mix_w_in_t"], w["conv_dw_w"] = got[0], _full_from_shards(got[1], 1)
        elif group == "mixo":
            (w["mix_w_out"],) = got
        elif group == "attn":
            w["attn_w_qkv"], w["attn_w_o"] = got
        elif group[:2] == "up":
            l = int(group[2])
            w["ffn_up_t"][l], w["ffn_dw_w"][l] = got[0], _full_from_shards(got[1], 1)
        else:
            l = int(group[2])
            w["ffn_w_down"][l], w["ple_w_gate"][l], w["ple_w_proj"][l] = got

    scatter = {}

    def emit(group, gr):
        if group[:3] == "ffn":
            l = int(group[3])
            pieces = [_pieces_from_full(gr["ffn_up_t"][l], 0),
                      _pieces_from_full(gr["ffn_dw_w"][l], 1), _pieces_from_full(gr["ffn_w_down"][l], 0),
                      _pieces_from_full(gr["ple_w_gate"][l], 0), gr["ple_w_proj"][l]]
        elif group == "attn":
            pieces = [gr["attn_w_qkv"], _pieces_from_full(gr["attn_w_o"], 0)]
        else:
            pieces = [_pieces_from_full(gr["mix_w_in_t"], 0), _pieces_from_full(gr["conv_dw_w"], 1),
                      _pieces_from_full(gr["mix_w_out"], 0)]
        scatter[group] = _exchange_start([a.astype(BF16) for a in pieces], ["pieces"] * len(pieces), None,
                                         name="grad_start_" + group)
        if group != "mix":
            return scatter[group]["token"]
        gfull = dict(
            pool_w=gr["pool_w"][None], pool_scale=gr["pool_scale"][None], conv_dw_b=gr["conv_dw_b"][None],
            conv_ln_g=gr["conv_ln_g"][None], conv_ln_b=gr["conv_ln_b"][None],
            attn_rel_bias=gr["attn_rel_bias"][None], ln_mix_g=jnp.stack(gr["ln_mix_g"]),
            ln_mix_b=jnp.stack(gr["ln_mix_b"]), ffn_dw_b=jnp.stack(gr["ffn_dw_b"]),
            ple_b_gate=jnp.stack(gr["ple_b_gate"]), ln_ffn_g=jnp.stack(gr["ln_ffn_g"]),
            ln_ffn_b=jnp.stack(gr["ln_ffn_b"]))
        rep_send = _pack_rows([gfull[n].reshape(-1) for n in _REPLICATED], 8, F32)
        scatter["replicated"] = _exchange_start([rep_send], ["stack"], scatter[group]["token"],
                                                name="grad_start_replicated")
        return scatter["replicated"]["token"]

    loss_part, grad_x, gr = _local_step(x[0], p[:, 0], loss_target[0], w, ready, emit)
    loss = lax.psum(loss_part, ("x", "y", "c"))

    group_weights = {"ffn1": (("ffn_w_up", 1), ("ffn_dw_w", 1), ("ffn_w_down", 1), ("ple_w_gate", 1), ("ple_w_proj", 1)),
                     "attn": (("attn_w_qkv", 0), ("attn_w_o", 0)),
                     "ffn0": (("ffn_w_up", 0), ("ffn_dw_w", 0), ("ffn_w_down", 0), ("ple_w_gate", 0), ("ple_w_proj", 0)),
                     "mix": (("mix_w_in", 0), ("conv_dw_w", 0), ("mix_w_out", 0))}
    updated = {}
    after = grad_x
    for group in ("ffn1", "attn", "ffn0", "mix"):
        recv = _exchange_wait(scatter[group], after, name="grad_wait_" + group)
        for (n, l), r in zip(group_weights[group], recv):
            updated[n] = _adamw(r, wts[n], mom[n], var[n], layer=l, into=updated.get(n), name=f"adamw_{n}{l}")
            after = updated[n][0]
    res = [{n: jnp.swapaxes(updated[n][k], 1, 2) if n in _TRANSPOSED else updated[n][k] for n in sh_names}
           for k in range(4)]
    (rep_recv,) = _exchange_wait(scatter["replicated"], after, name="grad_wait_replicated")

    def flat_state(d):
        return _pack_rows([d[n].reshape(-1) for n in _REPLICATED], 8, F32)[None]

    rep_out = _adamw(rep_recv, flat_state(wts), flat_state(mom), flat_state(var), name="adamw_replicated")
    for k in range(4):
        for n, arr in zip(_REPLICATED, _unpack(rep_out[k][0], [wts[n].shape for n in _REPLICATED])):
            res[k][n] = arr
    order = ["mix_w_in", "pool_w", "pool_scale", "conv_dw_w", "conv_dw_b", "conv_ln_g", "conv_ln_b", "mix_w_out",
             "attn_w_qkv", "attn_rel_bias", "attn_w_o", "ln_mix_g", "ln_mix_b", "ffn_w_up", "ffn_dw_w", "ffn_dw_b",
             "ffn_w_down", "ple_w_proj", "ple_w_gate", "ple_b_gate", "ln_ffn_g", "ln_ffn_b"]
    outs = [loss, grad_x[None]]
    for k in range(4):
        outs += [res[k][n] for n in order]
    return tuple(outs)
```

```python
import functools
import math

import jax
import jax.numpy as jnp
from jax import lax
from jax.experimental import pallas as pl
from jax.experimental.pallas import tpu as pltpu

F32 = jnp.float32
BF16 = jnp.bfloat16

N_DEV = 8
D_MODEL = 1024
D_POOL = 512
D_CONV = 512
POOL_WINDOWS = (2, 4, 8, 16)
POOL_GROUP = 128
CONV_KERNEL = 31
CHUNK = 64
HEAD_DIM = 64
N_HEADS = 16
LEFT_CHUNKS = 8
BAND = (LEFT_CHUNKS + 1) * CHUNK
MAX_REL = 256
D_FF = 2816
PLE_DIM = 256
ALPHA = 4.0 ** 0.25
LN_EPS = 1e-5
NEG_INF = -1e30
ADAM_LR, ADAM_B1, ADAM_B2, ADAM_EPS, ADAM_WD, ADAM_STEP = 0.001, 0.9, 0.999, 1e-08, 0.01, 10

Q_BLOCK = 4 * CHUNK
KV_PAD = LEFT_CHUNKS * CHUNK
KV_SPAN = KV_PAD + Q_BLOCK
CONV_HALO = 32
FFN_HALO = 16
SUB_ROWS, SUB_LANES = 64, 128
LANES = 1024
VMEM_LIMIT = 56 * 1024 * 1024


def _cparams(sem=None):
    return pltpu.CompilerParams(dimension_semantics=sem, vmem_limit_bytes=VMEM_LIMIT)


def _tile(dim, pref):
    if dim <= pref:
        return dim
    t = pref - pref % 128
    while t >= 128:
        if dim % t == 0:
            return t
        t -= 128
    return dim


def _sigmoid(x):
    return 1.0 / (1.0 + jnp.exp(-x))


def _bdot(a, b, dn=(((1,), (0,)), ((), ()))):
    return lax.dot_general(a.astype(BF16), b.astype(BF16), dn, preferred_element_type=F32)


WHOLE = (0, 1)
NT = (((1,), (1,)), ((), ()))
TN = (((0,), (0,)), ((), ()))


def _wgrad(a, b, *, tm=1024, tn=1024, tk=1024, piece=None, part=(0, 1), into=None, name):
    K, M = a.shape
    kb, N = b.shape
    assert K == kb, (a.shape, b.shape)
    tm, tn, tk = _tile(M, tm), _tile(N, tn), _tile(K, tk)
    nk = K // tk
    per = 1 if piece is None else tn // piece
    assert piece is None or tn == per * piece

    def body(a_ref, b_ref, *rest):
        o_ref, acc = rest[-2:]
        k = pl.program_id(2)

        @pl.when(k == 0)
        def _():
            acc[...] = jnp.zeros_like(acc)

        acc[...] += _bdot(a_ref[...], b_ref[...], TN)

        @pl.when(k == nk - 1)
        def _():
            if piece is None:
                o_ref[...] = acc[...].astype(BF16)
            else:
                for s in range(per):
                    o_ref[s] = acc[:, s * piece:(s + 1) * piece].astype(BF16)

    if piece is None:
        first = part[0] * (M // tm)
        out_shape = (part[1] * M, N)
        out_spec = pl.BlockSpec((tm, tn), lambda i, j, k: (first + i, j))
    else:
        out_shape, out_spec = (N // piece, M, piece), pl.BlockSpec((per, tm, piece), lambda i, j, k: (j, i, 0))
    others = [] if into is None else [into]
    return pl.pallas_call(
        body,
        out_shape=jax.ShapeDtypeStruct(out_shape, BF16),
        grid=(M // tm, N // tn, nk),
        in_specs=[pl.BlockSpec((tk, tm), lambda i, j, k: (k, i)), pl.BlockSpec((tk, tn), lambda i, j, k: (k, j))]
        + [pl.BlockSpec(memory_space=pl.ANY)] * len(others),
        out_specs=out_spec,
        input_output_aliases={2: 0} if others else {},
        scratch_shapes=[pltpu.VMEM((tm, tn), F32)],
        compiler_params=_cparams(("parallel", "parallel", "arbitrary")),
        name=name,
    )(a, b, *others)


def _mm_rows(pairs, *, add=None, add_scale=1.0, out_dtype=F32, tm=512, dep=None, ln_bwd=None, name):
    M = pairs[0][0].shape[0]
    n = len(pairs)
    has_add = add is not None
    w_rows = [w_.shape[0] // part[1] for _, w_, _, part in pairs]
    N = w_rows[0] if pairs[0][2] else pairs[0][1].shape[1]

    def body(*refs):
        acc = None
        for i, (_, _, tr, _) in enumerate(pairs):
            part = _bdot(refs[2 * i][...], refs[2 * i + 1][...], NT if tr else (((1,), (0,)), ((), ())))
            acc = part if acc is None else acc + part
        if has_add:
            acc = acc + add_scale * refs[2 * n][...]
        if ln_bwd is None:
            refs[-1][...] = acc.astype(out_dtype)
            return
        z_ref, g_ref = refs[2 * n + has_add], refs[2 * n + has_add + 1]
        dz_ref, dzb_ref, dg_ref, db_ref = refs[-4:]

        @pl.when(pl.program_id(0) == 0)
        def _():
            dg_ref[...] = jnp.zeros_like(dg_ref)
            db_ref[...] = jnp.zeros_like(db_ref)

        dg_acc = jnp.zeros((8, N), F32)
        db_acc = jnp.zeros((8, N), F32)
        for r0 in range(0, tm, LN_ROWS):
            rows = pl.ds(r0, LN_ROWS)
            do = acc[r0:r0 + LN_ROWS]
            dz, xh = _ln_bwd_rows(z_ref[rows, :], g_ref[...], do)
            dz_ref[rows, :] = dz
            dzb_ref[rows, :] = dz.astype(BF16)
            dg_acc = dg_acc + jnp.sum((do * xh).reshape(LN_ROWS // 8, 8, N), axis=0)
            db_acc = db_acc + jnp.sum(do.reshape(LN_ROWS // 8, 8, N), axis=0)
        dg_ref[...] += jnp.sum(dg_acc, axis=0, keepdims=True)
        db_ref[...] += jnp.sum(db_acc, axis=0, keepdims=True)

    in_specs, args = [], []
    for (a, w_, _, part), rows in zip(pairs, w_rows):
        in_specs += [pl.BlockSpec((tm, a.shape[1]), lambda i: (i, 0)),
                     pl.BlockSpec((rows, w_.shape[1]), functools.partial(lambda i, j: (j, 0), j=part[0]))]
        args += [a, w_]
    row = pl.BlockSpec((tm, N), lambda i: (i, 0))
    fix = pl.BlockSpec((1, N), lambda i: (0, 0))
    if has_add:
        in_specs.append(row)
        args.append(add)
    if ln_bwd is not None:
        in_specs += [row, fix]
        args += [ln_bwd[0], ln_bwd[1].reshape(1, N)]
    if dep is not None:
        in_specs.append(pl.BlockSpec(memory_space=pl.ANY))
        args.append(dep)
    if ln_bwd is None:
        out_shape, out_specs = jax.ShapeDtypeStruct((M, N), out_dtype), row
    else:
        out_shape = [jax.ShapeDtypeStruct((M, N), F32), jax.ShapeDtypeStruct((M, N), BF16),
                     jax.ShapeDtypeStruct((1, N), F32), jax.ShapeDtypeStruct((1, N), F32)]
        out_specs = [row, row, fix, fix]
    return pl.pallas_call(
        body,
        out_shape=out_shape,
        grid=(M // tm,),
        in_specs=in_specs,
        out_specs=out_specs,
        compiler_params=_cparams(("parallel",) if ln_bwd is None else ("arbitrary",)),
        name=name,
    )(*args)


def _ln_bwd_rows(zt, g, do):
    zc = zt - jnp.mean(zt, axis=-1, keepdims=True)
    rstd = lax.rsqrt(jnp.mean(zc * zc, axis=-1, keepdims=True) + LN_EPS)
    xh = zc * rstd
    dxh = do * g
    return rstd * (dxh - jnp.mean(dxh, axis=-1, keepdims=True) - xh * jnp.mean(dxh * xh, axis=-1, keepdims=True)), xh


def _layer_norm_rows(z, g, b):
    mu = jnp.mean(z, axis=-1, keepdims=True)
    zc = z - mu
    var = jnp.mean(zc * zc, axis=-1, keepdims=True)
    return zc * lax.rsqrt(var + LN_EPS) * g + b


def _proj_ln(res, a, w, ln_g, ln_b, *, ple=None, ts=512, name):
    S, D = res.shape
    ka = a.shape[1]
    has_ple = ple is not None
    row = lambda i: (i, 0)
    fix = lambda i: (0, 0)

    def body(*refs):
        if has_ple:
            (res_ref, a_ref, w_ref, g_ref, b_ref, wg_ref, bg_ref, p_ref, wp_ref, z_ref, r_ref, rb_ref, gate_ref,
             proj_ref, acc) = refs
        else:
            res_ref, a_ref, w_ref, g_ref, b_ref, z_ref, r_ref, rb_ref, acc = refs
        acc[...] = _bdot(a_ref[...], w_ref[...])
        if has_ple:
            gate_ref[...] = _bdot(res_ref[...], wg_ref[...])
            proj_ref[...] = _bdot(p_ref[...], wp_ref[...])
        for r0 in range(0, ts, LN_ROWS):
            rows = pl.ds(r0, LN_ROWS)
            z = ALPHA * res_ref[rows, :] + acc[rows, :]
            if has_ple:
                gate = _sigmoid(gate_ref[rows, :] + bg_ref[...])
                gate_ref[rows, :] = gate
                z = z + gate * proj_ref[rows, :]
            z_ref[rows, :] = z
            r = _layer_norm_rows(z, g_ref[...], b_ref[...])
            r_ref[rows, :] = r
            rb_ref[rows, :] = r.astype(BF16)

    in_specs = [pl.BlockSpec((ts, D), row), pl.BlockSpec((ts, ka), row), pl.BlockSpec((ka, D), fix),
                pl.BlockSpec((1, D), fix), pl.BlockSpec((1, D), fix)]
    args = [res, a, w, ln_g.reshape(1, D), ln_b.reshape(1, D)]
    out_dtypes = [F32, F32, BF16]
    if has_ple:
        wg, bg, p, wp = ple
        in_specs += [pl.BlockSpec((D, D), fix), pl.BlockSpec((1, D), fix), pl.BlockSpec((ts, PLE_DIM), row),
                     pl.BlockSpec((PLE_DIM, D), fix)]
        args += [wg, bg.reshape(1, D), p, wp]
        out_dtypes += [F32, F32]
    return pl.pallas_call(
        body,
        out_shape=[jax.ShapeDtypeStruct((S, D), dt) for dt in out_dtypes],
        grid=(S // ts,),
        in_specs=in_specs,
        out_specs=[pl.BlockSpec((ts, D), row)] * len(out_dtypes),
        scratch_shapes=[pltpu.VMEM((ts, D), F32)],
        compiler_params=_cparams(("parallel",)),
        name=name,
    )(*args)


CONV_ROWS = 32
LN_ROWS = 16


def _shifted_copies(src, dst, rows):
    for c0 in range(0, src.shape[1], SUB_LANES):
        ln = pl.ds(c0, SUB_LANES)
        for r0 in range(0, rows, SUB_ROWS):
            rc = min(SUB_ROWS, rows - r0)
            for b, shifted in enumerate(_rows_ahead(src, r0, rc, ln, range(1, 8))):
                dst[b, pl.ds(r0, rc), ln] = shifted


def _rows_at(src, copies, off, n, ln):
    b = off % 8
    return src[pl.ds(off, n), ln] if b == 0 else copies[b - 1, pl.ds(off - b, n), ln]


def _conv31(stg, gsh, cw_ref, cb_ref, out, rows, first_off):
    for c0 in range(0, D_CONV, SUB_LANES):
        ln = pl.ds(c0, SUB_LANES)
        for r0 in range(0, rows, CONV_ROWS):
            acc = jnp.zeros((CONV_ROWS, SUB_LANES), F32) + cb_ref[:, ln]
            for k in range(CONV_KERNEL):
                acc = acc + cw_ref[k:k + 1, ln] * _rows_at(stg, gsh, first_off + k + r0, CONV_ROWS, ln)
            out[pl.ds(r0, CONV_ROWS), ln] = acc


def _mixer_fwd(u, pool_w, pool_scale, conv_w, conv_b, cln_g, cln_b, *, ts=256):
    S = u.shape[0]
    hb = CONV_HALO
    nh = ts // hb

    def body(u_ref, uh_ref, pw_ref, ps_ref, cw_ref, cb_ref, g_ref, b_ref, y_ref, d_ref, hcs, sta, stg, gsh):
        i = pl.program_id(0)
        first = i == 0
        sta[pl.ds(0, hb), :] = jnp.where(first, 0.0, uh_ref[:, 0:D_POOL])
        sta[pl.ds(hb, ts), :] = u_ref[:, 0:D_POOL]
        glu_h = uh_ref[:, D_POOL:D_POOL + D_CONV] * _sigmoid(uh_ref[:, D_POOL + D_CONV:])
        stg[pl.ds(0, hb), :] = jnp.where(first, 0.0, glu_h)
        stg[pl.ds(hb, ts), :] = u_ref[:, D_POOL:D_POOL + D_CONV] * _sigmoid(u_ref[:, D_POOL + D_CONV:])

        for g, w in enumerate(POOL_WINDOWS):
            lanes = pl.ds(g * POOL_GROUP, POOL_GROUP)
            for r0 in range(0, ts, SUB_ROWS):
                s = None
                for q in range(0, w, 8):
                    for tap in _rows_back(sta, hb + r0 - q, SUB_ROWS, lanes, range(min(8, w - q))):
                        s = tap if s is None else s + tap
                pos = (i * ts + r0 + lax.broadcasted_iota(jnp.int32, (SUB_ROWS, 1), 0) + 1).astype(F32)
                d_g = s / jnp.minimum(pos, float(w)) - sta[pl.ds(hb + r0, SUB_ROWS), lanes]
                d_ref[pl.ds(r0, SUB_ROWS), lanes] = d_g.astype(BF16)
            y_ref[:, lanes] = (_bdot(d_ref[:, lanes], pw_ref[g]) * ps_ref[:, lanes]).astype(BF16)

        _shifted_copies(stg, gsh, hb + ts - 8)
        _conv31(stg, gsh, cw_ref, cb_ref, hcs, ts, hb - (CONV_KERNEL - 1))
        for r0 in range(0, ts, LN_ROWS):
            rows = pl.ds(r0, LN_ROWS)
            ln = _layer_norm_rows(hcs[rows, :], g_ref[...], b_ref[...])
            y_ref[rows, D_POOL:] = (ln * _sigmoid(ln)).astype(BF16)

    fix2 = lambda i: (0, 0)
    return pl.pallas_call(
        body,
        out_shape=[jax.ShapeDtypeStruct((S, D_MODEL), BF16), jax.ShapeDtypeStruct((S, D_POOL), BF16),
                   jax.ShapeDtypeStruct((S, D_CONV), F32)],
        grid=(S // ts,),
        in_specs=[pl.BlockSpec((ts, 3 * D_POOL), lambda i: (i, 0)),
                  pl.BlockSpec((hb, 3 * D_POOL), lambda i: (jnp.maximum(i * nh - 1, 0), 0)),
                  pl.BlockSpec((4, POOL_GROUP, POOL_GROUP), lambda i: (0, 0, 0)),
                  pl.BlockSpec((1, D_POOL), fix2), pl.BlockSpec((CONV_KERNEL, D_CONV), fix2),
                  pl.BlockSpec((1, D_CONV), fix2), pl.BlockSpec((1, D_CONV), fix2), pl.BlockSpec((1, D_CONV), fix2)],
        out_specs=[pl.BlockSpec((ts, D_MODEL), lambda i: (i, 0)), pl.BlockSpec((ts, D_POOL), lambda i: (i, 0)),
                   pl.BlockSpec((ts, D_CONV), lambda i: (i, 0))],
        scratch_shapes=[pltpu.VMEM((hb + ts, D_POOL), F32), pltpu.VMEM((hb + ts, D_CONV), F32),
                        pltpu.VMEM((7, hb + ts - 8, D_CONV), F32)],
        compiler_params=_cparams(("parallel",)),
        name="mixer_fwd",
    )(u, u, pool_w, pool_scale.reshape(1, D_POOL), conv_w, conv_b.reshape(1, D_CONV), cln_g.reshape(1, D_CONV),
      cln_b.reshape(1, D_CONV))


def _mixer_bwd(u, d, hc, dycat, pool_w, pool_scale, conv_w, cln_g, cln_b, *, ts=256):
    S = u.shape[0]
    hb = CONV_HALO
    nh = ts // hb
    n = S // ts
    te = ts + hb
    K = CONV_KERNEL

    def body(u_ref, up_ref, un_ref, d_ref, hc_ref, hcn_ref, dy_ref, dyn_ref, pw_ref, ps_ref, cw_ref, g_ref, b_ref,
             du_ref, dpw_ref, dps_ref, dcw_ref, dcb_ref, dg_ref, db_ref, stg, std, sth, gsh, hsh):
        i = pl.program_id(0)
        first = i == 0
        last = i == n - 1

        @pl.when(first)
        def _():
            dpw_ref[...] = jnp.zeros_like(dpw_ref)
            dps_ref[...] = jnp.zeros_like(dps_ref)
            dcw_ref[...] = jnp.zeros_like(dcw_ref)
            dcb_ref[...] = jnp.zeros_like(dcb_ref)
            dg_ref[...] = jnp.zeros_like(dg_ref)
            db_ref[...] = jnp.zeros_like(db_ref)

        pos_e = (i * ts + lax.broadcasted_iota(jnp.int32, (te, 1), 0) + 1).astype(F32)
        dya = dy_ref[:, 0:D_POOL]
        dya_n = jnp.where(last, 0.0, dyn_ref[:, 0:D_POOL])
        for g, w in enumerate(POOL_WINDOWS):
            lanes = pl.ds(g * POOL_GROUP, POOL_GROUP)
            sl = slice(g * POOL_GROUP, (g + 1) * POOL_GROUP)
            pw = pw_ref[g]
            scale = ps_ref[:, lanes]
            d_g = d_ref[:, lanes]
            pre = _bdot(d_g, pw)
            dps_ref[:, lanes] += jnp.sum(dya[:, sl] * pre, axis=0, keepdims=True)
            dys = dya[:, sl] * scale
            dpw_ref[g] += _bdot(d_g, dys, TN)
            dys_e = jnp.concatenate([dys, dya_n[:, sl] * scale], axis=0)
            dd = _bdot(dys_e, pw, NT)
            std[:, lanes] = dd / jnp.minimum(pos_e, float(w))
            for r0 in range(0, ts, SUB_ROWS):
                da = -dd[r0:r0 + SUB_ROWS]
                for q in range(0, w, 8):
                    for tap in _rows_ahead(std, r0 + q, SUB_ROWS, lanes, range(min(8, w - q))):
                        da = da + tap
                du_ref[pl.ds(r0, SUB_ROWS), lanes] = da.astype(BF16)

        glu_p = up_ref[:, D_POOL:D_POOL + D_CONV] * _sigmoid(up_ref[:, D_POOL + D_CONV:])
        stg[pl.ds(0, hb), :] = jnp.where(first, 0.0, glu_p)
        bv = u_ref[:, D_POOL:D_POOL + D_CONV]
        sg = _sigmoid(u_ref[:, D_POOL + D_CONV:])
        stg[pl.ds(hb, ts), :] = bv * sg
        glu_n = un_ref[:, D_POOL:D_POOL + D_CONV] * _sigmoid(un_ref[:, D_POOL + D_CONV:])
        stg[pl.ds(hb + ts, hb), :] = jnp.where(last, 0.0, glu_n)
        _shifted_copies(stg, gsh, hb + te - 8)

        sums = [jnp.zeros((8, D_CONV), F32) for _ in range(3)]
        for r0 in range(0, te, LN_ROWS):
            rows = pl.ds(r0, LN_ROWS)
            hc = hc_ref[rows, :] if r0 < ts else hcn_ref[pl.ds(r0 - ts, LN_ROWS), :]
            hcc = hc - jnp.mean(hc, axis=-1, keepdims=True)
            rstd = lax.rsqrt(jnp.mean(hcc * hcc, axis=-1, keepdims=True) + LN_EPS)
            xh = hcc * rstd
            ln = xh * g_ref[...] + b_ref[...]
            sl_ = _sigmoid(ln)
            if r0 < ts:
                dyb = dy_ref[rows, D_POOL:]
            else:
                dyb = jnp.where(last, 0.0, dyn_ref[pl.ds(r0 - ts, LN_ROWS), D_POOL:])
            dln = dyb * (sl_ * (1.0 + ln * (1.0 - sl_)))
            dxh = dln * g_ref[...]
            dhc = rstd * (dxh - jnp.mean(dxh, axis=-1, keepdims=True)
                          - xh * jnp.mean(dxh * xh, axis=-1, keepdims=True))
            sth[rows, :] = dhc
            if r0 < ts:
                for n_, term in enumerate((dln * xh, dln, dhc)):
                    sums[n_] = sums[n_] + jnp.sum(term.reshape(LN_ROWS // 8, 8, D_CONV), axis=0)
        dg_ref[...] += jnp.sum(sums[0], axis=0, keepdims=True)
        db_ref[...] += jnp.sum(sums[1], axis=0, keepdims=True)
        dcb_ref[...] += jnp.sum(sums[2], axis=0, keepdims=True)

        _shifted_copies(sth, hsh, te - 8)
        for c0 in range(0, D_CONV, SUB_LANES):
            ln_ = pl.ds(c0, SUB_LANES)
            for r0 in range(0, ts, CONV_ROWS):
                rows = pl.ds(r0, CONV_ROWS)
                dglu = jnp.zeros((CONV_ROWS, SUB_LANES), F32)
                for k in range(K):
                    dglu = dglu + cw_ref[k:k + 1, ln_] * _rows_at(sth, hsh, K - 1 - k + r0, CONV_ROWS, ln_)
                bv = u_ref[rows, pl.ds(D_POOL + c0, SUB_LANES)]
                sg = _sigmoid(u_ref[rows, pl.ds(D_POOL + D_CONV + c0, SUB_LANES)])
                du_ref[rows, pl.ds(D_POOL + c0, SUB_LANES)] = (dglu * sg).astype(BF16)
                du_ref[rows, pl.ds(D_POOL + D_CONV + c0, SUB_LANES)] = (dglu * bv * sg * (1.0 - sg)).astype(BF16)
            for k in range(K):
                tap = jnp.zeros((8, SUB_LANES), F32)
                for r0 in range(0, ts, CONV_ROWS):
                    prod = sth[pl.ds(r0, CONV_ROWS), ln_] * _rows_at(stg, gsh, hb - (K - 1) + k + r0, CONV_ROWS, ln_)
                    tap = tap + jnp.sum(prod.reshape(CONV_ROWS // 8, 8, SUB_LANES), axis=0)
                dcw_ref[k:k + 1, ln_] += jnp.sum(tap, axis=0, keepdims=True)

    fix2 = lambda i: (0, 0)
    prev = lambda i: (jnp.maximum(i * nh - 1, 0), 0)
    nxt = lambda i: (jnp.minimum((i + 1) * nh, S // hb - 1), 0)
    return pl.pallas_call(
        body,
        out_shape=[jax.ShapeDtypeStruct((S, 3 * D_POOL), BF16),
                   jax.ShapeDtypeStruct((4, POOL_GROUP, POOL_GROUP), F32),
                   jax.ShapeDtypeStruct((1, D_POOL), F32),
                   jax.ShapeDtypeStruct((K, D_CONV), F32),
                   jax.ShapeDtypeStruct((1, D_CONV), F32),
                   jax.ShapeDtypeStruct((1, D_CONV), F32),
                   jax.ShapeDtypeStruct((1, D_CONV), F32)],
        grid=(n,),
        in_specs=[pl.BlockSpec((ts, 3 * D_POOL), lambda i: (i, 0)),
                  pl.BlockSpec((hb, 3 * D_POOL), prev),
                  pl.BlockSpec((hb, 3 * D_POOL), nxt),
                  pl.BlockSpec((ts, D_POOL), lambda i: (i, 0)),
                  pl.BlockSpec((ts, D_CONV), lambda i: (i, 0)),
                  pl.BlockSpec((hb, D_CONV), nxt),
                  pl.BlockSpec((ts, D_MODEL), lambda i: (i, 0)),
                  pl.BlockSpec((hb, D_MODEL), nxt),
                  pl.BlockSpec((4, POOL_GROUP, POOL_GROUP), lambda i: (0, 0, 0)),
                  pl.BlockSpec((1, D_POOL), fix2), pl.BlockSpec((K, D_CONV), fix2),
                  pl.BlockSpec((1, D_CONV), fix2), pl.BlockSpec((1, D_CONV), fix2)],
        out_specs=[pl.BlockSpec((ts, 3 * D_POOL), lambda i: (i, 0)),
                   pl.BlockSpec((4, POOL_GROUP, POOL_GROUP), lambda i: (0, 0, 0)),
                   pl.BlockSpec((1, D_POOL), fix2), pl.BlockSpec((K, D_CONV), fix2),
                   pl.BlockSpec((1, D_CONV), fix2), pl.BlockSpec((1, D_CONV), fix2), pl.BlockSpec((1, D_CONV), fix2)],
        scratch_shapes=[pltpu.VMEM((hb + ts + hb, D_CONV), F32), pltpu.VMEM((te, D_POOL), F32),
                        pltpu.VMEM((te, D_CONV), F32), pltpu.VMEM((7, hb + te - 8, D_CONV), F32),
                        pltpu.VMEM((7, te - 8, D_CONV), F32)],
        compiler_params=_cparams(("arbitrary",)),
        name="mixer_bwd",
    )(u, u, u, d, hc, hc, dycat, dycat, pool_w, pool_scale.reshape(1, D_POOL), conv_w, cln_g.reshape(1, D_CONV),
      cln_b.reshape(1, D_CONV))


_GELU_C = math.sqrt(2.0 / math.pi)


def _gelu_parts(x):
    inner = _GELU_C * (x + 0.044715 * x * x * x)
    th = jnp.tanh(inner)
    ge = 0.5 * x * (1.0 + th)
    dge = 0.5 * (1.0 + th) + 0.5 * x * (1.0 - th * th) * (_GELU_C * (1.0 + 3.0 * 0.044715 * x * x))
    return ge, dge


def _rows_back(ref, r, n, ln, shifts):
    ext = ref[pl.ds(r - 8, n + 8), ln]
    return [(pltpu.roll(ext, s, 0) if s else ext)[8:] for s in shifts]


def _rows_ahead(ref, r, n, ln, shifts):
    ext = ref[pl.ds(r, n + 8), ln]
    return [(pltpu.roll(ext, n + 8 - s, 0) if s else ext)[:n] for s in shifts]


def _ffn_act_fwd(gate, val, dw_w, dw_b, *, ts=256, tc=1408, name):
    S, F = gate.shape
    hb = FFN_HALO
    nh = ts // hb
    tc = _tile(F, tc)

    def body(g_ref, gh_ref, v_ref, w_ref, b_ref, h_ref, st):
        i = pl.program_id(0)
        st[pl.ds(0, hb), :] = jnp.where(i == 0, 0.0, gh_ref[...].astype(F32))
        st[pl.ds(hb, ts), :] = g_ref[...].astype(F32)
        for c0 in range(0, tc, SUB_LANES):
            ln = pl.ds(c0, SUB_LANES)
            w0, w1, w2, b = w_ref[0:1, ln], w_ref[1:2, ln], w_ref[2:3, ln], b_ref[:, ln]
            for r0 in range(0, ts, SUB_ROWS):
                taps = _rows_back(st, hb + r0, SUB_ROWS, ln, (2, 1, 0))
                gc = b + w0 * taps[0] + w1 * taps[1] + w2 * taps[2]
                ge, _ = _gelu_parts(gc)
                rows = pl.ds(r0, SUB_ROWS)
                h_ref[rows, ln] = (ge * v_ref[rows, ln].astype(F32)).astype(BF16)

    return pl.pallas_call(
        body,
        out_shape=jax.ShapeDtypeStruct((S, F), BF16),
        grid=(S // ts, F // tc),
        in_specs=[pl.BlockSpec((ts, tc), lambda i, j: (i, j)),
                  pl.BlockSpec((hb, tc), lambda i, j: (jnp.maximum(i * nh - 1, 0), j)),
                  pl.BlockSpec((ts, tc), lambda i, j: (i, j)),
                  pl.BlockSpec((3, tc), lambda i, j: (0, j)),
                  pl.BlockSpec((1, tc), lambda i, j: (0, j))],
        out_specs=pl.BlockSpec((ts, tc), lambda i, j: (i, j)),
        scratch_shapes=[pltpu.VMEM((hb + ts, tc), F32)],
        compiler_params=_cparams(("parallel", "parallel")),
        name=name,
    )(gate, gate, val, dw_w, dw_b.reshape(1, F))


def _ffn_act_bwd(gate, val, dh, dw_w, dw_b, *, ts=256, tc=1408, name):
    S, F = gate.shape
    hb = FFN_HALO
    nh = ts // hb
    n = S // ts
    te = ts + hb
    tc = _tile(F, tc)

    def body(g_ref, gp_ref, gn_ref, v_ref, vn_ref, dh_ref, dhn_ref, w_ref, b_ref,
             dg_ref, dv_ref, dw_ref, db_ref, st, sd):
        i = pl.program_id(1)
        first = i == 0
        last = i == n - 1

        @pl.when(first)
        def _():
            dw_ref[...] = jnp.zeros_like(dw_ref)
            db_ref[...] = jnp.zeros_like(db_ref)

        st[pl.ds(0, hb), :] = jnp.where(first, 0.0, gp_ref[...].astype(F32))
        st[pl.ds(hb, ts), :] = g_ref[...].astype(F32)
        st[pl.ds(hb + ts, hb), :] = jnp.where(last, 0.0, gn_ref[...].astype(F32))
        for c0 in range(0, tc, SUB_LANES):
            ln = pl.ds(c0, SUB_LANES)
            w0, w1, w2, b = w_ref[0:1, ln], w_ref[1:2, ln], w_ref[2:3, ln], b_ref[:, ln]
            db_acc = jnp.zeros((8, SUB_LANES), F32)
            dw_acc = [jnp.zeros((8, SUB_LANES), F32) for _ in range(3)]
            for r0 in range(0, te, SUB_ROWS):
                rc = min(SUB_ROWS, te - r0)
                taps = _rows_back(st, hb + r0, rc, ln, (2, 1, 0))
                gc = b + w0 * taps[0] + w1 * taps[1] + w2 * taps[2]
                ge, dge = _gelu_parts(gc)
                if r0 < ts:
                    rows = pl.ds(r0, rc)
                    val, dh = v_ref[rows, ln].astype(F32), dh_ref[rows, ln].astype(F32)
                else:
                    val = jnp.where(last, 0.0, vn_ref[:, ln].astype(F32)[0:rc])
                    dh = jnp.where(last, 0.0, dhn_ref[:, ln].astype(F32)[0:rc])
                dgc = dh * val * dge
                sd[pl.ds(r0, rc), ln] = dgc
                if r0 < ts:
                    dv_ref[rows, ln] = (dh * ge).astype(BF16)
                    db_acc = db_acc + jnp.sum(dgc.reshape(rc // 8, 8, SUB_LANES), axis=0)
                    for k in range(3):
                        dw_acc[k] = dw_acc[k] + jnp.sum((dgc * taps[k]).reshape(rc // 8, 8, SUB_LANES), axis=0)
            db_ref[:, ln] += jnp.sum(db_acc, axis=0, keepdims=True)
            for k in range(3):
                dw_ref[k:k + 1, ln] += jnp.sum(dw_acc[k], axis=0, keepdims=True)
            for r0 in range(0, ts, SUB_ROWS):
                ahead = _rows_ahead(sd, r0, SUB_ROWS, ln, (2, 1, 0))
                dg_ref[pl.ds(r0, SUB_ROWS), ln] = (w0 * ahead[0] + w1 * ahead[1] + w2 * ahead[2]).astype(BF16)

    cur = lambda j, i: (i, j)
    prev = lambda j, i: (jnp.maximum(i * nh - 1, 0), j)
    nxt = lambda j, i: (jnp.minimum((i + 1) * nh, S // hb - 1), j)
    return pl.pallas_call(
        body,
        out_shape=[jax.ShapeDtypeStruct((S, F), BF16), jax.ShapeDtypeStruct((S, F), BF16),
                   jax.ShapeDtypeStruct((3, F), F32), jax.ShapeDtypeStruct((1, F), F32)],
        grid=(F // tc, n),
        in_specs=[pl.BlockSpec((ts, tc), cur), pl.BlockSpec((hb, tc), prev), pl.BlockSpec((hb, tc), nxt),
                  pl.BlockSpec((ts, tc), cur), pl.BlockSpec((hb, tc), nxt),
                  pl.BlockSpec((ts, tc), cur), pl.BlockSpec((hb, tc), nxt),
                  pl.BlockSpec((3, tc), lambda j, i: (0, j)), pl.BlockSpec((1, tc), lambda j, i: (0, j))],
        out_specs=[pl.BlockSpec((ts, tc), cur), pl.BlockSpec((ts, tc), cur),
                   pl.BlockSpec((3, tc), lambda j, i: (0, j)), pl.BlockSpec((1, tc), lambda j, i: (0, j))],
        scratch_shapes=[pltpu.VMEM((hb + ts + hb, tc), F32), pltpu.VMEM((te, tc), F32)],
        compiler_params=_cparams(("parallel", "arbitrary")),
        name=name,
    )(gate, gate, gate, val, val, dh, dh, dw_w, dw_b.reshape(1, F))


def _loss_ln_bwd(z, ln_g, ln_b, target, *, ts=256, name):
    S, D = z.shape

    def body(z_ref, g_ref, b_ref, t_ref, dz_ref, dzb_ref, dg_ref, db_ref, loss_ref):
        i = pl.program_id(0)

        @pl.when(i == 0)
        def _():
            dg_ref[...] = jnp.zeros_like(dg_ref)
            db_ref[...] = jnp.zeros_like(db_ref)
            loss_ref[...] = jnp.zeros_like(loss_ref)

        dg_acc = jnp.zeros((8, D), F32)
        db_acc = jnp.zeros((8, D), F32)
        loss_acc = jnp.zeros((1, 1), F32)
        for r0 in range(0, ts, LN_ROWS):
            rows = pl.ds(r0, LN_ROWS)
            zt = z_ref[rows, :]
            err = _layer_norm_rows(zt, g_ref[...], b_ref[...]) - t_ref[rows, :]
            loss_acc = loss_acc + 0.5 * jnp.sum(jnp.mean(err * err, axis=-1, keepdims=True), keepdims=True)
            do = err * (1.0 / D)
            dz, xh = _ln_bwd_rows(zt, g_ref[...], do)
            dg_acc = dg_acc + jnp.sum((do * xh).reshape(LN_ROWS // 8, 8, D), axis=0)
            db_acc = db_acc + jnp.sum(do.reshape(LN_ROWS // 8, 8, D), axis=0)
            dz_ref[rows, :] = dz
            dzb_ref[rows, :] = dz.astype(BF16)
        dg_ref[...] += jnp.sum(dg_acc, axis=0, keepdims=True)
        db_ref[...] += jnp.sum(db_acc, axis=0, keepdims=True)
        loss_ref[...] += loss_acc

    row = lambda i: (i, 0)
    fix = lambda i: (0, 0)
    return pl.pallas_call(
        body,
        out_shape=[jax.ShapeDtypeStruct((S, D), F32), jax.ShapeDtypeStruct((S, D), BF16),
                   jax.ShapeDtypeStruct((1, D), F32), jax.ShapeDtypeStruct((1, D), F32),
                   jax.ShapeDtypeStruct((8, 128), F32)],
        grid=(S // ts,),
        in_specs=[pl.BlockSpec((ts, D), row), pl.BlockSpec((1, D), fix), pl.BlockSpec((1, D), fix),
                  pl.BlockSpec((ts, D), row)],
        out_specs=[pl.BlockSpec((ts, D), row), pl.BlockSpec((ts, D), row), pl.BlockSpec((1, D), fix),
                   pl.BlockSpec((1, D), fix), pl.BlockSpec((8, 128), fix)],
        compiler_params=_cparams(("arbitrary",)),
        name=name,
    )(z, ln_g.reshape(1, D), ln_b.reshape(1, D), target)


def _ple_bwd(dz, gate, proj, *, ts=256, name):
    S, D = dz.shape

    def body(dz_ref, g_ref, p_ref, ds_ref, dp_ref, db_ref):
        @pl.when(pl.program_id(0) == 0)
        def _():
            db_ref[...] = jnp.zeros_like(db_ref)

        db_acc = jnp.zeros((8, D), F32)
        for r0 in range(0, ts, LN_ROWS):
            rows = pl.ds(r0, LN_ROWS)
            dzt = dz_ref[rows, :]
            g = g_ref[rows, :]
            ds = dzt * p_ref[rows, :] * g * (1.0 - g)
            ds_ref[rows, :] = ds.astype(BF16)
            dp_ref[rows, :] = (dzt * g).astype(BF16)
            db_acc = db_acc + jnp.sum(ds.reshape(LN_ROWS // 8, 8, D), axis=0)
        db_ref[...] += jnp.sum(db_acc, axis=0, keepdims=True)

    row = lambda i: (i, 0)
    return pl.pallas_call(
        body,
        out_shape=[jax.ShapeDtypeStruct((S, D), BF16), jax.ShapeDtypeStruct((S, D), BF16),
                   jax.ShapeDtypeStruct((1, D), F32)],
        grid=(S // ts,),
        in_specs=[pl.BlockSpec((ts, D), row)] * 3,
        out_specs=[pl.BlockSpec((ts, D), row), pl.BlockSpec((ts, D), row), pl.BlockSpec((1, D), lambda i: (0, 0))],
        compiler_params=_cparams(("arbitrary",)),
        name=name,
    )(dz, gate, proj)


HEAD_PAIR = 2 * HEAD_DIM


ATT_ROWS = 32
ATT_SCALE = HEAD_DIM ** -0.5


def _softmax_piece(scores, bias, qb):
    s = scores + bias
    kpos = qb * Q_BLOCK + lax.broadcasted_iota(jnp.int32, (1, KV_SPAN), 1)
    s = jnp.where(kpos >= KV_PAD, s, NEG_INF)
    e = jnp.exp(s - jnp.max(s, axis=-1, keepdims=True))
    return e * (1.0 / jnp.sum(e, axis=-1, keepdims=True))


def _pad_keys(qb, k_ref, v_ref, kp, vp):
    @pl.when(qb == 0)
    def _():
        kp[pl.ds(0, KV_PAD), :] = jnp.zeros((KV_PAD, HEAD_PAIR), BF16)
        vp[pl.ds(0, KV_PAD), :] = jnp.zeros((KV_PAD, HEAD_PAIR), BF16)
        kp[pl.ds(KV_PAD, k_ref.shape[0]), :] = k_ref[...]
        vp[pl.ds(KV_PAD, v_ref.shape[0]), :] = v_ref[...]


def _attn_fwd(qkv, bias):
    S = qkv.shape[0]
    nhp = N_HEADS // 2

    def body(q_ref, k_ref, v_ref, b_ref, o_ref, kp, vp, p_scr):
        qb = pl.program_id(1)
        _pad_keys(qb, k_ref, v_ref, kp, vp)
        span = pl.ds(pl.multiple_of(qb * Q_BLOCK, Q_BLOCK), KV_SPAN)
        kc, vc = kp[span, :], vp[span, :]
        qt = q_ref[...] * ATT_SCALE
        first = lax.broadcasted_iota(jnp.int32, (1, HEAD_PAIR), 1) < HEAD_DIM
        scores = [_bdot(jnp.where(first if j == 0 else ~first, qt, jnp.zeros_like(qt)), kc, NT) for j in range(2)]
        outs = []
        for j in range(2):
            for r0 in range(0, Q_BLOCK, ATT_ROWS):
                rows = pl.ds(r0, ATT_ROWS)
                p_scr[j, rows, :] = _softmax_piece(scores[j][r0:r0 + ATT_ROWS], b_ref[j, rows, :], qb).astype(BF16)
            outs.append(_bdot(p_scr[j], vc))
        o_ref[...] = jnp.where(first, outs[0], outs[1]).astype(BF16)

    return pl.pallas_call(
        body,
        out_shape=jax.ShapeDtypeStruct((S, D_MODEL), BF16),
        grid=(nhp, S // Q_BLOCK),
        in_specs=[pl.BlockSpec((Q_BLOCK, HEAD_PAIR), lambda h, i: (i, h)),
                  pl.BlockSpec((S, HEAD_PAIR), lambda h, i: (0, nhp + h)),
                  pl.BlockSpec((S, HEAD_PAIR), lambda h, i: (0, 2 * nhp + h)),
                  pl.BlockSpec((2, Q_BLOCK, KV_SPAN), lambda h, i: (h, 0, 0))],
        out_specs=pl.BlockSpec((Q_BLOCK, HEAD_PAIR), lambda h, i: (i, h)),
        scratch_shapes=[pltpu.VMEM((KV_PAD + S, HEAD_PAIR), BF16), pltpu.VMEM((KV_PAD + S, HEAD_PAIR), BF16),
                        pltpu.VMEM((2, Q_BLOCK, KV_SPAN), BF16)],
        compiler_params=_cparams(("parallel", "arbitrary")),
        name="attn_fwd",
    )(qkv, qkv, qkv, bias)


def _attn_bwd(qkv, bias, do):
    S = qkv.shape[0]
    nhp = N_HEADS // 2
    nq = S // Q_BLOCK
    scale = HEAD_DIM ** -0.5

    def body(q_ref, k_ref, v_ref, b_ref, do_ref, dq_ref, dk_ref, dv_ref, db_ref, kp, vp, dka, dva,
             p_scr, ds_scr):
        qb = pl.program_id(1)
        _pad_keys(qb, k_ref, v_ref, kp, vp)

        @pl.when(qb == 0)
        def _():
            dka[...] = jnp.zeros_like(dka)
            dva[...] = jnp.zeros_like(dva)
            db_ref[...] = jnp.zeros_like(db_ref)

        span = pl.ds(pl.multiple_of(qb * Q_BLOCK, Q_BLOCK), KV_SPAN)
        kc, vc = kp[span, :], vp[span, :]
        qt, dot = q_ref[...] * ATT_SCALE, do_ref[...]
        first = lax.broadcasted_iota(jnp.int32, (1, HEAD_PAIR), 1) < HEAD_DIM
        dqs = []
        qs = [jnp.where(first if j == 0 else ~first, qt, jnp.zeros_like(qt)) for j in range(2)]
        dos = [jnp.where(first if j == 0 else ~first, dot, jnp.zeros_like(dot)) for j in range(2)]
        scores = [_bdot(qs[j], kc, NT) for j in range(2)]
        dps = [_bdot(dos[j], vc, NT) for j in range(2)]
        for j in range(2):
            qj, doj = qs[j], dos[j]
            for r0 in range(0, Q_BLOCK, ATT_ROWS):
                rows = pl.ds(r0, ATT_ROWS)
                p = _softmax_piece(scores[j][r0:r0 + ATT_ROWS], b_ref[j, rows, :], qb)
                dp = dps[j][r0:r0 + ATT_ROWS]
                ds = p * (dp - jnp.sum(p * dp, axis=-1, keepdims=True))
                db_ref[j, rows, :] += ds
                p_scr[j, rows, :] = p.astype(BF16)
                ds_scr[j, rows, :] = ds.astype(BF16)
            dva[span, :] += _bdot(p_scr[j], doj, TN)
            dqs.append(_bdot(ds_scr[j], kc))
            dka[span, :] += _bdot(ds_scr[j], qj, TN)
        dq_ref[...] = (scale * jnp.where(first, dqs[0], dqs[1])).astype(BF16)

        @pl.when(qb == nq - 1)
        def _():
            dk_ref[...] = dka[pl.ds(KV_PAD, S), :].astype(BF16)
            dv_ref[...] = dva[pl.ds(KV_PAD, S), :].astype(BF16)

    blk = pl.BlockSpec((Q_BLOCK, HEAD_PAIR), lambda h, i: (i, h))
    col = pl.BlockSpec((S, HEAD_PAIR), lambda h, i: (0, h))
    bsp = pl.BlockSpec((2, Q_BLOCK, KV_SPAN), lambda h, i: (h, 0, 0))
    return pl.pallas_call(
        body,
        out_shape=[jax.ShapeDtypeStruct((S, D_MODEL), BF16)] * 3
        + [jax.ShapeDtypeStruct((N_HEADS, Q_BLOCK, KV_SPAN), F32)],
        grid=(nhp, nq),
        in_specs=[blk, pl.BlockSpec((S, HEAD_PAIR), lambda h, i: (0, nhp + h)),
                  pl.BlockSpec((S, HEAD_PAIR), lambda h, i: (0, 2 * nhp + h)), bsp, blk],
        out_specs=[blk, col, col, bsp],
        scratch_shapes=[pltpu.VMEM((KV_PAD + S, HEAD_PAIR), BF16), pltpu.VMEM((KV_PAD + S, HEAD_PAIR), BF16),
                        pltpu.VMEM((KV_PAD + S, HEAD_PAIR), F32), pltpu.VMEM((KV_PAD + S, HEAD_PAIR), F32),
                        pltpu.VMEM((2, Q_BLOCK, KV_SPAN), BF16), pltpu.VMEM((2, Q_BLOCK, KV_SPAN), BF16)],
        compiler_params=_cparams(("parallel", "arbitrary")),
        name="attn_bwd",
    )(qkv, qkv, qkv, bias, do)


N_DIST = BAND + CHUNK - 1
N_FAR = KV_PAD + CHUNK - MAX_REL


def _shear_rows(x, towards_right):
    row = lax.broadcasted_iota(jnp.int32, (Q_BLOCK, 1), 0)
    for bit in range(Q_BLOCK.bit_length() - 1):
        step = 1 << bit
        x = jnp.where((row & step) != 0, pltpu.roll(x, step if towards_right else KV_SPAN - step, 1), x)
    return x


def _bias_blocks(rel_bias):
    H = rel_bias.shape[0]
    e = jnp.concatenate([jnp.broadcast_to(rel_bias[:, 2 * MAX_REL:], (H, N_FAR)),
                         jnp.flip(rel_bias[:, 2 * MAX_REL - (N_DIST - N_FAR):2 * MAX_REL], axis=1),
                         jnp.zeros((H, KV_SPAN - N_DIST), F32)], axis=1).reshape(H, 1, KV_SPAN)

    def body(e_ref, o_ref):
        first = pltpu.roll(jnp.broadcast_to(e_ref[...], (Q_BLOCK, KV_SPAN)), KV_SPAN - (CHUNK - 1), 1)
        x = _shear_rows(first, True)
        row = lax.broadcasted_iota(jnp.int32, (Q_BLOCK, 1), 0)
        chunk0 = row - (row & (CHUNK - 1))
        k = lax.broadcasted_iota(jnp.int32, (1, KV_SPAN), 1)
        o_ref[...] = jnp.where((k >= chunk0) & (k < chunk0 + BAND), x, NEG_INF)

    return pl.pallas_call(
        body,
        out_shape=jax.ShapeDtypeStruct((H, Q_BLOCK, KV_SPAN), F32),
        grid=(H,),
        in_specs=[pl.BlockSpec((None, 1, KV_SPAN), lambda h: (h, 0, 0))],
        out_specs=pl.BlockSpec((None, Q_BLOCK, KV_SPAN), lambda h: (h, 0, 0)),
        compiler_params=_cparams(("parallel",)),
        name="bias_blocks",
    )(e)


def _bias_blocks_grad(dblk):
    H = dblk.shape[0]

    def body(d_ref, o_ref):
        x = pltpu.roll(_shear_rows(d_ref[...], False), CHUNK - 1, 1)
        de = jnp.sum(x, axis=0, keepdims=True)
        lane = lax.broadcasted_iota(jnp.int32, de.shape, 1)
        far = jnp.sum(jnp.where(lane < N_FAR, de, 0.0), axis=-1, keepdims=True)
        o_ref[...] = jnp.where(lane == 0, far, jnp.where(lane < N_FAR, 0.0, de))

    de = pl.pallas_call(
        body,
        out_shape=jax.ShapeDtypeStruct((H, 1, KV_SPAN), F32),
        grid=(H,),
        in_specs=[pl.BlockSpec((None, Q_BLOCK, KV_SPAN), lambda h: (h, 0, 0))],
        out_specs=pl.BlockSpec((None, 1, KV_SPAN), lambda h: (h, 0, 0)),
        compiler_params=_cparams(("parallel",)),
        name="bias_grad_sum",
    )(dblk).reshape(H, KV_SPAN)
    near = jnp.flip(de[:, N_FAR:N_DIST], axis=1)
    return jnp.concatenate([jnp.zeros((H, 2 * MAX_REL - (N_DIST - N_FAR)), F32), near, de[:, 0:1]], axis=1)


def _ffn_forward(r1, r1b, p_l, w, l, ready):
    ready(f"up{l}", r1b)
    up_g = _mm_rows([(r1b, w["ffn_up_t"][l], True, (0, 2))], out_dtype=BF16, name=f"ffn_up_g{l}")
    up_v = _mm_rows([(r1b, w["ffn_up_t"][l], True, (1, 2))], out_dtype=BF16, name=f"ffn_up_v{l}")
    h = _ffn_act_fwd(up_g, up_v, w["ffn_dw_w"][l], w["ffn_dw_b"][l], name=f"ffn_act{l}")
    ready(f"dn{l}", h)
    z2, r2, r2b, gate, proj = _proj_ln(r1, h, w["ffn_w_down"][l], w["ln_ffn_g"][l], w["ln_ffn_b"][l],
                                       ple=(w["ple_w_gate"][l], w["ple_b_gate"][l], p_l, w["ple_w_proj"][l]),
                                       name=f"ffn_down_ln{l}")
    return dict(r1b=r1b, up_g=up_g, up_v=up_v, h=h, z2=z2, gate=gate, proj=proj), r2, r2b


def _ffn_backward(sv, dz2, dz2b, p_l, w, l, grads, ln_bwd, emit):
    r1b = sv["r1b"]
    ds, dproj, db_gate = _ple_bwd(dz2, sv["gate"], sv["proj"], name=f"ple_bwd{l}")
    dh = _mm_rows([(dz2b, w["ffn_w_down"][l], True, WHOLE)], out_dtype=BF16, name=f"ffn_dh{l}")
    dgate, dval, d_dw_w, d_dw_b = _ffn_act_bwd(sv["up_g"], sv["up_v"], dh, w["ffn_dw_w"][l], w["ffn_dw_b"][l],
                                               name=f"ffn_act_bwd{l}")
    grads["ffn_w_down"][l] = _wgrad(sv["h"], dz2b, tm=1408, name=f"d_ffn_w_down{l}")
    d_up_g = _wgrad(dgate, r1b, tm=1408, part=(0, 2), name=f"d_ffn_up_g{l}")
    grads["ffn_up_t"][l] = _wgrad(dval, r1b, tm=1408, part=(1, 2), into=d_up_g, name=f"d_ffn_up_v{l}")
    grads["ple_w_gate"][l] = _wgrad(r1b, ds, name=f"d_ple_w_gate{l}")
    grads["ple_w_proj"][l] = _wgrad(p_l, dproj, piece=D_MODEL // N_DEV, name=f"d_ple_w_proj{l}")
    grads["ffn_dw_w"][l] = d_dw_w
    grads["ffn_dw_b"][l] = d_dw_b[0]
    grads["ple_b_gate"][l] = db_gate[0]
    return _mm_rows([(ds, w["ple_w_gate"][l], True, WHOLE), (dgate, w["ffn_up_t"][l], False, (0, 2)),
                     (dval, w["ffn_up_t"][l], False, (1, 2))], add=dz2, add_scale=ALPHA, ln_bwd=ln_bwd, dep=emit(),
                    name=f"dr1_{l}")


def _local_step(x, p, target, w, ready=lambda group, after: None, emit=lambda group, grads: None):
    grads = {k: [None, None] for k in ("ffn_w_down", "ffn_up_t", "ple_w_gate", "ple_w_proj", "ffn_dw_w",
                                       "ffn_dw_b", "ple_b_gate", "ln_ffn_g", "ln_ffn_b", "ln_mix_g", "ln_mix_b")}

    xb, pb = x.astype(BF16), p.astype(BF16)
    ready("mix", None)
    u = _mm_rows([(xb, w["mix_w_in_t"], True, WHOLE)], name="mix_in")
    ycat, dpool, hconv = _mixer_fwd(u, w["pool_w"], w["pool_scale"], w["conv_dw_w"], w["conv_dw_b"], w["conv_ln_g"],
                                    w["conv_ln_b"])
    ready("mixo", ycat)
    z1, r1, r1b = _proj_ln(x, ycat, w["mix_w_out"], w["ln_mix_g"][0], w["ln_mix_b"][0], name="mix_out_ln")
    sv0, r2, r2b = _ffn_forward(r1, r1b, pb[0], w, 0, ready)

    ready("attn", r2b)
    qkv = _mm_rows([(r2b, w["attn_w_qkv"], False, WHOLE)], out_dtype=BF16, name="attn_qkv")
    bias = _bias_blocks(w["attn_rel_bias"])
    attn = _attn_fwd(qkv, bias)
    z3, r3, r3b = _proj_ln(r2, attn, w["attn_w_o"], w["ln_mix_g"][1], w["ln_mix_b"][1], name="attn_out_ln")
    sv1, _, _ = _ffn_forward(r3, r3b, pb[1], w, 1, ready)

    dz4, dz4b, grads["ln_ffn_g"][1], grads["ln_ffn_b"][1], loss = _loss_ln_bwd(
        sv1["z2"], w["ln_ffn_g"][1], w["ln_ffn_b"][1], target, name="loss_ln_bwd")
    dz3, dz3b, grads["ln_mix_g"][1], grads["ln_mix_b"][1] = _ffn_backward(
        sv1, dz4, dz4b, pb[1], w, 1, grads, (z3, w["ln_mix_g"][1]), lambda: emit("ffn1", grads))
    grads["attn_w_o"] = _wgrad(attn, dz3b, name="d_attn_w_o")
    dattn = _mm_rows([(dz3b, w["attn_w_o"], True, WHOLE)], out_dtype=BF16, name="d_attn")
    dq, dk, dv, dbias = _attn_bwd(qkv, bias, dattn)
    grads["attn_rel_bias"] = _bias_blocks_grad(dbias)
    dqkv = jnp.concatenate([dq, dk, dv], axis=1)
    grads["attn_w_qkv"] = _wgrad(r2b, dqkv, tn=768, piece=3 * D_MODEL // N_DEV, name="d_attn_w_qkv")
    dz2, dz2b, grads["ln_ffn_g"][0], grads["ln_ffn_b"][0] = _mm_rows(
        [(dqkv, w["attn_w_qkv"], True, WHOLE)], add=dz3, add_scale=ALPHA, ln_bwd=(sv0["z2"], w["ln_ffn_g"][0]),
        dep=emit("attn", grads), name="dr2")
    dz1, dz1b, grads["ln_mix_g"][0], grads["ln_mix_b"][0] = _ffn_backward(
        sv0, dz2, dz2b, pb[0], w, 0, grads, (z1, w["ln_mix_g"][0]), lambda: emit("ffn0", grads))
    grads["mix_w_out"] = _wgrad(ycat, dz1b, name="d_mix_w_out")
    dycat = _mm_rows([(dz1b, w["mix_w_out"], True, WHOLE)], name="d_ycat")
    du, g_pw, g_ps, g_cw, g_cb, g_cg, g_cbb = _mixer_bwd(u, dpool, hconv, dycat, w["pool_w"], w["pool_scale"],
                                                         w["conv_dw_w"], w["conv_ln_g"], w["conv_ln_b"])
    grads["mix_w_in_t"] = _wgrad(du, xb, name="d_mix_w_in")
    grads.update(pool_w=g_pw, pool_scale=g_ps[0], conv_dw_w=g_cw, conv_dw_b=g_cb[0], conv_ln_g=g_cg[0],
                 conv_ln_b=g_cbb[0])
    for kname in ("ln_ffn_g", "ln_ffn_b", "ln_mix_g", "ln_mix_b"):
        grads[kname] = [a[0] for a in grads[kname]]
    grad_x = _mm_rows([(du, w["mix_w_in_t"], False, WHOLE)], add=dz1, add_scale=ALPHA, dep=emit("mix", grads),
                      name="grad_x")
    return loss[0, 0], grad_x, grads


_HBM = pl.BlockSpec(memory_space=pltpu.HBM)
_SEM = pl.BlockSpec(memory_space=pltpu.SEMAPHORE)
_EFFECT = pltpu.SideEffectType.DATAFLOW_SIDE_EFFECTING


def _slot(ref, place, shape, k):
    if place in ("stack", "pieces"):
        return ref.at[k]
    ax = place[1]
    n = shape[ax]
    return ref.at[(slice(None),) * ax + (pl.ds(pl.multiple_of(k * n, n), n),)]


def _result_shape(buf, place):
    if place == "stack":
        return (N_DEV,) + buf.shape
    if place == "pieces":
        return buf.shape
    return tuple(s * N_DEV if i == place[1] else s for i, s in enumerate(buf.shape))


def _peers(x, y, c):
    for d in range(1, N_DEV):
        px, py, pc = x ^ ((d >> 2) & 1), y ^ ((d >> 1) & 1), c ^ (d & 1)
        yield d, (px, py, pc), 4 * px + 2 * py + pc


def _exchange_start(bufs, places, after, *, name):
    nb = len(bufs)
    lands = [lax.empty(_result_shape(b, p_), b.dtype) for b, p_ in zip(bufs, places)]
    has_after = after is not None

    def body(*refs):
        srcs, dsts = refs[:nb], refs[nb:2 * nb]
        outs = refs[2 * nb + has_after:]
        send_sems, recv_sems, token = outs[0], outs[1], outs[2 + 2 * nb]
        x, y, c = lax.axis_index("x"), lax.axis_index("y"), lax.axis_index("c")
        me = 4 * x + 2 * y + c
        for b in range(nb):
            for d, dev, peer in _peers(x, y, c):
                pltpu.make_async_remote_copy(
                    src_ref=srcs[b].at[peer] if places[b] == "pieces" else srcs[b],
                    dst_ref=_slot(dsts[b], places[b], bufs[b].shape, me),
                    send_sem=send_sems.at[b * N_DEV + d], recv_sem=recv_sems.at[b * N_DEV + d],
                    device_id=dev, device_id_type=pl.DeviceIdType.MESH).start()
            pltpu.make_async_copy(srcs[b].at[me] if places[b] == "pieces" else srcs[b],
                                  _slot(dsts[b], places[b], bufs[b].shape, me), recv_sems.at[b * N_DEV]).start()
        token[...] = jnp.zeros_like(token)

    sems = pltpu.SemaphoreType.DMA((nb * N_DEV,))
    ins = [pltpu.with_memory_space_constraint(a, pltpu.HBM) for a in list(bufs) + lands]
    out = pl.pallas_call(
        body,
        out_shape=(sems, sems, *[pltpu.HBM(a.shape, a.dtype) for a in ins], jax.ShapeDtypeStruct((8, 128), F32)),
        in_specs=[_HBM] * (2 * nb) + ([pl.BlockSpec(memory_space=pl.ANY)] if has_after else []),
        out_specs=(_SEM, _SEM, *[_HBM] * (2 * nb), pl.BlockSpec(memory_space=pltpu.VMEM)),
        input_output_aliases={i: 2 + i for i in range(2 * nb)},
        compiler_params=pltpu.CompilerParams(has_side_effects=_EFFECT),
        name=name,
    )(*ins, *([after] if has_after else []))
    return dict(send=out[0], recv=out[1], srcs=out[2:2 + nb], lands=out[2 + nb:2 + 2 * nb], token=out[-1],
                places=places)


def _exchange_wait(h, after, *, name):
    nb = len(h["srcs"])
    places = h["places"]
    shapes = [a.shape for a in h["srcs"]]

    def body(*refs):
        srcs, dsts, send_sems, recv_sems = refs[:nb], refs[nb:2 * nb], refs[2 * nb], refs[2 * nb + 1]
        x, y, c = lax.axis_index("x"), lax.axis_index("y"), lax.axis_index("c")
        me = 4 * x + 2 * y + c
        for b in range(nb):
            pieces = places[b] == "pieces"
            for d, dev, peer in _peers(x, y, c):
                cp = pltpu.make_async_remote_copy(
                    src_ref=srcs[b].at[peer] if pieces else srcs[b],
                    dst_ref=_slot(dsts[b], places[b], shapes[b], peer),
                    send_sem=send_sems.at[b * N_DEV + d], recv_sem=recv_sems.at[b * N_DEV + d],
                    device_id=dev, device_id_type=pl.DeviceIdType.MESH)
                cp.wait_send()
                cp.wait_recv()
            pltpu.make_async_copy(srcs[b].at[me] if pieces else srcs[b], _slot(dsts[b], places[b], shapes[b], me),
                                  recv_sems.at[b * N_DEV]).wait()

    ins = list(h["srcs"]) + list(h["lands"])
    out = pl.pallas_call(
        body,
        out_shape=tuple(pltpu.HBM(a.shape, a.dtype) for a in ins),
        in_specs=[_HBM] * (2 * nb) + [_SEM, _SEM, pl.BlockSpec(memory_space=pl.ANY)],
        out_specs=tuple([_HBM] * (2 * nb)),
        input_output_aliases={i: i for i in range(2 * nb)},
        compiler_params=pltpu.CompilerParams(has_side_effects=_EFFECT),
        name=name,
    )(*ins, h["send"], h["recv"], after)
    return out[nb:]


def _adamw(recv, w, m, v, *, layer=0, into=None, name):
    L, R, C = w.shape
    tr = R
    for cand in (512, 256, 128, 64, 32, 16):
        if R % cand == 0 and cand * C * 4 <= 2 * 1024 * 1024:
            tr = cand
            break
    c1 = 1.0 - ADAM_B1 ** ADAM_STEP
    c2 = 1.0 - ADAM_B2 ** ADAM_STEP

    def body(r_ref, w_ref, m_ref, v_ref, *rest):
        g_ref, d_ref, mo_ref, vo_ref = rest[-4:]
        g = r_ref[0].astype(F32)
        for i in range(1, N_DEV):
            g = g + r_ref[i].astype(F32)
        m_new = ADAM_B1 * m_ref[...] + (1.0 - ADAM_B1) * g
        v_new = ADAM_B2 * v_ref[...] + (1.0 - ADAM_B2) * (g * g)
        m_hat = m_new / c1
        v_hat = v_new / c2
        g_ref[...] = g
        d_ref[...] = -ADAM_LR * (m_hat / (jnp.sqrt(v_hat) + ADAM_EPS) + ADAM_WD * w_ref[...])
        mo_ref[...] = m_new
        vo_ref[...] = v_new

    row = pl.BlockSpec((None, tr, C), lambda i: (layer, i, 0))
    others = [] if into is None else list(into)
    return pl.pallas_call(
        body,
        out_shape=[jax.ShapeDtypeStruct((L, R, C), F32)] * 4,
        grid=(R // tr,),
        in_specs=[pl.BlockSpec((N_DEV, tr, C), lambda i: (0, i, 0)), row, row, row]
        + [pl.BlockSpec(memory_space=pl.ANY)] * len(others),
        out_specs=[row] * 4,
        input_output_aliases={4 + k: k for k in range(len(others))},
        compiler_params=_cparams(("parallel",)),
        name=name,
    )(recv, w, m, v, *others)


_TRANSPOSED = ("mix_w_in", "ffn_w_up")


def _ffn_groups(l):
    return ((f"up{l}", (("ffn_w_up", l, BF16, ("axis", 0)), ("ffn_dw_w", l, F32, "stack"))),
            (f"dn{l}", (("ffn_w_down", l, BF16, ("axis", 0)), ("ple_w_gate", l, BF16, ("axis", 0)),
                        ("ple_w_proj", l, BF16, ("axis", 1)))))


_GATHER_GROUPS = (
    ("mix", (("mix_w_in", 0, BF16, ("axis", 0)), ("conv_dw_w", 0, F32, "stack"))),
    ("mixo", (("mix_w_out", 0, BF16, ("axis", 0)),)),
    *_ffn_groups(0),
    ("attn", (("attn_w_qkv", 0, BF16, ("axis", 1)), ("attn_w_o", 0, BF16, ("axis", 0)))),
    *_ffn_groups(1))
_SHARDED = ("mix_w_in", "conv_dw_w", "mix_w_out", "attn_w_qkv", "attn_w_o", "ffn_w_up", "ffn_dw_w", "ffn_w_down",
            "ple_w_gate", "ple_w_proj")
_REPLICATED = ("pool_w", "pool_scale", "conv_dw_b", "conv_ln_g", "conv_ln_b", "attn_rel_bias", "ln_mix_g",
               "ln_mix_b", "ffn_dw_b", "ple_b_gate", "ln_ffn_g", "ln_ffn_b")


def _pack_rows(parts, row_mult, dtype):
    lead = parts[0].shape[:-1]
    flat = jnp.concatenate([a.astype(dtype) for a in parts], axis=-1)
    n = flat.shape[-1]
    unit = row_mult * LANES
    padded = -(-n // unit) * unit
    flat = jnp.pad(flat, [(0, 0)] * len(lead) + [(0, padded - n)])
    return flat.reshape(lead + (padded // LANES, LANES))


def _unpack(flat2d, shapes):
    flat = flat2d.reshape(-1)
    out, o = [], 0
    for s in shapes:
        n = math.prod(s)
        out.append(flat[o:o + n].reshape(s))
        o += n
    return out


def _full_from_shards(g, axis):
    parts = jnp.moveaxis(g, 0, axis)
    shp = list(g.shape[1:])
    shp[axis] *= g.shape[0]
    return parts.reshape(shp)


def _pieces_from_full(full, axis, k=N_DEV):
    shp = list(full.shape)
    n = shp[axis] // k
    t = full.reshape(shp[:axis] + [k, n] + shp[axis + 1:])
    return jnp.moveaxis(t, axis, 0)


def kernel(x, p, mix_w_in, pool_w, pool_scale, conv_dw_w, conv_dw_b, conv_ln_g, conv_ln_b, mix_w_out, attn_w_qkv, attn_rel_bias, attn_w_o, ln_mix_g, ln_mix_b, ffn_w_up, ffn_dw_w, ffn_dw_b, ffn_w_down, ple_w_proj, ple_w_gate, ple_b_gate, ln_ffn_g, ln_ffn_b, loss_target, m_mix_w_in, m_pool_w, m_pool_scale, m_conv_dw_w, m_conv_dw_b, m_conv_ln_g, m_conv_ln_b, m_mix_w_out, m_attn_w_qkv, m_attn_rel_bias, m_attn_w_o, m_ln_mix_g, m_ln_mix_b, m_ffn_w_up, m_ffn_dw_w, m_ffn_dw_b, m_ffn_w_down, m_ple_w_proj, m_ple_w_gate, m_ple_b_gate, m_ln_ffn_g, m_ln_ffn_b, v_mix_w_in, v_pool_w, v_pool_scale, v_conv_dw_w, v_conv_dw_b, v_conv_ln_g, v_conv_ln_b, v_mix_w_out, v_attn_w_qkv, v_attn_rel_bias, v_attn_w_o, v_ln_mix_g, v_ln_mix_b, v_ffn_w_up, v_ffn_dw_w, v_ffn_dw_b, v_ffn_w_down, v_ple_w_proj, v_ple_w_gate, v_ple_b_gate, v_ln_ffn_g, v_ln_ffn_b):
    a = dict(locals())
    sh_names = list(_SHARDED)
    names = sh_names + list(_REPLICATED)
    wts = {n: a[n] for n in names}
    mom = {n: a["m_" + n] for n in names}
    var = {n: a["v_" + n] for n in names}

    for n in _TRANSPOSED:
        wts[n], mom[n], var[n] = (jnp.swapaxes(d[n], 1, 2) for d in (wts, mom, var))
    gather = {}
    token = None
    for group, items in _GATHER_GROUPS:
        gather[group] = _exchange_start([wts[n][l].astype(dt) for n, l, dt, _ in items], [pl_ for *_, pl_ in items],
                                        token, name="gather_start_" + group)
        token = gather[group]["token"]

    w = dict(pool_w=pool_w[0], pool_scale=pool_scale[0], conv_dw_b=conv_dw_b[0], conv_ln_g=conv_ln_g[0],
             conv_ln_b=conv_ln_b[0], attn_rel_bias=attn_rel_bias[0], ln_mix_g=ln_mix_g, ln_mix_b=ln_mix_b,
             ffn_dw_b=ffn_dw_b, ple_b_gate=ple_b_gate, ln_ffn_g=ln_ffn_g, ln_ffn_b=ln_ffn_b)
    for n in ("ffn_up_t", "ffn_dw_w", "ffn_w_down", "ple_w_gate", "ple_w_proj"):
        w[n] = [None, None]

    def ready(group, after):
        got = _exchange_wait(gather[group], token if after is None else after, name="gather_wait_" + group)
        if group == "mix":
            w["mix_w_in_t"], w["conv_dw_w"] = got[0], _full_from_shards(got[1], 1)
        elif group == "mixo":
            (w["mix_w_out"],) = got
        elif group == "attn":
            w["attn_w_qkv"], w["attn_w_o"] = got
        elif group[:2] == "up":
            l = int(group[2])
            w["ffn_up_t"][l], w["ffn_dw_w"][l] = got[0], _full_from_shards(got[1], 1)
        else:
            l = int(group[2])
            w["ffn_w_down"][l], w["ple_w_gate"][l], w["ple_w_proj"][l] = got

    scatter = {}

    def emit(group, gr):
        if group[:3] == "ffn":
            l = int(group[3])
            pieces = [_pieces_from_full(gr["ffn_up_t"][l], 0),
                      _pieces_from_full(gr["ffn_dw_w"][l], 1), _pieces_from_full(gr["ffn_w_down"][l], 0),
                      _pieces_from_full(gr["ple_w_gate"][l], 0), gr["ple_w_proj"][l]]
        elif group == "attn":
            pieces = [gr["attn_w_qkv"], _pieces_from_full(gr["attn_w_o"], 0)]
        else:
            pieces = [_pieces_from_full(gr["mix_w_in_t"], 0), _pieces_from_full(gr["conv_dw_w"], 1),
                      _pieces_from_full(gr["mix_w_out"], 0)]
        scatter[group] = _exchange_start([a.astype(BF16) for a in pieces], ["pieces"] * len(pieces), None,
                                         name="grad_start_" + group)
        if group != "mix":
            return scatter[group]["token"]
        gfull = dict(
            pool_w=gr["pool_w"][None], pool_scale=gr["pool_scale"][None], conv_dw_b=gr["conv_dw_b"][None],
            conv_ln_g=gr["conv_ln_g"][None], conv_ln_b=gr["conv_ln_b"][None],
            attn_rel_bias=gr["attn_rel_bias"][None], ln_mix_g=jnp.stack(gr["ln_mix_g"]),
            ln_mix_b=jnp.stack(gr["ln_mix_b"]), ffn_dw_b=jnp.stack(gr["ffn_dw_b"]),
            ple_b_gate=jnp.stack(gr["ple_b_gate"]), ln_ffn_g=jnp.stack(gr["ln_ffn_g"]),
            ln_ffn_b=jnp.stack(gr["ln_ffn_b"]))
        rep_send = _pack_rows([gfull[n].reshape(-1) for n in _REPLICATED], 8, F32)
        scatter["replicated"] = _exchange_start([rep_send], ["stack"], scatter[group]["token"],
                                                name="grad_start_replicated")
        return scatter["replicated"]["token"]

    loss_part, grad_x, gr = _local_step(x[0], p[:, 0], loss_target[0], w, ready, emit)
    loss = lax.psum(loss_part, ("x", "y", "c"))

    group_weights = {"ffn1": (("ffn_w_up", 1), ("ffn_dw_w", 1), ("ffn_w_down", 1), ("ple_w_gate", 1), ("ple_w_proj", 1)),
                     "attn": (("attn_w_qkv", 0), ("attn_w_o", 0)),
                     "ffn0": (("ffn_w_up", 0), ("ffn_dw_w", 0), ("ffn_w_down", 0), ("ple_w_gate", 0), ("ple_w_proj", 0)),
                     "mix": (("mix_w_in", 0), ("conv_dw_w", 0), ("mix_w_out", 0))}
    updated = {}
    after = grad_x
    for group in ("ffn1", "attn", "ffn0", "mix"):
        recv = _exchange_wait(scatter[group], after, name="grad_wait_" + group)
        for (n, l), r in zip(group_weights[group], recv):
            updated[n] = _adamw(r, wts[n], mom[n], var[n], layer=l, into=updated.get(n), name=f"adamw_{n}{l}")
            after = updated[n][0]
    res = [{n: jnp.swapaxes(updated[n][k], 1, 2) if n in _TRANSPOSED else updated[n][k] for n in sh_names}
           for k in range(4)]
    (rep_recv,) = _exchange_wait(scatter["replicated"], after, name="grad_wait_replicated")

    def flat_state(d):
        return _pack_rows([d[n].reshape(-1) for n in _REPLICATED], 8, F32)[None]

    rep_out = _adamw(rep_recv, flat_state(wts), flat_state(mom), flat_state(var), name="adamw_replicated")
    for k in range(4):
        for n, arr in zip(_REPLICATED, _unpack(rep_out[k][0], [wts[n].shape for n in _REPLICATED])):
            res[k][n] = arr
    order = ["mix_w_in", "pool_w", "pool_scale", "conv_dw_w", "conv_dw_b", "conv_ln_g", "conv_ln_b", "mix_w_out",
             "attn_w_qkv", "attn_rel_bias", "attn_w_o", "ln_mix_g", "ln_mix_b", "ffn_w_up", "ffn_dw_w", "ffn_dw_b",
             "ffn_w_down", "ple_w_proj", "ple_w_gate", "ple_b_gate", "ln_ffn_g", "ln_ffn_b"]
    outs = [loss, grad_x[None]]
    for k in range(4):
        outs += [res[k][n] for n in order]
    return tuple(outs)
```

```python
import functools
import math

import jax
import jax.numpy as jnp
from jax import lax
from jax.experimental import pallas as pl
from jax.experimental.pallas import tpu as pltpu

F32 = jnp.float32
BF16 = jnp.bfloat16

N_DEV = 8
D_MODEL = 1024
D_POOL = 512
D_CONV = 512
POOL_WINDOWS = (2, 4, 8, 16)
POOL_GROUP = 128
CONV_KERNEL = 31
CHUNK = 64
HEAD_DIM = 64
N_HEADS = 16
LEFT_CHUNKS = 8
BAND = (LEFT_CHUNKS + 1) * CHUNK
MAX_REL = 256
D_FF = 2816
PLE_DIM = 256
ALPHA = 4.0 ** 0.25
LN_EPS = 1e-5
NEG_INF = -1e30
ADAM_LR, ADAM_B1, ADAM_B2, ADAM_EPS, ADAM_WD, ADAM_STEP = 0.001, 0.9, 0.999, 1e-08, 0.01, 10

Q_BLOCK = 4 * CHUNK
KV_PAD = LEFT_CHUNKS * CHUNK
KV_SPAN = KV_PAD + Q_BLOCK
CONV_HALO = 32
FFN_HALO = 16
SUB_ROWS, SUB_LANES = 64, 128
LANES = 1024
VMEM_LIMIT = 56 * 1024 * 1024


def _cparams(sem=None):
    return pltpu.CompilerParams(dimension_semantics=sem, vmem_limit_bytes=VMEM_LIMIT)


def _tile(dim, pref):
    if dim <= pref:
        return dim
    t = pref - pref % 128
    while t >= 128:
        if dim % t == 0:
            return t
        t -= 128
    return dim


def _sigmoid(x):
    return 1.0 / (1.0 + jnp.exp(-x))


def _bdot(a, b, dn=(((1,), (0,)), ((), ()))):
    return lax.dot_general(a.astype(BF16), b.astype(BF16), dn, preferred_element_type=F32)


WHOLE = (0, 1)
NT = (((1,), (1,)), ((), ()))
TN = (((0,), (0,)), ((), ()))


def _wgrad(a, b, *, tm=1024, tn=1024, tk=2048, piece=None, part=(0, 1), into=None, name):
    K, M = a.shape
    kb, N = b.shape
    assert K == kb, (a.shape, b.shape)
    tm, tn, tk = _tile(M, tm), _tile(N, tn), _tile(K, tk)
    nk = K // tk
    per = 1 if piece is None else tn // piece
    assert piece is None or tn == per * piece

    def body(a_ref, b_ref, *rest):
        o_ref, acc = rest[-2:]
        k = pl.program_id(2)

        @pl.when(k == 0)
        def _():
            acc[...] = jnp.zeros_like(acc)

        acc[...] += _bdot(a_ref[...], b_ref[...], TN)

        @pl.when(k == nk - 1)
        def _():
            if piece is None:
                o_ref[...] = acc[...].astype(BF16)
            else:
                for s in range(per):
                    o_ref[s] = acc[:, s * piece:(s + 1) * piece].astype(BF16)

    if piece is None:
        first = part[0] * (M // tm)
        out_shape = (part[1] * M, N)
        out_spec = pl.BlockSpec((tm, tn), lambda i, j, k: (first + i, j))
    else:
        out_shape, out_spec = (N // piece, M, piece), pl.BlockSpec((per, tm, piece), lambda i, j, k: (j, i, 0))
    others = [] if into is None else [into]
    return pl.pallas_call(
        body,
        out_shape=jax.ShapeDtypeStruct(out_shape, BF16),
        grid=(M // tm, N // tn, nk),
        in_specs=[pl.BlockSpec((tk, tm), lambda i, j, k: (k, i)), pl.BlockSpec((tk, tn), lambda i, j, k: (k, j))]
        + [pl.BlockSpec(memory_space=pl.ANY)] * len(others),
        out_specs=out_spec,
        input_output_aliases={2: 0} if others else {},
        scratch_shapes=[pltpu.VMEM((tm, tn), F32)],
        compiler_params=_cparams(("parallel", "parallel", "arbitrary")),
        name=name,
    )(a, b, *others)


def _mm_rows(pairs, *, add=None, add_scale=1.0, out_dtype=F32, tm=512, dep=None, ln_bwd=None, name):
    M = pairs[0][0].shape[0]
    n = len(pairs)
    has_add = add is not None
    w_rows = [w_.shape[0] // part[1] for _, w_, _, part in pairs]
    N = w_rows[0] if pairs[0][2] else pairs[0][1].shape[1]

    def body(*refs):
        acc = None
        for i, (_, _, tr, _) in enumerate(pairs):
            part = _bdot(refs[2 * i][...], refs[2 * i + 1][...], NT if tr else (((1,), (0,)), ((), ())))
            acc = part if acc is None else acc + part
        if has_add:
            acc = acc + add_scale * refs[2 * n][...]
        if ln_bwd is None:
            refs[-1][...] = acc.astype(out_dtype)
            return
        z_ref, g_ref = refs[2 * n + has_add], refs[2 * n + has_add + 1]
        dz_ref, dzb_ref, dg_ref, db_ref = refs[-4:]

        @pl.when(pl.program_id(0) == 0)
        def _():
            dg_ref[...] = jnp.zeros_like(dg_ref)
            db_ref[...] = jnp.zeros_like(db_ref)

        dg_acc = jnp.zeros((8, N), F32)
        db_acc = jnp.zeros((8, N), F32)
        for r0 in range(0, tm, LN_ROWS):
            rows = pl.ds(r0, LN_ROWS)
            do = acc[r0:r0 + LN_ROWS]
            dz, xh = _ln_bwd_rows(z_ref[rows, :], g_ref[...], do)
            dz_ref[rows, :] = dz
            dzb_ref[rows, :] = dz.astype(BF16)
            dg_acc = dg_acc + jnp.sum((do * xh).reshape(LN_ROWS // 8, 8, N), axis=0)
            db_acc = db_acc + jnp.sum(do.reshape(LN_ROWS // 8, 8, N), axis=0)
        dg_ref[...] += jnp.sum(dg_acc, axis=0, keepdims=True)
        db_ref[...] += jnp.sum(db_acc, axis=0, keepdims=True)

    in_specs, args = [], []
    for (a, w_, _, part), rows in zip(pairs, w_rows):
        in_specs += [pl.BlockSpec((tm, a.shape[1]), lambda i: (i, 0)),
                     pl.BlockSpec((rows, w_.shape[1]), functools.partial(lambda i, j: (j, 0), j=part[0]))]
        args += [a, w_]
    row = pl.BlockSpec((tm, N), lambda i: (i, 0))
    fix = pl.BlockSpec((1, N), lambda i: (0, 0))
    if has_add:
        in_specs.append(row)
        args.append(add)
    if ln_bwd is not None:
        in_specs += [row, fix]
        args += [ln_bwd[0], ln_bwd[1].reshape(1, N)]
    if dep is not None:
        in_specs.append(pl.BlockSpec(memory_space=pl.ANY))
        args.append(dep)
    if ln_bwd is None:
        out_shape, out_specs = jax.ShapeDtypeStruct((M, N), out_dtype), row
    else:
        out_shape = [jax.ShapeDtypeStruct((M, N), F32), jax.ShapeDtypeStruct((M, N), BF16),
                     jax.ShapeDtypeStruct((1, N), F32), jax.ShapeDtypeStruct((1, N), F32)]
        out_specs = [row, row, fix, fix]
    return pl.pallas_call(
        body,
        out_shape=out_shape,
        grid=(M // tm,),
        in_specs=in_specs,
        out_specs=out_specs,
        compiler_params=_cparams(("parallel",) if ln_bwd is None else ("arbitrary",)),
        name=name,
    )(*args)


def _ln_bwd_rows(zt, g, do):
    zc = zt - jnp.mean(zt, axis=-1, keepdims=True)
    rstd = lax.rsqrt(jnp.mean(zc * zc, axis=-1, keepdims=True) + LN_EPS)
    xh = zc * rstd
    dxh = do * g
    return rstd * (dxh - jnp.mean(dxh, axis=-1, keepdims=True) - xh * jnp.mean(dxh * xh, axis=-1, keepdims=True)), xh


def _layer_norm_rows(z, g, b):
    mu = jnp.mean(z, axis=-1, keepdims=True)
    zc = z - mu
    var = jnp.mean(zc * zc, axis=-1, keepdims=True)
    return zc * lax.rsqrt(var + LN_EPS) * g + b


def _proj_ln(res, a, w, ln_g, ln_b, *, ple=None, ts=512, name):
    S, D = res.shape
    ka = a.shape[1]
    has_ple = ple is not None
    row = lambda i: (i, 0)
    fix = lambda i: (0, 0)

    def body(*refs):
        if has_ple:
            (res_ref, a_ref, w_ref, g_ref, b_ref, wg_ref, bg_ref, p_ref, wp_ref, z_ref, r_ref, rb_ref, gate_ref,
             proj_ref, acc) = refs
        else:
            res_ref, a_ref, w_ref, g_ref, b_ref, z_ref, r_ref, rb_ref, acc = refs
        acc[...] = _bdot(a_ref[...], w_ref[...])
        if has_ple:
            gate_ref[...] = _bdot(res_ref[...], wg_ref[...])
            proj_ref[...] = _bdot(p_ref[...], wp_ref[...])
        for r0 in range(0, ts, LN_ROWS):
            rows = pl.ds(r0, LN_ROWS)
            z = ALPHA * res_ref[rows, :] + acc[rows, :]
            if has_ple:
                gate = _sigmoid(gate_ref[rows, :] + bg_ref[...])
                gate_ref[rows, :] = gate
                z = z + gate * proj_ref[rows, :]
            z_ref[rows, :] = z
            r = _layer_norm_rows(z, g_ref[...], b_ref[...])
            r_ref[rows, :] = r
            rb_ref[rows, :] = r.astype(BF16)

    in_specs = [pl.BlockSpec((ts, D), row), pl.BlockSpec((ts, ka), row), pl.BlockSpec((ka, D), fix),
                pl.BlockSpec((1, D), fix), pl.BlockSpec((1, D), fix)]
    args = [res, a, w, ln_g.reshape(1, D), ln_b.reshape(1, D)]
    out_dtypes = [F32, F32, BF16]
    if has_ple:
        wg, bg, p, wp = ple
        in_specs += [pl.BlockSpec((D, D), fix), pl.BlockSpec((1, D), fix), pl.BlockSpec((ts, PLE_DIM), row),
                     pl.BlockSpec((PLE_DIM, D), fix)]
        args += [wg, bg.reshape(1, D), p, wp]
        out_dtypes += [F32, F32]
    return pl.pallas_call(
        body,
        out_shape=[jax.ShapeDtypeStruct((S, D), dt) for dt in out_dtypes],
        grid=(S // ts,),
        in_specs=in_specs,
        out_specs=[pl.BlockSpec((ts, D), row)] * len(out_dtypes),
        scratch_shapes=[pltpu.VMEM((ts, D), F32)],
        compiler_params=_cparams(("parallel",)),
        name=name,
    )(*args)


CONV_ROWS = 32
LN_ROWS = 16


def _shifted_copies(src, dst, rows):
    for c0 in range(0, src.shape[1], SUB_LANES):
        ln = pl.ds(c0, SUB_LANES)
        for r0 in range(0, rows, SUB_ROWS):
            rc = min(SUB_ROWS, rows - r0)
            for b, shifted in enumerate(_rows_ahead(src, r0, rc, ln, range(1, 8))):
                dst[b, pl.ds(r0, rc), ln] = shifted


def _rows_at(src, copies, off, n, ln):
    b = off % 8
    return src[pl.ds(off, n), ln] if b == 0 else copies[b - 1, pl.ds(off - b, n), ln]


def _conv31(stg, gsh, cw_ref, cb_ref, out, rows, first_off):
    for c0 in range(0, D_CONV, SUB_LANES):
        ln = pl.ds(c0, SUB_LANES)
        for r0 in range(0, rows, CONV_ROWS):
            acc = jnp.zeros((CONV_ROWS, SUB_LANES), F32) + cb_ref[:, ln]
            for k in range(CONV_KERNEL):
                acc = acc + cw_ref[k:k + 1, ln] * _rows_at(stg, gsh, first_off + k + r0, CONV_ROWS, ln)
            out[pl.ds(r0, CONV_ROWS), ln] = acc


def _mixer_fwd(u, pool_w, pool_scale, conv_w, conv_b, cln_g, cln_b, *, ts=256):
    S = u.shape[0]
    hb = CONV_HALO
    nh = ts // hb

    def body(u_ref, uh_ref, pw_ref, ps_ref, cw_ref, cb_ref, g_ref, b_ref, y_ref, d_ref, hcs, sta, stg, gsh):
        i = pl.program_id(0)
        first = i == 0
        sta[pl.ds(0, hb), :] = jnp.where(first, 0.0, uh_ref[:, 0:D_POOL])
        sta[pl.ds(hb, ts), :] = u_ref[:, 0:D_POOL]
        glu_h = uh_ref[:, D_POOL:D_POOL + D_CONV] * _sigmoid(uh_ref[:, D_POOL + D_CONV:])
        stg[pl.ds(0, hb), :] = jnp.where(first, 0.0, glu_h)
        stg[pl.ds(hb, ts), :] = u_ref[:, D_POOL:D_POOL + D_CONV] * _sigmoid(u_ref[:, D_POOL + D_CONV:])

        for g, w in enumerate(POOL_WINDOWS):
            lanes = pl.ds(g * POOL_GROUP, POOL_GROUP)
            for r0 in range(0, ts, SUB_ROWS):
                s = None
                for q in range(0, w, 8):
                    for tap in _rows_back(sta, hb + r0 - q, SUB_ROWS, lanes, range(min(8, w - q))):
                        s = tap if s is None else s + tap
                pos = (i * ts + r0 + lax.broadcasted_iota(jnp.int32, (SUB_ROWS, 1), 0) + 1).astype(F32)
                d_g = s / jnp.minimum(pos, float(w)) - sta[pl.ds(hb + r0, SUB_ROWS), lanes]
                d_ref[pl.ds(r0, SUB_ROWS), lanes] = d_g.astype(BF16)
            y_ref[:, lanes] = (_bdot(d_ref[:, lanes], pw_ref[g]) * ps_ref[:, lanes]).astype(BF16)

        _shifted_copies(stg, gsh, hb + ts - 8)
        _conv31(stg, gsh, cw_ref, cb_ref, hcs, ts, hb - (CONV_KERNEL - 1))
        for r0 in range(0, ts, LN_ROWS):
            rows = pl.ds(r0, LN_ROWS)
            ln = _layer_norm_rows(hcs[rows, :], g_ref[...], b_ref[...])
            y_ref[rows, D_POOL:] = (ln * _sigmoid(ln)).astype(BF16)

    fix2 = lambda i: (0, 0)
    return pl.pallas_call(
        body,
        out_shape=[jax.ShapeDtypeStruct((S, D_MODEL), BF16), jax.ShapeDtypeStruct((S, D_POOL), BF16),
                   jax.ShapeDtypeStruct((S, D_CONV), F32)],
        grid=(S // ts,),
        in_specs=[pl.BlockSpec((ts, 3 * D_POOL), lambda i: (i, 0)),
                  pl.BlockSpec((hb, 3 * D_POOL), lambda i: (jnp.maximum(i * nh - 1, 0), 0)),
                  pl.BlockSpec((4, POOL_GROUP, POOL_GROUP), lambda i: (0, 0, 0)),
                  pl.BlockSpec((1, D_POOL), fix2), pl.BlockSpec((CONV_KERNEL, D_CONV), fix2),
                  pl.BlockSpec((1, D_CONV), fix2), pl.BlockSpec((1, D_CONV), fix2), pl.BlockSpec((1, D_CONV), fix2)],
        out_specs=[pl.BlockSpec((ts, D_MODEL), lambda i: (i, 0)), pl.BlockSpec((ts, D_POOL), lambda i: (i, 0)),
                   pl.BlockSpec((ts, D_CONV), lambda i: (i, 0))],
        scratch_shapes=[pltpu.VMEM((hb + ts, D_POOL), F32), pltpu.VMEM((hb + ts, D_CONV), F32),
                        pltpu.VMEM((7, hb + ts - 8, D_CONV), F32)],
        compiler_params=_cparams(("parallel",)),
        name="mixer_fwd",
    )(u, u, pool_w, pool_scale.reshape(1, D_POOL), conv_w, conv_b.reshape(1, D_CONV), cln_g.reshape(1, D_CONV),
      cln_b.reshape(1, D_CONV))


def _mixer_bwd(u, d, hc, dycat, pool_w, pool_scale, conv_w, cln_g, cln_b, *, ts=256):
    S = u.shape[0]
    hb = CONV_HALO
    nh = ts // hb
    n = S // ts
    te = ts + hb
    K = CONV_KERNEL

    def body(u_ref, up_ref, un_ref, d_ref, hc_ref, hcn_ref, dy_ref, dyn_ref, pw_ref, ps_ref, cw_ref, g_ref, b_ref,
             du_ref, dpw_ref, dps_ref, dcw_ref, dcb_ref, dg_ref, db_ref, stg, std, sth, gsh, hsh):
        i = pl.program_id(0)
        first = i == 0
        last = i == n - 1

        @pl.when(first)
        def _():
            dpw_ref[...] = jnp.zeros_like(dpw_ref)
            dps_ref[...] = jnp.zeros_like(dps_ref)
            dcw_ref[...] = jnp.zeros_like(dcw_ref)
            dcb_ref[...] = jnp.zeros_like(dcb_ref)
            dg_ref[...] = jnp.zeros_like(dg_ref)
            db_ref[...] = jnp.zeros_like(db_ref)

        pos_e = (i * ts + lax.broadcasted_iota(jnp.int32, (te, 1), 0) + 1).astype(F32)
        dya = dy_ref[:, 0:D_POOL]
        dya_n = jnp.where(last, 0.0, dyn_ref[:, 0:D_POOL])
        for g, w in enumerate(POOL_WINDOWS):
            lanes = pl.ds(g * POOL_GROUP, POOL_GROUP)
            sl = slice(g * POOL_GROUP, (g + 1) * POOL_GROUP)
            pw = pw_ref[g]
            scale = ps_ref[:, lanes]
            d_g = d_ref[:, lanes]
            pre = _bdot(d_g, pw)
            dps_ref[:, lanes] += jnp.sum(dya[:, sl] * pre, axis=0, keepdims=True)
            dys = dya[:, sl] * scale
            dpw_ref[g] += _bdot(d_g, dys, TN)
            dys_e = jnp.concatenate([dys, dya_n[:, sl] * scale], axis=0)
            dd = _bdot(dys_e, pw, NT)
            std[:, lanes] = dd / jnp.minimum(pos_e, float(w))
            for r0 in range(0, ts, SUB_ROWS):
                da = -dd[r0:r0 + SUB_ROWS]
                for q in range(0, w, 8):
                    for tap in _rows_ahead(std, r0 + q, SUB_ROWS, lanes, range(min(8, w - q))):
                        da = da + tap
                du_ref[pl.ds(r0, SUB_ROWS), lanes] = da.astype(BF16)

        glu_p = up_ref[:, D_POOL:D_POOL + D_CONV] * _sigmoid(up_ref[:, D_POOL + D_CONV:])
        stg[pl.ds(0, hb), :] = jnp.where(first, 0.0, glu_p)
        bv = u_ref[:, D_POOL:D_POOL + D_CONV]
        sg = _sigmoid(u_ref[:, D_POOL + D_CONV:])
        stg[pl.ds(hb, ts), :] = bv * sg
        glu_n = un_ref[:, D_POOL:D_POOL + D_CONV] * _sigmoid(un_ref[:, D_POOL + D_CONV:])
        stg[pl.ds(hb + ts, hb), :] = jnp.where(last, 0.0, glu_n)
        _shifted_copies(stg, gsh, hb + te - 8)

        sums = [jnp.zeros((8, D_CONV), F32) for _ in range(3)]
        for r0 in range(0, te, LN_ROWS):
            rows = pl.ds(r0, LN_ROWS)
            hc = hc_ref[rows, :] if r0 < ts else hcn_ref[pl.ds(r0 - ts, LN_ROWS), :]
            hcc = hc - jnp.mean(hc, axis=-1, keepdims=True)
            rstd = lax.rsqrt(jnp.mean(hcc * hcc, axis=-1, keepdims=True) + LN_EPS)
            xh = hcc * rstd
            ln = xh * g_ref[...] + b_ref[...]
            sl_ = _sigmoid(ln)
            if r0 < ts:
                dyb = dy_ref[rows, D_POOL:]
            else:
                dyb = jnp.where(last, 0.0, dyn_ref[pl.ds(r0 - ts, LN_ROWS), D_POOL:])
            dln = dyb * (sl_ * (1.0 + ln * (1.0 - sl_)))
            dxh = dln * g_ref[...]
            dhc = rstd * (dxh - jnp.mean(dxh, axis=-1, keepdims=True)
                          - xh * jnp.mean(dxh * xh, axis=-1, keepdims=True))
            sth[rows, :] = dhc
            if r0 < ts:
                for n_, term in enumerate((dln * xh, dln, dhc)):
                    sums[n_] = sums[n_] + jnp.sum(term.reshape(LN_ROWS // 8, 8, D_CONV), axis=0)
        dg_ref[...] += jnp.sum(sums[0], axis=0, keepdims=True)
        db_ref[...] += jnp.sum(sums[1], axis=0, keepdims=True)
        dcb_ref[...] += jnp.sum(sums[2], axis=0, keepdims=True)

        _shifted_copies(sth, hsh, te - 8)
        for c0 in range(0, D_CONV, SUB_LANES):
            ln_ = pl.ds(c0, SUB_LANES)
            for r0 in range(0, ts, CONV_ROWS):
                rows = pl.ds(r0, CONV_ROWS)
                dglu = jnp.zeros((CONV_ROWS, SUB_LANES), F32)
                for k in range(K):
                    dglu = dglu + cw_ref[k:k + 1, ln_] * _rows_at(sth, hsh, K - 1 - k + r0, CONV_ROWS, ln_)
                bv = u_ref[rows, pl.ds(D_POOL + c0, SUB_LANES)]
                sg = _sigmoid(u_ref[rows, pl.ds(D_POOL + D_CONV + c0, SUB_LANES)])
                du_ref[rows, pl.ds(D_POOL + c0, SUB_LANES)] = (dglu * sg).astype(BF16)
                du_ref[rows, pl.ds(D_POOL + D_CONV + c0, SUB_LANES)] = (dglu * bv * sg * (1.0 - sg)).astype(BF16)
            for k in range(K):
                tap = jnp.zeros((8, SUB_LANES), F32)
                for r0 in range(0, ts, CONV_ROWS):
                    prod = sth[pl.ds(r0, CONV_ROWS), ln_] * _rows_at(stg, gsh, hb - (K - 1) + k + r0, CONV_ROWS, ln_)
                    tap = tap + jnp.sum(prod.reshape(CONV_ROWS // 8, 8, SUB_LANES), axis=0)
                dcw_ref[k:k + 1, ln_] += jnp.sum(tap, axis=0, keepdims=True)

    fix2 = lambda i: (0, 0)
    prev = lambda i: (jnp.maximum(i * nh - 1, 0), 0)
    nxt = lambda i: (jnp.minimum((i + 1) * nh, S // hb - 1), 0)
    return pl.pallas_call(
        body,
        out_shape=[jax.ShapeDtypeStruct((S, 3 * D_POOL), BF16),
                   jax.ShapeDtypeStruct((4, POOL_GROUP, POOL_GROUP), F32),
                   jax.ShapeDtypeStruct((1, D_POOL), F32),
                   jax.ShapeDtypeStruct((K, D_CONV), F32),
                   jax.ShapeDtypeStruct((1, D_CONV), F32),
                   jax.ShapeDtypeStruct((1, D_CONV), F32),
                   jax.ShapeDtypeStruct((1, D_CONV), F32)],
        grid=(n,),
        in_specs=[pl.BlockSpec((ts, 3 * D_POOL), lambda i: (i, 0)),
                  pl.BlockSpec((hb, 3 * D_POOL), prev),
                  pl.BlockSpec((hb, 3 * D_POOL), nxt),
                  pl.BlockSpec((ts, D_POOL), lambda i: (i, 0)),
                  pl.BlockSpec((ts, D_CONV), lambda i: (i, 0)),
                  pl.BlockSpec((hb, D_CONV), nxt),
                  pl.BlockSpec((ts, D_MODEL), lambda i: (i, 0)),
                  pl.BlockSpec((hb, D_MODEL), nxt),
                  pl.BlockSpec((4, POOL_GROUP, POOL_GROUP), lambda i: (0, 0, 0)),
                  pl.BlockSpec((1, D_POOL), fix2), pl.BlockSpec((K, D_CONV), fix2),
                  pl.BlockSpec((1, D_CONV), fix2), pl.BlockSpec((1, D_CONV), fix2)],
        out_specs=[pl.BlockSpec((ts, 3 * D_POOL), lambda i: (i, 0)),
                   pl.BlockSpec((4, POOL_GROUP, POOL_GROUP), lambda i: (0, 0, 0)),
                   pl.BlockSpec((1, D_POOL), fix2), pl.BlockSpec((K, D_CONV), fix2),
                   pl.BlockSpec((1, D_CONV), fix2), pl.BlockSpec((1, D_CONV), fix2), pl.BlockSpec((1, D_CONV), fix2)],
        scratch_shapes=[pltpu.VMEM((hb + ts + hb, D_CONV), F32), pltpu.VMEM((te, D_POOL), F32),
                        pltpu.VMEM((te, D_CONV), F32), pltpu.VMEM((7, hb + te - 8, D_CONV), F32),
                        pltpu.VMEM((7, te - 8, D_CONV), F32)],
        compiler_params=_cparams(("arbitrary",)),
        name="mixer_bwd",
    )(u, u, u, d, hc, hc, dycat, dycat, pool_w, pool_scale.reshape(1, D_POOL), conv_w, cln_g.reshape(1, D_CONV),
      cln_b.reshape(1, D_CONV))


_GELU_C = math.sqrt(2.0 / math.pi)


def _gelu_parts(x):
    inner = _GELU_C * (x + 0.044715 * x * x * x)
    th = jnp.tanh(inner)
    ge = 0.5 * x * (1.0 + th)
    dge = 0.5 * (1.0 + th) + 0.5 * x * (1.0 - th * th) * (_GELU_C * (1.0 + 3.0 * 0.044715 * x * x))
    return ge, dge


def _rows_back(ref, r, n, ln, shifts):
    ext = ref[pl.ds(r - 8, n + 8), ln]
    return [(pltpu.roll(ext, s, 0) if s else ext)[8:] for s in shifts]


def _rows_ahead(ref, r, n, ln, shifts):
    ext = ref[pl.ds(r, n + 8), ln]
    return [(pltpu.roll(ext, n + 8 - s, 0) if s else ext)[:n] for s in shifts]


def _ffn_act_fwd(gate, val, dw_w, dw_b, *, ts=512, tc=1408, name):
    S, F = gate.shape
    hb = FFN_HALO
    nh = ts // hb
    tc = _tile(F, tc)

    def body(g_ref, gh_ref, v_ref, w_ref, b_ref, h_ref, st):
        i = pl.program_id(0)
        st[pl.ds(0, hb), :] = jnp.where(i == 0, 0.0, gh_ref[...].astype(F32))
        st[pl.ds(hb, ts), :] = g_ref[...].astype(F32)
        for c0 in range(0, tc, SUB_LANES):
            ln = pl.ds(c0, SUB_LANES)
            w0, w1, w2, b = w_ref[0:1, ln], w_ref[1:2, ln], w_ref[2:3, ln], b_ref[:, ln]
            for r0 in range(0, ts, SUB_ROWS):
                taps = _rows_back(st, hb + r0, SUB_ROWS, ln, (2, 1, 0))
                gc = b + w0 * taps[0] + w1 * taps[1] + w2 * taps[2]
                ge, _ = _gelu_parts(gc)
                rows = pl.ds(r0, SUB_ROWS)
                h_ref[rows, ln] = (ge * v_ref[rows, ln].astype(F32)).astype(BF16)

    return pl.pallas_call(
        body,
        out_shape=jax.ShapeDtypeStruct((S, F), BF16),
        grid=(S // ts, F // tc),
        in_specs=[pl.BlockSpec((ts, tc), lambda i, j: (i, j)),
                  pl.BlockSpec((hb, tc), lambda i, j: (jnp.maximum(i * nh - 1, 0), j)),
                  pl.BlockSpec((ts, tc), lambda i, j: (i, j)),
                  pl.BlockSpec((3, tc), lambda i, j: (0, j)),
                  pl.BlockSpec((1, tc), lambda i, j: (0, j))],
        out_specs=pl.BlockSpec((ts, tc), lambda i, j: (i, j)),
        scratch_shapes=[pltpu.VMEM((hb + ts, tc), F32)],
        compiler_params=_cparams(("parallel", "parallel")),
        name=name,
    )(gate, gate, val, dw_w, dw_b.reshape(1, F))


def _ffn_act_bwd(gate, val, dh, dw_w, dw_b, *, ts=512, tc=1408, name):
    S, F = gate.shape
    hb = FFN_HALO
    nh = ts // hb
    n = S // ts
    te = ts + hb
    tc = _tile(F, tc)

    def body(g_ref, gp_ref, gn_ref, v_ref, vn_ref, dh_ref, dhn_ref, w_ref, b_ref,
             dg_ref, dv_ref, dw_ref, db_ref, st, sd):
        i = pl.program_id(1)
        first = i == 0
        last = i == n - 1

        @pl.when(first)
        def _():
            dw_ref[...] = jnp.zeros_like(dw_ref)
            db_ref[...] = jnp.zeros_like(db_ref)

        st[pl.ds(0, hb), :] = jnp.where(first, 0.0, gp_ref[...].astype(F32))
        st[pl.ds(hb, ts), :] = g_ref[...].astype(F32)
        st[pl.ds(hb + ts, hb), :] = jnp.where(last, 0.0, gn_ref[...].astype(F32))
        for c0 in range(0, tc, SUB_LANES):
            ln = pl.ds(c0, SUB_LANES)
            w0, w1, w2, b = w_ref[0:1, ln], w_ref[1:2, ln], w_ref[2:3, ln], b_ref[:, ln]
            db_acc = jnp.zeros((8, SUB_LANES), F32)
            dw_acc = [jnp.zeros((8, SUB_LANES), F32) for _ in range(3)]
            for r0 in range(0, te, SUB_ROWS):
                rc = min(SUB_ROWS, te - r0)
                taps = _rows_back(st, hb + r0, rc, ln, (2, 1, 0))
                gc = b + w0 * taps[0] + w1 * taps[1] + w2 * taps[2]
                ge, dge = _gelu_parts(gc)
                if r0 < ts:
                    rows = pl.ds(r0, rc)
                    val, dh = v_ref[rows, ln].astype(F32), dh_ref[rows, ln].astype(F32)
                else:
                    val = jnp.where(last, 0.0, vn_ref[:, ln].astype(F32)[0:rc])
                    dh = jnp.where(last, 0.0, dhn_ref[:, ln].astype(F32)[0:rc])
                dgc = dh * val * dge
                sd[pl.ds(r0, rc), ln] = dgc
                if r0 < ts:
                    dv_ref[rows, ln] = (dh * ge).astype(BF16)
                    db_acc = db_acc + jnp.sum(dgc.reshape(rc // 8, 8, SUB_LANES), axis=0)
                    for k in range(3):
                        dw_acc[k] = dw_acc[k] + jnp.sum((dgc * taps[k]).reshape(rc // 8, 8, SUB_LANES), axis=0)
            db_ref[:, ln] += jnp.sum(db_acc, axis=0, keepdims=True)
            for k in range(3):
                dw_ref[k:k + 1, ln] += jnp.sum(dw_acc[k], axis=0, keepdims=True)
            for r0 in range(0, ts, SUB_ROWS):
                ahead = _rows_ahead(sd, r0, SUB_ROWS, ln, (2, 1, 0))
                dg_ref[pl.ds(r0, SUB_ROWS), ln] = (w0 * ahead[0] + w1 * ahead[1] + w2 * ahead[2]).astype(BF16)

    cur = lambda j, i: (i, j)
    prev = lambda j, i: (jnp.maximum(i * nh - 1, 0), j)
    nxt = lambda j, i: (jnp.minimum((i + 1) * nh, S // hb - 1), j)
    return pl.pallas_call(
        body,
        out_shape=[jax.ShapeDtypeStruct((S, F), BF16), jax.ShapeDtypeStruct((S, F), BF16),
                   jax.ShapeDtypeStruct((3, F), F32), jax.ShapeDtypeStruct((1, F), F32)],
        grid=(F // tc, n),
        in_specs=[pl.BlockSpec((ts, tc), cur), pl.BlockSpec((hb, tc), prev), pl.BlockSpec((hb, tc), nxt),
                  pl.BlockSpec((ts, tc), cur), pl.BlockSpec((hb, tc), nxt),
                  pl.BlockSpec((ts, tc), cur), pl.BlockSpec((hb, tc), nxt),
                  pl.BlockSpec((3, tc), lambda j, i: (0, j)), pl.BlockSpec((1, tc), lambda j, i: (0, j))],
        out_specs=[pl.BlockSpec((ts, tc), cur), pl.BlockSpec((ts, tc), cur),
                   pl.BlockSpec((3, tc), lambda j, i: (0, j)), pl.BlockSpec((1, tc), lambda j, i: (0, j))],
        scratch_shapes=[pltpu.VMEM((hb + ts + hb, tc), F32), pltpu.VMEM((te, tc), F32)],
        compiler_params=_cparams(("parallel", "arbitrary")),
        name=name,
    )(gate, gate, gate, val, val, dh, dh, dw_w, dw_b.reshape(1, F))


def _loss_ln_bwd(z, ln_g, ln_b, target, *, ts=512, name):
    S, D = z.shape

    def body(z_ref, g_ref, b_ref, t_ref, dz_ref, dzb_ref, dg_ref, db_ref, loss_ref):
        i = pl.program_id(0)

        @pl.when(i == 0)
        def _():
            dg_ref[...] = jnp.zeros_like(dg_ref)
            db_ref[...] = jnp.zeros_like(db_ref)
            loss_ref[...] = jnp.zeros_like(loss_ref)

        dg_acc = jnp.zeros((8, D), F32)
        db_acc = jnp.zeros((8, D), F32)
        loss_acc = jnp.zeros((1, 1), F32)
        for r0 in range(0, ts, LN_ROWS):
            rows = pl.ds(r0, LN_ROWS)
            zt = z_ref[rows, :]
            err = _layer_norm_rows(zt, g_ref[...], b_ref[...]) - t_ref[rows, :]
            loss_acc = loss_acc + 0.5 * jnp.sum(jnp.mean(err * err, axis=-1, keepdims=True), keepdims=True)
            do = err * (1.0 / D)
            dz, xh = _ln_bwd_rows(zt, g_ref[...], do)
            dg_acc = dg_acc + jnp.sum((do * xh).reshape(LN_ROWS // 8, 8, D), axis=0)
            db_acc = db_acc + jnp.sum(do.reshape(LN_ROWS // 8, 8, D), axis=0)
            dz_ref[rows, :] = dz
            dzb_ref[rows, :] = dz.astype(BF16)
        dg_ref[...] += jnp.sum(dg_acc, axis=0, keepdims=True)
        db_ref[...] += jnp.sum(db_acc, axis=0, keepdims=True)
        loss_ref[...] += loss_acc

    row = lambda i: (i, 0)
    fix = lambda i: (0, 0)
    return pl.pallas_call(
        body,
        out_shape=[jax.ShapeDtypeStruct((S, D), F32), jax.ShapeDtypeStruct((S, D), BF16),
                   jax.ShapeDtypeStruct((1, D), F32), jax.ShapeDtypeStruct((1, D), F32),
                   jax.ShapeDtypeStruct((8, 128), F32)],
        grid=(S // ts,),
        in_specs=[pl.BlockSpec((ts, D), row), pl.BlockSpec((1, D), fix), pl.BlockSpec((1, D), fix),
                  pl.BlockSpec((ts, D), row)],
        out_specs=[pl.BlockSpec((ts, D), row), pl.BlockSpec((ts, D), row), pl.BlockSpec((1, D), fix),
                   pl.BlockSpec((1, D), fix), pl.BlockSpec((8, 128), fix)],
        compiler_params=_cparams(("arbitrary",)),
        name=name,
    )(z, ln_g.reshape(1, D), ln_b.reshape(1, D), target)


def _ple_bwd(dz, gate, proj, *, ts=512, name):
    S, D = dz.shape

    def body(dz_ref, g_ref, p_ref, ds_ref, dp_ref, db_ref):
        @pl.when(pl.program_id(0) == 0)
        def _():
            db_ref[...] = jnp.zeros_like(db_ref)

        db_acc = jnp.zeros((8, D), F32)
        for r0 in range(0, ts, LN_ROWS):
            rows = pl.ds(r0, LN_ROWS)
            dzt = dz_ref[rows, :]
            g = g_ref[rows, :]
            ds = dzt * p_ref[rows, :] * g * (1.0 - g)
            ds_ref[rows, :] = ds.astype(BF16)
            dp_ref[rows, :] = (dzt * g).astype(BF16)
            db_acc = db_acc + jnp.sum(ds.reshape(LN_ROWS // 8, 8, D), axis=0)
        db_ref[...] += jnp.sum(db_acc, axis=0, keepdims=True)

    row = lambda i: (i, 0)
    return pl.pallas_call(
        body,
        out_shape=[jax.ShapeDtypeStruct((S, D), BF16), jax.ShapeDtypeStruct((S, D), BF16),
                   jax.ShapeDtypeStruct((1, D), F32)],
        grid=(S // ts,),
        in_specs=[pl.BlockSpec((ts, D), row)] * 3,
        out_specs=[pl.BlockSpec((ts, D), row), pl.BlockSpec((ts, D), row), pl.BlockSpec((1, D), lambda i: (0, 0))],
        compiler_params=_cparams(("arbitrary",)),
        name=name,
    )(dz, gate, proj)


HEAD_PAIR = 2 * HEAD_DIM


ATT_ROWS = 32
ATT_SCALE = HEAD_DIM ** -0.5


def _softmax_piece(scores, bias, qb):
    s = scores + bias
    kpos = qb * Q_BLOCK + lax.broadcasted_iota(jnp.int32, (1, KV_SPAN), 1)
    s = jnp.where(kpos >= KV_PAD, s, NEG_INF)
    e = jnp.exp(s - jnp.max(s, axis=-1, keepdims=True))
    return e * (1.0 / jnp.sum(e, axis=-1, keepdims=True))


def _pad_keys(qb, k_ref, v_ref, kp, vp):
    @pl.when(qb == 0)
    def _():
        kp[pl.ds(0, KV_PAD), :] = jnp.zeros((KV_PAD, HEAD_PAIR), BF16)
        vp[pl.ds(0, KV_PAD), :] = jnp.zeros((KV_PAD, HEAD_PAIR), BF16)
        kp[pl.ds(KV_PAD, k_ref.shape[0]), :] = k_ref[...]
        vp[pl.ds(KV_PAD, v_ref.shape[0]), :] = v_ref[...]


def _attn_fwd(qkv, bias):
    S = qkv.shape[0]
    nhp = N_HEADS // 2

    def body(q_ref, k_ref, v_ref, b_ref, o_ref, kp, vp, p_scr):
        qb = pl.program_id(1)
        _pad_keys(qb, k_ref, v_ref, kp, vp)
        span = pl.ds(pl.multiple_of(qb * Q_BLOCK, Q_BLOCK), KV_SPAN)
        kc, vc = kp[span, :], vp[span, :]
        qt = q_ref[...] * ATT_SCALE
        first = lax.broadcasted_iota(jnp.int32, (1, HEAD_PAIR), 1) < HEAD_DIM
        scores = [_bdot(jnp.where(first if j == 0 else ~first, qt, jnp.zeros_like(qt)), kc, NT) for j in range(2)]
        outs = []
        for j in range(2):
            for r0 in range(0, Q_BLOCK, ATT_ROWS):
                rows = pl.ds(r0, ATT_ROWS)
                p_scr[j, rows, :] = _softmax_piece(scores[j][r0:r0 + ATT_ROWS], b_ref[j, rows, :], qb).astype(BF16)
            outs.append(_bdot(p_scr[j], vc))
        o_ref[...] = jnp.where(first, outs[0], outs[1]).astype(BF16)

    return pl.pallas_call(
        body,
        out_shape=jax.ShapeDtypeStruct((S, D_MODEL), BF16),
        grid=(nhp, S // Q_BLOCK),
        in_specs=[pl.BlockSpec((Q_BLOCK, HEAD_PAIR), lambda h, i: (i, h)),
                  pl.BlockSpec((S, HEAD_PAIR), lambda h, i: (0, nhp + h)),
                  pl.BlockSpec((S, HEAD_PAIR), lambda h, i: (0, 2 * nhp + h)),
                  pl.BlockSpec((2, Q_BLOCK, KV_SPAN), lambda h, i: (h, 0, 0))],
        out_specs=pl.BlockSpec((Q_BLOCK, HEAD_PAIR), lambda h, i: (i, h)),
        scratch_shapes=[pltpu.VMEM((KV_PAD + S, HEAD_PAIR), BF16), pltpu.VMEM((KV_PAD + S, HEAD_PAIR), BF16),
                        pltpu.VMEM((2, Q_BLOCK, KV_SPAN), BF16)],
        compiler_params=_cparams(("parallel", "arbitrary")),
        name="attn_fwd",
    )(qkv, qkv, qkv, bias)


def _attn_bwd(qkv, bias, do):
    S = qkv.shape[0]
    nhp = N_HEADS // 2
    nq = S // Q_BLOCK
    scale = HEAD_DIM ** -0.5

    def body(q_ref, k_ref, v_ref, b_ref, do_ref, dq_ref, dk_ref, dv_ref, db_ref, kp, vp, dka, dva,
             p_scr, ds_scr):
        qb = pl.program_id(1)
        _pad_keys(qb, k_ref, v_ref, kp, vp)

        @pl.when(qb == 0)
        def _():
            dka[...] = jnp.zeros_like(dka)
            dva[...] = jnp.zeros_like(dva)
            db_ref[...] = jnp.zeros_like(db_ref)

        span = pl.ds(pl.multiple_of(qb * Q_BLOCK, Q_BLOCK), KV_SPAN)
        kc, vc = kp[span, :], vp[span, :]
        qt, dot = q_ref[...] * ATT_SCALE, do_ref[...]
        first = lax.broadcasted_iota(jnp.int32, (1, HEAD_PAIR), 1) < HEAD_DIM
        dqs = []
        qs = [jnp.where(first if j == 0 else ~first, qt, jnp.zeros_like(qt)) for j in range(2)]
        dos = [jnp.where(first if j == 0 else ~first, dot, jnp.zeros_like(dot)) for j in range(2)]
        scores = [_bdot(qs[j], kc, NT) for j in range(2)]
        dps = [_bdot(dos[j], vc, NT) for j in range(2)]
        for j in range(2):
            qj, doj = qs[j], dos[j]
            for r0 in range(0, Q_BLOCK, ATT_ROWS):
                rows = pl.ds(r0, ATT_ROWS)
                p = _softmax_piece(scores[j][r0:r0 + ATT_ROWS], b_ref[j, rows, :], qb)
                dp = dps[j][r0:r0 + ATT_ROWS]
                ds = p * (dp - jnp.sum(p * dp, axis=-1, keepdims=True))
                db_ref[j, rows, :] += ds
                p_scr[j, rows, :] = p.astype(BF16)
                ds_scr[j, rows, :] = ds.astype(BF16)
            dva[span, :] += _bdot(p_scr[j], doj, TN)
            dqs.append(_bdot(ds_scr[j], kc))
            dka[span, :] += _bdot(ds_scr[j], qj, TN)
        dq_ref[...] = (scale * jnp.where(first, dqs[0], dqs[1])).astype(BF16)

        @pl.when(qb == nq - 1)
        def _():
            dk_ref[...] = dka[pl.ds(KV_PAD, S), :].astype(BF16)
            dv_ref[...] = dva[pl.ds(KV_PAD, S), :].astype(BF16)

    blk = pl.BlockSpec((Q_BLOCK, HEAD_PAIR), lambda h, i: (i, h))
    col = pl.BlockSpec((S, HEAD_PAIR), lambda h, i: (0, h))
    bsp = pl.BlockSpec((2, Q_BLOCK, KV_SPAN), lambda h, i: (h, 0, 0))
    return pl.pallas_call(
        body,
        out_shape=[jax.ShapeDtypeStruct((S, D_MODEL), BF16)] * 3
        + [jax.ShapeDtypeStruct((N_HEADS, Q_BLOCK, KV_SPAN), F32)],
        grid=(nhp, nq),
        in_specs=[blk, pl.BlockSpec((S, HEAD_PAIR), lambda h, i: (0, nhp + h)),
                  pl.BlockSpec((S, HEAD_PAIR), lambda h, i: (0, 2 * nhp + h)), bsp, blk],
        out_specs=[blk, col, col, bsp],
        scratch_shapes=[pltpu.VMEM((KV_PAD + S, HEAD_PAIR), BF16), pltpu.VMEM((KV_PAD + S, HEAD_PAIR), BF16),
                        pltpu.VMEM((KV_PAD + S, HEAD_PAIR), F32), pltpu.VMEM((KV_PAD + S, HEAD_PAIR), F32),
                        pltpu.VMEM((2, Q_BLOCK, KV_SPAN), BF16), pltpu.VMEM((2, Q_BLOCK, KV_SPAN), BF16)],
        compiler_params=_cparams(("parallel", "arbitrary")),
        name="attn_bwd",
    )(qkv, qkv, qkv, bias, do)


N_DIST = BAND + CHUNK - 1
N_FAR = KV_PAD + CHUNK - MAX_REL


def _shear_rows(x, towards_right):
    row = lax.broadcasted_iota(jnp.int32, (Q_BLOCK, 1), 0)
    for bit in range(Q_BLOCK.bit_length() - 1):
        step = 1 << bit
        x = jnp.where((row & step) != 0, pltpu.roll(x, step if towards_right else KV_SPAN - step, 1), x)
    return x


def _bias_blocks(rel_bias):
    H = rel_bias.shape[0]
    e = jnp.concatenate([jnp.broadcast_to(rel_bias[:, 2 * MAX_REL:], (H, N_FAR)),
                         jnp.flip(rel_bias[:, 2 * MAX_REL - (N_DIST - N_FAR):2 * MAX_REL], axis=1),
                         jnp.zeros((H, KV_SPAN - N_DIST), F32)], axis=1).reshape(H, 1, KV_SPAN)

    def body(e_ref, o_ref):
        first = pltpu.roll(jnp.broadcast_to(e_ref[...], (Q_BLOCK, KV_SPAN)), KV_SPAN - (CHUNK - 1), 1)
        x = _shear_rows(first, True)
        row = lax.broadcasted_iota(jnp.int32, (Q_BLOCK, 1), 0)
        chunk0 = row - (row & (CHUNK - 1))
        k = lax.broadcasted_iota(jnp.int32, (1, KV_SPAN), 1)
        o_ref[...] = jnp.where((k >= chunk0) & (k < chunk0 + BAND), x, NEG_INF)

    return pl.pallas_call(
        body,
        out_shape=jax.ShapeDtypeStruct((H, Q_BLOCK, KV_SPAN), F32),
        grid=(H,),
        in_specs=[pl.BlockSpec((None, 1, KV_SPAN), lambda h: (h, 0, 0))],
        out_specs=pl.BlockSpec((None, Q_BLOCK, KV_SPAN), lambda h: (h, 0, 0)),
        compiler_params=_cparams(("parallel",)),
        name="bias_blocks",
    )(e)


def _bias_blocks_grad(dblk):
    H = dblk.shape[0]

    def body(d_ref, o_ref):
        x = pltpu.roll(_shear_rows(d_ref[...], False), CHUNK - 1, 1)
        de = jnp.sum(x, axis=0, keepdims=True)
        lane = lax.broadcasted_iota(jnp.int32, de.shape, 1)
        far = jnp.sum(jnp.where(lane < N_FAR, de, 0.0), axis=-1, keepdims=True)
        o_ref[...] = jnp.where(lane == 0, far, jnp.where(lane < N_FAR, 0.0, de))

    de = pl.pallas_call(
        body,
        out_shape=jax.ShapeDtypeStruct((H, 1, KV_SPAN), F32),
        grid=(H,),
        in_specs=[pl.BlockSpec((None, Q_BLOCK, KV_SPAN), lambda h: (h, 0, 0))],
        out_specs=pl.BlockSpec((None, 1, KV_SPAN), lambda h: (h, 0, 0)),
        compiler_params=_cparams(("parallel",)),
        name="bias_grad_sum",
    )(dblk).reshape(H, KV_SPAN)
    near = jnp.flip(de[:, N_FAR:N_DIST], axis=1)
    return jnp.concatenate([jnp.zeros((H, 2 * MAX_REL - (N_DIST - N_FAR)), F32), near, de[:, 0:1]], axis=1)


def _ffn_forward(r1, r1b, p_l, w, l, ready):
    ready(f"up{l}", r1b)
    up_g = _mm_rows([(r1b, w["ffn_up_t"][l], True, (0, 2))], out_dtype=BF16, name=f"ffn_up_g{l}")
    up_v = _mm_rows([(r1b, w["ffn_up_t"][l], True, (1, 2))], out_dtype=BF16, name=f"ffn_up_v{l}")
    h = _ffn_act_fwd(up_g, up_v, w["ffn_dw_w"][l], w["ffn_dw_b"][l], name=f"ffn_act{l}")
    ready(f"dn{l}", h)
    z2, r2, r2b, gate, proj = _proj_ln(r1, h, w["ffn_w_down"][l], w["ln_ffn_g"][l], w["ln_ffn_b"][l],
                                       ple=(w["ple_w_gate"][l], w["ple_b_gate"][l], p_l, w["ple_w_proj"][l]),
                                       name=f"ffn_down_ln{l}")
    return dict(r1b=r1b, up_g=up_g, up_v=up_v, h=h, z2=z2, gate=gate, proj=proj), r2, r2b


def _ffn_backward(sv, dz2, dz2b, p_l, w, l, grads, ln_bwd, emit):
    r1b = sv["r1b"]
    ds, dproj, db_gate = _ple_bwd(dz2, sv["gate"], sv["proj"], name=f"ple_bwd{l}")
    dh = _mm_rows([(dz2b, w["ffn_w_down"][l], True, WHOLE)], out_dtype=BF16, name=f"ffn_dh{l}")
    dgate, dval, d_dw_w, d_dw_b = _ffn_act_bwd(sv["up_g"], sv["up_v"], dh, w["ffn_dw_w"][l], w["ffn_dw_b"][l],
                                               name=f"ffn_act_bwd{l}")
    grads["ffn_w_down"][l] = _wgrad(sv["h"], dz2b, tm=1408, name=f"d_ffn_w_down{l}")
    d_up_g = _wgrad(dgate, r1b, tm=1408, part=(0, 2), name=f"d_ffn_up_g{l}")
    grads["ffn_up_t"][l] = _wgrad(dval, r1b, tm=1408, part=(1, 2), into=d_up_g, name=f"d_ffn_up_v{l}")
    grads["ple_w_gate"][l] = _wgrad(r1b, ds, name=f"d_ple_w_gate{l}")
    grads["ple_w_proj"][l] = _wgrad(p_l, dproj, piece=D_MODEL // N_DEV, name=f"d_ple_w_proj{l}")
    grads["ffn_dw_w"][l] = d_dw_w
    grads["ffn_dw_b"][l] = d_dw_b[0]
    grads["ple_b_gate"][l] = db_gate[0]
    return _mm_rows([(ds, w["ple_w_gate"][l], True, WHOLE), (dgate, w["ffn_up_t"][l], False, (0, 2)),
                     (dval, w["ffn_up_t"][l], False, (1, 2))], add=dz2, add_scale=ALPHA, ln_bwd=ln_bwd, dep=emit(),
                    name=f"dr1_{l}")


def _local_step(x, p, target, w, ready=lambda group, after: None, emit=lambda group, grads: None):
    grads = {k: [None, None] for k in ("ffn_w_down", "ffn_up_t", "ple_w_gate", "ple_w_proj", "ffn_dw_w",
                                       "ffn_dw_b", "ple_b_gate", "ln_ffn_g", "ln_ffn_b", "ln_mix_g", "ln_mix_b")}

    xb, pb = x.astype(BF16), p.astype(BF16)
    ready("mix", None)
    u = _mm_rows([(xb, w["mix_w_in_t"], True, WHOLE)], name="mix_in")
    ycat, dpool, hconv = _mixer_fwd(u, w["pool_w"], w["pool_scale"], w["conv_dw_w"], w["conv_dw_b"], w["conv_ln_g"],
                                    w["conv_ln_b"])
    ready("mixo", ycat)
    z1, r1, r1b = _proj_ln(x, ycat, w["mix_w_out"], w["ln_mix_g"][0], w["ln_mix_b"][0], name="mix_out_ln")
    sv0, r2, r2b = _ffn_forward(r1, r1b, pb[0], w, 0, ready)

    ready("attn", r2b)
    qkv = _mm_rows([(r2b, w["attn_w_qkv"], False, WHOLE)], out_dtype=BF16, name="attn_qkv")
    bias = _bias_blocks(w["attn_rel_bias"])
    attn = _attn_fwd(qkv, bias)
    z3, r3, r3b = _proj_ln(r2, attn, w["attn_w_o"], w["ln_mix_g"][1], w["ln_mix_b"][1], name="attn_out_ln")
    sv1, _, _ = _ffn_forward(r3, r3b, pb[1], w, 1, ready)

    dz4, dz4b, grads["ln_ffn_g"][1], grads["ln_ffn_b"][1], loss = _loss_ln_bwd(
        sv1["z2"], w["ln_ffn_g"][1], w["ln_ffn_b"][1], target, name="loss_ln_bwd")
    dz3, dz3b, grads["ln_mix_g"][1], grads["ln_mix_b"][1] = _ffn_backward(
        sv1, dz4, dz4b, pb[1], w, 1, grads, (z3, w["ln_mix_g"][1]), lambda: emit("ffn1", grads))
    grads["attn_w_o"] = _wgrad(attn, dz3b, name="d_attn_w_o")
    dattn = _mm_rows([(dz3b, w["attn_w_o"], True, WHOLE)], out_dtype=BF16, name="d_attn")
    dq, dk, dv, dbias = _attn_bwd(qkv, bias, dattn)
    grads["attn_rel_bias"] = _bias_blocks_grad(dbias)
    dqkv = jnp.concatenate([dq, dk, dv], axis=1)
    grads["attn_w_qkv"] = _wgrad(r2b, dqkv, tn=768, piece=3 * D_MODEL // N_DEV, name="d_attn_w_qkv")
    dz2, dz2b, grads["ln_ffn_g"][0], grads["ln_ffn_b"][0] = _mm_rows(
        [(dqkv, w["attn_w_qkv"], True, WHOLE)], add=dz3, add_scale=ALPHA, ln_bwd=(sv0["z2"], w["ln_ffn_g"][0]),
        dep=emit("attn", grads), name="dr2")
    dz1, dz1b, grads["ln_mix_g"][0], grads["ln_mix_b"][0] = _ffn_backward(
        sv0, dz2, dz2b, pb[0], w, 0, grads, (z1, w["ln_mix_g"][0]), lambda: emit("ffn0", grads))
    grads["mix_w_out"] = _wgrad(ycat, dz1b, name="d_mix_w_out")
    dycat = _mm_rows([(dz1b, w["mix_w_out"], True, WHOLE)], name="d_ycat")
    du, g_pw, g_ps, g_cw, g_cb, g_cg, g_cbb = _mixer_bwd(u, dpool, hconv, dycat, w["pool_w"], w["pool_scale"],
                                                         w["conv_dw_w"], w["conv_ln_g"], w["conv_ln_b"])
    grads["mix_w_in_t"] = _wgrad(du, xb, name="d_mix_w_in")
    grads.update(pool_w=g_pw, pool_scale=g_ps[0], conv_dw_w=g_cw, conv_dw_b=g_cb[0], conv_ln_g=g_cg[0],
                 conv_ln_b=g_cbb[0])
    for kname in ("ln_ffn_g", "ln_ffn_b", "ln_mix_g", "ln_mix_b"):
        grads[kname] = [a[0] for a in grads[kname]]
    grad_x = _mm_rows([(du, w["mix_w_in_t"], False, WHOLE)], add=dz1, add_scale=ALPHA, dep=emit("mix", grads),
                      name="grad_x")
    return loss[0, 0], grad_x, grads


_HBM = pl.BlockSpec(memory_space=pltpu.HBM)
_SEM = pl.BlockSpec(memory_space=pltpu.SEMAPHORE)
_EFFECT = pltpu.SideEffectType.DATAFLOW_SIDE_EFFECTING


def _slot(ref, place, shape, k):
    if place in ("stack", "pieces"):
        return ref.at[k]
    ax = place[1]
    n = shape[ax]
    return ref.at[(slice(None),) * ax + (pl.ds(pl.multiple_of(k * n, n), n),)]


def _result_shape(buf, place):
    if place == "stack":
        return (N_DEV,) + buf.shape
    if place == "pieces":
        return buf.shape
    return tuple(s * N_DEV if i == place[1] else s for i, s in enumerate(buf.shape))


def _peers(x, y, c):
    for d in range(1, N_DEV):
        px, py, pc = x ^ ((d >> 2) & 1), y ^ ((d >> 1) & 1), c ^ (d & 1)
        yield d, (px, py, pc), 4 * px + 2 * py + pc


def _exchange_start(bufs, places, after, *, name):
    nb = len(bufs)
    lands = [lax.empty(_result_shape(b, p_), b.dtype) for b, p_ in zip(bufs, places)]
    has_after = after is not None

    def body(*refs):
        srcs, dsts = refs[:nb], refs[nb:2 * nb]
        outs = refs[2 * nb + has_after:]
        send_sems, recv_sems, token = outs[0], outs[1], outs[2 + 2 * nb]
        x, y, c = lax.axis_index("x"), lax.axis_index("y"), lax.axis_index("c")
        me = 4 * x + 2 * y + c
        for b in range(nb):
            for d, dev, peer in _peers(x, y, c):
                pltpu.make_async_remote_copy(
                    src_ref=srcs[b].at[peer] if places[b] == "pieces" else srcs[b],
                    dst_ref=_slot(dsts[b], places[b], bufs[b].shape, me),
                    send_sem=send_sems.at[b * N_DEV + d], recv_sem=recv_sems.at[b * N_DEV + d],
                    device_id=dev, device_id_type=pl.DeviceIdType.MESH).start()
            pltpu.make_async_copy(srcs[b].at[me] if places[b] == "pieces" else srcs[b],
                                  _slot(dsts[b], places[b], bufs[b].shape, me), recv_sems.at[b * N_DEV]).start()
        token[...] = jnp.zeros_like(token)

    sems = pltpu.SemaphoreType.DMA((nb * N_DEV,))
    ins = [pltpu.with_memory_space_constraint(a, pltpu.HBM) for a in list(bufs) + lands]
    out = pl.pallas_call(
        body,
        out_shape=(sems, sems, *[pltpu.HBM(a.shape, a.dtype) for a in ins], jax.ShapeDtypeStruct((8, 128), F32)),
        in_specs=[_HBM] * (2 * nb) + ([pl.BlockSpec(memory_space=pl.ANY)] if has_after else []),
        out_specs=(_SEM, _SEM, *[_HBM] * (2 * nb), pl.BlockSpec(memory_space=pltpu.VMEM)),
        input_output_aliases={i: 2 + i for i in range(2 * nb)},
        compiler_params=pltpu.CompilerParams(has_side_effects=_EFFECT),
        name=name,
    )(*ins, *([after] if has_after else []))
    return dict(send=out[0], recv=out[1], srcs=out[2:2 + nb], lands=out[2 + nb:2 + 2 * nb], token=out[-1],
                places=places)


def _exchange_wait(h, after, *, name):
    nb = len(h["srcs"])
    places = h["places"]
    shapes = [a.shape for a in h["srcs"]]

    def body(*refs):
        srcs, dsts, send_sems, recv_sems = refs[:nb], refs[nb:2 * nb], refs[2 * nb], refs[2 * nb + 1]
        x, y, c = lax.axis_index("x"), lax.axis_index("y"), lax.axis_index("c")
        me = 4 * x + 2 * y + c
        for b in range(nb):
            pieces = places[b] == "pieces"
            for d, dev, peer in _peers(x, y, c):
                cp = pltpu.make_async_remote_copy(
                    src_ref=srcs[b].at[peer] if pieces else srcs[b],
                    dst_ref=_slot(dsts[b], places[b], shapes[b], peer),
                    send_sem=send_sems.at[b * N_DEV + d], recv_sem=recv_sems.at[b * N_DEV + d],
                    device_id=dev, device_id_type=pl.DeviceIdType.MESH)
                cp.wait_send()
                cp.wait_recv()
            pltpu.make_async_copy(srcs[b].at[me] if pieces else srcs[b], _slot(dsts[b], places[b], shapes[b], me),
                                  recv_sems.at[b * N_DEV]).wait()

    ins = list(h["srcs"]) + list(h["lands"])
    out = pl.pallas_call(
        body,
        out_shape=tuple(pltpu.HBM(a.shape, a.dtype) for a in ins),
        in_specs=[_HBM] * (2 * nb) + [_SEM, _SEM, pl.BlockSpec(memory_space=pl.ANY)],
        out_specs=tuple([_HBM] * (2 * nb)),
        input_output_aliases={i: i for i in range(2 * nb)},
        compiler_params=pltpu.CompilerParams(has_side_effects=_EFFECT),
        name=name,
    )(*ins, h["send"], h["recv"], after)
    return out[nb:]


def _adamw(recv, w, m, v, *, layer=0, into=None, name):
    L, R, C = w.shape
    tr = R
    for cand in (512, 256, 128, 64, 32, 16):
        if R % cand == 0 and cand * C * 4 <= 2 * 1024 * 1024:
            tr = cand
            break
    c1 = 1.0 - ADAM_B1 ** ADAM_STEP
    c2 = 1.0 - ADAM_B2 ** ADAM_STEP

    def body(r_ref, w_ref, m_ref, v_ref, *rest):
        g_ref, d_ref, mo_ref, vo_ref = rest[-4:]
        g = r_ref[0].astype(F32)
        for i in range(1, N_DEV):
            g = g + r_ref[i].astype(F32)
        m_new = ADAM_B1 * m_ref[...] + (1.0 - ADAM_B1) * g
        v_new = ADAM_B2 * v_ref[...] + (1.0 - ADAM_B2) * (g * g)
        m_hat = m_new / c1
        v_hat = v_new / c2
        g_ref[...] = g
        d_ref[...] = -ADAM_LR * (m_hat / (jnp.sqrt(v_hat) + ADAM_EPS) + ADAM_WD * w_ref[...])
        mo_ref[...] = m_new
        vo_ref[...] = v_new

    row = pl.BlockSpec((None, tr, C), lambda i: (layer, i, 0))
    others = [] if into is None else list(into)
    return pl.pallas_call(
        body,
        out_shape=[jax.ShapeDtypeStruct((L, R, C), F32)] * 4,
        grid=(R // tr,),
        in_specs=[pl.BlockSpec((N_DEV, tr, C), lambda i: (0, i, 0)), row, row, row]
        + [pl.BlockSpec(memory_space=pl.ANY)] * len(others),
        out_specs=[row] * 4,
        input_output_aliases={4 + k: k for k in range(len(others))},
        compiler_params=_cparams(("parallel",)),
        name=name,
    )(recv, w, m, v, *others)


_TRANSPOSED = ("mix_w_in", "ffn_w_up")


def _ffn_groups(l):
    return ((f"up{l}", (("ffn_w_up", l, BF16, ("axis", 0)), ("ffn_dw_w", l, F32, "stack"))),
            (f"dn{l}", (("ffn_w_down", l, BF16, ("axis", 0)), ("ple_w_gate", l, BF16, ("axis", 0)),
                        ("ple_w_proj", l, BF16, ("axis", 1)))))


_GATHER_GROUPS = (
    ("mix", (("mix_w_in", 0, BF16, ("axis", 0)), ("conv_dw_w", 0, F32, "stack"))),
    ("mixo", (("mix_w_out", 0, BF16, ("axis", 0)),)),
    *_ffn_groups(0),
    ("attn", (("attn_w_qkv", 0, BF16, ("axis", 1)), ("attn_w_o", 0, BF16, ("axis", 0)))),
    *_ffn_groups(1))
_SHARDED = ("mix_w_in", "conv_dw_w", "mix_w_out", "attn_w_qkv", "attn_w_o", "ffn_w_up", "ffn_dw_w", "ffn_w_down",
            "ple_w_gate", "ple_w_proj")
_REPLICATED = ("pool_w", "pool_scale", "conv_dw_b", "conv_ln_g", "conv_ln_b", "attn_rel_bias", "ln_mix_g",
               "ln_mix_b", "ffn_dw_b", "ple_b_gate", "ln_ffn_g", "ln_ffn_b")


def _pack_rows(parts, row_mult, dtype):
    lead = parts[0].shape[:-1]
    flat = jnp.concatenate([a.astype(dtype) for a in parts], axis=-1)
    n = flat.shape[-1]
    unit = row_mult * LANES
    padded = -(-n // unit) * unit
    flat = jnp.pad(flat, [(0, 0)] * len(lead) + [(0, padded - n)])
    return flat.reshape(lead + (padded // LANES, LANES))


def _unpack(flat2d, shapes):
    flat = flat2d.reshape(-1)
    out, o = [], 0
    for s in shapes:
        n = math.prod(s)
        out.append(flat[o:o + n].reshape(s))
        o += n
    return out


def _full_from_shards(g, axis):
    parts = jnp.moveaxis(g, 0, axis)
    shp = list(g.shape[1:])
    shp[axis] *= g.shape[0]
    return parts.reshape(shp)


def _pieces_from_full(full, axis, k=N_DEV):
    shp = list(full.shape)
    n = shp[axis] // k
    t = full.reshape(shp[:axis] + [k, n] + shp[axis + 1:])
    return jnp.moveaxis(t, axis, 0)


def kernel(x, p, mix_w_in, pool_w, pool_scale, conv_dw_w, conv_dw_b, conv_ln_g, conv_ln_b, mix_w_out, attn_w_qkv, attn_rel_bias, attn_w_o, ln_mix_g, ln_mix_b, ffn_w_up, ffn_dw_w, ffn_dw_b, ffn_w_down, ple_w_proj, ple_w_gate, ple_b_gate, ln_ffn_g, ln_ffn_b, loss_target, m_mix_w_in, m_pool_w, m_pool_scale, m_conv_dw_w, m_conv_dw_b, m_conv_ln_g, m_conv_ln_b, m_mix_w_out, m_attn_w_qkv, m_attn_rel_bias, m_attn_w_o, m_ln_mix_g, m_ln_mix_b, m_ffn_w_up, m_ffn_dw_w, m_ffn_dw_b, m_ffn_w_down, m_ple_w_proj, m_ple_w_gate, m_ple_b_gate, m_ln_ffn_g, m_ln_ffn_b, v_mix_w_in, v_pool_w, v_pool_scale, v_conv_dw_w, v_conv_dw_b, v_conv_ln_g, v_conv_ln_b, v_mix_w_out, v_attn_w_qkv, v_attn_rel_bias, v_attn_w_o, v_ln_mix_g, v_ln_mix_b, v_ffn_w_up, v_ffn_dw_w, v_ffn_dw_b, v_ffn_w_down, v_ple_w_proj, v_ple_w_gate, v_ple_b_gate, v_ln_ffn_g, v_ln_ffn_b):
    a = dict(locals())
    sh_names = list(_SHARDED)
    names = sh_names + list(_REPLICATED)
    wts = {n: a[n] for n in names}
    mom = {n: a["m_" + n] for n in names}
    var = {n: a["v_" + n] for n in names}

    for n in _TRANSPOSED:
        wts[n], mom[n], var[n] = (jnp.swapaxes(d[n], 1, 2) for d in (wts, mom, var))
    gather = {}
    token = None
    for group, items in _GATHER_GROUPS:
        gather[group] = _exchange_start([wts[n][l].astype(dt) for n, l, dt, _ in items], [pl_ for *_, pl_ in items],
                                        token, name="gather_start_" + group)
        token = gather[group]["token"]

    w = dict(pool_w=pool_w[0], pool_scale=pool_scale[0], conv_dw_b=conv_dw_b[0], conv_ln_g=conv_ln_g[0],
             conv_ln_b=conv_ln_b[0], attn_rel_bias=attn_rel_bias[0], ln_mix_g=ln_mix_g, ln_mix_b=ln_mix_b,
             ffn_dw_b=ffn_dw_b, ple_b_gate=ple_b_gate, ln_ffn_g=ln_ffn_g, ln_ffn_b=ln_ffn_b)
    for n in ("ffn_up_t", "ffn_dw_w", "ffn_w_down", "ple_w_gate", "ple_w_proj"):
        w[n] = [None, None]

    def ready(group, after):
        got = _exchange_wait(gather[group], token if after is None else after, name="gather_wait_" + group)
        if group == "mix":
            w["mix_w_in_t"], w["conv_dw_w"] = got[0], _full_from_shards(got[1], 1)
        elif group == "mixo":
            (w["mix_w_out"],) = got
        elif group == "attn":
            w["attn_w_qkv"], w["attn_w_o"] = got
        elif group[:2] == "up":
            l = int(group[2])
            w["ffn_up_t"][l], w["ffn_dw_w"][l] = got[0], _full_from_shards(got[1], 1)
        else:
            l = int(group[2])
            w["ffn_w_down"][l], w["ple_w_gate"][l], w["ple_w_proj"][l] = got

    scatter = {}

    def emit(group, gr):
        if group[:3] == "ffn":
            l = int(group[3])
            pieces = [_pieces_from_full(gr["ffn_up_t"][l], 0),
                      _pieces_from_full(gr["ffn_dw_w"][l], 1), _pieces_from_full(gr["ffn_w_down"][l], 0),
                      _pieces_from_full(gr["ple_w_gate"][l], 0), gr["ple_w_proj"][l]]
        elif group == "attn":
            pieces = [gr["attn_w_qkv"], _pieces_from_full(gr["attn_w_o"], 0)]
        else:
            pieces = [_pieces_from_full(gr["mix_w_in_t"], 0), _pieces_from_full(gr["conv_dw_w"], 1),
                      _pieces_from_full(gr["mix_w_out"], 0)]
        scatter[group] = _exchange_start([a.astype(BF16) for a in pieces], ["pieces"] * len(pieces), None,
                                         name="grad_start_" + group)
        if group != "mix":
            return scatter[group]["token"]
        gfull = dict(
            pool_w=gr["pool_w"][None], pool_scale=gr["pool_scale"][None], conv_dw_b=gr["conv_dw_b"][None],
            conv_ln_g=gr["conv_ln_g"][None], conv_ln_b=gr["conv_ln_b"][None],
            attn_rel_bias=gr["attn_rel_bias"][None], ln_mix_g=jnp.stack(gr["ln_mix_g"]),
            ln_mix_b=jnp.stack(gr["ln_mix_b"]), ffn_dw_b=jnp.stack(gr["ffn_dw_b"]),
            ple_b_gate=jnp.stack(gr["ple_b_gate"]), ln_ffn_g=jnp.stack(gr["ln_ffn_g"]),
            ln_ffn_b=jnp.stack(gr["ln_ffn_b"]))
        rep_send = _pack_rows([gfull[n].reshape(-1) for n in _REPLICATED], 8, F32)
        scatter["replicated"] = _exchange_start([rep_send], ["stack"], scatter[group]["token"],
                                                name="grad_start_replicated")
        return scatter["replicated"]["token"]

    loss_part, grad_x, gr = _local_step(x[0], p[:, 0], loss_target[0], w, ready, emit)
    loss = lax.psum(loss_part, ("x", "y", "c"))

    group_weights = {"ffn1": (("ffn_w_up", 1), ("ffn_dw_w", 1), ("ffn_w_down", 1), ("ple_w_gate", 1), ("ple_w_proj", 1)),
                     "attn": (("attn_w_qkv", 0), ("attn_w_o", 0)),
                     "ffn0": (("ffn_w_up", 0), ("ffn_dw_w", 0), ("ffn_w_down", 0), ("ple_w_gate", 0), ("ple_w_proj", 0)),
                     "mix": (("mix_w_in", 0), ("conv_dw_w", 0), ("mix_w_out", 0))}
    updated = {}
    after = grad_x
    for group in ("ffn1", "attn", "ffn0", "mix"):
        recv = _exchange_wait(scatter[group], after, name="grad_wait_" + group)
        for (n, l), r in zip(group_weights[group], recv):
            updated[n] = _adamw(r, wts[n], mom[n], var[n], layer=l, into=updated.get(n), name=f"adamw_{n}{l}")
            after = updated[n][0]
    res = [{n: jnp.swapaxes(updated[n][k], 1, 2) if n in _TRANSPOSED else updated[n][k] for n in sh_names}
           for k in range(4)]
    (rep_recv,) = _exchange_wait(scatter["replicated"], after, name="grad_wait_replicated")

    def flat_state(d):
        return _pack_rows([d[n].reshape(-1) for n in _REPLICATED], 8, F32)[None]

    rep_out = _adamw(rep_recv, flat_state(wts), flat_state(mom), flat_state(var), name="adamw_replicated")
    for k in range(4):
        for n, arr in zip(_REPLICATED, _unpack(rep_out[k][0], [wts[n].shape for n in _REPLICATED])):
            res[k][n] = arr
    order = ["mix_w_in", "pool_w", "pool_scale", "conv_dw_w", "conv_dw_b", "conv_ln_g", "conv_ln_b", "mix_w_out",
             "attn_w_qkv", "attn_rel_bias", "attn_w_o", "ln_mix_g", "ln_mix_b", "ffn_w_up", "ffn_dw_w", "ffn_dw_b",
             "ffn_w_down", "ple_w_proj", "ple_w_gate", "ple_b_gate", "ln_ffn_g", "ln_ffn_b"]
    outs = [loss, grad_x[None]]
    for k in range(4):
        outs += [res[k][n] for n in order]
    return tuple(outs)
```

```python
import functools
import math

import jax
import jax.numpy as jnp
from jax import lax
from jax.experimental import pallas as pl
from jax.experimental.pallas import tpu as pltpu

F32 = jnp.float32
BF16 = jnp.bfloat16

N_DEV = 8
D_MODEL = 1024
D_POOL = 512
D_CONV = 512
POOL_WINDOWS = (2, 4, 8, 16)
POOL_GROUP = 128
CONV_KERNEL = 31
CHUNK = 64
HEAD_DIM = 64
N_HEADS = 16
LEFT_CHUNKS = 8
BAND = (LEFT_CHUNKS + 1) * CHUNK
MAX_REL = 256
D_FF = 2816
PLE_DIM = 256
ALPHA = 4.0 ** 0.25
LN_EPS = 1e-5
NEG_INF = -1e30
ADAM_LR, ADAM_B1, ADAM_B2, ADAM_EPS, ADAM_WD, ADAM_STEP = 0.001, 0.9, 0.999, 1e-08, 0.01, 10

Q_BLOCK = 4 * CHUNK
KV_PAD = LEFT_CHUNKS * CHUNK
KV_SPAN = KV_PAD + Q_BLOCK
CONV_HALO = 32
FFN_HALO = 16
SUB_ROWS, SUB_LANES = 64, 128
LANES = 1024
VMEM_LIMIT = 56 * 1024 * 1024


def _cparams(sem=None):
    return pltpu.CompilerParams(dimension_semantics=sem, vmem_limit_bytes=VMEM_LIMIT)


def _tile(dim, pref):
    if dim <= pref:
        return dim
    t = pref - pref % 128
    while t >= 128:
        if dim % t == 0:
            return t
        t -= 128
    return dim


def _sigmoid(x):
    return 1.0 / (1.0 + jnp.exp(-x))


def _bdot(a, b, dn=(((1,), (0,)), ((), ()))):
    return lax.dot_general(a.astype(BF16), b.astype(BF16), dn, preferred_element_type=F32)


WHOLE = (0, 1)
NT = (((1,), (1,)), ((), ()))
TN = (((0,), (0,)), ((), ()))


def _wgrad(a, b, *, tm=1024, tn=1024, tk=2048, piece=None, part=(0, 1), into=None, name):
    K, M = a.shape
    kb, N = b.shape
    assert K == kb, (a.shape, b.shape)
    tm, tn, tk = _tile(M, tm), _tile(N, tn), _tile(K, tk)
    nk = K // tk
    per = 1 if piece is None else tn // piece
    assert piece is None or tn == per * piece

    def body(a_ref, b_ref, *rest):
        o_ref, acc = rest[-2:]
        k = pl.program_id(2)

        @pl.when(k == 0)
        def _():
            acc[...] = jnp.zeros_like(acc)

        acc[...] += _bdot(a_ref[...], b_ref[...], TN)

        @pl.when(k == nk - 1)
        def _():
            if piece is None:
                o_ref[...] = acc[...].astype(BF16)
            else:
                for s in range(per):
                    o_ref[s] = acc[:, s * piece:(s + 1) * piece].astype(BF16)

    if piece is None:
        first = part[0] * (M // tm)
        out_shape = (part[1] * M, N)
        out_spec = pl.BlockSpec((tm, tn), lambda i, j, k: (first + i, j))
    else:
        out_shape, out_spec = (N // piece, M, piece), pl.BlockSpec((per, tm, piece), lambda i, j, k: (j, i, 0))
    others = [] if into is None else [into]
    return pl.pallas_call(
        body,
        out_shape=jax.ShapeDtypeStruct(out_shape, BF16),
        grid=(M // tm, N // tn, nk),
        in_specs=[pl.BlockSpec((tk, tm), lambda i, j, k: (k, i)), pl.BlockSpec((tk, tn), lambda i, j, k: (k, j))]
        + [pl.BlockSpec(memory_space=pl.ANY)] * len(others),
        out_specs=out_spec,
        input_output_aliases={2: 0} if others else {},
        scratch_shapes=[pltpu.VMEM((tm, tn), F32)],
        compiler_params=_cparams(("parallel", "parallel", "arbitrary")),
        name=name,
    )(a, b, *others)


def _mm_rows(pairs, *, add=None, add_scale=1.0, out_dtype=F32, tm=512, dep=None, ln_bwd=None, name):
    M = pairs[0][0].shape[0]
    n = len(pairs)
    has_add = add is not None
    w_rows = [w_.shape[0] // part[1] for _, w_, _, part in pairs]
    N = w_rows[0] if pairs[0][2] else pairs[0][1].shape[1]

    def body(*refs):
        acc = None
        for i, (_, _, tr, _) in enumerate(pairs):
            part = _bdot(refs[2 * i][...], refs[2 * i + 1][...], NT if tr else (((1,), (0,)), ((), ())))
            acc = part if acc is None else acc + part
        if has_add:
            acc = acc + add_scale * refs[2 * n][...]
        if ln_bwd is None:
            refs[-1][...] = acc.astype(out_dtype)
            return
        z_ref, g_ref = refs[2 * n + has_add], refs[2 * n + has_add + 1]
        dz_ref, dzb_ref, dg_ref, db_ref = refs[-4:]

        @pl.when(pl.program_id(0) == 0)
        def _():
            dg_ref[...] = jnp.zeros_like(dg_ref)
            db_ref[...] = jnp.zeros_like(db_ref)

        dg_acc = jnp.zeros((8, N), F32)
        db_acc = jnp.zeros((8, N), F32)
        for r0 in range(0, tm, LN_ROWS):
            rows = pl.ds(r0, LN_ROWS)
            do = acc[r0:r0 + LN_ROWS]
            dz, xh = _ln_bwd_rows(z_ref[rows, :], g_ref[...], do)
            dz_ref[rows, :] = dz
            dzb_ref[rows, :] = dz.astype(BF16)
            dg_acc = dg_acc + jnp.sum((do * xh).reshape(LN_ROWS // 8, 8, N), axis=0)
            db_acc = db_acc + jnp.sum(do.reshape(LN_ROWS // 8, 8, N), axis=0)
        dg_ref[...] += jnp.sum(dg_acc, axis=0, keepdims=True)
        db_ref[...] += jnp.sum(db_acc, axis=0, keepdims=True)

    in_specs, args = [], []
    for (a, w_, _, part), rows in zip(pairs, w_rows):
        in_specs += [pl.BlockSpec((tm, a.shape[1]), lambda i: (i, 0)),
                     pl.BlockSpec((rows, w_.shape[1]), functools.partial(lambda i, j: (j, 0), j=part[0]))]
        args += [a, w_]
    row = pl.BlockSpec((tm, N), lambda i: (i, 0))
    fix = pl.BlockSpec((1, N), lambda i: (0, 0))
    if has_add:
        in_specs.append(row)
        args.append(add)
    if ln_bwd is not None:
        in_specs += [row, fix]
        args += [ln_bwd[0], ln_bwd[1].reshape(1, N)]
    if dep is not None:
        in_specs.append(pl.BlockSpec(memory_space=pl.ANY))
        args.append(dep)
    if ln_bwd is None:
        out_shape, out_specs = jax.ShapeDtypeStruct((M, N), out_dtype), row
    else:
        out_shape = [jax.ShapeDtypeStruct((M, N), F32), jax.ShapeDtypeStruct((M, N), BF16),
                     jax.ShapeDtypeStruct((1, N), F32), jax.ShapeDtypeStruct((1, N), F32)]
        out_specs = [row, row, fix, fix]
    return pl.pallas_call(
        body,
        out_shape=out_shape,
        grid=(M // tm,),
        in_specs=in_specs,
        out_specs=out_specs,
        compiler_params=_cparams(("parallel",) if ln_bwd is None else ("arbitrary",)),
        name=name,
    )(*args)


def _ln_bwd_rows(zt, g, do):
    zc = zt - jnp.mean(zt, axis=-1, keepdims=True)
    rstd = lax.rsqrt(jnp.mean(zc * zc, axis=-1, keepdims=True) + LN_EPS)
    xh = zc * rstd
    dxh = do * g
    return rstd * (dxh - jnp.mean(dxh, axis=-1, keepdims=True) - xh * jnp.mean(dxh * xh, axis=-1, keepdims=True)), xh


def _layer_norm_rows(z, g, b):
    mu = jnp.mean(z, axis=-1, keepdims=True)
    zc = z - mu
    var = jnp.mean(zc * zc, axis=-1, keepdims=True)
    return zc * lax.rsqrt(var + LN_EPS) * g + b


def _proj_ln(res, a, w, ln_g, ln_b, *, ple=None, ts=512, name):
    S, D = res.shape
    ka = a.shape[1]
    has_ple = ple is not None
    row = lambda i: (i, 0)
    fix = lambda i: (0, 0)

    def body(*refs):
        if has_ple:
            (res_ref, a_ref, w_ref, g_ref, b_ref, wg_ref, bg_ref, p_ref, wp_ref, z_ref, r_ref, rb_ref, gate_ref,
             proj_ref, acc) = refs
        else:
            res_ref, a_ref, w_ref, g_ref, b_ref, z_ref, r_ref, rb_ref, acc = refs
        acc[...] = _bdot(a_ref[...], w_ref[...])
        if has_ple:
            gate_ref[...] = _bdot(res_ref[...], wg_ref[...])
            proj_ref[...] = _bdot(p_ref[...], wp_ref[...])
        for r0 in range(0, ts, LN_ROWS):
            rows = pl.ds(r0, LN_ROWS)
            z = ALPHA * res_ref[rows, :] + acc[rows, :]
            if has_ple:
                gate = _sigmoid(gate_ref[rows, :] + bg_ref[...])
                gate_ref[rows, :] = gate
                z = z + gate * proj_ref[rows, :]
            z_ref[rows, :] = z
            r = _layer_norm_rows(z, g_ref[...], b_ref[...])
            r_ref[rows, :] = r
            rb_ref[rows, :] = r.astype(BF16)

    in_specs = [pl.BlockSpec((ts, D), row), pl.BlockSpec((ts, ka), row), pl.BlockSpec((ka, D), fix),
                pl.BlockSpec((1, D), fix), pl.BlockSpec((1, D), fix)]
    args = [res, a, w, ln_g.reshape(1, D), ln_b.reshape(1, D)]
    out_dtypes = [F32, F32, BF16]
    if has_ple:
        wg, bg, p, wp = ple
        in_specs += [pl.BlockSpec((D, D), fix), pl.BlockSpec((1, D), fix), pl.BlockSpec((ts, PLE_DIM), row),
                     pl.BlockSpec((PLE_DIM, D), fix)]
        args += [wg, bg.reshape(1, D), p, wp]
        out_dtypes += [F32, F32]
    return pl.pallas_call(
        body,
        out_shape=[jax.ShapeDtypeStruct((S, D), dt) for dt in out_dtypes],
        grid=(S // ts,),
        in_specs=in_specs,
        out_specs=[pl.BlockSpec((ts, D), row)] * len(out_dtypes),
        scratch_shapes=[pltpu.VMEM((ts, D), F32)],
        compiler_params=_cparams(("parallel",)),
        name=name,
    )(*args)


CONV_ROWS = 32
LN_ROWS = 16


def _shifted_copies(src, dst, rows):
    for c0 in range(0, src.shape[1], SUB_LANES):
        ln = pl.ds(c0, SUB_LANES)
        for r0 in range(0, rows, SUB_ROWS):
            rc = min(SUB_ROWS, rows - r0)
            for b, shifted in enumerate(_rows_ahead(src, r0, rc, ln, range(1, 8))):
                dst[b, pl.ds(r0, rc), ln] = shifted


def _rows_at(src, copies, off, n, ln):
    b = off % 8
    return src[pl.ds(off, n), ln] if b == 0 else copies[b - 1, pl.ds(off - b, n), ln]


def _conv31(stg, gsh, cw_ref, cb_ref, out, rows, first_off):
    for c0 in range(0, D_CONV, SUB_LANES):
        ln = pl.ds(c0, SUB_LANES)
        for r0 in range(0, rows, CONV_ROWS):
            acc = jnp.zeros((CONV_ROWS, SUB_LANES), F32) + cb_ref[:, ln]
            for k in range(CONV_KERNEL):
                acc = acc + cw_ref[k:k + 1, ln] * _rows_at(stg, gsh, first_off + k + r0, CONV_ROWS, ln)
            out[pl.ds(r0, CONV_ROWS), ln] = acc


def _mixer_fwd(u, pool_w, pool_scale, conv_w, conv_b, cln_g, cln_b, *, ts=256):
    S = u.shape[0]
    hb = CONV_HALO
    nh = ts // hb

    def body(u_ref, uh_ref, pw_ref, ps_ref, cw_ref, cb_ref, g_ref, b_ref, y_ref, d_ref, hcs, sta, stg, gsh):
        i = pl.program_id(0)
        first = i == 0
        sta[pl.ds(0, hb), :] = jnp.where(first, 0.0, uh_ref[:, 0:D_POOL])
        sta[pl.ds(hb, ts), :] = u_ref[:, 0:D_POOL]
        glu_h = uh_ref[:, D_POOL:D_POOL + D_CONV] * _sigmoid(uh_ref[:, D_POOL + D_CONV:])
        stg[pl.ds(0, hb), :] = jnp.where(first, 0.0, glu_h)
        stg[pl.ds(hb, ts), :] = u_ref[:, D_POOL:D_POOL + D_CONV] * _sigmoid(u_ref[:, D_POOL + D_CONV:])

        for g, w in enumerate(POOL_WINDOWS):
            lanes = pl.ds(g * POOL_GROUP, POOL_GROUP)
            for r0 in range(0, ts, SUB_ROWS):
                s = None
                for q in range(0, w, 8):
                    for tap in _rows_back(sta, hb + r0 - q, SUB_ROWS, lanes, range(min(8, w - q))):
                        s = tap if s is None else s + tap
                pos = (i * ts + r0 + lax.broadcasted_iota(jnp.int32, (SUB_ROWS, 1), 0) + 1).astype(F32)
                d_g = s / jnp.minimum(pos, float(w)) - sta[pl.ds(hb + r0, SUB_ROWS), lanes]
                d_ref[pl.ds(r0, SUB_ROWS), lanes] = d_g.astype(BF16)
            y_ref[:, lanes] = (_bdot(d_ref[:, lanes], pw_ref[g]) * ps_ref[:, lanes]).astype(BF16)

        _shifted_copies(stg, gsh, hb + ts - 8)
        _conv31(stg, gsh, cw_ref, cb_ref, hcs, ts, hb - (CONV_KERNEL - 1))
        for r0 in range(0, ts, LN_ROWS):
            rows = pl.ds(r0, LN_ROWS)
            ln = _layer_norm_rows(hcs[rows, :], g_ref[...], b_ref[...])
            y_ref[rows, D_POOL:] = (ln * _sigmoid(ln)).astype(BF16)

    fix2 = lambda i: (0, 0)
    return pl.pallas_call(
        body,
        out_shape=[jax.ShapeDtypeStruct((S, D_MODEL), BF16), jax.ShapeDtypeStruct((S, D_POOL), BF16),
                   jax.ShapeDtypeStruct((S, D_CONV), F32)],
        grid=(S // ts,),
        in_specs=[pl.BlockSpec((ts, 3 * D_POOL), lambda i: (i, 0)),
                  pl.BlockSpec((hb, 3 * D_POOL), lambda i: (jnp.maximum(i * nh - 1, 0), 0)),
                  pl.BlockSpec((4, POOL_GROUP, POOL_GROUP), lambda i: (0, 0, 0)),
                  pl.BlockSpec((1, D_POOL), fix2), pl.BlockSpec((CONV_KERNEL, D_CONV), fix2),
                  pl.BlockSpec((1, D_CONV), fix2), pl.BlockSpec((1, D_CONV), fix2), pl.BlockSpec((1, D_CONV), fix2)],
        out_specs=[pl.BlockSpec((ts, D_MODEL), lambda i: (i, 0)), pl.BlockSpec((ts, D_POOL), lambda i: (i, 0)),
                   pl.BlockSpec((ts, D_CONV), lambda i: (i, 0))],
        scratch_shapes=[pltpu.VMEM((hb + ts, D_POOL), F32), pltpu.VMEM((hb + ts, D_CONV), F32),
                        pltpu.VMEM((7, hb + ts - 8, D_CONV), F32)],
        compiler_params=_cparams(("parallel",)),
        name="mixer_fwd",
    )(u, u, pool_w, pool_scale.reshape(1, D_POOL), conv_w, conv_b.reshape(1, D_CONV), cln_g.reshape(1, D_CONV),
      cln_b.reshape(1, D_CONV))


def _mixer_bwd(u, d, hc, dycat, pool_w, pool_scale, conv_w, cln_g, cln_b, *, ts=256):
    S = u.shape[0]
    hb = CONV_HALO
    nh = ts // hb
    n = S // ts
    te = ts + hb
    K = CONV_KERNEL

    def body(u_ref, up_ref, un_ref, d_ref, hc_ref, hcn_ref, dy_ref, dyn_ref, pw_ref, ps_ref, cw_ref, g_ref, b_ref,
             du_ref, dpw_ref, dps_ref, dcw_ref, dcb_ref, dg_ref, db_ref, stg, std, sth, gsh, hsh):
        i = pl.program_id(0)
        first = i == 0
        last = i == n - 1

        @pl.when(first)
        def _():
            dpw_ref[...] = jnp.zeros_like(dpw_ref)
            dps_ref[...] = jnp.zeros_like(dps_ref)
            dcw_ref[...] = jnp.zeros_like(dcw_ref)
            dcb_ref[...] = jnp.zeros_like(dcb_ref)
            dg_ref[...] = jnp.zeros_like(dg_ref)
            db_ref[...] = jnp.zeros_like(db_ref)

        pos_e = (i * ts + lax.broadcasted_iota(jnp.int32, (te, 1), 0) + 1).astype(F32)
        dya = dy_ref[:, 0:D_POOL]
        dya_n = jnp.where(last, 0.0, dyn_ref[:, 0:D_POOL])
        for g, w in enumerate(POOL_WINDOWS):
            lanes = pl.ds(g * POOL_GROUP, POOL_GROUP)
            sl = slice(g * POOL_GROUP, (g + 1) * POOL_GROUP)
            pw = pw_ref[g]
            scale = ps_ref[:, lanes]
            d_g = d_ref[:, lanes]
            pre = _bdot(d_g, pw)
            dps_ref[:, lanes] += jnp.sum(dya[:, sl] * pre, axis=0, keepdims=True)
            dys = dya[:, sl] * scale
            dpw_ref[g] += _bdot(d_g, dys, TN)
            dys_e = jnp.concatenate([dys, dya_n[:, sl] * scale], axis=0)
            dd = _bdot(dys_e, pw, NT)
            std[:, lanes] = dd / jnp.minimum(pos_e, float(w))
            for r0 in range(0, ts, SUB_ROWS):
                da = -dd[r0:r0 + SUB_ROWS]
                for q in range(0, w, 8):
                    for tap in _rows_ahead(std, r0 + q, SUB_ROWS, lanes, range(min(8, w - q))):
                        da = da + tap
                du_ref[pl.ds(r0, SUB_ROWS), lanes] = da.astype(BF16)

        glu_p = up_ref[:, D_POOL:D_POOL + D_CONV] * _sigmoid(up_ref[:, D_POOL + D_CONV:])
        stg[pl.ds(0, hb), :] = jnp.where(first, 0.0, glu_p)
        bv = u_ref[:, D_POOL:D_POOL + D_CONV]
        sg = _sigmoid(u_ref[:, D_POOL + D_CONV:])
        stg[pl.ds(hb, ts), :] = bv * sg
        glu_n = un_ref[:, D_POOL:D_POOL + D_CONV] * _sigmoid(un_ref[:, D_POOL + D_CONV:])
        stg[pl.ds(hb + ts, hb), :] = jnp.where(last, 0.0, glu_n)
        _shifted_copies(stg, gsh, hb + te - 8)

        sums = [jnp.zeros((8, D_CONV), F32) for _ in range(3)]
        for r0 in range(0, te, LN_ROWS):
            rows = pl.ds(r0, LN_ROWS)
            hc = hc_ref[rows, :] if r0 < ts else hcn_ref[pl.ds(r0 - ts, LN_ROWS), :]
            hcc = hc - jnp.mean(hc, axis=-1, keepdims=True)
            rstd = lax.rsqrt(jnp.mean(hcc * hcc, axis=-1, keepdims=True) + LN_EPS)
            xh = hcc * rstd
            ln = xh * g_ref[...] + b_ref[...]
            sl_ = _sigmoid(ln)
            if r0 < ts:
                dyb = dy_ref[rows, D_POOL:]
            else:
                dyb = jnp.where(last, 0.0, dyn_ref[pl.ds(r0 - ts, LN_ROWS), D_POOL:])
            dln = dyb * (sl_ * (1.0 + ln * (1.0 - sl_)))
            dxh = dln * g_ref[...]
            dhc = rstd * (dxh - jnp.mean(dxh, axis=-1, keepdims=True)
                          - xh * jnp.mean(dxh * xh, axis=-1, keepdims=True))
            sth[rows, :] = dhc
            if r0 < ts:
                for n_, term in enumerate((dln * xh, dln, dhc)):
                    sums[n_] = sums[n_] + jnp.sum(term.reshape(LN_ROWS // 8, 8, D_CONV), axis=0)
        dg_ref[...] += jnp.sum(sums[0], axis=0, keepdims=True)
        db_ref[...] += jnp.sum(sums[1], axis=0, keepdims=True)
        dcb_ref[...] += jnp.sum(sums[2], axis=0, keepdims=True)

        _shifted_copies(sth, hsh, te - 8)
        for c0 in range(0, D_CONV, SUB_LANES):
            ln_ = pl.ds(c0, SUB_LANES)
            for r0 in range(0, ts, CONV_ROWS):
                rows = pl.ds(r0, CONV_ROWS)
                dglu = jnp.zeros((CONV_ROWS, SUB_LANES), F32)
                for k in range(K):
                    dglu = dglu + cw_ref[k:k + 1, ln_] * _rows_at(sth, hsh, K - 1 - k + r0, CONV_ROWS, ln_)
                bv = u_ref[rows, pl.ds(D_POOL + c0, SUB_LANES)]
                sg = _sigmoid(u_ref[rows, pl.ds(D_POOL + D_CONV + c0, SUB_LANES)])
                du_ref[rows, pl.ds(D_POOL + c0, SUB_LANES)] = (dglu * sg).astype(BF16)
                du_ref[rows, pl.ds(D_POOL + D_CONV + c0, SUB_LANES)] = (dglu * bv * sg * (1.0 - sg)).astype(BF16)
            for k in range(K):
                tap = jnp.zeros((8, SUB_LANES), F32)
                for r0 in range(0, ts, CONV_ROWS):
                    prod = sth[pl.ds(r0, CONV_ROWS), ln_] * _rows_at(stg, gsh, hb - (K - 1) + k + r0, CONV_ROWS, ln_)
                    tap = tap + jnp.sum(prod.reshape(CONV_ROWS // 8, 8, SUB_LANES), axis=0)
                dcw_ref[k:k + 1, ln_] += jnp.sum(tap, axis=0, keepdims=True)

    fix2 = lambda i: (0, 0)
    prev = lambda i: (jnp.maximum(i * nh - 1, 0), 0)
    nxt = lambda i: (jnp.minimum((i + 1) * nh, S // hb - 1), 0)
    return pl.pallas_call(
        body,
        out_shape=[jax.ShapeDtypeStruct((S, 3 * D_POOL), BF16),
                   jax.ShapeDtypeStruct((4, POOL_GROUP, POOL_GROUP), F32),
                   jax.ShapeDtypeStruct((1, D_POOL), F32),
                   jax.ShapeDtypeStruct((K, D_CONV), F32),
                   jax.ShapeDtypeStruct((1, D_CONV), F32),
                   jax.ShapeDtypeStruct((1, D_CONV), F32),
                   jax.ShapeDtypeStruct((1, D_CONV), F32)],
        grid=(n,),
        in_specs=[pl.BlockSpec((ts, 3 * D_POOL), lambda i: (i, 0)),
                  pl.BlockSpec((hb, 3 * D_POOL), prev),
                  pl.BlockSpec((hb, 3 * D_POOL), nxt),
                  pl.BlockSpec((ts, D_POOL), lambda i: (i, 0)),
                  pl.BlockSpec((ts, D_CONV), lambda i: (i, 0)),
                  pl.BlockSpec((hb, D_CONV), nxt),
                  pl.BlockSpec((ts, D_MODEL), lambda i: (i, 0)),
                  pl.BlockSpec((hb, D_MODEL), nxt),
                  pl.BlockSpec((4, POOL_GROUP, POOL_GROUP), lambda i: (0, 0, 0)),
                  pl.BlockSpec((1, D_POOL), fix2), pl.BlockSpec((K, D_CONV), fix2),
                  pl.BlockSpec((1, D_CONV), fix2), pl.BlockSpec((1, D_CONV), fix2)],
        out_specs=[pl.BlockSpec((ts, 3 * D_POOL), lambda i: (i, 0)),
                   pl.BlockSpec((4, POOL_GROUP, POOL_GROUP), lambda i: (0, 0, 0)),
                   pl.BlockSpec((1, D_POOL), fix2), pl.BlockSpec((K, D_CONV), fix2),
                   pl.BlockSpec((1, D_CONV), fix2), pl.BlockSpec((1, D_CONV), fix2), pl.BlockSpec((1, D_CONV), fix2)],
        scratch_shapes=[pltpu.VMEM((hb + ts + hb, D_CONV), F32), pltpu.VMEM((te, D_POOL), F32),
                        pltpu.VMEM((te, D_CONV), F32), pltpu.VMEM((7, hb + te - 8, D_CONV), F32),
                        pltpu.VMEM((7, te - 8, D_CONV), F32)],
        compiler_params=_cparams(("arbitrary",)),
        name="mixer_bwd",
    )(u, u, u, d, hc, hc, dycat, dycat, pool_w, pool_scale.reshape(1, D_POOL), conv_w, cln_g.reshape(1, D_CONV),
      cln_b.reshape(1, D_CONV))


_GELU_C = math.sqrt(2.0 / math.pi)


def _gelu_parts(x):
    inner = _GELU_C * (x + 0.044715 * x * x * x)
    th = jnp.tanh(inner)
    ge = 0.5 * x * (1.0 + th)
    dge = 0.5 * (1.0 + th) + 0.5 * x * (1.0 - th * th) * (_GELU_C * (1.0 + 3.0 * 0.044715 * x * x))
    return ge, dge


def _rows_back(ref, r, n, ln, shifts):
    ext = ref[pl.ds(r - 8, n + 8), ln]
    return [(pltpu.roll(ext, s, 0) if s else ext)[8:] for s in shifts]


def _rows_ahead(ref, r, n, ln, shifts):
    ext = ref[pl.ds(r, n + 8), ln]
    return [(pltpu.roll(ext, n + 8 - s, 0) if s else ext)[:n] for s in shifts]


def _ffn_act_fwd(gate, val, dw_w, dw_b, *, ts=512, tc=1408, name):
    S, F = gate.shape
    hb = FFN_HALO
    nh = ts // hb
    tc = _tile(F, tc)

    def body(g_ref, gh_ref, v_ref, w_ref, b_ref, h_ref, st):
        i = pl.program_id(0)
        st[pl.ds(0, hb), :] = jnp.where(i == 0, 0.0, gh_ref[...].astype(F32))
        st[pl.ds(hb, ts), :] = g_ref[...].astype(F32)
        for c0 in range(0, tc, SUB_LANES):
            ln = pl.ds(c0, SUB_LANES)
            w0, w1, w2, b = w_ref[0:1, ln], w_ref[1:2, ln], w_ref[2:3, ln], b_ref[:, ln]
            for r0 in range(0, ts, SUB_ROWS):
                taps = _rows_back(st, hb + r0, SUB_ROWS, ln, (2, 1, 0))
                gc = b + w0 * taps[0] + w1 * taps[1] + w2 * taps[2]
                ge, _ = _gelu_parts(gc)
                rows = pl.ds(r0, SUB_ROWS)
                h_ref[rows, ln] = (ge * v_ref[rows, ln].astype(F32)).astype(BF16)

    return pl.pallas_call(
        body,
        out_shape=jax.ShapeDtypeStruct((S, F), BF16),
        grid=(S // ts, F // tc),
        in_specs=[pl.BlockSpec((ts, tc), lambda i, j: (i, j)),
                  pl.BlockSpec((hb, tc), lambda i, j: (jnp.maximum(i * nh - 1, 0), j)),
                  pl.BlockSpec((ts, tc), lambda i, j: (i, j)),
                  pl.BlockSpec((3, tc), lambda i, j: (0, j)),
                  pl.BlockSpec((1, tc), lambda i, j: (0, j))],
        out_specs=pl.BlockSpec((ts, tc), lambda i, j: (i, j)),
        scratch_shapes=[pltpu.VMEM((hb + ts, tc), F32)],
        compiler_params=_cparams(("parallel", "parallel")),
        name=name,
    )(gate, gate, val, dw_w, dw_b.reshape(1, F))


def _ffn_act_bwd(gate, val, dh, dw_w, dw_b, *, ts=512, tc=1408, name):
    S, F = gate.shape
    hb = FFN_HALO
    nh = ts // hb
    n = S // ts
    te = ts + hb
    tc = _tile(F, tc)

    def body(g_ref, gp_ref, gn_ref, v_ref, vn_ref, dh_ref, dhn_ref, w_ref, b_ref,
             dg_ref, dv_ref, dw_ref, db_ref, st, sd):
        i = pl.program_id(1)
        first = i == 0
        last = i == n - 1

        @pl.when(first)
        def _():
            dw_ref[...] = jnp.zeros_like(dw_ref)
            db_ref[...] = jnp.zeros_like(db_ref)

        st[pl.ds(0, hb), :] = jnp.where(first, 0.0, gp_ref[...].astype(F32))
        st[pl.ds(hb, ts), :] = g_ref[...].astype(F32)
        st[pl.ds(hb + ts, hb), :] = jnp.where(last, 0.0, gn_ref[...].astype(F32))
        for c0 in range(0, tc, SUB_LANES):
            ln = pl.ds(c0, SUB_LANES)
            w0, w1, w2, b = w_ref[0:1, ln], w_ref[1:2, ln], w_ref[2:3, ln], b_ref[:, ln]
            db_acc = jnp.zeros((8, SUB_LANES), F32)
            dw_acc = [jnp.zeros((8, SUB_LANES), F32) for _ in range(3)]
            for r0 in range(0, te, SUB_ROWS):
                rc = min(SUB_ROWS, te - r0)
                taps = _rows_back(st, hb + r0, rc, ln, (2, 1, 0))
                gc = b + w0 * taps[0] + w1 * taps[1] + w2 * taps[2]
                ge, dge = _gelu_parts(gc)
                if r0 < ts:
                    rows = pl.ds(r0, rc)
                    val, dh = v_ref[rows, ln].astype(F32), dh_ref[rows, ln].astype(F32)
                else:
                    val = jnp.where(last, 0.0, vn_ref[:, ln].astype(F32)[0:rc])
                    dh = jnp.where(last, 0.0, dhn_ref[:, ln].astype(F32)[0:rc])
                dgc = dh * val * dge
                sd[pl.ds(r0, rc), ln] = dgc
                if r0 < ts:
                    dv_ref[rows, ln] = (dh * ge).astype(BF16)
                    db_acc = db_acc + jnp.sum(dgc.reshape(rc // 8, 8, SUB_LANES), axis=0)
                    for k in range(3):
                        dw_acc[k] = dw_acc[k] + jnp.sum((dgc * taps[k]).reshape(rc // 8, 8, SUB_LANES), axis=0)
            db_ref[:, ln] += jnp.sum(db_acc, axis=0, keepdims=True)
            for k in range(3):
                dw_ref[k:k + 1, ln] += jnp.sum(dw_acc[k], axis=0, keepdims=True)
            for r0 in range(0, ts, SUB_ROWS):
                ahead = _rows_ahead(sd, r0, SUB_ROWS, ln, (2, 1, 0))
                dg_ref[pl.ds(r0, SUB_ROWS), ln] = (w0 * ahead[0] + w1 * ahead[1] + w2 * ahead[2]).astype(BF16)

    cur = lambda j, i: (i, j)
    prev = lambda j, i: (jnp.maximum(i * nh - 1, 0), j)
    nxt = lambda j, i: (jnp.minimum((i + 1) * nh, S // hb - 1), j)
    return pl.pallas_call(
        body,
        out_shape=[jax.ShapeDtypeStruct((S, F), BF16), jax.ShapeDtypeStruct((S, F), BF16),
                   jax.ShapeDtypeStruct((3, F), F32), jax.ShapeDtypeStruct((1, F), F32)],
        grid=(F // tc, n),
        in_specs=[pl.BlockSpec((ts, tc), cur), pl.BlockSpec((hb, tc), prev), pl.BlockSpec((hb, tc), nxt),
                  pl.BlockSpec((ts, tc), cur), pl.BlockSpec((hb, tc), nxt),
                  pl.BlockSpec((ts, tc), cur), pl.BlockSpec((hb, tc), nxt),
                  pl.BlockSpec((3, tc), lambda j, i: (0, j)), pl.BlockSpec((1, tc), lambda j, i: (0, j))],
        out_specs=[pl.BlockSpec((ts, tc), cur), pl.BlockSpec((ts, tc), cur),
                   pl.BlockSpec((3, tc), lambda j, i: (0, j)), pl.BlockSpec((1, tc), lambda j, i: (0, j))],
        scratch_shapes=[pltpu.VMEM((hb + ts + hb, tc), F32), pltpu.VMEM((te, tc), F32)],
        compiler_params=_cparams(("parallel", "arbitrary")),
        name=name,
    )(gate, gate, gate, val, val, dh, dh, dw_w, dw_b.reshape(1, F))


def _loss_ln_bwd(z, ln_g, ln_b, target, *, ts=512, name):
    S, D = z.shape

    def body(z_ref, g_ref, b_ref, t_ref, dz_ref, dzb_ref, dg_ref, db_ref, loss_ref):
        i = pl.program_id(0)

        @pl.when(i == 0)
        def _():
            dg_ref[...] = jnp.zeros_like(dg_ref)
            db_ref[...] = jnp.zeros_like(db_ref)
            loss_ref[...] = jnp.zeros_like(loss_ref)

        dg_acc = jnp.zeros((8, D), F32)
        db_acc = jnp.zeros((8, D), F32)
        loss_acc = jnp.zeros((1, 1), F32)
        for r0 in range(0, ts, LN_ROWS):
            rows = pl.ds(r0, LN_ROWS)
            zt = z_ref[rows, :]
            err = _layer_norm_rows(zt, g_ref[...], b_ref[...]) - t_ref[rows, :]
            loss_acc = loss_acc + 0.5 * jnp.sum(jnp.mean(err * err, axis=-1, keepdims=True), keepdims=True)
            do = err * (1.0 / D)
            dz, xh = _ln_bwd_rows(zt, g_ref[...], do)
            dg_acc = dg_acc + jnp.sum((do * xh).reshape(LN_ROWS // 8, 8, D), axis=0)
            db_acc = db_acc + jnp.sum(do.reshape(LN_ROWS // 8, 8, D), axis=0)
            dz_ref[rows, :] = dz
            dzb_ref[rows, :] = dz.astype(BF16)
        dg_ref[...] += jnp.sum(dg_acc, axis=0, keepdims=True)
        db_ref[...] += jnp.sum(db_acc, axis=0, keepdims=True)
        loss_ref[...] += loss_acc

    row = lambda i: (i, 0)
    fix = lambda i: (0, 0)
    return pl.pallas_call(
        body,
        out_shape=[jax.ShapeDtypeStruct((S, D), F32), jax.ShapeDtypeStruct((S, D), BF16),
                   jax.ShapeDtypeStruct((1, D), F32), jax.ShapeDtypeStruct((1, D), F32),
                   jax.ShapeDtypeStruct((8, 128), F32)],
        grid=(S // ts,),
        in_specs=[pl.BlockSpec((ts, D), row), pl.BlockSpec((1, D), fix), pl.BlockSpec((1, D), fix),
                  pl.BlockSpec((ts, D), row)],
        out_specs=[pl.BlockSpec((ts, D), row), pl.BlockSpec((ts, D), row), pl.BlockSpec((1, D), fix),
                   pl.BlockSpec((1, D), fix), pl.BlockSpec((8, 128), fix)],
        compiler_params=_cparams(("arbitrary",)),
        name=name,
    )(z, ln_g.reshape(1, D), ln_b.reshape(1, D), target)


def _ple_bwd(dz, gate, proj, *, ts=512, name):
    S, D = dz.shape

    def body(dz_ref, g_ref, p_ref, ds_ref, dp_ref, db_ref):
        @pl.when(pl.program_id(0) == 0)
        def _():
            db_ref[...] = jnp.zeros_like(db_ref)

        db_acc = jnp.zeros((8, D), F32)
        for r0 in range(0, ts, LN_ROWS):
            rows = pl.ds(r0, LN_ROWS)
            dzt = dz_ref[rows, :]
            g = g_ref[rows, :]
            ds = dzt * p_ref[rows, :] * g * (1.0 - g)
            ds_ref[rows, :] = ds.astype(BF16)
            dp_ref[rows, :] = (dzt * g).astype(BF16)
            db_acc = db_acc + jnp.sum(ds.reshape(LN_ROWS // 8, 8, D), axis=0)
        db_ref[...] += jnp.sum(db_acc, axis=0, keepdims=True)

    row = lambda i: (i, 0)
    return pl.pallas_call(
        body,
        out_shape=[jax.ShapeDtypeStruct((S, D), BF16), jax.ShapeDtypeStruct((S, D), BF16),
                   jax.ShapeDtypeStruct((1, D), F32)],
        grid=(S // ts,),
        in_specs=[pl.BlockSpec((ts, D), row)] * 3,
        out_specs=[pl.BlockSpec((ts, D), row), pl.BlockSpec((ts, D), row), pl.BlockSpec((1, D), lambda i: (0, 0))],
        compiler_params=_cparams(("arbitrary",)),
        name=name,
    )(dz, gate, proj)


HEADS_PER_STEP = 4
HEAD_LANES = HEADS_PER_STEP * HEAD_DIM


ATT_ROWS = 32
ATT_SCALE = HEAD_DIM ** -0.5


def _softmax_piece(scores, bias, qb):
    s = scores + bias
    kpos = qb * Q_BLOCK + lax.broadcasted_iota(jnp.int32, (1, KV_SPAN), 1)
    s = jnp.where(kpos >= KV_PAD, s, NEG_INF)
    e = jnp.exp(s - jnp.max(s, axis=-1, keepdims=True))
    return e * (1.0 / jnp.sum(e, axis=-1, keepdims=True))


def _head_masks():
    lane = lax.broadcasted_iota(jnp.int32, (1, HEAD_LANES), 1)
    return [(lane >= j * HEAD_DIM) & (lane < (j + 1) * HEAD_DIM) for j in range(HEADS_PER_STEP)]


def _pick_heads(masks, per_head):
    out = per_head[0]
    for mask, x in zip(masks[1:], per_head[1:]):
        out = jnp.where(mask, x, out)
    return out


def _pad_keys(qb, k_ref, v_ref, kp, vp):
    @pl.when(qb == 0)
    def _():
        kp[pl.ds(0, KV_PAD), :] = jnp.zeros((KV_PAD, HEAD_LANES), BF16)
        vp[pl.ds(0, KV_PAD), :] = jnp.zeros((KV_PAD, HEAD_LANES), BF16)
        kp[pl.ds(KV_PAD, k_ref.shape[0]), :] = k_ref[...]
        vp[pl.ds(KV_PAD, v_ref.shape[0]), :] = v_ref[...]


def _attn_fwd(qkv, bias):
    S = qkv.shape[0]
    nhp = N_HEADS // HEADS_PER_STEP

    def body(q_ref, k_ref, v_ref, b_ref, o_ref, kp, vp, p_scr):
        qb = pl.program_id(1)
        _pad_keys(qb, k_ref, v_ref, kp, vp)
        span = pl.ds(pl.multiple_of(qb * Q_BLOCK, Q_BLOCK), KV_SPAN)
        kc, vc = kp[span, :], vp[span, :]
        qt = q_ref[...] * ATT_SCALE
        mine = _head_masks()
        scores = [_bdot(jnp.where(mine[j], qt, jnp.zeros_like(qt)), kc, NT) for j in range(HEADS_PER_STEP)]
        outs = []
        for j in range(HEADS_PER_STEP):
            for r0 in range(0, Q_BLOCK, ATT_ROWS):
                rows = pl.ds(r0, ATT_ROWS)
                p_scr[j, rows, :] = _softmax_piece(scores[j][r0:r0 + ATT_ROWS], b_ref[j, rows, :], qb).astype(BF16)
            outs.append(_bdot(p_scr[j], vc))
        o_ref[...] = _pick_heads(mine, outs).astype(BF16)

    return pl.pallas_call(
        body,
        out_shape=jax.ShapeDtypeStruct((S, D_MODEL), BF16),
        grid=(nhp, S // Q_BLOCK),
        in_specs=[pl.BlockSpec((Q_BLOCK, HEAD_LANES), lambda h, i: (i, h)),
                  pl.BlockSpec((S, HEAD_LANES), lambda h, i: (0, nhp + h)),
                  pl.BlockSpec((S, HEAD_LANES), lambda h, i: (0, 2 * nhp + h)),
                  pl.BlockSpec((HEADS_PER_STEP, Q_BLOCK, KV_SPAN), lambda h, i: (h, 0, 0))],
        out_specs=pl.BlockSpec((Q_BLOCK, HEAD_LANES), lambda h, i: (i, h)),
        scratch_shapes=[pltpu.VMEM((KV_PAD + S, HEAD_LANES), BF16), pltpu.VMEM((KV_PAD + S, HEAD_LANES), BF16),
                        pltpu.VMEM((HEADS_PER_STEP, Q_BLOCK, KV_SPAN), BF16)],
        compiler_params=_cparams(("parallel", "arbitrary")),
        name="attn_fwd",
    )(qkv, qkv, qkv, bias)


def _attn_bwd(qkv, bias, do):
    S = qkv.shape[0]
    nhp = N_HEADS // HEADS_PER_STEP
    nq = S // Q_BLOCK
    scale = HEAD_DIM ** -0.5

    def body(q_ref, k_ref, v_ref, b_ref, do_ref, dq_ref, dk_ref, dv_ref, db_ref, kp, vp, dka, dva,
             p_scr, ds_scr):
        qb = pl.program_id(1)
        _pad_keys(qb, k_ref, v_ref, kp, vp)

        @pl.when(qb == 0)
        def _():
            dka[...] = jnp.zeros_like(dka)
            dva[...] = jnp.zeros_like(dva)
            db_ref[...] = jnp.zeros_like(db_ref)

        span = pl.ds(pl.multiple_of(qb * Q_BLOCK, Q_BLOCK), KV_SPAN)
        kc, vc = kp[span, :], vp[span, :]
        qt, dot = q_ref[...] * ATT_SCALE, do_ref[...]
        mine = _head_masks()
        dqs = []
        qs = [jnp.where(mine[j], qt, jnp.zeros_like(qt)) for j in range(HEADS_PER_STEP)]
        dos = [jnp.where(mine[j], dot, jnp.zeros_like(dot)) for j in range(HEADS_PER_STEP)]
        scores = [_bdot(qs[j], kc, NT) for j in range(HEADS_PER_STEP)]
        dps = [_bdot(dos[j], vc, NT) for j in range(HEADS_PER_STEP)]
        for j in range(HEADS_PER_STEP):
            qj, doj = qs[j], dos[j]
            for r0 in range(0, Q_BLOCK, ATT_ROWS):
                rows = pl.ds(r0, ATT_ROWS)
                p = _softmax_piece(scores[j][r0:r0 + ATT_ROWS], b_ref[j, rows, :], qb)
                dp = dps[j][r0:r0 + ATT_ROWS]
                ds = p * (dp - jnp.sum(p * dp, axis=-1, keepdims=True))
                db_ref[j, rows, :] += ds
                p_scr[j, rows, :] = p.astype(BF16)
                ds_scr[j, rows, :] = ds.astype(BF16)
            dva[span, :] += _bdot(p_scr[j], doj, TN)
            dqs.append(_bdot(ds_scr[j], kc))
            dka[span, :] += _bdot(ds_scr[j], qj, TN)
        dq_ref[...] = (scale * _pick_heads(mine, dqs)).astype(BF16)

        @pl.when(qb == nq - 1)
        def _():
            dk_ref[...] = dka[pl.ds(KV_PAD, S), :].astype(BF16)
            dv_ref[...] = dva[pl.ds(KV_PAD, S), :].astype(BF16)

    blk = pl.BlockSpec((Q_BLOCK, HEAD_LANES), lambda h, i: (i, h))
    col = pl.BlockSpec((S, HEAD_LANES), lambda h, i: (0, h))
    bsp = pl.BlockSpec((HEADS_PER_STEP, Q_BLOCK, KV_SPAN), lambda h, i: (h, 0, 0))
    return pl.pallas_call(
        body,
        out_shape=[jax.ShapeDtypeStruct((S, D_MODEL), BF16)] * 3
        + [jax.ShapeDtypeStruct((N_HEADS, Q_BLOCK, KV_SPAN), F32)],
        grid=(nhp, nq),
        in_specs=[blk, pl.BlockSpec((S, HEAD_LANES), lambda h, i: (0, nhp + h)),
                  pl.BlockSpec((S, HEAD_LANES), lambda h, i: (0, 2 * nhp + h)), bsp, blk],
        out_specs=[blk, col, col, bsp],
        scratch_shapes=[pltpu.VMEM((KV_PAD + S, HEAD_LANES), BF16), pltpu.VMEM((KV_PAD + S, HEAD_LANES), BF16),
                        pltpu.VMEM((KV_PAD + S, HEAD_LANES), F32), pltpu.VMEM((KV_PAD + S, HEAD_LANES), F32),
                        pltpu.VMEM((HEADS_PER_STEP, Q_BLOCK, KV_SPAN), BF16), pltpu.VMEM((HEADS_PER_STEP, Q_BLOCK, KV_SPAN), BF16)],
        compiler_params=_cparams(("parallel", "arbitrary")),
        name="attn_bwd",
    )(qkv, qkv, qkv, bias, do)


N_DIST = BAND + CHUNK - 1
N_FAR = KV_PAD + CHUNK - MAX_REL


def _shear_rows(x, towards_right):
    row = lax.broadcasted_iota(jnp.int32, (Q_BLOCK, 1), 0)
    for bit in range(Q_BLOCK.bit_length() - 1):
        step = 1 << bit
        x = jnp.where((row & step) != 0, pltpu.roll(x, step if towards_right else KV_SPAN - step, 1), x)
    return x


def _bias_blocks(rel_bias):
    H = rel_bias.shape[0]
    e = jnp.concatenate([jnp.broadcast_to(rel_bias[:, 2 * MAX_REL:], (H, N_FAR)),
                         jnp.flip(rel_bias[:, 2 * MAX_REL - (N_DIST - N_FAR):2 * MAX_REL], axis=1),
                         jnp.zeros((H, KV_SPAN - N_DIST), F32)], axis=1).reshape(H, 1, KV_SPAN)

    def body(e_ref, o_ref):
        first = pltpu.roll(jnp.broadcast_to(e_ref[...], (Q_BLOCK, KV_SPAN)), KV_SPAN - (CHUNK - 1), 1)
        x = _shear_rows(first, True)
        row = lax.broadcasted_iota(jnp.int32, (Q_BLOCK, 1), 0)
        chunk0 = row - (row & (CHUNK - 1))
        k = lax.broadcasted_iota(jnp.int32, (1, KV_SPAN), 1)
        o_ref[...] = jnp.where((k >= chunk0) & (k < chunk0 + BAND), x, NEG_INF)

    return pl.pallas_call(
        body,
        out_shape=jax.ShapeDtypeStruct((H, Q_BLOCK, KV_SPAN), F32),
        grid=(H,),
        in_specs=[pl.BlockSpec((None, 1, KV_SPAN), lambda h: (h, 0, 0))],
        out_specs=pl.BlockSpec((None, Q_BLOCK, KV_SPAN), lambda h: (h, 0, 0)),
        compiler_params=_cparams(("parallel",)),
        name="bias_blocks",
    )(e)


def _bias_blocks_grad(dblk):
    H = dblk.shape[0]

    def body(d_ref, o_ref):
        x = pltpu.roll(_shear_rows(d_ref[...], False), CHUNK - 1, 1)
        de = jnp.sum(x, axis=0, keepdims=True)
        lane = lax.broadcasted_iota(jnp.int32, de.shape, 1)
        far = jnp.sum(jnp.where(lane < N_FAR, de, 0.0), axis=-1, keepdims=True)
        o_ref[...] = jnp.where(lane == 0, far, jnp.where(lane < N_FAR, 0.0, de))

    de = pl.pallas_call(
        body,
        out_shape=jax.ShapeDtypeStruct((H, 1, KV_SPAN), F32),
        grid=(H,),
        in_specs=[pl.BlockSpec((None, Q_BLOCK, KV_SPAN), lambda h: (h, 0, 0))],
        out_specs=pl.BlockSpec((None, 1, KV_SPAN), lambda h: (h, 0, 0)),
        compiler_params=_cparams(("parallel",)),
        name="bias_grad_sum",
    )(dblk).reshape(H, KV_SPAN)
    near = jnp.flip(de[:, N_FAR:N_DIST], axis=1)
    return jnp.concatenate([jnp.zeros((H, 2 * MAX_REL - (N_DIST - N_FAR)), F32), near, de[:, 0:1]], axis=1)


def _ffn_forward(r1, r1b, p_l, w, l, ready):
    ready(f"up{l}", r1b)
    up_g = _mm_rows([(r1b, w["ffn_up_t"][l], True, (0, 2))], out_dtype=BF16, name=f"ffn_up_g{l}")
    up_v = _mm_rows([(r1b, w["ffn_up_t"][l], True, (1, 2))], out_dtype=BF16, name=f"ffn_up_v{l}")
    h = _ffn_act_fwd(up_g, up_v, w["ffn_dw_w"][l], w["ffn_dw_b"][l], name=f"ffn_act{l}")
    ready(f"dn{l}", h)
    z2, r2, r2b, gate, proj = _proj_ln(r1, h, w["ffn_w_down"][l], w["ln_ffn_g"][l], w["ln_ffn_b"][l],
                                       ple=(w["ple_w_gate"][l], w["ple_b_gate"][l], p_l, w["ple_w_proj"][l]),
                                       name=f"ffn_down_ln{l}")
    return dict(r1b=r1b, up_g=up_g, up_v=up_v, h=h, z2=z2, gate=gate, proj=proj), r2, r2b


def _ffn_backward(sv, dz2, dz2b, p_l, w, l, grads, ln_bwd, emit):
    r1b = sv["r1b"]
    ds, dproj, db_gate = _ple_bwd(dz2, sv["gate"], sv["proj"], name=f"ple_bwd{l}")
    dh = _mm_rows([(dz2b, w["ffn_w_down"][l], True, WHOLE)], out_dtype=BF16, name=f"ffn_dh{l}")
    dgate, dval, d_dw_w, d_dw_b = _ffn_act_bwd(sv["up_g"], sv["up_v"], dh, w["ffn_dw_w"][l], w["ffn_dw_b"][l],
                                               name=f"ffn_act_bwd{l}")
    grads["ffn_w_down"][l] = _wgrad(sv["h"], dz2b, tm=1408, name=f"d_ffn_w_down{l}")
    d_up_g = _wgrad(dgate, r1b, tm=1408, part=(0, 2), name=f"d_ffn_up_g{l}")
    grads["ffn_up_t"][l] = _wgrad(dval, r1b, tm=1408, part=(1, 2), into=d_up_g, name=f"d_ffn_up_v{l}")
    grads["ple_w_gate"][l] = _wgrad(r1b, ds, name=f"d_ple_w_gate{l}")
    grads["ple_w_proj"][l] = _wgrad(p_l, dproj, piece=D_MODEL // N_DEV, name=f"d_ple_w_proj{l}")
    grads["ffn_dw_w"][l] = d_dw_w
    grads["ffn_dw_b"][l] = d_dw_b[0]
    grads["ple_b_gate"][l] = db_gate[0]
    return _mm_rows([(ds, w["ple_w_gate"][l], True, WHOLE), (dgate, w["ffn_up_t"][l], False, (0, 2)),
                     (dval, w["ffn_up_t"][l], False, (1, 2))], add=dz2, add_scale=ALPHA, ln_bwd=ln_bwd, dep=emit(),
                    name=f"dr1_{l}")


def _local_step(x, p, target, w, ready=lambda group, after: None, emit=lambda group, grads: None):
    grads = {k: [None, None] for k in ("ffn_w_down", "ffn_up_t", "ple_w_gate", "ple_w_proj", "ffn_dw_w",
                                       "ffn_dw_b", "ple_b_gate", "ln_ffn_g", "ln_ffn_b", "ln_mix_g", "ln_mix_b")}

    xb, pb = x.astype(BF16), p.astype(BF16)
    ready("mix", None)
    u = _mm_rows([(xb, w["mix_w_in_t"], True, WHOLE)], name="mix_in")
    ycat, dpool, hconv = _mixer_fwd(u, w["pool_w"], w["pool_scale"], w["conv_dw_w"], w["conv_dw_b"], w["conv_ln_g"],
                                    w["conv_ln_b"])
    ready("mixo", ycat)
    z1, r1, r1b = _proj_ln(x, ycat, w["mix_w_out"], w["ln_mix_g"][0], w["ln_mix_b"][0], name="mix_out_ln")
    sv0, r2, r2b = _ffn_forward(r1, r1b, pb[0], w, 0, ready)

    ready("attn", r2b)
    qkv = _mm_rows([(r2b, w["attn_w_qkv"], False, WHOLE)], out_dtype=BF16, name="attn_qkv")
    bias = _bias_blocks(w["attn_rel_bias"])
    attn = _attn_fwd(qkv, bias)
    z3, r3, r3b = _proj_ln(r2, attn, w["attn_w_o"], w["ln_mix_g"][1], w["ln_mix_b"][1], name="attn_out_ln")
    sv1, _, _ = _ffn_forward(r3, r3b, pb[1], w, 1, ready)

    dz4, dz4b, grads["ln_ffn_g"][1], grads["ln_ffn_b"][1], loss = _loss_ln_bwd(
        sv1["z2"], w["ln_ffn_g"][1], w["ln_ffn_b"][1], target, name="loss_ln_bwd")
    dz3, dz3b, grads["ln_mix_g"][1], grads["ln_mix_b"][1] = _ffn_backward(
        sv1, dz4, dz4b, pb[1], w, 1, grads, (z3, w["ln_mix_g"][1]), lambda: emit("ffn1", grads))
    grads["attn_w_o"] = _wgrad(attn, dz3b, name="d_attn_w_o")
    dattn = _mm_rows([(dz3b, w["attn_w_o"], True, WHOLE)], out_dtype=BF16, name="d_attn")
    dq, dk, dv, dbias = _attn_bwd(qkv, bias, dattn)
    grads["attn_rel_bias"] = _bias_blocks_grad(dbias)
    dqkv = jnp.concatenate([dq, dk, dv], axis=1)
    grads["attn_w_qkv"] = _wgrad(r2b, dqkv, tn=768, piece=3 * D_MODEL // N_DEV, name="d_attn_w_qkv")
    dz2, dz2b, grads["ln_ffn_g"][0], grads["ln_ffn_b"][0] = _mm_rows(
        [(dqkv, w["attn_w_qkv"], True, WHOLE)], add=dz3, add_scale=ALPHA, ln_bwd=(sv0["z2"], w["ln_ffn_g"][0]),
        dep=emit("attn", grads), name="dr2")
    dz1, dz1b, grads["ln_mix_g"][0], grads["ln_mix_b"][0] = _ffn_backward(
        sv0, dz2, dz2b, pb[0], w, 0, grads, (z1, w["ln_mix_g"][0]), lambda: emit("ffn0", grads))
    grads["mix_w_out"] = _wgrad(ycat, dz1b, name="d_mix_w_out")
    dycat = _mm_rows([(dz1b, w["mix_w_out"], True, WHOLE)], name="d_ycat")
    du, g_pw, g_ps, g_cw, g_cb, g_cg, g_cbb = _mixer_bwd(u, dpool, hconv, dycat, w["pool_w"], w["pool_scale"],
                                                         w["conv_dw_w"], w["conv_ln_g"], w["conv_ln_b"])
    grads["mix_w_in_t"] = _wgrad(du, xb, name="d_mix_w_in")
    grads.update(pool_w=g_pw, pool_scale=g_ps[0], conv_dw_w=g_cw, conv_dw_b=g_cb[0], conv_ln_g=g_cg[0],
                 conv_ln_b=g_cbb[0])
    for kname in ("ln_ffn_g", "ln_ffn_b", "ln_mix_g", "ln_mix_b"):
        grads[kname] = [a[0] for a in grads[kname]]
    grad_x = _mm_rows([(du, w["mix_w_in_t"], False, WHOLE)], add=dz1, add_scale=ALPHA, dep=emit("mix", grads),
                      name="grad_x")
    return loss[0, 0], grad_x, grads


_HBM = pl.BlockSpec(memory_space=pltpu.HBM)
_SEM = pl.BlockSpec(memory_space=pltpu.SEMAPHORE)
_EFFECT = pltpu.SideEffectType.DATAFLOW_SIDE_EFFECTING


def _slot(ref, place, shape, k):
    if place in ("stack", "pieces"):
        return ref.at[k]
    ax = place[1]
    n = shape[ax]
    return ref.at[(slice(None),) * ax + (pl.ds(pl.multiple_of(k * n, n), n),)]


def _result_shape(buf, place):
    if place == "stack":
        return (N_DEV,) + buf.shape
    if place == "pieces":
        return buf.shape
    return tuple(s * N_DEV if i == place[1] else s for i, s in enumerate(buf.shape))


def _peers(x, y, c):
    for d in range(1, N_DEV):
        px, py, pc = x ^ ((d >> 2) & 1), y ^ ((d >> 1) & 1), c ^ (d & 1)
        yield d, (px, py, pc), 4 * px + 2 * py + pc


def _exchange_start(bufs, places, after, *, name):
    nb = len(bufs)
    lands = [lax.empty(_result_shape(b, p_), b.dtype) for b, p_ in zip(bufs, places)]
    has_after = after is not None

    def body(*refs):
        srcs, dsts = refs[:nb], refs[nb:2 * nb]
        outs = refs[2 * nb + has_after:]
        send_sems, recv_sems, token = outs[0], outs[1], outs[2 + 2 * nb]
        x, y, c = lax.axis_index("x"), lax.axis_index("y"), lax.axis_index("c")
        me = 4 * x + 2 * y + c
        for b in range(nb):
            for d, dev, peer in _peers(x, y, c):
                pltpu.make_async_remote_copy(
                    src_ref=srcs[b].at[peer] if places[b] == "pieces" else srcs[b],
                    dst_ref=_slot(dsts[b], places[b], bufs[b].shape, me),
                    send_sem=send_sems.at[b * N_DEV + d], recv_sem=recv_sems.at[b * N_DEV + d],
                    device_id=dev, device_id_type=pl.DeviceIdType.MESH).start()
            pltpu.make_async_copy(srcs[b].at[me] if places[b] == "pieces" else srcs[b],
                                  _slot(dsts[b], places[b], bufs[b].shape, me), recv_sems.at[b * N_DEV]).start()
        token[...] = jnp.zeros_like(token)

    sems = pltpu.SemaphoreType.DMA((nb * N_DEV,))
    ins = [pltpu.with_memory_space_constraint(a, pltpu.HBM) for a in list(bufs) + lands]
    out = pl.pallas_call(
        body,
        out_shape=(sems, sems, *[pltpu.HBM(a.shape, a.dtype) for a in ins], jax.ShapeDtypeStruct((8, 128), F32)),
        in_specs=[_HBM] * (2 * nb) + ([pl.BlockSpec(memory_space=pl.ANY)] if has_after else []),
        out_specs=(_SEM, _SEM, *[_HBM] * (2 * nb), pl.BlockSpec(memory_space=pltpu.VMEM)),
        input_output_aliases={i: 2 + i for i in range(2 * nb)},
        compiler_params=pltpu.CompilerParams(has_side_effects=_EFFECT),
        name=name,
    )(*ins, *([after] if has_after else []))
    return dict(send=out[0], recv=out[1], srcs=out[2:2 + nb], lands=out[2 + nb:2 + 2 * nb], token=out[-1],
                places=places)


def _exchange_wait(h, after, *, name):
    nb = len(h["srcs"])
    places = h["places"]
    shapes = [a.shape for a in h["srcs"]]

    def body(*refs):
        srcs, dsts, send_sems, recv_sems = refs[:nb], refs[nb:2 * nb], refs[2 * nb], refs[2 * nb + 1]
        x, y, c = lax.axis_index("x"), lax.axis_index("y"), lax.axis_index("c")
        me = 4 * x + 2 * y + c
        for b in range(nb):
            pieces = places[b] == "pieces"
            for d, dev, peer in _peers(x, y, c):
                cp = pltpu.make_async_remote_copy(
                    src_ref=srcs[b].at[peer] if pieces else srcs[b],
                    dst_ref=_slot(dsts[b], places[b], shapes[b], peer),
                    send_sem=send_sems.at[b * N_DEV + d], recv_sem=recv_sems.at[b * N_DEV + d],
                    device_id=dev, device_id_type=pl.DeviceIdType.MESH)
                cp.wait_send()
                cp.wait_recv()
            pltpu.make_async_copy(srcs[b].at[me] if pieces else srcs[b], _slot(dsts[b], places[b], shapes[b], me),
                                  recv_sems.at[b * N_DEV]).wait()

    ins = list(h["srcs"]) + list(h["lands"])
    out = pl.pallas_call(
        body,
        out_shape=tuple(pltpu.HBM(a.shape, a.dtype) for a in ins),
        in_specs=[_HBM] * (2 * nb) + [_SEM, _SEM, pl.BlockSpec(memory_space=pl.ANY)],
        out_specs=tuple([_HBM] * (2 * nb)),
        input_output_aliases={i: i for i in range(2 * nb)},
        compiler_params=pltpu.CompilerParams(has_side_effects=_EFFECT),
        name=name,
    )(*ins, h["send"], h["recv"], after)
    return out[nb:]


def _adamw(recv, w, m, v, *, layer=0, into=None, name):
    L, R, C = w.shape
    tr = R
    for cand in (512, 256, 128, 64, 32, 16):
        if R % cand == 0 and cand * C * 4 <= 2 * 1024 * 1024:
            tr = cand
            break
    c1 = 1.0 - ADAM_B1 ** ADAM_STEP
    c2 = 1.0 - ADAM_B2 ** ADAM_STEP

    def body(r_ref, w_ref, m_ref, v_ref, *rest):
        g_ref, d_ref, mo_ref, vo_ref = rest[-4:]
        g = r_ref[0].astype(F32)
        for i in range(1, N_DEV):
            g = g + r_ref[i].astype(F32)
        m_new = ADAM_B1 * m_ref[...] + (1.0 - ADAM_B1) * g
        v_new = ADAM_B2 * v_ref[...] + (1.0 - ADAM_B2) * (g * g)
        m_hat = m_new / c1
        v_hat = v_new / c2
        g_ref[...] = g
        d_ref[...] = -ADAM_LR * (m_hat / (jnp.sqrt(v_hat) + ADAM_EPS) + ADAM_WD * w_ref[...])
        mo_ref[...] = m_new
        vo_ref[...] = v_new

    row = pl.BlockSpec((None, tr, C), lambda i: (layer, i, 0))
    others = [] if into is None else list(into)
    return pl.pallas_call(
        body,
        out_shape=[jax.ShapeDtypeStruct((L, R, C), F32)] * 4,
        grid=(R // tr,),
        in_specs=[pl.BlockSpec((N_DEV, tr, C), lambda i: (0, i, 0)), row, row, row]
        + [pl.BlockSpec(memory_space=pl.ANY)] * len(others),
        out_specs=[row] * 4,
        input_output_aliases={4 + k: k for k in range(len(others))},
        compiler_params=_cparams(("parallel",)),
        name=name,
    )(recv, w, m, v, *others)


_TRANSPOSED = ("mix_w_in", "ffn_w_up")


def _ffn_groups(l):
    return ((f"up{l}", (("ffn_w_up", l, BF16, ("axis", 0)), ("ffn_dw_w", l, F32, "stack"))),
            (f"dn{l}", (("ffn_w_down", l, BF16, ("axis", 0)), ("ple_w_gate", l, BF16, ("axis", 0)),
                        ("ple_w_proj", l, BF16, ("axis", 1)))))


_GATHER_GROUPS = (
    ("mix", (("mix_w_in", 0, BF16, ("axis", 0)), ("conv_dw_w", 0, F32, "stack"))),
    ("mixo", (("mix_w_out", 0, BF16, ("axis", 0)),)),
    *_ffn_groups(0),
    ("attn", (("attn_w_qkv", 0, BF16, ("axis", 1)), ("attn_w_o", 0, BF16, ("axis", 0)))),
    *_ffn_groups(1))
_SHARDED = ("mix_w_in", "conv_dw_w", "mix_w_out", "attn_w_qkv", "attn_w_o", "ffn_w_up", "ffn_dw_w", "ffn_w_down",
            "ple_w_gate", "ple_w_proj")
_REPLICATED = ("pool_w", "pool_scale", "conv_dw_b", "conv_ln_g", "conv_ln_b", "attn_rel_bias", "ln_mix_g",
               "ln_mix_b", "ffn_dw_b", "ple_b_gate", "ln_ffn_g", "ln_ffn_b")


def _pack_rows(parts, row_mult, dtype):
    lead = parts[0].shape[:-1]
    flat = jnp.concatenate([a.astype(dtype) for a in parts], axis=-1)
    n = flat.shape[-1]
    unit = row_mult * LANES
    padded = -(-n // unit) * unit
    flat = jnp.pad(flat, [(0, 0)] * len(lead) + [(0, padded - n)])
    return flat.reshape(lead + (padded // LANES, LANES))


def _unpack(flat2d, shapes):
    flat = flat2d.reshape(-1)
    out, o = [], 0
    for s in shapes:
        n = math.prod(s)
        out.append(flat[o:o + n].reshape(s))
        o += n
    return out


def _full_from_shards(g, axis):
    parts = jnp.moveaxis(g, 0, axis)
    shp = list(g.shape[1:])
    shp[axis] *= g.shape[0]
    return parts.reshape(shp)


def _pieces_from_full(full, axis, k=N_DEV):
    shp = list(full.shape)
    n = shp[axis] // k
    t = full.reshape(shp[:axis] + [k, n] + shp[axis + 1:])
    return jnp.moveaxis(t, axis, 0)


def kernel(x, p, mix_w_in, pool_w, pool_scale, conv_dw_w, conv_dw_b, conv_ln_g, conv_ln_b, mix_w_out, attn_w_qkv, attn_rel_bias, attn_w_o, ln_mix_g, ln_mix_b, ffn_w_up, ffn_dw_w, ffn_dw_b, ffn_w_down, ple_w_proj, ple_w_gate, ple_b_gate, ln_ffn_g, ln_ffn_b, loss_target, m_mix_w_in, m_pool_w, m_pool_scale, m_conv_dw_w, m_conv_dw_b, m_conv_ln_g, m_conv_ln_b, m_mix_w_out, m_attn_w_qkv, m_attn_rel_bias, m_attn_w_o, m_ln_mix_g, m_ln_mix_b, m_ffn_w_up, m_ffn_dw_w, m_ffn_dw_b, m_ffn_w_down, m_ple_w_proj, m_ple_w_gate, m_ple_b_gate, m_ln_ffn_g, m_ln_ffn_b, v_mix_w_in, v_pool_w, v_pool_scale, v_conv_dw_w, v_conv_dw_b, v_conv_ln_g, v_conv_ln_b, v_mix_w_out, v_attn_w_qkv, v_attn_rel_bias, v_attn_w_o, v_ln_mix_g, v_ln_mix_b, v_ffn_w_up, v_ffn_dw_w, v_ffn_dw_b, v_ffn_w_down, v_ple_w_proj, v_ple_w_gate, v_ple_b_gate, v_ln_ffn_g, v_ln_ffn_b):
    a = dict(locals())
    sh_names = list(_SHARDED)
    names = sh_names + list(_REPLICATED)
    wts = {n: a[n] for n in names}
    mom = {n: a["m_" + n] for n in names}
    var = {n: a["v_" + n] for n in names}

    for n in _TRANSPOSED:
        wts[n], mom[n], var[n] = (jnp.swapaxes(d[n], 1, 2) for d in (wts, mom, var))
    gather = {}
    token = None
    for group, items in _GATHER_GROUPS:
        gather[group] = _exchange_start([wts[n][l].astype(dt) for n, l, dt, _ in items], [pl_ for *_, pl_ in items],
                                        token, name="gather_start_" + group)
        token = gather[group]["token"]

    w = dict(pool_w=pool_w[0], pool_scale=pool_scale[0], conv_dw_b=conv_dw_b[0], conv_ln_g=conv_ln_g[0],
             conv_ln_b=conv_ln_b[0], attn_rel_bias=attn_rel_bias[0], ln_mix_g=ln_mix_g, ln_mix_b=ln_mix_b,
             ffn_dw_b=ffn_dw_b, ple_b_gate=ple_b_gate, ln_ffn_g=ln_ffn_g, ln_ffn_b=ln_ffn_b)
    for n in ("ffn_up_t", "ffn_dw_w", "ffn_w_down", "ple_w_gate", "ple_w_proj"):
        w[n] = [None, None]

    def ready(group, after):
        got = _exchange_wait(gather[group], token if after is None else after, name="gather_wait_" + group)
        if group == "mix":
            w["mix_w_in_t"], w["conv_dw_w"] = got[0], _full_from_shards(got[1], 1)
        elif group == "mixo":
            (w["mix_w_out"],) = got
        elif group == "attn":
            w["attn_w_qkv"], w["attn_w_o"] = got
        elif group[:2] == "up":
            l = int(group[2])
            w["ffn_up_t"][l], w["ffn_dw_w"][l] = got[0], _full_from_shards(got[1], 1)
        else:
            l = int(group[2])
            w["ffn_w_down"][l], w["ple_w_gate"][l], w["ple_w_proj"][l] = got

    scatter = {}

    def emit(group, gr):
        if group[:3] == "ffn":
            l = int(group[3])
            pieces = [_pieces_from_full(gr["ffn_up_t"][l], 0),
                      _pieces_from_full(gr["ffn_dw_w"][l], 1), _pieces_from_full(gr["ffn_w_down"][l], 0),
                      _pieces_from_full(gr["ple_w_gate"][l], 0), gr["ple_w_proj"][l]]
        elif group == "attn":
            pieces = [gr["attn_w_qkv"], _pieces_from_full(gr["attn_w_o"], 0)]
        else:
            pieces = [_pieces_from_full(gr["mix_w_in_t"], 0), _pieces_from_full(gr["conv_dw_w"], 1),
                      _pieces_from_full(gr["mix_w_out"], 0)]
        scatter[group] = _exchange_start([a.astype(BF16) for a in pieces], ["pieces"] * len(pieces), None,
                                         name="grad_start_" + group)
        if group != "mix":
            return scatter[group]["token"]
        gfull = dict(
            pool_w=gr["pool_w"][None], pool_scale=gr["pool_scale"][None], conv_dw_b=gr["conv_dw_b"][None],
            conv_ln_g=gr["conv_ln_g"][None], conv_ln_b=gr["conv_ln_b"][None],
            attn_rel_bias=gr["attn_rel_bias"][None], ln_mix_g=jnp.stack(gr["ln_mix_g"]),
            ln_mix_b=jnp.stack(gr["ln_mix_b"]), ffn_dw_b=jnp.stack(gr["ffn_dw_b"]),
            ple_b_gate=jnp.stack(gr["ple_b_gate"]), ln_ffn_g=jnp.stack(gr["ln_ffn_g"]),
            ln_ffn_b=jnp.stack(gr["ln_ffn_b"]))
        rep_send = _pack_rows([gfull[n].reshape(-1) for n in _REPLICATED], 8, F32)
        scatter["replicated"] = _exchange_start([rep_send], ["stack"], scatter[group]["token"],
                                                name="grad_start_replicated")
        return scatter["replicated"]["token"]

    loss_part, grad_x, gr = _local_step(x[0], p[:, 0], loss_target[0], w, ready, emit)
    loss = lax.psum(loss_part, ("x", "y", "c"))

    group_weights = {"ffn1": (("ffn_w_up", 1), ("ffn_dw_w", 1), ("ffn_w_down", 1), ("ple_w_gate", 1), ("ple_w_proj", 1)),
                     "attn": (("attn_w_qkv", 0), ("attn_w_o", 0)),
                     "ffn0": (("ffn_w_up", 0), ("ffn_dw_w", 0), ("ffn_w_down", 0), ("ple_w_gate", 0), ("ple_w_proj", 0)),
                     "mix": (("mix_w_in", 0), ("conv_dw_w", 0), ("mix_w_out", 0))}
    updated = {}
    after = grad_x
    for group in ("ffn1", "attn", "ffn0", "mix"):
        recv = _exchange_wait(scatter[group], after, name="grad_wait_" + group)
        for (n, l), r in zip(group_weights[group], recv):
            updated[n] = _adamw(r, wts[n], mom[n], var[n], layer=l, into=updated.get(n), name=f"adamw_{n}{l}")
            after = updated[n][0]
    res = [{n: jnp.swapaxes(updated[n][k], 1, 2) if n in _TRANSPOSED else updated[n][k] for n in sh_names}
           for k in range(4)]
    (rep_recv,) = _exchange_wait(scatter["replicated"], after, name="grad_wait_replicated")

    def flat_state(d):
        return _pack_rows([d[n].reshape(-1) for n in _REPLICATED], 8, F32)[None]

    rep_out = _adamw(rep_recv, flat_state(wts), flat_state(mom), flat_state(var), name="adamw_replicated")
    for k in range(4):
        for n, arr in zip(_REPLICATED, _unpack(rep_out[k][0], [wts[n].shape for n in _REPLICATED])):
            res[k][n] = arr
    order = ["mix_w_in", "pool_w", "pool_scale", "conv_dw_w", "conv_dw_b", "conv_ln_g", "conv_ln_b", "mix_w_out",
             "attn_w_qkv", "attn_rel_bias", "attn_w_o", "ln_mix_g", "ln_mix_b", "ffn_w_up", "ffn_dw_w", "ffn_dw_b",
             "ffn_w_down", "ple_w_proj", "ple_w_gate", "ple_b_gate", "ln_ffn_g", "ln_ffn_b"]
    outs = [loss, grad_x[None]]
    for k in range(4):
        outs += [res[k][n] for n in order]
    return tuple(outs)
```

```python
import functools
import math

import jax
import jax.numpy as jnp
from jax import lax
from jax.experimental import pallas as pl
from jax.experimental.pallas import tpu as pltpu

F32 = jnp.float32
BF16 = jnp.bfloat16

N_DEV = 8
D_MODEL = 1024
D_POOL = 512
D_CONV = 512
POOL_WINDOWS = (2, 4, 8, 16)
POOL_GROUP = 128
CONV_KERNEL = 31
CHUNK = 64
HEAD_DIM = 64
N_HEADS = 16
LEFT_CHUNKS = 8
BAND = (LEFT_CHUNKS + 1) * CHUNK
MAX_REL = 256
D_FF = 2816
PLE_DIM = 256
ALPHA = 4.0 ** 0.25
LN_EPS = 1e-5
NEG_INF = -1e30
ADAM_LR, ADAM_B1, ADAM_B2, ADAM_EPS, ADAM_WD, ADAM_STEP = 0.001, 0.9, 0.999, 1e-08, 0.01, 10

Q_BLOCK = 4 * CHUNK
KV_PAD = LEFT_CHUNKS * CHUNK
KV_SPAN = KV_PAD + Q_BLOCK
CONV_HALO = 32
FFN_HALO = 16
SUB_ROWS, SUB_LANES = 64, 128
LANES = 1024
VMEM_LIMIT = 56 * 1024 * 1024


def _cparams(sem=None):
    return pltpu.CompilerParams(dimension_semantics=sem, vmem_limit_bytes=VMEM_LIMIT)


def _tile(dim, pref):
    if dim <= pref:
        return dim
    t = pref - pref % 128
    while t >= 128:
        if dim % t == 0:
            return t
        t -= 128
    return dim


def _sigmoid(x):
    return 1.0 / (1.0 + jnp.exp(-x))


def _bdot(a, b, dn=(((1,), (0,)), ((), ()))):
    return lax.dot_general(a.astype(BF16), b.astype(BF16), dn, preferred_element_type=F32)


WHOLE = (0, 1)
NT = (((1,), (1,)), ((), ()))
TN = (((0,), (0,)), ((), ()))


def _wgrad(a, b, *, tm=1024, tn=1024, tk=2048, piece=None, part=(0, 1), into=None, name):
    K, M = a.shape
    kb, N = b.shape
    assert K == kb, (a.shape, b.shape)
    tm, tn, tk = _tile(M, tm), _tile(N, tn), _tile(K, tk)
    nk = K // tk
    per = 1 if piece is None else tn // piece
    assert piece is None or tn == per * piece

    def body(a_ref, b_ref, *rest):
        o_ref, acc = rest[-2:]
        k = pl.program_id(2)

        @pl.when(k == 0)
        def _():
            acc[...] = jnp.zeros_like(acc)

        acc[...] += _bdot(a_ref[...], b_ref[...], TN)

        @pl.when(k == nk - 1)
        def _():
            if piece is None:
                o_ref[...] = acc[...].astype(BF16)
            else:
                for s in range(per):
                    o_ref[s] = acc[:, s * piece:(s + 1) * piece].astype(BF16)

    if piece is None:
        first = part[0] * (M // tm)
        out_shape = (part[1] * M, N)
        out_spec = pl.BlockSpec((tm, tn), lambda i, j, k: (first + i, j))
    else:
        out_shape, out_spec = (N // piece, M, piece), pl.BlockSpec((per, tm, piece), lambda i, j, k: (j, i, 0))
    others = [] if into is None else [into]
    return pl.pallas_call(
        body,
        out_shape=jax.ShapeDtypeStruct(out_shape, BF16),
        grid=(M // tm, N // tn, nk),
        in_specs=[pl.BlockSpec((tk, tm), lambda i, j, k: (k, i)), pl.BlockSpec((tk, tn), lambda i, j, k: (k, j))]
        + [pl.BlockSpec(memory_space=pl.ANY)] * len(others),
        out_specs=out_spec,
        input_output_aliases={2: 0} if others else {},
        scratch_shapes=[pltpu.VMEM((tm, tn), F32)],
        compiler_params=_cparams(("parallel", "parallel", "arbitrary")),
        name=name,
    )(a, b, *others)


def _mm_rows(pairs, *, add=None, add_scale=1.0, out_dtype=F32, tm=512, dep=None, ln_bwd=None, name):
    M = pairs[0][0].shape[0]
    n = len(pairs)
    has_add = add is not None
    has_ple = ln_bwd is not None and len(ln_bwd) == 4
    w_rows = [w_.shape[0] // part[1] for _, w_, _, part in pairs]
    N = w_rows[0] if pairs[0][2] else pairs[0][1].shape[1]

    def body(*refs):
        acc = None
        for i, (_, _, tr, _) in enumerate(pairs):
            part = _bdot(refs[2 * i][...], refs[2 * i + 1][...], NT if tr else (((1,), (0,)), ((), ())))
            acc = part if acc is None else acc + part
        if has_add:
            acc = acc + add_scale * refs[2 * n][...]
        if ln_bwd is None:
            refs[-1][...] = acc.astype(out_dtype)
            return
        first = 2 * n + has_add
        z_ref, g_ref = refs[first], refs[first + 1]
        outs = refs[-(7 if has_ple else 4):]
        dz_ref, dzb_ref, dg_ref, db_ref = outs[:4]

        @pl.when(pl.program_id(0) == 0)
        def _():
            for sums in outs[2:4] + outs[6:]:
                sums[...] = jnp.zeros_like(sums)

        dg_acc = jnp.zeros((8, N), F32)
        db_acc = jnp.zeros((8, N), F32)
        dbg_acc = jnp.zeros((8, N), F32)
        for r0 in range(0, tm, LN_ROWS):
            rows = pl.ds(r0, LN_ROWS)
            do = acc[r0:r0 + LN_ROWS]
            dz, xh = _ln_bwd_rows(z_ref[rows, :], g_ref[...], do)
            dz_ref[rows, :] = dz
            dzb_ref[rows, :] = dz.astype(BF16)
            dg_acc = dg_acc + jnp.sum((do * xh).reshape(LN_ROWS // 8, 8, N), axis=0)
            db_acc = db_acc + jnp.sum(do.reshape(LN_ROWS // 8, 8, N), axis=0)
            if has_ple:
                ds, dp = _ple_bwd_rows(dz, refs[first + 2][rows, :], refs[first + 3][rows, :])
                outs[4][rows, :] = ds.astype(BF16)
                outs[5][rows, :] = dp.astype(BF16)
                dbg_acc = dbg_acc + jnp.sum(ds.reshape(LN_ROWS // 8, 8, N), axis=0)
        dg_ref[...] += jnp.sum(dg_acc, axis=0, keepdims=True)
        db_ref[...] += jnp.sum(db_acc, axis=0, keepdims=True)
        if has_ple:
            outs[6][...] += jnp.sum(dbg_acc, axis=0, keepdims=True)

    in_specs, args = [], []
    for (a, w_, _, part), rows in zip(pairs, w_rows):
        in_specs += [pl.BlockSpec((tm, a.shape[1]), lambda i: (i, 0)),
                     pl.BlockSpec((rows, w_.shape[1]), functools.partial(lambda i, j: (j, 0), j=part[0]))]
        args += [a, w_]
    row = pl.BlockSpec((tm, N), lambda i: (i, 0))
    fix = pl.BlockSpec((1, N), lambda i: (0, 0))
    if has_add:
        in_specs.append(row)
        args.append(add)
    if ln_bwd is not None:
        in_specs += [row, fix] + [row] * (len(ln_bwd) - 2)
        args += [ln_bwd[0], ln_bwd[1].reshape(1, N), *ln_bwd[2:]]
    if dep is not None:
        in_specs.append(pl.BlockSpec(memory_space=pl.ANY))
        args.append(dep)
    if ln_bwd is None:
        out_shape, out_specs = jax.ShapeDtypeStruct((M, N), out_dtype), row
    else:
        out_shape = [jax.ShapeDtypeStruct((M, N), F32), jax.ShapeDtypeStruct((M, N), BF16),
                     jax.ShapeDtypeStruct((1, N), F32), jax.ShapeDtypeStruct((1, N), F32)]
        out_specs = [row, row, fix, fix]
        if has_ple:
            out_shape += [jax.ShapeDtypeStruct((M, N), BF16), jax.ShapeDtypeStruct((M, N), BF16),
                          jax.ShapeDtypeStruct((1, N), F32)]
            out_specs += [row, row, fix]
    return pl.pallas_call(
        body,
        out_shape=out_shape,
        grid=(M // tm,),
        in_specs=in_specs,
        out_specs=out_specs,
        compiler_params=_cparams(("parallel",) if ln_bwd is None else ("arbitrary",)),
        name=name,
    )(*args)


def _ln_bwd_rows(zt, g, do):
    zc = zt - jnp.mean(zt, axis=-1, keepdims=True)
    rstd = lax.rsqrt(jnp.mean(zc * zc, axis=-1, keepdims=True) + LN_EPS)
    xh = zc * rstd
    dxh = do * g
    return rstd * (dxh - jnp.mean(dxh, axis=-1, keepdims=True) - xh * jnp.mean(dxh * xh, axis=-1, keepdims=True)), xh


def _layer_norm_rows(z, g, b):
    mu = jnp.mean(z, axis=-1, keepdims=True)
    zc = z - mu
    var = jnp.mean(zc * zc, axis=-1, keepdims=True)
    return zc * lax.rsqrt(var + LN_EPS) * g + b


def _proj_ln(res, a, w, ln_g, ln_b, *, ple=None, ts=512, name):
    S, D = res.shape
    ka = a.shape[1]
    has_ple = ple is not None
    row = lambda i: (i, 0)
    fix = lambda i: (0, 0)

    def body(*refs):
        if has_ple:
            (res_ref, a_ref, w_ref, g_ref, b_ref, wg_ref, bg_ref, p_ref, wp_ref, z_ref, r_ref, rb_ref, gate_ref,
             proj_ref, acc) = refs
        else:
            res_ref, a_ref, w_ref, g_ref, b_ref, z_ref, r_ref, rb_ref, acc = refs
        acc[...] = _bdot(a_ref[...], w_ref[...])
        if has_ple:
            gate_ref[...] = _bdot(res_ref[...], wg_ref[...])
            proj_ref[...] = _bdot(p_ref[...], wp_ref[...])
        for r0 in range(0, ts, LN_ROWS):
            rows = pl.ds(r0, LN_ROWS)
            z = ALPHA * res_ref[rows, :] + acc[rows, :]
            if has_ple:
                gate = _sigmoid(gate_ref[rows, :] + bg_ref[...])
                gate_ref[rows, :] = gate
                z = z + gate * proj_ref[rows, :]
            z_ref[rows, :] = z
            r = _layer_norm_rows(z, g_ref[...], b_ref[...])
            r_ref[rows, :] = r
            rb_ref[rows, :] = r.astype(BF16)

    in_specs = [pl.BlockSpec((ts, D), row), pl.BlockSpec((ts, ka), row), pl.BlockSpec((ka, D), fix),
                pl.BlockSpec((1, D), fix), pl.BlockSpec((1, D), fix)]
    args = [res, a, w, ln_g.reshape(1, D), ln_b.reshape(1, D)]
    out_dtypes = [F32, F32, BF16]
    if has_ple:
        wg, bg, p, wp = ple
        in_specs += [pl.BlockSpec((D, D), fix), pl.BlockSpec((1, D), fix), pl.BlockSpec((ts, PLE_DIM), row),
                     pl.BlockSpec((PLE_DIM, D), fix)]
        args += [wg, bg.reshape(1, D), p, wp]
        out_dtypes += [F32, F32]
    return pl.pallas_call(
        body,
        out_shape=[jax.ShapeDtypeStruct((S, D), dt) for dt in out_dtypes],
        grid=(S // ts,),
        in_specs=in_specs,
        out_specs=[pl.BlockSpec((ts, D), row)] * len(out_dtypes),
        scratch_shapes=[pltpu.VMEM((ts, D), F32)],
        compiler_params=_cparams(("parallel",)),
        name=name,
    )(*args)


CONV_ROWS = 32
LN_ROWS = 16


def _shifted_copies(src, dst, rows):
    for c0 in range(0, src.shape[1], SUB_LANES):
        ln = pl.ds(c0, SUB_LANES)
        for r0 in range(0, rows, SUB_ROWS):
            rc = min(SUB_ROWS, rows - r0)
            for b, shifted in enumerate(_rows_ahead(src, r0, rc, ln, range(1, 8))):
                dst[b, pl.ds(r0, rc), ln] = shifted


def _rows_at(src, copies, off, n, ln):
    b = off % 8
    return src[pl.ds(off, n), ln] if b == 0 else copies[b - 1, pl.ds(off - b, n), ln]


def _conv31(stg, gsh, cw_ref, cb_ref, out, rows, first_off):
    for c0 in range(0, D_CONV, SUB_LANES):
        ln = pl.ds(c0, SUB_LANES)
        for r0 in range(0, rows, CONV_ROWS):
            acc = jnp.zeros((CONV_ROWS, SUB_LANES), F32) + cb_ref[:, ln]
            for k in range(CONV_KERNEL):
                acc = acc + cw_ref[k:k + 1, ln] * _rows_at(stg, gsh, first_off + k + r0, CONV_ROWS, ln)
            out[pl.ds(r0, CONV_ROWS), ln] = acc


def _mixer_fwd(u, pool_w, pool_scale, conv_w, conv_b, cln_g, cln_b, *, ts=256):
    S = u.shape[0]
    hb = CONV_HALO
    nh = ts // hb

    def body(u_ref, uh_ref, pw_ref, ps_ref, cw_ref, cb_ref, g_ref, b_ref, y_ref, d_ref, hcs, sta, stg, gsh):
        i = pl.program_id(0)
        first = i == 0
        sta[pl.ds(0, hb), :] = jnp.where(first, 0.0, uh_ref[:, 0:D_POOL])
        sta[pl.ds(hb, ts), :] = u_ref[:, 0:D_POOL]
        glu_h = uh_ref[:, D_POOL:D_POOL + D_CONV] * _sigmoid(uh_ref[:, D_POOL + D_CONV:])
        stg[pl.ds(0, hb), :] = jnp.where(first, 0.0, glu_h)
        stg[pl.ds(hb, ts), :] = u_ref[:, D_POOL:D_POOL + D_CONV] * _sigmoid(u_ref[:, D_POOL + D_CONV:])

        for g, w in enumerate(POOL_WINDOWS):
            lanes = pl.ds(g * POOL_GROUP, POOL_GROUP)
            for r0 in range(0, ts, SUB_ROWS):
                s = None
                for q in range(0, w, 8):
                    for tap in _rows_back(sta, hb + r0 - q, SUB_ROWS, lanes, range(min(8, w - q))):
                        s = tap if s is None else s + tap
                pos = (i * ts + r0 + lax.broadcasted_iota(jnp.int32, (SUB_ROWS, 1), 0) + 1).astype(F32)
                d_g = s / jnp.minimum(pos, float(w)) - sta[pl.ds(hb + r0, SUB_ROWS), lanes]
                d_ref[pl.ds(r0, SUB_ROWS), lanes] = d_g.astype(BF16)
            y_ref[:, lanes] = (_bdot(d_ref[:, lanes], pw_ref[g]) * ps_ref[:, lanes]).astype(BF16)

        _shifted_copies(stg, gsh, hb + ts - 8)
        _conv31(stg, gsh, cw_ref, cb_ref, hcs, ts, hb - (CONV_KERNEL - 1))
        for r0 in range(0, ts, LN_ROWS):
            rows = pl.ds(r0, LN_ROWS)
            ln = _layer_norm_rows(hcs[rows, :], g_ref[...], b_ref[...])
            y_ref[rows, D_POOL:] = (ln * _sigmoid(ln)).astype(BF16)

    fix2 = lambda i: (0, 0)
    return pl.pallas_call(
        body,
        out_shape=[jax.ShapeDtypeStruct((S, D_MODEL), BF16), jax.ShapeDtypeStruct((S, D_POOL), BF16),
                   jax.ShapeDtypeStruct((S, D_CONV), F32)],
        grid=(S // ts,),
        in_specs=[pl.BlockSpec((ts, 3 * D_POOL), lambda i: (i, 0)),
                  pl.BlockSpec((hb, 3 * D_POOL), lambda i: (jnp.maximum(i * nh - 1, 0), 0)),
                  pl.BlockSpec((4, POOL_GROUP, POOL_GROUP), lambda i: (0, 0, 0)),
                  pl.BlockSpec((1, D_POOL), fix2), pl.BlockSpec((CONV_KERNEL, D_CONV), fix2),
                  pl.BlockSpec((1, D_CONV), fix2), pl.BlockSpec((1, D_CONV), fix2), pl.BlockSpec((1, D_CONV), fix2)],
        out_specs=[pl.BlockSpec((ts, D_MODEL), lambda i: (i, 0)), pl.BlockSpec((ts, D_POOL), lambda i: (i, 0)),
                   pl.BlockSpec((ts, D_CONV), lambda i: (i, 0))],
        scratch_shapes=[pltpu.VMEM((hb + ts, D_POOL), F32), pltpu.VMEM((hb + ts, D_CONV), F32),
                        pltpu.VMEM((7, hb + ts - 8, D_CONV), F32)],
        compiler_params=_cparams(("parallel",)),
        name="mixer_fwd",
    )(u, u, pool_w, pool_scale.reshape(1, D_POOL), conv_w, conv_b.reshape(1, D_CONV), cln_g.reshape(1, D_CONV),
      cln_b.reshape(1, D_CONV))


def _mixer_bwd(u, d, hc, dycat, pool_w, pool_scale, conv_w, cln_g, cln_b, *, ts=256):
    S = u.shape[0]
    hb = CONV_HALO
    nh = ts // hb
    n = S // ts
    te = ts + hb
    K = CONV_KERNEL

    def body(u_ref, up_ref, un_ref, d_ref, hc_ref, hcn_ref, dy_ref, dyn_ref, pw_ref, ps_ref, cw_ref, g_ref, b_ref,
             du_ref, dpw_ref, dps_ref, dcw_ref, dcb_ref, dg_ref, db_ref, stg, std, sth, gsh, hsh):
        i = pl.program_id(0)
        first = i == 0
        last = i == n - 1

        @pl.when(first)
        def _():
            dpw_ref[...] = jnp.zeros_like(dpw_ref)
            dps_ref[...] = jnp.zeros_like(dps_ref)
            dcw_ref[...] = jnp.zeros_like(dcw_ref)
            dcb_ref[...] = jnp.zeros_like(dcb_ref)
            dg_ref[...] = jnp.zeros_like(dg_ref)
            db_ref[...] = jnp.zeros_like(db_ref)

        pos_e = (i * ts + lax.broadcasted_iota(jnp.int32, (te, 1), 0) + 1).astype(F32)
        dya = dy_ref[:, 0:D_POOL]
        dya_n = jnp.where(last, 0.0, dyn_ref[:, 0:D_POOL])
        for g, w in enumerate(POOL_WINDOWS):
            lanes = pl.ds(g * POOL_GROUP, POOL_GROUP)
            sl = slice(g * POOL_GROUP, (g + 1) * POOL_GROUP)
            pw = pw_ref[g]
            scale = ps_ref[:, lanes]
            d_g = d_ref[:, lanes]
            pre = _bdot(d_g, pw)
            dps_ref[:, lanes] += jnp.sum(dya[:, sl] * pre, axis=0, keepdims=True)
            dys = dya[:, sl] * scale
            dpw_ref[g] += _bdot(d_g, dys, TN)
            dys_e = jnp.concatenate([dys, dya_n[:, sl] * scale], axis=0)
            dd = _bdot(dys_e, pw, NT)
            std[:, lanes] = dd / jnp.minimum(pos_e, float(w))
            for r0 in range(0, ts, SUB_ROWS):
                da = -dd[r0:r0 + SUB_ROWS]
                for q in range(0, w, 8):
                    for tap in _rows_ahead(std, r0 + q, SUB_ROWS, lanes, range(min(8, w - q))):
                        da = da + tap
                du_ref[pl.ds(r0, SUB_ROWS), lanes] = da.astype(BF16)

        glu_p = up_ref[:, D_POOL:D_POOL + D_CONV] * _sigmoid(up_ref[:, D_POOL + D_CONV:])
        stg[pl.ds(0, hb), :] = jnp.where(first, 0.0, glu_p)
        bv = u_ref[:, D_POOL:D_POOL + D_CONV]
        sg = _sigmoid(u_ref[:, D_POOL + D_CONV:])
        stg[pl.ds(hb, ts), :] = bv * sg
        glu_n = un_ref[:, D_POOL:D_POOL + D_CONV] * _sigmoid(un_ref[:, D_POOL + D_CONV:])
        stg[pl.ds(hb + ts, hb), :] = jnp.where(last, 0.0, glu_n)
        _shifted_copies(stg, gsh, hb + te - 8)

        sums = [jnp.zeros((8, D_CONV), F32) for _ in range(3)]
        for r0 in range(0, te, LN_ROWS):
            rows = pl.ds(r0, LN_ROWS)
            hc = hc_ref[rows, :] if r0 < ts else hcn_ref[pl.ds(r0 - ts, LN_ROWS), :]
            hcc = hc - jnp.mean(hc, axis=-1, keepdims=True)
            rstd = lax.rsqrt(jnp.mean(hcc * hcc, axis=-1, keepdims=True) + LN_EPS)
            xh = hcc * rstd
            ln = xh * g_ref[...] + b_ref[...]
            sl_ = _sigmoid(ln)
            if r0 < ts:
                dyb = dy_ref[rows, D_POOL:]
            else:
                dyb = jnp.where(last, 0.0, dyn_ref[pl.ds(r0 - ts, LN_ROWS), D_POOL:])
            dln = dyb * (sl_ * (1.0 + ln * (1.0 - sl_)))
            dxh = dln * g_ref[...]
            dhc = rstd * (dxh - jnp.mean(dxh, axis=-1, keepdims=True)
                          - xh * jnp.mean(dxh * xh, axis=-1, keepdims=True))
            sth[rows, :] = dhc
            if r0 < ts:
                for n_, term in enumerate((dln * xh, dln, dhc)):
                    sums[n_] = sums[n_] + jnp.sum(term.reshape(LN_ROWS // 8, 8, D_CONV), axis=0)
        dg_ref[...] += jnp.sum(sums[0], axis=0, keepdims=True)
        db_ref[...] += jnp.sum(sums[1], axis=0, keepdims=True)
        dcb_ref[...] += jnp.sum(sums[2], axis=0, keepdims=True)

        _shifted_copies(sth, hsh, te - 8)
        for c0 in range(0, D_CONV, SUB_LANES):
            ln_ = pl.ds(c0, SUB_LANES)
            for r0 in range(0, ts, CONV_ROWS):
                rows = pl.ds(r0, CONV_ROWS)
                dglu = jnp.zeros((CONV_ROWS, SUB_LANES), F32)
                for k in range(K):
                    dglu = dglu + cw_ref[k:k + 1, ln_] * _rows_at(sth, hsh, K - 1 - k + r0, CONV_ROWS, ln_)
                bv = u_ref[rows, pl.ds(D_POOL + c0, SUB_LANES)]
                sg = _sigmoid(u_ref[rows, pl.ds(D_POOL + D_CONV + c0, SUB_LANES)])
                du_ref[rows, pl.ds(D_POOL + c0, SUB_LANES)] = (dglu * sg).astype(BF16)
                du_ref[rows, pl.ds(D_POOL + D_CONV + c0, SUB_LANES)] = (dglu * bv * sg * (1.0 - sg)).astype(BF16)
            for k in range(K):
                tap = jnp.zeros((8, SUB_LANES), F32)
                for r0 in range(0, ts, CONV_ROWS):
                    prod = sth[pl.ds(r0, CONV_ROWS), ln_] * _rows_at(stg, gsh, hb - (K - 1) + k + r0, CONV_ROWS, ln_)
                    tap = tap + jnp.sum(prod.reshape(CONV_ROWS // 8, 8, SUB_LANES), axis=0)
                dcw_ref[k:k + 1, ln_] += jnp.sum(tap, axis=0, keepdims=True)

    fix2 = lambda i: (0, 0)
    prev = lambda i: (jnp.maximum(i * nh - 1, 0), 0)
    nxt = lambda i: (jnp.minimum((i + 1) * nh, S // hb - 1), 0)
    return pl.pallas_call(
        body,
        out_shape=[jax.ShapeDtypeStruct((S, 3 * D_POOL), BF16),
                   jax.ShapeDtypeStruct((4, POOL_GROUP, POOL_GROUP), F32),
                   jax.ShapeDtypeStruct((1, D_POOL), F32),
                   jax.ShapeDtypeStruct((K, D_CONV), F32),
                   jax.ShapeDtypeStruct((1, D_CONV), F32),
                   jax.ShapeDtypeStruct((1, D_CONV), F32),
                   jax.ShapeDtypeStruct((1, D_CONV), F32)],
        grid=(n,),
        in_specs=[pl.BlockSpec((ts, 3 * D_POOL), lambda i: (i, 0)),
                  pl.BlockSpec((hb, 3 * D_POOL), prev),
                  pl.BlockSpec((hb, 3 * D_POOL), nxt),
                  pl.BlockSpec((ts, D_POOL), lambda i: (i, 0)),
                  pl.BlockSpec((ts, D_CONV), lambda i: (i, 0)),
                  pl.BlockSpec((hb, D_CONV), nxt),
                  pl.BlockSpec((ts, D_MODEL), lambda i: (i, 0)),
                  pl.BlockSpec((hb, D_MODEL), nxt),
                  pl.BlockSpec((4, POOL_GROUP, POOL_GROUP), lambda i: (0, 0, 0)),
                  pl.BlockSpec((1, D_POOL), fix2), pl.BlockSpec((K, D_CONV), fix2),
                  pl.BlockSpec((1, D_CONV), fix2), pl.BlockSpec((1, D_CONV), fix2)],
        out_specs=[pl.BlockSpec((ts, 3 * D_POOL), lambda i: (i, 0)),
                   pl.BlockSpec((4, POOL_GROUP, POOL_GROUP), lambda i: (0, 0, 0)),
                   pl.BlockSpec((1, D_POOL), fix2), pl.BlockSpec((K, D_CONV), fix2),
                   pl.BlockSpec((1, D_CONV), fix2), pl.BlockSpec((1, D_CONV), fix2), pl.BlockSpec((1, D_CONV), fix2)],
        scratch_shapes=[pltpu.VMEM((hb + ts + hb, D_CONV), F32), pltpu.VMEM((te, D_POOL), F32),
                        pltpu.VMEM((te, D_CONV), F32), pltpu.VMEM((7, hb + te - 8, D_CONV), F32),
                        pltpu.VMEM((7, te - 8, D_CONV), F32)],
        compiler_params=_cparams(("arbitrary",)),
        name="mixer_bwd",
    )(u, u, u, d, hc, hc, dycat, dycat, pool_w, pool_scale.reshape(1, D_POOL), conv_w, cln_g.reshape(1, D_CONV),
      cln_b.reshape(1, D_CONV))


_GELU_C = math.sqrt(2.0 / math.pi)


def _gelu_parts(x):
    inner = _GELU_C * (x + 0.044715 * x * x * x)
    th = jnp.tanh(inner)
    ge = 0.5 * x * (1.0 + th)
    dge = 0.5 * (1.0 + th) + 0.5 * x * (1.0 - th * th) * (_GELU_C * (1.0 + 3.0 * 0.044715 * x * x))
    return ge, dge


def _rows_back(ref, r, n, ln, shifts):
    ext = ref[pl.ds(r - 8, n + 8), ln]
    return [(pltpu.roll(ext, s, 0) if s else ext)[8:] for s in shifts]


def _rows_ahead(ref, r, n, ln, shifts):
    ext = ref[pl.ds(r, n + 8), ln]
    return [(pltpu.roll(ext, n + 8 - s, 0) if s else ext)[:n] for s in shifts]


def _ffn_act_fwd(gate, val, dw_w, dw_b, *, ts=512, tc=1408, name):
    S, F = gate.shape
    hb = FFN_HALO
    nh = ts // hb
    tc = _tile(F, tc)

    def body(g_ref, gh_ref, v_ref, w_ref, b_ref, h_ref, st):
        i = pl.program_id(0)
        st[pl.ds(0, hb), :] = jnp.where(i == 0, 0.0, gh_ref[...].astype(F32))
        st[pl.ds(hb, ts), :] = g_ref[...].astype(F32)
        for c0 in range(0, tc, SUB_LANES):
            ln = pl.ds(c0, SUB_LANES)
            w0, w1, w2, b = w_ref[0:1, ln], w_ref[1:2, ln], w_ref[2:3, ln], b_ref[:, ln]
            for r0 in range(0, ts, SUB_ROWS):
                taps = _rows_back(st, hb + r0, SUB_ROWS, ln, (2, 1, 0))
                gc = b + w0 * taps[0] + w1 * taps[1] + w2 * taps[2]
                ge, _ = _gelu_parts(gc)
                rows = pl.ds(r0, SUB_ROWS)
                h_ref[rows, ln] = (ge * v_ref[rows, ln].astype(F32)).astype(BF16)

    return pl.pallas_call(
        body,
        out_shape=jax.ShapeDtypeStruct((S, F), BF16),
        grid=(S // ts, F // tc),
        in_specs=[pl.BlockSpec((ts, tc), lambda i, j: (i, j)),
                  pl.BlockSpec((hb, tc), lambda i, j: (jnp.maximum(i * nh - 1, 0), j)),
                  pl.BlockSpec((ts, tc), lambda i, j: (i, j)),
                  pl.BlockSpec((3, tc), lambda i, j: (0, j)),
                  pl.BlockSpec((1, tc), lambda i, j: (0, j))],
        out_specs=pl.BlockSpec((ts, tc), lambda i, j: (i, j)),
        scratch_shapes=[pltpu.VMEM((hb + ts, tc), F32)],
        compiler_params=_cparams(("parallel", "parallel")),
        name=name,
    )(gate, gate, val, dw_w, dw_b.reshape(1, F))


def _ffn_act_bwd(gate, val, dh, dw_w, dw_b, *, ts=512, tc=1408, name):
    S, F = gate.shape
    hb = FFN_HALO
    nh = ts // hb
    n = S // ts
    te = ts + hb
    tc = _tile(F, tc)

    def body(g_ref, gp_ref, gn_ref, v_ref, vn_ref, dh_ref, dhn_ref, w_ref, b_ref,
             dg_ref, dv_ref, dw_ref, db_ref, st, sd):
        i = pl.program_id(1)
        first = i == 0
        last = i == n - 1

        @pl.when(first)
        def _():
            dw_ref[...] = jnp.zeros_like(dw_ref)
            db_ref[...] = jnp.zeros_like(db_ref)

        st[pl.ds(0, hb), :] = jnp.where(first, 0.0, gp_ref[...].astype(F32))
        st[pl.ds(hb, ts), :] = g_ref[...].astype(F32)
        st[pl.ds(hb + ts, hb), :] = jnp.where(last, 0.0, gn_ref[...].astype(F32))
        for c0 in range(0, tc, SUB_LANES):
            ln = pl.ds(c0, SUB_LANES)
            w0, w1, w2, b = w_ref[0:1, ln], w_ref[1:2, ln], w_ref[2:3, ln], b_ref[:, ln]
            db_acc = jnp.zeros((8, SUB_LANES), F32)
            dw_acc = [jnp.zeros((8, SUB_LANES), F32) for _ in range(3)]
            for r0 in range(0, te, SUB_ROWS):
                rc = min(SUB_ROWS, te - r0)
                taps = _rows_back(st, hb + r0, rc, ln, (2, 1, 0))
                gc = b + w0 * taps[0] + w1 * taps[1] + w2 * taps[2]
                ge, dge = _gelu_parts(gc)
                if r0 < ts:
                    rows = pl.ds(r0, rc)
                    val, dh = v_ref[rows, ln].astype(F32), dh_ref[rows, ln].astype(F32)
                else:
                    val = jnp.where(last, 0.0, vn_ref[:, ln].astype(F32)[0:rc])
                    dh = jnp.where(last, 0.0, dhn_ref[:, ln].astype(F32)[0:rc])
                dgc = dh * val * dge
                sd[pl.ds(r0, rc), ln] = dgc
                if r0 < ts:
                    dv_ref[rows, ln] = (dh * ge).astype(BF16)
                    db_acc = db_acc + jnp.sum(dgc.reshape(rc // 8, 8, SUB_LANES), axis=0)
                    for k in range(3):
                        dw_acc[k] = dw_acc[k] + jnp.sum((dgc * taps[k]).reshape(rc // 8, 8, SUB_LANES), axis=0)
            db_ref[:, ln] += jnp.sum(db_acc, axis=0, keepdims=True)
            for k in range(3):
                dw_ref[k:k + 1, ln] += jnp.sum(dw_acc[k], axis=0, keepdims=True)
            for r0 in range(0, ts, SUB_ROWS):
                ahead = _rows_ahead(sd, r0, SUB_ROWS, ln, (2, 1, 0))
                dg_ref[pl.ds(r0, SUB_ROWS), ln] = (w0 * ahead[0] + w1 * ahead[1] + w2 * ahead[2]).astype(BF16)

    cur = lambda j, i: (i, j)
    prev = lambda j, i: (jnp.maximum(i * nh - 1, 0), j)
    nxt = lambda j, i: (jnp.minimum((i + 1) * nh, S // hb - 1), j)
    return pl.pallas_call(
        body,
        out_shape=[jax.ShapeDtypeStruct((S, F), BF16), jax.ShapeDtypeStruct((S, F), BF16),
                   jax.ShapeDtypeStruct((3, F), F32), jax.ShapeDtypeStruct((1, F), F32)],
        grid=(F // tc, n),
        in_specs=[pl.BlockSpec((ts, tc), cur), pl.BlockSpec((hb, tc), prev), pl.BlockSpec((hb, tc), nxt),
                  pl.BlockSpec((ts, tc), cur), pl.BlockSpec((hb, tc), nxt),
                  pl.BlockSpec((ts, tc), cur), pl.BlockSpec((hb, tc), nxt),
                  pl.BlockSpec((3, tc), lambda j, i: (0, j)), pl.BlockSpec((1, tc), lambda j, i: (0, j))],
        out_specs=[pl.BlockSpec((ts, tc), cur), pl.BlockSpec((ts, tc), cur),
                   pl.BlockSpec((3, tc), lambda j, i: (0, j)), pl.BlockSpec((1, tc), lambda j, i: (0, j))],
        scratch_shapes=[pltpu.VMEM((hb + ts + hb, tc), F32), pltpu.VMEM((te, tc), F32)],
        compiler_params=_cparams(("parallel", "arbitrary")),
        name=name,
    )(gate, gate, gate, val, val, dh, dh, dw_w, dw_b.reshape(1, F))


def _ple_bwd_rows(dz, gate, proj):
    return dz * proj * gate * (1.0 - gate), dz * gate


def _loss_ln_bwd(z, ln_g, ln_b, target, gate, proj, *, ts=512, name):
    S, D = z.shape

    def body(z_ref, g_ref, b_ref, t_ref, gate_ref, proj_ref, dz_ref, dzb_ref, dg_ref, db_ref, loss_ref, ds_ref,
             dp_ref, dbg_ref):
        i = pl.program_id(0)

        @pl.when(i == 0)
        def _():
            dg_ref[...] = jnp.zeros_like(dg_ref)
            db_ref[...] = jnp.zeros_like(db_ref)
            loss_ref[...] = jnp.zeros_like(loss_ref)
            dbg_ref[...] = jnp.zeros_like(dbg_ref)

        dg_acc = jnp.zeros((8, D), F32)
        db_acc = jnp.zeros((8, D), F32)
        dbg_acc = jnp.zeros((8, D), F32)
        loss_acc = jnp.zeros((1, 1), F32)
        for r0 in range(0, ts, LN_ROWS):
            rows = pl.ds(r0, LN_ROWS)
            zt = z_ref[rows, :]
            err = _layer_norm_rows(zt, g_ref[...], b_ref[...]) - t_ref[rows, :]
            loss_acc = loss_acc + 0.5 * jnp.sum(jnp.mean(err * err, axis=-1, keepdims=True), keepdims=True)
            do = err * (1.0 / D)
            dz, xh = _ln_bwd_rows(zt, g_ref[...], do)
            dg_acc = dg_acc + jnp.sum((do * xh).reshape(LN_ROWS // 8, 8, D), axis=0)
            db_acc = db_acc + jnp.sum(do.reshape(LN_ROWS // 8, 8, D), axis=0)
            dz_ref[rows, :] = dz
            dzb_ref[rows, :] = dz.astype(BF16)
            ds, dp = _ple_bwd_rows(dz, gate_ref[rows, :], proj_ref[rows, :])
            ds_ref[rows, :] = ds.astype(BF16)
            dp_ref[rows, :] = dp.astype(BF16)
            dbg_acc = dbg_acc + jnp.sum(ds.reshape(LN_ROWS // 8, 8, D), axis=0)
        dg_ref[...] += jnp.sum(dg_acc, axis=0, keepdims=True)
        db_ref[...] += jnp.sum(db_acc, axis=0, keepdims=True)
        dbg_ref[...] += jnp.sum(dbg_acc, axis=0, keepdims=True)
        loss_ref[...] += loss_acc

    row = pl.BlockSpec((ts, D), lambda i: (i, 0))
    fix = pl.BlockSpec((1, D), lambda i: (0, 0))
    return pl.pallas_call(
        body,
        out_shape=[jax.ShapeDtypeStruct((S, D), F32), jax.ShapeDtypeStruct((S, D), BF16),
                   jax.ShapeDtypeStruct((1, D), F32), jax.ShapeDtypeStruct((1, D), F32),
                   jax.ShapeDtypeStruct((8, 128), F32), jax.ShapeDtypeStruct((S, D), BF16),
                   jax.ShapeDtypeStruct((S, D), BF16), jax.ShapeDtypeStruct((1, D), F32)],
        grid=(S // ts,),
        in_specs=[row, fix, fix, row, row, row],
        out_specs=[row, row, fix, fix, pl.BlockSpec((8, 128), lambda i: (0, 0)), row, row, fix],
        compiler_params=_cparams(("arbitrary",)),
        name=name,
    )(z, ln_g.reshape(1, D), ln_b.reshape(1, D), target, gate, proj)


HEADS_PER_STEP = 4
HEAD_LANES = HEADS_PER_STEP * HEAD_DIM


ATT_ROWS = 32
ATT_SCALE = HEAD_DIM ** -0.5


def _softmax_piece(scores, bias, qb):
    s = scores + bias
    kpos = qb * Q_BLOCK + lax.broadcasted_iota(jnp.int32, (1, KV_SPAN), 1)
    s = jnp.where(kpos >= KV_PAD, s, NEG_INF)
    e = jnp.exp(s - jnp.max(s, axis=-1, keepdims=True))
    return e * (1.0 / jnp.sum(e, axis=-1, keepdims=True))


def _head_masks():
    lane = lax.broadcasted_iota(jnp.int32, (1, HEAD_LANES), 1)
    return [(lane >= j * HEAD_DIM) & (lane < (j + 1) * HEAD_DIM) for j in range(HEADS_PER_STEP)]


def _pick_heads(masks, per_head):
    out = per_head[0]
    for mask, x in zip(masks[1:], per_head[1:]):
        out = jnp.where(mask, x, out)
    return out


def _pad_keys(qb, k_ref, v_ref, kp, vp):
    @pl.when(qb == 0)
    def _():
        kp[pl.ds(0, KV_PAD), :] = jnp.zeros((KV_PAD, HEAD_LANES), BF16)
        vp[pl.ds(0, KV_PAD), :] = jnp.zeros((KV_PAD, HEAD_LANES), BF16)
        kp[pl.ds(KV_PAD, k_ref.shape[0]), :] = k_ref[...]
        vp[pl.ds(KV_PAD, v_ref.shape[0]), :] = v_ref[...]


def _attn_fwd(qkv, bias):
    S = qkv.shape[0]
    nhp = N_HEADS // HEADS_PER_STEP

    def body(q_ref, k_ref, v_ref, b_ref, o_ref, kp, vp, p_scr):
        qb = pl.program_id(1)
        _pad_keys(qb, k_ref, v_ref, kp, vp)
        span = pl.ds(pl.multiple_of(qb * Q_BLOCK, Q_BLOCK), KV_SPAN)
        kc, vc = kp[span, :], vp[span, :]
        qt = q_ref[...] * ATT_SCALE
        mine = _head_masks()
        scores = [_bdot(jnp.where(mine[j], qt, jnp.zeros_like(qt)), kc, NT) for j in range(HEADS_PER_STEP)]
        outs = []
        for j in range(HEADS_PER_STEP):
            for r0 in range(0, Q_BLOCK, ATT_ROWS):
                rows = pl.ds(r0, ATT_ROWS)
                p_scr[j, rows, :] = _softmax_piece(scores[j][r0:r0 + ATT_ROWS], b_ref[j, rows, :], qb).astype(BF16)
            outs.append(_bdot(p_scr[j], vc))
        o_ref[...] = _pick_heads(mine, outs).astype(BF16)

    return pl.pallas_call(
        body,
        out_shape=jax.ShapeDtypeStruct((S, D_MODEL), BF16),
        grid=(nhp, S // Q_BLOCK),
        in_specs=[pl.BlockSpec((Q_BLOCK, HEAD_LANES), lambda h, i: (i, h)),
                  pl.BlockSpec((S, HEAD_LANES), lambda h, i: (0, nhp + h)),
                  pl.BlockSpec((S, HEAD_LANES), lambda h, i: (0, 2 * nhp + h)),
                  pl.BlockSpec((HEADS_PER_STEP, Q_BLOCK, KV_SPAN), lambda h, i: (h, 0, 0))],
        out_specs=pl.BlockSpec((Q_BLOCK, HEAD_LANES), lambda h, i: (i, h)),
        scratch_shapes=[pltpu.VMEM((KV_PAD + S, HEAD_LANES), BF16), pltpu.VMEM((KV_PAD + S, HEAD_LANES), BF16),
                        pltpu.VMEM((HEADS_PER_STEP, Q_BLOCK, KV_SPAN), BF16)],
        compiler_params=_cparams(("parallel", "arbitrary")),
        name="attn_fwd",
    )(qkv, qkv, qkv, bias)


def _attn_bwd(qkv, bias, do):
    S = qkv.shape[0]
    nhp = N_HEADS // HEADS_PER_STEP
    nq = S // Q_BLOCK
    scale = HEAD_DIM ** -0.5

    def body(q_ref, k_ref, v_ref, b_ref, do_ref, dq_ref, dk_ref, dv_ref, db_ref, kp, vp, dka, dva,
             p_scr, ds_scr):
        qb = pl.program_id(1)
        _pad_keys(qb, k_ref, v_ref, kp, vp)

        @pl.when(qb == 0)
        def _():
            dka[...] = jnp.zeros_like(dka)
            dva[...] = jnp.zeros_like(dva)
            db_ref[...] = jnp.zeros_like(db_ref)

        span = pl.ds(pl.multiple_of(qb * Q_BLOCK, Q_BLOCK), KV_SPAN)
        kc, vc = kp[span, :], vp[span, :]
        qt, dot = q_ref[...] * ATT_SCALE, do_ref[...]
        mine = _head_masks()
        dqs = []
        qs = [jnp.where(mine[j], qt, jnp.zeros_like(qt)) for j in range(HEADS_PER_STEP)]
        dos = [jnp.where(mine[j], dot, jnp.zeros_like(dot)) for j in range(HEADS_PER_STEP)]
        scores = [_bdot(qs[j], kc, NT) for j in range(HEADS_PER_STEP)]
        dps = [_bdot(dos[j], vc, NT) for j in range(HEADS_PER_STEP)]
        for j in range(HEADS_PER_STEP):
            qj, doj = qs[j], dos[j]
            for r0 in range(0, Q_BLOCK, ATT_ROWS):
                rows = pl.ds(r0, ATT_ROWS)
                p = _softmax_piece(scores[j][r0:r0 + ATT_ROWS], b_ref[j, rows, :], qb)
                dp = dps[j][r0:r0 + ATT_ROWS]
                ds = p * (dp - jnp.sum(p * dp, axis=-1, keepdims=True))
                db_ref[j, rows, :] += ds
                p_scr[j, rows, :] = p.astype(BF16)
                ds_scr[j, rows, :] = ds.astype(BF16)
            dva[span, :] += _bdot(p_scr[j], doj, TN)
            dqs.append(_bdot(ds_scr[j], kc))
            dka[span, :] += _bdot(ds_scr[j], qj, TN)
        dq_ref[...] = (scale * _pick_heads(mine, dqs)).astype(BF16)

        @pl.when(qb == nq - 1)
        def _():
            dk_ref[...] = dka[pl.ds(KV_PAD, S), :].astype(BF16)
            dv_ref[...] = dva[pl.ds(KV_PAD, S), :].astype(BF16)

    blk = pl.BlockSpec((Q_BLOCK, HEAD_LANES), lambda h, i: (i, h))
    col = pl.BlockSpec((S, HEAD_LANES), lambda h, i: (0, h))
    bsp = pl.BlockSpec((HEADS_PER_STEP, Q_BLOCK, KV_SPAN), lambda h, i: (h, 0, 0))
    return pl.pallas_call(
        body,
        out_shape=[jax.ShapeDtypeStruct((S, D_MODEL), BF16)] * 3
        + [jax.ShapeDtypeStruct((N_HEADS, Q_BLOCK, KV_SPAN), F32)],
        grid=(nhp, nq),
        in_specs=[blk, pl.BlockSpec((S, HEAD_LANES), lambda h, i: (0, nhp + h)),
                  pl.BlockSpec((S, HEAD_LANES), lambda h, i: (0, 2 * nhp + h)), bsp, blk],
        out_specs=[blk, col, col, bsp],
        scratch_shapes=[pltpu.VMEM((KV_PAD + S, HEAD_LANES), BF16), pltpu.VMEM((KV_PAD + S, HEAD_LANES), BF16),
                        pltpu.VMEM((KV_PAD + S, HEAD_LANES), F32), pltpu.VMEM((KV_PAD + S, HEAD_LANES), F32),
                        pltpu.VMEM((HEADS_PER_STEP, Q_BLOCK, KV_SPAN), BF16), pltpu.VMEM((HEADS_PER_STEP, Q_BLOCK, KV_SPAN), BF16)],
        compiler_params=_cparams(("parallel", "arbitrary")),
        name="attn_bwd",
    )(qkv, qkv, qkv, bias, do)


N_DIST = BAND + CHUNK - 1
N_FAR = KV_PAD + CHUNK - MAX_REL


def _shear_rows(x, towards_right):
    row = lax.broadcasted_iota(jnp.int32, (Q_BLOCK, 1), 0)
    for bit in range(Q_BLOCK.bit_length() - 1):
        step = 1 << bit
        x = jnp.where((row & step) != 0, pltpu.roll(x, step if towards_right else KV_SPAN - step, 1), x)
    return x


def _bias_blocks(rel_bias):
    H = rel_bias.shape[0]
    e = jnp.concatenate([jnp.broadcast_to(rel_bias[:, 2 * MAX_REL:], (H, N_FAR)),
                         jnp.flip(rel_bias[:, 2 * MAX_REL - (N_DIST - N_FAR):2 * MAX_REL], axis=1),
                         jnp.zeros((H, KV_SPAN - N_DIST), F32)], axis=1).reshape(H, 1, KV_SPAN)

    def body(e_ref, o_ref):
        first = pltpu.roll(jnp.broadcast_to(e_ref[...], (Q_BLOCK, KV_SPAN)), KV_SPAN - (CHUNK - 1), 1)
        x = _shear_rows(first, True)
        row = lax.broadcasted_iota(jnp.int32, (Q_BLOCK, 1), 0)
        chunk0 = row - (row & (CHUNK - 1))
        k = lax.broadcasted_iota(jnp.int32, (1, KV_SPAN), 1)
        o_ref[...] = jnp.where((k >= chunk0) & (k < chunk0 + BAND), x, NEG_INF)

    return pl.pallas_call(
        body,
        out_shape=jax.ShapeDtypeStruct((H, Q_BLOCK, KV_SPAN), F32),
        grid=(H,),
        in_specs=[pl.BlockSpec((None, 1, KV_SPAN), lambda h: (h, 0, 0))],
        out_specs=pl.BlockSpec((None, Q_BLOCK, KV_SPAN), lambda h: (h, 0, 0)),
        compiler_params=_cparams(("parallel",)),
        name="bias_blocks",
    )(e)


def _bias_blocks_grad(dblk):
    H = dblk.shape[0]

    def body(d_ref, o_ref):
        x = pltpu.roll(_shear_rows(d_ref[...], False), CHUNK - 1, 1)
        de = jnp.sum(x, axis=0, keepdims=True)
        lane = lax.broadcasted_iota(jnp.int32, de.shape, 1)
        far = jnp.sum(jnp.where(lane < N_FAR, de, 0.0), axis=-1, keepdims=True)
        o_ref[...] = jnp.where(lane == 0, far, jnp.where(lane < N_FAR, 0.0, de))

    de = pl.pallas_call(
        body,
        out_shape=jax.ShapeDtypeStruct((H, 1, KV_SPAN), F32),
        grid=(H,),
        in_specs=[pl.BlockSpec((None, Q_BLOCK, KV_SPAN), lambda h: (h, 0, 0))],
        out_specs=pl.BlockSpec((None, 1, KV_SPAN), lambda h: (h, 0, 0)),
        compiler_params=_cparams(("parallel",)),
        name="bias_grad_sum",
    )(dblk).reshape(H, KV_SPAN)
    near = jnp.flip(de[:, N_FAR:N_DIST], axis=1)
    return jnp.concatenate([jnp.zeros((H, 2 * MAX_REL - (N_DIST - N_FAR)), F32), near, de[:, 0:1]], axis=1)


def _ffn_forward(r1, r1b, p_l, w, l, ready):
    ready(f"up{l}", r1b)
    up_g = _mm_rows([(r1b, w["ffn_up_t"][l], True, (0, 2))], out_dtype=BF16, name=f"ffn_up_g{l}")
    up_v = _mm_rows([(r1b, w["ffn_up_t"][l], True, (1, 2))], out_dtype=BF16, name=f"ffn_up_v{l}")
    h = _ffn_act_fwd(up_g, up_v, w["ffn_dw_w"][l], w["ffn_dw_b"][l], name=f"ffn_act{l}")
    ready(f"dn{l}", h)
    z2, r2, r2b, gate, proj = _proj_ln(r1, h, w["ffn_w_down"][l], w["ln_ffn_g"][l], w["ln_ffn_b"][l],
                                       ple=(w["ple_w_gate"][l], w["ple_b_gate"][l], p_l, w["ple_w_proj"][l]),
                                       name=f"ffn_down_ln{l}")
    return dict(r1b=r1b, up_g=up_g, up_v=up_v, h=h, z2=z2, gate=gate, proj=proj), r2, r2b


def _ffn_backward(sv, dz2, dz2b, ple_bwd, p_l, w, l, grads, ln_bwd, emit):
    r1b = sv["r1b"]
    ds, dproj, db_gate = ple_bwd
    dh = _mm_rows([(dz2b, w["ffn_w_down"][l], True, WHOLE)], out_dtype=BF16, name=f"ffn_dh{l}")
    dgate, dval, d_dw_w, d_dw_b = _ffn_act_bwd(sv["up_g"], sv["up_v"], dh, w["ffn_dw_w"][l], w["ffn_dw_b"][l],
                                               name=f"ffn_act_bwd{l}")
    grads["ffn_w_down"][l] = _wgrad(sv["h"], dz2b, tm=1408, name=f"d_ffn_w_down{l}")
    d_up_g = _wgrad(dgate, r1b, tm=1408, part=(0, 2), name=f"d_ffn_up_g{l}")
    grads["ffn_up_t"][l] = _wgrad(dval, r1b, tm=1408, part=(1, 2), into=d_up_g, name=f"d_ffn_up_v{l}")
    grads["ple_w_gate"][l] = _wgrad(r1b, ds, name=f"d_ple_w_gate{l}")
    grads["ple_w_proj"][l] = _wgrad(p_l, dproj, piece=D_MODEL // N_DEV, name=f"d_ple_w_proj{l}")
    grads["ffn_dw_w"][l] = d_dw_w
    grads["ffn_dw_b"][l] = d_dw_b[0]
    grads["ple_b_gate"][l] = db_gate[0]
    return _mm_rows([(ds, w["ple_w_gate"][l], True, WHOLE), (dgate, w["ffn_up_t"][l], False, (0, 2)),
                     (dval, w["ffn_up_t"][l], False, (1, 2))], add=dz2, add_scale=ALPHA, ln_bwd=ln_bwd, dep=emit(),
                    name=f"dr1_{l}")


def _local_step(x, p, target, w, ready=lambda group, after: None, emit=lambda group, grads: None):
    grads = {k: [None, None] for k in ("ffn_w_down", "ffn_up_t", "ple_w_gate", "ple_w_proj", "ffn_dw_w",
                                       "ffn_dw_b", "ple_b_gate", "ln_ffn_g", "ln_ffn_b", "ln_mix_g", "ln_mix_b")}

    xb, pb = x.astype(BF16), p.astype(BF16)
    ready("mix", None)
    u = _mm_rows([(xb, w["mix_w_in_t"], True, WHOLE)], name="mix_in")
    ycat, dpool, hconv = _mixer_fwd(u, w["pool_w"], w["pool_scale"], w["conv_dw_w"], w["conv_dw_b"], w["conv_ln_g"],
                                    w["conv_ln_b"])
    ready("mixo", ycat)
    z1, r1, r1b = _proj_ln(x, ycat, w["mix_w_out"], w["ln_mix_g"][0], w["ln_mix_b"][0], name="mix_out_ln")
    sv0, r2, r2b = _ffn_forward(r1, r1b, pb[0], w, 0, ready)

    ready("attn", r2b)
    qkv = _mm_rows([(r2b, w["attn_w_qkv"], False, WHOLE)], out_dtype=BF16, name="attn_qkv")
    bias = _bias_blocks(w["attn_rel_bias"])
    attn = _attn_fwd(qkv, bias)
    z3, r3, r3b = _proj_ln(r2, attn, w["attn_w_o"], w["ln_mix_g"][1], w["ln_mix_b"][1], name="attn_out_ln")
    sv1, _, _ = _ffn_forward(r3, r3b, pb[1], w, 1, ready)

    dz4, dz4b, grads["ln_ffn_g"][1], grads["ln_ffn_b"][1], loss, *ple1 = _loss_ln_bwd(
        sv1["z2"], w["ln_ffn_g"][1], w["ln_ffn_b"][1], target, sv1["gate"], sv1["proj"], name="loss_ln_bwd")
    dz3, dz3b, grads["ln_mix_g"][1], grads["ln_mix_b"][1] = _ffn_backward(
        sv1, dz4, dz4b, ple1, pb[1], w, 1, grads, (z3, w["ln_mix_g"][1]), lambda: emit("ffn1", grads))
    grads["attn_w_o"] = _wgrad(attn, dz3b, name="d_attn_w_o")
    dattn = _mm_rows([(dz3b, w["attn_w_o"], True, WHOLE)], out_dtype=BF16, name="d_attn")
    dq, dk, dv, dbias = _attn_bwd(qkv, bias, dattn)
    grads["attn_rel_bias"] = _bias_blocks_grad(dbias)
    dqkv = jnp.concatenate([dq, dk, dv], axis=1)
    grads["attn_w_qkv"] = _wgrad(r2b, dqkv, tn=768, piece=3 * D_MODEL // N_DEV, name="d_attn_w_qkv")
    dz2, dz2b, grads["ln_ffn_g"][0], grads["ln_ffn_b"][0], *ple0 = _mm_rows(
        [(dqkv, w["attn_w_qkv"], True, WHOLE)], add=dz3, add_scale=ALPHA,
        ln_bwd=(sv0["z2"], w["ln_ffn_g"][0], sv0["gate"], sv0["proj"]), dep=emit("attn", grads), name="dr2")
    dz1, dz1b, grads["ln_mix_g"][0], grads["ln_mix_b"][0] = _ffn_backward(
        sv0, dz2, dz2b, ple0, pb[0], w, 0, grads, (z1, w["ln_mix_g"][0]), lambda: emit("ffn0", grads))
    grads["mix_w_out"] = _wgrad(ycat, dz1b, name="d_mix_w_out")
    dycat = _mm_rows([(dz1b, w["mix_w_out"], True, WHOLE)], name="d_ycat")
    du, g_pw, g_ps, g_cw, g_cb, g_cg, g_cbb = _mixer_bwd(u, dpool, hconv, dycat, w["pool_w"], w["pool_scale"],
                                                         w["conv_dw_w"], w["conv_ln_g"], w["conv_ln_b"])
    grads["mix_w_in_t"] = _wgrad(du, xb, name="d_mix_w_in")
    grads.update(pool_w=g_pw, pool_scale=g_ps[0], conv_dw_w=g_cw, conv_dw_b=g_cb[0], conv_ln_g=g_cg[0],
                 conv_ln_b=g_cbb[0])
    for kname in ("ln_ffn_g", "ln_ffn_b", "ln_mix_g", "ln_mix_b"):
        grads[kname] = [a[0] for a in grads[kname]]
    grad_x = _mm_rows([(du, w["mix_w_in_t"], False, WHOLE)], add=dz1, add_scale=ALPHA, dep=emit("mix", grads),
                      name="grad_x")
    return loss[0, 0], grad_x, grads


_HBM = pl.BlockSpec(memory_space=pltpu.HBM)
_SEM = pl.BlockSpec(memory_space=pltpu.SEMAPHORE)
_EFFECT = pltpu.SideEffectType.DATAFLOW_SIDE_EFFECTING


def _slot(ref, place, shape, k):
    if place in ("stack", "pieces"):
        return ref.at[k]
    ax = place[1]
    n = shape[ax]
    return ref.at[(slice(None),) * ax + (pl.ds(pl.multiple_of(k * n, n), n),)]


def _result_shape(buf, place):
    if place == "stack":
        return (N_DEV,) + buf.shape
    if place == "pieces":
        return buf.shape
    return tuple(s * N_DEV if i == place[1] else s for i, s in enumerate(buf.shape))


def _peers(x, y, c):
    for d in range(1, N_DEV):
        px, py, pc = x ^ ((d >> 2) & 1), y ^ ((d >> 1) & 1), c ^ (d & 1)
        yield d, (px, py, pc), 4 * px + 2 * py + pc


def _exchange_start(bufs, places, after, *, name):
    nb = len(bufs)
    lands = [lax.empty(_result_shape(b, p_), b.dtype) for b, p_ in zip(bufs, places)]
    has_after = after is not None

    def body(*refs):
        srcs, dsts = refs[:nb], refs[nb:2 * nb]
        outs = refs[2 * nb + has_after:]
        send_sems, recv_sems, token = outs[0], outs[1], outs[2 + 2 * nb]
        x, y, c = lax.axis_index("x"), lax.axis_index("y"), lax.axis_index("c")
        me = 4 * x + 2 * y + c
        for b in range(nb):
            for d, dev, peer in _peers(x, y, c):
                pltpu.make_async_remote_copy(
                    src_ref=srcs[b].at[peer] if places[b] == "pieces" else srcs[b],
                    dst_ref=_slot(dsts[b], places[b], bufs[b].shape, me),
                    send_sem=send_sems.at[b * N_DEV + d], recv_sem=recv_sems.at[b * N_DEV + d],
                    device_id=dev, device_id_type=pl.DeviceIdType.MESH).start()
            pltpu.make_async_copy(srcs[b].at[me] if places[b] == "pieces" else srcs[b],
                                  _slot(dsts[b], places[b], bufs[b].shape, me), recv_sems.at[b * N_DEV]).start()
        token[...] = jnp.zeros_like(token)

    sems = pltpu.SemaphoreType.DMA((nb * N_DEV,))
    ins = [pltpu.with_memory_space_constraint(a, pltpu.HBM) for a in list(bufs) + lands]
    out = pl.pallas_call(
        body,
        out_shape=(sems, sems, *[pltpu.HBM(a.shape, a.dtype) for a in ins], jax.ShapeDtypeStruct((8, 128), F32)),
        in_specs=[_HBM] * (2 * nb) + ([pl.BlockSpec(memory_space=pl.ANY)] if has_after else []),
        out_specs=(_SEM, _SEM, *[_HBM] * (2 * nb), pl.BlockSpec(memory_space=pltpu.VMEM)),
        input_output_aliases={i: 2 + i for i in range(2 * nb)},
        compiler_params=pltpu.CompilerParams(has_side_effects=_EFFECT),
        name=name,
    )(*ins, *([after] if has_after else []))
    return dict(send=out[0], recv=out[1], srcs=out[2:2 + nb], lands=out[2 + nb:2 + 2 * nb], token=out[-1],
                places=places)


def _exchange_wait(h, after, *, name):
    nb = len(h["srcs"])
    places = h["places"]
    shapes = [a.shape for a in h["srcs"]]

    def body(*refs):
        srcs, dsts, send_sems, recv_sems = refs[:nb], refs[nb:2 * nb], refs[2 * nb], refs[2 * nb + 1]
        x, y, c = lax.axis_index("x"), lax.axis_index("y"), lax.axis_index("c")
        me = 4 * x + 2 * y + c
        for b in range(nb):
            pieces = places[b] == "pieces"
            for d, dev, peer in _peers(x, y, c):
                cp = pltpu.make_async_remote_copy(
                    src_ref=srcs[b].at[peer] if pieces else srcs[b],
                    dst_ref=_slot(dsts[b], places[b], shapes[b], peer),
                    send_sem=send_sems.at[b * N_DEV + d], recv_sem=recv_sems.at[b * N_DEV + d],
                    device_id=dev, device_id_type=pl.DeviceIdType.MESH)
                cp.wait_send()
                cp.wait_recv()
            pltpu.make_async_copy(srcs[b].at[me] if pieces else srcs[b], _slot(dsts[b], places[b], shapes[b], me),
                                  recv_sems.at[b * N_DEV]).wait()

    ins = list(h["srcs"]) + list(h["lands"])
    out = pl.pallas_call(
        body,
        out_shape=tuple(pltpu.HBM(a.shape, a.dtype) for a in ins),
        in_specs=[_HBM] * (2 * nb) + [_SEM, _SEM, pl.BlockSpec(memory_space=pl.ANY)],
        out_specs=tuple([_HBM] * (2 * nb)),
        input_output_aliases={i: i for i in range(2 * nb)},
        compiler_params=pltpu.CompilerParams(has_side_effects=_EFFECT),
        name=name,
    )(*ins, h["send"], h["recv"], after)
    return out[nb:]


def _adamw(recv, w, m, v, *, layer=0, into=None, name):
    L, R, C = w.shape
    tr = R
    for cand in (512, 256, 128, 64, 32, 16):
        if R % cand == 0 and cand * C * 4 <= 2 * 1024 * 1024:
            tr = cand
            break
    c1 = 1.0 - ADAM_B1 ** ADAM_STEP
    c2 = 1.0 - ADAM_B2 ** ADAM_STEP

    def body(r_ref, w_ref, m_ref, v_ref, *rest):
        g_ref, d_ref, mo_ref, vo_ref = rest[-4:]
        g = r_ref[0].astype(F32)
        for i in range(1, N_DEV):
            g = g + r_ref[i].astype(F32)
        m_new = ADAM_B1 * m_ref[...] + (1.0 - ADAM_B1) * g
        v_new = ADAM_B2 * v_ref[...] + (1.0 - ADAM_B2) * (g * g)
        m_hat = m_new / c1
        v_hat = v_new / c2
        g_ref[...] = g
        d_ref[...] = -ADAM_LR * (m_hat / (jnp.sqrt(v_hat) + ADAM_EPS) + ADAM_WD * w_ref[...])
        mo_ref[...] = m_new
        vo_ref[...] = v_new

    row = pl.BlockSpec((None, tr, C), lambda i: (layer, i, 0))
    others = [] if into is None else list(into)
    return pl.pallas_call(
        body,
        out_shape=[jax.ShapeDtypeStruct((L, R, C), F32)] * 4,
        grid=(R // tr,),
        in_specs=[pl.BlockSpec((N_DEV, tr, C), lambda i: (0, i, 0)), row, row, row]
        + [pl.BlockSpec(memory_space=pl.ANY)] * len(others),
        out_specs=[row] * 4,
        input_output_aliases={4 + k: k for k in range(len(others))},
        compiler_params=_cparams(("parallel",)),
        name=name,
    )(recv, w, m, v, *others)


_TRANSPOSED = ("mix_w_in", "ffn_w_up")


def _ffn_groups(l):
    return ((f"up{l}", (("ffn_w_up", l, BF16, ("axis", 0)), ("ffn_dw_w", l, F32, "stack"))),
            (f"dn{l}", (("ffn_w_down", l, BF16, ("axis", 0)), ("ple_w_gate", l, BF16, ("axis", 0)),
                        ("ple_w_proj", l, BF16, ("axis", 1)))))


_GATHER_GROUPS = (
    ("mix", (("mix_w_in", 0, BF16, ("axis", 0)), ("conv_dw_w", 0, F32, "stack"))),
    ("mixo", (("mix_w_out", 0, BF16, ("axis", 0)),)),
    *_ffn_groups(0),
    ("attn", (("attn_w_qkv", 0, BF16, ("axis", 1)), ("attn_w_o", 0, BF16, ("axis", 0)))),
    *_ffn_groups(1))
_SHARDED = ("mix_w_in", "conv_dw_w", "mix_w_out", "attn_w_qkv", "attn_w_o", "ffn_w_up", "ffn_dw_w", "ffn_w_down",
            "ple_w_gate", "ple_w_proj")
_REPLICATED = ("pool_w", "pool_scale", "conv_dw_b", "conv_ln_g", "conv_ln_b", "attn_rel_bias", "ln_mix_g",
               "ln_mix_b", "ffn_dw_b", "ple_b_gate", "ln_ffn_g", "ln_ffn_b")


def _pack_rows(parts, row_mult, dtype):
    lead = parts[0].shape[:-1]
    flat = jnp.concatenate([a.astype(dtype) for a in parts], axis=-1)
    n = flat.shape[-1]
    unit = row_mult * LANES
    padded = -(-n // unit) * unit
    flat = jnp.pad(flat, [(0, 0)] * len(lead) + [(0, padded - n)])
    return flat.reshape(lead + (padded // LANES, LANES))


def _unpack(flat2d, shapes):
    flat = flat2d.reshape(-1)
    out, o = [], 0
    for s in shapes:
        n = math.prod(s)
        out.append(flat[o:o + n].reshape(s))
        o += n
    return out


def _full_from_shards(g, axis):
    parts = jnp.moveaxis(g, 0, axis)
    shp = list(g.shape[1:])
    shp[axis] *= g.shape[0]
    return parts.reshape(shp)


def _pieces_from_full(full, axis, k=N_DEV):
    shp = list(full.shape)
    n = shp[axis] // k
    t = full.reshape(shp[:axis] + [k, n] + shp[axis + 1:])
    return jnp.moveaxis(t, axis, 0)


def kernel(x, p, mix_w_in, pool_w, pool_scale, conv_dw_w, conv_dw_b, conv_ln_g, conv_ln_b, mix_w_out, attn_w_qkv, attn_rel_bias, attn_w_o, ln_mix_g, ln_mix_b, ffn_w_up, ffn_dw_w, ffn_dw_b, ffn_w_down, ple_w_proj, ple_w_gate, ple_b_gate, ln_ffn_g, ln_ffn_b, loss_target, m_mix_w_in, m_pool_w, m_pool_scale, m_conv_dw_w, m_conv_dw_b, m_conv_ln_g, m_conv_ln_b, m_mix_w_out, m_attn_w_qkv, m_attn_rel_bias, m_attn_w_o, m_ln_mix_g, m_ln_mix_b, m_ffn_w_up, m_ffn_dw_w, m_ffn_dw_b, m_ffn_w_down, m_ple_w_proj, m_ple_w_gate, m_ple_b_gate, m_ln_ffn_g, m_ln_ffn_b, v_mix_w_in, v_pool_w, v_pool_scale, v_conv_dw_w, v_conv_dw_b, v_conv_ln_g, v_conv_ln_b, v_mix_w_out, v_attn_w_qkv, v_attn_rel_bias, v_attn_w_o, v_ln_mix_g, v_ln_mix_b, v_ffn_w_up, v_ffn_dw_w, v_ffn_dw_b, v_ffn_w_down, v_ple_w_proj, v_ple_w_gate, v_ple_b_gate, v_ln_ffn_g, v_ln_ffn_b):
    a = dict(locals())
    sh_names = list(_SHARDED)
    names = sh_names + list(_REPLICATED)
    wts = {n: a[n] for n in names}
    mom = {n: a["m_" + n] for n in names}
    var = {n: a["v_" + n] for n in names}

    for n in _TRANSPOSED:
        wts[n], mom[n], var[n] = (jnp.swapaxes(d[n], 1, 2) for d in (wts, mom, var))
    gather = {}
    token = None
    for group, items in _GATHER_GROUPS:
        gather[group] = _exchange_start([wts[n][l].astype(dt) for n, l, dt, _ in items], [pl_ for *_, pl_ in items],
                                        token, name="gather_start_" + group)
        token = gather[group]["token"]

    w = dict(pool_w=pool_w[0], pool_scale=pool_scale[0], conv_dw_b=conv_dw_b[0], conv_ln_g=conv_ln_g[0],
             conv_ln_b=conv_ln_b[0], attn_rel_bias=attn_rel_bias[0], ln_mix_g=ln_mix_g, ln_mix_b=ln_mix_b,
             ffn_dw_b=ffn_dw_b, ple_b_gate=ple_b_gate, ln_ffn_g=ln_ffn_g, ln_ffn_b=ln_ffn_b)
    for n in ("ffn_up_t", "ffn_dw_w", "ffn_w_down", "ple_w_gate", "ple_w_proj"):
        w[n] = [None, None]

    def ready(group, after):
        got = _exchange_wait(gather[group], token if after is None else after, name="gather_wait_" + group)
        if group == "mix":
            w["mix_w_in_t"], w["conv_dw_w"] = got[0], _full_from_shards(got[1], 1)
        elif group == "mixo":
            (w["mix_w_out"],) = got
        elif group == "attn":
            w["attn_w_qkv"], w["attn_w_o"] = got
        elif group[:2] == "up":
            l = int(group[2])
            w["ffn_up_t"][l], w["ffn_dw_w"][l] = got[0], _full_from_shards(got[1], 1)
        else:
            l = int(group[2])
            w["ffn_w_down"][l], w["ple_w_gate"][l], w["ple_w_proj"][l] = got

    scatter = {}

    def emit(group, gr):
        if group[:3] == "ffn":
            l = int(group[3])
            pieces = [_pieces_from_full(gr["ffn_up_t"][l], 0),
                      _pieces_from_full(gr["ffn_dw_w"][l], 1), _pieces_from_full(gr["ffn_w_down"][l], 0),
                      _pieces_from_full(gr["ple_w_gate"][l], 0), gr["ple_w_proj"][l]]
        elif group == "attn":
            pieces = [gr["attn_w_qkv"], _pieces_from_full(gr["attn_w_o"], 0)]
        else:
            pieces = [_pieces_from_full(gr["mix_w_in_t"], 0), _pieces_from_full(gr["conv_dw_w"], 1),
                      _pieces_from_full(gr["mix_w_out"], 0)]
        scatter[group] = _exchange_start([a.astype(BF16) for a in pieces], ["pieces"] * len(pieces), None,
                                         name="grad_start_" + group)
        if group != "mix":
            return scatter[group]["token"]
        gfull = dict(
            pool_w=gr["pool_w"][None], pool_scale=gr["pool_scale"][None], conv_dw_b=gr["conv_dw_b"][None],
            conv_ln_g=gr["conv_ln_g"][None], conv_ln_b=gr["conv_ln_b"][None],
            attn_rel_bias=gr["attn_rel_bias"][None], ln_mix_g=jnp.stack(gr["ln_mix_g"]),
            ln_mix_b=jnp.stack(gr["ln_mix_b"]), ffn_dw_b=jnp.stack(gr["ffn_dw_b"]),
            ple_b_gate=jnp.stack(gr["ple_b_gate"]), ln_ffn_g=jnp.stack(gr["ln_ffn_g"]),
            ln_ffn_b=jnp.stack(gr["ln_ffn_b"]))
        rep_send = _pack_rows([gfull[n].reshape(-1) for n in _REPLICATED], 8, F32)
        scatter["replicated"] = _exchange_start([rep_send], ["stack"], scatter[group]["token"],
                                                name="grad_start_replicated")
        return scatter["replicated"]["token"]

    loss_part, grad_x, gr = _local_step(x[0], p[:, 0], loss_target[0], w, ready, emit)
    loss = lax.psum(loss_part, ("x", "y", "c"))

    group_weights = {"ffn1": (("ffn_w_up", 1), ("ffn_dw_w", 1), ("ffn_w_down", 1), ("ple_w_gate", 1), ("ple_w_proj", 1)),
                     "attn": (("attn_w_qkv", 0), ("attn_w_o", 0)),
                     "ffn0": (("ffn_w_up", 0), ("ffn_dw_w", 0), ("ffn_w_down", 0), ("ple_w_gate", 0), ("ple_w_proj", 0)),
                     "mix": (("mix_w_in", 0), ("conv_dw_w", 0), ("mix_w_out", 0))}
    updated = {}
    after = grad_x
    for group in ("ffn1", "attn", "ffn0", "mix"):
        recv = _exchange_wait(scatter[group], after, name="grad_wait_" + group)
        for (n, l), r in zip(group_weights[group], recv):
            updated[n] = _adamw(r, wts[n], mom[n], var[n], layer=l, into=updated.get(n), name=f"adamw_{n}{l}")
            after = updated[n][0]
    res = [{n: jnp.swapaxes(updated[n][k], 1, 2) if n in _TRANSPOSED else updated[n][k] for n in sh_names}
           for k in range(4)]
    (rep_recv,) = _exchange_wait(scatter["replicated"], after, name="grad_wait_replicated")

    def flat_state(d):
        return _pack_rows([d[n].reshape(-1) for n in _REPLICATED], 8, F32)[None]

    rep_out = _adamw(rep_recv, flat_state(wts), flat_state(mom), flat_state(var), name="adamw_replicated")
    for k in range(4):
        for n, arr in zip(_REPLICATED, _unpack(rep_out[k][0], [wts[n].shape for n in _REPLICATED])):
            res[k][n] = arr
    order = ["mix_w_in", "pool_w", "pool_scale", "conv_dw_w", "conv_dw_b", "conv_ln_g", "conv_ln_b", "mix_w_out",
             "attn_w_qkv", "attn_rel_bias", "attn_w_o", "ln_mix_g", "ln_mix_b", "ffn_w_up", "ffn_dw_w", "ffn_dw_b",
             "ffn_w_down", "ple_w_proj", "ple_w_gate", "ple_b_gate", "ln_ffn_g", "ln_ffn_b"]
    outs = [loss, grad_x[None]]
    for k in range(4):
        outs += [res[k][n] for n in order]
    return tuple(outs)
```

```python
import functools
import math

import jax
import jax.numpy as jnp
from jax import lax
from jax.experimental import pallas as pl
from jax.experimental.pallas import tpu as pltpu

F32 = jnp.float32
BF16 = jnp.bfloat16

N_DEV = 8
D_MODEL = 1024
D_POOL = 512
D_CONV = 512
POOL_WINDOWS = (2, 4, 8, 16)
POOL_GROUP = 128
CONV_KERNEL = 31
CHUNK = 64
HEAD_DIM = 64
N_HEADS = 16
LEFT_CHUNKS = 8
BAND = (LEFT_CHUNKS + 1) * CHUNK
MAX_REL = 256
D_FF = 2816
PLE_DIM = 256
ALPHA = 4.0 ** 0.25
LN_EPS = 1e-5
NEG_INF = -1e30
ADAM_LR, ADAM_B1, ADAM_B2, ADAM_EPS, ADAM_WD, ADAM_STEP = 0.001, 0.9, 0.999, 1e-08, 0.01, 10

Q_BLOCK = 4 * CHUNK
KV_PAD = LEFT_CHUNKS * CHUNK
KV_SPAN = KV_PAD + Q_BLOCK
CONV_HALO = 32
FFN_HALO = 16
SUB_ROWS, SUB_LANES = 64, 128
LANES = 1024
VMEM_LIMIT = 56 * 1024 * 1024


def _cparams(sem=None):
    return pltpu.CompilerParams(dimension_semantics=sem, vmem_limit_bytes=VMEM_LIMIT)


def _tile(dim, pref):
    if dim <= pref:
        return dim
    t = pref - pref % 128
    while t >= 128:
        if dim % t == 0:
            return t
        t -= 128
    return dim


def _sigmoid(x):
    return 1.0 / (1.0 + jnp.exp(-x))


def _bdot(a, b, dn=(((1,), (0,)), ((), ()))):
    return lax.dot_general(a.astype(BF16), b.astype(BF16), dn, preferred_element_type=F32)


WHOLE = (0, 1)
NT = (((1,), (1,)), ((), ()))
TN = (((0,), (0,)), ((), ()))


def _wgrad(a, b, *, tm=1024, tn=1024, tk=2048, piece=None, part=(0, 1), into=None, name):
    K, M = a.shape
    kb, N = b.shape
    assert K == kb, (a.shape, b.shape)
    tm, tn, tk = _tile(M, tm), _tile(N, tn), _tile(K, tk)
    nk = K // tk
    per = 1 if piece is None else tn // piece
    assert piece is None or tn == per * piece

    def body(a_ref, b_ref, *rest):
        o_ref, acc = rest[-2:]
        k = pl.program_id(2)

        @pl.when(k == 0)
        def _():
            acc[...] = jnp.zeros_like(acc)

        acc[...] += _bdot(a_ref[...], b_ref[...], TN)

        @pl.when(k == nk - 1)
        def _():
            if piece is None:
                o_ref[...] = acc[...].astype(BF16)
            else:
                for s in range(per):
                    o_ref[s] = acc[:, s * piece:(s + 1) * piece].astype(BF16)

    if piece is None:
        first = part[0] * (M // tm)
        out_shape = (part[1] * M, N)
        out_spec = pl.BlockSpec((tm, tn), lambda i, j, k: (first + i, j))
    else:
        out_shape, out_spec = (N // piece, M, piece), pl.BlockSpec((per, tm, piece), lambda i, j, k: (j, i, 0))
    others = [] if into is None else [into]
    return pl.pallas_call(
        body,
        out_shape=jax.ShapeDtypeStruct(out_shape, BF16),
        grid=(M // tm, N // tn, nk),
        in_specs=[pl.BlockSpec((tk, tm), lambda i, j, k: (k, i)), pl.BlockSpec((tk, tn), lambda i, j, k: (k, j))]
        + [pl.BlockSpec(memory_space=pl.ANY)] * len(others),
        out_specs=out_spec,
        input_output_aliases={2: 0} if others else {},
        scratch_shapes=[pltpu.VMEM((tm, tn), F32)],
        compiler_params=_cparams(("parallel", "parallel", "arbitrary")),
        name=name,
    )(a, b, *others)


def _mm_rows(pairs, *, add=None, add_scale=1.0, out_dtype=F32, tm=512, dep=None, ln_bwd=None, name):
    M = pairs[0][0].shape[0]
    n = len(pairs)
    has_add = add is not None
    has_ple = ln_bwd is not None and len(ln_bwd) == 4
    w_rows = [w_.shape[0] // part[1] for _, w_, _, part in pairs]
    N = w_rows[0] if pairs[0][2] else pairs[0][1].shape[1]

    def body(*refs):
        acc = None
        for i, (_, _, tr, _) in enumerate(pairs):
            part = _bdot(refs[2 * i][...], refs[2 * i + 1][...], NT if tr else (((1,), (0,)), ((), ())))
            acc = part if acc is None else acc + part
        if has_add:
            acc = acc + add_scale * refs[2 * n][...]
        if ln_bwd is None:
            refs[-1][...] = acc.astype(out_dtype)
            return
        first = 2 * n + has_add
        z_ref, g_ref = refs[first], refs[first + 1]
        outs = refs[-(7 if has_ple else 4):]
        dz_ref, dzb_ref, dg_ref, db_ref = outs[:4]

        @pl.when(pl.program_id(0) == 0)
        def _():
            for sums in outs[2:4] + outs[6:]:
                sums[...] = jnp.zeros_like(sums)

        dg_acc = jnp.zeros((8, N), F32)
        db_acc = jnp.zeros((8, N), F32)
        dbg_acc = jnp.zeros((8, N), F32)
        for r0 in range(0, tm, LN_ROWS):
            rows = pl.ds(r0, LN_ROWS)
            do = acc[r0:r0 + LN_ROWS]
            dz, xh = _ln_bwd_rows(z_ref[rows, :], g_ref[...], do)
            dz_ref[rows, :] = dz
            dzb_ref[rows, :] = dz.astype(BF16)
            dg_acc = dg_acc + jnp.sum((do * xh).reshape(LN_ROWS // 8, 8, N), axis=0)
            db_acc = db_acc + jnp.sum(do.reshape(LN_ROWS // 8, 8, N), axis=0)
            if has_ple:
                ds, dp = _ple_bwd_rows(dz, refs[first + 2][rows, :], refs[first + 3][rows, :])
                outs[4][rows, :] = ds.astype(BF16)
                outs[5][rows, :] = dp.astype(BF16)
                dbg_acc = dbg_acc + jnp.sum(ds.reshape(LN_ROWS // 8, 8, N), axis=0)
        dg_ref[...] += jnp.sum(dg_acc, axis=0, keepdims=True)
        db_ref[...] += jnp.sum(db_acc, axis=0, keepdims=True)
        if has_ple:
            outs[6][...] += jnp.sum(dbg_acc, axis=0, keepdims=True)

    in_specs, args = [], []
    for (a, w_, _, part), rows in zip(pairs, w_rows):
        in_specs += [pl.BlockSpec((tm, a.shape[1]), lambda i: (i, 0)),
                     pl.BlockSpec((rows, w_.shape[1]), functools.partial(lambda i, j: (j, 0), j=part[0]))]
        args += [a, w_]
    row = pl.BlockSpec((tm, N), lambda i: (i, 0))
    fix = pl.BlockSpec((1, N), lambda i: (0, 0))
    if has_add:
        in_specs.append(row)
        args.append(add)
    if ln_bwd is not None:
        in_specs += [row, fix] + [row] * (len(ln_bwd) - 2)
        args += [ln_bwd[0], ln_bwd[1].reshape(1, N), *ln_bwd[2:]]
    if dep is not None:
        in_specs.append(pl.BlockSpec(memory_space=pl.ANY))
        args.append(dep)
    if ln_bwd is None:
        out_shape, out_specs = jax.ShapeDtypeStruct((M, N), out_dtype), row
    else:
        out_shape = [jax.ShapeDtypeStruct((M, N), F32), jax.ShapeDtypeStruct((M, N), BF16),
                     jax.ShapeDtypeStruct((1, N), F32), jax.ShapeDtypeStruct((1, N), F32)]
        out_specs = [row, row, fix, fix]
        if has_ple:
            out_shape += [jax.ShapeDtypeStruct((M, N), BF16), jax.ShapeDtypeStruct((M, N), BF16),
                          jax.ShapeDtypeStruct((1, N), F32)]
            out_specs += [row, row, fix]
    return pl.pallas_call(
        body,
        out_shape=out_shape,
        grid=(M // tm,),
        in_specs=in_specs,
        out_specs=out_specs,
        compiler_params=_cparams(("parallel",) if ln_bwd is None else ("arbitrary",)),
        name=name,
    )(*args)


def _ln_bwd_rows(zt, g, do):
    zc = zt - jnp.mean(zt, axis=-1, keepdims=True)
    rstd = lax.rsqrt(jnp.mean(zc * zc, axis=-1, keepdims=True) + LN_EPS)
    xh = zc * rstd
    dxh = do * g
    return rstd * (dxh - jnp.mean(dxh, axis=-1, keepdims=True) - xh * jnp.mean(dxh * xh, axis=-1, keepdims=True)), xh


def _layer_norm_rows(z, g, b):
    mu = jnp.mean(z, axis=-1, keepdims=True)
    zc = z - mu
    var = jnp.mean(zc * zc, axis=-1, keepdims=True)
    return zc * lax.rsqrt(var + LN_EPS) * g + b


def _proj_ln(res, a, w, ln_g, ln_b, *, ple=None, ts=512, name):
    S, D = res.shape
    ka = a.shape[1]
    has_ple = ple is not None
    row = lambda i: (i, 0)
    fix = lambda i: (0, 0)

    def body(*refs):
        if has_ple:
            (res_ref, a_ref, w_ref, g_ref, b_ref, wg_ref, bg_ref, p_ref, wp_ref, z_ref, r_ref, rb_ref, gate_ref,
             proj_ref, acc) = refs
        else:
            res_ref, a_ref, w_ref, g_ref, b_ref, z_ref, r_ref, rb_ref, acc = refs
        acc[...] = _bdot(a_ref[...], w_ref[...])
        if has_ple:
            gate_ref[...] = _bdot(res_ref[...], wg_ref[...])
            proj_ref[...] = _bdot(p_ref[...], wp_ref[...])
        for r0 in range(0, ts, LN_ROWS):
            rows = pl.ds(r0, LN_ROWS)
            z = ALPHA * res_ref[rows, :] + acc[rows, :]
            if has_ple:
                gate = _sigmoid(gate_ref[rows, :] + bg_ref[...])
                gate_ref[rows, :] = gate
                z = z + gate * proj_ref[rows, :]
            z_ref[rows, :] = z
            r = _layer_norm_rows(z, g_ref[...], b_ref[...])
            r_ref[rows, :] = r
            rb_ref[rows, :] = r.astype(BF16)

    in_specs = [pl.BlockSpec((ts, D), row), pl.BlockSpec((ts, ka), row), pl.BlockSpec((ka, D), fix),
                pl.BlockSpec((1, D), fix), pl.BlockSpec((1, D), fix)]
    args = [res, a, w, ln_g.reshape(1, D), ln_b.reshape(1, D)]
    out_dtypes = [F32, F32, BF16]
    if has_ple:
        wg, bg, p, wp = ple
        in_specs += [pl.BlockSpec((D, D), fix), pl.BlockSpec((1, D), fix), pl.BlockSpec((ts, PLE_DIM), row),
                     pl.BlockSpec((PLE_DIM, D), fix)]
        args += [wg, bg.reshape(1, D), p, wp]
        out_dtypes += [F32, F32]
    return pl.pallas_call(
        body,
        out_shape=[jax.ShapeDtypeStruct((S, D), dt) for dt in out_dtypes],
        grid=(S // ts,),
        in_specs=in_specs,
        out_specs=[pl.BlockSpec((ts, D), row)] * len(out_dtypes),
        scratch_shapes=[pltpu.VMEM((ts, D), F32)],
        compiler_params=_cparams(("parallel",)),
        name=name,
    )(*args)


CONV_ROWS = 32
LN_ROWS = 16


def _shifted_copies(src, dst, rows):
    for c0 in range(0, src.shape[1], SUB_LANES):
        ln = pl.ds(c0, SUB_LANES)
        for r0 in range(0, rows, SUB_ROWS):
            rc = min(SUB_ROWS, rows - r0)
            for b, shifted in enumerate(_rows_ahead(src, r0, rc, ln, range(1, 8))):
                dst[b, pl.ds(r0, rc), ln] = shifted


def _rows_at(src, copies, off, n, ln):
    b = off % 8
    return src[pl.ds(off, n), ln] if b == 0 else copies[b - 1, pl.ds(off - b, n), ln]


def _conv31(stg, gsh, cw_ref, cb_ref, out, rows, first_off):
    for c0 in range(0, D_CONV, SUB_LANES):
        ln = pl.ds(c0, SUB_LANES)
        for r0 in range(0, rows, CONV_ROWS):
            acc = jnp.zeros((CONV_ROWS, SUB_LANES), F32) + cb_ref[:, ln]
            for k in range(CONV_KERNEL):
                acc = acc + cw_ref[k:k + 1, ln] * _rows_at(stg, gsh, first_off + k + r0, CONV_ROWS, ln)
            out[pl.ds(r0, CONV_ROWS), ln] = acc


def _mixer_fwd(u, pool_w, pool_scale, conv_w, conv_b, cln_g, cln_b, *, ts=256):
    S = u.shape[0]
    hb = CONV_HALO
    nh = ts // hb

    def body(u_ref, uh_ref, pw_ref, ps_ref, cw_ref, cb_ref, g_ref, b_ref, y_ref, d_ref, hcs, sta, stg, gsh):
        i = pl.program_id(0)
        first = i == 0
        sta[pl.ds(0, hb), :] = jnp.where(first, 0.0, uh_ref[:, 0:D_POOL])
        sta[pl.ds(hb, ts), :] = u_ref[:, 0:D_POOL]
        glu_h = uh_ref[:, D_POOL:D_POOL + D_CONV] * _sigmoid(uh_ref[:, D_POOL + D_CONV:])
        stg[pl.ds(0, hb), :] = jnp.where(first, 0.0, glu_h)
        stg[pl.ds(hb, ts), :] = u_ref[:, D_POOL:D_POOL + D_CONV] * _sigmoid(u_ref[:, D_POOL + D_CONV:])

        for g, w in enumerate(POOL_WINDOWS):
            lanes = pl.ds(g * POOL_GROUP, POOL_GROUP)
            for r0 in range(0, ts, SUB_ROWS):
                s = None
                for q in range(0, w, 8):
                    for tap in _rows_back(sta, hb + r0 - q, SUB_ROWS, lanes, range(min(8, w - q))):
                        s = tap if s is None else s + tap
                pos = (i * ts + r0 + lax.broadcasted_iota(jnp.int32, (SUB_ROWS, 1), 0) + 1).astype(F32)
                d_g = s / jnp.minimum(pos, float(w)) - sta[pl.ds(hb + r0, SUB_ROWS), lanes]
                d_ref[pl.ds(r0, SUB_ROWS), lanes] = d_g.astype(BF16)
            y_ref[:, lanes] = (_bdot(d_ref[:, lanes], pw_ref[g]) * ps_ref[:, lanes]).astype(BF16)

        _shifted_copies(stg, gsh, hb + ts - 8)
        _conv31(stg, gsh, cw_ref, cb_ref, hcs, ts, hb - (CONV_KERNEL - 1))
        for r0 in range(0, ts, LN_ROWS):
            rows = pl.ds(r0, LN_ROWS)
            ln = _layer_norm_rows(hcs[rows, :], g_ref[...], b_ref[...])
            y_ref[rows, D_POOL:] = (ln * _sigmoid(ln)).astype(BF16)

    fix2 = lambda i: (0, 0)
    return pl.pallas_call(
        body,
        out_shape=[jax.ShapeDtypeStruct((S, D_MODEL), BF16), jax.ShapeDtypeStruct((S, D_POOL), BF16),
                   jax.ShapeDtypeStruct((S, D_CONV), F32)],
        grid=(S // ts,),
        in_specs=[pl.BlockSpec((ts, 3 * D_POOL), lambda i: (i, 0)),
                  pl.BlockSpec((hb, 3 * D_POOL), lambda i: (jnp.maximum(i * nh - 1, 0), 0)),
                  pl.BlockSpec((4, POOL_GROUP, POOL_GROUP), lambda i: (0, 0, 0)),
                  pl.BlockSpec((1, D_POOL), fix2), pl.BlockSpec((CONV_KERNEL, D_CONV), fix2),
                  pl.BlockSpec((1, D_CONV), fix2), pl.BlockSpec((1, D_CONV), fix2), pl.BlockSpec((1, D_CONV), fix2)],
        out_specs=[pl.BlockSpec((ts, D_MODEL), lambda i: (i, 0)), pl.BlockSpec((ts, D_POOL), lambda i: (i, 0)),
                   pl.BlockSpec((ts, D_CONV), lambda i: (i, 0))],
        scratch_shapes=[pltpu.VMEM((hb + ts, D_POOL), F32), pltpu.VMEM((hb + ts, D_CONV), F32),
                        pltpu.VMEM((7, hb + ts - 8, D_CONV), F32)],
        compiler_params=_cparams(("parallel",)),
        name="mixer_fwd",
    )(u, u, pool_w, pool_scale.reshape(1, D_POOL), conv_w, conv_b.reshape(1, D_CONV), cln_g.reshape(1, D_CONV),
      cln_b.reshape(1, D_CONV))


def _mixer_bwd(u, d, hc, dycat, pool_w, pool_scale, conv_w, cln_g, cln_b, *, ts=256):
    S = u.shape[0]
    hb = CONV_HALO
    nh = ts // hb
    n = S // ts
    te = ts + hb
    K = CONV_KERNEL

    def body(u_ref, up_ref, un_ref, d_ref, hc_ref, hcn_ref, dy_ref, dyn_ref, pw_ref, ps_ref, cw_ref, g_ref, b_ref,
             du_ref, dpw_ref, dps_ref, dcw_ref, dcb_ref, dg_ref, db_ref, stg, std, sth, gsh, hsh):
        i = pl.program_id(0)
        first = i == 0
        last = i == n - 1

        @pl.when(first)
        def _():
            dpw_ref[...] = jnp.zeros_like(dpw_ref)
            dps_ref[...] = jnp.zeros_like(dps_ref)
            dcw_ref[...] = jnp.zeros_like(dcw_ref)
            dcb_ref[...] = jnp.zeros_like(dcb_ref)
            dg_ref[...] = jnp.zeros_like(dg_ref)
            db_ref[...] = jnp.zeros_like(db_ref)

        pos_e = (i * ts + lax.broadcasted_iota(jnp.int32, (te, 1), 0) + 1).astype(F32)
        dya = dy_ref[:, 0:D_POOL]
        dya_n = jnp.where(last, 0.0, dyn_ref[:, 0:D_POOL])
        for g, w in enumerate(POOL_WINDOWS):
            lanes = pl.ds(g * POOL_GROUP, POOL_GROUP)
            sl = slice(g * POOL_GROUP, (g + 1) * POOL_GROUP)
            pw = pw_ref[g]
            scale = ps_ref[:, lanes]
            d_g = d_ref[:, lanes]
            pre = _bdot(d_g, pw)
            dps_ref[:, lanes] += jnp.sum(dya[:, sl] * pre, axis=0, keepdims=True)
            dys = dya[:, sl] * scale
            dpw_ref[g] += _bdot(d_g, dys, TN)
            dys_e = jnp.concatenate([dys, dya_n[:, sl] * scale], axis=0)
            dd = _bdot(dys_e, pw, NT)
            std[:, lanes] = dd / jnp.minimum(pos_e, float(w))
            for r0 in range(0, ts, SUB_ROWS):
                da = -dd[r0:r0 + SUB_ROWS]
                for q in range(0, w, 8):
                    for tap in _rows_ahead(std, r0 + q, SUB_ROWS, lanes, range(min(8, w - q))):
                        da = da + tap
                du_ref[pl.ds(r0, SUB_ROWS), lanes] = da.astype(BF16)

        glu_p = up_ref[:, D_POOL:D_POOL + D_CONV] * _sigmoid(up_ref[:, D_POOL + D_CONV:])
        stg[pl.ds(0, hb), :] = jnp.where(first, 0.0, glu_p)
        bv = u_ref[:, D_POOL:D_POOL + D_CONV]
        sg = _sigmoid(u_ref[:, D_POOL + D_CONV:])
        stg[pl.ds(hb, ts), :] = bv * sg
        glu_n = un_ref[:, D_POOL:D_POOL + D_CONV] * _sigmoid(un_ref[:, D_POOL + D_CONV:])
        stg[pl.ds(hb + ts, hb), :] = jnp.where(last, 0.0, glu_n)
        _shifted_copies(stg, gsh, hb + te - 8)

        sums = [jnp.zeros((8, D_CONV), F32) for _ in range(3)]
        for r0 in range(0, te, LN_ROWS):
            rows = pl.ds(r0, LN_ROWS)
            hc = hc_ref[rows, :] if r0 < ts else hcn_ref[pl.ds(r0 - ts, LN_ROWS), :]
            hcc = hc - jnp.mean(hc, axis=-1, keepdims=True)
            rstd = lax.rsqrt(jnp.mean(hcc * hcc, axis=-1, keepdims=True) + LN_EPS)
            xh = hcc * rstd
            ln = xh * g_ref[...] + b_ref[...]
            sl_ = _sigmoid(ln)
            if r0 < ts:
                dyb = dy_ref[rows, D_POOL:]
            else:
                dyb = jnp.where(last, 0.0, dyn_ref[pl.ds(r0 - ts, LN_ROWS), D_POOL:])
            dln = dyb * (sl_ * (1.0 + ln * (1.0 - sl_)))
            dxh = dln * g_ref[...]
            dhc = rstd * (dxh - jnp.mean(dxh, axis=-1, keepdims=True)
                          - xh * jnp.mean(dxh * xh, axis=-1, keepdims=True))
            sth[rows, :] = dhc
            if r0 < ts:
                for n_, term in enumerate((dln * xh, dln, dhc)):
                    sums[n_] = sums[n_] + jnp.sum(term.reshape(LN_ROWS // 8, 8, D_CONV), axis=0)
        dg_ref[...] += jnp.sum(sums[0], axis=0, keepdims=True)
        db_ref[...] += jnp.sum(sums[1], axis=0, keepdims=True)
        dcb_ref[...] += jnp.sum(sums[2], axis=0, keepdims=True)

        _shifted_copies(sth, hsh, te - 8)
        for c0 in range(0, D_CONV, SUB_LANES):
            ln_ = pl.ds(c0, SUB_LANES)
            for r0 in range(0, ts, CONV_ROWS):
                rows = pl.ds(r0, CONV_ROWS)
                dglu = jnp.zeros((CONV_ROWS, SUB_LANES), F32)
                for k in range(K):
                    dglu = dglu + cw_ref[k:k + 1, ln_] * _rows_at(sth, hsh, K - 1 - k + r0, CONV_ROWS, ln_)
                bv = u_ref[rows, pl.ds(D_POOL + c0, SUB_LANES)]
                sg = _sigmoid(u_ref[rows, pl.ds(D_POOL + D_CONV + c0, SUB_LANES)])
                du_ref[rows, pl.ds(D_POOL + c0, SUB_LANES)] = (dglu * sg).astype(BF16)
                du_ref[rows, pl.ds(D_POOL + D_CONV + c0, SUB_LANES)] = (dglu * bv * sg * (1.0 - sg)).astype(BF16)
            for k in range(K):
                tap = jnp.zeros((8, SUB_LANES), F32)
                for r0 in range(0, ts, CONV_ROWS):
                    prod = sth[pl.ds(r0, CONV_ROWS), ln_] * _rows_at(stg, gsh, hb - (K - 1) + k + r0, CONV_ROWS, ln_)
                    tap = tap + jnp.sum(prod.reshape(CONV_ROWS // 8, 8, SUB_LANES), axis=0)
                dcw_ref[k:k + 1, ln_] += jnp.sum(tap, axis=0, keepdims=True)

    fix2 = lambda i: (0, 0)
    prev = lambda i: (jnp.maximum(i * nh - 1, 0), 0)
    nxt = lambda i: (jnp.minimum((i + 1) * nh, S // hb - 1), 0)
    return pl.pallas_call(
        body,
        out_shape=[jax.ShapeDtypeStruct((S, 3 * D_POOL), BF16),
                   jax.ShapeDtypeStruct((4, POOL_GROUP, POOL_GROUP), F32),
                   jax.ShapeDtypeStruct((1, D_POOL), F32),
                   jax.ShapeDtypeStruct((K, D_CONV), F32),
                   jax.ShapeDtypeStruct((1, D_CONV), F32),
                   jax.ShapeDtypeStruct((1, D_CONV), F32),
                   jax.ShapeDtypeStruct((1, D_CONV), F32)],
        grid=(n,),
        in_specs=[pl.BlockSpec((ts, 3 * D_POOL), lambda i: (i, 0)),
                  pl.BlockSpec((hb, 3 * D_POOL), prev),
                  pl.BlockSpec((hb, 3 * D_POOL), nxt),
                  pl.BlockSpec((ts, D_POOL), lambda i: (i, 0)),
                  pl.BlockSpec((ts, D_CONV), lambda i: (i, 0)),
                  pl.BlockSpec((hb, D_CONV), nxt),
                  pl.BlockSpec((ts, D_MODEL), lambda i: (i, 0)),
                  pl.BlockSpec((hb, D_MODEL), nxt),
                  pl.BlockSpec((4, POOL_GROUP, POOL_GROUP), lambda i: (0, 0, 0)),
                  pl.BlockSpec((1, D_POOL), fix2), pl.BlockSpec((K, D_CONV), fix2),
                  pl.BlockSpec((1, D_CONV), fix2), pl.BlockSpec((1, D_CONV), fix2)],
        out_specs=[pl.BlockSpec((ts, 3 * D_POOL), lambda i: (i, 0)),
                   pl.BlockSpec((4, POOL_GROUP, POOL_GROUP), lambda i: (0, 0, 0)),
                   pl.BlockSpec((1, D_POOL), fix2), pl.BlockSpec((K, D_CONV), fix2),
                   pl.BlockSpec((1, D_CONV), fix2), pl.BlockSpec((1, D_CONV), fix2), pl.BlockSpec((1, D_CONV), fix2)],
        scratch_shapes=[pltpu.VMEM((hb + ts + hb, D_CONV), F32), pltpu.VMEM((te, D_POOL), F32),
                        pltpu.VMEM((te, D_CONV), F32), pltpu.VMEM((7, hb + te - 8, D_CONV), F32),
                        pltpu.VMEM((7, te - 8, D_CONV), F32)],
        compiler_params=_cparams(("arbitrary",)),
        name="mixer_bwd",
    )(u, u, u, d, hc, hc, dycat, dycat, pool_w, pool_scale.reshape(1, D_POOL), conv_w, cln_g.reshape(1, D_CONV),
      cln_b.reshape(1, D_CONV))


_GELU_C = math.sqrt(2.0 / math.pi)


def _gelu_parts(x):
    inner = _GELU_C * (x + 0.044715 * x * x * x)
    th = jnp.tanh(inner)
    ge = 0.5 * x * (1.0 + th)
    dge = 0.5 * (1.0 + th) + 0.5 * x * (1.0 - th * th) * (_GELU_C * (1.0 + 3.0 * 0.044715 * x * x))
    return ge, dge


def _rows_back(ref, r, n, ln, shifts):
    ext = ref[pl.ds(r - 8, n + 8), ln]
    return [(pltpu.roll(ext, s, 0) if s else ext)[8:] for s in shifts]


def _rows_ahead(ref, r, n, ln, shifts):
    ext = ref[pl.ds(r, n + 8), ln]
    return [(pltpu.roll(ext, n + 8 - s, 0) if s else ext)[:n] for s in shifts]


def _ffn_act_fwd(gate, val, dw_w, dw_b, *, ts=512, tc=1408, name):
    S, F = gate.shape
    hb = FFN_HALO
    nh = ts // hb
    tc = _tile(F, tc)

    def body(g_ref, gh_ref, v_ref, w_ref, b_ref, h_ref, st):
        i = pl.program_id(0)
        st[pl.ds(0, hb), :] = jnp.where(i == 0, 0.0, gh_ref[...].astype(F32))
        st[pl.ds(hb, ts), :] = g_ref[...].astype(F32)
        for c0 in range(0, tc, SUB_LANES):
            ln = pl.ds(c0, SUB_LANES)
            w0, w1, w2, b = w_ref[0:1, ln], w_ref[1:2, ln], w_ref[2:3, ln], b_ref[:, ln]
            for r0 in range(0, ts, SUB_ROWS):
                taps = _rows_back(st, hb + r0, SUB_ROWS, ln, (2, 1, 0))
                gc = b + w0 * taps[0] + w1 * taps[1] + w2 * taps[2]
                ge, _ = _gelu_parts(gc)
                rows = pl.ds(r0, SUB_ROWS)
                h_ref[rows, ln] = (ge * v_ref[rows, ln].astype(F32)).astype(BF16)

    return pl.pallas_call(
        body,
        out_shape=jax.ShapeDtypeStruct((S, F), BF16),
        grid=(S // ts, F // tc),
        in_specs=[pl.BlockSpec((ts, tc), lambda i, j: (i, j)),
                  pl.BlockSpec((hb, tc), lambda i, j: (jnp.maximum(i * nh - 1, 0), j)),
                  pl.BlockSpec((ts, tc), lambda i, j: (i, j)),
                  pl.BlockSpec((3, tc), lambda i, j: (0, j)),
                  pl.BlockSpec((1, tc), lambda i, j: (0, j))],
        out_specs=pl.BlockSpec((ts, tc), lambda i, j: (i, j)),
        scratch_shapes=[pltpu.VMEM((hb + ts, tc), F32)],
        compiler_params=_cparams(("parallel", "parallel")),
        name=name,
    )(gate, gate, val, dw_w, dw_b.reshape(1, F))


def _ffn_act_bwd(gate, val, dh, dw_w, dw_b, *, ts=512, tc=1408, name):
    S, F = gate.shape
    hb = FFN_HALO
    nh = ts // hb
    n = S // ts
    te = ts + hb
    tc = _tile(F, tc)

    def body(g_ref, gp_ref, gn_ref, v_ref, vn_ref, dh_ref, dhn_ref, w_ref, b_ref,
             dg_ref, dv_ref, dw_ref, db_ref, st, sd):
        i = pl.program_id(1)
        first = i == 0
        last = i == n - 1

        @pl.when(first)
        def _():
            dw_ref[...] = jnp.zeros_like(dw_ref)
            db_ref[...] = jnp.zeros_like(db_ref)

        st[pl.ds(0, hb), :] = jnp.where(first, 0.0, gp_ref[...].astype(F32))
        st[pl.ds(hb, ts), :] = g_ref[...].astype(F32)
        st[pl.ds(hb + ts, hb), :] = jnp.where(last, 0.0, gn_ref[...].astype(F32))
        for c0 in range(0, tc, SUB_LANES):
            ln = pl.ds(c0, SUB_LANES)
            w0, w1, w2, b = w_ref[0:1, ln], w_ref[1:2, ln], w_ref[2:3, ln], b_ref[:, ln]
            db_acc = jnp.zeros((8, SUB_LANES), F32)
            dw_acc = [jnp.zeros((8, SUB_LANES), F32) for _ in range(3)]
            for r0 in range(0, te, SUB_ROWS):
                rc = min(SUB_ROWS, te - r0)
                taps = _rows_back(st, hb + r0, rc, ln, (2, 1, 0))
                gc = b + w0 * taps[0] + w1 * taps[1] + w2 * taps[2]
                ge, dge = _gelu_parts(gc)
                if r0 < ts:
                    rows = pl.ds(r0, rc)
                    val, dh = v_ref[rows, ln].astype(F32), dh_ref[rows, ln].astype(F32)
                else:
                    val = jnp.where(last, 0.0, vn_ref[:, ln].astype(F32)[0:rc])
                    dh = jnp.where(last, 0.0, dhn_ref[:, ln].astype(F32)[0:rc])
                dgc = dh * val * dge
                sd[pl.ds(r0, rc), ln] = dgc
                if r0 < ts:
                    dv_ref[rows, ln] = (dh * ge).astype(BF16)
                    db_acc = db_acc + jnp.sum(dgc.reshape(rc // 8, 8, SUB_LANES), axis=0)
                    for k in range(3):
                        dw_acc[k] = dw_acc[k] + jnp.sum((dgc * taps[k]).reshape(rc // 8, 8, SUB_LANES), axis=0)
            db_ref[:, ln] += jnp.sum(db_acc, axis=0, keepdims=True)
            for k in range(3):
                dw_ref[k:k + 1, ln] += jnp.sum(dw_acc[k], axis=0, keepdims=True)
            for r0 in range(0, ts, SUB_ROWS):
                ahead = _rows_ahead(sd, r0, SUB_ROWS, ln, (2, 1, 0))
                dg_ref[pl.ds(r0, SUB_ROWS), ln] = (w0 * ahead[0] + w1 * ahead[1] + w2 * ahead[2]).astype(BF16)

    cur = lambda j, i: (i, j)
    prev = lambda j, i: (jnp.maximum(i * nh - 1, 0), j)
    nxt = lambda j, i: (jnp.minimum((i + 1) * nh, S // hb - 1), j)
    return pl.pallas_call(
        body,
        out_shape=[jax.ShapeDtypeStruct((S, F), BF16), jax.ShapeDtypeStruct((S, F), BF16),
                   jax.ShapeDtypeStruct((3, F), F32), jax.ShapeDtypeStruct((1, F), F32)],
        grid=(F // tc, n),
        in_specs=[pl.BlockSpec((ts, tc), cur), pl.BlockSpec((hb, tc), prev), pl.BlockSpec((hb, tc), nxt),
                  pl.BlockSpec((ts, tc), cur), pl.BlockSpec((hb, tc), nxt),
                  pl.BlockSpec((ts, tc), cur), pl.BlockSpec((hb, tc), nxt),
                  pl.BlockSpec((3, tc), lambda j, i: (0, j)), pl.BlockSpec((1, tc), lambda j, i: (0, j))],
        out_specs=[pl.BlockSpec((ts, tc), cur), pl.BlockSpec((ts, tc), cur),
                   pl.BlockSpec((3, tc), lambda j, i: (0, j)), pl.BlockSpec((1, tc), lambda j, i: (0, j))],
        scratch_shapes=[pltpu.VMEM((hb + ts + hb, tc), F32), pltpu.VMEM((te, tc), F32)],
        compiler_params=_cparams(("parallel", "arbitrary")),
        name=name,
    )(gate, gate, gate, val, val, dh, dh, dw_w, dw_b.reshape(1, F))


def _ple_bwd_rows(dz, gate, proj):
    return dz * proj * gate * (1.0 - gate), dz * gate


def _loss_ln_bwd(z, ln_g, ln_b, target, gate, proj, *, ts=512, name):
    S, D = z.shape

    def body(z_ref, g_ref, b_ref, t_ref, gate_ref, proj_ref, dz_ref, dzb_ref, dg_ref, db_ref, loss_ref, ds_ref,
             dp_ref, dbg_ref):
        i = pl.program_id(0)

        @pl.when(i == 0)
        def _():
            dg_ref[...] = jnp.zeros_like(dg_ref)
            db_ref[...] = jnp.zeros_like(db_ref)
            loss_ref[...] = jnp.zeros_like(loss_ref)
            dbg_ref[...] = jnp.zeros_like(dbg_ref)

        dg_acc = jnp.zeros((8, D), F32)
        db_acc = jnp.zeros((8, D), F32)
        dbg_acc = jnp.zeros((8, D), F32)
        loss_acc = jnp.zeros((1, 1), F32)
        for r0 in range(0, ts, LN_ROWS):
            rows = pl.ds(r0, LN_ROWS)
            zt = z_ref[rows, :]
            err = _layer_norm_rows(zt, g_ref[...], b_ref[...]) - t_ref[rows, :]
            loss_acc = loss_acc + 0.5 * jnp.sum(jnp.mean(err * err, axis=-1, keepdims=True), keepdims=True)
            do = err * (1.0 / D)
            dz, xh = _ln_bwd_rows(zt, g_ref[...], do)
            dg_acc = dg_acc + jnp.sum((do * xh).reshape(LN_ROWS // 8, 8, D), axis=0)
            db_acc = db_acc + jnp.sum(do.reshape(LN_ROWS // 8, 8, D), axis=0)
            dz_ref[rows, :] = dz
            dzb_ref[rows, :] = dz.astype(BF16)
            ds, dp = _ple_bwd_rows(dz, gate_ref[rows, :], proj_ref[rows, :])
            ds_ref[rows, :] = ds.astype(BF16)
            dp_ref[rows, :] = dp.astype(BF16)
            dbg_acc = dbg_acc + jnp.sum(ds.reshape(LN_ROWS // 8, 8, D), axis=0)
        dg_ref[...] += jnp.sum(dg_acc, axis=0, keepdims=True)
        db_ref[...] += jnp.sum(db_acc, axis=0, keepdims=True)
        dbg_ref[...] += jnp.sum(dbg_acc, axis=0, keepdims=True)
        loss_ref[...] += loss_acc

    row = pl.BlockSpec((ts, D), lambda i: (i, 0))
    fix = pl.BlockSpec((1, D), lambda i: (0, 0))
    return pl.pallas_call(
        body,
        out_shape=[jax.ShapeDtypeStruct((S, D), F32), jax.ShapeDtypeStruct((S, D), BF16),
                   jax.ShapeDtypeStruct((1, D), F32), jax.ShapeDtypeStruct((1, D), F32),
                   jax.ShapeDtypeStruct((8, 128), F32), jax.ShapeDtypeStruct((S, D), BF16),
                   jax.ShapeDtypeStruct((S, D), BF16), jax.ShapeDtypeStruct((1, D), F32)],
        grid=(S // ts,),
        in_specs=[row, fix, fix, row, row, row],
        out_specs=[row, row, fix, fix, pl.BlockSpec((8, 128), lambda i: (0, 0)), row, row, fix],
        compiler_params=_cparams(("arbitrary",)),
        name=name,
    )(z, ln_g.reshape(1, D), ln_b.reshape(1, D), target, gate, proj)


HEADS_PER_STEP = 4
HEAD_LANES = HEADS_PER_STEP * HEAD_DIM


ATT_ROWS = 32
ATT_SCALE = HEAD_DIM ** -0.5


def _softmax_piece(scores, bias, qb):
    s = scores + bias
    kpos = qb * Q_BLOCK + lax.broadcasted_iota(jnp.int32, (1, KV_SPAN), 1)
    s = jnp.where(kpos >= KV_PAD, s, NEG_INF)
    e = jnp.exp(s - jnp.max(s, axis=-1, keepdims=True))
    return e * (1.0 / jnp.sum(e, axis=-1, keepdims=True))


def _head_masks():
    lane = lax.broadcasted_iota(jnp.int32, (1, HEAD_LANES), 1)
    return [(lane >= j * HEAD_DIM) & (lane < (j + 1) * HEAD_DIM) for j in range(HEADS_PER_STEP)]


def _pick_heads(masks, per_head):
    out = per_head[0]
    for mask, x in zip(masks[1:], per_head[1:]):
        out = jnp.where(mask, x, out)
    return out


def _pad_keys(qb, k_ref, v_ref, kp, vp):
    @pl.when(qb == 0)
    def _():
        kp[pl.ds(0, KV_PAD), :] = jnp.zeros((KV_PAD, HEAD_LANES), BF16)
        vp[pl.ds(0, KV_PAD), :] = jnp.zeros((KV_PAD, HEAD_LANES), BF16)
        kp[pl.ds(KV_PAD, k_ref.shape[0]), :] = k_ref[...]
        vp[pl.ds(KV_PAD, v_ref.shape[0]), :] = v_ref[...]


def _attn_fwd(qkv, bias):
    S = qkv.shape[0]
    nhp = N_HEADS // HEADS_PER_STEP

    def body(q_ref, k_ref, v_ref, b_ref, o_ref, kp, vp, p_scr):
        qb = pl.program_id(1)
        _pad_keys(qb, k_ref, v_ref, kp, vp)
        span = pl.ds(pl.multiple_of(qb * Q_BLOCK, Q_BLOCK), KV_SPAN)
        kc, vc = kp[span, :], vp[span, :]
        qt = q_ref[...] * ATT_SCALE
        mine = _head_masks()
        scores = [_bdot(jnp.where(mine[j], qt, jnp.zeros_like(qt)), kc, NT) for j in range(HEADS_PER_STEP)]
        outs = []
        for j in range(HEADS_PER_STEP):
            for r0 in range(0, Q_BLOCK, ATT_ROWS):
                rows = pl.ds(r0, ATT_ROWS)
                p_scr[j, rows, :] = _softmax_piece(scores[j][r0:r0 + ATT_ROWS], b_ref[j, rows, :], qb).astype(BF16)
            outs.append(_bdot(p_scr[j], vc))
        o_ref[...] = _pick_heads(mine, outs).astype(BF16)

    return pl.pallas_call(
        body,
        out_shape=jax.ShapeDtypeStruct((S, D_MODEL), BF16),
        grid=(nhp, S // Q_BLOCK),
        in_specs=[pl.BlockSpec((Q_BLOCK, HEAD_LANES), lambda h, i: (i, h)),
                  pl.BlockSpec((S, HEAD_LANES), lambda h, i: (0, nhp + h)),
                  pl.BlockSpec((S, HEAD_LANES), lambda h, i: (0, 2 * nhp + h)),
                  pl.BlockSpec((HEADS_PER_STEP, Q_BLOCK, KV_SPAN), lambda h, i: (h, 0, 0))],
        out_specs=pl.BlockSpec((Q_BLOCK, HEAD_LANES), lambda h, i: (i, h)),
        scratch_shapes=[pltpu.VMEM((KV_PAD + S, HEAD_LANES), BF16), pltpu.VMEM((KV_PAD + S, HEAD_LANES), BF16),
                        pltpu.VMEM((HEADS_PER_STEP, Q_BLOCK, KV_SPAN), BF16)],
        compiler_params=_cparams(("parallel", "arbitrary")),
        name="attn_fwd",
    )(qkv, qkv, qkv, bias)


def _attn_bwd(qkv, bias, do):
    S = qkv.shape[0]
    nhp = N_HEADS // HEADS_PER_STEP
    nq = S // Q_BLOCK
    scale = HEAD_DIM ** -0.5

    def body(q_ref, k_ref, v_ref, b_ref, do_ref, dq_ref, dk_ref, dv_ref, db_ref, kp, vp, dka, dva,
             p_scr, ds_scr):
        qb = pl.program_id(1)
        _pad_keys(qb, k_ref, v_ref, kp, vp)

        @pl.when(qb == 0)
        def _():
            dka[...] = jnp.zeros_like(dka)
            dva[...] = jnp.zeros_like(dva)
            db_ref[...] = jnp.zeros_like(db_ref)

        span = pl.ds(pl.multiple_of(qb * Q_BLOCK, Q_BLOCK), KV_SPAN)
        kc, vc = kp[span, :], vp[span, :]
        qt, dot = q_ref[...] * ATT_SCALE, do_ref[...]
        mine = _head_masks()
        dqs = []
        qs = [jnp.where(mine[j], qt, jnp.zeros_like(qt)) for j in range(HEADS_PER_STEP)]
        dos = [jnp.where(mine[j], dot, jnp.zeros_like(dot)) for j in range(HEADS_PER_STEP)]
        scores = [_bdot(qs[j], kc, NT) for j in range(HEADS_PER_STEP)]
        dps = [_bdot(dos[j], vc, NT) for j in range(HEADS_PER_STEP)]
        for j in range(HEADS_PER_STEP):
            qj, doj = qs[j], dos[j]
            for r0 in range(0, Q_BLOCK, ATT_ROWS):
                rows = pl.ds(r0, ATT_ROWS)
                p = _softmax_piece(scores[j][r0:r0 + ATT_ROWS], b_ref[j, rows, :], qb)
                dp = dps[j][r0:r0 + ATT_ROWS]
                ds = p * (dp - jnp.sum(p * dp, axis=-1, keepdims=True))
                db_ref[j, rows, :] += ds
                p_scr[j, rows, :] = p.astype(BF16)
                ds_scr[j, rows, :] = ds.astype(BF16)
            dva[span, :] += _bdot(p_scr[j], doj, TN)
            dqs.append(_bdot(ds_scr[j], kc))
            dka[span, :] += _bdot(ds_scr[j], qj, TN)
        dq_ref[...] = (scale * _pick_heads(mine, dqs)).astype(BF16)

        @pl.when(qb == nq - 1)
        def _():
            dk_ref[...] = dka[pl.ds(KV_PAD, S), :].astype(BF16)
            dv_ref[...] = dva[pl.ds(KV_PAD, S), :].astype(BF16)

    blk = pl.BlockSpec((Q_BLOCK, HEAD_LANES), lambda h, i: (i, h))
    col = pl.BlockSpec((S, HEAD_LANES), lambda h, i: (0, h))
    bsp = pl.BlockSpec((HEADS_PER_STEP, Q_BLOCK, KV_SPAN), lambda h, i: (h, 0, 0))
    return pl.pallas_call(
        body,
        out_shape=[jax.ShapeDtypeStruct((S, D_MODEL), BF16)] * 3
        + [jax.ShapeDtypeStruct((N_HEADS, Q_BLOCK, KV_SPAN), F32)],
        grid=(nhp, nq),
        in_specs=[blk, pl.BlockSpec((S, HEAD_LANES), lambda h, i: (0, nhp + h)),
                  pl.BlockSpec((S, HEAD_LANES), lambda h, i: (0, 2 * nhp + h)), bsp, blk],
        out_specs=[blk, col, col, bsp],
        scratch_shapes=[pltpu.VMEM((KV_PAD + S, HEAD_LANES), BF16), pltpu.VMEM((KV_PAD + S, HEAD_LANES), BF16),
                        pltpu.VMEM((KV_PAD + S, HEAD_LANES), F32), pltpu.VMEM((KV_PAD + S, HEAD_LANES), F32),
                        pltpu.VMEM((HEADS_PER_STEP, Q_BLOCK, KV_SPAN), BF16), pltpu.VMEM((HEADS_PER_STEP, Q_BLOCK, KV_SPAN), BF16)],
        compiler_params=_cparams(("parallel", "arbitrary")),
        name="attn_bwd",
    )(qkv, qkv, qkv, bias, do)


N_DIST = BAND + CHUNK - 1
N_FAR = KV_PAD + CHUNK - MAX_REL


def _shear_rows(x, towards_right):
    row = lax.broadcasted_iota(jnp.int32, (Q_BLOCK, 1), 0)
    for bit in range(Q_BLOCK.bit_length() - 1):
        step = 1 << bit
        x = jnp.where((row & step) != 0, pltpu.roll(x, step if towards_right else KV_SPAN - step, 1), x)
    return x


def _bias_blocks(rel_bias):
    H = rel_bias.shape[0]
    e = jnp.concatenate([jnp.broadcast_to(rel_bias[:, 2 * MAX_REL:], (H, N_FAR)),
                         jnp.flip(rel_bias[:, 2 * MAX_REL - (N_DIST - N_FAR):2 * MAX_REL], axis=1),
                         jnp.zeros((H, KV_SPAN - N_DIST), F32)], axis=1).reshape(H, 1, KV_SPAN)

    def body(e_ref, o_ref):
        first = pltpu.roll(jnp.broadcast_to(e_ref[...], (Q_BLOCK, KV_SPAN)), KV_SPAN - (CHUNK - 1), 1)
        x = _shear_rows(first, True)
        row = lax.broadcasted_iota(jnp.int32, (Q_BLOCK, 1), 0)
        chunk0 = row - (row & (CHUNK - 1))
        k = lax.broadcasted_iota(jnp.int32, (1, KV_SPAN), 1)
        o_ref[...] = jnp.where((k >= chunk0) & (k < chunk0 + BAND), x, NEG_INF)

    return pl.pallas_call(
        body,
        out_shape=jax.ShapeDtypeStruct((H, Q_BLOCK, KV_SPAN), F32),
        grid=(H,),
        in_specs=[pl.BlockSpec((None, 1, KV_SPAN), lambda h: (h, 0, 0))],
        out_specs=pl.BlockSpec((None, Q_BLOCK, KV_SPAN), lambda h: (h, 0, 0)),
        compiler_params=_cparams(("parallel",)),
        name="bias_blocks",
    )(e)


def _bias_blocks_grad(dblk):
    H = dblk.shape[0]

    def body(d_ref, o_ref):
        x = pltpu.roll(_shear_rows(d_ref[...], False), CHUNK - 1, 1)
        de = jnp.sum(x, axis=0, keepdims=True)
        lane = lax.broadcasted_iota(jnp.int32, de.shape, 1)
        far = jnp.sum(jnp.where(lane < N_FAR, de, 0.0), axis=-1, keepdims=True)
        o_ref[...] = jnp.where(lane == 0, far, jnp.where(lane < N_FAR, 0.0, de))

    de = pl.pallas_call(
        body,
        out_shape=jax.ShapeDtypeStruct((H, 1, KV_SPAN), F32),
        grid=(H,),
        in_specs=[pl.BlockSpec((None, Q_BLOCK, KV_SPAN), lambda h: (h, 0, 0))],
        out_specs=pl.BlockSpec((None, 1, KV_SPAN), lambda h: (h, 0, 0)),
        compiler_params=_cparams(("parallel",)),
        name="bias_grad_sum",
    )(dblk).reshape(H, KV_SPAN)
    near = jnp.flip(de[:, N_FAR:N_DIST], axis=1)
    return jnp.concatenate([jnp.zeros((H, 2 * MAX_REL - (N_DIST - N_FAR)), F32), near, de[:, 0:1]], axis=1)


def _ffn_forward(r1, r1b, p_l, w, l, ready):
    ready(f"up{l}", r1b)
    up_g = _mm_rows([(r1b, w["ffn_up_t"][l], True, (0, 2))], out_dtype=BF16, name=f"ffn_up_g{l}")
    up_v = _mm_rows([(r1b, w["ffn_up_t"][l], True, (1, 2))], out_dtype=BF16, name=f"ffn_up_v{l}")
    h = _ffn_act_fwd(up_g, up_v, w["ffn_dw_w"][l], w["ffn_dw_b"][l], name=f"ffn_act{l}")
    ready(f"dn{l}", h)
    z2, r2, r2b, gate, proj = _proj_ln(r1, h, w["ffn_w_down"][l], w["ln_ffn_g"][l], w["ln_ffn_b"][l],
                                       ple=(w["ple_w_gate"][l], w["ple_b_gate"][l], p_l, w["ple_w_proj"][l]),
                                       name=f"ffn_down_ln{l}")
    return dict(r1b=r1b, up_g=up_g, up_v=up_v, h=h, z2=z2, gate=gate, proj=proj), r2, r2b


def _ffn_backward(sv, dz2, dz2b, ple_bwd, p_l, w, l, grads, ln_bwd, emit):
    r1b = sv["r1b"]
    ds, dproj, db_gate = ple_bwd
    dh = _mm_rows([(dz2b, w["ffn_w_down"][l], True, WHOLE)], out_dtype=BF16, name=f"ffn_dh{l}")
    dgate, dval, d_dw_w, d_dw_b = _ffn_act_bwd(sv["up_g"], sv["up_v"], dh, w["ffn_dw_w"][l], w["ffn_dw_b"][l],
                                               name=f"ffn_act_bwd{l}")
    grads["ffn_w_down"][l] = _wgrad(sv["h"], dz2b, tm=1408, name=f"d_ffn_w_down{l}")
    d_up_g = _wgrad(dgate, r1b, tm=1408, part=(0, 2), name=f"d_ffn_up_g{l}")
    grads["ffn_up_t"][l] = _wgrad(dval, r1b, tm=1408, part=(1, 2), into=d_up_g, name=f"d_ffn_up_v{l}")
    grads["ple_w_gate"][l] = _wgrad(r1b, ds, name=f"d_ple_w_gate{l}")
    grads["ple_w_proj"][l] = _wgrad(p_l, dproj, piece=D_MODEL // N_DEV, name=f"d_ple_w_proj{l}")
    grads["ffn_dw_w"][l] = d_dw_w
    grads["ffn_dw_b"][l] = d_dw_b[0]
    grads["ple_b_gate"][l] = db_gate[0]
    return _mm_rows([(ds, w["ple_w_gate"][l], True, WHOLE), (dgate, w["ffn_up_t"][l], False, (0, 2)),
                     (dval, w["ffn_up_t"][l], False, (1, 2))], add=dz2, add_scale=ALPHA, ln_bwd=ln_bwd, dep=emit(),
                    name=f"dr1_{l}")


def _local_step(x, p, target, w, ready=lambda group, after: None, emit=lambda group, grads: None):
    grads = {k: [None, None] for k in ("ffn_w_down", "ffn_up_t", "ple_w_gate", "ple_w_proj", "ffn_dw_w",
                                       "ffn_dw_b", "ple_b_gate", "ln_ffn_g", "ln_ffn_b", "ln_mix_g", "ln_mix_b")}

    xb, pb = x.astype(BF16), p.astype(BF16)
    ready("mix", None)
    u = _mm_rows([(xb, w["mix_w_in_t"], True, WHOLE)], name="mix_in")
    ycat, dpool, hconv = _mixer_fwd(u, w["pool_w"], w["pool_scale"], w["conv_dw_w"], w["conv_dw_b"], w["conv_ln_g"],
                                    w["conv_ln_b"])
    ready("mixo", ycat)
    z1, r1, r1b = _proj_ln(x, ycat, w["mix_w_out"], w["ln_mix_g"][0], w["ln_mix_b"][0], name="mix_out_ln")
    sv0, r2, r2b = _ffn_forward(r1, r1b, pb[0], w, 0, ready)

    ready("attn", r2b)
    qkv = _mm_rows([(r2b, w["attn_w_qkv"], False, WHOLE)], out_dtype=BF16, name="attn_qkv")
    bias = _bias_blocks(w["attn_rel_bias"])
    attn = _attn_fwd(qkv, bias)
    z3, r3, r3b = _proj_ln(r2, attn, w["attn_w_o"], w["ln_mix_g"][1], w["ln_mix_b"][1], name="attn_out_ln")
    sv1, _, _ = _ffn_forward(r3, r3b, pb[1], w, 1, ready)

    dz4, dz4b, grads["ln_ffn_g"][1], grads["ln_ffn_b"][1], loss, *ple1 = _loss_ln_bwd(
        sv1["z2"], w["ln_ffn_g"][1], w["ln_ffn_b"][1], target, sv1["gate"], sv1["proj"], name="loss_ln_bwd")
    dz3, dz3b, grads["ln_mix_g"][1], grads["ln_mix_b"][1] = _ffn_backward(
        sv1, dz4, dz4b, ple1, pb[1], w, 1, grads, (z3, w["ln_mix_g"][1]), lambda: emit("ffn1", grads))
    grads["attn_w_o"] = _wgrad(attn, dz3b, name="d_attn_w_o")
    dattn = _mm_rows([(dz3b, w["attn_w_o"], True, WHOLE)], out_dtype=BF16, name="d_attn")
    dq, dk, dv, dbias = _attn_bwd(qkv, bias, dattn)
    grads["attn_rel_bias"] = _bias_blocks_grad(dbias)
    dqkv = jnp.concatenate([dq, dk, dv], axis=1)
    grads["attn_w_qkv"] = _wgrad(r2b, dqkv, tn=768, piece=3 * D_MODEL // N_DEV, name="d_attn_w_qkv")
    dz2, dz2b, grads["ln_ffn_g"][0], grads["ln_ffn_b"][0], *ple0 = _mm_rows(
        [(dqkv, w["attn_w_qkv"], True, WHOLE)], add=dz3, add_scale=ALPHA,
        ln_bwd=(sv0["z2"], w["ln_ffn_g"][0], sv0["gate"], sv0["proj"]), dep=emit("attn", grads), name="dr2")
    dz1, dz1b, grads["ln_mix_g"][0], grads["ln_mix_b"][0] = _ffn_backward(
        sv0, dz2, dz2b, ple0, pb[0], w, 0, grads, (z1, w["ln_mix_g"][0]), lambda: emit("ffn0", grads))
    grads["mix_w_out"] = _wgrad(ycat, dz1b, name="d_mix_w_out")
    dycat = _mm_rows([(dz1b, w["mix_w_out"], True, WHOLE)], name="d_ycat")
    du, g_pw, g_ps, g_cw, g_cb, g_cg, g_cbb = _mixer_bwd(u, dpool, hconv, dycat, w["pool_w"], w["pool_scale"],
                                                         w["conv_dw_w"], w["conv_ln_g"], w["conv_ln_b"])
    grads["mix_w_in_t"] = _wgrad(du, xb, name="d_mix_w_in")
    grads.update(pool_w=g_pw, pool_scale=g_ps[0], conv_dw_w=g_cw, conv_dw_b=g_cb[0], conv_ln_g=g_cg[0],
                 conv_ln_b=g_cbb[0])
    for kname in ("ln_ffn_g", "ln_ffn_b", "ln_mix_g", "ln_mix_b"):
        grads[kname] = [a[0] for a in grads[kname]]
    grad_x = _mm_rows([(du, w["mix_w_in_t"], False, WHOLE)], add=dz1, add_scale=ALPHA, dep=emit("mix", grads),
                      name="grad_x")
    return loss[0, 0], grad_x, grads


_HBM = pl.BlockSpec(memory_space=pltpu.HBM)
_SEM = pl.BlockSpec(memory_space=pltpu.SEMAPHORE)
_EFFECT = pltpu.SideEffectType.DATAFLOW_SIDE_EFFECTING


def _slot(ref, place, shape, k):
    if place in ("stack", "pieces"):
        return ref.at[k]
    ax = place[1]
    n = shape[ax]
    return ref.at[(slice(None),) * ax + (pl.ds(pl.multiple_of(k * n, n), n),)]


def _result_shape(buf, place):
    if place == "stack":
        return (N_DEV,) + buf.shape
    if place == "pieces":
        return buf.shape
    return tuple(s * N_DEV if i == place[1] else s for i, s in enumerate(buf.shape))


def _peers(x, y, c):
    for d in range(1, N_DEV):
        px, py, pc = x ^ ((d >> 2) & 1), y ^ ((d >> 1) & 1), c ^ (d & 1)
        yield d, (px, py, pc), 4 * px + 2 * py + pc


def _exchange_start(bufs, places, after, *, name):
    nb = len(bufs)
    lands = [lax.empty(_result_shape(b, p_), b.dtype) for b, p_ in zip(bufs, places)]
    has_after = after is not None

    def body(*refs):
        srcs, dsts = refs[:nb], refs[nb:2 * nb]
        outs = refs[2 * nb + has_after:]
        send_sems, recv_sems, token = outs[0], outs[1], outs[2 + 2 * nb]
        x, y, c = lax.axis_index("x"), lax.axis_index("y"), lax.axis_index("c")
        me = 4 * x + 2 * y + c
        for b in range(nb):
            for d, dev, peer in _peers(x, y, c):
                pltpu.make_async_remote_copy(
                    src_ref=srcs[b].at[peer] if places[b] == "pieces" else srcs[b],
                    dst_ref=_slot(dsts[b], places[b], bufs[b].shape, me),
                    send_sem=send_sems.at[b * N_DEV + d], recv_sem=recv_sems.at[b * N_DEV + d],
                    device_id=dev, device_id_type=pl.DeviceIdType.MESH).start()
            pltpu.make_async_copy(srcs[b].at[me] if places[b] == "pieces" else srcs[b],
                                  _slot(dsts[b], places[b], bufs[b].shape, me), recv_sems.at[b * N_DEV]).start()
        token[...] = jnp.zeros_like(token)

    sems = pltpu.SemaphoreType.DMA((nb * N_DEV,))
    ins = [pltpu.with_memory_space_constraint(a, pltpu.HBM) for a in list(bufs) + lands]
    out = pl.pallas_call(
        body,
        out_shape=(sems, sems, *[pltpu.HBM(a.shape, a.dtype) for a in ins], jax.ShapeDtypeStruct((8, 128), F32)),
        in_specs=[_HBM] * (2 * nb) + ([pl.BlockSpec(memory_space=pl.ANY)] if has_after else []),
        out_specs=(_SEM, _SEM, *[_HBM] * (2 * nb), pl.BlockSpec(memory_space=pltpu.VMEM)),
        input_output_aliases={i: 2 + i for i in range(2 * nb)},
        compiler_params=pltpu.CompilerParams(has_side_effects=_EFFECT),
        name=name,
    )(*ins, *([after] if has_after else []))
    return dict(send=out[0], recv=out[1], srcs=out[2:2 + nb], lands=out[2 + nb:2 + 2 * nb], token=out[-1],
                places=places)


def _exchange_wait(h, after, *, name):
    nb = len(h["srcs"])
    places = h["places"]
    shapes = [a.shape for a in h["srcs"]]

    def body(*refs):
        srcs, dsts, send_sems, recv_sems = refs[:nb], refs[nb:2 * nb], refs[2 * nb], refs[2 * nb + 1]
        x, y, c = lax.axis_index("x"), lax.axis_index("y"), lax.axis_index("c")
        me = 4 * x + 2 * y + c
        for b in range(nb):
            pieces = places[b] == "pieces"
            for d, dev, peer in _peers(x, y, c):
                cp = pltpu.make_async_remote_copy(
                    src_ref=srcs[b].at[peer] if pieces else srcs[b],
                    dst_ref=_slot(dsts[b], places[b], shapes[b], peer),
                    send_sem=send_sems.at[b * N_DEV + d], recv_sem=recv_sems.at[b * N_DEV + d],
                    device_id=dev, device_id_type=pl.DeviceIdType.MESH)
                cp.wait_send()
                cp.wait_recv()
            pltpu.make_async_copy(srcs[b].at[me] if pieces else srcs[b], _slot(dsts[b], places[b], shapes[b], me),
                                  recv_sems.at[b * N_DEV]).wait()

    ins = list(h["srcs"]) + list(h["lands"])
    out = pl.pallas_call(
        body,
        out_shape=tuple(pltpu.HBM(a.shape, a.dtype) for a in ins),
        in_specs=[_HBM] * (2 * nb) + [_SEM, _SEM, pl.BlockSpec(memory_space=pl.ANY)],
        out_specs=tuple([_HBM] * (2 * nb)),
        input_output_aliases={i: i for i in range(2 * nb)},
        compiler_params=pltpu.CompilerParams(has_side_effects=_EFFECT),
        name=name,
    )(*ins, h["send"], h["recv"], after)
    return out[nb:]


def _adamw(recv, w, m, v, *, layer=0, into=None, name):
    L, R, C = w.shape
    fits = [d for d in range(16, R + 1, 16) if R % d == 0 and d * C * 4 <= 2 * 1024 * 1024]
    tr = fits[-1] if fits else R
    c1 = 1.0 - ADAM_B1 ** ADAM_STEP
    c2 = 1.0 - ADAM_B2 ** ADAM_STEP

    def body(r_ref, w_ref, m_ref, v_ref, *rest):
        g_ref, d_ref, mo_ref, vo_ref = rest[-4:]
        g = r_ref[0].astype(F32)
        for i in range(1, N_DEV):
            g = g + r_ref[i].astype(F32)
        m_new = ADAM_B1 * m_ref[...] + (1.0 - ADAM_B1) * g
        v_new = ADAM_B2 * v_ref[...] + (1.0 - ADAM_B2) * (g * g)
        m_hat = m_new / c1
        v_hat = v_new / c2
        g_ref[...] = g
        d_ref[...] = -ADAM_LR * (m_hat / (jnp.sqrt(v_hat) + ADAM_EPS) + ADAM_WD * w_ref[...])
        mo_ref[...] = m_new
        vo_ref[...] = v_new

    row = pl.BlockSpec((None, tr, C), lambda i: (layer, i, 0))
    others = [] if into is None else list(into)
    return pl.pallas_call(
        body,
        out_shape=[jax.ShapeDtypeStruct((L, R, C), F32)] * 4,
        grid=(R // tr,),
        in_specs=[pl.BlockSpec((N_DEV, tr, C), lambda i: (0, i, 0)), row, row, row]
        + [pl.BlockSpec(memory_space=pl.ANY)] * len(others),
        out_specs=[row] * 4,
        input_output_aliases={4 + k: k for k in range(len(others))},
        compiler_params=_cparams(("parallel",)),
        name=name,
    )(recv, w, m, v, *others)


_TRANSPOSED = ("mix_w_in", "ffn_w_up")


def _ffn_groups(l):
    return ((f"up{l}", (("ffn_w_up", l, BF16, ("axis", 0)), ("ffn_dw_w", l, F32, "stack"))),
            (f"dn{l}", (("ffn_w_down", l, BF16, ("axis", 0)), ("ple_w_gate", l, BF16, ("axis", 0)),
                        ("ple_w_proj", l, BF16, ("axis", 1)))))


_GATHER_GROUPS = (
    ("mix", (("mix_w_in", 0, BF16, ("axis", 0)), ("conv_dw_w", 0, F32, "stack"))),
    ("mixo", (("mix_w_out", 0, BF16, ("axis", 0)),)),
    *_ffn_groups(0),
    ("attn", (("attn_w_qkv", 0, BF16, ("axis", 1)), ("attn_w_o", 0, BF16, ("axis", 0)))),
    *_ffn_groups(1))
_SHARDED = ("mix_w_in", "conv_dw_w", "mix_w_out", "attn_w_qkv", "attn_w_o", "ffn_w_up", "ffn_dw_w", "ffn_w_down",
            "ple_w_gate", "ple_w_proj")
_REPLICATED = ("pool_w", "pool_scale", "conv_dw_b", "conv_ln_g", "conv_ln_b", "attn_rel_bias", "ln_mix_g",
               "ln_mix_b", "ffn_dw_b", "ple_b_gate", "ln_ffn_g", "ln_ffn_b")


def _pack_rows(parts, row_mult, dtype):
    lead = parts[0].shape[:-1]
    flat = jnp.concatenate([a.astype(dtype) for a in parts], axis=-1)
    n = flat.shape[-1]
    unit = row_mult * LANES
    padded = -(-n // unit) * unit
    flat = jnp.pad(flat, [(0, 0)] * len(lead) + [(0, padded - n)])
    return flat.reshape(lead + (padded // LANES, LANES))


def _unpack(flat2d, shapes):
    flat = flat2d.reshape(-1)
    out, o = [], 0
    for s in shapes:
        n = math.prod(s)
        out.append(flat[o:o + n].reshape(s))
        o += n
    return out


def _full_from_shards(g, axis):
    parts = jnp.moveaxis(g, 0, axis)
    shp = list(g.shape[1:])
    shp[axis] *= g.shape[0]
    return parts.reshape(shp)


def _pieces_from_full(full, axis, k=N_DEV):
    shp = list(full.shape)
    n = shp[axis] // k
    t = full.reshape(shp[:axis] + [k, n] + shp[axis + 1:])
    return jnp.moveaxis(t, axis, 0)


def kernel(x, p, mix_w_in, pool_w, pool_scale, conv_dw_w, conv_dw_b, conv_ln_g, conv_ln_b, mix_w_out, attn_w_qkv, attn_rel_bias, attn_w_o, ln_mix_g, ln_mix_b, ffn_w_up, ffn_dw_w, ffn_dw_b, ffn_w_down, ple_w_proj, ple_w_gate, ple_b_gate, ln_ffn_g, ln_ffn_b, loss_target, m_mix_w_in, m_pool_w, m_pool_scale, m_conv_dw_w, m_conv_dw_b, m_conv_ln_g, m_conv_ln_b, m_mix_w_out, m_attn_w_qkv, m_attn_rel_bias, m_attn_w_o, m_ln_mix_g, m_ln_mix_b, m_ffn_w_up, m_ffn_dw_w, m_ffn_dw_b, m_ffn_w_down, m_ple_w_proj, m_ple_w_gate, m_ple_b_gate, m_ln_ffn_g, m_ln_ffn_b, v_mix_w_in, v_pool_w, v_pool_scale, v_conv_dw_w, v_conv_dw_b, v_conv_ln_g, v_conv_ln_b, v_mix_w_out, v_attn_w_qkv, v_attn_rel_bias, v_attn_w_o, v_ln_mix_g, v_ln_mix_b, v_ffn_w_up, v_ffn_dw_w, v_ffn_dw_b, v_ffn_w_down, v_ple_w_proj, v_ple_w_gate, v_ple_b_gate, v_ln_ffn_g, v_ln_ffn_b):
    a = dict(locals())
    sh_names = list(_SHARDED)
    names = sh_names + list(_REPLICATED)
    wts = {n: a[n] for n in names}
    mom = {n: a["m_" + n] for n in names}
    var = {n: a["v_" + n] for n in names}

    for n in _TRANSPOSED:
        wts[n], mom[n], var[n] = (jnp.swapaxes(d[n], 1, 2) for d in (wts, mom, var))
    gather = {}
    token = None
    for group, items in _GATHER_GROUPS:
        gather[group] = _exchange_start([wts[n][l].astype(dt) for n, l, dt, _ in items], [pl_ for *_, pl_ in items],
                                        token, name="gather_start_" + group)
        token = gather[group]["token"]

    w = dict(pool_w=pool_w[0], pool_scale=pool_scale[0], conv_dw_b=conv_dw_b[0], conv_ln_g=conv_ln_g[0],
             conv_ln_b=conv_ln_b[0], attn_rel_bias=attn_rel_bias[0], ln_mix_g=ln_mix_g, ln_mix_b=ln_mix_b,
             ffn_dw_b=ffn_dw_b, ple_b_gate=ple_b_gate, ln_ffn_g=ln_ffn_g, ln_ffn_b=ln_ffn_b)
    for n in ("ffn_up_t", "ffn_dw_w", "ffn_w_down", "ple_w_gate", "ple_w_proj"):
        w[n] = [None, None]

    def ready(group, after):
        got = _exchange_wait(gather[group], token if after is None else after, name="gather_wait_" + group)
        if group == "mix":
            w["mix_w_in_t"], w["conv_dw_w"] = got[0], _full_from_shards(got[1], 1)
        elif group == "mixo":
            (w["mix_w_out"],) = got
        elif group == "attn":
            w["attn_w_qkv"], w["attn_w_o"] = got
        elif group[:2] == "up":
            l = int(group[2])
            w["ffn_up_t"][l], w["ffn_dw_w"][l] = got[0], _full_from_shards(got[1], 1)
        else:
            l = int(group[2])
            w["ffn_w_down"][l], w["ple_w_gate"][l], w["ple_w_proj"][l] = got

    scatter = {}

    def emit(group, gr):
        if group[:3] == "ffn":
            l = int(group[3])
            pieces = [_pieces_from_full(gr["ffn_up_t"][l], 0),
                      _pieces_from_full(gr["ffn_dw_w"][l], 1), _pieces_from_full(gr["ffn_w_down"][l], 0),
                      _pieces_from_full(gr["ple_w_gate"][l], 0), gr["ple_w_proj"][l]]
        elif group == "attn":
            pieces = [gr["attn_w_qkv"], _pieces_from_full(gr["attn_w_o"], 0)]
        else:
            pieces = [_pieces_from_full(gr["mix_w_in_t"], 0), _pieces_from_full(gr["conv_dw_w"], 1),
                      _pieces_from_full(gr["mix_w_out"], 0)]
        scatter[group] = _exchange_start([a.astype(BF16) for a in pieces], ["pieces"] * len(pieces), None,
                                         name="grad_start_" + group)
        if group != "mix":
            return scatter[group]["token"]
        gfull = dict(
            pool_w=gr["pool_w"][None], pool_scale=gr["pool_scale"][None], conv_dw_b=gr["conv_dw_b"][None],
            conv_ln_g=gr["conv_ln_g"][None], conv_ln_b=gr["conv_ln_b"][None],
            attn_rel_bias=gr["attn_rel_bias"][None], ln_mix_g=jnp.stack(gr["ln_mix_g"]),
            ln_mix_b=jnp.stack(gr["ln_mix_b"]), ffn_dw_b=jnp.stack(gr["ffn_dw_b"]),
            ple_b_gate=jnp.stack(gr["ple_b_gate"]), ln_ffn_g=jnp.stack(gr["ln_ffn_g"]),
            ln_ffn_b=jnp.stack(gr["ln_ffn_b"]))
        rep_send = _pack_rows([gfull[n].reshape(-1) for n in _REPLICATED], 8, F32)
        scatter["replicated"] = _exchange_start([rep_send], ["stack"], scatter[group]["token"],
                                                name="grad_start_replicated")
        return scatter["replicated"]["token"]

    loss_part, grad_x, gr = _local_step(x[0], p[:, 0], loss_target[0], w, ready, emit)
    loss = lax.psum(loss_part, ("x", "y", "c"))

    group_weights = {"ffn1": (("ffn_w_up", 1), ("ffn_dw_w", 1), ("ffn_w_down", 1), ("ple_w_gate", 1), ("ple_w_proj", 1)),
                     "attn": (("attn_w_qkv", 0), ("attn_w_o", 0)),
                     "ffn0": (("ffn_w_up", 0), ("ffn_dw_w", 0), ("ffn_w_down", 0), ("ple_w_gate", 0), ("ple_w_proj", 0)),
                     "mix": (("mix_w_in", 0), ("conv_dw_w", 0), ("mix_w_out", 0))}
    updated = {}
    after = grad_x
    for group in ("ffn1", "attn", "ffn0", "mix"):
        recv = _exchange_wait(scatter[group], after, name="grad_wait_" + group)
        for (n, l), r in zip(group_weights[group], recv):
            updated[n] = _adamw(r, wts[n], mom[n], var[n], layer=l, into=updated.get(n), name=f"adamw_{n}{l}")
            after = updated[n][0]
    res = [{n: jnp.swapaxes(updated[n][k], 1, 2) if n in _TRANSPOSED else updated[n][k] for n in sh_names}
           for k in range(4)]
    (rep_recv,) = _exchange_wait(scatter["replicated"], after, name="grad_wait_replicated")

    def flat_state(d):
        return _pack_rows([d[n].reshape(-1) for n in _REPLICATED], 8, F32)[None]

    rep_out = _adamw(rep_recv, flat_state(wts), flat_state(mom), flat_state(var), name="adamw_replicated")
    for k in range(4):
        for n, arr in zip(_REPLICATED, _unpack(rep_out[k][0], [wts[n].shape for n in _REPLICATED])):
            res[k][n] = arr
    order = ["mix_w_in", "pool_w", "pool_scale", "conv_dw_w", "conv_dw_b", "conv_ln_g", "conv_ln_b", "mix_w_out",
             "attn_w_qkv", "attn_rel_bias", "attn_w_o", "ln_mix_g", "ln_mix_b", "ffn_w_up", "ffn_dw_w", "ffn_dw_b",
             "ffn_w_down", "ple_w_proj", "ple_w_gate", "ple_b_gate", "ln_ffn_g", "ln_ffn_b"]
    outs = [loss, grad_x[None]]
    for k in range(4):
        outs += [res[k][n] for n in order]
    return tuple(outs)
```

```python
import functools
import math

import jax
import jax.numpy as jnp
from jax import lax
from jax.experimental import pallas as pl
from jax.experimental.pallas import tpu as pltpu

F32 = jnp.float32
BF16 = jnp.bfloat16

N_DEV = 8
D_MODEL = 1024
D_POOL = 512
D_CONV = 512
POOL_WINDOWS = (2, 4, 8, 16)
POOL_GROUP = 128
CONV_KERNEL = 31
CHUNK = 64
HEAD_DIM = 64
N_HEADS = 16
LEFT_CHUNKS = 8
BAND = (LEFT_CHUNKS + 1) * CHUNK
MAX_REL = 256
D_FF = 2816
PLE_DIM = 256
ALPHA = 4.0 ** 0.25
LN_EPS = 1e-5
NEG_INF = -1e30
ADAM_LR, ADAM_B1, ADAM_B2, ADAM_EPS, ADAM_WD, ADAM_STEP = 0.001, 0.9, 0.999, 1e-08, 0.01, 10

Q_BLOCK = 4 * CHUNK
KV_PAD = LEFT_CHUNKS * CHUNK
KV_SPAN = KV_PAD + Q_BLOCK
CONV_HALO = 32
FFN_HALO = 16
SUB_ROWS, SUB_LANES = 64, 128
LANES = 1024
VMEM_LIMIT = 56 * 1024 * 1024


def _cparams(sem=None):
    return pltpu.CompilerParams(dimension_semantics=sem, vmem_limit_bytes=VMEM_LIMIT)


def _tile(dim, pref):
    if dim <= pref:
        return dim
    t = pref - pref % 128
    while t >= 128:
        if dim % t == 0:
            return t
        t -= 128
    return dim


def _sigmoid(x):
    return 1.0 / (1.0 + jnp.exp(-x))


def _bdot(a, b, dn=(((1,), (0,)), ((), ()))):
    return lax.dot_general(a.astype(BF16), b.astype(BF16), dn, preferred_element_type=F32)


WHOLE = (0, 1)
NT = (((1,), (1,)), ((), ()))
TN = (((0,), (0,)), ((), ()))


def _wgrad(a, b, *, tm=1024, tn=1024, tk=2048, piece=None, part=(0, 1), into=None, name):
    K, M = a.shape
    kb, N = b.shape
    assert K == kb, (a.shape, b.shape)
    tm, tn, tk = _tile(M, tm), _tile(N, tn), _tile(K, tk)
    nk = K // tk
    per = 1 if piece is None else tn // piece
    assert piece is None or tn == per * piece

    def body(a_ref, b_ref, *rest):
        o_ref, acc = rest[-2:]
        k = pl.program_id(2)

        @pl.when(k == 0)
        def _():
            acc[...] = jnp.zeros_like(acc)

        acc[...] += _bdot(a_ref[...], b_ref[...], TN)

        @pl.when(k == nk - 1)
        def _():
            if piece is None:
                o_ref[...] = acc[...].astype(BF16)
            else:
                for s in range(per):
                    o_ref[s] = acc[:, s * piece:(s + 1) * piece].astype(BF16)

    if piece is None:
        first = part[0] * (M // tm)
        out_shape = (part[1] * M, N)
        out_spec = pl.BlockSpec((tm, tn), lambda i, j, k: (first + i, j))
    else:
        out_shape, out_spec = (N // piece, M, piece), pl.BlockSpec((per, tm, piece), lambda i, j, k: (j, i, 0))
    others = [] if into is None else [into]
    return pl.pallas_call(
        body,
        out_shape=jax.ShapeDtypeStruct(out_shape, BF16),
        grid=(M // tm, N // tn, nk),
        in_specs=[pl.BlockSpec((tk, tm), lambda i, j, k: (k, i)), pl.BlockSpec((tk, tn), lambda i, j, k: (k, j))]
        + [pl.BlockSpec(memory_space=pl.ANY)] * len(others),
        out_specs=out_spec,
        input_output_aliases={2: 0} if others else {},
        scratch_shapes=[pltpu.VMEM((tm, tn), F32)],
        compiler_params=_cparams(("parallel", "parallel", "arbitrary")),
        name=name,
    )(a, b, *others)


def _mm_rows(pairs, *, add=None, add_scale=1.0, out_dtype=F32, tm=512, dep=None, ln_bwd=None, name):
    M = pairs[0][0].shape[0]
    n = len(pairs)
    has_add = add is not None
    has_ple = ln_bwd is not None and len(ln_bwd) == 4
    w_rows = [w_.shape[0] // part[1] for _, w_, _, part in pairs]
    N = w_rows[0] if pairs[0][2] else pairs[0][1].shape[1]

    def body(*refs):
        acc = None
        for i, (_, _, tr, _) in enumerate(pairs):
            part = _bdot(refs[2 * i][...], refs[2 * i + 1][...], NT if tr else (((1,), (0,)), ((), ())))
            acc = part if acc is None else acc + part
        if has_add:
            acc = acc + add_scale * refs[2 * n][...]
        if ln_bwd is None:
            refs[-1][...] = acc.astype(out_dtype)
            return
        first = 2 * n + has_add
        z_ref, g_ref = refs[first], refs[first + 1]
        outs = refs[-(7 if has_ple else 4):]
        dz_ref, dzb_ref, dg_ref, db_ref = outs[:4]

        @pl.when(pl.program_id(0) == 0)
        def _():
            for sums in outs[2:4] + outs[6:]:
                sums[...] = jnp.zeros_like(sums)

        dg_acc = jnp.zeros((8, N), F32)
        db_acc = jnp.zeros((8, N), F32)
        dbg_acc = jnp.zeros((8, N), F32)
        for r0 in range(0, tm, LN_ROWS):
            rows = pl.ds(r0, LN_ROWS)
            do = acc[r0:r0 + LN_ROWS]
            dz, xh = _ln_bwd_rows(z_ref[rows, :], g_ref[...], do)
            dz_ref[rows, :] = dz
            dzb_ref[rows, :] = dz.astype(BF16)
            dg_acc = dg_acc + jnp.sum((do * xh).reshape(LN_ROWS // 8, 8, N), axis=0)
            db_acc = db_acc + jnp.sum(do.reshape(LN_ROWS // 8, 8, N), axis=0)
            if has_ple:
                ds, dp = _ple_bwd_rows(dz, refs[first + 2][rows, :], refs[first + 3][rows, :])
                outs[4][rows, :] = ds.astype(BF16)
                outs[5][rows, :] = dp.astype(BF16)
                dbg_acc = dbg_acc + jnp.sum(ds.reshape(LN_ROWS // 8, 8, N), axis=0)
        dg_ref[...] += jnp.sum(dg_acc, axis=0, keepdims=True)
        db_ref[...] += jnp.sum(db_acc, axis=0, keepdims=True)
        if has_ple:
            outs[6][...] += jnp.sum(dbg_acc, axis=0, keepdims=True)

    in_specs, args = [], []
    for (a, w_, _, part), rows in zip(pairs, w_rows):
        in_specs += [pl.BlockSpec((tm, a.shape[1]), lambda i: (i, 0)),
                     pl.BlockSpec((rows, w_.shape[1]), functools.partial(lambda i, j: (j, 0), j=part[0]))]
        args += [a, w_]
    row = pl.BlockSpec((tm, N), lambda i: (i, 0))
    fix = pl.BlockSpec((1, N), lambda i: (0, 0))
    if has_add:
        in_specs.append(row)
        args.append(add)
    if ln_bwd is not None:
        in_specs += [row, fix] + [row] * (len(ln_bwd) - 2)
        args += [ln_bwd[0], ln_bwd[1].reshape(1, N), *ln_bwd[2:]]
    if dep is not None:
        in_specs.append(pl.BlockSpec(memory_space=pl.ANY))
        args.append(dep)
    if ln_bwd is None:
        out_shape, out_specs = jax.ShapeDtypeStruct((M, N), out_dtype), row
    else:
        out_shape = [jax.ShapeDtypeStruct((M, N), F32), jax.ShapeDtypeStruct((M, N), BF16),
                     jax.ShapeDtypeStruct((1, N), F32), jax.ShapeDtypeStruct((1, N), F32)]
        out_specs = [row, row, fix, fix]
        if has_ple:
            out_shape += [jax.ShapeDtypeStruct((M, N), BF16), jax.ShapeDtypeStruct((M, N), BF16),
                          jax.ShapeDtypeStruct((1, N), F32)]
            out_specs += [row, row, fix]
    return pl.pallas_call(
        body,
        out_shape=out_shape,
        grid=(M // tm,),
        in_specs=in_specs,
        out_specs=out_specs,
        compiler_params=_cparams(("parallel",) if ln_bwd is None else ("arbitrary",)),
        name=name,
    )(*args)


def _ln_bwd_rows(zt, g, do):
    zc = zt - jnp.mean(zt, axis=-1, keepdims=True)
    rstd = lax.rsqrt(jnp.mean(zc * zc, axis=-1, keepdims=True) + LN_EPS)
    xh = zc * rstd
    dxh = do * g
    return rstd * (dxh - jnp.mean(dxh, axis=-1, keepdims=True) - xh * jnp.mean(dxh * xh, axis=-1, keepdims=True)), xh


def _layer_norm_rows(z, g, b):
    mu = jnp.mean(z, axis=-1, keepdims=True)
    zc = z - mu
    var = jnp.mean(zc * zc, axis=-1, keepdims=True)
    return zc * lax.rsqrt(var + LN_EPS) * g + b


def _proj_ln(res, a, w, ln_g, ln_b, *, ple=None, ts=512, name):
    S, D = res.shape
    ka = a.shape[1]
    has_ple = ple is not None
    row = lambda i: (i, 0)
    fix = lambda i: (0, 0)

    def body(*refs):
        if has_ple:
            (res_ref, a_ref, w_ref, g_ref, b_ref, wg_ref, bg_ref, p_ref, wp_ref, z_ref, r_ref, rb_ref, gate_ref,
             proj_ref, acc) = refs
        else:
            res_ref, a_ref, w_ref, g_ref, b_ref, z_ref, r_ref, rb_ref, acc = refs
        acc[...] = _bdot(a_ref[...], w_ref[...])
        if has_ple:
            gate_ref[...] = _bdot(res_ref[...], wg_ref[...])
            proj_ref[...] = _bdot(p_ref[...], wp_ref[...])
        for r0 in range(0, ts, LN_ROWS):
            rows = pl.ds(r0, LN_ROWS)
            z = ALPHA * res_ref[rows, :] + acc[rows, :]
            if has_ple:
                gate = _sigmoid(gate_ref[rows, :] + bg_ref[...])
                gate_ref[rows, :] = gate
                z = z + gate * proj_ref[rows, :]
            z_ref[rows, :] = z
            r = _layer_norm_rows(z, g_ref[...], b_ref[...])
            r_ref[rows, :] = r
            rb_ref[rows, :] = r.astype(BF16)

    in_specs = [pl.BlockSpec((ts, D), row), pl.BlockSpec((ts, ka), row), pl.BlockSpec((ka, D), fix),
                pl.BlockSpec((1, D), fix), pl.BlockSpec((1, D), fix)]
    args = [res, a, w, ln_g.reshape(1, D), ln_b.reshape(1, D)]
    out_dtypes = [F32, F32, BF16]
    if has_ple:
        wg, bg, p, wp = ple
        in_specs += [pl.BlockSpec((D, D), fix), pl.BlockSpec((1, D), fix), pl.BlockSpec((ts, PLE_DIM), row),
                     pl.BlockSpec((PLE_DIM, D), fix)]
        args += [wg, bg.reshape(1, D), p, wp]
        out_dtypes += [F32, F32]
    return pl.pallas_call(
        body,
        out_shape=[jax.ShapeDtypeStruct((S, D), dt) for dt in out_dtypes],
        grid=(S // ts,),
        in_specs=in_specs,
        out_specs=[pl.BlockSpec((ts, D), row)] * len(out_dtypes),
        scratch_shapes=[pltpu.VMEM((ts, D), F32)],
        compiler_params=_cparams(("parallel",)),
        name=name,
    )(*args)


CONV_ROWS = 32
LN_ROWS = 16


def _shifted_copies(src, dst, rows):
    for c0 in range(0, src.shape[1], SUB_LANES):
        ln = pl.ds(c0, SUB_LANES)
        for r0 in range(0, rows, SUB_ROWS):
            rc = min(SUB_ROWS, rows - r0)
            for b, shifted in enumerate(_rows_ahead(src, r0, rc, ln, range(1, 8))):
                dst[b, pl.ds(r0, rc), ln] = shifted


def _rows_at(src, copies, off, n, ln):
    b = off % 8
    return src[pl.ds(off, n), ln] if b == 0 else copies[b - 1, pl.ds(off - b, n), ln]


def _conv31(stg, gsh, cw_ref, cb_ref, out, rows, first_off):
    for c0 in range(0, D_CONV, SUB_LANES):
        ln = pl.ds(c0, SUB_LANES)
        for r0 in range(0, rows, CONV_ROWS):
            acc = jnp.zeros((CONV_ROWS, SUB_LANES), F32) + cb_ref[:, ln]
            for k in range(CONV_KERNEL):
                acc = acc + cw_ref[k:k + 1, ln] * _rows_at(stg, gsh, first_off + k + r0, CONV_ROWS, ln)
            out[pl.ds(r0, CONV_ROWS), ln] = acc


def _mixer_fwd(u, pool_w, pool_scale, conv_w, conv_b, cln_g, cln_b, *, ts=256):
    S = u.shape[0]
    hb = CONV_HALO
    nh = ts // hb

    def body(u_ref, uh_ref, pw_ref, ps_ref, cw_ref, cb_ref, g_ref, b_ref, y_ref, d_ref, hcs, sta, stg, gsh):
        i = pl.program_id(0)
        first = i == 0
        sta[pl.ds(0, hb), :] = jnp.where(first, 0.0, uh_ref[:, 0:D_POOL])
        sta[pl.ds(hb, ts), :] = u_ref[:, 0:D_POOL]
        glu_h = uh_ref[:, D_POOL:D_POOL + D_CONV] * _sigmoid(uh_ref[:, D_POOL + D_CONV:])
        stg[pl.ds(0, hb), :] = jnp.where(first, 0.0, glu_h)
        stg[pl.ds(hb, ts), :] = u_ref[:, D_POOL:D_POOL + D_CONV] * _sigmoid(u_ref[:, D_POOL + D_CONV:])

        for g, w in enumerate(POOL_WINDOWS):
            lanes = pl.ds(g * POOL_GROUP, POOL_GROUP)
            for r0 in range(0, ts, SUB_ROWS):
                s = None
                for q in range(0, w, 8):
                    for tap in _rows_back(sta, hb + r0 - q, SUB_ROWS, lanes, range(min(8, w - q))):
                        s = tap if s is None else s + tap
                pos = (i * ts + r0 + lax.broadcasted_iota(jnp.int32, (SUB_ROWS, 1), 0) + 1).astype(F32)
                d_g = s / jnp.minimum(pos, float(w)) - sta[pl.ds(hb + r0, SUB_ROWS), lanes]
                d_ref[pl.ds(r0, SUB_ROWS), lanes] = d_g.astype(BF16)
            y_ref[:, lanes] = (_bdot(d_ref[:, lanes], pw_ref[g]) * ps_ref[:, lanes]).astype(BF16)

        _shifted_copies(stg, gsh, hb + ts - 8)
        _conv31(stg, gsh, cw_ref, cb_ref, hcs, ts, hb - (CONV_KERNEL - 1))
        for r0 in range(0, ts, LN_ROWS):
            rows = pl.ds(r0, LN_ROWS)
            ln = _layer_norm_rows(hcs[rows, :], g_ref[...], b_ref[...])
            y_ref[rows, D_POOL:] = (ln * _sigmoid(ln)).astype(BF16)

    fix2 = lambda i: (0, 0)
    return pl.pallas_call(
        body,
        out_shape=[jax.ShapeDtypeStruct((S, D_MODEL), BF16), jax.ShapeDtypeStruct((S, D_POOL), BF16),
                   jax.ShapeDtypeStruct((S, D_CONV), F32)],
        grid=(S // ts,),
        in_specs=[pl.BlockSpec((ts, 3 * D_POOL), lambda i: (i, 0)),
                  pl.BlockSpec((hb, 3 * D_POOL), lambda i: (jnp.maximum(i * nh - 1, 0), 0)),
                  pl.BlockSpec((4, POOL_GROUP, POOL_GROUP), lambda i: (0, 0, 0)),
                  pl.BlockSpec((1, D_POOL), fix2), pl.BlockSpec((CONV_KERNEL, D_CONV), fix2),
                  pl.BlockSpec((1, D_CONV), fix2), pl.BlockSpec((1, D_CONV), fix2), pl.BlockSpec((1, D_CONV), fix2)],
        out_specs=[pl.BlockSpec((ts, D_MODEL), lambda i: (i, 0)), pl.BlockSpec((ts, D_POOL), lambda i: (i, 0)),
                   pl.BlockSpec((ts, D_CONV), lambda i: (i, 0))],
        scratch_shapes=[pltpu.VMEM((hb + ts, D_POOL), F32), pltpu.VMEM((hb + ts, D_CONV), F32),
                        pltpu.VMEM((7, hb + ts - 8, D_CONV), F32)],
        compiler_params=_cparams(("parallel",)),
        name="mixer_fwd",
    )(u, u, pool_w, pool_scale.reshape(1, D_POOL), conv_w, conv_b.reshape(1, D_CONV), cln_g.reshape(1, D_CONV),
      cln_b.reshape(1, D_CONV))


def _mixer_bwd(u, d, hc, dycat, pool_w, pool_scale, conv_w, cln_g, cln_b, *, ts=256):
    S = u.shape[0]
    hb = CONV_HALO
    nh = ts // hb
    n = S // ts
    te = ts + hb
    K = CONV_KERNEL

    def body(u_ref, up_ref, un_ref, d_ref, hc_ref, hcn_ref, dy_ref, dyn_ref, pw_ref, ps_ref, cw_ref, g_ref, b_ref,
             du_ref, dpw_ref, dps_ref, dcw_ref, dcb_ref, dg_ref, db_ref, stg, std, sth, gsh, hsh):
        i = pl.program_id(0)
        first = i == 0
        last = i == n - 1

        @pl.when(first)
        def _():
            dpw_ref[...] = jnp.zeros_like(dpw_ref)
            dps_ref[...] = jnp.zeros_like(dps_ref)
            dcw_ref[...] = jnp.zeros_like(dcw_ref)
            dcb_ref[...] = jnp.zeros_like(dcb_ref)
            dg_ref[...] = jnp.zeros_like(dg_ref)
            db_ref[...] = jnp.zeros_like(db_ref)

        pos_e = (i * ts + lax.broadcasted_iota(jnp.int32, (te, 1), 0) + 1).astype(F32)
        dya = dy_ref[:, 0:D_POOL]
        dya_n = jnp.where(last, 0.0, dyn_ref[:, 0:D_POOL])
        for g, w in enumerate(POOL_WINDOWS):
            lanes = pl.ds(g * POOL_GROUP, POOL_GROUP)
            sl = slice(g * POOL_GROUP, (g + 1) * POOL_GROUP)
            pw = pw_ref[g]
            scale = ps_ref[:, lanes]
            d_g = d_ref[:, lanes]
            pre = _bdot(d_g, pw)
            dps_ref[:, lanes] += jnp.sum(dya[:, sl] * pre, axis=0, keepdims=True)
            dys = dya[:, sl] * scale
            dpw_ref[g] += _bdot(d_g, dys, TN)
            dys_e = jnp.concatenate([dys, dya_n[:, sl] * scale], axis=0)
            dd = _bdot(dys_e, pw, NT)
            std[:, lanes] = dd / jnp.minimum(pos_e, float(w))
            for r0 in range(0, ts, SUB_ROWS):
                da = -dd[r0:r0 + SUB_ROWS]
                for q in range(0, w, 8):
                    for tap in _rows_ahead(std, r0 + q, SUB_ROWS, lanes, range(min(8, w - q))):
                        da = da + tap
                du_ref[pl.ds(r0, SUB_ROWS), lanes] = da.astype(BF16)

        glu_p = up_ref[:, D_POOL:D_POOL + D_CONV] * _sigmoid(up_ref[:, D_POOL + D_CONV:])
        stg[pl.ds(0, hb), :] = jnp.where(first, 0.0, glu_p)
        bv = u_ref[:, D_POOL:D_POOL + D_CONV]
        sg = _sigmoid(u_ref[:, D_POOL + D_CONV:])
        stg[pl.ds(hb, ts), :] = bv * sg
        glu_n = un_ref[:, D_POOL:D_POOL + D_CONV] * _sigmoid(un_ref[:, D_POOL + D_CONV:])
        stg[pl.ds(hb + ts, hb), :] = jnp.where(last, 0.0, glu_n)
        _shifted_copies(stg, gsh, hb + te - 8)

        sums = [jnp.zeros((8, D_CONV), F32) for _ in range(3)]
        for r0 in range(0, te, LN_ROWS):
            rows = pl.ds(r0, LN_ROWS)
            hc = hc_ref[rows, :] if r0 < ts else hcn_ref[pl.ds(r0 - ts, LN_ROWS), :]
            hcc = hc - jnp.mean(hc, axis=-1, keepdims=True)
            rstd = lax.rsqrt(jnp.mean(hcc * hcc, axis=-1, keepdims=True) + LN_EPS)
            xh = hcc * rstd
            ln = xh * g_ref[...] + b_ref[...]
            sl_ = _sigmoid(ln)
            if r0 < ts:
                dyb = dy_ref[rows, D_POOL:]
            else:
                dyb = jnp.where(last, 0.0, dyn_ref[pl.ds(r0 - ts, LN_ROWS), D_POOL:])
            dln = dyb * (sl_ * (1.0 + ln * (1.0 - sl_)))
            dxh = dln * g_ref[...]
            dhc = rstd * (dxh - jnp.mean(dxh, axis=-1, keepdims=True)
                          - xh * jnp.mean(dxh * xh, axis=-1, keepdims=True))
            sth[rows, :] = dhc
            if r0 < ts:
                for n_, term in enumerate((dln * xh, dln, dhc)):
                    sums[n_] = sums[n_] + jnp.sum(term.reshape(LN_ROWS // 8, 8, D_CONV), axis=0)
        dg_ref[...] += jnp.sum(sums[0], axis=0, keepdims=True)
        db_ref[...] += jnp.sum(sums[1], axis=0, keepdims=True)
        dcb_ref[...] += jnp.sum(sums[2], axis=0, keepdims=True)

        _shifted_copies(sth, hsh, te - 8)
        for c0 in range(0, D_CONV, SUB_LANES):
            ln_ = pl.ds(c0, SUB_LANES)
            for r0 in range(0, ts, CONV_ROWS):
                rows = pl.ds(r0, CONV_ROWS)
                dglu = jnp.zeros((CONV_ROWS, SUB_LANES), F32)
                for k in range(K):
                    dglu = dglu + cw_ref[k:k + 1, ln_] * _rows_at(sth, hsh, K - 1 - k + r0, CONV_ROWS, ln_)
                bv = u_ref[rows, pl.ds(D_POOL + c0, SUB_LANES)]
                sg = _sigmoid(u_ref[rows, pl.ds(D_POOL + D_CONV + c0, SUB_LANES)])
                du_ref[rows, pl.ds(D_POOL + c0, SUB_LANES)] = (dglu * sg).astype(BF16)
                du_ref[rows, pl.ds(D_POOL + D_CONV + c0, SUB_LANES)] = (dglu * bv * sg * (1.0 - sg)).astype(BF16)
            for k in range(K):
                tap = jnp.zeros((8, SUB_LANES), F32)
                for r0 in range(0, ts, CONV_ROWS):
                    prod = sth[pl.ds(r0, CONV_ROWS), ln_] * _rows_at(stg, gsh, hb - (K - 1) + k + r0, CONV_ROWS, ln_)
                    tap = tap + jnp.sum(prod.reshape(CONV_ROWS // 8, 8, SUB_LANES), axis=0)
                dcw_ref[k:k + 1, ln_] += jnp.sum(tap, axis=0, keepdims=True)

    fix2 = lambda i: (0, 0)
    prev = lambda i: (jnp.maximum(i * nh - 1, 0), 0)
    nxt = lambda i: (jnp.minimum((i + 1) * nh, S // hb - 1), 0)
    return pl.pallas_call(
        body,
        out_shape=[jax.ShapeDtypeStruct((S, 3 * D_POOL), BF16),
                   jax.ShapeDtypeStruct((4, POOL_GROUP, POOL_GROUP), F32),
                   jax.ShapeDtypeStruct((1, D_POOL), F32),
                   jax.ShapeDtypeStruct((K, D_CONV), F32),
                   jax.ShapeDtypeStruct((1, D_CONV), F32),
                   jax.ShapeDtypeStruct((1, D_CONV), F32),
                   jax.ShapeDtypeStruct((1, D_CONV), F32)],
        grid=(n,),
        in_specs=[pl.BlockSpec((ts, 3 * D_POOL), lambda i: (i, 0)),
                  pl.BlockSpec((hb, 3 * D_POOL), prev),
                  pl.BlockSpec((hb, 3 * D_POOL), nxt),
                  pl.BlockSpec((ts, D_POOL), lambda i: (i, 0)),
                  pl.BlockSpec((ts, D_CONV), lambda i: (i, 0)),
                  pl.BlockSpec((hb, D_CONV), nxt),
                  pl.BlockSpec((ts, D_MODEL), lambda i: (i, 0)),
                  pl.BlockSpec((hb, D_MODEL), nxt),
                  pl.BlockSpec((4, POOL_GROUP, POOL_GROUP), lambda i: (0, 0, 0)),
                  pl.BlockSpec((1, D_POOL), fix2), pl.BlockSpec((K, D_CONV), fix2),
                  pl.BlockSpec((1, D_CONV), fix2), pl.BlockSpec((1, D_CONV), fix2)],
        out_specs=[pl.BlockSpec((ts, 3 * D_POOL), lambda i: (i, 0)),
                   pl.BlockSpec((4, POOL_GROUP, POOL_GROUP), lambda i: (0, 0, 0)),
                   pl.BlockSpec((1, D_POOL), fix2), pl.BlockSpec((K, D_CONV), fix2),
                   pl.BlockSpec((1, D_CONV), fix2), pl.BlockSpec((1, D_CONV), fix2), pl.BlockSpec((1, D_CONV), fix2)],
        scratch_shapes=[pltpu.VMEM((hb + ts + hb, D_CONV), F32), pltpu.VMEM((te, D_POOL), F32),
                        pltpu.VMEM((te, D_CONV), F32), pltpu.VMEM((7, hb + te - 8, D_CONV), F32),
                        pltpu.VMEM((7, te - 8, D_CONV), F32)],
        compiler_params=_cparams(("arbitrary",)),
        name="mixer_bwd",
    )(u, u, u, d, hc, hc, dycat, dycat, pool_w, pool_scale.reshape(1, D_POOL), conv_w, cln_g.reshape(1, D_CONV),
      cln_b.reshape(1, D_CONV))


_GELU_C = math.sqrt(2.0 / math.pi)


def _gelu_parts(x):
    inner = _GELU_C * (x + 0.044715 * x * x * x)
    th = jnp.tanh(inner)
    ge = 0.5 * x * (1.0 + th)
    dge = 0.5 * (1.0 + th) + 0.5 * x * (1.0 - th * th) * (_GELU_C * (1.0 + 3.0 * 0.044715 * x * x))
    return ge, dge


def _rows_back(ref, r, n, ln, shifts):
    ext = ref[pl.ds(r - 8, n + 8), ln]
    return [(pltpu.roll(ext, s, 0) if s else ext)[8:] for s in shifts]


def _rows_ahead(ref, r, n, ln, shifts):
    ext = ref[pl.ds(r, n + 8), ln]
    return [(pltpu.roll(ext, n + 8 - s, 0) if s else ext)[:n] for s in shifts]


def _ffn_act_fwd(gate, val, dw_w, dw_b, *, ts=512, tc=1408, name):
    S, F = gate.shape
    hb = FFN_HALO
    nh = ts // hb
    tc = _tile(F, tc)

    def body(g_ref, gh_ref, v_ref, w_ref, b_ref, h_ref, st):
        i = pl.program_id(0)
        st[pl.ds(0, hb), :] = jnp.where(i == 0, 0.0, gh_ref[...].astype(F32))
        st[pl.ds(hb, ts), :] = g_ref[...].astype(F32)
        for c0 in range(0, tc, SUB_LANES):
            ln = pl.ds(c0, SUB_LANES)
            w0, w1, w2, b = w_ref[0:1, ln], w_ref[1:2, ln], w_ref[2:3, ln], b_ref[:, ln]
            for r0 in range(0, ts, SUB_ROWS):
                taps = _rows_back(st, hb + r0, SUB_ROWS, ln, (2, 1, 0))
                gc = b + w0 * taps[0] + w1 * taps[1] + w2 * taps[2]
                ge, _ = _gelu_parts(gc)
                rows = pl.ds(r0, SUB_ROWS)
                h_ref[rows, ln] = (ge * v_ref[rows, ln].astype(F32)).astype(BF16)

    return pl.pallas_call(
        body,
        out_shape=jax.ShapeDtypeStruct((S, F), BF16),
        grid=(S // ts, F // tc),
        in_specs=[pl.BlockSpec((ts, tc), lambda i, j: (i, j)),
                  pl.BlockSpec((hb, tc), lambda i, j: (jnp.maximum(i * nh - 1, 0), j)),
                  pl.BlockSpec((ts, tc), lambda i, j: (i, j)),
                  pl.BlockSpec((3, tc), lambda i, j: (0, j)),
                  pl.BlockSpec((1, tc), lambda i, j: (0, j))],
        out_specs=pl.BlockSpec((ts, tc), lambda i, j: (i, j)),
        scratch_shapes=[pltpu.VMEM((hb + ts, tc), F32)],
        compiler_params=_cparams(("parallel", "parallel")),
        name=name,
    )(gate, gate, val, dw_w, dw_b.reshape(1, F))


def _ffn_act_bwd(gate, val, dh, dw_w, dw_b, *, ts=512, tc=1408, name):
    S, F = gate.shape
    hb = FFN_HALO
    nh = ts // hb
    n = S // ts
    te = ts + hb
    tc = _tile(F, tc)

    def body(g_ref, gp_ref, gn_ref, v_ref, vn_ref, dh_ref, dhn_ref, w_ref, b_ref,
             dg_ref, dv_ref, dw_ref, db_ref, st, sd):
        i = pl.program_id(1)
        first = i == 0
        last = i == n - 1

        @pl.when(first)
        def _():
            dw_ref[...] = jnp.zeros_like(dw_ref)
            db_ref[...] = jnp.zeros_like(db_ref)

        st[pl.ds(0, hb), :] = jnp.where(first, 0.0, gp_ref[...].astype(F32))
        st[pl.ds(hb, ts), :] = g_ref[...].astype(F32)
        st[pl.ds(hb + ts, hb), :] = jnp.where(last, 0.0, gn_ref[...].astype(F32))
        for c0 in range(0, tc, SUB_LANES):
            ln = pl.ds(c0, SUB_LANES)
            w0, w1, w2, b = w_ref[0:1, ln], w_ref[1:2, ln], w_ref[2:3, ln], b_ref[:, ln]
            db_acc = jnp.zeros((8, SUB_LANES), F32)
            dw_acc = [jnp.zeros((8, SUB_LANES), F32) for _ in range(3)]
            for r0 in range(0, te, SUB_ROWS):
                rc = min(SUB_ROWS, te - r0)
                taps = _rows_back(st, hb + r0, rc, ln, (2, 1, 0))
                gc = b + w0 * taps[0] + w1 * taps[1] + w2 * taps[2]
                ge, dge = _gelu_parts(gc)
                if r0 < ts:
                    rows = pl.ds(r0, rc)
                    val, dh = v_ref[rows, ln].astype(F32), dh_ref[rows, ln].astype(F32)
                else:
                    val = jnp.where(last, 0.0, vn_ref[:, ln].astype(F32)[0:rc])
                    dh = jnp.where(last, 0.0, dhn_ref[:, ln].astype(F32)[0:rc])
                dgc = dh * val * dge
                sd[pl.ds(r0, rc), ln] = dgc
                if r0 < ts:
                    dv_ref[rows, ln] = (dh * ge).astype(BF16)
                    db_acc = db_acc + jnp.sum(dgc.reshape(rc // 8, 8, SUB_LANES), axis=0)
                    for k in range(3):
                        dw_acc[k] = dw_acc[k] + jnp.sum((dgc * taps[k]).reshape(rc // 8, 8, SUB_LANES), axis=0)
            db_ref[:, ln] += jnp.sum(db_acc, axis=0, keepdims=True)
            for k in range(3):
                dw_ref[k:k + 1, ln] += jnp.sum(dw_acc[k], axis=0, keepdims=True)
            for r0 in range(0, ts, SUB_ROWS):
                ahead = _rows_ahead(sd, r0, SUB_ROWS, ln, (2, 1, 0))
                dg_ref[pl.ds(r0, SUB_ROWS), ln] = (w0 * ahead[0] + w1 * ahead[1] + w2 * ahead[2]).astype(BF16)

    cur = lambda j, i: (i, j)
    prev = lambda j, i: (jnp.maximum(i * nh - 1, 0), j)
    nxt = lambda j, i: (jnp.minimum((i + 1) * nh, S // hb - 1), j)
    return pl.pallas_call(
        body,
        out_shape=[jax.ShapeDtypeStruct((S, F), BF16), jax.ShapeDtypeStruct((S, F), BF16),
                   jax.ShapeDtypeStruct((3, F), F32), jax.ShapeDtypeStruct((1, F), F32)],
        grid=(F // tc, n),
        in_specs=[pl.BlockSpec((ts, tc), cur), pl.BlockSpec((hb, tc), prev), pl.BlockSpec((hb, tc), nxt),
                  pl.BlockSpec((ts, tc), cur), pl.BlockSpec((hb, tc), nxt),
                  pl.BlockSpec((ts, tc), cur), pl.BlockSpec((hb, tc), nxt),
                  pl.BlockSpec((3, tc), lambda j, i: (0, j)), pl.BlockSpec((1, tc), lambda j, i: (0, j))],
        out_specs=[pl.BlockSpec((ts, tc), cur), pl.BlockSpec((ts, tc), cur),
                   pl.BlockSpec((3, tc), lambda j, i: (0, j)), pl.BlockSpec((1, tc), lambda j, i: (0, j))],
        scratch_shapes=[pltpu.VMEM((hb + ts + hb, tc), F32), pltpu.VMEM((te, tc), F32)],
        compiler_params=_cparams(("parallel", "arbitrary")),
        name=name,
    )(gate, gate, gate, val, val, dh, dh, dw_w, dw_b.reshape(1, F))


def _ple_bwd_rows(dz, gate, proj):
    return dz * proj * gate * (1.0 - gate), dz * gate


def _loss_ln_bwd(z, ln_g, ln_b, target, gate, proj, *, ts=512, name):
    S, D = z.shape

    def body(z_ref, g_ref, b_ref, t_ref, gate_ref, proj_ref, dz_ref, dzb_ref, dg_ref, db_ref, loss_ref, ds_ref,
             dp_ref, dbg_ref):
        i = pl.program_id(0)

        @pl.when(i == 0)
        def _():
            dg_ref[...] = jnp.zeros_like(dg_ref)
            db_ref[...] = jnp.zeros_like(db_ref)
            loss_ref[...] = jnp.zeros_like(loss_ref)
            dbg_ref[...] = jnp.zeros_like(dbg_ref)

        dg_acc = jnp.zeros((8, D), F32)
        db_acc = jnp.zeros((8, D), F32)
        dbg_acc = jnp.zeros((8, D), F32)
        loss_acc = jnp.zeros((1, 1), F32)
        for r0 in range(0, ts, LN_ROWS):
            rows = pl.ds(r0, LN_ROWS)
            zt = z_ref[rows, :]
            err = _layer_norm_rows(zt, g_ref[...], b_ref[...]) - t_ref[rows, :]
            loss_acc = loss_acc + 0.5 * jnp.sum(jnp.mean(err * err, axis=-1, keepdims=True), keepdims=True)
            do = err * (1.0 / D)
            dz, xh = _ln_bwd_rows(zt, g_ref[...], do)
            dg_acc = dg_acc + jnp.sum((do * xh).reshape(LN_ROWS // 8, 8, D), axis=0)
            db_acc = db_acc + jnp.sum(do.reshape(LN_ROWS // 8, 8, D), axis=0)
            dz_ref[rows, :] = dz
            dzb_ref[rows, :] = dz.astype(BF16)
            ds, dp = _ple_bwd_rows(dz, gate_ref[rows, :], proj_ref[rows, :])
            ds_ref[rows, :] = ds.astype(BF16)
            dp_ref[rows, :] = dp.astype(BF16)
            dbg_acc = dbg_acc + jnp.sum(ds.reshape(LN_ROWS // 8, 8, D), axis=0)
        dg_ref[...] += jnp.sum(dg_acc, axis=0, keepdims=True)
        db_ref[...] += jnp.sum(db_acc, axis=0, keepdims=True)
        dbg_ref[...] += jnp.sum(dbg_acc, axis=0, keepdims=True)
        loss_ref[...] += loss_acc

    row = pl.BlockSpec((ts, D), lambda i: (i, 0))
    fix = pl.BlockSpec((1, D), lambda i: (0, 0))
    return pl.pallas_call(
        body,
        out_shape=[jax.ShapeDtypeStruct((S, D), F32), jax.ShapeDtypeStruct((S, D), BF16),
                   jax.ShapeDtypeStruct((1, D), F32), jax.ShapeDtypeStruct((1, D), F32),
                   jax.ShapeDtypeStruct((8, 128), F32), jax.ShapeDtypeStruct((S, D), BF16),
                   jax.ShapeDtypeStruct((S, D), BF16), jax.ShapeDtypeStruct((1, D), F32)],
        grid=(S // ts,),
        in_specs=[row, fix, fix, row, row, row],
        out_specs=[row, row, fix, fix, pl.BlockSpec((8, 128), lambda i: (0, 0)), row, row, fix],
        compiler_params=_cparams(("arbitrary",)),
        name=name,
    )(z, ln_g.reshape(1, D), ln_b.reshape(1, D), target, gate, proj)


HEADS_PER_STEP = 4
HEAD_LANES = HEADS_PER_STEP * HEAD_DIM


ATT_ROWS = 32
ATT_SCALE = HEAD_DIM ** -0.5


def _softmax_piece(scores, bias, qb):
    s = scores + bias
    kpos = qb * Q_BLOCK + lax.broadcasted_iota(jnp.int32, (1, KV_SPAN), 1)
    s = jnp.where(kpos >= KV_PAD, s, NEG_INF)
    e = jnp.exp(s - jnp.max(s, axis=-1, keepdims=True))
    return e * (1.0 / jnp.sum(e, axis=-1, keepdims=True))


def _head_masks():
    lane = lax.broadcasted_iota(jnp.int32, (1, HEAD_LANES), 1)
    return [(lane >= j * HEAD_DIM) & (lane < (j + 1) * HEAD_DIM) for j in range(HEADS_PER_STEP)]


def _pick_heads(masks, per_head):
    out = per_head[0]
    for mask, x in zip(masks[1:], per_head[1:]):
        out = jnp.where(mask, x, out)
    return out


def _pad_keys(qb, k_ref, v_ref, kp, vp):
    @pl.when(qb == 0)
    def _():
        kp[pl.ds(0, KV_PAD), :] = jnp.zeros((KV_PAD, HEAD_LANES), BF16)
        vp[pl.ds(0, KV_PAD), :] = jnp.zeros((KV_PAD, HEAD_LANES), BF16)
        kp[pl.ds(KV_PAD, k_ref.shape[0]), :] = k_ref[...]
        vp[pl.ds(KV_PAD, v_ref.shape[0]), :] = v_ref[...]


def _attn_fwd(qkv, bias):
    S = qkv.shape[0]
    nhp = N_HEADS // HEADS_PER_STEP

    def body(q_ref, k_ref, v_ref, b_ref, o_ref, kp, vp, p_scr):
        qb = pl.program_id(1)
        _pad_keys(qb, k_ref, v_ref, kp, vp)
        span = pl.ds(pl.multiple_of(qb * Q_BLOCK, Q_BLOCK), KV_SPAN)
        kc, vc = kp[span, :], vp[span, :]
        qt = q_ref[...] * ATT_SCALE
        mine = _head_masks()
        scores = [_bdot(jnp.where(mine[j], qt, jnp.zeros_like(qt)), kc, NT) for j in range(HEADS_PER_STEP)]
        outs = []
        for j in range(HEADS_PER_STEP):
            for r0 in range(0, Q_BLOCK, ATT_ROWS):
                rows = pl.ds(r0, ATT_ROWS)
                p_scr[j, rows, :] = _softmax_piece(scores[j][r0:r0 + ATT_ROWS], b_ref[j, rows, :], qb).astype(BF16)
            outs.append(_bdot(p_scr[j], vc))
        o_ref[...] = _pick_heads(mine, outs).astype(BF16)

    return pl.pallas_call(
        body,
        out_shape=jax.ShapeDtypeStruct((S, D_MODEL), BF16),
        grid=(nhp, S // Q_BLOCK),
        in_specs=[pl.BlockSpec((Q_BLOCK, HEAD_LANES), lambda h, i: (i, h)),
                  pl.BlockSpec((S, HEAD_LANES), lambda h, i: (0, nhp + h)),
                  pl.BlockSpec((S, HEAD_LANES), lambda h, i: (0, 2 * nhp + h)),
                  pl.BlockSpec((HEADS_PER_STEP, Q_BLOCK, KV_SPAN), lambda h, i: (h, 0, 0))],
        out_specs=pl.BlockSpec((Q_BLOCK, HEAD_LANES), lambda h, i: (i, h)),
        scratch_shapes=[pltpu.VMEM((KV_PAD + S, HEAD_LANES), BF16), pltpu.VMEM((KV_PAD + S, HEAD_LANES), BF16),
                        pltpu.VMEM((HEADS_PER_STEP, Q_BLOCK, KV_SPAN), BF16)],
        compiler_params=_cparams(("parallel", "arbitrary")),
        name="attn_fwd",
    )(qkv, qkv, qkv, bias)


def _attn_bwd(qkv, bias, do):
    S = qkv.shape[0]
    nhp = N_HEADS // HEADS_PER_STEP
    nq = S // Q_BLOCK
    scale = HEAD_DIM ** -0.5

    def body(q_ref, k_ref, v_ref, b_ref, do_ref, dq_ref, dk_ref, dv_ref, db_ref, kp, vp, dka, dva,
             p_scr, ds_scr):
        qb = pl.program_id(1)
        _pad_keys(qb, k_ref, v_ref, kp, vp)

        @pl.when(qb == 0)
        def _():
            dka[...] = jnp.zeros_like(dka)
            dva[...] = jnp.zeros_like(dva)
            db_ref[...] = jnp.zeros_like(db_ref)

        span = pl.ds(pl.multiple_of(qb * Q_BLOCK, Q_BLOCK), KV_SPAN)
        kc, vc = kp[span, :], vp[span, :]
        qt, dot = q_ref[...] * ATT_SCALE, do_ref[...]
        mine = _head_masks()
        dqs = []
        qs = [jnp.where(mine[j], qt, jnp.zeros_like(qt)) for j in range(HEADS_PER_STEP)]
        dos = [jnp.where(mine[j], dot, jnp.zeros_like(dot)) for j in range(HEADS_PER_STEP)]
        scores = [_bdot(qs[j], kc, NT) for j in range(HEADS_PER_STEP)]
        dps = [_bdot(dos[j], vc, NT) for j in range(HEADS_PER_STEP)]
        for j in range(HEADS_PER_STEP):
            qj, doj = qs[j], dos[j]
            for r0 in range(0, Q_BLOCK, ATT_ROWS):
                rows = pl.ds(r0, ATT_ROWS)
                p = _softmax_piece(scores[j][r0:r0 + ATT_ROWS], b_ref[j, rows, :], qb)
                dp = dps[j][r0:r0 + ATT_ROWS]
                ds = p * (dp - jnp.sum(p * dp, axis=-1, keepdims=True))
                db_ref[j, rows, :] += ds
                p_scr[j, rows, :] = p.astype(BF16)
                ds_scr[j, rows, :] = ds.astype(BF16)
            dva[span, :] += _bdot(p_scr[j], doj, TN)
            dqs.append(_bdot(ds_scr[j], kc))
            dka[span, :] += _bdot(ds_scr[j], qj, TN)
        dq_ref[...] = (scale * _pick_heads(mine, dqs)).astype(BF16)

        @pl.when(qb == nq - 1)
        def _():
            dk_ref[...] = dka[pl.ds(KV_PAD, S), :].astype(BF16)
            dv_ref[...] = dva[pl.ds(KV_PAD, S), :].astype(BF16)

    blk = pl.BlockSpec((Q_BLOCK, HEAD_LANES), lambda h, i: (i, h))
    col = pl.BlockSpec((S, HEAD_LANES), lambda h, i: (0, h))
    bsp = pl.BlockSpec((HEADS_PER_STEP, Q_BLOCK, KV_SPAN), lambda h, i: (h, 0, 0))
    return pl.pallas_call(
        body,
        out_shape=[jax.ShapeDtypeStruct((S, D_MODEL), BF16)] * 3
        + [jax.ShapeDtypeStruct((N_HEADS, Q_BLOCK, KV_SPAN), F32)],
        grid=(nhp, nq),
        in_specs=[blk, pl.BlockSpec((S, HEAD_LANES), lambda h, i: (0, nhp + h)),
                  pl.BlockSpec((S, HEAD_LANES), lambda h, i: (0, 2 * nhp + h)), bsp, blk],
        out_specs=[blk, col, col, bsp],
        scratch_shapes=[pltpu.VMEM((KV_PAD + S, HEAD_LANES), BF16), pltpu.VMEM((KV_PAD + S, HEAD_LANES), BF16),
                        pltpu.VMEM((KV_PAD + S, HEAD_LANES), F32), pltpu.VMEM((KV_PAD + S, HEAD_LANES), F32),
                        pltpu.VMEM((HEADS_PER_STEP, Q_BLOCK, KV_SPAN), BF16), pltpu.VMEM((HEADS_PER_STEP, Q_BLOCK, KV_SPAN), BF16)],
        compiler_params=_cparams(("parallel", "arbitrary")),
        name="attn_bwd",
    )(qkv, qkv, qkv, bias, do)


N_DIST = BAND + CHUNK - 1
N_FAR = KV_PAD + CHUNK - MAX_REL


def _shear_rows(x, towards_right):
    row = lax.broadcasted_iota(jnp.int32, (Q_BLOCK, 1), 0)
    for bit in range(Q_BLOCK.bit_length() - 1):
        step = 1 << bit
        x = jnp.where((row & step) != 0, pltpu.roll(x, step if towards_right else KV_SPAN - step, 1), x)
    return x


def _bias_blocks(rel_bias, dep):
    H = rel_bias.shape[0]
    e = jnp.concatenate([jnp.broadcast_to(rel_bias[:, 2 * MAX_REL:], (H, N_FAR)),
                         jnp.flip(rel_bias[:, 2 * MAX_REL - (N_DIST - N_FAR):2 * MAX_REL], axis=1),
                         jnp.zeros((H, KV_SPAN - N_DIST), F32)], axis=1).reshape(H, 1, KV_SPAN)

    def body(e_ref, dep_ref, o_ref):
        first = pltpu.roll(jnp.broadcast_to(e_ref[...], (Q_BLOCK, KV_SPAN)), KV_SPAN - (CHUNK - 1), 1)
        x = _shear_rows(first, True)
        row = lax.broadcasted_iota(jnp.int32, (Q_BLOCK, 1), 0)
        chunk0 = row - (row & (CHUNK - 1))
        k = lax.broadcasted_iota(jnp.int32, (1, KV_SPAN), 1)
        o_ref[...] = jnp.where((k >= chunk0) & (k < chunk0 + BAND), x, NEG_INF)

    return pl.pallas_call(
        body,
        out_shape=jax.ShapeDtypeStruct((H, Q_BLOCK, KV_SPAN), F32),
        grid=(H,),
        in_specs=[pl.BlockSpec((None, 1, KV_SPAN), lambda h: (h, 0, 0)), pl.BlockSpec(memory_space=pl.ANY)],
        out_specs=pl.BlockSpec((None, Q_BLOCK, KV_SPAN), lambda h: (h, 0, 0)),
        compiler_params=_cparams(("parallel",)),
        name="bias_blocks",
    )(e, dep)


def _bias_blocks_grad(dblk):
    H = dblk.shape[0]

    def body(d_ref, o_ref):
        x = pltpu.roll(_shear_rows(d_ref[...], False), CHUNK - 1, 1)
        de = jnp.sum(x, axis=0, keepdims=True)
        lane = lax.broadcasted_iota(jnp.int32, de.shape, 1)
        far = jnp.sum(jnp.where(lane < N_FAR, de, 0.0), axis=-1, keepdims=True)
        o_ref[...] = jnp.where(lane == 0, far, jnp.where(lane < N_FAR, 0.0, de))

    de = pl.pallas_call(
        body,
        out_shape=jax.ShapeDtypeStruct((H, 1, KV_SPAN), F32),
        grid=(H,),
        in_specs=[pl.BlockSpec((None, Q_BLOCK, KV_SPAN), lambda h: (h, 0, 0))],
        out_specs=pl.BlockSpec((None, 1, KV_SPAN), lambda h: (h, 0, 0)),
        compiler_params=_cparams(("parallel",)),
        name="bias_grad_sum",
    )(dblk).reshape(H, KV_SPAN)
    near = jnp.flip(de[:, N_FAR:N_DIST], axis=1)
    return jnp.concatenate([jnp.zeros((H, 2 * MAX_REL - (N_DIST - N_FAR)), F32), near, de[:, 0:1]], axis=1)


def _ffn_forward(r1, r1b, p_l, w, l, ready, after):
    ready(f"up{l}", after)
    up_g = _mm_rows([(r1b, w["ffn_up_t"][l], True, (0, 2))], out_dtype=BF16, name=f"ffn_up_g{l}")
    up_v = _mm_rows([(r1b, w["ffn_up_t"][l], True, (1, 2))], out_dtype=BF16, name=f"ffn_up_v{l}")
    h = _ffn_act_fwd(up_g, up_v, w["ffn_dw_w"][l], w["ffn_dw_b"][l], name=f"ffn_act{l}")
    ready(f"dn{l}", h)
    z2, r2, r2b, gate, proj = _proj_ln(r1, h, w["ffn_w_down"][l], w["ln_ffn_g"][l], w["ln_ffn_b"][l],
                                       ple=(w["ple_w_gate"][l], w["ple_b_gate"][l], p_l, w["ple_w_proj"][l]),
                                       name=f"ffn_down_ln{l}")
    return dict(r1b=r1b, up_g=up_g, up_v=up_v, h=h, z2=z2, gate=gate, proj=proj), r2, r2b


def _ffn_backward(sv, dz2, dz2b, ple_bwd, p_l, w, l, grads, ln_bwd, emit):
    r1b = sv["r1b"]
    ds, dproj, db_gate = ple_bwd
    dh = _mm_rows([(dz2b, w["ffn_w_down"][l], True, WHOLE)], out_dtype=BF16, name=f"ffn_dh{l}")
    dgate, dval, d_dw_w, d_dw_b = _ffn_act_bwd(sv["up_g"], sv["up_v"], dh, w["ffn_dw_w"][l], w["ffn_dw_b"][l],
                                               name=f"ffn_act_bwd{l}")
    grads["ffn_w_down"][l] = _wgrad(sv["h"], dz2b, tm=1408, name=f"d_ffn_w_down{l}")
    d_up_g = _wgrad(dgate, r1b, tm=1408, part=(0, 2), name=f"d_ffn_up_g{l}")
    grads["ffn_up_t"][l] = _wgrad(dval, r1b, tm=1408, part=(1, 2), into=d_up_g, name=f"d_ffn_up_v{l}")
    grads["ple_w_gate"][l] = _wgrad(r1b, ds, name=f"d_ple_w_gate{l}")
    grads["ple_w_proj"][l] = _wgrad(p_l, dproj, piece=D_MODEL // N_DEV, name=f"d_ple_w_proj{l}")
    grads["ffn_dw_w"][l] = d_dw_w
    grads["ffn_dw_b"][l] = d_dw_b[0]
    grads["ple_b_gate"][l] = db_gate[0]
    return _mm_rows([(ds, w["ple_w_gate"][l], True, WHOLE), (dgate, w["ffn_up_t"][l], False, (0, 2)),
                     (dval, w["ffn_up_t"][l], False, (1, 2))], add=dz2, add_scale=ALPHA, ln_bwd=ln_bwd, dep=emit(),
                    name=f"dr1_{l}")


def _local_step(x, p, target, w, ready=lambda group, after: None, emit=lambda group, grads: None):
    grads = {k: [None, None] for k in ("ffn_w_down", "ffn_up_t", "ple_w_gate", "ple_w_proj", "ffn_dw_w",
                                       "ffn_dw_b", "ple_b_gate", "ln_ffn_g", "ln_ffn_b", "ln_mix_g", "ln_mix_b")}

    xb, pb = x.astype(BF16), p.astype(BF16)
    ready("mix", None)
    u = _mm_rows([(xb, w["mix_w_in_t"], True, WHOLE)], name="mix_in")
    ycat, dpool, hconv = _mixer_fwd(u, w["pool_w"], w["pool_scale"], w["conv_dw_w"], w["conv_dw_b"], w["conv_ln_g"],
                                    w["conv_ln_b"])
    ready("mixo", ycat)
    z1, r1, r1b = _proj_ln(x, ycat, w["mix_w_out"], w["ln_mix_g"][0], w["ln_mix_b"][0], name="mix_out_ln")
    bias = _bias_blocks(w["attn_rel_bias"], r1b)
    sv0, r2, r2b = _ffn_forward(r1, r1b, pb[0], w, 0, ready, bias)

    ready("attn", r2b)
    qkv = _mm_rows([(r2b, w["attn_w_qkv"], False, WHOLE)], out_dtype=BF16, name="attn_qkv")
    attn = _attn_fwd(qkv, bias)
    z3, r3, r3b = _proj_ln(r2, attn, w["attn_w_o"], w["ln_mix_g"][1], w["ln_mix_b"][1], name="attn_out_ln")
    sv1, _, _ = _ffn_forward(r3, r3b, pb[1], w, 1, ready, r3b)

    dz4, dz4b, grads["ln_ffn_g"][1], grads["ln_ffn_b"][1], loss, *ple1 = _loss_ln_bwd(
        sv1["z2"], w["ln_ffn_g"][1], w["ln_ffn_b"][1], target, sv1["gate"], sv1["proj"], name="loss_ln_bwd")
    dz3, dz3b, grads["ln_mix_g"][1], grads["ln_mix_b"][1] = _ffn_backward(
        sv1, dz4, dz4b, ple1, pb[1], w, 1, grads, (z3, w["ln_mix_g"][1]), lambda: emit("ffn1", grads))
    grads["attn_w_o"] = _wgrad(attn, dz3b, name="d_attn_w_o")
    dattn = _mm_rows([(dz3b, w["attn_w_o"], True, WHOLE)], out_dtype=BF16, name="d_attn")
    dq, dk, dv, dbias = _attn_bwd(qkv, bias, dattn)
    grads["attn_rel_bias"] = _bias_blocks_grad(dbias)
    dqkv = jnp.concatenate([dq, dk, dv], axis=1)
    grads["attn_w_qkv"] = _wgrad(r2b, dqkv, tn=768, piece=3 * D_MODEL // N_DEV, name="d_attn_w_qkv")
    dz2, dz2b, grads["ln_ffn_g"][0], grads["ln_ffn_b"][0], *ple0 = _mm_rows(
        [(dqkv, w["attn_w_qkv"], True, WHOLE)], add=dz3, add_scale=ALPHA,
        ln_bwd=(sv0["z2"], w["ln_ffn_g"][0], sv0["gate"], sv0["proj"]), dep=emit("attn", grads), name="dr2")
    dz1, dz1b, grads["ln_mix_g"][0], grads["ln_mix_b"][0] = _ffn_backward(
        sv0, dz2, dz2b, ple0, pb[0], w, 0, grads, (z1, w["ln_mix_g"][0]), lambda: emit("ffn0", grads))
    grads["mix_w_out"] = _wgrad(ycat, dz1b, name="d_mix_w_out")
    dycat = _mm_rows([(dz1b, w["mix_w_out"], True, WHOLE)], name="d_ycat")
    du, g_pw, g_ps, g_cw, g_cb, g_cg, g_cbb = _mixer_bwd(u, dpool, hconv, dycat, w["pool_w"], w["pool_scale"],
                                                         w["conv_dw_w"], w["conv_ln_g"], w["conv_ln_b"])
    grads["mix_w_in_t"] = _wgrad(du, xb, name="d_mix_w_in")
    grads.update(pool_w=g_pw, pool_scale=g_ps[0], conv_dw_w=g_cw, conv_dw_b=g_cb[0], conv_ln_g=g_cg[0],
                 conv_ln_b=g_cbb[0])
    for kname in ("ln_ffn_g", "ln_ffn_b", "ln_mix_g", "ln_mix_b"):
        grads[kname] = [a[0] for a in grads[kname]]
    grad_x = _mm_rows([(du, w["mix_w_in_t"], False, WHOLE)], add=dz1, add_scale=ALPHA, dep=emit("mix", grads),
                      name="grad_x")
    return loss[0, 0], grad_x, grads


_HBM = pl.BlockSpec(memory_space=pltpu.HBM)
_SEM = pl.BlockSpec(memory_space=pltpu.SEMAPHORE)
_EFFECT = pltpu.SideEffectType.DATAFLOW_SIDE_EFFECTING


def _slot(ref, place, shape, k):
    if place in ("stack", "pieces"):
        return ref.at[k]
    ax = place[1]
    n = shape[ax]
    return ref.at[(slice(None),) * ax + (pl.ds(pl.multiple_of(k * n, n), n),)]


def _result_shape(buf, place):
    if place == "stack":
        return (N_DEV,) + buf.shape
    if place == "pieces":
        return buf.shape
    return tuple(s * N_DEV if i == place[1] else s for i, s in enumerate(buf.shape))


def _peers(x, y, c):
    for d in range(1, N_DEV):
        px, py, pc = x ^ ((d >> 2) & 1), y ^ ((d >> 1) & 1), c ^ (d & 1)
        yield d, (px, py, pc), 4 * px + 2 * py + pc


def _exchange_start(bufs, places, after, *, name):
    nb = len(bufs)
    lands = [lax.empty(_result_shape(b, p_), b.dtype) for b, p_ in zip(bufs, places)]
    has_after = after is not None

    def body(*refs):
        srcs, dsts = refs[:nb], refs[nb:2 * nb]
        outs = refs[2 * nb + has_after:]
        send_sems, recv_sems, token = outs[0], outs[1], outs[2 + 2 * nb]
        x, y, c = lax.axis_index("x"), lax.axis_index("y"), lax.axis_index("c")
        me = 4 * x + 2 * y + c
        for b in range(nb):
            for d, dev, peer in _peers(x, y, c):
                pltpu.make_async_remote_copy(
                    src_ref=srcs[b].at[peer] if places[b] == "pieces" else srcs[b],
                    dst_ref=_slot(dsts[b], places[b], bufs[b].shape, me),
                    send_sem=send_sems.at[b * N_DEV + d], recv_sem=recv_sems.at[b * N_DEV + d],
                    device_id=dev, device_id_type=pl.DeviceIdType.MESH).start()
            pltpu.make_async_copy(srcs[b].at[me] if places[b] == "pieces" else srcs[b],
                                  _slot(dsts[b], places[b], bufs[b].shape, me), recv_sems.at[b * N_DEV]).start()
        token[...] = jnp.zeros_like(token)

    sems = pltpu.SemaphoreType.DMA((nb * N_DEV,))
    ins = [pltpu.with_memory_space_constraint(a, pltpu.HBM) for a in list(bufs) + lands]
    out = pl.pallas_call(
        body,
        out_shape=(sems, sems, *[pltpu.HBM(a.shape, a.dtype) for a in ins], jax.ShapeDtypeStruct((8, 128), F32)),
        in_specs=[_HBM] * (2 * nb) + ([pl.BlockSpec(memory_space=pl.ANY)] if has_after else []),
        out_specs=(_SEM, _SEM, *[_HBM] * (2 * nb), pl.BlockSpec(memory_space=pltpu.VMEM)),
        input_output_aliases={i: 2 + i for i in range(2 * nb)},
        compiler_params=pltpu.CompilerParams(has_side_effects=_EFFECT),
        name=name,
    )(*ins, *([after] if has_after else []))
    return dict(send=out[0], recv=out[1], srcs=out[2:2 + nb], lands=out[2 + nb:2 + 2 * nb], token=out[-1],
                places=places)


def _exchange_wait(h, after, *, name):
    nb = len(h["srcs"])
    places = h["places"]
    shapes = [a.shape for a in h["srcs"]]

    def body(*refs):
        srcs, dsts, send_sems, recv_sems = refs[:nb], refs[nb:2 * nb], refs[2 * nb], refs[2 * nb + 1]
        x, y, c = lax.axis_index("x"), lax.axis_index("y"), lax.axis_index("c")
        me = 4 * x + 2 * y + c
        for b in range(nb):
            pieces = places[b] == "pieces"
            for d, dev, peer in _peers(x, y, c):
                cp = pltpu.make_async_remote_copy(
                    src_ref=srcs[b].at[peer] if pieces else srcs[b],
                    dst_ref=_slot(dsts[b], places[b], shapes[b], peer),
                    send_sem=send_sems.at[b * N_DEV + d], recv_sem=recv_sems.at[b * N_DEV + d],
                    device_id=dev, device_id_type=pl.DeviceIdType.MESH)
                cp.wait_send()
                cp.wait_recv()
            pltpu.make_async_copy(srcs[b].at[me] if pieces else srcs[b], _slot(dsts[b], places[b], shapes[b], me),
                                  recv_sems.at[b * N_DEV]).wait()

    ins = list(h["srcs"]) + list(h["lands"])
    out = pl.pallas_call(
        body,
        out_shape=tuple(pltpu.HBM(a.shape, a.dtype) for a in ins),
        in_specs=[_HBM] * (2 * nb) + [_SEM, _SEM, pl.BlockSpec(memory_space=pl.ANY)],
        out_specs=tuple([_HBM] * (2 * nb)),
        input_output_aliases={i: i for i in range(2 * nb)},
        compiler_params=pltpu.CompilerParams(has_side_effects=_EFFECT),
        name=name,
    )(*ins, h["send"], h["recv"], after)
    return out[nb:]


def _adamw(recv, w, m, v, *, layer=0, into=None, name):
    L, R, C = w.shape
    fits = [d for d in range(16, R + 1, 16) if R % d == 0 and d * C * 4 <= 2 * 1024 * 1024]
    tr = fits[-1] if fits else R
    c1 = 1.0 - ADAM_B1 ** ADAM_STEP
    c2 = 1.0 - ADAM_B2 ** ADAM_STEP

    def body(r_ref, w_ref, m_ref, v_ref, *rest):
        g_ref, d_ref, mo_ref, vo_ref = rest[-4:]
        g = r_ref[0].astype(F32)
        for i in range(1, N_DEV):
            g = g + r_ref[i].astype(F32)
        m_new = ADAM_B1 * m_ref[...] + (1.0 - ADAM_B1) * g
        v_new = ADAM_B2 * v_ref[...] + (1.0 - ADAM_B2) * (g * g)
        m_hat = m_new / c1
        v_hat = v_new / c2
        g_ref[...] = g
        d_ref[...] = -ADAM_LR * (m_hat / (jnp.sqrt(v_hat) + ADAM_EPS) + ADAM_WD * w_ref[...])
        mo_ref[...] = m_new
        vo_ref[...] = v_new

    row = pl.BlockSpec((None, tr, C), lambda i: (layer, i, 0))
    others = [] if into is None else list(into)
    return pl.pallas_call(
        body,
        out_shape=[jax.ShapeDtypeStruct((L, R, C), F32)] * 4,
        grid=(R // tr,),
        in_specs=[pl.BlockSpec((N_DEV, tr, C), lambda i: (0, i, 0)), row, row, row]
        + [pl.BlockSpec(memory_space=pl.ANY)] * len(others),
        out_specs=[row] * 4,
        input_output_aliases={4 + k: k for k in range(len(others))},
        compiler_params=_cparams(("parallel",)),
        name=name,
    )(recv, w, m, v, *others)


_TRANSPOSED = ("mix_w_in", "ffn_w_up")


def _ffn_groups(l):
    return ((f"up{l}", (("ffn_w_up", l, BF16, ("axis", 0)), ("ffn_dw_w", l, F32, "stack"))),
            (f"dn{l}", (("ffn_w_down", l, BF16, ("axis", 0)), ("ple_w_gate", l, BF16, ("axis", 0)),
                        ("ple_w_proj", l, BF16, ("axis", 1)))))


_GATHER_GROUPS = (
    ("mix", (("mix_w_in", 0, BF16, ("axis", 0)), ("conv_dw_w", 0, F32, "stack"))),
    ("mixo", (("mix_w_out", 0, BF16, ("axis", 0)),)),
    *_ffn_groups(0),
    ("attn", (("attn_w_qkv", 0, BF16, ("axis", 1)), ("attn_w_o", 0, BF16, ("axis", 0)))),
    *_ffn_groups(1))
_SHARDED = ("mix_w_in", "conv_dw_w", "mix_w_out", "attn_w_qkv", "attn_w_o", "ffn_w_up", "ffn_dw_w", "ffn_w_down",
            "ple_w_gate", "ple_w_proj")
_REPLICATED = ("pool_w", "pool_scale", "conv_dw_b", "conv_ln_g", "conv_ln_b", "attn_rel_bias", "ln_mix_g",
               "ln_mix_b", "ffn_dw_b", "ple_b_gate", "ln_ffn_g", "ln_ffn_b")


def _pack_rows(parts, row_mult, dtype):
    lead = parts[0].shape[:-1]
    flat = jnp.concatenate([a.astype(dtype) for a in parts], axis=-1)
    n = flat.shape[-1]
    unit = row_mult * LANES
    padded = -(-n // unit) * unit
    flat = jnp.pad(flat, [(0, 0)] * len(lead) + [(0, padded - n)])
    return flat.reshape(lead + (padded // LANES, LANES))


def _unpack(flat2d, shapes):
    flat = flat2d.reshape(-1)
    out, o = [], 0
    for s in shapes:
        n = math.prod(s)
        out.append(flat[o:o + n].reshape(s))
        o += n
    return out


def _full_from_shards(g, axis):
    parts = jnp.moveaxis(g, 0, axis)
    shp = list(g.shape[1:])
    shp[axis] *= g.shape[0]
    return parts.reshape(shp)


def _pieces_from_full(full, axis, k=N_DEV):
    shp = list(full.shape)
    n = shp[axis] // k
    t = full.reshape(shp[:axis] + [k, n] + shp[axis + 1:])
    return jnp.moveaxis(t, axis, 0)


def kernel(x, p, mix_w_in, pool_w, pool_scale, conv_dw_w, conv_dw_b, conv_ln_g, conv_ln_b, mix_w_out, attn_w_qkv, attn_rel_bias, attn_w_o, ln_mix_g, ln_mix_b, ffn_w_up, ffn_dw_w, ffn_dw_b, ffn_w_down, ple_w_proj, ple_w_gate, ple_b_gate, ln_ffn_g, ln_ffn_b, loss_target, m_mix_w_in, m_pool_w, m_pool_scale, m_conv_dw_w, m_conv_dw_b, m_conv_ln_g, m_conv_ln_b, m_mix_w_out, m_attn_w_qkv, m_attn_rel_bias, m_attn_w_o, m_ln_mix_g, m_ln_mix_b, m_ffn_w_up, m_ffn_dw_w, m_ffn_dw_b, m_ffn_w_down, m_ple_w_proj, m_ple_w_gate, m_ple_b_gate, m_ln_ffn_g, m_ln_ffn_b, v_mix_w_in, v_pool_w, v_pool_scale, v_conv_dw_w, v_conv_dw_b, v_conv_ln_g, v_conv_ln_b, v_mix_w_out, v_attn_w_qkv, v_attn_rel_bias, v_attn_w_o, v_ln_mix_g, v_ln_mix_b, v_ffn_w_up, v_ffn_dw_w, v_ffn_dw_b, v_ffn_w_down, v_ple_w_proj, v_ple_w_gate, v_ple_b_gate, v_ln_ffn_g, v_ln_ffn_b):
    a = dict(locals())
    sh_names = list(_SHARDED)
    names = sh_names + list(_REPLICATED)
    wts = {n: a[n] for n in names}
    mom = {n: a["m_" + n] for n in names}
    var = {n: a["v_" + n] for n in names}

    for n in _TRANSPOSED:
        wts[n], mom[n], var[n] = (jnp.swapaxes(d[n], 1, 2) for d in (wts, mom, var))
    gather = {}
    token = None
    for group, items in _GATHER_GROUPS:
        gather[group] = _exchange_start([wts[n][l].astype(dt) for n, l, dt, _ in items], [pl_ for *_, pl_ in items],
                                        token, name="gather_start_" + group)
        token = gather[group]["token"]

    w = dict(pool_w=pool_w[0], pool_scale=pool_scale[0], conv_dw_b=conv_dw_b[0], conv_ln_g=conv_ln_g[0],
             conv_ln_b=conv_ln_b[0], attn_rel_bias=attn_rel_bias[0], ln_mix_g=ln_mix_g, ln_mix_b=ln_mix_b,
             ffn_dw_b=ffn_dw_b, ple_b_gate=ple_b_gate, ln_ffn_g=ln_ffn_g, ln_ffn_b=ln_ffn_b)
    for n in ("ffn_up_t", "ffn_dw_w", "ffn_w_down", "ple_w_gate", "ple_w_proj"):
        w[n] = [None, None]

    def ready(group, after):
        got = _exchange_wait(gather[group], token if after is None else after, name="gather_wait_" + group)
        if group == "mix":
            w["mix_w_in_t"], w["conv_dw_w"] = got[0], _full_from_shards(got[1], 1)
        elif group == "mixo":
            (w["mix_w_out"],) = got
        elif group == "attn":
            w["attn_w_qkv"], w["attn_w_o"] = got
        elif group[:2] == "up":
            l = int(group[2])
            w["ffn_up_t"][l], w["ffn_dw_w"][l] = got[0], _full_from_shards(got[1], 1)
        else:
            l = int(group[2])
            w["ffn_w_down"][l], w["ple_w_gate"][l], w["ple_w_proj"][l] = got

    scatter = {}

    def emit(group, gr):
        if group[:3] == "ffn":
            l = int(group[3])
            pieces = [_pieces_from_full(gr["ffn_up_t"][l], 0),
                      _pieces_from_full(gr["ffn_dw_w"][l], 1), _pieces_from_full(gr["ffn_w_down"][l], 0),
                      _pieces_from_full(gr["ple_w_gate"][l], 0), gr["ple_w_proj"][l]]
        elif group == "attn":
            pieces = [gr["attn_w_qkv"], _pieces_from_full(gr["attn_w_o"], 0)]
        else:
            pieces = [_pieces_from_full(gr["mix_w_in_t"], 0), _pieces_from_full(gr["conv_dw_w"], 1),
                      _pieces_from_full(gr["mix_w_out"], 0)]
        scatter[group] = _exchange_start([a.astype(BF16) for a in pieces], ["pieces"] * len(pieces), None,
                                         name="grad_start_" + group)
        if group != "mix":
            return scatter[group]["token"]
        gfull = dict(
            pool_w=gr["pool_w"][None], pool_scale=gr["pool_scale"][None], conv_dw_b=gr["conv_dw_b"][None],
            conv_ln_g=gr["conv_ln_g"][None], conv_ln_b=gr["conv_ln_b"][None],
            attn_rel_bias=gr["attn_rel_bias"][None], ln_mix_g=jnp.stack(gr["ln_mix_g"]),
            ln_mix_b=jnp.stack(gr["ln_mix_b"]), ffn_dw_b=jnp.stack(gr["ffn_dw_b"]),
            ple_b_gate=jnp.stack(gr["ple_b_gate"]), ln_ffn_g=jnp.stack(gr["ln_ffn_g"]),
            ln_ffn_b=jnp.stack(gr["ln_ffn_b"]))
        rep_send = _pack_rows([gfull[n].reshape(-1) for n in _REPLICATED], 8, F32)
        scatter["replicated"] = _exchange_start([rep_send], ["stack"], scatter[group]["token"],
                                                name="grad_start_replicated")
        return scatter["replicated"]["token"]

    loss_part, grad_x, gr = _local_step(x[0], p[:, 0], loss_target[0], w, ready, emit)
    loss = lax.psum(loss_part, ("x", "y", "c"))

    group_weights = {"ffn1": (("ffn_w_up", 1), ("ffn_dw_w", 1), ("ffn_w_down", 1), ("ple_w_gate", 1), ("ple_w_proj", 1)),
                     "attn": (("attn_w_qkv", 0), ("attn_w_o", 0)),
                     "ffn0": (("ffn_w_up", 0), ("ffn_dw_w", 0), ("ffn_w_down", 0), ("ple_w_gate", 0), ("ple_w_proj", 0)),
                     "mix": (("mix_w_in", 0), ("conv_dw_w", 0), ("mix_w_out", 0))}
    updated = {}
    after = grad_x
    for group in ("ffn1", "attn", "ffn0", "mix"):
        recv = _exchange_wait(scatter[group], after, name="grad_wait_" + group)
        for (n, l), r in zip(group_weights[group], recv):
            updated[n] = _adamw(r, wts[n], mom[n], var[n], layer=l, into=updated.get(n), name=f"adamw_{n}{l}")
            after = updated[n][0]
    res = [{n: jnp.swapaxes(updated[n][k], 1, 2) if n in _TRANSPOSED else updated[n][k] for n in sh_names}
           for k in range(4)]
    (rep_recv,) = _exchange_wait(scatter["replicated"], after, name="grad_wait_replicated")

    def flat_state(d):
        return _pack_rows([d[n].reshape(-1) for n in _REPLICATED], 8, F32)[None]

    rep_out = _adamw(rep_recv, flat_state(wts), flat_state(mom), flat_state(var), name="adamw_replicated")
    for k in range(4):
        for n, arr in zip(_REPLICATED, _unpack(rep_out[k][0], [wts[n].shape for n in _REPLICATED])):
            res[k][n] = arr
    order = ["mix_w_in", "pool_w", "pool_scale", "conv_dw_w", "conv_dw_b", "conv_ln_g", "conv_ln_b", "mix_w_out",
             "attn_w_qkv", "attn_rel_bias", "attn_w_o", "ln_mix_g", "ln_mix_b", "ffn_w_up", "ffn_dw_w", "ffn_dw_b",
             "ffn_w_down", "ple_w_proj", "ple_w_gate", "ple_b_gate", "ln_ffn_g", "ln_ffn_b"]
    outs = [loss, grad_x[None]]
    for k in range(4):
        outs += [res[k][n] for n in order]
    return tuple(outs)
```

```python
import functools
import math

import jax
import jax.numpy as jnp
from jax import lax
from jax.experimental import pallas as pl
from jax.experimental.pallas import tpu as pltpu

F32 = jnp.float32
BF16 = jnp.bfloat16

N_DEV = 8
D_MODEL = 1024
D_POOL = 512
D_CONV = 512
POOL_WINDOWS = (2, 4, 8, 16)
POOL_GROUP = 128
CONV_KERNEL = 31
CHUNK = 64
HEAD_DIM = 64
N_HEADS = 16
LEFT_CHUNKS = 8
BAND = (LEFT_CHUNKS + 1) * CHUNK
MAX_REL = 256
D_FF = 2816
PLE_DIM = 256
ALPHA = 4.0 ** 0.25
LN_EPS = 1e-5
NEG_INF = -1e30
ADAM_LR, ADAM_B1, ADAM_B2, ADAM_EPS, ADAM_WD, ADAM_STEP = 0.001, 0.9, 0.999, 1e-08, 0.01, 10

Q_BLOCK = 4 * CHUNK
KV_PAD = LEFT_CHUNKS * CHUNK
KV_SPAN = KV_PAD + Q_BLOCK
CONV_HALO = 32
FFN_HALO = 16
SUB_ROWS, SUB_LANES = 64, 128
LANES = 1024
VMEM_LIMIT = 56 * 1024 * 1024


def _cparams(sem=None):
    return pltpu.CompilerParams(dimension_semantics=sem, vmem_limit_bytes=VMEM_LIMIT)


def _tile(dim, pref):
    if dim <= pref:
        return dim
    t = pref - pref % 128
    while t >= 128:
        if dim % t == 0:
            return t
        t -= 128
    return dim


def _sigmoid(x):
    return 1.0 / (1.0 + jnp.exp(-x))


def _bdot(a, b, dn=(((1,), (0,)), ((), ()))):
    return lax.dot_general(a.astype(BF16), b.astype(BF16), dn, preferred_element_type=F32)


WHOLE = (0, 1)
NT = (((1,), (1,)), ((), ()))
TN = (((0,), (0,)), ((), ()))


def _wgrad(a, b, *, tm=1024, tn=1024, tk=2048, piece=None, part=(0, 1), into=None, name):
    K, M = a.shape
    kb, N = b.shape
    assert K == kb, (a.shape, b.shape)
    tm, tn, tk = _tile(M, tm), _tile(N, tn), _tile(K, tk)
    nk = K // tk
    per = 1 if piece is None else tn // piece
    assert piece is None or tn == per * piece

    def body(a_ref, b_ref, *rest):
        o_ref, acc = rest[-2:]
        k = pl.program_id(2)

        @pl.when(k == 0)
        def _():
            acc[...] = jnp.zeros_like(acc)

        acc[...] += _bdot(a_ref[...], b_ref[...], TN)

        @pl.when(k == nk - 1)
        def _():
            if piece is None:
                o_ref[...] = acc[...].astype(BF16)
            else:
                for s in range(per):
                    o_ref[s] = acc[:, s * piece:(s + 1) * piece].astype(BF16)

    if piece is None:
        first = part[0] * (M // tm)
        out_shape = (part[1] * M, N)
        out_spec = pl.BlockSpec((tm, tn), lambda i, j, k: (first + i, j))
    else:
        out_shape, out_spec = (N // piece, M, piece), pl.BlockSpec((per, tm, piece), lambda i, j, k: (j, i, 0))
    others = [] if into is None else [into]
    return pl.pallas_call(
        body,
        out_shape=jax.ShapeDtypeStruct(out_shape, BF16),
        grid=(M // tm, N // tn, nk),
        in_specs=[pl.BlockSpec((tk, tm), lambda i, j, k: (k, i)), pl.BlockSpec((tk, tn), lambda i, j, k: (k, j))]
        + [pl.BlockSpec(memory_space=pl.ANY)] * len(others),
        out_specs=out_spec,
        input_output_aliases={2: 0} if others else {},
        scratch_shapes=[pltpu.VMEM((tm, tn), F32)],
        compiler_params=_cparams(("parallel", "parallel", "arbitrary")),
        name=name,
    )(a, b, *others)


def _mm_rows(pairs, *, add=None, add_scale=1.0, out_dtype=F32, tm=512, dep=None, ln_bwd=None, name):
    M = pairs[0][0].shape[0]
    n = len(pairs)
    has_add = add is not None
    has_ple = ln_bwd is not None and len(ln_bwd) == 4
    w_rows = [w_.shape[0] // part[1] for _, w_, _, part in pairs]
    N = w_rows[0] if pairs[0][2] else pairs[0][1].shape[1]

    def body(*refs):
        acc = None
        for i, (_, _, tr, _) in enumerate(pairs):
            part = _bdot(refs[2 * i][...], refs[2 * i + 1][...], NT if tr else (((1,), (0,)), ((), ())))
            acc = part if acc is None else acc + part
        if has_add:
            acc = acc + add_scale * refs[2 * n][...]
        if ln_bwd is None:
            refs[-1][...] = acc.astype(out_dtype)
            return
        first = 2 * n + has_add
        z_ref, g_ref = refs[first], refs[first + 1]
        outs = refs[-(7 if has_ple else 4):]
        dz_ref, dzb_ref, dg_ref, db_ref = outs[:4]

        @pl.when(pl.program_id(0) == 0)
        def _():
            for sums in outs[2:4] + outs[6:]:
                sums[...] = jnp.zeros_like(sums)

        dg_acc = jnp.zeros((8, N), F32)
        db_acc = jnp.zeros((8, N), F32)
        dbg_acc = jnp.zeros((8, N), F32)
        for r0 in range(0, tm, LN_ROWS):
            rows = pl.ds(r0, LN_ROWS)
            do = acc[r0:r0 + LN_ROWS]
            dz, xh = _ln_bwd_rows(z_ref[rows, :], g_ref[...], do)
            dz_ref[rows, :] = dz
            dzb_ref[rows, :] = dz.astype(BF16)
            dg_acc = dg_acc + jnp.sum((do * xh).reshape(LN_ROWS // 8, 8, N), axis=0)
            db_acc = db_acc + jnp.sum(do.reshape(LN_ROWS // 8, 8, N), axis=0)
            if has_ple:
                ds, dp = _ple_bwd_rows(dz, refs[first + 2][rows, :], refs[first + 3][rows, :])
                outs[4][rows, :] = ds.astype(BF16)
                outs[5][rows, :] = dp.astype(BF16)
                dbg_acc = dbg_acc + jnp.sum(ds.reshape(LN_ROWS // 8, 8, N), axis=0)
        dg_ref[...] += jnp.sum(dg_acc, axis=0, keepdims=True)
        db_ref[...] += jnp.sum(db_acc, axis=0, keepdims=True)
        if has_ple:
            outs[6][...] += jnp.sum(dbg_acc, axis=0, keepdims=True)

    in_specs, args = [], []
    for (a, w_, _, part), rows in zip(pairs, w_rows):
        in_specs += [pl.BlockSpec((tm, a.shape[1]), lambda i: (i, 0)),
                     pl.BlockSpec((rows, w_.shape[1]), functools.partial(lambda i, j: (j, 0), j=part[0]))]
        args += [a, w_]
    row = pl.BlockSpec((tm, N), lambda i: (i, 0))
    fix = pl.BlockSpec((1, N), lambda i: (0, 0))
    if has_add:
        in_specs.append(row)
        args.append(add)
    if ln_bwd is not None:
        in_specs += [row, fix] + [row] * (len(ln_bwd) - 2)
        args += [ln_bwd[0], ln_bwd[1].reshape(1, N), *ln_bwd[2:]]
    if dep is not None:
        in_specs.append(pl.BlockSpec(memory_space=pl.ANY))
        args.append(dep)
    if ln_bwd is None:
        out_shape, out_specs = jax.ShapeDtypeStruct((M, N), out_dtype), row
    else:
        out_shape = [jax.ShapeDtypeStruct((M, N), F32), jax.ShapeDtypeStruct((M, N), BF16),
                     jax.ShapeDtypeStruct((1, N), F32), jax.ShapeDtypeStruct((1, N), F32)]
        out_specs = [row, row, fix, fix]
        if has_ple:
            out_shape += [jax.ShapeDtypeStruct((M, N), BF16), jax.ShapeDtypeStruct((M, N), BF16),
                          jax.ShapeDtypeStruct((1, N), F32)]
            out_specs += [row, row, fix]
    return pl.pallas_call(
        body,
        out_shape=out_shape,
        grid=(M // tm,),
        in_specs=in_specs,
        out_specs=out_specs,
        compiler_params=_cparams(("parallel",) if ln_bwd is None else ("arbitrary",)),
        name=name,
    )(*args)


def _ln_bwd_rows(zt, g, do):
    zc = zt - jnp.mean(zt, axis=-1, keepdims=True)
    rstd = lax.rsqrt(jnp.mean(zc * zc, axis=-1, keepdims=True) + LN_EPS)
    xh = zc * rstd
    dxh = do * g
    return rstd * (dxh - jnp.mean(dxh, axis=-1, keepdims=True) - xh * jnp.mean(dxh * xh, axis=-1, keepdims=True)), xh


def _layer_norm_rows(z, g, b):
    mu = jnp.mean(z, axis=-1, keepdims=True)
    zc = z - mu
    var = jnp.mean(zc * zc, axis=-1, keepdims=True)
    return zc * lax.rsqrt(var + LN_EPS) * g + b


def _proj_ln(res, a, w, ln_g, ln_b, *, ple=None, ts=512, name):
    S, D = res.shape
    ka = a.shape[1]
    has_ple = ple is not None
    row = lambda i: (i, 0)
    fix = lambda i: (0, 0)

    def body(*refs):
        if has_ple:
            (res_ref, a_ref, w_ref, g_ref, b_ref, wg_ref, bg_ref, p_ref, wp_ref, z_ref, r_ref, rb_ref, gate_ref,
             proj_ref, acc) = refs
        else:
            res_ref, a_ref, w_ref, g_ref, b_ref, z_ref, r_ref, rb_ref, acc = refs
        acc[...] = _bdot(a_ref[...], w_ref[...])
        if has_ple:
            gate_ref[...] = _bdot(res_ref[...], wg_ref[...])
            proj_ref[...] = _bdot(p_ref[...], wp_ref[...])
        for r0 in range(0, ts, LN_ROWS):
            rows = pl.ds(r0, LN_ROWS)
            z = ALPHA * res_ref[rows, :] + acc[rows, :]
            if has_ple:
                gate = _sigmoid(gate_ref[rows, :] + bg_ref[...])
                gate_ref[rows, :] = gate
                z = z + gate * proj_ref[rows, :]
            z_ref[rows, :] = z
            r = _layer_norm_rows(z, g_ref[...], b_ref[...])
            r_ref[rows, :] = r
            rb_ref[rows, :] = r.astype(BF16)

    in_specs = [pl.BlockSpec((ts, D), row), pl.BlockSpec((ts, ka), row), pl.BlockSpec((ka, D), fix),
                pl.BlockSpec((1, D), fix), pl.BlockSpec((1, D), fix)]
    args = [res, a, w, ln_g.reshape(1, D), ln_b.reshape(1, D)]
    out_dtypes = [F32, F32, BF16]
    if has_ple:
        wg, bg, p, wp = ple
        in_specs += [pl.BlockSpec((D, D), fix), pl.BlockSpec((1, D), fix), pl.BlockSpec((ts, PLE_DIM), row),
                     pl.BlockSpec((PLE_DIM, D), fix)]
        args += [wg, bg.reshape(1, D), p, wp]
        out_dtypes += [F32, F32]
    return pl.pallas_call(
        body,
        out_shape=[jax.ShapeDtypeStruct((S, D), dt) for dt in out_dtypes],
        grid=(S // ts,),
        in_specs=in_specs,
        out_specs=[pl.BlockSpec((ts, D), row)] * len(out_dtypes),
        scratch_shapes=[pltpu.VMEM((ts, D), F32)],
        compiler_params=_cparams(("parallel",)),
        name=name,
    )(*args)


CONV_ROWS = 32
LN_ROWS = 16


def _shifted_copies(src, dst, rows):
    for c0 in range(0, src.shape[1], SUB_LANES):
        ln = pl.ds(c0, SUB_LANES)
        for r0 in range(0, rows, SUB_ROWS):
            rc = min(SUB_ROWS, rows - r0)
            for b, shifted in enumerate(_rows_ahead(src, r0, rc, ln, range(1, 8))):
                dst[b, pl.ds(r0, rc), ln] = shifted


def _rows_at(src, copies, off, n, ln):
    b = off % 8
    return src[pl.ds(off, n), ln] if b == 0 else copies[b - 1, pl.ds(off - b, n), ln]


def _conv31(stg, gsh, cw_ref, cb_ref, out, rows, first_off):
    for c0 in range(0, D_CONV, SUB_LANES):
        ln = pl.ds(c0, SUB_LANES)
        for r0 in range(0, rows, CONV_ROWS):
            acc = jnp.zeros((CONV_ROWS, SUB_LANES), F32) + cb_ref[:, ln]
            for k in range(CONV_KERNEL):
                acc = acc + cw_ref[k:k + 1, ln] * _rows_at(stg, gsh, first_off + k + r0, CONV_ROWS, ln)
            out[pl.ds(r0, CONV_ROWS), ln] = acc


def _mixer_fwd(u, pool_w, pool_scale, conv_w, conv_b, cln_g, cln_b, *, ts=256):
    S = u.shape[0]
    hb = CONV_HALO
    nh = ts // hb

    def body(u_ref, uh_ref, pw_ref, ps_ref, cw_ref, cb_ref, g_ref, b_ref, y_ref, d_ref, hcs, sta, stg, gsh):
        i = pl.program_id(0)
        first = i == 0
        sta[pl.ds(0, hb), :] = jnp.where(first, 0.0, uh_ref[:, 0:D_POOL])
        sta[pl.ds(hb, ts), :] = u_ref[:, 0:D_POOL]
        glu_h = uh_ref[:, D_POOL:D_POOL + D_CONV] * _sigmoid(uh_ref[:, D_POOL + D_CONV:])
        stg[pl.ds(0, hb), :] = jnp.where(first, 0.0, glu_h)
        stg[pl.ds(hb, ts), :] = u_ref[:, D_POOL:D_POOL + D_CONV] * _sigmoid(u_ref[:, D_POOL + D_CONV:])

        for g, w in enumerate(POOL_WINDOWS):
            lanes = pl.ds(g * POOL_GROUP, POOL_GROUP)
            for r0 in range(0, ts, SUB_ROWS):
                s = None
                for q in range(0, w, 8):
                    for tap in _rows_back(sta, hb + r0 - q, SUB_ROWS, lanes, range(min(8, w - q))):
                        s = tap if s is None else s + tap
                pos = (i * ts + r0 + lax.broadcasted_iota(jnp.int32, (SUB_ROWS, 1), 0) + 1).astype(F32)
                d_g = s / jnp.minimum(pos, float(w)) - sta[pl.ds(hb + r0, SUB_ROWS), lanes]
                d_ref[pl.ds(r0, SUB_ROWS), lanes] = d_g.astype(BF16)
            y_ref[:, lanes] = (_bdot(d_ref[:, lanes], pw_ref[g]) * ps_ref[:, lanes]).astype(BF16)

        _shifted_copies(stg, gsh, hb + ts - 8)
        _conv31(stg, gsh, cw_ref, cb_ref, hcs, ts, hb - (CONV_KERNEL - 1))
        for r0 in range(0, ts, LN_ROWS):
            rows = pl.ds(r0, LN_ROWS)
            ln = _layer_norm_rows(hcs[rows, :], g_ref[...], b_ref[...])
            y_ref[rows, D_POOL:] = (ln * _sigmoid(ln)).astype(BF16)

    fix2 = lambda i: (0, 0)
    return pl.pallas_call(
        body,
        out_shape=[jax.ShapeDtypeStruct((S, D_MODEL), BF16), jax.ShapeDtypeStruct((S, D_POOL), BF16),
                   jax.ShapeDtypeStruct((S, D_CONV), F32)],
        grid=(S // ts,),
        in_specs=[pl.BlockSpec((ts, 3 * D_POOL), lambda i: (i, 0)),
                  pl.BlockSpec((hb, 3 * D_POOL), lambda i: (jnp.maximum(i * nh - 1, 0), 0)),
                  pl.BlockSpec((4, POOL_GROUP, POOL_GROUP), lambda i: (0, 0, 0)),
                  pl.BlockSpec((1, D_POOL), fix2), pl.BlockSpec((CONV_KERNEL, D_CONV), fix2),
                  pl.BlockSpec((1, D_CONV), fix2), pl.BlockSpec((1, D_CONV), fix2), pl.BlockSpec((1, D_CONV), fix2)],
        out_specs=[pl.BlockSpec((ts, D_MODEL), lambda i: (i, 0)), pl.BlockSpec((ts, D_POOL), lambda i: (i, 0)),
                   pl.BlockSpec((ts, D_CONV), lambda i: (i, 0))],
        scratch_shapes=[pltpu.VMEM((hb + ts, D_POOL), F32), pltpu.VMEM((hb + ts, D_CONV), F32),
                        pltpu.VMEM((7, hb + ts - 8, D_CONV), F32)],
        compiler_params=_cparams(("parallel",)),
        name="mixer_fwd",
    )(u, u, pool_w, pool_scale.reshape(1, D_POOL), conv_w, conv_b.reshape(1, D_CONV), cln_g.reshape(1, D_CONV),
      cln_b.reshape(1, D_CONV))


def _mixer_bwd(u, d, hc, dycat, pool_w, pool_scale, conv_w, cln_g, cln_b, *, ts=256):
    S = u.shape[0]
    hb = CONV_HALO
    nh = ts // hb
    n = S // ts
    te = ts + hb
    K = CONV_KERNEL

    def body(u_ref, up_ref, un_ref, d_ref, hc_ref, hcn_ref, dy_ref, dyn_ref, pw_ref, ps_ref, cw_ref, g_ref, b_ref,
             du_ref, dpw_ref, dps_ref, dcw_ref, dcb_ref, dg_ref, db_ref, stg, std, sth, gsh, hsh):
        i = pl.program_id(0)
        first = i == 0
        last = i == n - 1

        @pl.when(first)
        def _():
            dpw_ref[...] = jnp.zeros_like(dpw_ref)
            dps_ref[...] = jnp.zeros_like(dps_ref)
            dcw_ref[...] = jnp.zeros_like(dcw_ref)
            dcb_ref[...] = jnp.zeros_like(dcb_ref)
            dg_ref[...] = jnp.zeros_like(dg_ref)
            db_ref[...] = jnp.zeros_like(db_ref)

        pos_e = (i * ts + lax.broadcasted_iota(jnp.int32, (te, 1), 0) + 1).astype(F32)
        dya = dy_ref[:, 0:D_POOL]
        dya_n = jnp.where(last, 0.0, dyn_ref[:, 0:D_POOL])
        for g, w in enumerate(POOL_WINDOWS):
            lanes = pl.ds(g * POOL_GROUP, POOL_GROUP)
            sl = slice(g * POOL_GROUP, (g + 1) * POOL_GROUP)
            pw = pw_ref[g]
            scale = ps_ref[:, lanes]
            d_g = d_ref[:, lanes]
            pre = _bdot(d_g, pw)
            dps_ref[:, lanes] += jnp.sum(dya[:, sl] * pre, axis=0, keepdims=True)
            dys = dya[:, sl] * scale
            dpw_ref[g] += _bdot(d_g, dys, TN)
            dys_e = jnp.concatenate([dys, dya_n[:, sl] * scale], axis=0)
            dd = _bdot(dys_e, pw, NT)
            std[:, lanes] = dd / jnp.minimum(pos_e, float(w))
            for r0 in range(0, ts, SUB_ROWS):
                da = -dd[r0:r0 + SUB_ROWS]
                for q in range(0, w, 8):
                    for tap in _rows_ahead(std, r0 + q, SUB_ROWS, lanes, range(min(8, w - q))):
                        da = da + tap
                du_ref[pl.ds(r0, SUB_ROWS), lanes] = da.astype(BF16)

        glu_p = up_ref[:, D_POOL:D_POOL + D_CONV] * _sigmoid(up_ref[:, D_POOL + D_CONV:])
        stg[pl.ds(0, hb), :] = jnp.where(first, 0.0, glu_p)
        bv = u_ref[:, D_POOL:D_POOL + D_CONV]
        sg = _sigmoid(u_ref[:, D_POOL + D_CONV:])
        stg[pl.ds(hb, ts), :] = bv * sg
        glu_n = un_ref[:, D_POOL:D_POOL + D_CONV] * _sigmoid(un_ref[:, D_POOL + D_CONV:])
        stg[pl.ds(hb + ts, hb), :] = jnp.where(last, 0.0, glu_n)
        _shifted_copies(stg, gsh, hb + te - 8)

        sums = [jnp.zeros((8, D_CONV), F32) for _ in range(3)]
        for r0 in range(0, te, LN_ROWS):
            rows = pl.ds(r0, LN_ROWS)
            hc = hc_ref[rows, :] if r0 < ts else hcn_ref[pl.ds(r0 - ts, LN_ROWS), :]
            hcc = hc - jnp.mean(hc, axis=-1, keepdims=True)
            rstd = lax.rsqrt(jnp.mean(hcc * hcc, axis=-1, keepdims=True) + LN_EPS)
            xh = hcc * rstd
            ln = xh * g_ref[...] + b_ref[...]
            sl_ = _sigmoid(ln)
            if r0 < ts:
                dyb = dy_ref[rows, D_POOL:]
            else:
                dyb = jnp.where(last, 0.0, dyn_ref[pl.ds(r0 - ts, LN_ROWS), D_POOL:])
            dln = dyb * (sl_ * (1.0 + ln * (1.0 - sl_)))
            dxh = dln * g_ref[...]
            dhc = rstd * (dxh - jnp.mean(dxh, axis=-1, keepdims=True)
                          - xh * jnp.mean(dxh * xh, axis=-1, keepdims=True))
            sth[rows, :] = dhc
            if r0 < ts:
                for n_, term in enumerate((dln * xh, dln, dhc)):
                    sums[n_] = sums[n_] + jnp.sum(term.reshape(LN_ROWS // 8, 8, D_CONV), axis=0)
        dg_ref[...] += jnp.sum(sums[0], axis=0, keepdims=True)
        db_ref[...] += jnp.sum(sums[1], axis=0, keepdims=True)
        dcb_ref[...] += jnp.sum(sums[2], axis=0, keepdims=True)

        _shifted_copies(sth, hsh, te - 8)
        for c0 in range(0, D_CONV, SUB_LANES):
            ln_ = pl.ds(c0, SUB_LANES)
            for r0 in range(0, ts, CONV_ROWS):
                rows = pl.ds(r0, CONV_ROWS)
                dglu = jnp.zeros((CONV_ROWS, SUB_LANES), F32)
                for k in range(K):
                    dglu = dglu + cw_ref[k:k + 1, ln_] * _rows_at(sth, hsh, K - 1 - k + r0, CONV_ROWS, ln_)
                bv = u_ref[rows, pl.ds(D_POOL + c0, SUB_LANES)]
                sg = _sigmoid(u_ref[rows, pl.ds(D_POOL + D_CONV + c0, SUB_LANES)])
                du_ref[rows, pl.ds(D_POOL + c0, SUB_LANES)] = (dglu * sg).astype(BF16)
                du_ref[rows, pl.ds(D_POOL + D_CONV + c0, SUB_LANES)] = (dglu * bv * sg * (1.0 - sg)).astype(BF16)
            for k in range(K):
                tap = jnp.zeros((8, SUB_LANES), F32)
                for r0 in range(0, ts, CONV_ROWS):
                    prod = sth[pl.ds(r0, CONV_ROWS), ln_] * _rows_at(stg, gsh, hb - (K - 1) + k + r0, CONV_ROWS, ln_)
                    tap = tap + jnp.sum(prod.reshape(CONV_ROWS // 8, 8, SUB_LANES), axis=0)
                dcw_ref[k:k + 1, ln_] += jnp.sum(tap, axis=0, keepdims=True)

    fix2 = lambda i: (0, 0)
    prev = lambda i: (jnp.maximum(i * nh - 1, 0), 0)
    nxt = lambda i: (jnp.minimum((i + 1) * nh, S // hb - 1), 0)
    return pl.pallas_call(
        body,
        out_shape=[jax.ShapeDtypeStruct((S, 3 * D_POOL), BF16),
                   jax.ShapeDtypeStruct((4, POOL_GROUP, POOL_GROUP), F32),
                   jax.ShapeDtypeStruct((1, D_POOL), F32),
                   jax.ShapeDtypeStruct((K, D_CONV), F32),
                   jax.ShapeDtypeStruct((1, D_CONV), F32),
                   jax.ShapeDtypeStruct((1, D_CONV), F32),
                   jax.ShapeDtypeStruct((1, D_CONV), F32)],
        grid=(n,),
        in_specs=[pl.BlockSpec((ts, 3 * D_POOL), lambda i: (i, 0)),
                  pl.BlockSpec((hb, 3 * D_POOL), prev),
                  pl.BlockSpec((hb, 3 * D_POOL), nxt),
                  pl.BlockSpec((ts, D_POOL), lambda i: (i, 0)),
                  pl.BlockSpec((ts, D_CONV), lambda i: (i, 0)),
                  pl.BlockSpec((hb, D_CONV), nxt),
                  pl.BlockSpec((ts, D_MODEL), lambda i: (i, 0)),
                  pl.BlockSpec((hb, D_MODEL), nxt),
                  pl.BlockSpec((4, POOL_GROUP, POOL_GROUP), lambda i: (0, 0, 0)),
                  pl.BlockSpec((1, D_POOL), fix2), pl.BlockSpec((K, D_CONV), fix2),
                  pl.BlockSpec((1, D_CONV), fix2), pl.BlockSpec((1, D_CONV), fix2)],
        out_specs=[pl.BlockSpec((ts, 3 * D_POOL), lambda i: (i, 0)),
                   pl.BlockSpec((4, POOL_GROUP, POOL_GROUP), lambda i: (0, 0, 0)),
                   pl.BlockSpec((1, D_POOL), fix2), pl.BlockSpec((K, D_CONV), fix2),
                   pl.BlockSpec((1, D_CONV), fix2), pl.BlockSpec((1, D_CONV), fix2), pl.BlockSpec((1, D_CONV), fix2)],
        scratch_shapes=[pltpu.VMEM((hb + ts + hb, D_CONV), F32), pltpu.VMEM((te, D_POOL), F32),
                        pltpu.VMEM((te, D_CONV), F32), pltpu.VMEM((7, hb + te - 8, D_CONV), F32),
                        pltpu.VMEM((7, te - 8, D_CONV), F32)],
        compiler_params=_cparams(("arbitrary",)),
        name="mixer_bwd",
    )(u, u, u, d, hc, hc, dycat, dycat, pool_w, pool_scale.reshape(1, D_POOL), conv_w, cln_g.reshape(1, D_CONV),
      cln_b.reshape(1, D_CONV))


_GELU_C = math.sqrt(2.0 / math.pi)


def _gelu_parts(x):
    inner = _GELU_C * (x + 0.044715 * x * x * x)
    th = jnp.tanh(inner)
    ge = 0.5 * x * (1.0 + th)
    dge = 0.5 * (1.0 + th) + 0.5 * x * (1.0 - th * th) * (_GELU_C * (1.0 + 3.0 * 0.044715 * x * x))
    return ge, dge


def _rows_back(ref, r, n, ln, shifts):
    ext = ref[pl.ds(r - 8, n + 8), ln]
    return [(pltpu.roll(ext, s, 0) if s else ext)[8:] for s in shifts]


def _rows_ahead(ref, r, n, ln, shifts):
    ext = ref[pl.ds(r, n + 8), ln]
    return [(pltpu.roll(ext, n + 8 - s, 0) if s else ext)[:n] for s in shifts]


def _ffn_act_fwd(gate, val, dw_w, dw_b, *, ts=512, tc=1408, name):
    S, F = gate.shape
    hb = FFN_HALO
    nh = ts // hb
    tc = _tile(F, tc)

    def body(g_ref, gh_ref, v_ref, w_ref, b_ref, h_ref, st):
        i = pl.program_id(0)
        st[pl.ds(0, hb), :] = jnp.where(i == 0, 0.0, gh_ref[...].astype(F32))
        st[pl.ds(hb, ts), :] = g_ref[...].astype(F32)
        for c0 in range(0, tc, SUB_LANES):
            ln = pl.ds(c0, SUB_LANES)
            w0, w1, w2, b = w_ref[0:1, ln], w_ref[1:2, ln], w_ref[2:3, ln], b_ref[:, ln]
            for r0 in range(0, ts, SUB_ROWS):
                taps = _rows_back(st, hb + r0, SUB_ROWS, ln, (2, 1, 0))
                gc = b + w0 * taps[0] + w1 * taps[1] + w2 * taps[2]
                ge, _ = _gelu_parts(gc)
                rows = pl.ds(r0, SUB_ROWS)
                h_ref[rows, ln] = (ge * v_ref[rows, ln].astype(F32)).astype(BF16)

    return pl.pallas_call(
        body,
        out_shape=jax.ShapeDtypeStruct((S, F), BF16),
        grid=(S // ts, F // tc),
        in_specs=[pl.BlockSpec((ts, tc), lambda i, j: (i, j)),
                  pl.BlockSpec((hb, tc), lambda i, j: (jnp.maximum(i * nh - 1, 0), j)),
                  pl.BlockSpec((ts, tc), lambda i, j: (i, j)),
                  pl.BlockSpec((3, tc), lambda i, j: (0, j)),
                  pl.BlockSpec((1, tc), lambda i, j: (0, j))],
        out_specs=pl.BlockSpec((ts, tc), lambda i, j: (i, j)),
        scratch_shapes=[pltpu.VMEM((hb + ts, tc), F32)],
        compiler_params=_cparams(("parallel", "parallel")),
        name=name,
    )(gate, gate, val, dw_w, dw_b.reshape(1, F))


def _ffn_act_bwd(gate, val, dh, dw_w, dw_b, *, ts=512, tc=1408, name):
    S, F = gate.shape
    hb = FFN_HALO
    nh = ts // hb
    n = S // ts
    te = ts + hb
    tc = _tile(F, tc)

    def body(g_ref, gp_ref, gn_ref, v_ref, vn_ref, dh_ref, dhn_ref, w_ref, b_ref,
             dg_ref, dv_ref, dw_ref, db_ref, st, sd):
        i = pl.program_id(1)
        first = i == 0
        last = i == n - 1

        @pl.when(first)
        def _():
            dw_ref[...] = jnp.zeros_like(dw_ref)
            db_ref[...] = jnp.zeros_like(db_ref)

        st[pl.ds(0, hb), :] = jnp.where(first, 0.0, gp_ref[...].astype(F32))
        st[pl.ds(hb, ts), :] = g_ref[...].astype(F32)
        st[pl.ds(hb + ts, hb), :] = jnp.where(last, 0.0, gn_ref[...].astype(F32))
        for c0 in range(0, tc, SUB_LANES):
            ln = pl.ds(c0, SUB_LANES)
            w0, w1, w2, b = w_ref[0:1, ln], w_ref[1:2, ln], w_ref[2:3, ln], b_ref[:, ln]
            db_acc = jnp.zeros((8, SUB_LANES), F32)
            dw_acc = [jnp.zeros((8, SUB_LANES), F32) for _ in range(3)]
            for r0 in range(0, te, SUB_ROWS):
                rc = min(SUB_ROWS, te - r0)
                taps = _rows_back(st, hb + r0, rc, ln, (2, 1, 0))
                gc = b + w0 * taps[0] + w1 * taps[1] + w2 * taps[2]
                ge, dge = _gelu_parts(gc)
                if r0 < ts:
                    rows = pl.ds(r0, rc)
                    val, dh = v_ref[rows, ln].astype(F32), dh_ref[rows, ln].astype(F32)
                else:
                    val = jnp.where(last, 0.0, vn_ref[:, ln].astype(F32)[0:rc])
                    dh = jnp.where(last, 0.0, dhn_ref[:, ln].astype(F32)[0:rc])
                dgc = dh * val * dge
                sd[pl.ds(r0, rc), ln] = dgc
                if r0 < ts:
                    dv_ref[rows, ln] = (dh * ge).astype(BF16)
                    db_acc = db_acc + jnp.sum(dgc.reshape(rc // 8, 8, SUB_LANES), axis=0)
                    for k in range(3):
                        dw_acc[k] = dw_acc[k] + jnp.sum((dgc * taps[k]).reshape(rc // 8, 8, SUB_LANES), axis=0)
            db_ref[:, ln] += jnp.sum(db_acc, axis=0, keepdims=True)
            for k in range(3):
                dw_ref[k:k + 1, ln] += jnp.sum(dw_acc[k], axis=0, keepdims=True)
            for r0 in range(0, ts, SUB_ROWS):
                ahead = _rows_ahead(sd, r0, SUB_ROWS, ln, (2, 1, 0))
                dg_ref[pl.ds(r0, SUB_ROWS), ln] = (w0 * ahead[0] + w1 * ahead[1] + w2 * ahead[2]).astype(BF16)

    cur = lambda j, i: (i, j)
    prev = lambda j, i: (jnp.maximum(i * nh - 1, 0), j)
    nxt = lambda j, i: (jnp.minimum((i + 1) * nh, S // hb - 1), j)
    return pl.pallas_call(
        body,
        out_shape=[jax.ShapeDtypeStruct((S, F), BF16), jax.ShapeDtypeStruct((S, F), BF16),
                   jax.ShapeDtypeStruct((3, F), F32), jax.ShapeDtypeStruct((1, F), F32)],
        grid=(F // tc, n),
        in_specs=[pl.BlockSpec((ts, tc), cur), pl.BlockSpec((hb, tc), prev), pl.BlockSpec((hb, tc), nxt),
                  pl.BlockSpec((ts, tc), cur), pl.BlockSpec((hb, tc), nxt),
                  pl.BlockSpec((ts, tc), cur), pl.BlockSpec((hb, tc), nxt),
                  pl.BlockSpec((3, tc), lambda j, i: (0, j)), pl.BlockSpec((1, tc), lambda j, i: (0, j))],
        out_specs=[pl.BlockSpec((ts, tc), cur), pl.BlockSpec((ts, tc), cur),
                   pl.BlockSpec((3, tc), lambda j, i: (0, j)), pl.BlockSpec((1, tc), lambda j, i: (0, j))],
        scratch_shapes=[pltpu.VMEM((hb + ts + hb, tc), F32), pltpu.VMEM((te, tc), F32)],
        compiler_params=_cparams(("parallel", "arbitrary")),
        name=name,
    )(gate, gate, gate, val, val, dh, dh, dw_w, dw_b.reshape(1, F))


def _ple_bwd_rows(dz, gate, proj):
    return dz * proj * gate * (1.0 - gate), dz * gate


def _loss_ln_bwd(z, ln_g, ln_b, target, gate, proj, *, ts=512, name):
    S, D = z.shape

    def body(z_ref, g_ref, b_ref, t_ref, gate_ref, proj_ref, dz_ref, dzb_ref, dg_ref, db_ref, loss_ref, ds_ref,
             dp_ref, dbg_ref):
        i = pl.program_id(0)

        @pl.when(i == 0)
        def _():
            dg_ref[...] = jnp.zeros_like(dg_ref)
            db_ref[...] = jnp.zeros_like(db_ref)
            loss_ref[...] = jnp.zeros_like(loss_ref)
            dbg_ref[...] = jnp.zeros_like(dbg_ref)

        dg_acc = jnp.zeros((8, D), F32)
        db_acc = jnp.zeros((8, D), F32)
        dbg_acc = jnp.zeros((8, D), F32)
        loss_acc = jnp.zeros((1, 1), F32)
        for r0 in range(0, ts, LN_ROWS):
            rows = pl.ds(r0, LN_ROWS)
            zt = z_ref[rows, :]
            err = _layer_norm_rows(zt, g_ref[...], b_ref[...]) - t_ref[rows, :]
            loss_acc = loss_acc + 0.5 * jnp.sum(jnp.mean(err * err, axis=-1, keepdims=True), keepdims=True)
            do = err * (1.0 / D)
            dz, xh = _ln_bwd_rows(zt, g_ref[...], do)
            dg_acc = dg_acc + jnp.sum((do * xh).reshape(LN_ROWS // 8, 8, D), axis=0)
            db_acc = db_acc + jnp.sum(do.reshape(LN_ROWS // 8, 8, D), axis=0)
            dz_ref[rows, :] = dz
            dzb_ref[rows, :] = dz.astype(BF16)
            ds, dp = _ple_bwd_rows(dz, gate_ref[rows, :], proj_ref[rows, :])
            ds_ref[rows, :] = ds.astype(BF16)
            dp_ref[rows, :] = dp.astype(BF16)
            dbg_acc = dbg_acc + jnp.sum(ds.reshape(LN_ROWS // 8, 8, D), axis=0)
        dg_ref[...] += jnp.sum(dg_acc, axis=0, keepdims=True)
        db_ref[...] += jnp.sum(db_acc, axis=0, keepdims=True)
        dbg_ref[...] += jnp.sum(dbg_acc, axis=0, keepdims=True)
        loss_ref[...] += loss_acc

    row = pl.BlockSpec((ts, D), lambda i: (i, 0))
    fix = pl.BlockSpec((1, D), lambda i: (0, 0))
    return pl.pallas_call(
        body,
        out_shape=[jax.ShapeDtypeStruct((S, D), F32), jax.ShapeDtypeStruct((S, D), BF16),
                   jax.ShapeDtypeStruct((1, D), F32), jax.ShapeDtypeStruct((1, D), F32),
                   jax.ShapeDtypeStruct((8, 128), F32), jax.ShapeDtypeStruct((S, D), BF16),
                   jax.ShapeDtypeStruct((S, D), BF16), jax.ShapeDtypeStruct((1, D), F32)],
        grid=(S // ts,),
        in_specs=[row, fix, fix, row, row, row],
        out_specs=[row, row, fix, fix, pl.BlockSpec((8, 128), lambda i: (0, 0)), row, row, fix],
        compiler_params=_cparams(("arbitrary",)),
        name=name,
    )(z, ln_g.reshape(1, D), ln_b.reshape(1, D), target, gate, proj)


HEADS_PER_STEP = 4
HEAD_LANES = HEADS_PER_STEP * HEAD_DIM


ATT_ROWS = 32
ATT_SCALE = HEAD_DIM ** -0.5


def _softmax_piece(scores, bias, qb):
    s = scores + bias
    kpos = qb * Q_BLOCK + lax.broadcasted_iota(jnp.int32, (1, KV_SPAN), 1)
    s = jnp.where(kpos >= KV_PAD, s, NEG_INF)
    e = jnp.exp(s - jnp.max(s, axis=-1, keepdims=True))
    return e * (1.0 / jnp.sum(e, axis=-1, keepdims=True))


def _head_masks():
    lane = lax.broadcasted_iota(jnp.int32, (1, HEAD_LANES), 1)
    return [(lane >= j * HEAD_DIM) & (lane < (j + 1) * HEAD_DIM) for j in range(HEADS_PER_STEP)]


def _pick_heads(masks, per_head):
    out = per_head[0]
    for mask, x in zip(masks[1:], per_head[1:]):
        out = jnp.where(mask, x, out)
    return out


def _pad_keys(qb, k_ref, v_ref, kp, vp):
    @pl.when(qb == 0)
    def _():
        kp[pl.ds(0, KV_PAD), :] = jnp.zeros((KV_PAD, HEAD_LANES), BF16)
        vp[pl.ds(0, KV_PAD), :] = jnp.zeros((KV_PAD, HEAD_LANES), BF16)
        kp[pl.ds(KV_PAD, k_ref.shape[0]), :] = k_ref[...]
        vp[pl.ds(KV_PAD, v_ref.shape[0]), :] = v_ref[...]


def _attn_fwd(qkv, bias):
    S = qkv.shape[0]
    nhp = N_HEADS // HEADS_PER_STEP

    def body(q_ref, k_ref, v_ref, b_ref, o_ref, kp, vp, p_scr):
        qb = pl.program_id(1)
        _pad_keys(qb, k_ref, v_ref, kp, vp)
        span = pl.ds(pl.multiple_of(qb * Q_BLOCK, Q_BLOCK), KV_SPAN)
        kc, vc = kp[span, :], vp[span, :]
        qt = q_ref[...] * ATT_SCALE
        mine = _head_masks()
        scores = [_bdot(jnp.where(mine[j], qt, jnp.zeros_like(qt)), kc, NT) for j in range(HEADS_PER_STEP)]
        outs = []
        for j in range(HEADS_PER_STEP):
            for r0 in range(0, Q_BLOCK, ATT_ROWS):
                rows = pl.ds(r0, ATT_ROWS)
                p_scr[j, rows, :] = _softmax_piece(scores[j][r0:r0 + ATT_ROWS], b_ref[j, rows, :], qb).astype(BF16)
            outs.append(_bdot(p_scr[j], vc))
        o_ref[...] = _pick_heads(mine, outs).astype(BF16)

    return pl.pallas_call(
        body,
        out_shape=jax.ShapeDtypeStruct((S, D_MODEL), BF16),
        grid=(nhp, S // Q_BLOCK),
        in_specs=[pl.BlockSpec((Q_BLOCK, HEAD_LANES), lambda h, i: (i, h)),
                  pl.BlockSpec((S, HEAD_LANES), lambda h, i: (0, nhp + h)),
                  pl.BlockSpec((S, HEAD_LANES), lambda h, i: (0, 2 * nhp + h)),
                  pl.BlockSpec((HEADS_PER_STEP, Q_BLOCK, KV_SPAN), lambda h, i: (h, 0, 0))],
        out_specs=pl.BlockSpec((Q_BLOCK, HEAD_LANES), lambda h, i: (i, h)),
        scratch_shapes=[pltpu.VMEM((KV_PAD + S, HEAD_LANES), BF16), pltpu.VMEM((KV_PAD + S, HEAD_LANES), BF16),
                        pltpu.VMEM((HEADS_PER_STEP, Q_BLOCK, KV_SPAN), BF16)],
        compiler_params=_cparams(("parallel", "arbitrary")),
        name="attn_fwd",
    )(qkv, qkv, qkv, bias)


def _attn_bwd(qkv, bias, do):
    S = qkv.shape[0]
    nhp = N_HEADS // HEADS_PER_STEP
    nq = S // Q_BLOCK
    scale = HEAD_DIM ** -0.5

    def body(q_ref, k_ref, v_ref, b_ref, do_ref, dq_ref, dk_ref, dv_ref, db_ref, kp, vp, dka, dva,
             p_scr, ds_scr):
        qb = pl.program_id(1)
        _pad_keys(qb, k_ref, v_ref, kp, vp)

        @pl.when(qb == 0)
        def _():
            dka[...] = jnp.zeros_like(dka)
            dva[...] = jnp.zeros_like(dva)
            db_ref[...] = jnp.zeros_like(db_ref)

        span = pl.ds(pl.multiple_of(qb * Q_BLOCK, Q_BLOCK), KV_SPAN)
        kc, vc = kp[span, :], vp[span, :]
        qt, dot = q_ref[...] * ATT_SCALE, do_ref[...]
        mine = _head_masks()
        dqs = []
        qs = [jnp.where(mine[j], qt, jnp.zeros_like(qt)) for j in range(HEADS_PER_STEP)]
        dos = [jnp.where(mine[j], dot, jnp.zeros_like(dot)) for j in range(HEADS_PER_STEP)]
        scores = [_bdot(qs[j], kc, NT) for j in range(HEADS_PER_STEP)]
        dps = [_bdot(dos[j], vc, NT) for j in range(HEADS_PER_STEP)]
        for j in range(HEADS_PER_STEP):
            qj, doj = qs[j], dos[j]
            for r0 in range(0, Q_BLOCK, ATT_ROWS):
                rows = pl.ds(r0, ATT_ROWS)
                p = _softmax_piece(scores[j][r0:r0 + ATT_ROWS], b_ref[j, rows, :], qb)
                dp = dps[j][r0:r0 + ATT_ROWS]
                ds = p * (dp - jnp.sum(p * dp, axis=-1, keepdims=True))
                db_ref[j, rows, :] += ds
                p_scr[j, rows, :] = p.astype(BF16)
                ds_scr[j, rows, :] = ds.astype(BF16)
            dva[span, :] += _bdot(p_scr[j], doj, TN)
            dqs.append(_bdot(ds_scr[j], kc))
            dka[span, :] += _bdot(ds_scr[j], qj, TN)
        dq_ref[...] = (scale * _pick_heads(mine, dqs)).astype(BF16)

        @pl.when(qb == nq - 1)
        def _():
            dk_ref[...] = dka[pl.ds(KV_PAD, S), :].astype(BF16)
            dv_ref[...] = dva[pl.ds(KV_PAD, S), :].astype(BF16)

    blk = pl.BlockSpec((Q_BLOCK, HEAD_LANES), lambda h, i: (i, h))
    col = pl.BlockSpec((S, HEAD_LANES), lambda h, i: (0, h))
    bsp = pl.BlockSpec((HEADS_PER_STEP, Q_BLOCK, KV_SPAN), lambda h, i: (h, 0, 0))
    return pl.pallas_call(
        body,
        out_shape=[jax.ShapeDtypeStruct((S, D_MODEL), BF16)] * 3
        + [jax.ShapeDtypeStruct((N_HEADS, Q_BLOCK, KV_SPAN), F32)],
        grid=(nhp, nq),
        in_specs=[blk, pl.BlockSpec((S, HEAD_LANES), lambda h, i: (0, nhp + h)),
                  pl.BlockSpec((S, HEAD_LANES), lambda h, i: (0, 2 * nhp + h)), bsp, blk],
        out_specs=[blk, col, col, bsp],
        scratch_shapes=[pltpu.VMEM((KV_PAD + S, HEAD_LANES), BF16), pltpu.VMEM((KV_PAD + S, HEAD_LANES), BF16),
                        pltpu.VMEM((KV_PAD + S, HEAD_LANES), F32), pltpu.VMEM((KV_PAD + S, HEAD_LANES), F32),
                        pltpu.VMEM((HEADS_PER_STEP, Q_BLOCK, KV_SPAN), BF16), pltpu.VMEM((HEADS_PER_STEP, Q_BLOCK, KV_SPAN), BF16)],
        compiler_params=_cparams(("parallel", "arbitrary")),
        name="attn_bwd",
    )(qkv, qkv, qkv, bias, do)


N_DIST = BAND + CHUNK - 1
N_FAR = KV_PAD + CHUNK - MAX_REL


def _shear_rows(x, towards_right):
    row = lax.broadcasted_iota(jnp.int32, (Q_BLOCK, 1), 0)
    for bit in range(Q_BLOCK.bit_length() - 1):
        step = 1 << bit
        x = jnp.where((row & step) != 0, pltpu.roll(x, step if towards_right else KV_SPAN - step, 1), x)
    return x


def _bias_blocks(rel_bias, dep):
    H = rel_bias.shape[0]
    e = jnp.concatenate([jnp.broadcast_to(rel_bias[:, 2 * MAX_REL:], (H, N_FAR)),
                         jnp.flip(rel_bias[:, 2 * MAX_REL - (N_DIST - N_FAR):2 * MAX_REL], axis=1),
                         jnp.zeros((H, KV_SPAN - N_DIST), F32)], axis=1).reshape(H, 1, KV_SPAN)

    def body(e_ref, dep_ref, o_ref):
        first = pltpu.roll(jnp.broadcast_to(e_ref[...], (Q_BLOCK, KV_SPAN)), KV_SPAN - (CHUNK - 1), 1)
        x = _shear_rows(first, True)
        row = lax.broadcasted_iota(jnp.int32, (Q_BLOCK, 1), 0)
        chunk0 = row - (row & (CHUNK - 1))
        k = lax.broadcasted_iota(jnp.int32, (1, KV_SPAN), 1)
        o_ref[...] = jnp.where((k >= chunk0) & (k < chunk0 + BAND), x, NEG_INF)

    return pl.pallas_call(
        body,
        out_shape=jax.ShapeDtypeStruct((H, Q_BLOCK, KV_SPAN), F32),
        grid=(H,),
        in_specs=[pl.BlockSpec((None, 1, KV_SPAN), lambda h: (h, 0, 0)), pl.BlockSpec(memory_space=pl.ANY)],
        out_specs=pl.BlockSpec((None, Q_BLOCK, KV_SPAN), lambda h: (h, 0, 0)),
        compiler_params=_cparams(("parallel",)),
        name="bias_blocks",
    )(e, dep)


def _bias_blocks_grad(dblk):
    H = dblk.shape[0]

    def body(d_ref, o_ref):
        x = pltpu.roll(_shear_rows(d_ref[...], False), CHUNK - 1, 1)
        de = jnp.sum(x, axis=0, keepdims=True)
        lane = lax.broadcasted_iota(jnp.int32, de.shape, 1)
        far = jnp.sum(jnp.where(lane < N_FAR, de, 0.0), axis=-1, keepdims=True)
        o_ref[...] = jnp.where(lane == 0, far, jnp.where(lane < N_FAR, 0.0, de))

    de = pl.pallas_call(
        body,
        out_shape=jax.ShapeDtypeStruct((H, 1, KV_SPAN), F32),
        grid=(H,),
        in_specs=[pl.BlockSpec((None, Q_BLOCK, KV_SPAN), lambda h: (h, 0, 0))],
        out_specs=pl.BlockSpec((None, 1, KV_SPAN), lambda h: (h, 0, 0)),
        compiler_params=_cparams(("parallel",)),
        name="bias_grad_sum",
    )(dblk).reshape(H, KV_SPAN)
    near = jnp.flip(de[:, N_FAR:N_DIST], axis=1)
    return jnp.concatenate([jnp.zeros((H, 2 * MAX_REL - (N_DIST - N_FAR)), F32), near, de[:, 0:1]], axis=1)


def _ffn_forward(r1, r1b, p_l, w, l, ready, after):
    ready(f"up{l}", after)
    up_g = _mm_rows([(r1b, w["ffn_up_t"][l], True, (0, 2))], out_dtype=BF16, name=f"ffn_up_g{l}")
    up_v = _mm_rows([(r1b, w["ffn_up_t"][l], True, (1, 2))], out_dtype=BF16, name=f"ffn_up_v{l}")
    h = _ffn_act_fwd(up_g, up_v, w["ffn_dw_w"][l], w["ffn_dw_b"][l], name=f"ffn_act{l}")
    ready(f"dn{l}", h)
    z2, r2, r2b, gate, proj = _proj_ln(r1, h, w["ffn_w_down"][l], w["ln_ffn_g"][l], w["ln_ffn_b"][l],
                                       ple=(w["ple_w_gate"][l], w["ple_b_gate"][l], p_l, w["ple_w_proj"][l]),
                                       name=f"ffn_down_ln{l}")
    return dict(r1b=r1b, up_g=up_g, up_v=up_v, h=h, z2=z2, gate=gate, proj=proj), r2, r2b


def _ffn_backward(sv, dz2, dz2b, ple_bwd, p_l, w, l, grads, ln_bwd, emit):
    r1b = sv["r1b"]
    ds, dproj, db_gate = ple_bwd
    dh = _mm_rows([(dz2b, w["ffn_w_down"][l], True, WHOLE)], out_dtype=BF16, name=f"ffn_dh{l}")
    dgate, dval, d_dw_w, d_dw_b = _ffn_act_bwd(sv["up_g"], sv["up_v"], dh, w["ffn_dw_w"][l], w["ffn_dw_b"][l],
                                               name=f"ffn_act_bwd{l}")
    grads["ffn_w_down"][l] = _wgrad(sv["h"], dz2b, tm=1408, name=f"d_ffn_w_down{l}")
    d_up_g = _wgrad(dgate, r1b, tm=1408, part=(0, 2), name=f"d_ffn_up_g{l}")
    grads["ffn_up_t"][l] = _wgrad(dval, r1b, tm=1408, part=(1, 2), into=d_up_g, name=f"d_ffn_up_v{l}")
    grads["ple_w_gate"][l] = _wgrad(r1b, ds, name=f"d_ple_w_gate{l}")
    grads["ple_w_proj"][l] = _wgrad(p_l, dproj, piece=D_MODEL // N_DEV, name=f"d_ple_w_proj{l}")
    grads["ffn_dw_w"][l] = d_dw_w
    grads["ffn_dw_b"][l] = d_dw_b[0]
    grads["ple_b_gate"][l] = db_gate[0]
    return _mm_rows([(ds, w["ple_w_gate"][l], True, WHOLE), (dgate, w["ffn_up_t"][l], False, (0, 2)),
                     (dval, w["ffn_up_t"][l], False, (1, 2))], add=dz2, add_scale=ALPHA, ln_bwd=ln_bwd, dep=emit(),
                    name=f"dr1_{l}")


def _local_step(x, p, target, w, ready=lambda group, after: None, emit=lambda group, grads: None):
    grads = {k: [None, None] for k in ("ffn_w_down", "ffn_up_t", "ple_w_gate", "ple_w_proj", "ffn_dw_w",
                                       "ffn_dw_b", "ple_b_gate", "ln_ffn_g", "ln_ffn_b", "ln_mix_g", "ln_mix_b")}

    pb = p.astype(BF16)
    ready("mix", None)
    u = _mm_rows([(x, w["mix_w_in_t"], True, WHOLE)], name="mix_in")
    ycat, dpool, hconv = _mixer_fwd(u, w["pool_w"], w["pool_scale"], w["conv_dw_w"], w["conv_dw_b"], w["conv_ln_g"],
                                    w["conv_ln_b"])
    ready("mixo", ycat)
    z1, r1, r1b = _proj_ln(x, ycat, w["mix_w_out"], w["ln_mix_g"][0], w["ln_mix_b"][0], name="mix_out_ln")
    bias = _bias_blocks(w["attn_rel_bias"], r1b)
    sv0, r2, r2b = _ffn_forward(r1, r1b, pb[0], w, 0, ready, bias)

    ready("attn", r2b)
    qkv = _mm_rows([(r2b, w["attn_w_qkv"], False, WHOLE)], out_dtype=BF16, name="attn_qkv")
    attn = _attn_fwd(qkv, bias)
    z3, r3, r3b = _proj_ln(r2, attn, w["attn_w_o"], w["ln_mix_g"][1], w["ln_mix_b"][1], name="attn_out_ln")
    sv1, _, _ = _ffn_forward(r3, r3b, pb[1], w, 1, ready, r3b)

    dz4, dz4b, grads["ln_ffn_g"][1], grads["ln_ffn_b"][1], loss, *ple1 = _loss_ln_bwd(
        sv1["z2"], w["ln_ffn_g"][1], w["ln_ffn_b"][1], target, sv1["gate"], sv1["proj"], name="loss_ln_bwd")
    dz3, dz3b, grads["ln_mix_g"][1], grads["ln_mix_b"][1] = _ffn_backward(
        sv1, dz4, dz4b, ple1, pb[1], w, 1, grads, (z3, w["ln_mix_g"][1]), lambda: emit("ffn1", grads))
    grads["attn_w_o"] = _wgrad(attn, dz3b, name="d_attn_w_o")
    dattn = _mm_rows([(dz3b, w["attn_w_o"], True, WHOLE)], out_dtype=BF16, name="d_attn")
    dq, dk, dv, dbias = _attn_bwd(qkv, bias, dattn)
    grads["attn_rel_bias"] = _bias_blocks_grad(dbias)
    dqkv = jnp.concatenate([dq, dk, dv], axis=1)
    grads["attn_w_qkv"] = _wgrad(r2b, dqkv, tn=768, piece=3 * D_MODEL // N_DEV, name="d_attn_w_qkv")
    dz2, dz2b, grads["ln_ffn_g"][0], grads["ln_ffn_b"][0], *ple0 = _mm_rows(
        [(dqkv, w["attn_w_qkv"], True, WHOLE)], add=dz3, add_scale=ALPHA,
        ln_bwd=(sv0["z2"], w["ln_ffn_g"][0], sv0["gate"], sv0["proj"]), dep=emit("attn", grads), name="dr2")
    dz1, dz1b, grads["ln_mix_g"][0], grads["ln_mix_b"][0] = _ffn_backward(
        sv0, dz2, dz2b, ple0, pb[0], w, 0, grads, (z1, w["ln_mix_g"][0]), lambda: emit("ffn0", grads))
    grads["mix_w_out"] = _wgrad(ycat, dz1b, name="d_mix_w_out")
    dycat = _mm_rows([(dz1b, w["mix_w_out"], True, WHOLE)], name="d_ycat")
    du, g_pw, g_ps, g_cw, g_cb, g_cg, g_cbb = _mixer_bwd(u, dpool, hconv, dycat, w["pool_w"], w["pool_scale"],
                                                         w["conv_dw_w"], w["conv_ln_g"], w["conv_ln_b"])
    grads["mix_w_in_t"] = _wgrad(du, x, name="d_mix_w_in")
    grads.update(pool_w=g_pw, pool_scale=g_ps[0], conv_dw_w=g_cw, conv_dw_b=g_cb[0], conv_ln_g=g_cg[0],
                 conv_ln_b=g_cbb[0])
    for kname in ("ln_ffn_g", "ln_ffn_b", "ln_mix_g", "ln_mix_b"):
        grads[kname] = [a[0] for a in grads[kname]]
    grad_x = _mm_rows([(du, w["mix_w_in_t"], False, WHOLE)], add=dz1, add_scale=ALPHA, dep=emit("mix", grads),
                      name="grad_x")
    return loss[0, 0], grad_x, grads


_HBM = pl.BlockSpec(memory_space=pltpu.HBM)
_SEM = pl.BlockSpec(memory_space=pltpu.SEMAPHORE)
_EFFECT = pltpu.SideEffectType.DATAFLOW_SIDE_EFFECTING


def _slot(ref, place, shape, k):
    if place in ("stack", "pieces"):
        return ref.at[k]
    ax = place[1]
    n = shape[ax]
    return ref.at[(slice(None),) * ax + (pl.ds(pl.multiple_of(k * n, n), n),)]


def _result_shape(buf, place):
    if place == "stack":
        return (N_DEV,) + buf.shape
    if place == "pieces":
        return buf.shape
    return tuple(s * N_DEV if i == place[1] else s for i, s in enumerate(buf.shape))


def _peers(x, y, c):
    for d in range(1, N_DEV):
        px, py, pc = x ^ ((d >> 2) & 1), y ^ ((d >> 1) & 1), c ^ (d & 1)
        yield d, (px, py, pc), 4 * px + 2 * py + pc


def _exchange_start(bufs, places, after, *, name):
    nb = len(bufs)
    lands = [lax.empty(_result_shape(b, p_), b.dtype) for b, p_ in zip(bufs, places)]
    has_after = after is not None

    def body(*refs):
        srcs, dsts = refs[:nb], refs[nb:2 * nb]
        outs = refs[2 * nb + has_after:]
        send_sems, recv_sems, token = outs[0], outs[1], outs[2 + 2 * nb]
        x, y, c = lax.axis_index("x"), lax.axis_index("y"), lax.axis_index("c")
        me = 4 * x + 2 * y + c
        for b in range(nb):
            for d, dev, peer in _peers(x, y, c):
                pltpu.make_async_remote_copy(
                    src_ref=srcs[b].at[peer] if places[b] == "pieces" else srcs[b],
                    dst_ref=_slot(dsts[b], places[b], bufs[b].shape, me),
                    send_sem=send_sems.at[b * N_DEV + d], recv_sem=recv_sems.at[b * N_DEV + d],
                    device_id=dev, device_id_type=pl.DeviceIdType.MESH).start()
            pltpu.make_async_copy(srcs[b].at[me] if places[b] == "pieces" else srcs[b],
                                  _slot(dsts[b], places[b], bufs[b].shape, me), recv_sems.at[b * N_DEV]).start()
        token[...] = jnp.zeros_like(token)

    sems = pltpu.SemaphoreType.DMA((nb * N_DEV,))
    ins = [pltpu.with_memory_space_constraint(a, pltpu.HBM) for a in list(bufs) + lands]
    out = pl.pallas_call(
        body,
        out_shape=(sems, sems, *[pltpu.HBM(a.shape, a.dtype) for a in ins], jax.ShapeDtypeStruct((8, 128), F32)),
        in_specs=[_HBM] * (2 * nb) + ([pl.BlockSpec(memory_space=pl.ANY)] if has_after else []),
        out_specs=(_SEM, _SEM, *[_HBM] * (2 * nb), pl.BlockSpec(memory_space=pltpu.VMEM)),
        input_output_aliases={i: 2 + i for i in range(2 * nb)},
        compiler_params=pltpu.CompilerParams(has_side_effects=_EFFECT),
        name=name,
    )(*ins, *([after] if has_after else []))
    return dict(send=out[0], recv=out[1], srcs=out[2:2 + nb], lands=out[2 + nb:2 + 2 * nb], token=out[-1],
                places=places)


def _exchange_wait(h, after, *, name):
    nb = len(h["srcs"])
    places = h["places"]
    shapes = [a.shape for a in h["srcs"]]

    def body(*refs):
        srcs, dsts, send_sems, recv_sems = refs[:nb], refs[nb:2 * nb], refs[2 * nb], refs[2 * nb + 1]
        x, y, c = lax.axis_index("x"), lax.axis_index("y"), lax.axis_index("c")
        me = 4 * x + 2 * y + c
        for b in range(nb):
            pieces = places[b] == "pieces"
            for d, dev, peer in _peers(x, y, c):
                cp = pltpu.make_async_remote_copy(
                    src_ref=srcs[b].at[peer] if pieces else srcs[b],
                    dst_ref=_slot(dsts[b], places[b], shapes[b], peer),
                    send_sem=send_sems.at[b * N_DEV + d], recv_sem=recv_sems.at[b * N_DEV + d],
                    device_id=dev, device_id_type=pl.DeviceIdType.MESH)
                cp.wait_send()
                cp.wait_recv()
            pltpu.make_async_copy(srcs[b].at[me] if pieces else srcs[b], _slot(dsts[b], places[b], shapes[b], me),
                                  recv_sems.at[b * N_DEV]).wait()

    ins = list(h["srcs"]) + list(h["lands"])
    out = pl.pallas_call(
        body,
        out_shape=tuple(pltpu.HBM(a.shape, a.dtype) for a in ins),
        in_specs=[_HBM] * (2 * nb) + [_SEM, _SEM, pl.BlockSpec(memory_space=pl.ANY)],
        out_specs=tuple([_HBM] * (2 * nb)),
        input_output_aliases={i: i for i in range(2 * nb)},
        compiler_params=pltpu.CompilerParams(has_side_effects=_EFFECT),
        name=name,
    )(*ins, h["send"], h["recv"], after)
    return out[nb:]


def _adamw(recv, w, m, v, *, layer=0, into=None, name):
    L, R, C = w.shape
    fits = [d for d in range(16, R + 1, 16) if R % d == 0 and d * C * 4 <= 2 * 1024 * 1024]
    tr = fits[-1] if fits else R
    c1 = 1.0 - ADAM_B1 ** ADAM_STEP
    c2 = 1.0 - ADAM_B2 ** ADAM_STEP

    def body(r_ref, w_ref, m_ref, v_ref, *rest):
        g_ref, d_ref, mo_ref, vo_ref = rest[-4:]
        g = r_ref[0].astype(F32)
        for i in range(1, N_DEV):
            g = g + r_ref[i].astype(F32)
        m_new = ADAM_B1 * m_ref[...] + (1.0 - ADAM_B1) * g
        v_new = ADAM_B2 * v_ref[...] + (1.0 - ADAM_B2) * (g * g)
        m_hat = m_new / c1
        v_hat = v_new / c2
        g_ref[...] = g
        d_ref[...] = -ADAM_LR * (m_hat / (jnp.sqrt(v_hat) + ADAM_EPS) + ADAM_WD * w_ref[...])
        mo_ref[...] = m_new
        vo_ref[...] = v_new

    row = pl.BlockSpec((None, tr, C), lambda i: (layer, i, 0))
    others = [] if into is None else list(into)
    return pl.pallas_call(
        body,
        out_shape=[jax.ShapeDtypeStruct((L, R, C), F32)] * 4,
        grid=(R // tr,),
        in_specs=[pl.BlockSpec((N_DEV, tr, C), lambda i: (0, i, 0)), row, row, row]
        + [pl.BlockSpec(memory_space=pl.ANY)] * len(others),
        out_specs=[row] * 4,
        input_output_aliases={4 + k: k for k in range(len(others))},
        compiler_params=_cparams(("parallel",)),
        name=name,
    )(recv, w, m, v, *others)


_TRANSPOSED = ("mix_w_in", "ffn_w_up")


def _ffn_groups(l):
    return ((f"up{l}", (("ffn_w_up", l, BF16, ("axis", 0)), ("ffn_dw_w", l, F32, "stack"))),
            (f"dn{l}", (("ffn_w_down", l, BF16, ("axis", 0)), ("ple_w_gate", l, BF16, ("axis", 0)),
                        ("ple_w_proj", l, BF16, ("axis", 1)))))


_GATHER_GROUPS = (
    ("mix", (("mix_w_in", 0, BF16, ("axis", 0)), ("conv_dw_w", 0, F32, "stack"))),
    ("mixo", (("mix_w_out", 0, BF16, ("axis", 0)),)),
    *_ffn_groups(0),
    ("attn", (("attn_w_qkv", 0, BF16, ("axis", 1)), ("attn_w_o", 0, BF16, ("axis", 0)))),
    *_ffn_groups(1))
_SHARDED = ("mix_w_in", "conv_dw_w", "mix_w_out", "attn_w_qkv", "attn_w_o", "ffn_w_up", "ffn_dw_w", "ffn_w_down",
            "ple_w_gate", "ple_w_proj")
_REPLICATED = ("pool_w", "pool_scale", "conv_dw_b", "conv_ln_g", "conv_ln_b", "attn_rel_bias", "ln_mix_g",
               "ln_mix_b", "ffn_dw_b", "ple_b_gate", "ln_ffn_g", "ln_ffn_b")


def _pack_rows(parts, row_mult, dtype):
    lead = parts[0].shape[:-1]
    flat = jnp.concatenate([a.astype(dtype) for a in parts], axis=-1)
    n = flat.shape[-1]
    unit = row_mult * LANES
    padded = -(-n // unit) * unit
    flat = jnp.pad(flat, [(0, 0)] * len(lead) + [(0, padded - n)])
    return flat.reshape(lead + (padded // LANES, LANES))


def _unpack(flat2d, shapes):
    flat = flat2d.reshape(-1)
    out, o = [], 0
    for s in shapes:
        n = math.prod(s)
        out.append(flat[o:o + n].reshape(s))
        o += n
    return out


def _full_from_shards(g, axis):
    parts = jnp.moveaxis(g, 0, axis)
    shp = list(g.shape[1:])
    shp[axis] *= g.shape[0]
    return parts.reshape(shp)


def _pieces_from_full(full, axis, k=N_DEV):
    shp = list(full.shape)
    n = shp[axis] // k
    t = full.reshape(shp[:axis] + [k, n] + shp[axis + 1:])
    return jnp.moveaxis(t, axis, 0)


def kernel(x, p, mix_w_in, pool_w, pool_scale, conv_dw_w, conv_dw_b, conv_ln_g, conv_ln_b, mix_w_out, attn_w_qkv, attn_rel_bias, attn_w_o, ln_mix_g, ln_mix_b, ffn_w_up, ffn_dw_w, ffn_dw_b, ffn_w_down, ple_w_proj, ple_w_gate, ple_b_gate, ln_ffn_g, ln_ffn_b, loss_target, m_mix_w_in, m_pool_w, m_pool_scale, m_conv_dw_w, m_conv_dw_b, m_conv_ln_g, m_conv_ln_b, m_mix_w_out, m_attn_w_qkv, m_attn_rel_bias, m_attn_w_o, m_ln_mix_g, m_ln_mix_b, m_ffn_w_up, m_ffn_dw_w, m_ffn_dw_b, m_ffn_w_down, m_ple_w_proj, m_ple_w_gate, m_ple_b_gate, m_ln_ffn_g, m_ln_ffn_b, v_mix_w_in, v_pool_w, v_pool_scale, v_conv_dw_w, v_conv_dw_b, v_conv_ln_g, v_conv_ln_b, v_mix_w_out, v_attn_w_qkv, v_attn_rel_bias, v_attn_w_o, v_ln_mix_g, v_ln_mix_b, v_ffn_w_up, v_ffn_dw_w, v_ffn_dw_b, v_ffn_w_down, v_ple_w_proj, v_ple_w_gate, v_ple_b_gate, v_ln_ffn_g, v_ln_ffn_b):
    a = dict(locals())
    sh_names = list(_SHARDED)
    names = sh_names + list(_REPLICATED)
    wts = {n: a[n] for n in names}
    mom = {n: a["m_" + n] for n in names}
    var = {n: a["v_" + n] for n in names}

    for n in _TRANSPOSED:
        wts[n], mom[n], var[n] = (jnp.swapaxes(d[n], 1, 2) for d in (wts, mom, var))
    gather = {}
    token = None
    for group, items in _GATHER_GROUPS:
        gather[group] = _exchange_start([wts[n][l].astype(dt) for n, l, dt, _ in items], [pl_ for *_, pl_ in items],
                                        token, name="gather_start_" + group)
        token = gather[group]["token"]

    w = dict(pool_w=pool_w[0], pool_scale=pool_scale[0], conv_dw_b=conv_dw_b[0], conv_ln_g=conv_ln_g[0],
             conv_ln_b=conv_ln_b[0], attn_rel_bias=attn_rel_bias[0], ln_mix_g=ln_mix_g, ln_mix_b=ln_mix_b,
             ffn_dw_b=ffn_dw_b, ple_b_gate=ple_b_gate, ln_ffn_g=ln_ffn_g, ln_ffn_b=ln_ffn_b)
    for n in ("ffn_up_t", "ffn_dw_w", "ffn_w_down", "ple_w_gate", "ple_w_proj"):
        w[n] = [None, None]

    def ready(group, after):
        got = _exchange_wait(gather[group], token if after is None else after, name="gather_wait_" + group)
        if group == "mix":
            w["mix_w_in_t"], w["conv_dw_w"] = got[0], _full_from_shards(got[1], 1)
        elif group == "mixo":
            (w["mix_w_out"],) = got
        elif group == "attn":
            w["attn_w_qkv"], w["attn_w_o"] = got
        elif group[:2] == "up":
            l = int(group[2])
            w["ffn_up_t"][l], w["ffn_dw_w"][l] = got[0], _full_from_shards(got[1], 1)
        else:
            l = int(group[2])
            w["ffn_w_down"][l], w["ple_w_gate"][l], w["ple_w_proj"][l] = got

    scatter = {}

    def emit(group, gr):
        if group[:3] == "ffn":
            l = int(group[3])
            pieces = [_pieces_from_full(gr["ffn_up_t"][l], 0),
                      _pieces_from_full(gr["ffn_dw_w"][l], 1), _pieces_from_full(gr["ffn_w_down"][l], 0),
                      _pieces_from_full(gr["ple_w_gate"][l], 0), gr["ple_w_proj"][l]]
        elif group == "attn":
            pieces = [gr["attn_w_qkv"], _pieces_from_full(gr["attn_w_o"], 0)]
        else:
            pieces = [_pieces_from_full(gr["mix_w_in_t"], 0), _pieces_from_full(gr["conv_dw_w"], 1),
                      _pieces_from_full(gr["mix_w_out"], 0)]
        scatter[group] = _exchange_start([a.astype(BF16) for a in pieces], ["pieces"] * len(pieces), None,
                                         name="grad_start_" + group)
        if group != "mix":
            return scatter[group]["token"]
        gfull = dict(
            pool_w=gr["pool_w"][None], pool_scale=gr["pool_scale"][None], conv_dw_b=gr["conv_dw_b"][None],
            conv_ln_g=gr["conv_ln_g"][None], conv_ln_b=gr["conv_ln_b"][None],
            attn_rel_bias=gr["attn_rel_bias"][None], ln_mix_g=jnp.stack(gr["ln_mix_g"]),
            ln_mix_b=jnp.stack(gr["ln_mix_b"]), ffn_dw_b=jnp.stack(gr["ffn_dw_b"]),
            ple_b_gate=jnp.stack(gr["ple_b_gate"]), ln_ffn_g=jnp.stack(gr["ln_ffn_g"]),
            ln_ffn_b=jnp.stack(gr["ln_ffn_b"]))
        rep_send = _pack_rows([gfull[n].reshape(-1) for n in _REPLICATED], 8, F32)
        scatter["replicated"] = _exchange_start([rep_send], ["stack"], scatter[group]["token"],
                                                name="grad_start_replicated")
        return scatter["replicated"]["token"]

    loss_part, grad_x, gr = _local_step(x[0], p[:, 0], loss_target[0], w, ready, emit)
    loss = lax.psum(loss_part, ("x", "y", "c"))

    group_weights = {"ffn1": (("ffn_w_up", 1), ("ffn_dw_w", 1), ("ffn_w_down", 1), ("ple_w_gate", 1), ("ple_w_proj", 1)),
                     "attn": (("attn_w_qkv", 0), ("attn_w_o", 0)),
                     "ffn0": (("ffn_w_up", 0), ("ffn_dw_w", 0), ("ffn_w_down", 0), ("ple_w_gate", 0), ("ple_w_proj", 0)),
                     "mix": (("mix_w_in", 0), ("conv_dw_w", 0), ("mix_w_out", 0))}
    updated = {}
    after = grad_x
    for group in ("ffn1", "attn", "ffn0", "mix"):
        recv = _exchange_wait(scatter[group], after, name="grad_wait_" + group)
        for (n, l), r in zip(group_weights[group], recv):
            updated[n] = _adamw(r, wts[n], mom[n], var[n], layer=l, into=updated.get(n), name=f"adamw_{n}{l}")
            after = updated[n][0]
    res = [{n: jnp.swapaxes(updated[n][k], 1, 2) if n in _TRANSPOSED else updated[n][k] for n in sh_names}
           for k in range(4)]
    (rep_recv,) = _exchange_wait(scatter["replicated"], after, name="grad_wait_replicated")

    def flat_state(d):
        return _pack_rows([d[n].reshape(-1) for n in _REPLICATED], 8, F32)[None]

    rep_out = _adamw(rep_recv, flat_state(wts), flat_state(mom), flat_state(var), name="adamw_replicated")
    for k in range(4):
        for n, arr in zip(_REPLICATED, _unpack(rep_out[k][0], [wts[n].shape for n in _REPLICATED])):
            res[k][n] = arr
    order = ["mix_w_in", "pool_w", "pool_scale", "conv_dw_w", "conv_dw_b", "conv_ln_g", "conv_ln_b", "mix_w_out",
             "attn_w_qkv", "attn_rel_bias", "attn_w_o", "ln_mix_g", "ln_mix_b", "ffn_w_up", "ffn_dw_w", "ffn_dw_b",
             "ffn_w_down", "ple_w_proj", "ple_w_gate", "ple_b_gate", "ln_ffn_g", "ln_ffn_b"]
    outs = [loss, grad_x[None]]
    for k in range(4):
        outs += [res[k][n] for n in order]
    return tuple(outs)
```

```python
import functools
import math

import jax
import jax.numpy as jnp
from jax import lax
from jax.experimental import pallas as pl
from jax.experimental.pallas import tpu as pltpu

F32 = jnp.float32
BF16 = jnp.bfloat16

N_DEV = 8
D_MODEL = 1024
D_POOL = 512
D_CONV = 512
POOL_WINDOWS = (2, 4, 8, 16)
POOL_GROUP = 128
CONV_KERNEL = 31
CHUNK = 64
HEAD_DIM = 64
N_HEADS = 16
LEFT_CHUNKS = 8
BAND = (LEFT_CHUNKS + 1) * CHUNK
MAX_REL = 256
D_FF = 2816
PLE_DIM = 256
ALPHA = 4.0 ** 0.25
LN_EPS = 1e-5
NEG_INF = -1e30
ADAM_LR, ADAM_B1, ADAM_B2, ADAM_EPS, ADAM_WD, ADAM_STEP = 0.001, 0.9, 0.999, 1e-08, 0.01, 10

Q_BLOCK = 4 * CHUNK
KV_PAD = LEFT_CHUNKS * CHUNK
KV_SPAN = KV_PAD + Q_BLOCK
CONV_HALO = 32
FFN_HALO = 16
SUB_ROWS, SUB_LANES = 64, 128
LANES = 1024
VMEM_LIMIT = 56 * 1024 * 1024


def _cparams(sem=None):
    return pltpu.CompilerParams(dimension_semantics=sem, vmem_limit_bytes=VMEM_LIMIT)


def _tile(dim, pref):
    if dim <= pref:
        return dim
    t = pref - pref % 128
    while t >= 128:
        if dim % t == 0:
            return t
        t -= 128
    return dim


def _sigmoid(x):
    return 1.0 / (1.0 + jnp.exp(-x))


def _bdot(a, b, dn=(((1,), (0,)), ((), ()))):
    return lax.dot_general(a.astype(BF16), b.astype(BF16), dn, preferred_element_type=F32)


WHOLE = (0, 1)
NT = (((1,), (1,)), ((), ()))
TN = (((0,), (0,)), ((), ()))


def _wgrad(a, b, *, tm=1024, tn=1024, tk=2048, piece=None, part=(0, 1), into=None, name):
    K, M = a.shape
    kb, N = b.shape
    assert K == kb, (a.shape, b.shape)
    tm, tn, tk = _tile(M, tm), _tile(N, tn), _tile(K, tk)
    nk = K // tk
    per = 1 if piece is None else tn // piece
    assert piece is None or tn == per * piece

    def body(a_ref, b_ref, *rest):
        o_ref, acc = rest[-2:]
        k = pl.program_id(2)

        @pl.when(k == 0)
        def _():
            acc[...] = jnp.zeros_like(acc)

        acc[...] += _bdot(a_ref[...], b_ref[...], TN)

        @pl.when(k == nk - 1)
        def _():
            if piece is None:
                o_ref[...] = acc[...].astype(BF16)
            else:
                for s in range(per):
                    o_ref[s] = acc[:, s * piece:(s + 1) * piece].astype(BF16)

    if piece is None:
        first = part[0] * (M // tm)
        out_shape = (part[1] * M, N)
        out_spec = pl.BlockSpec((tm, tn), lambda i, j, k: (first + i, j))
    else:
        out_shape, out_spec = (N // piece, M, piece), pl.BlockSpec((per, tm, piece), lambda i, j, k: (j, i, 0))
    others = [] if into is None else [into]
    return pl.pallas_call(
        body,
        out_shape=jax.ShapeDtypeStruct(out_shape, BF16),
        grid=(M // tm, N // tn, nk),
        in_specs=[pl.BlockSpec((tk, tm), lambda i, j, k: (k, i)), pl.BlockSpec((tk, tn), lambda i, j, k: (k, j))]
        + [pl.BlockSpec(memory_space=pl.ANY)] * len(others),
        out_specs=out_spec,
        input_output_aliases={2: 0} if others else {},
        scratch_shapes=[pltpu.VMEM((tm, tn), F32)],
        compiler_params=_cparams(("parallel", "parallel", "arbitrary")),
        name=name,
    )(a, b, *others)


def _mm_rows(pairs, *, add=None, add_scale=1.0, out_dtype=F32, tm=512, dep=None, ln_bwd=None, name):
    M = pairs[0][0].shape[0]
    n = len(pairs)
    has_add = add is not None
    has_ple = ln_bwd is not None and len(ln_bwd) == 4
    w_rows = [w_.shape[0] // part[1] for _, w_, _, part in pairs]
    N = w_rows[0] if pairs[0][2] else pairs[0][1].shape[1]

    def body(*refs):
        acc = None
        for i, (_, _, tr, _) in enumerate(pairs):
            part = _bdot(refs[2 * i][...], refs[2 * i + 1][...], NT if tr else (((1,), (0,)), ((), ())))
            acc = part if acc is None else acc + part
        if has_add:
            acc = acc + add_scale * refs[2 * n][...]
        if ln_bwd is None:
            refs[-1][...] = acc.astype(out_dtype)
            return
        first = 2 * n + has_add
        z_ref, g_ref = refs[first], refs[first + 1]
        outs = refs[-(7 if has_ple else 4):]
        dz_ref, dzb_ref, dg_ref, db_ref = outs[:4]

        @pl.when(pl.program_id(0) == 0)
        def _():
            for sums in outs[2:4] + outs[6:]:
                sums[...] = jnp.zeros_like(sums)

        dg_acc = jnp.zeros((8, N), F32)
        db_acc = jnp.zeros((8, N), F32)
        dbg_acc = jnp.zeros((8, N), F32)
        for r0 in range(0, tm, LN_ROWS):
            rows = pl.ds(r0, LN_ROWS)
            do = acc[r0:r0 + LN_ROWS]
            dz, xh = _ln_bwd_rows(z_ref[rows, :], g_ref[...], do)
            dz_ref[rows, :] = dz
            dzb_ref[rows, :] = dz.astype(BF16)
            dg_acc = dg_acc + jnp.sum((do * xh).reshape(LN_ROWS // 8, 8, N), axis=0)
            db_acc = db_acc + jnp.sum(do.reshape(LN_ROWS // 8, 8, N), axis=0)
            if has_ple:
                ds, dp = _ple_bwd_rows(dz, refs[first + 2][rows, :], refs[first + 3][rows, :])
                outs[4][rows, :] = ds.astype(BF16)
                outs[5][rows, :] = dp.astype(BF16)
                dbg_acc = dbg_acc + jnp.sum(ds.reshape(LN_ROWS // 8, 8, N), axis=0)
        dg_ref[...] += jnp.sum(dg_acc, axis=0, keepdims=True)
        db_ref[...] += jnp.sum(db_acc, axis=0, keepdims=True)
        if has_ple:
            outs[6][...] += jnp.sum(dbg_acc, axis=0, keepdims=True)

    in_specs, args = [], []
    for (a, w_, _, part), rows in zip(pairs, w_rows):
        in_specs += [pl.BlockSpec((tm, a.shape[1]), lambda i: (i, 0)),
                     pl.BlockSpec((rows, w_.shape[1]), functools.partial(lambda i, j: (j, 0), j=part[0]))]
        args += [a, w_]
    row = pl.BlockSpec((tm, N), lambda i: (i, 0))
    fix = pl.BlockSpec((1, N), lambda i: (0, 0))
    if has_add:
        in_specs.append(row)
        args.append(add)
    if ln_bwd is not None:
        in_specs += [row, fix] + [row] * (len(ln_bwd) - 2)
        args += [ln_bwd[0], ln_bwd[1].reshape(1, N), *ln_bwd[2:]]
    if dep is not None:
        in_specs.append(pl.BlockSpec(memory_space=pl.ANY))
        args.append(dep)
    if ln_bwd is None:
        out_shape, out_specs = jax.ShapeDtypeStruct((M, N), out_dtype), row
    else:
        out_shape = [jax.ShapeDtypeStruct((M, N), F32), jax.ShapeDtypeStruct((M, N), BF16),
                     jax.ShapeDtypeStruct((1, N), F32), jax.ShapeDtypeStruct((1, N), F32)]
        out_specs = [row, row, fix, fix]
        if has_ple:
            out_shape += [jax.ShapeDtypeStruct((M, N), BF16), jax.ShapeDtypeStruct((M, N), BF16),
                          jax.ShapeDtypeStruct((1, N), F32)]
            out_specs += [row, row, fix]
    return pl.pallas_call(
        body,
        out_shape=out_shape,
        grid=(M // tm,),
        in_specs=in_specs,
        out_specs=out_specs,
        compiler_params=_cparams(("parallel",) if ln_bwd is None else ("arbitrary",)),
        name=name,
    )(*args)


def _ln_bwd_rows(zt, g, do):
    zc = zt - jnp.mean(zt, axis=-1, keepdims=True)
    rstd = lax.rsqrt(jnp.mean(zc * zc, axis=-1, keepdims=True) + LN_EPS)
    xh = zc * rstd
    dxh = do * g
    return rstd * (dxh - jnp.mean(dxh, axis=-1, keepdims=True) - xh * jnp.mean(dxh * xh, axis=-1, keepdims=True)), xh


def _layer_norm_rows(z, g, b):
    mu = jnp.mean(z, axis=-1, keepdims=True)
    zc = z - mu
    var = jnp.mean(zc * zc, axis=-1, keepdims=True)
    return zc * lax.rsqrt(var + LN_EPS) * g + b


def _proj_ln(res, a, w, ln_g, ln_b, *, ple=None, ts=512, name):
    S, D = res.shape
    ka = a.shape[1]
    has_ple = ple is not None
    row = lambda i: (i, 0)
    fix = lambda i: (0, 0)

    def body(*refs):
        if has_ple:
            (res_ref, a_ref, w_ref, g_ref, b_ref, wg_ref, bg_ref, p_ref, wp_ref, z_ref, r_ref, rb_ref, gate_ref,
             proj_ref, acc) = refs
        else:
            res_ref, a_ref, w_ref, g_ref, b_ref, z_ref, r_ref, rb_ref, acc = refs
        acc[...] = _bdot(a_ref[...], w_ref[...])
        if has_ple:
            gate_ref[...] = _bdot(res_ref[...], wg_ref[...])
            proj_ref[...] = _bdot(p_ref[...], wp_ref[...])
        for r0 in range(0, ts, LN_ROWS):
            rows = pl.ds(r0, LN_ROWS)
            z = ALPHA * res_ref[rows, :] + acc[rows, :]
            if has_ple:
                gate = _sigmoid(gate_ref[rows, :] + bg_ref[...])
                gate_ref[rows, :] = gate
                z = z + gate * proj_ref[rows, :]
            z_ref[rows, :] = z
            r = _layer_norm_rows(z, g_ref[...], b_ref[...])
            r_ref[rows, :] = r
            rb_ref[rows, :] = r.astype(BF16)

    in_specs = [pl.BlockSpec((ts, D), row), pl.BlockSpec((ts, ka), row), pl.BlockSpec((ka, D), fix),
                pl.BlockSpec((1, D), fix), pl.BlockSpec((1, D), fix)]
    args = [res, a, w, ln_g.reshape(1, D), ln_b.reshape(1, D)]
    out_dtypes = [F32, F32, BF16]
    if has_ple:
        wg, bg, p, wp = ple
        in_specs += [pl.BlockSpec((D, D), fix), pl.BlockSpec((1, D), fix), pl.BlockSpec((ts, PLE_DIM), row),
                     pl.BlockSpec((PLE_DIM, D), fix)]
        args += [wg, bg.reshape(1, D), p, wp]
        out_dtypes += [F32, F32]
    return pl.pallas_call(
        body,
        out_shape=[jax.ShapeDtypeStruct((S, D), dt) for dt in out_dtypes],
        grid=(S // ts,),
        in_specs=in_specs,
        out_specs=[pl.BlockSpec((ts, D), row)] * len(out_dtypes),
        scratch_shapes=[pltpu.VMEM((ts, D), F32)],
        compiler_params=_cparams(("parallel",)),
        name=name,
    )(*args)


CONV_ROWS = 32
LN_ROWS = 16


def _shifted_copies(src, dst, rows):
    for c0 in range(0, src.shape[1], SUB_LANES):
        ln = pl.ds(c0, SUB_LANES)
        for r0 in range(0, rows, SUB_ROWS):
            rc = min(SUB_ROWS, rows - r0)
            for b, shifted in enumerate(_rows_ahead(src, r0, rc, ln, range(1, 8))):
                dst[b, pl.ds(r0, rc), ln] = shifted


def _rows_at(src, copies, off, n, ln):
    b = off % 8
    return src[pl.ds(off, n), ln] if b == 0 else copies[b - 1, pl.ds(off - b, n), ln]


def _conv31(stg, gsh, cw_ref, cb_ref, out, rows, first_off):
    for c0 in range(0, D_CONV, SUB_LANES):
        ln = pl.ds(c0, SUB_LANES)
        for r0 in range(0, rows, CONV_ROWS):
            acc = jnp.zeros((CONV_ROWS, SUB_LANES), F32) + cb_ref[:, ln]
            for k in range(CONV_KERNEL):
                acc = acc + cw_ref[k:k + 1, ln] * _rows_at(stg, gsh, first_off + k + r0, CONV_ROWS, ln)
            out[pl.ds(r0, CONV_ROWS), ln] = acc


def _mixer_fwd(u, pool_w, pool_scale, conv_w, conv_b, cln_g, cln_b, *, ts=256):
    S = u.shape[0]
    hb = CONV_HALO
    nh = ts // hb

    def body(u_ref, uh_ref, pw_ref, ps_ref, cw_ref, cb_ref, g_ref, b_ref, y_ref, d_ref, hcs, sta, stg, gsh):
        i = pl.program_id(0)
        first = i == 0
        sta[pl.ds(0, hb), :] = jnp.where(first, 0.0, uh_ref[:, 0:D_POOL])
        sta[pl.ds(hb, ts), :] = u_ref[:, 0:D_POOL]
        glu_h = uh_ref[:, D_POOL:D_POOL + D_CONV] * _sigmoid(uh_ref[:, D_POOL + D_CONV:])
        stg[pl.ds(0, hb), :] = jnp.where(first, 0.0, glu_h)
        stg[pl.ds(hb, ts), :] = u_ref[:, D_POOL:D_POOL + D_CONV] * _sigmoid(u_ref[:, D_POOL + D_CONV:])

        for g, w in enumerate(POOL_WINDOWS):
            lanes = pl.ds(g * POOL_GROUP, POOL_GROUP)
            for r0 in range(0, ts, SUB_ROWS):
                s = None
                for q in range(0, w, 8):
                    for tap in _rows_back(sta, hb + r0 - q, SUB_ROWS, lanes, range(min(8, w - q))):
                        s = tap if s is None else s + tap
                pos = (i * ts + r0 + lax.broadcasted_iota(jnp.int32, (SUB_ROWS, 1), 0) + 1).astype(F32)
                d_g = s / jnp.minimum(pos, float(w)) - sta[pl.ds(hb + r0, SUB_ROWS), lanes]
                d_ref[pl.ds(r0, SUB_ROWS), lanes] = d_g.astype(BF16)
            y_ref[:, lanes] = (_bdot(d_ref[:, lanes], pw_ref[g]) * ps_ref[:, lanes]).astype(BF16)

        _shifted_copies(stg, gsh, hb + ts - 8)
        _conv31(stg, gsh, cw_ref, cb_ref, hcs, ts, hb - (CONV_KERNEL - 1))
        for r0 in range(0, ts, LN_ROWS):
            rows = pl.ds(r0, LN_ROWS)
            ln = _layer_norm_rows(hcs[rows, :], g_ref[...], b_ref[...])
            y_ref[rows, D_POOL:] = (ln * _sigmoid(ln)).astype(BF16)

    fix2 = lambda i: (0, 0)
    return pl.pallas_call(
        body,
        out_shape=[jax.ShapeDtypeStruct((S, D_MODEL), BF16), jax.ShapeDtypeStruct((S, D_POOL), BF16),
                   jax.ShapeDtypeStruct((S, D_CONV), F32)],
        grid=(S // ts,),
        in_specs=[pl.BlockSpec((ts, 3 * D_POOL), lambda i: (i, 0)),
                  pl.BlockSpec((hb, 3 * D_POOL), lambda i: (jnp.maximum(i * nh - 1, 0), 0)),
                  pl.BlockSpec((4, POOL_GROUP, POOL_GROUP), lambda i: (0, 0, 0)),
                  pl.BlockSpec((1, D_POOL), fix2), pl.BlockSpec((CONV_KERNEL, D_CONV), fix2),
                  pl.BlockSpec((1, D_CONV), fix2), pl.BlockSpec((1, D_CONV), fix2), pl.BlockSpec((1, D_CONV), fix2)],
        out_specs=[pl.BlockSpec((ts, D_MODEL), lambda i: (i, 0)), pl.BlockSpec((ts, D_POOL), lambda i: (i, 0)),
                   pl.BlockSpec((ts, D_CONV), lambda i: (i, 0))],
        scratch_shapes=[pltpu.VMEM((hb + ts, D_POOL), F32), pltpu.VMEM((hb + ts, D_CONV), F32),
                        pltpu.VMEM((7, hb + ts - 8, D_CONV), F32)],
        compiler_params=_cparams(("parallel",)),
        name="mixer_fwd",
    )(u, u, pool_w, pool_scale.reshape(1, D_POOL), conv_w, conv_b.reshape(1, D_CONV), cln_g.reshape(1, D_CONV),
      cln_b.reshape(1, D_CONV))


def _mixer_bwd(u, d, hc, dycat, pool_w, pool_scale, conv_w, cln_g, cln_b, *, ts=256):
    S = u.shape[0]
    hb = CONV_HALO
    nh = ts // hb
    n = S // ts
    te = ts + hb
    K = CONV_KERNEL

    def body(u_ref, up_ref, un_ref, d_ref, hc_ref, hcn_ref, dy_ref, dyn_ref, pw_ref, ps_ref, cw_ref, g_ref, b_ref,
             du_ref, dpw_ref, dps_ref, dcw_ref, dcb_ref, dg_ref, db_ref, stg, std, sth, gsh, hsh):
        i = pl.program_id(0)
        first = i == 0
        last = i == n - 1

        @pl.when(first)
        def _():
            dpw_ref[...] = jnp.zeros_like(dpw_ref)
            dps_ref[...] = jnp.zeros_like(dps_ref)
            dcw_ref[...] = jnp.zeros_like(dcw_ref)
            dcb_ref[...] = jnp.zeros_like(dcb_ref)
            dg_ref[...] = jnp.zeros_like(dg_ref)
            db_ref[...] = jnp.zeros_like(db_ref)

        pos_e = (i * ts + lax.broadcasted_iota(jnp.int32, (te, 1), 0) + 1).astype(F32)
        dya = dy_ref[:, 0:D_POOL]
        dya_n = jnp.where(last, 0.0, dyn_ref[:, 0:D_POOL])
        for g, w in enumerate(POOL_WINDOWS):
            lanes = pl.ds(g * POOL_GROUP, POOL_GROUP)
            sl = slice(g * POOL_GROUP, (g + 1) * POOL_GROUP)
            pw = pw_ref[g]
            scale = ps_ref[:, lanes]
            d_g = d_ref[:, lanes]
            pre = _bdot(d_g, pw)
            dps_ref[:, lanes] += jnp.sum(dya[:, sl] * pre, axis=0, keepdims=True)
            dys = dya[:, sl] * scale
            dpw_ref[g] += _bdot(d_g, dys, TN)
            dys_e = jnp.concatenate([dys, dya_n[:, sl] * scale], axis=0)
            dd = _bdot(dys_e, pw, NT)
            std[:, lanes] = dd / jnp.minimum(pos_e, float(w))
            for r0 in range(0, ts, SUB_ROWS):
                da = -dd[r0:r0 + SUB_ROWS]
                for q in range(0, w, 8):
                    for tap in _rows_ahead(std, r0 + q, SUB_ROWS, lanes, range(min(8, w - q))):
                        da = da + tap
                du_ref[pl.ds(r0, SUB_ROWS), lanes] = da.astype(BF16)

        glu_p = up_ref[:, D_POOL:D_POOL + D_CONV] * _sigmoid(up_ref[:, D_POOL + D_CONV:])
        stg[pl.ds(0, hb), :] = jnp.where(first, 0.0, glu_p)
        bv = u_ref[:, D_POOL:D_POOL + D_CONV]
        sg = _sigmoid(u_ref[:, D_POOL + D_CONV:])
        stg[pl.ds(hb, ts), :] = bv * sg
        glu_n = un_ref[:, D_POOL:D_POOL + D_CONV] * _sigmoid(un_ref[:, D_POOL + D_CONV:])
        stg[pl.ds(hb + ts, hb), :] = jnp.where(last, 0.0, glu_n)
        _shifted_copies(stg, gsh, hb + te - 8)

        sums = [jnp.zeros((8, D_CONV), F32) for _ in range(3)]
        for r0 in range(0, te, LN_ROWS):
            rows = pl.ds(r0, LN_ROWS)
            hc = hc_ref[rows, :] if r0 < ts else hcn_ref[pl.ds(r0 - ts, LN_ROWS), :]
            hcc = hc - jnp.mean(hc, axis=-1, keepdims=True)
            rstd = lax.rsqrt(jnp.mean(hcc * hcc, axis=-1, keepdims=True) + LN_EPS)
            xh = hcc * rstd
            ln = xh * g_ref[...] + b_ref[...]
            sl_ = _sigmoid(ln)
            if r0 < ts:
                dyb = dy_ref[rows, D_POOL:]
            else:
                dyb = jnp.where(last, 0.0, dyn_ref[pl.ds(r0 - ts, LN_ROWS), D_POOL:])
            dln = dyb * (sl_ * (1.0 + ln * (1.0 - sl_)))
            dxh = dln * g_ref[...]
            dhc = rstd * (dxh - jnp.mean(dxh, axis=-1, keepdims=True)
                          - xh * jnp.mean(dxh * xh, axis=-1, keepdims=True))
            sth[rows, :] = dhc
            if r0 < ts:
                for n_, term in enumerate((dln * xh, dln, dhc)):
                    sums[n_] = sums[n_] + jnp.sum(term.reshape(LN_ROWS // 8, 8, D_CONV), axis=0)
        dg_ref[...] += jnp.sum(sums[0], axis=0, keepdims=True)
        db_ref[...] += jnp.sum(sums[1], axis=0, keepdims=True)
        dcb_ref[...] += jnp.sum(sums[2], axis=0, keepdims=True)

        _shifted_copies(sth, hsh, te - 8)
        for c0 in range(0, D_CONV, SUB_LANES):
            ln_ = pl.ds(c0, SUB_LANES)
            for r0 in range(0, ts, CONV_ROWS):
                rows = pl.ds(r0, CONV_ROWS)
                dglu = jnp.zeros((CONV_ROWS, SUB_LANES), F32)
                for k in range(K):
                    dglu = dglu + cw_ref[k:k + 1, ln_] * _rows_at(sth, hsh, K - 1 - k + r0, CONV_ROWS, ln_)
                bv = u_ref[rows, pl.ds(D_POOL + c0, SUB_LANES)]
                sg = _sigmoid(u_ref[rows, pl.ds(D_POOL + D_CONV + c0, SUB_LANES)])
                du_ref[rows, pl.ds(D_POOL + c0, SUB_LANES)] = (dglu * sg).astype(BF16)
                du_ref[rows, pl.ds(D_POOL + D_CONV + c0, SUB_LANES)] = (dglu * bv * sg * (1.0 - sg)).astype(BF16)
            for k in range(K):
                tap = jnp.zeros((8, SUB_LANES), F32)
                for r0 in range(0, ts, CONV_ROWS):
                    prod = sth[pl.ds(r0, CONV_ROWS), ln_] * _rows_at(stg, gsh, hb - (K - 1) + k + r0, CONV_ROWS, ln_)
                    tap = tap + jnp.sum(prod.reshape(CONV_ROWS // 8, 8, SUB_LANES), axis=0)
                dcw_ref[k:k + 1, ln_] += jnp.sum(tap, axis=0, keepdims=True)

    fix2 = lambda i: (0, 0)
    prev = lambda i: (jnp.maximum(i * nh - 1, 0), 0)
    nxt = lambda i: (jnp.minimum((i + 1) * nh, S // hb - 1), 0)
    return pl.pallas_call(
        body,
        out_shape=[jax.ShapeDtypeStruct((S, 3 * D_POOL), BF16),
                   jax.ShapeDtypeStruct((4, POOL_GROUP, POOL_GROUP), F32),
                   jax.ShapeDtypeStruct((1, D_POOL), F32),
                   jax.ShapeDtypeStruct((K, D_CONV), F32),
                   jax.ShapeDtypeStruct((1, D_CONV), F32),
                   jax.ShapeDtypeStruct((1, D_CONV), F32),
                   jax.ShapeDtypeStruct((1, D_CONV), F32)],
        grid=(n,),
        in_specs=[pl.BlockSpec((ts, 3 * D_POOL), lambda i: (i, 0)),
                  pl.BlockSpec((hb, 3 * D_POOL), prev),
                  pl.BlockSpec((hb, 3 * D_POOL), nxt),
                  pl.BlockSpec((ts, D_POOL), lambda i: (i, 0)),
                  pl.BlockSpec((ts, D_CONV), lambda i: (i, 0)),
                  pl.BlockSpec((hb, D_CONV), nxt),
                  pl.BlockSpec((ts, D_MODEL), lambda i: (i, 0)),
                  pl.BlockSpec((hb, D_MODEL), nxt),
                  pl.BlockSpec((4, POOL_GROUP, POOL_GROUP), lambda i: (0, 0, 0)),
                  pl.BlockSpec((1, D_POOL), fix2), pl.BlockSpec((K, D_CONV), fix2),
                  pl.BlockSpec((1, D_CONV), fix2), pl.BlockSpec((1, D_CONV), fix2)],
        out_specs=[pl.BlockSpec((ts, 3 * D_POOL), lambda i: (i, 0)),
                   pl.BlockSpec((4, POOL_GROUP, POOL_GROUP), lambda i: (0, 0, 0)),
                   pl.BlockSpec((1, D_POOL), fix2), pl.BlockSpec((K, D_CONV), fix2),
                   pl.BlockSpec((1, D_CONV), fix2), pl.BlockSpec((1, D_CONV), fix2), pl.BlockSpec((1, D_CONV), fix2)],
        scratch_shapes=[pltpu.VMEM((hb + ts + hb, D_CONV), F32), pltpu.VMEM((te, D_POOL), F32),
                        pltpu.VMEM((te, D_CONV), F32), pltpu.VMEM((7, hb + te - 8, D_CONV), F32),
                        pltpu.VMEM((7, te - 8, D_CONV), F32)],
        compiler_params=_cparams(("arbitrary",)),
        name="mixer_bwd",
    )(u, u, u, d, hc, hc, dycat, dycat, pool_w, pool_scale.reshape(1, D_POOL), conv_w, cln_g.reshape(1, D_CONV),
      cln_b.reshape(1, D_CONV))


_GELU_C = math.sqrt(2.0 / math.pi)


def _gelu_parts(x):
    inner = _GELU_C * (x + 0.044715 * x * x * x)
    th = jnp.tanh(inner)
    ge = 0.5 * x * (1.0 + th)
    dge = 0.5 * (1.0 + th) + 0.5 * x * (1.0 - th * th) * (_GELU_C * (1.0 + 3.0 * 0.044715 * x * x))
    return ge, dge


def _rows_back(ref, r, n, ln, shifts):
    ext = ref[pl.ds(r - 8, n + 8), ln]
    return [(pltpu.roll(ext, s, 0) if s else ext)[8:] for s in shifts]


def _rows_ahead(ref, r, n, ln, shifts):
    ext = ref[pl.ds(r, n + 8), ln]
    return [(pltpu.roll(ext, n + 8 - s, 0) if s else ext)[:n] for s in shifts]


def _ffn_act_fwd(gate, val, dw_w, dw_b, *, ts=512, tc=2816, name):
    S, F = gate.shape
    hb = FFN_HALO
    nh = ts // hb
    tc = _tile(F, tc)

    def body(g_ref, gh_ref, v_ref, w_ref, b_ref, h_ref, st):
        i = pl.program_id(0)
        st[pl.ds(0, hb), :] = jnp.where(i == 0, 0.0, gh_ref[...].astype(F32))
        st[pl.ds(hb, ts), :] = g_ref[...].astype(F32)
        for c0 in range(0, tc, SUB_LANES):
            ln = pl.ds(c0, SUB_LANES)
            w0, w1, w2, b = w_ref[0:1, ln], w_ref[1:2, ln], w_ref[2:3, ln], b_ref[:, ln]
            for r0 in range(0, ts, SUB_ROWS):
                taps = _rows_back(st, hb + r0, SUB_ROWS, ln, (2, 1, 0))
                gc = b + w0 * taps[0] + w1 * taps[1] + w2 * taps[2]
                ge, _ = _gelu_parts(gc)
                rows = pl.ds(r0, SUB_ROWS)
                h_ref[rows, ln] = (ge * v_ref[rows, ln].astype(F32)).astype(BF16)

    return pl.pallas_call(
        body,
        out_shape=jax.ShapeDtypeStruct((S, F), BF16),
        grid=(S // ts, F // tc),
        in_specs=[pl.BlockSpec((ts, tc), lambda i, j: (i, j)),
                  pl.BlockSpec((hb, tc), lambda i, j: (jnp.maximum(i * nh - 1, 0), j)),
                  pl.BlockSpec((ts, tc), lambda i, j: (i, j)),
                  pl.BlockSpec((3, tc), lambda i, j: (0, j)),
                  pl.BlockSpec((1, tc), lambda i, j: (0, j))],
        out_specs=pl.BlockSpec((ts, tc), lambda i, j: (i, j)),
        scratch_shapes=[pltpu.VMEM((hb + ts, tc), F32)],
        compiler_params=_cparams(("parallel", "parallel")),
        name=name,
    )(gate, gate, val, dw_w, dw_b.reshape(1, F))


def _ffn_act_bwd(gate, val, dh, dw_w, dw_b, *, ts=512, tc=2816, name):
    S, F = gate.shape
    hb = FFN_HALO
    nh = ts // hb
    n = S // ts
    te = ts + hb
    tc = _tile(F, tc)

    def body(g_ref, gp_ref, gn_ref, v_ref, vn_ref, dh_ref, dhn_ref, w_ref, b_ref,
             dg_ref, dv_ref, dw_ref, db_ref, st, sd):
        i = pl.program_id(1)
        first = i == 0
        last = i == n - 1

        @pl.when(first)
        def _():
            dw_ref[...] = jnp.zeros_like(dw_ref)
            db_ref[...] = jnp.zeros_like(db_ref)

        st[pl.ds(0, hb), :] = jnp.where(first, 0.0, gp_ref[...].astype(F32))
        st[pl.ds(hb, ts), :] = g_ref[...].astype(F32)
        st[pl.ds(hb + ts, hb), :] = jnp.where(last, 0.0, gn_ref[...].astype(F32))
        for c0 in range(0, tc, SUB_LANES):
            ln = pl.ds(c0, SUB_LANES)
            w0, w1, w2, b = w_ref[0:1, ln], w_ref[1:2, ln], w_ref[2:3, ln], b_ref[:, ln]
            db_acc = jnp.zeros((8, SUB_LANES), F32)
            dw_acc = [jnp.zeros((8, SUB_LANES), F32) for _ in range(3)]
            for r0 in range(0, te, SUB_ROWS):
                rc = min(SUB_ROWS, te - r0)
                taps = _rows_back(st, hb + r0, rc, ln, (2, 1, 0))
                gc = b + w0 * taps[0] + w1 * taps[1] + w2 * taps[2]
                ge, dge = _gelu_parts(gc)
                if r0 < ts:
                    rows = pl.ds(r0, rc)
                    val, dh = v_ref[rows, ln].astype(F32), dh_ref[rows, ln].astype(F32)
                else:
                    val = jnp.where(last, 0.0, vn_ref[:, ln].astype(F32)[0:rc])
                    dh = jnp.where(last, 0.0, dhn_ref[:, ln].astype(F32)[0:rc])
                dgc = dh * val * dge
                sd[pl.ds(r0, rc), ln] = dgc
                if r0 < ts:
                    dv_ref[rows, ln] = (dh * ge).astype(BF16)
                    db_acc = db_acc + jnp.sum(dgc.reshape(rc // 8, 8, SUB_LANES), axis=0)
                    for k in range(3):
                        dw_acc[k] = dw_acc[k] + jnp.sum((dgc * taps[k]).reshape(rc // 8, 8, SUB_LANES), axis=0)
            db_ref[:, ln] += jnp.sum(db_acc, axis=0, keepdims=True)
            for k in range(3):
                dw_ref[k:k + 1, ln] += jnp.sum(dw_acc[k], axis=0, keepdims=True)
            for r0 in range(0, ts, SUB_ROWS):
                ahead = _rows_ahead(sd, r0, SUB_ROWS, ln, (2, 1, 0))
                dg_ref[pl.ds(r0, SUB_ROWS), ln] = (w0 * ahead[0] + w1 * ahead[1] + w2 * ahead[2]).astype(BF16)

    cur = lambda j, i: (i, j)
    prev = lambda j, i: (jnp.maximum(i * nh - 1, 0), j)
    nxt = lambda j, i: (jnp.minimum((i + 1) * nh, S // hb - 1), j)
    return pl.pallas_call(
        body,
        out_shape=[jax.ShapeDtypeStruct((S, F), BF16), jax.ShapeDtypeStruct((S, F), BF16),
                   jax.ShapeDtypeStruct((3, F), F32), jax.ShapeDtypeStruct((1, F), F32)],
        grid=(F // tc, n),
        in_specs=[pl.BlockSpec((ts, tc), cur), pl.BlockSpec((hb, tc), prev), pl.BlockSpec((hb, tc), nxt),
                  pl.BlockSpec((ts, tc), cur), pl.BlockSpec((hb, tc), nxt),
                  pl.BlockSpec((ts, tc), cur), pl.BlockSpec((hb, tc), nxt),
                  pl.BlockSpec((3, tc), lambda j, i: (0, j)), pl.BlockSpec((1, tc), lambda j, i: (0, j))],
        out_specs=[pl.BlockSpec((ts, tc), cur), pl.BlockSpec((ts, tc), cur),
                   pl.BlockSpec((3, tc), lambda j, i: (0, j)), pl.BlockSpec((1, tc), lambda j, i: (0, j))],
        scratch_shapes=[pltpu.VMEM((hb + ts + hb, tc), F32), pltpu.VMEM((te, tc), F32)],
        compiler_params=_cparams(("parallel", "arbitrary")),
        name=name,
    )(gate, gate, gate, val, val, dh, dh, dw_w, dw_b.reshape(1, F))


def _ple_bwd_rows(dz, gate, proj):
    return dz * proj * gate * (1.0 - gate), dz * gate


def _loss_ln_bwd(z, ln_g, ln_b, target, gate, proj, *, ts=512, name):
    S, D = z.shape

    def body(z_ref, g_ref, b_ref, t_ref, gate_ref, proj_ref, dz_ref, dzb_ref, dg_ref, db_ref, loss_ref, ds_ref,
             dp_ref, dbg_ref):
        i = pl.program_id(0)

        @pl.when(i == 0)
        def _():
            dg_ref[...] = jnp.zeros_like(dg_ref)
            db_ref[...] = jnp.zeros_like(db_ref)
            loss_ref[...] = jnp.zeros_like(loss_ref)
            dbg_ref[...] = jnp.zeros_like(dbg_ref)

        dg_acc = jnp.zeros((8, D), F32)
        db_acc = jnp.zeros((8, D), F32)
        dbg_acc = jnp.zeros((8, D), F32)
        loss_acc = jnp.zeros((1, 1), F32)
        for r0 in range(0, ts, LN_ROWS):
            rows = pl.ds(r0, LN_ROWS)
            zt = z_ref[rows, :]
            err = _layer_norm_rows(zt, g_ref[...], b_ref[...]) - t_ref[rows, :]
            loss_acc = loss_acc + 0.5 * jnp.sum(jnp.mean(err * err, axis=-1, keepdims=True), keepdims=True)
            do = err * (1.0 / D)
            dz, xh = _ln_bwd_rows(zt, g_ref[...], do)
            dg_acc = dg_acc + jnp.sum((do * xh).reshape(LN_ROWS // 8, 8, D), axis=0)
            db_acc = db_acc + jnp.sum(do.reshape(LN_ROWS // 8, 8, D), axis=0)
            dz_ref[rows, :] = dz
            dzb_ref[rows, :] = dz.astype(BF16)
            ds, dp = _ple_bwd_rows(dz, gate_ref[rows, :], proj_ref[rows, :])
            ds_ref[rows, :] = ds.astype(BF16)
            dp_ref[rows, :] = dp.astype(BF16)
            dbg_acc = dbg_acc + jnp.sum(ds.reshape(LN_ROWS // 8, 8, D), axis=0)
        dg_ref[...] += jnp.sum(dg_acc, axis=0, keepdims=True)
        db_ref[...] += jnp.sum(db_acc, axis=0, keepdims=True)
        dbg_ref[...] += jnp.sum(dbg_acc, axis=0, keepdims=True)
        loss_ref[...] += loss_acc

    row = pl.BlockSpec((ts, D), lambda i: (i, 0))
    fix = pl.BlockSpec((1, D), lambda i: (0, 0))
    return pl.pallas_call(
        body,
        out_shape=[jax.ShapeDtypeStruct((S, D), F32), jax.ShapeDtypeStruct((S, D), BF16),
                   jax.ShapeDtypeStruct((1, D), F32), jax.ShapeDtypeStruct((1, D), F32),
                   jax.ShapeDtypeStruct((8, 128), F32), jax.ShapeDtypeStruct((S, D), BF16),
                   jax.ShapeDtypeStruct((S, D), BF16), jax.ShapeDtypeStruct((1, D), F32)],
        grid=(S // ts,),
        in_specs=[row, fix, fix, row, row, row],
        out_specs=[row, row, fix, fix, pl.BlockSpec((8, 128), lambda i: (0, 0)), row, row, fix],
        compiler_params=_cparams(("arbitrary",)),
        name=name,
    )(z, ln_g.reshape(1, D), ln_b.reshape(1, D), target, gate, proj)


HEADS_PER_STEP = 4
HEAD_LANES = HEADS_PER_STEP * HEAD_DIM


ATT_ROWS = 32
ATT_SCALE = HEAD_DIM ** -0.5


def _softmax_piece(scores, bias, qb):
    s = scores + bias
    kpos = qb * Q_BLOCK + lax.broadcasted_iota(jnp.int32, (1, KV_SPAN), 1)
    s = jnp.where(kpos >= KV_PAD, s, NEG_INF)
    e = jnp.exp(s - jnp.max(s, axis=-1, keepdims=True))
    return e * (1.0 / jnp.sum(e, axis=-1, keepdims=True))


def _head_masks():
    lane = lax.broadcasted_iota(jnp.int32, (1, HEAD_LANES), 1)
    return [(lane >= j * HEAD_DIM) & (lane < (j + 1) * HEAD_DIM) for j in range(HEADS_PER_STEP)]


def _pick_heads(masks, per_head):
    out = per_head[0]
    for mask, x in zip(masks[1:], per_head[1:]):
        out = jnp.where(mask, x, out)
    return out


def _pad_keys(qb, k_ref, v_ref, kp, vp):
    @pl.when(qb == 0)
    def _():
        kp[pl.ds(0, KV_PAD), :] = jnp.zeros((KV_PAD, HEAD_LANES), BF16)
        vp[pl.ds(0, KV_PAD), :] = jnp.zeros((KV_PAD, HEAD_LANES), BF16)
        kp[pl.ds(KV_PAD, k_ref.shape[0]), :] = k_ref[...]
        vp[pl.ds(KV_PAD, v_ref.shape[0]), :] = v_ref[...]


def _attn_fwd(qkv, bias):
    S = qkv.shape[0]
    nhp = N_HEADS // HEADS_PER_STEP

    def body(q_ref, k_ref, v_ref, b_ref, o_ref, kp, vp, p_scr):
        qb = pl.program_id(1)
        _pad_keys(qb, k_ref, v_ref, kp, vp)
        span = pl.ds(pl.multiple_of(qb * Q_BLOCK, Q_BLOCK), KV_SPAN)
        kc, vc = kp[span, :], vp[span, :]
        qt = q_ref[...] * ATT_SCALE
        mine = _head_masks()
        scores = [_bdot(jnp.where(mine[j], qt, jnp.zeros_like(qt)), kc, NT) for j in range(HEADS_PER_STEP)]
        outs = []
        for j in range(HEADS_PER_STEP):
            for r0 in range(0, Q_BLOCK, ATT_ROWS):
                rows = pl.ds(r0, ATT_ROWS)
                p_scr[j, rows, :] = _softmax_piece(scores[j][r0:r0 + ATT_ROWS], b_ref[j, rows, :], qb).astype(BF16)
            outs.append(_bdot(p_scr[j], vc))
        o_ref[...] = _pick_heads(mine, outs).astype(BF16)

    return pl.pallas_call(
        body,
        out_shape=jax.ShapeDtypeStruct((S, D_MODEL), BF16),
        grid=(nhp, S // Q_BLOCK),
        in_specs=[pl.BlockSpec((Q_BLOCK, HEAD_LANES), lambda h, i: (i, h)),
                  pl.BlockSpec((S, HEAD_LANES), lambda h, i: (0, nhp + h)),
                  pl.BlockSpec((S, HEAD_LANES), lambda h, i: (0, 2 * nhp + h)),
                  pl.BlockSpec((HEADS_PER_STEP, Q_BLOCK, KV_SPAN), lambda h, i: (h, 0, 0))],
        out_specs=pl.BlockSpec((Q_BLOCK, HEAD_LANES), lambda h, i: (i, h)),
        scratch_shapes=[pltpu.VMEM((KV_PAD + S, HEAD_LANES), BF16), pltpu.VMEM((KV_PAD + S, HEAD_LANES), BF16),
                        pltpu.VMEM((HEADS_PER_STEP, Q_BLOCK, KV_SPAN), BF16)],
        compiler_params=_cparams(("parallel", "arbitrary")),
        name="attn_fwd",
    )(qkv, qkv, qkv, bias)


def _attn_bwd(qkv, bias, do):
    S = qkv.shape[0]
    nhp = N_HEADS // HEADS_PER_STEP
    nq = S // Q_BLOCK
    scale = HEAD_DIM ** -0.5

    def body(q_ref, k_ref, v_ref, b_ref, do_ref, dq_ref, dk_ref, dv_ref, db_ref, kp, vp, dka, dva,
             p_scr, ds_scr):
        qb = pl.program_id(1)
        _pad_keys(qb, k_ref, v_ref, kp, vp)

        @pl.when(qb == 0)
        def _():
            dka[...] = jnp.zeros_like(dka)
            dva[...] = jnp.zeros_like(dva)
            db_ref[...] = jnp.zeros_like(db_ref)

        span = pl.ds(pl.multiple_of(qb * Q_BLOCK, Q_BLOCK), KV_SPAN)
        kc, vc = kp[span, :], vp[span, :]
        qt, dot = q_ref[...] * ATT_SCALE, do_ref[...]
        mine = _head_masks()
        dqs = []
        qs = [jnp.where(mine[j], qt, jnp.zeros_like(qt)) for j in range(HEADS_PER_STEP)]
        dos = [jnp.where(mine[j], dot, jnp.zeros_like(dot)) for j in range(HEADS_PER_STEP)]
        scores = [_bdot(qs[j], kc, NT) for j in range(HEADS_PER_STEP)]
        dps = [_bdot(dos[j], vc, NT) for j in range(HEADS_PER_STEP)]
        for j in range(HEADS_PER_STEP):
            qj, doj = qs[j], dos[j]
            for r0 in range(0, Q_BLOCK, ATT_ROWS):
                rows = pl.ds(r0, ATT_ROWS)
                p = _softmax_piece(scores[j][r0:r0 + ATT_ROWS], b_ref[j, rows, :], qb)
                dp = dps[j][r0:r0 + ATT_ROWS]
                ds = p * (dp - jnp.sum(p * dp, axis=-1, keepdims=True))
                db_ref[j, rows, :] += ds
                p_scr[j, rows, :] = p.astype(BF16)
                ds_scr[j, rows, :] = ds.astype(BF16)
            dva[span, :] += _bdot(p_scr[j], doj, TN)
            dqs.append(_bdot(ds_scr[j], kc))
            dka[span, :] += _bdot(ds_scr[j], qj, TN)
        dq_ref[...] = (scale * _pick_heads(mine, dqs)).astype(BF16)

        @pl.when(qb == nq - 1)
        def _():
            dk_ref[...] = dka[pl.ds(KV_PAD, S), :].astype(BF16)
            dv_ref[...] = dva[pl.ds(KV_PAD, S), :].astype(BF16)

    blk = pl.BlockSpec((Q_BLOCK, HEAD_LANES), lambda h, i: (i, h))
    col = pl.BlockSpec((S, HEAD_LANES), lambda h, i: (0, h))
    bsp = pl.BlockSpec((HEADS_PER_STEP, Q_BLOCK, KV_SPAN), lambda h, i: (h, 0, 0))
    return pl.pallas_call(
        body,
        out_shape=[jax.ShapeDtypeStruct((S, D_MODEL), BF16)] * 3
        + [jax.ShapeDtypeStruct((N_HEADS, Q_BLOCK, KV_SPAN), F32)],
        grid=(nhp, nq),
        in_specs=[blk, pl.BlockSpec((S, HEAD_LANES), lambda h, i: (0, nhp + h)),
                  pl.BlockSpec((S, HEAD_LANES), lambda h, i: (0, 2 * nhp + h)), bsp, blk],
        out_specs=[blk, col, col, bsp],
        scratch_shapes=[pltpu.VMEM((KV_PAD + S, HEAD_LANES), BF16), pltpu.VMEM((KV_PAD + S, HEAD_LANES), BF16),
                        pltpu.VMEM((KV_PAD + S, HEAD_LANES), F32), pltpu.VMEM((KV_PAD + S, HEAD_LANES), F32),
                        pltpu.VMEM((HEADS_PER_STEP, Q_BLOCK, KV_SPAN), BF16), pltpu.VMEM((HEADS_PER_STEP, Q_BLOCK, KV_SPAN), BF16)],
        compiler_params=_cparams(("parallel", "arbitrary")),
        name="attn_bwd",
    )(qkv, qkv, qkv, bias, do)


N_DIST = BAND + CHUNK - 1
N_FAR = KV_PAD + CHUNK - MAX_REL


def _shear_rows(x, towards_right):
    row = lax.broadcasted_iota(jnp.int32, (Q_BLOCK, 1), 0)
    for bit in range(Q_BLOCK.bit_length() - 1):
        step = 1 << bit
        x = jnp.where((row & step) != 0, pltpu.roll(x, step if towards_right else KV_SPAN - step, 1), x)
    return x


def _bias_blocks(rel_bias, dep):
    H = rel_bias.shape[0]
    e = jnp.concatenate([jnp.broadcast_to(rel_bias[:, 2 * MAX_REL:], (H, N_FAR)),
                         jnp.flip(rel_bias[:, 2 * MAX_REL - (N_DIST - N_FAR):2 * MAX_REL], axis=1),
                         jnp.zeros((H, KV_SPAN - N_DIST), F32)], axis=1).reshape(H, 1, KV_SPAN)

    def body(e_ref, dep_ref, o_ref):
        first = pltpu.roll(jnp.broadcast_to(e_ref[...], (Q_BLOCK, KV_SPAN)), KV_SPAN - (CHUNK - 1), 1)
        x = _shear_rows(first, True)
        row = lax.broadcasted_iota(jnp.int32, (Q_BLOCK, 1), 0)
        chunk0 = row - (row & (CHUNK - 1))
        k = lax.broadcasted_iota(jnp.int32, (1, KV_SPAN), 1)
        o_ref[...] = jnp.where((k >= chunk0) & (k < chunk0 + BAND), x, NEG_INF)

    return pl.pallas_call(
        body,
        out_shape=jax.ShapeDtypeStruct((H, Q_BLOCK, KV_SPAN), F32),
        grid=(H,),
        in_specs=[pl.BlockSpec((None, 1, KV_SPAN), lambda h: (h, 0, 0)), pl.BlockSpec(memory_space=pl.ANY)],
        out_specs=pl.BlockSpec((None, Q_BLOCK, KV_SPAN), lambda h: (h, 0, 0)),
        compiler_params=_cparams(("parallel",)),
        name="bias_blocks",
    )(e, dep)


def _bias_blocks_grad(dblk):
    H = dblk.shape[0]

    def body(d_ref, o_ref):
        x = pltpu.roll(_shear_rows(d_ref[...], False), CHUNK - 1, 1)
        de = jnp.sum(x, axis=0, keepdims=True)
        lane = lax.broadcasted_iota(jnp.int32, de.shape, 1)
        far = jnp.sum(jnp.where(lane < N_FAR, de, 0.0), axis=-1, keepdims=True)
        o_ref[...] = jnp.where(lane == 0, far, jnp.where(lane < N_FAR, 0.0, de))

    de = pl.pallas_call(
        body,
        out_shape=jax.ShapeDtypeStruct((H, 1, KV_SPAN), F32),
        grid=(H,),
        in_specs=[pl.BlockSpec((None, Q_BLOCK, KV_SPAN), lambda h: (h, 0, 0))],
        out_specs=pl.BlockSpec((None, 1, KV_SPAN), lambda h: (h, 0, 0)),
        compiler_params=_cparams(("parallel",)),
        name="bias_grad_sum",
    )(dblk).reshape(H, KV_SPAN)
    near = jnp.flip(de[:, N_FAR:N_DIST], axis=1)
    return jnp.concatenate([jnp.zeros((H, 2 * MAX_REL - (N_DIST - N_FAR)), F32), near, de[:, 0:1]], axis=1)


def _ffn_forward(r1, r1b, p_l, w, l, ready, after):
    ready(f"up{l}", after)
    up_g = _mm_rows([(r1b, w["ffn_up_t"][l], True, (0, 2))], out_dtype=BF16, name=f"ffn_up_g{l}")
    up_v = _mm_rows([(r1b, w["ffn_up_t"][l], True, (1, 2))], out_dtype=BF16, name=f"ffn_up_v{l}")
    h = _ffn_act_fwd(up_g, up_v, w["ffn_dw_w"][l], w["ffn_dw_b"][l], name=f"ffn_act{l}")
    ready(f"dn{l}", h)
    z2, r2, r2b, gate, proj = _proj_ln(r1, h, w["ffn_w_down"][l], w["ln_ffn_g"][l], w["ln_ffn_b"][l],
                                       ple=(w["ple_w_gate"][l], w["ple_b_gate"][l], p_l, w["ple_w_proj"][l]),
                                       name=f"ffn_down_ln{l}")
    return dict(r1b=r1b, up_g=up_g, up_v=up_v, h=h, z2=z2, gate=gate, proj=proj), r2, r2b


def _ffn_backward(sv, dz2, dz2b, ple_bwd, p_l, w, l, grads, ln_bwd, emit):
    r1b = sv["r1b"]
    ds, dproj, db_gate = ple_bwd
    dh = _mm_rows([(dz2b, w["ffn_w_down"][l], True, WHOLE)], out_dtype=BF16, name=f"ffn_dh{l}")
    dgate, dval, d_dw_w, d_dw_b = _ffn_act_bwd(sv["up_g"], sv["up_v"], dh, w["ffn_dw_w"][l], w["ffn_dw_b"][l],
                                               name=f"ffn_act_bwd{l}")
    grads["ffn_w_down"][l] = _wgrad(sv["h"], dz2b, tm=1408, name=f"d_ffn_w_down{l}")
    d_up_g = _wgrad(dgate, r1b, tm=1408, part=(0, 2), name=f"d_ffn_up_g{l}")
    grads["ffn_up_t"][l] = _wgrad(dval, r1b, tm=1408, part=(1, 2), into=d_up_g, name=f"d_ffn_up_v{l}")
    grads["ple_w_gate"][l] = _wgrad(r1b, ds, name=f"d_ple_w_gate{l}")
    grads["ple_w_proj"][l] = _wgrad(p_l, dproj, piece=D_MODEL // N_DEV, name=f"d_ple_w_proj{l}")
    grads["ffn_dw_w"][l] = d_dw_w
    grads["ffn_dw_b"][l] = d_dw_b[0]
    grads["ple_b_gate"][l] = db_gate[0]
    return _mm_rows([(ds, w["ple_w_gate"][l], True, WHOLE), (dgate, w["ffn_up_t"][l], False, (0, 2)),
                     (dval, w["ffn_up_t"][l], False, (1, 2))], add=dz2, add_scale=ALPHA, ln_bwd=ln_bwd, dep=emit(),
                    name=f"dr1_{l}")


def _local_step(x, p, target, w, ready=lambda group, after: None, emit=lambda group, grads: None):
    grads = {k: [None, None] for k in ("ffn_w_down", "ffn_up_t", "ple_w_gate", "ple_w_proj", "ffn_dw_w",
                                       "ffn_dw_b", "ple_b_gate", "ln_ffn_g", "ln_ffn_b", "ln_mix_g", "ln_mix_b")}

    xb, pb = x.astype(BF16), p.astype(BF16)
    ready("mix", None)
    u = _mm_rows([(xb, w["mix_w_in_t"], True, WHOLE)], name="mix_in")
    ycat, dpool, hconv = _mixer_fwd(u, w["pool_w"], w["pool_scale"], w["conv_dw_w"], w["conv_dw_b"], w["conv_ln_g"],
                                    w["conv_ln_b"])
    ready("mixo", ycat)
    z1, r1, r1b = _proj_ln(x, ycat, w["mix_w_out"], w["ln_mix_g"][0], w["ln_mix_b"][0], name="mix_out_ln")
    bias = _bias_blocks(w["attn_rel_bias"], r1b)
    sv0, r2, r2b = _ffn_forward(r1, r1b, pb[0], w, 0, ready, bias)

    ready("attn", r2b)
    qkv = _mm_rows([(r2b, w["attn_w_qkv"], False, WHOLE)], out_dtype=BF16, name="attn_qkv")
    attn = _attn_fwd(qkv, bias)
    z3, r3, r3b = _proj_ln(r2, attn, w["attn_w_o"], w["ln_mix_g"][1], w["ln_mix_b"][1], name="attn_out_ln")
    sv1, _, _ = _ffn_forward(r3, r3b, pb[1], w, 1, ready, r3b)

    dz4, dz4b, grads["ln_ffn_g"][1], grads["ln_ffn_b"][1], loss, *ple1 = _loss_ln_bwd(
        sv1["z2"], w["ln_ffn_g"][1], w["ln_ffn_b"][1], target, sv1["gate"], sv1["proj"], name="loss_ln_bwd")
    dz3, dz3b, grads["ln_mix_g"][1], grads["ln_mix_b"][1] = _ffn_backward(
        sv1, dz4, dz4b, ple1, pb[1], w, 1, grads, (z3, w["ln_mix_g"][1]), lambda: emit("ffn1", grads))
    grads["attn_w_o"] = _wgrad(attn, dz3b, name="d_attn_w_o")
    dattn = _mm_rows([(dz3b, w["attn_w_o"], True, WHOLE)], out_dtype=BF16, name="d_attn")
    dq, dk, dv, dbias = _attn_bwd(qkv, bias, dattn)
    grads["attn_rel_bias"] = _bias_blocks_grad(dbias)
    dqkv = jnp.concatenate([dq, dk, dv], axis=1)
    grads["attn_w_qkv"] = _wgrad(r2b, dqkv, tn=768, piece=3 * D_MODEL // N_DEV, name="d_attn_w_qkv")
    dz2, dz2b, grads["ln_ffn_g"][0], grads["ln_ffn_b"][0], *ple0 = _mm_rows(
        [(dqkv, w["attn_w_qkv"], True, WHOLE)], add=dz3, add_scale=ALPHA,
        ln_bwd=(sv0["z2"], w["ln_ffn_g"][0], sv0["gate"], sv0["proj"]), dep=emit("attn", grads), name="dr2")
    dz1, dz1b, grads["ln_mix_g"][0], grads["ln_mix_b"][0] = _ffn_backward(
        sv0, dz2, dz2b, ple0, pb[0], w, 0, grads, (z1, w["ln_mix_g"][0]), lambda: emit("ffn0", grads))
    grads["mix_w_out"] = _wgrad(ycat, dz1b, name="d_mix_w_out")
    dycat = _mm_rows([(dz1b, w["mix_w_out"], True, WHOLE)], name="d_ycat")
    du, g_pw, g_ps, g_cw, g_cb, g_cg, g_cbb = _mixer_bwd(u, dpool, hconv, dycat, w["pool_w"], w["pool_scale"],
                                                         w["conv_dw_w"], w["conv_ln_g"], w["conv_ln_b"])
    grads["mix_w_in_t"] = _wgrad(du, xb, name="d_mix_w_in")
    grads.update(pool_w=g_pw, pool_scale=g_ps[0], conv_dw_w=g_cw, conv_dw_b=g_cb[0], conv_ln_g=g_cg[0],
                 conv_ln_b=g_cbb[0])
    for kname in ("ln_ffn_g", "ln_ffn_b", "ln_mix_g", "ln_mix_b"):
        grads[kname] = [a[0] for a in grads[kname]]
    grad_x = _mm_rows([(du, w["mix_w_in_t"], False, WHOLE)], add=dz1, add_scale=ALPHA, dep=emit("mix", grads),
                      name="grad_x")
    return loss[0, 0], grad_x, grads


_HBM = pl.BlockSpec(memory_space=pltpu.HBM)
_SEM = pl.BlockSpec(memory_space=pltpu.SEMAPHORE)
_EFFECT = pltpu.SideEffectType.DATAFLOW_SIDE_EFFECTING


def _slot(ref, place, shape, k):
    if place in ("stack", "pieces"):
        return ref.at[k]
    ax = place[1]
    n = shape[ax]
    return ref.at[(slice(None),) * ax + (pl.ds(pl.multiple_of(k * n, n), n),)]


def _result_shape(buf, place):
    if place == "stack":
        return (N_DEV,) + buf.shape
    if place == "pieces":
        return buf.shape
    return tuple(s * N_DEV if i == place[1] else s for i, s in enumerate(buf.shape))


def _peers(x, y, c):
    for d in range(1, N_DEV):
        px, py, pc = x ^ ((d >> 2) & 1), y ^ ((d >> 1) & 1), c ^ (d & 1)
        yield d, (px, py, pc), 4 * px + 2 * py + pc


def _exchange_start(bufs, places, after, *, name):
    nb = len(bufs)
    lands = [lax.empty(_result_shape(b, p_), b.dtype) for b, p_ in zip(bufs, places)]
    has_after = after is not None

    def body(*refs):
        srcs, dsts = refs[:nb], refs[nb:2 * nb]
        outs = refs[2 * nb + has_after:]
        send_sems, recv_sems, token = outs[0], outs[1], outs[2 + 2 * nb]
        x, y, c = lax.axis_index("x"), lax.axis_index("y"), lax.axis_index("c")
        me = 4 * x + 2 * y + c
        for b in range(nb):
            for d, dev, peer in _peers(x, y, c):
                pltpu.make_async_remote_copy(
                    src_ref=srcs[b].at[peer] if places[b] == "pieces" else srcs[b],
                    dst_ref=_slot(dsts[b], places[b], bufs[b].shape, me),
                    send_sem=send_sems.at[b * N_DEV + d], recv_sem=recv_sems.at[b * N_DEV + d],
                    device_id=dev, device_id_type=pl.DeviceIdType.MESH).start()
            pltpu.make_async_copy(srcs[b].at[me] if places[b] == "pieces" else srcs[b],
                                  _slot(dsts[b], places[b], bufs[b].shape, me), recv_sems.at[b * N_DEV]).start()
        token[...] = jnp.zeros_like(token)

    sems = pltpu.SemaphoreType.DMA((nb * N_DEV,))
    ins = [pltpu.with_memory_space_constraint(a, pltpu.HBM) for a in list(bufs) + lands]
    out = pl.pallas_call(
        body,
        out_shape=(sems, sems, *[pltpu.HBM(a.shape, a.dtype) for a in ins], jax.ShapeDtypeStruct((8, 128), F32)),
        in_specs=[_HBM] * (2 * nb) + ([pl.BlockSpec(memory_space=pl.ANY)] if has_after else []),
        out_specs=(_SEM, _SEM, *[_HBM] * (2 * nb), pl.BlockSpec(memory_space=pltpu.VMEM)),
        input_output_aliases={i: 2 + i for i in range(2 * nb)},
        compiler_params=pltpu.CompilerParams(has_side_effects=_EFFECT),
        name=name,
    )(*ins, *([after] if has_after else []))
    return dict(send=out[0], recv=out[1], srcs=out[2:2 + nb], lands=out[2 + nb:2 + 2 * nb], token=out[-1],
                places=places)


def _exchange_wait(h, after, *, name):
    nb = len(h["srcs"])
    places = h["places"]
    shapes = [a.shape for a in h["srcs"]]

    def body(*refs):
        srcs, dsts, send_sems, recv_sems = refs[:nb], refs[nb:2 * nb], refs[2 * nb], refs[2 * nb + 1]
        x, y, c = lax.axis_index("x"), lax.axis_index("y"), lax.axis_index("c")
        me = 4 * x + 2 * y + c
        for b in range(nb):
            pieces = places[b] == "pieces"
            for d, dev, peer in _peers(x, y, c):
                cp = pltpu.make_async_remote_copy(
                    src_ref=srcs[b].at[peer] if pieces else srcs[b],
                    dst_ref=_slot(dsts[b], places[b], shapes[b], peer),
                    send_sem=send_sems.at[b * N_DEV + d], recv_sem=recv_sems.at[b * N_DEV + d],
                    device_id=dev, device_id_type=pl.DeviceIdType.MESH)
                cp.wait_send()
                cp.wait_recv()
            pltpu.make_async_copy(srcs[b].at[me] if pieces else srcs[b], _slot(dsts[b], places[b], shapes[b], me),
                                  recv_sems.at[b * N_DEV]).wait()

    ins = list(h["srcs"]) + list(h["lands"])
    out = pl.pallas_call(
        body,
        out_shape=tuple(pltpu.HBM(a.shape, a.dtype) for a in ins),
        in_specs=[_HBM] * (2 * nb) + [_SEM, _SEM, pl.BlockSpec(memory_space=pl.ANY)],
        out_specs=tuple([_HBM] * (2 * nb)),
        input_output_aliases={i: i for i in range(2 * nb)},
        compiler_params=pltpu.CompilerParams(has_side_effects=_EFFECT),
        name=name,
    )(*ins, h["send"], h["recv"], after)
    return out[nb:]


def _adamw(recv, w, m, v, *, layer=0, into=None, name):
    L, R, C = w.shape
    fits = [d for d in range(16, R + 1, 16) if R % d == 0 and d * C * 4 <= 2 * 1024 * 1024]
    tr = fits[-1] if fits else R
    c1 = 1.0 - ADAM_B1 ** ADAM_STEP
    c2 = 1.0 - ADAM_B2 ** ADAM_STEP

    def body(r_ref, w_ref, m_ref, v_ref, *rest):
        g_ref, d_ref, mo_ref, vo_ref = rest[-4:]
        g = r_ref[0].astype(F32)
        for i in range(1, N_DEV):
            g = g + r_ref[i].astype(F32)
        m_new = ADAM_B1 * m_ref[...] + (1.0 - ADAM_B1) * g
        v_new = ADAM_B2 * v_ref[...] + (1.0 - ADAM_B2) * (g * g)
        m_hat = m_new / c1
        v_hat = v_new / c2
        g_ref[...] = g
        d_ref[...] = -ADAM_LR * (m_hat / (jnp.sqrt(v_hat) + ADAM_EPS) + ADAM_WD * w_ref[...])
        mo_ref[...] = m_new
        vo_ref[...] = v_new

    row = pl.BlockSpec((None, tr, C), lambda i: (layer, i, 0))
    others = [] if into is None else list(into)
    return pl.pallas_call(
        body,
        out_shape=[jax.ShapeDtypeStruct((L, R, C), F32)] * 4,
        grid=(R // tr,),
        in_specs=[pl.BlockSpec((N_DEV, tr, C), lambda i: (0, i, 0)), row, row, row]
        + [pl.BlockSpec(memory_space=pl.ANY)] * len(others),
        out_specs=[row] * 4,
        input_output_aliases={4 + k: k for k in range(len(others))},
        compiler_params=_cparams(("parallel",)),
        name=name,
    )(recv, w, m, v, *others)


_TRANSPOSED = ("mix_w_in", "ffn_w_up")


def _ffn_groups(l):
    return ((f"up{l}", (("ffn_w_up", l, BF16, ("axis", 0)), ("ffn_dw_w", l, F32, "stack"))),
            (f"dn{l}", (("ffn_w_down", l, BF16, ("axis", 0)), ("ple_w_gate", l, BF16, ("axis", 0)),
                        ("ple_w_proj", l, BF16, ("axis", 1)))))


_GATHER_GROUPS = (
    ("mix", (("mix_w_in", 0, BF16, ("axis", 0)), ("conv_dw_w", 0, F32, "stack"))),
    ("mixo", (("mix_w_out", 0, BF16, ("axis", 0)),)),
    *_ffn_groups(0),
    ("attn", (("attn_w_qkv", 0, BF16, ("axis", 1)), ("attn_w_o", 0, BF16, ("axis", 0)))),
    *_ffn_groups(1))
_SHARDED = ("mix_w_in", "conv_dw_w", "mix_w_out", "attn_w_qkv", "attn_w_o", "ffn_w_up", "ffn_dw_w", "ffn_w_down",
            "ple_w_gate", "ple_w_proj")
_REPLICATED = ("pool_w", "pool_scale", "conv_dw_b", "conv_ln_g", "conv_ln_b", "attn_rel_bias", "ln_mix_g",
               "ln_mix_b", "ffn_dw_b", "ple_b_gate", "ln_ffn_g", "ln_ffn_b")


def _pack_rows(parts, row_mult, dtype):
    lead = parts[0].shape[:-1]
    flat = jnp.concatenate([a.astype(dtype) for a in parts], axis=-1)
    n = flat.shape[-1]
    unit = row_mult * LANES
    padded = -(-n // unit) * unit
    flat = jnp.pad(flat, [(0, 0)] * len(lead) + [(0, padded - n)])
    return flat.reshape(lead + (padded // LANES, LANES))


def _unpack(flat2d, shapes):
    flat = flat2d.reshape(-1)
    out, o = [], 0
    for s in shapes:
        n = math.prod(s)
        out.append(flat[o:o + n].reshape(s))
        o += n
    return out


def _full_from_shards(g, axis):
    parts = jnp.moveaxis(g, 0, axis)
    shp = list(g.shape[1:])
    shp[axis] *= g.shape[0]
    return parts.reshape(shp)


def _pieces_from_full(full, axis, k=N_DEV):
    shp = list(full.shape)
    n = shp[axis] // k
    t = full.reshape(shp[:axis] + [k, n] + shp[axis + 1:])
    return jnp.moveaxis(t, axis, 0)


def kernel(x, p, mix_w_in, pool_w, pool_scale, conv_dw_w, conv_dw_b, conv_ln_g, conv_ln_b, mix_w_out, attn_w_qkv, attn_rel_bias, attn_w_o, ln_mix_g, ln_mix_b, ffn_w_up, ffn_dw_w, ffn_dw_b, ffn_w_down, ple_w_proj, ple_w_gate, ple_b_gate, ln_ffn_g, ln_ffn_b, loss_target, m_mix_w_in, m_pool_w, m_pool_scale, m_conv_dw_w, m_conv_dw_b, m_conv_ln_g, m_conv_ln_b, m_mix_w_out, m_attn_w_qkv, m_attn_rel_bias, m_attn_w_o, m_ln_mix_g, m_ln_mix_b, m_ffn_w_up, m_ffn_dw_w, m_ffn_dw_b, m_ffn_w_down, m_ple_w_proj, m_ple_w_gate, m_ple_b_gate, m_ln_ffn_g, m_ln_ffn_b, v_mix_w_in, v_pool_w, v_pool_scale, v_conv_dw_w, v_conv_dw_b, v_conv_ln_g, v_conv_ln_b, v_mix_w_out, v_attn_w_qkv, v_attn_rel_bias, v_attn_w_o, v_ln_mix_g, v_ln_mix_b, v_ffn_w_up, v_ffn_dw_w, v_ffn_dw_b, v_ffn_w_down, v_ple_w_proj, v_ple_w_gate, v_ple_b_gate, v_ln_ffn_g, v_ln_ffn_b):
    a = dict(locals())
    sh_names = list(_SHARDED)
    names = sh_names + list(_REPLICATED)
    wts = {n: a[n] for n in names}
    mom = {n: a["m_" + n] for n in names}
    var = {n: a["v_" + n] for n in names}

    for n in _TRANSPOSED:
        wts[n], mom[n], var[n] = (jnp.swapaxes(d[n], 1, 2) for d in (wts, mom, var))
    gather = {}
    token = None
    for group, items in _GATHER_GROUPS:
        gather[group] = _exchange_start([wts[n][l].astype(dt) for n, l, dt, _ in items], [pl_ for *_, pl_ in items],
                                        token, name="gather_start_" + group)
        token = gather[group]["token"]

    w = dict(pool_w=pool_w[0], pool_scale=pool_scale[0], conv_dw_b=conv_dw_b[0], conv_ln_g=conv_ln_g[0],
             conv_ln_b=conv_ln_b[0], attn_rel_bias=attn_rel_bias[0], ln_mix_g=ln_mix_g, ln_mix_b=ln_mix_b,
             ffn_dw_b=ffn_dw_b, ple_b_gate=ple_b_gate, ln_ffn_g=ln_ffn_g, ln_ffn_b=ln_ffn_b)
    for n in ("ffn_up_t", "ffn_dw_w", "ffn_w_down", "ple_w_gate", "ple_w_proj"):
        w[n] = [None, None]

    def ready(group, after):
        got = _exchange_wait(gather[group], token if after is None else after, name="gather_wait_" + group)
        if group == "mix":
            w["mix_w_in_t"], w["conv_dw_w"] = got[0], _full_from_shards(got[1], 1)
        elif group == "mixo":
            (w["mix_w_out"],) = got
        elif group == "attn":
            w["attn_w_qkv"], w["attn_w_o"] = got
        elif group[:2] == "up":
            l = int(group[2])
            w["ffn_up_t"][l], w["ffn_dw_w"][l] = got[0], _full_from_shards(got[1], 1)
        else:
            l = int(group[2])
            w["ffn_w_down"][l], w["ple_w_gate"][l], w["ple_w_proj"][l] = got

    scatter = {}

    def emit(group, gr):
        if group[:3] == "ffn":
            l = int(group[3])
            pieces = [_pieces_from_full(gr["ffn_up_t"][l], 0),
                      _pieces_from_full(gr["ffn_dw_w"][l], 1), _pieces_from_full(gr["ffn_w_down"][l], 0),
                      _pieces_from_full(gr["ple_w_gate"][l], 0), gr["ple_w_proj"][l]]
        elif group == "attn":
            pieces = [gr["attn_w_qkv"], _pieces_from_full(gr["attn_w_o"], 0)]
        else:
            pieces = [_pieces_from_full(gr["mix_w_in_t"], 0), _pieces_from_full(gr["conv_dw_w"], 1),
                      _pieces_from_full(gr["mix_w_out"], 0)]
        scatter[group] = _exchange_start([a.astype(BF16) for a in pieces], ["pieces"] * len(pieces), None,
                                         name="grad_start_" + group)
        if group != "mix":
            return scatter[group]["token"]
        gfull = dict(
            pool_w=gr["pool_w"][None], pool_scale=gr["pool_scale"][None], conv_dw_b=gr["conv_dw_b"][None],
            conv_ln_g=gr["conv_ln_g"][None], conv_ln_b=gr["conv_ln_b"][None],
            attn_rel_bias=gr["attn_rel_bias"][None], ln_mix_g=jnp.stack(gr["ln_mix_g"]),
            ln_mix_b=jnp.stack(gr["ln_mix_b"]), ffn_dw_b=jnp.stack(gr["ffn_dw_b"]),
            ple_b_gate=jnp.stack(gr["ple_b_gate"]), ln_ffn_g=jnp.stack(gr["ln_ffn_g"]),
            ln_ffn_b=jnp.stack(gr["ln_ffn_b"]))
        rep_send = _pack_rows([gfull[n].reshape(-1) for n in _REPLICATED], 8, F32)
        scatter["replicated"] = _exchange_start([rep_send], ["stack"], scatter[group]["token"],
                                                name="grad_start_replicated")
        return scatter["replicated"]["token"]

    loss_part, grad_x, gr = _local_step(x[0], p[:, 0], loss_target[0], w, ready, emit)
    loss = lax.psum(loss_part, ("x", "y", "c"))

    group_weights = {"ffn1": (("ffn_w_up", 1), ("ffn_dw_w", 1), ("ffn_w_down", 1), ("ple_w_gate", 1), ("ple_w_proj", 1)),
                     "attn": (("attn_w_qkv", 0), ("attn_w_o", 0)),
                     "ffn0": (("ffn_w_up", 0), ("ffn_dw_w", 0), ("ffn_w_down", 0), ("ple_w_gate", 0), ("ple_w_proj", 0)),
                     "mix": (("mix_w_in", 0), ("conv_dw_w", 0), ("mix_w_out", 0))}
    updated = {}
    after = grad_x
    for group in ("ffn1", "attn", "ffn0", "mix"):
        recv = _exchange_wait(scatter[group], after, name="grad_wait_" + group)
        for (n, l), r in zip(group_weights[group], recv):
            updated[n] = _adamw(r, wts[n], mom[n], var[n], layer=l, into=updated.get(n), name=f"adamw_{n}{l}")
            after = updated[n][0]
    res = [{n: jnp.swapaxes(updated[n][k], 1, 2) if n in _TRANSPOSED else updated[n][k] for n in sh_names}
           for k in range(4)]
    (rep_recv,) = _exchange_wait(scatter["replicated"], after, name="grad_wait_replicated")

    def flat_state(d):
        return _pack_rows([d[n].reshape(-1) for n in _REPLICATED], 8, F32)[None]

    rep_out = _adamw(rep_recv, flat_state(wts), flat_state(mom), flat_state(var), name="adamw_replicated")
    for k in range(4):
        for n, arr in zip(_REPLICATED, _unpack(rep_out[k][0], [wts[n].shape for n in _REPLICATED])):
            res[k][n] = arr
    order = ["mix_w_in", "pool_w", "pool_scale", "conv_dw_w", "conv_dw_b", "conv_ln_g", "conv_ln_b", "mix_w_out",
             "attn_w_qkv", "attn_rel_bias", "attn_w_o", "ln_mix_g", "ln_mix_b", "ffn_w_up", "ffn_dw_w", "ffn_dw_b",
             "ffn_w_down", "ple_w_proj", "ple_w_gate", "ple_b_gate", "ln_ffn_g", "ln_ffn_b"]
    outs = [loss, grad_x[None]]
    for k in range(4):
        outs += [res[k][n] for n in order]
    return tuple(outs)
```

```python
import functools
import math

import jax
import jax.numpy as jnp
from jax import lax
from jax.experimental import pallas as pl
from jax.experimental.pallas import tpu as pltpu

F32 = jnp.float32
BF16 = jnp.bfloat16

N_DEV = 8
D_MODEL = 1024
D_POOL = 512
D_CONV = 512
POOL_WINDOWS = (2, 4, 8, 16)
POOL_GROUP = 128
CONV_KERNEL = 31
CHUNK = 64
HEAD_DIM = 64
N_HEADS = 16
LEFT_CHUNKS = 8
BAND = (LEFT_CHUNKS + 1) * CHUNK
MAX_REL = 256
D_FF = 2816
PLE_DIM = 256
ALPHA = 4.0 ** 0.25
LN_EPS = 1e-5
NEG_INF = -1e30
ADAM_LR, ADAM_B1, ADAM_B2, ADAM_EPS, ADAM_WD, ADAM_STEP = 0.001, 0.9, 0.999, 1e-08, 0.01, 10

Q_BLOCK = 4 * CHUNK
KV_PAD = LEFT_CHUNKS * CHUNK
KV_SPAN = KV_PAD + Q_BLOCK
CONV_HALO = 32
FFN_HALO = 16
SUB_ROWS, SUB_LANES = 64, 128
LANES = 1024
VMEM_LIMIT = 56 * 1024 * 1024


def _cparams(sem=None):
    return pltpu.CompilerParams(dimension_semantics=sem, vmem_limit_bytes=VMEM_LIMIT)


def _tile(dim, pref):
    if dim <= pref:
        return dim
    t = pref - pref % 128
    while t >= 128:
        if dim % t == 0:
            return t
        t -= 128
    return dim


def _sigmoid(x):
    return 1.0 / (1.0 + jnp.exp(-x))


def _bdot(a, b, dn=(((1,), (0,)), ((), ()))):
    return lax.dot_general(a.astype(BF16), b.astype(BF16), dn, preferred_element_type=F32)


WHOLE = (0, 1)
NT = (((1,), (1,)), ((), ()))
TN = (((0,), (0,)), ((), ()))


def _wgrad(a, b, *, tm=1024, tn=1024, tk=2048, piece=None, part=(0, 1), into=None, name):
    K, M = a.shape
    kb, N = b.shape
    assert K == kb, (a.shape, b.shape)
    tm, tn, tk = _tile(M, tm), _tile(N, tn), _tile(K, tk)
    nk = K // tk
    per = 1 if piece is None else tn // piece
    assert piece is None or tn == per * piece

    def body(a_ref, b_ref, *rest):
        o_ref, acc = rest[-2:]
        k = pl.program_id(2)

        @pl.when(k == 0)
        def _():
            acc[...] = jnp.zeros_like(acc)

        acc[...] += _bdot(a_ref[...], b_ref[...], TN)

        @pl.when(k == nk - 1)
        def _():
            if piece is None:
                o_ref[...] = acc[...].astype(BF16)
            else:
                for s in range(per):
                    o_ref[s] = acc[:, s * piece:(s + 1) * piece].astype(BF16)

    if piece is None:
        first = part[0] * (M // tm)
        out_shape = (part[1] * M, N)
        out_spec = pl.BlockSpec((tm, tn), lambda i, j, k: (first + i, j))
    else:
        out_shape, out_spec = (N // piece, M, piece), pl.BlockSpec((per, tm, piece), lambda i, j, k: (j, i, 0))
    others = [] if into is None else [into]
    return pl.pallas_call(
        body,
        out_shape=jax.ShapeDtypeStruct(out_shape, BF16),
        grid=(M // tm, N // tn, nk),
        in_specs=[pl.BlockSpec((tk, tm), lambda i, j, k: (k, i)), pl.BlockSpec((tk, tn), lambda i, j, k: (k, j))]
        + [pl.BlockSpec(memory_space=pl.ANY)] * len(others),
        out_specs=out_spec,
        input_output_aliases={2: 0} if others else {},
        scratch_shapes=[pltpu.VMEM((tm, tn), F32)],
        compiler_params=_cparams(("parallel", "parallel", "arbitrary")),
        name=name,
    )(a, b, *others)


def _mm_rows(pairs, *, add=None, add_scale=1.0, out_dtype=F32, tm=512, dep=None, ln_bwd=None, name):
    M = pairs[0][0].shape[0]
    n = len(pairs)
    has_add = add is not None
    has_ple = ln_bwd is not None and len(ln_bwd) == 4
    w_rows = [w_.shape[0] // part[1] for _, w_, _, part in pairs]
    N = w_rows[0] if pairs[0][2] else pairs[0][1].shape[1]

    def body(*refs):
        acc = None
        for i, (_, _, tr, _) in enumerate(pairs):
            part = _bdot(refs[2 * i][...], refs[2 * i + 1][...], NT if tr else (((1,), (0,)), ((), ())))
            acc = part if acc is None else acc + part
        if has_add:
            acc = acc + add_scale * refs[2 * n][...]
        if ln_bwd is None:
            refs[-1][...] = acc.astype(out_dtype)
            return
        first = 2 * n + has_add
        z_ref, g_ref = refs[first], refs[first + 1]
        outs = refs[-(7 if has_ple else 4):]
        dz_ref, dzb_ref, dg_ref, db_ref = outs[:4]

        @pl.when(pl.program_id(0) == 0)
        def _():
            for sums in outs[2:4] + outs[6:]:
                sums[...] = jnp.zeros_like(sums)

        dg_acc = jnp.zeros((8, N), F32)
        db_acc = jnp.zeros((8, N), F32)
        dbg_acc = jnp.zeros((8, N), F32)
        for r0 in range(0, tm, LN_ROWS):
            rows = pl.ds(r0, LN_ROWS)
            do = acc[r0:r0 + LN_ROWS]
            dz, xh = _ln_bwd_rows(z_ref[rows, :], g_ref[...], do)
            dz_ref[rows, :] = dz
            dzb_ref[rows, :] = dz.astype(BF16)
            dg_acc = dg_acc + jnp.sum((do * xh).reshape(LN_ROWS // 8, 8, N), axis=0)
            db_acc = db_acc + jnp.sum(do.reshape(LN_ROWS // 8, 8, N), axis=0)
            if has_ple:
                ds, dp = _ple_bwd_rows(dz, refs[first + 2][rows, :], refs[first + 3][rows, :])
                outs[4][rows, :] = ds.astype(BF16)
                outs[5][rows, :] = dp.astype(BF16)
                dbg_acc = dbg_acc + jnp.sum(ds.reshape(LN_ROWS // 8, 8, N), axis=0)
        dg_ref[...] += jnp.sum(dg_acc, axis=0, keepdims=True)
        db_ref[...] += jnp.sum(db_acc, axis=0, keepdims=True)
        if has_ple:
            outs[6][...] += jnp.sum(dbg_acc, axis=0, keepdims=True)

    in_specs, args = [], []
    for (a, w_, _, part), rows in zip(pairs, w_rows):
        in_specs += [pl.BlockSpec((tm, a.shape[1]), lambda i: (i, 0)),
                     pl.BlockSpec((rows, w_.shape[1]), functools.partial(lambda i, j: (j, 0), j=part[0]))]
        args += [a, w_]
    row = pl.BlockSpec((tm, N), lambda i: (i, 0))
    fix = pl.BlockSpec((1, N), lambda i: (0, 0))
    if has_add:
        in_specs.append(row)
        args.append(add)
    if ln_bwd is not None:
        in_specs += [row, fix] + [row] * (len(ln_bwd) - 2)
        args += [ln_bwd[0], ln_bwd[1].reshape(1, N), *ln_bwd[2:]]
    if dep is not None:
        in_specs.append(pl.BlockSpec(memory_space=pl.ANY))
        args.append(dep)
    if ln_bwd is None:
        out_shape, out_specs = jax.ShapeDtypeStruct((M, N), out_dtype), row
    else:
        out_shape = [jax.ShapeDtypeStruct((M, N), F32), jax.ShapeDtypeStruct((M, N), BF16),
                     jax.ShapeDtypeStruct((1, N), F32), jax.ShapeDtypeStruct((1, N), F32)]
        out_specs = [row, row, fix, fix]
        if has_ple:
            out_shape += [jax.ShapeDtypeStruct((M, N), BF16), jax.ShapeDtypeStruct((M, N), BF16),
                          jax.ShapeDtypeStruct((1, N), F32)]
            out_specs += [row, row, fix]
    return pl.pallas_call(
        body,
        out_shape=out_shape,
        grid=(M // tm,),
        in_specs=in_specs,
        out_specs=out_specs,
        compiler_params=_cparams(("parallel",) if ln_bwd is None else ("arbitrary",)),
        name=name,
    )(*args)


def _ln_bwd_rows(zt, g, do):
    zc = zt - jnp.mean(zt, axis=-1, keepdims=True)
    rstd = lax.rsqrt(jnp.mean(zc * zc, axis=-1, keepdims=True) + LN_EPS)
    xh = zc * rstd
    dxh = do * g
    return rstd * (dxh - jnp.mean(dxh, axis=-1, keepdims=True) - xh * jnp.mean(dxh * xh, axis=-1, keepdims=True)), xh


def _layer_norm_rows(z, g, b):
    mu = jnp.mean(z, axis=-1, keepdims=True)
    zc = z - mu
    var = jnp.mean(zc * zc, axis=-1, keepdims=True)
    return zc * lax.rsqrt(var + LN_EPS) * g + b


def _proj_ln(res, a, w, ln_g, ln_b, *, ple=None, ts=512, name):
    S, D = res.shape
    ka = a.shape[1]
    has_ple = ple is not None
    row = lambda i: (i, 0)
    fix = lambda i: (0, 0)

    def body(*refs):
        if has_ple:
            (res_ref, a_ref, w_ref, g_ref, b_ref, wg_ref, bg_ref, p_ref, wp_ref, z_ref, r_ref, rb_ref, gate_ref,
             proj_ref, acc) = refs
        else:
            res_ref, a_ref, w_ref, g_ref, b_ref, z_ref, r_ref, rb_ref, acc = refs
        acc[...] = _bdot(a_ref[...], w_ref[...])
        if has_ple:
            gate_ref[...] = _bdot(res_ref[...], wg_ref[...])
            proj_ref[...] = _bdot(p_ref[...], wp_ref[...])
        for r0 in range(0, ts, LN_ROWS):
            rows = pl.ds(r0, LN_ROWS)
            z = ALPHA * res_ref[rows, :] + acc[rows, :]
            if has_ple:
                gate = _sigmoid(gate_ref[rows, :] + bg_ref[...])
                gate_ref[rows, :] = gate
                z = z + gate * proj_ref[rows, :]
            z_ref[rows, :] = z
            r = _layer_norm_rows(z, g_ref[...], b_ref[...])
            r_ref[rows, :] = r
            rb_ref[rows, :] = r.astype(BF16)

    in_specs = [pl.BlockSpec((ts, D), row), pl.BlockSpec((ts, ka), row), pl.BlockSpec((ka, D), fix),
                pl.BlockSpec((1, D), fix), pl.BlockSpec((1, D), fix)]
    args = [res, a, w, ln_g.reshape(1, D), ln_b.reshape(1, D)]
    out_dtypes = [F32, F32, BF16]
    if has_ple:
        wg, bg, p, wp = ple
        in_specs += [pl.BlockSpec((D, D), fix), pl.BlockSpec((1, D), fix), pl.BlockSpec((ts, PLE_DIM), row),
                     pl.BlockSpec((PLE_DIM, D), fix)]
        args += [wg, bg.reshape(1, D), p, wp]
        out_dtypes += [F32, F32]
    return pl.pallas_call(
        body,
        out_shape=[jax.ShapeDtypeStruct((S, D), dt) for dt in out_dtypes],
        grid=(S // ts,),
        in_specs=in_specs,
        out_specs=[pl.BlockSpec((ts, D), row)] * len(out_dtypes),
        scratch_shapes=[pltpu.VMEM((ts, D), F32)],
        compiler_params=_cparams(("parallel",)),
        name=name,
    )(*args)


CONV_ROWS = 32
LN_ROWS = 16


def _shifted_copies(src, dst, rows):
    for c0 in range(0, src.shape[1], SUB_LANES):
        ln = pl.ds(c0, SUB_LANES)
        for r0 in range(0, rows, SUB_ROWS):
            rc = min(SUB_ROWS, rows - r0)
            for b, shifted in enumerate(_rows_ahead(src, r0, rc, ln, range(1, 8))):
                dst[b, pl.ds(r0, rc), ln] = shifted


def _rows_at(src, copies, off, n, ln):
    b = off % 8
    return src[pl.ds(off, n), ln] if b == 0 else copies[b - 1, pl.ds(off - b, n), ln]


def _conv31(stg, gsh, cw_ref, cb_ref, out, rows, first_off):
    for c0 in range(0, D_CONV, SUB_LANES):
        ln = pl.ds(c0, SUB_LANES)
        for r0 in range(0, rows, CONV_ROWS):
            acc = jnp.zeros((CONV_ROWS, SUB_LANES), F32) + cb_ref[:, ln]
            for k in range(CONV_KERNEL):
                acc = acc + cw_ref[k:k + 1, ln] * _rows_at(stg, gsh, first_off + k + r0, CONV_ROWS, ln)
            out[pl.ds(r0, CONV_ROWS), ln] = acc


def _mixer_fwd(u, pool_w, pool_scale, conv_w, conv_b, cln_g, cln_b, *, ts=512):
    S = u.shape[0]
    hb = CONV_HALO
    nh = ts // hb

    def body(u_ref, uh_ref, pw_ref, ps_ref, cw_ref, cb_ref, g_ref, b_ref, y_ref, d_ref, hcs, sta, stg, gsh):
        i = pl.program_id(0)
        first = i == 0
        sta[pl.ds(0, hb), :] = jnp.where(first, 0.0, uh_ref[:, 0:D_POOL])
        sta[pl.ds(hb, ts), :] = u_ref[:, 0:D_POOL]
        glu_h = uh_ref[:, D_POOL:D_POOL + D_CONV] * _sigmoid(uh_ref[:, D_POOL + D_CONV:])
        stg[pl.ds(0, hb), :] = jnp.where(first, 0.0, glu_h)
        stg[pl.ds(hb, ts), :] = u_ref[:, D_POOL:D_POOL + D_CONV] * _sigmoid(u_ref[:, D_POOL + D_CONV:])

        for g, w in enumerate(POOL_WINDOWS):
            lanes = pl.ds(g * POOL_GROUP, POOL_GROUP)
            for r0 in range(0, ts, SUB_ROWS):
                s = None
                for q in range(0, w, 8):
                    for tap in _rows_back(sta, hb + r0 - q, SUB_ROWS, lanes, range(min(8, w - q))):
                        s = tap if s is None else s + tap
                pos = (i * ts + r0 + lax.broadcasted_iota(jnp.int32, (SUB_ROWS, 1), 0) + 1).astype(F32)
                d_g = s / jnp.minimum(pos, float(w)) - sta[pl.ds(hb + r0, SUB_ROWS), lanes]
                d_ref[pl.ds(r0, SUB_ROWS), lanes] = d_g.astype(BF16)
            y_ref[:, lanes] = (_bdot(d_ref[:, lanes], pw_ref[g]) * ps_ref[:, lanes]).astype(BF16)

        _shifted_copies(stg, gsh, hb + ts - 8)
        _conv31(stg, gsh, cw_ref, cb_ref, hcs, ts, hb - (CONV_KERNEL - 1))
        for r0 in range(0, ts, LN_ROWS):
            rows = pl.ds(r0, LN_ROWS)
            ln = _layer_norm_rows(hcs[rows, :], g_ref[...], b_ref[...])
            y_ref[rows, D_POOL:] = (ln * _sigmoid(ln)).astype(BF16)

    fix2 = lambda i: (0, 0)
    return pl.pallas_call(
        body,
        out_shape=[jax.ShapeDtypeStruct((S, D_MODEL), BF16), jax.ShapeDtypeStruct((S, D_POOL), BF16),
                   jax.ShapeDtypeStruct((S, D_CONV), F32)],
        grid=(S // ts,),
        in_specs=[pl.BlockSpec((ts, 3 * D_POOL), lambda i: (i, 0)),
                  pl.BlockSpec((hb, 3 * D_POOL), lambda i: (jnp.maximum(i * nh - 1, 0), 0)),
                  pl.BlockSpec((4, POOL_GROUP, POOL_GROUP), lambda i: (0, 0, 0)),
                  pl.BlockSpec((1, D_POOL), fix2), pl.BlockSpec((CONV_KERNEL, D_CONV), fix2),
                  pl.BlockSpec((1, D_CONV), fix2), pl.BlockSpec((1, D_CONV), fix2), pl.BlockSpec((1, D_CONV), fix2)],
        out_specs=[pl.BlockSpec((ts, D_MODEL), lambda i: (i, 0)), pl.BlockSpec((ts, D_POOL), lambda i: (i, 0)),
                   pl.BlockSpec((ts, D_CONV), lambda i: (i, 0))],
        scratch_shapes=[pltpu.VMEM((hb + ts, D_POOL), F32), pltpu.VMEM((hb + ts, D_CONV), F32),
                        pltpu.VMEM((7, hb + ts - 8, D_CONV), F32)],
        compiler_params=_cparams(("parallel",)),
        name="mixer_fwd",
    )(u, u, pool_w, pool_scale.reshape(1, D_POOL), conv_w, conv_b.reshape(1, D_CONV), cln_g.reshape(1, D_CONV),
      cln_b.reshape(1, D_CONV))


def _mixer_bwd(u, d, hc, dycat, pool_w, pool_scale, conv_w, cln_g, cln_b, *, ts=512):
    S = u.shape[0]
    hb = CONV_HALO
    nh = ts // hb
    n = S // ts
    te = ts + hb
    K = CONV_KERNEL

    def body(u_ref, up_ref, un_ref, d_ref, hc_ref, hcn_ref, dy_ref, dyn_ref, pw_ref, ps_ref, cw_ref, g_ref, b_ref,
             du_ref, dpw_ref, dps_ref, dcw_ref, dcb_ref, dg_ref, db_ref, stg, std, sth, gsh, hsh):
        i = pl.program_id(0)
        first = i == 0
        last = i == n - 1

        @pl.when(first)
        def _():
            dpw_ref[...] = jnp.zeros_like(dpw_ref)
            dps_ref[...] = jnp.zeros_like(dps_ref)
            dcw_ref[...] = jnp.zeros_like(dcw_ref)
            dcb_ref[...] = jnp.zeros_like(dcb_ref)
            dg_ref[...] = jnp.zeros_like(dg_ref)
            db_ref[...] = jnp.zeros_like(db_ref)

        pos_e = (i * ts + lax.broadcasted_iota(jnp.int32, (te, 1), 0) + 1).astype(F32)
        dya = dy_ref[:, 0:D_POOL]
        dya_n = jnp.where(last, 0.0, dyn_ref[:, 0:D_POOL])
        for g, w in enumerate(POOL_WINDOWS):
            lanes = pl.ds(g * POOL_GROUP, POOL_GROUP)
            sl = slice(g * POOL_GROUP, (g + 1) * POOL_GROUP)
            pw = pw_ref[g]
            scale = ps_ref[:, lanes]
            d_g = d_ref[:, lanes]
            pre = _bdot(d_g, pw)
            dps_ref[:, lanes] += jnp.sum(dya[:, sl] * pre, axis=0, keepdims=True)
            dys = dya[:, sl] * scale
            dpw_ref[g] += _bdot(d_g, dys, TN)
            dys_e = jnp.concatenate([dys, dya_n[:, sl] * scale], axis=0)
            dd = _bdot(dys_e, pw, NT)
            std[:, lanes] = dd / jnp.minimum(pos_e, float(w))
            for r0 in range(0, ts, SUB_ROWS):
                da = -dd[r0:r0 + SUB_ROWS]
                for q in range(0, w, 8):
                    for tap in _rows_ahead(std, r0 + q, SUB_ROWS, lanes, range(min(8, w - q))):
                        da = da + tap
                du_ref[pl.ds(r0, SUB_ROWS), lanes] = da.astype(BF16)

        glu_p = up_ref[:, D_POOL:D_POOL + D_CONV] * _sigmoid(up_ref[:, D_POOL + D_CONV:])
        stg[pl.ds(0, hb), :] = jnp.where(first, 0.0, glu_p)
        bv = u_ref[:, D_POOL:D_POOL + D_CONV]
        sg = _sigmoid(u_ref[:, D_POOL + D_CONV:])
        stg[pl.ds(hb, ts), :] = bv * sg
        glu_n = un_ref[:, D_POOL:D_POOL + D_CONV] * _sigmoid(un_ref[:, D_POOL + D_CONV:])
        stg[pl.ds(hb + ts, hb), :] = jnp.where(last, 0.0, glu_n)
        _shifted_copies(stg, gsh, hb + te - 8)

        sums = [jnp.zeros((8, D_CONV), F32) for _ in range(3)]
        for r0 in range(0, te, LN_ROWS):
            rows = pl.ds(r0, LN_ROWS)
            hc = hc_ref[rows, :] if r0 < ts else hcn_ref[pl.ds(r0 - ts, LN_ROWS), :]
            hcc = hc - jnp.mean(hc, axis=-1, keepdims=True)
            rstd = lax.rsqrt(jnp.mean(hcc * hcc, axis=-1, keepdims=True) + LN_EPS)
            xh = hcc * rstd
            ln = xh * g_ref[...] + b_ref[...]
            sl_ = _sigmoid(ln)
            if r0 < ts:
                dyb = dy_ref[rows, D_POOL:]
            else:
                dyb = jnp.where(last, 0.0, dyn_ref[pl.ds(r0 - ts, LN_ROWS), D_POOL:])
            dln = dyb * (sl_ * (1.0 + ln * (1.0 - sl_)))
            dxh = dln * g_ref[...]
            dhc = rstd * (dxh - jnp.mean(dxh, axis=-1, keepdims=True)
                          - xh * jnp.mean(dxh * xh, axis=-1, keepdims=True))
            sth[rows, :] = dhc
            if r0 < ts:
                for n_, term in enumerate((dln * xh, dln, dhc)):
                    sums[n_] = sums[n_] + jnp.sum(term.reshape(LN_ROWS // 8, 8, D_CONV), axis=0)
        dg_ref[...] += jnp.sum(sums[0], axis=0, keepdims=True)
        db_ref[...] += jnp.sum(sums[1], axis=0, keepdims=True)
        dcb_ref[...] += jnp.sum(sums[2], axis=0, keepdims=True)

        _shifted_copies(sth, hsh, te - 8)
        for c0 in range(0, D_CONV, SUB_LANES):
            ln_ = pl.ds(c0, SUB_LANES)
            for r0 in range(0, ts, CONV_ROWS):
                rows = pl.ds(r0, CONV_ROWS)
                dglu = jnp.zeros((CONV_ROWS, SUB_LANES), F32)
                for k in range(K):
                    dglu = dglu + cw_ref[k:k + 1, ln_] * _rows_at(sth, hsh, K - 1 - k + r0, CONV_ROWS, ln_)
                bv = u_ref[rows, pl.ds(D_POOL + c0, SUB_LANES)]
                sg = _sigmoid(u_ref[rows, pl.ds(D_POOL + D_CONV + c0, SUB_LANES)])
                du_ref[rows, pl.ds(D_POOL + c0, SUB_LANES)] = (dglu * sg).astype(BF16)
                du_ref[rows, pl.ds(D_POOL + D_CONV + c0, SUB_LANES)] = (dglu * bv * sg * (1.0 - sg)).astype(BF16)
            for k in range(K):
                tap = jnp.zeros((8, SUB_LANES), F32)
                for r0 in range(0, ts, CONV_ROWS):
                    prod = sth[pl.ds(r0, CONV_ROWS), ln_] * _rows_at(stg, gsh, hb - (K - 1) + k + r0, CONV_ROWS, ln_)
                    tap = tap + jnp.sum(prod.reshape(CONV_ROWS // 8, 8, SUB_LANES), axis=0)
                dcw_ref[k:k + 1, ln_] += jnp.sum(tap, axis=0, keepdims=True)

    fix2 = lambda i: (0, 0)
    prev = lambda i: (jnp.maximum(i * nh - 1, 0), 0)
    nxt = lambda i: (jnp.minimum((i + 1) * nh, S // hb - 1), 0)
    return pl.pallas_call(
        body,
        out_shape=[jax.ShapeDtypeStruct((S, 3 * D_POOL), BF16),
                   jax.ShapeDtypeStruct((4, POOL_GROUP, POOL_GROUP), F32),
                   jax.ShapeDtypeStruct((1, D_POOL), F32),
                   jax.ShapeDtypeStruct((K, D_CONV), F32),
                   jax.ShapeDtypeStruct((1, D_CONV), F32),
                   jax.ShapeDtypeStruct((1, D_CONV), F32),
                   jax.ShapeDtypeStruct((1, D_CONV), F32)],
        grid=(n,),
        in_specs=[pl.BlockSpec((ts, 3 * D_POOL), lambda i: (i, 0)),
                  pl.BlockSpec((hb, 3 * D_POOL), prev),
                  pl.BlockSpec((hb, 3 * D_POOL), nxt),
                  pl.BlockSpec((ts, D_POOL), lambda i: (i, 0)),
                  pl.BlockSpec((ts, D_CONV), lambda i: (i, 0)),
                  pl.BlockSpec((hb, D_CONV), nxt),
                  pl.BlockSpec((ts, D_MODEL), lambda i: (i, 0)),
                  pl.BlockSpec((hb, D_MODEL), nxt),
                  pl.BlockSpec((4, POOL_GROUP, POOL_GROUP), lambda i: (0, 0, 0)),
                  pl.BlockSpec((1, D_POOL), fix2), pl.BlockSpec((K, D_CONV), fix2),
                  pl.BlockSpec((1, D_CONV), fix2), pl.BlockSpec((1, D_CONV), fix2)],
        out_specs=[pl.BlockSpec((ts, 3 * D_POOL), lambda i: (i, 0)),
                   pl.BlockSpec((4, POOL_GROUP, POOL_GROUP), lambda i: (0, 0, 0)),
                   pl.BlockSpec((1, D_POOL), fix2), pl.BlockSpec((K, D_CONV), fix2),
                   pl.BlockSpec((1, D_CONV), fix2), pl.BlockSpec((1, D_CONV), fix2), pl.BlockSpec((1, D_CONV), fix2)],
        scratch_shapes=[pltpu.VMEM((hb + ts + hb, D_CONV), F32), pltpu.VMEM((te, D_POOL), F32),
                        pltpu.VMEM((te, D_CONV), F32), pltpu.VMEM((7, hb + te - 8, D_CONV), F32),
                        pltpu.VMEM((7, te - 8, D_CONV), F32)],
        compiler_params=_cparams(("arbitrary",)),
        name="mixer_bwd",
    )(u, u, u, d, hc, hc, dycat, dycat, pool_w, pool_scale.reshape(1, D_POOL), conv_w, cln_g.reshape(1, D_CONV),
      cln_b.reshape(1, D_CONV))


_GELU_C = math.sqrt(2.0 / math.pi)


def _gelu_parts(x):
    inner = _GELU_C * (x + 0.044715 * x * x * x)
    th = jnp.tanh(inner)
    ge = 0.5 * x * (1.0 + th)
    dge = 0.5 * (1.0 + th) + 0.5 * x * (1.0 - th * th) * (_GELU_C * (1.0 + 3.0 * 0.044715 * x * x))
    return ge, dge


def _rows_back(ref, r, n, ln, shifts):
    ext = ref[pl.ds(r - 8, n + 8), ln]
    return [(pltpu.roll(ext, s, 0) if s else ext)[8:] for s in shifts]


def _rows_ahead(ref, r, n, ln, shifts):
    ext = ref[pl.ds(r, n + 8), ln]
    return [(pltpu.roll(ext, n + 8 - s, 0) if s else ext)[:n] for s in shifts]


def _ffn_act_fwd(gate, val, dw_w, dw_b, *, ts=512, tc=2816, name):
    S, F = gate.shape
    hb = FFN_HALO
    nh = ts // hb
    tc = _tile(F, tc)

    def body(g_ref, gh_ref, v_ref, w_ref, b_ref, h_ref, st):
        i = pl.program_id(0)
        st[pl.ds(0, hb), :] = jnp.where(i == 0, 0.0, gh_ref[...].astype(F32))
        st[pl.ds(hb, ts), :] = g_ref[...].astype(F32)
        for c0 in range(0, tc, SUB_LANES):
            ln = pl.ds(c0, SUB_LANES)
            w0, w1, w2, b = w_ref[0:1, ln], w_ref[1:2, ln], w_ref[2:3, ln], b_ref[:, ln]
            for r0 in range(0, ts, SUB_ROWS):
                taps = _rows_back(st, hb + r0, SUB_ROWS, ln, (2, 1, 0))
                gc = b + w0 * taps[0] + w1 * taps[1] + w2 * taps[2]
                ge, _ = _gelu_parts(gc)
                rows = pl.ds(r0, SUB_ROWS)
                h_ref[rows, ln] = (ge * v_ref[rows, ln].astype(F32)).astype(BF16)

    return pl.pallas_call(
        body,
        out_shape=jax.ShapeDtypeStruct((S, F), BF16),
        grid=(S // ts, F // tc),
        in_specs=[pl.BlockSpec((ts, tc), lambda i, j: (i, j)),
                  pl.BlockSpec((hb, tc), lambda i, j: (jnp.maximum(i * nh - 1, 0), j)),
                  pl.BlockSpec((ts, tc), lambda i, j: (i, j)),
                  pl.BlockSpec((3, tc), lambda i, j: (0, j)),
                  pl.BlockSpec((1, tc), lambda i, j: (0, j))],
        out_specs=pl.BlockSpec((ts, tc), lambda i, j: (i, j)),
        scratch_shapes=[pltpu.VMEM((hb + ts, tc), F32)],
        compiler_params=_cparams(("parallel", "parallel")),
        name=name,
    )(gate, gate, val, dw_w, dw_b.reshape(1, F))


def _ffn_act_bwd(gate, val, dh, dw_w, dw_b, *, ts=512, tc=2816, name):
    S, F = gate.shape
    hb = FFN_HALO
    nh = ts // hb
    n = S // ts
    te = ts + hb
    tc = _tile(F, tc)

    def body(g_ref, gp_ref, gn_ref, v_ref, vn_ref, dh_ref, dhn_ref, w_ref, b_ref,
             dg_ref, dv_ref, dw_ref, db_ref, st, sd):
        i = pl.program_id(1)
        first = i == 0
        last = i == n - 1

        @pl.when(first)
        def _():
            dw_ref[...] = jnp.zeros_like(dw_ref)
            db_ref[...] = jnp.zeros_like(db_ref)

        st[pl.ds(0, hb), :] = jnp.where(first, 0.0, gp_ref[...].astype(F32))
        st[pl.ds(hb, ts), :] = g_ref[...].astype(F32)
        st[pl.ds(hb + ts, hb), :] = jnp.where(last, 0.0, gn_ref[...].astype(F32))
        for c0 in range(0, tc, SUB_LANES):
            ln = pl.ds(c0, SUB_LANES)
            w0, w1, w2, b = w_ref[0:1, ln], w_ref[1:2, ln], w_ref[2:3, ln], b_ref[:, ln]
            db_acc = jnp.zeros((8, SUB_LANES), F32)
            dw_acc = [jnp.zeros((8, SUB_LANES), F32) for _ in range(3)]
            for r0 in range(0, te, SUB_ROWS):
                rc = min(SUB_ROWS, te - r0)
                taps = _rows_back(st, hb + r0, rc, ln, (2, 1, 0))
                gc = b + w0 * taps[0] + w1 * taps[1] + w2 * taps[2]
                ge, dge = _gelu_parts(gc)
                if r0 < ts:
                    rows = pl.ds(r0, rc)
                    val, dh = v_ref[rows, ln].astype(F32), dh_ref[rows, ln].astype(F32)
                else:
                    val = jnp.where(last, 0.0, vn_ref[:, ln].astype(F32)[0:rc])
                    dh = jnp.where(last, 0.0, dhn_ref[:, ln].astype(F32)[0:rc])
                dgc = dh * val * dge
                sd[pl.ds(r0, rc), ln] = dgc
                if r0 < ts:
                    dv_ref[rows, ln] = (dh * ge).astype(BF16)
                    db_acc = db_acc + jnp.sum(dgc.reshape(rc // 8, 8, SUB_LANES), axis=0)
                    for k in range(3):
                        dw_acc[k] = dw_acc[k] + jnp.sum((dgc * taps[k]).reshape(rc // 8, 8, SUB_LANES), axis=0)
            db_ref[:, ln] += jnp.sum(db_acc, axis=0, keepdims=True)
            for k in range(3):
                dw_ref[k:k + 1, ln] += jnp.sum(dw_acc[k], axis=0, keepdims=True)
            for r0 in range(0, ts, SUB_ROWS):
                ahead = _rows_ahead(sd, r0, SUB_ROWS, ln, (2, 1, 0))
                dg_ref[pl.ds(r0, SUB_ROWS), ln] = (w0 * ahead[0] + w1 * ahead[1] + w2 * ahead[2]).astype(BF16)

    cur = lambda j, i: (i, j)
    prev = lambda j, i: (jnp.maximum(i * nh - 1, 0), j)
    nxt = lambda j, i: (jnp.minimum((i + 1) * nh, S // hb - 1), j)
    return pl.pallas_call(
        body,
        out_shape=[jax.ShapeDtypeStruct((S, F), BF16), jax.ShapeDtypeStruct((S, F), BF16),
                   jax.ShapeDtypeStruct((3, F), F32), jax.ShapeDtypeStruct((1, F), F32)],
        grid=(F // tc, n),
        in_specs=[pl.BlockSpec((ts, tc), cur), pl.BlockSpec((hb, tc), prev), pl.BlockSpec((hb, tc), nxt),
                  pl.BlockSpec((ts, tc), cur), pl.BlockSpec((hb, tc), nxt),
                  pl.BlockSpec((ts, tc), cur), pl.BlockSpec((hb, tc), nxt),
                  pl.BlockSpec((3, tc), lambda j, i: (0, j)), pl.BlockSpec((1, tc), lambda j, i: (0, j))],
        out_specs=[pl.BlockSpec((ts, tc), cur), pl.BlockSpec((ts, tc), cur),
                   pl.BlockSpec((3, tc), lambda j, i: (0, j)), pl.BlockSpec((1, tc), lambda j, i: (0, j))],
        scratch_shapes=[pltpu.VMEM((hb + ts + hb, tc), F32), pltpu.VMEM((te, tc), F32)],
        compiler_params=_cparams(("parallel", "arbitrary")),
        name=name,
    )(gate, gate, gate, val, val, dh, dh, dw_w, dw_b.reshape(1, F))


def _ple_bwd_rows(dz, gate, proj):
    return dz * proj * gate * (1.0 - gate), dz * gate


def _loss_ln_bwd(z, ln_g, ln_b, target, gate, proj, *, ts=512, name):
    S, D = z.shape

    def body(z_ref, g_ref, b_ref, t_ref, gate_ref, proj_ref, dz_ref, dzb_ref, dg_ref, db_ref, loss_ref, ds_ref,
             dp_ref, dbg_ref):
        i = pl.program_id(0)

        @pl.when(i == 0)
        def _():
            dg_ref[...] = jnp.zeros_like(dg_ref)
            db_ref[...] = jnp.zeros_like(db_ref)
            loss_ref[...] = jnp.zeros_like(loss_ref)
            dbg_ref[...] = jnp.zeros_like(dbg_ref)

        dg_acc = jnp.zeros((8, D), F32)
        db_acc = jnp.zeros((8, D), F32)
        dbg_acc = jnp.zeros((8, D), F32)
        loss_acc = jnp.zeros((1, 1), F32)
        for r0 in range(0, ts, LN_ROWS):
            rows = pl.ds(r0, LN_ROWS)
            zt = z_ref[rows, :]
            err = _layer_norm_rows(zt, g_ref[...], b_ref[...]) - t_ref[rows, :]
            loss_acc = loss_acc + 0.5 * jnp.sum(jnp.mean(err * err, axis=-1, keepdims=True), keepdims=True)
            do = err * (1.0 / D)
            dz, xh = _ln_bwd_rows(zt, g_ref[...], do)
            dg_acc = dg_acc + jnp.sum((do * xh).reshape(LN_ROWS // 8, 8, D), axis=0)
            db_acc = db_acc + jnp.sum(do.reshape(LN_ROWS // 8, 8, D), axis=0)
            dz_ref[rows, :] = dz
            dzb_ref[rows, :] = dz.astype(BF16)
            ds, dp = _ple_bwd_rows(dz, gate_ref[rows, :], proj_ref[rows, :])
            ds_ref[rows, :] = ds.astype(BF16)
            dp_ref[rows, :] = dp.astype(BF16)
            dbg_acc = dbg_acc + jnp.sum(ds.reshape(LN_ROWS // 8, 8, D), axis=0)
        dg_ref[...] += jnp.sum(dg_acc, axis=0, keepdims=True)
        db_ref[...] += jnp.sum(db_acc, axis=0, keepdims=True)
        dbg_ref[...] += jnp.sum(dbg_acc, axis=0, keepdims=True)
        loss_ref[...] += loss_acc

    row = pl.BlockSpec((ts, D), lambda i: (i, 0))
    fix = pl.BlockSpec((1, D), lambda i: (0, 0))
    return pl.pallas_call(
        body,
        out_shape=[jax.ShapeDtypeStruct((S, D), F32), jax.ShapeDtypeStruct((S, D), BF16),
                   jax.ShapeDtypeStruct((1, D), F32), jax.ShapeDtypeStruct((1, D), F32),
                   jax.ShapeDtypeStruct((8, 128), F32), jax.ShapeDtypeStruct((S, D), BF16),
                   jax.ShapeDtypeStruct((S, D), BF16), jax.ShapeDtypeStruct((1, D), F32)],
        grid=(S // ts,),
        in_specs=[row, fix, fix, row, row, row],
        out_specs=[row, row, fix, fix, pl.BlockSpec((8, 128), lambda i: (0, 0)), row, row, fix],
        compiler_params=_cparams(("arbitrary",)),
        name=name,
    )(z, ln_g.reshape(1, D), ln_b.reshape(1, D), target, gate, proj)


HEADS_PER_STEP = 4
HEAD_LANES = HEADS_PER_STEP * HEAD_DIM


ATT_ROWS = 32
ATT_SCALE = HEAD_DIM ** -0.5


def _softmax_piece(scores, bias, qb):
    s = scores + bias
    kpos = qb * Q_BLOCK + lax.broadcasted_iota(jnp.int32, (1, KV_SPAN), 1)
    s = jnp.where(kpos >= KV_PAD, s, NEG_INF)
    e = jnp.exp(s - jnp.max(s, axis=-1, keepdims=True))
    return e * (1.0 / jnp.sum(e, axis=-1, keepdims=True))


def _head_masks():
    lane = lax.broadcasted_iota(jnp.int32, (1, HEAD_LANES), 1)
    return [(lane >= j * HEAD_DIM) & (lane < (j + 1) * HEAD_DIM) for j in range(HEADS_PER_STEP)]


def _pick_heads(masks, per_head):
    out = per_head[0]
    for mask, x in zip(masks[1:], per_head[1:]):
        out = jnp.where(mask, x, out)
    return out


def _pad_keys(qb, k_ref, v_ref, kp, vp):
    @pl.when(qb == 0)
    def _():
        kp[pl.ds(0, KV_PAD), :] = jnp.zeros((KV_PAD, HEAD_LANES), BF16)
        vp[pl.ds(0, KV_PAD), :] = jnp.zeros((KV_PAD, HEAD_LANES), BF16)
        kp[pl.ds(KV_PAD, k_ref.shape[0]), :] = k_ref[...]
        vp[pl.ds(KV_PAD, v_ref.shape[0]), :] = v_ref[...]


def _attn_fwd(qkv, bias):
    S = qkv.shape[0]
    nhp = N_HEADS // HEADS_PER_STEP

    def body(q_ref, k_ref, v_ref, b_ref, o_ref, kp, vp, p_scr):
        qb = pl.program_id(1)
        _pad_keys(qb, k_ref, v_ref, kp, vp)
        span = pl.ds(pl.multiple_of(qb * Q_BLOCK, Q_BLOCK), KV_SPAN)
        kc, vc = kp[span, :], vp[span, :]
        qt = q_ref[...] * ATT_SCALE
        mine = _head_masks()
        scores = [_bdot(jnp.where(mine[j], qt, jnp.zeros_like(qt)), kc, NT) for j in range(HEADS_PER_STEP)]
        outs = []
        for j in range(HEADS_PER_STEP):
            for r0 in range(0, Q_BLOCK, ATT_ROWS):
                rows = pl.ds(r0, ATT_ROWS)
                p_scr[j, rows, :] = _softmax_piece(scores[j][r0:r0 + ATT_ROWS], b_ref[j, rows, :], qb).astype(BF16)
            outs.append(_bdot(p_scr[j], vc))
        o_ref[...] = _pick_heads(mine, outs).astype(BF16)

    return pl.pallas_call(
        body,
        out_shape=jax.ShapeDtypeStruct((S, D_MODEL), BF16),
        grid=(nhp, S // Q_BLOCK),
        in_specs=[pl.BlockSpec((Q_BLOCK, HEAD_LANES), lambda h, i: (i, h)),
                  pl.BlockSpec((S, HEAD_LANES), lambda h, i: (0, nhp + h)),
                  pl.BlockSpec((S, HEAD_LANES), lambda h, i: (0, 2 * nhp + h)),
                  pl.BlockSpec((HEADS_PER_STEP, Q_BLOCK, KV_SPAN), lambda h, i: (h, 0, 0))],
        out_specs=pl.BlockSpec((Q_BLOCK, HEAD_LANES), lambda h, i: (i, h)),
        scratch_shapes=[pltpu.VMEM((KV_PAD + S, HEAD_LANES), BF16), pltpu.VMEM((KV_PAD + S, HEAD_LANES), BF16),
                        pltpu.VMEM((HEADS_PER_STEP, Q_BLOCK, KV_SPAN), BF16)],
        compiler_params=_cparams(("parallel", "arbitrary")),
        name="attn_fwd",
    )(qkv, qkv, qkv, bias)


def _attn_bwd(qkv, bias, do):
    S = qkv.shape[0]
    nhp = N_HEADS // HEADS_PER_STEP
    nq = S // Q_BLOCK
    scale = HEAD_DIM ** -0.5

    def body(q_ref, k_ref, v_ref, b_ref, do_ref, dq_ref, dk_ref, dv_ref, db_ref, kp, vp, dka, dva,
             p_scr, ds_scr):
        qb = pl.program_id(1)
        _pad_keys(qb, k_ref, v_ref, kp, vp)

        @pl.when(qb == 0)
        def _():
            dka[...] = jnp.zeros_like(dka)
            dva[...] = jnp.zeros_like(dva)
            db_ref[...] = jnp.zeros_like(db_ref)

        span = pl.ds(pl.multiple_of(qb * Q_BLOCK, Q_BLOCK), KV_SPAN)
        kc, vc = kp[span, :], vp[span, :]
        qt, dot = q_ref[...] * ATT_SCALE, do_ref[...]
        mine = _head_masks()
        dqs = []
        qs = [jnp.where(mine[j], qt, jnp.zeros_like(qt)) for j in range(HEADS_PER_STEP)]
        dos = [jnp.where(mine[j], dot, jnp.zeros_like(dot)) for j in range(HEADS_PER_STEP)]
        scores = [_bdot(qs[j], kc, NT) for j in range(HEADS_PER_STEP)]
        dps = [_bdot(dos[j], vc, NT) for j in range(HEADS_PER_STEP)]
        for j in range(HEADS_PER_STEP):
            qj, doj = qs[j], dos[j]
            for r0 in range(0, Q_BLOCK, ATT_ROWS):
                rows = pl.ds(r0, ATT_ROWS)
                p = _softmax_piece(scores[j][r0:r0 + ATT_ROWS], b_ref[j, rows, :], qb)
                dp = dps[j][r0:r0 + ATT_ROWS]
                ds = p * (dp - jnp.sum(p * dp, axis=-1, keepdims=True))
                db_ref[j, rows, :] += ds
                p_scr[j, rows, :] = p.astype(BF16)
                ds_scr[j, rows, :] = ds.astype(BF16)
            dva[span, :] += _bdot(p_scr[j], doj, TN)
            dqs.append(_bdot(ds_scr[j], kc))
            dka[span, :] += _bdot(ds_scr[j], qj, TN)
        dq_ref[...] = (scale * _pick_heads(mine, dqs)).astype(BF16)

        @pl.when(qb == nq - 1)
        def _():
            dk_ref[...] = dka[pl.ds(KV_PAD, S), :].astype(BF16)
            dv_ref[...] = dva[pl.ds(KV_PAD, S), :].astype(BF16)

    blk = pl.BlockSpec((Q_BLOCK, HEAD_LANES), lambda h, i: (i, h))
    col = pl.BlockSpec((S, HEAD_LANES), lambda h, i: (0, h))
    bsp = pl.BlockSpec((HEADS_PER_STEP, Q_BLOCK, KV_SPAN), lambda h, i: (h, 0, 0))
    return pl.pallas_call(
        body,
        out_shape=[jax.ShapeDtypeStruct((S, D_MODEL), BF16)] * 3
        + [jax.ShapeDtypeStruct((N_HEADS, Q_BLOCK, KV_SPAN), F32)],
        grid=(nhp, nq),
        in_specs=[blk, pl.BlockSpec((S, HEAD_LANES), lambda h, i: (0, nhp + h)),
                  pl.BlockSpec((S, HEAD_LANES), lambda h, i: (0, 2 * nhp + h)), bsp, blk],
        out_specs=[blk, col, col, bsp],
        scratch_shapes=[pltpu.VMEM((KV_PAD + S, HEAD_LANES), BF16), pltpu.VMEM((KV_PAD + S, HEAD_LANES), BF16),
                        pltpu.VMEM((KV_PAD + S, HEAD_LANES), F32), pltpu.VMEM((KV_PAD + S, HEAD_LANES), F32),
                        pltpu.VMEM((HEADS_PER_STEP, Q_BLOCK, KV_SPAN), BF16), pltpu.VMEM((HEADS_PER_STEP, Q_BLOCK, KV_SPAN), BF16)],
        compiler_params=_cparams(("parallel", "arbitrary")),
        name="attn_bwd",
    )(qkv, qkv, qkv, bias, do)


N_DIST = BAND + CHUNK - 1
N_FAR = KV_PAD + CHUNK - MAX_REL


def _shear_rows(x, towards_right):
    row = lax.broadcasted_iota(jnp.int32, (Q_BLOCK, 1), 0)
    for bit in range(Q_BLOCK.bit_length() - 1):
        step = 1 << bit
        x = jnp.where((row & step) != 0, pltpu.roll(x, step if towards_right else KV_SPAN - step, 1), x)
    return x


def _bias_blocks(rel_bias, dep):
    H = rel_bias.shape[0]
    e = jnp.concatenate([jnp.broadcast_to(rel_bias[:, 2 * MAX_REL:], (H, N_FAR)),
                         jnp.flip(rel_bias[:, 2 * MAX_REL - (N_DIST - N_FAR):2 * MAX_REL], axis=1),
                         jnp.zeros((H, KV_SPAN - N_DIST), F32)], axis=1).reshape(H, 1, KV_SPAN)

    def body(e_ref, dep_ref, o_ref):
        first = pltpu.roll(jnp.broadcast_to(e_ref[...], (Q_BLOCK, KV_SPAN)), KV_SPAN - (CHUNK - 1), 1)
        x = _shear_rows(first, True)
        row = lax.broadcasted_iota(jnp.int32, (Q_BLOCK, 1), 0)
        chunk0 = row - (row & (CHUNK - 1))
        k = lax.broadcasted_iota(jnp.int32, (1, KV_SPAN), 1)
        o_ref[...] = jnp.where((k >= chunk0) & (k < chunk0 + BAND), x, NEG_INF)

    return pl.pallas_call(
        body,
        out_shape=jax.ShapeDtypeStruct((H, Q_BLOCK, KV_SPAN), F32),
        grid=(H,),
        in_specs=[pl.BlockSpec((None, 1, KV_SPAN), lambda h: (h, 0, 0)), pl.BlockSpec(memory_space=pl.ANY)],
        out_specs=pl.BlockSpec((None, Q_BLOCK, KV_SPAN), lambda h: (h, 0, 0)),
        compiler_params=_cparams(("parallel",)),
        name="bias_blocks",
    )(e, dep)


def _bias_blocks_grad(dblk):
    H = dblk.shape[0]

    def body(d_ref, o_ref):
        x = pltpu.roll(_shear_rows(d_ref[...], False), CHUNK - 1, 1)
        de = jnp.sum(x, axis=0, keepdims=True)
        lane = lax.broadcasted_iota(jnp.int32, de.shape, 1)
        far = jnp.sum(jnp.where(lane < N_FAR, de, 0.0), axis=-1, keepdims=True)
        o_ref[...] = jnp.where(lane == 0, far, jnp.where(lane < N_FAR, 0.0, de))

    de = pl.pallas_call(
        body,
        out_shape=jax.ShapeDtypeStruct((H, 1, KV_SPAN), F32),
        grid=(H,),
        in_specs=[pl.BlockSpec((None, Q_BLOCK, KV_SPAN), lambda h: (h, 0, 0))],
        out_specs=pl.BlockSpec((None, 1, KV_SPAN), lambda h: (h, 0, 0)),
        compiler_params=_cparams(("parallel",)),
        name="bias_grad_sum",
    )(dblk).reshape(H, KV_SPAN)
    near = jnp.flip(de[:, N_FAR:N_DIST], axis=1)
    return jnp.concatenate([jnp.zeros((H, 2 * MAX_REL - (N_DIST - N_FAR)), F32), near, de[:, 0:1]], axis=1)


def _ffn_forward(r1, r1b, p_l, w, l, ready, after):
    ready(f"up{l}", after)
    up_g = _mm_rows([(r1b, w["ffn_up_t"][l], True, (0, 2))], out_dtype=BF16, name=f"ffn_up_g{l}")
    up_v = _mm_rows([(r1b, w["ffn_up_t"][l], True, (1, 2))], out_dtype=BF16, name=f"ffn_up_v{l}")
    h = _ffn_act_fwd(up_g, up_v, w["ffn_dw_w"][l], w["ffn_dw_b"][l], name=f"ffn_act{l}")
    ready(f"dn{l}", h)
    z2, r2, r2b, gate, proj = _proj_ln(r1, h, w["ffn_w_down"][l], w["ln_ffn_g"][l], w["ln_ffn_b"][l],
                                       ple=(w["ple_w_gate"][l], w["ple_b_gate"][l], p_l, w["ple_w_proj"][l]),
                                       name=f"ffn_down_ln{l}")
    return dict(r1b=r1b, up_g=up_g, up_v=up_v, h=h, z2=z2, gate=gate, proj=proj), r2, r2b


def _ffn_backward(sv, dz2, dz2b, ple_bwd, p_l, w, l, grads, ln_bwd, emit):
    r1b = sv["r1b"]
    ds, dproj, db_gate = ple_bwd
    dh = _mm_rows([(dz2b, w["ffn_w_down"][l], True, WHOLE)], out_dtype=BF16, name=f"ffn_dh{l}")
    dgate, dval, d_dw_w, d_dw_b = _ffn_act_bwd(sv["up_g"], sv["up_v"], dh, w["ffn_dw_w"][l], w["ffn_dw_b"][l],
                                               name=f"ffn_act_bwd{l}")
    grads["ffn_w_down"][l] = _wgrad(sv["h"], dz2b, tm=1408, name=f"d_ffn_w_down{l}")
    d_up_g = _wgrad(dgate, r1b, tm=1408, part=(0, 2), name=f"d_ffn_up_g{l}")
    grads["ffn_up_t"][l] = _wgrad(dval, r1b, tm=1408, part=(1, 2), into=d_up_g, name=f"d_ffn_up_v{l}")
    grads["ple_w_gate"][l] = _wgrad(r1b, ds, name=f"d_ple_w_gate{l}")
    grads["ple_w_proj"][l] = _wgrad(p_l, dproj, piece=D_MODEL // N_DEV, name=f"d_ple_w_proj{l}")
    grads["ffn_dw_w"][l] = d_dw_w
    grads["ffn_dw_b"][l] = d_dw_b[0]
    grads["ple_b_gate"][l] = db_gate[0]
    return _mm_rows([(ds, w["ple_w_gate"][l], True, WHOLE), (dgate, w["ffn_up_t"][l], False, (0, 2)),
                     (dval, w["ffn_up_t"][l], False, (1, 2))], add=dz2, add_scale=ALPHA, ln_bwd=ln_bwd, dep=emit(),
                    name=f"dr1_{l}")


def _local_step(x, p, target, w, ready=lambda group, after: None, emit=lambda group, grads: None):
    grads = {k: [None, None] for k in ("ffn_w_down", "ffn_up_t", "ple_w_gate", "ple_w_proj", "ffn_dw_w",
                                       "ffn_dw_b", "ple_b_gate", "ln_ffn_g", "ln_ffn_b", "ln_mix_g", "ln_mix_b")}

    xb, pb = x.astype(BF16), p.astype(BF16)
    ready("mix", None)
    u = _mm_rows([(xb, w["mix_w_in_t"], True, WHOLE)], name="mix_in")
    ycat, dpool, hconv = _mixer_fwd(u, w["pool_w"], w["pool_scale"], w["conv_dw_w"], w["conv_dw_b"], w["conv_ln_g"],
                                    w["conv_ln_b"])
    ready("mixo", ycat)
    z1, r1, r1b = _proj_ln(x, ycat, w["mix_w_out"], w["ln_mix_g"][0], w["ln_mix_b"][0], name="mix_out_ln")
    bias = _bias_blocks(w["attn_rel_bias"], r1b)
    sv0, r2, r2b = _ffn_forward(r1, r1b, pb[0], w, 0, ready, bias)

    ready("attn", r2b)
    qkv = _mm_rows([(r2b, w["attn_w_qkv"], False, WHOLE)], out_dtype=BF16, name="attn_qkv")
    attn = _attn_fwd(qkv, bias)
    z3, r3, r3b = _proj_ln(r2, attn, w["attn_w_o"], w["ln_mix_g"][1], w["ln_mix_b"][1], name="attn_out_ln")
    sv1, _, _ = _ffn_forward(r3, r3b, pb[1], w, 1, ready, r3b)

    dz4, dz4b, grads["ln_ffn_g"][1], grads["ln_ffn_b"][1], loss, *ple1 = _loss_ln_bwd(
        sv1["z2"], w["ln_ffn_g"][1], w["ln_ffn_b"][1], target, sv1["gate"], sv1["proj"], name="loss_ln_bwd")
    dz3, dz3b, grads["ln_mix_g"][1], grads["ln_mix_b"][1] = _ffn_backward(
        sv1, dz4, dz4b, ple1, pb[1], w, 1, grads, (z3, w["ln_mix_g"][1]), lambda: emit("ffn1", grads))
    grads["attn_w_o"] = _wgrad(attn, dz3b, name="d_attn_w_o")
    dattn = _mm_rows([(dz3b, w["attn_w_o"], True, WHOLE)], out_dtype=BF16, name="d_attn")
    dq, dk, dv, dbias = _attn_bwd(qkv, bias, dattn)
    grads["attn_rel_bias"] = _bias_blocks_grad(dbias)
    dqkv = jnp.concatenate([dq, dk, dv], axis=1)
    grads["attn_w_qkv"] = _wgrad(r2b, dqkv, tn=768, piece=3 * D_MODEL // N_DEV, name="d_attn_w_qkv")
    dz2, dz2b, grads["ln_ffn_g"][0], grads["ln_ffn_b"][0], *ple0 = _mm_rows(
        [(dqkv, w["attn_w_qkv"], True, WHOLE)], add=dz3, add_scale=ALPHA,
        ln_bwd=(sv0["z2"], w["ln_ffn_g"][0], sv0["gate"], sv0["proj"]), dep=emit("attn", grads), name="dr2")
    dz1, dz1b, grads["ln_mix_g"][0], grads["ln_mix_b"][0] = _ffn_backward(
        sv0, dz2, dz2b, ple0, pb[0], w, 0, grads, (z1, w["ln_mix_g"][0]), lambda: emit("ffn0", grads))
    grads["mix_w_out"] = _wgrad(ycat, dz1b, name="d_mix_w_out")
    dycat = _mm_rows([(dz1b, w["mix_w_out"], True, WHOLE)], name="d_ycat")
    du, g_pw, g_ps, g_cw, g_cb, g_cg, g_cbb = _mixer_bwd(u, dpool, hconv, dycat, w["pool_w"], w["pool_scale"],
                                                         w["conv_dw_w"], w["conv_ln_g"], w["conv_ln_b"])
    grads["mix_w_in_t"] = _wgrad(du, xb, name="d_mix_w_in")
    grads.update(pool_w=g_pw, pool_scale=g_ps[0], conv_dw_w=g_cw, conv_dw_b=g_cb[0], conv_ln_g=g_cg[0],
                 conv_ln_b=g_cbb[0])
    for kname in ("ln_ffn_g", "ln_ffn_b", "ln_mix_g", "ln_mix_b"):
        grads[kname] = [a[0] for a in grads[kname]]
    grad_x = _mm_rows([(du, w["mix_w_in_t"], False, WHOLE)], add=dz1, add_scale=ALPHA, dep=emit("mix", grads),
                      name="grad_x")
    return loss[0, 0], grad_x, grads


_HBM = pl.BlockSpec(memory_space=pltpu.HBM)
_SEM = pl.BlockSpec(memory_space=pltpu.SEMAPHORE)
_EFFECT = pltpu.SideEffectType.DATAFLOW_SIDE_EFFECTING


def _slot(ref, place, shape, k):
    if place in ("stack", "pieces"):
        return ref.at[k]
    ax = place[1]
    n = shape[ax]
    return ref.at[(slice(None),) * ax + (pl.ds(pl.multiple_of(k * n, n), n),)]


def _result_shape(buf, place):
    if place == "stack":
        return (N_DEV,) + buf.shape
    if place == "pieces":
        return buf.shape
    return tuple(s * N_DEV if i == place[1] else s for i, s in enumerate(buf.shape))


def _peers(x, y, c):
    for d in range(1, N_DEV):
        px, py, pc = x ^ ((d >> 2) & 1), y ^ ((d >> 1) & 1), c ^ (d & 1)
        yield d, (px, py, pc), 4 * px + 2 * py + pc


def _exchange_start(bufs, places, after, *, name):
    nb = len(bufs)
    lands = [lax.empty(_result_shape(b, p_), b.dtype) for b, p_ in zip(bufs, places)]
    has_after = after is not None

    def body(*refs):
        srcs, dsts = refs[:nb], refs[nb:2 * nb]
        outs = refs[2 * nb + has_after:]
        send_sems, recv_sems, token = outs[0], outs[1], outs[2 + 2 * nb]
        x, y, c = lax.axis_index("x"), lax.axis_index("y"), lax.axis_index("c")
        me = 4 * x + 2 * y + c
        for b in range(nb):
            for d, dev, peer in _peers(x, y, c):
                pltpu.make_async_remote_copy(
                    src_ref=srcs[b].at[peer] if places[b] == "pieces" else srcs[b],
                    dst_ref=_slot(dsts[b], places[b], bufs[b].shape, me),
                    send_sem=send_sems.at[b * N_DEV + d], recv_sem=recv_sems.at[b * N_DEV + d],
                    device_id=dev, device_id_type=pl.DeviceIdType.MESH).start()
            pltpu.make_async_copy(srcs[b].at[me] if places[b] == "pieces" else srcs[b],
                                  _slot(dsts[b], places[b], bufs[b].shape, me), recv_sems.at[b * N_DEV]).start()
        token[...] = jnp.zeros_like(token)

    sems = pltpu.SemaphoreType.DMA((nb * N_DEV,))
    ins = [pltpu.with_memory_space_constraint(a, pltpu.HBM) for a in list(bufs) + lands]
    out = pl.pallas_call(
        body,
        out_shape=(sems, sems, *[pltpu.HBM(a.shape, a.dtype) for a in ins], jax.ShapeDtypeStruct((8, 128), F32)),
        in_specs=[_HBM] * (2 * nb) + ([pl.BlockSpec(memory_space=pl.ANY)] if has_after else []),
        out_specs=(_SEM, _SEM, *[_HBM] * (2 * nb), pl.BlockSpec(memory_space=pltpu.VMEM)),
        input_output_aliases={i: 2 + i for i in range(2 * nb)},
        compiler_params=pltpu.CompilerParams(has_side_effects=_EFFECT),
        name=name,
    )(*ins, *([after] if has_after else []))
    return dict(send=out[0], recv=out[1], srcs=out[2:2 + nb], lands=out[2 + nb:2 + 2 * nb], token=out[-1],
                places=places)


def _exchange_wait(h, after, *, name):
    nb = len(h["srcs"])
    places = h["places"]
    shapes = [a.shape for a in h["srcs"]]

    def body(*refs):
        srcs, dsts, send_sems, recv_sems = refs[:nb], refs[nb:2 * nb], refs[2 * nb], refs[2 * nb + 1]
        x, y, c = lax.axis_index("x"), lax.axis_index("y"), lax.axis_index("c")
        me = 4 * x + 2 * y + c
        for b in range(nb):
            pieces = places[b] == "pieces"
            for d, dev, peer in _peers(x, y, c):
                cp = pltpu.make_async_remote_copy(
                    src_ref=srcs[b].at[peer] if pieces else srcs[b],
                    dst_ref=_slot(dsts[b], places[b], shapes[b], peer),
                    send_sem=send_sems.at[b * N_DEV + d], recv_sem=recv_sems.at[b * N_DEV + d],
                    device_id=dev, device_id_type=pl.DeviceIdType.MESH)
                cp.wait_send()
                cp.wait_recv()
            pltpu.make_async_copy(srcs[b].at[me] if pieces else srcs[b], _slot(dsts[b], places[b], shapes[b], me),
                                  recv_sems.at[b * N_DEV]).wait()

    ins = list(h["srcs"]) + list(h["lands"])
    out = pl.pallas_call(
        body,
        out_shape=tuple(pltpu.HBM(a.shape, a.dtype) for a in ins),
        in_specs=[_HBM] * (2 * nb) + [_SEM, _SEM, pl.BlockSpec(memory_space=pl.ANY)],
        out_specs=tuple([_HBM] * (2 * nb)),
        input_output_aliases={i: i for i in range(2 * nb)},
        compiler_params=pltpu.CompilerParams(has_side_effects=_EFFECT),
        name=name,
    )(*ins, h["send"], h["recv"], after)
    return out[nb:]


def _adamw(recv, w, m, v, *, layer=0, into=None, name):
    L, R, C = w.shape
    fits = [d for d in range(16, R + 1, 16) if R % d == 0 and d * C * 4 <= 2 * 1024 * 1024]
    tr = fits[-1] if fits else R
    c1 = 1.0 - ADAM_B1 ** ADAM_STEP
    c2 = 1.0 - ADAM_B2 ** ADAM_STEP

    def body(r_ref, w_ref, m_ref, v_ref, *rest):
        g_ref, d_ref, mo_ref, vo_ref = rest[-4:]
        g = r_ref[0].astype(F32)
        for i in range(1, N_DEV):
            g = g + r_ref[i].astype(F32)
        m_new = ADAM_B1 * m_ref[...] + (1.0 - ADAM_B1) * g
        v_new = ADAM_B2 * v_ref[...] + (1.0 - ADAM_B2) * (g * g)
        m_hat = m_new / c1
        v_hat = v_new / c2
        g_ref[...] = g
        d_ref[...] = -ADAM_LR * (m_hat / (jnp.sqrt(v_hat) + ADAM_EPS) + ADAM_WD * w_ref[...])
        mo_ref[...] = m_new
        vo_ref[...] = v_new

    row = pl.BlockSpec((None, tr, C), lambda i: (layer, i, 0))
    others = [] if into is None else list(into)
    return pl.pallas_call(
        body,
        out_shape=[jax.ShapeDtypeStruct((L, R, C), F32)] * 4,
        grid=(R // tr,),
        in_specs=[pl.BlockSpec((N_DEV, tr, C), lambda i: (0, i, 0)), row, row, row]
        + [pl.BlockSpec(memory_space=pl.ANY)] * len(others),
        out_specs=[row] * 4,
        input_output_aliases={4 + k: k for k in range(len(others))},
        compiler_params=_cparams(("parallel",)),
        name=name,
    )(recv, w, m, v, *others)


_TRANSPOSED = ("mix_w_in", "ffn_w_up")


def _ffn_groups(l):
    return ((f"up{l}", (("ffn_w_up", l, BF16, ("axis", 0)), ("ffn_dw_w", l, F32, "stack"))),
            (f"dn{l}", (("ffn_w_down", l, BF16, ("axis", 0)), ("ple_w_gate", l, BF16, ("axis", 0)),
                        ("ple_w_proj", l, BF16, ("axis", 1)))))


_GATHER_GROUPS = (
    ("mix", (("mix_w_in", 0, BF16, ("axis", 0)), ("conv_dw_w", 0, F32, "stack"))),
    ("mixo", (("mix_w_out", 0, BF16, ("axis", 0)),)),
    *_ffn_groups(0),
    ("attn", (("attn_w_qkv", 0, BF16, ("axis", 1)), ("attn_w_o", 0, BF16, ("axis", 0)))),
    *_ffn_groups(1))
_SHARDED = ("mix_w_in", "conv_dw_w", "mix_w_out", "attn_w_qkv", "attn_w_o", "ffn_w_up", "ffn_dw_w", "ffn_w_down",
            "ple_w_gate", "ple_w_proj")
_REPLICATED = ("pool_w", "pool_scale", "conv_dw_b", "conv_ln_g", "conv_ln_b", "attn_rel_bias", "ln_mix_g",
               "ln_mix_b", "ffn_dw_b", "ple_b_gate", "ln_ffn_g", "ln_ffn_b")


def _pack_rows(parts, row_mult, dtype):
    lead = parts[0].shape[:-1]
    flat = jnp.concatenate([a.astype(dtype) for a in parts], axis=-1)
    n = flat.shape[-1]
    unit = row_mult * LANES
    padded = -(-n // unit) * unit
    flat = jnp.pad(flat, [(0, 0)] * len(lead) + [(0, padded - n)])
    return flat.reshape(lead + (padded // LANES, LANES))


def _unpack(flat2d, shapes):
    flat = flat2d.reshape(-1)
    out, o = [], 0
    for s in shapes:
        n = math.prod(s)
        out.append(flat[o:o + n].reshape(s))
        o += n
    return out


def _full_from_shards(g, axis):
    parts = jnp.moveaxis(g, 0, axis)
    shp = list(g.shape[1:])
    shp[axis] *= g.shape[0]
    return parts.reshape(shp)


def _pieces_from_full(full, axis, k=N_DEV):
    shp = list(full.shape)
    n = shp[axis] // k
    t = full.reshape(shp[:axis] + [k, n] + shp[axis + 1:])
    return jnp.moveaxis(t, axis, 0)


def kernel(x, p, mix_w_in, pool_w, pool_scale, conv_dw_w, conv_dw_b, conv_ln_g, conv_ln_b, mix_w_out, attn_w_qkv, attn_rel_bias, attn_w_o, ln_mix_g, ln_mix_b, ffn_w_up, ffn_dw_w, ffn_dw_b, ffn_w_down, ple_w_proj, ple_w_gate, ple_b_gate, ln_ffn_g, ln_ffn_b, loss_target, m_mix_w_in, m_pool_w, m_pool_scale, m_conv_dw_w, m_conv_dw_b, m_conv_ln_g, m_conv_ln_b, m_mix_w_out, m_attn_w_qkv, m_attn_rel_bias, m_attn_w_o, m_ln_mix_g, m_ln_mix_b, m_ffn_w_up, m_ffn_dw_w, m_ffn_dw_b, m_ffn_w_down, m_ple_w_proj, m_ple_w_gate, m_ple_b_gate, m_ln_ffn_g, m_ln_ffn_b, v_mix_w_in, v_pool_w, v_pool_scale, v_conv_dw_w, v_conv_dw_b, v_conv_ln_g, v_conv_ln_b, v_mix_w_out, v_attn_w_qkv, v_attn_rel_bias, v_attn_w_o, v_ln_mix_g, v_ln_mix_b, v_ffn_w_up, v_ffn_dw_w, v_ffn_dw_b, v_ffn_w_down, v_ple_w_proj, v_ple_w_gate, v_ple_b_gate, v_ln_ffn_g, v_ln_ffn_b):
    a = dict(locals())
    sh_names = list(_SHARDED)
    names = sh_names + list(_REPLICATED)
    wts = {n: a[n] for n in names}
    mom = {n: a["m_" + n] for n in names}
    var = {n: a["v_" + n] for n in names}

    for n in _TRANSPOSED:
        wts[n], mom[n], var[n] = (jnp.swapaxes(d[n], 1, 2) for d in (wts, mom, var))
    gather = {}
    token = None
    for group, items in _GATHER_GROUPS:
        gather[group] = _exchange_start([wts[n][l].astype(dt) for n, l, dt, _ in items], [pl_ for *_, pl_ in items],
                                        token, name="gather_start_" + group)
        token = gather[group]["token"]

    w = dict(pool_w=pool_w[0], pool_scale=pool_scale[0], conv_dw_b=conv_dw_b[0], conv_ln_g=conv_ln_g[0],
             conv_ln_b=conv_ln_b[0], attn_rel_bias=attn_rel_bias[0], ln_mix_g=ln_mix_g, ln_mix_b=ln_mix_b,
             ffn_dw_b=ffn_dw_b, ple_b_gate=ple_b_gate, ln_ffn_g=ln_ffn_g, ln_ffn_b=ln_ffn_b)
    for n in ("ffn_up_t", "ffn_dw_w", "ffn_w_down", "ple_w_gate", "ple_w_proj"):
        w[n] = [None, None]

    def ready(group, after):
        got = _exchange_wait(gather[group], token if after is None else after, name="gather_wait_" + group)
        if group == "mix":
            w["mix_w_in_t"], w["conv_dw_w"] = got[0], _full_from_shards(got[1], 1)
        elif group == "mixo":
            (w["mix_w_out"],) = got
        elif group == "attn":
            w["attn_w_qkv"], w["attn_w_o"] = got
        elif group[:2] == "up":
            l = int(group[2])
            w["ffn_up_t"][l], w["ffn_dw_w"][l] = got[0], _full_from_shards(got[1], 1)
        else:
            l = int(group[2])
            w["ffn_w_down"][l], w["ple_w_gate"][l], w["ple_w_proj"][l] = got

    scatter = {}

    def emit(group, gr):
        if group[:3] == "ffn":
            l = int(group[3])
            pieces = [_pieces_from_full(gr["ffn_up_t"][l], 0),
                      _pieces_from_full(gr["ffn_dw_w"][l], 1), _pieces_from_full(gr["ffn_w_down"][l], 0),
                      _pieces_from_full(gr["ple_w_gate"][l], 0), gr["ple_w_proj"][l]]
        elif group == "attn":
            pieces = [gr["attn_w_qkv"], _pieces_from_full(gr["attn_w_o"], 0)]
        else:
            pieces = [_pieces_from_full(gr["mix_w_in_t"], 0), _pieces_from_full(gr["conv_dw_w"], 1),
                      _pieces_from_full(gr["mix_w_out"], 0)]
        scatter[group] = _exchange_start([a.astype(BF16) for a in pieces], ["pieces"] * len(pieces), None,
                                         name="grad_start_" + group)
        if group != "mix":
            return scatter[group]["token"]
        gfull = dict(
            pool_w=gr["pool_w"][None], pool_scale=gr["pool_scale"][None], conv_dw_b=gr["conv_dw_b"][None],
            conv_ln_g=gr["conv_ln_g"][None], conv_ln_b=gr["conv_ln_b"][None],
            attn_rel_bias=gr["attn_rel_bias"][None], ln_mix_g=jnp.stack(gr["ln_mix_g"]),
            ln_mix_b=jnp.stack(gr["ln_mix_b"]), ffn_dw_b=jnp.stack(gr["ffn_dw_b"]),
            ple_b_gate=jnp.stack(gr["ple_b_gate"]), ln_ffn_g=jnp.stack(gr["ln_ffn_g"]),
            ln_ffn_b=jnp.stack(gr["ln_ffn_b"]))
        rep_send = _pack_rows([gfull[n].reshape(-1) for n in _REPLICATED], 8, F32)
        scatter["replicated"] = _exchange_start([rep_send], ["stack"], scatter[group]["token"],
                                                name="grad_start_replicated")
        return scatter["replicated"]["token"]

    loss_part, grad_x, gr = _local_step(x[0], p[:, 0], loss_target[0], w, ready, emit)
    loss = lax.psum(loss_part, ("x", "y", "c"))

    group_weights = {"ffn1": (("ffn_w_up", 1), ("ffn_dw_w", 1), ("ffn_w_down", 1), ("ple_w_gate", 1), ("ple_w_proj", 1)),
                     "attn": (("attn_w_qkv", 0), ("attn_w_o", 0)),
                     "ffn0": (("ffn_w_up", 0), ("ffn_dw_w", 0), ("ffn_w_down", 0), ("ple_w_gate", 0), ("ple_w_proj", 0)),
                     "mix": (("mix_w_in", 0), ("conv_dw_w", 0), ("mix_w_out", 0))}
    updated = {}
    after = grad_x
    for group in ("ffn1", "attn", "ffn0", "mix"):
        recv = _exchange_wait(scatter[group], after, name="grad_wait_" + group)
        for (n, l), r in zip(group_weights[group], recv):
            updated[n] = _adamw(r, wts[n], mom[n], var[n], layer=l, into=updated.get(n), name=f"adamw_{n}{l}")
            after = updated[n][0]
    res = [{n: jnp.swapaxes(updated[n][k], 1, 2) if n in _TRANSPOSED else updated[n][k] for n in sh_names}
           for k in range(4)]
    (rep_recv,) = _exchange_wait(scatter["replicated"], after, name="grad_wait_replicated")

    def flat_state(d):
        return _pack_rows([d[n].reshape(-1) for n in _REPLICATED], 8, F32)[None]

    rep_out = _adamw(rep_recv, flat_state(wts), flat_state(mom), flat_state(var), name="adamw_replicated")
    for k in range(4):
        for n, arr in zip(_REPLICATED, _unpack(rep_out[k][0], [wts[n].shape for n in _REPLICATED])):
            res[k][n] = arr
    order = ["mix_w_in", "pool_w", "pool_scale", "conv_dw_w", "conv_dw_b", "conv_ln_g", "conv_ln_b", "mix_w_out",
             "attn_w_qkv", "attn_rel_bias", "attn_w_o", "ln_mix_g", "ln_mix_b", "ffn_w_up", "ffn_dw_w", "ffn_dw_b",
             "ffn_w_down", "ple_w_proj", "ple_w_gate", "ple_b_gate", "ln_ffn_g", "ln_ffn_b"]
    outs = [loss, grad_x[None]]
    for k in range(4):
        outs += [res[k][n] for n in order]
    return tuple(outs)
```
